```python
import jax, jax.numpy as jnp
from jax import lax
import numpy as np

D_MODEL = 1024
BATCH = 8
SEQ = 4096
DEPTH = 2

CHUNK = 64
N_META = 16
D_FF = 4 * D_MODEL
EPS = 1e-5

CONV_DIM = D_MODEL // 2
CONV_WIDTH = 31
POOL_DIM = D_MODEL // 2
POOL_WINDOWS = (2, 4, 8, 16)
POOL_GROUP = POOL_DIM // len(POOL_WINDOWS)
EVEN_IN = 2 * CONV_DIM + POOL_DIM
EVEN_MIX = CONV_DIM + POOL_DIM

GLA_HEADS = 4
GLA_DK = D_MODEL // 2
GLA_DV = D_MODEL
GLA_HK = GLA_DK // GLA_HEADS
GLA_HV = GLA_DV // GLA_HEADS
GLA_GATE_RANK = 16
GLA_GATE_NORM = 16.0
ODD_IN = 2 * GLA_DK + 2 * GLA_DV + GLA_GATE_RANK

N_EVEN = (DEPTH + 1) // 2
N_ODD = DEPTH // 2

kernel_name = "hybrid_conv_pool_gla_trunk"


def rms_norm(x, g):
    xf = x.astype(jnp.float32)
    y = xf * lax.rsqrt(jnp.mean(xf * xf, axis=-1, keepdims=True) + EPS)
    return (y * g.astype(jnp.float32)).astype(x.dtype)


def layer_norm(x, g, b):
    xf = x.astype(jnp.float32)
    mu = jnp.mean(xf, axis=-1, keepdims=True)
    xc = xf - mu
    y = xc * lax.rsqrt(jnp.mean(xc * xc, axis=-1, keepdims=True) + EPS)
    return (y * g.astype(jnp.float32) + b.astype(jnp.float32)).astype(x.dtype)


def causal_depthwise_conv(u, w, b):
    c = u.shape[-1]
    lhs = jnp.pad(u, ((0, 0), (CONV_WIDTH - 1, 0), (0, 0)))
    out = lax.conv_general_dilated(
        lhs, w[:, None, :].astype(u.dtype), window_strides=(1,), padding="VALID",
        dimension_numbers=("NWC", "WIO", "NWC"), feature_group_count=c)
    return out + b.astype(u.dtype)


def trailing_mean_minus_self(u, window):
    L = u.shape[1]
    uf = u.astype(jnp.float32)
    csum = jnp.cumsum(uf, axis=1)
    cpad = jnp.pad(csum, ((0, 0), (1, 0), (0, 0)))
    lower = jnp.pad(cpad[:, :L - window + 1], ((0, 0), (window - 1, 0), (0, 0)))
    count = jnp.minimum(jnp.arange(L) + 1, window).astype(jnp.float32)[None, :, None]
    return ((csum - lower) / count - uf).astype(u.dtype)


def conv_pool_mixer(h, w_in, conv_w, conv_b, ln_g, ln_b, pool_w, pool_scale, w_out):
    bsz, L, _ = h.shape
    z = h @ w_in
    a_in, p_in = z[..., :2 * CONV_DIM], z[..., 2 * CONV_DIM:]
    a = a_in[..., :CONV_DIM] * jax.nn.sigmoid(a_in[..., CONV_DIM:])
    a = causal_depthwise_conv(a, conv_w, conv_b)
    a = jax.nn.silu(layer_norm(a, ln_g, ln_b))
    groups = [trailing_mean_minus_self(p_in[..., i * POOL_GROUP:(i + 1) * POOL_GROUP], w)
              for i, w in enumerate(POOL_WINDOWS)]
    p = jnp.stack(groups, axis=2)
    p = jnp.einsum("blgc,gcd->blgd", p, pool_w).reshape(bsz, L, POOL_DIM) * pool_scale
    return jnp.concatenate([a, p], axis=-1) @ w_out


def gla_mixer(h, w_in, gate_w2, gate_b, head_g, w_out):
    bsz, L, _ = h.shape
    z = h @ w_in
    split_at = np.cumsum([GLA_DK, GLA_DK, GLA_DV, GLA_DV]).tolist()
    q, k, v, g, r = jnp.split(z, split_at, axis=-1)
    log_a = jax.nn.log_sigmoid((r @ gate_w2 + gate_b).astype(jnp.float32)) / GLA_GATE_NORM
    pad = (-L) % CHUNK
    n_chunks = (L + pad) // CHUNK

    def to_chunks(t, hd):
        t = jnp.pad(t.astype(jnp.float32), ((0, 0), (pad, 0), (0, 0)))
        return t.reshape(bsz, n_chunks, CHUNK, GLA_HEADS, hd).transpose(1, 0, 3, 2, 4)

    qc = to_chunks(q * (GLA_HK ** -0.5), GLA_HK)
    kc = to_chunks(k, GLA_HK)
    vc = to_chunks(v, GLA_HV)
    lac = to_chunks(log_a, GLA_HK)
    cum = jnp.cumsum(lac, axis=3)
    total = cum[:, :, :, -1]
    k_dec = kc * jnp.exp(total[:, :, :, None, :] - cum)

    def step(state, inp):
        q_i, k_i, v_i, tot_i = inp
        state = jnp.exp(tot_i)[..., None] * state + jnp.einsum("bhck,bhcv->bhkv", k_i, v_i)
        o_i = jnp.einsum("bhck,bhkv->bhcv", q_i, state)
        return state, o_i

    s0 = jnp.zeros((bsz, GLA_HEADS, GLA_HK, GLA_HV), jnp.float32)
    _, o = lax.scan(step, s0, (qc, k_dec, vc, total))
    o = o.transpose(1, 0, 3, 2, 4).reshape(bsz, n_chunks * CHUNK, GLA_HEADS, GLA_HV)[:, pad:]
    o = rms_norm(o, head_g).reshape(bsz, L, GLA_DV)
    o = o * jax.nn.silu(g.astype(jnp.float32))
    return o.astype(h.dtype) @ w_out


def squared_relu_mlp(u, w1, w2):
    a = jax.nn.relu(u @ w1)
    return (a * a) @ w2


def _fwd_setup_inputs(seed: int = 0) -> dict:
    key = jax.random.key(seed)
    ks = jax.random.split(key, 21)
    f32 = jnp.float32
    nrm = lambda k, shape, scale: jax.random.normal(k, shape, f32) * scale
    return {
        "x": nrm(ks[0], (BATCH, SEQ, D_MODEL), 1.0),
        "meta_tokens": nrm(ks[1], (N_META, D_MODEL), 1.0),
        "mix_norm_g": 1.0 + nrm(ks[2], (DEPTH, D_MODEL), 0.02),
        "ffn_norm_g": 1.0 + nrm(ks[3], (DEPTH, D_MODEL), 0.02),
        "ffn_w1": nrm(ks[4], (DEPTH, D_MODEL, D_FF), D_MODEL ** -0.5),
        "ffn_w2": nrm(ks[5], (DEPTH, D_FF, D_MODEL), D_FF ** -0.5),
        "cp_w_in": nrm(ks[6], (N_EVEN, D_MODEL, EVEN_IN), D_MODEL ** -0.5),
        "cp_conv_w": nrm(ks[7], (N_EVEN, CONV_WIDTH, CONV_DIM), CONV_WIDTH ** -0.5),
        "cp_conv_b": nrm(ks[8], (N_EVEN, CONV_DIM), 0.02),
        "cp_ln_g": 1.0 + nrm(ks[9], (N_EVEN, CONV_DIM), 0.02),
        "cp_ln_b": nrm(ks[10], (N_EVEN, CONV_DIM), 0.02),
        "cp_pool_w": nrm(ks[11], (N_EVEN, len(POOL_WINDOWS), POOL_GROUP, POOL_GROUP), POOL_GROUP ** -0.5),
        "cp_pool_scale": 1.0 + nrm(ks[12], (N_EVEN, POOL_DIM), 0.02),
        "cp_w_out": nrm(ks[13], (N_EVEN, EVEN_MIX, D_MODEL), EVEN_MIX ** -0.5),
        "gla_w_in": nrm(ks[14], (N_ODD, D_MODEL, ODD_IN), D_MODEL ** -0.5),
        "gla_gate_w2": nrm(ks[15], (N_ODD, GLA_GATE_RANK, GLA_DK), GLA_GATE_RANK ** -0.5),
        "gla_gate_b": nrm(ks[16], (N_ODD, GLA_DK), 0.02),
        "gla_head_g": 1.0 + nrm(ks[17], (N_ODD, GLA_HV), 0.02),
        "gla_w_out": nrm(ks[18], (N_ODD, GLA_DV, D_MODEL), GLA_DV ** -0.5),
        "final_norm_g": 1.0 + nrm(ks[19], (D_MODEL,), 0.02),
    }


def _fwd_reference(x, meta_tokens, mix_norm_g, ffn_norm_g, ffn_w1, ffn_w2, cp_w_in, cp_conv_w,
              cp_conv_b, cp_ln_g, cp_ln_b, cp_pool_w, cp_pool_scale, cp_w_out, gla_w_in,
              gla_gate_w2, gla_gate_b, gla_head_g, gla_w_out, final_norm_g):
    bsz = x.shape[0]
    meta = jnp.broadcast_to(meta_tokens[None].astype(x.dtype), (bsz, N_META, D_MODEL))
    h = jnp.concatenate([meta, x], axis=1)
    for i in range(DEPTH):
        j = i // 2
        u = rms_norm(h, mix_norm_g[i])
        if i % 2 == 0:
            h = h + conv_pool_mixer(u, cp_w_in[j], cp_conv_w[j], cp_conv_b[j], cp_ln_g[j],
                                    cp_ln_b[j], cp_pool_w[j], cp_pool_scale[j], cp_w_out[j])
        else:
            h = h + gla_mixer(u, gla_w_in[j], gla_gate_w2[j], gla_gate_b[j], gla_head_g[j],
                              gla_w_out[j])
        u = rms_norm(h, ffn_norm_g[i])
        h = h + squared_relu_mlp(u, ffn_w1[i], ffn_w2[i])
    return rms_norm(h[:, N_META:], final_norm_g)


import jax as _jax
import jax.numpy as _jnp

TWIN_FORMAT = 'train_step'
FWD_PARAMS = ['x', 'meta_tokens', 'mix_norm_g', 'ffn_norm_g', 'ffn_w1', 'ffn_w2', 'cp_w_in', 'cp_conv_w', 'cp_conv_b', 'cp_ln_g', 'cp_ln_b', 'cp_pool_w', 'cp_pool_scale', 'cp_w_out', 'gla_w_in', 'gla_gate_w2', 'gla_gate_b', 'gla_head_g', 'gla_w_out', 'final_norm_g']
TWIN_WEIGHTS = ['meta_tokens', 'mix_norm_g', 'ffn_norm_g', 'ffn_w1', 'ffn_w2', 'cp_w_in', 'cp_conv_w', 'cp_conv_b', 'cp_ln_g', 'cp_ln_b', 'cp_pool_w', 'cp_pool_scale', 'cp_w_out', 'gla_w_in', 'gla_gate_w2', 'gla_gate_b', 'gla_head_g', 'gla_w_out', 'final_norm_g']
TWIN_DIFF_INPUT = 'x'
TWIN_INPUTS = ['x', 'meta_tokens', 'mix_norm_g', 'ffn_norm_g', 'ffn_w1', 'ffn_w2', 'cp_w_in', 'cp_conv_w', 'cp_conv_b', 'cp_ln_g', 'cp_ln_b', 'cp_pool_w', 'cp_pool_scale', 'cp_w_out', 'gla_w_in', 'gla_gate_w2', 'gla_gate_b', 'gla_head_g', 'gla_w_out', 'final_norm_g', 'loss_target', 'm_meta_tokens', 'm_mix_norm_g', 'm_ffn_norm_g', 'm_ffn_w1', 'm_ffn_w2', 'm_cp_w_in', 'm_cp_conv_w', 'm_cp_conv_b', 'm_cp_ln_g', 'm_cp_ln_b', 'm_cp_pool_w', 'm_cp_pool_scale', 'm_cp_w_out', 'm_gla_w_in', 'm_gla_gate_w2', 'm_gla_gate_b', 'm_gla_head_g', 'm_gla_w_out', 'm_final_norm_g', 'v_meta_tokens', 'v_mix_norm_g', 'v_ffn_norm_g', 'v_ffn_w1', 'v_ffn_w2', 'v_cp_w_in', 'v_cp_conv_w', 'v_cp_conv_b', 'v_cp_ln_g', 'v_cp_ln_b', 'v_cp_pool_w', 'v_cp_pool_scale', 'v_cp_w_out', 'v_gla_w_in', 'v_gla_gate_w2', 'v_gla_gate_b', 'v_gla_head_g', 'v_gla_w_out', 'v_final_norm_g']
TWIN_OUTPUTS = ['loss', 'grad_x', 'grad_meta_tokens', 'grad_mix_norm_g', 'grad_ffn_norm_g', 'grad_ffn_w1', 'grad_ffn_w2', 'grad_cp_w_in', 'grad_cp_conv_w', 'grad_cp_conv_b', 'grad_cp_ln_g', 'grad_cp_ln_b', 'grad_cp_pool_w', 'grad_cp_pool_scale', 'grad_cp_w_out', 'grad_gla_w_in', 'grad_gla_gate_w2', 'grad_gla_gate_b', 'grad_gla_head_g', 'grad_gla_w_out', 'grad_final_norm_g', 'delta_meta_tokens', 'delta_mix_norm_g', 'delta_ffn_norm_g', 'delta_ffn_w1', 'delta_ffn_w2', 'delta_cp_w_in', 'delta_cp_conv_w', 'delta_cp_conv_b', 'delta_cp_ln_g', 'delta_cp_ln_b', 'delta_cp_pool_w', 'delta_cp_pool_scale', 'delta_cp_w_out', 'delta_gla_w_in', 'delta_gla_gate_w2', 'delta_gla_gate_b', 'delta_gla_head_g', 'delta_gla_w_out', 'delta_final_norm_g', 'new_m_meta_tokens', 'new_m_mix_norm_g', 'new_m_ffn_norm_g', 'new_m_ffn_w1', 'new_m_ffn_w2', 'new_m_cp_w_in', 'new_m_cp_conv_w', 'new_m_cp_conv_b', 'new_m_cp_ln_g', 'new_m_cp_ln_b', 'new_m_cp_pool_w', 'new_m_cp_pool_scale', 'new_m_cp_w_out', 'new_m_gla_w_in', 'new_m_gla_gate_w2', 'new_m_gla_gate_b', 'new_m_gla_head_g', 'new_m_gla_w_out', 'new_m_final_norm_g', 'new_v_meta_tokens', 'new_v_mix_norm_g', 'new_v_ffn_norm_g', 'new_v_ffn_w1', 'new_v_ffn_w2', 'new_v_cp_w_in', 'new_v_cp_conv_w', 'new_v_cp_conv_b', 'new_v_cp_ln_g', 'new_v_cp_ln_b', 'new_v_cp_pool_w', 'new_v_cp_pool_scale', 'new_v_cp_w_out', 'new_v_gla_w_in', 'new_v_gla_gate_w2', 'new_v_gla_gate_b', 'new_v_gla_head_g', 'new_v_gla_w_out', 'new_v_final_norm_g']
TWIN_LEAF_KINDS = {'loss': 'loss', 'grad_x': 'grad_x', 'grad_meta_tokens': 'grad_w', 'grad_mix_norm_g': 'grad_w', 'grad_ffn_norm_g': 'grad_w', 'grad_ffn_w1': 'grad_w', 'grad_ffn_w2': 'grad_w', 'grad_cp_w_in': 'grad_w', 'grad_cp_conv_w': 'grad_w', 'grad_cp_conv_b': 'grad_w', 'grad_cp_ln_g': 'grad_w', 'grad_cp_ln_b': 'grad_w', 'grad_cp_pool_w': 'grad_w', 'grad_cp_pool_scale': 'grad_w', 'grad_cp_w_out': 'grad_w', 'grad_gla_w_in': 'grad_w', 'grad_gla_gate_w2': 'grad_w', 'grad_gla_gate_b': 'grad_w', 'grad_gla_head_g': 'grad_w', 'grad_gla_w_out': 'grad_w', 'grad_final_norm_g': 'grad_w', 'delta_meta_tokens': 'delta_w', 'delta_mix_norm_g': 'delta_w', 'delta_ffn_norm_g': 'delta_w', 'delta_ffn_w1': 'delta_w', 'delta_ffn_w2': 'delta_w', 'delta_cp_w_in': 'delta_w', 'delta_cp_conv_w': 'delta_w', 'delta_cp_conv_b': 'delta_w', 'delta_cp_ln_g': 'delta_w', 'delta_cp_ln_b': 'delta_w', 'delta_cp_pool_w': 'delta_w', 'delta_cp_pool_scale': 'delta_w', 'delta_cp_w_out': 'delta_w', 'delta_gla_w_in': 'delta_w', 'delta_gla_gate_w2': 'delta_w', 'delta_gla_gate_b': 'delta_w', 'delta_gla_head_g': 'delta_w', 'delta_gla_w_out': 'delta_w', 'delta_final_norm_g': 'delta_w', 'new_m_meta_tokens': 'new_m', 'new_m_mix_norm_g': 'new_m', 'new_m_ffn_norm_g': 'new_m', 'new_m_ffn_w1': 'new_m', 'new_m_ffn_w2': 'new_m', 'new_m_cp_w_in': 'new_m', 'new_m_cp_conv_w': 'new_m', 'new_m_cp_conv_b': 'new_m', 'new_m_cp_ln_g': 'new_m', 'new_m_cp_ln_b': 'new_m', 'new_m_cp_pool_w': 'new_m', 'new_m_cp_pool_scale': 'new_m', 'new_m_cp_w_out': 'new_m', 'new_m_gla_w_in': 'new_m', 'new_m_gla_gate_w2': 'new_m', 'new_m_gla_gate_b': 'new_m', 'new_m_gla_head_g': 'new_m', 'new_m_gla_w_out': 'new_m', 'new_m_final_norm_g': 'new_m', 'new_v_meta_tokens': 'new_v', 'new_v_mix_norm_g': 'new_v', 'new_v_ffn_norm_g': 'new_v', 'new_v_ffn_w1': 'new_v', 'new_v_ffn_w2': 'new_v', 'new_v_cp_w_in': 'new_v', 'new_v_cp_conv_w': 'new_v', 'new_v_cp_conv_b': 'new_v', 'new_v_cp_ln_g': 'new_v', 'new_v_cp_ln_b': 'new_v', 'new_v_cp_pool_w': 'new_v', 'new_v_cp_pool_scale': 'new_v', 'new_v_cp_w_out': 'new_v', 'new_v_gla_w_in': 'new_v', 'new_v_gla_gate_w2': 'new_v', 'new_v_gla_gate_b': 'new_v', 'new_v_gla_head_g': 'new_v', 'new_v_gla_w_out': 'new_v', 'new_v_final_norm_g': 'new_v'}


def _forward(args):
    return _fwd_reference(*[args[k] for k in FWD_PARAMS])


def _output_shape():
    def fwd():
        inp = _fwd_setup_inputs(0)
        return _fwd_reference(*[inp[k] for k in FWD_PARAMS])
    out = _jax.eval_shape(fwd)
    return out.shape, out.dtype

N_MICROBATCH = 1
ADAM_LR = 0.001
ADAM_B1 = 0.9
ADAM_B2 = 0.999
ADAM_EPS = 1e-08
ADAM_WD = 0.01
ADAM_STEP = 10
PER_EXAMPLE_BATCH_AXIS = {'x': 0, 'loss_target': 0}
SHARED_INPUTS = []
_WEIGHT_DTYPES = {'meta_tokens': _jnp.float32, 'mix_norm_g': _jnp.float32, 'ffn_norm_g': _jnp.float32, 'ffn_w1': _jnp.float32, 'ffn_w2': _jnp.float32, 'cp_w_in': _jnp.float32, 'cp_conv_w': _jnp.float32, 'cp_conv_b': _jnp.float32, 'cp_ln_g': _jnp.float32, 'cp_ln_b': _jnp.float32, 'cp_pool_w': _jnp.float32, 'cp_pool_scale': _jnp.float32, 'cp_w_out': _jnp.float32, 'gla_w_in': _jnp.float32, 'gla_gate_w2': _jnp.float32, 'gla_gate_b': _jnp.float32, 'gla_head_g': _jnp.float32, 'gla_w_out': _jnp.float32, 'final_norm_g': _jnp.float32}
MOMENT_SCALE = {'meta_tokens': 4.889398e-03, 'mix_norm_g': 1.404423e-01, 'ffn_norm_g': 1.460706e-01, 'ffn_w1': 7.398822e-02, 'ffn_w2': 1.467015e-01, 'cp_w_in': 1.174493e-01, 'cp_conv_w': 1.158082e-01, 'cp_conv_b': 2.615543e-01, 'cp_ln_g': 1.354739e-01, 'cp_ln_b': 1.395996e-01, 'cp_pool_w': 1.625387e-01, 'cp_pool_scale': 1.631637e-01, 'cp_w_out': 1.409871e-01, 'gla_w_in': 7.565513e-02, 'gla_gate_w2': 1.897686e-02, 'gla_gate_b': 5.600200e-02, 'gla_head_g': 1.334626e-01, 'gla_w_out': 6.137399e-02, 'final_norm_g': 3.240250e+01}


def _to_microbatches(a, axis):
    t = _jnp.moveaxis(a, axis, 0)
    t = t.reshape((N_MICROBATCH, t.shape[0] // N_MICROBATCH) + t.shape[1:])
    return _jnp.moveaxis(t, 1, axis + 1)


def setup_inputs(seed: int = 0) -> dict:
    inp = _fwd_setup_inputs(seed)
    key = _jax.random.fold_in(_jax.random.key(seed), 7919)
    shape, _ = _output_shape()
    out = dict(inp)
    out["loss_target"] = _jax.random.normal(_jax.random.fold_in(key, 0), shape, _jnp.float32)
    for i, name in enumerate(TWIN_WEIGHTS):
        w = inp[name].astype(_jnp.float32)
        if MOMENT_SCALE is None:
            s = _jnp.sqrt(_jnp.mean(_jnp.square(w)) + 1e-30)
        else:
            s = MOMENT_SCALE[name]
        km, kv = _jax.random.split(_jax.random.fold_in(key, i + 1))
        out[name] = w
        out["m_" + name] = s * _jax.random.normal(km, w.shape, _jnp.float32)
        out["v_" + name] = (s * s) * _jax.random.uniform(kv, w.shape, _jnp.float32, 0.5, 1.5)
    if N_MICROBATCH > 1:
        for name, axis in PER_EXAMPLE_BATCH_AXIS.items():
            out[name] = _to_microbatches(out[name], axis)
    return {'x': out['x'], 'meta_tokens': out['meta_tokens'], 'mix_norm_g': out['mix_norm_g'], 'ffn_norm_g': out['ffn_norm_g'], 'ffn_w1': out['ffn_w1'], 'ffn_w2': out['ffn_w2'], 'cp_w_in': out['cp_w_in'], 'cp_conv_w': out['cp_conv_w'], 'cp_conv_b': out['cp_conv_b'], 'cp_ln_g': out['cp_ln_g'], 'cp_ln_b': out['cp_ln_b'], 'cp_pool_w': out['cp_pool_w'], 'cp_pool_scale': out['cp_pool_scale'], 'cp_w_out': out['cp_w_out'], 'gla_w_in': out['gla_w_in'], 'gla_gate_w2': out['gla_gate_w2'], 'gla_gate_b': out['gla_gate_b'], 'gla_head_g': out['gla_head_g'], 'gla_w_out': out['gla_w_out'], 'final_norm_g': out['final_norm_g'], 'loss_target': out['loss_target'], 'm_meta_tokens': out['m_meta_tokens'], 'm_mix_norm_g': out['m_mix_norm_g'], 'm_ffn_norm_g': out['m_ffn_norm_g'], 'm_ffn_w1': out['m_ffn_w1'], 'm_ffn_w2': out['m_ffn_w2'], 'm_cp_w_in': out['m_cp_w_in'], 'm_cp_conv_w': out['m_cp_conv_w'], 'm_cp_conv_b': out['m_cp_conv_b'], 'm_cp_ln_g': out['m_cp_ln_g'], 'm_cp_ln_b': out['m_cp_ln_b'], 'm_cp_pool_w': out['m_cp_pool_w'], 'm_cp_pool_scale': out['m_cp_pool_scale'], 'm_cp_w_out': out['m_cp_w_out'], 'm_gla_w_in': out['m_gla_w_in'], 'm_gla_gate_w2': out['m_gla_gate_w2'], 'm_gla_gate_b': out['m_gla_gate_b'], 'm_gla_head_g': out['m_gla_head_g'], 'm_gla_w_out': out['m_gla_w_out'], 'm_final_norm_g': out['m_final_norm_g'], 'v_meta_tokens': out['v_meta_tokens'], 'v_mix_norm_g': out['v_mix_norm_g'], 'v_ffn_norm_g': out['v_ffn_norm_g'], 'v_ffn_w1': out['v_ffn_w1'], 'v_ffn_w2': out['v_ffn_w2'], 'v_cp_w_in': out['v_cp_w_in'], 'v_cp_conv_w': out['v_cp_conv_w'], 'v_cp_conv_b': out['v_cp_conv_b'], 'v_cp_ln_g': out['v_cp_ln_g'], 'v_cp_ln_b': out['v_cp_ln_b'], 'v_cp_pool_w': out['v_cp_pool_w'], 'v_cp_pool_scale': out['v_cp_pool_scale'], 'v_cp_w_out': out['v_cp_w_out'], 'v_gla_w_in': out['v_gla_w_in'], 'v_gla_gate_w2': out['v_gla_gate_w2'], 'v_gla_gate_b': out['v_gla_gate_b'], 'v_gla_head_g': out['v_gla_head_g'], 'v_gla_w_out': out['v_gla_w_out'], 'v_final_norm_g': out['v_final_norm_g']}


def _loss(weights, diff, rest, loss_target):
    with _jax.named_scope("forward"):
        args = {**rest, TWIN_DIFF_INPUT: diff, **{k: w.astype(_WEIGHT_DTYPES[k]) for k, w in weights.items()}}
        y = _forward(args)
    with _jax.named_scope("loss_head"):
        err = _jnp.square(y.astype(_jnp.float32) - loss_target)
        return 0.5 * _jnp.sum(_jnp.mean(err, axis=-1)) if err.ndim else 0.5 * err


def _adamw(w, g, m, v):
    m = ADAM_B1 * m + (1.0 - ADAM_B1) * g
    v = ADAM_B2 * v + (1.0 - ADAM_B2) * _jnp.square(g)
    m_hat = m / (1.0 - ADAM_B1 ** ADAM_STEP)
    v_hat = v / (1.0 - ADAM_B2 ** ADAM_STEP)
    delta = -ADAM_LR * (m_hat / (_jnp.sqrt(v_hat) + ADAM_EPS) + ADAM_WD * w)
    return delta, m, v


def reference(x, meta_tokens, mix_norm_g, ffn_norm_g, ffn_w1, ffn_w2, cp_w_in, cp_conv_w, cp_conv_b, cp_ln_g, cp_ln_b, cp_pool_w, cp_pool_scale, cp_w_out, gla_w_in, gla_gate_w2, gla_gate_b, gla_head_g, gla_w_out, final_norm_g, loss_target, m_meta_tokens, m_mix_norm_g, m_ffn_norm_g, m_ffn_w1, m_ffn_w2, m_cp_w_in, m_cp_conv_w, m_cp_conv_b, m_cp_ln_g, m_cp_ln_b, m_cp_pool_w, m_cp_pool_scale, m_cp_w_out, m_gla_w_in, m_gla_gate_w2, m_gla_gate_b, m_gla_head_g, m_gla_w_out, m_final_norm_g, v_meta_tokens, v_mix_norm_g, v_ffn_norm_g, v_ffn_w1, v_ffn_w2, v_cp_w_in, v_cp_conv_w, v_cp_conv_b, v_cp_ln_g, v_cp_ln_b, v_cp_pool_w, v_cp_pool_scale, v_cp_w_out, v_gla_w_in, v_gla_gate_w2, v_gla_gate_b, v_gla_head_g, v_gla_w_out, v_final_norm_g):
    given = dict(x=x, meta_tokens=meta_tokens, mix_norm_g=mix_norm_g, ffn_norm_g=ffn_norm_g, ffn_w1=ffn_w1, ffn_w2=ffn_w2, cp_w_in=cp_w_in, cp_conv_w=cp_conv_w, cp_conv_b=cp_conv_b, cp_ln_g=cp_ln_g, cp_ln_b=cp_ln_b, cp_pool_w=cp_pool_w, cp_pool_scale=cp_pool_scale, cp_w_out=cp_w_out, gla_w_in=gla_w_in, gla_gate_w2=gla_gate_w2, gla_gate_b=gla_gate_b, gla_head_g=gla_head_g, gla_w_out=gla_w_out, final_norm_g=final_norm_g, loss_target=loss_target, m_meta_tokens=m_meta_tokens, m_mix_norm_g=m_mix_norm_g, m_ffn_norm_g=m_ffn_norm_g, m_ffn_w1=m_ffn_w1, m_ffn_w2=m_ffn_w2, m_cp_w_in=m_cp_w_in, m_cp_conv_w=m_cp_conv_w, m_cp_conv_b=m_cp_conv_b, m_cp_ln_g=m_cp_ln_g, m_cp_ln_b=m_cp_ln_b, m_cp_pool_w=m_cp_pool_w, m_cp_pool_scale=m_cp_pool_scale, m_cp_w_out=m_cp_w_out, m_gla_w_in=m_gla_w_in, m_gla_gate_w2=m_gla_gate_w2, m_gla_gate_b=m_gla_gate_b, m_gla_head_g=m_gla_head_g, m_gla_w_out=m_gla_w_out, m_final_norm_g=m_final_norm_g, v_meta_tokens=v_meta_tokens, v_mix_norm_g=v_mix_norm_g, v_ffn_norm_g=v_ffn_norm_g, v_ffn_w1=v_ffn_w1, v_ffn_w2=v_ffn_w2, v_cp_w_in=v_cp_w_in, v_cp_conv_w=v_cp_conv_w, v_cp_conv_b=v_cp_conv_b, v_cp_ln_g=v_cp_ln_g, v_cp_ln_b=v_cp_ln_b, v_cp_pool_w=v_cp_pool_w, v_cp_pool_scale=v_cp_pool_scale, v_cp_w_out=v_cp_w_out, v_gla_w_in=v_gla_w_in, v_gla_gate_w2=v_gla_gate_w2, v_gla_gate_b=v_gla_gate_b, v_gla_head_g=v_gla_head_g, v_gla_w_out=v_gla_w_out, v_final_norm_g=v_final_norm_g)
    weights = {n: given[n] for n in TWIN_WEIGHTS}
    shared = {n: given[n] for n in SHARED_INPUTS}
    per_example = {n: given[n] for n in ['x']}
    grad_fn = _jax.value_and_grad(_loss, argnums=(0, 1))

    def one_microbatch(ex, loss_target):
        ex = dict(ex)
        diff = ex.pop(TWIN_DIFF_INPUT)
        return grad_fn(weights, diff, {**shared, **ex}, loss_target)

    if N_MICROBATCH == 1:
        loss, (grad_w, grad_x) = one_microbatch(per_example, given["loss_target"])
    else:
        def body(carry, xs):
            loss_sum, grad_sum = carry
            l_k, (gw_k, gx_k) = one_microbatch(xs[0], xs[1])
            with _jax.named_scope("update"):
                return (loss_sum + l_k, _jax.tree.map(_jnp.add, grad_sum, gw_k)), gx_k

        init = (_jnp.zeros((), _jnp.float32), _jax.tree.map(_jnp.zeros_like, weights))
        (loss, grad_w), grad_x = _jax.lax.scan(body, init, (per_example, given["loss_target"]))
    with _jax.named_scope("update"):
        delta_w, new_m, new_v = {}, {}, {}
        for n in TWIN_WEIGHTS:
            delta_w[n], new_m[n], new_v[n] = _adamw(weights[n], grad_w[n], given["m_" + n], given["v_" + n])
    return (loss, grad_x, *[grad_w[n] for n in TWIN_WEIGHTS], *[delta_w[n] for n in TWIN_WEIGHTS],
            *[new_m[n] for n in TWIN_WEIGHTS], *[new_v[n] for n in TWIN_WEIGHTS])
```

```python
import functools

import jax
import jax.numpy as jnp
from jax import lax
from jax.experimental import pallas as pl
from jax.experimental.pallas import tpu as pltpu

F32 = jnp.float32
BF16 = jnp.bfloat16

D_MODEL = 1024
N_META = 16
CHUNK = 64
PAD_ROWS = CHUNK - N_META
EPS = 1e-5
CONV_DIM = 512
CONV_WIDTH = 31
CONV_HALO = 32
POOL_DIM = 512
POOL_WINDOWS = (2, 4, 8, 16)
POOL_GROUP = 128
POOL_HALO = 16
CP_IN = 2 * CONV_DIM + POOL_DIM
GLA_HEADS = 4
GLA_DK = 512
GLA_DV = 1024
GLA_HK = GLA_DK // GLA_HEADS
GLA_HV = GLA_DV // GLA_HEADS
GATE_RANK = 16
GATE_PAD = 128
GATE_NORM = 16.0
GLA_IN = 2 * GLA_DK + 2 * GLA_DV + GATE_RANK
GLA_IN_PAD = 2 * GLA_DK + 2 * GLA_DV + GATE_PAD
N_CHIPS = 4
ADAM_LR = 0.001
ADAM_B1 = 0.9
ADAM_B2 = 0.999
ADAM_EPS = 1e-08
ADAM_WD = 0.01
ADAM_STEP = 10

VMEM_LIMIT_BYTES = 56 * 1024 * 1024
ROW_TILE_TARGET = 832
PACK_WIDTH = 1024
MESH = pl.DeviceIdType.MESH
HBM_SPEC = pl.BlockSpec(memory_space=pltpu.HBM)


def _cparams(*sem):
    return pltpu.CompilerParams(dimension_semantics=sem, vmem_limit_bytes=VMEM_LIMIT_BYTES)


def _row_tile(t, target, mult):
    best = mult
    for cand in range(mult, min(t, target) + 1, mult):
        if t % cand == 0:
            best = cand
    assert t % best == 0, (t, best)
    return best


def _rms(h, g):
    return h * lax.rsqrt(jnp.mean(h * h, axis=-1, keepdims=True) + EPS) * g


def _rms_bwd(h, g, du):
    r = lax.rsqrt(jnp.mean(h * h, axis=-1, keepdims=True) + EPS)
    xhat = h * r
    dxh = du * g
    dh = r * (dxh - xhat * jnp.mean(dxh * xhat, axis=-1, keepdims=True))
    return dh, du * xhat


def _valid_rows(i, tm):
    row = i * tm + lax.broadcasted_iota(jnp.int32, (tm, 1), 0)
    return row >= PAD_ROWS


def _dot(a, b):
    return jnp.dot(a, b, preferred_element_type=F32)


def _dot_nt(a, b):
    return lax.dot_general(a, b, (((1,), (1,)), ((), ())), preferred_element_type=F32)


def _dot_tn(a, b):
    return lax.dot_general(a, b, (((0,), (0,)), ((), ())), preferred_element_type=F32)


def _accumulate(ref, val, first):
    @pl.when(first)
    def _():
        ref[...] = val

    @pl.when(jnp.logical_not(first))
    def _():
        ref[...] += val


def _norm_matmul(h, g, w, nc, name):
    t, d = h.shape
    n = w.shape[1]
    tm = _row_tile(t, ROW_TILE_TARGET, CHUNK)

    def body(h_ref, g_ref, w_ref, z_ref, u_ref):
        u = _rms(h_ref[...], g_ref[...]).astype(BF16)
        u_ref[...] = u
        for n0 in range(0, n, nc):
            z_ref[:, n0:n0 + nc] = _dot(u, w_ref[:, n0:n0 + nc]).astype(BF16)

    return pl.pallas_call(
        body, grid=(t // tm,),
        in_specs=[pl.BlockSpec((tm, d), lambda i: (i, 0)), pl.BlockSpec((1, d), lambda i: (0, 0)),
                  pl.BlockSpec((d, n), lambda i: (0, 0))],
        out_specs=[pl.BlockSpec((tm, n), lambda i: (i, 0)), pl.BlockSpec((tm, d), lambda i: (i, 0))],
        out_shape=[jax.ShapeDtypeStruct((t, n), BF16), jax.ShapeDtypeStruct((t, d), BF16)],
        compiler_params=_cparams("parallel"), name=name)(h, g, w)


def _matmul_residual(a, w, h, name):
    t, k = a.shape
    d = w.shape[1]
    tm = _row_tile(t, ROW_TILE_TARGET, CHUNK)

    def body(a_ref, w_ref, h_ref, o_ref):
        o_ref[...] = h_ref[...] + _dot(a_ref[...], w_ref[...])

    return pl.pallas_call(
        body, grid=(t // tm,),
        in_specs=[pl.BlockSpec((tm, k), lambda i: (i, 0)), pl.BlockSpec((k, d), lambda i: (0, 0)),
                  pl.BlockSpec((tm, d), lambda i: (i, 0))],
        out_specs=pl.BlockSpec((tm, d), lambda i: (i, 0)),
        out_shape=jax.ShapeDtypeStruct((t, d), F32),
        compiler_params=_cparams("parallel"), name=name)(a, w, h)


def _ffn_fwd(h, g, w1g, w2g, layer, name):
    t, d = h.shape
    ns, ffs = w1g.shape[0], w1g.shape[3]
    tm = _row_tile(t, ROW_TILE_TARGET, CHUNK)

    def body(h_ref, g_ref, w1_ref, w2_ref, ho_ref, hp_ref, u_ref, acc_ref):
        s = pl.program_id(1)

        @pl.when(s == 0)
        def _():
            u_ref[...] = _rms(h_ref[...], g_ref[...]).astype(BF16)

        hp = _dot(u_ref[...], w1_ref[...])
        hp_ref[...] = hp.astype(BF16)
        a = jnp.maximum(hp, 0.0)
        _accumulate(acc_ref, _dot((a * a).astype(BF16), w2_ref[...]), s == 0)

        @pl.when(s == ns - 1)
        def _():
            ho_ref[...] = h_ref[...] + acc_ref[...]

    return pl.pallas_call(
        body, grid=(t // tm, ns),
        in_specs=[pl.BlockSpec((tm, d), lambda i, s: (i, 0)), pl.BlockSpec((1, d), lambda i, s: (0, 0)),
                  pl.BlockSpec((None, None, d, ffs), lambda i, s: (s, layer, 0, 0)),
                  pl.BlockSpec((None, None, ffs, d), lambda i, s: (s, layer, 0, 0))],
        out_specs=[pl.BlockSpec((tm, d), lambda i, s: (i, 0)), pl.BlockSpec((tm, ffs), lambda i, s: (i, s)),
                   pl.BlockSpec((tm, d), lambda i, s: (i, 0))],
        out_shape=[jax.ShapeDtypeStruct((t, d), F32), jax.ShapeDtypeStruct((t, ns * ffs), BF16),
                   jax.ShapeDtypeStruct((t, d), BF16)],
        scratch_shapes=[pltpu.VMEM((tm, d), F32)],
        compiler_params=_cparams("parallel", "arbitrary"), name=name)(h, g, w1g, w2g)


def _ffn_bwd_data(dh, h, g, hp, w1g, w2g, layer, name):
    t, d = h.shape
    ns, ffs = w1g.shape[0], w1g.shape[3]
    tm = _row_tile(t, ROW_TILE_TARGET, CHUNK)

    def body(dh_ref, h_ref, g_ref, hp_ref, w1_ref, w2_ref, dhi_ref, dhp_ref, dg_ref, acc_ref):
        i, s = pl.program_id(0), pl.program_id(1)
        da = _dot_nt(dh_ref[...].astype(BF16), w2_ref[...])
        dhp = (da * (2.0 * jnp.maximum(hp_ref[...].astype(F32), 0.0))).astype(BF16)
        dhp_ref[...] = dhp
        _accumulate(acc_ref, _dot_nt(dhp, w1_ref[...]), s == 0)

        @pl.when(s == ns - 1)
        def _():
            dhn, dgr = _rms_bwd(h_ref[...], g_ref[...], acc_ref[...])
            dhi_ref[...] = jnp.where(_valid_rows(i, tm), dh_ref[...] + dhn, 0.0)
            _accumulate(dg_ref, jnp.sum(dgr, axis=0, keepdims=True), i == 0)

    return pl.pallas_call(
        body, grid=(t // tm, ns),
        in_specs=[pl.BlockSpec((tm, d), lambda i, s: (i, 0)), pl.BlockSpec((tm, d), lambda i, s: (i, 0)),
                  pl.BlockSpec((1, d), lambda i, s: (0, 0)), pl.BlockSpec((tm, ffs), lambda i, s: (i, s)),
                  pl.BlockSpec((None, None, d, ffs), lambda i, s: (s, layer, 0, 0)),
                  pl.BlockSpec((None, None, ffs, d), lambda i, s: (s, layer, 0, 0))],
        out_specs=[pl.BlockSpec((tm, d), lambda i, s: (i, 0)), pl.BlockSpec((tm, ffs), lambda i, s: (i, s)),
                   pl.BlockSpec((1, d), lambda i, s: (0, 0))],
        out_shape=[jax.ShapeDtypeStruct((t, d), F32), jax.ShapeDtypeStruct((t, ns * ffs), BF16),
                   jax.ShapeDtypeStruct((1, d), F32)],
        scratch_shapes=[pltpu.VMEM((tm, d), F32)],
        compiler_params=_cparams("arbitrary", "arbitrary"), name=name)(dh, h, g, hp, w1g, w2g)


def _wgrad(x, dy, nb, xc, yc, x_by_block, dy_by_block, relu2, name):
    t = x.shape[0]
    tm = _row_tile(t, ROW_TILE_TARGET, CHUNK)

    def body(x_ref, dy_ref, o_ref):
        k = pl.program_id(1)
        xv = x_ref[...]
        if relu2:
            xv = jnp.maximum(xv.astype(F32), 0.0)
            xv = xv * xv
        _accumulate(o_ref, _dot_tn(xv.astype(BF16), dy_ref[...].astype(BF16)), k == 0)

    return pl.pallas_call(
        body, grid=(nb, t // tm),
        in_specs=[pl.BlockSpec((tm, xc), (lambda b, k: (k, b)) if x_by_block else (lambda b, k: (k, 0))),
                  pl.BlockSpec((tm, yc), (lambda b, k: (k, b)) if dy_by_block else (lambda b, k: (k, 0)))],
        out_specs=pl.BlockSpec((None, xc, yc), lambda b, k: (b, 0, 0)),
        out_shape=jax.ShapeDtypeStruct((nb, xc, yc), F32),
        compiler_params=_cparams("parallel", "arbitrary"), name=name)(x, dy)


def _dgrad(dh, w, name):
    t, d = dh.shape
    k = w.shape[0]
    tm = _row_tile(t, ROW_TILE_TARGET, CHUNK)

    def body(dh_ref, w_ref, o_ref):
        o_ref[...] = _dot_nt(dh_ref[...].astype(BF16), w_ref[...]).astype(BF16)

    return pl.pallas_call(
        body, grid=(t // tm,),
        in_specs=[pl.BlockSpec((tm, d), lambda i: (i, 0)), pl.BlockSpec((k, d), lambda i: (0, 0))],
        out_specs=pl.BlockSpec((tm, k), lambda i: (i, 0)),
        out_shape=jax.ShapeDtypeStruct((t, k), BF16),
        compiler_params=_cparams("parallel"), name=name)(dh, w)


def _dgrad_norm_bwd(dz, w, h, g, dh, nc, name):
    t, d = h.shape
    n = w.shape[1]
    tm = _row_tile(t, ROW_TILE_TARGET // 2, 16)

    def body(dz_ref, w_ref, h_ref, g_ref, dh_ref, dhi_ref, dg_ref):
        i = pl.program_id(0)
        du = jnp.zeros((tm, d), F32)
        for n0 in range(0, n, nc):
            du = du + _dot_nt(dz_ref[:, n0:n0 + nc], w_ref[:, n0:n0 + nc])
        dhn, dgr = _rms_bwd(h_ref[...], g_ref[...], du)
        dhi_ref[...] = jnp.where(_valid_rows(i, tm), dh_ref[...] + dhn, 0.0)
        _accumulate(dg_ref, jnp.sum(dgr, axis=0, keepdims=True), i == 0)

    return pl.pallas_call(
        body, grid=(t // tm,),
        in_specs=[pl.BlockSpec((tm, n), lambda i: (i, 0)), pl.BlockSpec((d, n), lambda i: (0, 0)),
                  pl.BlockSpec((tm, d), lambda i: (i, 0)), pl.BlockSpec((1, d), lambda i: (0, 0)),
                  pl.BlockSpec((tm, d), lambda i: (i, 0))],
        out_specs=[pl.BlockSpec((tm, d), lambda i: (i, 0)), pl.BlockSpec((1, d), lambda i: (0, 0))],
        out_shape=[jax.ShapeDtypeStruct((t, d), F32), jax.ShapeDtypeStruct((1, d), F32)],
        compiler_params=_cparams("arbitrary"), name=name)(dz, w, h, g, dh)


def _loss_bwd(h, g, target):
    t, d = h.shape
    tm = _row_tile(t, ROW_TILE_TARGET, CHUNK)

    def body(h_ref, g_ref, t_ref, dh_ref, dg_ref, loss_ref):
        i = pl.program_id(0)
        row = i * tm + lax.broadcasted_iota(jnp.int32, (tm, 1), 0)
        keep = row >= CHUNK
        hv, gv = h_ref[...], g_ref[...]
        err = jnp.where(keep, _rms(hv, gv) - t_ref[...], 0.0)
        part = 0.5 * jnp.sum(jnp.mean(err * err, axis=-1, keepdims=True), axis=0, keepdims=True)
        dhn, dgr = _rms_bwd(hv, gv, err * (1.0 / d))
        dh_ref[...] = dhn
        _accumulate(dg_ref, jnp.sum(dgr, axis=0, keepdims=True), i == 0)
        _accumulate(loss_ref, jnp.broadcast_to(part, (8, 128)), i == 0)

    return pl.pallas_call(
        body, grid=(t // tm,),
        in_specs=[pl.BlockSpec((tm, d), lambda i: (i, 0)), pl.BlockSpec((1, d), lambda i: (0, 0)),
                  pl.BlockSpec((tm, d), lambda i: (i, 0))],
        out_specs=[pl.BlockSpec((tm, d), lambda i: (i, 0)), pl.BlockSpec((1, d), lambda i: (0, 0)),
                   pl.BlockSpec((8, 128), lambda i: (0, 0))],
        out_shape=[jax.ShapeDtypeStruct((t, d), F32), jax.ShapeDtypeStruct((1, d), F32),
                   jax.ShapeDtypeStruct((8, 128), F32)],
        compiler_params=_cparams("arbitrary"), name="loss_bwd")(h, g, target)


CONV_BLOCK = 32


def _silu(x):
    return x * jax.nn.sigmoid(x)


def _cp_seq_fwd(z, conv_w, conv_b, ln_g, ln_b, pool_w, pool_scale):
    t = z.shape[0]
    tm = _row_tile(t, ROW_TILE_TARGET, CHUNK)

    def body(z_ref, cw_ref, cb_ref, lg_ref, lb_ref, pw_ref, ps_ref, c_ref, pm_ref, mix_ref, gbuf, pbuf):
        i = pl.program_id(0)

        @pl.when(i == 0)
        def _():
            gbuf[0:CONV_HALO, :] = jnp.zeros((CONV_HALO, CONV_DIM), F32)
            pbuf[0:POOL_HALO, :] = jnp.zeros((POOL_HALO, POOL_DIM), F32)

        @pl.when(i > 0)
        def _():
            gbuf[0:CONV_HALO, :] = gbuf[tm:tm + CONV_HALO, :]
            pbuf[0:POOL_HALO, :] = pbuf[tm:tm + POOL_HALO, :]

        av = z_ref[:, 0:CONV_DIM].astype(F32)
        ag = z_ref[:, CONV_DIM:2 * CONV_DIM].astype(F32)
        gbuf[CONV_HALO:CONV_HALO + tm, :] = av * jax.nn.sigmoid(ag)
        pbuf[POOL_HALO:POOL_HALO + tm, :] = z_ref[:, 2 * CONV_DIM:CP_IN].astype(F32)

        def conv_block(rb, carry):
            base = pl.multiple_of(rb * CONV_BLOCK, CONV_BLOCK)
            win = gbuf[pl.ds(base, CONV_BLOCK + CONV_HALO), :]
            acc = jnp.zeros((CONV_BLOCK, CONV_DIM), F32)
            for k in range(CONV_WIDTH):
                off = CONV_HALO - (CONV_WIDTH - 1) + k
                acc = acc + cw_ref[k:k + 1, :] * win[off:off + CONV_BLOCK, :]
            c_ref[pl.ds(base, CONV_BLOCK), :] = acc + cb_ref[...]
            return carry

        lax.fori_loop(0, tm // CONV_BLOCK, conv_block, 0)

        c = c_ref[...]
        mu = jnp.mean(c, axis=-1, keepdims=True)
        xc = c - mu
        ln = xc * lax.rsqrt(jnp.mean(xc * xc, axis=-1, keepdims=True) + EPS) * lg_ref[...] + lb_ref[...]
        row = i * tm + lax.broadcasted_iota(jnp.int32, (tm, 1), 0)
        mix_ref[:, 0:CONV_DIM] = jnp.where(row >= PAD_ROWS, _silu(ln), 0.0).astype(BF16)

        tpos = (row - PAD_ROWS + 1).astype(F32)
        for gi, wdw in enumerate(POOL_WINDOWS):
            lo = POOL_GROUP * gi
            cur = pbuf[POOL_HALO:POOL_HALO + tm, lo:lo + POOL_GROUP]
            sacc = cur
            for j in range(1, wdw):
                sacc = sacc + pbuf[POOL_HALO - j:POOL_HALO - j + tm, lo:lo + POOL_GROUP]
            pm = (sacc / jnp.clip(tpos, 1.0, float(wdw)) - cur).astype(BF16)
            pm_ref[:, lo:lo + POOL_GROUP] = pm
            pg = _dot(pm, pw_ref[gi].astype(BF16))
            mix_ref[:, CONV_DIM + lo:CONV_DIM + lo + POOL_GROUP] = (pg * ps_ref[:, lo:lo + POOL_GROUP]).astype(BF16)

    vec = pl.BlockSpec((1, CONV_DIM), lambda i: (0, 0))
    return pl.pallas_call(
        body, grid=(t // tm,),
        in_specs=[pl.BlockSpec((tm, CP_IN), lambda i: (i, 0)),
                  pl.BlockSpec((CONV_WIDTH, CONV_DIM), lambda i: (0, 0)), vec, vec, vec,
                  pl.BlockSpec((len(POOL_WINDOWS), POOL_GROUP, POOL_GROUP), lambda i: (0, 0, 0)), vec],
        out_specs=[pl.BlockSpec((tm, CONV_DIM), lambda i: (i, 0)), pl.BlockSpec((tm, POOL_DIM), lambda i: (i, 0)),
                   pl.BlockSpec((tm, CONV_DIM + POOL_DIM), lambda i: (i, 0))],
        out_shape=[jax.ShapeDtypeStruct((t, CONV_DIM), F32), jax.ShapeDtypeStruct((t, POOL_DIM), BF16),
                   jax.ShapeDtypeStruct((t, CONV_DIM + POOL_DIM), BF16)],
        scratch_shapes=[pltpu.VMEM((tm + CONV_HALO, CONV_DIM), F32), pltpu.VMEM((tm + POOL_HALO, POOL_DIM), F32)],
        compiler_params=_cparams("arbitrary"), name="cp_seq_fwd")(z, conv_w, conv_b, ln_g, ln_b, pool_w, pool_scale)


def _cp_seq_bwd(dmix, z, c, pm, conv_w, ln_g, ln_b, pool_w, pool_scale):
    t = z.shape[0]
    tm = _row_tile(t, ROW_TILE_TARGET, CHUNK)
    nt = t // tm

    def body(dmix_ref, z_ref, c_ref, pm_ref, cw_ref, lg_ref, lb_ref, pw_ref, ps_ref,
             dz_ref, dcw_ref, dvec_ref, dpw_ref, dcbuf, qbuf, glu_buf, dwacc):
        i = pl.program_id(0)
        tile = nt - 1 - i

        @pl.when(i == 0)
        def _():
            dcbuf[tm:tm + CONV_HALO, :] = jnp.zeros((CONV_HALO, CONV_DIM), F32)
            qbuf[tm:tm + POOL_HALO, :] = jnp.zeros((POOL_HALO, POOL_DIM), F32)
            dcw_ref[...] = jnp.zeros_like(dcw_ref)
            dwacc[...] = jnp.zeros_like(dwacc)
            dvec_ref[...] = jnp.zeros_like(dvec_ref)
            dpw_ref[...] = jnp.zeros_like(dpw_ref)

        @pl.when(i > 0)
        def _():
            dcbuf[tm:tm + CONV_HALO, :] = dcbuf[0:CONV_HALO, :]
            qbuf[tm:tm + POOL_HALO, :] = qbuf[0:POOL_HALO, :]

        row = tile * tm + lax.broadcasted_iota(jnp.int32, (tm, 1), 0)
        cv = c_ref[...]
        mu = jnp.mean(cv, axis=-1, keepdims=True)
        xc = cv - mu
        rstd = lax.rsqrt(jnp.mean(xc * xc, axis=-1, keepdims=True) + EPS)
        xhat = xc * rstd
        ln = xhat * lg_ref[...] + lb_ref[...]
        sg = jax.nn.sigmoid(ln)
        da = jnp.where(row >= PAD_ROWS, dmix_ref[:, 0:CONV_DIM].astype(F32), 0.0)
        dln = da * (sg * (1.0 + ln * (1.0 - sg)))
        dxh = dln * lg_ref[...]
        dc = rstd * (dxh - jnp.mean(dxh, axis=-1, keepdims=True) - xhat * jnp.mean(dxh * xhat, axis=-1, keepdims=True))
        dcbuf[0:tm, :] = dc
        dvec_ref[0:1, :] += jnp.sum(dc, axis=0, keepdims=True)
        dvec_ref[1:2, :] += jnp.sum(dln * xhat, axis=0, keepdims=True)
        dvec_ref[2:3, :] += jnp.sum(dln, axis=0, keepdims=True)

        av = z_ref[:, 0:CONV_DIM].astype(F32)
        sig_g = jax.nn.sigmoid(z_ref[:, CONV_DIM:2 * CONV_DIM].astype(F32))
        glu_buf[...] = av * sig_g

        def conv_block(rb, carry):
            base = pl.multiple_of(rb * CONV_BLOCK, CONV_BLOCK)
            win = dcbuf[pl.ds(base, CONV_BLOCK + CONV_HALO), :]
            glu = glu_buf[pl.ds(base, CONV_BLOCK), :]
            acc = jnp.zeros((CONV_BLOCK, CONV_DIM), F32)
            for k in range(CONV_WIDTH):
                off = CONV_WIDTH - 1 - k
                slab = win[off:off + CONV_BLOCK, :]
                acc = acc + cw_ref[k:k + 1, :] * slab
                prod = slab * glu
                part = prod[0:8]
                for q in range(1, CONV_BLOCK // 8):
                    part = part + prod[8 * q:8 * q + 8]
                dwacc[k] += part
            glu_buf[pl.ds(base, CONV_BLOCK), :] = acc
            return carry

        lax.fori_loop(0, tm // CONV_BLOCK, conv_block, 0)

        @pl.when(i == nt - 1)
        def _():
            for k in range(CONV_WIDTH):
                dcw_ref[k:k + 1, :] = jnp.sum(dwacc[k], axis=0, keepdims=True)
        dglu = glu_buf[...]
        dz_ref[:, 0:CONV_DIM] = (dglu * sig_g).astype(BF16)
        dz_ref[:, CONV_DIM:2 * CONV_DIM] = (dglu * av * sig_g * (1.0 - sig_g)).astype(BF16)

        tpos = (row - PAD_ROWS + 1).astype(F32)
        for gi, wdw in enumerate(POOL_WINDOWS):
            lo = POOL_GROUP * gi
            dp = dmix_ref[:, CONV_DIM + lo:CONV_DIM + lo + POOL_GROUP].astype(F32)
            pmv = pm_ref[:, lo:lo + POOL_GROUP]
            pwb = pw_ref[gi].astype(BF16)
            dvec_ref[3:4, lo:lo + POOL_GROUP] += jnp.sum(dp * _dot(pmv, pwb), axis=0, keepdims=True)
            dq = (dp * ps_ref[:, lo:lo + POOL_GROUP]).astype(BF16)
            dpw_ref[gi] += _dot_tn(pmv, dq)
            dpm = _dot_nt(dq, pwb)
            qbuf[0:tm, lo:lo + POOL_GROUP] = dpm / jnp.clip(tpos, 1.0, float(wdw))
            sacc = -dpm
            for j in range(wdw):
                sacc = sacc + qbuf[j:j + tm, lo:lo + POOL_GROUP]
            dz_ref[:, 2 * CONV_DIM + lo:2 * CONV_DIM + lo + POOL_GROUP] = sacc.astype(BF16)

    vec = pl.BlockSpec((1, CONV_DIM), lambda i: (0, 0))
    rev = lambda i: (nt - 1 - i, 0)
    return pl.pallas_call(
        body, grid=(nt,),
        in_specs=[pl.BlockSpec((tm, CONV_DIM + POOL_DIM), rev), pl.BlockSpec((tm, CP_IN), rev),
                  pl.BlockSpec((tm, CONV_DIM), rev), pl.BlockSpec((tm, POOL_DIM), rev),
                  pl.BlockSpec((CONV_WIDTH, CONV_DIM), lambda i: (0, 0)), vec, vec,
                  pl.BlockSpec((len(POOL_WINDOWS), POOL_GROUP, POOL_GROUP), lambda i: (0, 0, 0)), vec],
        out_specs=[pl.BlockSpec((tm, CP_IN), rev), pl.BlockSpec((CONV_WIDTH + 1, CONV_DIM), lambda i: (0, 0)),
                   pl.BlockSpec((8, CONV_DIM), lambda i: (0, 0)),
                   pl.BlockSpec((len(POOL_WINDOWS), POOL_GROUP, POOL_GROUP), lambda i: (0, 0, 0))],
        out_shape=[jax.ShapeDtypeStruct((t, CP_IN), BF16), jax.ShapeDtypeStruct((CONV_WIDTH + 1, CONV_DIM), F32),
                   jax.ShapeDtypeStruct((8, CONV_DIM), F32),
                   jax.ShapeDtypeStruct((len(POOL_WINDOWS), POOL_GROUP, POOL_GROUP), F32)],
        scratch_shapes=[pltpu.VMEM((tm + CONV_HALO, CONV_DIM), F32), pltpu.VMEM((tm + POOL_HALO, POOL_DIM), F32),
                        pltpu.VMEM((tm, CONV_DIM), F32), pltpu.VMEM((CONV_WIDTH + 1, 8, CONV_DIM), F32)],
        compiler_params=_cparams("arbitrary"), name="cp_seq_bwd")(dmix, z, c, pm, conv_w, ln_g, ln_b, pool_w, pool_scale)


Q0, K0, V0, G0, R0 = 0, GLA_DK, 2 * GLA_DK, 2 * GLA_DK + GLA_DV, 2 * GLA_DK + 2 * GLA_DV


def _split3(x):
    hi = x.astype(BF16)
    r1 = x - hi.astype(F32)
    mid = r1.astype(BF16)
    lo = (r1 - mid.astype(F32)).astype(BF16)
    return hi, mid, lo


def _tri(strict):
    r = lax.broadcasted_iota(jnp.int32, (CHUNK, CHUNK), 0)
    c = lax.broadcasted_iota(jnp.int32, (CHUNK, CHUNK), 1)
    return ((r > c) if strict else (r >= c)).astype(BF16)


def _gate_decay(r, gw_ref, gb_ref, tri):
    pre = _dot(r, gw_ref[...]) + gb_ref[...]
    lac = (jnp.minimum(pre, 0.0) - jnp.log(1.0 + jnp.exp(-jnp.abs(pre)))) * (1.0 / GATE_NORM)
    hi, mid, lo = _split3(lac)
    cum = _dot(tri, hi) + _dot(tri, mid) + _dot(tri, lo)
    return pre, cum, cum[CHUNK - 1:CHUNK, :]


def _gla_seq_fwd(z, gate_w, gate_b, head_g):
    t = z.shape[0]
    tm = _row_tile(t, ROW_TILE_TARGET, CHUNK)
    cpt = tm // CHUNK
    scale = GLA_HK ** -0.5

    def body(z_ref, gw_ref, gb_ref, hg_ref, o_ref, mix_ref, st_ref, state):
        @pl.when(pl.program_id(0) == 0)
        def _():
            state[...] = jnp.zeros_like(state)

        tri = _tri(False)

        def chunk(ci, carry):
            r0 = pl.multiple_of(ci * CHUNK, CHUNK)
            rows = pl.ds(r0, CHUNK)
            _, cum, tot = _gate_decay(z_ref[rows, R0:R0 + GATE_PAD], gw_ref, gb_ref, tri)
            dec = jnp.exp(tot - cum)
            e = jnp.exp(tot)
            st_ref[ci] = state[...].astype(BF16)
            for hd in range(GLA_HEADS):
                ks = slice(hd * GLA_HK, (hd + 1) * GLA_HK)
                vs = slice(hd * GLA_HV, (hd + 1) * GLA_HV)
                kdec = (z_ref[rows, K0 + hd * GLA_HK:K0 + (hd + 1) * GLA_HK].astype(F32) * dec[:, ks]).astype(BF16)
                v = z_ref[rows, V0 + hd * GLA_HV:V0 + (hd + 1) * GLA_HV]
                st = state[vs, :] * e[:, ks] + _dot_tn(v, kdec)
                state[vs, :] = st
                q = z_ref[rows, Q0 + hd * GLA_HK:Q0 + (hd + 1) * GLA_HK]
                o = _dot_nt(q, st.astype(BF16)) * scale
                ob = o.astype(BF16)
                o_ref[rows, vs] = ob
                on = _rms(ob.astype(F32), hg_ref[...])
                gv = z_ref[rows, G0 + hd * GLA_HV:G0 + (hd + 1) * GLA_HV].astype(F32)
                mix_ref[rows, vs] = (on * _silu(gv)).astype(BF16)
            return carry

        lax.fori_loop(0, cpt, chunk, 0)

    return pl.pallas_call(
        body, grid=(t // tm,),
        in_specs=[pl.BlockSpec((tm, GLA_IN_PAD), lambda i: (i, 0)),
                  pl.BlockSpec((GATE_PAD, GLA_DK), lambda i: (0, 0)), pl.BlockSpec((1, GLA_DK), lambda i: (0, 0)),
                  pl.BlockSpec((1, GLA_HV), lambda i: (0, 0))],
        out_specs=[pl.BlockSpec((tm, GLA_DV), lambda i: (i, 0)), pl.BlockSpec((tm, GLA_DV), lambda i: (i, 0)),
                   pl.BlockSpec((cpt, GLA_DV, GLA_HK), lambda i: (i, 0, 0))],
        out_shape=[jax.ShapeDtypeStruct((t, GLA_DV), BF16), jax.ShapeDtypeStruct((t, GLA_DV), BF16),
                   jax.ShapeDtypeStruct((t // CHUNK, GLA_DV, GLA_HK), BF16)],
        scratch_shapes=[pltpu.VMEM((GLA_DV, GLA_HK), F32)],
        compiler_params=_cparams("arbitrary"), name="gla_seq_fwd")(z, gate_w, gate_b, head_g)


def _gla_seq_bwd(dmix, o, z, states, gate_w, gate_b, head_g):
    t = z.shape[0]
    tm = _row_tile(t, ROW_TILE_TARGET, CHUNK)
    cpt = tm // CHUNK
    nt = t // tm
    scale = GLA_HK ** -0.5

    def body(dmix_ref, o_ref, z_ref, st_ref, gw_ref, gb_ref, hg_ref, dz_ref, dgw_ref, dgb_ref, dhg_ref, dstate):
        @pl.when(pl.program_id(0) == 0)
        def _():
            dstate[...] = jnp.zeros_like(dstate)
            dgw_ref[...] = jnp.zeros_like(dgw_ref)
            dgb_ref[...] = jnp.zeros_like(dgb_ref)
            dhg_ref[...] = jnp.zeros_like(dhg_ref)

        tri = _tri(False)
        tri_strict = _tri(True)

        def chunk(cj, carry):
            ci = cpt - 1 - cj
            r0 = pl.multiple_of(ci * CHUNK, CHUNK)
            rows = pl.ds(r0, CHUNK)
            r = z_ref[rows, R0:R0 + GATE_PAD]
            pre, cum, tot = _gate_decay(r, gw_ref, gb_ref, tri)
            dec = jnp.exp(tot - cum)
            e = jnp.exp(tot)
            dlac_parts = []
            dhg = jnp.zeros((1, GLA_HV), F32)
            for hd in range(GLA_HEADS):
                ks = slice(hd * GLA_HK, (hd + 1) * GLA_HK)
                vs = slice(hd * GLA_HV, (hd + 1) * GLA_HV)
                kcols = slice(K0 + hd * GLA_HK, K0 + (hd + 1) * GLA_HK)
                vcols = slice(V0 + hd * GLA_HV, V0 + (hd + 1) * GLA_HV)
                qcols = slice(Q0 + hd * GLA_HK, Q0 + (hd + 1) * GLA_HK)
                gcols = slice(G0 + hd * GLA_HV, G0 + (hd + 1) * GLA_HV)
                ov = o_ref[rows, vs].astype(F32)
                gv = z_ref[rows, gcols].astype(F32)
                dm = dmix_ref[rows, vs].astype(F32)
                sg = jax.nn.sigmoid(gv)
                rr = lax.rsqrt(jnp.mean(ov * ov, axis=-1, keepdims=True) + EPS)
                xhat = ov * rr
                don = dm * (gv * sg)
                dz_ref[rows, gcols] = (dm * (xhat * hg_ref[...]) * (sg * (1.0 + gv * (1.0 - sg)))).astype(BF16)
                dhg = dhg + jnp.sum(don * xhat, axis=0, keepdims=True)
                dxh = don * hg_ref[...]
                do = (rr * (dxh - xhat * jnp.mean(dxh * xhat, axis=-1, keepdims=True)) * scale).astype(BF16)
                kdec = z_ref[rows, kcols].astype(F32) * dec[:, ks]
                kdb = kdec.astype(BF16)
                v = z_ref[rows, vcols]
                q = z_ref[rows, qcols]
                st_prev = st_ref[ci, vs, :].astype(F32)
                st = st_prev * e[:, ks] + _dot_tn(v, kdb)
                dz_ref[rows, qcols] = _dot(do, st.astype(BF16)).astype(BF16)
                dst = dstate[vs, :] + _dot_tn(do, q)
                dstb = dst.astype(BF16)
                dkdec = _dot(v, dstb)
                dz_ref[rows, vcols] = _dot_nt(kdb, dstb).astype(BF16)
                dtot = jnp.sum(dst * st_prev, axis=0, keepdims=True) * e[:, ks]
                dstate[vs, :] = dst * e[:, ks]
                dz_ref[rows, kcols] = (dkdec * dec[:, ks]).astype(BF16)
                gk = dkdec * kdec
                ghi, gmid, _ = _split3(gk)
                dlac_parts.append(dtot + _dot(tri_strict, ghi) + _dot(tri_strict, gmid))
            dlac = jnp.concatenate(dlac_parts, axis=1)
            dpre = dlac * (1.0 / GATE_NORM) * (1.0 - jax.nn.sigmoid(pre))
            dpb = dpre.astype(BF16)
            dz_ref[rows, R0:R0 + GATE_PAD] = _dot_nt(dpb, gw_ref[...]).astype(BF16)
            dgw_ref[...] += _dot_tn(r, dpb)
            dgb_ref[...] += jnp.sum(dpre, axis=0, keepdims=True)
            dhg_ref[...] += dhg
            return carry

        lax.fori_loop(0, cpt, chunk, 0)

    rev = lambda i: (nt - 1 - i, 0)
    return pl.pallas_call(
        body, grid=(nt,),
        in_specs=[pl.BlockSpec((tm, GLA_DV), rev), pl.BlockSpec((tm, GLA_DV), rev), pl.BlockSpec((tm, GLA_IN_PAD), rev),
                  pl.BlockSpec((cpt, GLA_DV, GLA_HK), lambda i: (nt - 1 - i, 0, 0)),
                  pl.BlockSpec((GATE_PAD, GLA_DK), lambda i: (0, 0)), pl.BlockSpec((1, GLA_DK), lambda i: (0, 0)),
                  pl.BlockSpec((1, GLA_HV), lambda i: (0, 0))],
        out_specs=[pl.BlockSpec((tm, GLA_IN_PAD), rev), pl.BlockSpec((GATE_PAD, GLA_DK), lambda i: (0, 0)),
                   pl.BlockSpec((1, GLA_DK), lambda i: (0, 0)), pl.BlockSpec((1, GLA_HV), lambda i: (0, 0))],
        out_shape=[jax.ShapeDtypeStruct((t, GLA_IN_PAD), BF16), jax.ShapeDtypeStruct((GATE_PAD, GLA_DK), F32),
                   jax.ShapeDtypeStruct((1, GLA_DK), F32), jax.ShapeDtypeStruct((1, GLA_HV), F32)],
        scratch_shapes=[pltpu.VMEM((GLA_DV, GLA_HK), F32)],
        compiler_params=_cparams("arbitrary"), name="gla_seq_bwd")(dmix, o, z, states, gate_w, gate_b, head_g)


def _sum_halves(g, recv, c_idx, name):
    n, r, cdim = g.shape
    h = r // 2
    tr = _row_tile(h, 256, 8)
    nh = h // tr

    def body(c_ref, g_ref, r_ref, o_ref):
        o_ref[...] = (g_ref[...] + r_ref[...]).astype(BF16)

    return pl.pallas_call(
        body,
        grid_spec=pltpu.PrefetchScalarGridSpec(
            num_scalar_prefetch=1, grid=(n, nh),
            in_specs=[pl.BlockSpec((None, tr, cdim), lambda s, i, c: (s, c[0] * nh + i, 0)),
                      pl.BlockSpec((None, tr, cdim), lambda s, i, c: (s, i, 0))],
            out_specs=pl.BlockSpec((None, tr, cdim), lambda s, i, c: (s, i, 0))),
        out_shape=jax.ShapeDtypeStruct((n, h, cdim), BF16),
        compiler_params=_cparams("parallel", "parallel"), name=name)(c_idx, g, recv)


def _sum_slots(x, name):
    n, r, cdim = x.shape
    tr = _row_tile(r, 256, 8)

    def body(x_ref, o_ref):
        acc = x_ref[0].astype(F32)
        for j in range(1, n):
            acc = acc + x_ref[j].astype(F32)
        o_ref[...] = acc

    return pl.pallas_call(
        body, grid=(r // tr,),
        in_specs=[pl.BlockSpec((n, tr, cdim), lambda i: (0, i, 0))],
        out_specs=pl.BlockSpec((tr, cdim), lambda i: (i, 0)),
        out_shape=jax.ShapeDtypeStruct((r, cdim), F32),
        compiler_params=_cparams("parallel"), name=name)(x)


def _add2(a, b, name):
    r, cdim = a.shape
    tr = _row_tile(r, 256, 8)

    def body(a_ref, b_ref, o_ref):
        o_ref[...] = a_ref[...] + b_ref[...]

    spec = pl.BlockSpec((tr, cdim), lambda i: (i, 0))
    return pl.pallas_call(body, grid=(r // tr,), in_specs=[spec, spec], out_specs=spec,
                          out_shape=jax.ShapeDtypeStruct((r, cdim), F32),
                          compiler_params=_cparams("parallel"), name=name)(a, b)


def _adamw(w, g, m, v, name):
    r, cdim = w.shape
    tr = _row_tile(r, 256, 8)

    def body(w_ref, g_ref, m_ref, v_ref, d_ref, mo_ref, vo_ref):
        gv = g_ref[...]
        mn = ADAM_B1 * m_ref[...] + (1.0 - ADAM_B1) * gv
        vn = ADAM_B2 * v_ref[...] + (1.0 - ADAM_B2) * (gv * gv)
        m_hat = mn / (1.0 - ADAM_B1 ** ADAM_STEP)
        v_hat = vn / (1.0 - ADAM_B2 ** ADAM_STEP)
        d_ref[...] = -ADAM_LR * (m_hat / (jnp.sqrt(v_hat) + ADAM_EPS) + ADAM_WD * w_ref[...])
        mo_ref[...] = mn
        vo_ref[...] = vn

    spec = pl.BlockSpec((tr, cdim), lambda i: (i, 0))
    shp = jax.ShapeDtypeStruct((r, cdim), F32)
    return pl.pallas_call(body, grid=(r // tr,), in_specs=[spec] * 4, out_specs=[spec] * 3,
                          out_shape=[shp] * 3, compiler_params=_cparams("parallel"), name=name)(w, g, m, v)


def _place():
    x, y, c = lax.axis_index("x"), lax.axis_index("y"), lax.axis_index("c")
    chips = [(1 - x, y), (x, 1 - y), (1 - x, 1 - y)]
    return x, y, c, chips


def _remote(src, dst, send_sem, recv_sem, to):
    return pltpu.make_async_remote_copy(src_ref=src, dst_ref=dst, send_sem=send_sem, recv_sem=recv_sem,
                                        device_id=to, device_id_type=MESH)


def _gather_weights(bigs, small):
    nb = len(bigs)

    def body(*refs):
        big_in, small_in = refs[:nb], refs[nb]
        big_out, small_out = refs[nb + 1:2 * nb + 1], refs[2 * nb + 1]
        ici_send, ici_recv, d2d_send, d2d_recv, loc_sem = refs[2 * nb + 2:]
        x, y, c, chips = _place()
        me = 2 * x + y
        sib = (x, y, 1 - c)

        def half(ref, k, which):
            h = bigs[k].shape[1] // 2
            return ref.at[:, pl.ds(which * h, h), :]

        local = [pltpu.make_async_copy(big_in[k], big_out[k].at[me], loc_sem.at[k]) for k in range(nb)]
        local.append(pltpu.make_async_copy(small_in, small_out.at[me], loc_sem.at[nb]))
        for cp in local:
            cp.start()
        sends = []
        for j, (px, py) in enumerate(chips):
            for k in range(nb):
                sends.append(_remote(half(big_in[k], k, c), half(big_out[k].at[me], k, c),
                                     ici_send.at[k * 3 + j], ici_recv.at[k * 3 + j], (px, py, c)))
            sends.append(_remote(small_in, small_out.at[me], ici_send.at[nb * 3 + j], ici_recv.at[nb * 3 + j], (px, py, c)))
        for cp in sends:
            cp.start()
        passed = []
        for j, (px, py) in enumerate(chips):
            frm = 2 * px + py
            for k in range(nb):
                landed = half(big_out[k].at[frm], k, c)
                _remote(landed, landed, ici_send.at[k * 3 + j], ici_recv.at[k * 3 + j], (px, py, c)).wait_recv()
                fwd = _remote(landed, landed, d2d_send.at[k * 3 + j], d2d_recv.at[k * 3 + j], sib)
                fwd.start()
                passed.append(fwd)
            _remote(small_in, small_out.at[frm], ici_send.at[nb * 3 + j], ici_recv.at[nb * 3 + j], (px, py, c)).wait_recv()
        for j, (px, py) in enumerate(chips):
            frm = 2 * px + py
            for k in range(nb):
                theirs = half(big_out[k].at[frm], k, 1 - c)
                _remote(theirs, theirs, d2d_send.at[k * 3 + j], d2d_recv.at[k * 3 + j], sib).wait_recv()
        for cp in sends + passed:
            cp.wait_send()
        for cp in local:
            cp.wait()

    out_shape = [jax.ShapeDtypeStruct((N_CHIPS,) + b.shape, b.dtype) for b in bigs]
    out_shape.append(jax.ShapeDtypeStruct((N_CHIPS,) + small.shape, small.dtype))
    return pl.pallas_call(
        body, in_specs=[HBM_SPEC] * (nb + 1), out_specs=[HBM_SPEC] * (nb + 1), out_shape=out_shape,
        scratch_shapes=[pltpu.SemaphoreType.DMA((3 * nb + 3,)), pltpu.SemaphoreType.DMA((3 * nb + 3,)),
                        pltpu.SemaphoreType.DMA((3 * nb,)), pltpu.SemaphoreType.DMA((3 * nb,)),
                        pltpu.SemaphoreType.DMA((nb + 1,))],
        name="gather_weights")(*bigs, small)


def _exchange_halves(grads, small):
    ng = len(grads)

    def body(*refs):
        g_in, s_in = refs[:ng], refs[ng]
        g_out, s_out = refs[ng + 1:2 * ng + 1], refs[2 * ng + 1]
        send_sem, recv_sem = refs[2 * ng + 2:]
        x, y, c, _ = _place()
        sib = (x, y, 1 - c)
        copies = []
        for k in range(ng):
            h = grads[k].shape[1] // 2
            copies.append(_remote(g_in[k].at[:, pl.ds((1 - c) * h, h), :], g_out[k], send_sem.at[k], recv_sem.at[k], sib))
        copies.append(_remote(s_in, s_out, send_sem.at[ng], recv_sem.at[ng], sib))
        for cp in copies:
            cp.start()
        for cp in copies:
            cp.wait_recv()
        for cp in copies:
            cp.wait_send()

    out_shape = [jax.ShapeDtypeStruct((g.shape[0], g.shape[1] // 2, g.shape[2]), g.dtype) for g in grads]
    out_shape.append(jax.ShapeDtypeStruct(small.shape, small.dtype))
    return pl.pallas_call(
        body, in_specs=[HBM_SPEC] * (ng + 1), out_specs=[HBM_SPEC] * (ng + 1), out_shape=out_shape,
        scratch_shapes=[pltpu.SemaphoreType.DMA((ng + 1,)), pltpu.SemaphoreType.DMA((ng + 1,))],
        name="exchange_halves")(*grads, small)


def _scatter_to_chips(parts, small):
    npart = len(parts)

    def body(*refs):
        p_in, s_in = refs[:npart], refs[npart]
        p_out, s_out = refs[npart + 1:2 * npart + 1], refs[2 * npart + 1]
        send_sem, recv_sem, loc_sem = refs[2 * npart + 2:]
        x, y, c, chips = _place()
        me = 2 * x + y
        local = [pltpu.make_async_copy(p_in[k].at[me], p_out[k].at[me], loc_sem.at[k]) for k in range(npart)]
        local.append(pltpu.make_async_copy(s_in, s_out.at[me], loc_sem.at[npart]))
        for cp in local:
            cp.start()
        sends, recvs = [], []
        for j, (px, py) in enumerate(chips):
            to = 2 * px + py
            for k in range(npart + 1):
                src = s_in if k == npart else p_in[k].at[to]
                out = s_out if k == npart else p_out[k]
                sems = (send_sem.at[k * 3 + j], recv_sem.at[k * 3 + j])
                sends.append(_remote(src, out.at[me], *sems, (px, py, c)))
                recvs.append(_remote(src, out.at[to], *sems, (px, py, c)))
        for cp in sends:
            cp.start()
        for cp in recvs:
            cp.wait_recv()
        for cp in sends:
            cp.wait_send()
        for cp in local:
            cp.wait()

    out_shape = [jax.ShapeDtypeStruct(p.shape, p.dtype) for p in parts]
    out_shape.append(jax.ShapeDtypeStruct((N_CHIPS,) + small.shape, small.dtype))
    return pl.pallas_call(
        body, in_specs=[HBM_SPEC] * (npart + 1), out_specs=[HBM_SPEC] * (npart + 1), out_shape=out_shape,
        scratch_shapes=[pltpu.SemaphoreType.DMA((3 * npart + 3,)), pltpu.SemaphoreType.DMA((3 * npart + 3,)),
                        pltpu.SemaphoreType.DMA((npart + 1,))],
        name="scatter_to_chips")(*parts, small)


def _join_halves(reduced, dests, out_shapes):
    nr = len(reduced)

    def body(*refs):
        r_in = refs[:nr]
        outs = refs[nr:nr + len(out_shapes)]
        send_sem, recv_sem, loc_sem = refs[nr + len(out_shapes):]
        x, y, c, _ = _place()
        sib = (x, y, 1 - c)
        local, sends, recvs = [], [], []
        for k in range(nr):
            oi, layer = dests[k]
            h = reduced[k].shape[0]
            mine = outs[oi].at[layer, pl.ds(c * h, h), :]
            theirs = outs[oi].at[layer, pl.ds((1 - c) * h, h), :]
            local.append(pltpu.make_async_copy(r_in[k], mine, loc_sem.at[k]))
            sends.append(_remote(r_in[k], mine, send_sem.at[k], recv_sem.at[k], sib))
            recvs.append(_remote(r_in[k], theirs, send_sem.at[k], recv_sem.at[k], sib))
        for cp in local + sends:
            cp.start()
        for cp in recvs:
            cp.wait_recv()
        for cp in sends:
            cp.wait_send()
        for cp in local:
            cp.wait()

    return pl.pallas_call(
        body, in_specs=[HBM_SPEC] * nr, out_specs=[HBM_SPEC] * len(out_shapes),
        out_shape=[jax.ShapeDtypeStruct(s, F32) for s in out_shapes],
        scratch_shapes=[pltpu.SemaphoreType.DMA((nr,)), pltpu.SemaphoreType.DMA((nr,)), pltpu.SemaphoreType.DMA((nr,))],
        name="join_halves")(*reduced)


def _pack(arrs):
    flat = jnp.concatenate([a.reshape(-1).astype(F32) for a in arrs])
    n = flat.shape[0]
    rows = -(-n // PACK_WIDTH)
    rows = -(-rows // 8) * 8
    return jnp.pad(flat, (0, rows * PACK_WIDTH - n)).reshape(rows, PACK_WIDTH)


def _unpack(buf, shapes):
    flat = buf.reshape(-1)
    out, off = [], 0
    for shp in shapes:
        n = 1
        for s in shp:
            n *= s
        out.append(flat[off:off + n].reshape(shp))
        off += n
    return out


def _unshard_cols(stacked):
    moved = jnp.moveaxis(stacked, 0, -2)
    return moved.reshape(moved.shape[:-2] + (moved.shape[-2] * moved.shape[-1],))


def _col_shard(full, s, width):
    return lax.dynamic_slice_in_dim(full, s * width, width, axis=full.ndim - 1)


def kernel(x, meta_tokens, mix_norm_g, ffn_norm_g, ffn_w1, ffn_w2, cp_w_in, cp_conv_w, cp_conv_b, cp_ln_g, cp_ln_b, cp_pool_w, cp_pool_scale, cp_w_out, gla_w_in, gla_gate_w2, gla_gate_b, gla_head_g, gla_w_out, final_norm_g, loss_target, m_meta_tokens, m_mix_norm_g, m_ffn_norm_g, m_ffn_w1, m_ffn_w2, m_cp_w_in, m_cp_conv_w, m_cp_conv_b, m_cp_ln_g, m_cp_ln_b, m_cp_pool_w, m_cp_pool_scale, m_cp_w_out, m_gla_w_in, m_gla_gate_w2, m_gla_gate_b, m_gla_head_g, m_gla_w_out, m_final_norm_g, v_meta_tokens, v_mix_norm_g, v_ffn_norm_g, v_ffn_w1, v_ffn_w2, v_cp_w_in, v_cp_conv_w, v_cp_conv_b, v_cp_ln_g, v_cp_ln_b, v_cp_pool_w, v_cp_pool_scale, v_cp_w_out, v_gla_w_in, v_gla_gate_w2, v_gla_gate_b, v_gla_head_g, v_gla_w_out, v_final_norm_g):
    d = D_MODEL
    chip = 2 * lax.axis_index("x") + lax.axis_index("y")
    core = lax.axis_index("c")
    seq = x.shape[1]
    t = seq + CHUNK

    big_w = [ffn_w1, ffn_w2, cp_w_in, cp_w_out, gla_w_in, gla_w_out]
    sharded_small = [meta_tokens, cp_conv_w, gla_gate_w2, gla_gate_b, gla_head_g]
    gathered = _gather_weights([w.astype(BF16) for w in big_w], _pack(sharded_small))
    w1g, w2g, cpin_g, cpout_g, glain_g, glaout_g, small_g = gathered
    per_chip = [_unpack(small_g[j], [a.shape for a in sharded_small]) for j in range(N_CHIPS)]
    meta_f, conv_w_f, gate_w_f, gate_b_f, head_g_f = [
        jnp.concatenate([per_chip[j][i] for j in range(N_CHIPS)], axis=-1) for i in range(len(sharded_small))]
    conv_w_f, gate_w_f = conv_w_f[0], gate_w_f[0]
    w_cp_in = _unshard_cols(cpin_g[:, 0])
    w_cp_out = cpout_g.reshape(CONV_DIM + POOL_DIM, d)
    w_gla_in = jnp.pad(_unshard_cols(glain_g[:, 0]), ((0, 0), (0, GLA_IN_PAD - GLA_IN)))
    w_gla_out = glaout_g.reshape(GLA_DV, d)
    gate_w_pad = jnp.pad(gate_w_f, ((0, GATE_PAD - GATE_RANK), (0, 0))).astype(BF16)
    row = lambda a: a.reshape(1, -1)

    h0 = jnp.concatenate([jnp.zeros((PAD_ROWS, d), F32), meta_f, x[0]], axis=0)
    z0, u0 = _norm_matmul(h0, row(mix_norm_g[0]), w_cp_in, 512, "cp_in_proj")
    c0, pm0, mix0 = _cp_seq_fwd(z0, conv_w_f, cp_conv_b, cp_ln_g, cp_ln_b, cp_pool_w[0], cp_pool_scale)
    h1 = _matmul_residual(mix0, w_cp_out, h0, "cp_out_proj")
    h2, hp0, uf0 = _ffn_fwd(h1, row(ffn_norm_g[0]), w1g, w2g, 0, "ffn0_fwd")
    z1, u2 = _norm_matmul(h2, row(mix_norm_g[1]), w_gla_in, 640, "gla_in_proj")
    o1, mix1, states = _gla_seq_fwd(z1, gate_w_pad, gate_b_f, head_g_f)
    h3 = _matmul_residual(mix1, w_gla_out, h2, "gla_out_proj")
    h4, hp1, uf1 = _ffn_fwd(h3, row(ffn_norm_g[1]), w1g, w2g, 1, "ffn1_fwd")

    target = jnp.pad(loss_target[0], ((CHUNK, 0), (0, 0)))
    dh4, d_final_g, loss_part = _loss_bwd(h4, row(final_norm_g), target)

    dh3, dhp1, d_ffn_g1 = _ffn_bwd_data(dh4, h3, row(ffn_norm_g[1]), hp1, w1g, w2g, 1, "ffn1_bwd")
    dw1_1 = _wgrad(uf1, dhp1, N_CHIPS, d, d, False, True, False, "ffn1_dw1")
    dw2_1 = _wgrad(hp1, dh4, N_CHIPS, d, d, True, False, True, "ffn1_dw2")

    dmix1 = _dgrad(dh3, w_gla_out, "gla_out_dgrad")
    dw_gla_out = _wgrad(mix1, dh3, 1, GLA_DV, d, False, False, False, "gla_out_dw")
    dz1, d_gate_w, d_gate_b, d_head_g = _gla_seq_bwd(dmix1, o1, z1, states, gate_w_pad, gate_b_f, head_g_f)
    dh2, d_mix_g1 = _dgrad_norm_bwd(dz1, w_gla_in, h2, row(mix_norm_g[1]), dh3, 640, "gla_in_dgrad")
    dw_gla_in = _wgrad(u2, dz1, GLA_IN_PAD // 640, d, 640, False, True, False, "gla_in_dw")

    dh1, dhp0, d_ffn_g0 = _ffn_bwd_data(dh2, h1, row(ffn_norm_g[0]), hp0, w1g, w2g, 0, "ffn0_bwd")
    dw1_0 = _wgrad(uf0, dhp0, N_CHIPS, d, d, False, True, False, "ffn0_dw1")
    dw2_0 = _wgrad(hp0, dh2, N_CHIPS, d, d, True, False, True, "ffn0_dw2")

    dmix0 = _dgrad(dh1, w_cp_out, "cp_out_dgrad")
    dw_cp_out = _wgrad(mix0, dh1, 1, CONV_DIM + POOL_DIM, d, False, False, False, "cp_out_dw")
    dz0, d_conv_w, d_cp_vec, d_pool_w = _cp_seq_bwd(dmix0, z0, c0, pm0, conv_w_f, cp_ln_g, cp_ln_b, cp_pool_w[0], cp_pool_scale)
    dh0, d_mix_g0 = _dgrad_norm_bwd(dz0, w_cp_in, h0, row(mix_norm_g[0]), dh1, 512, "cp_in_dgrad")
    dw_cp_in = _wgrad(u0, dz0, N_CHIPS, d, CP_IN // N_CHIPS, False, True, False, "cp_in_dw")

    grad_x = dh0[CHUNK:][None]

    gla_in_cols = jnp.moveaxis(dw_gla_in, 0, 1).reshape(d, GLA_IN_PAD)[:, :GLA_IN]
    grads = [dw1_0, dw1_1, dw2_0, dw2_1, dw_cp_in,
             dw_cp_out.reshape(N_CHIPS, -1, d),
             jnp.moveaxis(gla_in_cols.reshape(d, N_CHIPS, GLA_IN // N_CHIPS), 1, 0),
             dw_gla_out.reshape(N_CHIPS, -1, d)]
    dests = [(0, 0), (0, 1), (1, 0), (1, 1), (2, 0), (3, 0), (4, 0), (5, 0)]
    small_full = [dh0[PAD_ROWS:CHUNK], jnp.concatenate([d_mix_g0, d_mix_g1], axis=0),
                  jnp.concatenate([d_ffn_g0, d_ffn_g1], axis=0), d_conv_w[:CONV_WIDTH][None],
                  d_cp_vec[0:1], d_cp_vec[1:2], d_cp_vec[2:3], d_pool_w[None], d_cp_vec[3:4],
                  d_gate_w[:GATE_RANK][None], d_gate_b, d_head_g, d_final_g[0], loss_part[0, 0:1]]
    small_mine = _pack(small_full)
    c_idx = core.reshape(1).astype(jnp.int32)
    recv = _exchange_halves(grads, small_mine)
    chip_sums = [_sum_halves(g, r, c_idx, "chip_sum_%d" % k) for k, (g, r) in enumerate(zip(grads, recv[:-1]))]
    small_chip = _add2(small_mine, recv[-1], "chip_sum_small")
    slots = _scatter_to_chips(chip_sums, small_chip)
    reduced = [_sum_slots(s, "slot_sum_%d" % k) for k, s in enumerate(slots[:-1])]
    small_red = _sum_slots(slots[-1], "slot_sum_small")
    big_grads = _join_halves(reduced, dests, [w.shape for w in big_w])

    big_m = [m_ffn_w1, m_ffn_w2, m_cp_w_in, m_cp_w_out, m_gla_w_in, m_gla_w_out]
    big_v = [v_ffn_w1, v_ffn_w2, v_cp_w_in, v_cp_w_out, v_gla_w_in, v_gla_w_out]
    big_out = []
    for k, (w, g, m, v) in enumerate(zip(big_w, big_grads, big_m, big_v)):
        two_d = lambda a: a.reshape(-1, a.shape[-1])
        outs = _adamw(two_d(w), two_d(g), two_d(m), two_d(v), "adamw_%d" % k)
        big_out.append([g] + [o.reshape(w.shape) for o in outs])

    (g_meta, g_mix, g_ffn, g_conv_w, g_conv_b, g_ln_g, g_ln_b, g_pool_w, g_pool_scale, g_gate_w, g_gate_b, g_head,
     g_final, loss_sum) = _unpack(small_red, [a.shape for a in small_full])
    g_meta = _col_shard(g_meta, chip, meta_tokens.shape[-1])
    g_conv_w = _col_shard(g_conv_w, chip, cp_conv_w.shape[-1])
    g_gate_w = _col_shard(g_gate_w, chip, gla_gate_w2.shape[-1])
    g_gate_b = _col_shard(g_gate_b, chip, gla_gate_b.shape[-1])
    g_head = _col_shard(g_head, chip, gla_head_g.shape[-1])
    small_w = [meta_tokens, mix_norm_g, ffn_norm_g, cp_conv_w, cp_conv_b, cp_ln_g, cp_ln_b, cp_pool_w, cp_pool_scale,
               gla_gate_w2, gla_gate_b, gla_head_g, final_norm_g]
    small_m = [m_meta_tokens, m_mix_norm_g, m_ffn_norm_g, m_cp_conv_w, m_cp_conv_b, m_cp_ln_g, m_cp_ln_b, m_cp_pool_w,
               m_cp_pool_scale, m_gla_gate_w2, m_gla_gate_b, m_gla_head_g, m_final_norm_g]
    small_v = [v_meta_tokens, v_mix_norm_g, v_ffn_norm_g, v_cp_conv_w, v_cp_conv_b, v_cp_ln_g, v_cp_ln_b, v_cp_pool_w,
               v_cp_pool_scale, v_gla_gate_w2, v_gla_gate_b, v_gla_head_g, v_final_norm_g]
    small_g = [g_meta, g_mix, g_ffn, g_conv_w, g_conv_b, g_ln_g, g_ln_b, g_pool_w, g_pool_scale, g_gate_w, g_gate_b,
               g_head, g_final]
    shapes = [w.shape for w in small_w]
    small_g = [g.reshape(s) for g, s in zip(small_g, shapes)]
    s_delta, s_m, s_v = _adamw(_pack(small_w), _pack(small_g), _pack(small_m), _pack(small_v), "adamw_small")
    s_delta, s_m, s_v = _unpack(s_delta, shapes), _unpack(s_m, shapes), _unpack(s_v, shapes)

    order = ["meta", "mix", "ffn", "w1", "w2", "cp_in", "conv_w", "conv_b", "ln_g", "ln_b", "pool_w", "pool_scale",
             "cp_out", "gla_in", "gate_w", "gate_b", "head", "gla_out", "final"]
    small_names = ["meta", "mix", "ffn", "conv_w", "conv_b", "ln_g", "ln_b", "pool_w", "pool_scale", "gate_w", "gate_b",
                   "head", "final"]
    big_names = ["w1", "w2", "cp_in", "cp_out", "gla_in", "gla_out"]
    table = {n: (small_g[i], s_delta[i], s_m[i], s_v[i]) for i, n in enumerate(small_names)}
    table.update({n: tuple(big_out[i]) for i, n in enumerate(big_names)})
    loss = loss_sum.reshape(())
    return (loss, grad_x, *[table[n][0] for n in order], *[table[n][1] for n in order],
            *[table[n][2] for n in order], *[table[n][3] for n in order])
```

```python
import functools

import jax
import jax.numpy as jnp
from jax import lax
from jax.experimental import pallas as pl
from jax.experimental.pallas import tpu as pltpu

F32 = jnp.float32
BF16 = jnp.bfloat16

D_MODEL = 1024
N_META = 16
CHUNK = 64
PAD_ROWS = CHUNK - N_META
EPS = 1e-5
CONV_DIM = 512
CONV_WIDTH = 31
CONV_HALO = 32
POOL_DIM = 512
POOL_WINDOWS = (2, 4, 8, 16)
POOL_GROUP = 128
POOL_HALO = 16
CP_IN = 2 * CONV_DIM + POOL_DIM
GLA_HEADS = 4
GLA_DK = 512
GLA_DV = 1024
GLA_HK = GLA_DK // GLA_HEADS
GLA_HV = GLA_DV // GLA_HEADS
GATE_RANK = 16
GATE_PAD = 128
GATE_NORM = 16.0
GLA_IN = 2 * GLA_DK + 2 * GLA_DV + GATE_RANK
GLA_IN_PAD = 2 * GLA_DK + 2 * GLA_DV + GATE_PAD
N_CHIPS = 4
ADAM_LR = 0.001
ADAM_B1 = 0.9
ADAM_B2 = 0.999
ADAM_EPS = 1e-08
ADAM_WD = 0.01
ADAM_STEP = 10

VMEM_LIMIT_BYTES = 56 * 1024 * 1024
ROW_TILE_TARGET = 832
PACK_WIDTH = 1024
MESH = pl.DeviceIdType.MESH
HBM_SPEC = pl.BlockSpec(memory_space=pltpu.HBM)


def _cparams(*sem):
    return pltpu.CompilerParams(dimension_semantics=sem, vmem_limit_bytes=VMEM_LIMIT_BYTES)


def _row_tile(t, target, mult):
    best = mult
    for cand in range(mult, min(t, target) + 1, mult):
        if t % cand == 0:
            best = cand
    assert t % best == 0, (t, best)
    return best


def _rms(h, g):
    return h * lax.rsqrt(jnp.mean(h * h, axis=-1, keepdims=True) + EPS) * g


def _rms_bwd(h, g, du):
    r = lax.rsqrt(jnp.mean(h * h, axis=-1, keepdims=True) + EPS)
    xhat = h * r
    dxh = du * g
    dh = r * (dxh - xhat * jnp.mean(dxh * xhat, axis=-1, keepdims=True))
    return dh, du * xhat


def _valid_rows(i, tm):
    row = i * tm + lax.broadcasted_iota(jnp.int32, (tm, 1), 0)
    return row >= PAD_ROWS


def _dot(a, b):
    return jnp.dot(a, b, preferred_element_type=F32)


def _dot_nt(a, b):
    return lax.dot_general(a, b, (((1,), (1,)), ((), ())), preferred_element_type=F32)


def _dot_tn(a, b):
    return lax.dot_general(a, b, (((0,), (0,)), ((), ())), preferred_element_type=F32)


def _accumulate(ref, val, first):
    @pl.when(first)
    def _():
        ref[...] = val

    @pl.when(jnp.logical_not(first))
    def _():
        ref[...] += val


def _norm_matmul(h, g, w, nc, name):
    t, d = h.shape
    n = w.shape[1]
    tm = _row_tile(t, ROW_TILE_TARGET, CHUNK)

    def body(h_ref, g_ref, w_ref, z_ref, u_ref):
        u = _rms(h_ref[...], g_ref[...]).astype(BF16)
        u_ref[...] = u
        for n0 in range(0, n, nc):
            z_ref[:, n0:n0 + nc] = _dot(u, w_ref[:, n0:n0 + nc]).astype(BF16)

    return pl.pallas_call(
        body, grid=(t // tm,),
        in_specs=[pl.BlockSpec((tm, d), lambda i: (i, 0)), pl.BlockSpec((1, d), lambda i: (0, 0)),
                  pl.BlockSpec((d, n), lambda i: (0, 0))],
        out_specs=[pl.BlockSpec((tm, n), lambda i: (i, 0)), pl.BlockSpec((tm, d), lambda i: (i, 0))],
        out_shape=[jax.ShapeDtypeStruct((t, n), BF16), jax.ShapeDtypeStruct((t, d), BF16)],
        compiler_params=_cparams("parallel"), name=name)(h, g, w)


def _matmul_residual(a, w, h, name):
    t, k = a.shape
    d = w.shape[1]
    tm = _row_tile(t, ROW_TILE_TARGET, CHUNK)

    def body(a_ref, w_ref, h_ref, o_ref):
        o_ref[...] = h_ref[...] + _dot(a_ref[...], w_ref[...])

    return pl.pallas_call(
        body, grid=(t // tm,),
        in_specs=[pl.BlockSpec((tm, k), lambda i: (i, 0)), pl.BlockSpec((k, d), lambda i: (0, 0)),
                  pl.BlockSpec((tm, d), lambda i: (i, 0))],
        out_specs=pl.BlockSpec((tm, d), lambda i: (i, 0)),
        out_shape=jax.ShapeDtypeStruct((t, d), F32),
        compiler_params=_cparams("parallel"), name=name)(a, w, h)


def _ffn_fwd(h, g, w1g, w2g, layer, name):
    t, d = h.shape
    ns, ffs = w1g.shape[0], w1g.shape[3]
    tm = _row_tile(t, ROW_TILE_TARGET, CHUNK)

    def body(h_ref, g_ref, w1_ref, w2_ref, ho_ref, hp_ref, u_ref, acc_ref):
        s = pl.program_id(1)

        @pl.when(s == 0)
        def _():
            u_ref[...] = _rms(h_ref[...], g_ref[...]).astype(BF16)

        hp = _dot(u_ref[...], w1_ref[...])
        hp_ref[...] = hp.astype(BF16)
        a = jnp.maximum(hp, 0.0)
        _accumulate(acc_ref, _dot((a * a).astype(BF16), w2_ref[...]), s == 0)

        @pl.when(s == ns - 1)
        def _():
            ho_ref[...] = h_ref[...] + acc_ref[...]

    return pl.pallas_call(
        body, grid=(t // tm, ns),
        in_specs=[pl.BlockSpec((tm, d), lambda i, s: (i, 0)), pl.BlockSpec((1, d), lambda i, s: (0, 0)),
                  pl.BlockSpec((None, None, d, ffs), lambda i, s: (s, layer, 0, 0)),
                  pl.BlockSpec((None, None, ffs, d), lambda i, s: (s, layer, 0, 0))],
        out_specs=[pl.BlockSpec((tm, d), lambda i, s: (i, 0)), pl.BlockSpec((tm, ffs), lambda i, s: (i, s)),
                   pl.BlockSpec((tm, d), lambda i, s: (i, 0))],
        out_shape=[jax.ShapeDtypeStruct((t, d), F32), jax.ShapeDtypeStruct((t, ns * ffs), BF16),
                   jax.ShapeDtypeStruct((t, d), BF16)],
        scratch_shapes=[pltpu.VMEM((tm, d), F32)],
        compiler_params=_cparams("parallel", "arbitrary"), name=name)(h, g, w1g, w2g)


def _ffn_bwd_data(dh, h, g, hp, w1g, w2g, layer, name):
    t, d = h.shape
    ns, ffs = w1g.shape[0], w1g.shape[3]
    tm = _row_tile(t, ROW_TILE_TARGET, CHUNK)

    def body(dh_ref, h_ref, g_ref, hp_ref, w1_ref, w2_ref, dhi_ref, dhp_ref, dg_ref, acc_ref):
        i, s = pl.program_id(0), pl.program_id(1)
        da = _dot_nt(dh_ref[...].astype(BF16), w2_ref[...])
        dhp = (da * (2.0 * jnp.maximum(hp_ref[...].astype(F32), 0.0))).astype(BF16)
        dhp_ref[...] = dhp
        _accumulate(acc_ref, _dot_nt(dhp, w1_ref[...]), s == 0)

        @pl.when(s == ns - 1)
        def _():
            dhn, dgr = _rms_bwd(h_ref[...], g_ref[...], acc_ref[...])
            dhi_ref[...] = jnp.where(_valid_rows(i, tm), dh_ref[...] + dhn, 0.0)
            _accumulate(dg_ref, jnp.sum(dgr, axis=0, keepdims=True), i == 0)

    return pl.pallas_call(
        body, grid=(t // tm, ns),
        in_specs=[pl.BlockSpec((tm, d), lambda i, s: (i, 0)), pl.BlockSpec((tm, d), lambda i, s: (i, 0)),
                  pl.BlockSpec((1, d), lambda i, s: (0, 0)), pl.BlockSpec((tm, ffs), lambda i, s: (i, s)),
                  pl.BlockSpec((None, None, d, ffs), lambda i, s: (s, layer, 0, 0)),
                  pl.BlockSpec((None, None, ffs, d), lambda i, s: (s, layer, 0, 0))],
        out_specs=[pl.BlockSpec((tm, d), lambda i, s: (i, 0)), pl.BlockSpec((tm, ffs), lambda i, s: (i, s)),
                   pl.BlockSpec((1, d), lambda i, s: (0, 0))],
        out_shape=[jax.ShapeDtypeStruct((t, d), F32), jax.ShapeDtypeStruct((t, ns * ffs), BF16),
                   jax.ShapeDtypeStruct((1, d), F32)],
        scratch_shapes=[pltpu.VMEM((tm, d), F32)],
        compiler_params=_cparams("arbitrary", "arbitrary"), name=name)(dh, h, g, hp, w1g, w2g)


def _wgrad(x, dy, nb, xc, yc, x_by_block, dy_by_block, relu2, name):
    t = x.shape[0]
    tm = _row_tile(t, ROW_TILE_TARGET, CHUNK)

    def body(x_ref, dy_ref, o_ref):
        k = pl.program_id(1)
        xv = x_ref[...]
        if relu2:
            xv = jnp.maximum(xv.astype(F32), 0.0)
            xv = xv * xv
        _accumulate(o_ref, _dot_tn(xv.astype(BF16), dy_ref[...].astype(BF16)), k == 0)

    return pl.pallas_call(
        body, grid=(nb, t // tm),
        in_specs=[pl.BlockSpec((tm, xc), (lambda b, k: (k, b)) if x_by_block else (lambda b, k: (k, 0))),
                  pl.BlockSpec((tm, yc), (lambda b, k: (k, b)) if dy_by_block else (lambda b, k: (k, 0)))],
        out_specs=pl.BlockSpec((None, xc, yc), lambda b, k: (b, 0, 0)),
        out_shape=jax.ShapeDtypeStruct((nb, xc, yc), F32),
        compiler_params=_cparams("parallel", "arbitrary"), name=name)(x, dy)


def _dgrad(dh, w, name):
    t, d = dh.shape
    k = w.shape[0]
    tm = _row_tile(t, ROW_TILE_TARGET, CHUNK)

    def body(dh_ref, w_ref, o_ref):
        o_ref[...] = _dot_nt(dh_ref[...].astype(BF16), w_ref[...]).astype(BF16)

    return pl.pallas_call(
        body, grid=(t // tm,),
        in_specs=[pl.BlockSpec((tm, d), lambda i: (i, 0)), pl.BlockSpec((k, d), lambda i: (0, 0))],
        out_specs=pl.BlockSpec((tm, k), lambda i: (i, 0)),
        out_shape=jax.ShapeDtypeStruct((t, k), BF16),
        compiler_params=_cparams("parallel"), name=name)(dh, w)


def _dgrad_norm_bwd(dz, w, h, g, dh, nc, name):
    t, d = h.shape
    n = w.shape[1]
    tm = _row_tile(t, ROW_TILE_TARGET // 2, 16)

    def body(dz_ref, w_ref, h_ref, g_ref, dh_ref, dhi_ref, dg_ref):
        i = pl.program_id(0)
        du = jnp.zeros((tm, d), F32)
        for n0 in range(0, n, nc):
            du = du + _dot_nt(dz_ref[:, n0:n0 + nc], w_ref[:, n0:n0 + nc])
        dhn, dgr = _rms_bwd(h_ref[...], g_ref[...], du)
        dhi_ref[...] = jnp.where(_valid_rows(i, tm), dh_ref[...] + dhn, 0.0)
        _accumulate(dg_ref, jnp.sum(dgr, axis=0, keepdims=True), i == 0)

    return pl.pallas_call(
        body, grid=(t // tm,),
        in_specs=[pl.BlockSpec((tm, n), lambda i: (i, 0)), pl.BlockSpec((d, n), lambda i: (0, 0)),
                  pl.BlockSpec((tm, d), lambda i: (i, 0)), pl.BlockSpec((1, d), lambda i: (0, 0)),
                  pl.BlockSpec((tm, d), lambda i: (i, 0))],
        out_specs=[pl.BlockSpec((tm, d), lambda i: (i, 0)), pl.BlockSpec((1, d), lambda i: (0, 0))],
        out_shape=[jax.ShapeDtypeStruct((t, d), F32), jax.ShapeDtypeStruct((1, d), F32)],
        compiler_params=_cparams("arbitrary"), name=name)(dz, w, h, g, dh)


def _loss_bwd(h, g, target):
    t, d = h.shape
    tm = _row_tile(t, ROW_TILE_TARGET, CHUNK)

    def body(h_ref, g_ref, t_ref, dh_ref, dg_ref, loss_ref):
        i = pl.program_id(0)
        row = i * tm + lax.broadcasted_iota(jnp.int32, (tm, 1), 0)
        keep = row >= CHUNK
        hv, gv = h_ref[...], g_ref[...]
        err = jnp.where(keep, _rms(hv, gv) - t_ref[...], 0.0)
        part = 0.5 * jnp.sum(jnp.mean(err * err, axis=-1, keepdims=True), axis=0, keepdims=True)
        dhn, dgr = _rms_bwd(hv, gv, err * (1.0 / d))
        dh_ref[...] = dhn
        _accumulate(dg_ref, jnp.sum(dgr, axis=0, keepdims=True), i == 0)
        _accumulate(loss_ref, jnp.broadcast_to(part, (8, 128)), i == 0)

    return pl.pallas_call(
        body, grid=(t // tm,),
        in_specs=[pl.BlockSpec((tm, d), lambda i: (i, 0)), pl.BlockSpec((1, d), lambda i: (0, 0)),
                  pl.BlockSpec((tm, d), lambda i: (i, 0))],
        out_specs=[pl.BlockSpec((tm, d), lambda i: (i, 0)), pl.BlockSpec((1, d), lambda i: (0, 0)),
                   pl.BlockSpec((8, 128), lambda i: (0, 0))],
        out_shape=[jax.ShapeDtypeStruct((t, d), F32), jax.ShapeDtypeStruct((1, d), F32),
                   jax.ShapeDtypeStruct((8, 128), F32)],
        compiler_params=_cparams("arbitrary"), name="loss_bwd")(h, g, target)


CONV_BLOCK = 32


def _silu(x):
    return x * jax.nn.sigmoid(x)


def _cp_seq_fwd(z, conv_w, conv_b, ln_g, ln_b, pool_w, pool_scale):
    t = z.shape[0]
    tm = _row_tile(t, ROW_TILE_TARGET, CHUNK)

    def body(z_ref, cw_ref, cb_ref, lg_ref, lb_ref, pw_ref, ps_ref, c_ref, pm_ref, mix_ref, gbuf, pbuf):
        i = pl.program_id(0)

        @pl.when(i == 0)
        def _():
            gbuf[0:CONV_HALO, :] = jnp.zeros((CONV_HALO, CONV_DIM), F32)
            pbuf[0:POOL_HALO, :] = jnp.zeros((POOL_HALO, POOL_DIM), F32)

        @pl.when(i > 0)
        def _():
            gbuf[0:CONV_HALO, :] = gbuf[tm:tm + CONV_HALO, :]
            pbuf[0:POOL_HALO, :] = pbuf[tm:tm + POOL_HALO, :]

        av = z_ref[:, 0:CONV_DIM].astype(F32)
        ag = z_ref[:, CONV_DIM:2 * CONV_DIM].astype(F32)
        gbuf[CONV_HALO:CONV_HALO + tm, :] = av * jax.nn.sigmoid(ag)
        pbuf[POOL_HALO:POOL_HALO + tm, :] = z_ref[:, 2 * CONV_DIM:CP_IN].astype(F32)

        def conv_block(rb, carry):
            base = pl.multiple_of(rb * CONV_BLOCK, CONV_BLOCK)
            win = gbuf[pl.ds(base, CONV_BLOCK + CONV_HALO), :]
            acc = jnp.zeros((CONV_BLOCK, CONV_DIM), F32)
            for k in range(CONV_WIDTH):
                off = CONV_HALO - (CONV_WIDTH - 1) + k
                acc = acc + cw_ref[k:k + 1, :] * win[off:off + CONV_BLOCK, :]
            c_ref[pl.ds(base, CONV_BLOCK), :] = acc + cb_ref[...]
            return carry

        lax.fori_loop(0, tm // CONV_BLOCK, conv_block, 0)

        c = c_ref[...]
        mu = jnp.mean(c, axis=-1, keepdims=True)
        xc = c - mu
        ln = xc * lax.rsqrt(jnp.mean(xc * xc, axis=-1, keepdims=True) + EPS) * lg_ref[...] + lb_ref[...]
        row = i * tm + lax.broadcasted_iota(jnp.int32, (tm, 1), 0)
        mix_ref[:, 0:CONV_DIM] = jnp.where(row >= PAD_ROWS, _silu(ln), 0.0).astype(BF16)

        tpos = (row - PAD_ROWS + 1).astype(F32)
        for gi, wdw in enumerate(POOL_WINDOWS):
            lo = POOL_GROUP * gi
            cur = pbuf[POOL_HALO:POOL_HALO + tm, lo:lo + POOL_GROUP]
            sacc = cur
            for j in range(1, wdw):
                sacc = sacc + pbuf[POOL_HALO - j:POOL_HALO - j + tm, lo:lo + POOL_GROUP]
            pm = (sacc / jnp.clip(tpos, 1.0, float(wdw)) - cur).astype(BF16)
            pm_ref[:, lo:lo + POOL_GROUP] = pm
            pg = _dot(pm, pw_ref[gi].astype(BF16))
            mix_ref[:, CONV_DIM + lo:CONV_DIM + lo + POOL_GROUP] = (pg * ps_ref[:, lo:lo + POOL_GROUP]).astype(BF16)

    vec = pl.BlockSpec((1, CONV_DIM), lambda i: (0, 0))
    return pl.pallas_call(
        body, grid=(t // tm,),
        in_specs=[pl.BlockSpec((tm, CP_IN), lambda i: (i, 0)),
                  pl.BlockSpec((CONV_WIDTH, CONV_DIM), lambda i: (0, 0)), vec, vec, vec,
                  pl.BlockSpec((len(POOL_WINDOWS), POOL_GROUP, POOL_GROUP), lambda i: (0, 0, 0)), vec],
        out_specs=[pl.BlockSpec((tm, CONV_DIM), lambda i: (i, 0)), pl.BlockSpec((tm, POOL_DIM), lambda i: (i, 0)),
                   pl.BlockSpec((tm, CONV_DIM + POOL_DIM), lambda i: (i, 0))],
        out_shape=[jax.ShapeDtypeStruct((t, CONV_DIM), F32), jax.ShapeDtypeStruct((t, POOL_DIM), BF16),
                   jax.ShapeDtypeStruct((t, CONV_DIM + POOL_DIM), BF16)],
        scratch_shapes=[pltpu.VMEM((tm + CONV_HALO, CONV_DIM), F32), pltpu.VMEM((tm + POOL_HALO, POOL_DIM), F32)],
        compiler_params=_cparams("arbitrary"), name="cp_seq_fwd")(z, conv_w, conv_b, ln_g, ln_b, pool_w, pool_scale)


def _cp_seq_bwd(dmix, z, c, pm, conv_w, ln_g, ln_b, pool_w, pool_scale):
    t = z.shape[0]
    tm = _row_tile(t, ROW_TILE_TARGET, CHUNK)
    nt = t // tm

    def body(dmix_ref, z_ref, c_ref, pm_ref, cw_ref, lg_ref, lb_ref, pw_ref, ps_ref,
             dz_ref, dcw_ref, dvec_ref, dpw_ref, dcbuf, qbuf, glu_buf, dwacc):
        i = pl.program_id(0)
        tile = nt - 1 - i

        @pl.when(i == 0)
        def _():
            dcbuf[tm:tm + CONV_HALO, :] = jnp.zeros((CONV_HALO, CONV_DIM), F32)
            qbuf[tm:tm + POOL_HALO, :] = jnp.zeros((POOL_HALO, POOL_DIM), F32)
            dcw_ref[...] = jnp.zeros_like(dcw_ref)
            dwacc[...] = jnp.zeros_like(dwacc)
            dvec_ref[...] = jnp.zeros_like(dvec_ref)
            dpw_ref[...] = jnp.zeros_like(dpw_ref)

        @pl.when(i > 0)
        def _():
            dcbuf[tm:tm + CONV_HALO, :] = dcbuf[0:CONV_HALO, :]
            qbuf[tm:tm + POOL_HALO, :] = qbuf[0:POOL_HALO, :]

        row = tile * tm + lax.broadcasted_iota(jnp.int32, (tm, 1), 0)
        cv = c_ref[...]
        mu = jnp.mean(cv, axis=-1, keepdims=True)
        xc = cv - mu
        rstd = lax.rsqrt(jnp.mean(xc * xc, axis=-1, keepdims=True) + EPS)
        xhat = xc * rstd
        ln = xhat * lg_ref[...] + lb_ref[...]
        sg = jax.nn.sigmoid(ln)
        da = jnp.where(row >= PAD_ROWS, dmix_ref[:, 0:CONV_DIM].astype(F32), 0.0)
        dln = da * (sg * (1.0 + ln * (1.0 - sg)))
        dxh = dln * lg_ref[...]
        dc = rstd * (dxh - jnp.mean(dxh, axis=-1, keepdims=True) - xhat * jnp.mean(dxh * xhat, axis=-1, keepdims=True))
        dcbuf[0:tm, :] = dc
        dvec_ref[0:1, :] += jnp.sum(dc, axis=0, keepdims=True)
        dvec_ref[1:2, :] += jnp.sum(dln * xhat, axis=0, keepdims=True)
        dvec_ref[2:3, :] += jnp.sum(dln, axis=0, keepdims=True)

        av = z_ref[:, 0:CONV_DIM].astype(F32)
        sig_g = jax.nn.sigmoid(z_ref[:, CONV_DIM:2 * CONV_DIM].astype(F32))
        glu_buf[...] = av * sig_g

        def conv_block(rb, carry):
            base = pl.multiple_of(rb * CONV_BLOCK, CONV_BLOCK)
            win = dcbuf[pl.ds(base, CONV_BLOCK + CONV_HALO), :]
            glu = glu_buf[pl.ds(base, CONV_BLOCK), :]
            acc = jnp.zeros((CONV_BLOCK, CONV_DIM), F32)
            for k in range(CONV_WIDTH):
                off = CONV_WIDTH - 1 - k
                slab = win[off:off + CONV_BLOCK, :]
                acc = acc + cw_ref[k:k + 1, :] * slab
                prod = slab * glu
                part = prod[0:8]
                for q in range(1, CONV_BLOCK // 8):
                    part = part + prod[8 * q:8 * q + 8]
                dwacc[k] += part
            glu_buf[pl.ds(base, CONV_BLOCK), :] = acc
            return carry

        lax.fori_loop(0, tm // CONV_BLOCK, conv_block, 0)

        @pl.when(i == nt - 1)
        def _():
            for k in range(CONV_WIDTH):
                dcw_ref[k:k + 1, :] = jnp.sum(dwacc[k], axis=0, keepdims=True)
        dglu = glu_buf[...]
        dz_ref[:, 0:CONV_DIM] = (dglu * sig_g).astype(BF16)
        dz_ref[:, CONV_DIM:2 * CONV_DIM] = (dglu * av * sig_g * (1.0 - sig_g)).astype(BF16)

        tpos = (row - PAD_ROWS + 1).astype(F32)
        for gi, wdw in enumerate(POOL_WINDOWS):
            lo = POOL_GROUP * gi
            dp = dmix_ref[:, CONV_DIM + lo:CONV_DIM + lo + POOL_GROUP].astype(F32)
            pmv = pm_ref[:, lo:lo + POOL_GROUP]
            pwb = pw_ref[gi].astype(BF16)
            dvec_ref[3:4, lo:lo + POOL_GROUP] += jnp.sum(dp * _dot(pmv, pwb), axis=0, keepdims=True)
            dq = (dp * ps_ref[:, lo:lo + POOL_GROUP]).astype(BF16)
            dpw_ref[gi] += _dot_tn(pmv, dq)
            dpm = _dot_nt(dq, pwb)
            qbuf[0:tm, lo:lo + POOL_GROUP] = dpm / jnp.clip(tpos, 1.0, float(wdw))
            sacc = -dpm
            for j in range(wdw):
                sacc = sacc + qbuf[j:j + tm, lo:lo + POOL_GROUP]
            dz_ref[:, 2 * CONV_DIM + lo:2 * CONV_DIM + lo + POOL_GROUP] = sacc.astype(BF16)

    vec = pl.BlockSpec((1, CONV_DIM), lambda i: (0, 0))
    rev = lambda i: (nt - 1 - i, 0)
    return pl.pallas_call(
        body, grid=(nt,),
        in_specs=[pl.BlockSpec((tm, CONV_DIM + POOL_DIM), rev), pl.BlockSpec((tm, CP_IN), rev),
                  pl.BlockSpec((tm, CONV_DIM), rev), pl.BlockSpec((tm, POOL_DIM), rev),
                  pl.BlockSpec((CONV_WIDTH, CONV_DIM), lambda i: (0, 0)), vec, vec,
                  pl.BlockSpec((len(POOL_WINDOWS), POOL_GROUP, POOL_GROUP), lambda i: (0, 0, 0)), vec],
        out_specs=[pl.BlockSpec((tm, CP_IN), rev), pl.BlockSpec((CONV_WIDTH + 1, CONV_DIM), lambda i: (0, 0)),
                   pl.BlockSpec((8, CONV_DIM), lambda i: (0, 0)),
                   pl.BlockSpec((len(POOL_WINDOWS), POOL_GROUP, POOL_GROUP), lambda i: (0, 0, 0))],
        out_shape=[jax.ShapeDtypeStruct((t, CP_IN), BF16), jax.ShapeDtypeStruct((CONV_WIDTH + 1, CONV_DIM), F32),
                   jax.ShapeDtypeStruct((8, CONV_DIM), F32),
                   jax.ShapeDtypeStruct((len(POOL_WINDOWS), POOL_GROUP, POOL_GROUP), F32)],
        scratch_shapes=[pltpu.VMEM((tm + CONV_HALO, CONV_DIM), F32), pltpu.VMEM((tm + POOL_HALO, POOL_DIM), F32),
                        pltpu.VMEM((tm, CONV_DIM), F32), pltpu.VMEM((CONV_WIDTH + 1, 8, CONV_DIM), F32)],
        compiler_params=_cparams("arbitrary"), name="cp_seq_bwd")(dmix, z, c, pm, conv_w, ln_g, ln_b, pool_w, pool_scale)


Q0, K0, V0, G0, R0 = 0, GLA_DK, 2 * GLA_DK, 2 * GLA_DK + GLA_DV, 2 * GLA_DK + 2 * GLA_DV


def _split3(x):
    hi = x.astype(BF16)
    r1 = x - hi.astype(F32)
    mid = r1.astype(BF16)
    lo = (r1 - mid.astype(F32)).astype(BF16)
    return hi, mid, lo


def _tri(strict):
    r = lax.broadcasted_iota(jnp.int32, (CHUNK, CHUNK), 0)
    c = lax.broadcasted_iota(jnp.int32, (CHUNK, CHUNK), 1)
    return ((r > c) if strict else (r >= c)).astype(BF16)


def _gate_decay(r, gw_ref, gb_ref, tri):
    pre = _dot(r, gw_ref[...]) + gb_ref[...]
    lac = (jnp.minimum(pre, 0.0) - jnp.log(1.0 + jnp.exp(-jnp.abs(pre)))) * (1.0 / GATE_NORM)
    hi, mid, lo = _split3(lac)
    cum = _dot(tri, hi) + _dot(tri, mid) + _dot(tri, lo)
    return pre, cum, cum[CHUNK - 1:CHUNK, :]


def _gla_seq_fwd(z, gate_w, gate_b, head_g):
    t = z.shape[0]
    tm = _row_tile(t, ROW_TILE_TARGET, CHUNK)
    cpt = tm // CHUNK
    scale = GLA_HK ** -0.5

    def body(z_ref, gw_ref, gb_ref, hg_ref, o_ref, mix_ref, st_ref, state):
        @pl.when(pl.program_id(0) == 0)
        def _():
            state[...] = jnp.zeros_like(state)

        tri = _tri(False)

        def chunk(ci, carry):
            r0 = pl.multiple_of(ci * CHUNK, CHUNK)
            rows = pl.ds(r0, CHUNK)
            _, cum, tot = _gate_decay(z_ref[rows, R0:R0 + GATE_PAD], gw_ref, gb_ref, tri)
            dec = jnp.exp(tot - cum)
            e = jnp.exp(tot)
            st_ref[ci] = state[...].astype(BF16)
            for hd in range(GLA_HEADS):
                ks = slice(hd * GLA_HK, (hd + 1) * GLA_HK)
                vs = slice(hd * GLA_HV, (hd + 1) * GLA_HV)
                kdec = (z_ref[rows, K0 + hd * GLA_HK:K0 + (hd + 1) * GLA_HK].astype(F32) * dec[:, ks]).astype(BF16)
                v = z_ref[rows, V0 + hd * GLA_HV:V0 + (hd + 1) * GLA_HV]
                st = state[vs, :] * e[:, ks] + _dot_tn(v, kdec)
                state[vs, :] = st
                q = z_ref[rows, Q0 + hd * GLA_HK:Q0 + (hd + 1) * GLA_HK]
                o = _dot_nt(q, st.astype(BF16)) * scale
                ob = o.astype(BF16)
                o_ref[rows, vs] = ob
                on = _rms(ob.astype(F32), hg_ref[...])
                gv = z_ref[rows, G0 + hd * GLA_HV:G0 + (hd + 1) * GLA_HV].astype(F32)
                mix_ref[rows, vs] = (on * _silu(gv)).astype(BF16)
            return carry

        lax.fori_loop(0, cpt, chunk, 0)

    return pl.pallas_call(
        body, grid=(t // tm,),
        in_specs=[pl.BlockSpec((tm, GLA_IN_PAD), lambda i: (i, 0)),
                  pl.BlockSpec((GATE_PAD, GLA_DK), lambda i: (0, 0)), pl.BlockSpec((1, GLA_DK), lambda i: (0, 0)),
                  pl.BlockSpec((1, GLA_HV), lambda i: (0, 0))],
        out_specs=[pl.BlockSpec((tm, GLA_DV), lambda i: (i, 0)), pl.BlockSpec((tm, GLA_DV), lambda i: (i, 0)),
                   pl.BlockSpec((cpt, GLA_DV, GLA_HK), lambda i: (i, 0, 0))],
        out_shape=[jax.ShapeDtypeStruct((t, GLA_DV), BF16), jax.ShapeDtypeStruct((t, GLA_DV), BF16),
                   jax.ShapeDtypeStruct((t // CHUNK, GLA_DV, GLA_HK), BF16)],
        scratch_shapes=[pltpu.VMEM((GLA_DV, GLA_HK), F32)],
        compiler_params=_cparams("arbitrary"), name="gla_seq_fwd")(z, gate_w, gate_b, head_g)


def _gla_seq_bwd(dmix, o, z, states, gate_w, gate_b, head_g):
    t = z.shape[0]
    tm = _row_tile(t, ROW_TILE_TARGET, CHUNK)
    cpt = tm // CHUNK
    nt = t // tm
    scale = GLA_HK ** -0.5

    def body(dmix_ref, o_ref, z_ref, st_ref, gw_ref, gb_ref, hg_ref, dz_ref, dgw_ref, dgb_ref, dhg_ref, dstate):
        @pl.when(pl.program_id(0) == 0)
        def _():
            dstate[...] = jnp.zeros_like(dstate)
            dgw_ref[...] = jnp.zeros_like(dgw_ref)
            dgb_ref[...] = jnp.zeros_like(dgb_ref)
            dhg_ref[...] = jnp.zeros_like(dhg_ref)

        tri = _tri(False)
        tri_strict = _tri(True)

        def chunk(cj, carry):
            ci = cpt - 1 - cj
            r0 = pl.multiple_of(ci * CHUNK, CHUNK)
            rows = pl.ds(r0, CHUNK)
            r = z_ref[rows, R0:R0 + GATE_PAD]
            pre, cum, tot = _gate_decay(r, gw_ref, gb_ref, tri)
            dec = jnp.exp(tot - cum)
            e = jnp.exp(tot)
            dlac_parts = []
            dhg = jnp.zeros((1, GLA_HV), F32)
            for hd in range(GLA_HEADS):
                ks = slice(hd * GLA_HK, (hd + 1) * GLA_HK)
                vs = slice(hd * GLA_HV, (hd + 1) * GLA_HV)
                kcols = slice(K0 + hd * GLA_HK, K0 + (hd + 1) * GLA_HK)
                vcols = slice(V0 + hd * GLA_HV, V0 + (hd + 1) * GLA_HV)
                qcols = slice(Q0 + hd * GLA_HK, Q0 + (hd + 1) * GLA_HK)
                gcols = slice(G0 + hd * GLA_HV, G0 + (hd + 1) * GLA_HV)
                ov = o_ref[rows, vs].astype(F32)
                gv = z_ref[rows, gcols].astype(F32)
                dm = dmix_ref[rows, vs].astype(F32)
                sg = jax.nn.sigmoid(gv)
                rr = lax.rsqrt(jnp.mean(ov * ov, axis=-1, keepdims=True) + EPS)
                xhat = ov * rr
                don = dm * (gv * sg)
                dz_ref[rows, gcols] = (dm * (xhat * hg_ref[...]) * (sg * (1.0 + gv * (1.0 - sg)))).astype(BF16)
                dhg = dhg + jnp.sum(don * xhat, axis=0, keepdims=True)
                dxh = don * hg_ref[...]
                do = (rr * (dxh - xhat * jnp.mean(dxh * xhat, axis=-1, keepdims=True)) * scale).astype(BF16)
                kdec = z_ref[rows, kcols].astype(F32) * dec[:, ks]
                kdb = kdec.astype(BF16)
                v = z_ref[rows, vcols]
                q = z_ref[rows, qcols]
                st_prev = st_ref[ci, vs, :].astype(F32)
                st = st_prev * e[:, ks] + _dot_tn(v, kdb)
                dz_ref[rows, qcols] = _dot(do, st.astype(BF16)).astype(BF16)
                dst = dstate[vs, :] + _dot_tn(do, q)
                dstb = dst.astype(BF16)
                dkdec = _dot(v, dstb)
                dz_ref[rows, vcols] = _dot_nt(kdb, dstb).astype(BF16)
                dtot = jnp.sum(dst * st_prev, axis=0, keepdims=True) * e[:, ks]
                dstate[vs, :] = dst * e[:, ks]
                dz_ref[rows, kcols] = (dkdec * dec[:, ks]).astype(BF16)
                gk = dkdec * kdec
                ghi, gmid, _ = _split3(gk)
                dlac_parts.append(dtot + _dot(tri_strict, ghi) + _dot(tri_strict, gmid))
            dlac = jnp.concatenate(dlac_parts, axis=1)
            dpre = dlac * (1.0 / GATE_NORM) * (1.0 - jax.nn.sigmoid(pre))
            dpb = dpre.astype(BF16)
            dz_ref[rows, R0:R0 + GATE_PAD] = _dot_nt(dpb, gw_ref[...]).astype(BF16)
            dgw_ref[...] += _dot_tn(r, dpb)
            dgb_ref[...] += jnp.sum(dpre, axis=0, keepdims=True)
            dhg_ref[...] += dhg
            return carry

        lax.fori_loop(0, cpt, chunk, 0)

    rev = lambda i: (nt - 1 - i, 0)
    return pl.pallas_call(
        body, grid=(nt,),
        in_specs=[pl.BlockSpec((tm, GLA_DV), rev), pl.BlockSpec((tm, GLA_DV), rev), pl.BlockSpec((tm, GLA_IN_PAD), rev),
                  pl.BlockSpec((cpt, GLA_DV, GLA_HK), lambda i: (nt - 1 - i, 0, 0)),
                  pl.BlockSpec((GATE_PAD, GLA_DK), lambda i: (0, 0)), pl.BlockSpec((1, GLA_DK), lambda i: (0, 0)),
                  pl.BlockSpec((1, GLA_HV), lambda i: (0, 0))],
        out_specs=[pl.BlockSpec((tm, GLA_IN_PAD), rev), pl.BlockSpec((GATE_PAD, GLA_DK), lambda i: (0, 0)),
                   pl.BlockSpec((1, GLA_DK), lambda i: (0, 0)), pl.BlockSpec((1, GLA_HV), lambda i: (0, 0))],
        out_shape=[jax.ShapeDtypeStruct((t, GLA_IN_PAD), BF16), jax.ShapeDtypeStruct((GATE_PAD, GLA_DK), F32),
                   jax.ShapeDtypeStruct((1, GLA_DK), F32), jax.ShapeDtypeStruct((1, GLA_HV), F32)],
        scratch_shapes=[pltpu.VMEM((GLA_DV, GLA_HK), F32)],
        compiler_params=_cparams("arbitrary"), name="gla_seq_bwd")(dmix, o, z, states, gate_w, gate_b, head_g)


def _sum_halves(g, recv, c_idx, name):
    n, r, cdim = g.shape
    h = r // 2
    tr = _row_tile(h, 256, 8)
    nh = h // tr

    def body(c_ref, g_ref, r_ref, o_ref):
        o_ref[...] = (g_ref[...] + r_ref[...]).astype(BF16)

    return pl.pallas_call(
        body,
        grid_spec=pltpu.PrefetchScalarGridSpec(
            num_scalar_prefetch=1, grid=(n, nh),
            in_specs=[pl.BlockSpec((None, tr, cdim), lambda s, i, c: (s, c[0] * nh + i, 0)),
                      pl.BlockSpec((None, tr, cdim), lambda s, i, c: (s, i, 0))],
            out_specs=pl.BlockSpec((None, tr, cdim), lambda s, i, c: (s, i, 0))),
        out_shape=jax.ShapeDtypeStruct((n, h, cdim), BF16),
        compiler_params=_cparams("parallel", "parallel"), name=name)(c_idx, g, recv)


def _sum_slots(x, name):
    n, r, cdim = x.shape
    tr = _row_tile(r, 256, 8)

    def body(x_ref, o_ref):
        acc = x_ref[0].astype(F32)
        for j in range(1, n):
            acc = acc + x_ref[j].astype(F32)
        o_ref[...] = acc

    return pl.pallas_call(
        body, grid=(r // tr,),
        in_specs=[pl.BlockSpec((n, tr, cdim), lambda i: (0, i, 0))],
        out_specs=pl.BlockSpec((tr, cdim), lambda i: (i, 0)),
        out_shape=jax.ShapeDtypeStruct((r, cdim), F32),
        compiler_params=_cparams("parallel"), name=name)(x)


def _add2(a, b, name):
    r, cdim = a.shape
    tr = _row_tile(r, 256, 8)

    def body(a_ref, b_ref, o_ref):
        o_ref[...] = a_ref[...] + b_ref[...]

    spec = pl.BlockSpec((tr, cdim), lambda i: (i, 0))
    return pl.pallas_call(body, grid=(r // tr,), in_specs=[spec, spec], out_specs=spec,
                          out_shape=jax.ShapeDtypeStruct((r, cdim), F32),
                          compiler_params=_cparams("parallel"), name=name)(a, b)


def _adamw(w, g, m, v, name):
    r, cdim = w.shape
    tr = _row_tile(r, 256, 8)

    def body(w_ref, g_ref, m_ref, v_ref, d_ref, mo_ref, vo_ref):
        gv = g_ref[...]
        mn = ADAM_B1 * m_ref[...] + (1.0 - ADAM_B1) * gv
        vn = ADAM_B2 * v_ref[...] + (1.0 - ADAM_B2) * (gv * gv)
        m_hat = mn / (1.0 - ADAM_B1 ** ADAM_STEP)
        v_hat = vn / (1.0 - ADAM_B2 ** ADAM_STEP)
        d_ref[...] = -ADAM_LR * (m_hat / (jnp.sqrt(v_hat) + ADAM_EPS) + ADAM_WD * w_ref[...])
        mo_ref[...] = mn
        vo_ref[...] = vn

    spec = pl.BlockSpec((tr, cdim), lambda i: (i, 0))
    shp = jax.ShapeDtypeStruct((r, cdim), F32)
    return pl.pallas_call(body, grid=(r // tr,), in_specs=[spec] * 4, out_specs=[spec] * 3,
                          out_shape=[shp] * 3, compiler_params=_cparams("parallel"), name=name)(w, g, m, v)


def _split_rows(a):
    return a.reshape(a.shape[0], 2, a.shape[1] // 2, a.shape[2])


def _place():
    x, y, c = lax.axis_index("x"), lax.axis_index("y"), lax.axis_index("c")
    chips = [(1 - x, y), (x, 1 - y), (1 - x, 1 - y)]
    return x, y, c, chips


def _remote(src, dst, send_sem, recv_sem, to):
    return pltpu.make_async_remote_copy(src_ref=src, dst_ref=dst, send_sem=send_sem, recv_sem=recv_sem,
                                        device_id=to, device_id_type=MESH)


def _gather_weights(bigs, small):
    nb = len(bigs)
    shapes = [b.shape for b in bigs]
    bigs = [_split_rows(b) for b in bigs]

    def body(*refs):
        big_in, small_in = refs[:nb], refs[nb]
        big_out, small_out = refs[nb + 1:2 * nb + 1], refs[2 * nb + 1]
        ici_send, ici_recv, d2d_send, d2d_recv, loc_sem = refs[2 * nb + 2:]
        x, y, c, chips = _place()
        me = 2 * x + y
        sib = (x, y, 1 - c)

        def half(ref, k, which):
            return ref.at[:, which]

        local = [pltpu.make_async_copy(big_in[k], big_out[k].at[me], loc_sem.at[k]) for k in range(nb)]
        local.append(pltpu.make_async_copy(small_in, small_out.at[me], loc_sem.at[nb]))
        for cp in local:
            cp.start()
        sends = []
        for j, (px, py) in enumerate(chips):
            for k in range(nb):
                sends.append(_remote(half(big_in[k], k, c), half(big_out[k].at[me], k, c),
                                     ici_send.at[k * 3 + j], ici_recv.at[k * 3 + j], (px, py, c)))
            sends.append(_remote(small_in, small_out.at[me], ici_send.at[nb * 3 + j], ici_recv.at[nb * 3 + j], (px, py, c)))
        for cp in sends:
            cp.start()
        passed = []
        for j, (px, py) in enumerate(chips):
            frm = 2 * px + py
            for k in range(nb):
                landed = half(big_out[k].at[frm], k, c)
                _remote(landed, landed, ici_send.at[k * 3 + j], ici_recv.at[k * 3 + j], (px, py, c)).wait_recv()
                fwd = _remote(landed, landed, d2d_send.at[k * 3 + j], d2d_recv.at[k * 3 + j], sib)
                fwd.start()
                passed.append(fwd)
            _remote(small_in, small_out.at[frm], ici_send.at[nb * 3 + j], ici_recv.at[nb * 3 + j], (px, py, c)).wait_recv()
        for j, (px, py) in enumerate(chips):
            frm = 2 * px + py
            for k in range(nb):
                theirs = half(big_out[k].at[frm], k, 1 - c)
                _remote(theirs, theirs, d2d_send.at[k * 3 + j], d2d_recv.at[k * 3 + j], sib).wait_recv()
        for cp in sends + passed:
            cp.wait_send()
        for cp in local:
            cp.wait()

    out_shape = [jax.ShapeDtypeStruct((N_CHIPS,) + b.shape, b.dtype) for b in bigs]
    out_shape.append(jax.ShapeDtypeStruct((N_CHIPS,) + small.shape, small.dtype))
    outs = pl.pallas_call(
        body, in_specs=[HBM_SPEC] * (nb + 1), out_specs=[HBM_SPEC] * (nb + 1), out_shape=out_shape,
        scratch_shapes=[pltpu.SemaphoreType.DMA((3 * nb + 3,)), pltpu.SemaphoreType.DMA((3 * nb + 3,)),
                        pltpu.SemaphoreType.DMA((3 * nb,)), pltpu.SemaphoreType.DMA((3 * nb,)),
                        pltpu.SemaphoreType.DMA((nb + 1,))],
        name="gather_weights")(*bigs, small)
    return [o.reshape((N_CHIPS,) + s) for o, s in zip(outs[:-1], shapes)] + [outs[-1]]


def _exchange_halves(grads, small):
    ng = len(grads)
    grads = [_split_rows(g) for g in grads]

    def body(*refs):
        g_in, s_in = refs[:ng], refs[ng]
        g_out, s_out = refs[ng + 1:2 * ng + 1], refs[2 * ng + 1]
        send_sem, recv_sem = refs[2 * ng + 2:]
        x, y, c, _ = _place()
        sib = (x, y, 1 - c)
        copies = []
        for k in range(ng):
            copies.append(_remote(g_in[k].at[:, 1 - c], g_out[k], send_sem.at[k], recv_sem.at[k], sib))
        copies.append(_remote(s_in, s_out, send_sem.at[ng], recv_sem.at[ng], sib))
        for cp in copies:
            cp.start()
        for cp in copies:
            cp.wait_recv()
        for cp in copies:
            cp.wait_send()

    out_shape = [jax.ShapeDtypeStruct((g.shape[0], g.shape[2], g.shape[3]), g.dtype) for g in grads]
    out_shape.append(jax.ShapeDtypeStruct(small.shape, small.dtype))
    return pl.pallas_call(
        body, in_specs=[HBM_SPEC] * (ng + 1), out_specs=[HBM_SPEC] * (ng + 1), out_shape=out_shape,
        scratch_shapes=[pltpu.SemaphoreType.DMA((ng + 1,)), pltpu.SemaphoreType.DMA((ng + 1,))],
        name="exchange_halves")(*grads, small)


def _scatter_to_chips(parts, small):
    npart = len(parts)

    def body(*refs):
        p_in, s_in = refs[:npart], refs[npart]
        p_out, s_out = refs[npart + 1:2 * npart + 1], refs[2 * npart + 1]
        send_sem, recv_sem, loc_sem = refs[2 * npart + 2:]
        x, y, c, chips = _place()
        me = 2 * x + y
        local = [pltpu.make_async_copy(p_in[k].at[me], p_out[k].at[me], loc_sem.at[k]) for k in range(npart)]
        local.append(pltpu.make_async_copy(s_in, s_out.at[me], loc_sem.at[npart]))
        for cp in local:
            cp.start()
        sends, recvs = [], []
        for j, (px, py) in enumerate(chips):
            to = 2 * px + py
            for k in range(npart + 1):
                src = s_in if k == npart else p_in[k].at[to]
                out = s_out if k == npart else p_out[k]
                sems = (send_sem.at[k * 3 + j], recv_sem.at[k * 3 + j])
                sends.append(_remote(src, out.at[me], *sems, (px, py, c)))
                recvs.append(_remote(src, out.at[to], *sems, (px, py, c)))
        for cp in sends:
            cp.start()
        for cp in recvs:
            cp.wait_recv()
        for cp in sends:
            cp.wait_send()
        for cp in local:
            cp.wait()

    out_shape = [jax.ShapeDtypeStruct(p.shape, p.dtype) for p in parts]
    out_shape.append(jax.ShapeDtypeStruct((N_CHIPS,) + small.shape, small.dtype))
    return pl.pallas_call(
        body, in_specs=[HBM_SPEC] * (npart + 1), out_specs=[HBM_SPEC] * (npart + 1), out_shape=out_shape,
        scratch_shapes=[pltpu.SemaphoreType.DMA((3 * npart + 3,)), pltpu.SemaphoreType.DMA((3 * npart + 3,)),
                        pltpu.SemaphoreType.DMA((npart + 1,))],
        name="scatter_to_chips")(*parts, small)


def _join_halves(reduced, dests, out_shapes):
    nr = len(reduced)

    def body(*refs):
        r_in = refs[:nr]
        outs = refs[nr:nr + len(out_shapes)]
        send_sem, recv_sem, loc_sem = refs[nr + len(out_shapes):]
        x, y, c, _ = _place()
        sib = (x, y, 1 - c)
        local, sends, recvs = [], [], []
        for k in range(nr):
            oi, layer = dests[k]
            mine = outs[oi].at[layer, c]
            theirs = outs[oi].at[layer, 1 - c]
            local.append(pltpu.make_async_copy(r_in[k], mine, loc_sem.at[k]))
            sends.append(_remote(r_in[k], mine, send_sem.at[k], recv_sem.at[k], sib))
            recvs.append(_remote(r_in[k], theirs, send_sem.at[k], recv_sem.at[k], sib))
        for cp in local + sends:
            cp.start()
        for cp in recvs:
            cp.wait_recv()
        for cp in sends:
            cp.wait_send()
        for cp in local:
            cp.wait()

    split = [(s[0], 2, s[1] // 2, s[2]) for s in out_shapes]
    outs = pl.pallas_call(
        body, in_specs=[HBM_SPEC] * nr, out_specs=[HBM_SPEC] * len(out_shapes),
        out_shape=[jax.ShapeDtypeStruct(s, F32) for s in split],
        scratch_shapes=[pltpu.SemaphoreType.DMA((nr,)), pltpu.SemaphoreType.DMA((nr,)), pltpu.SemaphoreType.DMA((nr,))],
        name="join_halves")(*reduced)
    return [o.reshape(s) for o, s in zip(outs, out_shapes)]


def _pack(arrs):
    flat = jnp.concatenate([a.reshape(-1).astype(F32) for a in arrs])
    n = flat.shape[0]
    rows = -(-n // PACK_WIDTH)
    rows = -(-rows // 8) * 8
    return jnp.pad(flat, (0, rows * PACK_WIDTH - n)).reshape(rows, PACK_WIDTH)


def _unpack(buf, shapes):
    flat = buf.reshape(-1)
    out, off = [], 0
    for shp in shapes:
        n = 1
        for s in shp:
            n *= s
        out.append(flat[off:off + n].reshape(shp))
        off += n
    return out


def _unshard_cols(stacked):
    moved = jnp.moveaxis(stacked, 0, -2)
    return moved.reshape(moved.shape[:-2] + (moved.shape[-2] * moved.shape[-1],))


def _col_shard(full, s, width):
    return lax.dynamic_slice_in_dim(full, s * width, width, axis=full.ndim - 1)


def kernel(x, meta_tokens, mix_norm_g, ffn_norm_g, ffn_w1, ffn_w2, cp_w_in, cp_conv_w, cp_conv_b, cp_ln_g, cp_ln_b, cp_pool_w, cp_pool_scale, cp_w_out, gla_w_in, gla_gate_w2, gla_gate_b, gla_head_g, gla_w_out, final_norm_g, loss_target, m_meta_tokens, m_mix_norm_g, m_ffn_norm_g, m_ffn_w1, m_ffn_w2, m_cp_w_in, m_cp_conv_w, m_cp_conv_b, m_cp_ln_g, m_cp_ln_b, m_cp_pool_w, m_cp_pool_scale, m_cp_w_out, m_gla_w_in, m_gla_gate_w2, m_gla_gate_b, m_gla_head_g, m_gla_w_out, m_final_norm_g, v_meta_tokens, v_mix_norm_g, v_ffn_norm_g, v_ffn_w1, v_ffn_w2, v_cp_w_in, v_cp_conv_w, v_cp_conv_b, v_cp_ln_g, v_cp_ln_b, v_cp_pool_w, v_cp_pool_scale, v_cp_w_out, v_gla_w_in, v_gla_gate_w2, v_gla_gate_b, v_gla_head_g, v_gla_w_out, v_final_norm_g):
    d = D_MODEL
    chip = 2 * lax.axis_index("x") + lax.axis_index("y")
    core = lax.axis_index("c")
    seq = x.shape[1]
    t = seq + CHUNK

    big_w = [ffn_w1, ffn_w2, cp_w_in, cp_w_out, gla_w_in, gla_w_out]
    sharded_small = [meta_tokens, cp_conv_w, gla_gate_w2, gla_gate_b, gla_head_g]
    gathered = _gather_weights([w.astype(BF16) for w in big_w], _pack(sharded_small))
    w1g, w2g, cpin_g, cpout_g, glain_g, glaout_g, small_g = gathered
    per_chip = [_unpack(small_g[j], [a.shape for a in sharded_small]) for j in range(N_CHIPS)]
    meta_f, conv_w_f, gate_w_f, gate_b_f, head_g_f = [
        jnp.concatenate([per_chip[j][i] for j in range(N_CHIPS)], axis=-1) for i in range(len(sharded_small))]
    conv_w_f, gate_w_f = conv_w_f[0], gate_w_f[0]
    w_cp_in = _unshard_cols(cpin_g[:, 0])
    w_cp_out = cpout_g.reshape(CONV_DIM + POOL_DIM, d)
    w_gla_in = jnp.pad(_unshard_cols(glain_g[:, 0]), ((0, 0), (0, GLA_IN_PAD - GLA_IN)))
    w_gla_out = glaout_g.reshape(GLA_DV, d)
    gate_w_pad = jnp.pad(gate_w_f, ((0, GATE_PAD - GATE_RANK), (0, 0))).astype(BF16)
    row = lambda a: a.reshape(1, -1)

    h0 = jnp.concatenate([jnp.zeros((PAD_ROWS, d), F32), meta_f, x[0]], axis=0)
    z0, u0 = _norm_matmul(h0, row(mix_norm_g[0]), w_cp_in, 512, "cp_in_proj")
    c0, pm0, mix0 = _cp_seq_fwd(z0, conv_w_f, cp_conv_b, cp_ln_g, cp_ln_b, cp_pool_w[0], cp_pool_scale)
    h1 = _matmul_residual(mix0, w_cp_out, h0, "cp_out_proj")
    h2, hp0, uf0 = _ffn_fwd(h1, row(ffn_norm_g[0]), w1g, w2g, 0, "ffn0_fwd")
    z1, u2 = _norm_matmul(h2, row(mix_norm_g[1]), w_gla_in, 640, "gla_in_proj")
    o1, mix1, states = _gla_seq_fwd(z1, gate_w_pad, gate_b_f, head_g_f)
    h3 = _matmul_residual(mix1, w_gla_out, h2, "gla_out_proj")
    h4, hp1, uf1 = _ffn_fwd(h3, row(ffn_norm_g[1]), w1g, w2g, 1, "ffn1_fwd")

    target = jnp.pad(loss_target[0], ((CHUNK, 0), (0, 0)))
    dh4, d_final_g, loss_part = _loss_bwd(h4, row(final_norm_g), target)

    dh3, dhp1, d_ffn_g1 = _ffn_bwd_data(dh4, h3, row(ffn_norm_g[1]), hp1, w1g, w2g, 1, "ffn1_bwd")
    dw1_1 = _wgrad(uf1, dhp1, N_CHIPS, d, d, False, True, False, "ffn1_dw1")
    dw2_1 = _wgrad(hp1, dh4, N_CHIPS, d, d, True, False, True, "ffn1_dw2")

    dmix1 = _dgrad(dh3, w_gla_out, "gla_out_dgrad")
    dw_gla_out = _wgrad(mix1, dh3, 1, GLA_DV, d, False, False, False, "gla_out_dw")
    dz1, d_gate_w, d_gate_b, d_head_g = _gla_seq_bwd(dmix1, o1, z1, states, gate_w_pad, gate_b_f, head_g_f)
    dh2, d_mix_g1 = _dgrad_norm_bwd(dz1, w_gla_in, h2, row(mix_norm_g[1]), dh3, 640, "gla_in_dgrad")
    dw_gla_in = _wgrad(u2, dz1, GLA_IN_PAD // 640, d, 640, False, True, False, "gla_in_dw")

    dh1, dhp0, d_ffn_g0 = _ffn_bwd_data(dh2, h1, row(ffn_norm_g[0]), hp0, w1g, w2g, 0, "ffn0_bwd")
    dw1_0 = _wgrad(uf0, dhp0, N_CHIPS, d, d, False, True, False, "ffn0_dw1")
    dw2_0 = _wgrad(hp0, dh2, N_CHIPS, d, d, True, False, True, "ffn0_dw2")

    dmix0 = _dgrad(dh1, w_cp_out, "cp_out_dgrad")
    dw_cp_out = _wgrad(mix0, dh1, 1, CONV_DIM + POOL_DIM, d, False, False, False, "cp_out_dw")
    dz0, d_conv_w, d_cp_vec, d_pool_w = _cp_seq_bwd(dmix0, z0, c0, pm0, conv_w_f, cp_ln_g, cp_ln_b, cp_pool_w[0], cp_pool_scale)
    dh0, d_mix_g0 = _dgrad_norm_bwd(dz0, w_cp_in, h0, row(mix_norm_g[0]), dh1, 512, "cp_in_dgrad")
    dw_cp_in = _wgrad(u0, dz0, N_CHIPS, d, CP_IN // N_CHIPS, False, True, False, "cp_in_dw")

    grad_x = dh0[CHUNK:][None]

    gla_in_cols = jnp.moveaxis(dw_gla_in, 0, 1).reshape(d, GLA_IN_PAD)[:, :GLA_IN]
    grads = [dw1_0, dw1_1, dw2_0, dw2_1, dw_cp_in,
             dw_cp_out.reshape(N_CHIPS, -1, d),
             jnp.moveaxis(gla_in_cols.reshape(d, N_CHIPS, GLA_IN // N_CHIPS), 1, 0),
             dw_gla_out.reshape(N_CHIPS, -1, d)]
    dests = [(0, 0), (0, 1), (1, 0), (1, 1), (2, 0), (3, 0), (4, 0), (5, 0)]
    small_full = [dh0[PAD_ROWS:CHUNK], jnp.concatenate([d_mix_g0, d_mix_g1], axis=0),
                  jnp.concatenate([d_ffn_g0, d_ffn_g1], axis=0), d_conv_w[:CONV_WIDTH][None],
                  d_cp_vec[0:1], d_cp_vec[1:2], d_cp_vec[2:3], d_pool_w[None], d_cp_vec[3:4],
                  d_gate_w[:GATE_RANK][None], d_gate_b, d_head_g, d_final_g[0], loss_part[0, 0:1]]
    small_mine = _pack(small_full)
    c_idx = core.reshape(1).astype(jnp.int32)
    recv = _exchange_halves(grads, small_mine)
    chip_sums = [_sum_halves(g, r, c_idx, "chip_sum_%d" % k) for k, (g, r) in enumerate(zip(grads, recv[:-1]))]
    small_chip = _add2(small_mine, recv[-1], "chip_sum_small")
    slots = _scatter_to_chips(chip_sums, small_chip)
    reduced = [_sum_slots(s, "slot_sum_%d" % k) for k, s in enumerate(slots[:-1])]
    small_red = _sum_slots(slots[-1], "slot_sum_small")
    big_grads = _join_halves(reduced, dests, [w.shape for w in big_w])

    big_m = [m_ffn_w1, m_ffn_w2, m_cp_w_in, m_cp_w_out, m_gla_w_in, m_gla_w_out]
    big_v = [v_ffn_w1, v_ffn_w2, v_cp_w_in, v_cp_w_out, v_gla_w_in, v_gla_w_out]
    big_out = []
    for k, (w, g, m, v) in enumerate(zip(big_w, big_grads, big_m, big_v)):
        two_d = lambda a: a.reshape(-1, a.shape[-1])
        outs = _adamw(two_d(w), two_d(g), two_d(m), two_d(v), "adamw_%d" % k)
        big_out.append([g] + [o.reshape(w.shape) for o in outs])

    (g_meta, g_mix, g_ffn, g_conv_w, g_conv_b, g_ln_g, g_ln_b, g_pool_w, g_pool_scale, g_gate_w, g_gate_b, g_head,
     g_final, loss_sum) = _unpack(small_red, [a.shape for a in small_full])
    g_meta = _col_shard(g_meta, chip, meta_tokens.shape[-1])
    g_conv_w = _col_shard(g_conv_w, chip, cp_conv_w.shape[-1])
    g_gate_w = _col_shard(g_gate_w, chip, gla_gate_w2.shape[-1])
    g_gate_b = _col_shard(g_gate_b, chip, gla_gate_b.shape[-1])
    g_head = _col_shard(g_head, chip, gla_head_g.shape[-1])
    small_w = [meta_tokens, mix_norm_g, ffn_norm_g, cp_conv_w, cp_conv_b, cp_ln_g, cp_ln_b, cp_pool_w, cp_pool_scale,
               gla_gate_w2, gla_gate_b, gla_head_g, final_norm_g]
    small_m = [m_meta_tokens, m_mix_norm_g, m_ffn_norm_g, m_cp_conv_w, m_cp_conv_b, m_cp_ln_g, m_cp_ln_b, m_cp_pool_w,
               m_cp_pool_scale, m_gla_gate_w2, m_gla_gate_b, m_gla_head_g, m_final_norm_g]
    small_v = [v_meta_tokens, v_mix_norm_g, v_ffn_norm_g, v_cp_conv_w, v_cp_conv_b, v_cp_ln_g, v_cp_ln_b, v_cp_pool_w,
               v_cp_pool_scale, v_gla_gate_w2, v_gla_gate_b, v_gla_head_g, v_final_norm_g]
    small_g = [g_meta, g_mix, g_ffn, g_conv_w, g_conv_b, g_ln_g, g_ln_b, g_pool_w, g_pool_scale, g_gate_w, g_gate_b,
               g_head, g_final]
    shapes = [w.shape for w in small_w]
    small_g = [g.reshape(s) for g, s in zip(small_g, shapes)]
    s_delta, s_m, s_v = _adamw(_pack(small_w), _pack(small_g), _pack(small_m), _pack(small_v), "adamw_small")
    s_delta, s_m, s_v = _unpack(s_delta, shapes), _unpack(s_m, shapes), _unpack(s_v, shapes)

    order = ["meta", "mix", "ffn", "w1", "w2", "cp_in", "conv_w", "conv_b", "ln_g", "ln_b", "pool_w", "pool_scale",
             "cp_out", "gla_in", "gate_w", "gate_b", "head", "gla_out", "final"]
    small_names = ["meta", "mix", "ffn", "conv_w", "conv_b", "ln_g", "ln_b", "pool_w", "pool_scale", "gate_w", "gate_b",
                   "head", "final"]
    big_names = ["w1", "w2", "cp_in", "cp_out", "gla_in", "gla_out"]
    table = {n: (small_g[i], s_delta[i], s_m[i], s_v[i]) for i, n in enumerate(small_names)}
    table.update({n: tuple(big_out[i]) for i, n in enumerate(big_names)})
    loss = loss_sum.reshape(())
    return (loss, grad_x, *[table[n][0] for n in order], *[table[n][1] for n in order],
            *[table[n][2] for n in order], *[table[n][3] for n in order])
```

```python
import functools

import jax
import jax.numpy as jnp
from jax import lax
from jax.experimental import pallas as pl
from jax.experimental.pallas import tpu as pltpu

F32 = jnp.float32
BF16 = jnp.bfloat16

D_MODEL = 1024
N_META = 16
CHUNK = 64
PAD_ROWS = CHUNK - N_META
EPS = 1e-5
CONV_DIM = 512
CONV_WIDTH = 31
CONV_HALO = 32
POOL_DIM = 512
POOL_WINDOWS = (2, 4, 8, 16)
POOL_GROUP = 128
POOL_HALO = 16
CP_IN = 2 * CONV_DIM + POOL_DIM
GLA_HEADS = 4
GLA_DK = 512
GLA_DV = 1024
GLA_HK = GLA_DK // GLA_HEADS
GLA_HV = GLA_DV // GLA_HEADS
GATE_RANK = 16
GATE_PAD = 128
GATE_NORM = 16.0
GLA_IN = 2 * GLA_DK + 2 * GLA_DV + GATE_RANK
GLA_IN_PAD = 2 * GLA_DK + 2 * GLA_DV + GATE_PAD
N_CHIPS = 4
ADAM_LR = 0.001
ADAM_B1 = 0.9
ADAM_B2 = 0.999
ADAM_EPS = 1e-08
ADAM_WD = 0.01
ADAM_STEP = 10

VMEM_LIMIT_BYTES = 56 * 1024 * 1024
ROW_TILE_TARGET = 832
PACK_WIDTH = 1024
MESH = pl.DeviceIdType.MESH
HBM_SPEC = pl.BlockSpec(memory_space=pltpu.HBM)


def _cparams(*sem):
    return pltpu.CompilerParams(dimension_semantics=sem, vmem_limit_bytes=VMEM_LIMIT_BYTES)


def _row_tile(t, target, mult):
    best = mult
    for cand in range(mult, min(t, target) + 1, mult):
        if t % cand == 0:
            best = cand
    assert t % best == 0, (t, best)
    return best


def _rms(h, g):
    return h * lax.rsqrt(jnp.mean(h * h, axis=-1, keepdims=True) + EPS) * g


def _rms_bwd(h, g, du):
    r = lax.rsqrt(jnp.mean(h * h, axis=-1, keepdims=True) + EPS)
    xhat = h * r
    dxh = du * g
    dh = r * (dxh - xhat * jnp.mean(dxh * xhat, axis=-1, keepdims=True))
    return dh, du * xhat


def _valid_rows(i, tm):
    row = i * tm + lax.broadcasted_iota(jnp.int32, (tm, 1), 0)
    return row >= PAD_ROWS


def _dot(a, b):
    return jnp.dot(a, b, preferred_element_type=F32)


def _dot_nt(a, b):
    return lax.dot_general(a, b, (((1,), (1,)), ((), ())), preferred_element_type=F32)


def _dot_tn(a, b):
    return lax.dot_general(a, b, (((0,), (0,)), ((), ())), preferred_element_type=F32)


def _accumulate(ref, val, first):
    @pl.when(first)
    def _():
        ref[...] = val

    @pl.when(jnp.logical_not(first))
    def _():
        ref[...] += val


def _norm_matmul(h, g, w, nc, name):
    t, d = h.shape
    n = w.shape[1]
    tm = _row_tile(t, ROW_TILE_TARGET, CHUNK)

    def body(h_ref, g_ref, w_ref, z_ref, u_ref):
        u = _rms(h_ref[...], g_ref[...]).astype(BF16)
        u_ref[...] = u
        for n0 in range(0, n, nc):
            z_ref[:, n0:n0 + nc] = _dot(u, w_ref[:, n0:n0 + nc]).astype(BF16)

    return pl.pallas_call(
        body, grid=(t // tm,),
        in_specs=[pl.BlockSpec((tm, d), lambda i: (i, 0)), pl.BlockSpec((1, d), lambda i: (0, 0)),
                  pl.BlockSpec((d, n), lambda i: (0, 0))],
        out_specs=[pl.BlockSpec((tm, n), lambda i: (i, 0)), pl.BlockSpec((tm, d), lambda i: (i, 0))],
        out_shape=[jax.ShapeDtypeStruct((t, n), BF16), jax.ShapeDtypeStruct((t, d), BF16)],
        compiler_params=_cparams("parallel"), name=name)(h, g, w)


def _matmul_residual(a, w, h, name):
    t, k = a.shape
    d = w.shape[1]
    tm = _row_tile(t, ROW_TILE_TARGET, CHUNK)

    def body(a_ref, w_ref, h_ref, o_ref):
        o_ref[...] = h_ref[...] + _dot(a_ref[...], w_ref[...])

    return pl.pallas_call(
        body, grid=(t // tm,),
        in_specs=[pl.BlockSpec((tm, k), lambda i: (i, 0)), pl.BlockSpec((k, d), lambda i: (0, 0)),
                  pl.BlockSpec((tm, d), lambda i: (i, 0))],
        out_specs=pl.BlockSpec((tm, d), lambda i: (i, 0)),
        out_shape=jax.ShapeDtypeStruct((t, d), F32),
        compiler_params=_cparams("parallel"), name=name)(a, w, h)


def _ffn_fwd(h, g, w1g, w2g, layer, name):
    t, d = h.shape
    ns, ffs = w1g.shape[0], w1g.shape[3]
    tm = _row_tile(t, ROW_TILE_TARGET, CHUNK)

    def body(h_ref, g_ref, w1_ref, w2_ref, ho_ref, hp_ref, u_ref, acc_ref):
        s = pl.program_id(1)

        @pl.when(s == 0)
        def _():
            u_ref[...] = _rms(h_ref[...], g_ref[...]).astype(BF16)

        hp = _dot(u_ref[...], w1_ref[...])
        hp_ref[...] = hp.astype(BF16)
        a = jnp.maximum(hp, 0.0)
        _accumulate(acc_ref, _dot((a * a).astype(BF16), w2_ref[...]), s == 0)

        @pl.when(s == ns - 1)
        def _():
            ho_ref[...] = h_ref[...] + acc_ref[...]

    return pl.pallas_call(
        body, grid=(t // tm, ns),
        in_specs=[pl.BlockSpec((tm, d), lambda i, s: (i, 0)), pl.BlockSpec((1, d), lambda i, s: (0, 0)),
                  pl.BlockSpec((None, None, d, ffs), lambda i, s: (s, layer, 0, 0)),
                  pl.BlockSpec((None, None, ffs, d), lambda i, s: (s, layer, 0, 0))],
        out_specs=[pl.BlockSpec((tm, d), lambda i, s: (i, 0)), pl.BlockSpec((tm, ffs), lambda i, s: (i, s)),
                   pl.BlockSpec((tm, d), lambda i, s: (i, 0))],
        out_shape=[jax.ShapeDtypeStruct((t, d), F32), jax.ShapeDtypeStruct((t, ns * ffs), BF16),
                   jax.ShapeDtypeStruct((t, d), BF16)],
        scratch_shapes=[pltpu.VMEM((tm, d), F32)],
        compiler_params=_cparams("parallel", "arbitrary"), name=name)(h, g, w1g, w2g)


def _ffn_bwd_data(dh, h, g, hp, w1g, w2g, layer, name):
    t, d = h.shape
    ns, ffs = w1g.shape[0], w1g.shape[3]
    tm = _row_tile(t, ROW_TILE_TARGET, CHUNK)

    def body(dh_ref, h_ref, g_ref, hp_ref, w1_ref, w2_ref, dhi_ref, dhp_ref, dg_ref, acc_ref):
        i, s = pl.program_id(0), pl.program_id(1)
        da = _dot_nt(dh_ref[...].astype(BF16), w2_ref[...])
        dhp = (da * (2.0 * jnp.maximum(hp_ref[...].astype(F32), 0.0))).astype(BF16)
        dhp_ref[...] = dhp
        _accumulate(acc_ref, _dot_nt(dhp, w1_ref[...]), s == 0)

        @pl.when(s == ns - 1)
        def _():
            dhn, dgr = _rms_bwd(h_ref[...], g_ref[...], acc_ref[...])
            dhi_ref[...] = jnp.where(_valid_rows(i, tm), dh_ref[...] + dhn, 0.0)
            _accumulate(dg_ref, jnp.sum(dgr, axis=0, keepdims=True), i == 0)

    return pl.pallas_call(
        body, grid=(t // tm, ns),
        in_specs=[pl.BlockSpec((tm, d), lambda i, s: (i, 0)), pl.BlockSpec((tm, d), lambda i, s: (i, 0)),
                  pl.BlockSpec((1, d), lambda i, s: (0, 0)), pl.BlockSpec((tm, ffs), lambda i, s: (i, s)),
                  pl.BlockSpec((None, None, d, ffs), lambda i, s: (s, layer, 0, 0)),
                  pl.BlockSpec((None, None, ffs, d), lambda i, s: (s, layer, 0, 0))],
        out_specs=[pl.BlockSpec((tm, d), lambda i, s: (i, 0)), pl.BlockSpec((tm, ffs), lambda i, s: (i, s)),
                   pl.BlockSpec((1, d), lambda i, s: (0, 0))],
        out_shape=[jax.ShapeDtypeStruct((t, d), F32), jax.ShapeDtypeStruct((t, ns * ffs), BF16),
                   jax.ShapeDtypeStruct((1, d), F32)],
        scratch_shapes=[pltpu.VMEM((tm, d), F32)],
        compiler_params=_cparams("arbitrary", "arbitrary"), name=name)(dh, h, g, hp, w1g, w2g)


def _wgrad(x, dy, nb, xc, yc, x_by_block, dy_by_block, relu2, name):
    t = x.shape[0]
    tm = _row_tile(t, ROW_TILE_TARGET, CHUNK)

    def body(x_ref, dy_ref, o_ref):
        k = pl.program_id(1)
        xv = x_ref[...]
        if relu2:
            xv = jnp.maximum(xv.astype(F32), 0.0)
            xv = xv * xv
        _accumulate(o_ref, _dot_tn(xv.astype(BF16), dy_ref[...].astype(BF16)), k == 0)

    return pl.pallas_call(
        body, grid=(nb, t // tm),
        in_specs=[pl.BlockSpec((tm, xc), (lambda b, k: (k, b)) if x_by_block else (lambda b, k: (k, 0))),
                  pl.BlockSpec((tm, yc), (lambda b, k: (k, b)) if dy_by_block else (lambda b, k: (k, 0)))],
        out_specs=pl.BlockSpec((None, xc, yc), lambda b, k: (b, 0, 0)),
        out_shape=jax.ShapeDtypeStruct((nb, xc, yc), F32),
        compiler_params=_cparams("parallel", "arbitrary"), name=name)(x, dy)


def _dgrad(dh, w, name):
    t, d = dh.shape
    k = w.shape[0]
    tm = _row_tile(t, ROW_TILE_TARGET, CHUNK)

    def body(dh_ref, w_ref, o_ref):
        o_ref[...] = _dot_nt(dh_ref[...].astype(BF16), w_ref[...]).astype(BF16)

    return pl.pallas_call(
        body, grid=(t // tm,),
        in_specs=[pl.BlockSpec((tm, d), lambda i: (i, 0)), pl.BlockSpec((k, d), lambda i: (0, 0))],
        out_specs=pl.BlockSpec((tm, k), lambda i: (i, 0)),
        out_shape=jax.ShapeDtypeStruct((t, k), BF16),
        compiler_params=_cparams("parallel"), name=name)(dh, w)


def _dgrad_norm_bwd(dz, w, h, g, dh, nc, name):
    t, d = h.shape
    n = w.shape[1]
    tm = _row_tile(t, ROW_TILE_TARGET // 2, 16)

    def body(dz_ref, w_ref, h_ref, g_ref, dh_ref, dhi_ref, dg_ref):
        i = pl.program_id(0)
        du = jnp.zeros((tm, d), F32)
        for n0 in range(0, n, nc):
            du = du + _dot_nt(dz_ref[:, n0:n0 + nc], w_ref[:, n0:n0 + nc])
        dhn, dgr = _rms_bwd(h_ref[...], g_ref[...], du)
        dhi_ref[...] = jnp.where(_valid_rows(i, tm), dh_ref[...] + dhn, 0.0)
        _accumulate(dg_ref, jnp.sum(dgr, axis=0, keepdims=True), i == 0)

    return pl.pallas_call(
        body, grid=(t // tm,),
        in_specs=[pl.BlockSpec((tm, n), lambda i: (i, 0)), pl.BlockSpec((d, n), lambda i: (0, 0)),
                  pl.BlockSpec((tm, d), lambda i: (i, 0)), pl.BlockSpec((1, d), lambda i: (0, 0)),
                  pl.BlockSpec((tm, d), lambda i: (i, 0))],
        out_specs=[pl.BlockSpec((tm, d), lambda i: (i, 0)), pl.BlockSpec((1, d), lambda i: (0, 0))],
        out_shape=[jax.ShapeDtypeStruct((t, d), F32), jax.ShapeDtypeStruct((1, d), F32)],
        compiler_params=_cparams("arbitrary"), name=name)(dz, w, h, g, dh)


def _loss_bwd(h, g, target):
    t, d = h.shape
    tm = _row_tile(t, ROW_TILE_TARGET, CHUNK)

    def body(h_ref, g_ref, t_ref, dh_ref, dg_ref, loss_ref):
        i = pl.program_id(0)
        row = i * tm + lax.broadcasted_iota(jnp.int32, (tm, 1), 0)
        keep = row >= CHUNK
        hv, gv = h_ref[...], g_ref[...]
        err = jnp.where(keep, _rms(hv, gv) - t_ref[...], 0.0)
        part = 0.5 * jnp.sum(jnp.mean(err * err, axis=-1, keepdims=True), axis=0, keepdims=True)
        dhn, dgr = _rms_bwd(hv, gv, err * (1.0 / d))
        dh_ref[...] = dhn
        _accumulate(dg_ref, jnp.sum(dgr, axis=0, keepdims=True), i == 0)
        _accumulate(loss_ref, jnp.broadcast_to(part, (8, 128)), i == 0)

    return pl.pallas_call(
        body, grid=(t // tm,),
        in_specs=[pl.BlockSpec((tm, d), lambda i: (i, 0)), pl.BlockSpec((1, d), lambda i: (0, 0)),
                  pl.BlockSpec((tm, d), lambda i: (i, 0))],
        out_specs=[pl.BlockSpec((tm, d), lambda i: (i, 0)), pl.BlockSpec((1, d), lambda i: (0, 0)),
                   pl.BlockSpec((8, 128), lambda i: (0, 0))],
        out_shape=[jax.ShapeDtypeStruct((t, d), F32), jax.ShapeDtypeStruct((1, d), F32),
                   jax.ShapeDtypeStruct((8, 128), F32)],
        compiler_params=_cparams("arbitrary"), name="loss_bwd")(h, g, target)


CONV_BLOCK = 32


def _silu(x):
    return x * jax.nn.sigmoid(x)


def _cp_seq_fwd(z, conv_w, conv_b, ln_g, ln_b, pool_w, pool_scale):
    t = z.shape[0]
    tm = _row_tile(t, ROW_TILE_TARGET, CHUNK)

    def body(z_ref, cw_ref, cb_ref, lg_ref, lb_ref, pw_ref, ps_ref, c_ref, pm_ref, mix_ref, gbuf, pbuf):
        i = pl.program_id(0)

        @pl.when(i == 0)
        def _():
            gbuf[0:CONV_HALO, :] = jnp.zeros((CONV_HALO, CONV_DIM), F32)
            pbuf[0:POOL_HALO, :] = jnp.zeros((POOL_HALO, POOL_DIM), F32)

        @pl.when(i > 0)
        def _():
            gbuf[0:CONV_HALO, :] = gbuf[tm:tm + CONV_HALO, :]
            pbuf[0:POOL_HALO, :] = pbuf[tm:tm + POOL_HALO, :]

        av = z_ref[:, 0:CONV_DIM].astype(F32)
        ag = z_ref[:, CONV_DIM:2 * CONV_DIM].astype(F32)
        gbuf[CONV_HALO:CONV_HALO + tm, :] = av * jax.nn.sigmoid(ag)
        pbuf[POOL_HALO:POOL_HALO + tm, :] = z_ref[:, 2 * CONV_DIM:CP_IN].astype(F32)

        def conv_block(rb, carry):
            base = pl.multiple_of(rb * CONV_BLOCK, CONV_BLOCK)
            win = gbuf[pl.ds(base, CONV_BLOCK + CONV_HALO), :]
            acc = jnp.zeros((CONV_BLOCK, CONV_DIM), F32)
            for k in range(CONV_WIDTH):
                off = CONV_HALO - (CONV_WIDTH - 1) + k
                acc = acc + cw_ref[k:k + 1, :] * win[off:off + CONV_BLOCK, :]
            c_ref[pl.ds(base, CONV_BLOCK), :] = acc + cb_ref[...]
            return carry

        lax.fori_loop(0, tm // CONV_BLOCK, conv_block, 0)

        c = c_ref[...]
        mu = jnp.mean(c, axis=-1, keepdims=True)
        xc = c - mu
        ln = xc * lax.rsqrt(jnp.mean(xc * xc, axis=-1, keepdims=True) + EPS) * lg_ref[...] + lb_ref[...]
        row = i * tm + lax.broadcasted_iota(jnp.int32, (tm, 1), 0)
        mix_ref[:, 0:CONV_DIM] = jnp.where(row >= PAD_ROWS, _silu(ln), 0.0).astype(BF16)

        tpos = (row - PAD_ROWS + 1).astype(F32)
        for gi, wdw in enumerate(POOL_WINDOWS):
            lo = POOL_GROUP * gi
            cur = pbuf[POOL_HALO:POOL_HALO + tm, lo:lo + POOL_GROUP]
            sacc = cur
            for j in range(1, wdw):
                sacc = sacc + pbuf[POOL_HALO - j:POOL_HALO - j + tm, lo:lo + POOL_GROUP]
            pm = (sacc / jnp.clip(tpos, 1.0, float(wdw)) - cur).astype(BF16)
            pm_ref[:, lo:lo + POOL_GROUP] = pm
            pg = _dot(pm, pw_ref[gi].astype(BF16))
            mix_ref[:, CONV_DIM + lo:CONV_DIM + lo + POOL_GROUP] = (pg * ps_ref[:, lo:lo + POOL_GROUP]).astype(BF16)

    vec = pl.BlockSpec((1, CONV_DIM), lambda i: (0, 0))
    return pl.pallas_call(
        body, grid=(t // tm,),
        in_specs=[pl.BlockSpec((tm, CP_IN), lambda i: (i, 0)),
                  pl.BlockSpec((CONV_WIDTH, CONV_DIM), lambda i: (0, 0)), vec, vec, vec,
                  pl.BlockSpec((len(POOL_WINDOWS), POOL_GROUP, POOL_GROUP), lambda i: (0, 0, 0)), vec],
        out_specs=[pl.BlockSpec((tm, CONV_DIM), lambda i: (i, 0)), pl.BlockSpec((tm, POOL_DIM), lambda i: (i, 0)),
                   pl.BlockSpec((tm, CONV_DIM + POOL_DIM), lambda i: (i, 0))],
        out_shape=[jax.ShapeDtypeStruct((t, CONV_DIM), F32), jax.ShapeDtypeStruct((t, POOL_DIM), BF16),
                   jax.ShapeDtypeStruct((t, CONV_DIM + POOL_DIM), BF16)],
        scratch_shapes=[pltpu.VMEM((tm + CONV_HALO, CONV_DIM), F32), pltpu.VMEM((tm + POOL_HALO, POOL_DIM), F32)],
        compiler_params=_cparams("arbitrary"), name="cp_seq_fwd")(z, conv_w, conv_b, ln_g, ln_b, pool_w, pool_scale)


def _cp_seq_bwd(dmix, z, c, pm, conv_w, ln_g, ln_b, pool_w, pool_scale):
    t = z.shape[0]
    tm = _row_tile(t, ROW_TILE_TARGET, CHUNK)
    nt = t // tm

    def body(dmix_ref, z_ref, c_ref, pm_ref, cw_ref, lg_ref, lb_ref, pw_ref, ps_ref,
             dz_ref, dcw_ref, dvec_ref, dpw_ref, dcbuf, qbuf, glu_buf, dwacc):
        i = pl.program_id(0)
        tile = nt - 1 - i

        @pl.when(i == 0)
        def _():
            dcbuf[tm:tm + CONV_HALO, :] = jnp.zeros((CONV_HALO, CONV_DIM), F32)
            qbuf[tm:tm + POOL_HALO, :] = jnp.zeros((POOL_HALO, POOL_DIM), F32)
            dcw_ref[...] = jnp.zeros_like(dcw_ref)
            dwacc[...] = jnp.zeros_like(dwacc)
            dvec_ref[...] = jnp.zeros_like(dvec_ref)
            dpw_ref[...] = jnp.zeros_like(dpw_ref)

        @pl.when(i > 0)
        def _():
            dcbuf[tm:tm + CONV_HALO, :] = dcbuf[0:CONV_HALO, :]
            qbuf[tm:tm + POOL_HALO, :] = qbuf[0:POOL_HALO, :]

        row = tile * tm + lax.broadcasted_iota(jnp.int32, (tm, 1), 0)
        cv = c_ref[...]
        mu = jnp.mean(cv, axis=-1, keepdims=True)
        xc = cv - mu
        rstd = lax.rsqrt(jnp.mean(xc * xc, axis=-1, keepdims=True) + EPS)
        xhat = xc * rstd
        ln = xhat * lg_ref[...] + lb_ref[...]
        sg = jax.nn.sigmoid(ln)
        da = jnp.where(row >= PAD_ROWS, dmix_ref[:, 0:CONV_DIM].astype(F32), 0.0)
        dln = da * (sg * (1.0 + ln * (1.0 - sg)))
        dxh = dln * lg_ref[...]
        dc = rstd * (dxh - jnp.mean(dxh, axis=-1, keepdims=True) - xhat * jnp.mean(dxh * xhat, axis=-1, keepdims=True))
        dcbuf[0:tm, :] = dc
        dvec_ref[0:1, :] += jnp.sum(dc, axis=0, keepdims=True)
        dvec_ref[1:2, :] += jnp.sum(dln * xhat, axis=0, keepdims=True)
        dvec_ref[2:3, :] += jnp.sum(dln, axis=0, keepdims=True)

        av = z_ref[:, 0:CONV_DIM].astype(F32)
        sig_g = jax.nn.sigmoid(z_ref[:, CONV_DIM:2 * CONV_DIM].astype(F32))
        glu_buf[...] = av * sig_g

        def conv_block(rb, carry):
            base = pl.multiple_of(rb * CONV_BLOCK, CONV_BLOCK)
            win = dcbuf[pl.ds(base, CONV_BLOCK + CONV_HALO), :]
            glu = glu_buf[pl.ds(base, CONV_BLOCK), :]
            acc = jnp.zeros((CONV_BLOCK, CONV_DIM), F32)
            for k in range(CONV_WIDTH):
                off = CONV_WIDTH - 1 - k
                slab = win[off:off + CONV_BLOCK, :]
                acc = acc + cw_ref[k:k + 1, :] * slab
                prod = slab * glu
                part = prod[0:8]
                for q in range(1, CONV_BLOCK // 8):
                    part = part + prod[8 * q:8 * q + 8]
                dwacc[k] += part
            glu_buf[pl.ds(base, CONV_BLOCK), :] = acc
            return carry

        lax.fori_loop(0, tm // CONV_BLOCK, conv_block, 0)

        @pl.when(i == nt - 1)
        def _():
            for k in range(CONV_WIDTH):
                dcw_ref[k:k + 1, :] = jnp.sum(dwacc[k], axis=0, keepdims=True)
        dglu = glu_buf[...]
        dz_ref[:, 0:CONV_DIM] = (dglu * sig_g).astype(BF16)
        dz_ref[:, CONV_DIM:2 * CONV_DIM] = (dglu * av * sig_g * (1.0 - sig_g)).astype(BF16)

        tpos = (row - PAD_ROWS + 1).astype(F32)
        for gi, wdw in enumerate(POOL_WINDOWS):
            lo = POOL_GROUP * gi
            dp = dmix_ref[:, CONV_DIM + lo:CONV_DIM + lo + POOL_GROUP].astype(F32)
            pmv = pm_ref[:, lo:lo + POOL_GROUP]
            pwb = pw_ref[gi].astype(BF16)
            dvec_ref[3:4, lo:lo + POOL_GROUP] += jnp.sum(dp * _dot(pmv, pwb), axis=0, keepdims=True)
            dq = (dp * ps_ref[:, lo:lo + POOL_GROUP]).astype(BF16)
            dpw_ref[gi] += _dot_tn(pmv, dq)
            dpm = _dot_nt(dq, pwb)
            qbuf[0:tm, lo:lo + POOL_GROUP] = dpm / jnp.clip(tpos, 1.0, float(wdw))
            sacc = -dpm
            for j in range(wdw):
                sacc = sacc + qbuf[j:j + tm, lo:lo + POOL_GROUP]
            dz_ref[:, 2 * CONV_DIM + lo:2 * CONV_DIM + lo + POOL_GROUP] = sacc.astype(BF16)

    vec = pl.BlockSpec((1, CONV_DIM), lambda i: (0, 0))
    rev = lambda i: (nt - 1 - i, 0)
    return pl.pallas_call(
        body, grid=(nt,),
        in_specs=[pl.BlockSpec((tm, CONV_DIM + POOL_DIM), rev), pl.BlockSpec((tm, CP_IN), rev),
                  pl.BlockSpec((tm, CONV_DIM), rev), pl.BlockSpec((tm, POOL_DIM), rev),
                  pl.BlockSpec((CONV_WIDTH, CONV_DIM), lambda i: (0, 0)), vec, vec,
                  pl.BlockSpec((len(POOL_WINDOWS), POOL_GROUP, POOL_GROUP), lambda i: (0, 0, 0)), vec],
        out_specs=[pl.BlockSpec((tm, CP_IN), rev), pl.BlockSpec((CONV_WIDTH + 1, CONV_DIM), lambda i: (0, 0)),
                   pl.BlockSpec((8, CONV_DIM), lambda i: (0, 0)),
                   pl.BlockSpec((len(POOL_WINDOWS), POOL_GROUP, POOL_GROUP), lambda i: (0, 0, 0))],
        out_shape=[jax.ShapeDtypeStruct((t, CP_IN), BF16), jax.ShapeDtypeStruct((CONV_WIDTH + 1, CONV_DIM), F32),
                   jax.ShapeDtypeStruct((8, CONV_DIM), F32),
                   jax.ShapeDtypeStruct((len(POOL_WINDOWS), POOL_GROUP, POOL_GROUP), F32)],
        scratch_shapes=[pltpu.VMEM((tm + CONV_HALO, CONV_DIM), F32), pltpu.VMEM((tm + POOL_HALO, POOL_DIM), F32),
                        pltpu.VMEM((tm, CONV_DIM), F32), pltpu.VMEM((CONV_WIDTH + 1, 8, CONV_DIM), F32)],
        compiler_params=_cparams("arbitrary"), name="cp_seq_bwd")(dmix, z, c, pm, conv_w, ln_g, ln_b, pool_w, pool_scale)


Q0, K0, V0, G0, R0 = 0, GLA_DK, 2 * GLA_DK, 2 * GLA_DK + GLA_DV, 2 * GLA_DK + 2 * GLA_DV


def _split3(x):
    hi = x.astype(BF16)
    r1 = x - hi.astype(F32)
    mid = r1.astype(BF16)
    lo = (r1 - mid.astype(F32)).astype(BF16)
    return hi, mid, lo


def _tri(strict):
    r = lax.broadcasted_iota(jnp.int32, (CHUNK, CHUNK), 0)
    c = lax.broadcasted_iota(jnp.int32, (CHUNK, CHUNK), 1)
    return ((r > c) if strict else (r >= c)).astype(BF16)


def _gate_decay(r, gw_ref, gb_ref, tri):
    pre = _dot(r, gw_ref[...]) + gb_ref[...]
    lac = (jnp.minimum(pre, 0.0) - jnp.log(1.0 + jnp.exp(-jnp.abs(pre)))) * (1.0 / GATE_NORM)
    hi, mid, lo = _split3(lac)
    cum = _dot(tri, hi) + _dot(tri, mid) + _dot(tri, lo)
    return pre, cum, cum[CHUNK - 1:CHUNK, :]


def _gla_seq_fwd(z, gate_w, gate_b, head_g):
    t = z.shape[0]
    tm = _row_tile(t, ROW_TILE_TARGET, CHUNK)
    cpt = tm // CHUNK
    scale = GLA_HK ** -0.5

    def body(z_ref, gw_ref, gb_ref, hg_ref, o_ref, mix_ref, st_ref, state):
        @pl.when(pl.program_id(0) == 0)
        def _():
            state[...] = jnp.zeros_like(state)

        tri = _tri(False)

        def chunk(ci, carry):
            r0 = pl.multiple_of(ci * CHUNK, CHUNK)
            rows = pl.ds(r0, CHUNK)
            _, cum, tot = _gate_decay(z_ref[rows, R0:R0 + GATE_PAD], gw_ref, gb_ref, tri)
            dec = jnp.exp(tot - cum)
            e = jnp.exp(tot)
            st_ref[ci] = state[...].astype(BF16)
            for hd in range(GLA_HEADS):
                ks = slice(hd * GLA_HK, (hd + 1) * GLA_HK)
                vs = slice(hd * GLA_HV, (hd + 1) * GLA_HV)
                kdec = (z_ref[rows, K0 + hd * GLA_HK:K0 + (hd + 1) * GLA_HK].astype(F32) * dec[:, ks]).astype(BF16)
                v = z_ref[rows, V0 + hd * GLA_HV:V0 + (hd + 1) * GLA_HV]
                st = state[vs, :] * e[:, ks] + _dot_tn(v, kdec)
                state[vs, :] = st
                q = z_ref[rows, Q0 + hd * GLA_HK:Q0 + (hd + 1) * GLA_HK]
                o = _dot_nt(q, st.astype(BF16)) * scale
                ob = o.astype(BF16)
                o_ref[rows, vs] = ob
                on = _rms(ob.astype(F32), hg_ref[...])
                gv = z_ref[rows, G0 + hd * GLA_HV:G0 + (hd + 1) * GLA_HV].astype(F32)
                mix_ref[rows, vs] = (on * _silu(gv)).astype(BF16)
            return carry

        lax.fori_loop(0, cpt, chunk, 0)

    return pl.pallas_call(
        body, grid=(t // tm,),
        in_specs=[pl.BlockSpec((tm, GLA_IN_PAD), lambda i: (i, 0)),
                  pl.BlockSpec((GATE_PAD, GLA_DK), lambda i: (0, 0)), pl.BlockSpec((1, GLA_DK), lambda i: (0, 0)),
                  pl.BlockSpec((1, GLA_HV), lambda i: (0, 0))],
        out_specs=[pl.BlockSpec((tm, GLA_DV), lambda i: (i, 0)), pl.BlockSpec((tm, GLA_DV), lambda i: (i, 0)),
                   pl.BlockSpec((cpt, GLA_DV, GLA_HK), lambda i: (i, 0, 0))],
        out_shape=[jax.ShapeDtypeStruct((t, GLA_DV), BF16), jax.ShapeDtypeStruct((t, GLA_DV), BF16),
                   jax.ShapeDtypeStruct((t // CHUNK, GLA_DV, GLA_HK), BF16)],
        scratch_shapes=[pltpu.VMEM((GLA_DV, GLA_HK), F32)],
        compiler_params=_cparams("arbitrary"), name="gla_seq_fwd")(z, gate_w, gate_b, head_g)


def _gla_seq_bwd(dmix, o, z, states, gate_w, gate_b, head_g):
    t = z.shape[0]
    tm = _row_tile(t, ROW_TILE_TARGET, CHUNK)
    cpt = tm // CHUNK
    nt = t // tm
    scale = GLA_HK ** -0.5

    def body(dmix_ref, o_ref, z_ref, st_ref, gw_ref, gb_ref, hg_ref, dz_ref, dgw_ref, dgb_ref, dhg_ref, dstate):
        @pl.when(pl.program_id(0) == 0)
        def _():
            dstate[...] = jnp.zeros_like(dstate)
            dgw_ref[...] = jnp.zeros_like(dgw_ref)
            dgb_ref[...] = jnp.zeros_like(dgb_ref)
            dhg_ref[...] = jnp.zeros_like(dhg_ref)

        tri = _tri(False)
        tri_strict = _tri(True)

        def chunk(cj, carry):
            ci = cpt - 1 - cj
            r0 = pl.multiple_of(ci * CHUNK, CHUNK)
            rows = pl.ds(r0, CHUNK)
            r = z_ref[rows, R0:R0 + GATE_PAD]
            pre, cum, tot = _gate_decay(r, gw_ref, gb_ref, tri)
            dec = jnp.exp(tot - cum)
            e = jnp.exp(tot)
            dlac_parts = []
            dhg = jnp.zeros((1, GLA_HV), F32)
            for hd in range(GLA_HEADS):
                ks = slice(hd * GLA_HK, (hd + 1) * GLA_HK)
                vs = slice(hd * GLA_HV, (hd + 1) * GLA_HV)
                kcols = slice(K0 + hd * GLA_HK, K0 + (hd + 1) * GLA_HK)
                vcols = slice(V0 + hd * GLA_HV, V0 + (hd + 1) * GLA_HV)
                qcols = slice(Q0 + hd * GLA_HK, Q0 + (hd + 1) * GLA_HK)
                gcols = slice(G0 + hd * GLA_HV, G0 + (hd + 1) * GLA_HV)
                ov = o_ref[rows, vs].astype(F32)
                gv = z_ref[rows, gcols].astype(F32)
                dm = dmix_ref[rows, vs].astype(F32)
                sg = jax.nn.sigmoid(gv)
                rr = lax.rsqrt(jnp.mean(ov * ov, axis=-1, keepdims=True) + EPS)
                xhat = ov * rr
                don = dm * (gv * sg)
                dz_ref[rows, gcols] = (dm * (xhat * hg_ref[...]) * (sg * (1.0 + gv * (1.0 - sg)))).astype(BF16)
                dhg = dhg + jnp.sum(don * xhat, axis=0, keepdims=True)
                dxh = don * hg_ref[...]
                do = (rr * (dxh - xhat * jnp.mean(dxh * xhat, axis=-1, keepdims=True)) * scale).astype(BF16)
                kdec = z_ref[rows, kcols].astype(F32) * dec[:, ks]
                kdb = kdec.astype(BF16)
                v = z_ref[rows, vcols]
                q = z_ref[rows, qcols]
                st_prev = st_ref[ci, vs, :].astype(F32)
                st = st_prev * e[:, ks] + _dot_tn(v, kdb)
                dz_ref[rows, qcols] = _dot(do, st.astype(BF16)).astype(BF16)
                dst = dstate[vs, :] + _dot_tn(do, q)
                dstb = dst.astype(BF16)
                dkdec = _dot(v, dstb)
                dz_ref[rows, vcols] = _dot_nt(kdb, dstb).astype(BF16)
                dtot = jnp.sum(dst * st_prev, axis=0, keepdims=True) * e[:, ks]
                dstate[vs, :] = dst * e[:, ks]
                dz_ref[rows, kcols] = (dkdec * dec[:, ks]).astype(BF16)
                gk = dkdec * kdec
                ghi, gmid, _ = _split3(gk)
                dlac_parts.append(dtot + _dot(tri_strict, ghi) + _dot(tri_strict, gmid))
            dlac = jnp.concatenate(dlac_parts, axis=1)
            dpre = dlac * (1.0 / GATE_NORM) * (1.0 - jax.nn.sigmoid(pre))
            dpb = dpre.astype(BF16)
            dz_ref[rows, R0:R0 + GATE_PAD] = _dot_nt(dpb, gw_ref[...]).astype(BF16)
            dgw_ref[...] += _dot_tn(r, dpb)
            dgb_ref[...] += jnp.sum(dpre, axis=0, keepdims=True)
            dhg_ref[...] += dhg
            return carry

        lax.fori_loop(0, cpt, chunk, 0)

    rev = lambda i: (nt - 1 - i, 0)
    return pl.pallas_call(
        body, grid=(nt,),
        in_specs=[pl.BlockSpec((tm, GLA_DV), rev), pl.BlockSpec((tm, GLA_DV), rev), pl.BlockSpec((tm, GLA_IN_PAD), rev),
                  pl.BlockSpec((cpt, GLA_DV, GLA_HK), lambda i: (nt - 1 - i, 0, 0)),
                  pl.BlockSpec((GATE_PAD, GLA_DK), lambda i: (0, 0)), pl.BlockSpec((1, GLA_DK), lambda i: (0, 0)),
                  pl.BlockSpec((1, GLA_HV), lambda i: (0, 0))],
        out_specs=[pl.BlockSpec((tm, GLA_IN_PAD), rev), pl.BlockSpec((GATE_PAD, GLA_DK), lambda i: (0, 0)),
                   pl.BlockSpec((1, GLA_DK), lambda i: (0, 0)), pl.BlockSpec((1, GLA_HV), lambda i: (0, 0))],
        out_shape=[jax.ShapeDtypeStruct((t, GLA_IN_PAD), BF16), jax.ShapeDtypeStruct((GATE_PAD, GLA_DK), F32),
                   jax.ShapeDtypeStruct((1, GLA_DK), F32), jax.ShapeDtypeStruct((1, GLA_HV), F32)],
        scratch_shapes=[pltpu.VMEM((GLA_DV, GLA_HK), F32)],
        compiler_params=_cparams("arbitrary"), name="gla_seq_bwd")(dmix, o, z, states, gate_w, gate_b, head_g)


def _sum_halves(g, recv, c_idx, name):
    n, r, cdim = g.shape
    h = r // 2
    tr = _row_tile(h, 256, 8)
    nh = h // tr

    def body(c_ref, g_ref, r_ref, o_ref):
        o_ref[...] = (g_ref[...] + r_ref[...]).astype(BF16)

    return pl.pallas_call(
        body,
        grid_spec=pltpu.PrefetchScalarGridSpec(
            num_scalar_prefetch=1, grid=(n, nh),
            in_specs=[pl.BlockSpec((None, tr, cdim), lambda s, i, c: (s, c[0] * nh + i, 0)),
                      pl.BlockSpec((None, tr, cdim), lambda s, i, c: (s, i, 0))],
            out_specs=pl.BlockSpec((None, tr, cdim), lambda s, i, c: (s, i, 0))),
        out_shape=jax.ShapeDtypeStruct((n, h, cdim), BF16),
        compiler_params=_cparams("parallel", "parallel"), name=name)(c_idx, g, recv)


def _sum_slots(x, name):
    n, r, cdim = x.shape
    tr = _row_tile(r, 256, 8)

    def body(x_ref, o_ref):
        acc = x_ref[0].astype(F32)
        for j in range(1, n):
            acc = acc + x_ref[j].astype(F32)
        o_ref[...] = acc

    return pl.pallas_call(
        body, grid=(r // tr,),
        in_specs=[pl.BlockSpec((n, tr, cdim), lambda i: (0, i, 0))],
        out_specs=pl.BlockSpec((tr, cdim), lambda i: (i, 0)),
        out_shape=jax.ShapeDtypeStruct((r, cdim), F32),
        compiler_params=_cparams("parallel"), name=name)(x)


def _sum_own_and_slots(own, slots, chip_idx, name):
    n, r, cdim = own.shape
    tr = _row_tile(r, 256, 8)

    def body(s_ref, own_ref, a_ref, b_ref, c_ref, o_ref):
        o_ref[...] = (own_ref[...].astype(F32) + a_ref[...].astype(F32) + b_ref[...].astype(F32)
                      + c_ref[...].astype(F32))

    def slot(dd):
        return pl.BlockSpec((None, tr, cdim), lambda i, s: ((s[0] + dd) % n, i, 0))

    return pl.pallas_call(
        body,
        grid_spec=pltpu.PrefetchScalarGridSpec(
            num_scalar_prefetch=1, grid=(r // tr,), in_specs=[slot(0), slot(1), slot(2), slot(3)],
            out_specs=pl.BlockSpec((tr, cdim), lambda i, s: (i, 0))),
        out_shape=jax.ShapeDtypeStruct((r, cdim), F32),
        compiler_params=_cparams("parallel"), name=name)(chip_idx, own, slots, slots, slots)


def _add2(a, b, name):
    r, cdim = a.shape
    tr = _row_tile(r, 256, 8)

    def body(a_ref, b_ref, o_ref):
        o_ref[...] = a_ref[...] + b_ref[...]

    spec = pl.BlockSpec((tr, cdim), lambda i: (i, 0))
    return pl.pallas_call(body, grid=(r // tr,), in_specs=[spec, spec], out_specs=spec,
                          out_shape=jax.ShapeDtypeStruct((r, cdim), F32),
                          compiler_params=_cparams("parallel"), name=name)(a, b)


def _adamw(w, g, m, v, name):
    r, cdim = w.shape
    tr = _row_tile(r, 256, 8)

    def body(w_ref, g_ref, m_ref, v_ref, d_ref, mo_ref, vo_ref):
        gv = g_ref[...]
        mn = ADAM_B1 * m_ref[...] + (1.0 - ADAM_B1) * gv
        vn = ADAM_B2 * v_ref[...] + (1.0 - ADAM_B2) * (gv * gv)
        m_hat = mn / (1.0 - ADAM_B1 ** ADAM_STEP)
        v_hat = vn / (1.0 - ADAM_B2 ** ADAM_STEP)
        d_ref[...] = -ADAM_LR * (m_hat / (jnp.sqrt(v_hat) + ADAM_EPS) + ADAM_WD * w_ref[...])
        mo_ref[...] = mn
        vo_ref[...] = vn

    spec = pl.BlockSpec((tr, cdim), lambda i: (i, 0))
    shp = jax.ShapeDtypeStruct((r, cdim), F32)
    return pl.pallas_call(body, grid=(r // tr,), in_specs=[spec] * 4, out_specs=[spec] * 3,
                          out_shape=[shp] * 3, compiler_params=_cparams("parallel"), name=name)(w, g, m, v)


def _split_rows(a):
    return a.reshape(a.shape[0], 2, a.shape[1] // 2, a.shape[2])


def _place():
    x, y, c = lax.axis_index("x"), lax.axis_index("y"), lax.axis_index("c")
    chips = [(1 - x, y), (x, 1 - y), (1 - x, 1 - y)]
    return x, y, c, chips


def _remote(src, dst, send_sem, recv_sem, to):
    return pltpu.make_async_remote_copy(src_ref=src, dst_ref=dst, send_sem=send_sem, recv_sem=recv_sem,
                                        device_id=to, device_id_type=MESH)


def _gather_weights(bigs, small):
    nb = len(bigs)
    shapes = [b.shape for b in bigs]
    bigs = [_split_rows(b) for b in bigs]

    def body(*refs):
        big_in, small_in = refs[:nb], refs[nb]
        big_out, small_out = refs[nb + 1:2 * nb + 1], refs[2 * nb + 1]
        ici_send, ici_recv, d2d_send, d2d_recv, own_send, own_recv, loc_sem = refs[2 * nb + 2:]
        x, y, c, chips = _place()
        me = 2 * x + y
        sib = (x, y, 1 - c)

        def half(ref, k, which):
            return ref.at[:, which]

        local = [pltpu.make_async_copy(small_in, small_out.at[me], loc_sem.at[0])]
        for cp in local:
            cp.start()
        sends = [_remote(big_in[k], big_out[k].at[me], own_send.at[k], own_recv.at[k], sib) for k in range(nb)]
        for j, (px, py) in enumerate(chips):
            for k in range(nb):
                sends.append(_remote(half(big_in[k], k, c), half(big_out[k].at[me], k, c),
                                     ici_send.at[k * 3 + j], ici_recv.at[k * 3 + j], (px, py, c)))
            sends.append(_remote(small_in, small_out.at[me], ici_send.at[nb * 3 + j], ici_recv.at[nb * 3 + j], (px, py, c)))
        for cp in sends:
            cp.start()
        passed = []
        for j, (px, py) in enumerate(chips):
            frm = 2 * px + py
            for k in range(nb):
                landed = half(big_out[k].at[frm], k, c)
                _remote(landed, landed, ici_send.at[k * 3 + j], ici_recv.at[k * 3 + j], (px, py, c)).wait_recv()
                fwd = _remote(landed, landed, d2d_send.at[k * 3 + j], d2d_recv.at[k * 3 + j], sib)
                fwd.start()
                passed.append(fwd)
            _remote(small_in, small_out.at[frm], ici_send.at[nb * 3 + j], ici_recv.at[nb * 3 + j], (px, py, c)).wait_recv()
        for j, (px, py) in enumerate(chips):
            frm = 2 * px + py
            for k in range(nb):
                theirs = half(big_out[k].at[frm], k, 1 - c)
                _remote(theirs, theirs, d2d_send.at[k * 3 + j], d2d_recv.at[k * 3 + j], sib).wait_recv()
        for k in range(nb):
            _remote(big_in[k], big_out[k].at[me], own_send.at[k], own_recv.at[k], sib).wait_recv()
        for cp in sends + passed:
            cp.wait_send()
        for cp in local:
            cp.wait()

    out_shape = [jax.ShapeDtypeStruct((N_CHIPS,) + b.shape, b.dtype) for b in bigs]
    out_shape.append(jax.ShapeDtypeStruct((N_CHIPS,) + small.shape, small.dtype))
    outs = pl.pallas_call(
        body, in_specs=[HBM_SPEC] * (nb + 1), out_specs=[HBM_SPEC] * (nb + 1), out_shape=out_shape,
        scratch_shapes=[pltpu.SemaphoreType.DMA((3 * nb + 3,)), pltpu.SemaphoreType.DMA((3 * nb + 3,)),
                        pltpu.SemaphoreType.DMA((3 * nb,)), pltpu.SemaphoreType.DMA((3 * nb,)),
                        pltpu.SemaphoreType.DMA((nb,)), pltpu.SemaphoreType.DMA((nb,)),
                        pltpu.SemaphoreType.DMA((1,))],
        name="gather_weights")(*bigs, small)
    return [o.reshape((N_CHIPS,) + s) for o, s in zip(outs[:-1], shapes)] + [outs[-1]]


def _exchange_halves(grads, small):
    ng = len(grads)
    grads = [_split_rows(g) for g in grads]

    def body(*refs):
        g_in, s_in = refs[:ng], refs[ng]
        g_out, s_out = refs[ng + 1:2 * ng + 1], refs[2 * ng + 1]
        send_sem, recv_sem = refs[2 * ng + 2:]
        x, y, c, _ = _place()
        sib = (x, y, 1 - c)
        copies = []
        for k in range(ng):
            copies.append(_remote(g_in[k].at[:, 1 - c], g_out[k], send_sem.at[k], recv_sem.at[k], sib))
        copies.append(_remote(s_in, s_out, send_sem.at[ng], recv_sem.at[ng], sib))
        for cp in copies:
            cp.start()
        for cp in copies:
            cp.wait_recv()
        for cp in copies:
            cp.wait_send()

    out_shape = [jax.ShapeDtypeStruct((g.shape[0], g.shape[2], g.shape[3]), g.dtype) for g in grads]
    out_shape.append(jax.ShapeDtypeStruct(small.shape, small.dtype))
    return pl.pallas_call(
        body, in_specs=[HBM_SPEC] * (ng + 1), out_specs=[HBM_SPEC] * (ng + 1), out_shape=out_shape,
        scratch_shapes=[pltpu.SemaphoreType.DMA((ng + 1,)), pltpu.SemaphoreType.DMA((ng + 1,))],
        name="exchange_halves")(*grads, small)


def _scatter_to_chips(parts, small):
    npart = len(parts)

    def body(*refs):
        p_in, s_in = refs[:npart], refs[npart]
        p_out, s_out = refs[npart + 1:2 * npart + 1], refs[2 * npart + 1]
        send_sem, recv_sem, loc_sem = refs[2 * npart + 2:]
        x, y, c, chips = _place()
        me = 2 * x + y
        local = [pltpu.make_async_copy(s_in, s_out.at[me], loc_sem.at[0])]
        for cp in local:
            cp.start()
        sends, recvs = [], []
        for j, (px, py) in enumerate(chips):
            to = 2 * px + py
            for k in range(npart + 1):
                src = s_in if k == npart else p_in[k].at[to]
                out = s_out if k == npart else p_out[k]
                sems = (send_sem.at[k * 3 + j], recv_sem.at[k * 3 + j])
                sends.append(_remote(src, out.at[me], *sems, (px, py, c)))
                recvs.append(_remote(src, out.at[to], *sems, (px, py, c)))
        for cp in sends:
            cp.start()
        for cp in recvs:
            cp.wait_recv()
        for cp in sends:
            cp.wait_send()
        for cp in local:
            cp.wait()

    out_shape = [jax.ShapeDtypeStruct(p.shape, p.dtype) for p in parts]
    out_shape.append(jax.ShapeDtypeStruct((N_CHIPS,) + small.shape, small.dtype))
    return pl.pallas_call(
        body, in_specs=[HBM_SPEC] * (npart + 1), out_specs=[HBM_SPEC] * (npart + 1), out_shape=out_shape,
        scratch_shapes=[pltpu.SemaphoreType.DMA((3 * npart + 3,)), pltpu.SemaphoreType.DMA((3 * npart + 3,)),
                        pltpu.SemaphoreType.DMA((1,))],
        name="scatter_to_chips")(*parts, small)


def _join_halves(reduced, dests, out_shapes):
    nr = len(reduced)

    def body(*refs):
        r_in = refs[:nr]
        outs = refs[nr:nr + len(out_shapes)]
        send_sem, recv_sem, back_send, back_recv = refs[nr + len(out_shapes):]
        x, y, c, _ = _place()
        sib = (x, y, 1 - c)
        sends, backs = [], []
        for k in range(nr):
            oi, layer = dests[k]
            sends.append(_remote(r_in[k], outs[oi].at[layer, c], send_sem.at[k], recv_sem.at[k], sib))
        for cp in sends:
            cp.start()
        for k in range(nr):
            oi, layer = dests[k]
            theirs = outs[oi].at[layer, 1 - c]
            _remote(r_in[k], theirs, send_sem.at[k], recv_sem.at[k], sib).wait_recv()
            back = _remote(theirs, theirs, back_send.at[k], back_recv.at[k], sib)
            back.start()
            backs.append(back)
        for k in range(nr):
            oi, layer = dests[k]
            mine = outs[oi].at[layer, c]
            _remote(mine, mine, back_send.at[k], back_recv.at[k], sib).wait_recv()
        for cp in sends + backs:
            cp.wait_send()

    split = [(s[0], 2, s[1] // 2, s[2]) for s in out_shapes]
    outs = pl.pallas_call(
        body, in_specs=[HBM_SPEC] * nr, out_specs=[HBM_SPEC] * len(out_shapes),
        out_shape=[jax.ShapeDtypeStruct(s, F32) for s in split],
        scratch_shapes=[pltpu.SemaphoreType.DMA((nr,)), pltpu.SemaphoreType.DMA((nr,)),
                        pltpu.SemaphoreType.DMA((nr,)), pltpu.SemaphoreType.DMA((nr,))],
        name="join_halves")(*reduced)
    return [o.reshape(s) for o, s in zip(outs, out_shapes)]


def _pack(arrs):
    flat = jnp.concatenate([a.reshape(-1).astype(F32) for a in arrs])
    n = flat.shape[0]
    rows = -(-n // PACK_WIDTH)
    rows = -(-rows // 8) * 8
    return jnp.pad(flat, (0, rows * PACK_WIDTH - n)).reshape(rows, PACK_WIDTH)


def _unpack(buf, shapes):
    flat = buf.reshape(-1)
    out, off = [], 0
    for shp in shapes:
        n = 1
        for s in shp:
            n *= s
        out.append(flat[off:off + n].reshape(shp))
        off += n
    return out


def _unshard_cols(stacked):
    moved = jnp.moveaxis(stacked, 0, -2)
    return moved.reshape(moved.shape[:-2] + (moved.shape[-2] * moved.shape[-1],))


def _col_shard(full, s, width):
    return lax.dynamic_slice_in_dim(full, s * width, width, axis=full.ndim - 1)


def kernel(x, meta_tokens, mix_norm_g, ffn_norm_g, ffn_w1, ffn_w2, cp_w_in, cp_conv_w, cp_conv_b, cp_ln_g, cp_ln_b, cp_pool_w, cp_pool_scale, cp_w_out, gla_w_in, gla_gate_w2, gla_gate_b, gla_head_g, gla_w_out, final_norm_g, loss_target, m_meta_tokens, m_mix_norm_g, m_ffn_norm_g, m_ffn_w1, m_ffn_w2, m_cp_w_in, m_cp_conv_w, m_cp_conv_b, m_cp_ln_g, m_cp_ln_b, m_cp_pool_w, m_cp_pool_scale, m_cp_w_out, m_gla_w_in, m_gla_gate_w2, m_gla_gate_b, m_gla_head_g, m_gla_w_out, m_final_norm_g, v_meta_tokens, v_mix_norm_g, v_ffn_norm_g, v_ffn_w1, v_ffn_w2, v_cp_w_in, v_cp_conv_w, v_cp_conv_b, v_cp_ln_g, v_cp_ln_b, v_cp_pool_w, v_cp_pool_scale, v_cp_w_out, v_gla_w_in, v_gla_gate_w2, v_gla_gate_b, v_gla_head_g, v_gla_w_out, v_final_norm_g):
    d = D_MODEL
    chip = 2 * lax.axis_index("x") + lax.axis_index("y")
    core = lax.axis_index("c")
    seq = x.shape[1]
    t = seq + CHUNK

    big_w = [ffn_w1, ffn_w2, cp_w_in, cp_w_out, gla_w_in, gla_w_out]
    sharded_small = [meta_tokens, cp_conv_w, gla_gate_w2, gla_gate_b, gla_head_g]
    gathered = _gather_weights([w.astype(BF16) for w in big_w], _pack(sharded_small))
    w1g, w2g, cpin_g, cpout_g, glain_g, glaout_g, small_g = gathered
    per_chip = [_unpack(small_g[j], [a.shape for a in sharded_small]) for j in range(N_CHIPS)]
    meta_f, conv_w_f, gate_w_f, gate_b_f, head_g_f = [
        jnp.concatenate([per_chip[j][i] for j in range(N_CHIPS)], axis=-1) for i in range(len(sharded_small))]
    conv_w_f, gate_w_f = conv_w_f[0], gate_w_f[0]
    w_cp_in = _unshard_cols(cpin_g[:, 0])
    w_cp_out = cpout_g.reshape(CONV_DIM + POOL_DIM, d)
    w_gla_in = jnp.pad(_unshard_cols(glain_g[:, 0]), ((0, 0), (0, GLA_IN_PAD - GLA_IN)))
    w_gla_out = glaout_g.reshape(GLA_DV, d)
    gate_w_pad = jnp.pad(gate_w_f, ((0, GATE_PAD - GATE_RANK), (0, 0))).astype(BF16)
    row = lambda a: a.reshape(1, -1)

    h0 = jnp.concatenate([jnp.zeros((PAD_ROWS, d), F32), meta_f, x[0]], axis=0)
    z0, u0 = _norm_matmul(h0, row(mix_norm_g[0]), w_cp_in, 512, "cp_in_proj")
    c0, pm0, mix0 = _cp_seq_fwd(z0, conv_w_f, cp_conv_b, cp_ln_g, cp_ln_b, cp_pool_w[0], cp_pool_scale)
    h1 = _matmul_residual(mix0, w_cp_out, h0, "cp_out_proj")
    h2, hp0, uf0 = _ffn_fwd(h1, row(ffn_norm_g[0]), w1g, w2g, 0, "ffn0_fwd")
    z1, u2 = _norm_matmul(h2, row(mix_norm_g[1]), w_gla_in, 640, "gla_in_proj")
    o1, mix1, states = _gla_seq_fwd(z1, gate_w_pad, gate_b_f, head_g_f)
    h3 = _matmul_residual(mix1, w_gla_out, h2, "gla_out_proj")
    h4, hp1, uf1 = _ffn_fwd(h3, row(ffn_norm_g[1]), w1g, w2g, 1, "ffn1_fwd")

    target = jnp.pad(loss_target[0], ((CHUNK, 0), (0, 0)))
    dh4, d_final_g, loss_part = _loss_bwd(h4, row(final_norm_g), target)

    dh3, dhp1, d_ffn_g1 = _ffn_bwd_data(dh4, h3, row(ffn_norm_g[1]), hp1, w1g, w2g, 1, "ffn1_bwd")
    dw1_1 = _wgrad(uf1, dhp1, N_CHIPS, d, d, False, True, False, "ffn1_dw1")
    dw2_1 = _wgrad(hp1, dh4, N_CHIPS, d, d, True, False, True, "ffn1_dw2")

    dmix1 = _dgrad(dh3, w_gla_out, "gla_out_dgrad")
    dw_gla_out = _wgrad(mix1, dh3, 1, GLA_DV, d, False, False, False, "gla_out_dw")
    dz1, d_gate_w, d_gate_b, d_head_g = _gla_seq_bwd(dmix1, o1, z1, states, gate_w_pad, gate_b_f, head_g_f)
    dh2, d_mix_g1 = _dgrad_norm_bwd(dz1, w_gla_in, h2, row(mix_norm_g[1]), dh3, 640, "gla_in_dgrad")
    dw_gla_in = _wgrad(u2, dz1, GLA_IN_PAD // 640, d, 640, False, True, False, "gla_in_dw")

    dh1, dhp0, d_ffn_g0 = _ffn_bwd_data(dh2, h1, row(ffn_norm_g[0]), hp0, w1g, w2g, 0, "ffn0_bwd")
    dw1_0 = _wgrad(uf0, dhp0, N_CHIPS, d, d, False, True, False, "ffn0_dw1")
    dw2_0 = _wgrad(hp0, dh2, N_CHIPS, d, d, True, False, True, "ffn0_dw2")

    dmix0 = _dgrad(dh1, w_cp_out, "cp_out_dgrad")
    dw_cp_out = _wgrad(mix0, dh1, 1, CONV_DIM + POOL_DIM, d, False, False, False, "cp_out_dw")
    dz0, d_conv_w, d_cp_vec, d_pool_w = _cp_seq_bwd(dmix0, z0, c0, pm0, conv_w_f, cp_ln_g, cp_ln_b, cp_pool_w[0], cp_pool_scale)
    dh0, d_mix_g0 = _dgrad_norm_bwd(dz0, w_cp_in, h0, row(mix_norm_g[0]), dh1, 512, "cp_in_dgrad")
    dw_cp_in = _wgrad(u0, dz0, N_CHIPS, d, CP_IN // N_CHIPS, False, True, False, "cp_in_dw")

    grad_x = dh0[CHUNK:][None]

    gla_in_cols = jnp.moveaxis(dw_gla_in, 0, 1).reshape(d, GLA_IN_PAD)[:, :GLA_IN]
    grads = [dw1_0, dw1_1, dw2_0, dw2_1, dw_cp_in,
             dw_cp_out.reshape(N_CHIPS, -1, d),
             jnp.moveaxis(gla_in_cols.reshape(d, N_CHIPS, GLA_IN // N_CHIPS), 1, 0),
             dw_gla_out.reshape(N_CHIPS, -1, d)]
    dests = [(0, 0), (0, 1), (1, 0), (1, 1), (2, 0), (3, 0), (4, 0), (5, 0)]
    small_full = [dh0[PAD_ROWS:CHUNK], jnp.concatenate([d_mix_g0, d_mix_g1], axis=0),
                  jnp.concatenate([d_ffn_g0, d_ffn_g1], axis=0), d_conv_w[:CONV_WIDTH][None],
                  d_cp_vec[0:1], d_cp_vec[1:2], d_cp_vec[2:3], d_pool_w[None], d_cp_vec[3:4],
                  d_gate_w[:GATE_RANK][None], d_gate_b, d_head_g, d_final_g[0], loss_part[0, 0:1]]
    small_mine = _pack(small_full)
    c_idx = core.reshape(1).astype(jnp.int32)
    recv = _exchange_halves(grads, small_mine)
    chip_sums = [_sum_halves(g, r, c_idx, "chip_sum_%d" % k) for k, (g, r) in enumerate(zip(grads, recv[:-1]))]
    small_chip = _add2(small_mine, recv[-1], "chip_sum_small")
    slots = _scatter_to_chips(chip_sums, small_chip)
    chip_idx = chip.reshape(1).astype(jnp.int32)
    reduced = [_sum_own_and_slots(a, s, chip_idx, "slot_sum_%d" % k) for k, (a, s) in enumerate(zip(chip_sums, slots[:-1]))]
    small_red = _sum_slots(slots[-1], "slot_sum_small")
    big_grads = _join_halves(reduced, dests, [w.shape for w in big_w])

    big_m = [m_ffn_w1, m_ffn_w2, m_cp_w_in, m_cp_w_out, m_gla_w_in, m_gla_w_out]
    big_v = [v_ffn_w1, v_ffn_w2, v_cp_w_in, v_cp_w_out, v_gla_w_in, v_gla_w_out]
    big_out = []
    for k, (w, g, m, v) in enumerate(zip(big_w, big_grads, big_m, big_v)):
        two_d = lambda a: a.reshape(-1, a.shape[-1])
        outs = _adamw(two_d(w), two_d(g), two_d(m), two_d(v), "adamw_%d" % k)
        big_out.append([g] + [o.reshape(w.shape) for o in outs])

    (g_meta, g_mix, g_ffn, g_conv_w, g_conv_b, g_ln_g, g_ln_b, g_pool_w, g_pool_scale, g_gate_w, g_gate_b, g_head,
     g_final, loss_sum) = _unpack(small_red, [a.shape for a in small_full])
    g_meta = _col_shard(g_meta, chip, meta_tokens.shape[-1])
    g_conv_w = _col_shard(g_conv_w, chip, cp_conv_w.shape[-1])
    g_gate_w = _col_shard(g_gate_w, chip, gla_gate_w2.shape[-1])
    g_gate_b = _col_shard(g_gate_b, chip, gla_gate_b.shape[-1])
    g_head = _col_shard(g_head, chip, gla_head_g.shape[-1])
    small_w = [meta_tokens, mix_norm_g, ffn_norm_g, cp_conv_w, cp_conv_b, cp_ln_g, cp_ln_b, cp_pool_w, cp_pool_scale,
               gla_gate_w2, gla_gate_b, gla_head_g, final_norm_g]
    small_m = [m_meta_tokens, m_mix_norm_g, m_ffn_norm_g, m_cp_conv_w, m_cp_conv_b, m_cp_ln_g, m_cp_ln_b, m_cp_pool_w,
               m_cp_pool_scale, m_gla_gate_w2, m_gla_gate_b, m_gla_head_g, m_final_norm_g]
    small_v = [v_meta_tokens, v_mix_norm_g, v_ffn_norm_g, v_cp_conv_w, v_cp_conv_b, v_cp_ln_g, v_cp_ln_b, v_cp_pool_w,
               v_cp_pool_scale, v_gla_gate_w2, v_gla_gate_b, v_gla_head_g, v_final_norm_g]
    small_g = [g_meta, g_mix, g_ffn, g_conv_w, g_conv_b, g_ln_g, g_ln_b, g_pool_w, g_pool_scale, g_gate_w, g_gate_b,
               g_head, g_final]
    shapes = [w.shape for w in small_w]
    small_g = [g.reshape(s) for g, s in zip(small_g, shapes)]
    s_delta, s_m, s_v = _adamw(_pack(small_w), _pack(small_g), _pack(small_m), _pack(small_v), "adamw_small")
    s_delta, s_m, s_v = _unpack(s_delta, shapes), _unpack(s_m, shapes), _unpack(s_v, shapes)

    order = ["meta", "mix", "ffn", "w1", "w2", "cp_in", "conv_w", "conv_b", "ln_g", "ln_b", "pool_w", "pool_scale",
             "cp_out", "gla_in", "gate_w", "gate_b", "head", "gla_out", "final"]
    small_names = ["meta", "mix", "ffn", "conv_w", "conv_b", "ln_g", "ln_b", "pool_w", "pool_scale", "gate_w", "gate_b",
                   "head", "final"]
    big_names = ["w1", "w2", "cp_in", "cp_out", "gla_in", "gla_out"]
    table = {n: (small_g[i], s_delta[i], s_m[i], s_v[i]) for i, n in enumerate(small_names)}
    table.update({n: tuple(big_out[i]) for i, n in enumerate(big_names)})
    loss = loss_sum.reshape(())
    return (loss, grad_x, *[table[n][0] for n in order], *[table[n][1] for n in order],
            *[table[n][2] for n in order], *[table[n][3] for n in order])
```

```python
import functools

import jax
import jax.numpy as jnp
from jax import lax
from jax.experimental import pallas as pl
from jax.experimental.pallas import tpu as pltpu

F32 = jnp.float32
BF16 = jnp.bfloat16

D_MODEL = 1024
N_META = 16
CHUNK = 64
PAD_ROWS = CHUNK - N_META
EPS = 1e-5
CONV_DIM = 512
CONV_WIDTH = 31
CONV_HALO = 32
POOL_DIM = 512
POOL_WINDOWS = (2, 4, 8, 16)
POOL_GROUP = 128
POOL_HALO = 16
CP_IN = 2 * CONV_DIM + POOL_DIM
GLA_HEADS = 4
GLA_DK = 512
GLA_DV = 1024
GLA_HK = GLA_DK // GLA_HEADS
GLA_HV = GLA_DV // GLA_HEADS
GATE_RANK = 16
GATE_PAD = 128
GATE_NORM = 16.0
GLA_IN = 2 * GLA_DK + 2 * GLA_DV + GATE_RANK
GLA_IN_PAD = 2 * GLA_DK + 2 * GLA_DV + GATE_PAD
N_CHIPS = 4
ADAM_LR = 0.001
ADAM_B1 = 0.9
ADAM_B2 = 0.999
ADAM_EPS = 1e-08
ADAM_WD = 0.01
ADAM_STEP = 10

VMEM_LIMIT_BYTES = 56 * 1024 * 1024
ROW_TILE_TARGET = 832
PACK_WIDTH = 1024
MESH = pl.DeviceIdType.MESH
HBM_SPEC = pl.BlockSpec(memory_space=pltpu.HBM)
ANY_SPEC = pl.BlockSpec(memory_space=pl.ANY)
SEM_SPEC = pl.BlockSpec(memory_space=pltpu.SEMAPHORE)
SIDE_EFFECT = pltpu.SideEffectType.DATAFLOW_SIDE_EFFECTING


def _cparams(*sem):
    return pltpu.CompilerParams(dimension_semantics=sem, vmem_limit_bytes=VMEM_LIMIT_BYTES)


def _row_tile(t, target, mult):
    best = mult
    for cand in range(mult, min(t, target) + 1, mult):
        if t % cand == 0:
            best = cand
    assert t % best == 0, (t, best)
    return best


def _rms(h, g):
    return h * lax.rsqrt(jnp.mean(h * h, axis=-1, keepdims=True) + EPS) * g


def _rms_bwd(h, g, du):
    r = lax.rsqrt(jnp.mean(h * h, axis=-1, keepdims=True) + EPS)
    xhat = h * r
    dxh = du * g
    dh = r * (dxh - xhat * jnp.mean(dxh * xhat, axis=-1, keepdims=True))
    return dh, du * xhat


def _valid_rows(i, tm):
    row = i * tm + lax.broadcasted_iota(jnp.int32, (tm, 1), 0)
    return row >= PAD_ROWS


def _dot(a, b):
    return jnp.dot(a, b, preferred_element_type=F32)


def _dot_nt(a, b):
    return lax.dot_general(a, b, (((1,), (1,)), ((), ())), preferred_element_type=F32)


def _dot_tn(a, b):
    return lax.dot_general(a, b, (((0,), (0,)), ((), ())), preferred_element_type=F32)


def _accumulate(ref, val, first):
    @pl.when(first)
    def _():
        ref[...] = val

    @pl.when(jnp.logical_not(first))
    def _():
        ref[...] += val


def _call_after(dep, body, n_in, in_specs, args, **kw):
    if dep is None:
        return pl.pallas_call(body, in_specs=in_specs, **kw)(*args)

    def with_dep(*refs):
        body(*refs[:n_in], *refs[n_in + 1:])

    return pl.pallas_call(with_dep, in_specs=list(in_specs) + [ANY_SPEC], **kw)(*args, dep)


def _norm_matmul(h, g, w, nc, name, dep=None):
    t, d = h.shape
    n = w.shape[1]
    tm = _row_tile(t, ROW_TILE_TARGET, CHUNK)

    def body(h_ref, g_ref, w_ref, z_ref, u_ref):
        u = _rms(h_ref[...], g_ref[...]).astype(BF16)
        u_ref[...] = u
        for n0 in range(0, n, nc):
            z_ref[:, n0:n0 + nc] = _dot(u, w_ref[:, n0:n0 + nc]).astype(BF16)

    return _call_after(
        dep, body, 3,
        [pl.BlockSpec((tm, d), lambda i: (i, 0)), pl.BlockSpec((1, d), lambda i: (0, 0)),
         pl.BlockSpec((d, n), lambda i: (0, 0))], (h, g, w), grid=(t // tm,),
        out_specs=[pl.BlockSpec((tm, n), lambda i: (i, 0)), pl.BlockSpec((tm, d), lambda i: (i, 0))],
        out_shape=[jax.ShapeDtypeStruct((t, n), BF16), jax.ShapeDtypeStruct((t, d), BF16)],
        compiler_params=_cparams("parallel"), name=name)


def _matmul_residual(a, w, h, name):
    t, k = a.shape
    d = w.shape[1]
    tm = _row_tile(t, ROW_TILE_TARGET, CHUNK)

    def body(a_ref, w_ref, h_ref, o_ref):
        o_ref[...] = h_ref[...] + _dot(a_ref[...], w_ref[...])

    return pl.pallas_call(
        body, grid=(t // tm,),
        in_specs=[pl.BlockSpec((tm, k), lambda i: (i, 0)), pl.BlockSpec((k, d), lambda i: (0, 0)),
                  pl.BlockSpec((tm, d), lambda i: (i, 0))],
        out_specs=pl.BlockSpec((tm, d), lambda i: (i, 0)),
        out_shape=jax.ShapeDtypeStruct((t, d), F32),
        compiler_params=_cparams("parallel"), name=name)(a, w, h)


def _ffn_fwd(h, g, w1g, w2g, name):
    t, d = h.shape
    ns, ffs = w1g.shape[0], w1g.shape[2]
    tm = _row_tile(t, ROW_TILE_TARGET, CHUNK)

    def body(h_ref, g_ref, w1_ref, w2_ref, ho_ref, hp_ref, u_ref, acc_ref):
        s = pl.program_id(1)

        @pl.when(s == 0)
        def _():
            u_ref[...] = _rms(h_ref[...], g_ref[...]).astype(BF16)

        hp = _dot(u_ref[...], w1_ref[...])
        hp_ref[...] = hp.astype(BF16)
        a = jnp.maximum(hp, 0.0)
        _accumulate(acc_ref, _dot((a * a).astype(BF16), w2_ref[...]), s == 0)

        @pl.when(s == ns - 1)
        def _():
            ho_ref[...] = h_ref[...] + acc_ref[...]

    return pl.pallas_call(
        body, grid=(t // tm, ns),
        in_specs=[pl.BlockSpec((tm, d), lambda i, s: (i, 0)), pl.BlockSpec((1, d), lambda i, s: (0, 0)),
                  pl.BlockSpec((None, d, ffs), lambda i, s: (s, 0, 0)),
                  pl.BlockSpec((None, ffs, d), lambda i, s: (s, 0, 0))],
        out_specs=[pl.BlockSpec((tm, d), lambda i, s: (i, 0)), pl.BlockSpec((tm, ffs), lambda i, s: (i, s)),
                   pl.BlockSpec((tm, d), lambda i, s: (i, 0))],
        out_shape=[jax.ShapeDtypeStruct((t, d), F32), jax.ShapeDtypeStruct((t, ns * ffs), BF16),
                   jax.ShapeDtypeStruct((t, d), BF16)],
        scratch_shapes=[pltpu.VMEM((tm, d), F32)],
        compiler_params=_cparams("parallel", "arbitrary"), name=name)(h, g, w1g, w2g)


def _ffn_bwd_data(dh, h, g, hp, w1g, w2g, name, dep=None):
    t, d = h.shape
    ns, ffs = w1g.shape[0], w1g.shape[2]
    tm = _row_tile(t, ROW_TILE_TARGET, CHUNK)

    def body(dh_ref, h_ref, g_ref, hp_ref, w1_ref, w2_ref, dhi_ref, dhp_ref, dg_ref, acc_ref):
        i, s = pl.program_id(0), pl.program_id(1)
        da = _dot_nt(dh_ref[...].astype(BF16), w2_ref[...])
        dhp = (da * (2.0 * jnp.maximum(hp_ref[...].astype(F32), 0.0))).astype(BF16)
        dhp_ref[...] = dhp
        _accumulate(acc_ref, _dot_nt(dhp, w1_ref[...]), s == 0)

        @pl.when(s == ns - 1)
        def _():
            dhn, dgr = _rms_bwd(h_ref[...], g_ref[...], acc_ref[...])
            dhi_ref[...] = jnp.where(_valid_rows(i, tm), dh_ref[...] + dhn, 0.0)
            _accumulate(dg_ref, jnp.sum(dgr, axis=0, keepdims=True), i == 0)

    return _call_after(
        dep, body, 6,
        [pl.BlockSpec((tm, d), lambda i, s: (i, 0)), pl.BlockSpec((tm, d), lambda i, s: (i, 0)),
         pl.BlockSpec((1, d), lambda i, s: (0, 0)), pl.BlockSpec((tm, ffs), lambda i, s: (i, s)),
         pl.BlockSpec((None, d, ffs), lambda i, s: (s, 0, 0)),
         pl.BlockSpec((None, ffs, d), lambda i, s: (s, 0, 0))], (dh, h, g, hp, w1g, w2g), grid=(t // tm, ns),
        out_specs=[pl.BlockSpec((tm, d), lambda i, s: (i, 0)), pl.BlockSpec((tm, ffs), lambda i, s: (i, s)),
                   pl.BlockSpec((1, d), lambda i, s: (0, 0))],
        out_shape=[jax.ShapeDtypeStruct((t, d), F32), jax.ShapeDtypeStruct((t, ns * ffs), BF16),
                   jax.ShapeDtypeStruct((1, d), F32)],
        scratch_shapes=[pltpu.VMEM((tm, d), F32)],
        compiler_params=_cparams("arbitrary", "arbitrary"), name=name)


def _wgrad(x, dy, nb, xc, yc, x_by_block, dy_by_block, relu2, name):
    t = x.shape[0]
    tm = _row_tile(t, ROW_TILE_TARGET, CHUNK)

    def body(x_ref, dy_ref, o_ref):
        k = pl.program_id(1)
        xv = x_ref[...]
        if relu2:
            xv = jnp.maximum(xv.astype(F32), 0.0)
            xv = xv * xv
        _accumulate(o_ref, _dot_tn(xv.astype(BF16), dy_ref[...].astype(BF16)), k == 0)

    return pl.pallas_call(
        body, grid=(nb, t // tm),
        in_specs=[pl.BlockSpec((tm, xc), (lambda b, k: (k, b)) if x_by_block else (lambda b, k: (k, 0))),
                  pl.BlockSpec((tm, yc), (lambda b, k: (k, b)) if dy_by_block else (lambda b, k: (k, 0)))],
        out_specs=pl.BlockSpec((None, xc, yc), lambda b, k: (b, 0, 0)),
        out_shape=jax.ShapeDtypeStruct((nb, xc, yc), F32),
        compiler_params=_cparams("parallel", "arbitrary"), name=name)(x, dy)


def _dgrad(dh, w, name, dep=None):
    t, d = dh.shape
    k = w.shape[0]
    tm = _row_tile(t, ROW_TILE_TARGET, CHUNK)

    def body(dh_ref, w_ref, o_ref):
        o_ref[...] = _dot_nt(dh_ref[...].astype(BF16), w_ref[...]).astype(BF16)

    return _call_after(
        dep, body, 2,
        [pl.BlockSpec((tm, d), lambda i: (i, 0)), pl.BlockSpec((k, d), lambda i: (0, 0))], (dh, w), grid=(t // tm,),
        out_specs=pl.BlockSpec((tm, k), lambda i: (i, 0)),
        out_shape=jax.ShapeDtypeStruct((t, k), BF16),
        compiler_params=_cparams("parallel"), name=name)


def _dgrad_norm_bwd(dz, w, h, g, dh, nc, name):
    t, d = h.shape
    n = w.shape[1]
    tm = _row_tile(t, ROW_TILE_TARGET // 2, 16)

    def body(dz_ref, w_ref, h_ref, g_ref, dh_ref, dhi_ref, dg_ref):
        i = pl.program_id(0)
        du = jnp.zeros((tm, d), F32)
        for n0 in range(0, n, nc):
            du = du + _dot_nt(dz_ref[:, n0:n0 + nc], w_ref[:, n0:n0 + nc])
        dhn, dgr = _rms_bwd(h_ref[...], g_ref[...], du)
        dhi_ref[...] = jnp.where(_valid_rows(i, tm), dh_ref[...] + dhn, 0.0)
        _accumulate(dg_ref, jnp.sum(dgr, axis=0, keepdims=True), i == 0)

    return pl.pallas_call(
        body, grid=(t // tm,),
        in_specs=[pl.BlockSpec((tm, n), lambda i: (i, 0)), pl.BlockSpec((d, n), lambda i: (0, 0)),
                  pl.BlockSpec((tm, d), lambda i: (i, 0)), pl.BlockSpec((1, d), lambda i: (0, 0)),
                  pl.BlockSpec((tm, d), lambda i: (i, 0))],
        out_specs=[pl.BlockSpec((tm, d), lambda i: (i, 0)), pl.BlockSpec((1, d), lambda i: (0, 0))],
        out_shape=[jax.ShapeDtypeStruct((t, d), F32), jax.ShapeDtypeStruct((1, d), F32)],
        compiler_params=_cparams("arbitrary"), name=name)(dz, w, h, g, dh)


def _loss_bwd(h, g, target):
    t, d = h.shape
    tm = _row_tile(t, ROW_TILE_TARGET, CHUNK)

    def body(h_ref, g_ref, t_ref, dh_ref, dg_ref, loss_ref):
        i = pl.program_id(0)
        row = i * tm + lax.broadcasted_iota(jnp.int32, (tm, 1), 0)
        keep = row >= CHUNK
        hv, gv = h_ref[...], g_ref[...]
        err = jnp.where(keep, _rms(hv, gv) - t_ref[...], 0.0)
        part = 0.5 * jnp.sum(jnp.mean(err * err, axis=-1, keepdims=True), axis=0, keepdims=True)
        dhn, dgr = _rms_bwd(hv, gv, err * (1.0 / d))
        dh_ref[...] = dhn
        _accumulate(dg_ref, jnp.sum(dgr, axis=0, keepdims=True), i == 0)
        _accumulate(loss_ref, jnp.broadcast_to(part, (8, 128)), i == 0)

    return pl.pallas_call(
        body, grid=(t // tm,),
        in_specs=[pl.BlockSpec((tm, d), lambda i: (i, 0)), pl.BlockSpec((1, d), lambda i: (0, 0)),
                  pl.BlockSpec((tm, d), lambda i: (i, 0))],
        out_specs=[pl.BlockSpec((tm, d), lambda i: (i, 0)), pl.BlockSpec((1, d), lambda i: (0, 0)),
                   pl.BlockSpec((8, 128), lambda i: (0, 0))],
        out_shape=[jax.ShapeDtypeStruct((t, d), F32), jax.ShapeDtypeStruct((1, d), F32),
                   jax.ShapeDtypeStruct((8, 128), F32)],
        compiler_params=_cparams("arbitrary"), name="loss_bwd")(h, g, target)


CONV_BLOCK = 32


def _silu(x):
    return x * jax.nn.sigmoid(x)


def _cp_seq_fwd(z, conv_w, conv_b, ln_g, ln_b, pool_w, pool_scale):
    t = z.shape[0]
    tm = _row_tile(t, ROW_TILE_TARGET, CHUNK)

    def body(z_ref, cw_ref, cb_ref, lg_ref, lb_ref, pw_ref, ps_ref, c_ref, pm_ref, mix_ref, gbuf, pbuf):
        i = pl.program_id(0)

        @pl.when(i == 0)
        def _():
            gbuf[0:CONV_HALO, :] = jnp.zeros((CONV_HALO, CONV_DIM), F32)
            pbuf[0:POOL_HALO, :] = jnp.zeros((POOL_HALO, POOL_DIM), F32)

        @pl.when(i > 0)
        def _():
            gbuf[0:CONV_HALO, :] = gbuf[tm:tm + CONV_HALO, :]
            pbuf[0:POOL_HALO, :] = pbuf[tm:tm + POOL_HALO, :]

        av = z_ref[:, 0:CONV_DIM].astype(F32)
        ag = z_ref[:, CONV_DIM:2 * CONV_DIM].astype(F32)
        gbuf[CONV_HALO:CONV_HALO + tm, :] = av * jax.nn.sigmoid(ag)
        pbuf[POOL_HALO:POOL_HALO + tm, :] = z_ref[:, 2 * CONV_DIM:CP_IN].astype(F32)

        def conv_block(rb, carry):
            base = pl.multiple_of(rb * CONV_BLOCK, CONV_BLOCK)
            win = gbuf[pl.ds(base, CONV_BLOCK + CONV_HALO), :]
            acc = jnp.zeros((CONV_BLOCK, CONV_DIM), F32)
            for k in range(CONV_WIDTH):
                off = CONV_HALO - (CONV_WIDTH - 1) + k
                acc = acc + cw_ref[k:k + 1, :] * win[off:off + CONV_BLOCK, :]
            c_ref[pl.ds(base, CONV_BLOCK), :] = acc + cb_ref[...]
            return carry

        lax.fori_loop(0, tm // CONV_BLOCK, conv_block, 0)

        c = c_ref[...]
        mu = jnp.mean(c, axis=-1, keepdims=True)
        xc = c - mu
        ln = xc * lax.rsqrt(jnp.mean(xc * xc, axis=-1, keepdims=True) + EPS) * lg_ref[...] + lb_ref[...]
        row = i * tm + lax.broadcasted_iota(jnp.int32, (tm, 1), 0)
        mix_ref[:, 0:CONV_DIM] = jnp.where(row >= PAD_ROWS, _silu(ln), 0.0).astype(BF16)

        tpos = (row - PAD_ROWS + 1).astype(F32)
        for gi, wdw in enumerate(POOL_WINDOWS):
            lo = POOL_GROUP * gi
            cur = pbuf[POOL_HALO:POOL_HALO + tm, lo:lo + POOL_GROUP]
            sacc = cur
            for j in range(1, wdw):
                sacc = sacc + pbuf[POOL_HALO - j:POOL_HALO - j + tm, lo:lo + POOL_GROUP]
            pm = (sacc / jnp.clip(tpos, 1.0, float(wdw)) - cur).astype(BF16)
            pm_ref[:, lo:lo + POOL_GROUP] = pm
            pg = _dot(pm, pw_ref[gi].astype(BF16))
            mix_ref[:, CONV_DIM + lo:CONV_DIM + lo + POOL_GROUP] = (pg * ps_ref[:, lo:lo + POOL_GROUP]).astype(BF16)

    vec = pl.BlockSpec((1, CONV_DIM), lambda i: (0, 0))
    return pl.pallas_call(
        body, grid=(t // tm,),
        in_specs=[pl.BlockSpec((tm, CP_IN), lambda i: (i, 0)),
                  pl.BlockSpec((CONV_WIDTH, CONV_DIM), lambda i: (0, 0)), vec, vec, vec,
                  pl.BlockSpec((len(POOL_WINDOWS), POOL_GROUP, POOL_GROUP), lambda i: (0, 0, 0)), vec],
        out_specs=[pl.BlockSpec((tm, CONV_DIM), lambda i: (i, 0)), pl.BlockSpec((tm, POOL_DIM), lambda i: (i, 0)),
                   pl.BlockSpec((tm, CONV_DIM + POOL_DIM), lambda i: (i, 0))],
        out_shape=[jax.ShapeDtypeStruct((t, CONV_DIM), F32), jax.ShapeDtypeStruct((t, POOL_DIM), BF16),
                   jax.ShapeDtypeStruct((t, CONV_DIM + POOL_DIM), BF16)],
        scratch_shapes=[pltpu.VMEM((tm + CONV_HALO, CONV_DIM), F32), pltpu.VMEM((tm + POOL_HALO, POOL_DIM), F32)],
        compiler_params=_cparams("arbitrary"), name="cp_seq_fwd")(z, conv_w, conv_b, ln_g, ln_b, pool_w, pool_scale)


def _cp_seq_bwd(dmix, z, c, pm, conv_w, ln_g, ln_b, pool_w, pool_scale):
    t = z.shape[0]
    tm = _row_tile(t, ROW_TILE_TARGET, CHUNK)
    nt = t // tm

    def body(dmix_ref, z_ref, c_ref, pm_ref, cw_ref, lg_ref, lb_ref, pw_ref, ps_ref,
             dz_ref, dcw_ref, dvec_ref, dpw_ref, dcbuf, qbuf, glu_buf, dwacc):
        i = pl.program_id(0)
        tile = nt - 1 - i

        @pl.when(i == 0)
        def _():
            dcbuf[tm:tm + CONV_HALO, :] = jnp.zeros((CONV_HALO, CONV_DIM), F32)
            qbuf[tm:tm + POOL_HALO, :] = jnp.zeros((POOL_HALO, POOL_DIM), F32)
            dcw_ref[...] = jnp.zeros_like(dcw_ref)
            dwacc[...] = jnp.zeros_like(dwacc)
            dvec_ref[...] = jnp.zeros_like(dvec_ref)
            dpw_ref[...] = jnp.zeros_like(dpw_ref)

        @pl.when(i > 0)
        def _():
            dcbuf[tm:tm + CONV_HALO, :] = dcbuf[0:CONV_HALO, :]
            qbuf[tm:tm + POOL_HALO, :] = qbuf[0:POOL_HALO, :]

        row = tile * tm + lax.broadcasted_iota(jnp.int32, (tm, 1), 0)
        cv = c_ref[...]
        mu = jnp.mean(cv, axis=-1, keepdims=True)
        xc = cv - mu
        rstd = lax.rsqrt(jnp.mean(xc * xc, axis=-1, keepdims=True) + EPS)
        xhat = xc * rstd
        ln = xhat * lg_ref[...] + lb_ref[...]
        sg = jax.nn.sigmoid(ln)
        da = jnp.where(row >= PAD_ROWS, dmix_ref[:, 0:CONV_DIM].astype(F32), 0.0)
        dln = da * (sg * (1.0 + ln * (1.0 - sg)))
        dxh = dln * lg_ref[...]
        dc = rstd * (dxh - jnp.mean(dxh, axis=-1, keepdims=True) - xhat * jnp.mean(dxh * xhat, axis=-1, keepdims=True))
        dcbuf[0:tm, :] = dc
        dvec_ref[0:1, :] += jnp.sum(dc, axis=0, keepdims=True)
        dvec_ref[1:2, :] += jnp.sum(dln * xhat, axis=0, keepdims=True)
        dvec_ref[2:3, :] += jnp.sum(dln, axis=0, keepdims=True)

        av = z_ref[:, 0:CONV_DIM].astype(F32)
        sig_g = jax.nn.sigmoid(z_ref[:, CONV_DIM:2 * CONV_DIM].astype(F32))
        glu_buf[...] = av * sig_g

        def conv_block(rb, carry):
            base = pl.multiple_of(rb * CONV_BLOCK, CONV_BLOCK)
            win = dcbuf[pl.ds(base, CONV_BLOCK + CONV_HALO), :]
            glu = glu_buf[pl.ds(base, CONV_BLOCK), :]
            acc = jnp.zeros((CONV_BLOCK, CONV_DIM), F32)
            for k in range(CONV_WIDTH):
                off = CONV_WIDTH - 1 - k
                slab = win[off:off + CONV_BLOCK, :]
                acc = acc + cw_ref[k:k + 1, :] * slab
                prod = slab * glu
                part = prod[0:8]
                for q in range(1, CONV_BLOCK // 8):
                    part = part + prod[8 * q:8 * q + 8]
                dwacc[k] += part
            glu_buf[pl.ds(base, CONV_BLOCK), :] = acc
            return carry

        lax.fori_loop(0, tm // CONV_BLOCK, conv_block, 0)

        @pl.when(i == nt - 1)
        def _():
            for k in range(CONV_WIDTH):
                dcw_ref[k:k + 1, :] = jnp.sum(dwacc[k], axis=0, keepdims=True)
        dglu = glu_buf[...]
        dz_ref[:, 0:CONV_DIM] = (dglu * sig_g).astype(BF16)
        dz_ref[:, CONV_DIM:2 * CONV_DIM] = (dglu * av * sig_g * (1.0 - sig_g)).astype(BF16)

        tpos = (row - PAD_ROWS + 1).astype(F32)
        for gi, wdw in enumerate(POOL_WINDOWS):
            lo = POOL_GROUP * gi
            dp = dmix_ref[:, CONV_DIM + lo:CONV_DIM + lo + POOL_GROUP].astype(F32)
            pmv = pm_ref[:, lo:lo + POOL_GROUP]
            pwb = pw_ref[gi].astype(BF16)
            dvec_ref[3:4, lo:lo + POOL_GROUP] += jnp.sum(dp * _dot(pmv, pwb), axis=0, keepdims=True)
            dq = (dp * ps_ref[:, lo:lo + POOL_GROUP]).astype(BF16)
            dpw_ref[gi] += _dot_tn(pmv, dq)
            dpm = _dot_nt(dq, pwb)
            qbuf[0:tm, lo:lo + POOL_GROUP] = dpm / jnp.clip(tpos, 1.0, float(wdw))
            sacc = -dpm
            for j in range(wdw):
                sacc = sacc + qbuf[j:j + tm, lo:lo + POOL_GROUP]
            dz_ref[:, 2 * CONV_DIM + lo:2 * CONV_DIM + lo + POOL_GROUP] = sacc.astype(BF16)

    vec = pl.BlockSpec((1, CONV_DIM), lambda i: (0, 0))
    rev = lambda i: (nt - 1 - i, 0)
    return pl.pallas_call(
        body, grid=(nt,),
        in_specs=[pl.BlockSpec((tm, CONV_DIM + POOL_DIM), rev), pl.BlockSpec((tm, CP_IN), rev),
                  pl.BlockSpec((tm, CONV_DIM), rev), pl.BlockSpec((tm, POOL_DIM), rev),
                  pl.BlockSpec((CONV_WIDTH, CONV_DIM), lambda i: (0, 0)), vec, vec,
                  pl.BlockSpec((len(POOL_WINDOWS), POOL_GROUP, POOL_GROUP), lambda i: (0, 0, 0)), vec],
        out_specs=[pl.BlockSpec((tm, CP_IN), rev), pl.BlockSpec((CONV_WIDTH + 1, CONV_DIM), lambda i: (0, 0)),
                   pl.BlockSpec((8, CONV_DIM), lambda i: (0, 0)),
                   pl.BlockSpec((len(POOL_WINDOWS), POOL_GROUP, POOL_GROUP), lambda i: (0, 0, 0))],
        out_shape=[jax.ShapeDtypeStruct((t, CP_IN), BF16), jax.ShapeDtypeStruct((CONV_WIDTH + 1, CONV_DIM), F32),
                   jax.ShapeDtypeStruct((8, CONV_DIM), F32),
                   jax.ShapeDtypeStruct((len(POOL_WINDOWS), POOL_GROUP, POOL_GROUP), F32)],
        scratch_shapes=[pltpu.VMEM((tm + CONV_HALO, CONV_DIM), F32), pltpu.VMEM((tm + POOL_HALO, POOL_DIM), F32),
                        pltpu.VMEM((tm, CONV_DIM), F32), pltpu.VMEM((CONV_WIDTH + 1, 8, CONV_DIM), F32)],
        compiler_params=_cparams("arbitrary"), name="cp_seq_bwd")(dmix, z, c, pm, conv_w, ln_g, ln_b, pool_w, pool_scale)


Q0, K0, V0, G0, R0 = 0, GLA_DK, 2 * GLA_DK, 2 * GLA_DK + GLA_DV, 2 * GLA_DK + 2 * GLA_DV


def _split3(x):
    hi = x.astype(BF16)
    r1 = x - hi.astype(F32)
    mid = r1.astype(BF16)
    lo = (r1 - mid.astype(F32)).astype(BF16)
    return hi, mid, lo


def _tri(strict):
    r = lax.broadcasted_iota(jnp.int32, (CHUNK, CHUNK), 0)
    c = lax.broadcasted_iota(jnp.int32, (CHUNK, CHUNK), 1)
    return ((r > c) if strict else (r >= c)).astype(BF16)


def _gate_decay(r, gw_ref, gb_ref, tri):
    pre = _dot(r, gw_ref[...]) + gb_ref[...]
    lac = (jnp.minimum(pre, 0.0) - jnp.log(1.0 + jnp.exp(-jnp.abs(pre)))) * (1.0 / GATE_NORM)
    hi, mid, lo = _split3(lac)
    cum = _dot(tri, hi) + _dot(tri, mid) + _dot(tri, lo)
    return pre, cum, cum[CHUNK - 1:CHUNK, :]


def _gla_seq_fwd(z, gate_w, gate_b, head_g):
    t = z.shape[0]
    tm = _row_tile(t, ROW_TILE_TARGET, CHUNK)
    cpt = tm // CHUNK
    scale = GLA_HK ** -0.5

    def body(z_ref, gw_ref, gb_ref, hg_ref, o_ref, mix_ref, st_ref, state):
        @pl.when(pl.program_id(0) == 0)
        def _():
            state[...] = jnp.zeros_like(state)

        tri = _tri(False)

        def chunk(ci, carry):
            r0 = pl.multiple_of(ci * CHUNK, CHUNK)
            rows = pl.ds(r0, CHUNK)
            _, cum, tot = _gate_decay(z_ref[rows, R0:R0 + GATE_PAD], gw_ref, gb_ref, tri)
            dec = jnp.exp(tot - cum)
            e = jnp.exp(tot)
            st_ref[ci] = state[...].astype(BF16)
            for hd in range(GLA_HEADS):
                ks = slice(hd * GLA_HK, (hd + 1) * GLA_HK)
                vs = slice(hd * GLA_HV, (hd + 1) * GLA_HV)
                kdec = (z_ref[rows, K0 + hd * GLA_HK:K0 + (hd + 1) * GLA_HK].astype(F32) * dec[:, ks]).astype(BF16)
                v = z_ref[rows, V0 + hd * GLA_HV:V0 + (hd + 1) * GLA_HV]
                st = state[vs, :] * e[:, ks] + _dot_tn(v, kdec)
                state[vs, :] = st
                q = z_ref[rows, Q0 + hd * GLA_HK:Q0 + (hd + 1) * GLA_HK]
                o = _dot_nt(q, st.astype(BF16)) * scale
                ob = o.astype(BF16)
                o_ref[rows, vs] = ob
                on = _rms(ob.astype(F32), hg_ref[...])
                gv = z_ref[rows, G0 + hd * GLA_HV:G0 + (hd + 1) * GLA_HV].astype(F32)
                mix_ref[rows, vs] = (on * _silu(gv)).astype(BF16)
            return carry

        lax.fori_loop(0, cpt, chunk, 0)

    return pl.pallas_call(
        body, grid=(t // tm,),
        in_specs=[pl.BlockSpec((tm, GLA_IN_PAD), lambda i: (i, 0)),
                  pl.BlockSpec((GATE_PAD, GLA_DK), lambda i: (0, 0)), pl.BlockSpec((1, GLA_DK), lambda i: (0, 0)),
                  pl.BlockSpec((1, GLA_HV), lambda i: (0, 0))],
        out_specs=[pl.BlockSpec((tm, GLA_DV), lambda i: (i, 0)), pl.BlockSpec((tm, GLA_DV), lambda i: (i, 0)),
                   pl.BlockSpec((cpt, GLA_DV, GLA_HK), lambda i: (i, 0, 0))],
        out_shape=[jax.ShapeDtypeStruct((t, GLA_DV), BF16), jax.ShapeDtypeStruct((t, GLA_DV), BF16),
                   jax.ShapeDtypeStruct((t // CHUNK, GLA_DV, GLA_HK), BF16)],
        scratch_shapes=[pltpu.VMEM((GLA_DV, GLA_HK), F32)],
        compiler_params=_cparams("arbitrary"), name="gla_seq_fwd")(z, gate_w, gate_b, head_g)


def _gla_seq_bwd(dmix, o, z, states, gate_w, gate_b, head_g):
    t = z.shape[0]
    tm = _row_tile(t, ROW_TILE_TARGET, CHUNK)
    cpt = tm // CHUNK
    nt = t // tm
    scale = GLA_HK ** -0.5

    def body(dmix_ref, o_ref, z_ref, st_ref, gw_ref, gb_ref, hg_ref, dz_ref, dgw_ref, dgb_ref, dhg_ref, dstate):
        @pl.when(pl.program_id(0) == 0)
        def _():
            dstate[...] = jnp.zeros_like(dstate)
            dgw_ref[...] = jnp.zeros_like(dgw_ref)
            dgb_ref[...] = jnp.zeros_like(dgb_ref)
            dhg_ref[...] = jnp.zeros_like(dhg_ref)

        tri = _tri(False)
        tri_strict = _tri(True)

        def chunk(cj, carry):
            ci = cpt - 1 - cj
            r0 = pl.multiple_of(ci * CHUNK, CHUNK)
            rows = pl.ds(r0, CHUNK)
            r = z_ref[rows, R0:R0 + GATE_PAD]
            pre, cum, tot = _gate_decay(r, gw_ref, gb_ref, tri)
            dec = jnp.exp(tot - cum)
            e = jnp.exp(tot)
            dlac_parts = []
            dhg = jnp.zeros((1, GLA_HV), F32)
            for hd in range(GLA_HEADS):
                ks = slice(hd * GLA_HK, (hd + 1) * GLA_HK)
                vs = slice(hd * GLA_HV, (hd + 1) * GLA_HV)
                kcols = slice(K0 + hd * GLA_HK, K0 + (hd + 1) * GLA_HK)
                vcols = slice(V0 + hd * GLA_HV, V0 + (hd + 1) * GLA_HV)
                qcols = slice(Q0 + hd * GLA_HK, Q0 + (hd + 1) * GLA_HK)
                gcols = slice(G0 + hd * GLA_HV, G0 + (hd + 1) * GLA_HV)
                ov = o_ref[rows, vs].astype(F32)
                gv = z_ref[rows, gcols].astype(F32)
                dm = dmix_ref[rows, vs].astype(F32)
                sg = jax.nn.sigmoid(gv)
                rr = lax.rsqrt(jnp.mean(ov * ov, axis=-1, keepdims=True) + EPS)
                xhat = ov * rr
                don = dm * (gv * sg)
                dz_ref[rows, gcols] = (dm * (xhat * hg_ref[...]) * (sg * (1.0 + gv * (1.0 - sg)))).astype(BF16)
                dhg = dhg + jnp.sum(don * xhat, axis=0, keepdims=True)
                dxh = don * hg_ref[...]
                do = (rr * (dxh - xhat * jnp.mean(dxh * xhat, axis=-1, keepdims=True)) * scale).astype(BF16)
                kdec = z_ref[rows, kcols].astype(F32) * dec[:, ks]
                kdb = kdec.astype(BF16)
                v = z_ref[rows, vcols]
                q = z_ref[rows, qcols]
                st_prev = st_ref[ci, vs, :].astype(F32)
                st = st_prev * e[:, ks] + _dot_tn(v, kdb)
                dz_ref[rows, qcols] = _dot(do, st.astype(BF16)).astype(BF16)
                dst = dstate[vs, :] + _dot_tn(do, q)
                dstb = dst.astype(BF16)
                dkdec = _dot(v, dstb)
                dz_ref[rows, vcols] = _dot_nt(kdb, dstb).astype(BF16)
                dtot = jnp.sum(dst * st_prev, axis=0, keepdims=True) * e[:, ks]
                dstate[vs, :] = dst * e[:, ks]
                dz_ref[rows, kcols] = (dkdec * dec[:, ks]).astype(BF16)
                gk = dkdec * kdec
                ghi, gmid, _ = _split3(gk)
                dlac_parts.append(dtot + _dot(tri_strict, ghi) + _dot(tri_strict, gmid))
            dlac = jnp.concatenate(dlac_parts, axis=1)
            dpre = dlac * (1.0 / GATE_NORM) * (1.0 - jax.nn.sigmoid(pre))
            dpb = dpre.astype(BF16)
            dz_ref[rows, R0:R0 + GATE_PAD] = _dot_nt(dpb, gw_ref[...]).astype(BF16)
            dgw_ref[...] += _dot_tn(r, dpb)
            dgb_ref[...] += jnp.sum(dpre, axis=0, keepdims=True)
            dhg_ref[...] += dhg
            return carry

        lax.fori_loop(0, cpt, chunk, 0)

    rev = lambda i: (nt - 1 - i, 0)
    return pl.pallas_call(
        body, grid=(nt,),
        in_specs=[pl.BlockSpec((tm, GLA_DV), rev), pl.BlockSpec((tm, GLA_DV), rev), pl.BlockSpec((tm, GLA_IN_PAD), rev),
                  pl.BlockSpec((cpt, GLA_DV, GLA_HK), lambda i: (nt - 1 - i, 0, 0)),
                  pl.BlockSpec((GATE_PAD, GLA_DK), lambda i: (0, 0)), pl.BlockSpec((1, GLA_DK), lambda i: (0, 0)),
                  pl.BlockSpec((1, GLA_HV), lambda i: (0, 0))],
        out_specs=[pl.BlockSpec((tm, GLA_IN_PAD), rev), pl.BlockSpec((GATE_PAD, GLA_DK), lambda i: (0, 0)),
                   pl.BlockSpec((1, GLA_DK), lambda i: (0, 0)), pl.BlockSpec((1, GLA_HV), lambda i: (0, 0))],
        out_shape=[jax.ShapeDtypeStruct((t, GLA_IN_PAD), BF16), jax.ShapeDtypeStruct((GATE_PAD, GLA_DK), F32),
                   jax.ShapeDtypeStruct((1, GLA_DK), F32), jax.ShapeDtypeStruct((1, GLA_HV), F32)],
        scratch_shapes=[pltpu.VMEM((GLA_DV, GLA_HK), F32)],
        compiler_params=_cparams("arbitrary"), name="gla_seq_bwd")(dmix, o, z, states, gate_w, gate_b, head_g)


def _sum_halves(g, recv, c_idx, name):
    n, r, cdim = g.shape
    h = r // 2
    tr = _row_tile(h, 256, 8)
    nh = h // tr

    def body(c_ref, g_ref, r_ref, o_ref):
        o_ref[...] = (g_ref[...] + r_ref[...]).astype(BF16)

    return pl.pallas_call(
        body,
        grid_spec=pltpu.PrefetchScalarGridSpec(
            num_scalar_prefetch=1, grid=(n, nh),
            in_specs=[pl.BlockSpec((None, tr, cdim), lambda s, i, c: (s, c[0] * nh + i, 0)),
                      pl.BlockSpec((None, tr, cdim), lambda s, i, c: (s, i, 0))],
            out_specs=pl.BlockSpec((None, tr, cdim), lambda s, i, c: (s, i, 0))),
        out_shape=jax.ShapeDtypeStruct((n, h, cdim), BF16),
        compiler_params=_cparams("parallel", "parallel"), name=name)(c_idx, g, recv)


def _sum_slots(x, name):
    n, r, cdim = x.shape
    tr = _row_tile(r, 256, 8)

    def body(x_ref, o_ref):
        acc = x_ref[0].astype(F32)
        for j in range(1, n):
            acc = acc + x_ref[j].astype(F32)
        o_ref[...] = acc

    return pl.pallas_call(
        body, grid=(r // tr,),
        in_specs=[pl.BlockSpec((n, tr, cdim), lambda i: (0, i, 0))],
        out_specs=pl.BlockSpec((tr, cdim), lambda i: (i, 0)),
        out_shape=jax.ShapeDtypeStruct((r, cdim), F32),
        compiler_params=_cparams("parallel"), name=name)(x)


def _sum_own_and_slots(own, slots, chip_idx, name):
    n, r, cdim = own.shape
    tr = _row_tile(r, 256, 8)

    def body(s_ref, own_ref, a_ref, b_ref, c_ref, o_ref):
        o_ref[...] = (own_ref[...].astype(F32) + a_ref[...].astype(F32) + b_ref[...].astype(F32)
                      + c_ref[...].astype(F32))

    def slot(dd):
        return pl.BlockSpec((None, tr, cdim), lambda i, s: ((s[0] + dd) % n, i, 0))

    return pl.pallas_call(
        body,
        grid_spec=pltpu.PrefetchScalarGridSpec(
            num_scalar_prefetch=1, grid=(r // tr,), in_specs=[slot(0), slot(1), slot(2), slot(3)],
            out_specs=pl.BlockSpec((tr, cdim), lambda i, s: (i, 0))),
        out_shape=jax.ShapeDtypeStruct((r, cdim), F32),
        compiler_params=_cparams("parallel"), name=name)(chip_idx, own, slots, slots, slots)


def _add2(a, b, name):
    r, cdim = a.shape
    tr = _row_tile(r, 256, 8)

    def body(a_ref, b_ref, o_ref):
        o_ref[...] = a_ref[...] + b_ref[...]

    spec = pl.BlockSpec((tr, cdim), lambda i: (i, 0))
    return pl.pallas_call(body, grid=(r // tr,), in_specs=[spec, spec], out_specs=spec,
                          out_shape=jax.ShapeDtypeStruct((r, cdim), F32),
                          compiler_params=_cparams("parallel"), name=name)(a, b)


def _adamw(w, g, m, v, name):
    r, cdim = w.shape
    tr = _row_tile(r, 256, 8)

    def body(w_ref, g_ref, m_ref, v_ref, d_ref, mo_ref, vo_ref):
        gv = g_ref[...]
        mn = ADAM_B1 * m_ref[...] + (1.0 - ADAM_B1) * gv
        vn = ADAM_B2 * v_ref[...] + (1.0 - ADAM_B2) * (gv * gv)
        m_hat = mn / (1.0 - ADAM_B1 ** ADAM_STEP)
        v_hat = vn / (1.0 - ADAM_B2 ** ADAM_STEP)
        d_ref[...] = -ADAM_LR * (m_hat / (jnp.sqrt(v_hat) + ADAM_EPS) + ADAM_WD * w_ref[...])
        mo_ref[...] = mn
        vo_ref[...] = vn

    spec = pl.BlockSpec((tr, cdim), lambda i: (i, 0))
    shp = jax.ShapeDtypeStruct((r, cdim), F32)
    return pl.pallas_call(body, grid=(r // tr,), in_specs=[spec] * 4, out_specs=[spec] * 3,
                          out_shape=[shp] * 3, compiler_params=_cparams("parallel"), name=name)(w, g, m, v)


def _split_rows(a):
    return a.reshape(a.shape[0], 2, a.shape[1] // 2, a.shape[2])


def _place():
    x, y, c = lax.axis_index("x"), lax.axis_index("y"), lax.axis_index("c")
    chips = [(1 - x, y), (x, 1 - y), (1 - x, 1 - y)]
    return x, y, c, chips


def _remote(src, dst, send_sem, recv_sem, to):
    return pltpu.make_async_remote_copy(src_ref=src, dst_ref=dst, send_sem=send_sem, recv_sem=recv_sem,
                                        device_id=to, device_id_type=MESH)


def _plan_gather(src_refs, land_refs):
    x, y, c, chips = _place()
    me = 2 * x + y
    return [(src.at[c], land.at[me, c], (px, py, c), land.at[2 * px + py, c])
            for src, land in zip(src_refs, land_refs) for (px, py) in chips]


def _plan_scatter(src_refs, land_refs):
    x, y, c, chips = _place()
    me = 2 * x + y
    return [(src.at[2 * px + py], land.at[me], (px, py, c), land.at[2 * px + py])
            for src, land in zip(src_refs, land_refs) for (px, py) in chips]


def _hbm(a):
    return pltpu.HBM(a.shape, a.dtype)


def _start_copies(name, srcs, lands, plan, dep=None):
    ns, nl = len(srcs), len(lands)
    ncopy = 3 * ns
    nin = ns + nl + (0 if dep is None else 1)

    def body(*refs):
        send_sems, recv_sems, token = refs[nin], refs[nin + 1], refs[-1]
        for k, (src, dst, dev, _) in enumerate(plan(refs[:ns], refs[ns:ns + nl])):
            _remote(src, dst, send_sems.at[k], recv_sems.at[k], dev).start()
        token[...] = jnp.zeros_like(token)

    args = [pltpu.with_memory_space_constraint(a, pltpu.HBM) for a in list(srcs) + list(lands)]
    outs = pl.pallas_call(
        body, name=name,
        out_shape=(pltpu.SemaphoreType.DMA((ncopy,)), pltpu.SemaphoreType.DMA((ncopy,)),
                   *[_hbm(a) for a in list(srcs) + list(lands)], jax.ShapeDtypeStruct((8, 128), F32)),
        in_specs=[HBM_SPEC] * (ns + nl) + ([] if dep is None else [ANY_SPEC]),
        out_specs=(SEM_SPEC, SEM_SPEC, *([HBM_SPEC] * (ns + nl)), pl.BlockSpec(memory_space=pltpu.VMEM)),
        input_output_aliases={i: 2 + i for i in range(ns + nl)},
        compiler_params=pltpu.CompilerParams(has_side_effects=SIDE_EFFECT),
    )(*args, *([] if dep is None else [dep]))
    return outs[0], outs[1], list(outs[2:2 + ns]), list(outs[2 + ns:2 + ns + nl]), outs[-1]


def _wait_copies(name, started, plan, after):
    send_sems, recv_sems, srcs, lands, _ = started
    ns, nl = len(srcs), len(lands)

    def body(*refs):
        send_ref, recv_ref = refs[ns + nl], refs[ns + nl + 1]
        for k, (src, _, dev, mine) in enumerate(plan(refs[:ns], refs[ns:ns + nl])):
            copy = _remote(src, mine, send_ref.at[k], recv_ref.at[k], dev)
            copy.wait_send()
            copy.wait_recv()

    outs = pl.pallas_call(
        body, name=name, out_shape=tuple(_hbm(a) for a in srcs + lands),
        in_specs=[HBM_SPEC] * (ns + nl) + [SEM_SPEC, SEM_SPEC, ANY_SPEC], out_specs=tuple([HBM_SPEC] * (ns + nl)),
        input_output_aliases={i: i for i in range(ns + nl)},
        compiler_params=pltpu.CompilerParams(has_side_effects=SIDE_EFFECT),
    )(*srcs, *lands, send_sems, recv_sems, after)
    return list(outs[:ns]), list(outs[ns:])


def _share_with_sibling(name, srcs, lands):
    n = len(srcs)

    def body(*refs):
        src_refs, land_refs, out_refs = refs[:n], refs[n:2 * n], refs[2 * n:3 * n]
        send_sem, recv_sem = refs[3 * n:]
        x, y, c, chips = _place()
        me = 2 * x + y
        sib = (x, y, 1 - c)
        sends, recvs = [], []
        for k in range(n):
            sems = (send_sem.at[4 * k], recv_sem.at[4 * k])
            sends.append(_remote(src_refs[k], out_refs[k].at[me], *sems, sib))
            recvs.append(_remote(src_refs[k], out_refs[k].at[me], *sems, sib))
            for j, (px, py) in enumerate(chips):
                frm = 2 * px + py
                sems = (send_sem.at[4 * k + 1 + j], recv_sem.at[4 * k + 1 + j])
                sends.append(_remote(land_refs[k].at[frm, c], out_refs[k].at[frm, c], *sems, sib))
                recvs.append(_remote(land_refs[k].at[frm, c], out_refs[k].at[frm, 1 - c], *sems, sib))
        for cp in sends:
            cp.start()
        for cp in recvs:
            cp.wait_recv()
        for cp in sends:
            cp.wait_send()

    return pl.pallas_call(
        body, name=name, in_specs=[HBM_SPEC] * (2 * n), out_specs=[HBM_SPEC] * n,
        out_shape=[jax.ShapeDtypeStruct(a.shape, a.dtype) for a in lands],
        input_output_aliases={n + k: k for k in range(n)},
        scratch_shapes=[pltpu.SemaphoreType.DMA((4 * n,)), pltpu.SemaphoreType.DMA((4 * n,))],
    )(*srcs, *lands)


def _gather_weights(bigs, small):
    nb = len(bigs)
    shapes = [b.shape for b in bigs]
    bigs = [_split_rows(b) for b in bigs]

    def body(*refs):
        big_in, small_in = refs[:nb], refs[nb]
        big_out, small_out = refs[nb + 1:2 * nb + 1], refs[2 * nb + 1]
        ici_send, ici_recv, d2d_send, d2d_recv, own_send, own_recv, loc_sem = refs[2 * nb + 2:]
        x, y, c, chips = _place()
        me = 2 * x + y
        sib = (x, y, 1 - c)

        def half(ref, k, which):
            return ref.at[:, which]

        local = [pltpu.make_async_copy(small_in, small_out.at[me], loc_sem.at[0])]
        for cp in local:
            cp.start()
        sends = [_remote(big_in[k], big_out[k].at[me], own_send.at[k], own_recv.at[k], sib) for k in range(nb)]
        for j, (px, py) in enumerate(chips):
            for k in range(nb):
                sends.append(_remote(half(big_in[k], k, c), half(big_out[k].at[me], k, c),
                                     ici_send.at[k * 3 + j], ici_recv.at[k * 3 + j], (px, py, c)))
            sends.append(_remote(small_in, small_out.at[me], ici_send.at[nb * 3 + j], ici_recv.at[nb * 3 + j], (px, py, c)))
        for cp in sends:
            cp.start()
        passed = []
        for j, (px, py) in enumerate(chips):
            frm = 2 * px + py
            for k in range(nb):
                landed = half(big_out[k].at[frm], k, c)
                _remote(landed, landed, ici_send.at[k * 3 + j], ici_recv.at[k * 3 + j], (px, py, c)).wait_recv()
                fwd = _remote(landed, landed, d2d_send.at[k * 3 + j], d2d_recv.at[k * 3 + j], sib)
                fwd.start()
                passed.append(fwd)
            _remote(small_in, small_out.at[frm], ici_send.at[nb * 3 + j], ici_recv.at[nb * 3 + j], (px, py, c)).wait_recv()
        for j, (px, py) in enumerate(chips):
            frm = 2 * px + py
            for k in range(nb):
                theirs = half(big_out[k].at[frm], k, 1 - c)
                _remote(theirs, theirs, d2d_send.at[k * 3 + j], d2d_recv.at[k * 3 + j], sib).wait_recv()
        for k in range(nb):
            _remote(big_in[k], big_out[k].at[me], own_send.at[k], own_recv.at[k], sib).wait_recv()
        for cp in sends + passed:
            cp.wait_send()
        for cp in local:
            cp.wait()

    out_shape = [jax.ShapeDtypeStruct((N_CHIPS,) + b.shape, b.dtype) for b in bigs]
    out_shape.append(jax.ShapeDtypeStruct((N_CHIPS,) + small.shape, small.dtype))
    outs = pl.pallas_call(
        body, in_specs=[HBM_SPEC] * (nb + 1), out_specs=[HBM_SPEC] * (nb + 1), out_shape=out_shape,
        scratch_shapes=[pltpu.SemaphoreType.DMA((3 * nb + 3,)), pltpu.SemaphoreType.DMA((3 * nb + 3,)),
                        pltpu.SemaphoreType.DMA((3 * nb,)), pltpu.SemaphoreType.DMA((3 * nb,)),
                        pltpu.SemaphoreType.DMA((nb,)), pltpu.SemaphoreType.DMA((nb,)),
                        pltpu.SemaphoreType.DMA((1,))],
        name="gather_weights")(*bigs, small)
    return [o.reshape((N_CHIPS,) + s) for o, s in zip(outs[:-1], shapes)] + [outs[-1]]


def _exchange_halves(grads, small, name):
    ng = len(grads)
    grads = [_split_rows(g) for g in grads]
    extra = [] if small is None else [small]
    nall = ng + len(extra)

    def body(*refs):
        ins, outs = refs[:nall], refs[nall:2 * nall]
        send_sem, recv_sem = refs[2 * nall:]
        x, y, c, _ = _place()
        sib = (x, y, 1 - c)
        copies = []
        for k in range(nall):
            src = ins[k].at[:, 1 - c] if k < ng else ins[k]
            copies.append(_remote(src, outs[k], send_sem.at[k], recv_sem.at[k], sib))
        for cp in copies:
            cp.start()
        for cp in copies:
            cp.wait_recv()
        for cp in copies:
            cp.wait_send()

    out_shape = [jax.ShapeDtypeStruct((g.shape[0], g.shape[2], g.shape[3]), g.dtype) for g in grads]
    out_shape += [jax.ShapeDtypeStruct(s.shape, s.dtype) for s in extra]
    return pl.pallas_call(
        body, in_specs=[HBM_SPEC] * nall, out_specs=[HBM_SPEC] * nall, out_shape=out_shape,
        scratch_shapes=[pltpu.SemaphoreType.DMA((nall,)), pltpu.SemaphoreType.DMA((nall,))],
        name=name)(*grads, *extra)


def _scatter_to_chips(parts, small):
    npart = len(parts)

    def body(*refs):
        p_in, s_in = refs[:npart], refs[npart]
        p_out, s_out = refs[npart + 1:2 * npart + 1], refs[2 * npart + 1]
        send_sem, recv_sem, loc_sem = refs[2 * npart + 2:]
        x, y, c, chips = _place()
        me = 2 * x + y
        local = [pltpu.make_async_copy(s_in, s_out.at[me], loc_sem.at[0])]
        for cp in local:
            cp.start()
        sends, recvs = [], []
        for j, (px, py) in enumerate(chips):
            to = 2 * px + py
            for k in range(npart + 1):
                src = s_in if k == npart else p_in[k].at[to]
                out = s_out if k == npart else p_out[k]
                sems = (send_sem.at[k * 3 + j], recv_sem.at[k * 3 + j])
                sends.append(_remote(src, out.at[me], *sems, (px, py, c)))
                recvs.append(_remote(src, out.at[to], *sems, (px, py, c)))
        for cp in sends:
            cp.start()
        for cp in recvs:
            cp.wait_recv()
        for cp in sends:
            cp.wait_send()
        for cp in local:
            cp.wait()

    out_shape = [jax.ShapeDtypeStruct(p.shape, p.dtype) for p in parts]
    out_shape.append(jax.ShapeDtypeStruct((N_CHIPS,) + small.shape, small.dtype))
    return pl.pallas_call(
        body, in_specs=[HBM_SPEC] * (npart + 1), out_specs=[HBM_SPEC] * (npart + 1), out_shape=out_shape,
        scratch_shapes=[pltpu.SemaphoreType.DMA((3 * npart + 3,)), pltpu.SemaphoreType.DMA((3 * npart + 3,)),
                        pltpu.SemaphoreType.DMA((1,))],
        name="scatter_to_chips")(*parts, small)


def _join_halves(reduced, dests, out_shapes):
    nr = len(reduced)

    def body(*refs):
        r_in = refs[:nr]
        outs = refs[nr:nr + len(out_shapes)]
        send_sem, recv_sem, back_send, back_recv = refs[nr + len(out_shapes):]
        x, y, c, _ = _place()
        sib = (x, y, 1 - c)
        sends, backs = [], []
        for k in range(nr):
            oi, layer = dests[k]
            sends.append(_remote(r_in[k], outs[oi].at[layer, c], send_sem.at[k], recv_sem.at[k], sib))
        for cp in sends:
            cp.start()
        for k in range(nr):
            oi, layer = dests[k]
            theirs = outs[oi].at[layer, 1 - c]
            _remote(r_in[k], theirs, send_sem.at[k], recv_sem.at[k], sib).wait_recv()
            back = _remote(theirs, theirs, back_send.at[k], back_recv.at[k], sib)
            back.start()
            backs.append(back)
        for k in range(nr):
            oi, layer = dests[k]
            mine = outs[oi].at[layer, c]
            _remote(mine, mine, back_send.at[k], back_recv.at[k], sib).wait_recv()
        for cp in sends + backs:
            cp.wait_send()

    split = [(s[0], 2, s[1] // 2, s[2]) for s in out_shapes]
    outs = pl.pallas_call(
        body, in_specs=[HBM_SPEC] * nr, out_specs=[HBM_SPEC] * len(out_shapes),
        out_shape=[jax.ShapeDtypeStruct(s, F32) for s in split],
        scratch_shapes=[pltpu.SemaphoreType.DMA((nr,)), pltpu.SemaphoreType.DMA((nr,)),
                        pltpu.SemaphoreType.DMA((nr,)), pltpu.SemaphoreType.DMA((nr,))],
        name="join_halves")(*reduced)
    return [o.reshape(s) for o, s in zip(outs, out_shapes)]


def _pack(arrs):
    flat = jnp.concatenate([a.reshape(-1).astype(F32) for a in arrs])
    n = flat.shape[0]
    rows = -(-n // PACK_WIDTH)
    rows = -(-rows // 8) * 8
    return jnp.pad(flat, (0, rows * PACK_WIDTH - n)).reshape(rows, PACK_WIDTH)


def _unpack(buf, shapes):
    flat = buf.reshape(-1)
    out, off = [], 0
    for shp in shapes:
        n = 1
        for s in shp:
            n *= s
        out.append(flat[off:off + n].reshape(shp))
        off += n
    return out


def _unshard_cols(stacked):
    moved = jnp.moveaxis(stacked, 0, -2)
    return moved.reshape(moved.shape[:-2] + (moved.shape[-2] * moved.shape[-1],))


def _col_shard(full, s, width):
    return lax.dynamic_slice_in_dim(full, s * width, width, axis=full.ndim - 1)


def kernel(x, meta_tokens, mix_norm_g, ffn_norm_g, ffn_w1, ffn_w2, cp_w_in, cp_conv_w, cp_conv_b, cp_ln_g, cp_ln_b, cp_pool_w, cp_pool_scale, cp_w_out, gla_w_in, gla_gate_w2, gla_gate_b, gla_head_g, gla_w_out, final_norm_g, loss_target, m_meta_tokens, m_mix_norm_g, m_ffn_norm_g, m_ffn_w1, m_ffn_w2, m_cp_w_in, m_cp_conv_w, m_cp_conv_b, m_cp_ln_g, m_cp_ln_b, m_cp_pool_w, m_cp_pool_scale, m_cp_w_out, m_gla_w_in, m_gla_gate_w2, m_gla_gate_b, m_gla_head_g, m_gla_w_out, m_final_norm_g, v_meta_tokens, v_mix_norm_g, v_ffn_norm_g, v_ffn_w1, v_ffn_w2, v_cp_w_in, v_cp_conv_w, v_cp_conv_b, v_cp_ln_g, v_cp_ln_b, v_cp_pool_w, v_cp_pool_scale, v_cp_w_out, v_gla_w_in, v_gla_gate_w2, v_gla_gate_b, v_gla_head_g, v_gla_w_out, v_final_norm_g):
    d = D_MODEL
    chip = 2 * lax.axis_index("x") + lax.axis_index("y")
    core = lax.axis_index("c")
    seq = x.shape[1]
    t = seq + CHUNK

    big_w = [ffn_w1, ffn_w2, cp_w_in, cp_w_out, gla_w_in, gla_w_out]
    sharded_small = [meta_tokens, cp_conv_w, gla_gate_w2, gla_gate_b, gla_head_g]
    cpin_g, cpout_g, small_g = _gather_weights([cp_w_in.astype(BF16), cp_w_out.astype(BF16)], _pack(sharded_small))

    def halves(w):
        return w.astype(BF16).reshape(2, w.shape[0] // 2, w.shape[1])

    def start_gather(name, srcs, dep):
        lands = [lax.empty((N_CHIPS,) + s.shape, s.dtype) for s in srcs]
        return _start_copies(name, srcs, lands, _plan_gather, dep)

    def finish_gather(name, started, after):
        srcs, lands = _wait_copies(name + "_wait", started, _plan_gather, after)
        return [g.reshape(N_CHIPS, 2 * g.shape[2], g.shape[3]) for g in _share_with_sibling(name + "_share", srcs, lands)]

    ffn0_started = start_gather("gather_ffn0_start", [halves(ffn_w1[0]), halves(ffn_w2[0])], small_g)
    gla_started = start_gather("gather_gla_start", [halves(gla_w_in[0]), halves(gla_w_out[0])], ffn0_started[-1])
    ffn1_started = start_gather("gather_ffn1_start", [halves(ffn_w1[1]), halves(ffn_w2[1])], gla_started[-1])
    per_chip = [_unpack(small_g[j], [a.shape for a in sharded_small]) for j in range(N_CHIPS)]
    meta_f, conv_w_f, gate_w_f, gate_b_f, head_g_f = [
        jnp.concatenate([per_chip[j][i] for j in range(N_CHIPS)], axis=-1) for i in range(len(sharded_small))]
    conv_w_f, gate_w_f = conv_w_f[0], gate_w_f[0]
    w_cp_in = _unshard_cols(cpin_g[:, 0])
    w_cp_out = cpout_g.reshape(CONV_DIM + POOL_DIM, d)
    gate_w_pad = jnp.pad(gate_w_f, ((0, GATE_PAD - GATE_RANK), (0, 0))).astype(BF16)
    row = lambda a: a.reshape(1, -1)
    c_idx = core.reshape(1).astype(jnp.int32)
    chip_idx = chip.reshape(1).astype(jnp.int32)

    h0 = jnp.concatenate([jnp.zeros((PAD_ROWS, d), F32), meta_f, x[0]], axis=0)
    z0, u0 = _norm_matmul(h0, row(mix_norm_g[0]), w_cp_in, 512, "cp_in_proj", dep=ffn1_started[-1])
    c0, pm0, mix0 = _cp_seq_fwd(z0, conv_w_f, cp_conv_b, cp_ln_g, cp_ln_b, cp_pool_w[0], cp_pool_scale)
    h1 = _matmul_residual(mix0, w_cp_out, h0, "cp_out_proj")
    w1g0, w2g0 = finish_gather("gather_ffn0", ffn0_started, h1)
    h2, hp0, uf0 = _ffn_fwd(h1, row(ffn_norm_g[0]), w1g0, w2g0, "ffn0_fwd")
    glain_g, glaout_g = finish_gather("gather_gla", gla_started, h2)
    w_gla_in = jnp.pad(_unshard_cols(glain_g), ((0, 0), (0, GLA_IN_PAD - GLA_IN)))
    w_gla_out = glaout_g.reshape(GLA_DV, d)
    z1, u2 = _norm_matmul(h2, row(mix_norm_g[1]), w_gla_in, 640, "gla_in_proj")
    o1, mix1, states = _gla_seq_fwd(z1, gate_w_pad, gate_b_f, head_g_f)
    h3 = _matmul_residual(mix1, w_gla_out, h2, "gla_out_proj")
    w1g1, w2g1 = finish_gather("gather_ffn1", ffn1_started, h3)
    h4, hp1, uf1 = _ffn_fwd(h3, row(ffn_norm_g[1]), w1g1, w2g1, "ffn1_fwd")

    def start_reduce(name, grads):
        recv = _exchange_halves(grads, None, name + "_exchange")
        parts = [_sum_halves(g, r, c_idx, "%s_chip_sum_%d" % (name, k)) for k, (g, r) in enumerate(zip(grads, recv))]
        lands = [lax.empty(p.shape, p.dtype) for p in parts]
        return _start_copies(name + "_scatter_start", parts, lands, _plan_scatter)

    def finish_reduce(name, started, after):
        parts, lands = _wait_copies(name + "_scatter_wait", started, _plan_scatter, after)
        return [_sum_own_and_slots(p, s, chip_idx, "%s_slot_sum_%d" % (name, k)) for k, (p, s) in enumerate(zip(parts, lands))]

    target = jnp.pad(loss_target[0], ((CHUNK, 0), (0, 0)))
    dh4, d_final_g, loss_part = _loss_bwd(h4, row(final_norm_g), target)

    dh3, dhp1, d_ffn_g1 = _ffn_bwd_data(dh4, h3, row(ffn_norm_g[1]), hp1, w1g1, w2g1, "ffn1_bwd")
    dw1_1 = _wgrad(uf1, dhp1, N_CHIPS, d, d, False, True, False, "ffn1_dw1")
    dw2_1 = _wgrad(hp1, dh4, N_CHIPS, d, d, True, False, True, "ffn1_dw2")
    ffn1_reduce = start_reduce("ffn1", [dw1_1, dw2_1])

    dmix1 = _dgrad(dh3, w_gla_out, "gla_out_dgrad", dep=ffn1_reduce[-1])
    dw_gla_out = _wgrad(mix1, dh3, 1, GLA_DV, d, False, False, False, "gla_out_dw")
    dz1, d_gate_w, d_gate_b, d_head_g = _gla_seq_bwd(dmix1, o1, z1, states, gate_w_pad, gate_b_f, head_g_f)
    dh2, d_mix_g1 = _dgrad_norm_bwd(dz1, w_gla_in, h2, row(mix_norm_g[1]), dh3, 640, "gla_in_dgrad")
    dw_gla_in = _wgrad(u2, dz1, GLA_IN_PAD // 640, d, 640, False, True, False, "gla_in_dw")
    gla_in_cols = jnp.moveaxis(dw_gla_in, 0, 1).reshape(d, GLA_IN_PAD)[:, :GLA_IN]
    gla_reduce = start_reduce("gla", [jnp.moveaxis(gla_in_cols.reshape(d, N_CHIPS, GLA_IN // N_CHIPS), 1, 0),
                                      dw_gla_out.reshape(N_CHIPS, -1, d)])

    dh1, dhp0, d_ffn_g0 = _ffn_bwd_data(dh2, h1, row(ffn_norm_g[0]), hp0, w1g0, w2g0, "ffn0_bwd", dep=gla_reduce[-1])
    dw1_0 = _wgrad(uf0, dhp0, N_CHIPS, d, d, False, True, False, "ffn0_dw1")
    dw2_0 = _wgrad(hp0, dh2, N_CHIPS, d, d, True, False, True, "ffn0_dw2")
    ffn0_reduce = start_reduce("ffn0", [dw1_0, dw2_0])

    dmix0 = _dgrad(dh1, w_cp_out, "cp_out_dgrad", dep=ffn0_reduce[-1])
    dw_cp_out = _wgrad(mix0, dh1, 1, CONV_DIM + POOL_DIM, d, False, False, False, "cp_out_dw")
    dz0, d_conv_w, d_cp_vec, d_pool_w = _cp_seq_bwd(dmix0, z0, c0, pm0, conv_w_f, cp_ln_g, cp_ln_b, cp_pool_w[0], cp_pool_scale)
    dh0, d_mix_g0 = _dgrad_norm_bwd(dz0, w_cp_in, h0, row(mix_norm_g[0]), dh1, 512, "cp_in_dgrad")
    dw_cp_in = _wgrad(u0, dz0, N_CHIPS, d, CP_IN // N_CHIPS, False, True, False, "cp_in_dw")

    grad_x = dh0[CHUNK:][None]

    cp_grads = [dw_cp_in, dw_cp_out.reshape(N_CHIPS, -1, d)]
    small_full = [dh0[PAD_ROWS:CHUNK], jnp.concatenate([d_mix_g0, d_mix_g1], axis=0),
                  jnp.concatenate([d_ffn_g0, d_ffn_g1], axis=0), d_conv_w[:CONV_WIDTH][None],
                  d_cp_vec[0:1], d_cp_vec[1:2], d_cp_vec[2:3], d_pool_w[None], d_cp_vec[3:4],
                  d_gate_w[:GATE_RANK][None], d_gate_b, d_head_g, d_final_g[0], loss_part[0, 0:1]]
    small_mine = _pack(small_full)
    recv = _exchange_halves(cp_grads, small_mine, "cp_exchange")
    chip_sums = [_sum_halves(g, r, c_idx, "cp_chip_sum_%d" % k) for k, (g, r) in enumerate(zip(cp_grads, recv[:-1]))]
    small_chip = _add2(small_mine, recv[-1], "chip_sum_small")
    slots = _scatter_to_chips(chip_sums, small_chip)
    red_cp = [_sum_own_and_slots(a, s, chip_idx, "cp_slot_sum_%d" % k) for k, (a, s) in enumerate(zip(chip_sums, slots[:-1]))]
    small_red = _sum_slots(slots[-1], "slot_sum_small")
    red_ffn1 = finish_reduce("ffn1", ffn1_reduce, small_red)
    red_gla = finish_reduce("gla", gla_reduce, small_red)
    red_ffn0 = finish_reduce("ffn0", ffn0_reduce, small_red)
    reduced = [red_ffn0[0], red_ffn1[0], red_ffn0[1], red_ffn1[1], red_cp[0], red_cp[1], red_gla[0], red_gla[1]]
    dests = [(0, 0), (0, 1), (1, 0), (1, 1), (2, 0), (3, 0), (4, 0), (5, 0)]
    big_grads = _join_halves(reduced, dests, [w.shape for w in big_w])

    big_m = [m_ffn_w1, m_ffn_w2, m_cp_w_in, m_cp_w_out, m_gla_w_in, m_gla_w_out]
    big_v = [v_ffn_w1, v_ffn_w2, v_cp_w_in, v_cp_w_out, v_gla_w_in, v_gla_w_out]
    big_out = []
    for k, (w, g, m, v) in enumerate(zip(big_w, big_grads, big_m, big_v)):
        two_d = lambda a: a.reshape(-1, a.shape[-1])
        outs = _adamw(two_d(w), two_d(g), two_d(m), two_d(v), "adamw_%d" % k)
        big_out.append([g] + [o.reshape(w.shape) for o in outs])

    (g_meta, g_mix, g_ffn, g_conv_w, g_conv_b, g_ln_g, g_ln_b, g_pool_w, g_pool_scale, g_gate_w, g_gate_b, g_head,
     g_final, loss_sum) = _unpack(small_red, [a.shape for a in small_full])
    g_meta = _col_shard(g_meta, chip, meta_tokens.shape[-1])
    g_conv_w = _col_shard(g_conv_w, chip, cp_conv_w.shape[-1])
    g_gate_w = _col_shard(g_gate_w, chip, gla_gate_w2.shape[-1])
    g_gate_b = _col_shard(g_gate_b, chip, gla_gate_b.shape[-1])
    g_head = _col_shard(g_head, chip, gla_head_g.shape[-1])
    small_w = [meta_tokens, mix_norm_g, ffn_norm_g, cp_conv_w, cp_conv_b, cp_ln_g, cp_ln_b, cp_pool_w, cp_pool_scale,
               gla_gate_w2, gla_gate_b, gla_head_g, final_norm_g]
    small_m = [m_meta_tokens, m_mix_norm_g, m_ffn_norm_g, m_cp_conv_w, m_cp_conv_b, m_cp_ln_g, m_cp_ln_b, m_cp_pool_w,
               m_cp_pool_scale, m_gla_gate_w2, m_gla_gate_b, m_gla_head_g, m_final_norm_g]
    small_v = [v_meta_tokens, v_mix_norm_g, v_ffn_norm_g, v_cp_conv_w, v_cp_conv_b, v_cp_ln_g, v_cp_ln_b, v_cp_pool_w,
               v_cp_pool_scale, v_gla_gate_w2, v_gla_gate_b, v_gla_head_g, v_final_norm_g]
    small_g = [g_meta, g_mix, g_ffn, g_conv_w, g_conv_b, g_ln_g, g_ln_b, g_pool_w, g_pool_scale, g_gate_w, g_gate_b,
               g_head, g_final]
    shapes = [w.shape for w in small_w]
    small_g = [g.reshape(s) for g, s in zip(small_g, shapes)]
    s_delta, s_m, s_v = _adamw(_pack(small_w), _pack(small_g), _pack(small_m), _pack(small_v), "adamw_small")
    s_delta, s_m, s_v = _unpack(s_delta, shapes), _unpack(s_m, shapes), _unpack(s_v, shapes)

    order = ["meta", "mix", "ffn", "w1", "w2", "cp_in", "conv_w", "conv_b", "ln_g", "ln_b", "pool_w", "pool_scale",
             "cp_out", "gla_in", "gate_w", "gate_b", "head", "gla_out", "final"]
    small_names = ["meta", "mix", "ffn", "conv_w", "conv_b", "ln_g", "ln_b", "pool_w", "pool_scale", "gate_w", "gate_b",
                   "head", "final"]
    big_names = ["w1", "w2", "cp_in", "cp_out", "gla_in", "gla_out"]
    table = {n: (small_g[i], s_delta[i], s_m[i], s_v[i]) for i, n in enumerate(small_names)}
    table.update({n: tuple(big_out[i]) for i, n in enumerate(big_names)})
    loss = loss_sum.reshape(())
    return (loss, grad_x, *[table[n][0] for n in order], *[table[n][1] for n in order],
            *[table[n][2] for n in order], *[table[n][3] for n in order])
```

```python
import functools

import jax
import jax.numpy as jnp
from jax import lax
from jax.experimental import pallas as pl
from jax.experimental.pallas import tpu as pltpu

F32 = jnp.float32
BF16 = jnp.bfloat16

D_MODEL = 1024
N_META = 16
CHUNK = 64
PAD_ROWS = CHUNK - N_META
EPS = 1e-5
CONV_DIM = 512
CONV_WIDTH = 31
CONV_HALO = 32
POOL_DIM = 512
POOL_WINDOWS = (2, 4, 8, 16)
POOL_GROUP = 128
POOL_HALO = 16
CP_IN = 2 * CONV_DIM + POOL_DIM
GLA_HEADS = 4
GLA_DK = 512
GLA_DV = 1024
GLA_HK = GLA_DK // GLA_HEADS
GLA_HV = GLA_DV // GLA_HEADS
GATE_RANK = 16
GATE_PAD = 128
GATE_NORM = 16.0
GLA_IN = 2 * GLA_DK + 2 * GLA_DV + GATE_RANK
GLA_IN_PAD = 2 * GLA_DK + 2 * GLA_DV + GATE_PAD
N_CHIPS = 4
ADAM_LR = 0.001
ADAM_B1 = 0.9
ADAM_B2 = 0.999
ADAM_EPS = 1e-08
ADAM_WD = 0.01
ADAM_STEP = 10

VMEM_LIMIT_BYTES = 56 * 1024 * 1024
ROW_TILE_TARGET = 832
PACK_WIDTH = 1024
MESH = pl.DeviceIdType.MESH
HBM_SPEC = pl.BlockSpec(memory_space=pltpu.HBM)
ANY_SPEC = pl.BlockSpec(memory_space=pl.ANY)
SEM_SPEC = pl.BlockSpec(memory_space=pltpu.SEMAPHORE)
SIDE_EFFECT = pltpu.SideEffectType.DATAFLOW_SIDE_EFFECTING


def _cparams(*sem):
    return pltpu.CompilerParams(dimension_semantics=sem, vmem_limit_bytes=VMEM_LIMIT_BYTES)


def _row_tile(t, target, mult):
    best = mult
    for cand in range(mult, min(t, target) + 1, mult):
        if t % cand == 0:
            best = cand
    assert t % best == 0, (t, best)
    return best


def _rms(h, g):
    return h * lax.rsqrt(jnp.mean(h * h, axis=-1, keepdims=True) + EPS) * g


def _rms_bwd(h, g, du):
    r = lax.rsqrt(jnp.mean(h * h, axis=-1, keepdims=True) + EPS)
    xhat = h * r
    dxh = du * g
    dh = r * (dxh - xhat * jnp.mean(dxh * xhat, axis=-1, keepdims=True))
    return dh, du * xhat


def _valid_rows(i, tm):
    row = i * tm + lax.broadcasted_iota(jnp.int32, (tm, 1), 0)
    return row >= PAD_ROWS


def _dot(a, b):
    return jnp.dot(a, b, preferred_element_type=F32)


def _dot_nt(a, b):
    return lax.dot_general(a, b, (((1,), (1,)), ((), ())), preferred_element_type=F32)


def _dot_tn(a, b):
    return lax.dot_general(a, b, (((0,), (0,)), ((), ())), preferred_element_type=F32)


def _accumulate(ref, val, first):
    @pl.when(first)
    def _():
        ref[...] = val

    @pl.when(jnp.logical_not(first))
    def _():
        ref[...] += val


def _call_after(dep, body, n_in, in_specs, args, **kw):
    if dep is None:
        return pl.pallas_call(body, in_specs=in_specs, **kw)(*args)

    def with_dep(*refs):
        body(*refs[:n_in], *refs[n_in + 1:])

    return pl.pallas_call(with_dep, in_specs=list(in_specs) + [ANY_SPEC], **kw)(*args, dep)


def _norm_matmul(h, g, w, nc, name, dep=None):
    t, d = h.shape
    n = w.shape[1]
    tm = _row_tile(t, ROW_TILE_TARGET, CHUNK)

    def body(h_ref, g_ref, w_ref, z_ref, u_ref):
        u = _rms(h_ref[...], g_ref[...]).astype(BF16)
        u_ref[...] = u
        for n0 in range(0, n, nc):
            z_ref[:, n0:n0 + nc] = _dot(u, w_ref[:, n0:n0 + nc]).astype(BF16)

    return _call_after(
        dep, body, 3,
        [pl.BlockSpec((tm, d), lambda i: (i, 0)), pl.BlockSpec((1, d), lambda i: (0, 0)),
         pl.BlockSpec((d, n), lambda i: (0, 0))], (h, g, w), grid=(t // tm,),
        out_specs=[pl.BlockSpec((tm, n), lambda i: (i, 0)), pl.BlockSpec((tm, d), lambda i: (i, 0))],
        out_shape=[jax.ShapeDtypeStruct((t, n), BF16), jax.ShapeDtypeStruct((t, d), BF16)],
        compiler_params=_cparams("parallel"), name=name)


def _matmul_residual(a, w, h, name):
    t, k = a.shape
    d = w.shape[1]
    tm = _row_tile(t, ROW_TILE_TARGET, CHUNK)

    def body(a_ref, w_ref, h_ref, o_ref):
        o_ref[...] = h_ref[...] + _dot(a_ref[...], w_ref[...])

    return pl.pallas_call(
        body, grid=(t // tm,),
        in_specs=[pl.BlockSpec((tm, k), lambda i: (i, 0)), pl.BlockSpec((k, d), lambda i: (0, 0)),
                  pl.BlockSpec((tm, d), lambda i: (i, 0))],
        out_specs=pl.BlockSpec((tm, d), lambda i: (i, 0)),
        out_shape=jax.ShapeDtypeStruct((t, d), F32),
        compiler_params=_cparams("parallel"), name=name)(a, w, h)


def _ffn_fwd(h, g, w1g, w2g, name):
    t, d = h.shape
    ns, ffs = w1g.shape[0], w1g.shape[2]
    tm = _row_tile(t, ROW_TILE_TARGET, CHUNK)

    def body(h_ref, g_ref, w1_ref, w2_ref, ho_ref, hp_ref, u_ref, acc_ref):
        s = pl.program_id(1)

        @pl.when(s == 0)
        def _():
            u_ref[...] = _rms(h_ref[...], g_ref[...]).astype(BF16)

        hp = _dot(u_ref[...], w1_ref[...])
        hp_ref[...] = hp.astype(BF16)
        a = jnp.maximum(hp, 0.0)
        _accumulate(acc_ref, _dot((a * a).astype(BF16), w2_ref[...]), s == 0)

        @pl.when(s == ns - 1)
        def _():
            ho_ref[...] = h_ref[...] + acc_ref[...]

    return pl.pallas_call(
        body, grid=(t // tm, ns),
        in_specs=[pl.BlockSpec((tm, d), lambda i, s: (i, 0)), pl.BlockSpec((1, d), lambda i, s: (0, 0)),
                  pl.BlockSpec((None, d, ffs), lambda i, s: (s, 0, 0)),
                  pl.BlockSpec((None, ffs, d), lambda i, s: (s, 0, 0))],
        out_specs=[pl.BlockSpec((tm, d), lambda i, s: (i, 0)), pl.BlockSpec((tm, ffs), lambda i, s: (i, s)),
                   pl.BlockSpec((tm, d), lambda i, s: (i, 0))],
        out_shape=[jax.ShapeDtypeStruct((t, d), F32), jax.ShapeDtypeStruct((t, ns * ffs), BF16),
                   jax.ShapeDtypeStruct((t, d), BF16)],
        scratch_shapes=[pltpu.VMEM((tm, d), F32)],
        compiler_params=_cparams("parallel", "arbitrary"), name=name)(h, g, w1g, w2g)


def _ffn_bwd_data(dh, h, g, hp, w1g, w2g, name, dep=None):
    t, d = h.shape
    ns, ffs = w1g.shape[0], w1g.shape[2]
    tm = _row_tile(t, ROW_TILE_TARGET, CHUNK)

    def body(dh_ref, h_ref, g_ref, hp_ref, w1_ref, w2_ref, dhi_ref, dhp_ref, dg_ref, acc_ref):
        i, s = pl.program_id(0), pl.program_id(1)
        da = _dot_nt(dh_ref[...].astype(BF16), w2_ref[...])
        dhp = (da * (2.0 * jnp.maximum(hp_ref[...].astype(F32), 0.0))).astype(BF16)
        dhp_ref[...] = dhp
        _accumulate(acc_ref, _dot_nt(dhp, w1_ref[...]), s == 0)

        @pl.when(s == ns - 1)
        def _():
            dhn, dgr = _rms_bwd(h_ref[...], g_ref[...], acc_ref[...])
            dhi_ref[...] = jnp.where(_valid_rows(i, tm), dh_ref[...] + dhn, 0.0)
            _accumulate(dg_ref, jnp.sum(dgr, axis=0, keepdims=True), i == 0)

    return _call_after(
        dep, body, 6,
        [pl.BlockSpec((tm, d), lambda i, s: (i, 0)), pl.BlockSpec((tm, d), lambda i, s: (i, 0)),
         pl.BlockSpec((1, d), lambda i, s: (0, 0)), pl.BlockSpec((tm, ffs), lambda i, s: (i, s)),
         pl.BlockSpec((None, d, ffs), lambda i, s: (s, 0, 0)),
         pl.BlockSpec((None, ffs, d), lambda i, s: (s, 0, 0))], (dh, h, g, hp, w1g, w2g), grid=(t // tm, ns),
        out_specs=[pl.BlockSpec((tm, d), lambda i, s: (i, 0)), pl.BlockSpec((tm, ffs), lambda i, s: (i, s)),
                   pl.BlockSpec((1, d), lambda i, s: (0, 0))],
        out_shape=[jax.ShapeDtypeStruct((t, d), F32), jax.ShapeDtypeStruct((t, ns * ffs), BF16),
                   jax.ShapeDtypeStruct((1, d), F32)],
        scratch_shapes=[pltpu.VMEM((tm, d), F32)],
        compiler_params=_cparams("arbitrary", "arbitrary"), name=name)


WGRAD_ROWS = 1024


def _wgrad(x, dy, nb, xc, yc, x_by_block, dy_by_block, relu2, name, dep=None):
    t = x.shape[0]
    tk = _row_tile(t - CHUNK, WGRAD_ROWS, CHUNK)

    def prep(xv):
        if relu2:
            xv = jnp.maximum(xv.astype(F32), 0.0)
            xv = xv * xv
        return xv.astype(BF16)

    def body(xh_ref, dyh_ref, x_ref, dy_ref, o_ref):
        k = pl.program_id(1)
        p = _dot_tn(prep(x_ref[...]), dy_ref[...].astype(BF16))

        @pl.when(k == 0)
        def _():
            o_ref[...] = p + _dot_tn(prep(xh_ref[...]), dyh_ref[...].astype(BF16))

        @pl.when(k > 0)
        def _():
            o_ref[...] += p

    def head(width, by_block):
        return pl.BlockSpec((CHUNK, width), (lambda b, k: (0, b)) if by_block else (lambda b, k: (0, 0)))

    def rest(width, by_block):
        def index(b, k):
            return pl.multiple_of(CHUNK + k * tk, CHUNK), (pl.multiple_of(b * width, 128) if by_block else 0)
        return pl.BlockSpec((pl.Element(tk), pl.Element(width)), index)

    return _call_after(
        dep, body, 4,
        [head(xc, x_by_block), head(yc, dy_by_block), rest(xc, x_by_block), rest(yc, dy_by_block)], (x, dy, x, dy),
        grid=(nb, (t - CHUNK) // tk),
        out_specs=pl.BlockSpec((None, xc, yc), lambda b, k: (b, 0, 0)),
        out_shape=jax.ShapeDtypeStruct((nb, xc, yc), F32),
        compiler_params=_cparams("parallel", "arbitrary"), name=name)


def _dgrad(dh, w, name, dep=None):
    t, d = dh.shape
    k = w.shape[0]
    tm = _row_tile(t, ROW_TILE_TARGET, CHUNK)

    def body(dh_ref, w_ref, o_ref):
        o_ref[...] = _dot_nt(dh_ref[...].astype(BF16), w_ref[...]).astype(BF16)

    return _call_after(
        dep, body, 2,
        [pl.BlockSpec((tm, d), lambda i: (i, 0)), pl.BlockSpec((k, d), lambda i: (0, 0))], (dh, w), grid=(t // tm,),
        out_specs=pl.BlockSpec((tm, k), lambda i: (i, 0)),
        out_shape=jax.ShapeDtypeStruct((t, k), BF16),
        compiler_params=_cparams("parallel"), name=name)


def _dgrad_norm_bwd(dz, w, h, g, dh, nc, name):
    t, d = h.shape
    n = w.shape[1]
    tm = _row_tile(t, ROW_TILE_TARGET // 2, 16)

    def body(dz_ref, w_ref, h_ref, g_ref, dh_ref, dhi_ref, dg_ref):
        i = pl.program_id(0)
        du = jnp.zeros((tm, d), F32)
        for n0 in range(0, n, nc):
            du = du + _dot_nt(dz_ref[:, n0:n0 + nc], w_ref[:, n0:n0 + nc])
        dhn, dgr = _rms_bwd(h_ref[...], g_ref[...], du)
        dhi_ref[...] = jnp.where(_valid_rows(i, tm), dh_ref[...] + dhn, 0.0)
        _accumulate(dg_ref, jnp.sum(dgr, axis=0, keepdims=True), i == 0)

    return pl.pallas_call(
        body, grid=(t // tm,),
        in_specs=[pl.BlockSpec((tm, n), lambda i: (i, 0)), pl.BlockSpec((d, n), lambda i: (0, 0)),
                  pl.BlockSpec((tm, d), lambda i: (i, 0)), pl.BlockSpec((1, d), lambda i: (0, 0)),
                  pl.BlockSpec((tm, d), lambda i: (i, 0))],
        out_specs=[pl.BlockSpec((tm, d), lambda i: (i, 0)), pl.BlockSpec((1, d), lambda i: (0, 0))],
        out_shape=[jax.ShapeDtypeStruct((t, d), F32), jax.ShapeDtypeStruct((1, d), F32)],
        compiler_params=_cparams("arbitrary"), name=name)(dz, w, h, g, dh)


def _loss_bwd(h, g, target):
    t, d = h.shape
    tm = _row_tile(t, ROW_TILE_TARGET, CHUNK)

    def body(h_ref, g_ref, t_ref, dh_ref, dg_ref, loss_ref):
        i = pl.program_id(0)
        row = i * tm + lax.broadcasted_iota(jnp.int32, (tm, 1), 0)
        keep = row >= CHUNK
        hv, gv = h_ref[...], g_ref[...]
        err = jnp.where(keep, _rms(hv, gv) - t_ref[...], 0.0)
        part = 0.5 * jnp.sum(jnp.mean(err * err, axis=-1, keepdims=True), axis=0, keepdims=True)
        dhn, dgr = _rms_bwd(hv, gv, err * (1.0 / d))
        dh_ref[...] = dhn
        _accumulate(dg_ref, jnp.sum(dgr, axis=0, keepdims=True), i == 0)
        _accumulate(loss_ref, jnp.broadcast_to(part, (8, 128)), i == 0)

    return pl.pallas_call(
        body, grid=(t // tm,),
        in_specs=[pl.BlockSpec((tm, d), lambda i: (i, 0)), pl.BlockSpec((1, d), lambda i: (0, 0)),
                  pl.BlockSpec((tm, d), lambda i: (i, 0))],
        out_specs=[pl.BlockSpec((tm, d), lambda i: (i, 0)), pl.BlockSpec((1, d), lambda i: (0, 0)),
                   pl.BlockSpec((8, 128), lambda i: (0, 0))],
        out_shape=[jax.ShapeDtypeStruct((t, d), F32), jax.ShapeDtypeStruct((1, d), F32),
                   jax.ShapeDtypeStruct((8, 128), F32)],
        compiler_params=_cparams("arbitrary"), name="loss_bwd")(h, g, target)


CONV_BLOCK = 32


def _silu(x):
    return x * jax.nn.sigmoid(x)


def _cp_seq_fwd(z, conv_w, conv_b, ln_g, ln_b, pool_w, pool_scale):
    t = z.shape[0]
    tm = _row_tile(t, ROW_TILE_TARGET, CHUNK)

    def body(z_ref, cw_ref, cb_ref, lg_ref, lb_ref, pw_ref, ps_ref, c_ref, pm_ref, mix_ref, gbuf, pbuf):
        i = pl.program_id(0)

        @pl.when(i == 0)
        def _():
            gbuf[0:CONV_HALO, :] = jnp.zeros((CONV_HALO, CONV_DIM), F32)
            pbuf[0:POOL_HALO, :] = jnp.zeros((POOL_HALO, POOL_DIM), F32)

        @pl.when(i > 0)
        def _():
            gbuf[0:CONV_HALO, :] = gbuf[tm:tm + CONV_HALO, :]
            pbuf[0:POOL_HALO, :] = pbuf[tm:tm + POOL_HALO, :]

        av = z_ref[:, 0:CONV_DIM].astype(F32)
        ag = z_ref[:, CONV_DIM:2 * CONV_DIM].astype(F32)
        gbuf[CONV_HALO:CONV_HALO + tm, :] = av * jax.nn.sigmoid(ag)
        pbuf[POOL_HALO:POOL_HALO + tm, :] = z_ref[:, 2 * CONV_DIM:CP_IN].astype(F32)

        def conv_block(rb, carry):
            base = pl.multiple_of(rb * CONV_BLOCK, CONV_BLOCK)
            win = gbuf[pl.ds(base, CONV_BLOCK + CONV_HALO), :]
            acc = jnp.zeros((CONV_BLOCK, CONV_DIM), F32)
            for k in range(CONV_WIDTH):
                off = CONV_HALO - (CONV_WIDTH - 1) + k
                acc = acc + cw_ref[k:k + 1, :] * win[off:off + CONV_BLOCK, :]
            c_ref[pl.ds(base, CONV_BLOCK), :] = acc + cb_ref[...]
            return carry

        lax.fori_loop(0, tm // CONV_BLOCK, conv_block, 0)

        c = c_ref[...]
        mu = jnp.mean(c, axis=-1, keepdims=True)
        xc = c - mu
        ln = xc * lax.rsqrt(jnp.mean(xc * xc, axis=-1, keepdims=True) + EPS) * lg_ref[...] + lb_ref[...]
        row = i * tm + lax.broadcasted_iota(jnp.int32, (tm, 1), 0)
        mix_ref[:, 0:CONV_DIM] = jnp.where(row >= PAD_ROWS, _silu(ln), 0.0).astype(BF16)

        tpos = (row - PAD_ROWS + 1).astype(F32)
        for gi, wdw in enumerate(POOL_WINDOWS):
            lo = POOL_GROUP * gi
            cur = pbuf[POOL_HALO:POOL_HALO + tm, lo:lo + POOL_GROUP]
            sacc = cur
            for j in range(1, wdw):
                sacc = sacc + pbuf[POOL_HALO - j:POOL_HALO - j + tm, lo:lo + POOL_GROUP]
            pm = (sacc / jnp.clip(tpos, 1.0, float(wdw)) - cur).astype(BF16)
            pm_ref[:, lo:lo + POOL_GROUP] = pm
            pg = _dot(pm, pw_ref[gi].astype(BF16))
            mix_ref[:, CONV_DIM + lo:CONV_DIM + lo + POOL_GROUP] = (pg * ps_ref[:, lo:lo + POOL_GROUP]).astype(BF16)

    vec = pl.BlockSpec((1, CONV_DIM), lambda i: (0, 0))
    return pl.pallas_call(
        body, grid=(t // tm,),
        in_specs=[pl.BlockSpec((tm, CP_IN), lambda i: (i, 0)),
                  pl.BlockSpec((CONV_WIDTH, CONV_DIM), lambda i: (0, 0)), vec, vec, vec,
                  pl.BlockSpec((len(POOL_WINDOWS), POOL_GROUP, POOL_GROUP), lambda i: (0, 0, 0)), vec],
        out_specs=[pl.BlockSpec((tm, CONV_DIM), lambda i: (i, 0)), pl.BlockSpec((tm, POOL_DIM), lambda i: (i, 0)),
                   pl.BlockSpec((tm, CONV_DIM + POOL_DIM), lambda i: (i, 0))],
        out_shape=[jax.ShapeDtypeStruct((t, CONV_DIM), F32), jax.ShapeDtypeStruct((t, POOL_DIM), BF16),
                   jax.ShapeDtypeStruct((t, CONV_DIM + POOL_DIM), BF16)],
        scratch_shapes=[pltpu.VMEM((tm + CONV_HALO, CONV_DIM), F32), pltpu.VMEM((tm + POOL_HALO, POOL_DIM), F32)],
        compiler_params=_cparams("arbitrary"), name="cp_seq_fwd")(z, conv_w, conv_b, ln_g, ln_b, pool_w, pool_scale)


def _cp_seq_bwd(dmix, z, c, pm, conv_w, ln_g, ln_b, pool_w, pool_scale, dep=None):
    t = z.shape[0]
    tm = _row_tile(t, ROW_TILE_TARGET, CHUNK)
    nt = t // tm

    def body(dmix_ref, z_ref, c_ref, pm_ref, cw_ref, lg_ref, lb_ref, pw_ref, ps_ref,
             dz_ref, dcw_ref, dvec_ref, dpw_ref, dcbuf, qbuf, glu_buf, dwacc):
        i = pl.program_id(0)
        tile = nt - 1 - i

        @pl.when(i == 0)
        def _():
            dcbuf[tm:tm + CONV_HALO, :] = jnp.zeros((CONV_HALO, CONV_DIM), F32)
            qbuf[tm:tm + POOL_HALO, :] = jnp.zeros((POOL_HALO, POOL_DIM), F32)
            dcw_ref[...] = jnp.zeros_like(dcw_ref)
            dwacc[...] = jnp.zeros_like(dwacc)
            dvec_ref[...] = jnp.zeros_like(dvec_ref)
            dpw_ref[...] = jnp.zeros_like(dpw_ref)

        @pl.when(i > 0)
        def _():
            dcbuf[tm:tm + CONV_HALO, :] = dcbuf[0:CONV_HALO, :]
            qbuf[tm:tm + POOL_HALO, :] = qbuf[0:POOL_HALO, :]

        row = tile * tm + lax.broadcasted_iota(jnp.int32, (tm, 1), 0)
        cv = c_ref[...]
        mu = jnp.mean(cv, axis=-1, keepdims=True)
        xc = cv - mu
        rstd = lax.rsqrt(jnp.mean(xc * xc, axis=-1, keepdims=True) + EPS)
        xhat = xc * rstd
        ln = xhat * lg_ref[...] + lb_ref[...]
        sg = jax.nn.sigmoid(ln)
        da = jnp.where(row >= PAD_ROWS, dmix_ref[:, 0:CONV_DIM].astype(F32), 0.0)
        dln = da * (sg * (1.0 + ln * (1.0 - sg)))
        dxh = dln * lg_ref[...]
        dc = rstd * (dxh - jnp.mean(dxh, axis=-1, keepdims=True) - xhat * jnp.mean(dxh * xhat, axis=-1, keepdims=True))
        dcbuf[0:tm, :] = dc
        dvec_ref[0:1, :] += jnp.sum(dc, axis=0, keepdims=True)
        dvec_ref[1:2, :] += jnp.sum(dln * xhat, axis=0, keepdims=True)
        dvec_ref[2:3, :] += jnp.sum(dln, axis=0, keepdims=True)

        av = z_ref[:, 0:CONV_DIM].astype(F32)
        sig_g = jax.nn.sigmoid(z_ref[:, CONV_DIM:2 * CONV_DIM].astype(F32))
        glu_buf[...] = av * sig_g

        def conv_block(rb, carry):
            base = pl.multiple_of(rb * CONV_BLOCK, CONV_BLOCK)
            win = dcbuf[pl.ds(base, CONV_BLOCK + CONV_HALO), :]
            glu = glu_buf[pl.ds(base, CONV_BLOCK), :]
            acc = jnp.zeros((CONV_BLOCK, CONV_DIM), F32)
            for k in range(CONV_WIDTH):
                off = CONV_WIDTH - 1 - k
                slab = win[off:off + CONV_BLOCK, :]
                acc = acc + cw_ref[k:k + 1, :] * slab
                prod = slab * glu
                part = prod[0:8]
                for q in range(1, CONV_BLOCK // 8):
                    part = part + prod[8 * q:8 * q + 8]
                dwacc[k] += part
            glu_buf[pl.ds(base, CONV_BLOCK), :] = acc
            return carry

        lax.fori_loop(0, tm // CONV_BLOCK, conv_block, 0)

        @pl.when(i == nt - 1)
        def _():
            for k in range(CONV_WIDTH):
                dcw_ref[k:k + 1, :] = jnp.sum(dwacc[k], axis=0, keepdims=True)
        dglu = glu_buf[...]
        dz_ref[:, 0:CONV_DIM] = (dglu * sig_g).astype(BF16)
        dz_ref[:, CONV_DIM:2 * CONV_DIM] = (dglu * av * sig_g * (1.0 - sig_g)).astype(BF16)

        tpos = (row - PAD_ROWS + 1).astype(F32)
        for gi, wdw in enumerate(POOL_WINDOWS):
            lo = POOL_GROUP * gi
            dp = dmix_ref[:, CONV_DIM + lo:CONV_DIM + lo + POOL_GROUP].astype(F32)
            pmv = pm_ref[:, lo:lo + POOL_GROUP]
            pwb = pw_ref[gi].astype(BF16)
            dvec_ref[3:4, lo:lo + POOL_GROUP] += jnp.sum(dp * _dot(pmv, pwb), axis=0, keepdims=True)
            dq = (dp * ps_ref[:, lo:lo + POOL_GROUP]).astype(BF16)
            dpw_ref[gi] += _dot_tn(pmv, dq)
            dpm = _dot_nt(dq, pwb)
            qbuf[0:tm, lo:lo + POOL_GROUP] = dpm / jnp.clip(tpos, 1.0, float(wdw))
            sacc = -dpm
            for j in range(wdw):
                sacc = sacc + qbuf[j:j + tm, lo:lo + POOL_GROUP]
            dz_ref[:, 2 * CONV_DIM + lo:2 * CONV_DIM + lo + POOL_GROUP] = sacc.astype(BF16)

    vec = pl.BlockSpec((1, CONV_DIM), lambda i: (0, 0))
    rev = lambda i: (nt - 1 - i, 0)
    return _call_after(
        dep, body, 9,
        [pl.BlockSpec((tm, CONV_DIM + POOL_DIM), rev), pl.BlockSpec((tm, CP_IN), rev),
         pl.BlockSpec((tm, CONV_DIM), rev), pl.BlockSpec((tm, POOL_DIM), rev),
         pl.BlockSpec((CONV_WIDTH, CONV_DIM), lambda i: (0, 0)), vec, vec,
         pl.BlockSpec((len(POOL_WINDOWS), POOL_GROUP, POOL_GROUP), lambda i: (0, 0, 0)), vec],
        (dmix, z, c, pm, conv_w, ln_g, ln_b, pool_w, pool_scale), grid=(nt,),
        out_specs=[pl.BlockSpec((tm, CP_IN), rev), pl.BlockSpec((CONV_WIDTH + 1, CONV_DIM), lambda i: (0, 0)),
                   pl.BlockSpec((8, CONV_DIM), lambda i: (0, 0)),
                   pl.BlockSpec((len(POOL_WINDOWS), POOL_GROUP, POOL_GROUP), lambda i: (0, 0, 0))],
        out_shape=[jax.ShapeDtypeStruct((t, CP_IN), BF16), jax.ShapeDtypeStruct((CONV_WIDTH + 1, CONV_DIM), F32),
                   jax.ShapeDtypeStruct((8, CONV_DIM), F32),
                   jax.ShapeDtypeStruct((len(POOL_WINDOWS), POOL_GROUP, POOL_GROUP), F32)],
        scratch_shapes=[pltpu.VMEM((tm + CONV_HALO, CONV_DIM), F32), pltpu.VMEM((tm + POOL_HALO, POOL_DIM), F32),
                        pltpu.VMEM((tm, CONV_DIM), F32), pltpu.VMEM((CONV_WIDTH + 1, 8, CONV_DIM), F32)],
        compiler_params=_cparams("arbitrary"), name="cp_seq_bwd")


GLA_UNROLL = 2
Q0, K0, V0, G0, R0 =0, GLA_DK, 2 * GLA_DK, 2 * GLA_DK + GLA_DV, 2 * GLA_DK + 2 * GLA_DV


def _split3(x):
    hi = x.astype(BF16)
    r1 = x - hi.astype(F32)
    mid = r1.astype(BF16)
    lo = (r1 - mid.astype(F32)).astype(BF16)
    return hi, mid, lo


def _tri(strict):
    r = lax.broadcasted_iota(jnp.int32, (CHUNK, CHUNK), 0)
    c = lax.broadcasted_iota(jnp.int32, (CHUNK, CHUNK), 1)
    return ((r > c) if strict else (r >= c)).astype(BF16)


def _gate_decay(r, gw_ref, gb_ref, tri):
    pre = _dot(r, gw_ref[...]) + gb_ref[...]
    lac = (jnp.minimum(pre, 0.0) - jnp.log(1.0 + jnp.exp(-jnp.abs(pre)))) * (1.0 / GATE_NORM)
    hi, mid, lo = _split3(lac)
    cum = _dot(tri, hi) + _dot(tri, mid) + _dot(tri, lo)
    return pre, cum, cum[CHUNK - 1:CHUNK, :]


def _gla_seq_fwd(z, gate_w, gate_b, head_g):
    t = z.shape[0]
    tm = _row_tile(t, ROW_TILE_TARGET, CHUNK)
    cpt = tm // CHUNK
    scale = GLA_HK ** -0.5

    def body(z_ref, gw_ref, gb_ref, hg_ref, o_ref, mix_ref, st_ref, state):
        @pl.when(pl.program_id(0) == 0)
        def _():
            state[...] = jnp.zeros_like(state)

        tri = _tri(False)

        def chunk(ci, carry):
            r0 = pl.multiple_of(ci * CHUNK, CHUNK)
            rows = pl.ds(r0, CHUNK)
            _, cum, tot = _gate_decay(z_ref[rows, R0:R0 + GATE_PAD], gw_ref, gb_ref, tri)
            dec = jnp.exp(tot - cum)
            e = jnp.exp(tot)
            st_ref[ci] = state[...].astype(BF16)
            for hd in range(GLA_HEADS):
                ks = slice(hd * GLA_HK, (hd + 1) * GLA_HK)
                vs = slice(hd * GLA_HV, (hd + 1) * GLA_HV)
                kdec = (z_ref[rows, K0 + hd * GLA_HK:K0 + (hd + 1) * GLA_HK].astype(F32) * dec[:, ks]).astype(BF16)
                v = z_ref[rows, V0 + hd * GLA_HV:V0 + (hd + 1) * GLA_HV]
                st = state[vs, :] * e[:, ks] + _dot_tn(v, kdec)
                state[vs, :] = st
                q = z_ref[rows, Q0 + hd * GLA_HK:Q0 + (hd + 1) * GLA_HK]
                o = _dot_nt(q, st.astype(BF16)) * scale
                ob = o.astype(BF16)
                o_ref[rows, vs] = ob
                on = _rms(ob.astype(F32), hg_ref[...])
                gv = z_ref[rows, G0 + hd * GLA_HV:G0 + (hd + 1) * GLA_HV].astype(F32)
                mix_ref[rows, vs] = (on * _silu(gv)).astype(BF16)
            return carry

        lax.fori_loop(0, cpt, chunk, 0, unroll=GLA_UNROLL)

    return pl.pallas_call(
        body, grid=(t // tm,),
        in_specs=[pl.BlockSpec((tm, GLA_IN_PAD), lambda i: (i, 0)),
                  pl.BlockSpec((GATE_PAD, GLA_DK), lambda i: (0, 0)), pl.BlockSpec((1, GLA_DK), lambda i: (0, 0)),
                  pl.BlockSpec((1, GLA_HV), lambda i: (0, 0))],
        out_specs=[pl.BlockSpec((tm, GLA_DV), lambda i: (i, 0)), pl.BlockSpec((tm, GLA_DV), lambda i: (i, 0)),
                   pl.BlockSpec((cpt, GLA_DV, GLA_HK), lambda i: (i, 0, 0))],
        out_shape=[jax.ShapeDtypeStruct((t, GLA_DV), BF16), jax.ShapeDtypeStruct((t, GLA_DV), BF16),
                   jax.ShapeDtypeStruct((t // CHUNK, GLA_DV, GLA_HK), BF16)],
        scratch_shapes=[pltpu.VMEM((GLA_DV, GLA_HK), F32)],
        compiler_params=_cparams("arbitrary"), name="gla_seq_fwd")(z, gate_w, gate_b, head_g)


def _gla_seq_bwd(dmix, o, z, states, gate_w, gate_b, head_g, dep=None):
    t = z.shape[0]
    tm = _row_tile(t, ROW_TILE_TARGET, CHUNK)
    cpt = tm // CHUNK
    nt = t // tm
    scale = GLA_HK ** -0.5

    def body(dmix_ref, o_ref, z_ref, st_ref, gw_ref, gb_ref, hg_ref, dz_ref, dgw_ref, dgb_ref, dhg_ref, dstate):
        @pl.when(pl.program_id(0) == 0)
        def _():
            dstate[...] = jnp.zeros_like(dstate)
            dgw_ref[...] = jnp.zeros_like(dgw_ref)
            dgb_ref[...] = jnp.zeros_like(dgb_ref)
            dhg_ref[...] = jnp.zeros_like(dhg_ref)

        tri = _tri(False)
        tri_strict = _tri(True)

        def chunk(cj, carry):
            ci = cpt - 1 - cj
            r0 = pl.multiple_of(ci * CHUNK, CHUNK)
            rows = pl.ds(r0, CHUNK)
            r = z_ref[rows, R0:R0 + GATE_PAD]
            pre, cum, tot = _gate_decay(r, gw_ref, gb_ref, tri)
            dec = jnp.exp(tot - cum)
            e = jnp.exp(tot)
            dlac_parts = []
            dhg = jnp.zeros((1, GLA_HV), F32)
            for hd in range(GLA_HEADS):
                ks = slice(hd * GLA_HK, (hd + 1) * GLA_HK)
                vs = slice(hd * GLA_HV, (hd + 1) * GLA_HV)
                kcols = slice(K0 + hd * GLA_HK, K0 + (hd + 1) * GLA_HK)
                vcols = slice(V0 + hd * GLA_HV, V0 + (hd + 1) * GLA_HV)
                qcols = slice(Q0 + hd * GLA_HK, Q0 + (hd + 1) * GLA_HK)
                gcols = slice(G0 + hd * GLA_HV, G0 + (hd + 1) * GLA_HV)
                ov = o_ref[rows, vs].astype(F32)
                gv = z_ref[rows, gcols].astype(F32)
                dm = dmix_ref[rows, vs].astype(F32)
                sg = jax.nn.sigmoid(gv)
                rr = lax.rsqrt(jnp.mean(ov * ov, axis=-1, keepdims=True) + EPS)
                xhat = ov * rr
                don = dm * (gv * sg)
                dz_ref[rows, gcols] = (dm * (xhat * hg_ref[...]) * (sg * (1.0 + gv * (1.0 - sg)))).astype(BF16)
                dhg = dhg + jnp.sum(don * xhat, axis=0, keepdims=True)
                dxh = don * hg_ref[...]
                do = (rr * (dxh - xhat * jnp.mean(dxh * xhat, axis=-1, keepdims=True)) * scale).astype(BF16)
                kdec = z_ref[rows, kcols].astype(F32) * dec[:, ks]
                kdb = kdec.astype(BF16)
                v = z_ref[rows, vcols]
                q = z_ref[rows, qcols]
                st_prev = st_ref[ci, vs, :].astype(F32)
                st = st_prev * e[:, ks] + _dot_tn(v, kdb)
                dz_ref[rows, qcols] = _dot(do, st.astype(BF16)).astype(BF16)
                dst = dstate[vs, :] + _dot_tn(do, q)
                dstb = dst.astype(BF16)
                dkdec = _dot(v, dstb)
                dz_ref[rows, vcols] = _dot_nt(kdb, dstb).astype(BF16)
                dtot = jnp.sum(dst * st_prev, axis=0, keepdims=True) * e[:, ks]
                dstate[vs, :] = dst * e[:, ks]
                dz_ref[rows, kcols] = (dkdec * dec[:, ks]).astype(BF16)
                gk = dkdec * kdec
                ghi, gmid, _ = _split3(gk)
                dlac_parts.append(dtot + _dot(tri_strict, ghi) + _dot(tri_strict, gmid))
            dlac = jnp.concatenate(dlac_parts, axis=1)
            dpre = dlac * (1.0 / GATE_NORM) * (1.0 - jax.nn.sigmoid(pre))
            dpb = dpre.astype(BF16)
            dz_ref[rows, R0:R0 + GATE_PAD] = _dot_nt(dpb, gw_ref[...]).astype(BF16)
            dgw_ref[...] += _dot_tn(r, dpb)
            dgb_ref[...] += jnp.sum(dpre, axis=0, keepdims=True)
            dhg_ref[...] += dhg
            return carry

        lax.fori_loop(0, cpt, chunk, 0)

    rev = lambda i: (nt - 1 - i, 0)
    return _call_after(
        dep, body, 7,
        [pl.BlockSpec((tm, GLA_DV), rev), pl.BlockSpec((tm, GLA_DV), rev), pl.BlockSpec((tm, GLA_IN_PAD), rev),
         pl.BlockSpec((cpt, GLA_DV, GLA_HK), lambda i: (nt - 1 - i, 0, 0)),
         pl.BlockSpec((GATE_PAD, GLA_DK), lambda i: (0, 0)), pl.BlockSpec((1, GLA_DK), lambda i: (0, 0)),
         pl.BlockSpec((1, GLA_HV), lambda i: (0, 0))],
        (dmix, o, z, states, gate_w, gate_b, head_g), grid=(nt,),
        out_specs=[pl.BlockSpec((tm, GLA_IN_PAD), rev), pl.BlockSpec((GATE_PAD, GLA_DK), lambda i: (0, 0)),
                   pl.BlockSpec((1, GLA_DK), lambda i: (0, 0)), pl.BlockSpec((1, GLA_HV), lambda i: (0, 0))],
        out_shape=[jax.ShapeDtypeStruct((t, GLA_IN_PAD), BF16), jax.ShapeDtypeStruct((GATE_PAD, GLA_DK), F32),
                   jax.ShapeDtypeStruct((1, GLA_DK), F32), jax.ShapeDtypeStruct((1, GLA_HV), F32)],
        scratch_shapes=[pltpu.VMEM((GLA_DV, GLA_HK), F32)],
        compiler_params=_cparams("arbitrary"), name="gla_seq_bwd")


def _sum_halves(g, recv, c_idx, name):
    n, r, cdim = g.shape
    h = r // 2
    tr = _row_tile(h, 256, 8)
    nh = h // tr

    def body(c_ref, g_ref, r_ref, o_ref):
        o_ref[...] = (g_ref[...] + r_ref[...]).astype(BF16)

    return pl.pallas_call(
        body,
        grid_spec=pltpu.PrefetchScalarGridSpec(
            num_scalar_prefetch=1, grid=(n, nh),
            in_specs=[pl.BlockSpec((None, tr, cdim), lambda s, i, c: (s, c[0] * nh + i, 0)),
                      pl.BlockSpec((None, tr, cdim), lambda s, i, c: (s, i, 0))],
            out_specs=pl.BlockSpec((None, tr, cdim), lambda s, i, c: (s, i, 0))),
        out_shape=jax.ShapeDtypeStruct((n, h, cdim), BF16),
        compiler_params=_cparams("parallel", "parallel"), name=name)(c_idx, g, recv)


def _sum_slots(x, name):
    n, r, cdim = x.shape
    tr = _row_tile(r, 256, 8)

    def body(x_ref, o_ref):
        acc = x_ref[0].astype(F32)
        for j in range(1, n):
            acc = acc + x_ref[j].astype(F32)
        o_ref[...] = acc

    return pl.pallas_call(
        body, grid=(r // tr,),
        in_specs=[pl.BlockSpec((n, tr, cdim), lambda i: (0, i, 0))],
        out_specs=pl.BlockSpec((tr, cdim), lambda i: (i, 0)),
        out_shape=jax.ShapeDtypeStruct((r, cdim), F32),
        compiler_params=_cparams("parallel"), name=name)(x)


def _sum_own_and_slots(own, slots, chip_idx, name):
    n, r, cdim = own.shape
    tr = _row_tile(r, 256, 8)

    def body(s_ref, own_ref, a_ref, b_ref, c_ref, o_ref):
        o_ref[...] = (own_ref[...].astype(F32) + a_ref[...].astype(F32) + b_ref[...].astype(F32)
                      + c_ref[...].astype(F32))

    def slot(dd):
        return pl.BlockSpec((None, tr, cdim), lambda i, s: ((s[0] + dd) % n, i, 0))

    return pl.pallas_call(
        body,
        grid_spec=pltpu.PrefetchScalarGridSpec(
            num_scalar_prefetch=1, grid=(r // tr,), in_specs=[slot(0), slot(1), slot(2), slot(3)],
            out_specs=pl.BlockSpec((tr, cdim), lambda i, s: (i, 0))),
        out_shape=jax.ShapeDtypeStruct((r, cdim), F32),
        compiler_params=_cparams("parallel"), name=name)(chip_idx, own, slots, slots, slots)


def _add2(a, b, name):
    r, cdim = a.shape
    tr = _row_tile(r, 256, 8)

    def body(a_ref, b_ref, o_ref):
        o_ref[...] = a_ref[...] + b_ref[...]

    spec = pl.BlockSpec((tr, cdim), lambda i: (i, 0))
    return pl.pallas_call(body, grid=(r // tr,), in_specs=[spec, spec], out_specs=spec,
                          out_shape=jax.ShapeDtypeStruct((r, cdim), F32),
                          compiler_params=_cparams("parallel"), name=name)(a, b)


def _adamw(w, g, m, v, name):
    r, cdim = w.shape
    tr = _row_tile(r, 256, 8)

    def body(w_ref, g_ref, m_ref, v_ref, d_ref, mo_ref, vo_ref):
        gv = g_ref[...]
        mn = ADAM_B1 * m_ref[...] + (1.0 - ADAM_B1) * gv
        vn = ADAM_B2 * v_ref[...] + (1.0 - ADAM_B2) * (gv * gv)
        m_hat = mn / (1.0 - ADAM_B1 ** ADAM_STEP)
        v_hat = vn / (1.0 - ADAM_B2 ** ADAM_STEP)
        d_ref[...] = -ADAM_LR * (m_hat / (jnp.sqrt(v_hat) + ADAM_EPS) + ADAM_WD * w_ref[...])
        mo_ref[...] = mn
        vo_ref[...] = vn

    spec = pl.BlockSpec((tr, cdim), lambda i: (i, 0))
    shp = jax.ShapeDtypeStruct((r, cdim), F32)
    return pl.pallas_call(body, grid=(r // tr,), in_specs=[spec] * 4, out_specs=[spec] * 3,
                          out_shape=[shp] * 3, compiler_params=_cparams("parallel"), name=name)(w, g, m, v)


def _split_rows(a):
    return a.reshape(a.shape[0], 2, a.shape[1] // 2, a.shape[2])


def _place():
    x, y, c = lax.axis_index("x"), lax.axis_index("y"), lax.axis_index("c")
    chips = [(1 - x, y), (x, 1 - y), (1 - x, 1 - y)]
    return x, y, c, chips


def _remote(src, dst, send_sem, recv_sem, to):
    return pltpu.make_async_remote_copy(src_ref=src, dst_ref=dst, send_sem=send_sem, recv_sem=recv_sem,
                                        device_id=to, device_id_type=MESH)


def _plan_gather(src_refs, land_refs):
    x, y, c, chips = _place()
    me = 2 * x + y
    return [(src.at[c], land.at[me, c], (px, py, c), land.at[2 * px + py, c])
            for src, land in zip(src_refs, land_refs) for (px, py) in chips]


def _plan_scatter(n_parts):
    def plan(src_refs, land_refs):
        x, y, c, chips = _place()
        me = 2 * x + y
        copies = []
        for k, (src, land) in enumerate(zip(src_refs, land_refs)):
            for (px, py) in chips:
                to = 2 * px + py
                copies.append((src.at[to] if k < n_parts else src, land.at[me], (px, py, c), land.at[to]))
        return copies
    return plan


def _plan_exchange(n_split):
    def plan(src_refs, land_refs):
        x, y, c, _ = _place()
        sib = (x, y, 1 - c)
        return [(src.at[:, 1 - c] if k < n_split else src, land, sib, land)
                for k, (src, land) in enumerate(zip(src_refs, land_refs))]
    return plan


def _hbm(a):
    return pltpu.HBM(a.shape, a.dtype)


def _start_copies(name, srcs, lands, plan, ncopy, dep=None):
    ns, nl = len(srcs), len(lands)
    nin = ns + nl + (0 if dep is None else 1)

    def body(*refs):
        send_sems, recv_sems, token = refs[nin], refs[nin + 1], refs[-1]
        for k, (src, dst, dev, _) in enumerate(plan(refs[:ns], refs[ns:ns + nl])):
            _remote(src, dst, send_sems.at[k], recv_sems.at[k], dev).start()
        token[...] = jnp.zeros_like(token)

    args = [pltpu.with_memory_space_constraint(a, pltpu.HBM) for a in list(srcs) + list(lands)]
    outs = pl.pallas_call(
        body, name=name,
        out_shape=(pltpu.SemaphoreType.DMA((ncopy,)), pltpu.SemaphoreType.DMA((ncopy,)),
                   *[_hbm(a) for a in list(srcs) + list(lands)], jax.ShapeDtypeStruct((8, 128), F32)),
        in_specs=[HBM_SPEC] * (ns + nl) + ([] if dep is None else [ANY_SPEC]),
        out_specs=(SEM_SPEC, SEM_SPEC, *([HBM_SPEC] * (ns + nl)), pl.BlockSpec(memory_space=pltpu.VMEM)),
        input_output_aliases={i: 2 + i for i in range(ns + nl)},
        compiler_params=pltpu.CompilerParams(has_side_effects=SIDE_EFFECT),
    )(*args, *([] if dep is None else [dep]))
    return outs[0], outs[1], list(outs[2:2 + ns]), list(outs[2 + ns:2 + ns + nl]), outs[-1]


def _wait_copies(name, started, plan, after):
    send_sems, recv_sems, srcs, lands, _ = started
    ns, nl = len(srcs), len(lands)

    def body(*refs):
        send_ref, recv_ref = refs[ns + nl], refs[ns + nl + 1]
        for k, (src, _, dev, mine) in enumerate(plan(refs[:ns], refs[ns:ns + nl])):
            copy = _remote(src, mine, send_ref.at[k], recv_ref.at[k], dev)
            copy.wait_send()
            copy.wait_recv()

    outs = pl.pallas_call(
        body, name=name, out_shape=tuple(_hbm(a) for a in srcs + lands),
        in_specs=[HBM_SPEC] * (ns + nl) + [SEM_SPEC, SEM_SPEC, ANY_SPEC], out_specs=tuple([HBM_SPEC] * (ns + nl)),
        input_output_aliases={i: i for i in range(ns + nl)},
        compiler_params=pltpu.CompilerParams(has_side_effects=SIDE_EFFECT),
    )(*srcs, *lands, send_sems, recv_sems, after)
    return list(outs[:ns]), list(outs[ns:])


def _share_with_sibling(name, srcs, lands):
    n = len(srcs)

    def body(*refs):
        src_refs, land_refs, out_refs = refs[:n], refs[n:2 * n], refs[2 * n:3 * n]
        send_sem, recv_sem = refs[3 * n:]
        x, y, c, chips = _place()
        me = 2 * x + y
        sib = (x, y, 1 - c)
        sends, recvs = [], []
        for k in range(n):
            sems = (send_sem.at[4 * k], recv_sem.at[4 * k])
            sends.append(_remote(src_refs[k], out_refs[k].at[me], *sems, sib))
            recvs.append(_remote(src_refs[k], out_refs[k].at[me], *sems, sib))
            for j, (px, py) in enumerate(chips):
                frm = 2 * px + py
                sems = (send_sem.at[4 * k + 1 + j], recv_sem.at[4 * k + 1 + j])
                sends.append(_remote(land_refs[k].at[frm, c], out_refs[k].at[frm, c], *sems, sib))
                recvs.append(_remote(land_refs[k].at[frm, c], out_refs[k].at[frm, 1 - c], *sems, sib))
        for cp in sends:
            cp.start()
        for cp in recvs:
            cp.wait_recv()
        for cp in sends:
            cp.wait_send()

    return pl.pallas_call(
        body, name=name, in_specs=[HBM_SPEC] * (2 * n), out_specs=[HBM_SPEC] * n,
        out_shape=[jax.ShapeDtypeStruct(a.shape, a.dtype) for a in lands],
        input_output_aliases={n + k: k for k in range(n)},
        scratch_shapes=[pltpu.SemaphoreType.DMA((4 * n,)), pltpu.SemaphoreType.DMA((4 * n,))],
    )(*srcs, *lands)


def _gather_weights(bigs, small):
    nb = len(bigs)
    shapes = [b.shape for b in bigs]
    bigs = [_split_rows(b) for b in bigs]

    def body(*refs):
        big_in, small_in = refs[:nb], refs[nb]
        big_out, small_out = refs[nb + 1:2 * nb + 1], refs[2 * nb + 1]
        ici_send, ici_recv, d2d_send, d2d_recv, own_send, own_recv, loc_sem = refs[2 * nb + 2:]
        x, y, c, chips = _place()
        me = 2 * x + y
        sib = (x, y, 1 - c)

        def half(ref, k, which):
            return ref.at[:, which]

        local = [pltpu.make_async_copy(small_in, small_out.at[me], loc_sem.at[0])]
        for cp in local:
            cp.start()
        sends = [_remote(big_in[k], big_out[k].at[me], own_send.at[k], own_recv.at[k], sib) for k in range(nb)]
        for j, (px, py) in enumerate(chips):
            for k in range(nb):
                sends.append(_remote(half(big_in[k], k, c), half(big_out[k].at[me], k, c),
                                     ici_send.at[k * 3 + j], ici_recv.at[k * 3 + j], (px, py, c)))
            sends.append(_remote(small_in, small_out.at[me], ici_send.at[nb * 3 + j], ici_recv.at[nb * 3 + j], (px, py, c)))
        for cp in sends:
            cp.start()
        passed = []
        for j, (px, py) in enumerate(chips):
            frm = 2 * px + py
            for k in range(nb):
                landed = half(big_out[k].at[frm], k, c)
                _remote(landed, landed, ici_send.at[k * 3 + j], ici_recv.at[k * 3 + j], (px, py, c)).wait_recv()
                fwd = _remote(landed, landed, d2d_send.at[k * 3 + j], d2d_recv.at[k * 3 + j], sib)
                fwd.start()
                passed.append(fwd)
            _remote(small_in, small_out.at[frm], ici_send.at[nb * 3 + j], ici_recv.at[nb * 3 + j], (px, py, c)).wait_recv()
        for j, (px, py) in enumerate(chips):
            frm = 2 * px + py
            for k in range(nb):
                theirs = half(big_out[k].at[frm], k, 1 - c)
                _remote(theirs, theirs, d2d_send.at[k * 3 + j], d2d_recv.at[k * 3 + j], sib).wait_recv()
        for k in range(nb):
            _remote(big_in[k], big_out[k].at[me], own_send.at[k], own_recv.at[k], sib).wait_recv()
        for cp in sends + passed:
            cp.wait_send()
        for cp in local:
            cp.wait()

    out_shape = [jax.ShapeDtypeStruct((N_CHIPS,) + b.shape, b.dtype) for b in bigs]
    out_shape.append(jax.ShapeDtypeStruct((N_CHIPS,) + small.shape, small.dtype))
    outs = pl.pallas_call(
        body, in_specs=[HBM_SPEC] * (nb + 1), out_specs=[HBM_SPEC] * (nb + 1), out_shape=out_shape,
        scratch_shapes=[pltpu.SemaphoreType.DMA((3 * nb + 3,)), pltpu.SemaphoreType.DMA((3 * nb + 3,)),
                        pltpu.SemaphoreType.DMA((3 * nb,)), pltpu.SemaphoreType.DMA((3 * nb,)),
                        pltpu.SemaphoreType.DMA((nb,)), pltpu.SemaphoreType.DMA((nb,)),
                        pltpu.SemaphoreType.DMA((1,))],
        name="gather_weights")(*bigs, small)
    return [o.reshape((N_CHIPS,) + s) for o, s in zip(outs[:-1], shapes)] + [outs[-1]]


def _exchange_halves(grads, small, name):
    ng = len(grads)
    grads = [_split_rows(g) for g in grads]
    extra = [] if small is None else [small]
    nall = ng + len(extra)

    def body(*refs):
        ins, outs = refs[:nall], refs[nall:2 * nall]
        send_sem, recv_sem = refs[2 * nall:]
        x, y, c, _ = _place()
        sib = (x, y, 1 - c)
        copies = []
        for k in range(nall):
            src = ins[k].at[:, 1 - c] if k < ng else ins[k]
            copies.append(_remote(src, outs[k], send_sem.at[k], recv_sem.at[k], sib))
        for cp in copies:
            cp.start()
        for cp in copies:
            cp.wait_recv()
        for cp in copies:
            cp.wait_send()

    out_shape = [jax.ShapeDtypeStruct((g.shape[0], g.shape[2], g.shape[3]), g.dtype) for g in grads]
    out_shape += [jax.ShapeDtypeStruct(s.shape, s.dtype) for s in extra]
    return pl.pallas_call(
        body, in_specs=[HBM_SPEC] * nall, out_specs=[HBM_SPEC] * nall, out_shape=out_shape,
        scratch_shapes=[pltpu.SemaphoreType.DMA((nall,)), pltpu.SemaphoreType.DMA((nall,))],
        name=name)(*grads, *extra)


def _join_halves(reduced, dests, out_shapes, name):
    nr = len(reduced)

    def body(*refs):
        r_in = refs[:nr]
        outs = refs[nr:nr + len(out_shapes)]
        send_sem, recv_sem, back_send, back_recv = refs[nr + len(out_shapes):]
        x, y, c, _ = _place()
        sib = (x, y, 1 - c)
        sends, backs = [], []
        for k in range(nr):
            oi, layer = dests[k]
            sends.append(_remote(r_in[k], outs[oi].at[layer, c], send_sem.at[k], recv_sem.at[k], sib))
        for cp in sends:
            cp.start()
        for k in range(nr):
            oi, layer = dests[k]
            theirs = outs[oi].at[layer, 1 - c]
            _remote(r_in[k], theirs, send_sem.at[k], recv_sem.at[k], sib).wait_recv()
            back = _remote(theirs, theirs, back_send.at[k], back_recv.at[k], sib)
            back.start()
            backs.append(back)
        for k in range(nr):
            oi, layer = dests[k]
            mine = outs[oi].at[layer, c]
            _remote(mine, mine, back_send.at[k], back_recv.at[k], sib).wait_recv()
        for cp in sends + backs:
            cp.wait_send()

    split = [(s[0], 2, s[1] // 2, s[2]) for s in out_shapes]
    outs = pl.pallas_call(
        body, in_specs=[HBM_SPEC] * nr, out_specs=[HBM_SPEC] * len(out_shapes),
        out_shape=[jax.ShapeDtypeStruct(s, F32) for s in split],
        scratch_shapes=[pltpu.SemaphoreType.DMA((nr,)), pltpu.SemaphoreType.DMA((nr,)),
                        pltpu.SemaphoreType.DMA((nr,)), pltpu.SemaphoreType.DMA((nr,))],
        name=name)(*reduced)
    return [o.reshape(s) for o, s in zip(outs, out_shapes)]


def _pack(arrs):
    flat = jnp.concatenate([a.reshape(-1).astype(F32) for a in arrs])
    n = flat.shape[0]
    rows = -(-n // PACK_WIDTH)
    rows = -(-rows // 8) * 8
    return jnp.pad(flat, (0, rows * PACK_WIDTH - n)).reshape(rows, PACK_WIDTH)


def _unpack(buf, shapes):
    flat = buf.reshape(-1)
    out, off = [], 0
    for shp in shapes:
        n = 1
        for s in shp:
            n *= s
        out.append(flat[off:off + n].reshape(shp))
        off += n
    return out


def _unshard_cols(stacked):
    moved = jnp.moveaxis(stacked, 0, -2)
    return moved.reshape(moved.shape[:-2] + (moved.shape[-2] * moved.shape[-1],))


def _col_shard(full, s, width):
    return lax.dynamic_slice_in_dim(full, s * width, width, axis=full.ndim - 1)


def kernel(x, meta_tokens, mix_norm_g, ffn_norm_g, ffn_w1, ffn_w2, cp_w_in, cp_conv_w, cp_conv_b, cp_ln_g, cp_ln_b, cp_pool_w, cp_pool_scale, cp_w_out, gla_w_in, gla_gate_w2, gla_gate_b, gla_head_g, gla_w_out, final_norm_g, loss_target, m_meta_tokens, m_mix_norm_g, m_ffn_norm_g, m_ffn_w1, m_ffn_w2, m_cp_w_in, m_cp_conv_w, m_cp_conv_b, m_cp_ln_g, m_cp_ln_b, m_cp_pool_w, m_cp_pool_scale, m_cp_w_out, m_gla_w_in, m_gla_gate_w2, m_gla_gate_b, m_gla_head_g, m_gla_w_out, m_final_norm_g, v_meta_tokens, v_mix_norm_g, v_ffn_norm_g, v_ffn_w1, v_ffn_w2, v_cp_w_in, v_cp_conv_w, v_cp_conv_b, v_cp_ln_g, v_cp_ln_b, v_cp_pool_w, v_cp_pool_scale, v_cp_w_out, v_gla_w_in, v_gla_gate_w2, v_gla_gate_b, v_gla_head_g, v_gla_w_out, v_final_norm_g):
    d = D_MODEL
    chip = 2 * lax.axis_index("x") + lax.axis_index("y")
    core = lax.axis_index("c")
    seq = x.shape[1]
    t = seq + CHUNK

    big_w = [ffn_w1, ffn_w2, cp_w_in, cp_w_out, gla_w_in, gla_w_out]
    sharded_small = [meta_tokens, cp_conv_w, gla_gate_w2, gla_gate_b, gla_head_g]
    cpin_g, cpout_g, small_g = _gather_weights([cp_w_in.astype(BF16), cp_w_out.astype(BF16)], _pack(sharded_small))

    def halves(w):
        return w.astype(BF16).reshape(2, w.shape[0] // 2, w.shape[1])

    def start_gather(name, srcs, dep):
        lands = [lax.empty((N_CHIPS,) + s.shape, s.dtype) for s in srcs]
        return _start_copies(name, srcs, lands, _plan_gather, 3 * len(srcs), dep)

    def finish_gather(name, started, after):
        srcs, lands = _wait_copies(name + "_wait", started, _plan_gather, after)
        return [g.reshape(N_CHIPS, 2 * g.shape[2], g.shape[3]) for g in _share_with_sibling(name + "_share", srcs, lands)]

    ffn0_started = start_gather("gather_ffn0_start", [halves(ffn_w1[0]), halves(ffn_w2[0])], small_g)
    gla_started = start_gather("gather_gla_start", [halves(gla_w_in[0]), halves(gla_w_out[0])], ffn0_started[-1])
    ffn1_started = start_gather("gather_ffn1_start", [halves(ffn_w1[1]), halves(ffn_w2[1])], gla_started[-1])
    per_chip = [_unpack(small_g[j], [a.shape for a in sharded_small]) for j in range(N_CHIPS)]
    meta_f, conv_w_f, gate_w_f, gate_b_f, head_g_f = [
        jnp.concatenate([per_chip[j][i] for j in range(N_CHIPS)], axis=-1) for i in range(len(sharded_small))]
    conv_w_f, gate_w_f = conv_w_f[0], gate_w_f[0]
    w_cp_in = _unshard_cols(cpin_g[:, 0])
    w_cp_out = cpout_g.reshape(CONV_DIM + POOL_DIM, d)
    gate_w_pad = jnp.pad(gate_w_f, ((0, GATE_PAD - GATE_RANK), (0, 0))).astype(BF16)
    row = lambda a: a.reshape(1, -1)
    c_idx = core.reshape(1).astype(jnp.int32)
    chip_idx = chip.reshape(1).astype(jnp.int32)

    h0 = jnp.concatenate([jnp.zeros((PAD_ROWS, d), F32), meta_f, x[0]], axis=0)
    z0, u0 = _norm_matmul(h0, row(mix_norm_g[0]), w_cp_in, 512, "cp_in_proj", dep=ffn1_started[-1])
    c0, pm0, mix0 = _cp_seq_fwd(z0, conv_w_f, cp_conv_b, cp_ln_g, cp_ln_b, cp_pool_w[0], cp_pool_scale)
    h1 = _matmul_residual(mix0, w_cp_out, h0, "cp_out_proj")
    w1g0, w2g0 = finish_gather("gather_ffn0", ffn0_started, h1)
    h2, hp0, uf0 = _ffn_fwd(h1, row(ffn_norm_g[0]), w1g0, w2g0, "ffn0_fwd")
    glain_g, glaout_g = finish_gather("gather_gla", gla_started, h2)
    w_gla_in = jnp.pad(_unshard_cols(glain_g), ((0, 0), (0, GLA_IN_PAD - GLA_IN)))
    w_gla_out = glaout_g.reshape(GLA_DV, d)
    z1, u2 = _norm_matmul(h2, row(mix_norm_g[1]), w_gla_in, 640, "gla_in_proj")
    o1, mix1, states = _gla_seq_fwd(z1, gate_w_pad, gate_b_f, head_g_f)
    h3 = _matmul_residual(mix1, w_gla_out, h2, "gla_out_proj")
    w1g1, w2g1 = finish_gather("gather_ffn1", ffn1_started, h3)
    h4, hp1, uf1 = _ffn_fwd(h3, row(ffn_norm_g[1]), w1g1, w2g1, "ffn1_fwd")

    def start_exchange(name, grads):
        srcs = [_split_rows(g) for g in grads]
        lands = [lax.empty((g.shape[0], g.shape[1] // 2, g.shape[2]), g.dtype) for g in grads]
        return _start_copies(name + "_exchange_start", srcs, lands, _plan_exchange(len(grads)), len(grads))

    def start_scatter(name, exchange, after):
        srcs, recv = _wait_copies(name + "_exchange_wait", exchange, _plan_exchange(len(exchange[2])), after)
        parts = [_sum_halves(g.reshape(g.shape[0], -1, g.shape[3]), r, c_idx, "%s_chip_sum_%d" % (name, k))
                 for k, (g, r) in enumerate(zip(srcs, recv))]
        lands = [lax.empty(p.shape, p.dtype) for p in parts]
        return _start_copies(name + "_scatter_start", parts, lands, _plan_scatter(len(parts)), 3 * len(parts))

    def finish_reduce(name, started, after):
        n = len(started[2])
        parts, lands = _wait_copies(name + "_scatter_wait", started, _plan_scatter(n), after)
        return [_sum_own_and_slots(p, s, chip_idx, "%s_slot_sum_%d" % (name, k)) for k, (p, s) in enumerate(zip(parts, lands))]

    target = jnp.pad(loss_target[0], ((CHUNK, 0), (0, 0)))
    dh4, d_final_g, loss_part = _loss_bwd(h4, row(final_norm_g), target)

    dh3, dhp1, d_ffn_g1 = _ffn_bwd_data(dh4, h3, row(ffn_norm_g[1]), hp1, w1g1, w2g1, "ffn1_bwd")
    dw1_1 = _wgrad(uf1, dhp1, N_CHIPS, d, d, False, True, False, "ffn1_dw1")
    dw2_1 = _wgrad(hp1, dh4, N_CHIPS, d, d, True, False, True, "ffn1_dw2")
    ffn1_exchange = start_exchange("ffn1", [dw1_1, dw2_1])

    dmix1 = _dgrad(dh3, w_gla_out, "gla_out_dgrad", dep=ffn1_exchange[-1])
    dw_gla_out = _wgrad(mix1, dh3, 1, GLA_DV, d, False, False, False, "gla_out_dw")
    ffn1_reduce = start_scatter("ffn1", ffn1_exchange, dw_gla_out)
    dz1, d_gate_w, d_gate_b, d_head_g = _gla_seq_bwd(dmix1, o1, z1, states, gate_w_pad, gate_b_f, head_g_f,
                                                     dep=ffn1_reduce[-1])
    dh2, d_mix_g1 = _dgrad_norm_bwd(dz1, w_gla_in, h2, row(mix_norm_g[1]), dh3, 640, "gla_in_dgrad")
    dw_gla_in = _wgrad(u2, dz1, GLA_IN_PAD // 640, d, 640, False, True, False, "gla_in_dw")
    gla_in_cols = jnp.moveaxis(dw_gla_in, 0, 1).reshape(d, GLA_IN_PAD)[:, :GLA_IN]
    gla_exchange = start_exchange("gla", [jnp.moveaxis(gla_in_cols.reshape(d, N_CHIPS, GLA_IN // N_CHIPS), 1, 0),
                                          dw_gla_out.reshape(N_CHIPS, -1, d)])

    dh1, dhp0, d_ffn_g0 = _ffn_bwd_data(dh2, h1, row(ffn_norm_g[0]), hp0, w1g0, w2g0, "ffn0_bwd", dep=gla_exchange[-1])
    gla_reduce = start_scatter("gla", gla_exchange, dh1)
    dw1_0 = _wgrad(uf0, dhp0, N_CHIPS, d, d, False, True, False, "ffn0_dw1", dep=gla_reduce[-1])
    dw2_0 = _wgrad(hp0, dh2, N_CHIPS, d, d, True, False, True, "ffn0_dw2")
    ffn0_exchange = start_exchange("ffn0", [dw1_0, dw2_0])

    dmix0 = _dgrad(dh1, w_cp_out, "cp_out_dgrad", dep=ffn0_exchange[-1])
    dw_cp_out = _wgrad(mix0, dh1, 1, CONV_DIM + POOL_DIM, d, False, False, False, "cp_out_dw")
    ffn0_reduce = start_scatter("ffn0", ffn0_exchange, dw_cp_out)
    dz0, d_conv_w, d_cp_vec, d_pool_w = _cp_seq_bwd(dmix0, z0, c0, pm0, conv_w_f, cp_ln_g, cp_ln_b, cp_pool_w[0],
                                                    cp_pool_scale, dep=ffn0_reduce[-1])
    dh0, d_mix_g0 = _dgrad_norm_bwd(dz0, w_cp_in, h0, row(mix_norm_g[0]), dh1, 512, "cp_in_dgrad")
    dw_cp_in = _wgrad(u0, dz0, N_CHIPS, d, CP_IN // N_CHIPS, False, True, False, "cp_in_dw")

    grad_x = dh0[CHUNK:][None]

    cp_grads = [dw_cp_in, dw_cp_out.reshape(N_CHIPS, -1, d)]
    small_full = [dh0[PAD_ROWS:CHUNK], jnp.concatenate([d_mix_g0, d_mix_g1], axis=0),
                  jnp.concatenate([d_ffn_g0, d_ffn_g1], axis=0), d_conv_w[:CONV_WIDTH][None],
                  d_cp_vec[0:1], d_cp_vec[1:2], d_cp_vec[2:3], d_pool_w[None], d_cp_vec[3:4],
                  d_gate_w[:GATE_RANK][None], d_gate_b, d_head_g, d_final_g[0], loss_part[0, 0:1]]
    small_mine = _pack(small_full)
    recv = _exchange_halves(cp_grads, small_mine, "cp_exchange")
    chip_sums = [_sum_halves(g, r, c_idx, "cp_chip_sum_%d" % k) for k, (g, r) in enumerate(zip(cp_grads, recv[:-1]))]
    small_chip = _add2(small_mine, recv[-1], "chip_sum_small")
    small_slots = lax.dynamic_update_slice(jnp.zeros((N_CHIPS,) + small_chip.shape, F32), small_chip[None], (chip, 0, 0))
    cp_lands = [lax.empty(p.shape, p.dtype) for p in chip_sums] + [small_slots]
    cp_reduce = _start_copies("cp_scatter_start", chip_sums + [small_chip], cp_lands, _plan_scatter(len(chip_sums)),
                              3 * (len(chip_sums) + 1))

    def adamw_big(names, grads):
        outs = {}
        for n, g in zip(names, grads):
            w, m, v = big[n]
            two_d = lambda a: a.reshape(-1, a.shape[-1])
            res = _adamw(two_d(w), two_d(g), two_d(m), two_d(v), "adamw_" + n)
            outs[n] = [g] + [o.reshape(w.shape) for o in res]
        return outs

    big = {"w1": (ffn_w1, m_ffn_w1, v_ffn_w1), "w2": (ffn_w2, m_ffn_w2, v_ffn_w2),
           "cp_in": (cp_w_in, m_cp_w_in, v_cp_w_in), "cp_out": (cp_w_out, m_cp_w_out, v_cp_w_out),
           "gla_in": (gla_w_in, m_gla_w_in, v_gla_w_in), "gla_out": (gla_w_out, m_gla_w_out, v_gla_w_out)}
    red_ffn1 = finish_reduce("ffn1", ffn1_reduce, cp_reduce[-1])
    red_gla = finish_reduce("gla", gla_reduce, cp_reduce[-1])
    red_ffn0 = finish_reduce("ffn0", ffn0_reduce, cp_reduce[-1])
    first = ["w1", "w2", "gla_in", "gla_out"]
    first_grads = _join_halves([red_ffn0[0], red_ffn1[0], red_ffn0[1], red_ffn1[1], red_gla[0], red_gla[1]],
                               [(0, 0), (0, 1), (1, 0), (1, 1), (2, 0), (3, 0)], [big[n][0].shape for n in first],
                               "join_halves_ffn_gla")
    big_out = adamw_big(first, first_grads)
    cp_parts, cp_slots = _wait_copies("cp_scatter_wait", cp_reduce, _plan_scatter(len(chip_sums)), big_out["gla_out"][1])
    red_cp = [_sum_own_and_slots(a, s, chip_idx, "cp_slot_sum_%d" % k)
              for k, (a, s) in enumerate(zip(cp_parts[:-1], cp_slots[:-1]))]
    small_red = _sum_slots(cp_slots[-1], "slot_sum_small")
    last = ["cp_in", "cp_out"]
    last_grads = _join_halves(red_cp, [(0, 0), (1, 0)], [big[n][0].shape for n in last], "join_halves_cp")
    big_out.update(adamw_big(last, last_grads))

    (g_meta, g_mix, g_ffn, g_conv_w, g_conv_b, g_ln_g, g_ln_b, g_pool_w, g_pool_scale, g_gate_w, g_gate_b, g_head,
     g_final, loss_sum) = _unpack(small_red, [a.shape for a in small_full])
    g_meta = _col_shard(g_meta, chip, meta_tokens.shape[-1])
    g_conv_w = _col_shard(g_conv_w, chip, cp_conv_w.shape[-1])
    g_gate_w = _col_shard(g_gate_w, chip, gla_gate_w2.shape[-1])
    g_gate_b = _col_shard(g_gate_b, chip, gla_gate_b.shape[-1])
    g_head = _col_shard(g_head, chip, gla_head_g.shape[-1])
    small_w = [meta_tokens, mix_norm_g, ffn_norm_g, cp_conv_w, cp_conv_b, cp_ln_g, cp_ln_b, cp_pool_w, cp_pool_scale,
               gla_gate_w2, gla_gate_b, gla_head_g, final_norm_g]
    small_m = [m_meta_tokens, m_mix_norm_g, m_ffn_norm_g, m_cp_conv_w, m_cp_conv_b, m_cp_ln_g, m_cp_ln_b, m_cp_pool_w,
               m_cp_pool_scale, m_gla_gate_w2, m_gla_gate_b, m_gla_head_g, m_final_norm_g]
    small_v = [v_meta_tokens, v_mix_norm_g, v_ffn_norm_g, v_cp_conv_w, v_cp_conv_b, v_cp_ln_g, v_cp_ln_b, v_cp_pool_w,
               v_cp_pool_scale, v_gla_gate_w2, v_gla_gate_b, v_gla_head_g, v_final_norm_g]
    small_g = [g_meta, g_mix, g_ffn, g_conv_w, g_conv_b, g_ln_g, g_ln_b, g_pool_w, g_pool_scale, g_gate_w, g_gate_b,
               g_head, g_final]
    shapes = [w.shape for w in small_w]
    small_g = [g.reshape(s) for g, s in zip(small_g, shapes)]
    s_delta, s_m, s_v = _adamw(_pack(small_w), _pack(small_g), _pack(small_m), _pack(small_v), "adamw_small")
    s_delta, s_m, s_v = _unpack(s_delta, shapes), _unpack(s_m, shapes), _unpack(s_v, shapes)

    order = ["meta", "mix", "ffn", "w1", "w2", "cp_in", "conv_w", "conv_b", "ln_g", "ln_b", "pool_w", "pool_scale",
             "cp_out", "gla_in", "gate_w", "gate_b", "head", "gla_out", "final"]
    small_names = ["meta", "mix", "ffn", "conv_w", "conv_b", "ln_g", "ln_b", "pool_w", "pool_scale", "gate_w", "gate_b",
                   "head", "final"]
    big_names = ["w1", "w2", "cp_in", "cp_out", "gla_in", "gla_out"]
    table = {n: (small_g[i], s_delta[i], s_m[i], s_v[i]) for i, n in enumerate(small_names)}
    table.update({n: tuple(big_out[n]) for n in big_names})
    loss = loss_sum.reshape(())
    return (loss, grad_x, *[table[n][0] for n in order], *[table[n][1] for n in order],
            *[table[n][2] for n in order], *[table[n][3] for n in order])
```

```python
import functools

import jax
import jax.numpy as jnp
from jax import lax
from jax.experimental import pallas as pl
from jax.experimental.pallas import tpu as pltpu

F32 = jnp.float32
BF16 = jnp.bfloat16

D_MODEL = 1024
N_META = 16
CHUNK = 64
PAD_ROWS = CHUNK - N_META
EPS = 1e-5
CONV_DIM = 512
CONV_WIDTH = 31
CONV_HALO = 32
POOL_DIM = 512
POOL_WINDOWS = (2, 4, 8, 16)
POOL_GROUP = 128
POOL_HALO = 16
CP_IN = 2 * CONV_DIM + POOL_DIM
GLA_HEADS = 4
GLA_DK = 512
GLA_DV = 1024
GLA_HK = GLA_DK // GLA_HEADS
GLA_HV = GLA_DV // GLA_HEADS
GATE_RANK = 16
GATE_PAD = 128
GATE_NORM = 16.0
GLA_IN = 2 * GLA_DK + 2 * GLA_DV + GATE_RANK
GLA_IN_PAD = 2 * GLA_DK + 2 * GLA_DV + GATE_PAD
N_CHIPS = 4
ADAM_LR = 0.001
ADAM_B1 = 0.9
ADAM_B2 = 0.999
ADAM_EPS = 1e-08
ADAM_WD = 0.01
ADAM_STEP = 10

VMEM_LIMIT_BYTES = 56 * 1024 * 1024
ROW_TILE_TARGET = 832
TOKEN_TILE_TARGET = 1040
PACK_WIDTH = 1024
MESH = pl.DeviceIdType.MESH
HBM_SPEC = pl.BlockSpec(memory_space=pltpu.HBM)
ANY_SPEC = pl.BlockSpec(memory_space=pl.ANY)
SEM_SPEC = pl.BlockSpec(memory_space=pltpu.SEMAPHORE)
SIDE_EFFECT = pltpu.SideEffectType.DATAFLOW_SIDE_EFFECTING


def _cparams(*sem):
    return pltpu.CompilerParams(dimension_semantics=sem, vmem_limit_bytes=VMEM_LIMIT_BYTES)


def _row_tile(t, target, mult):
    best = mult
    for cand in range(mult, min(t, target) + 1, mult):
        if t % cand == 0:
            best = cand
    assert t % best == 0, (t, best)
    return best


def _rms(h, g):
    return h * lax.rsqrt(jnp.mean(h * h, axis=-1, keepdims=True) + EPS) * g


def _rms_bwd(h, g, du):
    r = lax.rsqrt(jnp.mean(h * h, axis=-1, keepdims=True) + EPS)
    xhat = h * r
    dxh = du * g
    dh = r * (dxh - xhat * jnp.mean(dxh * xhat, axis=-1, keepdims=True))
    return dh, du * xhat


def _valid_rows(i, tm):
    row = i * tm + lax.broadcasted_iota(jnp.int32, (tm, 1), 0)
    return row >= PAD_ROWS


def _dot(a, b):
    return jnp.dot(a, b, preferred_element_type=F32)


def _dot_nt(a, b):
    return lax.dot_general(a, b, (((1,), (1,)), ((), ())), preferred_element_type=F32)


def _dot_tn(a, b):
    return lax.dot_general(a, b, (((0,), (0,)), ((), ())), preferred_element_type=F32)


def _accumulate(ref, val, first):
    @pl.when(first)
    def _():
        ref[...] = val

    @pl.when(jnp.logical_not(first))
    def _():
        ref[...] += val


def _call_after(dep, body, n_in, in_specs, args, **kw):
    if dep is None:
        return pl.pallas_call(body, in_specs=in_specs, **kw)(*args)

    def with_dep(*refs):
        body(*refs[:n_in], *refs[n_in + 1:])

    return pl.pallas_call(with_dep, in_specs=list(in_specs) + [ANY_SPEC], **kw)(*args, dep)


def _norm_matmul(h, g, w, nc, name, dep=None):
    t, d = h.shape
    n = w.shape[1]
    tm = _row_tile(t, TOKEN_TILE_TARGET, 16)

    def body(h_ref, g_ref, w_ref, z_ref, u_ref):
        u = _rms(h_ref[...], g_ref[...]).astype(BF16)
        u_ref[...] = u
        for n0 in range(0, n, nc):
            z_ref[:, n0:n0 + nc] = _dot(u, w_ref[:, n0:n0 + nc]).astype(BF16)

    return _call_after(
        dep, body, 3,
        [pl.BlockSpec((tm, d), lambda i: (i, 0)), pl.BlockSpec((1, d), lambda i: (0, 0)),
         pl.BlockSpec((d, n), lambda i: (0, 0))], (h, g, w), grid=(t // tm,),
        out_specs=[pl.BlockSpec((tm, n), lambda i: (i, 0)), pl.BlockSpec((tm, d), lambda i: (i, 0))],
        out_shape=[jax.ShapeDtypeStruct((t, n), BF16), jax.ShapeDtypeStruct((t, d), BF16)],
        compiler_params=_cparams("parallel"), name=name)


def _matmul_residual(a, w, h, name):
    t, k = a.shape
    d = w.shape[1]
    tm = _row_tile(t, TOKEN_TILE_TARGET, 16)

    def body(a_ref, w_ref, h_ref, o_ref):
        o_ref[...] = h_ref[...] + _dot(a_ref[...], w_ref[...])

    return pl.pallas_call(
        body, grid=(t // tm,),
        in_specs=[pl.BlockSpec((tm, k), lambda i: (i, 0)), pl.BlockSpec((k, d), lambda i: (0, 0)),
                  pl.BlockSpec((tm, d), lambda i: (i, 0))],
        out_specs=pl.BlockSpec((tm, d), lambda i: (i, 0)),
        out_shape=jax.ShapeDtypeStruct((t, d), F32),
        compiler_params=_cparams("parallel"), name=name)(a, w, h)


def _ffn_fwd(h, g, w1g, w2g, name):
    t, d = h.shape
    ns, ffs = w1g.shape[0], w1g.shape[2]
    tm = _row_tile(t, TOKEN_TILE_TARGET, 16)

    def body(h_ref, g_ref, w1_ref, w2_ref, ho_ref, hp_ref, u_ref, acc_ref):
        s = pl.program_id(1)

        @pl.when(s == 0)
        def _():
            u_ref[...] = _rms(h_ref[...], g_ref[...]).astype(BF16)

        hp = _dot(u_ref[...], w1_ref[...])
        hp_ref[...] = hp.astype(BF16)
        a = jnp.maximum(hp, 0.0)
        _accumulate(acc_ref, _dot((a * a).astype(BF16), w2_ref[...]), s == 0)

        @pl.when(s == ns - 1)
        def _():
            ho_ref[...] = h_ref[...] + acc_ref[...]

    return pl.pallas_call(
        body, grid=(t // tm, ns),
        in_specs=[pl.BlockSpec((tm, d), lambda i, s: (i, 0)), pl.BlockSpec((1, d), lambda i, s: (0, 0)),
                  pl.BlockSpec((None, d, ffs), lambda i, s: (s, 0, 0)),
                  pl.BlockSpec((None, ffs, d), lambda i, s: (s, 0, 0))],
        out_specs=[pl.BlockSpec((tm, d), lambda i, s: (i, 0)), pl.BlockSpec((tm, ffs), lambda i, s: (i, s)),
                   pl.BlockSpec((tm, d), lambda i, s: (i, 0))],
        out_shape=[jax.ShapeDtypeStruct((t, d), F32), jax.ShapeDtypeStruct((t, ns * ffs), BF16),
                   jax.ShapeDtypeStruct((t, d), BF16)],
        scratch_shapes=[pltpu.VMEM((tm, d), F32)],
        compiler_params=_cparams("parallel", "arbitrary"), name=name)(h, g, w1g, w2g)


def _ffn_bwd_data(dh, h, g, hp, w1g, w2g, name, dep=None):
    t, d = h.shape
    ns, ffs = w1g.shape[0], w1g.shape[2]
    tm = _row_tile(t, ROW_TILE_TARGET, CHUNK)

    def body(dh_ref, h_ref, g_ref, hp_ref, w1_ref, w2_ref, dhi_ref, dhp_ref, dg_ref, acc_ref):
        i, s = pl.program_id(0), pl.program_id(1)
        da = _dot_nt(dh_ref[...].astype(BF16), w2_ref[...])
        dhp = (da * (2.0 * jnp.maximum(hp_ref[...].astype(F32), 0.0))).astype(BF16)
        dhp_ref[...] = dhp
        _accumulate(acc_ref, _dot_nt(dhp, w1_ref[...]), s == 0)

        @pl.when(s == ns - 1)
        def _():
            dhn, dgr = _rms_bwd(h_ref[...], g_ref[...], acc_ref[...])
            dhi_ref[...] = jnp.where(_valid_rows(i, tm), dh_ref[...] + dhn, 0.0)
            _accumulate(dg_ref, jnp.sum(dgr, axis=0, keepdims=True), i == 0)

    return _call_after(
        dep, body, 6,
        [pl.BlockSpec((tm, d), lambda i, s: (i, 0)), pl.BlockSpec((tm, d), lambda i, s: (i, 0)),
         pl.BlockSpec((1, d), lambda i, s: (0, 0)), pl.BlockSpec((tm, ffs), lambda i, s: (i, s)),
         pl.BlockSpec((None, d, ffs), lambda i, s: (s, 0, 0)),
         pl.BlockSpec((None, ffs, d), lambda i, s: (s, 0, 0))], (dh, h, g, hp, w1g, w2g), grid=(t // tm, ns),
        out_specs=[pl.BlockSpec((tm, d), lambda i, s: (i, 0)), pl.BlockSpec((tm, ffs), lambda i, s: (i, s)),
                   pl.BlockSpec((1, d), lambda i, s: (0, 0))],
        out_shape=[jax.ShapeDtypeStruct((t, d), F32), jax.ShapeDtypeStruct((t, ns * ffs), BF16),
                   jax.ShapeDtypeStruct((1, d), F32)],
        scratch_shapes=[pltpu.VMEM((tm, d), F32)],
        compiler_params=_cparams("arbitrary", "arbitrary"), name=name)


WGRAD_ROWS = 1024


def _wgrad(x, dy, nb, xc, yc, x_by_block, dy_by_block, relu2, name, dep=None):
    t = x.shape[0]
    tk = _row_tile(t - CHUNK, WGRAD_ROWS, CHUNK)

    def prep(xv):
        if relu2:
            xv = jnp.maximum(xv.astype(F32), 0.0)
            xv = xv * xv
        return xv.astype(BF16)

    def body(xh_ref, dyh_ref, x_ref, dy_ref, o_ref):
        k = pl.program_id(1)
        p = _dot_tn(prep(x_ref[...]), dy_ref[...].astype(BF16))

        @pl.when(k == 0)
        def _():
            o_ref[...] = p + _dot_tn(prep(xh_ref[...]), dyh_ref[...].astype(BF16))

        @pl.when(k > 0)
        def _():
            o_ref[...] += p

    def head(width, by_block):
        return pl.BlockSpec((CHUNK, width), (lambda b, k: (0, b)) if by_block else (lambda b, k: (0, 0)))

    def rest(width, by_block):
        def index(b, k):
            return pl.multiple_of(CHUNK + k * tk, CHUNK), (pl.multiple_of(b * width, 128) if by_block else 0)
        return pl.BlockSpec((pl.Element(tk), pl.Element(width)), index)

    return _call_after(
        dep, body, 4,
        [head(xc, x_by_block), head(yc, dy_by_block), rest(xc, x_by_block), rest(yc, dy_by_block)], (x, dy, x, dy),
        grid=(nb, (t - CHUNK) // tk),
        out_specs=pl.BlockSpec((None, xc, yc), lambda b, k: (b, 0, 0)),
        out_shape=jax.ShapeDtypeStruct((nb, xc, yc), F32),
        compiler_params=_cparams("parallel", "arbitrary"), name=name)


def _dgrad(dh, w, name, dep=None):
    t, d = dh.shape
    k = w.shape[0]
    tm = _row_tile(t, TOKEN_TILE_TARGET, 16)

    def body(dh_ref, w_ref, o_ref):
        o_ref[...] = _dot_nt(dh_ref[...].astype(BF16), w_ref[...]).astype(BF16)

    return _call_after(
        dep, body, 2,
        [pl.BlockSpec((tm, d), lambda i: (i, 0)), pl.BlockSpec((k, d), lambda i: (0, 0))], (dh, w), grid=(t // tm,),
        out_specs=pl.BlockSpec((tm, k), lambda i: (i, 0)),
        out_shape=jax.ShapeDtypeStruct((t, k), BF16),
        compiler_params=_cparams("parallel"), name=name)


def _dgrad_norm_bwd(dz, w, h, g, dh, nc, name):
    t, d = h.shape
    n = w.shape[1]
    tm = _row_tile(t, ROW_TILE_TARGET // 2, 16)

    def body(dz_ref, w_ref, h_ref, g_ref, dh_ref, dhi_ref, dg_ref):
        i = pl.program_id(0)
        du = jnp.zeros((tm, d), F32)
        for n0 in range(0, n, nc):
            du = du + _dot_nt(dz_ref[:, n0:n0 + nc], w_ref[:, n0:n0 + nc])
        dhn, dgr = _rms_bwd(h_ref[...], g_ref[...], du)
        dhi_ref[...] = jnp.where(_valid_rows(i, tm), dh_ref[...] + dhn, 0.0)
        _accumulate(dg_ref, jnp.sum(dgr, axis=0, keepdims=True), i == 0)

    return pl.pallas_call(
        body, grid=(t // tm,),
        in_specs=[pl.BlockSpec((tm, n), lambda i: (i, 0)), pl.BlockSpec((d, n), lambda i: (0, 0)),
                  pl.BlockSpec((tm, d), lambda i: (i, 0)), pl.BlockSpec((1, d), lambda i: (0, 0)),
                  pl.BlockSpec((tm, d), lambda i: (i, 0))],
        out_specs=[pl.BlockSpec((tm, d), lambda i: (i, 0)), pl.BlockSpec((1, d), lambda i: (0, 0))],
        out_shape=[jax.ShapeDtypeStruct((t, d), F32), jax.ShapeDtypeStruct((1, d), F32)],
        compiler_params=_cparams("arbitrary"), name=name)(dz, w, h, g, dh)


def _loss_bwd(h, g, target):
    t, d = h.shape
    tl = _row_tile(t - CHUNK, 1024, CHUNK)

    def body(h_ref, g_ref, t_ref, dh_ref, dg_ref, loss_ref):
        i = pl.program_id(0)
        hv, gv = h_ref[...], g_ref[...]
        err = _rms(hv, gv) - t_ref[...]
        part = 0.5 * jnp.sum(jnp.mean(err * err, axis=-1, keepdims=True), axis=0, keepdims=True)
        dhn, dgr = _rms_bwd(hv, gv, err * (1.0 / d))
        dh_ref[...] = dhn
        _accumulate(dg_ref, jnp.sum(dgr, axis=0, keepdims=True), i == 0)
        _accumulate(loss_ref, jnp.broadcast_to(part, (8, 128)), i == 0)

    shifted = pl.BlockSpec((pl.Element(tl), pl.Element(d)), lambda i: (pl.multiple_of(CHUNK + i * tl, CHUNK), 0))
    dh, dg, loss = pl.pallas_call(
        body, grid=((t - CHUNK) // tl,),
        in_specs=[shifted, pl.BlockSpec((1, d), lambda i: (0, 0)), pl.BlockSpec((tl, d), lambda i: (i, 0))],
        out_specs=[shifted, pl.BlockSpec((1, d), lambda i: (0, 0)), pl.BlockSpec((8, 128), lambda i: (0, 0))],
        out_shape=[jax.ShapeDtypeStruct((t, d), F32), jax.ShapeDtypeStruct((1, d), F32),
                   jax.ShapeDtypeStruct((8, 128), F32)],
        compiler_params=_cparams("arbitrary"), name="loss_bwd")(h, g, target)

    def zero_head(dh_ref, o_ref):
        o_ref[...] = jnp.zeros_like(o_ref)

    dh = pl.pallas_call(
        zero_head, grid=(1,), in_specs=[ANY_SPEC], out_specs=pl.BlockSpec((CHUNK, d), lambda i: (0, 0)),
        out_shape=jax.ShapeDtypeStruct((t, d), F32), input_output_aliases={0: 0}, name="loss_bwd_head")(dh)
    return dh, dg, loss


CONV_BLOCK = 32


def _silu(x):
    return x * jax.nn.sigmoid(x)


def _cp_seq_fwd(z, conv_w, conv_b, ln_g, ln_b, pool_w, pool_scale):
    t = z.shape[0]
    tm = _row_tile(t, ROW_TILE_TARGET, CHUNK)

    def body(z_ref, cw_ref, cb_ref, lg_ref, lb_ref, pw_ref, ps_ref, c_ref, pm_ref, mix_ref, gbuf, pbuf):
        i = pl.program_id(0)

        @pl.when(i == 0)
        def _():
            gbuf[0:CONV_HALO, :] = jnp.zeros((CONV_HALO, CONV_DIM), F32)
            pbuf[0:POOL_HALO, :] = jnp.zeros((POOL_HALO, POOL_DIM), F32)

        @pl.when(i > 0)
        def _():
            gbuf[0:CONV_HALO, :] = gbuf[tm:tm + CONV_HALO, :]
            pbuf[0:POOL_HALO, :] = pbuf[tm:tm + POOL_HALO, :]

        av = z_ref[:, 0:CONV_DIM].astype(F32)
        ag = z_ref[:, CONV_DIM:2 * CONV_DIM].astype(F32)
        gbuf[CONV_HALO:CONV_HALO + tm, :] = av * jax.nn.sigmoid(ag)
        pbuf[POOL_HALO:POOL_HALO + tm, :] = z_ref[:, 2 * CONV_DIM:CP_IN].astype(F32)

        def conv_block(rb, carry):
            base = pl.multiple_of(rb * CONV_BLOCK, CONV_BLOCK)
            win = gbuf[pl.ds(base, CONV_BLOCK + CONV_HALO), :]
            acc = jnp.zeros((CONV_BLOCK, CONV_DIM), F32)
            for k in range(CONV_WIDTH):
                off = CONV_HALO - (CONV_WIDTH - 1) + k
                acc = acc + cw_ref[k:k + 1, :] * win[off:off + CONV_BLOCK, :]
            c_ref[pl.ds(base, CONV_BLOCK), :] = acc + cb_ref[...]
            return carry

        lax.fori_loop(0, tm // CONV_BLOCK, conv_block, 0)

        c = c_ref[...]
        mu = jnp.mean(c, axis=-1, keepdims=True)
        xc = c - mu
        ln = xc * lax.rsqrt(jnp.mean(xc * xc, axis=-1, keepdims=True) + EPS) * lg_ref[...] + lb_ref[...]
        row = i * tm + lax.broadcasted_iota(jnp.int32, (tm, 1), 0)
        mix_ref[:, 0:CONV_DIM] = jnp.where(row >= PAD_ROWS, _silu(ln), 0.0).astype(BF16)

        tpos = (row - PAD_ROWS + 1).astype(F32)
        for gi, wdw in enumerate(POOL_WINDOWS):
            lo = POOL_GROUP * gi
            cur = pbuf[POOL_HALO:POOL_HALO + tm, lo:lo + POOL_GROUP]
            sacc = cur
            for j in range(1, wdw):
                sacc = sacc + pbuf[POOL_HALO - j:POOL_HALO - j + tm, lo:lo + POOL_GROUP]
            pm = (sacc / jnp.clip(tpos, 1.0, float(wdw)) - cur).astype(BF16)
            pm_ref[:, lo:lo + POOL_GROUP] = pm
            pg = _dot(pm, pw_ref[gi].astype(BF16))
            mix_ref[:, CONV_DIM + lo:CONV_DIM + lo + POOL_GROUP] = (pg * ps_ref[:, lo:lo + POOL_GROUP]).astype(BF16)

    vec = pl.BlockSpec((1, CONV_DIM), lambda i: (0, 0))
    return pl.pallas_call(
        body, grid=(t // tm,),
        in_specs=[pl.BlockSpec((tm, CP_IN), lambda i: (i, 0)),
                  pl.BlockSpec((CONV_WIDTH, CONV_DIM), lambda i: (0, 0)), vec, vec, vec,
                  pl.BlockSpec((len(POOL_WINDOWS), POOL_GROUP, POOL_GROUP), lambda i: (0, 0, 0)), vec],
        out_specs=[pl.BlockSpec((tm, CONV_DIM), lambda i: (i, 0)), pl.BlockSpec((tm, POOL_DIM), lambda i: (i, 0)),
                   pl.BlockSpec((tm, CONV_DIM + POOL_DIM), lambda i: (i, 0))],
        out_shape=[jax.ShapeDtypeStruct((t, CONV_DIM), F32), jax.ShapeDtypeStruct((t, POOL_DIM), BF16),
                   jax.ShapeDtypeStruct((t, CONV_DIM + POOL_DIM), BF16)],
        scratch_shapes=[pltpu.VMEM((tm + CONV_HALO, CONV_DIM), F32), pltpu.VMEM((tm + POOL_HALO, POOL_DIM), F32)],
        compiler_params=_cparams("arbitrary"), name="cp_seq_fwd")(z, conv_w, conv_b, ln_g, ln_b, pool_w, pool_scale)


def _cp_seq_bwd(dmix, z, c, pm, conv_w, ln_g, ln_b, pool_w, pool_scale, dep=None):
    t = z.shape[0]
    tm = _row_tile(t, ROW_TILE_TARGET, CHUNK)
    nt = t // tm

    def body(dmix_ref, z_ref, c_ref, pm_ref, cw_ref, lg_ref, lb_ref, pw_ref, ps_ref,
             dz_ref, dcw_ref, dvec_ref, dpw_ref, dcbuf, qbuf, glu_buf, dwacc):
        i = pl.program_id(0)
        tile = nt - 1 - i

        @pl.when(i == 0)
        def _():
            dcbuf[tm:tm + CONV_HALO, :] = jnp.zeros((CONV_HALO, CONV_DIM), F32)
            qbuf[tm:tm + POOL_HALO, :] = jnp.zeros((POOL_HALO, POOL_DIM), F32)
            dcw_ref[...] = jnp.zeros_like(dcw_ref)
            dwacc[...] = jnp.zeros_like(dwacc)
            dvec_ref[...] = jnp.zeros_like(dvec_ref)
            dpw_ref[...] = jnp.zeros_like(dpw_ref)

        @pl.when(i > 0)
        def _():
            dcbuf[tm:tm + CONV_HALO, :] = dcbuf[0:CONV_HALO, :]
            qbuf[tm:tm + POOL_HALO, :] = qbuf[0:POOL_HALO, :]

        row = tile * tm + lax.broadcasted_iota(jnp.int32, (tm, 1), 0)
        cv = c_ref[...]
        mu = jnp.mean(cv, axis=-1, keepdims=True)
        xc = cv - mu
        rstd = lax.rsqrt(jnp.mean(xc * xc, axis=-1, keepdims=True) + EPS)
        xhat = xc * rstd
        ln = xhat * lg_ref[...] + lb_ref[...]
        sg = jax.nn.sigmoid(ln)
        da = jnp.where(row >= PAD_ROWS, dmix_ref[:, 0:CONV_DIM].astype(F32), 0.0)
        dln = da * (sg * (1.0 + ln * (1.0 - sg)))
        dxh = dln * lg_ref[...]
        dc = rstd * (dxh - jnp.mean(dxh, axis=-1, keepdims=True) - xhat * jnp.mean(dxh * xhat, axis=-1, keepdims=True))
        dcbuf[0:tm, :] = dc
        dvec_ref[0:1, :] += jnp.sum(dc, axis=0, keepdims=True)
        dvec_ref[1:2, :] += jnp.sum(dln * xhat, axis=0, keepdims=True)
        dvec_ref[2:3, :] += jnp.sum(dln, axis=0, keepdims=True)

        av = z_ref[:, 0:CONV_DIM].astype(F32)
        sig_g = jax.nn.sigmoid(z_ref[:, CONV_DIM:2 * CONV_DIM].astype(F32))
        glu_buf[...] = av * sig_g

        def conv_block(rb, carry):
            base = pl.multiple_of(rb * CONV_BLOCK, CONV_BLOCK)
            win = dcbuf[pl.ds(base, CONV_BLOCK + CONV_HALO), :]
            glu = glu_buf[pl.ds(base, CONV_BLOCK), :]
            acc = jnp.zeros((CONV_BLOCK, CONV_DIM), F32)
            for k in range(CONV_WIDTH):
                off = CONV_WIDTH - 1 - k
                slab = win[off:off + CONV_BLOCK, :]
                acc = acc + cw_ref[k:k + 1, :] * slab
                prod = slab * glu
                part = prod[0:8]
                for q in range(1, CONV_BLOCK // 8):
                    part = part + prod[8 * q:8 * q + 8]
                dwacc[k] += part
            glu_buf[pl.ds(base, CONV_BLOCK), :] = acc
            return carry

        lax.fori_loop(0, tm // CONV_BLOCK, conv_block, 0)

        @pl.when(i == nt - 1)
        def _():
            for k in range(CONV_WIDTH):
                dcw_ref[k:k + 1, :] = jnp.sum(dwacc[k], axis=0, keepdims=True)
        dglu = glu_buf[...]
        dz_ref[:, 0:CONV_DIM] = (dglu * sig_g).astype(BF16)
        dz_ref[:, CONV_DIM:2 * CONV_DIM] = (dglu * av * sig_g * (1.0 - sig_g)).astype(BF16)

        tpos = (row - PAD_ROWS + 1).astype(F32)
        for gi, wdw in enumerate(POOL_WINDOWS):
            lo = POOL_GROUP * gi
            dp = dmix_ref[:, CONV_DIM + lo:CONV_DIM + lo + POOL_GROUP].astype(F32)
            pmv = pm_ref[:, lo:lo + POOL_GROUP]
            pwb = pw_ref[gi].astype(BF16)
            dvec_ref[3:4, lo:lo + POOL_GROUP] += jnp.sum(dp * _dot(pmv, pwb), axis=0, keepdims=True)
            dq = (dp * ps_ref[:, lo:lo + POOL_GROUP]).astype(BF16)
            dpw_ref[gi] += _dot_tn(pmv, dq)
            dpm = _dot_nt(dq, pwb)
            qbuf[0:tm, lo:lo + POOL_GROUP] = dpm / jnp.clip(tpos, 1.0, float(wdw))
            sacc = -dpm
            for j in range(wdw):
                sacc = sacc + qbuf[j:j + tm, lo:lo + POOL_GROUP]
            dz_ref[:, 2 * CONV_DIM + lo:2 * CONV_DIM + lo + POOL_GROUP] = sacc.astype(BF16)

    vec = pl.BlockSpec((1, CONV_DIM), lambda i: (0, 0))
    rev = lambda i: (nt - 1 - i, 0)
    return _call_after(
        dep, body, 9,
        [pl.BlockSpec((tm, CONV_DIM + POOL_DIM), rev), pl.BlockSpec((tm, CP_IN), rev),
         pl.BlockSpec((tm, CONV_DIM), rev), pl.BlockSpec((tm, POOL_DIM), rev),
         pl.BlockSpec((CONV_WIDTH, CONV_DIM), lambda i: (0, 0)), vec, vec,
         pl.BlockSpec((len(POOL_WINDOWS), POOL_GROUP, POOL_GROUP), lambda i: (0, 0, 0)), vec],
        (dmix, z, c, pm, conv_w, ln_g, ln_b, pool_w, pool_scale), grid=(nt,),
        out_specs=[pl.BlockSpec((tm, CP_IN), rev), pl.BlockSpec((CONV_WIDTH + 1, CONV_DIM), lambda i: (0, 0)),
                   pl.BlockSpec((8, CONV_DIM), lambda i: (0, 0)),
                   pl.BlockSpec((len(POOL_WINDOWS), POOL_GROUP, POOL_GROUP), lambda i: (0, 0, 0))],
        out_shape=[jax.ShapeDtypeStruct((t, CP_IN), BF16), jax.ShapeDtypeStruct((CONV_WIDTH + 1, CONV_DIM), F32),
                   jax.ShapeDtypeStruct((8, CONV_DIM), F32),
                   jax.ShapeDtypeStruct((len(POOL_WINDOWS), POOL_GROUP, POOL_GROUP), F32)],
        scratch_shapes=[pltpu.VMEM((tm + CONV_HALO, CONV_DIM), F32), pltpu.VMEM((tm + POOL_HALO, POOL_DIM), F32),
                        pltpu.VMEM((tm, CONV_DIM), F32), pltpu.VMEM((CONV_WIDTH + 1, 8, CONV_DIM), F32)],
        compiler_params=_cparams("arbitrary"), name="cp_seq_bwd")


GLA_UNROLL = 2
Q0, K0, V0, G0, R0 =0, GLA_DK, 2 * GLA_DK, 2 * GLA_DK + GLA_DV, 2 * GLA_DK + 2 * GLA_DV


def _split3(x):
    hi = x.astype(BF16)
    r1 = x - hi.astype(F32)
    mid = r1.astype(BF16)
    lo = (r1 - mid.astype(F32)).astype(BF16)
    return hi, mid, lo


def _tri(strict):
    r = lax.broadcasted_iota(jnp.int32, (CHUNK, CHUNK), 0)
    c = lax.broadcasted_iota(jnp.int32, (CHUNK, CHUNK), 1)
    return ((r > c) if strict else (r >= c)).astype(BF16)


def _gate_decay(r, gw_ref, gb_ref, tri):
    pre = _dot(r, gw_ref[...]) + gb_ref[...]
    lac = (jnp.minimum(pre, 0.0) - jnp.log(1.0 + jnp.exp(-jnp.abs(pre)))) * (1.0 / GATE_NORM)
    hi, mid, lo = _split3(lac)
    cum = _dot(tri, hi) + _dot(tri, mid) + _dot(tri, lo)
    return pre, cum, cum[CHUNK - 1:CHUNK, :]


def _gla_seq_fwd(z, gate_w, gate_b, head_g):
    t = z.shape[0]
    tm = _row_tile(t, ROW_TILE_TARGET, CHUNK)
    cpt = tm // CHUNK
    scale = GLA_HK ** -0.5

    def body(z_ref, gw_ref, gb_ref, hg_ref, o_ref, mix_ref, st_ref, state):
        @pl.when(pl.program_id(0) == 0)
        def _():
            state[...] = jnp.zeros_like(state)

        tri = _tri(False)

        def chunk(ci, carry):
            r0 = pl.multiple_of(ci * CHUNK, CHUNK)
            rows = pl.ds(r0, CHUNK)
            _, cum, tot = _gate_decay(z_ref[rows, R0:R0 + GATE_PAD], gw_ref, gb_ref, tri)
            dec = jnp.exp(tot - cum)
            e = jnp.exp(tot)
            st_ref[ci] = state[...].astype(BF16)
            for hd in range(GLA_HEADS):
                ks = slice(hd * GLA_HK, (hd + 1) * GLA_HK)
                vs = slice(hd * GLA_HV, (hd + 1) * GLA_HV)
                kdec = (z_ref[rows, K0 + hd * GLA_HK:K0 + (hd + 1) * GLA_HK].astype(F32) * dec[:, ks]).astype(BF16)
                v = z_ref[rows, V0 + hd * GLA_HV:V0 + (hd + 1) * GLA_HV]
                st = state[vs, :] * e[:, ks] + _dot_tn(v, kdec)
                state[vs, :] = st
                q = z_ref[rows, Q0 + hd * GLA_HK:Q0 + (hd + 1) * GLA_HK]
                o = _dot_nt(q, st.astype(BF16)) * scale
                ob = o.astype(BF16)
                o_ref[rows, vs] = ob
                on = _rms(ob.astype(F32), hg_ref[...])
                gv = z_ref[rows, G0 + hd * GLA_HV:G0 + (hd + 1) * GLA_HV].astype(F32)
                mix_ref[rows, vs] = (on * _silu(gv)).astype(BF16)
            return carry

        lax.fori_loop(0, cpt, chunk, 0, unroll=GLA_UNROLL)

    return pl.pallas_call(
        body, grid=(t // tm,),
        in_specs=[pl.BlockSpec((tm, GLA_IN_PAD), lambda i: (i, 0)),
                  pl.BlockSpec((GATE_PAD, GLA_DK), lambda i: (0, 0)), pl.BlockSpec((1, GLA_DK), lambda i: (0, 0)),
                  pl.BlockSpec((1, GLA_HV), lambda i: (0, 0))],
        out_specs=[pl.BlockSpec((tm, GLA_DV), lambda i: (i, 0)), pl.BlockSpec((tm, GLA_DV), lambda i: (i, 0)),
                   pl.BlockSpec((cpt, GLA_DV, GLA_HK), lambda i: (i, 0, 0))],
        out_shape=[jax.ShapeDtypeStruct((t, GLA_DV), BF16), jax.ShapeDtypeStruct((t, GLA_DV), BF16),
                   jax.ShapeDtypeStruct((t // CHUNK, GLA_DV, GLA_HK), BF16)],
        scratch_shapes=[pltpu.VMEM((GLA_DV, GLA_HK), F32)],
        compiler_params=_cparams("arbitrary"), name="gla_seq_fwd")(z, gate_w, gate_b, head_g)


def _gla_seq_bwd(dmix, o, z, states, gate_w, gate_b, head_g, dep=None):
    t = z.shape[0]
    tm = _row_tile(t, ROW_TILE_TARGET, CHUNK)
    cpt = tm // CHUNK
    nt = t // tm
    scale = GLA_HK ** -0.5

    def body(dmix_ref, o_ref, z_ref, st_ref, gw_ref, gb_ref, hg_ref, dz_ref, dgw_ref, dgb_ref, dhg_ref, dstate):
        @pl.when(pl.program_id(0) == 0)
        def _():
            dstate[...] = jnp.zeros_like(dstate)
            dgw_ref[...] = jnp.zeros_like(dgw_ref)
            dgb_ref[...] = jnp.zeros_like(dgb_ref)
            dhg_ref[...] = jnp.zeros_like(dhg_ref)

        tri = _tri(False)
        tri_strict = _tri(True)

        def chunk(cj, carry):
            ci = cpt - 1 - cj
            r0 = pl.multiple_of(ci * CHUNK, CHUNK)
            rows = pl.ds(r0, CHUNK)
            r = z_ref[rows, R0:R0 + GATE_PAD]
            pre, cum, tot = _gate_decay(r, gw_ref, gb_ref, tri)
            dec = jnp.exp(tot - cum)
            e = jnp.exp(tot)
            dlac_parts = []
            dhg = jnp.zeros((1, GLA_HV), F32)
            for hd in range(GLA_HEADS):
                ks = slice(hd * GLA_HK, (hd + 1) * GLA_HK)
                vs = slice(hd * GLA_HV, (hd + 1) * GLA_HV)
                kcols = slice(K0 + hd * GLA_HK, K0 + (hd + 1) * GLA_HK)
                vcols = slice(V0 + hd * GLA_HV, V0 + (hd + 1) * GLA_HV)
                qcols = slice(Q0 + hd * GLA_HK, Q0 + (hd + 1) * GLA_HK)
                gcols = slice(G0 + hd * GLA_HV, G0 + (hd + 1) * GLA_HV)
                ov = o_ref[rows, vs].astype(F32)
                gv = z_ref[rows, gcols].astype(F32)
                dm = dmix_ref[rows, vs].astype(F32)
                sg = jax.nn.sigmoid(gv)
                rr = lax.rsqrt(jnp.mean(ov * ov, axis=-1, keepdims=True) + EPS)
                xhat = ov * rr
                don = dm * (gv * sg)
                dz_ref[rows, gcols] = (dm * (xhat * hg_ref[...]) * (sg * (1.0 + gv * (1.0 - sg)))).astype(BF16)
                dhg = dhg + jnp.sum(don * xhat, axis=0, keepdims=True)
                dxh = don * hg_ref[...]
                do = (rr * (dxh - xhat * jnp.mean(dxh * xhat, axis=-1, keepdims=True)) * scale).astype(BF16)
                kdec = z_ref[rows, kcols].astype(F32) * dec[:, ks]
                kdb = kdec.astype(BF16)
                v = z_ref[rows, vcols]
                q = z_ref[rows, qcols]
                st_prev = st_ref[ci, vs, :].astype(F32)
                st = st_prev * e[:, ks] + _dot_tn(v, kdb)
                dz_ref[rows, qcols] = _dot(do, st.astype(BF16)).astype(BF16)
                dst = dstate[vs, :] + _dot_tn(do, q)
                dstb = dst.astype(BF16)
                dkdec = _dot(v, dstb)
                dz_ref[rows, vcols] = _dot_nt(kdb, dstb).astype(BF16)
                dtot = jnp.sum(dst * st_prev, axis=0, keepdims=True) * e[:, ks]
                dstate[vs, :] = dst * e[:, ks]
                dz_ref[rows, kcols] = (dkdec * dec[:, ks]).astype(BF16)
                gk = dkdec * kdec
                ghi, gmid, _ = _split3(gk)
                dlac_parts.append(dtot + _dot(tri_strict, ghi) + _dot(tri_strict, gmid))
            dlac = jnp.concatenate(dlac_parts, axis=1)
            dpre = dlac * (1.0 / GATE_NORM) * (1.0 - jax.nn.sigmoid(pre))
            dpb = dpre.astype(BF16)
            dz_ref[rows, R0:R0 + GATE_PAD] = _dot_nt(dpb, gw_ref[...]).astype(BF16)
            dgw_ref[...] += _dot_tn(r, dpb)
            dgb_ref[...] += jnp.sum(dpre, axis=0, keepdims=True)
            dhg_ref[...] += dhg
            return carry

        lax.fori_loop(0, cpt, chunk, 0)

    rev = lambda i: (nt - 1 - i, 0)
    return _call_after(
        dep, body, 7,
        [pl.BlockSpec((tm, GLA_DV), rev), pl.BlockSpec((tm, GLA_DV), rev), pl.BlockSpec((tm, GLA_IN_PAD), rev),
         pl.BlockSpec((cpt, GLA_DV, GLA_HK), lambda i: (nt - 1 - i, 0, 0)),
         pl.BlockSpec((GATE_PAD, GLA_DK), lambda i: (0, 0)), pl.BlockSpec((1, GLA_DK), lambda i: (0, 0)),
         pl.BlockSpec((1, GLA_HV), lambda i: (0, 0))],
        (dmix, o, z, states, gate_w, gate_b, head_g), grid=(nt,),
        out_specs=[pl.BlockSpec((tm, GLA_IN_PAD), rev), pl.BlockSpec((GATE_PAD, GLA_DK), lambda i: (0, 0)),
                   pl.BlockSpec((1, GLA_DK), lambda i: (0, 0)), pl.BlockSpec((1, GLA_HV), lambda i: (0, 0))],
        out_shape=[jax.ShapeDtypeStruct((t, GLA_IN_PAD), BF16), jax.ShapeDtypeStruct((GATE_PAD, GLA_DK), F32),
                   jax.ShapeDtypeStruct((1, GLA_DK), F32), jax.ShapeDtypeStruct((1, GLA_HV), F32)],
        scratch_shapes=[pltpu.VMEM((GLA_DV, GLA_HK), F32)],
        compiler_params=_cparams("arbitrary"), name="gla_seq_bwd")


def _sum_halves(g, recv, c_idx, name):
    n, r, cdim = g.shape
    h = r // 2
    tr = _row_tile(h, 256, 8)
    nh = h // tr

    def body(c_ref, g_ref, r_ref, o_ref):
        o_ref[...] = (g_ref[...] + r_ref[...]).astype(BF16)

    return pl.pallas_call(
        body,
        grid_spec=pltpu.PrefetchScalarGridSpec(
            num_scalar_prefetch=1, grid=(n, nh),
            in_specs=[pl.BlockSpec((None, tr, cdim), lambda s, i, c: (s, c[0] * nh + i, 0)),
                      pl.BlockSpec((None, tr, cdim), lambda s, i, c: (s, i, 0))],
            out_specs=pl.BlockSpec((None, tr, cdim), lambda s, i, c: (s, i, 0))),
        out_shape=jax.ShapeDtypeStruct((n, h, cdim), BF16),
        compiler_params=_cparams("parallel", "parallel"), name=name)(c_idx, g, recv)


def _sum_slots(x, name):
    n, r, cdim = x.shape
    tr = _row_tile(r, 256, 8)

    def body(x_ref, o_ref):
        acc = x_ref[0].astype(F32)
        for j in range(1, n):
            acc = acc + x_ref[j].astype(F32)
        o_ref[...] = acc

    return pl.pallas_call(
        body, grid=(r // tr,),
        in_specs=[pl.BlockSpec((n, tr, cdim), lambda i: (0, i, 0))],
        out_specs=pl.BlockSpec((tr, cdim), lambda i: (i, 0)),
        out_shape=jax.ShapeDtypeStruct((r, cdim), F32),
        compiler_params=_cparams("parallel"), name=name)(x)


def _sum_own_and_slots(own, slots, chip_idx, name):
    n, r, cdim = own.shape
    tr = _row_tile(r, 256, 8)

    def body(s_ref, own_ref, a_ref, b_ref, c_ref, o_ref):
        o_ref[...] = (own_ref[...].astype(F32) + a_ref[...].astype(F32) + b_ref[...].astype(F32)
                      + c_ref[...].astype(F32))

    def slot(dd):
        return pl.BlockSpec((None, tr, cdim), lambda i, s: ((s[0] + dd) % n, i, 0))

    return pl.pallas_call(
        body,
        grid_spec=pltpu.PrefetchScalarGridSpec(
            num_scalar_prefetch=1, grid=(r // tr,), in_specs=[slot(0), slot(1), slot(2), slot(3)],
            out_specs=pl.BlockSpec((tr, cdim), lambda i, s: (i, 0))),
        out_shape=jax.ShapeDtypeStruct((r, cdim), F32),
        compiler_params=_cparams("parallel"), name=name)(chip_idx, own, slots, slots, slots)


def _add2(a, b, name):
    r, cdim = a.shape
    tr = _row_tile(r, 256, 8)

    def body(a_ref, b_ref, o_ref):
        o_ref[...] = a_ref[...] + b_ref[...]

    spec = pl.BlockSpec((tr, cdim), lambda i: (i, 0))
    return pl.pallas_call(body, grid=(r // tr,), in_specs=[spec, spec], out_specs=spec,
                          out_shape=jax.ShapeDtypeStruct((r, cdim), F32),
                          compiler_params=_cparams("parallel"), name=name)(a, b)


def _adamw(w, g, m, v, name):
    r, cdim = w.shape
    tr = _row_tile(r, 256, 8)

    def body(w_ref, g_ref, m_ref, v_ref, go_ref, d_ref, mo_ref, vo_ref):
        gv = g_ref[...]
        go_ref[...] = gv
        mn = ADAM_B1 * m_ref[...] + (1.0 - ADAM_B1) * gv
        vn = ADAM_B2 * v_ref[...] + (1.0 - ADAM_B2) * (gv * gv)
        m_hat = mn / (1.0 - ADAM_B1 ** ADAM_STEP)
        v_hat = vn / (1.0 - ADAM_B2 ** ADAM_STEP)
        d_ref[...] = -ADAM_LR * (m_hat / (jnp.sqrt(v_hat) + ADAM_EPS) + ADAM_WD * w_ref[...])
        mo_ref[...] = mn
        vo_ref[...] = vn

    spec = pl.BlockSpec((tr, cdim), lambda i: (i, 0))
    shp = jax.ShapeDtypeStruct((r, cdim), F32)
    return pl.pallas_call(body, grid=(r // tr,), in_specs=[spec] * 4, out_specs=[spec] * 4,
                          out_shape=[shp] * 4, compiler_params=_cparams("parallel"), name=name)(w, g, m, v)


def _split_rows(a):
    return a.reshape(a.shape[0], 2, a.shape[1] // 2, a.shape[2])


def _place():
    x, y, c = lax.axis_index("x"), lax.axis_index("y"), lax.axis_index("c")
    chips = [(1 - x, y), (x, 1 - y), (1 - x, 1 - y)]
    return x, y, c, chips


def _remote(src, dst, send_sem, recv_sem, to):
    return pltpu.make_async_remote_copy(src_ref=src, dst_ref=dst, send_sem=send_sem, recv_sem=recv_sem,
                                        device_id=to, device_id_type=MESH)


def _plan_gather(src_refs, land_refs):
    x, y, c, chips = _place()
    me = 2 * x + y
    return [(src.at[c], land.at[me, c], (px, py, c), land.at[2 * px + py, c])
            for src, land in zip(src_refs, land_refs) for (px, py) in chips]


def _plan_scatter(n_parts):
    def plan(src_refs, land_refs):
        x, y, c, chips = _place()
        me = 2 * x + y
        copies = []
        for k, (src, land) in enumerate(zip(src_refs, land_refs)):
            for (px, py) in chips:
                to = 2 * px + py
                copies.append((src.at[to] if k < n_parts else src, land.at[me], (px, py, c), land.at[to]))
        return copies
    return plan


def _plan_exchange(n_split):
    def plan(src_refs, land_refs):
        x, y, c, _ = _place()
        sib = (x, y, 1 - c)
        return [(src.at[:, 1 - c] if k < n_split else src, land, sib, land)
                for k, (src, land) in enumerate(zip(src_refs, land_refs))]
    return plan


def _hbm(a):
    return pltpu.HBM(a.shape, a.dtype)


def _start_copies(name, srcs, lands, plan, ncopy, dep=None):
    ns, nl = len(srcs), len(lands)
    nin = ns + nl + (0 if dep is None else 1)

    def body(*refs):
        send_sems, recv_sems, token = refs[nin], refs[nin + 1], refs[-1]
        for k, (src, dst, dev, _) in enumerate(plan(refs[:ns], refs[ns:ns + nl])):
            _remote(src, dst, send_sems.at[k], recv_sems.at[k], dev).start()
        token[...] = jnp.zeros_like(token)

    args = [pltpu.with_memory_space_constraint(a, pltpu.HBM) for a in list(srcs) + list(lands)]
    outs = pl.pallas_call(
        body, name=name,
        out_shape=(pltpu.SemaphoreType.DMA((ncopy,)), pltpu.SemaphoreType.DMA((ncopy,)),
                   *[_hbm(a) for a in list(srcs) + list(lands)], jax.ShapeDtypeStruct((8, 128), F32)),
        in_specs=[HBM_SPEC] * (ns + nl) + ([] if dep is None else [ANY_SPEC]),
        out_specs=(SEM_SPEC, SEM_SPEC, *([HBM_SPEC] * (ns + nl)), pl.BlockSpec(memory_space=pltpu.VMEM)),
        input_output_aliases={i: 2 + i for i in range(ns + nl)},
        compiler_params=pltpu.CompilerParams(has_side_effects=SIDE_EFFECT),
    )(*args, *([] if dep is None else [dep]))
    return outs[0], outs[1], list(outs[2:2 + ns]), list(outs[2 + ns:2 + ns + nl]), outs[-1]


def _wait_copies(name, started, plan, after):
    send_sems, recv_sems, srcs, lands, _ = started
    ns, nl = len(srcs), len(lands)

    def body(*refs):
        send_ref, recv_ref = refs[ns + nl], refs[ns + nl + 1]
        for k, (src, _, dev, mine) in enumerate(plan(refs[:ns], refs[ns:ns + nl])):
            copy = _remote(src, mine, send_ref.at[k], recv_ref.at[k], dev)
            copy.wait_send()
            copy.wait_recv()

    outs = pl.pallas_call(
        body, name=name, out_shape=tuple(_hbm(a) for a in srcs + lands),
        in_specs=[HBM_SPEC] * (ns + nl) + [SEM_SPEC, SEM_SPEC, ANY_SPEC], out_specs=tuple([HBM_SPEC] * (ns + nl)),
        input_output_aliases={i: i for i in range(ns + nl)},
        compiler_params=pltpu.CompilerParams(has_side_effects=SIDE_EFFECT),
    )(*srcs, *lands, send_sems, recv_sems, after)
    return list(outs[:ns]), list(outs[ns:])


def _share_with_sibling(name, srcs, lands):
    n = len(srcs)

    def body(*refs):
        src_refs, land_refs, out_refs = refs[:n], refs[n:2 * n], refs[2 * n:3 * n]
        send_sem, recv_sem = refs[3 * n:]
        x, y, c, chips = _place()
        me = 2 * x + y
        sib = (x, y, 1 - c)
        sends, recvs = [], []
        for k in range(n):
            sems = (send_sem.at[4 * k], recv_sem.at[4 * k])
            sends.append(_remote(src_refs[k], out_refs[k].at[me], *sems, sib))
            recvs.append(_remote(src_refs[k], out_refs[k].at[me], *sems, sib))
            for j, (px, py) in enumerate(chips):
                frm = 2 * px + py
                sems = (send_sem.at[4 * k + 1 + j], recv_sem.at[4 * k + 1 + j])
                sends.append(_remote(land_refs[k].at[frm, c], out_refs[k].at[frm, c], *sems, sib))
                recvs.append(_remote(land_refs[k].at[frm, c], out_refs[k].at[frm, 1 - c], *sems, sib))
        for cp in sends:
            cp.start()
        for cp in recvs:
            cp.wait_recv()
        for cp in sends:
            cp.wait_send()

    return pl.pallas_call(
        body, name=name, in_specs=[HBM_SPEC] * (2 * n), out_specs=[HBM_SPEC] * n,
        out_shape=[jax.ShapeDtypeStruct(a.shape, a.dtype) for a in lands],
        input_output_aliases={n + k: k for k in range(n)},
        scratch_shapes=[pltpu.SemaphoreType.DMA((4 * n,)), pltpu.SemaphoreType.DMA((4 * n,))],
    )(*srcs, *lands)


def _gather_weights(bigs, small):
    nb = len(bigs)
    shapes = [b.shape for b in bigs]
    bigs = [_split_rows(b) for b in bigs]

    def body(*refs):
        big_in, small_in = refs[:nb], refs[nb]
        big_out, small_out = refs[nb + 1:2 * nb + 1], refs[2 * nb + 1]
        ici_send, ici_recv, d2d_send, d2d_recv, own_send, own_recv, loc_sem = refs[2 * nb + 2:]
        x, y, c, chips = _place()
        me = 2 * x + y
        sib = (x, y, 1 - c)

        def half(ref, k, which):
            return ref.at[:, which]

        local = [pltpu.make_async_copy(small_in, small_out.at[me], loc_sem.at[0])]
        for cp in local:
            cp.start()
        sends = [_remote(big_in[k], big_out[k].at[me], own_send.at[k], own_recv.at[k], sib) for k in range(nb)]
        for j, (px, py) in enumerate(chips):
            for k in range(nb):
                sends.append(_remote(half(big_in[k], k, c), half(big_out[k].at[me], k, c),
                                     ici_send.at[k * 3 + j], ici_recv.at[k * 3 + j], (px, py, c)))
            sends.append(_remote(small_in, small_out.at[me], ici_send.at[nb * 3 + j], ici_recv.at[nb * 3 + j], (px, py, c)))
        for cp in sends:
            cp.start()
        passed = []
        for j, (px, py) in enumerate(chips):
            frm = 2 * px + py
            for k in range(nb):
                landed = half(big_out[k].at[frm], k, c)
                _remote(landed, landed, ici_send.at[k * 3 + j], ici_recv.at[k * 3 + j], (px, py, c)).wait_recv()
                fwd = _remote(landed, landed, d2d_send.at[k * 3 + j], d2d_recv.at[k * 3 + j], sib)
                fwd.start()
                passed.append(fwd)
            _remote(small_in, small_out.at[frm], ici_send.at[nb * 3 + j], ici_recv.at[nb * 3 + j], (px, py, c)).wait_recv()
        for j, (px, py) in enumerate(chips):
            frm = 2 * px + py
            for k in range(nb):
                theirs = half(big_out[k].at[frm], k, 1 - c)
                _remote(theirs, theirs, d2d_send.at[k * 3 + j], d2d_recv.at[k * 3 + j], sib).wait_recv()
        for k in range(nb):
            _remote(big_in[k], big_out[k].at[me], own_send.at[k], own_recv.at[k], sib).wait_recv()
        for cp in sends + passed:
            cp.wait_send()
        for cp in local:
            cp.wait()

    out_shape = [jax.ShapeDtypeStruct((N_CHIPS,) + b.shape, b.dtype) for b in bigs]
    out_shape.append(jax.ShapeDtypeStruct((N_CHIPS,) + small.shape, small.dtype))
    outs = pl.pallas_call(
        body, in_specs=[HBM_SPEC] * (nb + 1), out_specs=[HBM_SPEC] * (nb + 1), out_shape=out_shape,
        scratch_shapes=[pltpu.SemaphoreType.DMA((3 * nb + 3,)), pltpu.SemaphoreType.DMA((3 * nb + 3,)),
                        pltpu.SemaphoreType.DMA((3 * nb,)), pltpu.SemaphoreType.DMA((3 * nb,)),
                        pltpu.SemaphoreType.DMA((nb,)), pltpu.SemaphoreType.DMA((nb,)),
                        pltpu.SemaphoreType.DMA((1,))],
        name="gather_weights")(*bigs, small)
    return [o.reshape((N_CHIPS,) + s) for o, s in zip(outs[:-1], shapes)] + [outs[-1]]


def _exchange_halves(grads, small, name):
    ng = len(grads)
    grads = [_split_rows(g) for g in grads]
    extra = [] if small is None else [small]
    nall = ng + len(extra)

    def body(*refs):
        ins, outs = refs[:nall], refs[nall:2 * nall]
        send_sem, recv_sem = refs[2 * nall:]
        x, y, c, _ = _place()
        sib = (x, y, 1 - c)
        copies = []
        for k in range(nall):
            src = ins[k].at[:, 1 - c] if k < ng else ins[k]
            copies.append(_remote(src, outs[k], send_sem.at[k], recv_sem.at[k], sib))
        for cp in copies:
            cp.start()
        for cp in copies:
            cp.wait_recv()
        for cp in copies:
            cp.wait_send()

    out_shape = [jax.ShapeDtypeStruct((g.shape[0], g.shape[2], g.shape[3]), g.dtype) for g in grads]
    out_shape += [jax.ShapeDtypeStruct(s.shape, s.dtype) for s in extra]
    return pl.pallas_call(
        body, in_specs=[HBM_SPEC] * nall, out_specs=[HBM_SPEC] * nall, out_shape=out_shape,
        scratch_shapes=[pltpu.SemaphoreType.DMA((nall,)), pltpu.SemaphoreType.DMA((nall,))],
        name=name)(*grads, *extra)


def _join_halves(reduced, dests, out_shapes, name):
    nr = len(reduced)

    def body(*refs):
        r_in = refs[:nr]
        outs = refs[nr:nr + len(out_shapes)]
        send_sem, recv_sem, back_send, back_recv = refs[nr + len(out_shapes):]
        x, y, c, _ = _place()
        sib = (x, y, 1 - c)
        sends, backs = [], []
        for k in range(nr):
            oi, layer = dests[k]
            sends.append(_remote(r_in[k], outs[oi].at[layer, c], send_sem.at[k], recv_sem.at[k], sib))
        for cp in sends:
            cp.start()
        for k in range(nr):
            oi, layer = dests[k]
            theirs = outs[oi].at[layer, 1 - c]
            _remote(r_in[k], theirs, send_sem.at[k], recv_sem.at[k], sib).wait_recv()
            back = _remote(theirs, theirs, back_send.at[k], back_recv.at[k], sib)
            back.start()
            backs.append(back)
        for k in range(nr):
            oi, layer = dests[k]
            mine = outs[oi].at[layer, c]
            _remote(mine, mine, back_send.at[k], back_recv.at[k], sib).wait_recv()
        for cp in sends + backs:
            cp.wait_send()

    split = [(s[0], 2, s[1] // 2, s[2]) for s in out_shapes]
    outs = pl.pallas_call(
        body, in_specs=[HBM_SPEC] * nr, out_specs=[HBM_SPEC] * len(out_shapes),
        out_shape=[jax.ShapeDtypeStruct(s, F32) for s in split],
        scratch_shapes=[pltpu.SemaphoreType.DMA((nr,)), pltpu.SemaphoreType.DMA((nr,)),
                        pltpu.SemaphoreType.DMA((nr,)), pltpu.SemaphoreType.DMA((nr,))],
        name=name)(*reduced)
    return [o.reshape(s) for o, s in zip(outs, out_shapes)]


def _pack(arrs):
    flat = jnp.concatenate([a.reshape(-1).astype(F32) for a in arrs])
    n = flat.shape[0]
    rows = -(-n // PACK_WIDTH)
    rows = -(-rows // 8) * 8
    return jnp.pad(flat, (0, rows * PACK_WIDTH - n)).reshape(rows, PACK_WIDTH)


def _unpack(buf, shapes):
    flat = buf.reshape(-1)
    out, off = [], 0
    for shp in shapes:
        n = 1
        for s in shp:
            n *= s
        out.append(flat[off:off + n].reshape(shp))
        off += n
    return out


def _unshard_cols(stacked):
    moved = jnp.moveaxis(stacked, 0, -2)
    return moved.reshape(moved.shape[:-2] + (moved.shape[-2] * moved.shape[-1],))


def _col_shard(full, s, width):
    return lax.dynamic_slice_in_dim(full, s * width, width, axis=full.ndim - 1)


def kernel(x, meta_tokens, mix_norm_g, ffn_norm_g, ffn_w1, ffn_w2, cp_w_in, cp_conv_w, cp_conv_b, cp_ln_g, cp_ln_b, cp_pool_w, cp_pool_scale, cp_w_out, gla_w_in, gla_gate_w2, gla_gate_b, gla_head_g, gla_w_out, final_norm_g, loss_target, m_meta_tokens, m_mix_norm_g, m_ffn_norm_g, m_ffn_w1, m_ffn_w2, m_cp_w_in, m_cp_conv_w, m_cp_conv_b, m_cp_ln_g, m_cp_ln_b, m_cp_pool_w, m_cp_pool_scale, m_cp_w_out, m_gla_w_in, m_gla_gate_w2, m_gla_gate_b, m_gla_head_g, m_gla_w_out, m_final_norm_g, v_meta_tokens, v_mix_norm_g, v_ffn_norm_g, v_ffn_w1, v_ffn_w2, v_cp_w_in, v_cp_conv_w, v_cp_conv_b, v_cp_ln_g, v_cp_ln_b, v_cp_pool_w, v_cp_pool_scale, v_cp_w_out, v_gla_w_in, v_gla_gate_w2, v_gla_gate_b, v_gla_head_g, v_gla_w_out, v_final_norm_g):
    d = D_MODEL
    chip = 2 * lax.axis_index("x") + lax.axis_index("y")
    core = lax.axis_index("c")
    seq = x.shape[1]
    t = seq + CHUNK

    big_w = [ffn_w1, ffn_w2, cp_w_in, cp_w_out, gla_w_in, gla_w_out]
    sharded_small = [meta_tokens, cp_conv_w, gla_gate_w2, gla_gate_b, gla_head_g]
    cpin_g, cpout_g, small_g = _gather_weights([cp_w_in.astype(BF16), cp_w_out.astype(BF16)], _pack(sharded_small))

    def halves(w):
        return w.astype(BF16).reshape(2, w.shape[0] // 2, w.shape[1])

    def start_gather(name, srcs, dep):
        lands = [lax.empty((N_CHIPS,) + s.shape, s.dtype) for s in srcs]
        return _start_copies(name, srcs, lands, _plan_gather, 3 * len(srcs), dep)

    def finish_gather(name, started, after):
        srcs, lands = _wait_copies(name + "_wait", started, _plan_gather, after)
        return [g.reshape(N_CHIPS, 2 * g.shape[2], g.shape[3]) for g in _share_with_sibling(name + "_share", srcs, lands)]

    ffn0_started = start_gather("gather_ffn0_start", [halves(ffn_w1[0]), halves(ffn_w2[0])], small_g)
    gla_started = start_gather("gather_gla_start", [halves(gla_w_in[0]), halves(gla_w_out[0])], ffn0_started[-1])
    ffn1_started = start_gather("gather_ffn1_start", [halves(ffn_w1[1]), halves(ffn_w2[1])], gla_started[-1])
    per_chip = [_unpack(small_g[j], [a.shape for a in sharded_small]) for j in range(N_CHIPS)]
    meta_f, conv_w_f, gate_w_f, gate_b_f, head_g_f = [
        jnp.concatenate([per_chip[j][i] for j in range(N_CHIPS)], axis=-1) for i in range(len(sharded_small))]
    conv_w_f, gate_w_f = conv_w_f[0], gate_w_f[0]
    w_cp_in = _unshard_cols(cpin_g[:, 0])
    w_cp_out = cpout_g.reshape(CONV_DIM + POOL_DIM, d)
    gate_w_pad = jnp.pad(gate_w_f, ((0, GATE_PAD - GATE_RANK), (0, 0))).astype(BF16)
    row = lambda a: a.reshape(1, -1)
    c_idx = core.reshape(1).astype(jnp.int32)
    chip_idx = chip.reshape(1).astype(jnp.int32)

    h0 = jnp.concatenate([jnp.zeros((PAD_ROWS, d), F32), meta_f, x[0]], axis=0)
    z0, u0 = _norm_matmul(h0, row(mix_norm_g[0]), w_cp_in, 512, "cp_in_proj", dep=ffn1_started[-1])
    c0, pm0, mix0 = _cp_seq_fwd(z0, conv_w_f, cp_conv_b, cp_ln_g, cp_ln_b, cp_pool_w[0], cp_pool_scale)
    h1 = _matmul_residual(mix0, w_cp_out, h0, "cp_out_proj")
    w1g0, w2g0 = finish_gather("gather_ffn0", ffn0_started, h1)
    h2, hp0, uf0 = _ffn_fwd(h1, row(ffn_norm_g[0]), w1g0, w2g0, "ffn0_fwd")
    glain_g, glaout_g = finish_gather("gather_gla", gla_started, h2)
    w_gla_in = jnp.pad(_unshard_cols(glain_g), ((0, 0), (0, GLA_IN_PAD - GLA_IN)))
    w_gla_out = glaout_g.reshape(GLA_DV, d)
    z1, u2 = _norm_matmul(h2, row(mix_norm_g[1]), w_gla_in, 640, "gla_in_proj")
    o1, mix1, states = _gla_seq_fwd(z1, gate_w_pad, gate_b_f, head_g_f)
    h3 = _matmul_residual(mix1, w_gla_out, h2, "gla_out_proj")
    w1g1, w2g1 = finish_gather("gather_ffn1", ffn1_started, h3)
    h4, hp1, uf1 = _ffn_fwd(h3, row(ffn_norm_g[1]), w1g1, w2g1, "ffn1_fwd")

    def start_exchange(name, grads):
        srcs = [_split_rows(g) for g in grads]
        lands = [lax.empty((g.shape[0], g.shape[1] // 2, g.shape[2]), g.dtype) for g in grads]
        return _start_copies(name + "_exchange_start", srcs, lands, _plan_exchange(len(grads)), len(grads))

    def start_scatter(name, exchange, after):
        srcs, recv = _wait_copies(name + "_exchange_wait", exchange, _plan_exchange(len(exchange[2])), after)
        parts = [_sum_halves(g.reshape(g.shape[0], -1, g.shape[3]), r, c_idx, "%s_chip_sum_%d" % (name, k))
                 for k, (g, r) in enumerate(zip(srcs, recv))]
        lands = [lax.empty(p.shape, p.dtype) for p in parts]
        return _start_copies(name + "_scatter_start", parts, lands, _plan_scatter(len(parts)), 3 * len(parts))

    def finish_reduce(name, started, after):
        n = len(started[2])
        parts, lands = _wait_copies(name + "_scatter_wait", started, _plan_scatter(n), after)
        return [_sum_own_and_slots(p, s, chip_idx, "%s_slot_sum_%d" % (name, k)) for k, (p, s) in enumerate(zip(parts, lands))]

    dh4, d_final_g, loss_part = _loss_bwd(h4, row(final_norm_g), loss_target[0])

    dh3, dhp1, d_ffn_g1 = _ffn_bwd_data(dh4, h3, row(ffn_norm_g[1]), hp1, w1g1, w2g1, "ffn1_bwd")
    dw1_1 = _wgrad(uf1, dhp1, N_CHIPS, d, d, False, True, False, "ffn1_dw1")
    dw2_1 = _wgrad(hp1, dh4, N_CHIPS, d, d, True, False, True, "ffn1_dw2")
    ffn1_exchange = start_exchange("ffn1", [dw1_1, dw2_1])

    dmix1 = _dgrad(dh3, w_gla_out, "gla_out_dgrad", dep=ffn1_exchange[-1])
    dw_gla_out = _wgrad(mix1, dh3, 1, GLA_DV, d, False, False, False, "gla_out_dw")
    ffn1_reduce = start_scatter("ffn1", ffn1_exchange, dw_gla_out)
    dz1, d_gate_w, d_gate_b, d_head_g = _gla_seq_bwd(dmix1, o1, z1, states, gate_w_pad, gate_b_f, head_g_f,
                                                     dep=ffn1_reduce[-1])
    dh2, d_mix_g1 = _dgrad_norm_bwd(dz1, w_gla_in, h2, row(mix_norm_g[1]), dh3, 640, "gla_in_dgrad")
    dw_gla_in = _wgrad(u2, dz1, GLA_IN_PAD // 640, d, 640, False, True, False, "gla_in_dw")
    gla_in_cols = jnp.moveaxis(dw_gla_in, 0, 1).reshape(d, GLA_IN_PAD)[:, :GLA_IN]
    gla_exchange = start_exchange("gla", [jnp.moveaxis(gla_in_cols.reshape(d, N_CHIPS, GLA_IN // N_CHIPS), 1, 0),
                                          dw_gla_out.reshape(N_CHIPS, -1, d)])

    dh1, dhp0, d_ffn_g0 = _ffn_bwd_data(dh2, h1, row(ffn_norm_g[0]), hp0, w1g0, w2g0, "ffn0_bwd", dep=gla_exchange[-1])
    gla_reduce = start_scatter("gla", gla_exchange, dh1)
    dw1_0 = _wgrad(uf0, dhp0, N_CHIPS, d, d, False, True, False, "ffn0_dw1", dep=gla_reduce[-1])
    dw2_0 = _wgrad(hp0, dh2, N_CHIPS, d, d, True, False, True, "ffn0_dw2")
    ffn0_exchange = start_exchange("ffn0", [dw1_0, dw2_0])

    dmix0 = _dgrad(dh1, w_cp_out, "cp_out_dgrad", dep=ffn0_exchange[-1])
    dw_cp_out = _wgrad(mix0, dh1, 1, CONV_DIM + POOL_DIM, d, False, False, False, "cp_out_dw")
    ffn0_reduce = start_scatter("ffn0", ffn0_exchange, dw_cp_out)
    dz0, d_conv_w, d_cp_vec, d_pool_w = _cp_seq_bwd(dmix0, z0, c0, pm0, conv_w_f, cp_ln_g, cp_ln_b, cp_pool_w[0],
                                                    cp_pool_scale, dep=ffn0_reduce[-1])
    dh0, d_mix_g0 = _dgrad_norm_bwd(dz0, w_cp_in, h0, row(mix_norm_g[0]), dh1, 512, "cp_in_dgrad")
    dw_cp_in = _wgrad(u0, dz0, N_CHIPS, d, CP_IN // N_CHIPS, False, True, False, "cp_in_dw")

    grad_x = dh0[CHUNK:][None]

    cp_grads = [dw_cp_in, dw_cp_out.reshape(N_CHIPS, -1, d)]
    small_full = [dh0[PAD_ROWS:CHUNK], jnp.concatenate([d_mix_g0, d_mix_g1], axis=0),
                  jnp.concatenate([d_ffn_g0, d_ffn_g1], axis=0), d_conv_w[:CONV_WIDTH][None],
                  d_cp_vec[0:1], d_cp_vec[1:2], d_cp_vec[2:3], d_pool_w[None], d_cp_vec[3:4],
                  d_gate_w[:GATE_RANK][None], d_gate_b, d_head_g, d_final_g[0], loss_part[0, 0:1]]
    small_mine = _pack(small_full)
    recv = _exchange_halves(cp_grads, small_mine, "cp_exchange")
    chip_sums = [_sum_halves(g, r, c_idx, "cp_chip_sum_%d" % k) for k, (g, r) in enumerate(zip(cp_grads, recv[:-1]))]
    small_chip = _add2(small_mine, recv[-1], "chip_sum_small")
    small_slots = lax.dynamic_update_slice(jnp.zeros((N_CHIPS,) + small_chip.shape, F32), small_chip[None], (chip, 0, 0))
    cp_lands = [lax.empty(p.shape, p.dtype) for p in chip_sums] + [small_slots]
    cp_reduce = _start_copies("cp_scatter_start", chip_sums + [small_chip], cp_lands, _plan_scatter(len(chip_sums)),
                              3 * (len(chip_sums) + 1))

    def adamw_big(names, grads):
        outs = {}
        for n, g in zip(names, grads):
            w, m, v = big[n]
            two_d = lambda a: a.reshape(-1, a.shape[-1])
            res = _adamw(two_d(w), two_d(g), two_d(m), two_d(v), "adamw_" + n)
            outs[n] = [o.reshape(w.shape) for o in res]
        return outs

    big = {"w1": (ffn_w1, m_ffn_w1, v_ffn_w1), "w2": (ffn_w2, m_ffn_w2, v_ffn_w2),
           "cp_in": (cp_w_in, m_cp_w_in, v_cp_w_in), "cp_out": (cp_w_out, m_cp_w_out, v_cp_w_out),
           "gla_in": (gla_w_in, m_gla_w_in, v_gla_w_in), "gla_out": (gla_w_out, m_gla_w_out, v_gla_w_out)}
    red_ffn1 = finish_reduce("ffn1", ffn1_reduce, cp_reduce[-1])
    red_gla = finish_reduce("gla", gla_reduce, cp_reduce[-1])
    red_ffn0 = finish_reduce("ffn0", ffn0_reduce, cp_reduce[-1])
    first = ["w1", "w2", "gla_in", "gla_out"]
    first_grads = _join_halves([red_ffn0[0], red_ffn1[0], red_ffn0[1], red_ffn1[1], red_gla[0], red_gla[1]],
                               [(0, 0), (0, 1), (1, 0), (1, 1), (2, 0), (3, 0)], [big[n][0].shape for n in first],
                               "join_halves_ffn_gla")
    big_out = adamw_big(first, first_grads)
    cp_parts, cp_slots = _wait_copies("cp_scatter_wait", cp_reduce, _plan_scatter(len(chip_sums)), big_out["gla_out"][1])
    red_cp = [_sum_own_and_slots(a, s, chip_idx, "cp_slot_sum_%d" % k)
              for k, (a, s) in enumerate(zip(cp_parts[:-1], cp_slots[:-1]))]
    small_red = _sum_slots(cp_slots[-1], "slot_sum_small")
    last = ["cp_in", "cp_out"]
    last_grads = _join_halves(red_cp, [(0, 0), (1, 0)], [big[n][0].shape for n in last], "join_halves_cp")
    big_out.update(adamw_big(last, last_grads))

    (g_meta, g_mix, g_ffn, g_conv_w, g_conv_b, g_ln_g, g_ln_b, g_pool_w, g_pool_scale, g_gate_w, g_gate_b, g_head,
     g_final, loss_sum) = _unpack(small_red, [a.shape for a in small_full])
    g_meta = _col_shard(g_meta, chip, meta_tokens.shape[-1])
    g_conv_w = _col_shard(g_conv_w, chip, cp_conv_w.shape[-1])
    g_gate_w = _col_shard(g_gate_w, chip, gla_gate_w2.shape[-1])
    g_gate_b = _col_shard(g_gate_b, chip, gla_gate_b.shape[-1])
    g_head = _col_shard(g_head, chip, gla_head_g.shape[-1])
    small_w = [meta_tokens, mix_norm_g, ffn_norm_g, cp_conv_w, cp_conv_b, cp_ln_g, cp_ln_b, cp_pool_w, cp_pool_scale,
               gla_gate_w2, gla_gate_b, gla_head_g, final_norm_g]
    small_m = [m_meta_tokens, m_mix_norm_g, m_ffn_norm_g, m_cp_conv_w, m_cp_conv_b, m_cp_ln_g, m_cp_ln_b, m_cp_pool_w,
               m_cp_pool_scale, m_gla_gate_w2, m_gla_gate_b, m_gla_head_g, m_final_norm_g]
    small_v = [v_meta_tokens, v_mix_norm_g, v_ffn_norm_g, v_cp_conv_w, v_cp_conv_b, v_cp_ln_g, v_cp_ln_b, v_cp_pool_w,
               v_cp_pool_scale, v_gla_gate_w2, v_gla_gate_b, v_gla_head_g, v_final_norm_g]
    small_g = [g_meta, g_mix, g_ffn, g_conv_w, g_conv_b, g_ln_g, g_ln_b, g_pool_w, g_pool_scale, g_gate_w, g_gate_b,
               g_head, g_final]
    shapes = [w.shape for w in small_w]
    small_g = [g.reshape(s) for g, s in zip(small_g, shapes)]
    _, s_delta, s_m, s_v = _adamw(_pack(small_w), _pack(small_g), _pack(small_m), _pack(small_v), "adamw_small")
    s_delta, s_m, s_v = _unpack(s_delta, shapes), _unpack(s_m, shapes), _unpack(s_v, shapes)

    order = ["meta", "mix", "ffn", "w1", "w2", "cp_in", "conv_w", "conv_b", "ln_g", "ln_b", "pool_w", "pool_scale",
             "cp_out", "gla_in", "gate_w", "gate_b", "head", "gla_out", "final"]
    small_names = ["meta", "mix", "ffn", "conv_w", "conv_b", "ln_g", "ln_b", "pool_w", "pool_scale", "gate_w", "gate_b",
                   "head", "final"]
    big_names = ["w1", "w2", "cp_in", "cp_out", "gla_in", "gla_out"]
    table = {n: (small_g[i], s_delta[i], s_m[i], s_v[i]) for i, n in enumerate(small_names)}
    table.update({n: tuple(big_out[n]) for n in big_names})
    loss = loss_sum.reshape(())
    return (loss, grad_x, *[table[n][0] for n in order], *[table[n][1] for n in order],
            *[table[n][2] for n in order], *[table[n][3] for n in order])
```

```python
import functools

import jax
import jax.numpy as jnp
from jax import lax
from jax.experimental import pallas as pl
from jax.experimental.pallas import tpu as pltpu

F32 = jnp.float32
BF16 = jnp.bfloat16

D_MODEL = 1024
N_META = 16
CHUNK = 64
PAD_ROWS = CHUNK - N_META
EPS = 1e-5
CONV_DIM = 512
CONV_WIDTH = 31
CONV_HALO = 32
POOL_DIM = 512
POOL_WINDOWS = (2, 4, 8, 16)
POOL_GROUP = 128
POOL_HALO = 16
CP_IN = 2 * CONV_DIM + POOL_DIM
GLA_HEADS = 4
GLA_DK = 512
GLA_DV = 1024
GLA_HK = GLA_DK // GLA_HEADS
GLA_HV = GLA_DV // GLA_HEADS
GATE_RANK = 16
GATE_PAD = 128
GATE_NORM = 16.0
GLA_IN = 2 * GLA_DK + 2 * GLA_DV + GATE_RANK
GLA_IN_PAD = 2 * GLA_DK + 2 * GLA_DV + GATE_PAD
N_CHIPS = 4
ADAM_LR = 0.001
ADAM_B1 = 0.9
ADAM_B2 = 0.999
ADAM_EPS = 1e-08
ADAM_WD = 0.01
ADAM_STEP = 10

VMEM_LIMIT_BYTES = 56 * 1024 * 1024
ROW_TILE_TARGET = 832
TOKEN_TILE_TARGET = 1040
PACK_WIDTH = 1024
MESH = pl.DeviceIdType.MESH
HBM_SPEC = pl.BlockSpec(memory_space=pltpu.HBM)
ANY_SPEC = pl.BlockSpec(memory_space=pl.ANY)
SEM_SPEC = pl.BlockSpec(memory_space=pltpu.SEMAPHORE)
SIDE_EFFECT = pltpu.SideEffectType.DATAFLOW_SIDE_EFFECTING


def _cparams(*sem):
    return pltpu.CompilerParams(dimension_semantics=sem, vmem_limit_bytes=VMEM_LIMIT_BYTES)


def _row_tile(t, target, mult):
    best = mult
    for cand in range(mult, min(t, target) + 1, mult):
        if t % cand == 0:
            best = cand
    assert t % best == 0, (t, best)
    return best


def _rms(h, g):
    return h * lax.rsqrt(jnp.mean(h * h, axis=-1, keepdims=True) + EPS) * g


def _rms_bwd(h, g, du):
    r = lax.rsqrt(jnp.mean(h * h, axis=-1, keepdims=True) + EPS)
    xhat = h * r
    dxh = du * g
    dh = r * (dxh - xhat * jnp.mean(dxh * xhat, axis=-1, keepdims=True))
    return dh, du * xhat


def _valid_rows(i, tm):
    row = i * tm + lax.broadcasted_iota(jnp.int32, (tm, 1), 0)
    return row >= PAD_ROWS


def _dot(a, b):
    return jnp.dot(a, b, preferred_element_type=F32)


def _dot_nt(a, b):
    return lax.dot_general(a, b, (((1,), (1,)), ((), ())), preferred_element_type=F32)


def _dot_tn(a, b):
    return lax.dot_general(a, b, (((0,), (0,)), ((), ())), preferred_element_type=F32)


def _accumulate(ref, val, first):
    @pl.when(first)
    def _():
        ref[...] = val

    @pl.when(jnp.logical_not(first))
    def _():
        ref[...] += val


def _call_after(dep, body, n_in, in_specs, args, **kw):
    if dep is None:
        return pl.pallas_call(body, in_specs=in_specs, **kw)(*args)

    def with_dep(*refs):
        body(*refs[:n_in], *refs[n_in + 1:])

    return pl.pallas_call(with_dep, in_specs=list(in_specs) + [ANY_SPEC], **kw)(*args, dep)


def _norm_matmul(h, g, w, nc, name, dep=None):
    t, d = h.shape
    n = w.shape[1]
    tm = _row_tile(t, TOKEN_TILE_TARGET, 16)

    def body(h_ref, g_ref, w_ref, z_ref, u_ref):
        u = _rms(h_ref[...], g_ref[...]).astype(BF16)
        u_ref[...] = u
        for n0 in range(0, n, nc):
            z_ref[:, n0:n0 + nc] = _dot(u, w_ref[:, n0:n0 + nc]).astype(BF16)

    return _call_after(
        dep, body, 3,
        [pl.BlockSpec((tm, d), lambda i: (i, 0)), pl.BlockSpec((1, d), lambda i: (0, 0)),
         pl.BlockSpec((d, n), lambda i: (0, 0))], (h, g, w), grid=(t // tm,),
        out_specs=[pl.BlockSpec((tm, n), lambda i: (i, 0)), pl.BlockSpec((tm, d), lambda i: (i, 0))],
        out_shape=[jax.ShapeDtypeStruct((t, n), BF16), jax.ShapeDtypeStruct((t, d), BF16)],
        compiler_params=_cparams("parallel"), name=name)


def _matmul_residual(a, w, h, name):
    t, k = a.shape
    d = w.shape[1]
    tm = _row_tile(t, TOKEN_TILE_TARGET, 16)

    def body(a_ref, w_ref, h_ref, o_ref):
        o_ref[...] = h_ref[...] + _dot(a_ref[...], w_ref[...])

    return pl.pallas_call(
        body, grid=(t // tm,),
        in_specs=[pl.BlockSpec((tm, k), lambda i: (i, 0)), pl.BlockSpec((k, d), lambda i: (0, 0)),
                  pl.BlockSpec((tm, d), lambda i: (i, 0))],
        out_specs=pl.BlockSpec((tm, d), lambda i: (i, 0)),
        out_shape=jax.ShapeDtypeStruct((t, d), F32),
        compiler_params=_cparams("parallel"), name=name)(a, w, h)


def _ffn_fwd(h, g, w1g, w2g, name):
    t, d = h.shape
    ns, ffs = w1g.shape[0], w1g.shape[2]
    tm = _row_tile(t, TOKEN_TILE_TARGET, 16)

    def body(h_ref, g_ref, w1_ref, w2_ref, ho_ref, hp_ref, u_ref, acc_ref):
        s = pl.program_id(1)

        @pl.when(s == 0)
        def _():
            u_ref[...] = _rms(h_ref[...], g_ref[...]).astype(BF16)

        hp = _dot(u_ref[...], w1_ref[...])
        hp_ref[...] = hp.astype(BF16)
        a = jnp.maximum(hp, 0.0)
        _accumulate(acc_ref, _dot((a * a).astype(BF16), w2_ref[...]), s == 0)

        @pl.when(s == ns - 1)
        def _():
            ho_ref[...] = h_ref[...] + acc_ref[...]

    return pl.pallas_call(
        body, grid=(t // tm, ns),
        in_specs=[pl.BlockSpec((tm, d), lambda i, s: (i, 0)), pl.BlockSpec((1, d), lambda i, s: (0, 0)),
                  pl.BlockSpec((None, d, ffs), lambda i, s: (s, 0, 0)),
                  pl.BlockSpec((None, ffs, d), lambda i, s: (s, 0, 0))],
        out_specs=[pl.BlockSpec((tm, d), lambda i, s: (i, 0)), pl.BlockSpec((tm, ffs), lambda i, s: (i, s)),
                   pl.BlockSpec((tm, d), lambda i, s: (i, 0))],
        out_shape=[jax.ShapeDtypeStruct((t, d), F32), jax.ShapeDtypeStruct((t, ns * ffs), BF16),
                   jax.ShapeDtypeStruct((t, d), BF16)],
        scratch_shapes=[pltpu.VMEM((tm, d), F32)],
        compiler_params=_cparams("parallel", "arbitrary"), name=name)(h, g, w1g, w2g)


def _ffn_bwd_data(dh, h, g, hp, w1g, w2g, name, dep=None):
    t, d = h.shape
    ns, ffs = w1g.shape[0], w1g.shape[2]
    tm = _row_tile(t, ROW_TILE_TARGET, CHUNK)

    def body(dh_ref, h_ref, g_ref, hp_ref, w1_ref, w2_ref, dhi_ref, dhp_ref, dg_ref, acc_ref):
        i, s = pl.program_id(0), pl.program_id(1)
        da = _dot_nt(dh_ref[...].astype(BF16), w2_ref[...])
        dhp = (da * (2.0 * jnp.maximum(hp_ref[...].astype(F32), 0.0))).astype(BF16)
        dhp_ref[...] = dhp
        _accumulate(acc_ref, _dot_nt(dhp, w1_ref[...]), s == 0)

        @pl.when(s == ns - 1)
        def _():
            dhn, dgr = _rms_bwd(h_ref[...], g_ref[...], acc_ref[...])
            dhi_ref[...] = jnp.where(_valid_rows(i, tm), dh_ref[...] + dhn, 0.0)
            _accumulate(dg_ref, jnp.sum(dgr, axis=0, keepdims=True), i == 0)

    return _call_after(
        dep, body, 6,
        [pl.BlockSpec((tm, d), lambda i, s: (i, 0)), pl.BlockSpec((tm, d), lambda i, s: (i, 0)),
         pl.BlockSpec((1, d), lambda i, s: (0, 0)), pl.BlockSpec((tm, ffs), lambda i, s: (i, s)),
         pl.BlockSpec((None, d, ffs), lambda i, s: (s, 0, 0)),
         pl.BlockSpec((None, ffs, d), lambda i, s: (s, 0, 0))], (dh, h, g, hp, w1g, w2g), grid=(t // tm, ns),
        out_specs=[pl.BlockSpec((tm, d), lambda i, s: (i, 0)), pl.BlockSpec((tm, ffs), lambda i, s: (i, s)),
                   pl.BlockSpec((1, d), lambda i, s: (0, 0))],
        out_shape=[jax.ShapeDtypeStruct((t, d), F32), jax.ShapeDtypeStruct((t, ns * ffs), BF16),
                   jax.ShapeDtypeStruct((1, d), F32)],
        scratch_shapes=[pltpu.VMEM((tm, d), F32)],
        compiler_params=_cparams("arbitrary", "arbitrary"), name=name)


WGRAD_ROWS = 1024


def _wgrad(x, dy, nb, xc, yc, x_by_block, dy_by_block, relu2, name, dep=None):
    t = x.shape[0]
    tk = _row_tile(t - CHUNK, WGRAD_ROWS, CHUNK)

    def prep(xv):
        if relu2:
            xv = jnp.maximum(xv.astype(F32), 0.0)
            xv = xv * xv
        return xv.astype(BF16)

    def body(xh_ref, dyh_ref, x_ref, dy_ref, o_ref):
        k = pl.program_id(1)
        p = _dot_tn(prep(x_ref[...]), dy_ref[...].astype(BF16))

        @pl.when(k == 0)
        def _():
            o_ref[...] = p + _dot_tn(prep(xh_ref[...]), dyh_ref[...].astype(BF16))

        @pl.when(k > 0)
        def _():
            o_ref[...] += p

    def head(width, by_block):
        return pl.BlockSpec((CHUNK, width), (lambda b, k: (0, b)) if by_block else (lambda b, k: (0, 0)))

    def rest(width, by_block):
        def index(b, k):
            return pl.multiple_of(CHUNK + k * tk, CHUNK), (pl.multiple_of(b * width, 128) if by_block else 0)
        return pl.BlockSpec((pl.Element(tk), pl.Element(width)), index)

    return _call_after(
        dep, body, 4,
        [head(xc, x_by_block), head(yc, dy_by_block), rest(xc, x_by_block), rest(yc, dy_by_block)], (x, dy, x, dy),
        grid=(nb, (t - CHUNK) // tk),
        out_specs=pl.BlockSpec((None, xc, yc), lambda b, k: (b, 0, 0)),
        out_shape=jax.ShapeDtypeStruct((nb, xc, yc), F32),
        compiler_params=_cparams("parallel", "arbitrary"), name=name)


def _dgrad(dh, w, name, dep=None):
    t, d = dh.shape
    k = w.shape[0]
    tm = _row_tile(t, TOKEN_TILE_TARGET, 16)

    def body(dh_ref, w_ref, o_ref):
        o_ref[...] = _dot_nt(dh_ref[...].astype(BF16), w_ref[...]).astype(BF16)

    return _call_after(
        dep, body, 2,
        [pl.BlockSpec((tm, d), lambda i: (i, 0)), pl.BlockSpec((k, d), lambda i: (0, 0))], (dh, w), grid=(t // tm,),
        out_specs=pl.BlockSpec((tm, k), lambda i: (i, 0)),
        out_shape=jax.ShapeDtypeStruct((t, k), BF16),
        compiler_params=_cparams("parallel"), name=name)


def _dgrad_norm_bwd(dz, w, h, g, dh, nc, name):
    t, d = h.shape
    n = w.shape[1]
    tm = _row_tile(t, ROW_TILE_TARGET // 2, 16)

    def body(dz_ref, w_ref, h_ref, g_ref, dh_ref, dhi_ref, dg_ref):
        i = pl.program_id(0)
        du = jnp.zeros((tm, d), F32)
        for n0 in range(0, n, nc):
            du = du + _dot_nt(dz_ref[:, n0:n0 + nc], w_ref[:, n0:n0 + nc])
        dhn, dgr = _rms_bwd(h_ref[...], g_ref[...], du)
        dhi_ref[...] = jnp.where(_valid_rows(i, tm), dh_ref[...] + dhn, 0.0)
        _accumulate(dg_ref, jnp.sum(dgr, axis=0, keepdims=True), i == 0)

    return pl.pallas_call(
        body, grid=(t // tm,),
        in_specs=[pl.BlockSpec((tm, n), lambda i: (i, 0)), pl.BlockSpec((d, n), lambda i: (0, 0)),
                  pl.BlockSpec((tm, d), lambda i: (i, 0)), pl.BlockSpec((1, d), lambda i: (0, 0)),
                  pl.BlockSpec((tm, d), lambda i: (i, 0))],
        out_specs=[pl.BlockSpec((tm, d), lambda i: (i, 0)), pl.BlockSpec((1, d), lambda i: (0, 0))],
        out_shape=[jax.ShapeDtypeStruct((t, d), F32), jax.ShapeDtypeStruct((1, d), F32)],
        compiler_params=_cparams("arbitrary"), name=name)(dz, w, h, g, dh)


def _loss_bwd(h, g, target):
    t, d = h.shape
    tl = _row_tile(t - CHUNK, 1024, CHUNK)

    def body(h_ref, g_ref, t_ref, dh_ref, dg_ref, loss_ref):
        i = pl.program_id(0)
        hv, gv = h_ref[...], g_ref[...]
        err = _rms(hv, gv) - t_ref[...]
        part = 0.5 * jnp.sum(jnp.mean(err * err, axis=-1, keepdims=True), axis=0, keepdims=True)
        dhn, dgr = _rms_bwd(hv, gv, err * (1.0 / d))
        dh_ref[...] = dhn
        _accumulate(dg_ref, jnp.sum(dgr, axis=0, keepdims=True), i == 0)
        _accumulate(loss_ref, jnp.broadcast_to(part, (8, 128)), i == 0)

    shifted = pl.BlockSpec((pl.Element(tl), pl.Element(d)), lambda i: (pl.multiple_of(CHUNK + i * tl, CHUNK), 0))
    dh, dg, loss = pl.pallas_call(
        body, grid=((t - CHUNK) // tl,),
        in_specs=[shifted, pl.BlockSpec((1, d), lambda i: (0, 0)), pl.BlockSpec((tl, d), lambda i: (i, 0))],
        out_specs=[shifted, pl.BlockSpec((1, d), lambda i: (0, 0)), pl.BlockSpec((8, 128), lambda i: (0, 0))],
        out_shape=[jax.ShapeDtypeStruct((t, d), F32), jax.ShapeDtypeStruct((1, d), F32),
                   jax.ShapeDtypeStruct((8, 128), F32)],
        compiler_params=_cparams("arbitrary"), name="loss_bwd")(h, g, target)

    def zero_head(dh_ref, o_ref):
        o_ref[...] = jnp.zeros_like(o_ref)

    dh = pl.pallas_call(
        zero_head, grid=(1,), in_specs=[ANY_SPEC], out_specs=pl.BlockSpec((CHUNK, d), lambda i: (0, 0)),
        out_shape=jax.ShapeDtypeStruct((t, d), F32), input_output_aliases={0: 0}, name="loss_bwd_head")(dh)
    return dh, dg, loss


CONV_BLOCK = 32


def _silu(x):
    return x * jax.nn.sigmoid(x)


def _cp_seq_fwd(z, conv_w, conv_b, ln_g, ln_b, pool_w, pool_scale):
    t = z.shape[0]
    tm = _row_tile(t, ROW_TILE_TARGET, CHUNK)

    def body(z_ref, cw_ref, cb_ref, lg_ref, lb_ref, pw_ref, ps_ref, c_ref, pm_ref, mix_ref, gbuf, pbuf):
        i = pl.program_id(0)

        @pl.when(i == 0)
        def _():
            gbuf[0:CONV_HALO, :] = jnp.zeros((CONV_HALO, CONV_DIM), F32)
            pbuf[0:POOL_HALO, :] = jnp.zeros((POOL_HALO, POOL_DIM), F32)

        @pl.when(i > 0)
        def _():
            gbuf[0:CONV_HALO, :] = gbuf[tm:tm + CONV_HALO, :]
            pbuf[0:POOL_HALO, :] = pbuf[tm:tm + POOL_HALO, :]

        av = z_ref[:, 0:CONV_DIM].astype(F32)
        ag = z_ref[:, CONV_DIM:2 * CONV_DIM].astype(F32)
        gbuf[CONV_HALO:CONV_HALO + tm, :] = av * jax.nn.sigmoid(ag)
        pbuf[POOL_HALO:POOL_HALO + tm, :] = z_ref[:, 2 * CONV_DIM:CP_IN].astype(F32)

        def conv_block(rb, carry):
            base = pl.multiple_of(rb * CONV_BLOCK, CONV_BLOCK)
            win = gbuf[pl.ds(base, CONV_BLOCK + CONV_HALO), :]
            acc = jnp.zeros((CONV_BLOCK, CONV_DIM), F32)
            for k in range(CONV_WIDTH):
                off = CONV_HALO - (CONV_WIDTH - 1) + k
                acc = acc + cw_ref[k:k + 1, :] * win[off:off + CONV_BLOCK, :]
            c_ref[pl.ds(base, CONV_BLOCK), :] = acc + cb_ref[...]
            return carry

        lax.fori_loop(0, tm // CONV_BLOCK, conv_block, 0)

        c = c_ref[...]
        mu = jnp.mean(c, axis=-1, keepdims=True)
        xc = c - mu
        ln = xc * lax.rsqrt(jnp.mean(xc * xc, axis=-1, keepdims=True) + EPS) * lg_ref[...] + lb_ref[...]
        row = i * tm + lax.broadcasted_iota(jnp.int32, (tm, 1), 0)
        mix_ref[:, 0:CONV_DIM] = jnp.where(row >= PAD_ROWS, _silu(ln), 0.0).astype(BF16)

        tpos = (row - PAD_ROWS + 1).astype(F32)
        for gi, wdw in enumerate(POOL_WINDOWS):
            lo = POOL_GROUP * gi
            cur = pbuf[POOL_HALO:POOL_HALO + tm, lo:lo + POOL_GROUP]
            sacc = cur
            for j in range(1, wdw):
                sacc = sacc + pbuf[POOL_HALO - j:POOL_HALO - j + tm, lo:lo + POOL_GROUP]
            pm = (sacc / jnp.clip(tpos, 1.0, float(wdw)) - cur).astype(BF16)
            pm_ref[:, lo:lo + POOL_GROUP] = pm
            pg = _dot(pm, pw_ref[gi].astype(BF16))
            mix_ref[:, CONV_DIM + lo:CONV_DIM + lo + POOL_GROUP] = (pg * ps_ref[:, lo:lo + POOL_GROUP]).astype(BF16)

    vec = pl.BlockSpec((1, CONV_DIM), lambda i: (0, 0))
    return pl.pallas_call(
        body, grid=(t // tm,),
        in_specs=[pl.BlockSpec((tm, CP_IN), lambda i: (i, 0)),
                  pl.BlockSpec((CONV_WIDTH, CONV_DIM), lambda i: (0, 0)), vec, vec, vec,
                  pl.BlockSpec((len(POOL_WINDOWS), POOL_GROUP, POOL_GROUP), lambda i: (0, 0, 0)), vec],
        out_specs=[pl.BlockSpec((tm, CONV_DIM), lambda i: (i, 0)), pl.BlockSpec((tm, POOL_DIM), lambda i: (i, 0)),
                   pl.BlockSpec((tm, CONV_DIM + POOL_DIM), lambda i: (i, 0))],
        out_shape=[jax.ShapeDtypeStruct((t, CONV_DIM), F32), jax.ShapeDtypeStruct((t, POOL_DIM), BF16),
                   jax.ShapeDtypeStruct((t, CONV_DIM + POOL_DIM), BF16)],
        scratch_shapes=[pltpu.VMEM((tm + CONV_HALO, CONV_DIM), F32), pltpu.VMEM((tm + POOL_HALO, POOL_DIM), F32)],
        compiler_params=_cparams("arbitrary"), name="cp_seq_fwd")(z, conv_w, conv_b, ln_g, ln_b, pool_w, pool_scale)


def _cp_seq_bwd(dmix, z, c, pm, conv_w, ln_g, ln_b, pool_w, pool_scale, dep=None):
    t = z.shape[0]
    tm = _row_tile(t, ROW_TILE_TARGET, CHUNK)
    nt = t // tm

    def body(dmix_ref, z_ref, c_ref, pm_ref, cw_ref, lg_ref, lb_ref, pw_ref, ps_ref,
             dz_ref, dcw_ref, dvec_ref, dpw_ref, dcbuf, qbuf, glu_buf, dwacc):
        i = pl.program_id(0)
        tile = nt - 1 - i

        @pl.when(i == 0)
        def _():
            dcbuf[tm:tm + CONV_HALO, :] = jnp.zeros((CONV_HALO, CONV_DIM), F32)
            qbuf[tm:tm + POOL_HALO, :] = jnp.zeros((POOL_HALO, POOL_DIM), F32)
            dcw_ref[...] = jnp.zeros_like(dcw_ref)
            dwacc[...] = jnp.zeros_like(dwacc)
            dvec_ref[...] = jnp.zeros_like(dvec_ref)
            dpw_ref[...] = jnp.zeros_like(dpw_ref)

        @pl.when(i > 0)
        def _():
            dcbuf[tm:tm + CONV_HALO, :] = dcbuf[0:CONV_HALO, :]
            qbuf[tm:tm + POOL_HALO, :] = qbuf[0:POOL_HALO, :]

        row = tile * tm + lax.broadcasted_iota(jnp.int32, (tm, 1), 0)
        cv = c_ref[...]
        mu = jnp.mean(cv, axis=-1, keepdims=True)
        xc = cv - mu
        rstd = lax.rsqrt(jnp.mean(xc * xc, axis=-1, keepdims=True) + EPS)
        xhat = xc * rstd
        ln = xhat * lg_ref[...] + lb_ref[...]
        sg = jax.nn.sigmoid(ln)
        da = jnp.where(row >= PAD_ROWS, dmix_ref[:, 0:CONV_DIM].astype(F32), 0.0)
        dln = da * (sg * (1.0 + ln * (1.0 - sg)))
        dxh = dln * lg_ref[...]
        dc = rstd * (dxh - jnp.mean(dxh, axis=-1, keepdims=True) - xhat * jnp.mean(dxh * xhat, axis=-1, keepdims=True))
        dcbuf[0:tm, :] = dc
        dvec_ref[0:1, :] += jnp.sum(dc, axis=0, keepdims=True)
        dvec_ref[1:2, :] += jnp.sum(dln * xhat, axis=0, keepdims=True)
        dvec_ref[2:3, :] += jnp.sum(dln, axis=0, keepdims=True)

        av = z_ref[:, 0:CONV_DIM].astype(F32)
        sig_g = jax.nn.sigmoid(z_ref[:, CONV_DIM:2 * CONV_DIM].astype(F32))
        glu_buf[...] = av * sig_g

        def conv_block(rb, carry):
            base = pl.multiple_of(rb * CONV_BLOCK, CONV_BLOCK)
            win = dcbuf[pl.ds(base, CONV_BLOCK + CONV_HALO), :]
            glu = glu_buf[pl.ds(base, CONV_BLOCK), :]
            acc = jnp.zeros((CONV_BLOCK, CONV_DIM), F32)
            for k in range(CONV_WIDTH):
                off = CONV_WIDTH - 1 - k
                slab = win[off:off + CONV_BLOCK, :]
                acc = acc + cw_ref[k:k + 1, :] * slab
                prod = slab * glu
                part = prod[0:8]
                for q in range(1, CONV_BLOCK // 8):
                    part = part + prod[8 * q:8 * q + 8]
                dwacc[k] += part
            glu_buf[pl.ds(base, CONV_BLOCK), :] = acc
            return carry

        lax.fori_loop(0, tm // CONV_BLOCK, conv_block, 0)

        @pl.when(i == nt - 1)
        def _():
            for k in range(CONV_WIDTH):
                dcw_ref[k:k + 1, :] = jnp.sum(dwacc[k], axis=0, keepdims=True)
        dglu = glu_buf[...]
        dz_ref[:, 0:CONV_DIM] = (dglu * sig_g).astype(BF16)
        dz_ref[:, CONV_DIM:2 * CONV_DIM] = (dglu * av * sig_g * (1.0 - sig_g)).astype(BF16)

        tpos = (row - PAD_ROWS + 1).astype(F32)
        for gi, wdw in enumerate(POOL_WINDOWS):
            lo = POOL_GROUP * gi
            dp = dmix_ref[:, CONV_DIM + lo:CONV_DIM + lo + POOL_GROUP].astype(F32)
            pmv = pm_ref[:, lo:lo + POOL_GROUP]
            pwb = pw_ref[gi].astype(BF16)
            dvec_ref[3:4, lo:lo + POOL_GROUP] += jnp.sum(dp * _dot(pmv, pwb), axis=0, keepdims=True)
            dq = (dp * ps_ref[:, lo:lo + POOL_GROUP]).astype(BF16)
            dpw_ref[gi] += _dot_tn(pmv, dq)
            dpm = _dot_nt(dq, pwb)
            qbuf[0:tm, lo:lo + POOL_GROUP] = dpm / jnp.clip(tpos, 1.0, float(wdw))
            sacc = -dpm
            for j in range(wdw):
                sacc = sacc + qbuf[j:j + tm, lo:lo + POOL_GROUP]
            dz_ref[:, 2 * CONV_DIM + lo:2 * CONV_DIM + lo + POOL_GROUP] = sacc.astype(BF16)

    vec = pl.BlockSpec((1, CONV_DIM), lambda i: (0, 0))
    rev = lambda i: (nt - 1 - i, 0)
    return _call_after(
        dep, body, 9,
        [pl.BlockSpec((tm, CONV_DIM + POOL_DIM), rev), pl.BlockSpec((tm, CP_IN), rev),
         pl.BlockSpec((tm, CONV_DIM), rev), pl.BlockSpec((tm, POOL_DIM), rev),
         pl.BlockSpec((CONV_WIDTH, CONV_DIM), lambda i: (0, 0)), vec, vec,
         pl.BlockSpec((len(POOL_WINDOWS), POOL_GROUP, POOL_GROUP), lambda i: (0, 0, 0)), vec],
        (dmix, z, c, pm, conv_w, ln_g, ln_b, pool_w, pool_scale), grid=(nt,),
        out_specs=[pl.BlockSpec((tm, CP_IN), rev), pl.BlockSpec((CONV_WIDTH + 1, CONV_DIM), lambda i: (0, 0)),
                   pl.BlockSpec((8, CONV_DIM), lambda i: (0, 0)),
                   pl.BlockSpec((len(POOL_WINDOWS), POOL_GROUP, POOL_GROUP), lambda i: (0, 0, 0))],
        out_shape=[jax.ShapeDtypeStruct((t, CP_IN), BF16), jax.ShapeDtypeStruct((CONV_WIDTH + 1, CONV_DIM), F32),
                   jax.ShapeDtypeStruct((8, CONV_DIM), F32),
                   jax.ShapeDtypeStruct((len(POOL_WINDOWS), POOL_GROUP, POOL_GROUP), F32)],
        scratch_shapes=[pltpu.VMEM((tm + CONV_HALO, CONV_DIM), F32), pltpu.VMEM((tm + POOL_HALO, POOL_DIM), F32),
                        pltpu.VMEM((tm, CONV_DIM), F32), pltpu.VMEM((CONV_WIDTH + 1, 8, CONV_DIM), F32)],
        compiler_params=_cparams("arbitrary"), name="cp_seq_bwd")


GLA_UNROLL = 2
Q0, K0, V0, G0, R0 =0, GLA_DK, 2 * GLA_DK, 2 * GLA_DK + GLA_DV, 2 * GLA_DK + 2 * GLA_DV


def _split3(x):
    hi = x.astype(BF16)
    r1 = x - hi.astype(F32)
    mid = r1.astype(BF16)
    lo = (r1 - mid.astype(F32)).astype(BF16)
    return hi, mid, lo


def _tri(strict):
    r = lax.broadcasted_iota(jnp.int32, (CHUNK, CHUNK), 0)
    c = lax.broadcasted_iota(jnp.int32, (CHUNK, CHUNK), 1)
    return ((r > c) if strict else (r >= c)).astype(BF16)


def _gate_decay(r, gw_ref, gb_ref, tri):
    pre = _dot(r, gw_ref[...]) + gb_ref[...]
    lac = (jnp.minimum(pre, 0.0) - jnp.log(1.0 + jnp.exp(-jnp.abs(pre)))) * (1.0 / GATE_NORM)
    hi, mid, lo = _split3(lac)
    cum = _dot(tri, hi) + _dot(tri, mid) + _dot(tri, lo)
    return pre, cum, cum[CHUNK - 1:CHUNK, :]


def _gla_seq_fwd(z, gate_w, gate_b, head_g):
    t = z.shape[0]
    tm = _row_tile(t, ROW_TILE_TARGET, CHUNK)
    cpt = tm // CHUNK
    scale = GLA_HK ** -0.5

    def body(z_ref, gw_ref, gb_ref, hg_ref, o_ref, mix_ref, st_ref, state):
        @pl.when(pl.program_id(0) == 0)
        def _():
            state[...] = jnp.zeros_like(state)

        tri = _tri(False)

        def chunk(ci, carry):
            r0 = pl.multiple_of(ci * CHUNK, CHUNK)
            rows = pl.ds(r0, CHUNK)
            _, cum, tot = _gate_decay(z_ref[rows, R0:R0 + GATE_PAD], gw_ref, gb_ref, tri)
            dec = jnp.exp(tot - cum)
            e = jnp.exp(tot)
            st_ref[ci] = state[...].astype(BF16)
            for hd in range(GLA_HEADS):
                ks = slice(hd * GLA_HK, (hd + 1) * GLA_HK)
                vs = slice(hd * GLA_HV, (hd + 1) * GLA_HV)
                kdec = (z_ref[rows, K0 + hd * GLA_HK:K0 + (hd + 1) * GLA_HK].astype(F32) * dec[:, ks]).astype(BF16)
                v = z_ref[rows, V0 + hd * GLA_HV:V0 + (hd + 1) * GLA_HV]
                st = state[vs, :] * e[:, ks] + _dot_tn(v, kdec)
                state[vs, :] = st
                q = z_ref[rows, Q0 + hd * GLA_HK:Q0 + (hd + 1) * GLA_HK]
                o = _dot_nt(q, st.astype(BF16)) * scale
                ob = o.astype(BF16)
                o_ref[rows, vs] = ob
                on = _rms(ob.astype(F32), hg_ref[...])
                gv = z_ref[rows, G0 + hd * GLA_HV:G0 + (hd + 1) * GLA_HV].astype(F32)
                mix_ref[rows, vs] = (on * _silu(gv)).astype(BF16)
            return carry

        lax.fori_loop(0, cpt, chunk, 0, unroll=GLA_UNROLL)

    return pl.pallas_call(
        body, grid=(t // tm,),
        in_specs=[pl.BlockSpec((tm, GLA_IN_PAD), lambda i: (i, 0)),
                  pl.BlockSpec((GATE_PAD, GLA_DK), lambda i: (0, 0)), pl.BlockSpec((1, GLA_DK), lambda i: (0, 0)),
                  pl.BlockSpec((1, GLA_HV), lambda i: (0, 0))],
        out_specs=[pl.BlockSpec((tm, GLA_DV), lambda i: (i, 0)), pl.BlockSpec((tm, GLA_DV), lambda i: (i, 0)),
                   pl.BlockSpec((cpt, GLA_DV, GLA_HK), lambda i: (i, 0, 0))],
        out_shape=[jax.ShapeDtypeStruct((t, GLA_DV), BF16), jax.ShapeDtypeStruct((t, GLA_DV), BF16),
                   jax.ShapeDtypeStruct((t // CHUNK, GLA_DV, GLA_HK), BF16)],
        scratch_shapes=[pltpu.VMEM((GLA_DV, GLA_HK), F32)],
        compiler_params=_cparams("arbitrary"), name="gla_seq_fwd")(z, gate_w, gate_b, head_g)


def _gla_seq_bwd(dmix, o, z, states, gate_w, gate_b, head_g, dep=None):
    t = z.shape[0]
    tm = _row_tile(t, ROW_TILE_TARGET, CHUNK)
    cpt = tm // CHUNK
    nt = t // tm
    scale = GLA_HK ** -0.5

    def body(dmix_ref, o_ref, z_ref, st_ref, gw_ref, gb_ref, hg_ref, dz_ref, dgw_ref, dgb_ref, dhg_ref,
             dstate, dec_s, kdec_s, dkdec_s, do_s, e_s, dtot_s):
        @pl.when(pl.program_id(0) == 0)
        def _():
            dstate[...] = jnp.zeros_like(dstate)
            dgw_ref[...] = jnp.zeros_like(dgw_ref)
            dgb_ref[...] = jnp.zeros_like(dgb_ref)
            dhg_ref[...] = jnp.zeros_like(dhg_ref)

        def chunk_sums(x, strict, pieces):
            tri3 = jnp.broadcast_to(_tri(strict)[None], (cpt, CHUNK, CHUNK))
            acc = None
            for piece in _split3(x.reshape(cpt, CHUNK, GLA_DK))[:pieces]:
                part = jnp.einsum("bij,bjk->bik", tri3, piece, preferred_element_type=F32)
                acc = part if acc is None else acc + part
            return acc

        def gate_pre():
            return _dot(z_ref[:, R0:R0 + GATE_PAD], gw_ref[...]) + gb_ref[...]

        pre = gate_pre()
        lac = (jnp.minimum(pre, 0.0) - jnp.log(1.0 + jnp.exp(-jnp.abs(pre)))) * (1.0 / GATE_NORM)
        cum3 = chunk_sums(lac, False, 3)
        tot3 = cum3[:, CHUNK - 1:CHUNK, :]
        dec = jnp.exp(jnp.broadcast_to(tot3, cum3.shape) - cum3).reshape(tm, GLA_DK)
        dec_s[...] = dec
        kdec_s[...] = z_ref[:, K0:K0 + GLA_DK].astype(F32) * dec
        e_s[...] = jnp.exp(jnp.broadcast_to(tot3, (cpt, 8, GLA_DK))).reshape(cpt * 8, GLA_DK)
        dhg = jnp.zeros((1, GLA_HV), F32)
        for hd in range(GLA_HEADS):
            vs = slice(hd * GLA_HV, (hd + 1) * GLA_HV)
            gcols = slice(G0 + hd * GLA_HV, G0 + (hd + 1) * GLA_HV)
            ov = o_ref[:, vs].astype(F32)
            gv = z_ref[:, gcols].astype(F32)
            dm = dmix_ref[:, vs].astype(F32)
            sg = jax.nn.sigmoid(gv)
            rr = lax.rsqrt(jnp.mean(ov * ov, axis=-1, keepdims=True) + EPS)
            xhat = ov * rr
            don = dm * (gv * sg)
            dz_ref[:, gcols] = (dm * (xhat * hg_ref[...]) * (sg * (1.0 + gv * (1.0 - sg)))).astype(BF16)
            dhg = dhg + jnp.sum(don * xhat, axis=0, keepdims=True)
            dxh = don * hg_ref[...]
            do_s[:, vs] = (rr * (dxh - xhat * jnp.mean(dxh * xhat, axis=-1, keepdims=True)) * scale).astype(BF16)
        dhg_ref[...] += dhg

        def chunk(cj, carry):
            ci = cpt - 1 - cj
            rows = pl.ds(pl.multiple_of(ci * CHUNK, CHUNK), CHUNK)
            erows = pl.ds(pl.multiple_of(ci * 8, 8), 8)
            e_all = e_s[erows, :][0:1, :]
            for hd in range(GLA_HEADS):
                ks = slice(hd * GLA_HK, (hd + 1) * GLA_HK)
                vs = slice(hd * GLA_HV, (hd + 1) * GLA_HV)
                e = e_all[:, ks]
                kdb = kdec_s[rows, ks].astype(BF16)
                v = z_ref[rows, V0 + hd * GLA_HV:V0 + (hd + 1) * GLA_HV]
                q = z_ref[rows, Q0 + hd * GLA_HK:Q0 + (hd + 1) * GLA_HK]
                do = do_s[rows, vs]
                st_prev = st_ref[ci, vs, :].astype(F32)
                st = st_prev * e + _dot_tn(v, kdb)
                dz_ref[rows, Q0 + hd * GLA_HK:Q0 + (hd + 1) * GLA_HK] = _dot(do, st.astype(BF16)).astype(BF16)
                dst = dstate[vs, :] + _dot_tn(do, q)
                dstb = dst.astype(BF16)
                dkdec_s[rows, ks] = _dot(v, dstb)
                dz_ref[rows, V0 + hd * GLA_HV:V0 + (hd + 1) * GLA_HV] = _dot_nt(kdb, dstb).astype(BF16)
                dtot = jnp.sum(dst * st_prev, axis=0, keepdims=True) * e
                dtot_s[erows, ks] = jnp.broadcast_to(dtot, (8, GLA_HK))
                dstate[vs, :] = dst * e
            return carry

        lax.fori_loop(0, cpt, chunk, 0)

        dkdec = dkdec_s[...]
        dz_ref[:, K0:K0 + GLA_DK] = (dkdec * dec_s[...]).astype(BF16)
        before = chunk_sums(dkdec * kdec_s[...], True, 2)
        dtot3 = dtot_s[...].reshape(cpt, 8, GLA_DK)[:, 0:1, :]
        dlac = (jnp.broadcast_to(dtot3, before.shape) + before).reshape(tm, GLA_DK)
        dpre = dlac * (1.0 / GATE_NORM) * (1.0 - jax.nn.sigmoid(gate_pre()))
        dpb = dpre.astype(BF16)
        dz_ref[:, R0:R0 + GATE_PAD] = _dot_nt(dpb, gw_ref[...]).astype(BF16)
        dgw_ref[...] += _dot_tn(z_ref[:, R0:R0 + GATE_PAD], dpb)
        dgb_ref[...] += jnp.sum(dpre, axis=0, keepdims=True)

    rev = lambda i: (nt - 1 - i, 0)
    return _call_after(
        dep, body, 7,
        [pl.BlockSpec((tm, GLA_DV), rev), pl.BlockSpec((tm, GLA_DV), rev), pl.BlockSpec((tm, GLA_IN_PAD), rev),
         pl.BlockSpec((cpt, GLA_DV, GLA_HK), lambda i: (nt - 1 - i, 0, 0)),
         pl.BlockSpec((GATE_PAD, GLA_DK), lambda i: (0, 0)), pl.BlockSpec((1, GLA_DK), lambda i: (0, 0)),
         pl.BlockSpec((1, GLA_HV), lambda i: (0, 0))],
        (dmix, o, z, states, gate_w, gate_b, head_g), grid=(nt,),
        out_specs=[pl.BlockSpec((tm, GLA_IN_PAD), rev), pl.BlockSpec((GATE_PAD, GLA_DK), lambda i: (0, 0)),
                   pl.BlockSpec((1, GLA_DK), lambda i: (0, 0)), pl.BlockSpec((1, GLA_HV), lambda i: (0, 0))],
        out_shape=[jax.ShapeDtypeStruct((t, GLA_IN_PAD), BF16), jax.ShapeDtypeStruct((GATE_PAD, GLA_DK), F32),
                   jax.ShapeDtypeStruct((1, GLA_DK), F32), jax.ShapeDtypeStruct((1, GLA_HV), F32)],
        scratch_shapes=[pltpu.VMEM((GLA_DV, GLA_HK), F32), pltpu.VMEM((tm, GLA_DK), F32), pltpu.VMEM((tm, GLA_DK), F32),
                        pltpu.VMEM((tm, GLA_DK), F32), pltpu.VMEM((tm, GLA_DV), BF16),
                        pltpu.VMEM((cpt * 8, GLA_DK), F32), pltpu.VMEM((cpt * 8, GLA_DK), F32)],
        compiler_params=_cparams("arbitrary"), name="gla_seq_bwd")


def _sum_halves(g, recv, c_idx, name):
    n, r, cdim = g.shape
    h = r // 2
    tr = _row_tile(h, 256, 8)
    nh = h // tr

    def body(c_ref, g_ref, r_ref, o_ref):
        o_ref[...] = (g_ref[...] + r_ref[...]).astype(BF16)

    return pl.pallas_call(
        body,
        grid_spec=pltpu.PrefetchScalarGridSpec(
            num_scalar_prefetch=1, grid=(n, nh),
            in_specs=[pl.BlockSpec((None, tr, cdim), lambda s, i, c: (s, c[0] * nh + i, 0)),
                      pl.BlockSpec((None, tr, cdim), lambda s, i, c: (s, i, 0))],
            out_specs=pl.BlockSpec((None, tr, cdim), lambda s, i, c: (s, i, 0))),
        out_shape=jax.ShapeDtypeStruct((n, h, cdim), BF16),
        compiler_params=_cparams("parallel", "parallel"), name=name)(c_idx, g, recv)


def _sum_slots(x, name):
    n, r, cdim = x.shape
    tr = _row_tile(r, 256, 8)

    def body(x_ref, o_ref):
        acc = x_ref[0].astype(F32)
        for j in range(1, n):
            acc = acc + x_ref[j].astype(F32)
        o_ref[...] = acc

    return pl.pallas_call(
        body, grid=(r // tr,),
        in_specs=[pl.BlockSpec((n, tr, cdim), lambda i: (0, i, 0))],
        out_specs=pl.BlockSpec((tr, cdim), lambda i: (i, 0)),
        out_shape=jax.ShapeDtypeStruct((r, cdim), F32),
        compiler_params=_cparams("parallel"), name=name)(x)


def _sum_own_and_slots(own, slots, chip_idx, name):
    n, r, cdim = own.shape
    tr = _row_tile(r, 256, 8)

    def body(s_ref, own_ref, a_ref, b_ref, c_ref, o_ref):
        o_ref[...] = (own_ref[...].astype(F32) + a_ref[...].astype(F32) + b_ref[...].astype(F32)
                      + c_ref[...].astype(F32))

    def slot(dd):
        return pl.BlockSpec((None, tr, cdim), lambda i, s: ((s[0] + dd) % n, i, 0))

    return pl.pallas_call(
        body,
        grid_spec=pltpu.PrefetchScalarGridSpec(
            num_scalar_prefetch=1, grid=(r // tr,), in_specs=[slot(0), slot(1), slot(2), slot(3)],
            out_specs=pl.BlockSpec((tr, cdim), lambda i, s: (i, 0))),
        out_shape=jax.ShapeDtypeStruct((r, cdim), F32),
        compiler_params=_cparams("parallel"), name=name)(chip_idx, own, slots, slots, slots)


def _add2(a, b, name):
    r, cdim = a.shape
    tr = _row_tile(r, 256, 8)

    def body(a_ref, b_ref, o_ref):
        o_ref[...] = a_ref[...] + b_ref[...]

    spec = pl.BlockSpec((tr, cdim), lambda i: (i, 0))
    return pl.pallas_call(body, grid=(r // tr,), in_specs=[spec, spec], out_specs=spec,
                          out_shape=jax.ShapeDtypeStruct((r, cdim), F32),
                          compiler_params=_cparams("parallel"), name=name)(a, b)


def _adamw(w, g, m, v, name):
    r, cdim = w.shape
    tr = _row_tile(r, 256, 8)

    def body(w_ref, g_ref, m_ref, v_ref, go_ref, d_ref, mo_ref, vo_ref):
        gv = g_ref[...]
        go_ref[...] = gv
        mn = ADAM_B1 * m_ref[...] + (1.0 - ADAM_B1) * gv
        vn = ADAM_B2 * v_ref[...] + (1.0 - ADAM_B2) * (gv * gv)
        m_hat = mn / (1.0 - ADAM_B1 ** ADAM_STEP)
        v_hat = vn / (1.0 - ADAM_B2 ** ADAM_STEP)
        d_ref[...] = -ADAM_LR * (m_hat / (jnp.sqrt(v_hat) + ADAM_EPS) + ADAM_WD * w_ref[...])
        mo_ref[...] = mn
        vo_ref[...] = vn

    spec = pl.BlockSpec((tr, cdim), lambda i: (i, 0))
    shp = jax.ShapeDtypeStruct((r, cdim), F32)
    return pl.pallas_call(body, grid=(r // tr,), in_specs=[spec] * 4, out_specs=[spec] * 4,
                          out_shape=[shp] * 4, compiler_params=_cparams("parallel"), name=name)(w, g, m, v)


def _split_rows(a):
    return a.reshape(a.shape[0], 2, a.shape[1] // 2, a.shape[2])


def _place():
    x, y, c = lax.axis_index("x"), lax.axis_index("y"), lax.axis_index("c")
    chips = [(1 - x, y), (x, 1 - y), (1 - x, 1 - y)]
    return x, y, c, chips


def _remote(src, dst, send_sem, recv_sem, to):
    return pltpu.make_async_remote_copy(src_ref=src, dst_ref=dst, send_sem=send_sem, recv_sem=recv_sem,
                                        device_id=to, device_id_type=MESH)


def _plan_gather(src_refs, land_refs):
    x, y, c, chips = _place()
    me = 2 * x + y
    return [(src.at[c], land.at[me, c], (px, py, c), land.at[2 * px + py, c])
            for src, land in zip(src_refs, land_refs) for (px, py) in chips]


def _plan_scatter(n_parts):
    def plan(src_refs, land_refs):
        x, y, c, chips = _place()
        me = 2 * x + y
        copies = []
        for k, (src, land) in enumerate(zip(src_refs, land_refs)):
            for (px, py) in chips:
                to = 2 * px + py
                copies.append((src.at[to] if k < n_parts else src, land.at[me], (px, py, c), land.at[to]))
        return copies
    return plan


def _plan_exchange(n_split):
    def plan(src_refs, land_refs):
        x, y, c, _ = _place()
        sib = (x, y, 1 - c)
        return [(src.at[:, 1 - c] if k < n_split else src, land, sib, land)
                for k, (src, land) in enumerate(zip(src_refs, land_refs))]
    return plan


def _hbm(a):
    return pltpu.HBM(a.shape, a.dtype)


def _start_copies(name, srcs, lands, plan, ncopy, dep=None):
    ns, nl = len(srcs), len(lands)
    nin = ns + nl + (0 if dep is None else 1)

    def body(*refs):
        send_sems, recv_sems, token = refs[nin], refs[nin + 1], refs[-1]
        for k, (src, dst, dev, _) in enumerate(plan(refs[:ns], refs[ns:ns + nl])):
            _remote(src, dst, send_sems.at[k], recv_sems.at[k], dev).start()
        token[...] = jnp.zeros_like(token)

    args = [pltpu.with_memory_space_constraint(a, pltpu.HBM) for a in list(srcs) + list(lands)]
    outs = pl.pallas_call(
        body, name=name,
        out_shape=(pltpu.SemaphoreType.DMA((ncopy,)), pltpu.SemaphoreType.DMA((ncopy,)),
                   *[_hbm(a) for a in list(srcs) + list(lands)], jax.ShapeDtypeStruct((8, 128), F32)),
        in_specs=[HBM_SPEC] * (ns + nl) + ([] if dep is None else [ANY_SPEC]),
        out_specs=(SEM_SPEC, SEM_SPEC, *([HBM_SPEC] * (ns + nl)), pl.BlockSpec(memory_space=pltpu.VMEM)),
        input_output_aliases={i: 2 + i for i in range(ns + nl)},
        compiler_params=pltpu.CompilerParams(has_side_effects=SIDE_EFFECT),
    )(*args, *([] if dep is None else [dep]))
    return outs[0], outs[1], list(outs[2:2 + ns]), list(outs[2 + ns:2 + ns + nl]), outs[-1]


def _wait_copies(name, started, plan, after):
    send_sems, recv_sems, srcs, lands, _ = started
    ns, nl = len(srcs), len(lands)

    def body(*refs):
        send_ref, recv_ref = refs[ns + nl], refs[ns + nl + 1]
        for k, (src, _, dev, mine) in enumerate(plan(refs[:ns], refs[ns:ns + nl])):
            copy = _remote(src, mine, send_ref.at[k], recv_ref.at[k], dev)
            copy.wait_send()
            copy.wait_recv()

    outs = pl.pallas_call(
        body, name=name, out_shape=tuple(_hbm(a) for a in srcs + lands),
        in_specs=[HBM_SPEC] * (ns + nl) + [SEM_SPEC, SEM_SPEC, ANY_SPEC], out_specs=tuple([HBM_SPEC] * (ns + nl)),
        input_output_aliases={i: i for i in range(ns + nl)},
        compiler_params=pltpu.CompilerParams(has_side_effects=SIDE_EFFECT),
    )(*srcs, *lands, send_sems, recv_sems, after)
    return list(outs[:ns]), list(outs[ns:])


def _share_with_sibling(name, srcs, lands):
    n = len(srcs)

    def body(*refs):
        src_refs, land_refs, out_refs = refs[:n], refs[n:2 * n], refs[2 * n:3 * n]
        send_sem, recv_sem = refs[3 * n:]
        x, y, c, chips = _place()
        me = 2 * x + y
        sib = (x, y, 1 - c)
        sends, recvs = [], []
        for k in range(n):
            sems = (send_sem.at[4 * k], recv_sem.at[4 * k])
            sends.append(_remote(src_refs[k], out_refs[k].at[me], *sems, sib))
            recvs.append(_remote(src_refs[k], out_refs[k].at[me], *sems, sib))
            for j, (px, py) in enumerate(chips):
                frm = 2 * px + py
                sems = (send_sem.at[4 * k + 1 + j], recv_sem.at[4 * k + 1 + j])
                sends.append(_remote(land_refs[k].at[frm, c], out_refs[k].at[frm, c], *sems, sib))
                recvs.append(_remote(land_refs[k].at[frm, c], out_refs[k].at[frm, 1 - c], *sems, sib))
        for cp in sends:
            cp.start()
        for cp in recvs:
            cp.wait_recv()
        for cp in sends:
            cp.wait_send()

    return pl.pallas_call(
        body, name=name, in_specs=[HBM_SPEC] * (2 * n), out_specs=[HBM_SPEC] * n,
        out_shape=[jax.ShapeDtypeStruct(a.shape, a.dtype) for a in lands],
        input_output_aliases={n + k: k for k in range(n)},
        scratch_shapes=[pltpu.SemaphoreType.DMA((4 * n,)), pltpu.SemaphoreType.DMA((4 * n,))],
    )(*srcs, *lands)


def _gather_weights(bigs, small):
    nb = len(bigs)
    shapes = [b.shape for b in bigs]
    bigs = [_split_rows(b) for b in bigs]

    def body(*refs):
        big_in, small_in = refs[:nb], refs[nb]
        big_out, small_out = refs[nb + 1:2 * nb + 1], refs[2 * nb + 1]
        ici_send, ici_recv, d2d_send, d2d_recv, own_send, own_recv, loc_sem = refs[2 * nb + 2:]
        x, y, c, chips = _place()
        me = 2 * x + y
        sib = (x, y, 1 - c)

        def half(ref, k, which):
            return ref.at[:, which]

        local = [pltpu.make_async_copy(small_in, small_out.at[me], loc_sem.at[0])]
        for cp in local:
            cp.start()
        sends = [_remote(big_in[k], big_out[k].at[me], own_send.at[k], own_recv.at[k], sib) for k in range(nb)]
        for j, (px, py) in enumerate(chips):
            for k in range(nb):
                sends.append(_remote(half(big_in[k], k, c), half(big_out[k].at[me], k, c),
                                     ici_send.at[k * 3 + j], ici_recv.at[k * 3 + j], (px, py, c)))
            sends.append(_remote(small_in, small_out.at[me], ici_send.at[nb * 3 + j], ici_recv.at[nb * 3 + j], (px, py, c)))
        for cp in sends:
            cp.start()
        passed = []
        for j, (px, py) in enumerate(chips):
            frm = 2 * px + py
            for k in range(nb):
                landed = half(big_out[k].at[frm], k, c)
                _remote(landed, landed, ici_send.at[k * 3 + j], ici_recv.at[k * 3 + j], (px, py, c)).wait_recv()
                fwd = _remote(landed, landed, d2d_send.at[k * 3 + j], d2d_recv.at[k * 3 + j], sib)
                fwd.start()
                passed.append(fwd)
            _remote(small_in, small_out.at[frm], ici_send.at[nb * 3 + j], ici_recv.at[nb * 3 + j], (px, py, c)).wait_recv()
        for j, (px, py) in enumerate(chips):
            frm = 2 * px + py
            for k in range(nb):
                theirs = half(big_out[k].at[frm], k, 1 - c)
                _remote(theirs, theirs, d2d_send.at[k * 3 + j], d2d_recv.at[k * 3 + j], sib).wait_recv()
        for k in range(nb):
            _remote(big_in[k], big_out[k].at[me], own_send.at[k], own_recv.at[k], sib).wait_recv()
        for cp in sends + passed:
            cp.wait_send()
        for cp in local:
            cp.wait()

    out_shape = [jax.ShapeDtypeStruct((N_CHIPS,) + b.shape, b.dtype) for b in bigs]
    out_shape.append(jax.ShapeDtypeStruct((N_CHIPS,) + small.shape, small.dtype))
    outs = pl.pallas_call(
        body, in_specs=[HBM_SPEC] * (nb + 1), out_specs=[HBM_SPEC] * (nb + 1), out_shape=out_shape,
        scratch_shapes=[pltpu.SemaphoreType.DMA((3 * nb + 3,)), pltpu.SemaphoreType.DMA((3 * nb + 3,)),
                        pltpu.SemaphoreType.DMA((3 * nb,)), pltpu.SemaphoreType.DMA((3 * nb,)),
                        pltpu.SemaphoreType.DMA((nb,)), pltpu.SemaphoreType.DMA((nb,)),
                        pltpu.SemaphoreType.DMA((1,))],
        name="gather_weights")(*bigs, small)
    return [o.reshape((N_CHIPS,) + s) for o, s in zip(outs[:-1], shapes)] + [outs[-1]]


def _exchange_halves(grads, small, name):
    ng = len(grads)
    grads = [_split_rows(g) for g in grads]
    extra = [] if small is None else [small]
    nall = ng + len(extra)

    def body(*refs):
        ins, outs = refs[:nall], refs[nall:2 * nall]
        send_sem, recv_sem = refs[2 * nall:]
        x, y, c, _ = _place()
        sib = (x, y, 1 - c)
        copies = []
        for k in range(nall):
            src = ins[k].at[:, 1 - c] if k < ng else ins[k]
            copies.append(_remote(src, outs[k], send_sem.at[k], recv_sem.at[k], sib))
        for cp in copies:
            cp.start()
        for cp in copies:
            cp.wait_recv()
        for cp in copies:
            cp.wait_send()

    out_shape = [jax.ShapeDtypeStruct((g.shape[0], g.shape[2], g.shape[3]), g.dtype) for g in grads]
    out_shape += [jax.ShapeDtypeStruct(s.shape, s.dtype) for s in extra]
    return pl.pallas_call(
        body, in_specs=[HBM_SPEC] * nall, out_specs=[HBM_SPEC] * nall, out_shape=out_shape,
        scratch_shapes=[pltpu.SemaphoreType.DMA((nall,)), pltpu.SemaphoreType.DMA((nall,))],
        name=name)(*grads, *extra)


def _join_halves(reduced, dests, out_shapes, name):
    nr = len(reduced)

    def body(*refs):
        r_in = refs[:nr]
        outs = refs[nr:nr + len(out_shapes)]
        send_sem, recv_sem, back_send, back_recv = refs[nr + len(out_shapes):]
        x, y, c, _ = _place()
        sib = (x, y, 1 - c)
        sends, backs = [], []
        for k in range(nr):
            oi, layer = dests[k]
            sends.append(_remote(r_in[k], outs[oi].at[layer, c], send_sem.at[k], recv_sem.at[k], sib))
        for cp in sends:
            cp.start()
        for k in range(nr):
            oi, layer = dests[k]
            theirs = outs[oi].at[layer, 1 - c]
            _remote(r_in[k], theirs, send_sem.at[k], recv_sem.at[k], sib).wait_recv()
            back = _remote(theirs, theirs, back_send.at[k], back_recv.at[k], sib)
            back.start()
            backs.append(back)
        for k in range(nr):
            oi, layer = dests[k]
            mine = outs[oi].at[layer, c]
            _remote(mine, mine, back_send.at[k], back_recv.at[k], sib).wait_recv()
        for cp in sends + backs:
            cp.wait_send()

    split = [(s[0], 2, s[1] // 2, s[2]) for s in out_shapes]
    outs = pl.pallas_call(
        body, in_specs=[HBM_SPEC] * nr, out_specs=[HBM_SPEC] * len(out_shapes),
        out_shape=[jax.ShapeDtypeStruct(s, F32) for s in split],
        scratch_shapes=[pltpu.SemaphoreType.DMA((nr,)), pltpu.SemaphoreType.DMA((nr,)),
                        pltpu.SemaphoreType.DMA((nr,)), pltpu.SemaphoreType.DMA((nr,))],
        name=name)(*reduced)
    return [o.reshape(s) for o, s in zip(outs, out_shapes)]


def _pack(arrs):
    flat = jnp.concatenate([a.reshape(-1).astype(F32) for a in arrs])
    n = flat.shape[0]
    rows = -(-n // PACK_WIDTH)
    rows = -(-rows // 8) * 8
    return jnp.pad(flat, (0, rows * PACK_WIDTH - n)).reshape(rows, PACK_WIDTH)


def _unpack(buf, shapes):
    flat = buf.reshape(-1)
    out, off = [], 0
    for shp in shapes:
        n = 1
        for s in shp:
            n *= s
        out.append(flat[off:off + n].reshape(shp))
        off += n
    return out


def _unshard_cols(stacked):
    moved = jnp.moveaxis(stacked, 0, -2)
    return moved.reshape(moved.shape[:-2] + (moved.shape[-2] * moved.shape[-1],))


def _col_shard(full, s, width):
    return lax.dynamic_slice_in_dim(full, s * width, width, axis=full.ndim - 1)


def kernel(x, meta_tokens, mix_norm_g, ffn_norm_g, ffn_w1, ffn_w2, cp_w_in, cp_conv_w, cp_conv_b, cp_ln_g, cp_ln_b, cp_pool_w, cp_pool_scale, cp_w_out, gla_w_in, gla_gate_w2, gla_gate_b, gla_head_g, gla_w_out, final_norm_g, loss_target, m_meta_tokens, m_mix_norm_g, m_ffn_norm_g, m_ffn_w1, m_ffn_w2, m_cp_w_in, m_cp_conv_w, m_cp_conv_b, m_cp_ln_g, m_cp_ln_b, m_cp_pool_w, m_cp_pool_scale, m_cp_w_out, m_gla_w_in, m_gla_gate_w2, m_gla_gate_b, m_gla_head_g, m_gla_w_out, m_final_norm_g, v_meta_tokens, v_mix_norm_g, v_ffn_norm_g, v_ffn_w1, v_ffn_w2, v_cp_w_in, v_cp_conv_w, v_cp_conv_b, v_cp_ln_g, v_cp_ln_b, v_cp_pool_w, v_cp_pool_scale, v_cp_w_out, v_gla_w_in, v_gla_gate_w2, v_gla_gate_b, v_gla_head_g, v_gla_w_out, v_final_norm_g):
    d = D_MODEL
    chip = 2 * lax.axis_index("x") + lax.axis_index("y")
    core = lax.axis_index("c")
    seq = x.shape[1]
    t = seq + CHUNK

    big_w = [ffn_w1, ffn_w2, cp_w_in, cp_w_out, gla_w_in, gla_w_out]
    sharded_small = [meta_tokens, cp_conv_w, gla_gate_w2, gla_gate_b, gla_head_g]
    cpin_g, cpout_g, small_g = _gather_weights([cp_w_in.astype(BF16), cp_w_out.astype(BF16)], _pack(sharded_small))

    def halves(w):
        return w.astype(BF16).reshape(2, w.shape[0] // 2, w.shape[1])

    def start_gather(name, srcs, dep):
        lands = [lax.empty((N_CHIPS,) + s.shape, s.dtype) for s in srcs]
        return _start_copies(name, srcs, lands, _plan_gather, 3 * len(srcs), dep)

    def finish_gather(name, started, after):
        srcs, lands = _wait_copies(name + "_wait", started, _plan_gather, after)
        return [g.reshape(N_CHIPS, 2 * g.shape[2], g.shape[3]) for g in _share_with_sibling(name + "_share", srcs, lands)]

    ffn0_started = start_gather("gather_ffn0_start", [halves(ffn_w1[0]), halves(ffn_w2[0])], small_g)
    gla_started = start_gather("gather_gla_start", [halves(gla_w_in[0]), halves(gla_w_out[0])], ffn0_started[-1])
    ffn1_started = start_gather("gather_ffn1_start", [halves(ffn_w1[1]), halves(ffn_w2[1])], gla_started[-1])
    per_chip = [_unpack(small_g[j], [a.shape for a in sharded_small]) for j in range(N_CHIPS)]
    meta_f, conv_w_f, gate_w_f, gate_b_f, head_g_f = [
        jnp.concatenate([per_chip[j][i] for j in range(N_CHIPS)], axis=-1) for i in range(len(sharded_small))]
    conv_w_f, gate_w_f = conv_w_f[0], gate_w_f[0]
    w_cp_in = _unshard_cols(cpin_g[:, 0])
    w_cp_out = cpout_g.reshape(CONV_DIM + POOL_DIM, d)
    gate_w_pad = jnp.pad(gate_w_f, ((0, GATE_PAD - GATE_RANK), (0, 0))).astype(BF16)
    row = lambda a: a.reshape(1, -1)
    c_idx = core.reshape(1).astype(jnp.int32)
    chip_idx = chip.reshape(1).astype(jnp.int32)

    h0 = jnp.concatenate([jnp.zeros((PAD_ROWS, d), F32), meta_f, x[0]], axis=0)
    z0, u0 = _norm_matmul(h0, row(mix_norm_g[0]), w_cp_in, 512, "cp_in_proj", dep=ffn1_started[-1])
    c0, pm0, mix0 = _cp_seq_fwd(z0, conv_w_f, cp_conv_b, cp_ln_g, cp_ln_b, cp_pool_w[0], cp_pool_scale)
    h1 = _matmul_residual(mix0, w_cp_out, h0, "cp_out_proj")
    w1g0, w2g0 = finish_gather("gather_ffn0", ffn0_started, h1)
    h2, hp0, uf0 = _ffn_fwd(h1, row(ffn_norm_g[0]), w1g0, w2g0, "ffn0_fwd")
    glain_g, glaout_g = finish_gather("gather_gla", gla_started, h2)
    w_gla_in = jnp.pad(_unshard_cols(glain_g), ((0, 0), (0, GLA_IN_PAD - GLA_IN)))
    w_gla_out = glaout_g.reshape(GLA_DV, d)
    z1, u2 = _norm_matmul(h2, row(mix_norm_g[1]), w_gla_in, 640, "gla_in_proj")
    o1, mix1, states = _gla_seq_fwd(z1, gate_w_pad, gate_b_f, head_g_f)
    h3 = _matmul_residual(mix1, w_gla_out, h2, "gla_out_proj")
    w1g1, w2g1 = finish_gather("gather_ffn1", ffn1_started, h3)
    h4, hp1, uf1 = _ffn_fwd(h3, row(ffn_norm_g[1]), w1g1, w2g1, "ffn1_fwd")

    def start_exchange(name, grads):
        srcs = [_split_rows(g) for g in grads]
        lands = [lax.empty((g.shape[0], g.shape[1] // 2, g.shape[2]), g.dtype) for g in grads]
        return _start_copies(name + "_exchange_start", srcs, lands, _plan_exchange(len(grads)), len(grads))

    def start_scatter(name, exchange, after):
        srcs, recv = _wait_copies(name + "_exchange_wait", exchange, _plan_exchange(len(exchange[2])), after)
        parts = [_sum_halves(g.reshape(g.shape[0], -1, g.shape[3]), r, c_idx, "%s_chip_sum_%d" % (name, k))
                 for k, (g, r) in enumerate(zip(srcs, recv))]
        lands = [lax.empty(p.shape, p.dtype) for p in parts]
        return _start_copies(name + "_scatter_start", parts, lands, _plan_scatter(len(parts)), 3 * len(parts))

    def finish_reduce(name, started, after):
        n = len(started[2])
        parts, lands = _wait_copies(name + "_scatter_wait", started, _plan_scatter(n), after)
        return [_sum_own_and_slots(p, s, chip_idx, "%s_slot_sum_%d" % (name, k)) for k, (p, s) in enumerate(zip(parts, lands))]

    dh4, d_final_g, loss_part = _loss_bwd(h4, row(final_norm_g), loss_target[0])

    dh3, dhp1, d_ffn_g1 = _ffn_bwd_data(dh4, h3, row(ffn_norm_g[1]), hp1, w1g1, w2g1, "ffn1_bwd")
    dw1_1 = _wgrad(uf1, dhp1, N_CHIPS, d, d, False, True, False, "ffn1_dw1")
    dw2_1 = _wgrad(hp1, dh4, N_CHIPS, d, d, True, False, True, "ffn1_dw2")
    ffn1_exchange = start_exchange("ffn1", [dw1_1, dw2_1])

    dmix1 = _dgrad(dh3, w_gla_out, "gla_out_dgrad", dep=ffn1_exchange[-1])
    dw_gla_out = _wgrad(mix1, dh3, 1, GLA_DV, d, False, False, False, "gla_out_dw")
    ffn1_reduce = start_scatter("ffn1", ffn1_exchange, dw_gla_out)
    dz1, d_gate_w, d_gate_b, d_head_g = _gla_seq_bwd(dmix1, o1, z1, states, gate_w_pad, gate_b_f, head_g_f,
                                                     dep=ffn1_reduce[-1])
    dh2, d_mix_g1 = _dgrad_norm_bwd(dz1, w_gla_in, h2, row(mix_norm_g[1]), dh3, 640, "gla_in_dgrad")
    dw_gla_in = _wgrad(u2, dz1, GLA_IN_PAD // 640, d, 640, False, True, False, "gla_in_dw")
    gla_in_cols = jnp.moveaxis(dw_gla_in, 0, 1).reshape(d, GLA_IN_PAD)[:, :GLA_IN]
    gla_exchange = start_exchange("gla", [jnp.moveaxis(gla_in_cols.reshape(d, N_CHIPS, GLA_IN // N_CHIPS), 1, 0),
                                          dw_gla_out.reshape(N_CHIPS, -1, d)])

    dh1, dhp0, d_ffn_g0 = _ffn_bwd_data(dh2, h1, row(ffn_norm_g[0]), hp0, w1g0, w2g0, "ffn0_bwd", dep=gla_exchange[-1])
    gla_reduce = start_scatter("gla", gla_exchange, dh1)
    dw1_0 = _wgrad(uf0, dhp0, N_CHIPS, d, d, False, True, False, "ffn0_dw1", dep=gla_reduce[-1])
    dw2_0 = _wgrad(hp0, dh2, N_CHIPS, d, d, True, False, True, "ffn0_dw2")
    ffn0_exchange = start_exchange("ffn0", [dw1_0, dw2_0])

    dmix0 = _dgrad(dh1, w_cp_out, "cp_out_dgrad", dep=ffn0_exchange[-1])
    dw_cp_out = _wgrad(mix0, dh1, 1, CONV_DIM + POOL_DIM, d, False, False, False, "cp_out_dw")
    ffn0_reduce = start_scatter("ffn0", ffn0_exchange, dw_cp_out)
    dz0, d_conv_w, d_cp_vec, d_pool_w = _cp_seq_bwd(dmix0, z0, c0, pm0, conv_w_f, cp_ln_g, cp_ln_b, cp_pool_w[0],
                                                    cp_pool_scale, dep=ffn0_reduce[-1])
    dh0, d_mix_g0 = _dgrad_norm_bwd(dz0, w_cp_in, h0, row(mix_norm_g[0]), dh1, 512, "cp_in_dgrad")
    dw_cp_in = _wgrad(u0, dz0, N_CHIPS, d, CP_IN // N_CHIPS, False, True, False, "cp_in_dw")

    grad_x = dh0[CHUNK:][None]

    cp_grads = [dw_cp_in, dw_cp_out.reshape(N_CHIPS, -1, d)]
    small_full = [dh0[PAD_ROWS:CHUNK], jnp.concatenate([d_mix_g0, d_mix_g1], axis=0),
                  jnp.concatenate([d_ffn_g0, d_ffn_g1], axis=0), d_conv_w[:CONV_WIDTH][None],
                  d_cp_vec[0:1], d_cp_vec[1:2], d_cp_vec[2:3], d_pool_w[None], d_cp_vec[3:4],
                  d_gate_w[:GATE_RANK][None], d_gate_b, d_head_g, d_final_g[0], loss_part[0, 0:1]]
    small_mine = _pack(small_full)
    recv = _exchange_halves(cp_grads, small_mine, "cp_exchange")
    chip_sums = [_sum_halves(g, r, c_idx, "cp_chip_sum_%d" % k) for k, (g, r) in enumerate(zip(cp_grads, recv[:-1]))]
    small_chip = _add2(small_mine, recv[-1], "chip_sum_small")
    small_slots = lax.dynamic_update_slice(jnp.zeros((N_CHIPS,) + small_chip.shape, F32), small_chip[None], (chip, 0, 0))
    cp_lands = [lax.empty(p.shape, p.dtype) for p in chip_sums] + [small_slots]
    cp_reduce = _start_copies("cp_scatter_start", chip_sums + [small_chip], cp_lands, _plan_scatter(len(chip_sums)),
                              3 * (len(chip_sums) + 1))

    def adamw_big(names, grads):
        outs = {}
        for n, g in zip(names, grads):
            w, m, v = big[n]
            two_d = lambda a: a.reshape(-1, a.shape[-1])
            res = _adamw(two_d(w), two_d(g), two_d(m), two_d(v), "adamw_" + n)
            outs[n] = [o.reshape(w.shape) for o in res]
        return outs

    big = {"w1": (ffn_w1, m_ffn_w1, v_ffn_w1), "w2": (ffn_w2, m_ffn_w2, v_ffn_w2),
           "cp_in": (cp_w_in, m_cp_w_in, v_cp_w_in), "cp_out": (cp_w_out, m_cp_w_out, v_cp_w_out),
           "gla_in": (gla_w_in, m_gla_w_in, v_gla_w_in), "gla_out": (gla_w_out, m_gla_w_out, v_gla_w_out)}
    red_ffn1 = finish_reduce("ffn1", ffn1_reduce, cp_reduce[-1])
    red_gla = finish_reduce("gla", gla_reduce, cp_reduce[-1])
    red_ffn0 = finish_reduce("ffn0", ffn0_reduce, cp_reduce[-1])
    first = ["w1", "w2", "gla_in", "gla_out"]
    first_grads = _join_halves([red_ffn0[0], red_ffn1[0], red_ffn0[1], red_ffn1[1], red_gla[0], red_gla[1]],
                               [(0, 0), (0, 1), (1, 0), (1, 1), (2, 0), (3, 0)], [big[n][0].shape for n in first],
                               "join_halves_ffn_gla")
    big_out = adamw_big(first, first_grads)
    cp_parts, cp_slots = _wait_copies("cp_scatter_wait", cp_reduce, _plan_scatter(len(chip_sums)), big_out["gla_out"][1])
    red_cp = [_sum_own_and_slots(a, s, chip_idx, "cp_slot_sum_%d" % k)
              for k, (a, s) in enumerate(zip(cp_parts[:-1], cp_slots[:-1]))]
    small_red = _sum_slots(cp_slots[-1], "slot_sum_small")
    last = ["cp_in", "cp_out"]
    last_grads = _join_halves(red_cp, [(0, 0), (1, 0)], [big[n][0].shape for n in last], "join_halves_cp")
    big_out.update(adamw_big(last, last_grads))

    (g_meta, g_mix, g_ffn, g_conv_w, g_conv_b, g_ln_g, g_ln_b, g_pool_w, g_pool_scale, g_gate_w, g_gate_b, g_head,
     g_final, loss_sum) = _unpack(small_red, [a.shape for a in small_full])
    g_meta = _col_shard(g_meta, chip, meta_tokens.shape[-1])
    g_conv_w = _col_shard(g_conv_w, chip, cp_conv_w.shape[-1])
    g_gate_w = _col_shard(g_gate_w, chip, gla_gate_w2.shape[-1])
    g_gate_b = _col_shard(g_gate_b, chip, gla_gate_b.shape[-1])
    g_head = _col_shard(g_head, chip, gla_head_g.shape[-1])
    small_w = [meta_tokens, mix_norm_g, ffn_norm_g, cp_conv_w, cp_conv_b, cp_ln_g, cp_ln_b, cp_pool_w, cp_pool_scale,
               gla_gate_w2, gla_gate_b, gla_head_g, final_norm_g]
    small_m = [m_meta_tokens, m_mix_norm_g, m_ffn_norm_g, m_cp_conv_w, m_cp_conv_b, m_cp_ln_g, m_cp_ln_b, m_cp_pool_w,
               m_cp_pool_scale, m_gla_gate_w2, m_gla_gate_b, m_gla_head_g, m_final_norm_g]
    small_v = [v_meta_tokens, v_mix_norm_g, v_ffn_norm_g, v_cp_conv_w, v_cp_conv_b, v_cp_ln_g, v_cp_ln_b, v_cp_pool_w,
               v_cp_pool_scale, v_gla_gate_w2, v_gla_gate_b, v_gla_head_g, v_final_norm_g]
    small_g = [g_meta, g_mix, g_ffn, g_conv_w, g_conv_b, g_ln_g, g_ln_b, g_pool_w, g_pool_scale, g_gate_w, g_gate_b,
               g_head, g_final]
    shapes = [w.shape for w in small_w]
    small_g = [g.reshape(s) for g, s in zip(small_g, shapes)]
    _, s_delta, s_m, s_v = _adamw(_pack(small_w), _pack(small_g), _pack(small_m), _pack(small_v), "adamw_small")
    s_delta, s_m, s_v = _unpack(s_delta, shapes), _unpack(s_m, shapes), _unpack(s_v, shapes)

    order = ["meta", "mix", "ffn", "w1", "w2", "cp_in", "conv_w", "conv_b", "ln_g", "ln_b", "pool_w", "pool_scale",
             "cp_out", "gla_in", "gate_w", "gate_b", "head", "gla_out", "final"]
    small_names = ["meta", "mix", "ffn", "conv_w", "conv_b", "ln_g", "ln_b", "pool_w", "pool_scale", "gate_w", "gate_b",
                   "head", "final"]
    big_names = ["w1", "w2", "cp_in", "cp_out", "gla_in", "gla_out"]
    table = {n: (small_g[i], s_delta[i], s_m[i], s_v[i]) for i, n in enumerate(small_names)}
    table.update({n: tuple(big_out[n]) for n in big_names})
    loss = loss_sum.reshape(())
    return (loss, grad_x, *[table[n][0] for n in order], *[table[n][1] for n in order],
            *[table[n][2] for n in order], *[table[n][3] for n in order])
```

```python
import functools

import jax
import jax.numpy as jnp
from jax import lax
from jax.experimental import pallas as pl
from jax.experimental.pallas import tpu as pltpu

F32 = jnp.float32
BF16 = jnp.bfloat16

D_MODEL = 1024
N_META = 16
CHUNK = 64
PAD_ROWS = CHUNK - N_META
EPS = 1e-5
CONV_DIM = 512
CONV_WIDTH = 31
CONV_HALO = 32
POOL_DIM = 512
POOL_WINDOWS = (2, 4, 8, 16)
POOL_GROUP = 128
POOL_HALO = 16
CP_IN = 2 * CONV_DIM + POOL_DIM
GLA_HEADS = 4
GLA_DK = 512
GLA_DV = 1024
GLA_HK = GLA_DK // GLA_HEADS
GLA_HV = GLA_DV // GLA_HEADS
GATE_RANK = 16
GATE_PAD = 128
GATE_NORM = 16.0
GLA_IN = 2 * GLA_DK + 2 * GLA_DV + GATE_RANK
GLA_IN_PAD = 2 * GLA_DK + 2 * GLA_DV + GATE_PAD
N_CHIPS = 4
ADAM_LR = 0.001
ADAM_B1 = 0.9
ADAM_B2 = 0.999
ADAM_EPS = 1e-08
ADAM_WD = 0.01
ADAM_STEP = 10

VMEM_LIMIT_BYTES = 56 * 1024 * 1024
ROW_TILE_TARGET = 832
TOKEN_TILE_TARGET = 1040
PACK_WIDTH = 1024
MESH = pl.DeviceIdType.MESH
HBM_SPEC = pl.BlockSpec(memory_space=pltpu.HBM)
ANY_SPEC = pl.BlockSpec(memory_space=pl.ANY)
SEM_SPEC = pl.BlockSpec(memory_space=pltpu.SEMAPHORE)
SIDE_EFFECT = pltpu.SideEffectType.DATAFLOW_SIDE_EFFECTING


def _cparams(*sem):
    return pltpu.CompilerParams(dimension_semantics=sem, vmem_limit_bytes=VMEM_LIMIT_BYTES)


def _row_tile(t, target, mult):
    best = mult
    for cand in range(mult, min(t, target) + 1, mult):
        if t % cand == 0:
            best = cand
    assert t % best == 0, (t, best)
    return best


def _rms(h, g):
    return h * lax.rsqrt(jnp.mean(h * h, axis=-1, keepdims=True) + EPS) * g


def _rms_bwd(h, g, du):
    r = lax.rsqrt(jnp.mean(h * h, axis=-1, keepdims=True) + EPS)
    xhat = h * r
    dxh = du * g
    dh = r * (dxh - xhat * jnp.mean(dxh * xhat, axis=-1, keepdims=True))
    return dh, du * xhat


def _valid_rows(i, tm):
    row = i * tm + lax.broadcasted_iota(jnp.int32, (tm, 1), 0)
    return row >= PAD_ROWS


def _dot(a, b):
    return jnp.dot(a, b, preferred_element_type=F32)


def _dot_nt(a, b):
    return lax.dot_general(a, b, (((1,), (1,)), ((), ())), preferred_element_type=F32)


def _dot_tn(a, b):
    return lax.dot_general(a, b, (((0,), (0,)), ((), ())), preferred_element_type=F32)


def _accumulate(ref, val, first):
    @pl.when(first)
    def _():
        ref[...] = val

    @pl.when(jnp.logical_not(first))
    def _():
        ref[...] += val


def _call_after(dep, body, n_in, in_specs, args, **kw):
    if dep is None:
        return pl.pallas_call(body, in_specs=in_specs, **kw)(*args)

    def with_dep(*refs):
        body(*refs[:n_in], *refs[n_in + 1:])

    return pl.pallas_call(with_dep, in_specs=list(in_specs) + [ANY_SPEC], **kw)(*args, dep)


def _norm_matmul(h, g, w, nc, name, dep=None):
    t, d = h.shape
    n = w.shape[1]
    tm = _row_tile(t, TOKEN_TILE_TARGET, 16)

    def body(h_ref, g_ref, w_ref, z_ref, u_ref):
        u = _rms(h_ref[...], g_ref[...]).astype(BF16)
        u_ref[...] = u
        for n0 in range(0, n, nc):
            z_ref[:, n0:n0 + nc] = _dot(u, w_ref[:, n0:n0 + nc]).astype(BF16)

    return _call_after(
        dep, body, 3,
        [pl.BlockSpec((tm, d), lambda i: (i, 0)), pl.BlockSpec((1, d), lambda i: (0, 0)),
         pl.BlockSpec((d, n), lambda i: (0, 0))], (h, g, w), grid=(t // tm,),
        out_specs=[pl.BlockSpec((tm, n), lambda i: (i, 0)), pl.BlockSpec((tm, d), lambda i: (i, 0))],
        out_shape=[jax.ShapeDtypeStruct((t, n), BF16), jax.ShapeDtypeStruct((t, d), BF16)],
        compiler_params=_cparams("parallel"), name=name)


def _matmul_residual(a, w, h, name):
    t, k = a.shape
    d = w.shape[1]
    tm = _row_tile(t, TOKEN_TILE_TARGET, 16)

    def body(a_ref, w_ref, h_ref, o_ref):
        o_ref[...] = h_ref[...] + _dot(a_ref[...], w_ref[...])

    return pl.pallas_call(
        body, grid=(t // tm,),
        in_specs=[pl.BlockSpec((tm, k), lambda i: (i, 0)), pl.BlockSpec((k, d), lambda i: (0, 0)),
                  pl.BlockSpec((tm, d), lambda i: (i, 0))],
        out_specs=pl.BlockSpec((tm, d), lambda i: (i, 0)),
        out_shape=jax.ShapeDtypeStruct((t, d), F32),
        compiler_params=_cparams("parallel"), name=name)(a, w, h)


def _ffn_fwd(h, g, w1g, w2g, name):
    t, d = h.shape
    ns, ffs = w1g.shape[0], w1g.shape[2]
    tm = _row_tile(t, TOKEN_TILE_TARGET, 16)

    def body(h_ref, g_ref, w1_ref, w2_ref, ho_ref, hp_ref, u_ref, acc_ref):
        s = pl.program_id(1)

        @pl.when(s == 0)
        def _():
            u_ref[...] = _rms(h_ref[...], g_ref[...]).astype(BF16)

        hp = _dot(u_ref[...], w1_ref[...])
        hp_ref[...] = hp.astype(BF16)
        a = jnp.maximum(hp, 0.0)
        _accumulate(acc_ref, _dot((a * a).astype(BF16), w2_ref[...]), s == 0)

        @pl.when(s == ns - 1)
        def _():
            ho_ref[...] = h_ref[...] + acc_ref[...]

    return pl.pallas_call(
        body, grid=(t // tm, ns),
        in_specs=[pl.BlockSpec((tm, d), lambda i, s: (i, 0)), pl.BlockSpec((1, d), lambda i, s: (0, 0)),
                  pl.BlockSpec((None, d, ffs), lambda i, s: (s, 0, 0)),
                  pl.BlockSpec((None, ffs, d), lambda i, s: (s, 0, 0))],
        out_specs=[pl.BlockSpec((tm, d), lambda i, s: (i, 0)), pl.BlockSpec((tm, ffs), lambda i, s: (i, s)),
                   pl.BlockSpec((tm, d), lambda i, s: (i, 0))],
        out_shape=[jax.ShapeDtypeStruct((t, d), F32), jax.ShapeDtypeStruct((t, ns * ffs), BF16),
                   jax.ShapeDtypeStruct((t, d), BF16)],
        scratch_shapes=[pltpu.VMEM((tm, d), F32)],
        compiler_params=_cparams("parallel", "arbitrary"), name=name)(h, g, w1g, w2g)


def _ffn_bwd_data(dh, h, g, hp, w1g, w2g, name, dep=None):
    t, d = h.shape
    ns, ffs = w1g.shape[0], w1g.shape[2]
    tm = _row_tile(t, ROW_TILE_TARGET, CHUNK)

    def body(dh_ref, h_ref, g_ref, hp_ref, w1_ref, w2_ref, dhi_ref, dhp_ref, dg_ref, acc_ref):
        i, s = pl.program_id(0), pl.program_id(1)
        da = _dot_nt(dh_ref[...].astype(BF16), w2_ref[...])
        dhp = (da * (2.0 * jnp.maximum(hp_ref[...].astype(F32), 0.0))).astype(BF16)
        dhp_ref[...] = dhp
        _accumulate(acc_ref, _dot_nt(dhp, w1_ref[...]), s == 0)

        @pl.when(s == ns - 1)
        def _():
            dhn, dgr = _rms_bwd(h_ref[...], g_ref[...], acc_ref[...])
            dhi_ref[...] = jnp.where(_valid_rows(i, tm), dh_ref[...] + dhn, 0.0)
            _accumulate(dg_ref, jnp.sum(dgr, axis=0, keepdims=True), i == 0)

    return _call_after(
        dep, body, 6,
        [pl.BlockSpec((tm, d), lambda i, s: (i, 0)), pl.BlockSpec((tm, d), lambda i, s: (i, 0)),
         pl.BlockSpec((1, d), lambda i, s: (0, 0)), pl.BlockSpec((tm, ffs), lambda i, s: (i, s)),
         pl.BlockSpec((None, d, ffs), lambda i, s: (s, 0, 0)),
         pl.BlockSpec((None, ffs, d), lambda i, s: (s, 0, 0))], (dh, h, g, hp, w1g, w2g), grid=(t // tm, ns),
        out_specs=[pl.BlockSpec((tm, d), lambda i, s: (i, 0)), pl.BlockSpec((tm, ffs), lambda i, s: (i, s)),
                   pl.BlockSpec((1, d), lambda i, s: (0, 0))],
        out_shape=[jax.ShapeDtypeStruct((t, d), F32), jax.ShapeDtypeStruct((t, ns * ffs), BF16),
                   jax.ShapeDtypeStruct((1, d), F32)],
        scratch_shapes=[pltpu.VMEM((tm, d), F32)],
        compiler_params=_cparams("arbitrary", "arbitrary"), name=name)


WGRAD_ROWS = 1024


def _wgrad(x, dy, nb, xc, yc, x_by_block, dy_by_block, relu2, name, dep=None):
    t = x.shape[0]
    tk = _row_tile(t - CHUNK, WGRAD_ROWS, CHUNK)

    def prep(xv):
        if relu2:
            xv = jnp.maximum(xv.astype(F32), 0.0)
            xv = xv * xv
        return xv.astype(BF16)

    def body(xh_ref, dyh_ref, x_ref, dy_ref, o_ref):
        k = pl.program_id(1)
        p = _dot_tn(prep(x_ref[...]), dy_ref[...].astype(BF16))

        @pl.when(k == 0)
        def _():
            o_ref[...] = p + _dot_tn(prep(xh_ref[...]), dyh_ref[...].astype(BF16))

        @pl.when(k > 0)
        def _():
            o_ref[...] += p

    def head(width, by_block):
        return pl.BlockSpec((CHUNK, width), (lambda b, k: (0, b)) if by_block else (lambda b, k: (0, 0)))

    def rest(width, by_block):
        def index(b, k):
            return pl.multiple_of(CHUNK + k * tk, CHUNK), (pl.multiple_of(b * width, 128) if by_block else 0)
        return pl.BlockSpec((pl.Element(tk), pl.Element(width)), index)

    return _call_after(
        dep, body, 4,
        [head(xc, x_by_block), head(yc, dy_by_block), rest(xc, x_by_block), rest(yc, dy_by_block)], (x, dy, x, dy),
        grid=(nb, (t - CHUNK) // tk),
        out_specs=pl.BlockSpec((None, xc, yc), lambda b, k: (b, 0, 0)),
        out_shape=jax.ShapeDtypeStruct((nb, xc, yc), F32),
        compiler_params=_cparams("parallel", "arbitrary"), name=name)


def _dgrad(dh, w, name, dep=None):
    t, d = dh.shape
    k = w.shape[0]
    tm = _row_tile(t, TOKEN_TILE_TARGET, 16)

    def body(dh_ref, w_ref, o_ref):
        o_ref[...] = _dot_nt(dh_ref[...].astype(BF16), w_ref[...]).astype(BF16)

    return _call_after(
        dep, body, 2,
        [pl.BlockSpec((tm, d), lambda i: (i, 0)), pl.BlockSpec((k, d), lambda i: (0, 0))], (dh, w), grid=(t // tm,),
        out_specs=pl.BlockSpec((tm, k), lambda i: (i, 0)),
        out_shape=jax.ShapeDtypeStruct((t, k), BF16),
        compiler_params=_cparams("parallel"), name=name)


def _dgrad_norm_bwd(dz, w, h, g, dh, nc, name):
    t, d = h.shape
    n = w.shape[1]
    tm = _row_tile(t, ROW_TILE_TARGET // 2, 16)

    def body(dz_ref, w_ref, h_ref, g_ref, dh_ref, dhi_ref, dg_ref):
        i = pl.program_id(0)
        du = jnp.zeros((tm, d), F32)
        for n0 in range(0, n, nc):
            du = du + _dot_nt(dz_ref[:, n0:n0 + nc], w_ref[:, n0:n0 + nc])
        dhn, dgr = _rms_bwd(h_ref[...], g_ref[...], du)
        dhi_ref[...] = jnp.where(_valid_rows(i, tm), dh_ref[...] + dhn, 0.0)
        _accumulate(dg_ref, jnp.sum(dgr, axis=0, keepdims=True), i == 0)

    return pl.pallas_call(
        body, grid=(t // tm,),
        in_specs=[pl.BlockSpec((tm, n), lambda i: (i, 0)), pl.BlockSpec((d, n), lambda i: (0, 0)),
                  pl.BlockSpec((tm, d), lambda i: (i, 0)), pl.BlockSpec((1, d), lambda i: (0, 0)),
                  pl.BlockSpec((tm, d), lambda i: (i, 0))],
        out_specs=[pl.BlockSpec((tm, d), lambda i: (i, 0)), pl.BlockSpec((1, d), lambda i: (0, 0))],
        out_shape=[jax.ShapeDtypeStruct((t, d), F32), jax.ShapeDtypeStruct((1, d), F32)],
        compiler_params=_cparams("arbitrary"), name=name)(dz, w, h, g, dh)


def _loss_bwd(h, g, target):
    t, d = h.shape
    tl = _row_tile(t - CHUNK, 1024, CHUNK)

    def body(h_ref, g_ref, t_ref, dh_ref, dg_ref, loss_ref):
        i = pl.program_id(0)
        hv, gv = h_ref[...], g_ref[...]
        err = _rms(hv, gv) - t_ref[...]
        part = 0.5 * jnp.sum(jnp.mean(err * err, axis=-1, keepdims=True), axis=0, keepdims=True)
        dhn, dgr = _rms_bwd(hv, gv, err * (1.0 / d))
        dh_ref[...] = dhn
        _accumulate(dg_ref, jnp.sum(dgr, axis=0, keepdims=True), i == 0)
        _accumulate(loss_ref, jnp.broadcast_to(part, (8, 128)), i == 0)

    shifted = pl.BlockSpec((pl.Element(tl), pl.Element(d)), lambda i: (pl.multiple_of(CHUNK + i * tl, CHUNK), 0))
    dh, dg, loss = pl.pallas_call(
        body, grid=((t - CHUNK) // tl,),
        in_specs=[shifted, pl.BlockSpec((1, d), lambda i: (0, 0)), pl.BlockSpec((tl, d), lambda i: (i, 0))],
        out_specs=[shifted, pl.BlockSpec((1, d), lambda i: (0, 0)), pl.BlockSpec((8, 128), lambda i: (0, 0))],
        out_shape=[jax.ShapeDtypeStruct((t, d), F32), jax.ShapeDtypeStruct((1, d), F32),
                   jax.ShapeDtypeStruct((8, 128), F32)],
        compiler_params=_cparams("arbitrary"), name="loss_bwd")(h, g, target)

    def zero_head(dh_ref, o_ref):
        o_ref[...] = jnp.zeros_like(o_ref)

    dh = pl.pallas_call(
        zero_head, grid=(1,), in_specs=[ANY_SPEC], out_specs=pl.BlockSpec((CHUNK, d), lambda i: (0, 0)),
        out_shape=jax.ShapeDtypeStruct((t, d), F32), input_output_aliases={0: 0}, name="loss_bwd_head")(dh)
    return dh, dg, loss


CONV_BLOCK = 32


def _silu(x):
    return x * jax.nn.sigmoid(x)


def _row_shifts(win):
    n = win.shape[0]
    return [win] + [pltpu.roll(win, n - j, 0) for j in range(1, 8)]


def _cp_seq_fwd(z, conv_w, conv_b, ln_g, ln_b, pool_w, pool_scale):
    t = z.shape[0]
    tm = _row_tile(t, ROW_TILE_TARGET, CHUNK)

    def body(z_ref, cw_ref, cb_ref, lg_ref, lb_ref, pw_ref, ps_ref, c_ref, pm_ref, mix_ref, gbuf, pbuf):
        i = pl.program_id(0)

        @pl.when(i == 0)
        def _():
            gbuf[0:CONV_HALO, :] = jnp.zeros((CONV_HALO, CONV_DIM), F32)
            pbuf[0:POOL_HALO, :] = jnp.zeros((POOL_HALO, POOL_DIM), F32)

        @pl.when(i > 0)
        def _():
            gbuf[0:CONV_HALO, :] = gbuf[tm:tm + CONV_HALO, :]
            pbuf[0:POOL_HALO, :] = pbuf[tm:tm + POOL_HALO, :]

        av = z_ref[:, 0:CONV_DIM].astype(F32)
        ag = z_ref[:, CONV_DIM:2 * CONV_DIM].astype(F32)
        gbuf[CONV_HALO:CONV_HALO + tm, :] = av * jax.nn.sigmoid(ag)
        pbuf[POOL_HALO:POOL_HALO + tm, :] = z_ref[:, 2 * CONV_DIM:CP_IN].astype(F32)

        def conv_block(rb, carry):
            base = pl.multiple_of(rb * CONV_BLOCK, CONV_BLOCK)
            shifted = _row_shifts(gbuf[pl.ds(base, CONV_BLOCK + CONV_HALO), :])
            acc = jnp.zeros((CONV_BLOCK, CONV_DIM), F32)
            for k in range(CONV_WIDTH):
                whole, part = divmod(CONV_HALO - (CONV_WIDTH - 1) + k, 8)
                acc = acc + cw_ref[k:k + 1, :] * shifted[part][8 * whole:8 * whole + CONV_BLOCK, :]
            c_ref[pl.ds(base, CONV_BLOCK), :] = acc + cb_ref[...]
            return carry

        lax.fori_loop(0, tm // CONV_BLOCK, conv_block, 0)

        c = c_ref[...]
        mu = jnp.mean(c, axis=-1, keepdims=True)
        xc = c - mu
        ln = xc * lax.rsqrt(jnp.mean(xc * xc, axis=-1, keepdims=True) + EPS) * lg_ref[...] + lb_ref[...]
        row = i * tm + lax.broadcasted_iota(jnp.int32, (tm, 1), 0)
        mix_ref[:, 0:CONV_DIM] = jnp.where(row >= PAD_ROWS, _silu(ln), 0.0).astype(BF16)

        tpos = (row - PAD_ROWS + 1).astype(F32)
        for gi, wdw in enumerate(POOL_WINDOWS):
            lo = POOL_GROUP * gi
            cur = pbuf[POOL_HALO:POOL_HALO + tm, lo:lo + POOL_GROUP]
            sacc = cur
            for j in range(1, wdw):
                sacc = sacc + pbuf[POOL_HALO - j:POOL_HALO - j + tm, lo:lo + POOL_GROUP]
            pm = (sacc / jnp.clip(tpos, 1.0, float(wdw)) - cur).astype(BF16)
            pm_ref[:, lo:lo + POOL_GROUP] = pm
            pg = _dot(pm, pw_ref[gi].astype(BF16))
            mix_ref[:, CONV_DIM + lo:CONV_DIM + lo + POOL_GROUP] = (pg * ps_ref[:, lo:lo + POOL_GROUP]).astype(BF16)

    vec = pl.BlockSpec((1, CONV_DIM), lambda i: (0, 0))
    return pl.pallas_call(
        body, grid=(t // tm,),
        in_specs=[pl.BlockSpec((tm, CP_IN), lambda i: (i, 0)),
                  pl.BlockSpec((CONV_WIDTH, CONV_DIM), lambda i: (0, 0)), vec, vec, vec,
                  pl.BlockSpec((len(POOL_WINDOWS), POOL_GROUP, POOL_GROUP), lambda i: (0, 0, 0)), vec],
        out_specs=[pl.BlockSpec((tm, CONV_DIM), lambda i: (i, 0)), pl.BlockSpec((tm, POOL_DIM), lambda i: (i, 0)),
                   pl.BlockSpec((tm, CONV_DIM + POOL_DIM), lambda i: (i, 0))],
        out_shape=[jax.ShapeDtypeStruct((t, CONV_DIM), F32), jax.ShapeDtypeStruct((t, POOL_DIM), BF16),
                   jax.ShapeDtypeStruct((t, CONV_DIM + POOL_DIM), BF16)],
        scratch_shapes=[pltpu.VMEM((tm + CONV_HALO, CONV_DIM), F32), pltpu.VMEM((tm + POOL_HALO, POOL_DIM), F32)],
        compiler_params=_cparams("arbitrary"), name="cp_seq_fwd")(z, conv_w, conv_b, ln_g, ln_b, pool_w, pool_scale)


def _cp_seq_bwd(dmix, z, c, pm, conv_w, ln_g, ln_b, pool_w, pool_scale, dep=None):
    t = z.shape[0]
    tm = _row_tile(t, ROW_TILE_TARGET, CHUNK)
    nt = t // tm

    def body(dmix_ref, z_ref, c_ref, pm_ref, cw_ref, lg_ref, lb_ref, pw_ref, ps_ref,
             dz_ref, dcw_ref, dvec_ref, dpw_ref, dcbuf, qbuf, glu_buf, dwacc):
        i = pl.program_id(0)
        tile = nt - 1 - i

        @pl.when(i == 0)
        def _():
            dcbuf[tm:tm + CONV_HALO, :] = jnp.zeros((CONV_HALO, CONV_DIM), F32)
            qbuf[tm:tm + POOL_HALO, :] = jnp.zeros((POOL_HALO, POOL_DIM), F32)
            dcw_ref[...] = jnp.zeros_like(dcw_ref)
            dwacc[...] = jnp.zeros_like(dwacc)
            dvec_ref[...] = jnp.zeros_like(dvec_ref)
            dpw_ref[...] = jnp.zeros_like(dpw_ref)

        @pl.when(i > 0)
        def _():
            dcbuf[tm:tm + CONV_HALO, :] = dcbuf[0:CONV_HALO, :]
            qbuf[tm:tm + POOL_HALO, :] = qbuf[0:POOL_HALO, :]

        row = tile * tm + lax.broadcasted_iota(jnp.int32, (tm, 1), 0)
        cv = c_ref[...]
        mu = jnp.mean(cv, axis=-1, keepdims=True)
        xc = cv - mu
        rstd = lax.rsqrt(jnp.mean(xc * xc, axis=-1, keepdims=True) + EPS)
        xhat = xc * rstd
        ln = xhat * lg_ref[...] + lb_ref[...]
        sg = jax.nn.sigmoid(ln)
        da = jnp.where(row >= PAD_ROWS, dmix_ref[:, 0:CONV_DIM].astype(F32), 0.0)
        dln = da * (sg * (1.0 + ln * (1.0 - sg)))
        dxh = dln * lg_ref[...]
        dc = rstd * (dxh - jnp.mean(dxh, axis=-1, keepdims=True) - xhat * jnp.mean(dxh * xhat, axis=-1, keepdims=True))
        dcbuf[0:tm, :] = dc
        dvec_ref[0:1, :] += jnp.sum(dc, axis=0, keepdims=True)
        dvec_ref[1:2, :] += jnp.sum(dln * xhat, axis=0, keepdims=True)
        dvec_ref[2:3, :] += jnp.sum(dln, axis=0, keepdims=True)

        av = z_ref[:, 0:CONV_DIM].astype(F32)
        sig_g = jax.nn.sigmoid(z_ref[:, CONV_DIM:2 * CONV_DIM].astype(F32))
        glu_buf[...] = av * sig_g

        def conv_block(rb, carry):
            base = pl.multiple_of(rb * CONV_BLOCK, CONV_BLOCK)
            shifted = _row_shifts(dcbuf[pl.ds(base, CONV_BLOCK + CONV_HALO), :])
            glu = glu_buf[pl.ds(base, CONV_BLOCK), :]
            acc = jnp.zeros((CONV_BLOCK, CONV_DIM), F32)
            for k in range(CONV_WIDTH):
                whole, part = divmod(CONV_WIDTH - 1 - k, 8)
                slab = shifted[part][8 * whole:8 * whole + CONV_BLOCK, :]
                acc = acc + cw_ref[k:k + 1, :] * slab
                prod = slab * glu
                part = prod[0:8]
                for q in range(1, CONV_BLOCK // 8):
                    part = part + prod[8 * q:8 * q + 8]
                dwacc[k] += part
            glu_buf[pl.ds(base, CONV_BLOCK), :] = acc
            return carry

        lax.fori_loop(0, tm // CONV_BLOCK, conv_block, 0)

        @pl.when(i == nt - 1)
        def _():
            for k in range(CONV_WIDTH):
                dcw_ref[k:k + 1, :] = jnp.sum(dwacc[k], axis=0, keepdims=True)
        dglu = glu_buf[...]
        dz_ref[:, 0:CONV_DIM] = (dglu * sig_g).astype(BF16)
        dz_ref[:, CONV_DIM:2 * CONV_DIM] = (dglu * av * sig_g * (1.0 - sig_g)).astype(BF16)

        tpos = (row - PAD_ROWS + 1).astype(F32)
        for gi, wdw in enumerate(POOL_WINDOWS):
            lo = POOL_GROUP * gi
            dp = dmix_ref[:, CONV_DIM + lo:CONV_DIM + lo + POOL_GROUP].astype(F32)
            pmv = pm_ref[:, lo:lo + POOL_GROUP]
            pwb = pw_ref[gi].astype(BF16)
            dvec_ref[3:4, lo:lo + POOL_GROUP] += jnp.sum(dp * _dot(pmv, pwb), axis=0, keepdims=True)
            dq = (dp * ps_ref[:, lo:lo + POOL_GROUP]).astype(BF16)
            dpw_ref[gi] += _dot_tn(pmv, dq)
            dpm = _dot_nt(dq, pwb)
            qbuf[0:tm, lo:lo + POOL_GROUP] = dpm / jnp.clip(tpos, 1.0, float(wdw))
            sacc = -dpm
            for j in range(wdw):
                sacc = sacc + qbuf[j:j + tm, lo:lo + POOL_GROUP]
            dz_ref[:, 2 * CONV_DIM + lo:2 * CONV_DIM + lo + POOL_GROUP] = sacc.astype(BF16)

    vec = pl.BlockSpec((1, CONV_DIM), lambda i: (0, 0))
    rev = lambda i: (nt - 1 - i, 0)
    return _call_after(
        dep, body, 9,
        [pl.BlockSpec((tm, CONV_DIM + POOL_DIM), rev), pl.BlockSpec((tm, CP_IN), rev),
         pl.BlockSpec((tm, CONV_DIM), rev), pl.BlockSpec((tm, POOL_DIM), rev),
         pl.BlockSpec((CONV_WIDTH, CONV_DIM), lambda i: (0, 0)), vec, vec,
         pl.BlockSpec((len(POOL_WINDOWS), POOL_GROUP, POOL_GROUP), lambda i: (0, 0, 0)), vec],
        (dmix, z, c, pm, conv_w, ln_g, ln_b, pool_w, pool_scale), grid=(nt,),
        out_specs=[pl.BlockSpec((tm, CP_IN), rev), pl.BlockSpec((CONV_WIDTH + 1, CONV_DIM), lambda i: (0, 0)),
                   pl.BlockSpec((8, CONV_DIM), lambda i: (0, 0)),
                   pl.BlockSpec((len(POOL_WINDOWS), POOL_GROUP, POOL_GROUP), lambda i: (0, 0, 0))],
        out_shape=[jax.ShapeDtypeStruct((t, CP_IN), BF16), jax.ShapeDtypeStruct((CONV_WIDTH + 1, CONV_DIM), F32),
                   jax.ShapeDtypeStruct((8, CONV_DIM), F32),
                   jax.ShapeDtypeStruct((len(POOL_WINDOWS), POOL_GROUP, POOL_GROUP), F32)],
        scratch_shapes=[pltpu.VMEM((tm + CONV_HALO, CONV_DIM), F32), pltpu.VMEM((tm + POOL_HALO, POOL_DIM), F32),
                        pltpu.VMEM((tm, CONV_DIM), F32), pltpu.VMEM((CONV_WIDTH + 1, 8, CONV_DIM), F32)],
        compiler_params=_cparams("arbitrary"), name="cp_seq_bwd")


GLA_UNROLL = 2
Q0, K0, V0, G0, R0 =0, GLA_DK, 2 * GLA_DK, 2 * GLA_DK + GLA_DV, 2 * GLA_DK + 2 * GLA_DV


def _split3(x):
    hi = x.astype(BF16)
    r1 = x - hi.astype(F32)
    mid = r1.astype(BF16)
    lo = (r1 - mid.astype(F32)).astype(BF16)
    return hi, mid, lo


def _tri(strict):
    r = lax.broadcasted_iota(jnp.int32, (CHUNK, CHUNK), 0)
    c = lax.broadcasted_iota(jnp.int32, (CHUNK, CHUNK), 1)
    return ((r > c) if strict else (r >= c)).astype(BF16)


def _gate_decay(r, gw_ref, gb_ref, tri):
    pre = _dot(r, gw_ref[...]) + gb_ref[...]
    lac = (jnp.minimum(pre, 0.0) - jnp.log(1.0 + jnp.exp(-jnp.abs(pre)))) * (1.0 / GATE_NORM)
    hi, mid, lo = _split3(lac)
    cum = _dot(tri, hi) + _dot(tri, mid) + _dot(tri, lo)
    return pre, cum, cum[CHUNK - 1:CHUNK, :]


def _gla_seq_fwd(z, gate_w, gate_b, head_g):
    t = z.shape[0]
    tm = _row_tile(t, ROW_TILE_TARGET, CHUNK)
    cpt = tm // CHUNK
    scale = GLA_HK ** -0.5

    def body(z_ref, gw_ref, gb_ref, hg_ref, o_ref, mix_ref, st_ref, state):
        @pl.when(pl.program_id(0) == 0)
        def _():
            state[...] = jnp.zeros_like(state)

        tri = _tri(False)

        def chunk(ci, carry):
            r0 = pl.multiple_of(ci * CHUNK, CHUNK)
            rows = pl.ds(r0, CHUNK)
            _, cum, tot = _gate_decay(z_ref[rows, R0:R0 + GATE_PAD], gw_ref, gb_ref, tri)
            dec = jnp.exp(tot - cum)
            e = jnp.exp(tot)
            st_ref[ci] = state[...].astype(BF16)
            for hd in range(GLA_HEADS):
                ks = slice(hd * GLA_HK, (hd + 1) * GLA_HK)
                vs = slice(hd * GLA_HV, (hd + 1) * GLA_HV)
                kdec = (z_ref[rows, K0 + hd * GLA_HK:K0 + (hd + 1) * GLA_HK].astype(F32) * dec[:, ks]).astype(BF16)
                v = z_ref[rows, V0 + hd * GLA_HV:V0 + (hd + 1) * GLA_HV]
                st = state[vs, :] * e[:, ks] + _dot_tn(v, kdec)
                state[vs, :] = st
                q = z_ref[rows, Q0 + hd * GLA_HK:Q0 + (hd + 1) * GLA_HK]
                o = _dot_nt(q, st.astype(BF16)) * scale
                ob = o.astype(BF16)
                o_ref[rows, vs] = ob
                on = _rms(ob.astype(F32), hg_ref[...])
                gv = z_ref[rows, G0 + hd * GLA_HV:G0 + (hd + 1) * GLA_HV].astype(F32)
                mix_ref[rows, vs] = (on * _silu(gv)).astype(BF16)
            return carry

        lax.fori_loop(0, cpt, chunk, 0, unroll=GLA_UNROLL)

    return pl.pallas_call(
        body, grid=(t // tm,),
        in_specs=[pl.BlockSpec((tm, GLA_IN_PAD), lambda i: (i, 0)),
                  pl.BlockSpec((GATE_PAD, GLA_DK), lambda i: (0, 0)), pl.BlockSpec((1, GLA_DK), lambda i: (0, 0)),
                  pl.BlockSpec((1, GLA_HV), lambda i: (0, 0))],
        out_specs=[pl.BlockSpec((tm, GLA_DV), lambda i: (i, 0)), pl.BlockSpec((tm, GLA_DV), lambda i: (i, 0)),
                   pl.BlockSpec((cpt, GLA_DV, GLA_HK), lambda i: (i, 0, 0))],
        out_shape=[jax.ShapeDtypeStruct((t, GLA_DV), BF16), jax.ShapeDtypeStruct((t, GLA_DV), BF16),
                   jax.ShapeDtypeStruct((t // CHUNK, GLA_DV, GLA_HK), BF16)],
        scratch_shapes=[pltpu.VMEM((GLA_DV, GLA_HK), F32)],
        compiler_params=_cparams("arbitrary"), name="gla_seq_fwd")(z, gate_w, gate_b, head_g)


def _gla_seq_bwd(dmix, o, z, states, gate_w, gate_b, head_g, dep=None):
    t = z.shape[0]
    tm = _row_tile(t, ROW_TILE_TARGET, CHUNK)
    cpt = tm // CHUNK
    nt = t // tm
    scale = GLA_HK ** -0.5

    def body(dmix_ref, o_ref, z_ref, st_ref, gw_ref, gb_ref, hg_ref, dz_ref, dgw_ref, dgb_ref, dhg_ref,
             dstate, dec_s, kdec_s, dkdec_s, do_s, e_s, dtot_s):
        @pl.when(pl.program_id(0) == 0)
        def _():
            dstate[...] = jnp.zeros_like(dstate)
            dgw_ref[...] = jnp.zeros_like(dgw_ref)
            dgb_ref[...] = jnp.zeros_like(dgb_ref)
            dhg_ref[...] = jnp.zeros_like(dhg_ref)

        def chunk_sums(x, strict, pieces):
            tri3 = jnp.broadcast_to(_tri(strict)[None], (cpt, CHUNK, CHUNK))
            acc = None
            for piece in _split3(x.reshape(cpt, CHUNK, GLA_DK))[:pieces]:
                part = jnp.einsum("bij,bjk->bik", tri3, piece, preferred_element_type=F32)
                acc = part if acc is None else acc + part
            return acc

        def gate_pre():
            return _dot(z_ref[:, R0:R0 + GATE_PAD], gw_ref[...]) + gb_ref[...]

        pre = gate_pre()
        lac = (jnp.minimum(pre, 0.0) - jnp.log(1.0 + jnp.exp(-jnp.abs(pre)))) * (1.0 / GATE_NORM)
        cum3 = chunk_sums(lac, False, 3)
        tot3 = cum3[:, CHUNK - 1:CHUNK, :]
        dec = jnp.exp(jnp.broadcast_to(tot3, cum3.shape) - cum3).reshape(tm, GLA_DK)
        dec_s[...] = dec
        kdec_s[...] = z_ref[:, K0:K0 + GLA_DK].astype(F32) * dec
        e_s[...] = jnp.exp(jnp.broadcast_to(tot3, (cpt, 8, GLA_DK))).reshape(cpt * 8, GLA_DK)
        dhg = jnp.zeros((1, GLA_HV), F32)
        for hd in range(GLA_HEADS):
            vs = slice(hd * GLA_HV, (hd + 1) * GLA_HV)
            gcols = slice(G0 + hd * GLA_HV, G0 + (hd + 1) * GLA_HV)
            ov = o_ref[:, vs].astype(F32)
            gv = z_ref[:, gcols].astype(F32)
            dm = dmix_ref[:, vs].astype(F32)
            sg = jax.nn.sigmoid(gv)
            rr = lax.rsqrt(jnp.mean(ov * ov, axis=-1, keepdims=True) + EPS)
            xhat = ov * rr
            don = dm * (gv * sg)
            dz_ref[:, gcols] = (dm * (xhat * hg_ref[...]) * (sg * (1.0 + gv * (1.0 - sg)))).astype(BF16)
            dhg = dhg + jnp.sum(don * xhat, axis=0, keepdims=True)
            dxh = don * hg_ref[...]
            do_s[:, vs] = (rr * (dxh - xhat * jnp.mean(dxh * xhat, axis=-1, keepdims=True)) * scale).astype(BF16)
        dhg_ref[...] += dhg

        def chunk(cj, carry):
            ci = cpt - 1 - cj
            rows = pl.ds(pl.multiple_of(ci * CHUNK, CHUNK), CHUNK)
            erows = pl.ds(pl.multiple_of(ci * 8, 8), 8)
            e_all = e_s[erows, :][0:1, :]
            for hd in range(GLA_HEADS):
                ks = slice(hd * GLA_HK, (hd + 1) * GLA_HK)
                vs = slice(hd * GLA_HV, (hd + 1) * GLA_HV)
                e = e_all[:, ks]
                kdb = kdec_s[rows, ks].astype(BF16)
                v = z_ref[rows, V0 + hd * GLA_HV:V0 + (hd + 1) * GLA_HV]
                q = z_ref[rows, Q0 + hd * GLA_HK:Q0 + (hd + 1) * GLA_HK]
                do = do_s[rows, vs]
                st_prev = st_ref[ci, vs, :].astype(F32)
                st = st_prev * e + _dot_tn(v, kdb)
                dz_ref[rows, Q0 + hd * GLA_HK:Q0 + (hd + 1) * GLA_HK] = _dot(do, st.astype(BF16)).astype(BF16)
                dst = dstate[vs, :] + _dot_tn(do, q)
                dstb = dst.astype(BF16)
                dkdec_s[rows, ks] = _dot(v, dstb)
                dz_ref[rows, V0 + hd * GLA_HV:V0 + (hd + 1) * GLA_HV] = _dot_nt(kdb, dstb).astype(BF16)
                dtot = jnp.sum(dst * st_prev, axis=0, keepdims=True) * e
                dtot_s[erows, ks] = jnp.broadcast_to(dtot, (8, GLA_HK))
                dstate[vs, :] = dst * e
            return carry

        lax.fori_loop(0, cpt, chunk, 0)

        dkdec = dkdec_s[...]
        dz_ref[:, K0:K0 + GLA_DK] = (dkdec * dec_s[...]).astype(BF16)
        before = chunk_sums(dkdec * kdec_s[...], True, 2)
        dtot3 = dtot_s[...].reshape(cpt, 8, GLA_DK)[:, 0:1, :]
        dlac = (jnp.broadcast_to(dtot3, before.shape) + before).reshape(tm, GLA_DK)
        dpre = dlac * (1.0 / GATE_NORM) * (1.0 - jax.nn.sigmoid(gate_pre()))
        dpb = dpre.astype(BF16)
        dz_ref[:, R0:R0 + GATE_PAD] = _dot_nt(dpb, gw_ref[...]).astype(BF16)
        dgw_ref[...] += _dot_tn(z_ref[:, R0:R0 + GATE_PAD], dpb)
        dgb_ref[...] += jnp.sum(dpre, axis=0, keepdims=True)

    rev = lambda i: (nt - 1 - i, 0)
    return _call_after(
        dep, body, 7,
        [pl.BlockSpec((tm, GLA_DV), rev), pl.BlockSpec((tm, GLA_DV), rev), pl.BlockSpec((tm, GLA_IN_PAD), rev),
         pl.BlockSpec((cpt, GLA_DV, GLA_HK), lambda i: (nt - 1 - i, 0, 0)),
         pl.BlockSpec((GATE_PAD, GLA_DK), lambda i: (0, 0)), pl.BlockSpec((1, GLA_DK), lambda i: (0, 0)),
         pl.BlockSpec((1, GLA_HV), lambda i: (0, 0))],
        (dmix, o, z, states, gate_w, gate_b, head_g), grid=(nt,),
        out_specs=[pl.BlockSpec((tm, GLA_IN_PAD), rev), pl.BlockSpec((GATE_PAD, GLA_DK), lambda i: (0, 0)),
                   pl.BlockSpec((1, GLA_DK), lambda i: (0, 0)), pl.BlockSpec((1, GLA_HV), lambda i: (0, 0))],
        out_shape=[jax.ShapeDtypeStruct((t, GLA_IN_PAD), BF16), jax.ShapeDtypeStruct((GATE_PAD, GLA_DK), F32),
                   jax.ShapeDtypeStruct((1, GLA_DK), F32), jax.ShapeDtypeStruct((1, GLA_HV), F32)],
        scratch_shapes=[pltpu.VMEM((GLA_DV, GLA_HK), F32), pltpu.VMEM((tm, GLA_DK), F32), pltpu.VMEM((tm, GLA_DK), F32),
                        pltpu.VMEM((tm, GLA_DK), F32), pltpu.VMEM((tm, GLA_DV), BF16),
                        pltpu.VMEM((cpt * 8, GLA_DK), F32), pltpu.VMEM((cpt * 8, GLA_DK), F32)],
        compiler_params=_cparams("arbitrary"), name="gla_seq_bwd")


def _sum_halves(g, recv, c_idx, name):
    n, r, cdim = g.shape
    h = r // 2
    tr = _row_tile(h, 256, 8)
    nh = h // tr

    def body(c_ref, g_ref, r_ref, o_ref):
        o_ref[...] = (g_ref[...] + r_ref[...]).astype(BF16)

    return pl.pallas_call(
        body,
        grid_spec=pltpu.PrefetchScalarGridSpec(
            num_scalar_prefetch=1, grid=(n, nh),
            in_specs=[pl.BlockSpec((None, tr, cdim), lambda s, i, c: (s, c[0] * nh + i, 0)),
                      pl.BlockSpec((None, tr, cdim), lambda s, i, c: (s, i, 0))],
            out_specs=pl.BlockSpec((None, tr, cdim), lambda s, i, c: (s, i, 0))),
        out_shape=jax.ShapeDtypeStruct((n, h, cdim), BF16),
        compiler_params=_cparams("parallel", "parallel"), name=name)(c_idx, g, recv)


def _sum_slots(x, name):
    n, r, cdim = x.shape
    tr = _row_tile(r, 256, 8)

    def body(x_ref, o_ref):
        acc = x_ref[0].astype(F32)
        for j in range(1, n):
            acc = acc + x_ref[j].astype(F32)
        o_ref[...] = acc

    return pl.pallas_call(
        body, grid=(r // tr,),
        in_specs=[pl.BlockSpec((n, tr, cdim), lambda i: (0, i, 0))],
        out_specs=pl.BlockSpec((tr, cdim), lambda i: (i, 0)),
        out_shape=jax.ShapeDtypeStruct((r, cdim), F32),
        compiler_params=_cparams("parallel"), name=name)(x)


def _sum_own_and_slots(own, slots, chip_idx, name):
    n, r, cdim = own.shape
    tr = _row_tile(r, 256, 8)

    def body(s_ref, own_ref, a_ref, b_ref, c_ref, o_ref):
        o_ref[...] = (own_ref[...].astype(F32) + a_ref[...].astype(F32) + b_ref[...].astype(F32)
                      + c_ref[...].astype(F32))

    def slot(dd):
        return pl.BlockSpec((None, tr, cdim), lambda i, s: ((s[0] + dd) % n, i, 0))

    return pl.pallas_call(
        body,
        grid_spec=pltpu.PrefetchScalarGridSpec(
            num_scalar_prefetch=1, grid=(r // tr,), in_specs=[slot(0), slot(1), slot(2), slot(3)],
            out_specs=pl.BlockSpec((tr, cdim), lambda i, s: (i, 0))),
        out_shape=jax.ShapeDtypeStruct((r, cdim), F32),
        compiler_params=_cparams("parallel"), name=name)(chip_idx, own, slots, slots, slots)


def _add2(a, b, name):
    r, cdim = a.shape
    tr = _row_tile(r, 256, 8)

    def body(a_ref, b_ref, o_ref):
        o_ref[...] = a_ref[...] + b_ref[...]

    spec = pl.BlockSpec((tr, cdim), lambda i: (i, 0))
    return pl.pallas_call(body, grid=(r // tr,), in_specs=[spec, spec], out_specs=spec,
                          out_shape=jax.ShapeDtypeStruct((r, cdim), F32),
                          compiler_params=_cparams("parallel"), name=name)(a, b)


def _adamw(w, g, m, v, name):
    r, cdim = w.shape
    tr = _row_tile(r, 256, 8)

    def body(w_ref, g_ref, m_ref, v_ref, go_ref, d_ref, mo_ref, vo_ref):
        gv = g_ref[...]
        go_ref[...] = gv
        mn = ADAM_B1 * m_ref[...] + (1.0 - ADAM_B1) * gv
        vn = ADAM_B2 * v_ref[...] + (1.0 - ADAM_B2) * (gv * gv)
        m_hat = mn / (1.0 - ADAM_B1 ** ADAM_STEP)
        v_hat = vn / (1.0 - ADAM_B2 ** ADAM_STEP)
        d_ref[...] = -ADAM_LR * (m_hat / (jnp.sqrt(v_hat) + ADAM_EPS) + ADAM_WD * w_ref[...])
        mo_ref[...] = mn
        vo_ref[...] = vn

    spec = pl.BlockSpec((tr, cdim), lambda i: (i, 0))
    shp = jax.ShapeDtypeStruct((r, cdim), F32)
    return pl.pallas_call(body, grid=(r // tr,), in_specs=[spec] * 4, out_specs=[spec] * 4,
                          out_shape=[shp] * 4, compiler_params=_cparams("parallel"), name=name)(w, g, m, v)


def _adamw_many(ws, gs, ms, vs):
    n = len(ws)

    def body(*refs):
        for i in range(n):
            w_ref, g_ref, m_ref, v_ref = refs[i], refs[n + i], refs[2 * n + i], refs[3 * n + i]
            d_ref, mo_ref, vo_ref = refs[4 * n + i], refs[5 * n + i], refs[6 * n + i]
            gv = g_ref[...]
            mn = ADAM_B1 * m_ref[...] + (1.0 - ADAM_B1) * gv
            vn = ADAM_B2 * v_ref[...] + (1.0 - ADAM_B2) * (gv * gv)
            m_hat = mn / (1.0 - ADAM_B1 ** ADAM_STEP)
            v_hat = vn / (1.0 - ADAM_B2 ** ADAM_STEP)
            d_ref[...] = -ADAM_LR * (m_hat / (jnp.sqrt(v_hat) + ADAM_EPS) + ADAM_WD * w_ref[...])
            mo_ref[...] = mn
            vo_ref[...] = vn

    shapes = [jax.ShapeDtypeStruct(w.shape, F32) for w in ws]
    outs = pl.pallas_call(body, out_shape=shapes * 3, name="adamw_small")(*ws, *gs, *ms, *vs)
    return outs[:n], outs[n:2 * n], outs[2 * n:]


def _split_rows(a):
    return a.reshape(a.shape[0], 2, a.shape[1] // 2, a.shape[2])


def _place():
    x, y, c = lax.axis_index("x"), lax.axis_index("y"), lax.axis_index("c")
    chips = [(1 - x, y), (x, 1 - y), (1 - x, 1 - y)]
    return x, y, c, chips


def _remote(src, dst, send_sem, recv_sem, to):
    return pltpu.make_async_remote_copy(src_ref=src, dst_ref=dst, send_sem=send_sem, recv_sem=recv_sem,
                                        device_id=to, device_id_type=MESH)


def _plan_gather(src_refs, land_refs):
    x, y, c, chips = _place()
    me = 2 * x + y
    return [(src.at[c], land.at[me, c], (px, py, c), land.at[2 * px + py, c])
            for src, land in zip(src_refs, land_refs) for (px, py) in chips]


def _plan_scatter(n_parts):
    def plan(src_refs, land_refs):
        x, y, c, chips = _place()
        me = 2 * x + y
        copies = []
        for k, (src, land) in enumerate(zip(src_refs, land_refs)):
            for (px, py) in chips:
                to = 2 * px + py
                copies.append((src.at[to] if k < n_parts else src, land.at[me], (px, py, c), land.at[to]))
        return copies
    return plan


def _plan_exchange(n_split):
    def plan(src_refs, land_refs):
        x, y, c, _ = _place()
        sib = (x, y, 1 - c)
        return [(src.at[:, 1 - c] if k < n_split else src, land, sib, land)
                for k, (src, land) in enumerate(zip(src_refs, land_refs))]
    return plan


def _hbm(a):
    return pltpu.HBM(a.shape, a.dtype)


def _start_copies(name, srcs, lands, plan, ncopy, dep=None):
    ns, nl = len(srcs), len(lands)
    nin = ns + nl + (0 if dep is None else 1)

    def body(*refs):
        send_sems, recv_sems, token = refs[nin], refs[nin + 1], refs[-1]
        for k, (src, dst, dev, _) in enumerate(plan(refs[:ns], refs[ns:ns + nl])):
            _remote(src, dst, send_sems.at[k], recv_sems.at[k], dev).start()
        token[...] = jnp.zeros_like(token)

    args = [pltpu.with_memory_space_constraint(a, pltpu.HBM) for a in list(srcs) + list(lands)]
    outs = pl.pallas_call(
        body, name=name,
        out_shape=(pltpu.SemaphoreType.DMA((ncopy,)), pltpu.SemaphoreType.DMA((ncopy,)),
                   *[_hbm(a) for a in list(srcs) + list(lands)], jax.ShapeDtypeStruct((8, 128), F32)),
        in_specs=[HBM_SPEC] * (ns + nl) + ([] if dep is None else [ANY_SPEC]),
        out_specs=(SEM_SPEC, SEM_SPEC, *([HBM_SPEC] * (ns + nl)), pl.BlockSpec(memory_space=pltpu.VMEM)),
        input_output_aliases={i: 2 + i for i in range(ns + nl)},
        compiler_params=pltpu.CompilerParams(has_side_effects=SIDE_EFFECT),
    )(*args, *([] if dep is None else [dep]))
    return outs[0], outs[1], list(outs[2:2 + ns]), list(outs[2 + ns:2 + ns + nl]), outs[-1]


def _wait_copies(name, started, plan, after):
    send_sems, recv_sems, srcs, lands, _ = started
    ns, nl = len(srcs), len(lands)

    def body(*refs):
        send_ref, recv_ref = refs[ns + nl], refs[ns + nl + 1]
        for k, (src, _, dev, mine) in enumerate(plan(refs[:ns], refs[ns:ns + nl])):
            copy = _remote(src, mine, send_ref.at[k], recv_ref.at[k], dev)
            copy.wait_send()
            copy.wait_recv()

    outs = pl.pallas_call(
        body, name=name, out_shape=tuple(_hbm(a) for a in srcs + lands),
        in_specs=[HBM_SPEC] * (ns + nl) + [SEM_SPEC, SEM_SPEC, ANY_SPEC], out_specs=tuple([HBM_SPEC] * (ns + nl)),
        input_output_aliases={i: i for i in range(ns + nl)},
        compiler_params=pltpu.CompilerParams(has_side_effects=SIDE_EFFECT),
    )(*srcs, *lands, send_sems, recv_sems, after)
    return list(outs[:ns]), list(outs[ns:])


def _share_with_sibling(name, srcs, lands):
    n = len(srcs)

    def body(*refs):
        src_refs, land_refs, out_refs = refs[:n], refs[n:2 * n], refs[2 * n:3 * n]
        send_sem, recv_sem = refs[3 * n:]
        x, y, c, chips = _place()
        me = 2 * x + y
        sib = (x, y, 1 - c)
        sends, recvs = [], []
        for k in range(n):
            sems = (send_sem.at[4 * k], recv_sem.at[4 * k])
            sends.append(_remote(src_refs[k], out_refs[k].at[me], *sems, sib))
            recvs.append(_remote(src_refs[k], out_refs[k].at[me], *sems, sib))
            for j, (px, py) in enumerate(chips):
                frm = 2 * px + py
                sems = (send_sem.at[4 * k + 1 + j], recv_sem.at[4 * k + 1 + j])
                sends.append(_remote(land_refs[k].at[frm, c], out_refs[k].at[frm, c], *sems, sib))
                recvs.append(_remote(land_refs[k].at[frm, c], out_refs[k].at[frm, 1 - c], *sems, sib))
        for cp in sends:
            cp.start()
        for cp in recvs:
            cp.wait_recv()
        for cp in sends:
            cp.wait_send()

    return pl.pallas_call(
        body, name=name, in_specs=[HBM_SPEC] * (2 * n), out_specs=[HBM_SPEC] * n,
        out_shape=[jax.ShapeDtypeStruct(a.shape, a.dtype) for a in lands],
        input_output_aliases={n + k: k for k in range(n)},
        scratch_shapes=[pltpu.SemaphoreType.DMA((4 * n,)), pltpu.SemaphoreType.DMA((4 * n,))],
    )(*srcs, *lands)


def _gather_weights(bigs, small):
    nb = len(bigs)
    shapes = [b.shape for b in bigs]
    bigs = [_split_rows(b) for b in bigs]

    def body(*refs):
        big_in, small_in = refs[:nb], refs[nb]
        big_out, small_out = refs[nb + 1:2 * nb + 1], refs[2 * nb + 1]
        ici_send, ici_recv, d2d_send, d2d_recv, own_send, own_recv, loc_sem = refs[2 * nb + 2:]
        x, y, c, chips = _place()
        me = 2 * x + y
        sib = (x, y, 1 - c)

        def half(ref, k, which):
            return ref.at[:, which]

        local = [pltpu.make_async_copy(small_in, small_out.at[me], loc_sem.at[0])]
        for cp in local:
            cp.start()
        sends = [_remote(big_in[k], big_out[k].at[me], own_send.at[k], own_recv.at[k], sib) for k in range(nb)]
        for j, (px, py) in enumerate(chips):
            for k in range(nb):
                sends.append(_remote(half(big_in[k], k, c), half(big_out[k].at[me], k, c),
                                     ici_send.at[k * 3 + j], ici_recv.at[k * 3 + j], (px, py, c)))
            sends.append(_remote(small_in, small_out.at[me], ici_send.at[nb * 3 + j], ici_recv.at[nb * 3 + j], (px, py, c)))
        for cp in sends:
            cp.start()
        passed = []
        for j, (px, py) in enumerate(chips):
            frm = 2 * px + py
            for k in range(nb):
                landed = half(big_out[k].at[frm], k, c)
                _remote(landed, landed, ici_send.at[k * 3 + j], ici_recv.at[k * 3 + j], (px, py, c)).wait_recv()
                fwd = _remote(landed, landed, d2d_send.at[k * 3 + j], d2d_recv.at[k * 3 + j], sib)
                fwd.start()
                passed.append(fwd)
            _remote(small_in, small_out.at[frm], ici_send.at[nb * 3 + j], ici_recv.at[nb * 3 + j], (px, py, c)).wait_recv()
        for j, (px, py) in enumerate(chips):
            frm = 2 * px + py
            for k in range(nb):
                theirs = half(big_out[k].at[frm], k, 1 - c)
                _remote(theirs, theirs, d2d_send.at[k * 3 + j], d2d_recv.at[k * 3 + j], sib).wait_recv()
        for k in range(nb):
            _remote(big_in[k], big_out[k].at[me], own_send.at[k], own_recv.at[k], sib).wait_recv()
        for cp in sends + passed:
            cp.wait_send()
        for cp in local:
            cp.wait()

    out_shape = [jax.ShapeDtypeStruct((N_CHIPS,) + b.shape, b.dtype) for b in bigs]
    out_shape.append(jax.ShapeDtypeStruct((N_CHIPS,) + small.shape, small.dtype))
    outs = pl.pallas_call(
        body, in_specs=[HBM_SPEC] * (nb + 1), out_specs=[HBM_SPEC] * (nb + 1), out_shape=out_shape,
        scratch_shapes=[pltpu.SemaphoreType.DMA((3 * nb + 3,)), pltpu.SemaphoreType.DMA((3 * nb + 3,)),
                        pltpu.SemaphoreType.DMA((3 * nb,)), pltpu.SemaphoreType.DMA((3 * nb,)),
                        pltpu.SemaphoreType.DMA((nb,)), pltpu.SemaphoreType.DMA((nb,)),
                        pltpu.SemaphoreType.DMA((1,))],
        name="gather_weights")(*bigs, small)
    return [o.reshape((N_CHIPS,) + s) for o, s in zip(outs[:-1], shapes)] + [outs[-1]]


def _exchange_halves(grads, small, name):
    ng = len(grads)
    grads = [_split_rows(g) for g in grads]
    extra = [] if small is None else [small]
    nall = ng + len(extra)

    def body(*refs):
        ins, outs = refs[:nall], refs[nall:2 * nall]
        send_sem, recv_sem = refs[2 * nall:]
        x, y, c, _ = _place()
        sib = (x, y, 1 - c)
        copies = []
        for k in range(nall):
            src = ins[k].at[:, 1 - c] if k < ng else ins[k]
            copies.append(_remote(src, outs[k], send_sem.at[k], recv_sem.at[k], sib))
        for cp in copies:
            cp.start()
        for cp in copies:
            cp.wait_recv()
        for cp in copies:
            cp.wait_send()

    out_shape = [jax.ShapeDtypeStruct((g.shape[0], g.shape[2], g.shape[3]), g.dtype) for g in grads]
    out_shape += [jax.ShapeDtypeStruct(s.shape, s.dtype) for s in extra]
    return pl.pallas_call(
        body, in_specs=[HBM_SPEC] * nall, out_specs=[HBM_SPEC] * nall, out_shape=out_shape,
        scratch_shapes=[pltpu.SemaphoreType.DMA((nall,)), pltpu.SemaphoreType.DMA((nall,))],
        name=name)(*grads, *extra)


def _join_halves(reduced, dests, out_shapes, name):
    nr = len(reduced)

    def body(*refs):
        r_in = refs[:nr]
        outs = refs[nr:nr + len(out_shapes)]
        send_sem, recv_sem, back_send, back_recv = refs[nr + len(out_shapes):]
        x, y, c, _ = _place()
        sib = (x, y, 1 - c)
        sends, backs = [], []
        for k in range(nr):
            oi, layer = dests[k]
            sends.append(_remote(r_in[k], outs[oi].at[layer, c], send_sem.at[k], recv_sem.at[k], sib))
        for cp in sends:
            cp.start()
        for k in range(nr):
            oi, layer = dests[k]
            theirs = outs[oi].at[layer, 1 - c]
            _remote(r_in[k], theirs, send_sem.at[k], recv_sem.at[k], sib).wait_recv()
            back = _remote(theirs, theirs, back_send.at[k], back_recv.at[k], sib)
            back.start()
            backs.append(back)
        for k in range(nr):
            oi, layer = dests[k]
            mine = outs[oi].at[layer, c]
            _remote(mine, mine, back_send.at[k], back_recv.at[k], sib).wait_recv()
        for cp in sends + backs:
            cp.wait_send()

    split = [(s[0], 2, s[1] // 2, s[2]) for s in out_shapes]
    outs = pl.pallas_call(
        body, in_specs=[HBM_SPEC] * nr, out_specs=[HBM_SPEC] * len(out_shapes),
        out_shape=[jax.ShapeDtypeStruct(s, F32) for s in split],
        scratch_shapes=[pltpu.SemaphoreType.DMA((nr,)), pltpu.SemaphoreType.DMA((nr,)),
                        pltpu.SemaphoreType.DMA((nr,)), pltpu.SemaphoreType.DMA((nr,))],
        name=name)(*reduced)
    return [o.reshape(s) for o, s in zip(outs, out_shapes)]


def _pack(arrs):
    flat = jnp.concatenate([a.reshape(-1).astype(F32) for a in arrs])
    n = flat.shape[0]
    rows = -(-n // PACK_WIDTH)
    rows = -(-rows // 8) * 8
    return jnp.pad(flat, (0, rows * PACK_WIDTH - n)).reshape(rows, PACK_WIDTH)


def _unpack(buf, shapes):
    flat = buf.reshape(-1)
    out, off = [], 0
    for shp in shapes:
        n = 1
        for s in shp:
            n *= s
        out.append(flat[off:off + n].reshape(shp))
        off += n
    return out


def _unshard_cols(stacked):
    moved = jnp.moveaxis(stacked, 0, -2)
    return moved.reshape(moved.shape[:-2] + (moved.shape[-2] * moved.shape[-1],))


def _take_cols(blocks, start, width):
    bw = blocks.shape[2]
    pieces, lo = [], start
    while lo < start + width:
        b = lo // bw
        hi = min(start + width, (b + 1) * bw)
        pieces.append(blocks[b][:, lo - b * bw:hi - b * bw])
        lo = hi
    return jnp.concatenate(pieces, axis=1)


def _col_shard(full, s, width):
    return lax.dynamic_slice_in_dim(full, s * width, width, axis=full.ndim - 1)


def kernel(x, meta_tokens, mix_norm_g, ffn_norm_g, ffn_w1, ffn_w2, cp_w_in, cp_conv_w, cp_conv_b, cp_ln_g, cp_ln_b, cp_pool_w, cp_pool_scale, cp_w_out, gla_w_in, gla_gate_w2, gla_gate_b, gla_head_g, gla_w_out, final_norm_g, loss_target, m_meta_tokens, m_mix_norm_g, m_ffn_norm_g, m_ffn_w1, m_ffn_w2, m_cp_w_in, m_cp_conv_w, m_cp_conv_b, m_cp_ln_g, m_cp_ln_b, m_cp_pool_w, m_cp_pool_scale, m_cp_w_out, m_gla_w_in, m_gla_gate_w2, m_gla_gate_b, m_gla_head_g, m_gla_w_out, m_final_norm_g, v_meta_tokens, v_mix_norm_g, v_ffn_norm_g, v_ffn_w1, v_ffn_w2, v_cp_w_in, v_cp_conv_w, v_cp_conv_b, v_cp_ln_g, v_cp_ln_b, v_cp_pool_w, v_cp_pool_scale, v_cp_w_out, v_gla_w_in, v_gla_gate_w2, v_gla_gate_b, v_gla_head_g, v_gla_w_out, v_final_norm_g):
    d = D_MODEL
    chip = 2 * lax.axis_index("x") + lax.axis_index("y")
    core = lax.axis_index("c")
    seq = x.shape[1]
    t = seq + CHUNK

    big_w = [ffn_w1, ffn_w2, cp_w_in, cp_w_out, gla_w_in, gla_w_out]
    sharded_small = [meta_tokens, cp_conv_w, gla_gate_w2, gla_gate_b, gla_head_g]
    cpin_g, cpout_g, small_g = _gather_weights([cp_w_in.astype(BF16), cp_w_out.astype(BF16)], _pack(sharded_small))

    def halves(w):
        return w.astype(BF16).reshape(2, w.shape[0] // 2, w.shape[1])

    def start_gather(name, srcs, dep):
        lands = [lax.empty((N_CHIPS,) + s.shape, s.dtype) for s in srcs]
        return _start_copies(name, srcs, lands, _plan_gather, 3 * len(srcs), dep)

    def finish_gather(name, started, after):
        srcs, lands = _wait_copies(name + "_wait", started, _plan_gather, after)
        return [g.reshape(N_CHIPS, 2 * g.shape[2], g.shape[3]) for g in _share_with_sibling(name + "_share", srcs, lands)]

    ffn0_started = start_gather("gather_ffn0_start", [halves(ffn_w1[0]), halves(ffn_w2[0])], small_g)
    gla_started = start_gather("gather_gla_start", [halves(gla_w_in[0]), halves(gla_w_out[0])], ffn0_started[-1])
    ffn1_started = start_gather("gather_ffn1_start", [halves(ffn_w1[1]), halves(ffn_w2[1])], gla_started[-1])
    per_chip = [_unpack(small_g[j], [a.shape for a in sharded_small]) for j in range(N_CHIPS)]
    meta_f, conv_w_f, gate_w_f, gate_b_f, head_g_f = [
        jnp.concatenate([per_chip[j][i] for j in range(N_CHIPS)], axis=-1) for i in range(len(sharded_small))]
    conv_w_f, gate_w_f = conv_w_f[0], gate_w_f[0]
    w_cp_in = _unshard_cols(cpin_g[:, 0])
    w_cp_out = cpout_g.reshape(CONV_DIM + POOL_DIM, d)
    gate_w_pad = jnp.pad(gate_w_f, ((0, GATE_PAD - GATE_RANK), (0, 0))).astype(BF16)
    row = lambda a: a.reshape(1, -1)
    c_idx = core.reshape(1).astype(jnp.int32)
    chip_idx = chip.reshape(1).astype(jnp.int32)

    h0 = jnp.concatenate([jnp.zeros((PAD_ROWS, d), F32), meta_f, x[0]], axis=0)
    z0, u0 = _norm_matmul(h0, row(mix_norm_g[0]), w_cp_in, 512, "cp_in_proj", dep=ffn1_started[-1])
    c0, pm0, mix0 = _cp_seq_fwd(z0, conv_w_f, cp_conv_b, cp_ln_g, cp_ln_b, cp_pool_w[0], cp_pool_scale)
    h1 = _matmul_residual(mix0, w_cp_out, h0, "cp_out_proj")
    w1g0, w2g0 = finish_gather("gather_ffn0", ffn0_started, h1)
    h2, hp0, uf0 = _ffn_fwd(h1, row(ffn_norm_g[0]), w1g0, w2g0, "ffn0_fwd")
    glain_g, glaout_g = finish_gather("gather_gla", gla_started, h2)
    w_gla_in = jnp.concatenate([glain_g[j] for j in range(N_CHIPS)] + [jnp.zeros((d, GLA_IN_PAD - GLA_IN), BF16)], axis=1)
    w_gla_out = glaout_g.reshape(GLA_DV, d)
    z1, u2 = _norm_matmul(h2, row(mix_norm_g[1]), w_gla_in, 640, "gla_in_proj")
    o1, mix1, states = _gla_seq_fwd(z1, gate_w_pad, gate_b_f, head_g_f)
    h3 = _matmul_residual(mix1, w_gla_out, h2, "gla_out_proj")
    w1g1, w2g1 = finish_gather("gather_ffn1", ffn1_started, h3)
    h4, hp1, uf1 = _ffn_fwd(h3, row(ffn_norm_g[1]), w1g1, w2g1, "ffn1_fwd")

    def start_exchange(name, grads):
        srcs = [_split_rows(g) for g in grads]
        lands = [lax.empty((g.shape[0], g.shape[1] // 2, g.shape[2]), g.dtype) for g in grads]
        return _start_copies(name + "_exchange_start", srcs, lands, _plan_exchange(len(grads)), len(grads))

    def start_scatter(name, exchange, after):
        srcs, recv = _wait_copies(name + "_exchange_wait", exchange, _plan_exchange(len(exchange[2])), after)
        parts = [_sum_halves(g.reshape(g.shape[0], -1, g.shape[3]), r, c_idx, "%s_chip_sum_%d" % (name, k))
                 for k, (g, r) in enumerate(zip(srcs, recv))]
        lands = [lax.empty(p.shape, p.dtype) for p in parts]
        return _start_copies(name + "_scatter_start", parts, lands, _plan_scatter(len(parts)), 3 * len(parts))

    def finish_reduce(name, started, after):
        n = len(started[2])
        parts, lands = _wait_copies(name + "_scatter_wait", started, _plan_scatter(n), after)
        return [_sum_own_and_slots(p, s, chip_idx, "%s_slot_sum_%d" % (name, k)) for k, (p, s) in enumerate(zip(parts, lands))]

    dh4, d_final_g, loss_part = _loss_bwd(h4, row(final_norm_g), loss_target[0])

    dh3, dhp1, d_ffn_g1 = _ffn_bwd_data(dh4, h3, row(ffn_norm_g[1]), hp1, w1g1, w2g1, "ffn1_bwd")
    dw1_1 = _wgrad(uf1, dhp1, N_CHIPS, d, d, False, True, False, "ffn1_dw1")
    dw2_1 = _wgrad(hp1, dh4, N_CHIPS, d, d, True, False, True, "ffn1_dw2")
    ffn1_exchange = start_exchange("ffn1", [dw1_1, dw2_1])

    dmix1 = _dgrad(dh3, w_gla_out, "gla_out_dgrad", dep=ffn1_exchange[-1])
    dw_gla_out = _wgrad(mix1, dh3, 1, GLA_DV, d, False, False, False, "gla_out_dw")
    ffn1_reduce = start_scatter("ffn1", ffn1_exchange, dw_gla_out)
    dz1, d_gate_w, d_gate_b, d_head_g = _gla_seq_bwd(dmix1, o1, z1, states, gate_w_pad, gate_b_f, head_g_f,
                                                     dep=ffn1_reduce[-1])
    dh2, d_mix_g1 = _dgrad_norm_bwd(dz1, w_gla_in, h2, row(mix_norm_g[1]), dh3, 640, "gla_in_dgrad")
    dw_gla_in = _wgrad(u2, dz1, GLA_IN_PAD // 640, d, 640, False, True, False, "gla_in_dw")
    gla_in_shards = jnp.stack([_take_cols(dw_gla_in, j * (GLA_IN // N_CHIPS), GLA_IN // N_CHIPS) for j in range(N_CHIPS)])
    gla_exchange = start_exchange("gla", [gla_in_shards, dw_gla_out.reshape(N_CHIPS, -1, d)])

    dh1, dhp0, d_ffn_g0 = _ffn_bwd_data(dh2, h1, row(ffn_norm_g[0]), hp0, w1g0, w2g0, "ffn0_bwd", dep=gla_exchange[-1])
    gla_reduce = start_scatter("gla", gla_exchange, dh1)
    dw1_0 = _wgrad(uf0, dhp0, N_CHIPS, d, d, False, True, False, "ffn0_dw1", dep=gla_reduce[-1])
    dw2_0 = _wgrad(hp0, dh2, N_CHIPS, d, d, True, False, True, "ffn0_dw2")
    ffn0_exchange = start_exchange("ffn0", [dw1_0, dw2_0])

    dmix0 = _dgrad(dh1, w_cp_out, "cp_out_dgrad", dep=ffn0_exchange[-1])
    dw_cp_out = _wgrad(mix0, dh1, 1, CONV_DIM + POOL_DIM, d, False, False, False, "cp_out_dw")
    ffn0_reduce = start_scatter("ffn0", ffn0_exchange, dw_cp_out)
    dz0, d_conv_w, d_cp_vec, d_pool_w = _cp_seq_bwd(dmix0, z0, c0, pm0, conv_w_f, cp_ln_g, cp_ln_b, cp_pool_w[0],
                                                    cp_pool_scale, dep=ffn0_reduce[-1])
    dh0, d_mix_g0 = _dgrad_norm_bwd(dz0, w_cp_in, h0, row(mix_norm_g[0]), dh1, 512, "cp_in_dgrad")
    dw_cp_in = _wgrad(u0, dz0, N_CHIPS, d, CP_IN // N_CHIPS, False, True, False, "cp_in_dw")

    grad_x = dh0[CHUNK:][None]

    cp_grads = [dw_cp_in, dw_cp_out.reshape(N_CHIPS, -1, d)]
    small_full = [dh0[PAD_ROWS:CHUNK], jnp.concatenate([d_mix_g0, d_mix_g1], axis=0),
                  jnp.concatenate([d_ffn_g0, d_ffn_g1], axis=0), d_conv_w[:CONV_WIDTH][None],
                  d_cp_vec[0:1], d_cp_vec[1:2], d_cp_vec[2:3], d_pool_w[None], d_cp_vec[3:4],
                  d_gate_w[:GATE_RANK][None], d_gate_b, d_head_g, d_final_g[0], loss_part[0, 0:1]]
    small_mine = _pack(small_full)
    recv = _exchange_halves(cp_grads, small_mine, "cp_exchange")
    chip_sums = [_sum_halves(g, r, c_idx, "cp_chip_sum_%d" % k) for k, (g, r) in enumerate(zip(cp_grads, recv[:-1]))]
    small_chip = _add2(small_mine, recv[-1], "chip_sum_small")
    small_slots = lax.dynamic_update_slice(jnp.zeros((N_CHIPS,) + small_chip.shape, F32), small_chip[None], (chip, 0, 0))
    cp_lands = [lax.empty(p.shape, p.dtype) for p in chip_sums] + [small_slots]
    cp_reduce = _start_copies("cp_scatter_start", chip_sums + [small_chip], cp_lands, _plan_scatter(len(chip_sums)),
                              3 * (len(chip_sums) + 1))

    def adamw_big(names, grads):
        outs = {}
        for n, g in zip(names, grads):
            w, m, v = big[n]
            two_d = lambda a: a.reshape(-1, a.shape[-1])
            res = _adamw(two_d(w), two_d(g), two_d(m), two_d(v), "adamw_" + n)
            outs[n] = [o.reshape(w.shape) for o in res]
        return outs

    big = {"w1": (ffn_w1, m_ffn_w1, v_ffn_w1), "w2": (ffn_w2, m_ffn_w2, v_ffn_w2),
           "cp_in": (cp_w_in, m_cp_w_in, v_cp_w_in), "cp_out": (cp_w_out, m_cp_w_out, v_cp_w_out),
           "gla_in": (gla_w_in, m_gla_w_in, v_gla_w_in), "gla_out": (gla_w_out, m_gla_w_out, v_gla_w_out)}
    red_ffn1 = finish_reduce("ffn1", ffn1_reduce, cp_reduce[-1])
    red_gla = finish_reduce("gla", gla_reduce, cp_reduce[-1])
    red_ffn0 = finish_reduce("ffn0", ffn0_reduce, cp_reduce[-1])
    first = ["w1", "w2", "gla_in", "gla_out"]
    first_grads = _join_halves([red_ffn0[0], red_ffn1[0], red_ffn0[1], red_ffn1[1], red_gla[0], red_gla[1]],
                               [(0, 0), (0, 1), (1, 0), (1, 1), (2, 0), (3, 0)], [big[n][0].shape for n in first],
                               "join_halves_ffn_gla")
    big_out = adamw_big(first, first_grads)
    cp_parts, cp_slots = _wait_copies("cp_scatter_wait", cp_reduce, _plan_scatter(len(chip_sums)), big_out["gla_out"][1])
    red_cp = [_sum_own_and_slots(a, s, chip_idx, "cp_slot_sum_%d" % k)
              for k, (a, s) in enumerate(zip(cp_parts[:-1], cp_slots[:-1]))]
    small_red = _sum_slots(cp_slots[-1], "slot_sum_small")
    last = ["cp_in", "cp_out"]
    last_grads = _join_halves(red_cp, [(0, 0), (1, 0)], [big[n][0].shape for n in last], "join_halves_cp")
    big_out.update(adamw_big(last, last_grads))

    (g_meta, g_mix, g_ffn, g_conv_w, g_conv_b, g_ln_g, g_ln_b, g_pool_w, g_pool_scale, g_gate_w, g_gate_b, g_head,
     g_final, loss_sum) = _unpack(small_red, [a.shape for a in small_full])
    g_meta = _col_shard(g_meta, chip, meta_tokens.shape[-1])
    g_conv_w = _col_shard(g_conv_w, chip, cp_conv_w.shape[-1])
    g_gate_w = _col_shard(g_gate_w, chip, gla_gate_w2.shape[-1])
    g_gate_b = _col_shard(g_gate_b, chip, gla_gate_b.shape[-1])
    g_head = _col_shard(g_head, chip, gla_head_g.shape[-1])
    small_w = [meta_tokens, mix_norm_g, ffn_norm_g, cp_conv_w, cp_conv_b, cp_ln_g, cp_ln_b, cp_pool_w, cp_pool_scale,
               gla_gate_w2, gla_gate_b, gla_head_g, final_norm_g]
    small_m = [m_meta_tokens, m_mix_norm_g, m_ffn_norm_g, m_cp_conv_w, m_cp_conv_b, m_cp_ln_g, m_cp_ln_b, m_cp_pool_w,
               m_cp_pool_scale, m_gla_gate_w2, m_gla_gate_b, m_gla_head_g, m_final_norm_g]
    small_v = [v_meta_tokens, v_mix_norm_g, v_ffn_norm_g, v_cp_conv_w, v_cp_conv_b, v_cp_ln_g, v_cp_ln_b, v_cp_pool_w,
               v_cp_pool_scale, v_gla_gate_w2, v_gla_gate_b, v_gla_head_g, v_final_norm_g]
    small_g = [g_meta, g_mix, g_ffn, g_conv_w, g_conv_b, g_ln_g, g_ln_b, g_pool_w, g_pool_scale, g_gate_w, g_gate_b,
               g_head, g_final]
    shapes = [w.shape for w in small_w]
    small_g = [g.reshape(s) for g, s in zip(small_g, shapes)]
    at_least_2d = lambda arrs: [a.reshape(1, -1) if a.ndim == 1 else a for a in arrs]
    s_delta, s_m, s_v = _adamw_many(at_least_2d(small_w), at_least_2d(small_g), at_least_2d(small_m), at_least_2d(small_v))
    s_delta, s_m, s_v = [[a.reshape(s) for a, s in zip(group, shapes)] for group in (s_delta, s_m, s_v)]

    order = ["meta", "mix", "ffn", "w1", "w2", "cp_in", "conv_w", "conv_b", "ln_g", "ln_b", "pool_w", "pool_scale",
             "cp_out", "gla_in", "gate_w", "gate_b", "head", "gla_out", "final"]
    small_names = ["meta", "mix", "ffn", "conv_w", "conv_b", "ln_g", "ln_b", "pool_w", "pool_scale", "gate_w", "gate_b",
                   "head", "final"]
    big_names = ["w1", "w2", "cp_in", "cp_out", "gla_in", "gla_out"]
    table = {n: (small_g[i], s_delta[i], s_m[i], s_v[i]) for i, n in enumerate(small_names)}
    table.update({n: tuple(big_out[n]) for n in big_names})
    loss = loss_sum.reshape(())
    return (loss, grad_x, *[table[n][0] for n in order], *[table[n][1] for n in order],
            *[table[n][2] for n in order], *[table[n][3] for n in order])
```

```python
import functools

import jax
import jax.numpy as jnp
from jax import lax
from jax.experimental import pallas as pl
from jax.experimental.pallas import tpu as pltpu

F32 = jnp.float32
BF16 = jnp.bfloat16

D_MODEL = 1024
N_META = 16
CHUNK = 64
PAD_ROWS = CHUNK - N_META
EPS = 1e-5
CONV_DIM = 512
CONV_WIDTH = 31
CONV_HALO = 32
POOL_DIM = 512
POOL_WINDOWS = (2, 4, 8, 16)
POOL_GROUP = 128
POOL_HALO = 16
CP_IN = 2 * CONV_DIM + POOL_DIM
GLA_HEADS = 4
GLA_DK = 512
GLA_DV = 1024
GLA_HK = GLA_DK // GLA_HEADS
GLA_HV = GLA_DV // GLA_HEADS
GATE_RANK = 16
GATE_PAD = 128
GATE_NORM = 16.0
GLA_IN = 2 * GLA_DK + 2 * GLA_DV + GATE_RANK
GLA_IN_PAD = 2 * GLA_DK + 2 * GLA_DV + GATE_PAD
N_CHIPS = 4
ADAM_LR = 0.001
ADAM_B1 = 0.9
ADAM_B2 = 0.999
ADAM_EPS = 1e-08
ADAM_WD = 0.01
ADAM_STEP = 10

VMEM_LIMIT_BYTES = 56 * 1024 * 1024
ROW_TILE_TARGET = 832
TOKEN_TILE_TARGET = 1040
PACK_WIDTH = 1024
MESH = pl.DeviceIdType.MESH
HBM_SPEC = pl.BlockSpec(memory_space=pltpu.HBM)
ANY_SPEC = pl.BlockSpec(memory_space=pl.ANY)
SEM_SPEC = pl.BlockSpec(memory_space=pltpu.SEMAPHORE)
SIDE_EFFECT = pltpu.SideEffectType.DATAFLOW_SIDE_EFFECTING


def _cparams(*sem):
    return pltpu.CompilerParams(dimension_semantics=sem, vmem_limit_bytes=VMEM_LIMIT_BYTES)


def _row_tile(t, target, mult):
    best = mult
    for cand in range(mult, min(t, target) + 1, mult):
        if t % cand == 0:
            best = cand
    assert t % best == 0, (t, best)
    return best


def _rms(h, g):
    return h * lax.rsqrt(jnp.mean(h * h, axis=-1, keepdims=True) + EPS) * g


def _rms_bwd(h, g, du):
    r = lax.rsqrt(jnp.mean(h * h, axis=-1, keepdims=True) + EPS)
    xhat = h * r
    dxh = du * g
    dh = r * (dxh - xhat * jnp.mean(dxh * xhat, axis=-1, keepdims=True))
    return dh, du * xhat


def _valid_rows(i, tm):
    row = i * tm + lax.broadcasted_iota(jnp.int32, (tm, 1), 0)
    return row >= PAD_ROWS


def _dot(a, b):
    return jnp.dot(a, b, preferred_element_type=F32)


def _dot_nt(a, b):
    return lax.dot_general(a, b, (((1,), (1,)), ((), ())), preferred_element_type=F32)


def _dot_tn(a, b):
    return lax.dot_general(a, b, (((0,), (0,)), ((), ())), preferred_element_type=F32)


def _accumulate(ref, val, first):
    @pl.when(first)
    def _():
        ref[...] = val

    @pl.when(jnp.logical_not(first))
    def _():
        ref[...] += val


def _call_after(dep, body, n_in, in_specs, args, **kw):
    if dep is None:
        return pl.pallas_call(body, in_specs=in_specs, **kw)(*args)

    def with_dep(*refs):
        body(*refs[:n_in], *refs[n_in + 1:])

    return pl.pallas_call(with_dep, in_specs=list(in_specs) + [ANY_SPEC], **kw)(*args, dep)


def _norm_matmul(h, g, w, nc, name, dep=None):
    t, d = h.shape
    n = w.shape[1]
    tm = _row_tile(t, TOKEN_TILE_TARGET, 16)

    def body(h_ref, g_ref, w_ref, z_ref, u_ref):
        u = _rms(h_ref[...], g_ref[...]).astype(BF16)
        u_ref[...] = u
        for n0 in range(0, n, nc):
            z_ref[:, n0:n0 + nc] = _dot(u, w_ref[:, n0:n0 + nc]).astype(BF16)

    return _call_after(
        dep, body, 3,
        [pl.BlockSpec((tm, d), lambda i: (i, 0)), pl.BlockSpec((1, d), lambda i: (0, 0)),
         pl.BlockSpec((d, n), lambda i: (0, 0))], (h, g, w), grid=(t // tm,),
        out_specs=[pl.BlockSpec((tm, n), lambda i: (i, 0)), pl.BlockSpec((tm, d), lambda i: (i, 0))],
        out_shape=[jax.ShapeDtypeStruct((t, n), BF16), jax.ShapeDtypeStruct((t, d), BF16)],
        compiler_params=_cparams("parallel"), name=name)


def _matmul_residual(a, w, h, name):
    t, k = a.shape
    d = w.shape[1]
    tm = _row_tile(t, TOKEN_TILE_TARGET, 16)

    def body(a_ref, w_ref, h_ref, o_ref):
        o_ref[...] = h_ref[...] + _dot(a_ref[...], w_ref[...])

    return pl.pallas_call(
        body, grid=(t // tm,),
        in_specs=[pl.BlockSpec((tm, k), lambda i: (i, 0)), pl.BlockSpec((k, d), lambda i: (0, 0)),
                  pl.BlockSpec((tm, d), lambda i: (i, 0))],
        out_specs=pl.BlockSpec((tm, d), lambda i: (i, 0)),
        out_shape=jax.ShapeDtypeStruct((t, d), F32),
        compiler_params=_cparams("parallel"), name=name)(a, w, h)


def _ffn_fwd(h, g, w1g, w2g, name):
    t, d = h.shape
    ns, ffs = w1g.shape[0], w1g.shape[2]
    tm = _row_tile(t, TOKEN_TILE_TARGET, 16)

    def body(h_ref, g_ref, w1_ref, w2_ref, ho_ref, hp_ref, u_ref, acc_ref):
        s = pl.program_id(1)

        @pl.when(s == 0)
        def _():
            u_ref[...] = _rms(h_ref[...], g_ref[...]).astype(BF16)

        hp = _dot(u_ref[...], w1_ref[...])
        hp_ref[...] = hp.astype(BF16)
        a = jnp.maximum(hp, 0.0)
        _accumulate(acc_ref, _dot((a * a).astype(BF16), w2_ref[...]), s == 0)

        @pl.when(s == ns - 1)
        def _():
            ho_ref[...] = h_ref[...] + acc_ref[...]

    return pl.pallas_call(
        body, grid=(t // tm, ns),
        in_specs=[pl.BlockSpec((tm, d), lambda i, s: (i, 0)), pl.BlockSpec((1, d), lambda i, s: (0, 0)),
                  pl.BlockSpec((None, d, ffs), lambda i, s: (s, 0, 0)),
                  pl.BlockSpec((None, ffs, d), lambda i, s: (s, 0, 0))],
        out_specs=[pl.BlockSpec((tm, d), lambda i, s: (i, 0)), pl.BlockSpec((tm, ffs), lambda i, s: (i, s)),
                   pl.BlockSpec((tm, d), lambda i, s: (i, 0))],
        out_shape=[jax.ShapeDtypeStruct((t, d), F32), jax.ShapeDtypeStruct((t, ns * ffs), BF16),
                   jax.ShapeDtypeStruct((t, d), BF16)],
        scratch_shapes=[pltpu.VMEM((tm, d), F32)],
        compiler_params=_cparams("parallel", "arbitrary"), name=name)(h, g, w1g, w2g)


def _ffn_bwd_data(dh, h, g, hp, w1g, w2g, name, dep=None):
    t, d = h.shape
    ns, ffs = w1g.shape[0], w1g.shape[2]
    tm = _row_tile(t, ROW_TILE_TARGET, CHUNK)

    def body(dh_ref, h_ref, g_ref, hp_ref, w1_ref, w2_ref, dhi_ref, dhp_ref, dg_ref, acc_ref):
        i, s = pl.program_id(0), pl.program_id(1)
        da = _dot_nt(dh_ref[...].astype(BF16), w2_ref[...])
        dhp = (da * (2.0 * jnp.maximum(hp_ref[...].astype(F32), 0.0))).astype(BF16)
        dhp_ref[...] = dhp
        _accumulate(acc_ref, _dot_nt(dhp, w1_ref[...]), s == 0)

        @pl.when(s == ns - 1)
        def _():
            dhn, dgr = _rms_bwd(h_ref[...], g_ref[...], acc_ref[...])
            dhi_ref[...] = jnp.where(_valid_rows(i, tm), dh_ref[...] + dhn, 0.0)
            _accumulate(dg_ref, jnp.sum(dgr, axis=0, keepdims=True), i == 0)

    return _call_after(
        dep, body, 6,
        [pl.BlockSpec((tm, d), lambda i, s: (i, 0)), pl.BlockSpec((tm, d), lambda i, s: (i, 0)),
         pl.BlockSpec((1, d), lambda i, s: (0, 0)), pl.BlockSpec((tm, ffs), lambda i, s: (i, s)),
         pl.BlockSpec((None, d, ffs), lambda i, s: (s, 0, 0)),
         pl.BlockSpec((None, ffs, d), lambda i, s: (s, 0, 0))], (dh, h, g, hp, w1g, w2g), grid=(t // tm, ns),
        out_specs=[pl.BlockSpec((tm, d), lambda i, s: (i, 0)), pl.BlockSpec((tm, ffs), lambda i, s: (i, s)),
                   pl.BlockSpec((1, d), lambda i, s: (0, 0))],
        out_shape=[jax.ShapeDtypeStruct((t, d), F32), jax.ShapeDtypeStruct((t, ns * ffs), BF16),
                   jax.ShapeDtypeStruct((1, d), F32)],
        scratch_shapes=[pltpu.VMEM((tm, d), F32)],
        compiler_params=_cparams("arbitrary", "arbitrary"), name=name)


WGRAD_ROWS = 1024


def _wgrad(x, dy, nb, xc, yc, x_by_block, dy_by_block, relu2, name, dep=None):
    t = x.shape[0]
    tk = _row_tile(t - CHUNK, WGRAD_ROWS, CHUNK)

    def prep(xv):
        if relu2:
            xv = jnp.maximum(xv.astype(F32), 0.0)
            xv = xv * xv
        return xv.astype(BF16)

    def body(xh_ref, dyh_ref, x_ref, dy_ref, o_ref):
        k = pl.program_id(1)
        p = _dot_tn(prep(x_ref[...]), dy_ref[...].astype(BF16))

        @pl.when(k == 0)
        def _():
            o_ref[...] = p + _dot_tn(prep(xh_ref[...]), dyh_ref[...].astype(BF16))

        @pl.when(k > 0)
        def _():
            o_ref[...] += p

    def head(width, by_block):
        return pl.BlockSpec((CHUNK, width), (lambda b, k: (0, b)) if by_block else (lambda b, k: (0, 0)))

    def rest(width, by_block):
        def index(b, k):
            return pl.multiple_of(CHUNK + k * tk, CHUNK), (pl.multiple_of(b * width, 128) if by_block else 0)
        return pl.BlockSpec((pl.Element(tk), pl.Element(width)), index)

    return _call_after(
        dep, body, 4,
        [head(xc, x_by_block), head(yc, dy_by_block), rest(xc, x_by_block), rest(yc, dy_by_block)], (x, dy, x, dy),
        grid=(nb, (t - CHUNK) // tk),
        out_specs=pl.BlockSpec((None, xc, yc), lambda b, k: (b, 0, 0)),
        out_shape=jax.ShapeDtypeStruct((nb, xc, yc), F32),
        compiler_params=_cparams("parallel", "arbitrary"), name=name)


def _dgrad(dh, w, name, dep=None):
    t, d = dh.shape
    k = w.shape[0]
    tm = _row_tile(t, TOKEN_TILE_TARGET, 16)

    def body(dh_ref, w_ref, o_ref):
        o_ref[...] = _dot_nt(dh_ref[...].astype(BF16), w_ref[...]).astype(BF16)

    return _call_after(
        dep, body, 2,
        [pl.BlockSpec((tm, d), lambda i: (i, 0)), pl.BlockSpec((k, d), lambda i: (0, 0))], (dh, w), grid=(t // tm,),
        out_specs=pl.BlockSpec((tm, k), lambda i: (i, 0)),
        out_shape=jax.ShapeDtypeStruct((t, k), BF16),
        compiler_params=_cparams("parallel"), name=name)


def _dgrad_norm_bwd(dz, w, h, g, dh, nc, name):
    t, d = h.shape
    n = w.shape[1]
    tm = _row_tile(t, ROW_TILE_TARGET // 2, 16)

    def body(dz_ref, w_ref, h_ref, g_ref, dh_ref, dhi_ref, dg_ref):
        i = pl.program_id(0)
        du = jnp.zeros((tm, d), F32)
        for n0 in range(0, n, nc):
            du = du + _dot_nt(dz_ref[:, n0:n0 + nc], w_ref[:, n0:n0 + nc])
        dhn, dgr = _rms_bwd(h_ref[...], g_ref[...], du)
        dhi_ref[...] = jnp.where(_valid_rows(i, tm), dh_ref[...] + dhn, 0.0)
        _accumulate(dg_ref, jnp.sum(dgr, axis=0, keepdims=True), i == 0)

    return pl.pallas_call(
        body, grid=(t // tm,),
        in_specs=[pl.BlockSpec((tm, n), lambda i: (i, 0)), pl.BlockSpec((d, n), lambda i: (0, 0)),
                  pl.BlockSpec((tm, d), lambda i: (i, 0)), pl.BlockSpec((1, d), lambda i: (0, 0)),
                  pl.BlockSpec((tm, d), lambda i: (i, 0))],
        out_specs=[pl.BlockSpec((tm, d), lambda i: (i, 0)), pl.BlockSpec((1, d), lambda i: (0, 0))],
        out_shape=[jax.ShapeDtypeStruct((t, d), F32), jax.ShapeDtypeStruct((1, d), F32)],
        compiler_params=_cparams("arbitrary"), name=name)(dz, w, h, g, dh)


def _loss_bwd(h, g, target):
    t, d = h.shape
    tl = _row_tile(t - CHUNK, 1024, CHUNK)

    def body(h_ref, g_ref, t_ref, dh_ref, dg_ref, loss_ref):
        i = pl.program_id(0)
        hv, gv = h_ref[...], g_ref[...]
        err = _rms(hv, gv) - t_ref[...]
        part = 0.5 * jnp.sum(jnp.mean(err * err, axis=-1, keepdims=True), axis=0, keepdims=True)
        dhn, dgr = _rms_bwd(hv, gv, err * (1.0 / d))
        dh_ref[...] = dhn
        _accumulate(dg_ref, jnp.sum(dgr, axis=0, keepdims=True), i == 0)
        _accumulate(loss_ref, jnp.broadcast_to(part, (8, 128)), i == 0)

    shifted = pl.BlockSpec((pl.Element(tl), pl.Element(d)), lambda i: (pl.multiple_of(CHUNK + i * tl, CHUNK), 0))
    dh, dg, loss = pl.pallas_call(
        body, grid=((t - CHUNK) // tl,),
        in_specs=[shifted, pl.BlockSpec((1, d), lambda i: (0, 0)), pl.BlockSpec((tl, d), lambda i: (i, 0))],
        out_specs=[shifted, pl.BlockSpec((1, d), lambda i: (0, 0)), pl.BlockSpec((8, 128), lambda i: (0, 0))],
        out_shape=[jax.ShapeDtypeStruct((t, d), F32), jax.ShapeDtypeStruct((1, d), F32),
                   jax.ShapeDtypeStruct((8, 128), F32)],
        compiler_params=_cparams("arbitrary"), name="loss_bwd")(h, g, target)

    def zero_head(dh_ref, o_ref):
        o_ref[...] = jnp.zeros_like(o_ref)

    dh = pl.pallas_call(
        zero_head, grid=(1,), in_specs=[ANY_SPEC], out_specs=pl.BlockSpec((CHUNK, d), lambda i: (0, 0)),
        out_shape=jax.ShapeDtypeStruct((t, d), F32), input_output_aliases={0: 0}, name="loss_bwd_head")(dh)
    return dh, dg, loss


CONV_BLOCK = 32


def _silu(x):
    return x * jax.nn.sigmoid(x)


def _row_shifts(win):
    n = win.shape[0]
    return [win] + [pltpu.roll(win, n - j, 0) for j in range(1, 8)]


def _cp_seq_fwd(z, conv_w, conv_b, ln_g, ln_b, pool_w, pool_scale):
    t = z.shape[0]
    tm = _row_tile(t, ROW_TILE_TARGET, CHUNK)

    def body(z_ref, cw_ref, cb_ref, lg_ref, lb_ref, pw_ref, ps_ref, c_ref, pm_ref, mix_ref, gbuf, pbuf):
        i = pl.program_id(0)

        @pl.when(i == 0)
        def _():
            gbuf[0:CONV_HALO, :] = jnp.zeros((CONV_HALO, CONV_DIM), F32)
            pbuf[0:POOL_HALO, :] = jnp.zeros((POOL_HALO, POOL_DIM), F32)

        @pl.when(i > 0)
        def _():
            gbuf[0:CONV_HALO, :] = gbuf[tm:tm + CONV_HALO, :]
            pbuf[0:POOL_HALO, :] = pbuf[tm:tm + POOL_HALO, :]

        av = z_ref[:, 0:CONV_DIM].astype(F32)
        ag = z_ref[:, CONV_DIM:2 * CONV_DIM].astype(F32)
        gbuf[CONV_HALO:CONV_HALO + tm, :] = av * jax.nn.sigmoid(ag)
        pbuf[POOL_HALO:POOL_HALO + tm, :] = z_ref[:, 2 * CONV_DIM:CP_IN].astype(F32)

        def conv_block(rb, carry):
            base = pl.multiple_of(rb * CONV_BLOCK, CONV_BLOCK)
            shifted = _row_shifts(gbuf[pl.ds(base, CONV_BLOCK + CONV_HALO), :])
            acc = jnp.zeros((CONV_BLOCK, CONV_DIM), F32)
            for k in range(CONV_WIDTH):
                whole, part = divmod(CONV_HALO - (CONV_WIDTH - 1) + k, 8)
                acc = acc + cw_ref[k:k + 1, :] * shifted[part][8 * whole:8 * whole + CONV_BLOCK, :]
            c_ref[pl.ds(base, CONV_BLOCK), :] = acc + cb_ref[...]
            return carry

        lax.fori_loop(0, tm // CONV_BLOCK, conv_block, 0)

        c = c_ref[...]
        mu = jnp.mean(c, axis=-1, keepdims=True)
        xc = c - mu
        ln = xc * lax.rsqrt(jnp.mean(xc * xc, axis=-1, keepdims=True) + EPS) * lg_ref[...] + lb_ref[...]
        row = i * tm + lax.broadcasted_iota(jnp.int32, (tm, 1), 0)
        mix_ref[:, 0:CONV_DIM] = jnp.where(row >= PAD_ROWS, _silu(ln), 0.0).astype(BF16)

        tpos = (row - PAD_ROWS + 1).astype(F32)
        for gi, wdw in enumerate(POOL_WINDOWS):
            lo = POOL_GROUP * gi
            cur = pbuf[POOL_HALO:POOL_HALO + tm, lo:lo + POOL_GROUP]
            sacc = cur
            for j in range(1, wdw):
                sacc = sacc + pbuf[POOL_HALO - j:POOL_HALO - j + tm, lo:lo + POOL_GROUP]
            pm = (sacc / jnp.clip(tpos, 1.0, float(wdw)) - cur).astype(BF16)
            pm_ref[:, lo:lo + POOL_GROUP] = pm
            pg = _dot(pm, pw_ref[gi].astype(BF16))
            mix_ref[:, CONV_DIM + lo:CONV_DIM + lo + POOL_GROUP] = (pg * ps_ref[:, lo:lo + POOL_GROUP]).astype(BF16)

    vec = pl.BlockSpec((1, CONV_DIM), lambda i: (0, 0))
    return pl.pallas_call(
        body, grid=(t // tm,),
        in_specs=[pl.BlockSpec((tm, CP_IN), lambda i: (i, 0)),
                  pl.BlockSpec((CONV_WIDTH, CONV_DIM), lambda i: (0, 0)), vec, vec, vec,
                  pl.BlockSpec((len(POOL_WINDOWS), POOL_GROUP, POOL_GROUP), lambda i: (0, 0, 0)), vec],
        out_specs=[pl.BlockSpec((tm, CONV_DIM), lambda i: (i, 0)), pl.BlockSpec((tm, POOL_DIM), lambda i: (i, 0)),
                   pl.BlockSpec((tm, CONV_DIM + POOL_DIM), lambda i: (i, 0))],
        out_shape=[jax.ShapeDtypeStruct((t, CONV_DIM), F32), jax.ShapeDtypeStruct((t, POOL_DIM), BF16),
                   jax.ShapeDtypeStruct((t, CONV_DIM + POOL_DIM), BF16)],
        scratch_shapes=[pltpu.VMEM((tm + CONV_HALO, CONV_DIM), F32), pltpu.VMEM((tm + POOL_HALO, POOL_DIM), F32)],
        compiler_params=_cparams("arbitrary"), name="cp_seq_fwd")(z, conv_w, conv_b, ln_g, ln_b, pool_w, pool_scale)


def _cp_seq_bwd(dmix, z, c, pm, conv_w, ln_g, ln_b, pool_w, pool_scale, dep=None):
    t = z.shape[0]
    tm = _row_tile(t, ROW_TILE_TARGET, CHUNK)
    nt = t // tm

    def body(dmix_ref, z_ref, c_ref, pm_ref, cw_ref, lg_ref, lb_ref, pw_ref, ps_ref,
             dz_ref, dcw_ref, dvec_ref, dpw_ref, dcbuf, qbuf, glu_buf, dwacc):
        i = pl.program_id(0)
        tile = nt - 1 - i

        @pl.when(i == 0)
        def _():
            dcbuf[tm:tm + CONV_HALO, :] = jnp.zeros((CONV_HALO, CONV_DIM), F32)
            qbuf[tm:tm + POOL_HALO, :] = jnp.zeros((POOL_HALO, POOL_DIM), F32)
            dcw_ref[...] = jnp.zeros_like(dcw_ref)
            dwacc[...] = jnp.zeros_like(dwacc)
            dvec_ref[...] = jnp.zeros_like(dvec_ref)
            dpw_ref[...] = jnp.zeros_like(dpw_ref)

        @pl.when(i > 0)
        def _():
            dcbuf[tm:tm + CONV_HALO, :] = dcbuf[0:CONV_HALO, :]
            qbuf[tm:tm + POOL_HALO, :] = qbuf[0:POOL_HALO, :]

        row = tile * tm + lax.broadcasted_iota(jnp.int32, (tm, 1), 0)
        cv = c_ref[...]
        mu = jnp.mean(cv, axis=-1, keepdims=True)
        xc = cv - mu
        rstd = lax.rsqrt(jnp.mean(xc * xc, axis=-1, keepdims=True) + EPS)
        xhat = xc * rstd
        ln = xhat * lg_ref[...] + lb_ref[...]
        sg = jax.nn.sigmoid(ln)
        da = jnp.where(row >= PAD_ROWS, dmix_ref[:, 0:CONV_DIM].astype(F32), 0.0)
        dln = da * (sg * (1.0 + ln * (1.0 - sg)))
        dxh = dln * lg_ref[...]
        dc = rstd * (dxh - jnp.mean(dxh, axis=-1, keepdims=True) - xhat * jnp.mean(dxh * xhat, axis=-1, keepdims=True))
        dcbuf[0:tm, :] = dc
        dvec_ref[0:1, :] += jnp.sum(dc, axis=0, keepdims=True)
        dvec_ref[1:2, :] += jnp.sum(dln * xhat, axis=0, keepdims=True)
        dvec_ref[2:3, :] += jnp.sum(dln, axis=0, keepdims=True)

        av = z_ref[:, 0:CONV_DIM].astype(F32)
        sig_g = jax.nn.sigmoid(z_ref[:, CONV_DIM:2 * CONV_DIM].astype(F32))
        glu_buf[...] = av * sig_g

        def conv_block(rb, carry):
            base = pl.multiple_of(rb * CONV_BLOCK, CONV_BLOCK)
            shifted = _row_shifts(dcbuf[pl.ds(base, CONV_BLOCK + CONV_HALO), :])
            glu = glu_buf[pl.ds(base, CONV_BLOCK), :]
            acc = jnp.zeros((CONV_BLOCK, CONV_DIM), F32)
            for k in range(CONV_WIDTH):
                whole, part = divmod(CONV_WIDTH - 1 - k, 8)
                slab = shifted[part][8 * whole:8 * whole + CONV_BLOCK, :]
                acc = acc + cw_ref[k:k + 1, :] * slab
                prod = slab * glu
                part = prod[0:8]
                for q in range(1, CONV_BLOCK // 8):
                    part = part + prod[8 * q:8 * q + 8]
                dwacc[k] += part
            glu_buf[pl.ds(base, CONV_BLOCK), :] = acc
            return carry

        lax.fori_loop(0, tm // CONV_BLOCK, conv_block, 0)

        @pl.when(i == nt - 1)
        def _():
            for k in range(CONV_WIDTH):
                dcw_ref[k:k + 1, :] = jnp.sum(dwacc[k], axis=0, keepdims=True)
        dglu = glu_buf[...]
        dz_ref[:, 0:CONV_DIM] = (dglu * sig_g).astype(BF16)
        dz_ref[:, CONV_DIM:2 * CONV_DIM] = (dglu * av * sig_g * (1.0 - sig_g)).astype(BF16)

        tpos = (row - PAD_ROWS + 1).astype(F32)
        for gi, wdw in enumerate(POOL_WINDOWS):
            lo = POOL_GROUP * gi
            dp = dmix_ref[:, CONV_DIM + lo:CONV_DIM + lo + POOL_GROUP].astype(F32)
            pmv = pm_ref[:, lo:lo + POOL_GROUP]
            pwb = pw_ref[gi].astype(BF16)
            dvec_ref[3:4, lo:lo + POOL_GROUP] += jnp.sum(dp * _dot(pmv, pwb), axis=0, keepdims=True)
            dq = (dp * ps_ref[:, lo:lo + POOL_GROUP]).astype(BF16)
            dpw_ref[gi] += _dot_tn(pmv, dq)
            dpm = _dot_nt(dq, pwb)
            qbuf[0:tm, lo:lo + POOL_GROUP] = dpm / jnp.clip(tpos, 1.0, float(wdw))
            sacc = -dpm
            for j in range(wdw):
                sacc = sacc + qbuf[j:j + tm, lo:lo + POOL_GROUP]
            dz_ref[:, 2 * CONV_DIM + lo:2 * CONV_DIM + lo + POOL_GROUP] = sacc.astype(BF16)

    vec = pl.BlockSpec((1, CONV_DIM), lambda i: (0, 0))
    rev = lambda i: (nt - 1 - i, 0)
    return _call_after(
        dep, body, 9,
        [pl.BlockSpec((tm, CONV_DIM + POOL_DIM), rev), pl.BlockSpec((tm, CP_IN), rev),
         pl.BlockSpec((tm, CONV_DIM), rev), pl.BlockSpec((tm, POOL_DIM), rev),
         pl.BlockSpec((CONV_WIDTH, CONV_DIM), lambda i: (0, 0)), vec, vec,
         pl.BlockSpec((len(POOL_WINDOWS), POOL_GROUP, POOL_GROUP), lambda i: (0, 0, 0)), vec],
        (dmix, z, c, pm, conv_w, ln_g, ln_b, pool_w, pool_scale), grid=(nt,),
        out_specs=[pl.BlockSpec((tm, CP_IN), rev), pl.BlockSpec((CONV_WIDTH + 1, CONV_DIM), lambda i: (0, 0)),
                   pl.BlockSpec((8, CONV_DIM), lambda i: (0, 0)),
                   pl.BlockSpec((len(POOL_WINDOWS), POOL_GROUP, POOL_GROUP), lambda i: (0, 0, 0))],
        out_shape=[jax.ShapeDtypeStruct((t, CP_IN), BF16), jax.ShapeDtypeStruct((CONV_WIDTH + 1, CONV_DIM), F32),
                   jax.ShapeDtypeStruct((8, CONV_DIM), F32),
                   jax.ShapeDtypeStruct((len(POOL_WINDOWS), POOL_GROUP, POOL_GROUP), F32)],
        scratch_shapes=[pltpu.VMEM((tm + CONV_HALO, CONV_DIM), F32), pltpu.VMEM((tm + POOL_HALO, POOL_DIM), F32),
                        pltpu.VMEM((tm, CONV_DIM), F32), pltpu.VMEM((CONV_WIDTH + 1, 8, CONV_DIM), F32)],
        compiler_params=_cparams("arbitrary"), name="cp_seq_bwd")


GLA_UNROLL = 2
Q0, K0, V0, G0, R0 =0, GLA_DK, 2 * GLA_DK, 2 * GLA_DK + GLA_DV, 2 * GLA_DK + 2 * GLA_DV


def _split3(x):
    hi = x.astype(BF16)
    r1 = x - hi.astype(F32)
    mid = r1.astype(BF16)
    lo = (r1 - mid.astype(F32)).astype(BF16)
    return hi, mid, lo


def _tri(strict):
    r = lax.broadcasted_iota(jnp.int32, (CHUNK, CHUNK), 0)
    c = lax.broadcasted_iota(jnp.int32, (CHUNK, CHUNK), 1)
    return ((r > c) if strict else (r >= c)).astype(BF16)


def _chunk_sums(x, cpt, strict, pieces):
    tri3 = jnp.broadcast_to(_tri(strict)[None], (cpt, CHUNK, CHUNK))
    acc = None
    for piece in _split3(x.reshape(cpt, CHUNK, x.shape[-1]))[:pieces]:
        part = jnp.einsum("bij,bjk->bik", tri3, piece, preferred_element_type=F32)
        acc = part if acc is None else acc + part
    return acc


def _chunk_decay(r, gw_ref, gb_ref, cpt):
    pre = _dot(r, gw_ref[...]) + gb_ref[...]
    lac = (jnp.minimum(pre, 0.0) - jnp.log(1.0 + jnp.exp(-jnp.abs(pre)))) * (1.0 / GATE_NORM)
    cum3 = _chunk_sums(lac, cpt, False, 3)
    return cum3, cum3[:, CHUNK - 1:CHUNK, :]


def _gla_seq_fwd(z, gate_w, gate_b, head_g):
    t = z.shape[0]
    tm = _row_tile(t, ROW_TILE_TARGET, CHUNK)
    cpt = tm // CHUNK
    scale = GLA_HK ** -0.5

    def body(z_ref, gw_ref, gb_ref, hg_ref, o_ref, mix_ref, st_ref, state, kdec_s, e_s):
        @pl.when(pl.program_id(0) == 0)
        def _():
            state[...] = jnp.zeros_like(state)

        cum3, tot3 = _chunk_decay(z_ref[:, R0:R0 + GATE_PAD], gw_ref, gb_ref, cpt)
        dec = jnp.exp(jnp.broadcast_to(tot3, cum3.shape) - cum3).reshape(tm, GLA_DK)
        kdec_s[...] = (z_ref[:, K0:K0 + GLA_DK].astype(F32) * dec).astype(BF16)
        e_s[...] = jnp.exp(jnp.broadcast_to(tot3, (cpt, 8, GLA_DK))).reshape(cpt * 8, GLA_DK)

        def chunk(ci, carry):
            rows = pl.ds(pl.multiple_of(ci * CHUNK, CHUNK), CHUNK)
            e_all = e_s[pl.ds(pl.multiple_of(ci * 8, 8), 8), :][0:1, :]
            st_ref[ci] = state[...].astype(BF16)
            for hd in range(GLA_HEADS):
                ks = slice(hd * GLA_HK, (hd + 1) * GLA_HK)
                vs = slice(hd * GLA_HV, (hd + 1) * GLA_HV)
                v = z_ref[rows, V0 + hd * GLA_HV:V0 + (hd + 1) * GLA_HV]
                st = state[vs, :] * e_all[:, ks] + _dot_tn(v, kdec_s[rows, ks])
                state[vs, :] = st
                q = z_ref[rows, Q0 + hd * GLA_HK:Q0 + (hd + 1) * GLA_HK]
                o_ref[rows, vs] = (_dot_nt(q, st.astype(BF16)) * scale).astype(BF16)
            return carry

        lax.fori_loop(0, cpt, chunk, 0, unroll=GLA_UNROLL)

        for hd in range(GLA_HEADS):
            vs = slice(hd * GLA_HV, (hd + 1) * GLA_HV)
            on = _rms(o_ref[:, vs].astype(F32), hg_ref[...])
            gv = z_ref[:, G0 + hd * GLA_HV:G0 + (hd + 1) * GLA_HV].astype(F32)
            mix_ref[:, vs] = (on * _silu(gv)).astype(BF16)

    return pl.pallas_call(
        body, grid=(t // tm,),
        in_specs=[pl.BlockSpec((tm, GLA_IN_PAD), lambda i: (i, 0)),
                  pl.BlockSpec((GATE_PAD, GLA_DK), lambda i: (0, 0)), pl.BlockSpec((1, GLA_DK), lambda i: (0, 0)),
                  pl.BlockSpec((1, GLA_HV), lambda i: (0, 0))],
        out_specs=[pl.BlockSpec((tm, GLA_DV), lambda i: (i, 0)), pl.BlockSpec((tm, GLA_DV), lambda i: (i, 0)),
                   pl.BlockSpec((cpt, GLA_DV, GLA_HK), lambda i: (i, 0, 0))],
        out_shape=[jax.ShapeDtypeStruct((t, GLA_DV), BF16), jax.ShapeDtypeStruct((t, GLA_DV), BF16),
                   jax.ShapeDtypeStruct((t // CHUNK, GLA_DV, GLA_HK), BF16)],
        scratch_shapes=[pltpu.VMEM((GLA_DV, GLA_HK), F32), pltpu.VMEM((tm, GLA_DK), BF16),
                        pltpu.VMEM((cpt * 8, GLA_DK), F32)],
        compiler_params=_cparams("arbitrary"), name="gla_seq_fwd")(z, gate_w, gate_b, head_g)


def _gla_seq_bwd(dmix, o, z, states, gate_w, gate_b, head_g, dep=None):
    t = z.shape[0]
    tm = _row_tile(t, ROW_TILE_TARGET, CHUNK)
    cpt = tm // CHUNK
    nt = t // tm
    scale = GLA_HK ** -0.5

    def body(dmix_ref, o_ref, z_ref, st_ref, gw_ref, gb_ref, hg_ref, dz_ref, dgw_ref, dgb_ref, dhg_ref,
             dstate, dec_s, kdec_s, dkdec_s, do_s, e_s, dtot_s):
        @pl.when(pl.program_id(0) == 0)
        def _():
            dstate[...] = jnp.zeros_like(dstate)
            dgw_ref[...] = jnp.zeros_like(dgw_ref)
            dgb_ref[...] = jnp.zeros_like(dgb_ref)
            dhg_ref[...] = jnp.zeros_like(dhg_ref)

        cum3, tot3 = _chunk_decay(z_ref[:, R0:R0 + GATE_PAD], gw_ref, gb_ref, cpt)
        dec = jnp.exp(jnp.broadcast_to(tot3, cum3.shape) - cum3).reshape(tm, GLA_DK)
        dec_s[...] = dec
        kdec_s[...] = z_ref[:, K0:K0 + GLA_DK].astype(F32) * dec
        e_s[...] = jnp.exp(jnp.broadcast_to(tot3, (cpt, 8, GLA_DK))).reshape(cpt * 8, GLA_DK)
        dhg = jnp.zeros((1, GLA_HV), F32)
        for hd in range(GLA_HEADS):
            vs = slice(hd * GLA_HV, (hd + 1) * GLA_HV)
            gcols = slice(G0 + hd * GLA_HV, G0 + (hd + 1) * GLA_HV)
            ov = o_ref[:, vs].astype(F32)
            gv = z_ref[:, gcols].astype(F32)
            dm = dmix_ref[:, vs].astype(F32)
            sg = jax.nn.sigmoid(gv)
            rr = lax.rsqrt(jnp.mean(ov * ov, axis=-1, keepdims=True) + EPS)
            xhat = ov * rr
            don = dm * (gv * sg)
            dz_ref[:, gcols] = (dm * (xhat * hg_ref[...]) * (sg * (1.0 + gv * (1.0 - sg)))).astype(BF16)
            dhg = dhg + jnp.sum(don * xhat, axis=0, keepdims=True)
            dxh = don * hg_ref[...]
            do_s[:, vs] = (rr * (dxh - xhat * jnp.mean(dxh * xhat, axis=-1, keepdims=True)) * scale).astype(BF16)
        dhg_ref[...] += dhg

        def chunk(cj, carry):
            ci = cpt - 1 - cj
            rows = pl.ds(pl.multiple_of(ci * CHUNK, CHUNK), CHUNK)
            erows = pl.ds(pl.multiple_of(ci * 8, 8), 8)
            e_all = e_s[erows, :][0:1, :]
            for hd in range(GLA_HEADS):
                ks = slice(hd * GLA_HK, (hd + 1) * GLA_HK)
                vs = slice(hd * GLA_HV, (hd + 1) * GLA_HV)
                e = e_all[:, ks]
                kdb = kdec_s[rows, ks].astype(BF16)
                v = z_ref[rows, V0 + hd * GLA_HV:V0 + (hd + 1) * GLA_HV]
                q = z_ref[rows, Q0 + hd * GLA_HK:Q0 + (hd + 1) * GLA_HK]
                do = do_s[rows, vs]
                st_prev = st_ref[ci, vs, :].astype(F32)
                st = st_prev * e + _dot_tn(v, kdb)
                dz_ref[rows, Q0 + hd * GLA_HK:Q0 + (hd + 1) * GLA_HK] = _dot(do, st.astype(BF16)).astype(BF16)
                dst = dstate[vs, :] + _dot_tn(do, q)
                dstb = dst.astype(BF16)
                dkdec_s[rows, ks] = _dot(v, dstb)
                dz_ref[rows, V0 + hd * GLA_HV:V0 + (hd + 1) * GLA_HV] = _dot_nt(kdb, dstb).astype(BF16)
                dtot = jnp.sum(dst * st_prev, axis=0, keepdims=True) * e
                dtot_s[erows, ks] = jnp.broadcast_to(dtot, (8, GLA_HK))
                dstate[vs, :] = dst * e
            return carry

        lax.fori_loop(0, cpt, chunk, 0, unroll=GLA_UNROLL)

        dkdec = dkdec_s[...]
        dz_ref[:, K0:K0 + GLA_DK] = (dkdec * dec_s[...]).astype(BF16)
        before = _chunk_sums(dkdec * kdec_s[...], cpt, True, 2)
        dtot3 = dtot_s[...].reshape(cpt, 8, GLA_DK)[:, 0:1, :]
        dlac = (jnp.broadcast_to(dtot3, before.shape) + before).reshape(tm, GLA_DK)
        pre = _dot(z_ref[:, R0:R0 + GATE_PAD], gw_ref[...]) + gb_ref[...]
        dpre = dlac * (1.0 / GATE_NORM) * (1.0 - jax.nn.sigmoid(pre))
        dpb = dpre.astype(BF16)
        dz_ref[:, R0:R0 + GATE_PAD] = _dot_nt(dpb, gw_ref[...]).astype(BF16)
        dgw_ref[...] += _dot_tn(z_ref[:, R0:R0 + GATE_PAD], dpb)
        dgb_ref[...] += jnp.sum(dpre, axis=0, keepdims=True)

    rev = lambda i: (nt - 1 - i, 0)
    return _call_after(
        dep, body, 7,
        [pl.BlockSpec((tm, GLA_DV), rev), pl.BlockSpec((tm, GLA_DV), rev), pl.BlockSpec((tm, GLA_IN_PAD), rev),
         pl.BlockSpec((cpt, GLA_DV, GLA_HK), lambda i: (nt - 1 - i, 0, 0)),
         pl.BlockSpec((GATE_PAD, GLA_DK), lambda i: (0, 0)), pl.BlockSpec((1, GLA_DK), lambda i: (0, 0)),
         pl.BlockSpec((1, GLA_HV), lambda i: (0, 0))],
        (dmix, o, z, states, gate_w, gate_b, head_g), grid=(nt,),
        out_specs=[pl.BlockSpec((tm, GLA_IN_PAD), rev), pl.BlockSpec((GATE_PAD, GLA_DK), lambda i: (0, 0)),
                   pl.BlockSpec((1, GLA_DK), lambda i: (0, 0)), pl.BlockSpec((1, GLA_HV), lambda i: (0, 0))],
        out_shape=[jax.ShapeDtypeStruct((t, GLA_IN_PAD), BF16), jax.ShapeDtypeStruct((GATE_PAD, GLA_DK), F32),
                   jax.ShapeDtypeStruct((1, GLA_DK), F32), jax.ShapeDtypeStruct((1, GLA_HV), F32)],
        scratch_shapes=[pltpu.VMEM((GLA_DV, GLA_HK), F32), pltpu.VMEM((tm, GLA_DK), F32), pltpu.VMEM((tm, GLA_DK), F32),
                        pltpu.VMEM((tm, GLA_DK), F32), pltpu.VMEM((tm, GLA_DV), BF16),
                        pltpu.VMEM((cpt * 8, GLA_DK), F32), pltpu.VMEM((cpt * 8, GLA_DK), F32)],
        compiler_params=_cparams("arbitrary"), name="gla_seq_bwd")


def _sum_halves(g, recv, c_idx, name):
    n, r, cdim = g.shape
    h = r // 2
    tr = _row_tile(h, 256, 8)
    nh = h // tr

    def body(c_ref, g_ref, r_ref, o_ref):
        o_ref[...] = (g_ref[...] + r_ref[...]).astype(BF16)

    return pl.pallas_call(
        body,
        grid_spec=pltpu.PrefetchScalarGridSpec(
            num_scalar_prefetch=1, grid=(n, nh),
            in_specs=[pl.BlockSpec((None, tr, cdim), lambda s, i, c: (s, c[0] * nh + i, 0)),
                      pl.BlockSpec((None, tr, cdim), lambda s, i, c: (s, i, 0))],
            out_specs=pl.BlockSpec((None, tr, cdim), lambda s, i, c: (s, i, 0))),
        out_shape=jax.ShapeDtypeStruct((n, h, cdim), BF16),
        compiler_params=_cparams("parallel", "parallel"), name=name)(c_idx, g, recv)


def _sum_slots(x, name):
    n, r, cdim = x.shape
    tr = _row_tile(r, 256, 8)

    def body(x_ref, o_ref):
        acc = x_ref[0].astype(F32)
        for j in range(1, n):
            acc = acc + x_ref[j].astype(F32)
        o_ref[...] = acc

    return pl.pallas_call(
        body, grid=(r // tr,),
        in_specs=[pl.BlockSpec((n, tr, cdim), lambda i: (0, i, 0))],
        out_specs=pl.BlockSpec((tr, cdim), lambda i: (i, 0)),
        out_shape=jax.ShapeDtypeStruct((r, cdim), F32),
        compiler_params=_cparams("parallel"), name=name)(x)


def _sum_own_and_slots(own, slots, chip_idx, name):
    n, r, cdim = own.shape
    tr = _row_tile(r, 256, 8)

    def body(s_ref, own_ref, a_ref, b_ref, c_ref, o_ref):
        o_ref[...] = (own_ref[...].astype(F32) + a_ref[...].astype(F32) + b_ref[...].astype(F32)
                      + c_ref[...].astype(F32))

    def slot(dd):
        return pl.BlockSpec((None, tr, cdim), lambda i, s: ((s[0] + dd) % n, i, 0))

    return pl.pallas_call(
        body,
        grid_spec=pltpu.PrefetchScalarGridSpec(
            num_scalar_prefetch=1, grid=(r // tr,), in_specs=[slot(0), slot(1), slot(2), slot(3)],
            out_specs=pl.BlockSpec((tr, cdim), lambda i, s: (i, 0))),
        out_shape=jax.ShapeDtypeStruct((r, cdim), F32),
        compiler_params=_cparams("parallel"), name=name)(chip_idx, own, slots, slots, slots)


def _add2(a, b, name):
    r, cdim = a.shape
    tr = _row_tile(r, 256, 8)

    def body(a_ref, b_ref, o_ref):
        o_ref[...] = a_ref[...] + b_ref[...]

    spec = pl.BlockSpec((tr, cdim), lambda i: (i, 0))
    return pl.pallas_call(body, grid=(r // tr,), in_specs=[spec, spec], out_specs=spec,
                          out_shape=jax.ShapeDtypeStruct((r, cdim), F32),
                          compiler_params=_cparams("parallel"), name=name)(a, b)


def _adamw(w, g, m, v, name):
    r, cdim = w.shape
    tr = _row_tile(r, 256, 8)

    def body(w_ref, g_ref, m_ref, v_ref, go_ref, d_ref, mo_ref, vo_ref):
        gv = g_ref[...]
        go_ref[...] = gv
        mn = ADAM_B1 * m_ref[...] + (1.0 - ADAM_B1) * gv
        vn = ADAM_B2 * v_ref[...] + (1.0 - ADAM_B2) * (gv * gv)
        m_hat = mn / (1.0 - ADAM_B1 ** ADAM_STEP)
        v_hat = vn / (1.0 - ADAM_B2 ** ADAM_STEP)
        d_ref[...] = -ADAM_LR * (m_hat / (jnp.sqrt(v_hat) + ADAM_EPS) + ADAM_WD * w_ref[...])
        mo_ref[...] = mn
        vo_ref[...] = vn

    spec = pl.BlockSpec((tr, cdim), lambda i: (i, 0))
    shp = jax.ShapeDtypeStruct((r, cdim), F32)
    return pl.pallas_call(body, grid=(r // tr,), in_specs=[spec] * 4, out_specs=[spec] * 4,
                          out_shape=[shp] * 4, compiler_params=_cparams("parallel"), name=name)(w, g, m, v)


def _adamw_many(ws, gs, ms, vs):
    n = len(ws)

    def body(*refs):
        for i in range(n):
            w_ref, g_ref, m_ref, v_ref = refs[i], refs[n + i], refs[2 * n + i], refs[3 * n + i]
            d_ref, mo_ref, vo_ref = refs[4 * n + i], refs[5 * n + i], refs[6 * n + i]
            gv = g_ref[...]
            mn = ADAM_B1 * m_ref[...] + (1.0 - ADAM_B1) * gv
            vn = ADAM_B2 * v_ref[...] + (1.0 - ADAM_B2) * (gv * gv)
            m_hat = mn / (1.0 - ADAM_B1 ** ADAM_STEP)
            v_hat = vn / (1.0 - ADAM_B2 ** ADAM_STEP)
            d_ref[...] = -ADAM_LR * (m_hat / (jnp.sqrt(v_hat) + ADAM_EPS) + ADAM_WD * w_ref[...])
            mo_ref[...] = mn
            vo_ref[...] = vn

    shapes = [jax.ShapeDtypeStruct(w.shape, F32) for w in ws]
    outs = pl.pallas_call(body, out_shape=shapes * 3, name="adamw_small")(*ws, *gs, *ms, *vs)
    return outs[:n], outs[n:2 * n], outs[2 * n:]


def _split_rows(a):
    return a.reshape(a.shape[0], 2, a.shape[1] // 2, a.shape[2])


def _place():
    x, y, c = lax.axis_index("x"), lax.axis_index("y"), lax.axis_index("c")
    chips = [(1 - x, y), (x, 1 - y), (1 - x, 1 - y)]
    return x, y, c, chips


def _remote(src, dst, send_sem, recv_sem, to):
    return pltpu.make_async_remote_copy(src_ref=src, dst_ref=dst, send_sem=send_sem, recv_sem=recv_sem,
                                        device_id=to, device_id_type=MESH)


def _plan_gather(src_refs, land_refs):
    x, y, c, chips = _place()
    me = 2 * x + y
    return [(src.at[c], land.at[me, c], (px, py, c), land.at[2 * px + py, c])
            for src, land in zip(src_refs, land_refs) for (px, py) in chips]


def _plan_scatter(n_parts):
    def plan(src_refs, land_refs):
        x, y, c, chips = _place()
        me = 2 * x + y
        copies = []
        for k, (src, land) in enumerate(zip(src_refs, land_refs)):
            for (px, py) in chips:
                to = 2 * px + py
                copies.append((src.at[to] if k < n_parts else src, land.at[me], (px, py, c), land.at[to]))
        return copies
    return plan


def _plan_exchange(n_split):
    def plan(src_refs, land_refs):
        x, y, c, _ = _place()
        sib = (x, y, 1 - c)
        return [(src.at[:, 1 - c] if k < n_split else src, land, sib, land)
                for k, (src, land) in enumerate(zip(src_refs, land_refs))]
    return plan


def _hbm(a):
    return pltpu.HBM(a.shape, a.dtype)


def _start_copies(name, srcs, lands, plan, ncopy, dep=None):
    ns, nl = len(srcs), len(lands)
    nin = ns + nl + (0 if dep is None else 1)

    def body(*refs):
        send_sems, recv_sems, token = refs[nin], refs[nin + 1], refs[-1]
        for k, (src, dst, dev, _) in enumerate(plan(refs[:ns], refs[ns:ns + nl])):
            _remote(src, dst, send_sems.at[k], recv_sems.at[k], dev).start()
        token[...] = jnp.zeros_like(token)

    args = [pltpu.with_memory_space_constraint(a, pltpu.HBM) for a in list(srcs) + list(lands)]
    outs = pl.pallas_call(
        body, name=name,
        out_shape=(pltpu.SemaphoreType.DMA((ncopy,)), pltpu.SemaphoreType.DMA((ncopy,)),
                   *[_hbm(a) for a in list(srcs) + list(lands)], jax.ShapeDtypeStruct((8, 128), F32)),
        in_specs=[HBM_SPEC] * (ns + nl) + ([] if dep is None else [ANY_SPEC]),
        out_specs=(SEM_SPEC, SEM_SPEC, *([HBM_SPEC] * (ns + nl)), pl.BlockSpec(memory_space=pltpu.VMEM)),
        input_output_aliases={i: 2 + i for i in range(ns + nl)},
        compiler_params=pltpu.CompilerParams(has_side_effects=SIDE_EFFECT),
    )(*args, *([] if dep is None else [dep]))
    return outs[0], outs[1], list(outs[2:2 + ns]), list(outs[2 + ns:2 + ns + nl]), outs[-1]


def _wait_copies(name, started, plan, after):
    send_sems, recv_sems, srcs, lands, _ = started
    ns, nl = len(srcs), len(lands)

    def body(*refs):
        send_ref, recv_ref = refs[ns + nl], refs[ns + nl + 1]
        for k, (src, _, dev, mine) in enumerate(plan(refs[:ns], refs[ns:ns + nl])):
            copy = _remote(src, mine, send_ref.at[k], recv_ref.at[k], dev)
            copy.wait_send()
            copy.wait_recv()

    outs = pl.pallas_call(
        body, name=name, out_shape=tuple(_hbm(a) for a in srcs + lands),
        in_specs=[HBM_SPEC] * (ns + nl) + [SEM_SPEC, SEM_SPEC, ANY_SPEC], out_specs=tuple([HBM_SPEC] * (ns + nl)),
        input_output_aliases={i: i for i in range(ns + nl)},
        compiler_params=pltpu.CompilerParams(has_side_effects=SIDE_EFFECT),
    )(*srcs, *lands, send_sems, recv_sems, after)
    return list(outs[:ns]), list(outs[ns:])


def _share_with_sibling(name, srcs, lands):
    n = len(srcs)

    def body(*refs):
        src_refs, land_refs, out_refs = refs[:n], refs[n:2 * n], refs[2 * n:3 * n]
        send_sem, recv_sem = refs[3 * n:]
        x, y, c, chips = _place()
        me = 2 * x + y
        sib = (x, y, 1 - c)
        sends, recvs = [], []
        for k in range(n):
            sems = (send_sem.at[4 * k], recv_sem.at[4 * k])
            sends.append(_remote(src_refs[k], out_refs[k].at[me], *sems, sib))
            recvs.append(_remote(src_refs[k], out_refs[k].at[me], *sems, sib))
            for j, (px, py) in enumerate(chips):
                frm = 2 * px + py
                sems = (send_sem.at[4 * k + 1 + j], recv_sem.at[4 * k + 1 + j])
                sends.append(_remote(land_refs[k].at[frm, c], out_refs[k].at[frm, c], *sems, sib))
                recvs.append(_remote(land_refs[k].at[frm, c], out_refs[k].at[frm, 1 - c], *sems, sib))
        for cp in sends:
            cp.start()
        for cp in recvs:
            cp.wait_recv()
        for cp in sends:
            cp.wait_send()

    return pl.pallas_call(
        body, name=name, in_specs=[HBM_SPEC] * (2 * n), out_specs=[HBM_SPEC] * n,
        out_shape=[jax.ShapeDtypeStruct(a.shape, a.dtype) for a in lands],
        input_output_aliases={n + k: k for k in range(n)},
        scratch_shapes=[pltpu.SemaphoreType.DMA((4 * n,)), pltpu.SemaphoreType.DMA((4 * n,))],
    )(*srcs, *lands)


def _gather_weights(bigs, small):
    nb = len(bigs)
    shapes = [b.shape for b in bigs]
    bigs = [_split_rows(b) for b in bigs]

    def body(*refs):
        big_in, small_in = refs[:nb], refs[nb]
        big_out, small_out = refs[nb + 1:2 * nb + 1], refs[2 * nb + 1]
        ici_send, ici_recv, d2d_send, d2d_recv, own_send, own_recv, loc_sem = refs[2 * nb + 2:]
        x, y, c, chips = _place()
        me = 2 * x + y
        sib = (x, y, 1 - c)

        def half(ref, k, which):
            return ref.at[:, which]

        local = [pltpu.make_async_copy(small_in, small_out.at[me], loc_sem.at[0])]
        for cp in local:
            cp.start()
        sends = [_remote(big_in[k], big_out[k].at[me], own_send.at[k], own_recv.at[k], sib) for k in range(nb)]
        for j, (px, py) in enumerate(chips):
            for k in range(nb):
                sends.append(_remote(half(big_in[k], k, c), half(big_out[k].at[me], k, c),
                                     ici_send.at[k * 3 + j], ici_recv.at[k * 3 + j], (px, py, c)))
            sends.append(_remote(small_in, small_out.at[me], ici_send.at[nb * 3 + j], ici_recv.at[nb * 3 + j], (px, py, c)))
        for cp in sends:
            cp.start()
        passed = []
        for j, (px, py) in enumerate(chips):
            frm = 2 * px + py
            for k in range(nb):
                landed = half(big_out[k].at[frm], k, c)
                _remote(landed, landed, ici_send.at[k * 3 + j], ici_recv.at[k * 3 + j], (px, py, c)).wait_recv()
                fwd = _remote(landed, landed, d2d_send.at[k * 3 + j], d2d_recv.at[k * 3 + j], sib)
                fwd.start()
                passed.append(fwd)
            _remote(small_in, small_out.at[frm], ici_send.at[nb * 3 + j], ici_recv.at[nb * 3 + j], (px, py, c)).wait_recv()
        for j, (px, py) in enumerate(chips):
            frm = 2 * px + py
            for k in range(nb):
                theirs = half(big_out[k].at[frm], k, 1 - c)
                _remote(theirs, theirs, d2d_send.at[k * 3 + j], d2d_recv.at[k * 3 + j], sib).wait_recv()
        for k in range(nb):
            _remote(big_in[k], big_out[k].at[me], own_send.at[k], own_recv.at[k], sib).wait_recv()
        for cp in sends + passed:
            cp.wait_send()
        for cp in local:
            cp.wait()

    out_shape = [jax.ShapeDtypeStruct((N_CHIPS,) + b.shape, b.dtype) for b in bigs]
    out_shape.append(jax.ShapeDtypeStruct((N_CHIPS,) + small.shape, small.dtype))
    outs = pl.pallas_call(
        body, in_specs=[HBM_SPEC] * (nb + 1), out_specs=[HBM_SPEC] * (nb + 1), out_shape=out_shape,
        scratch_shapes=[pltpu.SemaphoreType.DMA((3 * nb + 3,)), pltpu.SemaphoreType.DMA((3 * nb + 3,)),
                        pltpu.SemaphoreType.DMA((3 * nb,)), pltpu.SemaphoreType.DMA((3 * nb,)),
                        pltpu.SemaphoreType.DMA((nb,)), pltpu.SemaphoreType.DMA((nb,)),
                        pltpu.SemaphoreType.DMA((1,))],
        name="gather_weights")(*bigs, small)
    return [o.reshape((N_CHIPS,) + s) for o, s in zip(outs[:-1], shapes)] + [outs[-1]]


def _exchange_halves(grads, small, name):
    ng = len(grads)
    grads = [_split_rows(g) for g in grads]
    extra = [] if small is None else [small]
    nall = ng + len(extra)

    def body(*refs):
        ins, outs = refs[:nall], refs[nall:2 * nall]
        send_sem, recv_sem = refs[2 * nall:]
        x, y, c, _ = _place()
        sib = (x, y, 1 - c)
        copies = []
        for k in range(nall):
            src = ins[k].at[:, 1 - c] if k < ng else ins[k]
            copies.append(_remote(src, outs[k], send_sem.at[k], recv_sem.at[k], sib))
        for cp in copies:
            cp.start()
        for cp in copies:
            cp.wait_recv()
        for cp in copies:
            cp.wait_send()

    out_shape = [jax.ShapeDtypeStruct((g.shape[0], g.shape[2], g.shape[3]), g.dtype) for g in grads]
    out_shape += [jax.ShapeDtypeStruct(s.shape, s.dtype) for s in extra]
    return pl.pallas_call(
        body, in_specs=[HBM_SPEC] * nall, out_specs=[HBM_SPEC] * nall, out_shape=out_shape,
        scratch_shapes=[pltpu.SemaphoreType.DMA((nall,)), pltpu.SemaphoreType.DMA((nall,))],
        name=name)(*grads, *extra)


def _join_halves(reduced, dests, out_shapes, name):
    nr = len(reduced)

    def body(*refs):
        r_in = refs[:nr]
        outs = refs[nr:nr + len(out_shapes)]
        send_sem, recv_sem, back_send, back_recv = refs[nr + len(out_shapes):]
        x, y, c, _ = _place()
        sib = (x, y, 1 - c)
        sends, backs = [], []
        for k in range(nr):
            oi, layer = dests[k]
            sends.append(_remote(r_in[k], outs[oi].at[layer, c], send_sem.at[k], recv_sem.at[k], sib))
        for cp in sends:
            cp.start()
        for k in range(nr):
            oi, layer = dests[k]
            theirs = outs[oi].at[layer, 1 - c]
            _remote(r_in[k], theirs, send_sem.at[k], recv_sem.at[k], sib).wait_recv()
            back = _remote(theirs, theirs, back_send.at[k], back_recv.at[k], sib)
            back.start()
            backs.append(back)
        for k in range(nr):
            oi, layer = dests[k]
            mine = outs[oi].at[layer, c]
            _remote(mine, mine, back_send.at[k], back_recv.at[k], sib).wait_recv()
        for cp in sends + backs:
            cp.wait_send()

    split = [(s[0], 2, s[1] // 2, s[2]) for s in out_shapes]
    outs = pl.pallas_call(
        body, in_specs=[HBM_SPEC] * nr, out_specs=[HBM_SPEC] * len(out_shapes),
        out_shape=[jax.ShapeDtypeStruct(s, F32) for s in split],
        scratch_shapes=[pltpu.SemaphoreType.DMA((nr,)), pltpu.SemaphoreType.DMA((nr,)),
                        pltpu.SemaphoreType.DMA((nr,)), pltpu.SemaphoreType.DMA((nr,))],
        name=name)(*reduced)
    return [o.reshape(s) for o, s in zip(outs, out_shapes)]


def _pack(arrs):
    flat = jnp.concatenate([a.reshape(-1).astype(F32) for a in arrs])
    n = flat.shape[0]
    rows = -(-n // PACK_WIDTH)
    rows = -(-rows // 8) * 8
    return jnp.pad(flat, (0, rows * PACK_WIDTH - n)).reshape(rows, PACK_WIDTH)


def _unpack(buf, shapes):
    flat = buf.reshape(-1)
    out, off = [], 0
    for shp in shapes:
        n = 1
        for s in shp:
            n *= s
        out.append(flat[off:off + n].reshape(shp))
        off += n
    return out


def _unshard_cols(stacked):
    moved = jnp.moveaxis(stacked, 0, -2)
    return moved.reshape(moved.shape[:-2] + (moved.shape[-2] * moved.shape[-1],))


def _take_cols(blocks, start, width):
    bw = blocks.shape[2]
    pieces, lo = [], start
    while lo < start + width:
        b = lo // bw
        hi = min(start + width, (b + 1) * bw)
        pieces.append(blocks[b][:, lo - b * bw:hi - b * bw])
        lo = hi
    return jnp.concatenate(pieces, axis=1)


def _col_shard(full, s, width):
    return lax.dynamic_slice_in_dim(full, s * width, width, axis=full.ndim - 1)


def kernel(x, meta_tokens, mix_norm_g, ffn_norm_g, ffn_w1, ffn_w2, cp_w_in, cp_conv_w, cp_conv_b, cp_ln_g, cp_ln_b, cp_pool_w, cp_pool_scale, cp_w_out, gla_w_in, gla_gate_w2, gla_gate_b, gla_head_g, gla_w_out, final_norm_g, loss_target, m_meta_tokens, m_mix_norm_g, m_ffn_norm_g, m_ffn_w1, m_ffn_w2, m_cp_w_in, m_cp_conv_w, m_cp_conv_b, m_cp_ln_g, m_cp_ln_b, m_cp_pool_w, m_cp_pool_scale, m_cp_w_out, m_gla_w_in, m_gla_gate_w2, m_gla_gate_b, m_gla_head_g, m_gla_w_out, m_final_norm_g, v_meta_tokens, v_mix_norm_g, v_ffn_norm_g, v_ffn_w1, v_ffn_w2, v_cp_w_in, v_cp_conv_w, v_cp_conv_b, v_cp_ln_g, v_cp_ln_b, v_cp_pool_w, v_cp_pool_scale, v_cp_w_out, v_gla_w_in, v_gla_gate_w2, v_gla_gate_b, v_gla_head_g, v_gla_w_out, v_final_norm_g):
    d = D_MODEL
    chip = 2 * lax.axis_index("x") + lax.axis_index("y")
    core = lax.axis_index("c")
    seq = x.shape[1]
    t = seq + CHUNK

    big_w = [ffn_w1, ffn_w2, cp_w_in, cp_w_out, gla_w_in, gla_w_out]
    sharded_small = [meta_tokens, cp_conv_w, gla_gate_w2, gla_gate_b, gla_head_g]
    cpin_g, cpout_g, small_g = _gather_weights([cp_w_in.astype(BF16), cp_w_out.astype(BF16)], _pack(sharded_small))

    def halves(w):
        return w.astype(BF16).reshape(2, w.shape[0] // 2, w.shape[1])

    def start_gather(name, srcs, dep):
        lands = [lax.empty((N_CHIPS,) + s.shape, s.dtype) for s in srcs]
        return _start_copies(name, srcs, lands, _plan_gather, 3 * len(srcs), dep)

    def finish_gather(name, started, after):
        srcs, lands = _wait_copies(name + "_wait", started, _plan_gather, after)
        return [g.reshape(N_CHIPS, 2 * g.shape[2], g.shape[3]) for g in _share_with_sibling(name + "_share", srcs, lands)]

    ffn0_started = start_gather("gather_ffn0_start", [halves(ffn_w1[0]), halves(ffn_w2[0])], small_g)
    gla_started = start_gather("gather_gla_start", [halves(gla_w_in[0]), halves(gla_w_out[0])], ffn0_started[-1])
    ffn1_started = start_gather("gather_ffn1_start", [halves(ffn_w1[1]), halves(ffn_w2[1])], gla_started[-1])
    per_chip = [_unpack(small_g[j], [a.shape for a in sharded_small]) for j in range(N_CHIPS)]
    meta_f, conv_w_f, gate_w_f, gate_b_f, head_g_f = [
        jnp.concatenate([per_chip[j][i] for j in range(N_CHIPS)], axis=-1) for i in range(len(sharded_small))]
    conv_w_f, gate_w_f = conv_w_f[0], gate_w_f[0]
    w_cp_in = _unshard_cols(cpin_g[:, 0])
    w_cp_out = cpout_g.reshape(CONV_DIM + POOL_DIM, d)
    gate_w_pad = jnp.pad(gate_w_f, ((0, GATE_PAD - GATE_RANK), (0, 0))).astype(BF16)
    row = lambda a: a.reshape(1, -1)
    c_idx = core.reshape(1).astype(jnp.int32)
    chip_idx = chip.reshape(1).astype(jnp.int32)

    h0 = jnp.concatenate([jnp.zeros((PAD_ROWS, d), F32), meta_f, x[0]], axis=0)
    z0, u0 = _norm_matmul(h0, row(mix_norm_g[0]), w_cp_in, 512, "cp_in_proj", dep=ffn1_started[-1])
    c0, pm0, mix0 = _cp_seq_fwd(z0, conv_w_f, cp_conv_b, cp_ln_g, cp_ln_b, cp_pool_w[0], cp_pool_scale)
    h1 = _matmul_residual(mix0, w_cp_out, h0, "cp_out_proj")
    w1g0, w2g0 = finish_gather("gather_ffn0", ffn0_started, h1)
    h2, hp0, uf0 = _ffn_fwd(h1, row(ffn_norm_g[0]), w1g0, w2g0, "ffn0_fwd")
    glain_g, glaout_g = finish_gather("gather_gla", gla_started, h2)
    w_gla_in = jnp.concatenate([glain_g[j] for j in range(N_CHIPS)] + [jnp.zeros((d, GLA_IN_PAD - GLA_IN), BF16)], axis=1)
    w_gla_out = glaout_g.reshape(GLA_DV, d)
    z1, u2 = _norm_matmul(h2, row(mix_norm_g[1]), w_gla_in, 640, "gla_in_proj")
    o1, mix1, states = _gla_seq_fwd(z1, gate_w_pad, gate_b_f, head_g_f)
    h3 = _matmul_residual(mix1, w_gla_out, h2, "gla_out_proj")
    w1g1, w2g1 = finish_gather("gather_ffn1", ffn1_started, h3)
    h4, hp1, uf1 = _ffn_fwd(h3, row(ffn_norm_g[1]), w1g1, w2g1, "ffn1_fwd")

    def start_exchange(name, grads):
        srcs = [_split_rows(g) for g in grads]
        lands = [lax.empty((g.shape[0], g.shape[1] // 2, g.shape[2]), g.dtype) for g in grads]
        return _start_copies(name + "_exchange_start", srcs, lands, _plan_exchange(len(grads)), len(grads))

    def start_scatter(name, exchange, after):
        srcs, recv = _wait_copies(name + "_exchange_wait", exchange, _plan_exchange(len(exchange[2])), after)
        parts = [_sum_halves(g.reshape(g.shape[0], -1, g.shape[3]), r, c_idx, "%s_chip_sum_%d" % (name, k))
                 for k, (g, r) in enumerate(zip(srcs, recv))]
        lands = [lax.empty(p.shape, p.dtype) for p in parts]
        return _start_copies(name + "_scatter_start", parts, lands, _plan_scatter(len(parts)), 3 * len(parts))

    def finish_reduce(name, started, after):
        n = len(started[2])
        parts, lands = _wait_copies(name + "_scatter_wait", started, _plan_scatter(n), after)
        return [_sum_own_and_slots(p, s, chip_idx, "%s_slot_sum_%d" % (name, k)) for k, (p, s) in enumerate(zip(parts, lands))]

    dh4, d_final_g, loss_part = _loss_bwd(h4, row(final_norm_g), loss_target[0])

    dh3, dhp1, d_ffn_g1 = _ffn_bwd_data(dh4, h3, row(ffn_norm_g[1]), hp1, w1g1, w2g1, "ffn1_bwd")
    dw1_1 = _wgrad(uf1, dhp1, N_CHIPS, d, d, False, True, False, "ffn1_dw1")
    dw2_1 = _wgrad(hp1, dh4, N_CHIPS, d, d, True, False, True, "ffn1_dw2")
    ffn1_exchange = start_exchange("ffn1", [dw1_1, dw2_1])

    dmix1 = _dgrad(dh3, w_gla_out, "gla_out_dgrad", dep=ffn1_exchange[-1])
    dw_gla_out = _wgrad(mix1, dh3, 1, GLA_DV, d, False, False, False, "gla_out_dw")
    ffn1_reduce = start_scatter("ffn1", ffn1_exchange, dw_gla_out)
    dz1, d_gate_w, d_gate_b, d_head_g = _gla_seq_bwd(dmix1, o1, z1, states, gate_w_pad, gate_b_f, head_g_f,
                                                     dep=ffn1_reduce[-1])
    dh2, d_mix_g1 = _dgrad_norm_bwd(dz1, w_gla_in, h2, row(mix_norm_g[1]), dh3, 640, "gla_in_dgrad")
    dw_gla_in = _wgrad(u2, dz1, GLA_IN_PAD // 640, d, 640, False, True, False, "gla_in_dw")
    gla_in_shards = jnp.stack([_take_cols(dw_gla_in, j * (GLA_IN // N_CHIPS), GLA_IN // N_CHIPS) for j in range(N_CHIPS)])
    gla_exchange = start_exchange("gla", [gla_in_shards, dw_gla_out.reshape(N_CHIPS, -1, d)])

    dh1, dhp0, d_ffn_g0 = _ffn_bwd_data(dh2, h1, row(ffn_norm_g[0]), hp0, w1g0, w2g0, "ffn0_bwd", dep=gla_exchange[-1])
    gla_reduce = start_scatter("gla", gla_exchange, dh1)
    dw1_0 = _wgrad(uf0, dhp0, N_CHIPS, d, d, False, True, False, "ffn0_dw1", dep=gla_reduce[-1])
    dw2_0 = _wgrad(hp0, dh2, N_CHIPS, d, d, True, False, True, "ffn0_dw2")
    ffn0_exchange = start_exchange("ffn0", [dw1_0, dw2_0])

    dmix0 = _dgrad(dh1, w_cp_out, "cp_out_dgrad", dep=ffn0_exchange[-1])
    dw_cp_out = _wgrad(mix0, dh1, 1, CONV_DIM + POOL_DIM, d, False, False, False, "cp_out_dw")
    ffn0_reduce = start_scatter("ffn0", ffn0_exchange, dw_cp_out)
    dz0, d_conv_w, d_cp_vec, d_pool_w = _cp_seq_bwd(dmix0, z0, c0, pm0, conv_w_f, cp_ln_g, cp_ln_b, cp_pool_w[0],
                                                    cp_pool_scale, dep=ffn0_reduce[-1])
    dh0, d_mix_g0 = _dgrad_norm_bwd(dz0, w_cp_in, h0, row(mix_norm_g[0]), dh1, 512, "cp_in_dgrad")
    dw_cp_in = _wgrad(u0, dz0, N_CHIPS, d, CP_IN // N_CHIPS, False, True, False, "cp_in_dw")

    grad_x = dh0[CHUNK:][None]

    cp_grads = [dw_cp_in, dw_cp_out.reshape(N_CHIPS, -1, d)]
    small_full = [dh0[PAD_ROWS:CHUNK], jnp.concatenate([d_mix_g0, d_mix_g1], axis=0),
                  jnp.concatenate([d_ffn_g0, d_ffn_g1], axis=0), d_conv_w[:CONV_WIDTH][None],
                  d_cp_vec[0:1], d_cp_vec[1:2], d_cp_vec[2:3], d_pool_w[None], d_cp_vec[3:4],
                  d_gate_w[:GATE_RANK][None], d_gate_b, d_head_g, d_final_g[0], loss_part[0, 0:1]]
    small_mine = _pack(small_full)
    recv = _exchange_halves(cp_grads, small_mine, "cp_exchange")
    chip_sums = [_sum_halves(g, r, c_idx, "cp_chip_sum_%d" % k) for k, (g, r) in enumerate(zip(cp_grads, recv[:-1]))]
    small_chip = _add2(small_mine, recv[-1], "chip_sum_small")
    small_slots = lax.dynamic_update_slice(jnp.zeros((N_CHIPS,) + small_chip.shape, F32), small_chip[None], (chip, 0, 0))
    cp_lands = [lax.empty(p.shape, p.dtype) for p in chip_sums] + [small_slots]
    cp_reduce = _start_copies("cp_scatter_start", chip_sums + [small_chip], cp_lands, _plan_scatter(len(chip_sums)),
                              3 * (len(chip_sums) + 1))

    def adamw_big(names, grads):
        outs = {}
        for n, g in zip(names, grads):
            w, m, v = big[n]
            two_d = lambda a: a.reshape(-1, a.shape[-1])
            res = _adamw(two_d(w), two_d(g), two_d(m), two_d(v), "adamw_" + n)
            outs[n] = [o.reshape(w.shape) for o in res]
        return outs

    big = {"w1": (ffn_w1, m_ffn_w1, v_ffn_w1), "w2": (ffn_w2, m_ffn_w2, v_ffn_w2),
           "cp_in": (cp_w_in, m_cp_w_in, v_cp_w_in), "cp_out": (cp_w_out, m_cp_w_out, v_cp_w_out),
           "gla_in": (gla_w_in, m_gla_w_in, v_gla_w_in), "gla_out": (gla_w_out, m_gla_w_out, v_gla_w_out)}
    red_ffn1 = finish_reduce("ffn1", ffn1_reduce, cp_reduce[-1])
    red_gla = finish_reduce("gla", gla_reduce, cp_reduce[-1])
    red_ffn0 = finish_reduce("ffn0", ffn0_reduce, cp_reduce[-1])
    first = ["w1", "w2", "gla_in", "gla_out"]
    first_grads = _join_halves([red_ffn0[0], red_ffn1[0], red_ffn0[1], red_ffn1[1], red_gla[0], red_gla[1]],
                               [(0, 0), (0, 1), (1, 0), (1, 1), (2, 0), (3, 0)], [big[n][0].shape for n in first],
                               "join_halves_ffn_gla")
    big_out = adamw_big(first, first_grads)
    cp_parts, cp_slots = _wait_copies("cp_scatter_wait", cp_reduce, _plan_scatter(len(chip_sums)), big_out["gla_out"][1])
    red_cp = [_sum_own_and_slots(a, s, chip_idx, "cp_slot_sum_%d" % k)
              for k, (a, s) in enumerate(zip(cp_parts[:-1], cp_slots[:-1]))]
    small_red = _sum_slots(cp_slots[-1], "slot_sum_small")
    last = ["cp_in", "cp_out"]
    last_grads = _join_halves(red_cp, [(0, 0), (1, 0)], [big[n][0].shape for n in last], "join_halves_cp")
    big_out.update(adamw_big(last, last_grads))

    (g_meta, g_mix, g_ffn, g_conv_w, g_conv_b, g_ln_g, g_ln_b, g_pool_w, g_pool_scale, g_gate_w, g_gate_b, g_head,
     g_final, loss_sum) = _unpack(small_red, [a.shape for a in small_full])
    g_meta = _col_shard(g_meta, chip, meta_tokens.shape[-1])
    g_conv_w = _col_shard(g_conv_w, chip, cp_conv_w.shape[-1])
    g_gate_w = _col_shard(g_gate_w, chip, gla_gate_w2.shape[-1])
    g_gate_b = _col_shard(g_gate_b, chip, gla_gate_b.shape[-1])
    g_head = _col_shard(g_head, chip, gla_head_g.shape[-1])
    small_w = [meta_tokens, mix_norm_g, ffn_norm_g, cp_conv_w, cp_conv_b, cp_ln_g, cp_ln_b, cp_pool_w, cp_pool_scale,
               gla_gate_w2, gla_gate_b, gla_head_g, final_norm_g]
    small_m = [m_meta_tokens, m_mix_norm_g, m_ffn_norm_g, m_cp_conv_w, m_cp_conv_b, m_cp_ln_g, m_cp_ln_b, m_cp_pool_w,
               m_cp_pool_scale, m_gla_gate_w2, m_gla_gate_b, m_gla_head_g, m_final_norm_g]
    small_v = [v_meta_tokens, v_mix_norm_g, v_ffn_norm_g, v_cp_conv_w, v_cp_conv_b, v_cp_ln_g, v_cp_ln_b, v_cp_pool_w,
               v_cp_pool_scale, v_gla_gate_w2, v_gla_gate_b, v_gla_head_g, v_final_norm_g]
    small_g = [g_meta, g_mix, g_ffn, g_conv_w, g_conv_b, g_ln_g, g_ln_b, g_pool_w, g_pool_scale, g_gate_w, g_gate_b,
               g_head, g_final]
    shapes = [w.shape for w in small_w]
    small_g = [g.reshape(s) for g, s in zip(small_g, shapes)]
    at_least_2d = lambda arrs: [a.reshape(1, -1) if a.ndim == 1 else a for a in arrs]
    s_delta, s_m, s_v = _adamw_many(at_least_2d(small_w), at_least_2d(small_g), at_least_2d(small_m), at_least_2d(small_v))
    s_delta, s_m, s_v = [[a.reshape(s) for a, s in zip(group, shapes)] for group in (s_delta, s_m, s_v)]

    order = ["meta", "mix", "ffn", "w1", "w2", "cp_in", "conv_w", "conv_b", "ln_g", "ln_b", "pool_w", "pool_scale",
             "cp_out", "gla_in", "gate_w", "gate_b", "head", "gla_out", "final"]
    small_names = ["meta", "mix", "ffn", "conv_w", "conv_b", "ln_g", "ln_b", "pool_w", "pool_scale", "gate_w", "gate_b",
                   "head", "final"]
    big_names = ["w1", "w2", "cp_in", "cp_out", "gla_in", "gla_out"]
    table = {n: (small_g[i], s_delta[i], s_m[i], s_v[i]) for i, n in enumerate(small_names)}
    table.update({n: tuple(big_out[n]) for n in big_names})
    loss = loss_sum.reshape(())
    return (loss, grad_x, *[table[n][0] for n in order], *[table[n][1] for n in order],
            *[table[n][2] for n in order], *[table[n][3] for n in order])
```

```python
import functools

import jax
import jax.numpy as jnp
from jax import lax
from jax.experimental import pallas as pl
from jax.experimental.pallas import tpu as pltpu

F32 = jnp.float32
BF16 = jnp.bfloat16

D_MODEL = 1024
N_META = 16
CHUNK = 64
PAD_ROWS = CHUNK - N_META
EPS = 1e-5
CONV_DIM = 512
CONV_WIDTH = 31
CONV_HALO = 32
POOL_DIM = 512
POOL_WINDOWS = (2, 4, 8, 16)
POOL_GROUP = 128
POOL_HALO = 16
CP_IN = 2 * CONV_DIM + POOL_DIM
GLA_HEADS = 4
GLA_DK = 512
GLA_DV = 1024
GLA_HK = GLA_DK // GLA_HEADS
GLA_HV = GLA_DV // GLA_HEADS
GATE_RANK = 16
GATE_PAD = 128
GATE_NORM = 16.0
GLA_IN = 2 * GLA_DK + 2 * GLA_DV + GATE_RANK
GLA_IN_PAD = 2 * GLA_DK + 2 * GLA_DV + GATE_PAD
N_CHIPS = 4
ADAM_LR = 0.001
ADAM_B1 = 0.9
ADAM_B2 = 0.999
ADAM_EPS = 1e-08
ADAM_WD = 0.01
ADAM_STEP = 10

VMEM_LIMIT_BYTES = 56 * 1024 * 1024
ROW_TILE_TARGET = 832
TOKEN_TILE_TARGET = 1040
PACK_WIDTH = 1024
MESH = pl.DeviceIdType.MESH
HBM_SPEC = pl.BlockSpec(memory_space=pltpu.HBM)
ANY_SPEC = pl.BlockSpec(memory_space=pl.ANY)
SEM_SPEC = pl.BlockSpec(memory_space=pltpu.SEMAPHORE)
SIDE_EFFECT = pltpu.SideEffectType.DATAFLOW_SIDE_EFFECTING


def _cparams(*sem):
    return pltpu.CompilerParams(dimension_semantics=sem, vmem_limit_bytes=VMEM_LIMIT_BYTES)


def _row_tile(t, target, mult):
    best = mult
    for cand in range(mult, min(t, target) + 1, mult):
        if t % cand == 0:
            best = cand
    assert t % best == 0, (t, best)
    return best


def _rms(h, g):
    return h * lax.rsqrt(jnp.mean(h * h, axis=-1, keepdims=True) + EPS) * g


def _rms_bwd(h, g, du):
    r = lax.rsqrt(jnp.mean(h * h, axis=-1, keepdims=True) + EPS)
    xhat = h * r
    dxh = du * g
    dh = r * (dxh - xhat * jnp.mean(dxh * xhat, axis=-1, keepdims=True))
    return dh, du * xhat


def _valid_rows(i, tm):
    row = i * tm + lax.broadcasted_iota(jnp.int32, (tm, 1), 0)
    return row >= PAD_ROWS


def _dot(a, b):
    return jnp.dot(a, b, preferred_element_type=F32)


def _dot_nt(a, b):
    return lax.dot_general(a, b, (((1,), (1,)), ((), ())), preferred_element_type=F32)


def _dot_tn(a, b):
    return lax.dot_general(a, b, (((0,), (0,)), ((), ())), preferred_element_type=F32)


def _accumulate(ref, val, first):
    @pl.when(first)
    def _():
        ref[...] = val

    @pl.when(jnp.logical_not(first))
    def _():
        ref[...] += val


def _call_after(dep, body, n_in, in_specs, args, **kw):
    if dep is None:
        return pl.pallas_call(body, in_specs=in_specs, **kw)(*args)

    def with_dep(*refs):
        body(*refs[:n_in], *refs[n_in + 1:])

    return pl.pallas_call(with_dep, in_specs=list(in_specs) + [ANY_SPEC], **kw)(*args, dep)


def _norm_matmul(h, g, w, nc, name, dep=None):
    t, d = h.shape
    n = w.shape[1]
    tm = _row_tile(t, TOKEN_TILE_TARGET, 16)

    def body(h_ref, g_ref, w_ref, z_ref, u_ref):
        u = _rms(h_ref[...], g_ref[...]).astype(BF16)
        u_ref[...] = u
        for n0 in range(0, n, nc):
            z_ref[:, n0:n0 + nc] = _dot(u, w_ref[:, n0:n0 + nc]).astype(BF16)

    return _call_after(
        dep, body, 3,
        [pl.BlockSpec((tm, d), lambda i: (i, 0)), pl.BlockSpec((1, d), lambda i: (0, 0)),
         pl.BlockSpec((d, n), lambda i: (0, 0))], (h, g, w), grid=(t // tm,),
        out_specs=[pl.BlockSpec((tm, n), lambda i: (i, 0)), pl.BlockSpec((tm, d), lambda i: (i, 0))],
        out_shape=[jax.ShapeDtypeStruct((t, n), BF16), jax.ShapeDtypeStruct((t, d), BF16)],
        compiler_params=_cparams("parallel"), name=name)


def _matmul_residual(a, w, h, name, dep=None):
    t, k = a.shape
    d = w.shape[1]
    tm = _row_tile(t, TOKEN_TILE_TARGET, 16)

    def body(a_ref, w_ref, h_ref, o_ref):
        o_ref[...] = h_ref[...] + _dot(a_ref[...], w_ref[...])

    return _call_after(
        dep, body, 3,
        [pl.BlockSpec((tm, k), lambda i: (i, 0)), pl.BlockSpec((k, d), lambda i: (0, 0)),
         pl.BlockSpec((tm, d), lambda i: (i, 0))], (a, w, h), grid=(t // tm,),
        out_specs=pl.BlockSpec((tm, d), lambda i: (i, 0)),
        out_shape=jax.ShapeDtypeStruct((t, d), F32),
        compiler_params=_cparams("parallel"), name=name)


def _ffn_fwd(h, g, w1g, w2g, name):
    t, d = h.shape
    ns, ffs = w1g.shape[0], w1g.shape[2]
    tm = _row_tile(t, TOKEN_TILE_TARGET, 16)

    def body(h_ref, g_ref, w1_ref, w2_ref, ho_ref, hp_ref, u_ref, acc_ref):
        s = pl.program_id(1)

        @pl.when(s == 0)
        def _():
            u_ref[...] = _rms(h_ref[...], g_ref[...]).astype(BF16)

        hp = _dot(u_ref[...], w1_ref[...])
        hp_ref[...] = hp.astype(BF16)
        a = jnp.maximum(hp, 0.0)
        _accumulate(acc_ref, _dot((a * a).astype(BF16), w2_ref[...]), s == 0)

        @pl.when(s == ns - 1)
        def _():
            ho_ref[...] = h_ref[...] + acc_ref[...]

    return pl.pallas_call(
        body, grid=(t // tm, ns),
        in_specs=[pl.BlockSpec((tm, d), lambda i, s: (i, 0)), pl.BlockSpec((1, d), lambda i, s: (0, 0)),
                  pl.BlockSpec((None, d, ffs), lambda i, s: (s, 0, 0)),
                  pl.BlockSpec((None, ffs, d), lambda i, s: (s, 0, 0))],
        out_specs=[pl.BlockSpec((tm, d), lambda i, s: (i, 0)), pl.BlockSpec((tm, ffs), lambda i, s: (i, s)),
                   pl.BlockSpec((tm, d), lambda i, s: (i, 0))],
        out_shape=[jax.ShapeDtypeStruct((t, d), F32), jax.ShapeDtypeStruct((t, ns * ffs), BF16),
                   jax.ShapeDtypeStruct((t, d), BF16)],
        scratch_shapes=[pltpu.VMEM((tm, d), F32)],
        compiler_params=_cparams("parallel", "arbitrary"), name=name)(h, g, w1g, w2g)


def _ffn_bwd_data(dh, h, g, hp, w1g, w2g, name, dep=None):
    t, d = h.shape
    ns, ffs = w1g.shape[0], w1g.shape[2]
    tm = _row_tile(t, ROW_TILE_TARGET, CHUNK)

    def body(dh_ref, h_ref, g_ref, hp_ref, w1_ref, w2_ref, dhi_ref, dhp_ref, dg_ref, acc_ref):
        i, s = pl.program_id(0), pl.program_id(1)
        da = _dot_nt(dh_ref[...].astype(BF16), w2_ref[...])
        dhp = (da * (2.0 * jnp.maximum(hp_ref[...].astype(F32), 0.0))).astype(BF16)
        dhp_ref[...] = dhp
        _accumulate(acc_ref, _dot_nt(dhp, w1_ref[...]), s == 0)

        @pl.when(s == ns - 1)
        def _():
            dhn, dgr = _rms_bwd(h_ref[...], g_ref[...], acc_ref[...])
            dhi_ref[...] = jnp.where(_valid_rows(i, tm), dh_ref[...] + dhn, 0.0)
            _accumulate(dg_ref, jnp.sum(dgr, axis=0, keepdims=True), i == 0)

    return _call_after(
        dep, body, 6,
        [pl.BlockSpec((tm, d), lambda i, s: (i, 0)), pl.BlockSpec((tm, d), lambda i, s: (i, 0)),
         pl.BlockSpec((1, d), lambda i, s: (0, 0)), pl.BlockSpec((tm, ffs), lambda i, s: (i, s)),
         pl.BlockSpec((None, d, ffs), lambda i, s: (s, 0, 0)),
         pl.BlockSpec((None, ffs, d), lambda i, s: (s, 0, 0))], (dh, h, g, hp, w1g, w2g), grid=(t // tm, ns),
        out_specs=[pl.BlockSpec((tm, d), lambda i, s: (i, 0)), pl.BlockSpec((tm, ffs), lambda i, s: (i, s)),
                   pl.BlockSpec((1, d), lambda i, s: (0, 0))],
        out_shape=[jax.ShapeDtypeStruct((t, d), F32), jax.ShapeDtypeStruct((t, ns * ffs), BF16),
                   jax.ShapeDtypeStruct((1, d), F32)],
        scratch_shapes=[pltpu.VMEM((tm, d), F32)],
        compiler_params=_cparams("arbitrary", "arbitrary"), name=name)


WGRAD_ROWS = 1024


def _wgrad(x, dy, nb, xc, yc, x_by_block, dy_by_block, relu2, name, dep=None):
    t = x.shape[0]
    tk = _row_tile(t - CHUNK, WGRAD_ROWS, CHUNK)

    def prep(xv):
        if relu2:
            xv = jnp.maximum(xv.astype(F32), 0.0)
            xv = xv * xv
        return xv.astype(BF16)

    def body(xh_ref, dyh_ref, x_ref, dy_ref, o_ref):
        k = pl.program_id(1)
        p = _dot_tn(prep(x_ref[...]), dy_ref[...].astype(BF16))

        @pl.when(k == 0)
        def _():
            o_ref[...] = p + _dot_tn(prep(xh_ref[...]), dyh_ref[...].astype(BF16))

        @pl.when(k > 0)
        def _():
            o_ref[...] += p

    def head(width, by_block):
        return pl.BlockSpec((CHUNK, width), (lambda b, k: (0, b)) if by_block else (lambda b, k: (0, 0)))

    def rest(width, by_block):
        def index(b, k):
            return pl.multiple_of(CHUNK + k * tk, CHUNK), (pl.multiple_of(b * width, 128) if by_block else 0)
        return pl.BlockSpec((pl.Element(tk), pl.Element(width)), index)

    return _call_after(
        dep, body, 4,
        [head(xc, x_by_block), head(yc, dy_by_block), rest(xc, x_by_block), rest(yc, dy_by_block)], (x, dy, x, dy),
        grid=(nb, (t - CHUNK) // tk),
        out_specs=pl.BlockSpec((None, xc, yc), lambda b, k: (b, 0, 0)),
        out_shape=jax.ShapeDtypeStruct((nb, xc, yc), F32),
        compiler_params=_cparams("parallel", "arbitrary"), name=name)


def _dgrad(dh, w, name, dep=None):
    t, d = dh.shape
    k = w.shape[0]
    tm = _row_tile(t, TOKEN_TILE_TARGET, 16)

    def body(dh_ref, w_ref, o_ref):
        o_ref[...] = _dot_nt(dh_ref[...].astype(BF16), w_ref[...]).astype(BF16)

    return _call_after(
        dep, body, 2,
        [pl.BlockSpec((tm, d), lambda i: (i, 0)), pl.BlockSpec((k, d), lambda i: (0, 0))], (dh, w), grid=(t // tm,),
        out_specs=pl.BlockSpec((tm, k), lambda i: (i, 0)),
        out_shape=jax.ShapeDtypeStruct((t, k), BF16),
        compiler_params=_cparams("parallel"), name=name)


def _dgrad_norm_bwd(dz, w, h, g, dh, nc, name):
    t, d = h.shape
    n = w.shape[1]
    tm = _row_tile(t, ROW_TILE_TARGET // 2, 16)

    def body(dz_ref, w_ref, h_ref, g_ref, dh_ref, dhi_ref, dg_ref):
        i = pl.program_id(0)
        du = jnp.zeros((tm, d), F32)
        for n0 in range(0, n, nc):
            du = du + _dot_nt(dz_ref[:, n0:n0 + nc], w_ref[:, n0:n0 + nc])
        dhn, dgr = _rms_bwd(h_ref[...], g_ref[...], du)
        dhi_ref[...] = jnp.where(_valid_rows(i, tm), dh_ref[...] + dhn, 0.0)
        _accumulate(dg_ref, jnp.sum(dgr, axis=0, keepdims=True), i == 0)

    return pl.pallas_call(
        body, grid=(t // tm,),
        in_specs=[pl.BlockSpec((tm, n), lambda i: (i, 0)), pl.BlockSpec((d, n), lambda i: (0, 0)),
                  pl.BlockSpec((tm, d), lambda i: (i, 0)), pl.BlockSpec((1, d), lambda i: (0, 0)),
                  pl.BlockSpec((tm, d), lambda i: (i, 0))],
        out_specs=[pl.BlockSpec((tm, d), lambda i: (i, 0)), pl.BlockSpec((1, d), lambda i: (0, 0))],
        out_shape=[jax.ShapeDtypeStruct((t, d), F32), jax.ShapeDtypeStruct((1, d), F32)],
        compiler_params=_cparams("arbitrary"), name=name)(dz, w, h, g, dh)


def _loss_bwd(h, g, target):
    t, d = h.shape
    tl = _row_tile(t - CHUNK, 1024, CHUNK)

    def body(h_ref, g_ref, t_ref, dh_ref, dg_ref, loss_ref):
        i = pl.program_id(0)
        hv, gv = h_ref[...], g_ref[...]
        err = _rms(hv, gv) - t_ref[...]
        part = 0.5 * jnp.sum(jnp.mean(err * err, axis=-1, keepdims=True), axis=0, keepdims=True)
        dhn, dgr = _rms_bwd(hv, gv, err * (1.0 / d))
        dh_ref[...] = dhn
        _accumulate(dg_ref, jnp.sum(dgr, axis=0, keepdims=True), i == 0)
        _accumulate(loss_ref, jnp.broadcast_to(part, (8, 128)), i == 0)

    shifted = pl.BlockSpec((pl.Element(tl), pl.Element(d)), lambda i: (pl.multiple_of(CHUNK + i * tl, CHUNK), 0))
    dh, dg, loss = pl.pallas_call(
        body, grid=((t - CHUNK) // tl,),
        in_specs=[shifted, pl.BlockSpec((1, d), lambda i: (0, 0)), pl.BlockSpec((tl, d), lambda i: (i, 0))],
        out_specs=[shifted, pl.BlockSpec((1, d), lambda i: (0, 0)), pl.BlockSpec((8, 128), lambda i: (0, 0))],
        out_shape=[jax.ShapeDtypeStruct((t, d), F32), jax.ShapeDtypeStruct((1, d), F32),
                   jax.ShapeDtypeStruct((8, 128), F32)],
        compiler_params=_cparams("arbitrary"), name="loss_bwd")(h, g, target)

    def zero_head(dh_ref, o_ref):
        o_ref[...] = jnp.zeros_like(o_ref)

    dh = pl.pallas_call(
        zero_head, grid=(1,), in_specs=[ANY_SPEC], out_specs=pl.BlockSpec((CHUNK, d), lambda i: (0, 0)),
        out_shape=jax.ShapeDtypeStruct((t, d), F32), input_output_aliases={0: 0}, name="loss_bwd_head")(dh)
    return dh, dg, loss


CONV_BLOCK = 32


def _silu(x):
    return x * jax.nn.sigmoid(x)


def _row_shifts(win):
    n = win.shape[0]
    return [win] + [pltpu.roll(win, n - j, 0) for j in range(1, 8)]


def _cp_seq_fwd(z, conv_w, conv_b, ln_g, ln_b, pool_w, pool_scale):
    t = z.shape[0]
    tm = _row_tile(t, ROW_TILE_TARGET, CHUNK)

    def body(z_ref, cw_ref, cb_ref, lg_ref, lb_ref, pw_ref, ps_ref, c_ref, pm_ref, mix_ref, gbuf, pbuf):
        i = pl.program_id(0)

        @pl.when(i == 0)
        def _():
            gbuf[0:CONV_HALO, :] = jnp.zeros((CONV_HALO, CONV_DIM), F32)
            pbuf[0:POOL_HALO, :] = jnp.zeros((POOL_HALO, POOL_DIM), F32)

        @pl.when(i > 0)
        def _():
            gbuf[0:CONV_HALO, :] = gbuf[tm:tm + CONV_HALO, :]
            pbuf[0:POOL_HALO, :] = pbuf[tm:tm + POOL_HALO, :]

        av = z_ref[:, 0:CONV_DIM].astype(F32)
        ag = z_ref[:, CONV_DIM:2 * CONV_DIM].astype(F32)
        gbuf[CONV_HALO:CONV_HALO + tm, :] = av * jax.nn.sigmoid(ag)
        pbuf[POOL_HALO:POOL_HALO + tm, :] = z_ref[:, 2 * CONV_DIM:CP_IN].astype(F32)

        def conv_block(rb, carry):
            base = pl.multiple_of(rb * CONV_BLOCK, CONV_BLOCK)
            shifted = _row_shifts(gbuf[pl.ds(base, CONV_BLOCK + CONV_HALO), :])
            acc = jnp.zeros((CONV_BLOCK, CONV_DIM), F32)
            for k in range(CONV_WIDTH):
                whole, part = divmod(CONV_HALO - (CONV_WIDTH - 1) + k, 8)
                acc = acc + cw_ref[k:k + 1, :] * shifted[part][8 * whole:8 * whole + CONV_BLOCK, :]
            c_ref[pl.ds(base, CONV_BLOCK), :] = acc + cb_ref[...]
            return carry

        lax.fori_loop(0, tm // CONV_BLOCK, conv_block, 0)

        c = c_ref[...]
        mu = jnp.mean(c, axis=-1, keepdims=True)
        xc = c - mu
        ln = xc * lax.rsqrt(jnp.mean(xc * xc, axis=-1, keepdims=True) + EPS) * lg_ref[...] + lb_ref[...]
        row = i * tm + lax.broadcasted_iota(jnp.int32, (tm, 1), 0)
        mix_ref[:, 0:CONV_DIM] = jnp.where(row >= PAD_ROWS, _silu(ln), 0.0).astype(BF16)

        tpos = (row - PAD_ROWS + 1).astype(F32)
        for gi, wdw in enumerate(POOL_WINDOWS):
            lo = POOL_GROUP * gi
            cur = pbuf[POOL_HALO:POOL_HALO + tm, lo:lo + POOL_GROUP]
            sacc = cur
            for j in range(1, wdw):
                sacc = sacc + pbuf[POOL_HALO - j:POOL_HALO - j + tm, lo:lo + POOL_GROUP]
            pm = (sacc / jnp.clip(tpos, 1.0, float(wdw)) - cur).astype(BF16)
            pm_ref[:, lo:lo + POOL_GROUP] = pm
            pg = _dot(pm, pw_ref[gi].astype(BF16))
            mix_ref[:, CONV_DIM + lo:CONV_DIM + lo + POOL_GROUP] = (pg * ps_ref[:, lo:lo + POOL_GROUP]).astype(BF16)

    vec = pl.BlockSpec((1, CONV_DIM), lambda i: (0, 0))
    return pl.pallas_call(
        body, grid=(t // tm,),
        in_specs=[pl.BlockSpec((tm, CP_IN), lambda i: (i, 0)),
                  pl.BlockSpec((CONV_WIDTH, CONV_DIM), lambda i: (0, 0)), vec, vec, vec,
                  pl.BlockSpec((len(POOL_WINDOWS), POOL_GROUP, POOL_GROUP), lambda i: (0, 0, 0)), vec],
        out_specs=[pl.BlockSpec((tm, CONV_DIM), lambda i: (i, 0)), pl.BlockSpec((tm, POOL_DIM), lambda i: (i, 0)),
                   pl.BlockSpec((tm, CONV_DIM + POOL_DIM), lambda i: (i, 0))],
        out_shape=[jax.ShapeDtypeStruct((t, CONV_DIM), F32), jax.ShapeDtypeStruct((t, POOL_DIM), BF16),
                   jax.ShapeDtypeStruct((t, CONV_DIM + POOL_DIM), BF16)],
        scratch_shapes=[pltpu.VMEM((tm + CONV_HALO, CONV_DIM), F32), pltpu.VMEM((tm + POOL_HALO, POOL_DIM), F32)],
        compiler_params=_cparams("arbitrary"), name="cp_seq_fwd")(z, conv_w, conv_b, ln_g, ln_b, pool_w, pool_scale)


def _cp_seq_bwd(dmix, z, c, pm, conv_w, ln_g, ln_b, pool_w, pool_scale, dep=None):
    t = z.shape[0]
    tm = _row_tile(t, ROW_TILE_TARGET, CHUNK)
    nt = t // tm

    def body(dmix_ref, z_ref, c_ref, pm_ref, cw_ref, lg_ref, lb_ref, pw_ref, ps_ref,
             dz_ref, dcw_ref, dvec_ref, dpw_ref, dcbuf, qbuf, glu_buf, dwacc):
        i = pl.program_id(0)
        tile = nt - 1 - i

        @pl.when(i == 0)
        def _():
            dcbuf[tm:tm + CONV_HALO, :] = jnp.zeros((CONV_HALO, CONV_DIM), F32)
            qbuf[tm:tm + POOL_HALO, :] = jnp.zeros((POOL_HALO, POOL_DIM), F32)
            dcw_ref[...] = jnp.zeros_like(dcw_ref)
            dwacc[...] = jnp.zeros_like(dwacc)
            dvec_ref[...] = jnp.zeros_like(dvec_ref)
            dpw_ref[...] = jnp.zeros_like(dpw_ref)

        @pl.when(i > 0)
        def _():
            dcbuf[tm:tm + CONV_HALO, :] = dcbuf[0:CONV_HALO, :]
            qbuf[tm:tm + POOL_HALO, :] = qbuf[0:POOL_HALO, :]

        row = tile * tm + lax.broadcasted_iota(jnp.int32, (tm, 1), 0)
        cv = c_ref[...]
        mu = jnp.mean(cv, axis=-1, keepdims=True)
        xc = cv - mu
        rstd = lax.rsqrt(jnp.mean(xc * xc, axis=-1, keepdims=True) + EPS)
        xhat = xc * rstd
        ln = xhat * lg_ref[...] + lb_ref[...]
        sg = jax.nn.sigmoid(ln)
        da = jnp.where(row >= PAD_ROWS, dmix_ref[:, 0:CONV_DIM].astype(F32), 0.0)
        dln = da * (sg * (1.0 + ln * (1.0 - sg)))
        dxh = dln * lg_ref[...]
        dc = rstd * (dxh - jnp.mean(dxh, axis=-1, keepdims=True) - xhat * jnp.mean(dxh * xhat, axis=-1, keepdims=True))
        dcbuf[0:tm, :] = dc
        dvec_ref[0:1, :] += jnp.sum(dc, axis=0, keepdims=True)
        dvec_ref[1:2, :] += jnp.sum(dln * xhat, axis=0, keepdims=True)
        dvec_ref[2:3, :] += jnp.sum(dln, axis=0, keepdims=True)

        av = z_ref[:, 0:CONV_DIM].astype(F32)
        sig_g = jax.nn.sigmoid(z_ref[:, CONV_DIM:2 * CONV_DIM].astype(F32))
        glu_buf[...] = av * sig_g

        def conv_block(rb, carry):
            base = pl.multiple_of(rb * CONV_BLOCK, CONV_BLOCK)
            shifted = _row_shifts(dcbuf[pl.ds(base, CONV_BLOCK + CONV_HALO), :])
            glu = glu_buf[pl.ds(base, CONV_BLOCK), :]
            acc = jnp.zeros((CONV_BLOCK, CONV_DIM), F32)
            for k in range(CONV_WIDTH):
                whole, part = divmod(CONV_WIDTH - 1 - k, 8)
                slab = shifted[part][8 * whole:8 * whole + CONV_BLOCK, :]
                acc = acc + cw_ref[k:k + 1, :] * slab
                prod = slab * glu
                part = prod[0:8]
                for q in range(1, CONV_BLOCK // 8):
                    part = part + prod[8 * q:8 * q + 8]
                dwacc[k] += part
            glu_buf[pl.ds(base, CONV_BLOCK), :] = acc
            return carry

        lax.fori_loop(0, tm // CONV_BLOCK, conv_block, 0)

        @pl.when(i == nt - 1)
        def _():
            for k in range(CONV_WIDTH):
                dcw_ref[k:k + 1, :] = jnp.sum(dwacc[k], axis=0, keepdims=True)
        dglu = glu_buf[...]
        dz_ref[:, 0:CONV_DIM] = (dglu * sig_g).astype(BF16)
        dz_ref[:, CONV_DIM:2 * CONV_DIM] = (dglu * av * sig_g * (1.0 - sig_g)).astype(BF16)

        tpos = (row - PAD_ROWS + 1).astype(F32)
        for gi, wdw in enumerate(POOL_WINDOWS):
            lo = POOL_GROUP * gi
            dp = dmix_ref[:, CONV_DIM + lo:CONV_DIM + lo + POOL_GROUP].astype(F32)
            pmv = pm_ref[:, lo:lo + POOL_GROUP]
            pwb = pw_ref[gi].astype(BF16)
            dvec_ref[3:4, lo:lo + POOL_GROUP] += jnp.sum(dp * _dot(pmv, pwb), axis=0, keepdims=True)
            dq = (dp * ps_ref[:, lo:lo + POOL_GROUP]).astype(BF16)
            dpw_ref[gi] += _dot_tn(pmv, dq)
            dpm = _dot_nt(dq, pwb)
            qbuf[0:tm, lo:lo + POOL_GROUP] = dpm / jnp.clip(tpos, 1.0, float(wdw))
            sacc = -dpm
            for j in range(wdw):
                sacc = sacc + qbuf[j:j + tm, lo:lo + POOL_GROUP]
            dz_ref[:, 2 * CONV_DIM + lo:2 * CONV_DIM + lo + POOL_GROUP] = sacc.astype(BF16)

    vec = pl.BlockSpec((1, CONV_DIM), lambda i: (0, 0))
    rev = lambda i: (nt - 1 - i, 0)
    return _call_after(
        dep, body, 9,
        [pl.BlockSpec((tm, CONV_DIM + POOL_DIM), rev), pl.BlockSpec((tm, CP_IN), rev),
         pl.BlockSpec((tm, CONV_DIM), rev), pl.BlockSpec((tm, POOL_DIM), rev),
         pl.BlockSpec((CONV_WIDTH, CONV_DIM), lambda i: (0, 0)), vec, vec,
         pl.BlockSpec((len(POOL_WINDOWS), POOL_GROUP, POOL_GROUP), lambda i: (0, 0, 0)), vec],
        (dmix, z, c, pm, conv_w, ln_g, ln_b, pool_w, pool_scale), grid=(nt,),
        out_specs=[pl.BlockSpec((tm, CP_IN), rev), pl.BlockSpec((CONV_WIDTH + 1, CONV_DIM), lambda i: (0, 0)),
                   pl.BlockSpec((8, CONV_DIM), lambda i: (0, 0)),
                   pl.BlockSpec((len(POOL_WINDOWS), POOL_GROUP, POOL_GROUP), lambda i: (0, 0, 0))],
        out_shape=[jax.ShapeDtypeStruct((t, CP_IN), BF16), jax.ShapeDtypeStruct((CONV_WIDTH + 1, CONV_DIM), F32),
                   jax.ShapeDtypeStruct((8, CONV_DIM), F32),
                   jax.ShapeDtypeStruct((len(POOL_WINDOWS), POOL_GROUP, POOL_GROUP), F32)],
        scratch_shapes=[pltpu.VMEM((tm + CONV_HALO, CONV_DIM), F32), pltpu.VMEM((tm + POOL_HALO, POOL_DIM), F32),
                        pltpu.VMEM((tm, CONV_DIM), F32), pltpu.VMEM((CONV_WIDTH + 1, 8, CONV_DIM), F32)],
        compiler_params=_cparams("arbitrary"), name="cp_seq_bwd")


GLA_UNROLL = 2
Q0, K0, V0, G0, R0 =0, GLA_DK, 2 * GLA_DK, 2 * GLA_DK + GLA_DV, 2 * GLA_DK + 2 * GLA_DV


def _split3(x):
    hi = x.astype(BF16)
    r1 = x - hi.astype(F32)
    mid = r1.astype(BF16)
    lo = (r1 - mid.astype(F32)).astype(BF16)
    return hi, mid, lo


def _tri(strict):
    r = lax.broadcasted_iota(jnp.int32, (CHUNK, CHUNK), 0)
    c = lax.broadcasted_iota(jnp.int32, (CHUNK, CHUNK), 1)
    return ((r > c) if strict else (r >= c)).astype(BF16)


def _chunk_sums(x, cpt, strict, pieces):
    tri3 = jnp.broadcast_to(_tri(strict)[None], (cpt, CHUNK, CHUNK))
    acc = None
    for piece in _split3(x.reshape(cpt, CHUNK, x.shape[-1]))[:pieces]:
        part = jnp.einsum("bij,bjk->bik", tri3, piece, preferred_element_type=F32)
        acc = part if acc is None else acc + part
    return acc


def _chunk_decay(r, gw_ref, gb_ref, cpt):
    pre = _dot(r, gw_ref[...]) + gb_ref[...]
    lac = (jnp.minimum(pre, 0.0) - jnp.log(1.0 + jnp.exp(-jnp.abs(pre)))) * (1.0 / GATE_NORM)
    cum3 = _chunk_sums(lac, cpt, False, 3)
    return cum3, cum3[:, CHUNK - 1:CHUNK, :]


def _gla_seq_fwd(z, gate_w, gate_b, head_g, dep=None):
    t = z.shape[0]
    tm = _row_tile(t, ROW_TILE_TARGET, CHUNK)
    cpt = tm // CHUNK
    scale = GLA_HK ** -0.5

    def body(z_ref, gw_ref, gb_ref, hg_ref, o_ref, mix_ref, st_ref, state, kdec_s, e_s):
        @pl.when(pl.program_id(0) == 0)
        def _():
            state[...] = jnp.zeros_like(state)

        cum3, tot3 = _chunk_decay(z_ref[:, R0:R0 + GATE_PAD], gw_ref, gb_ref, cpt)
        dec = jnp.exp(jnp.broadcast_to(tot3, cum3.shape) - cum3).reshape(tm, GLA_DK)
        kdec_s[...] = (z_ref[:, K0:K0 + GLA_DK].astype(F32) * dec).astype(BF16)
        e_s[...] = jnp.exp(jnp.broadcast_to(tot3, (cpt, 8, GLA_DK))).reshape(cpt * 8, GLA_DK)

        def chunk(ci, carry):
            rows = pl.ds(pl.multiple_of(ci * CHUNK, CHUNK), CHUNK)
            e_all = e_s[pl.ds(pl.multiple_of(ci * 8, 8), 8), :][0:1, :]
            st_ref[ci] = state[...].astype(BF16)
            for hd in range(GLA_HEADS):
                ks = slice(hd * GLA_HK, (hd + 1) * GLA_HK)
                vs = slice(hd * GLA_HV, (hd + 1) * GLA_HV)
                v = z_ref[rows, V0 + hd * GLA_HV:V0 + (hd + 1) * GLA_HV]
                st = state[vs, :] * e_all[:, ks] + _dot_tn(v, kdec_s[rows, ks])
                state[vs, :] = st
                q = z_ref[rows, Q0 + hd * GLA_HK:Q0 + (hd + 1) * GLA_HK]
                o_ref[rows, vs] = (_dot_nt(q, st.astype(BF16)) * scale).astype(BF16)
            return carry

        lax.fori_loop(0, cpt, chunk, 0, unroll=GLA_UNROLL)

        for hd in range(GLA_HEADS):
            vs = slice(hd * GLA_HV, (hd + 1) * GLA_HV)
            on = _rms(o_ref[:, vs].astype(F32), hg_ref[...])
            gv = z_ref[:, G0 + hd * GLA_HV:G0 + (hd + 1) * GLA_HV].astype(F32)
            mix_ref[:, vs] = (on * _silu(gv)).astype(BF16)

    return _call_after(
        dep, body, 4,
        [pl.BlockSpec((tm, GLA_IN_PAD), lambda i: (i, 0)),
         pl.BlockSpec((GATE_PAD, GLA_DK), lambda i: (0, 0)), pl.BlockSpec((1, GLA_DK), lambda i: (0, 0)),
         pl.BlockSpec((1, GLA_HV), lambda i: (0, 0))], (z, gate_w, gate_b, head_g), grid=(t // tm,),
        out_specs=[pl.BlockSpec((tm, GLA_DV), lambda i: (i, 0)), pl.BlockSpec((tm, GLA_DV), lambda i: (i, 0)),
                   pl.BlockSpec((cpt, GLA_DV, GLA_HK), lambda i: (i, 0, 0))],
        out_shape=[jax.ShapeDtypeStruct((t, GLA_DV), BF16), jax.ShapeDtypeStruct((t, GLA_DV), BF16),
                   jax.ShapeDtypeStruct((t // CHUNK, GLA_DV, GLA_HK), BF16)],
        scratch_shapes=[pltpu.VMEM((GLA_DV, GLA_HK), F32), pltpu.VMEM((tm, GLA_DK), BF16),
                        pltpu.VMEM((cpt * 8, GLA_DK), F32)],
        compiler_params=_cparams("arbitrary"), name="gla_seq_fwd")


def _gla_seq_bwd(dmix, o, z, states, gate_w, gate_b, head_g, dep=None):
    t = z.shape[0]
    tm = _row_tile(t, ROW_TILE_TARGET, CHUNK)
    cpt = tm // CHUNK
    nt = t // tm
    scale = GLA_HK ** -0.5

    def body(dmix_ref, o_ref, z_ref, st_ref, gw_ref, gb_ref, hg_ref, dz_ref, dgw_ref, dgb_ref, dhg_ref,
             dstate, dec_s, kdec_s, dkdec_s, do_s, e_s, dtot_s):
        @pl.when(pl.program_id(0) == 0)
        def _():
            dstate[...] = jnp.zeros_like(dstate)
            dgw_ref[...] = jnp.zeros_like(dgw_ref)
            dgb_ref[...] = jnp.zeros_like(dgb_ref)
            dhg_ref[...] = jnp.zeros_like(dhg_ref)

        cum3, tot3 = _chunk_decay(z_ref[:, R0:R0 + GATE_PAD], gw_ref, gb_ref, cpt)
        dec = jnp.exp(jnp.broadcast_to(tot3, cum3.shape) - cum3).reshape(tm, GLA_DK)
        dec_s[...] = dec
        kdec_s[...] = z_ref[:, K0:K0 + GLA_DK].astype(F32) * dec
        e_s[...] = jnp.exp(jnp.broadcast_to(tot3, (cpt, 8, GLA_DK))).reshape(cpt * 8, GLA_DK)
        dhg = jnp.zeros((1, GLA_HV), F32)
        for hd in range(GLA_HEADS):
            vs = slice(hd * GLA_HV, (hd + 1) * GLA_HV)
            gcols = slice(G0 + hd * GLA_HV, G0 + (hd + 1) * GLA_HV)
            ov = o_ref[:, vs].astype(F32)
            gv = z_ref[:, gcols].astype(F32)
            dm = dmix_ref[:, vs].astype(F32)
            sg = jax.nn.sigmoid(gv)
            rr = lax.rsqrt(jnp.mean(ov * ov, axis=-1, keepdims=True) + EPS)
            xhat = ov * rr
            don = dm * (gv * sg)
            dz_ref[:, gcols] = (dm * (xhat * hg_ref[...]) * (sg * (1.0 + gv * (1.0 - sg)))).astype(BF16)
            dhg = dhg + jnp.sum(don * xhat, axis=0, keepdims=True)
            dxh = don * hg_ref[...]
            do_s[:, vs] = (rr * (dxh - xhat * jnp.mean(dxh * xhat, axis=-1, keepdims=True)) * scale).astype(BF16)
        dhg_ref[...] += dhg

        def chunk(cj, carry):
            ci = cpt - 1 - cj
            rows = pl.ds(pl.multiple_of(ci * CHUNK, CHUNK), CHUNK)
            erows = pl.ds(pl.multiple_of(ci * 8, 8), 8)
            e_all = e_s[erows, :][0:1, :]
            for hd in range(GLA_HEADS):
                ks = slice(hd * GLA_HK, (hd + 1) * GLA_HK)
                vs = slice(hd * GLA_HV, (hd + 1) * GLA_HV)
                e = e_all[:, ks]
                kdb = kdec_s[rows, ks].astype(BF16)
                v = z_ref[rows, V0 + hd * GLA_HV:V0 + (hd + 1) * GLA_HV]
                q = z_ref[rows, Q0 + hd * GLA_HK:Q0 + (hd + 1) * GLA_HK]
                do = do_s[rows, vs]
                st_prev = st_ref[ci, vs, :].astype(F32)
                st = st_prev * e + _dot_tn(v, kdb)
                dz_ref[rows, Q0 + hd * GLA_HK:Q0 + (hd + 1) * GLA_HK] = _dot(do, st.astype(BF16)).astype(BF16)
                dst = dstate[vs, :] + _dot_tn(do, q)
                dstb = dst.astype(BF16)
                dkdec_s[rows, ks] = _dot(v, dstb)
                dz_ref[rows, V0 + hd * GLA_HV:V0 + (hd + 1) * GLA_HV] = _dot_nt(kdb, dstb).astype(BF16)
                dtot = jnp.sum(dst * st_prev, axis=0, keepdims=True) * e
                dtot_s[erows, ks] = jnp.broadcast_to(dtot, (8, GLA_HK))
                dstate[vs, :] = dst * e
            return carry

        lax.fori_loop(0, cpt, chunk, 0, unroll=GLA_UNROLL)

        dkdec = dkdec_s[...]
        dz_ref[:, K0:K0 + GLA_DK] = (dkdec * dec_s[...]).astype(BF16)
        before = _chunk_sums(dkdec * kdec_s[...], cpt, True, 2)
        dtot3 = dtot_s[...].reshape(cpt, 8, GLA_DK)[:, 0:1, :]
        dlac = (jnp.broadcast_to(dtot3, before.shape) + before).reshape(tm, GLA_DK)
        pre = _dot(z_ref[:, R0:R0 + GATE_PAD], gw_ref[...]) + gb_ref[...]
        dpre = dlac * (1.0 / GATE_NORM) * (1.0 - jax.nn.sigmoid(pre))
        dpb = dpre.astype(BF16)
        dz_ref[:, R0:R0 + GATE_PAD] = _dot_nt(dpb, gw_ref[...]).astype(BF16)
        dgw_ref[...] += _dot_tn(z_ref[:, R0:R0 + GATE_PAD], dpb)
        dgb_ref[...] += jnp.sum(dpre, axis=0, keepdims=True)

    rev = lambda i: (nt - 1 - i, 0)
    return _call_after(
        dep, body, 7,
        [pl.BlockSpec((tm, GLA_DV), rev), pl.BlockSpec((tm, GLA_DV), rev), pl.BlockSpec((tm, GLA_IN_PAD), rev),
         pl.BlockSpec((cpt, GLA_DV, GLA_HK), lambda i: (nt - 1 - i, 0, 0)),
         pl.BlockSpec((GATE_PAD, GLA_DK), lambda i: (0, 0)), pl.BlockSpec((1, GLA_DK), lambda i: (0, 0)),
         pl.BlockSpec((1, GLA_HV), lambda i: (0, 0))],
        (dmix, o, z, states, gate_w, gate_b, head_g), grid=(nt,),
        out_specs=[pl.BlockSpec((tm, GLA_IN_PAD), rev), pl.BlockSpec((GATE_PAD, GLA_DK), lambda i: (0, 0)),
                   pl.BlockSpec((1, GLA_DK), lambda i: (0, 0)), pl.BlockSpec((1, GLA_HV), lambda i: (0, 0))],
        out_shape=[jax.ShapeDtypeStruct((t, GLA_IN_PAD), BF16), jax.ShapeDtypeStruct((GATE_PAD, GLA_DK), F32),
                   jax.ShapeDtypeStruct((1, GLA_DK), F32), jax.ShapeDtypeStruct((1, GLA_HV), F32)],
        scratch_shapes=[pltpu.VMEM((GLA_DV, GLA_HK), F32), pltpu.VMEM((tm, GLA_DK), F32), pltpu.VMEM((tm, GLA_DK), F32),
                        pltpu.VMEM((tm, GLA_DK), F32), pltpu.VMEM((tm, GLA_DV), BF16),
                        pltpu.VMEM((cpt * 8, GLA_DK), F32), pltpu.VMEM((cpt * 8, GLA_DK), F32)],
        compiler_params=_cparams("arbitrary"), name="gla_seq_bwd")


def _sum_halves(g, recv, c_idx, name):
    n, r, cdim = g.shape
    h = r // 2
    tr = _row_tile(h, 256, 8)
    nh = h // tr

    def body(c_ref, g_ref, r_ref, o_ref):
        o_ref[...] = (g_ref[...] + r_ref[...]).astype(BF16)

    return pl.pallas_call(
        body,
        grid_spec=pltpu.PrefetchScalarGridSpec(
            num_scalar_prefetch=1, grid=(n, nh),
            in_specs=[pl.BlockSpec((None, tr, cdim), lambda s, i, c: (s, c[0] * nh + i, 0)),
                      pl.BlockSpec((None, tr, cdim), lambda s, i, c: (s, i, 0))],
            out_specs=pl.BlockSpec((None, tr, cdim), lambda s, i, c: (s, i, 0))),
        out_shape=jax.ShapeDtypeStruct((n, h, cdim), BF16),
        compiler_params=_cparams("parallel", "parallel"), name=name)(c_idx, g, recv)


def _sum_slots(x, name):
    n, r, cdim = x.shape
    tr = _row_tile(r, 256, 8)

    def body(x_ref, o_ref):
        acc = x_ref[0].astype(F32)
        for j in range(1, n):
            acc = acc + x_ref[j].astype(F32)
        o_ref[...] = acc

    return pl.pallas_call(
        body, grid=(r // tr,),
        in_specs=[pl.BlockSpec((n, tr, cdim), lambda i: (0, i, 0))],
        out_specs=pl.BlockSpec((tr, cdim), lambda i: (i, 0)),
        out_shape=jax.ShapeDtypeStruct((r, cdim), F32),
        compiler_params=_cparams("parallel"), name=name)(x)


def _sum_own_and_slots(own, slots, chip_idx, name):
    n, r, cdim = own.shape
    tr = _row_tile(r, 256, 8)

    def body(s_ref, own_ref, a_ref, b_ref, c_ref, o_ref):
        o_ref[...] = (own_ref[...].astype(F32) + a_ref[...].astype(F32) + b_ref[...].astype(F32)
                      + c_ref[...].astype(F32))

    def slot(dd):
        return pl.BlockSpec((None, tr, cdim), lambda i, s: ((s[0] + dd) % n, i, 0))

    return pl.pallas_call(
        body,
        grid_spec=pltpu.PrefetchScalarGridSpec(
            num_scalar_prefetch=1, grid=(r // tr,), in_specs=[slot(0), slot(1), slot(2), slot(3)],
            out_specs=pl.BlockSpec((tr, cdim), lambda i, s: (i, 0))),
        out_shape=jax.ShapeDtypeStruct((r, cdim), F32),
        compiler_params=_cparams("parallel"), name=name)(chip_idx, own, slots, slots, slots)


def _add2(a, b, name):
    r, cdim = a.shape
    tr = _row_tile(r, 256, 8)

    def body(a_ref, b_ref, o_ref):
        o_ref[...] = a_ref[...] + b_ref[...]

    spec = pl.BlockSpec((tr, cdim), lambda i: (i, 0))
    return pl.pallas_call(body, grid=(r // tr,), in_specs=[spec, spec], out_specs=spec,
                          out_shape=jax.ShapeDtypeStruct((r, cdim), F32),
                          compiler_params=_cparams("parallel"), name=name)(a, b)


def _adamw(w, g, m, v, name):
    r, cdim = w.shape
    tr = _row_tile(r, 256, 8)

    def body(w_ref, g_ref, m_ref, v_ref, go_ref, d_ref, mo_ref, vo_ref):
        gv = g_ref[...]
        go_ref[...] = gv
        mn = ADAM_B1 * m_ref[...] + (1.0 - ADAM_B1) * gv
        vn = ADAM_B2 * v_ref[...] + (1.0 - ADAM_B2) * (gv * gv)
        m_hat = mn / (1.0 - ADAM_B1 ** ADAM_STEP)
        v_hat = vn / (1.0 - ADAM_B2 ** ADAM_STEP)
        d_ref[...] = -ADAM_LR * (m_hat / (jnp.sqrt(v_hat) + ADAM_EPS) + ADAM_WD * w_ref[...])
        mo_ref[...] = mn
        vo_ref[...] = vn

    spec = pl.BlockSpec((tr, cdim), lambda i: (i, 0))
    shp = jax.ShapeDtypeStruct((r, cdim), F32)
    return pl.pallas_call(body, grid=(r // tr,), in_specs=[spec] * 4, out_specs=[spec] * 4,
                          out_shape=[shp] * 4, compiler_params=_cparams("parallel"), name=name)(w, g, m, v)


def _adamw_many(ws, gs, ms, vs):
    n = len(ws)

    def body(*refs):
        for i in range(n):
            w_ref, g_ref, m_ref, v_ref = refs[i], refs[n + i], refs[2 * n + i], refs[3 * n + i]
            d_ref, mo_ref, vo_ref = refs[4 * n + i], refs[5 * n + i], refs[6 * n + i]
            gv = g_ref[...]
            mn = ADAM_B1 * m_ref[...] + (1.0 - ADAM_B1) * gv
            vn = ADAM_B2 * v_ref[...] + (1.0 - ADAM_B2) * (gv * gv)
            m_hat = mn / (1.0 - ADAM_B1 ** ADAM_STEP)
            v_hat = vn / (1.0 - ADAM_B2 ** ADAM_STEP)
            d_ref[...] = -ADAM_LR * (m_hat / (jnp.sqrt(v_hat) + ADAM_EPS) + ADAM_WD * w_ref[...])
            mo_ref[...] = mn
            vo_ref[...] = vn

    shapes = [jax.ShapeDtypeStruct(w.shape, F32) for w in ws]
    outs = pl.pallas_call(body, out_shape=shapes * 3, name="adamw_small")(*ws, *gs, *ms, *vs)
    return outs[:n], outs[n:2 * n], outs[2 * n:]


def _split_rows(a):
    return a.reshape(a.shape[0], 2, a.shape[1] // 2, a.shape[2])


def _place():
    x, y, c = lax.axis_index("x"), lax.axis_index("y"), lax.axis_index("c")
    chips = [(1 - x, y), (x, 1 - y), (1 - x, 1 - y)]
    return x, y, c, chips


def _remote(src, dst, send_sem, recv_sem, to):
    return pltpu.make_async_remote_copy(src_ref=src, dst_ref=dst, send_sem=send_sem, recv_sem=recv_sem,
                                        device_id=to, device_id_type=MESH)


def _plan_gather(n_halved):
    def plan(src_refs, land_refs):
        x, y, c, chips = _place()
        me = 2 * x + y
        copies = []
        for k, (src, land) in enumerate(zip(src_refs, land_refs)):
            for (px, py) in chips:
                frm = 2 * px + py
                if k < n_halved:
                    copies.append((src.at[c], land.at[me, c], (px, py, c), land.at[frm, c]))
                else:
                    copies.append((src, land.at[me], (px, py, c), land.at[frm]))
        return copies
    return plan


def _plan_share(src_refs, land_refs):
    x, y, c, chips = _place()
    me = 2 * x + y
    sib = (x, y, 1 - c)
    copies = []
    for src, land in zip(src_refs, land_refs):
        copies.append((src, land.at[me], sib, land.at[me]))
        for (px, py) in chips:
            frm = 2 * px + py
            copies.append((land.at[frm, c], land.at[frm, c], sib, land.at[frm, 1 - c]))
    return copies


def _plan_scatter(n_parts):
    def plan(src_refs, land_refs):
        x, y, c, chips = _place()
        me = 2 * x + y
        copies = []
        for k, (src, land) in enumerate(zip(src_refs, land_refs)):
            for (px, py) in chips:
                to = 2 * px + py
                copies.append((src.at[to] if k < n_parts else src, land.at[me], (px, py, c), land.at[to]))
        return copies
    return plan


def _plan_exchange(n_split):
    def plan(src_refs, land_refs):
        x, y, c, _ = _place()
        sib = (x, y, 1 - c)
        return [(src.at[:, 1 - c] if k < n_split else src, land, sib, land)
                for k, (src, land) in enumerate(zip(src_refs, land_refs))]
    return plan


def _hbm(a):
    return pltpu.HBM(a.shape, a.dtype)


def _start_copies(name, srcs, lands, plan, ncopy, dep=None):
    ns, nl = len(srcs), len(lands)
    nin = ns + nl + (0 if dep is None else 1)

    def body(*refs):
        send_sems, recv_sems, token = refs[nin], refs[nin + 1], refs[-1]
        for k, (src, dst, dev, _) in enumerate(plan(refs[:ns], refs[ns:ns + nl])):
            _remote(src, dst, send_sems.at[k], recv_sems.at[k], dev).start()
        token[...] = jnp.zeros_like(token)

    args = [pltpu.with_memory_space_constraint(a, pltpu.HBM) for a in list(srcs) + list(lands)]
    outs = pl.pallas_call(
        body, name=name,
        out_shape=(pltpu.SemaphoreType.DMA((ncopy,)), pltpu.SemaphoreType.DMA((ncopy,)),
                   *[_hbm(a) for a in list(srcs) + list(lands)], jax.ShapeDtypeStruct((8, 128), F32)),
        in_specs=[HBM_SPEC] * (ns + nl) + ([] if dep is None else [ANY_SPEC]),
        out_specs=(SEM_SPEC, SEM_SPEC, *([HBM_SPEC] * (ns + nl)), pl.BlockSpec(memory_space=pltpu.VMEM)),
        input_output_aliases={i: 2 + i for i in range(ns + nl)},
        compiler_params=pltpu.CompilerParams(has_side_effects=SIDE_EFFECT),
    )(*args, *([] if dep is None else [dep]))
    return outs[0], outs[1], list(outs[2:2 + ns]), list(outs[2 + ns:2 + ns + nl]), outs[-1]


def _wait_copies(name, started, plan, after):
    send_sems, recv_sems, srcs, lands, _ = started
    ns, nl = len(srcs), len(lands)
    after = list(after) if isinstance(after, (list, tuple)) else [after]

    def body(*refs):
        send_ref, recv_ref = refs[ns + nl], refs[ns + nl + 1]
        for k, (src, _, dev, mine) in enumerate(plan(refs[:ns], refs[ns:ns + nl])):
            copy = _remote(src, mine, send_ref.at[k], recv_ref.at[k], dev)
            copy.wait_send()
            copy.wait_recv()

    outs = pl.pallas_call(
        body, name=name, out_shape=tuple(_hbm(a) for a in srcs + lands),
        in_specs=[HBM_SPEC] * (ns + nl) + [SEM_SPEC, SEM_SPEC] + [ANY_SPEC] * len(after),
        out_specs=tuple([HBM_SPEC] * (ns + nl)),
        input_output_aliases={i: i for i in range(ns + nl)},
        compiler_params=pltpu.CompilerParams(has_side_effects=SIDE_EFFECT),
    )(*srcs, *lands, send_sems, recv_sems, *after)
    return list(outs[:ns]), list(outs[ns:])


def _share_with_sibling(name, srcs, lands):
    n = len(srcs)

    def body(*refs):
        src_refs, land_refs, out_refs = refs[:n], refs[n:2 * n], refs[2 * n:3 * n]
        send_sem, recv_sem = refs[3 * n:]
        x, y, c, chips = _place()
        me = 2 * x + y
        sib = (x, y, 1 - c)
        sends, recvs = [], []
        for k in range(n):
            sems = (send_sem.at[4 * k], recv_sem.at[4 * k])
            sends.append(_remote(src_refs[k], out_refs[k].at[me], *sems, sib))
            recvs.append(_remote(src_refs[k], out_refs[k].at[me], *sems, sib))
            for j, (px, py) in enumerate(chips):
                frm = 2 * px + py
                sems = (send_sem.at[4 * k + 1 + j], recv_sem.at[4 * k + 1 + j])
                sends.append(_remote(land_refs[k].at[frm, c], out_refs[k].at[frm, c], *sems, sib))
                recvs.append(_remote(land_refs[k].at[frm, c], out_refs[k].at[frm, 1 - c], *sems, sib))
        for cp in sends:
            cp.start()
        for cp in recvs:
            cp.wait_recv()
        for cp in sends:
            cp.wait_send()

    return pl.pallas_call(
        body, name=name, in_specs=[HBM_SPEC] * (2 * n), out_specs=[HBM_SPEC] * n,
        out_shape=[jax.ShapeDtypeStruct(a.shape, a.dtype) for a in lands],
        input_output_aliases={n + k: k for k in range(n)},
        scratch_shapes=[pltpu.SemaphoreType.DMA((4 * n,)), pltpu.SemaphoreType.DMA((4 * n,))],
    )(*srcs, *lands)


def _exchange_halves(grads, small, name):
    ng = len(grads)
    grads = [_split_rows(g) for g in grads]
    extra = [] if small is None else [small]
    nall = ng + len(extra)

    def body(*refs):
        ins, outs = refs[:nall], refs[nall:2 * nall]
        send_sem, recv_sem = refs[2 * nall:]
        x, y, c, _ = _place()
        sib = (x, y, 1 - c)
        copies = []
        for k in range(nall):
            src = ins[k].at[:, 1 - c] if k < ng else ins[k]
            copies.append(_remote(src, outs[k], send_sem.at[k], recv_sem.at[k], sib))
        for cp in copies:
            cp.start()
        for cp in copies:
            cp.wait_recv()
        for cp in copies:
            cp.wait_send()

    out_shape = [jax.ShapeDtypeStruct((g.shape[0], g.shape[2], g.shape[3]), g.dtype) for g in grads]
    out_shape += [jax.ShapeDtypeStruct(s.shape, s.dtype) for s in extra]
    return pl.pallas_call(
        body, in_specs=[HBM_SPEC] * nall, out_specs=[HBM_SPEC] * nall, out_shape=out_shape,
        scratch_shapes=[pltpu.SemaphoreType.DMA((nall,)), pltpu.SemaphoreType.DMA((nall,))],
        name=name)(*grads, *extra)


def _join_halves(reduced, dests, out_shapes, name):
    nr = len(reduced)

    def body(*refs):
        r_in = refs[:nr]
        outs = refs[nr:nr + len(out_shapes)]
        send_sem, recv_sem, back_send, back_recv = refs[nr + len(out_shapes):]
        x, y, c, _ = _place()
        sib = (x, y, 1 - c)
        sends, backs = [], []
        for k in range(nr):
            oi, layer = dests[k]
            sends.append(_remote(r_in[k], outs[oi].at[layer, c], send_sem.at[k], recv_sem.at[k], sib))
        for cp in sends:
            cp.start()
        for k in range(nr):
            oi, layer = dests[k]
            theirs = outs[oi].at[layer, 1 - c]
            _remote(r_in[k], theirs, send_sem.at[k], recv_sem.at[k], sib).wait_recv()
            back = _remote(theirs, theirs, back_send.at[k], back_recv.at[k], sib)
            back.start()
            backs.append(back)
        for k in range(nr):
            oi, layer = dests[k]
            mine = outs[oi].at[layer, c]
            _remote(mine, mine, back_send.at[k], back_recv.at[k], sib).wait_recv()
        for cp in sends + backs:
            cp.wait_send()

    split = [(s[0], 2, s[1] // 2, s[2]) for s in out_shapes]
    outs = pl.pallas_call(
        body, in_specs=[HBM_SPEC] * nr, out_specs=[HBM_SPEC] * len(out_shapes),
        out_shape=[jax.ShapeDtypeStruct(s, F32) for s in split],
        scratch_shapes=[pltpu.SemaphoreType.DMA((nr,)), pltpu.SemaphoreType.DMA((nr,)),
                        pltpu.SemaphoreType.DMA((nr,)), pltpu.SemaphoreType.DMA((nr,))],
        name=name)(*reduced)
    return [o.reshape(s) for o, s in zip(outs, out_shapes)]


def _pack(arrs):
    flat = jnp.concatenate([a.reshape(-1).astype(F32) for a in arrs])
    n = flat.shape[0]
    rows = -(-n // PACK_WIDTH)
    rows = -(-rows // 8) * 8
    return jnp.pad(flat, (0, rows * PACK_WIDTH - n)).reshape(rows, PACK_WIDTH)


def _unpack(buf, shapes):
    flat = buf.reshape(-1)
    out, off = [], 0
    for shp in shapes:
        n = 1
        for s in shp:
            n *= s
        out.append(flat[off:off + n].reshape(shp))
        off += n
    return out


def _unshard_cols(stacked):
    moved = jnp.moveaxis(stacked, 0, -2)
    return moved.reshape(moved.shape[:-2] + (moved.shape[-2] * moved.shape[-1],))


def _take_cols(blocks, start, width):
    bw = blocks.shape[2]
    pieces, lo = [], start
    while lo < start + width:
        b = lo // bw
        hi = min(start + width, (b + 1) * bw)
        pieces.append(blocks[b][:, lo - b * bw:hi - b * bw])
        lo = hi
    return jnp.concatenate(pieces, axis=1)


def _col_shard(full, s, width):
    return lax.dynamic_slice_in_dim(full, s * width, width, axis=full.ndim - 1)


def kernel(x, meta_tokens, mix_norm_g, ffn_norm_g, ffn_w1, ffn_w2, cp_w_in, cp_conv_w, cp_conv_b, cp_ln_g, cp_ln_b, cp_pool_w, cp_pool_scale, cp_w_out, gla_w_in, gla_gate_w2, gla_gate_b, gla_head_g, gla_w_out, final_norm_g, loss_target, m_meta_tokens, m_mix_norm_g, m_ffn_norm_g, m_ffn_w1, m_ffn_w2, m_cp_w_in, m_cp_conv_w, m_cp_conv_b, m_cp_ln_g, m_cp_ln_b, m_cp_pool_w, m_cp_pool_scale, m_cp_w_out, m_gla_w_in, m_gla_gate_w2, m_gla_gate_b, m_gla_head_g, m_gla_w_out, m_final_norm_g, v_meta_tokens, v_mix_norm_g, v_ffn_norm_g, v_ffn_w1, v_ffn_w2, v_cp_w_in, v_cp_conv_w, v_cp_conv_b, v_cp_ln_g, v_cp_ln_b, v_cp_pool_w, v_cp_pool_scale, v_cp_w_out, v_gla_w_in, v_gla_gate_w2, v_gla_gate_b, v_gla_head_g, v_gla_w_out, v_final_norm_g):
    d = D_MODEL
    chip = 2 * lax.axis_index("x") + lax.axis_index("y")
    core = lax.axis_index("c")
    seq = x.shape[1]
    t = seq + CHUNK

    sharded_small = [meta_tokens, cp_conv_w, gla_gate_w2, gla_gate_b, gla_head_g]

    def halves(w):
        return w.astype(BF16).reshape(2, w.shape[0] // 2, w.shape[1])

    def unhalve(g):
        return g.reshape(N_CHIPS, 2 * g.shape[2], g.shape[3])

    def start_gather(name, srcs, dep, whole=()):
        lands = [lax.empty((N_CHIPS,) + s.shape, s.dtype) for s in srcs]
        for a in whole:
            lands.append(lax.dynamic_update_slice(jnp.zeros((N_CHIPS,) + a.shape, a.dtype), a[None], (chip,) + (0,) * a.ndim))
        plan = _plan_gather(len(srcs))
        return _start_copies(name, list(srcs) + list(whole), lands, plan, 3 * len(lands), dep), plan, len(srcs)

    def arrived(name, gather, after):
        started, plan, n = gather
        srcs, lands = _wait_copies(name + "_wait", started, plan, after)
        return srcs[:n], lands[:n], lands[n:]

    cp_gather = start_gather("gather_cp_start", [halves(cp_w_in[0]), halves(cp_w_out[0])], None, [_pack(sharded_small)])
    ffn0_gather = start_gather("gather_ffn0_start", [halves(ffn_w1[0]), halves(ffn_w2[0])], cp_gather[0][-1])
    gla_gather = start_gather("gather_gla_start", [halves(gla_w_in[0]), halves(gla_w_out[0])], ffn0_gather[0][-1])
    ffn1_gather = start_gather("gather_ffn1_start", [halves(ffn_w1[1]), halves(ffn_w2[1])], gla_gather[0][-1])
    cp_srcs, cp_lands, (small_g,) = arrived("gather_cp", cp_gather, ffn1_gather[0][-1])
    cpin_g, cpout_g = [unhalve(g) for g in _share_with_sibling("gather_cp_share", cp_srcs, cp_lands)]
    per_chip = [_unpack(small_g[j], [a.shape for a in sharded_small]) for j in range(N_CHIPS)]
    meta_f, conv_w_f, gate_w_f, gate_b_f, head_g_f = [
        jnp.concatenate([per_chip[j][i] for j in range(N_CHIPS)], axis=-1) for i in range(len(sharded_small))]
    conv_w_f, gate_w_f = conv_w_f[0], gate_w_f[0]
    w_cp_in = _unshard_cols(cpin_g)
    w_cp_out = cpout_g.reshape(CONV_DIM + POOL_DIM, d)
    gate_w_pad = jnp.pad(gate_w_f, ((0, GATE_PAD - GATE_RANK), (0, 0))).astype(BF16)
    row = lambda a: a.reshape(1, -1)
    c_idx = core.reshape(1).astype(jnp.int32)
    chip_idx = chip.reshape(1).astype(jnp.int32)

    h0 = jnp.concatenate([jnp.zeros((PAD_ROWS, d), F32), meta_f, x[0]], axis=0)
    z0, u0 = _norm_matmul(h0, row(mix_norm_g[0]), w_cp_in, 512, "cp_in_proj")
    c0, pm0, mix0 = _cp_seq_fwd(z0, conv_w_f, cp_conv_b, cp_ln_g, cp_ln_b, cp_pool_w[0], cp_pool_scale)
    ffn0_srcs, ffn0_lands, _ = arrived("gather_ffn0", ffn0_gather, mix0)
    ffn0_share = _start_copies("gather_ffn0_share_start", ffn0_srcs, ffn0_lands, _plan_share, 4 * len(ffn0_srcs))
    h1 = _matmul_residual(mix0, w_cp_out, h0, "cp_out_proj", dep=ffn0_share[-1])
    w1g0, w2g0 = [unhalve(g) for g in _wait_copies("gather_ffn0_share_wait", ffn0_share, _plan_share, h1)[1]]
    h2, hp0, uf0 = _ffn_fwd(h1, row(ffn_norm_g[0]), w1g0, w2g0, "ffn0_fwd")
    gla_srcs, gla_lands, _ = arrived("gather_gla", gla_gather, h2)
    glain_g, glaout_g = [unhalve(g) for g in _share_with_sibling("gather_gla_share", gla_srcs, gla_lands)]
    w_gla_in = jnp.concatenate([glain_g[j] for j in range(N_CHIPS)] + [jnp.zeros((d, GLA_IN_PAD - GLA_IN), BF16)], axis=1)
    w_gla_out = glaout_g.reshape(GLA_DV, d)
    z1, u2 = _norm_matmul(h2, row(mix_norm_g[1]), w_gla_in, 640, "gla_in_proj")
    ffn1_srcs, ffn1_lands, _ = arrived("gather_ffn1", ffn1_gather, z1)
    ffn1_share = _start_copies("gather_ffn1_share_start", ffn1_srcs, ffn1_lands, _plan_share, 4 * len(ffn1_srcs))
    o1, mix1, states = _gla_seq_fwd(z1, gate_w_pad, gate_b_f, head_g_f, dep=ffn1_share[-1])
    h3 = _matmul_residual(mix1, w_gla_out, h2, "gla_out_proj")
    w1g1, w2g1 = [unhalve(g) for g in _wait_copies("gather_ffn1_share_wait", ffn1_share, _plan_share, h3)[1]]
    h4, hp1, uf1 = _ffn_fwd(h3, row(ffn_norm_g[1]), w1g1, w2g1, "ffn1_fwd")

    def start_exchange(name, grads):
        srcs = [_split_rows(g) for g in grads]
        lands = [lax.empty((g.shape[0], g.shape[1] // 2, g.shape[2]), g.dtype) for g in grads]
        return _start_copies(name + "_exchange_start", srcs, lands, _plan_exchange(len(grads)), len(grads))

    def start_scatter(name, exchange, after):
        srcs, recv = _wait_copies(name + "_exchange_wait", exchange, _plan_exchange(len(exchange[2])), after)
        parts = [_sum_halves(g.reshape(g.shape[0], -1, g.shape[3]), r, c_idx, "%s_chip_sum_%d" % (name, k))
                 for k, (g, r) in enumerate(zip(srcs, recv))]
        lands = [lax.empty(p.shape, p.dtype) for p in parts]
        return _start_copies(name + "_scatter_start", parts, lands, _plan_scatter(len(parts)), 3 * len(parts))

    def finish_reduce(name, started, after):
        n = len(started[2])
        parts, lands = _wait_copies(name + "_scatter_wait", started, _plan_scatter(n), after)
        return [_sum_own_and_slots(p, s, chip_idx, "%s_slot_sum_%d" % (name, k)) for k, (p, s) in enumerate(zip(parts, lands))]

    dh4, d_final_g, loss_part = _loss_bwd(h4, row(final_norm_g), loss_target[0])

    dh3, dhp1, d_ffn_g1 = _ffn_bwd_data(dh4, h3, row(ffn_norm_g[1]), hp1, w1g1, w2g1, "ffn1_bwd")
    dw1_1 = _wgrad(uf1, dhp1, N_CHIPS, d, d, False, True, False, "ffn1_dw1")
    dw2_1 = _wgrad(hp1, dh4, N_CHIPS, d, d, True, False, True, "ffn1_dw2")
    ffn1_exchange = start_exchange("ffn1", [dw1_1, dw2_1])

    dmix1 = _dgrad(dh3, w_gla_out, "gla_out_dgrad", dep=ffn1_exchange[-1])
    dw_gla_out = _wgrad(mix1, dh3, 1, GLA_DV, d, False, False, False, "gla_out_dw", dep=ffn1_exchange[-1])
    ffn1_reduce = start_scatter("ffn1", ffn1_exchange, [dmix1, dw_gla_out])
    dz1, d_gate_w, d_gate_b, d_head_g = _gla_seq_bwd(dmix1, o1, z1, states, gate_w_pad, gate_b_f, head_g_f,
                                                     dep=ffn1_reduce[-1])
    dh2, d_mix_g1 = _dgrad_norm_bwd(dz1, w_gla_in, h2, row(mix_norm_g[1]), dh3, 640, "gla_in_dgrad")
    dw_gla_in = _wgrad(u2, dz1, GLA_IN_PAD // 640, d, 640, False, True, False, "gla_in_dw")
    gla_in_shards = jnp.stack([_take_cols(dw_gla_in, j * (GLA_IN // N_CHIPS), GLA_IN // N_CHIPS) for j in range(N_CHIPS)])
    gla_exchange = start_exchange("gla", [gla_in_shards, dw_gla_out.reshape(N_CHIPS, -1, d)])

    dh1, dhp0, d_ffn_g0 = _ffn_bwd_data(dh2, h1, row(ffn_norm_g[0]), hp0, w1g0, w2g0, "ffn0_bwd", dep=gla_exchange[-1])
    gla_reduce = start_scatter("gla", gla_exchange, dh1)
    dw1_0 = _wgrad(uf0, dhp0, N_CHIPS, d, d, False, True, False, "ffn0_dw1", dep=gla_reduce[-1])
    dw2_0 = _wgrad(hp0, dh2, N_CHIPS, d, d, True, False, True, "ffn0_dw2")
    ffn0_exchange = start_exchange("ffn0", [dw1_0, dw2_0])

    dmix0 = _dgrad(dh1, w_cp_out, "cp_out_dgrad", dep=ffn0_exchange[-1])
    dw_cp_out = _wgrad(mix0, dh1, 1, CONV_DIM + POOL_DIM, d, False, False, False, "cp_out_dw", dep=ffn0_exchange[-1])
    ffn0_reduce = start_scatter("ffn0", ffn0_exchange, [dmix0, dw_cp_out])
    dz0, d_conv_w, d_cp_vec, d_pool_w = _cp_seq_bwd(dmix0, z0, c0, pm0, conv_w_f, cp_ln_g, cp_ln_b, cp_pool_w[0],
                                                    cp_pool_scale, dep=ffn0_reduce[-1])
    dh0, d_mix_g0 = _dgrad_norm_bwd(dz0, w_cp_in, h0, row(mix_norm_g[0]), dh1, 512, "cp_in_dgrad")
    dw_cp_in = _wgrad(u0, dz0, N_CHIPS, d, CP_IN // N_CHIPS, False, True, False, "cp_in_dw")

    grad_x = dh0[CHUNK:][None]

    cp_grads = [dw_cp_in, dw_cp_out.reshape(N_CHIPS, -1, d)]
    small_full = [dh0[PAD_ROWS:CHUNK], jnp.concatenate([d_mix_g0, d_mix_g1], axis=0),
                  jnp.concatenate([d_ffn_g0, d_ffn_g1], axis=0), d_conv_w[:CONV_WIDTH][None],
                  d_cp_vec[0:1], d_cp_vec[1:2], d_cp_vec[2:3], d_pool_w[None], d_cp_vec[3:4],
                  d_gate_w[:GATE_RANK][None], d_gate_b, d_head_g, d_final_g[0], loss_part[0, 0:1]]
    small_mine = _pack(small_full)
    recv = _exchange_halves(cp_grads, small_mine, "cp_exchange")
    chip_sums = [_sum_halves(g, r, c_idx, "cp_chip_sum_%d" % k) for k, (g, r) in enumerate(zip(cp_grads, recv[:-1]))]
    small_chip = _add2(small_mine, recv[-1], "chip_sum_small")
    small_slots = lax.dynamic_update_slice(jnp.zeros((N_CHIPS,) + small_chip.shape, F32), small_chip[None], (chip, 0, 0))
    cp_lands = [lax.empty(p.shape, p.dtype) for p in chip_sums] + [small_slots]
    cp_reduce = _start_copies("cp_scatter_start", chip_sums + [small_chip], cp_lands, _plan_scatter(len(chip_sums)),
                              3 * (len(chip_sums) + 1))

    def adamw_big(names, grads):
        outs = {}
        for n, g in zip(names, grads):
            w, m, v = big[n]
            two_d = lambda a: a.reshape(-1, a.shape[-1])
            res = _adamw(two_d(w), two_d(g), two_d(m), two_d(v), "adamw_" + n)
            outs[n] = [o.reshape(w.shape) for o in res]
        return outs

    big = {"w1": (ffn_w1, m_ffn_w1, v_ffn_w1), "w2": (ffn_w2, m_ffn_w2, v_ffn_w2),
           "cp_in": (cp_w_in, m_cp_w_in, v_cp_w_in), "cp_out": (cp_w_out, m_cp_w_out, v_cp_w_out),
           "gla_in": (gla_w_in, m_gla_w_in, v_gla_w_in), "gla_out": (gla_w_out, m_gla_w_out, v_gla_w_out)}
    red_ffn1 = finish_reduce("ffn1", ffn1_reduce, cp_reduce[-1])
    red_gla = finish_reduce("gla", gla_reduce, cp_reduce[-1])
    red_ffn0 = finish_reduce("ffn0", ffn0_reduce, cp_reduce[-1])
    first = ["w1", "w2", "gla_in", "gla_out"]
    first_grads = _join_halves([red_ffn0[0], red_ffn1[0], red_ffn0[1], red_ffn1[1], red_gla[0], red_gla[1]],
                               [(0, 0), (0, 1), (1, 0), (1, 1), (2, 0), (3, 0)], [big[n][0].shape for n in first],
                               "join_halves_ffn_gla")
    big_out = adamw_big(first, first_grads)
    cp_parts, cp_slots = _wait_copies("cp_scatter_wait", cp_reduce, _plan_scatter(len(chip_sums)), big_out["gla_out"][1])
    red_cp = [_sum_own_and_slots(a, s, chip_idx, "cp_slot_sum_%d" % k)
              for k, (a, s) in enumerate(zip(cp_parts[:-1], cp_slots[:-1]))]
    small_red = _sum_slots(cp_slots[-1], "slot_sum_small")
    last = ["cp_in", "cp_out"]
    last_grads = _join_halves(red_cp, [(0, 0), (1, 0)], [big[n][0].shape for n in last], "join_halves_cp")
    big_out.update(adamw_big(last, last_grads))

    (g_meta, g_mix, g_ffn, g_conv_w, g_conv_b, g_ln_g, g_ln_b, g_pool_w, g_pool_scale, g_gate_w, g_gate_b, g_head,
     g_final, loss_sum) = _unpack(small_red, [a.shape for a in small_full])
    g_meta = _col_shard(g_meta, chip, meta_tokens.shape[-1])
    g_conv_w = _col_shard(g_conv_w, chip, cp_conv_w.shape[-1])
    g_gate_w = _col_shard(g_gate_w, chip, gla_gate_w2.shape[-1])
    g_gate_b = _col_shard(g_gate_b, chip, gla_gate_b.shape[-1])
    g_head = _col_shard(g_head, chip, gla_head_g.shape[-1])
    small_w = [meta_tokens, mix_norm_g, ffn_norm_g, cp_conv_w, cp_conv_b, cp_ln_g, cp_ln_b, cp_pool_w, cp_pool_scale,
               gla_gate_w2, gla_gate_b, gla_head_g, final_norm_g]
    small_m = [m_meta_tokens, m_mix_norm_g, m_ffn_norm_g, m_cp_conv_w, m_cp_conv_b, m_cp_ln_g, m_cp_ln_b, m_cp_pool_w,
               m_cp_pool_scale, m_gla_gate_w2, m_gla_gate_b, m_gla_head_g, m_final_norm_g]
    small_v = [v_meta_tokens, v_mix_norm_g, v_ffn_norm_g, v_cp_conv_w, v_cp_conv_b, v_cp_ln_g, v_cp_ln_b, v_cp_pool_w,
               v_cp_pool_scale, v_gla_gate_w2, v_gla_gate_b, v_gla_head_g, v_final_norm_g]
    small_g = [g_meta, g_mix, g_ffn, g_conv_w, g_conv_b, g_ln_g, g_ln_b, g_pool_w, g_pool_scale, g_gate_w, g_gate_b,
               g_head, g_final]
    shapes = [w.shape for w in small_w]
    small_g = [g.reshape(s) for g, s in zip(small_g, shapes)]
    at_least_2d = lambda arrs: [a.reshape(1, -1) if a.ndim == 1 else a for a in arrs]
    s_delta, s_m, s_v = _adamw_many(at_least_2d(small_w), at_least_2d(small_g), at_least_2d(small_m), at_least_2d(small_v))
    s_delta, s_m, s_v = [[a.reshape(s) for a, s in zip(group, shapes)] for group in (s_delta, s_m, s_v)]

    order = ["meta", "mix", "ffn", "w1", "w2", "cp_in", "conv_w", "conv_b", "ln_g", "ln_b", "pool_w", "pool_scale",
             "cp_out", "gla_in", "gate_w", "gate_b", "head", "gla_out", "final"]
    small_names = ["meta", "mix", "ffn", "conv_w", "conv_b", "ln_g", "ln_b", "pool_w", "pool_scale", "gate_w", "gate_b",
                   "head", "final"]
    big_names = ["w1", "w2", "cp_in", "cp_out", "gla_in", "gla_out"]
    table = {n: (small_g[i], s_delta[i], s_m[i], s_v[i]) for i, n in enumerate(small_names)}
    table.update({n: tuple(big_out[n]) for n in big_names})
    loss = loss_sum.reshape(())
    return (loss, grad_x, *[table[n][0] for n in order], *[table[n][1] for n in order],
            *[table[n][2] for n in order], *[table[n][3] for n in order])
```

```python
import functools

import jax
import jax.numpy as jnp
from jax import lax
from jax.experimental import pallas as pl
from jax.experimental.pallas import tpu as pltpu

F32 = jnp.float32
BF16 = jnp.bfloat16

D_MODEL = 1024
N_META = 16
CHUNK = 64
PAD_ROWS = CHUNK - N_META
EPS = 1e-5
CONV_DIM = 512
CONV_WIDTH = 31
CONV_HALO = 32
POOL_DIM = 512
POOL_WINDOWS = (2, 4, 8, 16)
POOL_GROUP = 128
POOL_HALO = 16
CP_IN = 2 * CONV_DIM + POOL_DIM
GLA_HEADS = 4
GLA_DK = 512
GLA_DV = 1024
GLA_HK = GLA_DK // GLA_HEADS
GLA_HV = GLA_DV // GLA_HEADS
GATE_RANK = 16
GATE_PAD = 128
GATE_NORM = 16.0
GLA_IN = 2 * GLA_DK + 2 * GLA_DV + GATE_RANK
GLA_IN_PAD = 2 * GLA_DK + 2 * GLA_DV + GATE_PAD
N_CHIPS = 4
ADAM_LR = 0.001
ADAM_B1 = 0.9
ADAM_B2 = 0.999
ADAM_EPS = 1e-08
ADAM_WD = 0.01
ADAM_STEP = 10

VMEM_LIMIT_BYTES = 56 * 1024 * 1024
ROW_TILE_TARGET = 832
TOKEN_TILE_TARGET = 1040
PACK_WIDTH = 1024
MESH = pl.DeviceIdType.MESH
HBM_SPEC = pl.BlockSpec(memory_space=pltpu.HBM)
ANY_SPEC = pl.BlockSpec(memory_space=pl.ANY)
SEM_SPEC = pl.BlockSpec(memory_space=pltpu.SEMAPHORE)
SIDE_EFFECT = pltpu.SideEffectType.DATAFLOW_SIDE_EFFECTING


def _cparams(*sem):
    return pltpu.CompilerParams(dimension_semantics=sem, vmem_limit_bytes=VMEM_LIMIT_BYTES)


def _row_tile(t, target, mult):
    best = mult
    for cand in range(mult, min(t, target) + 1, mult):
        if t % cand == 0:
            best = cand
    assert t % best == 0, (t, best)
    return best


def _rms(h, g):
    return h * lax.rsqrt(jnp.mean(h * h, axis=-1, keepdims=True) + EPS) * g


def _rms_bwd(h, g, du):
    r = lax.rsqrt(jnp.mean(h * h, axis=-1, keepdims=True) + EPS)
    xhat = h * r
    dxh = du * g
    dh = r * (dxh - xhat * jnp.mean(dxh * xhat, axis=-1, keepdims=True))
    return dh, du * xhat


def _valid_rows(i, tm):
    row = i * tm + lax.broadcasted_iota(jnp.int32, (tm, 1), 0)
    return row >= PAD_ROWS


def _dot(a, b):
    return jnp.dot(a, b, preferred_element_type=F32)


def _dot_nt(a, b):
    return lax.dot_general(a, b, (((1,), (1,)), ((), ())), preferred_element_type=F32)


def _dot_tn(a, b):
    return lax.dot_general(a, b, (((0,), (0,)), ((), ())), preferred_element_type=F32)


def _accumulate(ref, val, first):
    @pl.when(first)
    def _():
        ref[...] = val

    @pl.when(jnp.logical_not(first))
    def _():
        ref[...] += val


def _call_after(dep, body, n_in, in_specs, args, **kw):
    if dep is None:
        return pl.pallas_call(body, in_specs=in_specs, **kw)(*args)

    def with_dep(*refs):
        body(*refs[:n_in], *refs[n_in + 1:])

    return pl.pallas_call(with_dep, in_specs=list(in_specs) + [ANY_SPEC], **kw)(*args, dep)


def _norm_matmul(h, g, w, nc, name, dep=None):
    t, d = h.shape
    n = w.shape[1]
    tm = _row_tile(t, TOKEN_TILE_TARGET, 16)

    def body(h_ref, g_ref, w_ref, z_ref, u_ref):
        u = _rms(h_ref[...], g_ref[...]).astype(BF16)
        u_ref[...] = u
        for n0 in range(0, n, nc):
            z_ref[:, n0:n0 + nc] = _dot(u, w_ref[:, n0:n0 + nc]).astype(BF16)

    return _call_after(
        dep, body, 3,
        [pl.BlockSpec((tm, d), lambda i: (i, 0)), pl.BlockSpec((1, d), lambda i: (0, 0)),
         pl.BlockSpec((d, n), lambda i: (0, 0))], (h, g, w), grid=(t // tm,),
        out_specs=[pl.BlockSpec((tm, n), lambda i: (i, 0)), pl.BlockSpec((tm, d), lambda i: (i, 0))],
        out_shape=[jax.ShapeDtypeStruct((t, n), BF16), jax.ShapeDtypeStruct((t, d), BF16)],
        compiler_params=_cparams("parallel"), name=name)


def _matmul_residual(a, w, h, name, dep=None):
    t, k = a.shape
    d = w.shape[1]
    tm = _row_tile(t, TOKEN_TILE_TARGET, 16)

    def body(a_ref, w_ref, h_ref, o_ref):
        o_ref[...] = h_ref[...] + _dot(a_ref[...], w_ref[...])

    return _call_after(
        dep, body, 3,
        [pl.BlockSpec((tm, k), lambda i: (i, 0)), pl.BlockSpec((k, d), lambda i: (0, 0)),
         pl.BlockSpec((tm, d), lambda i: (i, 0))], (a, w, h), grid=(t // tm,),
        out_specs=pl.BlockSpec((tm, d), lambda i: (i, 0)),
        out_shape=jax.ShapeDtypeStruct((t, d), F32),
        compiler_params=_cparams("parallel"), name=name)


def _ffn_fwd(h, g, w1g, w2g, name):
    t, d = h.shape
    ns, ffs = w1g.shape[0], w1g.shape[2]
    tm = _row_tile(t, TOKEN_TILE_TARGET, 16)

    def body(h_ref, g_ref, w1_ref, w2_ref, ho_ref, hp_ref, u_ref, acc_ref):
        s = pl.program_id(1)

        @pl.when(s == 0)
        def _():
            u_ref[...] = _rms(h_ref[...], g_ref[...]).astype(BF16)

        hp = _dot(u_ref[...], w1_ref[...])
        hp_ref[...] = hp.astype(BF16)
        a = jnp.maximum(hp, 0.0)
        _accumulate(acc_ref, _dot((a * a).astype(BF16), w2_ref[...]), s == 0)

        @pl.when(s == ns - 1)
        def _():
            ho_ref[...] = h_ref[...] + acc_ref[...]

    return pl.pallas_call(
        body, grid=(t // tm, ns),
        in_specs=[pl.BlockSpec((tm, d), lambda i, s: (i, 0)), pl.BlockSpec((1, d), lambda i, s: (0, 0)),
                  pl.BlockSpec((None, d, ffs), lambda i, s: (s, 0, 0)),
                  pl.BlockSpec((None, ffs, d), lambda i, s: (s, 0, 0))],
        out_specs=[pl.BlockSpec((tm, d), lambda i, s: (i, 0)), pl.BlockSpec((tm, ffs), lambda i, s: (i, s)),
                   pl.BlockSpec((tm, d), lambda i, s: (i, 0))],
        out_shape=[jax.ShapeDtypeStruct((t, d), F32), jax.ShapeDtypeStruct((t, ns * ffs), BF16),
                   jax.ShapeDtypeStruct((t, d), BF16)],
        scratch_shapes=[pltpu.VMEM((tm, d), F32)],
        compiler_params=_cparams("parallel", "arbitrary"), name=name)(h, g, w1g, w2g)


def _ffn_bwd_data(dh, h, g, hp, w1g, w2g, name, dep=None):
    t, d = h.shape
    ns, ffs = w1g.shape[0], w1g.shape[2]
    tm = _row_tile(t, ROW_TILE_TARGET, CHUNK)

    def body(dh_ref, h_ref, g_ref, hp_ref, w1_ref, w2_ref, dhi_ref, dhp_ref, dg_ref, acc_ref):
        i, s = pl.program_id(0), pl.program_id(1)
        da = _dot_nt(dh_ref[...].astype(BF16), w2_ref[...])
        dhp = (da * (2.0 * jnp.maximum(hp_ref[...].astype(F32), 0.0))).astype(BF16)
        dhp_ref[...] = dhp
        _accumulate(acc_ref, _dot_nt(dhp, w1_ref[...]), s == 0)

        @pl.when(s == ns - 1)
        def _():
            dhn, dgr = _rms_bwd(h_ref[...], g_ref[...], acc_ref[...])
            dhi_ref[...] = jnp.where(_valid_rows(i, tm), dh_ref[...] + dhn, 0.0)
            _accumulate(dg_ref, jnp.sum(dgr, axis=0, keepdims=True), i == 0)

    return _call_after(
        dep, body, 6,
        [pl.BlockSpec((tm, d), lambda i, s: (i, 0)), pl.BlockSpec((tm, d), lambda i, s: (i, 0)),
         pl.BlockSpec((1, d), lambda i, s: (0, 0)), pl.BlockSpec((tm, ffs), lambda i, s: (i, s)),
         pl.BlockSpec((None, d, ffs), lambda i, s: (s, 0, 0)),
         pl.BlockSpec((None, ffs, d), lambda i, s: (s, 0, 0))], (dh, h, g, hp, w1g, w2g), grid=(t // tm, ns),
        out_specs=[pl.BlockSpec((tm, d), lambda i, s: (i, 0)), pl.BlockSpec((tm, ffs), lambda i, s: (i, s)),
                   pl.BlockSpec((1, d), lambda i, s: (0, 0))],
        out_shape=[jax.ShapeDtypeStruct((t, d), F32), jax.ShapeDtypeStruct((t, ns * ffs), BF16),
                   jax.ShapeDtypeStruct((1, d), F32)],
        scratch_shapes=[pltpu.VMEM((tm, d), F32)],
        compiler_params=_cparams("arbitrary", "arbitrary"), name=name)


WGRAD_ROWS = 1024


def _wgrad(x, dy, nb, xc, yc, x_by_block, dy_by_block, relu2, name, dep=None):
    t = x.shape[0]
    tk = _row_tile(t - CHUNK, WGRAD_ROWS, CHUNK)

    def prep(xv):
        if relu2:
            xv = jnp.maximum(xv.astype(F32), 0.0)
            xv = xv * xv
        return xv.astype(BF16)

    def body(xh_ref, dyh_ref, x_ref, dy_ref, o_ref):
        k = pl.program_id(1)
        p = _dot_tn(prep(x_ref[...]), dy_ref[...].astype(BF16))

        @pl.when(k == 0)
        def _():
            o_ref[...] = p + _dot_tn(prep(xh_ref[...]), dyh_ref[...].astype(BF16))

        @pl.when(k > 0)
        def _():
            o_ref[...] += p

    def head(width, by_block):
        return pl.BlockSpec((CHUNK, width), (lambda b, k: (0, b)) if by_block else (lambda b, k: (0, 0)))

    def rest(width, by_block):
        def index(b, k):
            return pl.multiple_of(CHUNK + k * tk, CHUNK), (pl.multiple_of(b * width, 128) if by_block else 0)
        return pl.BlockSpec((pl.Element(tk), pl.Element(width)), index)

    return _call_after(
        dep, body, 4,
        [head(xc, x_by_block), head(yc, dy_by_block), rest(xc, x_by_block), rest(yc, dy_by_block)], (x, dy, x, dy),
        grid=(nb, (t - CHUNK) // tk),
        out_specs=pl.BlockSpec((None, xc, yc), lambda b, k: (b, 0, 0)),
        out_shape=jax.ShapeDtypeStruct((nb, xc, yc), F32),
        compiler_params=_cparams("parallel", "arbitrary"), name=name)


def _dgrad(dh, w, name, dep=None):
    t, d = dh.shape
    k = w.shape[0]
    tm = _row_tile(t, TOKEN_TILE_TARGET, 16)

    def body(dh_ref, w_ref, o_ref):
        o_ref[...] = _dot_nt(dh_ref[...].astype(BF16), w_ref[...]).astype(BF16)

    return _call_after(
        dep, body, 2,
        [pl.BlockSpec((tm, d), lambda i: (i, 0)), pl.BlockSpec((k, d), lambda i: (0, 0))], (dh, w), grid=(t // tm,),
        out_specs=pl.BlockSpec((tm, k), lambda i: (i, 0)),
        out_shape=jax.ShapeDtypeStruct((t, k), BF16),
        compiler_params=_cparams("parallel"), name=name)


def _dgrad_norm_bwd(dz, w, h, g, dh, nc, name):
    t, d = h.shape
    n = w.shape[1]
    tm = _row_tile(t, ROW_TILE_TARGET // 2, 16)

    def body(dz_ref, w_ref, h_ref, g_ref, dh_ref, dhi_ref, dg_ref):
        i = pl.program_id(0)
        du = jnp.zeros((tm, d), F32)
        for n0 in range(0, n, nc):
            du = du + _dot_nt(dz_ref[:, n0:n0 + nc], w_ref[:, n0:n0 + nc])
        dhn, dgr = _rms_bwd(h_ref[...], g_ref[...], du)
        dhi_ref[...] = jnp.where(_valid_rows(i, tm), dh_ref[...] + dhn, 0.0)
        _accumulate(dg_ref, jnp.sum(dgr, axis=0, keepdims=True), i == 0)

    return pl.pallas_call(
        body, grid=(t // tm,),
        in_specs=[pl.BlockSpec((tm, n), lambda i: (i, 0)), pl.BlockSpec((d, n), lambda i: (0, 0)),
                  pl.BlockSpec((tm, d), lambda i: (i, 0)), pl.BlockSpec((1, d), lambda i: (0, 0)),
                  pl.BlockSpec((tm, d), lambda i: (i, 0))],
        out_specs=[pl.BlockSpec((tm, d), lambda i: (i, 0)), pl.BlockSpec((1, d), lambda i: (0, 0))],
        out_shape=[jax.ShapeDtypeStruct((t, d), F32), jax.ShapeDtypeStruct((1, d), F32)],
        compiler_params=_cparams("arbitrary"), name=name)(dz, w, h, g, dh)


def _dgrad_norm_bwd_input(dz, w, h, g, dh, nc, name):
    t, d = h.shape
    n = w.shape[1]
    tl = _row_tile(t - CHUNK, 512, CHUNK)

    def grads(dz_ref, w_ref, h_ref, g_ref, dh_ref, rows):
        du = jnp.zeros((rows, d), F32)
        for n0 in range(0, n, nc):
            du = du + _dot_nt(dz_ref[:, n0:n0 + nc], w_ref[:, n0:n0 + nc])
        dhn, dgr = _rms_bwd(h_ref[...], g_ref[...], du)
        return dh_ref[...] + dhn, jnp.sum(dgr, axis=0, keepdims=True)

    def rest_body(dz_ref, w_ref, h_ref, g_ref, dh_ref, dg_head_ref, dx_ref, dg_ref):
        dx, dg = grads(dz_ref, w_ref, h_ref, g_ref, dh_ref, tl)
        dx_ref[...] = dx

        @pl.when(pl.program_id(0) == 0)
        def _():
            dg_ref[...] = dg_head_ref[...] + dg

        @pl.when(pl.program_id(0) > 0)
        def _():
            dg_ref[...] += dg

    def head_body(dz_ref, w_ref, h_ref, g_ref, dh_ref, dx_ref, dg_ref):
        dx, dg = grads(dz_ref, w_ref, h_ref, g_ref, dh_ref, CHUNK)
        dx_ref[...] = jnp.where(_valid_rows(0, CHUNK), dx, 0.0)
        dg_ref[...] = dg

    def shifted(width):
        return pl.BlockSpec((pl.Element(tl), pl.Element(width)), lambda i: (pl.multiple_of(CHUNK + i * tl, CHUNK), 0))

    whole = [pl.BlockSpec((d, n), lambda i: (0, 0)), pl.BlockSpec((1, d), lambda i: (0, 0))]
    head = lambda width: pl.BlockSpec((CHUNK, width), lambda i: (0, 0))
    dh_head, dg_head = pl.pallas_call(
        head_body, grid=(1,), in_specs=[head(n), whole[0], head(d), whole[1], head(d)],
        out_specs=[head(d), whole[1]],
        out_shape=[jax.ShapeDtypeStruct((CHUNK, d), F32), jax.ShapeDtypeStruct((1, d), F32)],
        compiler_params=_cparams("arbitrary"), name=name + "_head")(dz, w, h, g, dh)
    dx, dg = pl.pallas_call(
        rest_body, grid=((t - CHUNK) // tl,),
        in_specs=[shifted(n), whole[0], shifted(d), whole[1], shifted(d), whole[1]],
        out_specs=[pl.BlockSpec((tl, d), lambda i: (i, 0)), whole[1]],
        out_shape=[jax.ShapeDtypeStruct((t - CHUNK, d), F32), jax.ShapeDtypeStruct((1, d), F32)],
        compiler_params=_cparams("arbitrary"), name=name)(dz, w, h, g, dh, dg_head)
    return dx, dh_head, dg


def _loss_bwd(h, g, target):
    t, d = h.shape
    tl = _row_tile(t - CHUNK, 1024, CHUNK)

    def body(h_ref, g_ref, t_ref, dh_ref, dg_ref, loss_ref):
        i = pl.program_id(0)
        hv, gv = h_ref[...], g_ref[...]
        err = _rms(hv, gv) - t_ref[...]
        part = 0.5 * jnp.sum(jnp.mean(err * err, axis=-1, keepdims=True), axis=0, keepdims=True)
        dhn, dgr = _rms_bwd(hv, gv, err * (1.0 / d))
        dh_ref[...] = dhn
        _accumulate(dg_ref, jnp.sum(dgr, axis=0, keepdims=True), i == 0)
        _accumulate(loss_ref, jnp.broadcast_to(part, (8, 128)), i == 0)

    shifted = pl.BlockSpec((pl.Element(tl), pl.Element(d)), lambda i: (pl.multiple_of(CHUNK + i * tl, CHUNK), 0))
    dh, dg, loss = pl.pallas_call(
        body, grid=((t - CHUNK) // tl,),
        in_specs=[shifted, pl.BlockSpec((1, d), lambda i: (0, 0)), pl.BlockSpec((tl, d), lambda i: (i, 0))],
        out_specs=[shifted, pl.BlockSpec((1, d), lambda i: (0, 0)), pl.BlockSpec((8, 128), lambda i: (0, 0))],
        out_shape=[jax.ShapeDtypeStruct((t, d), F32), jax.ShapeDtypeStruct((1, d), F32),
                   jax.ShapeDtypeStruct((8, 128), F32)],
        compiler_params=_cparams("arbitrary"), name="loss_bwd")(h, g, target)

    def zero_head(dh_ref, o_ref):
        o_ref[...] = jnp.zeros_like(o_ref)

    dh = pl.pallas_call(
        zero_head, grid=(1,), in_specs=[ANY_SPEC], out_specs=pl.BlockSpec((CHUNK, d), lambda i: (0, 0)),
        out_shape=jax.ShapeDtypeStruct((t, d), F32), input_output_aliases={0: 0}, name="loss_bwd_head")(dh)
    return dh, dg, loss


CONV_BLOCK = 32


def _silu(x):
    return x * jax.nn.sigmoid(x)


def _row_shifts(win):
    n = win.shape[0]
    return [win] + [pltpu.roll(win, n - j, 0) for j in range(1, 8)]


def _cp_seq_fwd(z, conv_w, conv_b, ln_g, ln_b, pool_w, pool_scale):
    t = z.shape[0]
    tm = _row_tile(t, ROW_TILE_TARGET, CHUNK)

    def body(z_ref, cw_ref, cb_ref, lg_ref, lb_ref, pw_ref, ps_ref, c_ref, pm_ref, mix_ref, gbuf, pbuf):
        i = pl.program_id(0)

        @pl.when(i == 0)
        def _():
            gbuf[0:CONV_HALO, :] = jnp.zeros((CONV_HALO, CONV_DIM), F32)
            pbuf[0:POOL_HALO, :] = jnp.zeros((POOL_HALO, POOL_DIM), F32)

        @pl.when(i > 0)
        def _():
            gbuf[0:CONV_HALO, :] = gbuf[tm:tm + CONV_HALO, :]
            pbuf[0:POOL_HALO, :] = pbuf[tm:tm + POOL_HALO, :]

        av = z_ref[:, 0:CONV_DIM].astype(F32)
        ag = z_ref[:, CONV_DIM:2 * CONV_DIM].astype(F32)
        gbuf[CONV_HALO:CONV_HALO + tm, :] = av * jax.nn.sigmoid(ag)
        pbuf[POOL_HALO:POOL_HALO + tm, :] = z_ref[:, 2 * CONV_DIM:CP_IN].astype(F32)

        def conv_block(rb, carry):
            base = pl.multiple_of(rb * CONV_BLOCK, CONV_BLOCK)
            shifted = _row_shifts(gbuf[pl.ds(base, CONV_BLOCK + CONV_HALO), :])
            acc = jnp.zeros((CONV_BLOCK, CONV_DIM), F32)
            for k in range(CONV_WIDTH):
                whole, part = divmod(CONV_HALO - (CONV_WIDTH - 1) + k, 8)
                acc = acc + cw_ref[k:k + 1, :] * shifted[part][8 * whole:8 * whole + CONV_BLOCK, :]
            c_ref[pl.ds(base, CONV_BLOCK), :] = acc + cb_ref[...]
            return carry

        lax.fori_loop(0, tm // CONV_BLOCK, conv_block, 0)

        c = c_ref[...]
        mu = jnp.mean(c, axis=-1, keepdims=True)
        xc = c - mu
        ln = xc * lax.rsqrt(jnp.mean(xc * xc, axis=-1, keepdims=True) + EPS) * lg_ref[...] + lb_ref[...]
        row = i * tm + lax.broadcasted_iota(jnp.int32, (tm, 1), 0)
        mix_ref[:, 0:CONV_DIM] = jnp.where(row >= PAD_ROWS, _silu(ln), 0.0).astype(BF16)

        tpos = (row - PAD_ROWS + 1).astype(F32)
        for gi, wdw in enumerate(POOL_WINDOWS):
            lo = POOL_GROUP * gi
            cur = pbuf[POOL_HALO:POOL_HALO + tm, lo:lo + POOL_GROUP]
            sacc = cur
            for j in range(1, wdw):
                sacc = sacc + pbuf[POOL_HALO - j:POOL_HALO - j + tm, lo:lo + POOL_GROUP]
            pm = (sacc / jnp.clip(tpos, 1.0, float(wdw)) - cur).astype(BF16)
            pm_ref[:, lo:lo + POOL_GROUP] = pm
            pg = _dot(pm, pw_ref[gi].astype(BF16))
            mix_ref[:, CONV_DIM + lo:CONV_DIM + lo + POOL_GROUP] = (pg * ps_ref[:, lo:lo + POOL_GROUP]).astype(BF16)

    vec = pl.BlockSpec((1, CONV_DIM), lambda i: (0, 0))
    return pl.pallas_call(
        body, grid=(t // tm,),
        in_specs=[pl.BlockSpec((tm, CP_IN), lambda i: (i, 0)),
                  pl.BlockSpec((CONV_WIDTH, CONV_DIM), lambda i: (0, 0)), vec, vec, vec,
                  pl.BlockSpec((len(POOL_WINDOWS), POOL_GROUP, POOL_GROUP), lambda i: (0, 0, 0)), vec],
        out_specs=[pl.BlockSpec((tm, CONV_DIM), lambda i: (i, 0)), pl.BlockSpec((tm, POOL_DIM), lambda i: (i, 0)),
                   pl.BlockSpec((tm, CONV_DIM + POOL_DIM), lambda i: (i, 0))],
        out_shape=[jax.ShapeDtypeStruct((t, CONV_DIM), F32), jax.ShapeDtypeStruct((t, POOL_DIM), BF16),
                   jax.ShapeDtypeStruct((t, CONV_DIM + POOL_DIM), BF16)],
        scratch_shapes=[pltpu.VMEM((tm + CONV_HALO, CONV_DIM), F32), pltpu.VMEM((tm + POOL_HALO, POOL_DIM), F32)],
        compiler_params=_cparams("arbitrary"), name="cp_seq_fwd")(z, conv_w, conv_b, ln_g, ln_b, pool_w, pool_scale)


def _cp_seq_bwd(dmix, z, c, pm, conv_w, ln_g, ln_b, pool_w, pool_scale, dep=None):
    t = z.shape[0]
    tm = _row_tile(t, ROW_TILE_TARGET, CHUNK)
    nt = t // tm

    def body(dmix_ref, z_ref, c_ref, pm_ref, cw_ref, lg_ref, lb_ref, pw_ref, ps_ref,
             dz_ref, dcw_ref, dvec_ref, dpw_ref, dcbuf, qbuf, glu_buf, dwacc):
        i = pl.program_id(0)
        tile = nt - 1 - i

        @pl.when(i == 0)
        def _():
            dcbuf[tm:tm + CONV_HALO, :] = jnp.zeros((CONV_HALO, CONV_DIM), F32)
            qbuf[tm:tm + POOL_HALO, :] = jnp.zeros((POOL_HALO, POOL_DIM), F32)
            dcw_ref[...] = jnp.zeros_like(dcw_ref)
            dwacc[...] = jnp.zeros_like(dwacc)
            dvec_ref[...] = jnp.zeros_like(dvec_ref)
            dpw_ref[...] = jnp.zeros_like(dpw_ref)

        @pl.when(i > 0)
        def _():
            dcbuf[tm:tm + CONV_HALO, :] = dcbuf[0:CONV_HALO, :]
            qbuf[tm:tm + POOL_HALO, :] = qbuf[0:POOL_HALO, :]

        row = tile * tm + lax.broadcasted_iota(jnp.int32, (tm, 1), 0)
        cv = c_ref[...]
        mu = jnp.mean(cv, axis=-1, keepdims=True)
        xc = cv - mu
        rstd = lax.rsqrt(jnp.mean(xc * xc, axis=-1, keepdims=True) + EPS)
        xhat = xc * rstd
        ln = xhat * lg_ref[...] + lb_ref[...]
        sg = jax.nn.sigmoid(ln)
        da = jnp.where(row >= PAD_ROWS, dmix_ref[:, 0:CONV_DIM].astype(F32), 0.0)
        dln = da * (sg * (1.0 + ln * (1.0 - sg)))
        dxh = dln * lg_ref[...]
        dc = rstd * (dxh - jnp.mean(dxh, axis=-1, keepdims=True) - xhat * jnp.mean(dxh * xhat, axis=-1, keepdims=True))
        dcbuf[0:tm, :] = dc
        dvec_ref[0:1, :] += jnp.sum(dc, axis=0, keepdims=True)
        dvec_ref[1:2, :] += jnp.sum(dln * xhat, axis=0, keepdims=True)
        dvec_ref[2:3, :] += jnp.sum(dln, axis=0, keepdims=True)

        av = z_ref[:, 0:CONV_DIM].astype(F32)
        sig_g = jax.nn.sigmoid(z_ref[:, CONV_DIM:2 * CONV_DIM].astype(F32))
        glu_buf[...] = av * sig_g

        def conv_block(rb, carry):
            base = pl.multiple_of(rb * CONV_BLOCK, CONV_BLOCK)
            shifted = _row_shifts(dcbuf[pl.ds(base, CONV_BLOCK + CONV_HALO), :])
            glu = glu_buf[pl.ds(base, CONV_BLOCK), :]
            acc = jnp.zeros((CONV_BLOCK, CONV_DIM), F32)
            for k in range(CONV_WIDTH):
                whole, part = divmod(CONV_WIDTH - 1 - k, 8)
                slab = shifted[part][8 * whole:8 * whole + CONV_BLOCK, :]
                acc = acc + cw_ref[k:k + 1, :] * slab
                prod = slab * glu
                part = prod[0:8]
                for q in range(1, CONV_BLOCK // 8):
                    part = part + prod[8 * q:8 * q + 8]
                dwacc[k] += part
            glu_buf[pl.ds(base, CONV_BLOCK), :] = acc
            return carry

        lax.fori_loop(0, tm // CONV_BLOCK, conv_block, 0)

        @pl.when(i == nt - 1)
        def _():
            for k in range(CONV_WIDTH):
                dcw_ref[k:k + 1, :] = jnp.sum(dwacc[k], axis=0, keepdims=True)
        dglu = glu_buf[...]
        dz_ref[:, 0:CONV_DIM] = (dglu * sig_g).astype(BF16)
        dz_ref[:, CONV_DIM:2 * CONV_DIM] = (dglu * av * sig_g * (1.0 - sig_g)).astype(BF16)

        tpos = (row - PAD_ROWS + 1).astype(F32)
        for gi, wdw in enumerate(POOL_WINDOWS):
            lo = POOL_GROUP * gi
            dp = dmix_ref[:, CONV_DIM + lo:CONV_DIM + lo + POOL_GROUP].astype(F32)
            pmv = pm_ref[:, lo:lo + POOL_GROUP]
            pwb = pw_ref[gi].astype(BF16)
            dvec_ref[3:4, lo:lo + POOL_GROUP] += jnp.sum(dp * _dot(pmv, pwb), axis=0, keepdims=True)
            dq = (dp * ps_ref[:, lo:lo + POOL_GROUP]).astype(BF16)
            dpw_ref[gi] += _dot_tn(pmv, dq)
            dpm = _dot_nt(dq, pwb)
            qbuf[0:tm, lo:lo + POOL_GROUP] = dpm / jnp.clip(tpos, 1.0, float(wdw))
            sacc = -dpm
            for j in range(wdw):
                sacc = sacc + qbuf[j:j + tm, lo:lo + POOL_GROUP]
            dz_ref[:, 2 * CONV_DIM + lo:2 * CONV_DIM + lo + POOL_GROUP] = sacc.astype(BF16)

    vec = pl.BlockSpec((1, CONV_DIM), lambda i: (0, 0))
    rev = lambda i: (nt - 1 - i, 0)
    return _call_after(
        dep, body, 9,
        [pl.BlockSpec((tm, CONV_DIM + POOL_DIM), rev), pl.BlockSpec((tm, CP_IN), rev),
         pl.BlockSpec((tm, CONV_DIM), rev), pl.BlockSpec((tm, POOL_DIM), rev),
         pl.BlockSpec((CONV_WIDTH, CONV_DIM), lambda i: (0, 0)), vec, vec,
         pl.BlockSpec((len(POOL_WINDOWS), POOL_GROUP, POOL_GROUP), lambda i: (0, 0, 0)), vec],
        (dmix, z, c, pm, conv_w, ln_g, ln_b, pool_w, pool_scale), grid=(nt,),
        out_specs=[pl.BlockSpec((tm, CP_IN), rev), pl.BlockSpec((CONV_WIDTH + 1, CONV_DIM), lambda i: (0, 0)),
                   pl.BlockSpec((8, CONV_DIM), lambda i: (0, 0)),
                   pl.BlockSpec((len(POOL_WINDOWS), POOL_GROUP, POOL_GROUP), lambda i: (0, 0, 0))],
        out_shape=[jax.ShapeDtypeStruct((t, CP_IN), BF16), jax.ShapeDtypeStruct((CONV_WIDTH + 1, CONV_DIM), F32),
                   jax.ShapeDtypeStruct((8, CONV_DIM), F32),
                   jax.ShapeDtypeStruct((len(POOL_WINDOWS), POOL_GROUP, POOL_GROUP), F32)],
        scratch_shapes=[pltpu.VMEM((tm + CONV_HALO, CONV_DIM), F32), pltpu.VMEM((tm + POOL_HALO, POOL_DIM), F32),
                        pltpu.VMEM((tm, CONV_DIM), F32), pltpu.VMEM((CONV_WIDTH + 1, 8, CONV_DIM), F32)],
        compiler_params=_cparams("arbitrary"), name="cp_seq_bwd")


GLA_UNROLL = 2
Q0, K0, V0, G0, R0 =0, GLA_DK, 2 * GLA_DK, 2 * GLA_DK + GLA_DV, 2 * GLA_DK + 2 * GLA_DV


def _split3(x):
    hi = x.astype(BF16)
    r1 = x - hi.astype(F32)
    mid = r1.astype(BF16)
    lo = (r1 - mid.astype(F32)).astype(BF16)
    return hi, mid, lo


def _tri(strict):
    r = lax.broadcasted_iota(jnp.int32, (CHUNK, CHUNK), 0)
    c = lax.broadcasted_iota(jnp.int32, (CHUNK, CHUNK), 1)
    return ((r > c) if strict else (r >= c)).astype(BF16)


def _chunk_sums(x, cpt, strict, pieces):
    tri3 = jnp.broadcast_to(_tri(strict)[None], (cpt, CHUNK, CHUNK))
    acc = None
    for piece in _split3(x.reshape(cpt, CHUNK, x.shape[-1]))[:pieces]:
        part = jnp.einsum("bij,bjk->bik", tri3, piece, preferred_element_type=F32)
        acc = part if acc is None else acc + part
    return acc


def _chunk_decay(r, gw_ref, gb_ref, cpt):
    pre = _dot(r, gw_ref[...]) + gb_ref[...]
    lac = (jnp.minimum(pre, 0.0) - jnp.log(1.0 + jnp.exp(-jnp.abs(pre)))) * (1.0 / GATE_NORM)
    cum3 = _chunk_sums(lac, cpt, False, 3)
    return cum3, cum3[:, CHUNK - 1:CHUNK, :]


def _gla_seq_fwd(z, gate_w, gate_b, head_g, dep=None):
    t = z.shape[0]
    tm = _row_tile(t, ROW_TILE_TARGET, CHUNK)
    cpt = tm // CHUNK
    scale = GLA_HK ** -0.5

    def body(z_ref, gw_ref, gb_ref, hg_ref, o_ref, mix_ref, st_ref, state, kdec_s, e_s):
        @pl.when(pl.program_id(0) == 0)
        def _():
            state[...] = jnp.zeros_like(state)

        cum3, tot3 = _chunk_decay(z_ref[:, R0:R0 + GATE_PAD], gw_ref, gb_ref, cpt)
        dec = jnp.exp(jnp.broadcast_to(tot3, cum3.shape) - cum3).reshape(tm, GLA_DK)
        kdec_s[...] = (z_ref[:, K0:K0 + GLA_DK].astype(F32) * dec).astype(BF16)
        e_s[...] = jnp.exp(jnp.broadcast_to(tot3, (cpt, 8, GLA_DK))).reshape(cpt * 8, GLA_DK)

        def chunk(ci, carry):
            rows = pl.ds(pl.multiple_of(ci * CHUNK, CHUNK), CHUNK)
            e_all = e_s[pl.ds(pl.multiple_of(ci * 8, 8), 8), :][0:1, :]
            st_ref[ci] = state[...].astype(BF16)
            for hd in range(GLA_HEADS):
                ks = slice(hd * GLA_HK, (hd + 1) * GLA_HK)
                vs = slice(hd * GLA_HV, (hd + 1) * GLA_HV)
                v = z_ref[rows, V0 + hd * GLA_HV:V0 + (hd + 1) * GLA_HV]
                st = state[vs, :] * e_all[:, ks] + _dot_tn(v, kdec_s[rows, ks])
                state[vs, :] = st
                q = z_ref[rows, Q0 + hd * GLA_HK:Q0 + (hd + 1) * GLA_HK]
                o_ref[rows, vs] = (_dot_nt(q, st.astype(BF16)) * scale).astype(BF16)
            return carry

        lax.fori_loop(0, cpt, chunk, 0, unroll=GLA_UNROLL)

        for hd in range(GLA_HEADS):
            vs = slice(hd * GLA_HV, (hd + 1) * GLA_HV)
            on = _rms(o_ref[:, vs].astype(F32), hg_ref[...])
            gv = z_ref[:, G0 + hd * GLA_HV:G0 + (hd + 1) * GLA_HV].astype(F32)
            mix_ref[:, vs] = (on * _silu(gv)).astype(BF16)

    return _call_after(
        dep, body, 4,
        [pl.BlockSpec((tm, GLA_IN_PAD), lambda i: (i, 0)),
         pl.BlockSpec((GATE_PAD, GLA_DK), lambda i: (0, 0)), pl.BlockSpec((1, GLA_DK), lambda i: (0, 0)),
         pl.BlockSpec((1, GLA_HV), lambda i: (0, 0))], (z, gate_w, gate_b, head_g), grid=(t // tm,),
        out_specs=[pl.BlockSpec((tm, GLA_DV), lambda i: (i, 0)), pl.BlockSpec((tm, GLA_DV), lambda i: (i, 0)),
                   pl.BlockSpec((cpt, GLA_DV, GLA_HK), lambda i: (i, 0, 0))],
        out_shape=[jax.ShapeDtypeStruct((t, GLA_DV), BF16), jax.ShapeDtypeStruct((t, GLA_DV), BF16),
                   jax.ShapeDtypeStruct((t // CHUNK, GLA_DV, GLA_HK), BF16)],
        scratch_shapes=[pltpu.VMEM((GLA_DV, GLA_HK), F32), pltpu.VMEM((tm, GLA_DK), BF16),
                        pltpu.VMEM((cpt * 8, GLA_DK), F32)],
        compiler_params=_cparams("arbitrary"), name="gla_seq_fwd")


def _gla_seq_bwd(dmix, o, z, states, gate_w, gate_b, head_g, dep=None):
    t = z.shape[0]
    tm = _row_tile(t, ROW_TILE_TARGET, CHUNK)
    cpt = tm // CHUNK
    nt = t // tm
    scale = GLA_HK ** -0.5

    def body(dmix_ref, o_ref, z_ref, st_ref, gw_ref, gb_ref, hg_ref, dz_ref, dgw_ref, dgb_ref, dhg_ref,
             dstate, dec_s, kdec_s, dkdec_s, do_s, e_s, dtot_s):
        @pl.when(pl.program_id(0) == 0)
        def _():
            dstate[...] = jnp.zeros_like(dstate)
            dgw_ref[...] = jnp.zeros_like(dgw_ref)
            dgb_ref[...] = jnp.zeros_like(dgb_ref)
            dhg_ref[...] = jnp.zeros_like(dhg_ref)

        cum3, tot3 = _chunk_decay(z_ref[:, R0:R0 + GATE_PAD], gw_ref, gb_ref, cpt)
        dec = jnp.exp(jnp.broadcast_to(tot3, cum3.shape) - cum3).reshape(tm, GLA_DK)
        dec_s[...] = dec
        kdec_s[...] = z_ref[:, K0:K0 + GLA_DK].astype(F32) * dec
        e_s[...] = jnp.exp(jnp.broadcast_to(tot3, (cpt, 8, GLA_DK))).reshape(cpt * 8, GLA_DK)
        dhg = jnp.zeros((1, GLA_HV), F32)
        for hd in range(GLA_HEADS):
            vs = slice(hd * GLA_HV, (hd + 1) * GLA_HV)
            gcols = slice(G0 + hd * GLA_HV, G0 + (hd + 1) * GLA_HV)
            ov = o_ref[:, vs].astype(F32)
            gv = z_ref[:, gcols].astype(F32)
            dm = dmix_ref[:, vs].astype(F32)
            sg = jax.nn.sigmoid(gv)
            rr = lax.rsqrt(jnp.mean(ov * ov, axis=-1, keepdims=True) + EPS)
            xhat = ov * rr
            don = dm * (gv * sg)
            dz_ref[:, gcols] = (dm * (xhat * hg_ref[...]) * (sg * (1.0 + gv * (1.0 - sg)))).astype(BF16)
            dhg = dhg + jnp.sum(don * xhat, axis=0, keepdims=True)
            dxh = don * hg_ref[...]
            do_s[:, vs] = (rr * (dxh - xhat * jnp.mean(dxh * xhat, axis=-1, keepdims=True)) * scale).astype(BF16)
        dhg_ref[...] += dhg

        def chunk(cj, carry):
            ci = cpt - 1 - cj
            rows = pl.ds(pl.multiple_of(ci * CHUNK, CHUNK), CHUNK)
            erows = pl.ds(pl.multiple_of(ci * 8, 8), 8)
            e_all = e_s[erows, :][0:1, :]
            for hd in range(GLA_HEADS):
                ks = slice(hd * GLA_HK, (hd + 1) * GLA_HK)
                vs = slice(hd * GLA_HV, (hd + 1) * GLA_HV)
                e = e_all[:, ks]
                kdb = kdec_s[rows, ks].astype(BF16)
                v = z_ref[rows, V0 + hd * GLA_HV:V0 + (hd + 1) * GLA_HV]
                q = z_ref[rows, Q0 + hd * GLA_HK:Q0 + (hd + 1) * GLA_HK]
                do = do_s[rows, vs]
                st_prev = st_ref[ci, vs, :].astype(F32)
                st = st_prev * e + _dot_tn(v, kdb)
                dz_ref[rows, Q0 + hd * GLA_HK:Q0 + (hd + 1) * GLA_HK] = _dot(do, st.astype(BF16)).astype(BF16)
                dst = dstate[vs, :] + _dot_tn(do, q)
                dstb = dst.astype(BF16)
                dkdec_s[rows, ks] = _dot(v, dstb)
                dz_ref[rows, V0 + hd * GLA_HV:V0 + (hd + 1) * GLA_HV] = _dot_nt(kdb, dstb).astype(BF16)
                dtot = jnp.sum(dst * st_prev, axis=0, keepdims=True) * e
                dtot_s[erows, ks] = jnp.broadcast_to(dtot, (8, GLA_HK))
                dstate[vs, :] = dst * e
            return carry

        lax.fori_loop(0, cpt, chunk, 0, unroll=GLA_UNROLL)

        dkdec = dkdec_s[...]
        dz_ref[:, K0:K0 + GLA_DK] = (dkdec * dec_s[...]).astype(BF16)
        before = _chunk_sums(dkdec * kdec_s[...], cpt, True, 2)
        dtot3 = dtot_s[...].reshape(cpt, 8, GLA_DK)[:, 0:1, :]
        dlac = (jnp.broadcast_to(dtot3, before.shape) + before).reshape(tm, GLA_DK)
        pre = _dot(z_ref[:, R0:R0 + GATE_PAD], gw_ref[...]) + gb_ref[...]
        dpre = dlac * (1.0 / GATE_NORM) * (1.0 - jax.nn.sigmoid(pre))
        dpb = dpre.astype(BF16)
        dz_ref[:, R0:R0 + GATE_PAD] = _dot_nt(dpb, gw_ref[...]).astype(BF16)
        dgw_ref[...] += _dot_tn(z_ref[:, R0:R0 + GATE_PAD], dpb)
        dgb_ref[...] += jnp.sum(dpre, axis=0, keepdims=True)

    rev = lambda i: (nt - 1 - i, 0)
    return _call_after(
        dep, body, 7,
        [pl.BlockSpec((tm, GLA_DV), rev), pl.BlockSpec((tm, GLA_DV), rev), pl.BlockSpec((tm, GLA_IN_PAD), rev),
         pl.BlockSpec((cpt, GLA_DV, GLA_HK), lambda i: (nt - 1 - i, 0, 0)),
         pl.BlockSpec((GATE_PAD, GLA_DK), lambda i: (0, 0)), pl.BlockSpec((1, GLA_DK), lambda i: (0, 0)),
         pl.BlockSpec((1, GLA_HV), lambda i: (0, 0))],
        (dmix, o, z, states, gate_w, gate_b, head_g), grid=(nt,),
        out_specs=[pl.BlockSpec((tm, GLA_IN_PAD), rev), pl.BlockSpec((GATE_PAD, GLA_DK), lambda i: (0, 0)),
                   pl.BlockSpec((1, GLA_DK), lambda i: (0, 0)), pl.BlockSpec((1, GLA_HV), lambda i: (0, 0))],
        out_shape=[jax.ShapeDtypeStruct((t, GLA_IN_PAD), BF16), jax.ShapeDtypeStruct((GATE_PAD, GLA_DK), F32),
                   jax.ShapeDtypeStruct((1, GLA_DK), F32), jax.ShapeDtypeStruct((1, GLA_HV), F32)],
        scratch_shapes=[pltpu.VMEM((GLA_DV, GLA_HK), F32), pltpu.VMEM((tm, GLA_DK), F32), pltpu.VMEM((tm, GLA_DK), F32),
                        pltpu.VMEM((tm, GLA_DK), F32), pltpu.VMEM((tm, GLA_DV), BF16),
                        pltpu.VMEM((cpt * 8, GLA_DK), F32), pltpu.VMEM((cpt * 8, GLA_DK), F32)],
        compiler_params=_cparams("arbitrary"), name="gla_seq_bwd")


def _sum_halves(g, recv, c_idx, name):
    n, r, cdim = g.shape
    h = r // 2
    tr = _row_tile(h, 256, 8)
    nh = h // tr

    def body(c_ref, g_ref, r_ref, o_ref):
        o_ref[...] = (g_ref[...] + r_ref[...]).astype(BF16)

    return pl.pallas_call(
        body,
        grid_spec=pltpu.PrefetchScalarGridSpec(
            num_scalar_prefetch=1, grid=(n, nh),
            in_specs=[pl.BlockSpec((None, tr, cdim), lambda s, i, c: (s, c[0] * nh + i, 0)),
                      pl.BlockSpec((None, tr, cdim), lambda s, i, c: (s, i, 0))],
            out_specs=pl.BlockSpec((None, tr, cdim), lambda s, i, c: (s, i, 0))),
        out_shape=jax.ShapeDtypeStruct((n, h, cdim), BF16),
        compiler_params=_cparams("parallel", "parallel"), name=name)(c_idx, g, recv)


def _sum_slots(x, name):
    n, r, cdim = x.shape
    tr = _row_tile(r, 256, 8)

    def body(x_ref, o_ref):
        acc = x_ref[0].astype(F32)
        for j in range(1, n):
            acc = acc + x_ref[j].astype(F32)
        o_ref[...] = acc

    return pl.pallas_call(
        body, grid=(r // tr,),
        in_specs=[pl.BlockSpec((n, tr, cdim), lambda i: (0, i, 0))],
        out_specs=pl.BlockSpec((tr, cdim), lambda i: (i, 0)),
        out_shape=jax.ShapeDtypeStruct((r, cdim), F32),
        compiler_params=_cparams("parallel"), name=name)(x)


def _sum_own_and_slots(own, slots, chip_idx, name):
    n, r, cdim = own.shape
    tr = _row_tile(r, 256, 8)

    def body(s_ref, own_ref, a_ref, b_ref, c_ref, o_ref):
        o_ref[...] = (own_ref[...].astype(F32) + a_ref[...].astype(F32) + b_ref[...].astype(F32)
                      + c_ref[...].astype(F32))

    def slot(dd):
        return pl.BlockSpec((None, tr, cdim), lambda i, s: ((s[0] + dd) % n, i, 0))

    return pl.pallas_call(
        body,
        grid_spec=pltpu.PrefetchScalarGridSpec(
            num_scalar_prefetch=1, grid=(r // tr,), in_specs=[slot(0), slot(1), slot(2), slot(3)],
            out_specs=pl.BlockSpec((tr, cdim), lambda i, s: (i, 0))),
        out_shape=jax.ShapeDtypeStruct((r, cdim), F32),
        compiler_params=_cparams("parallel"), name=name)(chip_idx, own, slots, slots, slots)


def _add2(a, b, name):
    r, cdim = a.shape
    tr = _row_tile(r, 256, 8)

    def body(a_ref, b_ref, o_ref):
        o_ref[...] = a_ref[...] + b_ref[...]

    spec = pl.BlockSpec((tr, cdim), lambda i: (i, 0))
    return pl.pallas_call(body, grid=(r // tr,), in_specs=[spec, spec], out_specs=spec,
                          out_shape=jax.ShapeDtypeStruct((r, cdim), F32),
                          compiler_params=_cparams("parallel"), name=name)(a, b)


def _adamw(w, g, m, v, name):
    r, cdim = w.shape
    tr = _row_tile(r, 256, 8)

    def body(w_ref, g_ref, m_ref, v_ref, go_ref, d_ref, mo_ref, vo_ref):
        gv = g_ref[...]
        go_ref[...] = gv
        mn = ADAM_B1 * m_ref[...] + (1.0 - ADAM_B1) * gv
        vn = ADAM_B2 * v_ref[...] + (1.0 - ADAM_B2) * (gv * gv)
        m_hat = mn / (1.0 - ADAM_B1 ** ADAM_STEP)
        v_hat = vn / (1.0 - ADAM_B2 ** ADAM_STEP)
        d_ref[...] = -ADAM_LR * (m_hat / (jnp.sqrt(v_hat) + ADAM_EPS) + ADAM_WD * w_ref[...])
        mo_ref[...] = mn
        vo_ref[...] = vn

    spec = pl.BlockSpec((tr, cdim), lambda i: (i, 0))
    shp = jax.ShapeDtypeStruct((r, cdim), F32)
    return pl.pallas_call(body, grid=(r // tr,), in_specs=[spec] * 4, out_specs=[spec] * 4,
                          out_shape=[shp] * 4, compiler_params=_cparams("parallel"), name=name)(w, g, m, v)


def _adamw_many(ws, gs, ms, vs):
    n = len(ws)

    def body(*refs):
        for i in range(n):
            w_ref, g_ref, m_ref, v_ref = refs[i], refs[n + i], refs[2 * n + i], refs[3 * n + i]
            d_ref, mo_ref, vo_ref = refs[4 * n + i], refs[5 * n + i], refs[6 * n + i]
            gv = g_ref[...]
            mn = ADAM_B1 * m_ref[...] + (1.0 - ADAM_B1) * gv
            vn = ADAM_B2 * v_ref[...] + (1.0 - ADAM_B2) * (gv * gv)
            m_hat = mn / (1.0 - ADAM_B1 ** ADAM_STEP)
            v_hat = vn / (1.0 - ADAM_B2 ** ADAM_STEP)
            d_ref[...] = -ADAM_LR * (m_hat / (jnp.sqrt(v_hat) + ADAM_EPS) + ADAM_WD * w_ref[...])
            mo_ref[...] = mn
            vo_ref[...] = vn

    shapes = [jax.ShapeDtypeStruct(w.shape, F32) for w in ws]
    outs = pl.pallas_call(body, out_shape=shapes * 3, name="adamw_small")(*ws, *gs, *ms, *vs)
    return outs[:n], outs[n:2 * n], outs[2 * n:]


def _split_rows(a):
    return a.reshape(a.shape[0], 2, a.shape[1] // 2, a.shape[2])


def _place():
    x, y, c = lax.axis_index("x"), lax.axis_index("y"), lax.axis_index("c")
    chips = [(1 - x, y), (x, 1 - y), (1 - x, 1 - y)]
    return x, y, c, chips


def _remote(src, dst, send_sem, recv_sem, to):
    return pltpu.make_async_remote_copy(src_ref=src, dst_ref=dst, send_sem=send_sem, recv_sem=recv_sem,
                                        device_id=to, device_id_type=MESH)


def _plan_gather(n_halved):
    def plan(src_refs, land_refs):
        x, y, c, chips = _place()
        me = 2 * x + y
        copies = []
        for k, (src, land) in enumerate(zip(src_refs, land_refs)):
            for (px, py) in chips:
                frm = 2 * px + py
                if k < n_halved:
                    copies.append((src.at[c], land.at[me, c], (px, py, c), land.at[frm, c]))
                else:
                    copies.append((src, land.at[me], (px, py, c), land.at[frm]))
        return copies
    return plan


def _plan_share(src_refs, land_refs):
    x, y, c, chips = _place()
    me = 2 * x + y
    sib = (x, y, 1 - c)
    copies = []
    for src, land in zip(src_refs, land_refs):
        copies.append((src, land.at[me], sib, land.at[me]))
        for (px, py) in chips:
            frm = 2 * px + py
            copies.append((land.at[frm, c], land.at[frm, c], sib, land.at[frm, 1 - c]))
    return copies


def _plan_scatter(n_parts):
    def plan(src_refs, land_refs):
        x, y, c, chips = _place()
        me = 2 * x + y
        copies = []
        for k, (src, land) in enumerate(zip(src_refs, land_refs)):
            for (px, py) in chips:
                to = 2 * px + py
                copies.append((src.at[to] if k < n_parts else src, land.at[me], (px, py, c), land.at[to]))
        return copies
    return plan


def _plan_exchange(n_split):
    def plan(src_refs, land_refs):
        x, y, c, _ = _place()
        sib = (x, y, 1 - c)
        return [(src.at[:, 1 - c] if k < n_split else src, land, sib, land)
                for k, (src, land) in enumerate(zip(src_refs, land_refs))]
    return plan


def _hbm(a):
    return pltpu.HBM(a.shape, a.dtype)


def _start_copies(name, srcs, lands, plan, ncopy, dep=None):
    ns, nl = len(srcs), len(lands)
    nin = ns + nl + (0 if dep is None else 1)

    def body(*refs):
        send_sems, recv_sems, token = refs[nin], refs[nin + 1], refs[-1]
        for k, (src, dst, dev, _) in enumerate(plan(refs[:ns], refs[ns:ns + nl])):
            _remote(src, dst, send_sems.at[k], recv_sems.at[k], dev).start()
        token[...] = jnp.zeros_like(token)

    args = [pltpu.with_memory_space_constraint(a, pltpu.HBM) for a in list(srcs) + list(lands)]
    outs = pl.pallas_call(
        body, name=name,
        out_shape=(pltpu.SemaphoreType.DMA((ncopy,)), pltpu.SemaphoreType.DMA((ncopy,)),
                   *[_hbm(a) for a in list(srcs) + list(lands)], jax.ShapeDtypeStruct((8, 128), F32)),
        in_specs=[HBM_SPEC] * (ns + nl) + ([] if dep is None else [ANY_SPEC]),
        out_specs=(SEM_SPEC, SEM_SPEC, *([HBM_SPEC] * (ns + nl)), pl.BlockSpec(memory_space=pltpu.VMEM)),
        input_output_aliases={i: 2 + i for i in range(ns + nl)},
        compiler_params=pltpu.CompilerParams(has_side_effects=SIDE_EFFECT),
    )(*args, *([] if dep is None else [dep]))
    return outs[0], outs[1], list(outs[2:2 + ns]), list(outs[2 + ns:2 + ns + nl]), outs[-1]


def _wait_copies(name, started, plan, after):
    send_sems, recv_sems, srcs, lands, _ = started
    ns, nl = len(srcs), len(lands)
    after = list(after) if isinstance(after, (list, tuple)) else [after]

    def body(*refs):
        send_ref, recv_ref = refs[ns + nl], refs[ns + nl + 1]
        for k, (src, _, dev, mine) in enumerate(plan(refs[:ns], refs[ns:ns + nl])):
            copy = _remote(src, mine, send_ref.at[k], recv_ref.at[k], dev)
            copy.wait_send()
            copy.wait_recv()

    outs = pl.pallas_call(
        body, name=name, out_shape=tuple(_hbm(a) for a in srcs + lands),
        in_specs=[HBM_SPEC] * (ns + nl) + [SEM_SPEC, SEM_SPEC] + [ANY_SPEC] * len(after),
        out_specs=tuple([HBM_SPEC] * (ns + nl)),
        input_output_aliases={i: i for i in range(ns + nl)},
        compiler_params=pltpu.CompilerParams(has_side_effects=SIDE_EFFECT),
    )(*srcs, *lands, send_sems, recv_sems, *after)
    return list(outs[:ns]), list(outs[ns:])


def _share_with_sibling(name, srcs, lands):
    n = len(srcs)

    def body(*refs):
        src_refs, land_refs, out_refs = refs[:n], refs[n:2 * n], refs[2 * n:3 * n]
        send_sem, recv_sem = refs[3 * n:]
        x, y, c, chips = _place()
        me = 2 * x + y
        sib = (x, y, 1 - c)
        sends, recvs = [], []
        for k in range(n):
            sems = (send_sem.at[4 * k], recv_sem.at[4 * k])
            sends.append(_remote(src_refs[k], out_refs[k].at[me], *sems, sib))
            recvs.append(_remote(src_refs[k], out_refs[k].at[me], *sems, sib))
            for j, (px, py) in enumerate(chips):
                frm = 2 * px + py
                sems = (send_sem.at[4 * k + 1 + j], recv_sem.at[4 * k + 1 + j])
                sends.append(_remote(land_refs[k].at[frm, c], out_refs[k].at[frm, c], *sems, sib))
                recvs.append(_remote(land_refs[k].at[frm, c], out_refs[k].at[frm, 1 - c], *sems, sib))
        for cp in sends:
            cp.start()
        for cp in recvs:
            cp.wait_recv()
        for cp in sends:
            cp.wait_send()

    return pl.pallas_call(
        body, name=name, in_specs=[HBM_SPEC] * (2 * n), out_specs=[HBM_SPEC] * n,
        out_shape=[jax.ShapeDtypeStruct(a.shape, a.dtype) for a in lands],
        input_output_aliases={n + k: k for k in range(n)},
        scratch_shapes=[pltpu.SemaphoreType.DMA((4 * n,)), pltpu.SemaphoreType.DMA((4 * n,))],
    )(*srcs, *lands)


def _join_halves(reduced, dests, out_shapes, name):
    nr = len(reduced)

    def body(*refs):
        r_in = refs[:nr]
        outs = refs[nr:nr + len(out_shapes)]
        send_sem, recv_sem, back_send, back_recv = refs[nr + len(out_shapes):]
        x, y, c, _ = _place()
        sib = (x, y, 1 - c)
        sends, backs = [], []
        for k in range(nr):
            oi, layer = dests[k]
            sends.append(_remote(r_in[k], outs[oi].at[layer, c], send_sem.at[k], recv_sem.at[k], sib))
        for cp in sends:
            cp.start()
        for k in range(nr):
            oi, layer = dests[k]
            theirs = outs[oi].at[layer, 1 - c]
            _remote(r_in[k], theirs, send_sem.at[k], recv_sem.at[k], sib).wait_recv()
            back = _remote(theirs, theirs, back_send.at[k], back_recv.at[k], sib)
            back.start()
            backs.append(back)
        for k in range(nr):
            oi, layer = dests[k]
            mine = outs[oi].at[layer, c]
            _remote(mine, mine, back_send.at[k], back_recv.at[k], sib).wait_recv()
        for cp in sends + backs:
            cp.wait_send()

    split = [(s[0], 2, s[1] // 2, s[2]) for s in out_shapes]
    outs = pl.pallas_call(
        body, in_specs=[HBM_SPEC] * nr, out_specs=[HBM_SPEC] * len(out_shapes),
        out_shape=[jax.ShapeDtypeStruct(s, F32) for s in split],
        scratch_shapes=[pltpu.SemaphoreType.DMA((nr,)), pltpu.SemaphoreType.DMA((nr,)),
                        pltpu.SemaphoreType.DMA((nr,)), pltpu.SemaphoreType.DMA((nr,))],
        name=name)(*reduced)
    return [o.reshape(s) for o, s in zip(outs, out_shapes)]


def _pack(arrs):
    flat = jnp.concatenate([a.reshape(-1).astype(F32) for a in arrs])
    n = flat.shape[0]
    rows = -(-n // PACK_WIDTH)
    rows = -(-rows // 8) * 8
    return jnp.pad(flat, (0, rows * PACK_WIDTH - n)).reshape(rows, PACK_WIDTH)


def _unpack(buf, shapes):
    flat = buf.reshape(-1)
    out, off = [], 0
    for shp in shapes:
        n = 1
        for s in shp:
            n *= s
        out.append(flat[off:off + n].reshape(shp))
        off += n
    return out


def _unshard_cols(stacked):
    moved = jnp.moveaxis(stacked, 0, -2)
    return moved.reshape(moved.shape[:-2] + (moved.shape[-2] * moved.shape[-1],))


def _take_cols(blocks, start, width):
    bw = blocks.shape[2]
    pieces, lo = [], start
    while lo < start + width:
        b = lo // bw
        hi = min(start + width, (b + 1) * bw)
        pieces.append(blocks[b][:, lo - b * bw:hi - b * bw])
        lo = hi
    return jnp.concatenate(pieces, axis=1)


def _col_shard(full, s, width):
    return lax.dynamic_slice_in_dim(full, s * width, width, axis=full.ndim - 1)


def kernel(x, meta_tokens, mix_norm_g, ffn_norm_g, ffn_w1, ffn_w2, cp_w_in, cp_conv_w, cp_conv_b, cp_ln_g, cp_ln_b, cp_pool_w, cp_pool_scale, cp_w_out, gla_w_in, gla_gate_w2, gla_gate_b, gla_head_g, gla_w_out, final_norm_g, loss_target, m_meta_tokens, m_mix_norm_g, m_ffn_norm_g, m_ffn_w1, m_ffn_w2, m_cp_w_in, m_cp_conv_w, m_cp_conv_b, m_cp_ln_g, m_cp_ln_b, m_cp_pool_w, m_cp_pool_scale, m_cp_w_out, m_gla_w_in, m_gla_gate_w2, m_gla_gate_b, m_gla_head_g, m_gla_w_out, m_final_norm_g, v_meta_tokens, v_mix_norm_g, v_ffn_norm_g, v_ffn_w1, v_ffn_w2, v_cp_w_in, v_cp_conv_w, v_cp_conv_b, v_cp_ln_g, v_cp_ln_b, v_cp_pool_w, v_cp_pool_scale, v_cp_w_out, v_gla_w_in, v_gla_gate_w2, v_gla_gate_b, v_gla_head_g, v_gla_w_out, v_final_norm_g):
    d = D_MODEL
    chip = 2 * lax.axis_index("x") + lax.axis_index("y")
    core = lax.axis_index("c")
    seq = x.shape[1]
    t = seq + CHUNK

    sharded_small = [meta_tokens, cp_conv_w, gla_gate_w2, gla_gate_b, gla_head_g]

    def halves(w):
        return w.astype(BF16).reshape(2, w.shape[0] // 2, w.shape[1])

    def unhalve(g):
        return g.reshape(N_CHIPS, 2 * g.shape[2], g.shape[3])

    def start_gather(name, srcs, dep, whole=()):
        lands = [lax.empty((N_CHIPS,) + s.shape, s.dtype) for s in srcs]
        for a in whole:
            lands.append(lax.dynamic_update_slice(jnp.zeros((N_CHIPS,) + a.shape, a.dtype), a[None], (chip,) + (0,) * a.ndim))
        plan = _plan_gather(len(srcs))
        return _start_copies(name, list(srcs) + list(whole), lands, plan, 3 * len(lands), dep), plan, len(srcs)

    def arrived(name, gather, after):
        started, plan, n = gather
        srcs, lands = _wait_copies(name + "_wait", started, plan, after)
        return srcs[:n], lands[:n], lands[n:]

    cp_gather = start_gather("gather_cp_start", [halves(cp_w_in[0]), halves(cp_w_out[0])], None, [_pack(sharded_small)])
    ffn0_gather = start_gather("gather_ffn0_start", [halves(ffn_w1[0]), halves(ffn_w2[0])], cp_gather[0][-1])
    gla_gather = start_gather("gather_gla_start", [halves(gla_w_in[0]), halves(gla_w_out[0])], ffn0_gather[0][-1])
    ffn1_gather = start_gather("gather_ffn1_start", [halves(ffn_w1[1]), halves(ffn_w2[1])], gla_gather[0][-1])
    h0_rows = jnp.concatenate([jnp.zeros((CHUNK, d), F32), x[0]], axis=0)
    cp_srcs, cp_lands, (small_g,) = arrived("gather_cp", cp_gather, [ffn1_gather[0][-1], h0_rows])
    cpin_g, cpout_g = [unhalve(g) for g in _share_with_sibling("gather_cp_share", cp_srcs, cp_lands)]
    per_chip = [_unpack(small_g[j], [a.shape for a in sharded_small]) for j in range(N_CHIPS)]
    meta_f, conv_w_f, gate_w_f, gate_b_f, head_g_f = [
        jnp.concatenate([per_chip[j][i] for j in range(N_CHIPS)], axis=-1) for i in range(len(sharded_small))]
    conv_w_f, gate_w_f = conv_w_f[0], gate_w_f[0]
    w_cp_in = _unshard_cols(cpin_g)
    w_cp_out = cpout_g.reshape(CONV_DIM + POOL_DIM, d)
    gate_w_pad = jnp.pad(gate_w_f, ((0, GATE_PAD - GATE_RANK), (0, 0))).astype(BF16)
    row = lambda a: a.reshape(1, -1)
    c_idx = core.reshape(1).astype(jnp.int32)
    chip_idx = chip.reshape(1).astype(jnp.int32)

    h0 = lax.dynamic_update_slice(h0_rows, meta_f, (PAD_ROWS, 0))
    z0, u0 = _norm_matmul(h0, row(mix_norm_g[0]), w_cp_in, 512, "cp_in_proj")
    c0, pm0, mix0 = _cp_seq_fwd(z0, conv_w_f, cp_conv_b, cp_ln_g, cp_ln_b, cp_pool_w[0], cp_pool_scale)
    ffn0_srcs, ffn0_lands, _ = arrived("gather_ffn0", ffn0_gather, mix0)
    ffn0_share = _start_copies("gather_ffn0_share_start", ffn0_srcs, ffn0_lands, _plan_share, 4 * len(ffn0_srcs))
    h1 = _matmul_residual(mix0, w_cp_out, h0, "cp_out_proj", dep=ffn0_share[-1])
    w1g0, w2g0 = [unhalve(g) for g in _wait_copies("gather_ffn0_share_wait", ffn0_share, _plan_share, h1)[1]]
    h2, hp0, uf0 = _ffn_fwd(h1, row(ffn_norm_g[0]), w1g0, w2g0, "ffn0_fwd")
    gla_srcs, gla_lands, _ = arrived("gather_gla", gla_gather, h2)
    glain_g, glaout_g = [unhalve(g) for g in _share_with_sibling("gather_gla_share", gla_srcs, gla_lands)]
    w_gla_in = jnp.concatenate([glain_g[j] for j in range(N_CHIPS)] + [jnp.zeros((d, GLA_IN_PAD - GLA_IN), BF16)], axis=1)
    w_gla_out = glaout_g.reshape(GLA_DV, d)
    z1, u2 = _norm_matmul(h2, row(mix_norm_g[1]), w_gla_in, 640, "gla_in_proj")
    ffn1_srcs, ffn1_lands, _ = arrived("gather_ffn1", ffn1_gather, z1)
    ffn1_share = _start_copies("gather_ffn1_share_start", ffn1_srcs, ffn1_lands, _plan_share, 4 * len(ffn1_srcs))
    o1, mix1, states = _gla_seq_fwd(z1, gate_w_pad, gate_b_f, head_g_f, dep=ffn1_share[-1])
    h3 = _matmul_residual(mix1, w_gla_out, h2, "gla_out_proj")
    w1g1, w2g1 = [unhalve(g) for g in _wait_copies("gather_ffn1_share_wait", ffn1_share, _plan_share, h3)[1]]
    h4, hp1, uf1 = _ffn_fwd(h3, row(ffn_norm_g[1]), w1g1, w2g1, "ffn1_fwd")

    def start_exchange(name, grads):
        srcs = [_split_rows(g) for g in grads]
        lands = [lax.empty((g.shape[0], g.shape[1] // 2, g.shape[2]), g.dtype) for g in grads]
        return _start_copies(name + "_exchange_start", srcs, lands, _plan_exchange(len(grads)), len(grads))

    def start_scatter(name, exchange, after):
        srcs, recv = _wait_copies(name + "_exchange_wait", exchange, _plan_exchange(len(exchange[2])), after)
        parts = [_sum_halves(g.reshape(g.shape[0], -1, g.shape[3]), r, c_idx, "%s_chip_sum_%d" % (name, k))
                 for k, (g, r) in enumerate(zip(srcs, recv))]
        lands = [lax.empty(p.shape, p.dtype) for p in parts]
        return _start_copies(name + "_scatter_start", parts, lands, _plan_scatter(len(parts)), 3 * len(parts))

    def finish_reduce(name, started, after):
        n = len(started[2])
        parts, lands = _wait_copies(name + "_scatter_wait", started, _plan_scatter(n), after)
        return [_sum_own_and_slots(p, s, chip_idx, "%s_slot_sum_%d" % (name, k)) for k, (p, s) in enumerate(zip(parts, lands))]

    dh4, d_final_g, loss_part = _loss_bwd(h4, row(final_norm_g), loss_target[0])

    dh3, dhp1, d_ffn_g1 = _ffn_bwd_data(dh4, h3, row(ffn_norm_g[1]), hp1, w1g1, w2g1, "ffn1_bwd")
    dw1_1 = _wgrad(uf1, dhp1, N_CHIPS, d, d, False, True, False, "ffn1_dw1")
    dw2_1 = _wgrad(hp1, dh4, N_CHIPS, d, d, True, False, True, "ffn1_dw2")
    ffn1_exchange = start_exchange("ffn1", [dw1_1, dw2_1])

    dmix1 = _dgrad(dh3, w_gla_out, "gla_out_dgrad", dep=ffn1_exchange[-1])
    dw_gla_out = _wgrad(mix1, dh3, 1, GLA_DV, d, False, False, False, "gla_out_dw", dep=ffn1_exchange[-1])
    ffn1_reduce = start_scatter("ffn1", ffn1_exchange, [dmix1, dw_gla_out])
    dz1, d_gate_w, d_gate_b, d_head_g = _gla_seq_bwd(dmix1, o1, z1, states, gate_w_pad, gate_b_f, head_g_f,
                                                     dep=ffn1_reduce[-1])
    dh2, d_mix_g1 = _dgrad_norm_bwd(dz1, w_gla_in, h2, row(mix_norm_g[1]), dh3, 640, "gla_in_dgrad")
    dw_gla_in = _wgrad(u2, dz1, GLA_IN_PAD // 640, d, 640, False, True, False, "gla_in_dw")
    gla_in_shards = jnp.stack([_take_cols(dw_gla_in, j * (GLA_IN // N_CHIPS), GLA_IN // N_CHIPS) for j in range(N_CHIPS)])
    gla_exchange = start_exchange("gla", [gla_in_shards, dw_gla_out.reshape(N_CHIPS, -1, d)])

    dh1, dhp0, d_ffn_g0 = _ffn_bwd_data(dh2, h1, row(ffn_norm_g[0]), hp0, w1g0, w2g0, "ffn0_bwd", dep=gla_exchange[-1])
    gla_reduce = start_scatter("gla", gla_exchange, dh1)
    dw1_0 = _wgrad(uf0, dhp0, N_CHIPS, d, d, False, True, False, "ffn0_dw1", dep=gla_reduce[-1])
    dw2_0 = _wgrad(hp0, dh2, N_CHIPS, d, d, True, False, True, "ffn0_dw2")
    ffn0_exchange = start_exchange("ffn0", [dw1_0, dw2_0])

    dmix0 = _dgrad(dh1, w_cp_out, "cp_out_dgrad", dep=ffn0_exchange[-1])
    dw_cp_out = _wgrad(mix0, dh1, 1, CONV_DIM + POOL_DIM, d, False, False, False, "cp_out_dw", dep=ffn0_exchange[-1])
    ffn0_reduce = start_scatter("ffn0", ffn0_exchange, [dmix0, dw_cp_out])
    dz0, d_conv_w, d_cp_vec, d_pool_w = _cp_seq_bwd(dmix0, z0, c0, pm0, conv_w_f, cp_ln_g, cp_ln_b, cp_pool_w[0],
                                                    cp_pool_scale, dep=ffn0_reduce[-1])
    grad_x, dh0_head, d_mix_g0 = _dgrad_norm_bwd_input(dz0, w_cp_in, h0, row(mix_norm_g[0]), dh1, 512, "cp_in_dgrad")
    grad_x = grad_x[None]
    dw_cp_in = _wgrad(u0, dz0, N_CHIPS, d, CP_IN // N_CHIPS, False, True, False, "cp_in_dw")

    cp_grads = [dw_cp_in, dw_cp_out.reshape(N_CHIPS, -1, d)]
    small_full = [dh0_head[PAD_ROWS:CHUNK],jnp.concatenate([d_mix_g0, d_mix_g1], axis=0),
                  jnp.concatenate([d_ffn_g0, d_ffn_g1], axis=0), d_conv_w[:CONV_WIDTH][None],
                  d_cp_vec[0:1], d_cp_vec[1:2], d_cp_vec[2:3], d_pool_w[None], d_cp_vec[3:4],
                  d_gate_w[:GATE_RANK][None], d_gate_b, d_head_g, d_final_g[0], loss_part[0, 0:1]]
    small_mine = _pack(small_full)
    cp_exchange_plan = _plan_exchange(len(cp_grads))
    cp_exchange = _start_copies(
        "cp_exchange_start", [_split_rows(g) for g in cp_grads] + [small_mine],
        [lax.empty((g.shape[0], g.shape[1] // 2, g.shape[2]), F32) for g in cp_grads] + [lax.empty(small_mine.shape, F32)],
        cp_exchange_plan, len(cp_grads) + 1)
    red_ffn1 = finish_reduce("ffn1", ffn1_reduce, cp_exchange[-1])
    red_gla = finish_reduce("gla", gla_reduce, cp_exchange[-1])
    red_ffn0 = finish_reduce("ffn0", ffn0_reduce, cp_exchange[-1])
    cp_srcs, cp_recv = _wait_copies("cp_exchange_wait", cp_exchange, cp_exchange_plan, [red_ffn1[1], red_gla[1], red_ffn0[1]])
    chip_sums = [_sum_halves(g.reshape(g.shape[0], -1, g.shape[3]), r, c_idx, "cp_chip_sum_%d" % k)
                 for k, (g, r) in enumerate(zip(cp_srcs[:-1], cp_recv[:-1]))]
    small_chip = _add2(cp_srcs[-1], cp_recv[-1], "chip_sum_small")
    small_slots = lax.dynamic_update_slice(jnp.zeros((N_CHIPS,) + small_chip.shape, F32), small_chip[None], (chip, 0, 0))
    cp_lands = [lax.empty(p.shape, p.dtype) for p in chip_sums] + [small_slots]
    cp_reduce = _start_copies("cp_scatter_start", chip_sums + [small_chip], cp_lands, _plan_scatter(len(chip_sums)),
                              3 * (len(chip_sums) + 1))

    def adamw_big(names, grads):
        outs = {}
        for n, g in zip(names, grads):
            w, m, v = big[n]
            two_d = lambda a: a.reshape(-1, a.shape[-1])
            res = _adamw(two_d(w), two_d(g), two_d(m), two_d(v), "adamw_" + n)
            outs[n] = [o.reshape(w.shape) for o in res]
        return outs

    big = {"w1": (ffn_w1, m_ffn_w1, v_ffn_w1), "w2": (ffn_w2, m_ffn_w2, v_ffn_w2),
           "cp_in": (cp_w_in, m_cp_w_in, v_cp_w_in), "cp_out": (cp_w_out, m_cp_w_out, v_cp_w_out),
           "gla_in": (gla_w_in, m_gla_w_in, v_gla_w_in), "gla_out": (gla_w_out, m_gla_w_out, v_gla_w_out)}
    first =["w1", "w2", "gla_in", "gla_out"]
    first_grads = _join_halves([red_ffn0[0], red_ffn1[0], red_ffn0[1], red_ffn1[1], red_gla[0], red_gla[1]],
                               [(0, 0), (0, 1), (1, 0), (1, 1), (2, 0), (3, 0)], [big[n][0].shape for n in first],
                               "join_halves_ffn_gla")
    big_out = adamw_big(first, first_grads)
    cp_parts, cp_slots = _wait_copies("cp_scatter_wait", cp_reduce, _plan_scatter(len(chip_sums)), big_out["gla_out"][1])
    red_cp = [_sum_own_and_slots(a, s, chip_idx, "cp_slot_sum_%d" % k)
              for k, (a, s) in enumerate(zip(cp_parts[:-1], cp_slots[:-1]))]
    small_red = _sum_slots(cp_slots[-1], "slot_sum_small")
    last = ["cp_in", "cp_out"]
    last_grads = _join_halves(red_cp, [(0, 0), (1, 0)], [big[n][0].shape for n in last], "join_halves_cp")
    big_out.update(adamw_big(last, last_grads))

    (g_meta, g_mix, g_ffn, g_conv_w, g_conv_b, g_ln_g, g_ln_b, g_pool_w, g_pool_scale, g_gate_w, g_gate_b, g_head,
     g_final, loss_sum) = _unpack(small_red, [a.shape for a in small_full])
    g_meta = _col_shard(g_meta, chip, meta_tokens.shape[-1])
    g_conv_w = _col_shard(g_conv_w, chip, cp_conv_w.shape[-1])
    g_gate_w = _col_shard(g_gate_w, chip, gla_gate_w2.shape[-1])
    g_gate_b = _col_shard(g_gate_b, chip, gla_gate_b.shape[-1])
    g_head = _col_shard(g_head, chip, gla_head_g.shape[-1])
    small_w = [meta_tokens, mix_norm_g, ffn_norm_g, cp_conv_w, cp_conv_b, cp_ln_g, cp_ln_b, cp_pool_w, cp_pool_scale,
               gla_gate_w2, gla_gate_b, gla_head_g, final_norm_g]
    small_m = [m_meta_tokens, m_mix_norm_g, m_ffn_norm_g, m_cp_conv_w, m_cp_conv_b, m_cp_ln_g, m_cp_ln_b, m_cp_pool_w,
               m_cp_pool_scale, m_gla_gate_w2, m_gla_gate_b, m_gla_head_g, m_final_norm_g]
    small_v = [v_meta_tokens, v_mix_norm_g, v_ffn_norm_g, v_cp_conv_w, v_cp_conv_b, v_cp_ln_g, v_cp_ln_b, v_cp_pool_w,
               v_cp_pool_scale, v_gla_gate_w2, v_gla_gate_b, v_gla_head_g, v_final_norm_g]
    small_g = [g_meta, g_mix, g_ffn, g_conv_w, g_conv_b, g_ln_g, g_ln_b, g_pool_w, g_pool_scale, g_gate_w, g_gate_b,
               g_head, g_final]
    shapes = [w.shape for w in small_w]
    small_g = [g.reshape(s) for g, s in zip(small_g, shapes)]
    at_least_2d = lambda arrs: [a.reshape(1, -1) if a.ndim == 1 else a for a in arrs]
    s_delta, s_m, s_v = _adamw_many(at_least_2d(small_w), at_least_2d(small_g), at_least_2d(small_m), at_least_2d(small_v))
    s_delta, s_m, s_v = [[a.reshape(s) for a, s in zip(group, shapes)] for group in (s_delta, s_m, s_v)]

    order = ["meta", "mix", "ffn", "w1", "w2", "cp_in", "conv_w", "conv_b", "ln_g", "ln_b", "pool_w", "pool_scale",
             "cp_out", "gla_in", "gate_w", "gate_b", "head", "gla_out", "final"]
    small_names = ["meta", "mix", "ffn", "conv_w", "conv_b", "ln_g", "ln_b", "pool_w", "pool_scale", "gate_w", "gate_b",
                   "head", "final"]
    big_names = ["w1", "w2", "cp_in", "cp_out", "gla_in", "gla_out"]
    table = {n: (small_g[i], s_delta[i], s_m[i], s_v[i]) for i, n in enumerate(small_names)}
    table.update({n: tuple(big_out[n]) for n in big_names})
    loss = loss_sum.reshape(())
    return (loss, grad_x, *[table[n][0] for n in order], *[table[n][1] for n in order],
            *[table[n][2] for n in order], *[table[n][3] for n in order])
```

```python
import functools

import jax
import jax.numpy as jnp
from jax import lax
from jax.experimental import pallas as pl
from jax.experimental.pallas import tpu as pltpu

F32 = jnp.float32
BF16 = jnp.bfloat16

D_MODEL = 1024
N_META = 16
CHUNK = 64
PAD_ROWS = CHUNK - N_META
EPS = 1e-5
CONV_DIM = 512
CONV_WIDTH = 31
CONV_HALO = 32
POOL_DIM = 512
POOL_WINDOWS = (2, 4, 8, 16)
POOL_GROUP = 128
POOL_HALO = 16
CP_IN = 2 * CONV_DIM + POOL_DIM
GLA_HEADS = 4
GLA_DK = 512
GLA_DV = 1024
GLA_HK = GLA_DK // GLA_HEADS
GLA_HV = GLA_DV // GLA_HEADS
GATE_RANK = 16
GATE_PAD = 128
GATE_NORM = 16.0
GLA_IN = 2 * GLA_DK + 2 * GLA_DV + GATE_RANK
GLA_IN_PAD = 2 * GLA_DK + 2 * GLA_DV + GATE_PAD
N_CHIPS = 4
ADAM_LR = 0.001
ADAM_B1 = 0.9
ADAM_B2 = 0.999
ADAM_EPS = 1e-08
ADAM_WD = 0.01
ADAM_STEP = 10

VMEM_LIMIT_BYTES = 56 * 1024 * 1024
ROW_TILE_TARGET = 832
TOKEN_TILE_TARGET = 1040
PACK_WIDTH = 1024
MESH = pl.DeviceIdType.MESH
HBM_SPEC = pl.BlockSpec(memory_space=pltpu.HBM)
ANY_SPEC = pl.BlockSpec(memory_space=pl.ANY)
SEM_SPEC = pl.BlockSpec(memory_space=pltpu.SEMAPHORE)
SIDE_EFFECT = pltpu.SideEffectType.DATAFLOW_SIDE_EFFECTING


def _cparams(*sem):
    return pltpu.CompilerParams(dimension_semantics=sem, vmem_limit_bytes=VMEM_LIMIT_BYTES)


def _row_tile(t, target, mult):
    best = mult
    for cand in range(mult, min(t, target) + 1, mult):
        if t % cand == 0:
            best = cand
    assert t % best == 0, (t, best)
    return best


def _rms(h, g):
    return h * lax.rsqrt(jnp.mean(h * h, axis=-1, keepdims=True) + EPS) * g


def _rms_bwd(h, g, du):
    r = lax.rsqrt(jnp.mean(h * h, axis=-1, keepdims=True) + EPS)
    xhat = h * r
    dxh = du * g
    dh = r * (dxh - xhat * jnp.mean(dxh * xhat, axis=-1, keepdims=True))
    return dh, du * xhat


def _valid_rows(i, tm):
    row = i * tm + lax.broadcasted_iota(jnp.int32, (tm, 1), 0)
    return row >= PAD_ROWS


def _dot(a, b):
    return jnp.dot(a, b, preferred_element_type=F32)


def _dot_nt(a, b):
    return lax.dot_general(a, b, (((1,), (1,)), ((), ())), preferred_element_type=F32)


def _dot_tn(a, b):
    return lax.dot_general(a, b, (((0,), (0,)), ((), ())), preferred_element_type=F32)


def _accumulate(ref, val, first):
    @pl.when(first)
    def _():
        ref[...] = val

    @pl.when(jnp.logical_not(first))
    def _():
        ref[...] += val


def _call_after(dep, body, n_in, in_specs, args, **kw):
    if dep is None:
        return pl.pallas_call(body, in_specs=in_specs, **kw)(*args)

    def with_dep(*refs):
        body(*refs[:n_in], *refs[n_in + 1:])

    return pl.pallas_call(with_dep, in_specs=list(in_specs) + [ANY_SPEC], **kw)(*args, dep)


def _norm_matmul(h, g, w, nc, name, dep=None):
    t, d = h.shape
    n = w.shape[1]
    tm = _row_tile(t, TOKEN_TILE_TARGET, 16)

    def body(h_ref, g_ref, w_ref, z_ref, u_ref):
        u = _rms(h_ref[...], g_ref[...]).astype(BF16)
        u_ref[...] = u
        for n0 in range(0, n, nc):
            z_ref[:, n0:n0 + nc] = _dot(u, w_ref[:, n0:n0 + nc]).astype(BF16)

    return _call_after(
        dep, body, 3,
        [pl.BlockSpec((tm, d), lambda i: (i, 0)), pl.BlockSpec((1, d), lambda i: (0, 0)),
         pl.BlockSpec((d, n), lambda i: (0, 0))], (h, g, w), grid=(t // tm,),
        out_specs=[pl.BlockSpec((tm, n), lambda i: (i, 0)), pl.BlockSpec((tm, d), lambda i: (i, 0))],
        out_shape=[jax.ShapeDtypeStruct((t, n), BF16), jax.ShapeDtypeStruct((t, d), BF16)],
        compiler_params=_cparams("parallel"), name=name)


def _matmul_residual(a, w, h, name, dep=None):
    t, k = a.shape
    d = w.shape[1]
    tm = _row_tile(t, TOKEN_TILE_TARGET, 16)

    def body(a_ref, w_ref, h_ref, o_ref):
        o_ref[...] = h_ref[...] + _dot(a_ref[...], w_ref[...])

    return _call_after(
        dep, body, 3,
        [pl.BlockSpec((tm, k), lambda i: (i, 0)), pl.BlockSpec((k, d), lambda i: (0, 0)),
         pl.BlockSpec((tm, d), lambda i: (i, 0))], (a, w, h), grid=(t // tm,),
        out_specs=pl.BlockSpec((tm, d), lambda i: (i, 0)),
        out_shape=jax.ShapeDtypeStruct((t, d), F32),
        compiler_params=_cparams("parallel"), name=name)


def _ffn_fwd(h, g, w1g, w2g, name):
    t, d = h.shape
    ns, ffs = w1g.shape[0], w1g.shape[2]
    tm = _row_tile(t, TOKEN_TILE_TARGET, 16)

    def body(h_ref, g_ref, w1_ref, w2_ref, ho_ref, hp_ref, u_ref, acc_ref):
        s = pl.program_id(1)

        @pl.when(s == 0)
        def _():
            u_ref[...] = _rms(h_ref[...], g_ref[...]).astype(BF16)

        hp = _dot(u_ref[...], w1_ref[...])
        hp_ref[...] = hp.astype(BF16)
        a = jnp.maximum(hp, 0.0)
        _accumulate(acc_ref, _dot((a * a).astype(BF16), w2_ref[...]), s == 0)

        @pl.when(s == ns - 1)
        def _():
            ho_ref[...] = h_ref[...] + acc_ref[...]

    return pl.pallas_call(
        body, grid=(t // tm, ns),
        in_specs=[pl.BlockSpec((tm, d), lambda i, s: (i, 0)), pl.BlockSpec((1, d), lambda i, s: (0, 0)),
                  pl.BlockSpec((None, d, ffs), lambda i, s: (s, 0, 0)),
                  pl.BlockSpec((None, ffs, d), lambda i, s: (s, 0, 0))],
        out_specs=[pl.BlockSpec((tm, d), lambda i, s: (i, 0)), pl.BlockSpec((tm, ffs), lambda i, s: (i, s)),
                   pl.BlockSpec((tm, d), lambda i, s: (i, 0))],
        out_shape=[jax.ShapeDtypeStruct((t, d), F32), jax.ShapeDtypeStruct((t, ns * ffs), BF16),
                   jax.ShapeDtypeStruct((t, d), BF16)],
        scratch_shapes=[pltpu.VMEM((tm, d), F32)],
        compiler_params=_cparams("parallel", "arbitrary"), name=name)(h, g, w1g, w2g)


def _ffn_bwd_data(dh, h, g, hp, w1g, w2g, name, dep=None):
    t, d = h.shape
    ns, ffs = w1g.shape[0], w1g.shape[2]
    tm = _row_tile(t, ROW_TILE_TARGET, CHUNK)

    def body(dh_ref, h_ref, g_ref, hp_ref, w1_ref, w2_ref, dhi_ref, dhp_ref, dg_ref, acc_ref):
        i, s = pl.program_id(0), pl.program_id(1)
        da = _dot_nt(dh_ref[...].astype(BF16), w2_ref[...])
        dhp = (da * (2.0 * jnp.maximum(hp_ref[...].astype(F32), 0.0))).astype(BF16)
        dhp_ref[...] = dhp
        _accumulate(acc_ref, _dot_nt(dhp, w1_ref[...]), s == 0)

        @pl.when(s == ns - 1)
        def _():
            dhn, dgr = _rms_bwd(h_ref[...], g_ref[...], acc_ref[...])
            dhi_ref[...] = jnp.where(_valid_rows(i, tm), dh_ref[...] + dhn, 0.0)
            _accumulate(dg_ref, jnp.sum(dgr, axis=0, keepdims=True), i == 0)

    return _call_after(
        dep, body, 6,
        [pl.BlockSpec((tm, d), lambda i, s: (i, 0)), pl.BlockSpec((tm, d), lambda i, s: (i, 0)),
         pl.BlockSpec((1, d), lambda i, s: (0, 0)), pl.BlockSpec((tm, ffs), lambda i, s: (i, s)),
         pl.BlockSpec((None, d, ffs), lambda i, s: (s, 0, 0)),
         pl.BlockSpec((None, ffs, d), lambda i, s: (s, 0, 0))], (dh, h, g, hp, w1g, w2g), grid=(t // tm, ns),
        out_specs=[pl.BlockSpec((tm, d), lambda i, s: (i, 0)), pl.BlockSpec((tm, ffs), lambda i, s: (i, s)),
                   pl.BlockSpec((1, d), lambda i, s: (0, 0))],
        out_shape=[jax.ShapeDtypeStruct((t, d), F32), jax.ShapeDtypeStruct((t, ns * ffs), BF16),
                   jax.ShapeDtypeStruct((1, d), F32)],
        scratch_shapes=[pltpu.VMEM((tm, d), F32)],
        compiler_params=_cparams("arbitrary", "arbitrary"), name=name)


WGRAD_ROWS = 1024


def _wgrad(x, dy, nb, xc, yc, x_by_block, dy_by_block, relu2, name, dep=None):
    t = x.shape[0]
    tk = _row_tile(t - CHUNK, WGRAD_ROWS, CHUNK)

    def prep(xv):
        if relu2:
            xv = jnp.maximum(xv.astype(F32), 0.0)
            xv = xv * xv
        return xv.astype(BF16)

    def body(xh_ref, dyh_ref, x_ref, dy_ref, o_ref):
        k = pl.program_id(1)
        p = _dot_tn(prep(x_ref[...]), dy_ref[...].astype(BF16))

        @pl.when(k == 0)
        def _():
            o_ref[...] = p + _dot_tn(prep(xh_ref[...]), dyh_ref[...].astype(BF16))

        @pl.when(k > 0)
        def _():
            o_ref[...] += p

    def head(width, by_block):
        return pl.BlockSpec((CHUNK, width), (lambda b, k: (0, b)) if by_block else (lambda b, k: (0, 0)))

    def rest(width, by_block):
        def index(b, k):
            return pl.multiple_of(CHUNK + k * tk, CHUNK), (pl.multiple_of(b * width, 128) if by_block else 0)
        return pl.BlockSpec((pl.Element(tk), pl.Element(width)), index)

    return _call_after(
        dep, body, 4,
        [head(xc, x_by_block), head(yc, dy_by_block), rest(xc, x_by_block), rest(yc, dy_by_block)], (x, dy, x, dy),
        grid=(nb, (t - CHUNK) // tk),
        out_specs=pl.BlockSpec((None, xc, yc), lambda b, k: (b, 0, 0)),
        out_shape=jax.ShapeDtypeStruct((nb, xc, yc), F32),
        compiler_params=_cparams("parallel", "arbitrary"), name=name)


def _dgrad(dh, w, name, dep=None):
    t, d = dh.shape
    k = w.shape[0]
    tm = _row_tile(t, TOKEN_TILE_TARGET, 16)

    def body(dh_ref, w_ref, o_ref):
        o_ref[...] = _dot_nt(dh_ref[...].astype(BF16), w_ref[...]).astype(BF16)

    return _call_after(
        dep, body, 2,
        [pl.BlockSpec((tm, d), lambda i: (i, 0)), pl.BlockSpec((k, d), lambda i: (0, 0))], (dh, w), grid=(t // tm,),
        out_specs=pl.BlockSpec((tm, k), lambda i: (i, 0)),
        out_shape=jax.ShapeDtypeStruct((t, k), BF16),
        compiler_params=_cparams("parallel"), name=name)


def _dgrad_norm_bwd(dz, w, h, g, dh, nc, name):
    t, d = h.shape
    n = w.shape[1]
    tm = _row_tile(t, ROW_TILE_TARGET // 2, 16)

    def body(dz_ref, w_ref, h_ref, g_ref, dh_ref, dhi_ref, dg_ref):
        i = pl.program_id(0)
        du = jnp.zeros((tm, d), F32)
        for n0 in range(0, n, nc):
            du = du + _dot_nt(dz_ref[:, n0:n0 + nc], w_ref[:, n0:n0 + nc])
        dhn, dgr = _rms_bwd(h_ref[...], g_ref[...], du)
        dhi_ref[...] = jnp.where(_valid_rows(i, tm), dh_ref[...] + dhn, 0.0)
        _accumulate(dg_ref, jnp.sum(dgr, axis=0, keepdims=True), i == 0)

    return pl.pallas_call(
        body, grid=(t // tm,),
        in_specs=[pl.BlockSpec((tm, n), lambda i: (i, 0)), pl.BlockSpec((d, n), lambda i: (0, 0)),
                  pl.BlockSpec((tm, d), lambda i: (i, 0)), pl.BlockSpec((1, d), lambda i: (0, 0)),
                  pl.BlockSpec((tm, d), lambda i: (i, 0))],
        out_specs=[pl.BlockSpec((tm, d), lambda i: (i, 0)), pl.BlockSpec((1, d), lambda i: (0, 0))],
        out_shape=[jax.ShapeDtypeStruct((t, d), F32), jax.ShapeDtypeStruct((1, d), F32)],
        compiler_params=_cparams("arbitrary"), name=name)(dz, w, h, g, dh)


def _dgrad_norm_bwd_input(dz, w, h, g, dh, nc, name):
    t, d = h.shape
    n = w.shape[1]
    tl = _row_tile(t - CHUNK, 512, CHUNK)

    def grads(dz_ref, w_ref, h_ref, g_ref, dh_ref, rows):
        du = jnp.zeros((rows, d), F32)
        for n0 in range(0, n, nc):
            du = du + _dot_nt(dz_ref[:, n0:n0 + nc], w_ref[:, n0:n0 + nc])
        dhn, dgr = _rms_bwd(h_ref[...], g_ref[...], du)
        return dh_ref[...] + dhn, jnp.sum(dgr, axis=0, keepdims=True)

    def rest_body(dz_ref, w_ref, h_ref, g_ref, dh_ref, dg_head_ref, dx_ref, dg_ref):
        dx, dg = grads(dz_ref, w_ref, h_ref, g_ref, dh_ref, tl)
        dx_ref[...] = dx

        @pl.when(pl.program_id(0) == 0)
        def _():
            dg_ref[...] = dg_head_ref[...] + dg

        @pl.when(pl.program_id(0) > 0)
        def _():
            dg_ref[...] += dg

    def head_body(dz_ref, w_ref, h_ref, g_ref, dh_ref, dx_ref, dg_ref):
        dx, dg = grads(dz_ref, w_ref, h_ref, g_ref, dh_ref, CHUNK)
        dx_ref[...] = jnp.where(_valid_rows(0, CHUNK), dx, 0.0)
        dg_ref[...] = dg

    def shifted(width):
        return pl.BlockSpec((pl.Element(tl), pl.Element(width)), lambda i: (pl.multiple_of(CHUNK + i * tl, CHUNK), 0))

    whole = [pl.BlockSpec((d, n), lambda i: (0, 0)), pl.BlockSpec((1, d), lambda i: (0, 0))]
    head = lambda width: pl.BlockSpec((CHUNK, width), lambda i: (0, 0))
    dh_head, dg_head = pl.pallas_call(
        head_body, grid=(1,), in_specs=[head(n), whole[0], head(d), whole[1], head(d)],
        out_specs=[head(d), whole[1]],
        out_shape=[jax.ShapeDtypeStruct((CHUNK, d), F32), jax.ShapeDtypeStruct((1, d), F32)],
        compiler_params=_cparams("arbitrary"), name=name + "_head")(dz, w, h, g, dh)
    dx, dg = pl.pallas_call(
        rest_body, grid=((t - CHUNK) // tl,),
        in_specs=[shifted(n), whole[0], shifted(d), whole[1], shifted(d), whole[1]],
        out_specs=[pl.BlockSpec((tl, d), lambda i: (i, 0)), whole[1]],
        out_shape=[jax.ShapeDtypeStruct((t - CHUNK, d), F32), jax.ShapeDtypeStruct((1, d), F32)],
        compiler_params=_cparams("arbitrary"), name=name)(dz, w, h, g, dh, dg_head)
    return dx, dh_head, dg


def _loss_bwd(h, g, target):
    t, d = h.shape
    tl = _row_tile(t - CHUNK, 1024, CHUNK)

    def body(h_ref, g_ref, t_ref, dh_ref, dg_ref, loss_ref):
        i = pl.program_id(0)
        hv, gv = h_ref[...], g_ref[...]
        err = _rms(hv, gv) - t_ref[...]
        part = 0.5 * jnp.sum(jnp.mean(err * err, axis=-1, keepdims=True), axis=0, keepdims=True)
        dhn, dgr = _rms_bwd(hv, gv, err * (1.0 / d))
        dh_ref[...] = dhn
        _accumulate(dg_ref, jnp.sum(dgr, axis=0, keepdims=True), i == 0)
        _accumulate(loss_ref, jnp.broadcast_to(part, (8, 128)), i == 0)

    shifted = pl.BlockSpec((pl.Element(tl), pl.Element(d)), lambda i: (pl.multiple_of(CHUNK + i * tl, CHUNK), 0))
    dh, dg, loss = pl.pallas_call(
        body, grid=((t - CHUNK) // tl,),
        in_specs=[shifted, pl.BlockSpec((1, d), lambda i: (0, 0)), pl.BlockSpec((tl, d), lambda i: (i, 0))],
        out_specs=[shifted, pl.BlockSpec((1, d), lambda i: (0, 0)), pl.BlockSpec((8, 128), lambda i: (0, 0))],
        out_shape=[jax.ShapeDtypeStruct((t, d), F32), jax.ShapeDtypeStruct((1, d), F32),
                   jax.ShapeDtypeStruct((8, 128), F32)],
        compiler_params=_cparams("arbitrary"), name="loss_bwd")(h, g, target)

    def zero_head(dh_ref, o_ref):
        o_ref[...] = jnp.zeros_like(o_ref)

    dh = pl.pallas_call(
        zero_head, grid=(1,), in_specs=[ANY_SPEC], out_specs=pl.BlockSpec((CHUNK, d), lambda i: (0, 0)),
        out_shape=jax.ShapeDtypeStruct((t, d), F32), input_output_aliases={0: 0}, name="loss_bwd_head")(dh)
    return dh, dg, loss


CONV_BLOCK = 32


def _silu(x):
    return x * jax.nn.sigmoid(x)


def _row_shifts(win):
    n = win.shape[0]
    return [win] + [pltpu.roll(win, n - j, 0) for j in range(1, 8)]


def _cp_seq_fwd(z, conv_w, conv_b, ln_g, ln_b, pool_w, pool_scale):
    t = z.shape[0]
    tm = _row_tile(t, ROW_TILE_TARGET, CHUNK)

    def body(z_ref, cw_ref, cb_ref, lg_ref, lb_ref, pw_ref, ps_ref, c_ref, pm_ref, mix_ref, gbuf, pbuf):
        i = pl.program_id(0)

        @pl.when(i == 0)
        def _():
            gbuf[0:CONV_HALO, :] = jnp.zeros((CONV_HALO, CONV_DIM), F32)
            pbuf[0:POOL_HALO, :] = jnp.zeros((POOL_HALO, POOL_DIM), F32)

        @pl.when(i > 0)
        def _():
            gbuf[0:CONV_HALO, :] = gbuf[tm:tm + CONV_HALO, :]
            pbuf[0:POOL_HALO, :] = pbuf[tm:tm + POOL_HALO, :]

        av = z_ref[:, 0:CONV_DIM].astype(F32)
        ag = z_ref[:, CONV_DIM:2 * CONV_DIM].astype(F32)
        gbuf[CONV_HALO:CONV_HALO + tm, :] = av * jax.nn.sigmoid(ag)
        pbuf[POOL_HALO:POOL_HALO + tm, :] = z_ref[:, 2 * CONV_DIM:CP_IN].astype(F32)

        def conv_block(rb, carry):
            base = pl.multiple_of(rb * CONV_BLOCK, CONV_BLOCK)
            shifted = _row_shifts(gbuf[pl.ds(base, CONV_BLOCK + CONV_HALO), :])
            acc = jnp.zeros((CONV_BLOCK, CONV_DIM), F32)
            for k in range(CONV_WIDTH):
                whole, part = divmod(CONV_HALO - (CONV_WIDTH - 1) + k, 8)
                acc = acc + cw_ref[k:k + 1, :] * shifted[part][8 * whole:8 * whole + CONV_BLOCK, :]
            c_ref[pl.ds(base, CONV_BLOCK), :] = acc + cb_ref[...]
            return carry

        lax.fori_loop(0, tm // CONV_BLOCK, conv_block, 0)

        c = c_ref[...]
        mu = jnp.mean(c, axis=-1, keepdims=True)
        xc = c - mu
        ln = xc * lax.rsqrt(jnp.mean(xc * xc, axis=-1, keepdims=True) + EPS) * lg_ref[...] + lb_ref[...]
        row = i * tm + lax.broadcasted_iota(jnp.int32, (tm, 1), 0)
        mix_ref[:, 0:CONV_DIM] = jnp.where(row >= PAD_ROWS, _silu(ln), 0.0).astype(BF16)

        tpos = (row - PAD_ROWS + 1).astype(F32)
        for gi, wdw in enumerate(POOL_WINDOWS):
            lo = POOL_GROUP * gi
            cur = pbuf[POOL_HALO:POOL_HALO + tm, lo:lo + POOL_GROUP]
            sacc = cur
            for j in range(1, wdw):
                sacc = sacc + pbuf[POOL_HALO - j:POOL_HALO - j + tm, lo:lo + POOL_GROUP]
            pm = (sacc / jnp.clip(tpos, 1.0, float(wdw)) - cur).astype(BF16)
            pm_ref[:, lo:lo + POOL_GROUP] = pm
            pg = _dot(pm, pw_ref[gi].astype(BF16))
            mix_ref[:, CONV_DIM + lo:CONV_DIM + lo + POOL_GROUP] = (pg * ps_ref[:, lo:lo + POOL_GROUP]).astype(BF16)

    vec = pl.BlockSpec((1, CONV_DIM), lambda i: (0, 0))
    return pl.pallas_call(
        body, grid=(t // tm,),
        in_specs=[pl.BlockSpec((tm, CP_IN), lambda i: (i, 0)),
                  pl.BlockSpec((CONV_WIDTH, CONV_DIM), lambda i: (0, 0)), vec, vec, vec,
                  pl.BlockSpec((len(POOL_WINDOWS), POOL_GROUP, POOL_GROUP), lambda i: (0, 0, 0)), vec],
        out_specs=[pl.BlockSpec((tm, CONV_DIM), lambda i: (i, 0)), pl.BlockSpec((tm, POOL_DIM), lambda i: (i, 0)),
                   pl.BlockSpec((tm, CONV_DIM + POOL_DIM), lambda i: (i, 0))],
        out_shape=[jax.ShapeDtypeStruct((t, CONV_DIM), F32), jax.ShapeDtypeStruct((t, POOL_DIM), BF16),
                   jax.ShapeDtypeStruct((t, CONV_DIM + POOL_DIM), BF16)],
        scratch_shapes=[pltpu.VMEM((tm + CONV_HALO, CONV_DIM), F32), pltpu.VMEM((tm + POOL_HALO, POOL_DIM), F32)],
        compiler_params=_cparams("arbitrary"), name="cp_seq_fwd")(z, conv_w, conv_b, ln_g, ln_b, pool_w, pool_scale)


def _cp_seq_bwd(dmix, z, c, pm, conv_w, ln_g, ln_b, pool_w, pool_scale, dep=None):
    t = z.shape[0]
    tm = _row_tile(t, ROW_TILE_TARGET, CHUNK)
    nt = t // tm

    def body(dmix_ref, z_ref, c_ref, pm_ref, cw_ref, lg_ref, lb_ref, pw_ref, ps_ref,
             dz_ref, dcw_ref, dvec_ref, dpw_ref, dcbuf, qbuf, glu_buf, dwacc):
        i = pl.program_id(0)
        tile = nt - 1 - i

        @pl.when(i == 0)
        def _():
            dcbuf[tm:tm + CONV_HALO, :] = jnp.zeros((CONV_HALO, CONV_DIM), F32)
            qbuf[tm:tm + POOL_HALO, :] = jnp.zeros((POOL_HALO, POOL_DIM), F32)
            dcw_ref[...] = jnp.zeros_like(dcw_ref)
            dwacc[...] = jnp.zeros_like(dwacc)
            dvec_ref[...] = jnp.zeros_like(dvec_ref)
            dpw_ref[...] = jnp.zeros_like(dpw_ref)

        @pl.when(i > 0)
        def _():
            dcbuf[tm:tm + CONV_HALO, :] = dcbuf[0:CONV_HALO, :]
            qbuf[tm:tm + POOL_HALO, :] = qbuf[0:POOL_HALO, :]

        row = tile * tm + lax.broadcasted_iota(jnp.int32, (tm, 1), 0)
        cv = c_ref[...]
        mu = jnp.mean(cv, axis=-1, keepdims=True)
        xc = cv - mu
        rstd = lax.rsqrt(jnp.mean(xc * xc, axis=-1, keepdims=True) + EPS)
        xhat = xc * rstd
        ln = xhat * lg_ref[...] + lb_ref[...]
        sg = jax.nn.sigmoid(ln)
        da = jnp.where(row >= PAD_ROWS, dmix_ref[:, 0:CONV_DIM].astype(F32), 0.0)
        dln = da * (sg * (1.0 + ln * (1.0 - sg)))
        dxh = dln * lg_ref[...]
        dc = rstd * (dxh - jnp.mean(dxh, axis=-1, keepdims=True) - xhat * jnp.mean(dxh * xhat, axis=-1, keepdims=True))
        dcbuf[0:tm, :] = dc
        dvec_ref[0:1, :] += jnp.sum(dc, axis=0, keepdims=True)
        dvec_ref[1:2, :] += jnp.sum(dln * xhat, axis=0, keepdims=True)
        dvec_ref[2:3, :] += jnp.sum(dln, axis=0, keepdims=True)

        av = z_ref[:, 0:CONV_DIM].astype(F32)
        sig_g = jax.nn.sigmoid(z_ref[:, CONV_DIM:2 * CONV_DIM].astype(F32))
        glu_buf[...] = av * sig_g

        def conv_block(rb, carry):
            base = pl.multiple_of(rb * CONV_BLOCK, CONV_BLOCK)
            shifted = _row_shifts(dcbuf[pl.ds(base, CONV_BLOCK + CONV_HALO), :])
            glu = glu_buf[pl.ds(base, CONV_BLOCK), :]
            acc = jnp.zeros((CONV_BLOCK, CONV_DIM), F32)
            for k in range(CONV_WIDTH):
                whole, part = divmod(CONV_WIDTH - 1 - k, 8)
                slab = shifted[part][8 * whole:8 * whole + CONV_BLOCK, :]
                acc = acc + cw_ref[k:k + 1, :] * slab
                prod = slab * glu
                part = prod[0:8]
                for q in range(1, CONV_BLOCK // 8):
                    part = part + prod[8 * q:8 * q + 8]
                dwacc[k] += part
            glu_buf[pl.ds(base, CONV_BLOCK), :] = acc
            return carry

        lax.fori_loop(0, tm // CONV_BLOCK, conv_block, 0)

        @pl.when(i == nt - 1)
        def _():
            for k in range(CONV_WIDTH):
                dcw_ref[k:k + 1, :] = jnp.sum(dwacc[k], axis=0, keepdims=True)
        dglu = glu_buf[...]
        dz_ref[:, 0:CONV_DIM] = (dglu * sig_g).astype(BF16)
        dz_ref[:, CONV_DIM:2 * CONV_DIM] = (dglu * av * sig_g * (1.0 - sig_g)).astype(BF16)

        tpos = (row - PAD_ROWS + 1).astype(F32)
        for gi, wdw in enumerate(POOL_WINDOWS):
            lo = POOL_GROUP * gi
            dp = dmix_ref[:, CONV_DIM + lo:CONV_DIM + lo + POOL_GROUP].astype(F32)
            pmv = pm_ref[:, lo:lo + POOL_GROUP]
            pwb = pw_ref[gi].astype(BF16)
            dvec_ref[3:4, lo:lo + POOL_GROUP] += jnp.sum(dp * _dot(pmv, pwb), axis=0, keepdims=True)
            dq = (dp * ps_ref[:, lo:lo + POOL_GROUP]).astype(BF16)
            dpw_ref[gi] += _dot_tn(pmv, dq)
            dpm = _dot_nt(dq, pwb)
            qbuf[0:tm, lo:lo + POOL_GROUP] = dpm / jnp.clip(tpos, 1.0, float(wdw))
            sacc = -dpm
            for j in range(wdw):
                sacc = sacc + qbuf[j:j + tm, lo:lo + POOL_GROUP]
            dz_ref[:, 2 * CONV_DIM + lo:2 * CONV_DIM + lo + POOL_GROUP] = sacc.astype(BF16)

    vec = pl.BlockSpec((1, CONV_DIM), lambda i: (0, 0))
    rev = lambda i: (nt - 1 - i, 0)
    return _call_after(
        dep, body, 9,
        [pl.BlockSpec((tm, CONV_DIM + POOL_DIM), rev), pl.BlockSpec((tm, CP_IN), rev),
         pl.BlockSpec((tm, CONV_DIM), rev), pl.BlockSpec((tm, POOL_DIM), rev),
         pl.BlockSpec((CONV_WIDTH, CONV_DIM), lambda i: (0, 0)), vec, vec,
         pl.BlockSpec((len(POOL_WINDOWS), POOL_GROUP, POOL_GROUP), lambda i: (0, 0, 0)), vec],
        (dmix, z, c, pm, conv_w, ln_g, ln_b, pool_w, pool_scale), grid=(nt,),
        out_specs=[pl.BlockSpec((tm, CP_IN), rev), pl.BlockSpec((CONV_WIDTH + 1, CONV_DIM), lambda i: (0, 0)),
                   pl.BlockSpec((8, CONV_DIM), lambda i: (0, 0)),
                   pl.BlockSpec((len(POOL_WINDOWS), POOL_GROUP, POOL_GROUP), lambda i: (0, 0, 0))],
        out_shape=[jax.ShapeDtypeStruct((t, CP_IN), BF16), jax.ShapeDtypeStruct((CONV_WIDTH + 1, CONV_DIM), F32),
                   jax.ShapeDtypeStruct((8, CONV_DIM), F32),
                   jax.ShapeDtypeStruct((len(POOL_WINDOWS), POOL_GROUP, POOL_GROUP), F32)],
        scratch_shapes=[pltpu.VMEM((tm + CONV_HALO, CONV_DIM), F32), pltpu.VMEM((tm + POOL_HALO, POOL_DIM), F32),
                        pltpu.VMEM((tm, CONV_DIM), F32), pltpu.VMEM((CONV_WIDTH + 1, 8, CONV_DIM), F32)],
        compiler_params=_cparams("arbitrary"), name="cp_seq_bwd")


GLA_UNROLL = 2
Q0, K0, V0, G0, R0 =0, GLA_DK, 2 * GLA_DK, 2 * GLA_DK + GLA_DV, 2 * GLA_DK + 2 * GLA_DV


def _split3(x):
    hi = x.astype(BF16)
    r1 = x - hi.astype(F32)
    mid = r1.astype(BF16)
    lo = (r1 - mid.astype(F32)).astype(BF16)
    return hi, mid, lo


def _tri(strict):
    r = lax.broadcasted_iota(jnp.int32, (CHUNK, CHUNK), 0)
    c = lax.broadcasted_iota(jnp.int32, (CHUNK, CHUNK), 1)
    return ((r > c) if strict else (r >= c)).astype(BF16)


def _chunk_sums(x, cpt, strict, pieces):
    tri3 = jnp.broadcast_to(_tri(strict)[None], (cpt, CHUNK, CHUNK))
    acc = None
    for piece in _split3(x.reshape(cpt, CHUNK, x.shape[-1]))[:pieces]:
        part = jnp.einsum("bij,bjk->bik", tri3, piece, preferred_element_type=F32)
        acc = part if acc is None else acc + part
    return acc


def _chunk_decay(r, gw_ref, gb_ref, cpt):
    pre = _dot(r, gw_ref[...]) + gb_ref[...]
    lac = (jnp.minimum(pre, 0.0) - jnp.log(1.0 + jnp.exp(-jnp.abs(pre)))) * (1.0 / GATE_NORM)
    cum3 = _chunk_sums(lac, cpt, False, 3)
    return cum3, cum3[:, CHUNK - 1:CHUNK, :]


def _gla_seq_fwd(z, gate_w, gate_b, head_g, dep=None):
    t = z.shape[0]
    tm = _row_tile(t, ROW_TILE_TARGET, CHUNK)
    cpt = tm // CHUNK
    scale = GLA_HK ** -0.5

    def body(z_ref, gw_ref, gb_ref, hg_ref, o_ref, mix_ref, st_ref, state, kdec_s, e_s):
        @pl.when(pl.program_id(0) == 0)
        def _():
            state[...] = jnp.zeros_like(state)

        cum3, tot3 = _chunk_decay(z_ref[:, R0:R0 + GATE_PAD], gw_ref, gb_ref, cpt)
        dec = jnp.exp(jnp.broadcast_to(tot3, cum3.shape) - cum3).reshape(tm, GLA_DK)
        kdec_s[...] = (z_ref[:, K0:K0 + GLA_DK].astype(F32) * dec).astype(BF16)
        e_s[...] = jnp.exp(jnp.broadcast_to(tot3, (cpt, 8, GLA_DK))).reshape(cpt * 8, GLA_DK)

        def chunk(ci, carry):
            rows = pl.ds(pl.multiple_of(ci * CHUNK, CHUNK), CHUNK)
            e_all = e_s[pl.ds(pl.multiple_of(ci * 8, 8), 8), :][0:1, :]
            st_ref[ci] = state[...].astype(BF16)
            for hd in range(GLA_HEADS):
                ks = slice(hd * GLA_HK, (hd + 1) * GLA_HK)
                vs = slice(hd * GLA_HV, (hd + 1) * GLA_HV)
                v = z_ref[rows, V0 + hd * GLA_HV:V0 + (hd + 1) * GLA_HV]
                st = state[vs, :] * e_all[:, ks] + _dot_tn(v, kdec_s[rows, ks])
                state[vs, :] = st
                q = z_ref[rows, Q0 + hd * GLA_HK:Q0 + (hd + 1) * GLA_HK]
                o_ref[rows, vs] = (_dot_nt(q, st.astype(BF16)) * scale).astype(BF16)
            return carry

        lax.fori_loop(0, cpt, chunk, 0, unroll=GLA_UNROLL)

        for hd in range(GLA_HEADS):
            vs = slice(hd * GLA_HV, (hd + 1) * GLA_HV)
            on = _rms(o_ref[:, vs].astype(F32), hg_ref[...])
            gv = z_ref[:, G0 + hd * GLA_HV:G0 + (hd + 1) * GLA_HV].astype(F32)
            mix_ref[:, vs] = (on * _silu(gv)).astype(BF16)

    return _call_after(
        dep, body, 4,
        [pl.BlockSpec((tm, GLA_IN_PAD), lambda i: (i, 0)),
         pl.BlockSpec((GATE_PAD, GLA_DK), lambda i: (0, 0)), pl.BlockSpec((1, GLA_DK), lambda i: (0, 0)),
         pl.BlockSpec((1, GLA_HV), lambda i: (0, 0))], (z, gate_w, gate_b, head_g), grid=(t // tm,),
        out_specs=[pl.BlockSpec((tm, GLA_DV), lambda i: (i, 0)), pl.BlockSpec((tm, GLA_DV), lambda i: (i, 0)),
                   pl.BlockSpec((cpt, GLA_DV, GLA_HK), lambda i: (i, 0, 0))],
        out_shape=[jax.ShapeDtypeStruct((t, GLA_DV), BF16), jax.ShapeDtypeStruct((t, GLA_DV), BF16),
                   jax.ShapeDtypeStruct((t // CHUNK, GLA_DV, GLA_HK), BF16)],
        scratch_shapes=[pltpu.VMEM((GLA_DV, GLA_HK), F32), pltpu.VMEM((tm, GLA_DK), BF16),
                        pltpu.VMEM((cpt * 8, GLA_DK), F32)],
        compiler_params=_cparams("arbitrary"), name="gla_seq_fwd")


def _gla_seq_bwd(dmix, o, z, states, gate_w, gate_b, head_g, dep=None):
    t = z.shape[0]
    tm = _row_tile(t, ROW_TILE_TARGET, CHUNK)
    cpt = tm // CHUNK
    nt = t // tm
    scale = GLA_HK ** -0.5

    def body(dmix_ref, o_ref, z_ref, st_ref, gw_ref, gb_ref, hg_ref, dz_ref, dgw_ref, dgb_ref, dhg_ref,
             dstate, dec_s, kdec_s, dkdec_s, do_s, e_s, dtot_s):
        @pl.when(pl.program_id(0) == 0)
        def _():
            dstate[...] = jnp.zeros_like(dstate)
            dgw_ref[...] = jnp.zeros_like(dgw_ref)
            dgb_ref[...] = jnp.zeros_like(dgb_ref)
            dhg_ref[...] = jnp.zeros_like(dhg_ref)

        cum3, tot3 = _chunk_decay(z_ref[:, R0:R0 + GATE_PAD], gw_ref, gb_ref, cpt)
        dec = jnp.exp(jnp.broadcast_to(tot3, cum3.shape) - cum3).reshape(tm, GLA_DK)
        dec_s[...] = dec
        kdec_s[...] = z_ref[:, K0:K0 + GLA_DK].astype(F32) * dec
        e_s[...] = jnp.exp(jnp.broadcast_to(tot3, (cpt, 8, GLA_DK))).reshape(cpt * 8, GLA_DK)
        dhg = jnp.zeros((1, GLA_HV), F32)
        for hd in range(GLA_HEADS):
            vs = slice(hd * GLA_HV, (hd + 1) * GLA_HV)
            gcols = slice(G0 + hd * GLA_HV, G0 + (hd + 1) * GLA_HV)
            ov = o_ref[:, vs].astype(F32)
            gv = z_ref[:, gcols].astype(F32)
            dm = dmix_ref[:, vs].astype(F32)
            sg = jax.nn.sigmoid(gv)
            rr = lax.rsqrt(jnp.mean(ov * ov, axis=-1, keepdims=True) + EPS)
            xhat = ov * rr
            don = dm * (gv * sg)
            dz_ref[:, gcols] = (dm * (xhat * hg_ref[...]) * (sg * (1.0 + gv * (1.0 - sg)))).astype(BF16)
            dhg = dhg + jnp.sum(don * xhat, axis=0, keepdims=True)
            dxh = don * hg_ref[...]
            do_s[:, vs] = (rr * (dxh - xhat * jnp.mean(dxh * xhat, axis=-1, keepdims=True)) * scale).astype(BF16)
        dhg_ref[...] += dhg

        def chunk(cj, carry):
            ci = cpt - 1 - cj
            rows = pl.ds(pl.multiple_of(ci * CHUNK, CHUNK), CHUNK)
            erows = pl.ds(pl.multiple_of(ci * 8, 8), 8)
            e_all = e_s[erows, :][0:1, :]
            for hd in range(GLA_HEADS):
                ks = slice(hd * GLA_HK, (hd + 1) * GLA_HK)
                vs = slice(hd * GLA_HV, (hd + 1) * GLA_HV)
                e = e_all[:, ks]
                kdb = kdec_s[rows, ks].astype(BF16)
                v = z_ref[rows, V0 + hd * GLA_HV:V0 + (hd + 1) * GLA_HV]
                q = z_ref[rows, Q0 + hd * GLA_HK:Q0 + (hd + 1) * GLA_HK]
                do = do_s[rows, vs]
                st_prev = st_ref[ci, vs, :].astype(F32)
                st = st_prev * e + _dot_tn(v, kdb)
                dz_ref[rows, Q0 + hd * GLA_HK:Q0 + (hd + 1) * GLA_HK] = _dot(do, st.astype(BF16)).astype(BF16)
                dst = dstate[vs, :] + _dot_tn(do, q)
                dstb = dst.astype(BF16)
                dkdec_s[rows, ks] = _dot(v, dstb)
                dz_ref[rows, V0 + hd * GLA_HV:V0 + (hd + 1) * GLA_HV] = _dot_nt(kdb, dstb).astype(BF16)
                dtot = jnp.sum(dst * st_prev, axis=0, keepdims=True) * e
                dtot_s[erows, ks] = jnp.broadcast_to(dtot, (8, GLA_HK))
                dstate[vs, :] = dst * e
            return carry

        lax.fori_loop(0, cpt, chunk, 0, unroll=GLA_UNROLL)

        dkdec = dkdec_s[...]
        dz_ref[:, K0:K0 + GLA_DK] = (dkdec * dec_s[...]).astype(BF16)
        before = _chunk_sums(dkdec * kdec_s[...], cpt, True, 2)
        dtot3 = dtot_s[...].reshape(cpt, 8, GLA_DK)[:, 0:1, :]
        dlac = (jnp.broadcast_to(dtot3, before.shape) + before).reshape(tm, GLA_DK)
        pre = _dot(z_ref[:, R0:R0 + GATE_PAD], gw_ref[...]) + gb_ref[...]
        dpre = dlac * (1.0 / GATE_NORM) * (1.0 - jax.nn.sigmoid(pre))
        dpb = dpre.astype(BF16)
        dz_ref[:, R0:R0 + GATE_PAD] = _dot_nt(dpb, gw_ref[...]).astype(BF16)
        dgw_ref[...] += _dot_tn(z_ref[:, R0:R0 + GATE_PAD], dpb)
        dgb_ref[...] += jnp.sum(dpre, axis=0, keepdims=True)

    rev = lambda i: (nt - 1 - i, 0)
    return _call_after(
        dep, body, 7,
        [pl.BlockSpec((tm, GLA_DV), rev), pl.BlockSpec((tm, GLA_DV), rev), pl.BlockSpec((tm, GLA_IN_PAD), rev),
         pl.BlockSpec((cpt, GLA_DV, GLA_HK), lambda i: (nt - 1 - i, 0, 0)),
         pl.BlockSpec((GATE_PAD, GLA_DK), lambda i: (0, 0)), pl.BlockSpec((1, GLA_DK), lambda i: (0, 0)),
         pl.BlockSpec((1, GLA_HV), lambda i: (0, 0))],
        (dmix, o, z, states, gate_w, gate_b, head_g), grid=(nt,),
        out_specs=[pl.BlockSpec((tm, GLA_IN_PAD), rev), pl.BlockSpec((GATE_PAD, GLA_DK), lambda i: (0, 0)),
                   pl.BlockSpec((1, GLA_DK), lambda i: (0, 0)), pl.BlockSpec((1, GLA_HV), lambda i: (0, 0))],
        out_shape=[jax.ShapeDtypeStruct((t, GLA_IN_PAD), BF16), jax.ShapeDtypeStruct((GATE_PAD, GLA_DK), F32),
                   jax.ShapeDtypeStruct((1, GLA_DK), F32), jax.ShapeDtypeStruct((1, GLA_HV), F32)],
        scratch_shapes=[pltpu.VMEM((GLA_DV, GLA_HK), F32), pltpu.VMEM((tm, GLA_DK), F32), pltpu.VMEM((tm, GLA_DK), F32),
                        pltpu.VMEM((tm, GLA_DK), F32), pltpu.VMEM((tm, GLA_DV), BF16),
                        pltpu.VMEM((cpt * 8, GLA_DK), F32), pltpu.VMEM((cpt * 8, GLA_DK), F32)],
        compiler_params=_cparams("arbitrary"), name="gla_seq_bwd")


def _sum_halves(g, recv, c_idx, name):
    n, r, cdim = g.shape
    h = r // 2
    tr = _row_tile(h, 256, 8)
    nh = h // tr

    def body(c_ref, g_ref, r_ref, o_ref):
        o_ref[...] = (g_ref[...] + r_ref[...]).astype(BF16)

    return pl.pallas_call(
        body,
        grid_spec=pltpu.PrefetchScalarGridSpec(
            num_scalar_prefetch=1, grid=(n, nh),
            in_specs=[pl.BlockSpec((None, tr, cdim), lambda s, i, c: (s, c[0] * nh + i, 0)),
                      pl.BlockSpec((None, tr, cdim), lambda s, i, c: (s, i, 0))],
            out_specs=pl.BlockSpec((None, tr, cdim), lambda s, i, c: (s, i, 0))),
        out_shape=jax.ShapeDtypeStruct((n, h, cdim), BF16),
        compiler_params=_cparams("parallel", "parallel"), name=name)(c_idx, g, recv)


def _sum_slots(x, name):
    n, r, cdim = x.shape
    tr = _row_tile(r, 256, 8)

    def body(x_ref, o_ref):
        acc = x_ref[0].astype(F32)
        for j in range(1, n):
            acc = acc + x_ref[j].astype(F32)
        o_ref[...] = acc

    return pl.pallas_call(
        body, grid=(r // tr,),
        in_specs=[pl.BlockSpec((n, tr, cdim), lambda i: (0, i, 0))],
        out_specs=pl.BlockSpec((tr, cdim), lambda i: (i, 0)),
        out_shape=jax.ShapeDtypeStruct((r, cdim), F32),
        compiler_params=_cparams("parallel"), name=name)(x)


def _sum_own_and_slots(own, slots, chip_idx, name):
    n, r, cdim = own.shape
    tr = _row_tile(r, 256, 8)

    def body(s_ref, own_ref, a_ref, b_ref, c_ref, o_ref):
        o_ref[...] = (own_ref[...].astype(F32) + a_ref[...].astype(F32) + b_ref[...].astype(F32)
                      + c_ref[...].astype(F32))

    def slot(dd):
        return pl.BlockSpec((None, tr, cdim), lambda i, s: ((s[0] + dd) % n, i, 0))

    return pl.pallas_call(
        body,
        grid_spec=pltpu.PrefetchScalarGridSpec(
            num_scalar_prefetch=1, grid=(r // tr,), in_specs=[slot(0), slot(1), slot(2), slot(3)],
            out_specs=pl.BlockSpec((tr, cdim), lambda i, s: (i, 0))),
        out_shape=jax.ShapeDtypeStruct((r, cdim), F32),
        compiler_params=_cparams("parallel"), name=name)(chip_idx, own, slots, slots, slots)


def _add2(a, b, name):
    r, cdim = a.shape
    tr = _row_tile(r, 256, 8)

    def body(a_ref, b_ref, o_ref):
        o_ref[...] = a_ref[...] + b_ref[...]

    spec = pl.BlockSpec((tr, cdim), lambda i: (i, 0))
    return pl.pallas_call(body, grid=(r // tr,), in_specs=[spec, spec], out_specs=spec,
                          out_shape=jax.ShapeDtypeStruct((r, cdim), F32),
                          compiler_params=_cparams("parallel"), name=name)(a, b)


def _adamw_half(w, gs, m, v, half_idx, prev, name, dep=None):
    nl, _, h, cdim = w.shape
    tr = _row_tile(h, 256, 8)
    nprev = 0 if prev is None else 4
    extra = [] if dep is None else [dep]

    def body(s_ref, w_ref, m_ref, v_ref, *rest):
        g_refs = rest[:nl]
        go_ref, d_ref, mo_ref, vo_ref = rest[nl + nprev + len(extra):]
        layer = pl.program_id(0)
        gv = g_refs[0][...]
        for j in range(1, nl):
            gv = jnp.where(layer == j, g_refs[j][...], gv)
        go_ref[...] = gv
        mn = ADAM_B1 * m_ref[...] + (1.0 - ADAM_B1) * gv
        vn = ADAM_B2 * v_ref[...] + (1.0 - ADAM_B2) * (gv * gv)
        m_hat = mn / (1.0 - ADAM_B1 ** ADAM_STEP)
        v_hat = vn / (1.0 - ADAM_B2 ** ADAM_STEP)
        d_ref[...] = -ADAM_LR * (m_hat / (jnp.sqrt(v_hat) + ADAM_EPS) + ADAM_WD * w_ref[...])
        mo_ref[...] = mn
        vo_ref[...] = vn

    half = pl.BlockSpec((None, None, tr, cdim), lambda l, i, s: (l, s[0], i, 0))

    def of_layer(j):
        return pl.BlockSpec((tr, cdim), lambda l, i, s: (jnp.where(l == j, i, 0), 0))

    shp = jax.ShapeDtypeStruct(w.shape, F32)
    return pl.pallas_call(
        body,
        grid_spec=pltpu.PrefetchScalarGridSpec(
            num_scalar_prefetch=1, grid=(nl, h // tr),
            in_specs=[half] * 3 + [of_layer(j) for j in range(nl)] + [ANY_SPEC] * (nprev + len(extra)),
            out_specs=[half] * 4),
        out_shape=[shp] * 4, input_output_aliases={4 + nl + k: k for k in range(nprev)},
        compiler_params=_cparams("arbitrary", "arbitrary"), name=name,
    )(half_idx, w, m, v, *gs, *([] if prev is None else prev), *extra)


def _adamw_many(ws, gs, ms, vs):
    n = len(ws)

    def body(*refs):
        for i in range(n):
            w_ref, g_ref, m_ref, v_ref = refs[i], refs[n + i], refs[2 * n + i], refs[3 * n + i]
            d_ref, mo_ref, vo_ref = refs[4 * n + i], refs[5 * n + i], refs[6 * n + i]
            gv = g_ref[...]
            mn = ADAM_B1 * m_ref[...] + (1.0 - ADAM_B1) * gv
            vn = ADAM_B2 * v_ref[...] + (1.0 - ADAM_B2) * (gv * gv)
            m_hat = mn / (1.0 - ADAM_B1 ** ADAM_STEP)
            v_hat = vn / (1.0 - ADAM_B2 ** ADAM_STEP)
            d_ref[...] = -ADAM_LR * (m_hat / (jnp.sqrt(v_hat) + ADAM_EPS) + ADAM_WD * w_ref[...])
            mo_ref[...] = mn
            vo_ref[...] = vn

    shapes = [jax.ShapeDtypeStruct(w.shape, F32) for w in ws]
    outs = pl.pallas_call(body, out_shape=shapes * 3, name="adamw_small")(*ws, *gs, *ms, *vs)
    return outs[:n], outs[n:2 * n], outs[2 * n:]


def _split_rows(a):
    return a.reshape(a.shape[0], 2, a.shape[1] // 2, a.shape[2])


def _place():
    x, y, c = lax.axis_index("x"), lax.axis_index("y"), lax.axis_index("c")
    chips = [(1 - x, y), (x, 1 - y), (1 - x, 1 - y)]
    return x, y, c, chips


def _remote(src, dst, send_sem, recv_sem, to):
    return pltpu.make_async_remote_copy(src_ref=src, dst_ref=dst, send_sem=send_sem, recv_sem=recv_sem,
                                        device_id=to, device_id_type=MESH)


def _plan_gather(n_halved):
    def plan(src_refs, land_refs):
        x, y, c, chips = _place()
        me = 2 * x + y
        copies = []
        for k, (src, land) in enumerate(zip(src_refs, land_refs)):
            for (px, py) in chips:
                frm = 2 * px + py
                if k < n_halved:
                    copies.append((src.at[c], land.at[me, c], (px, py, c), land.at[frm, c]))
                else:
                    copies.append((src, land.at[me], (px, py, c), land.at[frm]))
        return copies
    return plan


def _plan_share(src_refs, land_refs):
    x, y, c, chips = _place()
    me = 2 * x + y
    sib = (x, y, 1 - c)
    copies = []
    for src, land in zip(src_refs, land_refs):
        copies.append((src, land.at[me], sib, land.at[me]))
        for (px, py) in chips:
            frm = 2 * px + py
            copies.append((land.at[frm, c], land.at[frm, c], sib, land.at[frm, 1 - c]))
    return copies


def _plan_scatter(n_parts):
    def plan(src_refs, land_refs):
        x, y, c, chips = _place()
        me = 2 * x + y
        copies = []
        for k, (src, land) in enumerate(zip(src_refs, land_refs)):
            for (px, py) in chips:
                to = 2 * px + py
                copies.append((src.at[to] if k < n_parts else src, land.at[me], (px, py, c), land.at[to]))
        return copies
    return plan


def _plan_exchange(n_split):
    def plan(src_refs, land_refs):
        x, y, c, _ = _place()
        sib = (x, y, 1 - c)
        return [(src.at[:, 1 - c] if k < n_split else src, land, sib, land)
                for k, (src, land) in enumerate(zip(src_refs, land_refs))]
    return plan


def _hbm(a):
    return pltpu.HBM(a.shape, a.dtype)


def _start_copies(name, srcs, lands, plan, ncopy, dep=None):
    ns, nl = len(srcs), len(lands)
    nin = ns + nl + (0 if dep is None else 1)

    def body(*refs):
        send_sems, recv_sems, token = refs[nin], refs[nin + 1], refs[-1]
        for k, (src, dst, dev, _) in enumerate(plan(refs[:ns], refs[ns:ns + nl])):
            _remote(src, dst, send_sems.at[k], recv_sems.at[k], dev).start()
        token[...] = jnp.zeros_like(token)

    args = [pltpu.with_memory_space_constraint(a, pltpu.HBM) for a in list(srcs) + list(lands)]
    outs = pl.pallas_call(
        body, name=name,
        out_shape=(pltpu.SemaphoreType.DMA((ncopy,)), pltpu.SemaphoreType.DMA((ncopy,)),
                   *[_hbm(a) for a in list(srcs) + list(lands)], jax.ShapeDtypeStruct((8, 128), F32)),
        in_specs=[HBM_SPEC] * (ns + nl) + ([] if dep is None else [ANY_SPEC]),
        out_specs=(SEM_SPEC, SEM_SPEC, *([HBM_SPEC] * (ns + nl)), pl.BlockSpec(memory_space=pltpu.VMEM)),
        input_output_aliases={i: 2 + i for i in range(ns + nl)},
        compiler_params=pltpu.CompilerParams(has_side_effects=SIDE_EFFECT),
    )(*args, *([] if dep is None else [dep]))
    return outs[0], outs[1], list(outs[2:2 + ns]), list(outs[2 + ns:2 + ns + nl]), outs[-1]


def _wait_copies(name, started, plan, after):
    send_sems, recv_sems, srcs, lands, _ = started
    ns, nl = len(srcs), len(lands)
    after = list(after) if isinstance(after, (list, tuple)) else [after]

    def body(*refs):
        send_ref, recv_ref = refs[ns + nl], refs[ns + nl + 1]
        for k, (src, _, dev, mine) in enumerate(plan(refs[:ns], refs[ns:ns + nl])):
            copy = _remote(src, mine, send_ref.at[k], recv_ref.at[k], dev)
            copy.wait_send()
            copy.wait_recv()

    outs = pl.pallas_call(
        body, name=name, out_shape=tuple(_hbm(a) for a in srcs + lands),
        in_specs=[HBM_SPEC] * (ns + nl) + [SEM_SPEC, SEM_SPEC] + [ANY_SPEC] * len(after),
        out_specs=tuple([HBM_SPEC] * (ns + nl)),
        input_output_aliases={i: i for i in range(ns + nl)},
        compiler_params=pltpu.CompilerParams(has_side_effects=SIDE_EFFECT),
    )(*srcs, *lands, send_sems, recv_sems, *after)
    return list(outs[:ns]), list(outs[ns:])


def _share_with_sibling(name, srcs, lands):
    n = len(srcs)

    def body(*refs):
        src_refs, land_refs, out_refs = refs[:n], refs[n:2 * n], refs[2 * n:3 * n]
        send_sem, recv_sem = refs[3 * n:]
        x, y, c, chips = _place()
        me = 2 * x + y
        sib = (x, y, 1 - c)
        sends, recvs = [], []
        for k in range(n):
            sems = (send_sem.at[4 * k], recv_sem.at[4 * k])
            sends.append(_remote(src_refs[k], out_refs[k].at[me], *sems, sib))
            recvs.append(_remote(src_refs[k], out_refs[k].at[me], *sems, sib))
            for j, (px, py) in enumerate(chips):
                frm = 2 * px + py
                sems = (send_sem.at[4 * k + 1 + j], recv_sem.at[4 * k + 1 + j])
                sends.append(_remote(land_refs[k].at[frm, c], out_refs[k].at[frm, c], *sems, sib))
                recvs.append(_remote(land_refs[k].at[frm, c], out_refs[k].at[frm, 1 - c], *sems, sib))
        for cp in sends:
            cp.start()
        for cp in recvs:
            cp.wait_recv()
        for cp in sends:
            cp.wait_send()

    return pl.pallas_call(
        body, name=name, in_specs=[HBM_SPEC] * (2 * n), out_specs=[HBM_SPEC] * n,
        out_shape=[jax.ShapeDtypeStruct(a.shape, a.dtype) for a in lands],
        input_output_aliases={n + k: k for k in range(n)},
        scratch_shapes=[pltpu.SemaphoreType.DMA((4 * n,)), pltpu.SemaphoreType.DMA((4 * n,))],
    )(*srcs, *lands)


def _pack(arrs):
    flat = jnp.concatenate([a.reshape(-1).astype(F32) for a in arrs])
    n = flat.shape[0]
    rows = -(-n // PACK_WIDTH)
    rows = -(-rows // 8) * 8
    return jnp.pad(flat, (0, rows * PACK_WIDTH - n)).reshape(rows, PACK_WIDTH)


def _unpack(buf, shapes):
    flat = buf.reshape(-1)
    out, off = [], 0
    for shp in shapes:
        n = 1
        for s in shp:
            n *= s
        out.append(flat[off:off + n].reshape(shp))
        off += n
    return out


def _unshard_cols(stacked):
    moved = jnp.moveaxis(stacked, 0, -2)
    return moved.reshape(moved.shape[:-2] + (moved.shape[-2] * moved.shape[-1],))


def _take_cols(blocks, start, width):
    bw = blocks.shape[2]
    pieces, lo = [], start
    while lo < start + width:
        b = lo // bw
        hi = min(start + width, (b + 1) * bw)
        pieces.append(blocks[b][:, lo - b * bw:hi - b * bw])
        lo = hi
    return jnp.concatenate(pieces, axis=1)


def _col_shard(full, s, width):
    return lax.dynamic_slice_in_dim(full, s * width, width, axis=full.ndim - 1)


def kernel(x, meta_tokens, mix_norm_g, ffn_norm_g, ffn_w1, ffn_w2, cp_w_in, cp_conv_w, cp_conv_b, cp_ln_g, cp_ln_b, cp_pool_w, cp_pool_scale, cp_w_out, gla_w_in, gla_gate_w2, gla_gate_b, gla_head_g, gla_w_out, final_norm_g, loss_target, m_meta_tokens, m_mix_norm_g, m_ffn_norm_g, m_ffn_w1, m_ffn_w2, m_cp_w_in, m_cp_conv_w, m_cp_conv_b, m_cp_ln_g, m_cp_ln_b, m_cp_pool_w, m_cp_pool_scale, m_cp_w_out, m_gla_w_in, m_gla_gate_w2, m_gla_gate_b, m_gla_head_g, m_gla_w_out, m_final_norm_g, v_meta_tokens, v_mix_norm_g, v_ffn_norm_g, v_ffn_w1, v_ffn_w2, v_cp_w_in, v_cp_conv_w, v_cp_conv_b, v_cp_ln_g, v_cp_ln_b, v_cp_pool_w, v_cp_pool_scale, v_cp_w_out, v_gla_w_in, v_gla_gate_w2, v_gla_gate_b, v_gla_head_g, v_gla_w_out, v_final_norm_g):
    d = D_MODEL
    chip = 2 * lax.axis_index("x") + lax.axis_index("y")
    core = lax.axis_index("c")
    seq = x.shape[1]
    t = seq + CHUNK

    sharded_small = [meta_tokens, cp_conv_w, gla_gate_w2, gla_gate_b, gla_head_g]

    def halves(w):
        return w.astype(BF16).reshape(2, w.shape[0] // 2, w.shape[1])

    def unhalve(g):
        return g.reshape(N_CHIPS, 2 * g.shape[2], g.shape[3])

    def start_gather(name, srcs, dep, whole=()):
        lands = [lax.empty((N_CHIPS,) + s.shape, s.dtype) for s in srcs]
        for a in whole:
            lands.append(lax.dynamic_update_slice(jnp.zeros((N_CHIPS,) + a.shape, a.dtype), a[None], (chip,) + (0,) * a.ndim))
        plan = _plan_gather(len(srcs))
        return _start_copies(name, list(srcs) + list(whole), lands, plan, 3 * len(lands), dep), plan, len(srcs)

    def arrived(name, gather, after):
        started, plan, n = gather
        srcs, lands = _wait_copies(name + "_wait", started, plan, after)
        return srcs[:n], lands[:n], lands[n:]

    cp_gather = start_gather("gather_cp_start", [halves(cp_w_in[0]), halves(cp_w_out[0])], None, [_pack(sharded_small)])
    ffn0_gather = start_gather("gather_ffn0_start", [halves(ffn_w1[0]), halves(ffn_w2[0])], cp_gather[0][-1])
    gla_gather = start_gather("gather_gla_start", [halves(gla_w_in[0]), halves(gla_w_out[0])], ffn0_gather[0][-1])
    ffn1_gather = start_gather("gather_ffn1_start", [halves(ffn_w1[1]), halves(ffn_w2[1])], gla_gather[0][-1])
    h0_rows = jnp.concatenate([jnp.zeros((CHUNK, d), F32) + ffn1_gather[0][-1][0, 0], x[0]], axis=0)
    cp_srcs, cp_lands, (small_g,) = arrived("gather_cp", cp_gather, [ffn1_gather[0][-1], h0_rows])
    cpin_g, cpout_g = [unhalve(g) for g in _share_with_sibling("gather_cp_share", cp_srcs, cp_lands)]
    per_chip = [_unpack(small_g[j], [a.shape for a in sharded_small]) for j in range(N_CHIPS)]
    meta_f, conv_w_f, gate_w_f, gate_b_f, head_g_f = [
        jnp.concatenate([per_chip[j][i] for j in range(N_CHIPS)], axis=-1) for i in range(len(sharded_small))]
    conv_w_f, gate_w_f = conv_w_f[0], gate_w_f[0]
    w_cp_in = _unshard_cols(cpin_g)
    w_cp_out = cpout_g.reshape(CONV_DIM + POOL_DIM, d)
    gate_w_pad = jnp.pad(gate_w_f, ((0, GATE_PAD - GATE_RANK), (0, 0))).astype(BF16)
    row = lambda a: a.reshape(1, -1)
    c_idx = core.reshape(1).astype(jnp.int32)
    chip_idx = chip.reshape(1).astype(jnp.int32)

    h0 = lax.dynamic_update_slice(h0_rows, meta_f, (PAD_ROWS, 0))
    z0, u0 = _norm_matmul(h0, row(mix_norm_g[0]), w_cp_in, 512, "cp_in_proj")
    c0, pm0, mix0 = _cp_seq_fwd(z0, conv_w_f, cp_conv_b, cp_ln_g, cp_ln_b, cp_pool_w[0], cp_pool_scale)
    ffn0_srcs, ffn0_lands, _ = arrived("gather_ffn0", ffn0_gather, mix0)
    ffn0_share = _start_copies("gather_ffn0_share_start", ffn0_srcs, ffn0_lands, _plan_share, 4 * len(ffn0_srcs))
    h1 = _matmul_residual(mix0, w_cp_out, h0, "cp_out_proj", dep=ffn0_share[-1])
    w1g0, w2g0 = [unhalve(g) for g in _wait_copies("gather_ffn0_share_wait", ffn0_share, _plan_share, h1)[1]]
    h2, hp0, uf0 = _ffn_fwd(h1, row(ffn_norm_g[0]), w1g0, w2g0, "ffn0_fwd")
    gla_srcs, gla_lands, _ = arrived("gather_gla", gla_gather, h2)
    glain_g, glaout_g = [unhalve(g) for g in _share_with_sibling("gather_gla_share", gla_srcs, gla_lands)]
    w_gla_in = jnp.concatenate([glain_g[j] for j in range(N_CHIPS)] + [jnp.zeros((d, GLA_IN_PAD - GLA_IN), BF16)], axis=1)
    w_gla_out = glaout_g.reshape(GLA_DV, d)
    z1, u2 = _norm_matmul(h2, row(mix_norm_g[1]), w_gla_in, 640, "gla_in_proj")
    ffn1_srcs, ffn1_lands, _ = arrived("gather_ffn1", ffn1_gather, z1)
    ffn1_share = _start_copies("gather_ffn1_share_start", ffn1_srcs, ffn1_lands, _plan_share, 4 * len(ffn1_srcs))
    o1, mix1, states = _gla_seq_fwd(z1, gate_w_pad, gate_b_f, head_g_f, dep=ffn1_share[-1])
    h3 = _matmul_residual(mix1, w_gla_out, h2, "gla_out_proj")
    w1g1, w2g1 = [unhalve(g) for g in _wait_copies("gather_ffn1_share_wait", ffn1_share, _plan_share, h3)[1]]
    h4, hp1, uf1 = _ffn_fwd(h3, row(ffn_norm_g[1]), w1g1, w2g1, "ffn1_fwd")

    def start_exchange(name, grads):
        srcs = [_split_rows(g) for g in grads]
        lands = [lax.empty((g.shape[0], g.shape[1] // 2, g.shape[2]), g.dtype) for g in grads]
        return _start_copies(name + "_exchange_start", srcs, lands, _plan_exchange(len(grads)), len(grads))

    def start_scatter(name, exchange, after):
        srcs, recv = _wait_copies(name + "_exchange_wait", exchange, _plan_exchange(len(exchange[2])), after)
        parts = [_sum_halves(g.reshape(g.shape[0], -1, g.shape[3]), r, c_idx, "%s_chip_sum_%d" % (name, k))
                 for k, (g, r) in enumerate(zip(srcs, recv))]
        lands = [lax.empty(p.shape, p.dtype) for p in parts]
        return _start_copies(name + "_scatter_start", parts, lands, _plan_scatter(len(parts)), 3 * len(parts))

    def finish_reduce(name, started, after):
        n = len(started[2])
        parts, lands = _wait_copies(name + "_scatter_wait", started, _plan_scatter(n), after)
        return [_sum_own_and_slots(p, s, chip_idx, "%s_slot_sum_%d" % (name, k)) for k, (p, s) in enumerate(zip(parts, lands))]

    dh4, d_final_g, loss_part = _loss_bwd(h4, row(final_norm_g), loss_target[0])

    dh3, dhp1, d_ffn_g1 = _ffn_bwd_data(dh4, h3, row(ffn_norm_g[1]), hp1, w1g1, w2g1, "ffn1_bwd")
    dw1_1 = _wgrad(uf1, dhp1, N_CHIPS, d, d, False, True, False, "ffn1_dw1")
    dw2_1 = _wgrad(hp1, dh4, N_CHIPS, d, d, True, False, True, "ffn1_dw2")
    ffn1_exchange = start_exchange("ffn1", [dw1_1, dw2_1])

    dmix1 = _dgrad(dh3, w_gla_out, "gla_out_dgrad", dep=ffn1_exchange[-1])
    dw_gla_out = _wgrad(mix1, dh3, 1, GLA_DV, d, False, False, False, "gla_out_dw", dep=ffn1_exchange[-1])
    ffn1_reduce = start_scatter("ffn1", ffn1_exchange, [dmix1, dw_gla_out])
    dz1, d_gate_w, d_gate_b, d_head_g = _gla_seq_bwd(dmix1, o1, z1, states, gate_w_pad, gate_b_f, head_g_f,
                                                     dep=ffn1_reduce[-1])
    dh2, d_mix_g1 = _dgrad_norm_bwd(dz1, w_gla_in, h2, row(mix_norm_g[1]), dh3, 640, "gla_in_dgrad")
    dw_gla_in = _wgrad(u2, dz1, GLA_IN_PAD // 640, d, 640, False, True, False, "gla_in_dw")
    gla_in_shards = jnp.stack([_take_cols(dw_gla_in, j * (GLA_IN // N_CHIPS), GLA_IN // N_CHIPS) for j in range(N_CHIPS)])
    gla_exchange = start_exchange("gla", [gla_in_shards, dw_gla_out.reshape(N_CHIPS, -1, d)])

    dh1, dhp0, d_ffn_g0 = _ffn_bwd_data(dh2, h1, row(ffn_norm_g[0]), hp0, w1g0, w2g0, "ffn0_bwd", dep=gla_exchange[-1])
    gla_reduce = start_scatter("gla", gla_exchange, dh1)
    dw1_0 = _wgrad(uf0, dhp0, N_CHIPS, d, d, False, True, False, "ffn0_dw1", dep=gla_reduce[-1])
    dw2_0 = _wgrad(hp0, dh2, N_CHIPS, d, d, True, False, True, "ffn0_dw2")
    ffn0_exchange = start_exchange("ffn0", [dw1_0, dw2_0])

    dmix0 = _dgrad(dh1, w_cp_out, "cp_out_dgrad", dep=ffn0_exchange[-1])
    dw_cp_out = _wgrad(mix0, dh1, 1, CONV_DIM + POOL_DIM, d, False, False, False, "cp_out_dw", dep=ffn0_exchange[-1])
    ffn0_reduce = start_scatter("ffn0", ffn0_exchange, [dmix0, dw_cp_out])
    dz0, d_conv_w, d_cp_vec, d_pool_w = _cp_seq_bwd(dmix0, z0, c0, pm0, conv_w_f, cp_ln_g, cp_ln_b, cp_pool_w[0],
                                                    cp_pool_scale, dep=ffn0_reduce[-1])
    grad_x, dh0_head, d_mix_g0 = _dgrad_norm_bwd_input(dz0, w_cp_in, h0, row(mix_norm_g[0]), dh1, 512, "cp_in_dgrad")
    grad_x = grad_x[None]
    dw_cp_in = _wgrad(u0, dz0, N_CHIPS, d, CP_IN // N_CHIPS, False, True, False, "cp_in_dw")

    cp_grads = [dw_cp_in, dw_cp_out.reshape(N_CHIPS, -1, d)]
    small_full = [dh0_head[PAD_ROWS:CHUNK],jnp.concatenate([d_mix_g0, d_mix_g1], axis=0),
                  jnp.concatenate([d_ffn_g0, d_ffn_g1], axis=0), d_conv_w[:CONV_WIDTH][None],
                  d_cp_vec[0:1], d_cp_vec[1:2], d_cp_vec[2:3], d_pool_w[None], d_cp_vec[3:4],
                  d_gate_w[:GATE_RANK][None], d_gate_b, d_head_g, d_final_g[0], loss_part[0, 0:1]]
    small_mine = _pack(small_full)
    cp_exchange_plan = _plan_exchange(len(cp_grads))
    cp_exchange = _start_copies(
        "cp_exchange_start", [_split_rows(g) for g in cp_grads] + [small_mine],
        [lax.empty((g.shape[0], g.shape[1] // 2, g.shape[2]), F32) for g in cp_grads] + [lax.empty(small_mine.shape, F32)],
        cp_exchange_plan, len(cp_grads) + 1)
    red_ffn1 = finish_reduce("ffn1", ffn1_reduce, cp_exchange[-1])
    red_gla = finish_reduce("gla", gla_reduce, cp_exchange[-1])
    red_ffn0 = finish_reduce("ffn0", ffn0_reduce, cp_exchange[-1])
    cp_srcs, cp_recv = _wait_copies("cp_exchange_wait", cp_exchange, cp_exchange_plan, [red_ffn1[1], red_gla[1], red_ffn0[1]])
    chip_sums = [_sum_halves(g.reshape(g.shape[0], -1, g.shape[3]), r, c_idx, "cp_chip_sum_%d" % k)
                 for k, (g, r) in enumerate(zip(cp_srcs[:-1], cp_recv[:-1]))]
    small_chip = _add2(cp_srcs[-1], cp_recv[-1], "chip_sum_small")
    small_slots = lax.dynamic_update_slice(jnp.zeros((N_CHIPS,) + small_chip.shape, F32), small_chip[None], (chip, 0, 0))
    cp_lands = [lax.empty(p.shape, p.dtype) for p in chip_sums] + [small_slots]
    cp_reduce = _start_copies("cp_scatter_start", chip_sums + [small_chip], cp_lands, _plan_scatter(len(chip_sums)),
                              3 * (len(chip_sums) + 1))

    big = {"w1": (ffn_w1, m_ffn_w1, v_ffn_w1), "w2": (ffn_w2, m_ffn_w2, v_ffn_w2),
           "cp_in": (cp_w_in, m_cp_w_in, v_cp_w_in), "cp_out": (cp_w_out, m_cp_w_out, v_cp_w_out),
           "gla_in": (gla_w_in, m_gla_w_in, v_gla_w_in), "gla_out": (gla_w_out, m_gla_w_out, v_gla_w_out)}
    other_idx = (1 - core).reshape(1).astype(jnp.int32)

    def adamw_by_halves(tag, reduced):
        flat = [r for n in reduced for r in reduced[n]]
        join_plan = _plan_exchange(0)
        join = _start_copies(tag + "_join_start", flat, [lax.empty(r.shape, F32) for r in flat], join_plan, len(flat))
        views = {n: [_split_rows(a) for a in big[n]] for n in reduced}
        own = {n: _adamw_half(views[n][0], reduced[n], views[n][1], views[n][2], c_idx, None, "adamw_%s_own" % n,
                              dep=join[-1]) for n in reduced}
        _, arrived_halves = _wait_copies(tag + "_join_wait", join, join_plan, [own[n][1] for n in reduced])
        outs, k = {}, 0
        for n in reduced:
            theirs = arrived_halves[k:k + len(reduced[n])]
            k += len(reduced[n])
            res = _adamw_half(views[n][0], theirs, views[n][1], views[n][2], other_idx, own[n], "adamw_%s_sibling" % n)
            outs[n] = [o.reshape(big[n][0].shape) for o in res]
        return outs

    big_out = adamw_by_halves("ffn_gla", {"w1": [red_ffn0[0], red_ffn1[0]], "w2": [red_ffn0[1], red_ffn1[1]],
                                          "gla_in": [red_gla[0]], "gla_out": [red_gla[1]]})
    cp_parts, cp_slots = _wait_copies("cp_scatter_wait", cp_reduce, _plan_scatter(len(chip_sums)), big_out["gla_out"][1])
    red_cp = [_sum_own_and_slots(a, s, chip_idx, "cp_slot_sum_%d" % k)
              for k, (a, s) in enumerate(zip(cp_parts[:-1], cp_slots[:-1]))]
    small_red = _sum_slots(cp_slots[-1], "slot_sum_small")
    big_out.update(adamw_by_halves("cp", {"cp_in": [red_cp[0]], "cp_out": [red_cp[1]]}))

    (g_meta, g_mix, g_ffn, g_conv_w, g_conv_b, g_ln_g, g_ln_b, g_pool_w, g_pool_scale, g_gate_w, g_gate_b, g_head,
     g_final, loss_sum) = _unpack(small_red, [a.shape for a in small_full])
    g_meta = _col_shard(g_meta, chip, meta_tokens.shape[-1])
    g_conv_w = _col_shard(g_conv_w, chip, cp_conv_w.shape[-1])
    g_gate_w = _col_shard(g_gate_w, chip, gla_gate_w2.shape[-1])
    g_gate_b = _col_shard(g_gate_b, chip, gla_gate_b.shape[-1])
    g_head = _col_shard(g_head, chip, gla_head_g.shape[-1])
    small_w = [meta_tokens, mix_norm_g, ffn_norm_g, cp_conv_w, cp_conv_b, cp_ln_g, cp_ln_b, cp_pool_w, cp_pool_scale,
               gla_gate_w2, gla_gate_b, gla_head_g, final_norm_g]
    small_m = [m_meta_tokens, m_mix_norm_g, m_ffn_norm_g, m_cp_conv_w, m_cp_conv_b, m_cp_ln_g, m_cp_ln_b, m_cp_pool_w,
               m_cp_pool_scale, m_gla_gate_w2, m_gla_gate_b, m_gla_head_g, m_final_norm_g]
    small_v = [v_meta_tokens, v_mix_norm_g, v_ffn_norm_g, v_cp_conv_w, v_cp_conv_b, v_cp_ln_g, v_cp_ln_b, v_cp_pool_w,
               v_cp_pool_scale, v_gla_gate_w2, v_gla_gate_b, v_gla_head_g, v_final_norm_g]
    small_g = [g_meta, g_mix, g_ffn, g_conv_w, g_conv_b, g_ln_g, g_ln_b, g_pool_w, g_pool_scale, g_gate_w, g_gate_b,
               g_head, g_final]
    shapes = [w.shape for w in small_w]
    small_g = [g.reshape(s) for g, s in zip(small_g, shapes)]
    at_least_2d = lambda arrs: [a.reshape(1, -1) if a.ndim == 1 else a for a in arrs]
    s_delta, s_m, s_v = _adamw_many(at_least_2d(small_w), at_least_2d(small_g), at_least_2d(small_m), at_least_2d(small_v))
    s_delta, s_m, s_v = [[a.reshape(s) for a, s in zip(group, shapes)] for group in (s_delta, s_m, s_v)]

    order = ["meta", "mix", "ffn", "w1", "w2", "cp_in", "conv_w", "conv_b", "ln_g", "ln_b", "pool_w", "pool_scale",
             "cp_out", "gla_in", "gate_w", "gate_b", "head", "gla_out", "final"]
    small_names = ["meta", "mix", "ffn", "conv_w", "conv_b", "ln_g", "ln_b", "pool_w", "pool_scale", "gate_w", "gate_b",
                   "head", "final"]
    big_names = ["w1", "w2", "cp_in", "cp_out", "gla_in", "gla_out"]
    table = {n: (small_g[i], s_delta[i], s_m[i], s_v[i]) for i, n in enumerate(small_names)}
    table.update({n: tuple(big_out[n]) for n in big_names})
    loss = loss_sum.reshape(())
    return (loss, grad_x, *[table[n][0] for n in order], *[table[n][1] for n in order],
            *[table[n][2] for n in order], *[table[n][3] for n in order])
```

```python
import functools

import jax
import jax.numpy as jnp
from jax import lax
from jax.experimental import pallas as pl
from jax.experimental.pallas import tpu as pltpu

F32 = jnp.float32
BF16 = jnp.bfloat16

D_MODEL = 1024
N_META = 16
CHUNK = 64
PAD_ROWS = CHUNK - N_META
EPS = 1e-5
CONV_DIM = 512
CONV_WIDTH = 31
CONV_HALO = 32
POOL_DIM = 512
POOL_WINDOWS = (2, 4, 8, 16)
POOL_GROUP = 128
POOL_HALO = 16
CP_IN = 2 * CONV_DIM + POOL_DIM
GLA_HEADS = 4
GLA_DK = 512
GLA_DV = 1024
GLA_HK = GLA_DK // GLA_HEADS
GLA_HV = GLA_DV // GLA_HEADS
GATE_RANK = 16
GATE_PAD = 128
GATE_NORM = 16.0
GLA_IN = 2 * GLA_DK + 2 * GLA_DV + GATE_RANK
GLA_IN_PAD = 2 * GLA_DK + 2 * GLA_DV + GATE_PAD
N_CHIPS = 4
ADAM_LR = 0.001
ADAM_B1 = 0.9
ADAM_B2 = 0.999
ADAM_EPS = 1e-08
ADAM_WD = 0.01
ADAM_STEP = 10

VMEM_LIMIT_BYTES = 56 * 1024 * 1024
ROW_TILE_TARGET = 832
TOKEN_TILE_TARGET = 1040
PACK_WIDTH = 1024
MESH = pl.DeviceIdType.MESH
HBM_SPEC = pl.BlockSpec(memory_space=pltpu.HBM)
ANY_SPEC = pl.BlockSpec(memory_space=pl.ANY)
SEM_SPEC = pl.BlockSpec(memory_space=pltpu.SEMAPHORE)
SIDE_EFFECT = pltpu.SideEffectType.DATAFLOW_SIDE_EFFECTING


def _cparams(*sem):
    return pltpu.CompilerParams(dimension_semantics=sem, vmem_limit_bytes=VMEM_LIMIT_BYTES)


def _row_tile(t, target, mult):
    best = mult
    for cand in range(mult, min(t, target) + 1, mult):
        if t % cand == 0:
            best = cand
    assert t % best == 0, (t, best)
    return best


def _rms(h, g):
    return h * lax.rsqrt(jnp.mean(h * h, axis=-1, keepdims=True) + EPS) * g


def _rms_bwd(h, g, du):
    r = lax.rsqrt(jnp.mean(h * h, axis=-1, keepdims=True) + EPS)
    xhat = h * r
    dxh = du * g
    dh = r * (dxh - xhat * jnp.mean(dxh * xhat, axis=-1, keepdims=True))
    return dh, du * xhat


def _valid_rows(i, tm):
    row = i * tm + lax.broadcasted_iota(jnp.int32, (tm, 1), 0)
    return row >= PAD_ROWS


def _dot(a, b):
    return jnp.dot(a, b, preferred_element_type=F32)


def _dot_nt(a, b):
    return lax.dot_general(a, b, (((1,), (1,)), ((), ())), preferred_element_type=F32)


def _dot_tn(a, b):
    return lax.dot_general(a, b, (((0,), (0,)), ((), ())), preferred_element_type=F32)


def _accumulate(ref, val, first):
    @pl.when(first)
    def _():
        ref[...] = val

    @pl.when(jnp.logical_not(first))
    def _():
        ref[...] += val


def _call_after(dep, body, n_in, in_specs, args, **kw):
    if dep is None:
        return pl.pallas_call(body, in_specs=in_specs, **kw)(*args)

    def with_dep(*refs):
        body(*refs[:n_in], *refs[n_in + 1:])

    return pl.pallas_call(with_dep, in_specs=list(in_specs) + [ANY_SPEC], **kw)(*args, dep)


def _norm_matmul(h, g, w, nc, name, dep=None):
    t, d = h.shape
    n = w.shape[1]
    tm = _row_tile(t, TOKEN_TILE_TARGET, 16)

    def body(h_ref, g_ref, w_ref, z_ref, u_ref):
        u = _rms(h_ref[...], g_ref[...]).astype(BF16)
        u_ref[...] = u
        for n0 in range(0, n, nc):
            z_ref[:, n0:n0 + nc] = _dot(u, w_ref[:, n0:n0 + nc]).astype(BF16)

    return _call_after(
        dep, body, 3,
        [pl.BlockSpec((tm, d), lambda i: (i, 0)), pl.BlockSpec((1, d), lambda i: (0, 0)),
         pl.BlockSpec((d, n), lambda i: (0, 0))], (h, g, w), grid=(t // tm,),
        out_specs=[pl.BlockSpec((tm, n), lambda i: (i, 0)), pl.BlockSpec((tm, d), lambda i: (i, 0))],
        out_shape=[jax.ShapeDtypeStruct((t, n), BF16), jax.ShapeDtypeStruct((t, d), BF16)],
        compiler_params=_cparams("parallel"), name=name)


def _matmul_residual(a, w, h, name, dep=None):
    t, k = a.shape
    d = w.shape[1]
    tm = _row_tile(t, TOKEN_TILE_TARGET, 16)

    def body(a_ref, w_ref, h_ref, o_ref):
        o_ref[...] = h_ref[...] + _dot(a_ref[...], w_ref[...])

    return _call_after(
        dep, body, 3,
        [pl.BlockSpec((tm, k), lambda i: (i, 0)), pl.BlockSpec((k, d), lambda i: (0, 0)),
         pl.BlockSpec((tm, d), lambda i: (i, 0))], (a, w, h), grid=(t // tm,),
        out_specs=pl.BlockSpec((tm, d), lambda i: (i, 0)),
        out_shape=jax.ShapeDtypeStruct((t, d), F32),
        compiler_params=_cparams("parallel"), name=name)


def _ffn_fwd(h, g, w1g, w2g, name):
    t, d = h.shape
    ns, ffs = w1g.shape[0], w1g.shape[2]
    tm = _row_tile(t, TOKEN_TILE_TARGET, 16)

    def body(h_ref, g_ref, w1_ref, w2_ref, ho_ref, hp_ref, u_ref, acc_ref):
        s = pl.program_id(1)

        @pl.when(s == 0)
        def _():
            u_ref[...] = _rms(h_ref[...], g_ref[...]).astype(BF16)

        hp = _dot(u_ref[...], w1_ref[...])
        hp_ref[...] = hp.astype(BF16)
        a = jnp.maximum(hp, 0.0)
        _accumulate(acc_ref, _dot((a * a).astype(BF16), w2_ref[...]), s == 0)

        @pl.when(s == ns - 1)
        def _():
            ho_ref[...] = h_ref[...] + acc_ref[...]

    return pl.pallas_call(
        body, grid=(t // tm, ns),
        in_specs=[pl.BlockSpec((tm, d), lambda i, s: (i, 0)), pl.BlockSpec((1, d), lambda i, s: (0, 0)),
                  pl.BlockSpec((None, d, ffs), lambda i, s: (s, 0, 0)),
                  pl.BlockSpec((None, ffs, d), lambda i, s: (s, 0, 0))],
        out_specs=[pl.BlockSpec((tm, d), lambda i, s: (i, 0)), pl.BlockSpec((tm, ffs), lambda i, s: (i, s)),
                   pl.BlockSpec((tm, d), lambda i, s: (i, 0))],
        out_shape=[jax.ShapeDtypeStruct((t, d), F32), jax.ShapeDtypeStruct((t, ns * ffs), BF16),
                   jax.ShapeDtypeStruct((t, d), BF16)],
        scratch_shapes=[pltpu.VMEM((tm, d), F32)],
        compiler_params=_cparams("parallel", "arbitrary"), name=name)(h, g, w1g, w2g)


def _ffn_bwd_data(dh, h, g, hp, w1g, w2g, name, dep=None):
    t, d = h.shape
    ns, ffs = w1g.shape[0], w1g.shape[2]
    tm = _row_tile(t, ROW_TILE_TARGET, CHUNK)

    def body(dh_ref, h_ref, g_ref, hp_ref, w1_ref, w2_ref, dhi_ref, dhp_ref, dg_ref, acc_ref):
        i, s = pl.program_id(0), pl.program_id(1)
        da = _dot_nt(dh_ref[...].astype(BF16), w2_ref[...])
        dhp = (da * (2.0 * jnp.maximum(hp_ref[...].astype(F32), 0.0))).astype(BF16)
        dhp_ref[...] = dhp
        _accumulate(acc_ref, _dot_nt(dhp, w1_ref[...]), s == 0)

        @pl.when(s == ns - 1)
        def _():
            dhn, dgr = _rms_bwd(h_ref[...], g_ref[...], acc_ref[...])
            dhi_ref[...] = jnp.where(_valid_rows(i, tm), dh_ref[...] + dhn, 0.0)
            _accumulate(dg_ref, jnp.sum(dgr, axis=0, keepdims=True), i == 0)

    return _call_after(
        dep, body, 6,
        [pl.BlockSpec((tm, d), lambda i, s: (i, 0)), pl.BlockSpec((tm, d), lambda i, s: (i, 0)),
         pl.BlockSpec((1, d), lambda i, s: (0, 0)), pl.BlockSpec((tm, ffs), lambda i, s: (i, s)),
         pl.BlockSpec((None, d, ffs), lambda i, s: (s, 0, 0)),
         pl.BlockSpec((None, ffs, d), lambda i, s: (s, 0, 0))], (dh, h, g, hp, w1g, w2g), grid=(t // tm, ns),
        out_specs=[pl.BlockSpec((tm, d), lambda i, s: (i, 0)), pl.BlockSpec((tm, ffs), lambda i, s: (i, s)),
                   pl.BlockSpec((1, d), lambda i, s: (0, 0))],
        out_shape=[jax.ShapeDtypeStruct((t, d), F32), jax.ShapeDtypeStruct((t, ns * ffs), BF16),
                   jax.ShapeDtypeStruct((1, d), F32)],
        scratch_shapes=[pltpu.VMEM((tm, d), F32)],
        compiler_params=_cparams("arbitrary", "arbitrary"), name=name)


WGRAD_ROWS = 1024


def _wgrad(x, dy, nb, xc, yc, x_by_block, dy_by_block, relu2, name, dep=None):
    t = x.shape[0]
    tk = _row_tile(t - CHUNK, WGRAD_ROWS, CHUNK)

    def prep(xv):
        if relu2:
            xv = jnp.maximum(xv.astype(F32), 0.0)
            xv = xv * xv
        return xv.astype(BF16)

    def body(xh_ref, dyh_ref, x_ref, dy_ref, o_ref):
        k = pl.program_id(1)
        p = _dot_tn(prep(x_ref[...]), dy_ref[...].astype(BF16))

        @pl.when(k == 0)
        def _():
            o_ref[...] = p + _dot_tn(prep(xh_ref[...]), dyh_ref[...].astype(BF16))

        @pl.when(k > 0)
        def _():
            o_ref[...] += p

    def head(width, by_block):
        return pl.BlockSpec((CHUNK, width), (lambda b, k: (0, b)) if by_block else (lambda b, k: (0, 0)))

    def rest(width, by_block):
        def index(b, k):
            return pl.multiple_of(CHUNK + k * tk, CHUNK), (pl.multiple_of(b * width, 128) if by_block else 0)
        return pl.BlockSpec((pl.Element(tk), pl.Element(width)), index)

    return _call_after(
        dep, body, 4,
        [head(xc, x_by_block), head(yc, dy_by_block), rest(xc, x_by_block), rest(yc, dy_by_block)], (x, dy, x, dy),
        grid=(nb, (t - CHUNK) // tk),
        out_specs=pl.BlockSpec((None, xc, yc), lambda b, k: (b, 0, 0)),
        out_shape=jax.ShapeDtypeStruct((nb, xc, yc), F32),
        compiler_params=_cparams("parallel", "arbitrary"), name=name)


def _dgrad(dh, w, name, dep=None):
    t, d = dh.shape
    k = w.shape[0]
    tm = _row_tile(t, TOKEN_TILE_TARGET, 16)

    def body(dh_ref, w_ref, o_ref):
        o_ref[...] = _dot_nt(dh_ref[...].astype(BF16), w_ref[...]).astype(BF16)

    return _call_after(
        dep, body, 2,
        [pl.BlockSpec((tm, d), lambda i: (i, 0)), pl.BlockSpec((k, d), lambda i: (0, 0))], (dh, w), grid=(t // tm,),
        out_specs=pl.BlockSpec((tm, k), lambda i: (i, 0)),
        out_shape=jax.ShapeDtypeStruct((t, k), BF16),
        compiler_params=_cparams("parallel"), name=name)


def _dgrad_norm_bwd(dz, w, h, g, dh, nc, name):
    t, d = h.shape
    n = w.shape[1]
    tm = _row_tile(t, ROW_TILE_TARGET // 2, 16)

    def body(dz_ref, w_ref, h_ref, g_ref, dh_ref, dhi_ref, dg_ref):
        i = pl.program_id(0)
        du = jnp.zeros((tm, d), F32)
        for n0 in range(0, n, nc):
            du = du + _dot_nt(dz_ref[:, n0:n0 + nc], w_ref[:, n0:n0 + nc])
        dhn, dgr = _rms_bwd(h_ref[...], g_ref[...], du)
        dhi_ref[...] = jnp.where(_valid_rows(i, tm), dh_ref[...] + dhn, 0.0)
        _accumulate(dg_ref, jnp.sum(dgr, axis=0, keepdims=True), i == 0)

    return pl.pallas_call(
        body, grid=(t // tm,),
        in_specs=[pl.BlockSpec((tm, n), lambda i: (i, 0)), pl.BlockSpec((d, n), lambda i: (0, 0)),
                  pl.BlockSpec((tm, d), lambda i: (i, 0)), pl.BlockSpec((1, d), lambda i: (0, 0)),
                  pl.BlockSpec((tm, d), lambda i: (i, 0))],
        out_specs=[pl.BlockSpec((tm, d), lambda i: (i, 0)), pl.BlockSpec((1, d), lambda i: (0, 0))],
        out_shape=[jax.ShapeDtypeStruct((t, d), F32), jax.ShapeDtypeStruct((1, d), F32)],
        compiler_params=_cparams("arbitrary"), name=name)(dz, w, h, g, dh)


def _dgrad_norm_bwd_input(dz, w, h, g, dh, nc, name):
    t, d = h.shape
    n = w.shape[1]
    tl = _row_tile(t - CHUNK, 512, CHUNK)

    def grads(dz_ref, w_ref, h_ref, g_ref, dh_ref, rows):
        du = jnp.zeros((rows, d), F32)
        for n0 in range(0, n, nc):
            du = du + _dot_nt(dz_ref[:, n0:n0 + nc], w_ref[:, n0:n0 + nc])
        dhn, dgr = _rms_bwd(h_ref[...], g_ref[...], du)
        return dh_ref[...] + dhn, jnp.sum(dgr, axis=0, keepdims=True)

    def rest_body(dz_ref, w_ref, h_ref, g_ref, dh_ref, dg_head_ref, dx_ref, dg_ref):
        dx, dg = grads(dz_ref, w_ref, h_ref, g_ref, dh_ref, tl)
        dx_ref[...] = dx

        @pl.when(pl.program_id(0) == 0)
        def _():
            dg_ref[...] = dg_head_ref[...] + dg

        @pl.when(pl.program_id(0) > 0)
        def _():
            dg_ref[...] += dg

    def head_body(dz_ref, w_ref, h_ref, g_ref, dh_ref, dx_ref, dg_ref):
        dx, dg = grads(dz_ref, w_ref, h_ref, g_ref, dh_ref, CHUNK)
        dx_ref[...] = jnp.where(_valid_rows(0, CHUNK), dx, 0.0)
        dg_ref[...] = dg

    def shifted(width):
        return pl.BlockSpec((pl.Element(tl), pl.Element(width)), lambda i: (pl.multiple_of(CHUNK + i * tl, CHUNK), 0))

    whole = [pl.BlockSpec((d, n), lambda i: (0, 0)), pl.BlockSpec((1, d), lambda i: (0, 0))]
    head = lambda width: pl.BlockSpec((CHUNK, width), lambda i: (0, 0))
    dh_head, dg_head = pl.pallas_call(
        head_body, grid=(1,), in_specs=[head(n), whole[0], head(d), whole[1], head(d)],
        out_specs=[head(d), whole[1]],
        out_shape=[jax.ShapeDtypeStruct((CHUNK, d), F32), jax.ShapeDtypeStruct((1, d), F32)],
        compiler_params=_cparams("arbitrary"), name=name + "_head")(dz, w, h, g, dh)
    dx, dg = pl.pallas_call(
        rest_body, grid=((t - CHUNK) // tl,),
        in_specs=[shifted(n), whole[0], shifted(d), whole[1], shifted(d), whole[1]],
        out_specs=[pl.BlockSpec((tl, d), lambda i: (i, 0)), whole[1]],
        out_shape=[jax.ShapeDtypeStruct((t - CHUNK, d), F32), jax.ShapeDtypeStruct((1, d), F32)],
        compiler_params=_cparams("arbitrary"), name=name)(dz, w, h, g, dh, dg_head)
    return dx, dh_head, dg


def _loss_bwd(h, g, target):
    t, d = h.shape
    tl = _row_tile(t - CHUNK, 1024, CHUNK)

    def body(h_ref, g_ref, t_ref, dh_ref, dg_ref, loss_ref):
        i = pl.program_id(0)
        hv, gv = h_ref[...], g_ref[...]
        err = _rms(hv, gv) - t_ref[...]
        part = 0.5 * jnp.sum(jnp.mean(err * err, axis=-1, keepdims=True), axis=0, keepdims=True)
        dhn, dgr = _rms_bwd(hv, gv, err * (1.0 / d))
        dh_ref[...] = dhn
        _accumulate(dg_ref, jnp.sum(dgr, axis=0, keepdims=True), i == 0)
        _accumulate(loss_ref, jnp.broadcast_to(part, (8, 128)), i == 0)

    shifted = pl.BlockSpec((pl.Element(tl), pl.Element(d)), lambda i: (pl.multiple_of(CHUNK + i * tl, CHUNK), 0))
    dh, dg, loss = pl.pallas_call(
        body, grid=((t - CHUNK) // tl,),
        in_specs=[shifted, pl.BlockSpec((1, d), lambda i: (0, 0)), pl.BlockSpec((tl, d), lambda i: (i, 0))],
        out_specs=[shifted, pl.BlockSpec((1, d), lambda i: (0, 0)), pl.BlockSpec((8, 128), lambda i: (0, 0))],
        out_shape=[jax.ShapeDtypeStruct((t, d), F32), jax.ShapeDtypeStruct((1, d), F32),
                   jax.ShapeDtypeStruct((8, 128), F32)],
        compiler_params=_cparams("arbitrary"), name="loss_bwd")(h, g, target)

    def zero_head(dh_ref, o_ref):
        o_ref[...] = jnp.zeros_like(o_ref)

    dh = pl.pallas_call(
        zero_head, grid=(1,), in_specs=[ANY_SPEC], out_specs=pl.BlockSpec((CHUNK, d), lambda i: (0, 0)),
        out_shape=jax.ShapeDtypeStruct((t, d), F32), input_output_aliases={0: 0}, name="loss_bwd_head")(dh)
    return dh, dg, loss


CONV_BLOCK = 32


def _silu(x):
    return x * jax.nn.sigmoid(x)


def _row_shifts(win):
    n = win.shape[0]
    return [win] + [pltpu.roll(win, n - j, 0) for j in range(1, 8)]


def _cp_seq_fwd(z, conv_w, conv_b, ln_g, ln_b, pool_w, pool_scale):
    t = z.shape[0]
    tm = _row_tile(t, ROW_TILE_TARGET, CHUNK)

    def body(z_ref, cw_ref, cb_ref, lg_ref, lb_ref, pw_ref, ps_ref, c_ref, pm_ref, mix_ref, gbuf, pbuf):
        i = pl.program_id(0)

        @pl.when(i == 0)
        def _():
            gbuf[0:CONV_HALO, :] = jnp.zeros((CONV_HALO, CONV_DIM), F32)
            pbuf[0:POOL_HALO, :] = jnp.zeros((POOL_HALO, POOL_DIM), F32)

        @pl.when(i > 0)
        def _():
            gbuf[0:CONV_HALO, :] = gbuf[tm:tm + CONV_HALO, :]
            pbuf[0:POOL_HALO, :] = pbuf[tm:tm + POOL_HALO, :]

        av = z_ref[:, 0:CONV_DIM].astype(F32)
        ag = z_ref[:, CONV_DIM:2 * CONV_DIM].astype(F32)
        gbuf[CONV_HALO:CONV_HALO + tm, :] = av * jax.nn.sigmoid(ag)
        pbuf[POOL_HALO:POOL_HALO + tm, :] = z_ref[:, 2 * CONV_DIM:CP_IN].astype(F32)

        def conv_block(rb, carry):
            base = pl.multiple_of(rb * CONV_BLOCK, CONV_BLOCK)
            shifted = _row_shifts(gbuf[pl.ds(base, CONV_BLOCK + CONV_HALO), :])
            acc = jnp.zeros((CONV_BLOCK, CONV_DIM), F32)
            for k in range(CONV_WIDTH):
                whole, part = divmod(CONV_HALO - (CONV_WIDTH - 1) + k, 8)
                acc = acc + cw_ref[k:k + 1, :] * shifted[part][8 * whole:8 * whole + CONV_BLOCK, :]
            c_ref[pl.ds(base, CONV_BLOCK), :] = acc + cb_ref[...]
            return carry

        lax.fori_loop(0, tm // CONV_BLOCK, conv_block, 0)

        c = c_ref[...]
        mu = jnp.mean(c, axis=-1, keepdims=True)
        xc = c - mu
        ln = xc * lax.rsqrt(jnp.mean(xc * xc, axis=-1, keepdims=True) + EPS) * lg_ref[...] + lb_ref[...]
        row = i * tm + lax.broadcasted_iota(jnp.int32, (tm, 1), 0)
        mix_ref[:, 0:CONV_DIM] = jnp.where(row >= PAD_ROWS, _silu(ln), 0.0).astype(BF16)

        tpos = (row - PAD_ROWS + 1).astype(F32)
        for gi, wdw in enumerate(POOL_WINDOWS):
            lo = POOL_GROUP * gi
            cur = pbuf[POOL_HALO:POOL_HALO + tm, lo:lo + POOL_GROUP]
            sacc = cur
            for j in range(1, wdw):
                sacc = sacc + pbuf[POOL_HALO - j:POOL_HALO - j + tm, lo:lo + POOL_GROUP]
            pm = (sacc / jnp.clip(tpos, 1.0, float(wdw)) - cur).astype(BF16)
            pm_ref[:, lo:lo + POOL_GROUP] = pm
            pg = _dot(pm, pw_ref[gi].astype(BF16))
            mix_ref[:, CONV_DIM + lo:CONV_DIM + lo + POOL_GROUP] = (pg * ps_ref[:, lo:lo + POOL_GROUP]).astype(BF16)

    vec = pl.BlockSpec((1, CONV_DIM), lambda i: (0, 0))
    return pl.pallas_call(
        body, grid=(t // tm,),
        in_specs=[pl.BlockSpec((tm, CP_IN), lambda i: (i, 0)),
                  pl.BlockSpec((CONV_WIDTH, CONV_DIM), lambda i: (0, 0)), vec, vec, vec,
                  pl.BlockSpec((len(POOL_WINDOWS), POOL_GROUP, POOL_GROUP), lambda i: (0, 0, 0)), vec],
        out_specs=[pl.BlockSpec((tm, CONV_DIM), lambda i: (i, 0)), pl.BlockSpec((tm, POOL_DIM), lambda i: (i, 0)),
                   pl.BlockSpec((tm, CONV_DIM + POOL_DIM), lambda i: (i, 0))],
        out_shape=[jax.ShapeDtypeStruct((t, CONV_DIM), F32), jax.ShapeDtypeStruct((t, POOL_DIM), BF16),
                   jax.ShapeDtypeStruct((t, CONV_DIM + POOL_DIM), BF16)],
        scratch_shapes=[pltpu.VMEM((tm + CONV_HALO, CONV_DIM), F32), pltpu.VMEM((tm + POOL_HALO, POOL_DIM), F32)],
        compiler_params=_cparams("arbitrary"), name="cp_seq_fwd")(z, conv_w, conv_b, ln_g, ln_b, pool_w, pool_scale)


def _cp_seq_bwd(dmix, z, c, pm, conv_w, ln_g, ln_b, pool_w, pool_scale, dep=None):
    t = z.shape[0]
    tm = _row_tile(t, ROW_TILE_TARGET, CHUNK)
    nt = t // tm

    def body(dmix_ref, z_ref, c_ref, pm_ref, cw_ref, lg_ref, lb_ref, pw_ref, ps_ref,
             dz_ref, dcw_ref, dvec_ref, dpw_ref, dcbuf, qbuf, glu_buf, dwacc):
        i = pl.program_id(0)
        tile = nt - 1 - i

        @pl.when(i == 0)
        def _():
            dcbuf[tm:tm + CONV_HALO, :] = jnp.zeros((CONV_HALO, CONV_DIM), F32)
            qbuf[tm:tm + POOL_HALO, :] = jnp.zeros((POOL_HALO, POOL_DIM), F32)
            dcw_ref[...] = jnp.zeros_like(dcw_ref)
            dwacc[...] = jnp.zeros_like(dwacc)
            dvec_ref[...] = jnp.zeros_like(dvec_ref)
            dpw_ref[...] = jnp.zeros_like(dpw_ref)

        @pl.when(i > 0)
        def _():
            dcbuf[tm:tm + CONV_HALO, :] = dcbuf[0:CONV_HALO, :]
            qbuf[tm:tm + POOL_HALO, :] = qbuf[0:POOL_HALO, :]

        row = tile * tm + lax.broadcasted_iota(jnp.int32, (tm, 1), 0)
        cv = c_ref[...]
        mu = jnp.mean(cv, axis=-1, keepdims=True)
        xc = cv - mu
        rstd = lax.rsqrt(jnp.mean(xc * xc, axis=-1, keepdims=True) + EPS)
        xhat = xc * rstd
        ln = xhat * lg_ref[...] + lb_ref[...]
        sg = jax.nn.sigmoid(ln)
        da = jnp.where(row >= PAD_ROWS, dmix_ref[:, 0:CONV_DIM].astype(F32), 0.0)
        dln = da * (sg * (1.0 + ln * (1.0 - sg)))
        dxh = dln * lg_ref[...]
        dc = rstd * (dxh - jnp.mean(dxh, axis=-1, keepdims=True) - xhat * jnp.mean(dxh * xhat, axis=-1, keepdims=True))
        dcbuf[0:tm, :] = dc
        dvec_ref[0:1, :] += jnp.sum(dc, axis=0, keepdims=True)
        dvec_ref[1:2, :] += jnp.sum(dln * xhat, axis=0, keepdims=True)
        dvec_ref[2:3, :] += jnp.sum(dln, axis=0, keepdims=True)

        av = z_ref[:, 0:CONV_DIM].astype(F32)
        sig_g = jax.nn.sigmoid(z_ref[:, CONV_DIM:2 * CONV_DIM].astype(F32))
        glu_buf[...] = av * sig_g

        def conv_block(rb, carry):
            base = pl.multiple_of(rb * CONV_BLOCK, CONV_BLOCK)
            shifted = _row_shifts(dcbuf[pl.ds(base, CONV_BLOCK + CONV_HALO), :])
            glu = glu_buf[pl.ds(base, CONV_BLOCK), :]
            acc = jnp.zeros((CONV_BLOCK, CONV_DIM), F32)
            for k in range(CONV_WIDTH):
                whole, part = divmod(CONV_WIDTH - 1 - k, 8)
                slab = shifted[part][8 * whole:8 * whole + CONV_BLOCK, :]
                acc = acc + cw_ref[k:k + 1, :] * slab
                prod = slab * glu
                part = prod[0:8]
                for q in range(1, CONV_BLOCK // 8):
                    part = part + prod[8 * q:8 * q + 8]
                dwacc[k] += part
            glu_buf[pl.ds(base, CONV_BLOCK), :] = acc
            return carry

        lax.fori_loop(0, tm // CONV_BLOCK, conv_block, 0)

        @pl.when(i == nt - 1)
        def _():
            for k in range(CONV_WIDTH):
                dcw_ref[k:k + 1, :] = jnp.sum(dwacc[k], axis=0, keepdims=True)
        dglu = glu_buf[...]
        dz_ref[:, 0:CONV_DIM] = (dglu * sig_g).astype(BF16)
        dz_ref[:, CONV_DIM:2 * CONV_DIM] = (dglu * av * sig_g * (1.0 - sig_g)).astype(BF16)

        tpos = (row - PAD_ROWS + 1).astype(F32)
        for gi, wdw in enumerate(POOL_WINDOWS):
            lo = POOL_GROUP * gi
            dp = dmix_ref[:, CONV_DIM + lo:CONV_DIM + lo + POOL_GROUP].astype(F32)
            pmv = pm_ref[:, lo:lo + POOL_GROUP]
            pwb = pw_ref[gi].astype(BF16)
            dvec_ref[3:4, lo:lo + POOL_GROUP] += jnp.sum(dp * _dot(pmv, pwb), axis=0, keepdims=True)
            dq = (dp * ps_ref[:, lo:lo + POOL_GROUP]).astype(BF16)
            dpw_ref[gi] += _dot_tn(pmv, dq)
            dpm = _dot_nt(dq, pwb)
            qbuf[0:tm, lo:lo + POOL_GROUP] = dpm / jnp.clip(tpos, 1.0, float(wdw))
            sacc = -dpm
            for j in range(wdw):
                sacc = sacc + qbuf[j:j + tm, lo:lo + POOL_GROUP]
            dz_ref[:, 2 * CONV_DIM + lo:2 * CONV_DIM + lo + POOL_GROUP] = sacc.astype(BF16)

    vec = pl.BlockSpec((1, CONV_DIM), lambda i: (0, 0))
    rev = lambda i: (nt - 1 - i, 0)
    return _call_after(
        dep, body, 9,
        [pl.BlockSpec((tm, CONV_DIM + POOL_DIM), rev), pl.BlockSpec((tm, CP_IN), rev),
         pl.BlockSpec((tm, CONV_DIM), rev), pl.BlockSpec((tm, POOL_DIM), rev),
         pl.BlockSpec((CONV_WIDTH, CONV_DIM), lambda i: (0, 0)), vec, vec,
         pl.BlockSpec((len(POOL_WINDOWS), POOL_GROUP, POOL_GROUP), lambda i: (0, 0, 0)), vec],
        (dmix, z, c, pm, conv_w, ln_g, ln_b, pool_w, pool_scale), grid=(nt,),
        out_specs=[pl.BlockSpec((tm, CP_IN), rev), pl.BlockSpec((CONV_WIDTH + 1, CONV_DIM), lambda i: (0, 0)),
                   pl.BlockSpec((8, CONV_DIM), lambda i: (0, 0)),
                   pl.BlockSpec((len(POOL_WINDOWS), POOL_GROUP, POOL_GROUP), lambda i: (0, 0, 0))],
        out_shape=[jax.ShapeDtypeStruct((t, CP_IN), BF16), jax.ShapeDtypeStruct((CONV_WIDTH + 1, CONV_DIM), F32),
                   jax.ShapeDtypeStruct((8, CONV_DIM), F32),
                   jax.ShapeDtypeStruct((len(POOL_WINDOWS), POOL_GROUP, POOL_GROUP), F32)],
        scratch_shapes=[pltpu.VMEM((tm + CONV_HALO, CONV_DIM), F32), pltpu.VMEM((tm + POOL_HALO, POOL_DIM), F32),
                        pltpu.VMEM((tm, CONV_DIM), F32), pltpu.VMEM((CONV_WIDTH + 1, 8, CONV_DIM), F32)],
        compiler_params=_cparams("arbitrary"), name="cp_seq_bwd")


GLA_UNROLL = 2
Q0, K0, V0, G0, R0 =0, GLA_DK, 2 * GLA_DK, 2 * GLA_DK + GLA_DV, 2 * GLA_DK + 2 * GLA_DV


def _split3(x):
    hi = x.astype(BF16)
    r1 = x - hi.astype(F32)
    mid = r1.astype(BF16)
    lo = (r1 - mid.astype(F32)).astype(BF16)
    return hi, mid, lo


def _tri(strict):
    r = lax.broadcasted_iota(jnp.int32, (CHUNK, CHUNK), 0)
    c = lax.broadcasted_iota(jnp.int32, (CHUNK, CHUNK), 1)
    return ((r > c) if strict else (r >= c)).astype(BF16)


def _chunk_sums(x, cpt, strict, pieces):
    tri3 = jnp.broadcast_to(_tri(strict)[None], (cpt, CHUNK, CHUNK))
    acc = None
    for piece in _split3(x.reshape(cpt, CHUNK, x.shape[-1]))[:pieces]:
        part = jnp.einsum("bij,bjk->bik", tri3, piece, preferred_element_type=F32)
        acc = part if acc is None else acc + part
    return acc


def _chunk_decay(r, gw_ref, gb_ref, cpt):
    pre = _dot(r, gw_ref[...]) + gb_ref[...]
    lac = (jnp.minimum(pre, 0.0) - jnp.log(1.0 + jnp.exp(-jnp.abs(pre)))) * (1.0 / GATE_NORM)
    cum3 = _chunk_sums(lac, cpt, False, 3)
    return cum3, cum3[:, CHUNK - 1:CHUNK, :]


def _gla_seq_fwd(z, gate_w, gate_b, head_g, dep=None):
    t = z.shape[0]
    tm = _row_tile(t, ROW_TILE_TARGET, CHUNK)
    cpt = tm // CHUNK
    scale = GLA_HK ** -0.5

    def body(z_ref, gw_ref, gb_ref, hg_ref, o_ref, mix_ref, st_ref, state, kdec_s, e_s):
        @pl.when(pl.program_id(0) == 0)
        def _():
            state[...] = jnp.zeros_like(state)

        cum3, tot3 = _chunk_decay(z_ref[:, R0:R0 + GATE_PAD], gw_ref, gb_ref, cpt)
        dec = jnp.exp(jnp.broadcast_to(tot3, cum3.shape) - cum3).reshape(tm, GLA_DK)
        kdec_s[...] = (z_ref[:, K0:K0 + GLA_DK].astype(F32) * dec).astype(BF16)
        e_s[...] = jnp.exp(jnp.broadcast_to(tot3, (cpt, 8, GLA_DK))).reshape(cpt * 8, GLA_DK)

        def chunk(ci, carry):
            rows = pl.ds(pl.multiple_of(ci * CHUNK, CHUNK), CHUNK)
            e_all = e_s[pl.ds(pl.multiple_of(ci * 8, 8), 8), :][0:1, :]
            st_ref[ci] = state[...].astype(BF16)
            for hd in range(GLA_HEADS):
                ks = slice(hd * GLA_HK, (hd + 1) * GLA_HK)
                vs = slice(hd * GLA_HV, (hd + 1) * GLA_HV)
                v = z_ref[rows, V0 + hd * GLA_HV:V0 + (hd + 1) * GLA_HV]
                st = state[vs, :] * e_all[:, ks] + _dot_tn(v, kdec_s[rows, ks])
                state[vs, :] = st
                q = z_ref[rows, Q0 + hd * GLA_HK:Q0 + (hd + 1) * GLA_HK]
                o_ref[rows, vs] = (_dot_nt(q, st.astype(BF16)) * scale).astype(BF16)
            return carry

        lax.fori_loop(0, cpt, chunk, 0, unroll=GLA_UNROLL)

        for hd in range(GLA_HEADS):
            vs = slice(hd * GLA_HV, (hd + 1) * GLA_HV)
            on = _rms(o_ref[:, vs].astype(F32), hg_ref[...])
            gv = z_ref[:, G0 + hd * GLA_HV:G0 + (hd + 1) * GLA_HV].astype(F32)
            mix_ref[:, vs] = (on * _silu(gv)).astype(BF16)

    return _call_after(
        dep, body, 4,
        [pl.BlockSpec((tm, GLA_IN_PAD), lambda i: (i, 0)),
         pl.BlockSpec((GATE_PAD, GLA_DK), lambda i: (0, 0)), pl.BlockSpec((1, GLA_DK), lambda i: (0, 0)),
         pl.BlockSpec((1, GLA_HV), lambda i: (0, 0))], (z, gate_w, gate_b, head_g), grid=(t // tm,),
        out_specs=[pl.BlockSpec((tm, GLA_DV), lambda i: (i, 0)), pl.BlockSpec((tm, GLA_DV), lambda i: (i, 0)),
                   pl.BlockSpec((cpt, GLA_DV, GLA_HK), lambda i: (i, 0, 0))],
        out_shape=[jax.ShapeDtypeStruct((t, GLA_DV), BF16), jax.ShapeDtypeStruct((t, GLA_DV), BF16),
                   jax.ShapeDtypeStruct((t // CHUNK, GLA_DV, GLA_HK), BF16)],
        scratch_shapes=[pltpu.VMEM((GLA_DV, GLA_HK), F32), pltpu.VMEM((tm, GLA_DK), BF16),
                        pltpu.VMEM((cpt * 8, GLA_DK), F32)],
        compiler_params=_cparams("arbitrary"), name="gla_seq_fwd")


def _gla_seq_bwd(dmix, o, z, states, gate_w, gate_b, head_g, dep=None):
    t = z.shape[0]
    tm = _row_tile(t, ROW_TILE_TARGET, CHUNK)
    cpt = tm // CHUNK
    nt = t // tm
    scale = GLA_HK ** -0.5

    def body(dmix_ref, o_ref, z_ref, st_ref, gw_ref, gb_ref, hg_ref, dz_ref, dgw_ref, dgb_ref, dhg_ref,
             dstate, dec_s, kdec_s, dkdec_s, do_s, e_s, dtot_s):
        @pl.when(pl.program_id(0) == 0)
        def _():
            dstate[...] = jnp.zeros_like(dstate)
            dgw_ref[...] = jnp.zeros_like(dgw_ref)
            dgb_ref[...] = jnp.zeros_like(dgb_ref)
            dhg_ref[...] = jnp.zeros_like(dhg_ref)

        cum3, tot3 = _chunk_decay(z_ref[:, R0:R0 + GATE_PAD], gw_ref, gb_ref, cpt)
        dec = jnp.exp(jnp.broadcast_to(tot3, cum3.shape) - cum3).reshape(tm, GLA_DK)
        dec_s[...] = dec
        kdec_s[...] = z_ref[:, K0:K0 + GLA_DK].astype(F32) * dec
        e_s[...] = jnp.exp(jnp.broadcast_to(tot3, (cpt, 8, GLA_DK))).reshape(cpt * 8, GLA_DK)
        dhg = jnp.zeros((1, GLA_HV), F32)
        for hd in range(GLA_HEADS):
            vs = slice(hd * GLA_HV, (hd + 1) * GLA_HV)
            gcols = slice(G0 + hd * GLA_HV, G0 + (hd + 1) * GLA_HV)
            ov = o_ref[:, vs].astype(F32)
            gv = z_ref[:, gcols].astype(F32)
            dm = dmix_ref[:, vs].astype(F32)
            sg = jax.nn.sigmoid(gv)
            rr = lax.rsqrt(jnp.mean(ov * ov, axis=-1, keepdims=True) + EPS)
            xhat = ov * rr
            don = dm * (gv * sg)
            dz_ref[:, gcols] = (dm * (xhat * hg_ref[...]) * (sg * (1.0 + gv * (1.0 - sg)))).astype(BF16)
            dhg = dhg + jnp.sum(don * xhat, axis=0, keepdims=True)
            dxh = don * hg_ref[...]
            do_s[:, vs] = (rr * (dxh - xhat * jnp.mean(dxh * xhat, axis=-1, keepdims=True)) * scale).astype(BF16)
        dhg_ref[...] += dhg

        def chunk(cj, carry):
            ci = cpt - 1 - cj
            rows = pl.ds(pl.multiple_of(ci * CHUNK, CHUNK), CHUNK)
            erows = pl.ds(pl.multiple_of(ci * 8, 8), 8)
            e_all = e_s[erows, :][0:1, :]
            for hd in range(GLA_HEADS):
                ks = slice(hd * GLA_HK, (hd + 1) * GLA_HK)
                vs = slice(hd * GLA_HV, (hd + 1) * GLA_HV)
                e = e_all[:, ks]
                kdb = kdec_s[rows, ks].astype(BF16)
                v = z_ref[rows, V0 + hd * GLA_HV:V0 + (hd + 1) * GLA_HV]
                q = z_ref[rows, Q0 + hd * GLA_HK:Q0 + (hd + 1) * GLA_HK]
                do = do_s[rows, vs]
                st_prev = st_ref[ci, vs, :].astype(F32)
                st = st_prev * e + _dot_tn(v, kdb)
                dz_ref[rows, Q0 + hd * GLA_HK:Q0 + (hd + 1) * GLA_HK] = _dot(do, st.astype(BF16)).astype(BF16)
                dst = dstate[vs, :] + _dot_tn(do, q)
                dstb = dst.astype(BF16)
                dkdec_s[rows, ks] = _dot(v, dstb)
                dz_ref[rows, V0 + hd * GLA_HV:V0 + (hd + 1) * GLA_HV] = _dot_nt(kdb, dstb).astype(BF16)
                dtot = jnp.sum(dst * st_prev, axis=0, keepdims=True) * e
                dtot_s[erows, ks] = jnp.broadcast_to(dtot, (8, GLA_HK))
                dstate[vs, :] = dst * e
            return carry

        lax.fori_loop(0, cpt, chunk, 0, unroll=GLA_UNROLL)

        dkdec = dkdec_s[...]
        dz_ref[:, K0:K0 + GLA_DK] = (dkdec * dec_s[...]).astype(BF16)
        before = _chunk_sums(dkdec * kdec_s[...], cpt, True, 2)
        dtot3 = dtot_s[...].reshape(cpt, 8, GLA_DK)[:, 0:1, :]
        dlac = (jnp.broadcast_to(dtot3, before.shape) + before).reshape(tm, GLA_DK)
        pre = _dot(z_ref[:, R0:R0 + GATE_PAD], gw_ref[...]) + gb_ref[...]
        dpre = dlac * (1.0 / GATE_NORM) * (1.0 - jax.nn.sigmoid(pre))
        dpb = dpre.astype(BF16)
        dz_ref[:, R0:R0 + GATE_PAD] = _dot_nt(dpb, gw_ref[...]).astype(BF16)
        dgw_ref[...] += _dot_tn(z_ref[:, R0:R0 + GATE_PAD], dpb)
        dgb_ref[...] += jnp.sum(dpre, axis=0, keepdims=True)

    rev = lambda i: (nt - 1 - i, 0)
    return _call_after(
        dep, body, 7,
        [pl.BlockSpec((tm, GLA_DV), rev), pl.BlockSpec((tm, GLA_DV), rev), pl.BlockSpec((tm, GLA_IN_PAD), rev),
         pl.BlockSpec((cpt, GLA_DV, GLA_HK), lambda i: (nt - 1 - i, 0, 0)),
         pl.BlockSpec((GATE_PAD, GLA_DK), lambda i: (0, 0)), pl.BlockSpec((1, GLA_DK), lambda i: (0, 0)),
         pl.BlockSpec((1, GLA_HV), lambda i: (0, 0))],
        (dmix, o, z, states, gate_w, gate_b, head_g), grid=(nt,),
        out_specs=[pl.BlockSpec((tm, GLA_IN_PAD), rev), pl.BlockSpec((GATE_PAD, GLA_DK), lambda i: (0, 0)),
                   pl.BlockSpec((1, GLA_DK), lambda i: (0, 0)), pl.BlockSpec((1, GLA_HV), lambda i: (0, 0))],
        out_shape=[jax.ShapeDtypeStruct((t, GLA_IN_PAD), BF16), jax.ShapeDtypeStruct((GATE_PAD, GLA_DK), F32),
                   jax.ShapeDtypeStruct((1, GLA_DK), F32), jax.ShapeDtypeStruct((1, GLA_HV), F32)],
        scratch_shapes=[pltpu.VMEM((GLA_DV, GLA_HK), F32), pltpu.VMEM((tm, GLA_DK), F32), pltpu.VMEM((tm, GLA_DK), F32),
                        pltpu.VMEM((tm, GLA_DK), F32), pltpu.VMEM((tm, GLA_DV), BF16),
                        pltpu.VMEM((cpt * 8, GLA_DK), F32), pltpu.VMEM((cpt * 8, GLA_DK), F32)],
        compiler_params=_cparams("arbitrary"), name="gla_seq_bwd")


def _sum_halves(g, recv, c_idx, name):
    n, r, cdim = g.shape
    h = r // 2
    tr = _row_tile(h, 256, 8)
    nh = h // tr

    def body(c_ref, g_ref, r_ref, o_ref):
        o_ref[...] = (g_ref[...] + r_ref[...]).astype(BF16)

    return pl.pallas_call(
        body,
        grid_spec=pltpu.PrefetchScalarGridSpec(
            num_scalar_prefetch=1, grid=(n, nh),
            in_specs=[pl.BlockSpec((None, tr, cdim), lambda s, i, c: (s, c[0] * nh + i, 0)),
                      pl.BlockSpec((None, tr, cdim), lambda s, i, c: (s, i, 0))],
            out_specs=pl.BlockSpec((None, tr, cdim), lambda s, i, c: (s, i, 0))),
        out_shape=jax.ShapeDtypeStruct((n, h, cdim), BF16),
        compiler_params=_cparams("parallel", "parallel"), name=name)(c_idx, g, recv)


def _sum_slots(x, name):
    n, r, cdim = x.shape
    tr = _row_tile(r, 256, 8)

    def body(x_ref, o_ref):
        acc = x_ref[0].astype(F32)
        for j in range(1, n):
            acc = acc + x_ref[j].astype(F32)
        o_ref[...] = acc

    return pl.pallas_call(
        body, grid=(r // tr,),
        in_specs=[pl.BlockSpec((n, tr, cdim), lambda i: (0, i, 0))],
        out_specs=pl.BlockSpec((tr, cdim), lambda i: (i, 0)),
        out_shape=jax.ShapeDtypeStruct((r, cdim), F32),
        compiler_params=_cparams("parallel"), name=name)(x)


def _sum_own_and_slots(own, slots, chip_idx, name):
    n, r, cdim = own.shape
    tr = _row_tile(r, 256, 8)

    def body(s_ref, own_ref, a_ref, b_ref, c_ref, o_ref):
        o_ref[...] = (own_ref[...].astype(F32) + a_ref[...].astype(F32) + b_ref[...].astype(F32)
                      + c_ref[...].astype(F32))

    def slot(dd):
        return pl.BlockSpec((None, tr, cdim), lambda i, s: ((s[0] + dd) % n, i, 0))

    return pl.pallas_call(
        body,
        grid_spec=pltpu.PrefetchScalarGridSpec(
            num_scalar_prefetch=1, grid=(r // tr,), in_specs=[slot(0), slot(1), slot(2), slot(3)],
            out_specs=pl.BlockSpec((tr, cdim), lambda i, s: (i, 0))),
        out_shape=jax.ShapeDtypeStruct((r, cdim), F32),
        compiler_params=_cparams("parallel"), name=name)(chip_idx, own, slots, slots, slots)


def _add2(a, b, name):
    r, cdim = a.shape
    tr = _row_tile(r, 256, 8)

    def body(a_ref, b_ref, o_ref):
        o_ref[...] = a_ref[...] + b_ref[...]

    spec = pl.BlockSpec((tr, cdim), lambda i: (i, 0))
    return pl.pallas_call(body, grid=(r // tr,), in_specs=[spec, spec], out_specs=spec,
                          out_shape=jax.ShapeDtypeStruct((r, cdim), F32),
                          compiler_params=_cparams("parallel"), name=name)(a, b)


def _adamw_half(w, gs, m, v, half_idx, prev, name, dep=None):
    nl, _, h, cdim = w.shape
    tr = _row_tile(h, 256, 8)
    nprev = 0 if prev is None else 4
    extra = [] if dep is None else [dep]

    def body(s_ref, w_ref, m_ref, v_ref, *rest):
        g_refs = rest[:nl]
        go_ref, d_ref, mo_ref, vo_ref = rest[nl + nprev + len(extra):]
        layer = pl.program_id(0)
        gv = g_refs[0][...]
        for j in range(1, nl):
            gv = jnp.where(layer == j, g_refs[j][...], gv)
        go_ref[...] = gv
        mn = ADAM_B1 * m_ref[...] + (1.0 - ADAM_B1) * gv
        vn = ADAM_B2 * v_ref[...] + (1.0 - ADAM_B2) * (gv * gv)
        m_hat = mn / (1.0 - ADAM_B1 ** ADAM_STEP)
        v_hat = vn / (1.0 - ADAM_B2 ** ADAM_STEP)
        d_ref[...] = -ADAM_LR * (m_hat / (jnp.sqrt(v_hat) + ADAM_EPS) + ADAM_WD * w_ref[...])
        mo_ref[...] = mn
        vo_ref[...] = vn

    half = pl.BlockSpec((None, None, tr, cdim), lambda l, i, s: (l, s[0], i, 0))

    def of_layer(j):
        return pl.BlockSpec((tr, cdim), lambda l, i, s: (jnp.where(l == j, i, 0), 0))

    shp = jax.ShapeDtypeStruct(w.shape, F32)
    return pl.pallas_call(
        body,
        grid_spec=pltpu.PrefetchScalarGridSpec(
            num_scalar_prefetch=1, grid=(nl, h // tr),
            in_specs=[half] * 3 + [of_layer(j) for j in range(nl)] + [ANY_SPEC] * (nprev + len(extra)),
            out_specs=[half] * 4),
        out_shape=[shp] * 4, input_output_aliases={4 + nl + k: k for k in range(nprev)},
        compiler_params=_cparams("arbitrary", "arbitrary"), name=name,
    )(half_idx, w, m, v, *gs, *([] if prev is None else prev), *extra)


def _adamw_many(ws, gs, ms, vs):
    n = len(ws)

    def body(*refs):
        for i in range(n):
            w_ref, g_ref, m_ref, v_ref = refs[i], refs[n + i], refs[2 * n + i], refs[3 * n + i]
            d_ref, mo_ref, vo_ref = refs[4 * n + i], refs[5 * n + i], refs[6 * n + i]
            gv = g_ref[...]
            mn = ADAM_B1 * m_ref[...] + (1.0 - ADAM_B1) * gv
            vn = ADAM_B2 * v_ref[...] + (1.0 - ADAM_B2) * (gv * gv)
            m_hat = mn / (1.0 - ADAM_B1 ** ADAM_STEP)
            v_hat = vn / (1.0 - ADAM_B2 ** ADAM_STEP)
            d_ref[...] = -ADAM_LR * (m_hat / (jnp.sqrt(v_hat) + ADAM_EPS) + ADAM_WD * w_ref[...])
            mo_ref[...] = mn
            vo_ref[...] = vn

    shapes = [jax.ShapeDtypeStruct(w.shape, F32) for w in ws]
    outs = pl.pallas_call(body, out_shape=shapes * 3, name="adamw_small")(*ws, *gs, *ms, *vs)
    return outs[:n], outs[n:2 * n], outs[2 * n:]


def _split_rows(a):
    return a.reshape(a.shape[0], 2, a.shape[1] // 2, a.shape[2])


def _place():
    x, y, c = lax.axis_index("x"), lax.axis_index("y"), lax.axis_index("c")
    chips = [(1 - x, y), (x, 1 - y), (1 - x, 1 - y)]
    return x, y, c, chips


def _remote(src, dst, send_sem, recv_sem, to):
    return pltpu.make_async_remote_copy(src_ref=src, dst_ref=dst, send_sem=send_sem, recv_sem=recv_sem,
                                        device_id=to, device_id_type=MESH)


def _plan_gather(n_halved):
    def plan(src_refs, land_refs):
        x, y, c, chips = _place()
        me = 2 * x + y
        copies = []
        for k, (src, land) in enumerate(zip(src_refs, land_refs)):
            for (px, py) in chips:
                frm = 2 * px + py
                if k < n_halved:
                    copies.append((src.at[c], land.at[me, c], (px, py, c), land.at[frm, c]))
                else:
                    copies.append((src, land.at[me], (px, py, c), land.at[frm]))
        return copies
    return plan


def _plan_share(src_refs, land_refs):
    x, y, c, chips = _place()
    me = 2 * x + y
    sib = (x, y, 1 - c)
    copies = []
    for src, land in zip(src_refs, land_refs):
        copies.append((src, land.at[me], sib, land.at[me]))
        for (px, py) in chips:
            frm = 2 * px + py
            copies.append((land.at[frm, c], land.at[frm, c], sib, land.at[frm, 1 - c]))
    return copies


def _plan_scatter(n_parts):
    def plan(src_refs, land_refs):
        x, y, c, chips = _place()
        me = 2 * x + y
        copies = []
        for k, (src, land) in enumerate(zip(src_refs, land_refs)):
            for (px, py) in chips:
                to = 2 * px + py
                copies.append((src.at[to] if k < n_parts else src, land.at[me], (px, py, c), land.at[to]))
        return copies
    return plan


def _plan_exchange(n_split):
    def plan(src_refs, land_refs):
        x, y, c, _ = _place()
        sib = (x, y, 1 - c)
        return [(src.at[:, 1 - c] if k < n_split else src, land, sib, land)
                for k, (src, land) in enumerate(zip(src_refs, land_refs))]
    return plan


def _hbm(a):
    return pltpu.HBM(a.shape, a.dtype)


def _start_copies(name, srcs, lands, plan, ncopy, dep=None):
    ns, nl = len(srcs), len(lands)
    nin = ns + nl + (0 if dep is None else 1)

    def body(*refs):
        send_sems, recv_sems, token = refs[nin], refs[nin + 1], refs[-1]
        for k, (src, dst, dev, _) in enumerate(plan(refs[:ns], refs[ns:ns + nl])):
            _remote(src, dst, send_sems.at[k], recv_sems.at[k], dev).start()
        token[...] = jnp.zeros_like(token)

    args = [pltpu.with_memory_space_constraint(a, pltpu.HBM) for a in list(srcs) + list(lands)]
    outs = pl.pallas_call(
        body, name=name,
        out_shape=(pltpu.SemaphoreType.DMA((ncopy,)), pltpu.SemaphoreType.DMA((ncopy,)),
                   *[_hbm(a) for a in list(srcs) + list(lands)], jax.ShapeDtypeStruct((8, 128), F32)),
        in_specs=[HBM_SPEC] * (ns + nl) + ([] if dep is None else [ANY_SPEC]),
        out_specs=(SEM_SPEC, SEM_SPEC, *([HBM_SPEC] * (ns + nl)), pl.BlockSpec(memory_space=pltpu.VMEM)),
        input_output_aliases={i: 2 + i for i in range(ns + nl)},
        compiler_params=pltpu.CompilerParams(has_side_effects=SIDE_EFFECT),
    )(*args, *([] if dep is None else [dep]))
    return outs[0], outs[1], list(outs[2:2 + ns]), list(outs[2 + ns:2 + ns + nl]), outs[-1]


def _wait_copies(name, started, plan, after):
    send_sems, recv_sems, srcs, lands, _ = started
    ns, nl = len(srcs), len(lands)
    after = list(after) if isinstance(after, (list, tuple)) else [after]

    def body(*refs):
        send_ref, recv_ref = refs[ns + nl], refs[ns + nl + 1]
        for k, (src, _, dev, mine) in enumerate(plan(refs[:ns], refs[ns:ns + nl])):
            copy = _remote(src, mine, send_ref.at[k], recv_ref.at[k], dev)
            copy.wait_send()
            copy.wait_recv()

    outs = pl.pallas_call(
        body, name=name, out_shape=tuple(_hbm(a) for a in srcs + lands),
        in_specs=[HBM_SPEC] * (ns + nl) + [SEM_SPEC, SEM_SPEC] + [ANY_SPEC] * len(after),
        out_specs=tuple([HBM_SPEC] * (ns + nl)),
        input_output_aliases={i: i for i in range(ns + nl)},
        compiler_params=pltpu.CompilerParams(has_side_effects=SIDE_EFFECT),
    )(*srcs, *lands, send_sems, recv_sems, *after)
    return list(outs[:ns]), list(outs[ns:])


def _share_with_sibling(name, srcs, lands):
    n = len(srcs)

    def body(*refs):
        src_refs, land_refs, out_refs = refs[:n], refs[n:2 * n], refs[2 * n:3 * n]
        send_sem, recv_sem = refs[3 * n:]
        x, y, c, chips = _place()
        me = 2 * x + y
        sib = (x, y, 1 - c)
        sends, recvs = [], []
        for k in range(n):
            sems = (send_sem.at[4 * k], recv_sem.at[4 * k])
            sends.append(_remote(src_refs[k], out_refs[k].at[me], *sems, sib))
            recvs.append(_remote(src_refs[k], out_refs[k].at[me], *sems, sib))
            for j, (px, py) in enumerate(chips):
                frm = 2 * px + py
                sems = (send_sem.at[4 * k + 1 + j], recv_sem.at[4 * k + 1 + j])
                sends.append(_remote(land_refs[k].at[frm, c], out_refs[k].at[frm, c], *sems, sib))
                recvs.append(_remote(land_refs[k].at[frm, c], out_refs[k].at[frm, 1 - c], *sems, sib))
        for cp in sends:
            cp.start()
        for cp in recvs:
            cp.wait_recv()
        for cp in sends:
            cp.wait_send()

    return pl.pallas_call(
        body, name=name, in_specs=[HBM_SPEC] * (2 * n), out_specs=[HBM_SPEC] * n,
        out_shape=[jax.ShapeDtypeStruct(a.shape, a.dtype) for a in lands],
        input_output_aliases={n + k: k for k in range(n)},
        scratch_shapes=[pltpu.SemaphoreType.DMA((4 * n,)), pltpu.SemaphoreType.DMA((4 * n,))],
    )(*srcs, *lands)


def _pack(arrs):
    flat = jnp.concatenate([a.reshape(-1).astype(F32) for a in arrs])
    n = flat.shape[0]
    rows = -(-n // PACK_WIDTH)
    rows = -(-rows // 8) * 8
    return jnp.pad(flat, (0, rows * PACK_WIDTH - n)).reshape(rows, PACK_WIDTH)


def _unpack(buf, shapes):
    flat = buf.reshape(-1)
    out, off = [], 0
    for shp in shapes:
        n = 1
        for s in shp:
            n *= s
        out.append(flat[off:off + n].reshape(shp))
        off += n
    return out


def _unshard_cols(stacked):
    moved = jnp.moveaxis(stacked, 0, -2)
    return moved.reshape(moved.shape[:-2] + (moved.shape[-2] * moved.shape[-1],))


def _take_cols(blocks, start, width):
    bw = blocks.shape[2]
    pieces, lo = [], start
    while lo < start + width:
        b = lo // bw
        hi = min(start + width, (b + 1) * bw)
        pieces.append(blocks[b][:, lo - b * bw:hi - b * bw])
        lo = hi
    return jnp.concatenate(pieces, axis=1)


def _col_shard(full, s, width):
    return lax.dynamic_slice_in_dim(full, s * width, width, axis=full.ndim - 1)


def kernel(x, meta_tokens, mix_norm_g, ffn_norm_g, ffn_w1, ffn_w2, cp_w_in, cp_conv_w, cp_conv_b, cp_ln_g, cp_ln_b, cp_pool_w, cp_pool_scale, cp_w_out, gla_w_in, gla_gate_w2, gla_gate_b, gla_head_g, gla_w_out, final_norm_g, loss_target, m_meta_tokens, m_mix_norm_g, m_ffn_norm_g, m_ffn_w1, m_ffn_w2, m_cp_w_in, m_cp_conv_w, m_cp_conv_b, m_cp_ln_g, m_cp_ln_b, m_cp_pool_w, m_cp_pool_scale, m_cp_w_out, m_gla_w_in, m_gla_gate_w2, m_gla_gate_b, m_gla_head_g, m_gla_w_out, m_final_norm_g, v_meta_tokens, v_mix_norm_g, v_ffn_norm_g, v_ffn_w1, v_ffn_w2, v_cp_w_in, v_cp_conv_w, v_cp_conv_b, v_cp_ln_g, v_cp_ln_b, v_cp_pool_w, v_cp_pool_scale, v_cp_w_out, v_gla_w_in, v_gla_gate_w2, v_gla_gate_b, v_gla_head_g, v_gla_w_out, v_final_norm_g):
    d = D_MODEL
    chip = 2 * lax.axis_index("x") + lax.axis_index("y")
    core = lax.axis_index("c")
    seq = x.shape[1]
    t = seq + CHUNK

    sharded_small = [meta_tokens, cp_conv_w, gla_gate_w2, gla_gate_b, gla_head_g]

    def halves(w):
        return w.astype(BF16).reshape(2, w.shape[0] // 2, w.shape[1])

    def unhalve(g):
        return g.reshape(N_CHIPS, 2 * g.shape[2], g.shape[3])

    def start_gather(name, srcs, dep, whole=()):
        lands = [lax.empty((N_CHIPS,) + s.shape, s.dtype) for s in srcs]
        for a in whole:
            lands.append(lax.dynamic_update_slice(jnp.zeros((N_CHIPS,) + a.shape, a.dtype), a[None], (chip,) + (0,) * a.ndim))
        plan = _plan_gather(len(srcs))
        return _start_copies(name, list(srcs) + list(whole), lands, plan, 3 * len(lands), dep), plan, len(srcs)

    def arrived(name, gather, after):
        started, plan, n = gather
        srcs, lands = _wait_copies(name + "_wait", started, plan, after)
        return srcs[:n], lands[:n], lands[n:]

    cp_gather = start_gather("gather_cp_start", [halves(cp_w_in[0]), halves(cp_w_out[0])], None, [_pack(sharded_small)])
    ffn0_gather = start_gather("gather_ffn0_start", [halves(ffn_w1[0]), halves(ffn_w2[0])], cp_gather[0][-1])
    gla_gather = start_gather("gather_gla_start", [halves(gla_w_in[0]), halves(gla_w_out[0])], ffn0_gather[0][-1])
    ffn1_gather = start_gather("gather_ffn1_start", [halves(ffn_w1[1]), halves(ffn_w2[1])], gla_gather[0][-1])
    h0_rows = jnp.concatenate([jnp.zeros((CHUNK, d), F32) + ffn1_gather[0][-1][0, 0], x[0]], axis=0)
    cp_srcs, cp_lands, (small_g,) = arrived("gather_cp", cp_gather, [ffn1_gather[0][-1], h0_rows])
    cpin_g, cpout_g = [unhalve(g) for g in _share_with_sibling("gather_cp_share", cp_srcs, cp_lands)]
    per_chip = [_unpack(small_g[j], [a.shape for a in sharded_small]) for j in range(N_CHIPS)]
    meta_f, conv_w_f, gate_w_f, gate_b_f, head_g_f = [
        jnp.concatenate([per_chip[j][i] for j in range(N_CHIPS)], axis=-1) for i in range(len(sharded_small))]
    conv_w_f, gate_w_f = conv_w_f[0], gate_w_f[0]
    w_cp_in = _unshard_cols(cpin_g)
    w_cp_out = cpout_g.reshape(CONV_DIM + POOL_DIM, d)
    gate_w_pad = jnp.pad(gate_w_f, ((0, GATE_PAD - GATE_RANK), (0, 0))).astype(BF16)
    row = lambda a: a.reshape(1, -1)
    c_idx = core.reshape(1).astype(jnp.int32)
    chip_idx = chip.reshape(1).astype(jnp.int32)

    h0 = lax.dynamic_update_slice(h0_rows, meta_f, (PAD_ROWS, 0))
    z0, u0 = _norm_matmul(h0, row(mix_norm_g[0]), w_cp_in, 512, "cp_in_proj")
    c0, pm0, mix0 = _cp_seq_fwd(z0, conv_w_f, cp_conv_b, cp_ln_g, cp_ln_b, cp_pool_w[0], cp_pool_scale)
    ffn0_srcs, ffn0_lands, _ = arrived("gather_ffn0", ffn0_gather, mix0)
    ffn0_share = _start_copies("gather_ffn0_share_start", ffn0_srcs, ffn0_lands, _plan_share, 4 * len(ffn0_srcs))
    h1 = _matmul_residual(mix0, w_cp_out, h0, "cp_out_proj", dep=ffn0_share[-1])
    w1g0, w2g0 = [unhalve(g) for g in _wait_copies("gather_ffn0_share_wait", ffn0_share, _plan_share, h1)[1]]
    h2, hp0, uf0 = _ffn_fwd(h1, row(ffn_norm_g[0]), w1g0, w2g0, "ffn0_fwd")
    gla_srcs, gla_lands, _ = arrived("gather_gla", gla_gather, h2)
    glain_g, glaout_g = [unhalve(g) for g in _share_with_sibling("gather_gla_share", gla_srcs, gla_lands)]
    w_gla_in = jnp.concatenate([glain_g[j] for j in range(N_CHIPS)] + [jnp.zeros((d, GLA_IN_PAD - GLA_IN), BF16)], axis=1)
    w_gla_out = glaout_g.reshape(GLA_DV, d)
    z1, u2 = _norm_matmul(h2, row(mix_norm_g[1]), w_gla_in, 640, "gla_in_proj")
    ffn1_srcs, ffn1_lands, _ = arrived("gather_ffn1", ffn1_gather, z1)
    ffn1_share = _start_copies("gather_ffn1_share_start", ffn1_srcs, ffn1_lands, _plan_share, 4 * len(ffn1_srcs))
    o1, mix1, states = _gla_seq_fwd(z1, gate_w_pad, gate_b_f, head_g_f, dep=ffn1_share[-1])
    h3 = _matmul_residual(mix1, w_gla_out, h2, "gla_out_proj")
    w1g1, w2g1 = [unhalve(g) for g in _wait_copies("gather_ffn1_share_wait", ffn1_share, _plan_share, h3)[1]]
    h4, hp1, uf1 = _ffn_fwd(h3, row(ffn_norm_g[1]), w1g1, w2g1, "ffn1_fwd")

    def start_exchange(name, grads):
        srcs = [_split_rows(g) for g in grads]
        lands = [lax.empty((g.shape[0], g.shape[1] // 2, g.shape[2]), g.dtype) for g in grads]
        return _start_copies(name + "_exchange_start", srcs, lands, _plan_exchange(len(grads)), len(grads))

    def start_scatter(name, exchange, after):
        srcs, recv = _wait_copies(name + "_exchange_wait", exchange, _plan_exchange(len(exchange[2])), after)
        parts = [_sum_halves(g.reshape(g.shape[0], -1, g.shape[3]), r, c_idx, "%s_chip_sum_%d" % (name, k))
                 for k, (g, r) in enumerate(zip(srcs, recv))]
        lands = [lax.empty(p.shape, p.dtype) for p in parts]
        return _start_copies(name + "_scatter_start", parts, lands, _plan_scatter(len(parts)), 3 * len(parts))

    def finish_reduce(name, started, after):
        n = len(started[2])
        parts, lands = _wait_copies(name + "_scatter_wait", started, _plan_scatter(n), after)
        return [_sum_own_and_slots(p, s, chip_idx, "%s_slot_sum_%d" % (name, k)) for k, (p, s) in enumerate(zip(parts, lands))]

    dh4, d_final_g, loss_part = _loss_bwd(h4, row(final_norm_g), loss_target[0])

    dh3, dhp1, d_ffn_g1 = _ffn_bwd_data(dh4, h3, row(ffn_norm_g[1]), hp1, w1g1, w2g1, "ffn1_bwd")
    dw1_1 = _wgrad(uf1, dhp1, N_CHIPS, d, d, False, True, False, "ffn1_dw1")
    dw2_1 = _wgrad(hp1, dh4, N_CHIPS, d, d, True, False, True, "ffn1_dw2")
    ffn1_exchange = start_exchange("ffn1", [dw1_1, dw2_1])

    dmix1 = _dgrad(dh3, w_gla_out, "gla_out_dgrad", dep=ffn1_exchange[-1])
    dw_gla_out = _wgrad(mix1, dh3, 1, GLA_DV, d, False, False, False, "gla_out_dw", dep=ffn1_exchange[-1])
    ffn1_reduce = start_scatter("ffn1", ffn1_exchange, [dmix1, dw_gla_out])
    dz1, d_gate_w, d_gate_b, d_head_g = _gla_seq_bwd(dmix1, o1, z1, states, gate_w_pad, gate_b_f, head_g_f,
                                                     dep=ffn1_reduce[-1])
    dh2, d_mix_g1 = _dgrad_norm_bwd(dz1, w_gla_in, h2, row(mix_norm_g[1]), dh3, 640, "gla_in_dgrad")
    dw_gla_in = _wgrad(u2, dz1, GLA_IN_PAD // 640, d, 640, False, True, False, "gla_in_dw")
    gla_in_shards = jnp.stack([_take_cols(dw_gla_in, j * (GLA_IN // N_CHIPS), GLA_IN // N_CHIPS) for j in range(N_CHIPS)])
    gla_exchange = start_exchange("gla", [gla_in_shards, dw_gla_out.reshape(N_CHIPS, -1, d)])

    dh1, dhp0, d_ffn_g0 = _ffn_bwd_data(dh2, h1, row(ffn_norm_g[0]), hp0, w1g0, w2g0, "ffn0_bwd", dep=gla_exchange[-1])
    gla_reduce = start_scatter("gla", gla_exchange, dh1)
    dw1_0 = _wgrad(uf0, dhp0, N_CHIPS, d, d, False, True, False, "ffn0_dw1", dep=gla_reduce[-1])
    dw2_0 = _wgrad(hp0, dh2, N_CHIPS, d, d, True, False, True, "ffn0_dw2")
    ffn0_exchange = start_exchange("ffn0", [dw1_0, dw2_0])

    dmix0 = _dgrad(dh1, w_cp_out, "cp_out_dgrad", dep=ffn0_exchange[-1])
    dw_cp_out = _wgrad(mix0, dh1, 1, CONV_DIM + POOL_DIM, d, False, False, False, "cp_out_dw", dep=ffn0_exchange[-1])
    ffn0_reduce = start_scatter("ffn0", ffn0_exchange, [dmix0, dw_cp_out])
    dz0, d_conv_w, d_cp_vec, d_pool_w = _cp_seq_bwd(dmix0, z0, c0, pm0, conv_w_f, cp_ln_g, cp_ln_b, cp_pool_w[0],
                                                    cp_pool_scale, dep=ffn0_reduce[-1])
    grad_x, dh0_head, d_mix_g0 = _dgrad_norm_bwd_input(dz0, w_cp_in, h0, row(mix_norm_g[0]), dh1, 512, "cp_in_dgrad")
    grad_x = grad_x[None]
    dw_cp_in = _wgrad(u0, dz0, N_CHIPS, d, CP_IN // N_CHIPS, False, True, False, "cp_in_dw")

    cp_grads = [dw_cp_in, dw_cp_out.reshape(N_CHIPS, -1, d)]
    small_full = [dh0_head[PAD_ROWS:CHUNK],jnp.concatenate([d_mix_g0, d_mix_g1], axis=0),
                  jnp.concatenate([d_ffn_g0, d_ffn_g1], axis=0), d_conv_w[:CONV_WIDTH][None],
                  d_cp_vec[0:1], d_cp_vec[1:2], d_cp_vec[2:3], d_pool_w[None], d_cp_vec[3:4],
                  d_gate_w[:GATE_RANK][None], d_gate_b, d_head_g, d_final_g[0], loss_part[0, 0:1]]
    small_mine = _pack(small_full)
    cp_exchange_plan = _plan_exchange(len(cp_grads))
    cp_exchange = _start_copies(
        "cp_exchange_start", [_split_rows(g) for g in cp_grads] + [small_mine],
        [lax.empty((g.shape[0], g.shape[1] // 2, g.shape[2]), F32) for g in cp_grads] + [lax.empty(small_mine.shape, F32)],
        cp_exchange_plan, len(cp_grads) + 1)
    red_ffn1 = finish_reduce("ffn1", ffn1_reduce, cp_exchange[-1])
    red_gla = finish_reduce("gla", gla_reduce, cp_exchange[-1])
    red_ffn0 = finish_reduce("ffn0", ffn0_reduce, cp_exchange[-1])
    cp_srcs, cp_recv = _wait_copies("cp_exchange_wait", cp_exchange, cp_exchange_plan, [red_ffn1[1], red_gla[1], red_ffn0[1]])
    chip_sums = [_sum_halves(g.reshape(g.shape[0], -1, g.shape[3]), r, c_idx, "cp_chip_sum_%d" % k)
                 for k, (g, r) in enumerate(zip(cp_srcs[:-1], cp_recv[:-1]))]
    small_chip = _add2(cp_srcs[-1], cp_recv[-1], "chip_sum_small")
    small_slots = lax.dynamic_update_slice(jnp.zeros((N_CHIPS,) + small_chip.shape, F32), small_chip[None], (chip, 0, 0))
    cp_lands = [lax.empty(p.shape, p.dtype) for p in chip_sums] + [small_slots]
    cp_reduce = _start_copies("cp_scatter_start", chip_sums + [small_chip], cp_lands, _plan_scatter(len(chip_sums)),
                              3 * (len(chip_sums) + 1))

    big = {"w1": (ffn_w1, m_ffn_w1, v_ffn_w1), "w2": (ffn_w2, m_ffn_w2, v_ffn_w2),
           "cp_in": (cp_w_in, m_cp_w_in, v_cp_w_in), "cp_out": (cp_w_out, m_cp_w_out, v_cp_w_out),
           "gla_in": (gla_w_in, m_gla_w_in, v_gla_w_in), "gla_out": (gla_w_out, m_gla_w_out, v_gla_w_out)}
    other_idx = (1 - core).reshape(1).astype(jnp.int32)

    def adamw_by_halves(tag, reduced):
        flat = [r for n in reduced for r in reduced[n]]
        join_plan = _plan_exchange(0)
        join = _start_copies(tag + "_join_start", flat, [lax.empty(r.shape, F32) for r in flat], join_plan, len(flat))
        views = {n: [_split_rows(a) for a in big[n]] for n in reduced}
        own, k = {}, 0
        for n in reduced:
            mine = join[2][k:k + len(reduced[n])]
            k += len(reduced[n])
            own[n] = _adamw_half(views[n][0], mine, views[n][1], views[n][2], c_idx, None, "adamw_%s_own" % n)
        _, arrived_halves = _wait_copies(tag + "_join_wait", join, join_plan, [own[n][1] for n in reduced])
        outs, k = {}, 0
        for n in reduced:
            theirs = arrived_halves[k:k + len(reduced[n])]
            k += len(reduced[n])
            res = _adamw_half(views[n][0], theirs, views[n][1], views[n][2], other_idx, own[n], "adamw_%s_sibling" % n)
            outs[n] = [o.reshape(big[n][0].shape) for o in res]
        return outs

    big_out = adamw_by_halves("ffn_gla", {"w1": [red_ffn0[0], red_ffn1[0]], "w2": [red_ffn0[1], red_ffn1[1]],
                                          "gla_in": [red_gla[0]], "gla_out": [red_gla[1]]})
    cp_parts, cp_slots = _wait_copies("cp_scatter_wait", cp_reduce, _plan_scatter(len(chip_sums)), big_out["gla_out"][1])
    red_cp = [_sum_own_and_slots(a, s, chip_idx, "cp_slot_sum_%d" % k)
              for k, (a, s) in enumerate(zip(cp_parts[:-1], cp_slots[:-1]))]
    small_red = _sum_slots(cp_slots[-1], "slot_sum_small")
    big_out.update(adamw_by_halves("cp", {"cp_in": [red_cp[0]], "cp_out": [red_cp[1]]}))

    (g_meta, g_mix, g_ffn, g_conv_w, g_conv_b, g_ln_g, g_ln_b, g_pool_w, g_pool_scale, g_gate_w, g_gate_b, g_head,
     g_final, loss_sum) = _unpack(small_red, [a.shape for a in small_full])
    g_meta = _col_shard(g_meta, chip, meta_tokens.shape[-1])
    g_conv_w = _col_shard(g_conv_w, chip, cp_conv_w.shape[-1])
    g_gate_w = _col_shard(g_gate_w, chip, gla_gate_w2.shape[-1])
    g_gate_b = _col_shard(g_gate_b, chip, gla_gate_b.shape[-1])
    g_head = _col_shard(g_head, chip, gla_head_g.shape[-1])
    small_w = [meta_tokens, mix_norm_g, ffn_norm_g, cp_conv_w, cp_conv_b, cp_ln_g, cp_ln_b, cp_pool_w, cp_pool_scale,
               gla_gate_w2, gla_gate_b, gla_head_g, final_norm_g]
    small_m = [m_meta_tokens, m_mix_norm_g, m_ffn_norm_g, m_cp_conv_w, m_cp_conv_b, m_cp_ln_g, m_cp_ln_b, m_cp_pool_w,
               m_cp_pool_scale, m_gla_gate_w2, m_gla_gate_b, m_gla_head_g, m_final_norm_g]
    small_v = [v_meta_tokens, v_mix_norm_g, v_ffn_norm_g, v_cp_conv_w, v_cp_conv_b, v_cp_ln_g, v_cp_ln_b, v_cp_pool_w,
               v_cp_pool_scale, v_gla_gate_w2, v_gla_gate_b, v_gla_head_g, v_final_norm_g]
    small_g = [g_meta, g_mix, g_ffn, g_conv_w, g_conv_b, g_ln_g, g_ln_b, g_pool_w, g_pool_scale, g_gate_w, g_gate_b,
               g_head, g_final]
    shapes = [w.shape for w in small_w]
    small_g = [g.reshape(s) for g, s in zip(small_g, shapes)]
    at_least_2d = lambda arrs: [a.reshape(1, -1) if a.ndim == 1 else a for a in arrs]
    s_delta, s_m, s_v = _adamw_many(at_least_2d(small_w), at_least_2d(small_g), at_least_2d(small_m), at_least_2d(small_v))
    s_delta, s_m, s_v = [[a.reshape(s) for a, s in zip(group, shapes)] for group in (s_delta, s_m, s_v)]

    order = ["meta", "mix", "ffn", "w1", "w2", "cp_in", "conv_w", "conv_b", "ln_g", "ln_b", "pool_w", "pool_scale",
             "cp_out", "gla_in", "gate_w", "gate_b", "head", "gla_out", "final"]
    small_names = ["meta", "mix", "ffn", "conv_w", "conv_b", "ln_g", "ln_b", "pool_w", "pool_scale", "gate_w", "gate_b",
                   "head", "final"]
    big_names = ["w1", "w2", "cp_in", "cp_out", "gla_in", "gla_out"]
    table = {n: (small_g[i], s_delta[i], s_m[i], s_v[i]) for i, n in enumerate(small_names)}
    table.update({n: tuple(big_out[n]) for n in big_names})
    loss = loss_sum.reshape(())
    return (loss, grad_x, *[table[n][0] for n in order], *[table[n][1] for n in order],
            *[table[n][2] for n in order], *[table[n][3] for n in order])
```

```python
import functools

import jax
import jax.numpy as jnp
from jax import lax
from jax.experimental import pallas as pl
from jax.experimental.pallas import tpu as pltpu

F32 = jnp.float32
BF16 = jnp.bfloat16

D_MODEL = 1024
N_META = 16
CHUNK = 64
PAD_ROWS = CHUNK - N_META
EPS = 1e-5
CONV_DIM = 512
CONV_WIDTH = 31
CONV_HALO = 32
POOL_DIM = 512
POOL_WINDOWS = (2, 4, 8, 16)
POOL_GROUP = 128
POOL_HALO = 16
CP_IN = 2 * CONV_DIM + POOL_DIM
GLA_HEADS = 4
GLA_DK = 512
GLA_DV = 1024
GLA_HK = GLA_DK // GLA_HEADS
GLA_HV = GLA_DV // GLA_HEADS
GATE_RANK = 16
GATE_PAD = 128
GATE_NORM = 16.0
GLA_IN = 2 * GLA_DK + 2 * GLA_DV + GATE_RANK
GLA_IN_PAD = 2 * GLA_DK + 2 * GLA_DV + GATE_PAD
N_CHIPS = 4
ADAM_LR = 0.001
ADAM_B1 = 0.9
ADAM_B2 = 0.999
ADAM_EPS = 1e-08
ADAM_WD = 0.01
ADAM_STEP = 10

VMEM_LIMIT_BYTES = 56 * 1024 * 1024
ROW_TILE_TARGET = 832
TOKEN_TILE_TARGET = 1040
PACK_WIDTH = 1024
MESH = pl.DeviceIdType.MESH
HBM_SPEC = pl.BlockSpec(memory_space=pltpu.HBM)
ANY_SPEC = pl.BlockSpec(memory_space=pl.ANY)
SEM_SPEC = pl.BlockSpec(memory_space=pltpu.SEMAPHORE)
SIDE_EFFECT = pltpu.SideEffectType.DATAFLOW_SIDE_EFFECTING


def _cparams(*sem):
    return pltpu.CompilerParams(dimension_semantics=sem, vmem_limit_bytes=VMEM_LIMIT_BYTES)


def _row_tile(t, target, mult):
    best = mult
    for cand in range(mult, min(t, target) + 1, mult):
        if t % cand == 0:
            best = cand
    assert t % best == 0, (t, best)
    return best


def _rms(h, g):
    return h * lax.rsqrt(jnp.mean(h * h, axis=-1, keepdims=True) + EPS) * g


def _rms_bwd(h, g, du):
    r = lax.rsqrt(jnp.mean(h * h, axis=-1, keepdims=True) + EPS)
    xhat = h * r
    dxh = du * g
    dh = r * (dxh - xhat * jnp.mean(dxh * xhat, axis=-1, keepdims=True))
    return dh, du * xhat


def _valid_rows(i, tm):
    row = i * tm + lax.broadcasted_iota(jnp.int32, (tm, 1), 0)
    return row >= PAD_ROWS


def _dot(a, b):
    return jnp.dot(a, b, preferred_element_type=F32)


def _dot_nt(a, b):
    return lax.dot_general(a, b, (((1,), (1,)), ((), ())), preferred_element_type=F32)


def _dot_tn(a, b):
    return lax.dot_general(a, b, (((0,), (0,)), ((), ())), preferred_element_type=F32)


def _accumulate(ref, val, first):
    @pl.when(first)
    def _():
        ref[...] = val

    @pl.when(jnp.logical_not(first))
    def _():
        ref[...] += val


def _call_after(dep, body, n_in, in_specs, args, **kw):
    if dep is None:
        return pl.pallas_call(body, in_specs=in_specs, **kw)(*args)

    def with_dep(*refs):
        body(*refs[:n_in], *refs[n_in + 1:])

    return pl.pallas_call(with_dep, in_specs=list(in_specs) + [ANY_SPEC], **kw)(*args, dep)


def _norm_matmul(h, g, w, nc, name, dep=None):
    t, d = h.shape
    n = w.shape[1]
    tm = _row_tile(t, TOKEN_TILE_TARGET, 16)

    def body(h_ref, g_ref, w_ref, z_ref, u_ref):
        u = _rms(h_ref[...], g_ref[...]).astype(BF16)
        u_ref[...] = u
        for n0 in range(0, n, nc):
            z_ref[:, n0:n0 + nc] = _dot(u, w_ref[:, n0:n0 + nc]).astype(BF16)

    return _call_after(
        dep, body, 3,
        [pl.BlockSpec((tm, d), lambda i: (i, 0)), pl.BlockSpec((1, d), lambda i: (0, 0)),
         pl.BlockSpec((d, n), lambda i: (0, 0))], (h, g, w), grid=(t // tm,),
        out_specs=[pl.BlockSpec((tm, n), lambda i: (i, 0)), pl.BlockSpec((tm, d), lambda i: (i, 0))],
        out_shape=[jax.ShapeDtypeStruct((t, n), BF16), jax.ShapeDtypeStruct((t, d), BF16)],
        compiler_params=_cparams("parallel"), name=name)


def _matmul_residual(a, w, h, name, dep=None):
    t, k = a.shape
    d = w.shape[1]
    tm = _row_tile(t, TOKEN_TILE_TARGET, 16)

    def body(a_ref, w_ref, h_ref, o_ref):
        o_ref[...] = h_ref[...] + _dot(a_ref[...], w_ref[...])

    return _call_after(
        dep, body, 3,
        [pl.BlockSpec((tm, k), lambda i: (i, 0)), pl.BlockSpec((k, d), lambda i: (0, 0)),
         pl.BlockSpec((tm, d), lambda i: (i, 0))], (a, w, h), grid=(t // tm,),
        out_specs=pl.BlockSpec((tm, d), lambda i: (i, 0)),
        out_shape=jax.ShapeDtypeStruct((t, d), F32),
        compiler_params=_cparams("parallel"), name=name)


def _ffn_fwd(h, g, w1g, w2g, name):
    t, d = h.shape
    ns, ffs = w1g.shape[0], w1g.shape[2]
    tm = _row_tile(t, TOKEN_TILE_TARGET, 16)

    def body(h_ref, g_ref, w1_ref, w2_ref, ho_ref, hp_ref, u_ref, acc_ref):
        s = pl.program_id(1)

        @pl.when(s == 0)
        def _():
            u_ref[...] = _rms(h_ref[...], g_ref[...]).astype(BF16)

        hp = _dot(u_ref[...], w1_ref[...])
        hp_ref[...] = hp.astype(BF16)
        a = jnp.maximum(hp, 0.0)
        _accumulate(acc_ref, _dot((a * a).astype(BF16), w2_ref[...]), s == 0)

        @pl.when(s == ns - 1)
        def _():
            ho_ref[...] = h_ref[...] + acc_ref[...]

    return pl.pallas_call(
        body, grid=(t // tm, ns),
        in_specs=[pl.BlockSpec((tm, d), lambda i, s: (i, 0)), pl.BlockSpec((1, d), lambda i, s: (0, 0)),
                  pl.BlockSpec((None, d, ffs), lambda i, s: (s, 0, 0)),
                  pl.BlockSpec((None, ffs, d), lambda i, s: (s, 0, 0))],
        out_specs=[pl.BlockSpec((tm, d), lambda i, s: (i, 0)), pl.BlockSpec((tm, ffs), lambda i, s: (i, s)),
                   pl.BlockSpec((tm, d), lambda i, s: (i, 0))],
        out_shape=[jax.ShapeDtypeStruct((t, d), F32), jax.ShapeDtypeStruct((t, ns * ffs), BF16),
                   jax.ShapeDtypeStruct((t, d), BF16)],
        scratch_shapes=[pltpu.VMEM((tm, d), F32)],
        compiler_params=_cparams("parallel", "arbitrary"), name=name)(h, g, w1g, w2g)


def _ffn_bwd_data(dh, h, g, hp, w1g, w2g, name, dep=None):
    t, d = h.shape
    ns, ffs = w1g.shape[0], w1g.shape[2]
    tm = _row_tile(t, ROW_TILE_TARGET, CHUNK)

    def body(dh_ref, h_ref, g_ref, hp_ref, w1_ref, w2_ref, dhi_ref, dhp_ref, dg_ref, acc_ref):
        i, s = pl.program_id(0), pl.program_id(1)
        da = _dot_nt(dh_ref[...].astype(BF16), w2_ref[...])
        dhp = (da * (2.0 * jnp.maximum(hp_ref[...].astype(F32), 0.0))).astype(BF16)
        dhp_ref[...] = dhp
        _accumulate(acc_ref, _dot_nt(dhp, w1_ref[...]), s == 0)

        @pl.when(s == ns - 1)
        def _():
            dhn, dgr = _rms_bwd(h_ref[...], g_ref[...], acc_ref[...])
            dhi_ref[...] = jnp.where(_valid_rows(i, tm), dh_ref[...] + dhn, 0.0)
            _accumulate(dg_ref, jnp.sum(dgr, axis=0, keepdims=True), i == 0)

    return _call_after(
        dep, body, 6,
        [pl.BlockSpec((tm, d), lambda i, s: (i, 0)), pl.BlockSpec((tm, d), lambda i, s: (i, 0)),
         pl.BlockSpec((1, d), lambda i, s: (0, 0)), pl.BlockSpec((tm, ffs), lambda i, s: (i, s)),
         pl.BlockSpec((None, d, ffs), lambda i, s: (s, 0, 0)),
         pl.BlockSpec((None, ffs, d), lambda i, s: (s, 0, 0))], (dh, h, g, hp, w1g, w2g), grid=(t // tm, ns),
        out_specs=[pl.BlockSpec((tm, d), lambda i, s: (i, 0)), pl.BlockSpec((tm, ffs), lambda i, s: (i, s)),
                   pl.BlockSpec((1, d), lambda i, s: (0, 0))],
        out_shape=[jax.ShapeDtypeStruct((t, d), F32), jax.ShapeDtypeStruct((t, ns * ffs), BF16),
                   jax.ShapeDtypeStruct((1, d), F32)],
        scratch_shapes=[pltpu.VMEM((tm, d), F32)],
        compiler_params=_cparams("arbitrary", "arbitrary"), name=name)


WGRAD_ROWS = 1024


def _wgrad(x, dy, nb, xc, yc, x_by_block, dy_by_block, relu2, name, dep=None):
    t = x.shape[0]
    tk = _row_tile(t - CHUNK, WGRAD_ROWS, CHUNK)

    def prep(xv):
        if relu2:
            xv = jnp.maximum(xv.astype(F32), 0.0)
            xv = xv * xv
        return xv.astype(BF16)

    def body(xh_ref, dyh_ref, x_ref, dy_ref, o_ref):
        k = pl.program_id(1)
        p = _dot_tn(prep(x_ref[...]), dy_ref[...].astype(BF16))

        @pl.when(k == 0)
        def _():
            o_ref[...] = p + _dot_tn(prep(xh_ref[...]), dyh_ref[...].astype(BF16))

        @pl.when(k > 0)
        def _():
            o_ref[...] += p

    def head(width, by_block):
        return pl.BlockSpec((CHUNK, width), (lambda b, k: (0, b)) if by_block else (lambda b, k: (0, 0)))

    def rest(width, by_block):
        def index(b, k):
            return pl.multiple_of(CHUNK + k * tk, CHUNK), (pl.multiple_of(b * width, 128) if by_block else 0)
        return pl.BlockSpec((pl.Element(tk), pl.Element(width)), index)

    return _call_after(
        dep, body, 4,
        [head(xc, x_by_block), head(yc, dy_by_block), rest(xc, x_by_block), rest(yc, dy_by_block)], (x, dy, x, dy),
        grid=(nb, (t - CHUNK) // tk),
        out_specs=pl.BlockSpec((None, xc, yc), lambda b, k: (b, 0, 0)),
        out_shape=jax.ShapeDtypeStruct((nb, xc, yc), F32),
        compiler_params=_cparams("parallel", "arbitrary"), name=name)


def _dgrad(dh, w, name, dep=None):
    t, d = dh.shape
    k = w.shape[0]
    tm = _row_tile(t, TOKEN_TILE_TARGET, 16)

    def body(dh_ref, w_ref, o_ref):
        o_ref[...] = _dot_nt(dh_ref[...].astype(BF16), w_ref[...]).astype(BF16)

    return _call_after(
        dep, body, 2,
        [pl.BlockSpec((tm, d), lambda i: (i, 0)), pl.BlockSpec((k, d), lambda i: (0, 0))], (dh, w), grid=(t // tm,),
        out_specs=pl.BlockSpec((tm, k), lambda i: (i, 0)),
        out_shape=jax.ShapeDtypeStruct((t, k), BF16),
        compiler_params=_cparams("parallel"), name=name)


def _dgrad_norm_bwd(dz, w, h, g, dh, nc, name):
    t, d = h.shape
    n = w.shape[1]
    tm = _row_tile(t, ROW_TILE_TARGET // 2, 16)

    def body(dz_ref, w_ref, h_ref, g_ref, dh_ref, dhi_ref, dg_ref):
        i = pl.program_id(0)
        du = jnp.zeros((tm, d), F32)
        for n0 in range(0, n, nc):
            du = du + _dot_nt(dz_ref[:, n0:n0 + nc], w_ref[:, n0:n0 + nc])
        dhn, dgr = _rms_bwd(h_ref[...], g_ref[...], du)
        dhi_ref[...] = jnp.where(_valid_rows(i, tm), dh_ref[...] + dhn, 0.0)
        _accumulate(dg_ref, jnp.sum(dgr, axis=0, keepdims=True), i == 0)

    return pl.pallas_call(
        body, grid=(t // tm,),
        in_specs=[pl.BlockSpec((tm, n), lambda i: (i, 0)), pl.BlockSpec((d, n), lambda i: (0, 0)),
                  pl.BlockSpec((tm, d), lambda i: (i, 0)), pl.BlockSpec((1, d), lambda i: (0, 0)),
                  pl.BlockSpec((tm, d), lambda i: (i, 0))],
        out_specs=[pl.BlockSpec((tm, d), lambda i: (i, 0)), pl.BlockSpec((1, d), lambda i: (0, 0))],
        out_shape=[jax.ShapeDtypeStruct((t, d), F32), jax.ShapeDtypeStruct((1, d), F32)],
        compiler_params=_cparams("arbitrary"), name=name)(dz, w, h, g, dh)


def _dgrad_norm_bwd_input(dz, w, h, g, dh, nc, name):
    t, d = h.shape
    n = w.shape[1]
    tl = _row_tile(t - CHUNK, 512, CHUNK)

    def grads(dz_ref, w_ref, h_ref, g_ref, dh_ref, rows):
        du = jnp.zeros((rows, d), F32)
        for n0 in range(0, n, nc):
            du = du + _dot_nt(dz_ref[:, n0:n0 + nc], w_ref[:, n0:n0 + nc])
        dhn, dgr = _rms_bwd(h_ref[...], g_ref[...], du)
        return dh_ref[...] + dhn, jnp.sum(dgr, axis=0, keepdims=True)

    def rest_body(dz_ref, w_ref, h_ref, g_ref, dh_ref, dg_head_ref, dx_ref, dg_ref):
        dx, dg = grads(dz_ref, w_ref, h_ref, g_ref, dh_ref, tl)
        dx_ref[...] = dx

        @pl.when(pl.program_id(0) == 0)
        def _():
            dg_ref[...] = dg_head_ref[...] + dg

        @pl.when(pl.program_id(0) > 0)
        def _():
            dg_ref[...] += dg

    def head_body(dz_ref, w_ref, h_ref, g_ref, dh_ref, dx_ref, dg_ref):
        dx, dg = grads(dz_ref, w_ref, h_ref, g_ref, dh_ref, CHUNK)
        dx_ref[...] = jnp.where(_valid_rows(0, CHUNK), dx, 0.0)
        dg_ref[...] = dg

    def shifted(width):
        return pl.BlockSpec((pl.Element(tl), pl.Element(width)), lambda i: (pl.multiple_of(CHUNK + i * tl, CHUNK), 0))

    whole = [pl.BlockSpec((d, n), lambda i: (0, 0)), pl.BlockSpec((1, d), lambda i: (0, 0))]
    head = lambda width: pl.BlockSpec((CHUNK, width), lambda i: (0, 0))
    dh_head, dg_head = pl.pallas_call(
        head_body, grid=(1,), in_specs=[head(n), whole[0], head(d), whole[1], head(d)],
        out_specs=[head(d), whole[1]],
        out_shape=[jax.ShapeDtypeStruct((CHUNK, d), F32), jax.ShapeDtypeStruct((1, d), F32)],
        compiler_params=_cparams("arbitrary"), name=name + "_head")(dz, w, h, g, dh)
    dx, dg = pl.pallas_call(
        rest_body, grid=((t - CHUNK) // tl,),
        in_specs=[shifted(n), whole[0], shifted(d), whole[1], shifted(d), whole[1]],
        out_specs=[pl.BlockSpec((tl, d), lambda i: (i, 0)), whole[1]],
        out_shape=[jax.ShapeDtypeStruct((t - CHUNK, d), F32), jax.ShapeDtypeStruct((1, d), F32)],
        compiler_params=_cparams("arbitrary"), name=name)(dz, w, h, g, dh, dg_head)
    return dx, dh_head, dg


def _loss_bwd(h, g, target):
    t, d = h.shape
    tl = _row_tile(t - CHUNK, 1024, CHUNK)

    def body(h_ref, g_ref, t_ref, dh_ref, dg_ref, loss_ref):
        i = pl.program_id(0)
        hv, gv = h_ref[...], g_ref[...]
        err = _rms(hv, gv) - t_ref[...]
        part = 0.5 * jnp.sum(jnp.mean(err * err, axis=-1, keepdims=True), axis=0, keepdims=True)
        dhn, dgr = _rms_bwd(hv, gv, err * (1.0 / d))
        dh_ref[...] = dhn
        _accumulate(dg_ref, jnp.sum(dgr, axis=0, keepdims=True), i == 0)
        _accumulate(loss_ref, jnp.broadcast_to(part, (8, 128)), i == 0)

    shifted = pl.BlockSpec((pl.Element(tl), pl.Element(d)), lambda i: (pl.multiple_of(CHUNK + i * tl, CHUNK), 0))
    dh, dg, loss = pl.pallas_call(
        body, grid=((t - CHUNK) // tl,),
        in_specs=[shifted, pl.BlockSpec((1, d), lambda i: (0, 0)), pl.BlockSpec((tl, d), lambda i: (i, 0))],
        out_specs=[shifted, pl.BlockSpec((1, d), lambda i: (0, 0)), pl.BlockSpec((8, 128), lambda i: (0, 0))],
        out_shape=[jax.ShapeDtypeStruct((t, d), F32), jax.ShapeDtypeStruct((1, d), F32),
                   jax.ShapeDtypeStruct((8, 128), F32)],
        compiler_params=_cparams("arbitrary"), name="loss_bwd")(h, g, target)

    def zero_head(dh_ref, o_ref):
        o_ref[...] = jnp.zeros_like(o_ref)

    dh = pl.pallas_call(
        zero_head, grid=(1,), in_specs=[ANY_SPEC], out_specs=pl.BlockSpec((CHUNK, d), lambda i: (0, 0)),
        out_shape=jax.ShapeDtypeStruct((t, d), F32), input_output_aliases={0: 0}, name="loss_bwd_head")(dh)
    return dh, dg, loss


CONV_BLOCK = 32


def _silu(x):
    return x * jax.nn.sigmoid(x)


def _row_shifts(win):
    n = win.shape[0]
    return [win] + [pltpu.roll(win, n - j, 0) for j in range(1, 8)]


def _cp_seq_fwd(z, conv_w, conv_b, ln_g, ln_b, pool_w, pool_scale):
    t = z.shape[0]
    tm = _row_tile(t, ROW_TILE_TARGET, CHUNK)

    def body(z_ref, cw_ref, cb_ref, lg_ref, lb_ref, pw_ref, ps_ref, c_ref, pm_ref, mix_ref, gbuf, pbuf):
        i = pl.program_id(0)

        @pl.when(i == 0)
        def _():
            gbuf[0:CONV_HALO, :] = jnp.zeros((CONV_HALO, CONV_DIM), F32)
            pbuf[0:POOL_HALO, :] = jnp.zeros((POOL_HALO, POOL_DIM), F32)

        @pl.when(i > 0)
        def _():
            gbuf[0:CONV_HALO, :] = gbuf[tm:tm + CONV_HALO, :]
            pbuf[0:POOL_HALO, :] = pbuf[tm:tm + POOL_HALO, :]

        av = z_ref[:, 0:CONV_DIM].astype(F32)
        ag = z_ref[:, CONV_DIM:2 * CONV_DIM].astype(F32)
        gbuf[CONV_HALO:CONV_HALO + tm, :] = av * jax.nn.sigmoid(ag)
        pbuf[POOL_HALO:POOL_HALO + tm, :] = z_ref[:, 2 * CONV_DIM:CP_IN].astype(F32)

        def conv_block(rb, carry):
            base = pl.multiple_of(rb * CONV_BLOCK, CONV_BLOCK)
            shifted = _row_shifts(gbuf[pl.ds(base, CONV_BLOCK + CONV_HALO), :])
            acc = jnp.zeros((CONV_BLOCK, CONV_DIM), F32)
            for k in range(CONV_WIDTH):
                whole, part = divmod(CONV_HALO - (CONV_WIDTH - 1) + k, 8)
                acc = acc + cw_ref[k:k + 1, :] * shifted[part][8 * whole:8 * whole + CONV_BLOCK, :]
            c_ref[pl.ds(base, CONV_BLOCK), :] = acc + cb_ref[...]
            return carry

        lax.fori_loop(0, tm // CONV_BLOCK, conv_block, 0)

        c = c_ref[...]
        mu = jnp.mean(c, axis=-1, keepdims=True)
        xc = c - mu
        ln = xc * lax.rsqrt(jnp.mean(xc * xc, axis=-1, keepdims=True) + EPS) * lg_ref[...] + lb_ref[...]
        row = i * tm + lax.broadcasted_iota(jnp.int32, (tm, 1), 0)
        mix_ref[:, 0:CONV_DIM] = jnp.where(row >= PAD_ROWS, _silu(ln), 0.0).astype(BF16)

        tpos = (row - PAD_ROWS + 1).astype(F32)
        for gi, wdw in enumerate(POOL_WINDOWS):
            lo = POOL_GROUP * gi
            cur = pbuf[POOL_HALO:POOL_HALO + tm, lo:lo + POOL_GROUP]
            sacc = cur
            for j in range(1, wdw):
                sacc = sacc + pbuf[POOL_HALO - j:POOL_HALO - j + tm, lo:lo + POOL_GROUP]
            pm = (sacc / jnp.clip(tpos, 1.0, float(wdw)) - cur).astype(BF16)
            pm_ref[:, lo:lo + POOL_GROUP] = pm
            pg = _dot(pm, pw_ref[gi].astype(BF16))
            mix_ref[:, CONV_DIM + lo:CONV_DIM + lo + POOL_GROUP] = (pg * ps_ref[:, lo:lo + POOL_GROUP]).astype(BF16)

    vec = pl.BlockSpec((1, CONV_DIM), lambda i: (0, 0))
    return pl.pallas_call(
        body, grid=(t // tm,),
        in_specs=[pl.BlockSpec((tm, CP_IN), lambda i: (i, 0)),
                  pl.BlockSpec((CONV_WIDTH, CONV_DIM), lambda i: (0, 0)), vec, vec, vec,
                  pl.BlockSpec((len(POOL_WINDOWS), POOL_GROUP, POOL_GROUP), lambda i: (0, 0, 0)), vec],
        out_specs=[pl.BlockSpec((tm, CONV_DIM), lambda i: (i, 0)), pl.BlockSpec((tm, POOL_DIM), lambda i: (i, 0)),
                   pl.BlockSpec((tm, CONV_DIM + POOL_DIM), lambda i: (i, 0))],
        out_shape=[jax.ShapeDtypeStruct((t, CONV_DIM), F32), jax.ShapeDtypeStruct((t, POOL_DIM), BF16),
                   jax.ShapeDtypeStruct((t, CONV_DIM + POOL_DIM), BF16)],
        scratch_shapes=[pltpu.VMEM((tm + CONV_HALO, CONV_DIM), F32), pltpu.VMEM((tm + POOL_HALO, POOL_DIM), F32)],
        compiler_params=_cparams("arbitrary"), name="cp_seq_fwd")(z, conv_w, conv_b, ln_g, ln_b, pool_w, pool_scale)


def _cp_seq_bwd(dmix, z, c, pm, conv_w, ln_g, ln_b, pool_w, pool_scale, dep=None):
    t = z.shape[0]
    tm = _row_tile(t, ROW_TILE_TARGET, CHUNK)
    nt = t // tm

    def body(dmix_ref, z_ref, c_ref, pm_ref, cw_ref, lg_ref, lb_ref, pw_ref, ps_ref,
             dz_ref, dcw_ref, dvec_ref, dpw_ref, dcbuf, qbuf, glu_buf, dwacc):
        i = pl.program_id(0)
        tile = nt - 1 - i

        @pl.when(i == 0)
        def _():
            dcbuf[tm:tm + CONV_HALO, :] = jnp.zeros((CONV_HALO, CONV_DIM), F32)
            qbuf[tm:tm + POOL_HALO, :] = jnp.zeros((POOL_HALO, POOL_DIM), F32)
            dcw_ref[...] = jnp.zeros_like(dcw_ref)
            dwacc[...] = jnp.zeros_like(dwacc)
            dvec_ref[...] = jnp.zeros_like(dvec_ref)
            dpw_ref[...] = jnp.zeros_like(dpw_ref)

        @pl.when(i > 0)
        def _():
            dcbuf[tm:tm + CONV_HALO, :] = dcbuf[0:CONV_HALO, :]
            qbuf[tm:tm + POOL_HALO, :] = qbuf[0:POOL_HALO, :]

        row = tile * tm + lax.broadcasted_iota(jnp.int32, (tm, 1), 0)
        cv = c_ref[...]
        mu = jnp.mean(cv, axis=-1, keepdims=True)
        xc = cv - mu
        rstd = lax.rsqrt(jnp.mean(xc * xc, axis=-1, keepdims=True) + EPS)
        xhat = xc * rstd
        ln = xhat * lg_ref[...] + lb_ref[...]
        sg = jax.nn.sigmoid(ln)
        da = jnp.where(row >= PAD_ROWS, dmix_ref[:, 0:CONV_DIM].astype(F32), 0.0)
        dln = da * (sg * (1.0 + ln * (1.0 - sg)))
        dxh = dln * lg_ref[...]
        dc = rstd * (dxh - jnp.mean(dxh, axis=-1, keepdims=True) - xhat * jnp.mean(dxh * xhat, axis=-1, keepdims=True))
        dcbuf[0:tm, :] = dc
        dvec_ref[0:1, :] += jnp.sum(dc, axis=0, keepdims=True)
        dvec_ref[1:2, :] += jnp.sum(dln * xhat, axis=0, keepdims=True)
        dvec_ref[2:3, :] += jnp.sum(dln, axis=0, keepdims=True)

        av = z_ref[:, 0:CONV_DIM].astype(F32)
        sig_g = jax.nn.sigmoid(z_ref[:, CONV_DIM:2 * CONV_DIM].astype(F32))
        glu_buf[...] = av * sig_g

        def conv_block(rb, carry):
            base = pl.multiple_of(rb * CONV_BLOCK, CONV_BLOCK)
            shifted = _row_shifts(dcbuf[pl.ds(base, CONV_BLOCK + CONV_HALO), :])
            glu = glu_buf[pl.ds(base, CONV_BLOCK), :]
            acc = jnp.zeros((CONV_BLOCK, CONV_DIM), F32)
            for k in range(CONV_WIDTH):
                whole, part = divmod(CONV_WIDTH - 1 - k, 8)
                slab = shifted[part][8 * whole:8 * whole + CONV_BLOCK, :]
                acc = acc + cw_ref[k:k + 1, :] * slab
                prod = slab * glu
                part = prod[0:8]
                for q in range(1, CONV_BLOCK // 8):
                    part = part + prod[8 * q:8 * q + 8]
                dwacc[k] += part
            glu_buf[pl.ds(base, CONV_BLOCK), :] = acc
            return carry

        lax.fori_loop(0, tm // CONV_BLOCK, conv_block, 0)

        @pl.when(i == nt - 1)
        def _():
            for k in range(CONV_WIDTH):
                dcw_ref[k:k + 1, :] = jnp.sum(dwacc[k], axis=0, keepdims=True)
        dglu = glu_buf[...]
        dz_ref[:, 0:CONV_DIM] = (dglu * sig_g).astype(BF16)
        dz_ref[:, CONV_DIM:2 * CONV_DIM] = (dglu * av * sig_g * (1.0 - sig_g)).astype(BF16)

        tpos = (row - PAD_ROWS + 1).astype(F32)
        for gi, wdw in enumerate(POOL_WINDOWS):
            lo = POOL_GROUP * gi
            dp = dmix_ref[:, CONV_DIM + lo:CONV_DIM + lo + POOL_GROUP].astype(F32)
            pmv = pm_ref[:, lo:lo + POOL_GROUP]
            pwb = pw_ref[gi].astype(BF16)
            dvec_ref[3:4, lo:lo + POOL_GROUP] += jnp.sum(dp * _dot(pmv, pwb), axis=0, keepdims=True)
            dq = (dp * ps_ref[:, lo:lo + POOL_GROUP]).astype(BF16)
            dpw_ref[gi] += _dot_tn(pmv, dq)
            dpm = _dot_nt(dq, pwb)
            qbuf[0:tm, lo:lo + POOL_GROUP] = dpm / jnp.clip(tpos, 1.0, float(wdw))
            sacc = -dpm
            for j in range(wdw):
                sacc = sacc + qbuf[j:j + tm, lo:lo + POOL_GROUP]
            dz_ref[:, 2 * CONV_DIM + lo:2 * CONV_DIM + lo + POOL_GROUP] = sacc.astype(BF16)

    vec = pl.BlockSpec((1, CONV_DIM), lambda i: (0, 0))
    rev = lambda i: (nt - 1 - i, 0)
    return _call_after(
        dep, body, 9,
        [pl.BlockSpec((tm, CONV_DIM + POOL_DIM), rev), pl.BlockSpec((tm, CP_IN), rev),
         pl.BlockSpec((tm, CONV_DIM), rev), pl.BlockSpec((tm, POOL_DIM), rev),
         pl.BlockSpec((CONV_WIDTH, CONV_DIM), lambda i: (0, 0)), vec, vec,
         pl.BlockSpec((len(POOL_WINDOWS), POOL_GROUP, POOL_GROUP), lambda i: (0, 0, 0)), vec],
        (dmix, z, c, pm, conv_w, ln_g, ln_b, pool_w, pool_scale), grid=(nt,),
        out_specs=[pl.BlockSpec((tm, CP_IN), rev), pl.BlockSpec((CONV_WIDTH + 1, CONV_DIM), lambda i: (0, 0)),
                   pl.BlockSpec((8, CONV_DIM), lambda i: (0, 0)),
                   pl.BlockSpec((len(POOL_WINDOWS), POOL_GROUP, POOL_GROUP), lambda i: (0, 0, 0))],
        out_shape=[jax.ShapeDtypeStruct((t, CP_IN), BF16), jax.ShapeDtypeStruct((CONV_WIDTH + 1, CONV_DIM), F32),
                   jax.ShapeDtypeStruct((8, CONV_DIM), F32),
                   jax.ShapeDtypeStruct((len(POOL_WINDOWS), POOL_GROUP, POOL_GROUP), F32)],
        scratch_shapes=[pltpu.VMEM((tm + CONV_HALO, CONV_DIM), F32), pltpu.VMEM((tm + POOL_HALO, POOL_DIM), F32),
                        pltpu.VMEM((tm, CONV_DIM), F32), pltpu.VMEM((CONV_WIDTH + 1, 8, CONV_DIM), F32)],
        compiler_params=_cparams("arbitrary"), name="cp_seq_bwd")


GLA_UNROLL = 2
Q0, K0, V0, G0, R0 =0, GLA_DK, 2 * GLA_DK, 2 * GLA_DK + GLA_DV, 2 * GLA_DK + 2 * GLA_DV


def _split3(x):
    hi = x.astype(BF16)
    r1 = x - hi.astype(F32)
    mid = r1.astype(BF16)
    lo = (r1 - mid.astype(F32)).astype(BF16)
    return hi, mid, lo


def _tri(strict):
    r = lax.broadcasted_iota(jnp.int32, (CHUNK, CHUNK), 0)
    c = lax.broadcasted_iota(jnp.int32, (CHUNK, CHUNK), 1)
    return ((r > c) if strict else (r >= c)).astype(BF16)


def _chunk_sums(x, cpt, strict, pieces):
    tri3 = jnp.broadcast_to(_tri(strict)[None], (cpt, CHUNK, CHUNK))
    acc = None
    for piece in _split3(x.reshape(cpt, CHUNK, x.shape[-1]))[:pieces]:
        part = jnp.einsum("bij,bjk->bik", tri3, piece, preferred_element_type=F32)
        acc = part if acc is None else acc + part
    return acc


def _chunk_decay(r, gw_ref, gb_ref, cpt):
    pre = _dot(r, gw_ref[...]) + gb_ref[...]
    lac = (jnp.minimum(pre, 0.0) - jnp.log(1.0 + jnp.exp(-jnp.abs(pre)))) * (1.0 / GATE_NORM)
    cum3 = _chunk_sums(lac, cpt, False, 3)
    return cum3, cum3[:, CHUNK - 1:CHUNK, :]


def _gla_seq_fwd(z, gate_w, gate_b, head_g, dep=None):
    t = z.shape[0]
    tm = _row_tile(t, ROW_TILE_TARGET, CHUNK)
    cpt = tm // CHUNK
    scale = GLA_HK ** -0.5

    def body(z_ref, gw_ref, gb_ref, hg_ref, o_ref, mix_ref, st_ref, state, kdec_s, e_s):
        @pl.when(pl.program_id(0) == 0)
        def _():
            state[...] = jnp.zeros_like(state)

        cum3, tot3 = _chunk_decay(z_ref[:, R0:R0 + GATE_PAD], gw_ref, gb_ref, cpt)
        dec = jnp.exp(jnp.broadcast_to(tot3, cum3.shape) - cum3).reshape(tm, GLA_DK)
        kdec_s[...] = (z_ref[:, K0:K0 + GLA_DK].astype(F32) * dec).astype(BF16)
        e_s[...] = jnp.exp(jnp.broadcast_to(tot3, (cpt, 8, GLA_DK))).reshape(cpt * 8, GLA_DK)

        def chunk(ci, carry):
            rows = pl.ds(pl.multiple_of(ci * CHUNK, CHUNK), CHUNK)
            e_all = e_s[pl.ds(pl.multiple_of(ci * 8, 8), 8), :][0:1, :]
            st_ref[ci] = state[...].astype(BF16)
            for hd in range(GLA_HEADS):
                ks = slice(hd * GLA_HK, (hd + 1) * GLA_HK)
                vs = slice(hd * GLA_HV, (hd + 1) * GLA_HV)
                v = z_ref[rows, V0 + hd * GLA_HV:V0 + (hd + 1) * GLA_HV]
                st = state[vs, :] * e_all[:, ks] + _dot_tn(v, kdec_s[rows, ks])
                state[vs, :] = st
                q = z_ref[rows, Q0 + hd * GLA_HK:Q0 + (hd + 1) * GLA_HK]
                o_ref[rows, vs] = (_dot_nt(q, st.astype(BF16)) * scale).astype(BF16)
            return carry

        lax.fori_loop(0, cpt, chunk, 0, unroll=GLA_UNROLL)

        for hd in range(GLA_HEADS):
            vs = slice(hd * GLA_HV, (hd + 1) * GLA_HV)
            on = _rms(o_ref[:, vs].astype(F32), hg_ref[...])
            gv = z_ref[:, G0 + hd * GLA_HV:G0 + (hd + 1) * GLA_HV].astype(F32)
            mix_ref[:, vs] = (on * _silu(gv)).astype(BF16)

    return _call_after(
        dep, body, 4,
        [pl.BlockSpec((tm, GLA_IN_PAD), lambda i: (i, 0)),
         pl.BlockSpec((GATE_PAD, GLA_DK), lambda i: (0, 0)), pl.BlockSpec((1, GLA_DK), lambda i: (0, 0)),
         pl.BlockSpec((1, GLA_HV), lambda i: (0, 0))], (z, gate_w, gate_b, head_g), grid=(t // tm,),
        out_specs=[pl.BlockSpec((tm, GLA_DV), lambda i: (i, 0)), pl.BlockSpec((tm, GLA_DV), lambda i: (i, 0)),
                   pl.BlockSpec((cpt, GLA_DV, GLA_HK), lambda i: (i, 0, 0))],
        out_shape=[jax.ShapeDtypeStruct((t, GLA_DV), BF16), jax.ShapeDtypeStruct((t, GLA_DV), BF16),
                   jax.ShapeDtypeStruct((t // CHUNK, GLA_DV, GLA_HK), BF16)],
        scratch_shapes=[pltpu.VMEM((GLA_DV, GLA_HK), F32), pltpu.VMEM((tm, GLA_DK), BF16),
                        pltpu.VMEM((cpt * 8, GLA_DK), F32)],
        compiler_params=_cparams("arbitrary"), name="gla_seq_fwd")


def _gla_seq_bwd(dmix, o, z, states, gate_w, gate_b, head_g, dep=None):
    t = z.shape[0]
    tm = _row_tile(t, ROW_TILE_TARGET, CHUNK)
    cpt = tm // CHUNK
    nt = t // tm
    scale = GLA_HK ** -0.5

    def body(dmix_ref, o_ref, z_ref, st_ref, gw_ref, gb_ref, hg_ref, dz_ref, dgw_ref, dgb_ref, dhg_ref,
             dstate, dec_s, kdec_s, dkdec_s, do_s, e_s, dtot_s):
        @pl.when(pl.program_id(0) == 0)
        def _():
            dstate[...] = jnp.zeros_like(dstate)
            dgw_ref[...] = jnp.zeros_like(dgw_ref)
            dgb_ref[...] = jnp.zeros_like(dgb_ref)
            dhg_ref[...] = jnp.zeros_like(dhg_ref)

        cum3, tot3 = _chunk_decay(z_ref[:, R0:R0 + GATE_PAD], gw_ref, gb_ref, cpt)
        dec = jnp.exp(jnp.broadcast_to(tot3, cum3.shape) - cum3).reshape(tm, GLA_DK)
        dec_s[...] = dec
        kdec = z_ref[:, K0:K0 + GLA_DK].astype(F32) * dec
        kdec_s[...] = kdec
        e3 = jnp.exp(tot3)
        e_s[...] = jnp.broadcast_to(e3, (cpt, 8, GLA_DK)).reshape(cpt * 8, GLA_DK)
        dhg = jnp.zeros((1, GLA_HV), F32)
        for hd in range(GLA_HEADS):
            ks = slice(hd * GLA_HK, (hd + 1) * GLA_HK)
            vs = slice(hd * GLA_HV, (hd + 1) * GLA_HV)
            gcols = slice(G0 + hd * GLA_HV, G0 + (hd + 1) * GLA_HV)
            ov = o_ref[:, vs].astype(F32)
            gv = z_ref[:, gcols].astype(F32)
            dm = dmix_ref[:, vs].astype(F32)
            sg = jax.nn.sigmoid(gv)
            rr = lax.rsqrt(jnp.mean(ov * ov, axis=-1, keepdims=True) + EPS)
            xhat = ov * rr
            don = dm * (gv * sg)
            dz_ref[:, gcols] = (dm * (xhat * hg_ref[...]) * (sg * (1.0 + gv * (1.0 - sg)))).astype(BF16)
            dhg = dhg + jnp.sum(don * xhat, axis=0, keepdims=True)
            dxh = don * hg_ref[...]
            do = (rr * (dxh - xhat * jnp.mean(dxh * xhat, axis=-1, keepdims=True)) * scale).astype(BF16)
            do_s[:, vs] = do
            v3 = z_ref[:, V0 + hd * GLA_HV:V0 + (hd + 1) * GLA_HV].reshape(cpt, CHUNK, GLA_HV)
            kdb3 = kdec[:, ks].astype(BF16).reshape(cpt, CHUNK, GLA_HK)
            st3 = st_ref[:, vs, :].astype(F32) * e3[:, :, ks] + jnp.einsum("bcv,bck->bvk", v3, kdb3,
                                                                            preferred_element_type=F32)
            dq3 = jnp.einsum("bcv,bvk->bck", do.reshape(cpt, CHUNK, GLA_HV), st3.astype(BF16), preferred_element_type=F32)
            dz_ref[:, Q0 + hd * GLA_HK:Q0 + (hd + 1) * GLA_HK] = dq3.reshape(tm, GLA_HK).astype(BF16)
        dhg_ref[...] += dhg

        def chunk(cj, carry):
            ci = cpt - 1 - cj
            rows = pl.ds(pl.multiple_of(ci * CHUNK, CHUNK), CHUNK)
            erows = pl.ds(pl.multiple_of(ci * 8, 8), 8)
            e_all = e_s[erows, :][0:1, :]
            for hd in range(GLA_HEADS):
                ks = slice(hd * GLA_HK, (hd + 1) * GLA_HK)
                vs = slice(hd * GLA_HV, (hd + 1) * GLA_HV)
                e = e_all[:, ks]
                kdb = kdec_s[rows, ks].astype(BF16)
                v = z_ref[rows, V0 + hd * GLA_HV:V0 + (hd + 1) * GLA_HV]
                q = z_ref[rows, Q0 + hd * GLA_HK:Q0 + (hd + 1) * GLA_HK]
                do = do_s[rows, vs]
                st_prev = st_ref[ci, vs, :].astype(F32)
                dst = dstate[vs, :] + _dot_tn(do, q)
                dstb = dst.astype(BF16)
                dkdec_s[rows, ks] = _dot(v, dstb)
                dz_ref[rows, V0 + hd * GLA_HV:V0 + (hd + 1) * GLA_HV] = _dot_nt(kdb, dstb).astype(BF16)
                dtot = jnp.sum(dst * st_prev, axis=0, keepdims=True) * e
                dtot_s[erows, ks] = jnp.broadcast_to(dtot, (8, GLA_HK))
                dstate[vs, :] = dst * e
            return carry

        lax.fori_loop(0, cpt, chunk, 0, unroll=GLA_UNROLL)

        dkdec = dkdec_s[...]
        dz_ref[:, K0:K0 + GLA_DK] = (dkdec * dec_s[...]).astype(BF16)
        before = _chunk_sums(dkdec * kdec_s[...], cpt, True, 2)
        dtot3 = dtot_s[...].reshape(cpt, 8, GLA_DK)[:, 0:1, :]
        dlac = (jnp.broadcast_to(dtot3, before.shape) + before).reshape(tm, GLA_DK)
        pre = _dot(z_ref[:, R0:R0 + GATE_PAD], gw_ref[...]) + gb_ref[...]
        dpre = dlac * (1.0 / GATE_NORM) * (1.0 - jax.nn.sigmoid(pre))
        dpb = dpre.astype(BF16)
        dz_ref[:, R0:R0 + GATE_PAD] = _dot_nt(dpb, gw_ref[...]).astype(BF16)
        dgw_ref[...] += _dot_tn(z_ref[:, R0:R0 + GATE_PAD], dpb)
        dgb_ref[...] += jnp.sum(dpre, axis=0, keepdims=True)

    rev = lambda i: (nt - 1 - i, 0)
    return _call_after(
        dep, body, 7,
        [pl.BlockSpec((tm, GLA_DV), rev), pl.BlockSpec((tm, GLA_DV), rev), pl.BlockSpec((tm, GLA_IN_PAD), rev),
         pl.BlockSpec((cpt, GLA_DV, GLA_HK), lambda i: (nt - 1 - i, 0, 0)),
         pl.BlockSpec((GATE_PAD, GLA_DK), lambda i: (0, 0)), pl.BlockSpec((1, GLA_DK), lambda i: (0, 0)),
         pl.BlockSpec((1, GLA_HV), lambda i: (0, 0))],
        (dmix, o, z, states, gate_w, gate_b, head_g), grid=(nt,),
        out_specs=[pl.BlockSpec((tm, GLA_IN_PAD), rev), pl.BlockSpec((GATE_PAD, GLA_DK), lambda i: (0, 0)),
                   pl.BlockSpec((1, GLA_DK), lambda i: (0, 0)), pl.BlockSpec((1, GLA_HV), lambda i: (0, 0))],
        out_shape=[jax.ShapeDtypeStruct((t, GLA_IN_PAD), BF16), jax.ShapeDtypeStruct((GATE_PAD, GLA_DK), F32),
                   jax.ShapeDtypeStruct((1, GLA_DK), F32), jax.ShapeDtypeStruct((1, GLA_HV), F32)],
        scratch_shapes=[pltpu.VMEM((GLA_DV, GLA_HK), F32), pltpu.VMEM((tm, GLA_DK), F32), pltpu.VMEM((tm, GLA_DK), F32),
                        pltpu.VMEM((tm, GLA_DK), F32), pltpu.VMEM((tm, GLA_DV), BF16),
                        pltpu.VMEM((cpt * 8, GLA_DK), F32), pltpu.VMEM((cpt * 8, GLA_DK), F32)],
        compiler_params=_cparams("arbitrary"), name="gla_seq_bwd")


def _sum_halves(g, recv, c_idx, name):
    n, r, cdim = g.shape
    h = r // 2
    tr = _row_tile(h, 256, 8)
    nh = h // tr

    def body(c_ref, g_ref, r_ref, o_ref):
        o_ref[...] = (g_ref[...] + r_ref[...]).astype(BF16)

    return pl.pallas_call(
        body,
        grid_spec=pltpu.PrefetchScalarGridSpec(
            num_scalar_prefetch=1, grid=(n, nh),
            in_specs=[pl.BlockSpec((None, tr, cdim), lambda s, i, c: (s, c[0] * nh + i, 0)),
                      pl.BlockSpec((None, tr, cdim), lambda s, i, c: (s, i, 0))],
            out_specs=pl.BlockSpec((None, tr, cdim), lambda s, i, c: (s, i, 0))),
        out_shape=jax.ShapeDtypeStruct((n, h, cdim), BF16),
        compiler_params=_cparams("parallel", "parallel"), name=name)(c_idx, g, recv)


def _sum_slots(x, name):
    n, r, cdim = x.shape
    tr = _row_tile(r, 256, 8)

    def body(x_ref, o_ref):
        acc = x_ref[0].astype(F32)
        for j in range(1, n):
            acc = acc + x_ref[j].astype(F32)
        o_ref[...] = acc

    return pl.pallas_call(
        body, grid=(r // tr,),
        in_specs=[pl.BlockSpec((n, tr, cdim), lambda i: (0, i, 0))],
        out_specs=pl.BlockSpec((tr, cdim), lambda i: (i, 0)),
        out_shape=jax.ShapeDtypeStruct((r, cdim), F32),
        compiler_params=_cparams("parallel"), name=name)(x)


def _sum_own_and_slots(own, slots, chip_idx, name):
    n, r, cdim = own.shape
    tr = _row_tile(r, 256, 8)

    def body(s_ref, own_ref, a_ref, b_ref, c_ref, o_ref):
        o_ref[...] = (own_ref[...].astype(F32) + a_ref[...].astype(F32) + b_ref[...].astype(F32)
                      + c_ref[...].astype(F32))

    def slot(dd):
        return pl.BlockSpec((None, tr, cdim), lambda i, s: ((s[0] + dd) % n, i, 0))

    return pl.pallas_call(
        body,
        grid_spec=pltpu.PrefetchScalarGridSpec(
            num_scalar_prefetch=1, grid=(r // tr,), in_specs=[slot(0), slot(1), slot(2), slot(3)],
            out_specs=pl.BlockSpec((tr, cdim), lambda i, s: (i, 0))),
        out_shape=jax.ShapeDtypeStruct((r, cdim), F32),
        compiler_params=_cparams("parallel"), name=name)(chip_idx, own, slots, slots, slots)


def _add2(a, b, name):
    r, cdim = a.shape
    tr = _row_tile(r, 256, 8)

    def body(a_ref, b_ref, o_ref):
        o_ref[...] = a_ref[...] + b_ref[...]

    spec = pl.BlockSpec((tr, cdim), lambda i: (i, 0))
    return pl.pallas_call(body, grid=(r // tr,), in_specs=[spec, spec], out_specs=spec,
                          out_shape=jax.ShapeDtypeStruct((r, cdim), F32),
                          compiler_params=_cparams("parallel"), name=name)(a, b)


def _adamw_half(w, gs, m, v, half_idx, prev, name, dep=None):
    nl, _, h, cdim = w.shape
    tr = _row_tile(h, 256, 8)
    nprev = 0 if prev is None else 4
    extra = [] if dep is None else [dep]

    def body(s_ref, w_ref, m_ref, v_ref, *rest):
        g_refs = rest[:nl]
        go_ref, d_ref, mo_ref, vo_ref = rest[nl + nprev + len(extra):]
        layer = pl.program_id(0)
        gv = g_refs[0][...]
        for j in range(1, nl):
            gv = jnp.where(layer == j, g_refs[j][...], gv)
        go_ref[...] = gv
        mn = ADAM_B1 * m_ref[...] + (1.0 - ADAM_B1) * gv
        vn = ADAM_B2 * v_ref[...] + (1.0 - ADAM_B2) * (gv * gv)
        m_hat = mn / (1.0 - ADAM_B1 ** ADAM_STEP)
        v_hat = vn / (1.0 - ADAM_B2 ** ADAM_STEP)
        d_ref[...] = -ADAM_LR * (m_hat / (jnp.sqrt(v_hat) + ADAM_EPS) + ADAM_WD * w_ref[...])
        mo_ref[...] = mn
        vo_ref[...] = vn

    half = pl.BlockSpec((None, None, tr, cdim), lambda l, i, s: (l, s[0], i, 0))

    def of_layer(j):
        return pl.BlockSpec((tr, cdim), lambda l, i, s: (jnp.where(l == j, i, 0), 0))

    shp = jax.ShapeDtypeStruct(w.shape, F32)
    return pl.pallas_call(
        body,
        grid_spec=pltpu.PrefetchScalarGridSpec(
            num_scalar_prefetch=1, grid=(nl, h // tr),
            in_specs=[half] * 3 + [of_layer(j) for j in range(nl)] + [ANY_SPEC] * (nprev + len(extra)),
            out_specs=[half] * 4),
        out_shape=[shp] * 4, input_output_aliases={4 + nl + k: k for k in range(nprev)},
        compiler_params=_cparams("arbitrary", "arbitrary"), name=name,
    )(half_idx, w, m, v, *gs, *([] if prev is None else prev), *extra)


def _adamw_many(ws, gs, ms, vs):
    n = len(ws)

    def body(*refs):
        for i in range(n):
            w_ref, g_ref, m_ref, v_ref = refs[i], refs[n + i], refs[2 * n + i], refs[3 * n + i]
            d_ref, mo_ref, vo_ref = refs[4 * n + i], refs[5 * n + i], refs[6 * n + i]
            gv = g_ref[...]
            mn = ADAM_B1 * m_ref[...] + (1.0 - ADAM_B1) * gv
            vn = ADAM_B2 * v_ref[...] + (1.0 - ADAM_B2) * (gv * gv)
            m_hat = mn / (1.0 - ADAM_B1 ** ADAM_STEP)
            v_hat = vn / (1.0 - ADAM_B2 ** ADAM_STEP)
            d_ref[...] = -ADAM_LR * (m_hat / (jnp.sqrt(v_hat) + ADAM_EPS) + ADAM_WD * w_ref[...])
            mo_ref[...] = mn
            vo_ref[...] = vn

    shapes = [jax.ShapeDtypeStruct(w.shape, F32) for w in ws]
    outs = pl.pallas_call(body, out_shape=shapes * 3, name="adamw_small")(*ws, *gs, *ms, *vs)
    return outs[:n], outs[n:2 * n], outs[2 * n:]


def _split_rows(a):
    return a.reshape(a.shape[0], 2, a.shape[1] // 2, a.shape[2])


def _place():
    x, y, c = lax.axis_index("x"), lax.axis_index("y"), lax.axis_index("c")
    chips = [(1 - x, y), (x, 1 - y), (1 - x, 1 - y)]
    return x, y, c, chips


def _remote(src, dst, send_sem, recv_sem, to):
    return pltpu.make_async_remote_copy(src_ref=src, dst_ref=dst, send_sem=send_sem, recv_sem=recv_sem,
                                        device_id=to, device_id_type=MESH)


def _plan_gather(n_halved):
    def plan(src_refs, land_refs):
        x, y, c, chips = _place()
        me = 2 * x + y
        copies = []
        for k, (src, land) in enumerate(zip(src_refs, land_refs)):
            for (px, py) in chips:
                frm = 2 * px + py
                if k < n_halved:
                    copies.append((src.at[c], land.at[me, c], (px, py, c), land.at[frm, c]))
                else:
                    copies.append((src, land.at[me], (px, py, c), land.at[frm]))
        return copies
    return plan


def _plan_share(src_refs, land_refs):
    x, y, c, chips = _place()
    me = 2 * x + y
    sib = (x, y, 1 - c)
    copies = []
    for src, land in zip(src_refs, land_refs):
        copies.append((src, land.at[me], sib, land.at[me]))
        for (px, py) in chips:
            frm = 2 * px + py
            copies.append((land.at[frm, c], land.at[frm, c], sib, land.at[frm, 1 - c]))
    return copies


def _plan_scatter(n_parts):
    def plan(src_refs, land_refs):
        x, y, c, chips = _place()
        me = 2 * x + y
        copies = []
        for k, (src, land) in enumerate(zip(src_refs, land_refs)):
            for (px, py) in chips:
                to = 2 * px + py
                copies.append((src.at[to] if k < n_parts else src, land.at[me], (px, py, c), land.at[to]))
        return copies
    return plan


def _plan_exchange(n_split):
    def plan(src_refs, land_refs):
        x, y, c, _ = _place()
        sib = (x, y, 1 - c)
        return [(src.at[:, 1 - c] if k < n_split else src, land, sib, land)
                for k, (src, land) in enumerate(zip(src_refs, land_refs))]
    return plan


def _hbm(a):
    return pltpu.HBM(a.shape, a.dtype)


def _start_copies(name, srcs, lands, plan, ncopy, dep=None):
    ns, nl = len(srcs), len(lands)
    nin = ns + nl + (0 if dep is None else 1)

    def body(*refs):
        send_sems, recv_sems, token = refs[nin], refs[nin + 1], refs[-1]
        for k, (src, dst, dev, _) in enumerate(plan(refs[:ns], refs[ns:ns + nl])):
            _remote(src, dst, send_sems.at[k], recv_sems.at[k], dev).start()
        token[...] = jnp.zeros_like(token)

    args = [pltpu.with_memory_space_constraint(a, pltpu.HBM) for a in list(srcs) + list(lands)]
    outs = pl.pallas_call(
        body, name=name,
        out_shape=(pltpu.SemaphoreType.DMA((ncopy,)), pltpu.SemaphoreType.DMA((ncopy,)),
                   *[_hbm(a) for a in list(srcs) + list(lands)], jax.ShapeDtypeStruct((8, 128), F32)),
        in_specs=[HBM_SPEC] * (ns + nl) + ([] if dep is None else [ANY_SPEC]),
        out_specs=(SEM_SPEC, SEM_SPEC, *([HBM_SPEC] * (ns + nl)), pl.BlockSpec(memory_space=pltpu.VMEM)),
        input_output_aliases={i: 2 + i for i in range(ns + nl)},
        compiler_params=pltpu.CompilerParams(has_side_effects=SIDE_EFFECT),
    )(*args, *([] if dep is None else [dep]))
    return outs[0], outs[1], list(outs[2:2 + ns]), list(outs[2 + ns:2 + ns + nl]), outs[-1]


def _wait_copies(name, started, plan, after, sem_offset=0):
    send_sems, recv_sems, srcs, lands, _ = started
    ns, nl = len(srcs), len(lands)
    after = list(after) if isinstance(after, (list, tuple)) else [after]

    def body(*refs):
        send_ref, recv_ref = refs[ns + nl], refs[ns + nl + 1]
        for k, (src, _, dev, mine) in enumerate(plan(refs[:ns], refs[ns:ns + nl])):
            copy = _remote(src, mine, send_ref.at[sem_offset + k], recv_ref.at[sem_offset + k], dev)
            copy.wait_send()
            copy.wait_recv()

    outs = pl.pallas_call(
        body, name=name, out_shape=tuple(_hbm(a) for a in srcs + lands),
        in_specs=[HBM_SPEC] * (ns + nl) + [SEM_SPEC, SEM_SPEC] + [ANY_SPEC] * len(after),
        out_specs=tuple([HBM_SPEC] * (ns + nl)),
        input_output_aliases={i: i for i in range(ns + nl)},
        compiler_params=pltpu.CompilerParams(has_side_effects=SIDE_EFFECT),
    )(*srcs, *lands, send_sems, recv_sems, *after)
    return list(outs[:ns]), list(outs[ns:])


def _share_with_sibling(name, srcs, lands):
    n = len(srcs)

    def body(*refs):
        src_refs, land_refs, out_refs = refs[:n], refs[n:2 * n], refs[2 * n:3 * n]
        send_sem, recv_sem = refs[3 * n:]
        x, y, c, chips = _place()
        me = 2 * x + y
        sib = (x, y, 1 - c)
        sends, recvs = [], []
        for k in range(n):
            sems = (send_sem.at[4 * k], recv_sem.at[4 * k])
            sends.append(_remote(src_refs[k], out_refs[k].at[me], *sems, sib))
            recvs.append(_remote(src_refs[k], out_refs[k].at[me], *sems, sib))
            for j, (px, py) in enumerate(chips):
                frm = 2 * px + py
                sems = (send_sem.at[4 * k + 1 + j], recv_sem.at[4 * k + 1 + j])
                sends.append(_remote(land_refs[k].at[frm, c], out_refs[k].at[frm, c], *sems, sib))
                recvs.append(_remote(land_refs[k].at[frm, c], out_refs[k].at[frm, 1 - c], *sems, sib))
        for cp in sends:
            cp.start()
        for cp in recvs:
            cp.wait_recv()
        for cp in sends:
            cp.wait_send()

    return pl.pallas_call(
        body, name=name, in_specs=[HBM_SPEC] * (2 * n), out_specs=[HBM_SPEC] * n,
        out_shape=[jax.ShapeDtypeStruct(a.shape, a.dtype) for a in lands],
        input_output_aliases={n + k: k for k in range(n)},
        scratch_shapes=[pltpu.SemaphoreType.DMA((4 * n,)), pltpu.SemaphoreType.DMA((4 * n,))],
    )(*srcs, *lands)


def _pack(arrs):
    flat = jnp.concatenate([a.reshape(-1).astype(F32) for a in arrs])
    n = flat.shape[0]
    rows = -(-n // PACK_WIDTH)
    rows = -(-rows // 8) * 8
    return jnp.pad(flat, (0, rows * PACK_WIDTH - n)).reshape(rows, PACK_WIDTH)


def _unpack(buf, shapes):
    flat = buf.reshape(-1)
    out, off = [], 0
    for shp in shapes:
        n = 1
        for s in shp:
            n *= s
        out.append(flat[off:off + n].reshape(shp))
        off += n
    return out


def _unshard_cols(stacked):
    moved = jnp.moveaxis(stacked, 0, -2)
    return moved.reshape(moved.shape[:-2] + (moved.shape[-2] * moved.shape[-1],))


def _take_cols(blocks, start, width):
    bw = blocks.shape[2]
    pieces, lo = [], start
    while lo < start + width:
        b = lo // bw
        hi = min(start + width, (b + 1) * bw)
        pieces.append(blocks[b][:, lo - b * bw:hi - b * bw])
        lo = hi
    return jnp.concatenate(pieces, axis=1)


def _col_shard(full, s, width):
    return lax.dynamic_slice_in_dim(full, s * width, width, axis=full.ndim - 1)


def kernel(x, meta_tokens, mix_norm_g, ffn_norm_g, ffn_w1, ffn_w2, cp_w_in, cp_conv_w, cp_conv_b, cp_ln_g, cp_ln_b, cp_pool_w, cp_pool_scale, cp_w_out, gla_w_in, gla_gate_w2, gla_gate_b, gla_head_g, gla_w_out, final_norm_g, loss_target, m_meta_tokens, m_mix_norm_g, m_ffn_norm_g, m_ffn_w1, m_ffn_w2, m_cp_w_in, m_cp_conv_w, m_cp_conv_b, m_cp_ln_g, m_cp_ln_b, m_cp_pool_w, m_cp_pool_scale, m_cp_w_out, m_gla_w_in, m_gla_gate_w2, m_gla_gate_b, m_gla_head_g, m_gla_w_out, m_final_norm_g, v_meta_tokens, v_mix_norm_g, v_ffn_norm_g, v_ffn_w1, v_ffn_w2, v_cp_w_in, v_cp_conv_w, v_cp_conv_b, v_cp_ln_g, v_cp_ln_b, v_cp_pool_w, v_cp_pool_scale, v_cp_w_out, v_gla_w_in, v_gla_gate_w2, v_gla_gate_b, v_gla_head_g, v_gla_w_out, v_final_norm_g):
    d = D_MODEL
    chip = 2 * lax.axis_index("x") + lax.axis_index("y")
    core = lax.axis_index("c")
    seq = x.shape[1]
    t = seq + CHUNK

    sharded_small = [meta_tokens, cp_conv_w, gla_gate_w2, gla_gate_b, gla_head_g]

    def halves(w):
        return w.astype(BF16).reshape(2, w.shape[0] // 2, w.shape[1])

    def unhalve(g):
        return g.reshape(N_CHIPS, 2 * g.shape[2], g.shape[3])

    def gather_group(srcs, whole=()):
        lands = [lax.empty((N_CHIPS,) + s.shape, s.dtype) for s in srcs]
        for a in whole:
            lands.append(lax.dynamic_update_slice(jnp.zeros((N_CHIPS,) + a.shape, a.dtype), a[None], (chip,) + (0,) * a.ndim))
        return list(srcs) + list(whole), lands, _plan_gather(len(srcs)), len(srcs)

    groups = [gather_group([halves(cp_w_in[0]), halves(cp_w_out[0])], [_pack(sharded_small)]),
              gather_group([halves(ffn_w1[0]), halves(ffn_w2[0])]), gather_group([halves(gla_w_in[0]), halves(gla_w_out[0])]),
              gather_group([halves(ffn_w1[1]), halves(ffn_w2[1])])]
    bounds, all_srcs, all_lands = [], [], []
    for srcs, lands, _, _ in groups:
        bounds.append((len(all_srcs), len(all_srcs) + len(srcs)))
        all_srcs += srcs
        all_lands += lands

    def plan_all(src_refs, land_refs):
        return [cp for (lo, hi), group in zip(bounds, groups) for cp in group[2](src_refs[lo:hi], land_refs[lo:hi])]

    gathers = _start_copies("gather_start", all_srcs, all_lands, plan_all, 3 * len(all_srcs))

    def arrived(name, gi, after):
        (lo, hi), plan, n = bounds[gi], groups[gi][2], groups[gi][3]
        mine = (gathers[0], gathers[1], gathers[2][lo:hi], gathers[3][lo:hi], gathers[4])
        srcs, lands = _wait_copies(name + "_wait", mine, plan, after, sem_offset=3 * lo)
        return srcs[:n], lands[:n], lands[n:]

    cp_gather, ffn0_gather, gla_gather, ffn1_gather = 0, 1, 2, 3
    h0_rows = jnp.concatenate([jnp.zeros((CHUNK, d), F32) + gathers[4][0, 0], x[0]], axis=0)
    cp_srcs, cp_lands, (small_g,) = arrived("gather_cp", cp_gather, h0_rows)
    cpin_g, cpout_g = [unhalve(g) for g in _share_with_sibling("gather_cp_share", cp_srcs, cp_lands)]
    per_chip = [_unpack(small_g[j], [a.shape for a in sharded_small]) for j in range(N_CHIPS)]
    meta_f, conv_w_f, gate_w_f, gate_b_f, head_g_f = [
        jnp.concatenate([per_chip[j][i] for j in range(N_CHIPS)], axis=-1) for i in range(len(sharded_small))]
    conv_w_f, gate_w_f = conv_w_f[0], gate_w_f[0]
    w_cp_in = _unshard_cols(cpin_g)
    w_cp_out = cpout_g.reshape(CONV_DIM + POOL_DIM, d)
    gate_w_pad = jnp.pad(gate_w_f, ((0, GATE_PAD - GATE_RANK), (0, 0))).astype(BF16)
    row = lambda a: a.reshape(1, -1)
    c_idx = core.reshape(1).astype(jnp.int32)
    chip_idx = chip.reshape(1).astype(jnp.int32)

    h0 = lax.dynamic_update_slice(h0_rows, meta_f, (PAD_ROWS, 0))
    z0, u0 = _norm_matmul(h0, row(mix_norm_g[0]), w_cp_in, 512, "cp_in_proj")
    c0, pm0, mix0 = _cp_seq_fwd(z0, conv_w_f, cp_conv_b, cp_ln_g, cp_ln_b, cp_pool_w[0], cp_pool_scale)
    ffn0_srcs, ffn0_lands, _ = arrived("gather_ffn0", ffn0_gather, mix0)
    ffn0_share = _start_copies("gather_ffn0_share_start", ffn0_srcs, ffn0_lands, _plan_share, 4 * len(ffn0_srcs))
    h1 = _matmul_residual(mix0, w_cp_out, h0, "cp_out_proj", dep=ffn0_share[-1])
    w1g0, w2g0 = [unhalve(g) for g in _wait_copies("gather_ffn0_share_wait", ffn0_share, _plan_share, h1)[1]]
    h2, hp0, uf0 = _ffn_fwd(h1, row(ffn_norm_g[0]), w1g0, w2g0, "ffn0_fwd")
    gla_srcs, gla_lands, _ = arrived("gather_gla", gla_gather, h2)
    glain_g, glaout_g = [unhalve(g) for g in _share_with_sibling("gather_gla_share", gla_srcs, gla_lands)]
    w_gla_in = jnp.concatenate([glain_g[j] for j in range(N_CHIPS)] + [jnp.zeros((d, GLA_IN_PAD - GLA_IN), BF16)], axis=1)
    w_gla_out = glaout_g.reshape(GLA_DV, d)
    z1, u2 = _norm_matmul(h2, row(mix_norm_g[1]), w_gla_in, 640, "gla_in_proj")
    ffn1_srcs, ffn1_lands, _ = arrived("gather_ffn1", ffn1_gather, z1)
    ffn1_share = _start_copies("gather_ffn1_share_start", ffn1_srcs, ffn1_lands, _plan_share, 4 * len(ffn1_srcs))
    o1, mix1, states = _gla_seq_fwd(z1, gate_w_pad, gate_b_f, head_g_f, dep=ffn1_share[-1])
    h3 = _matmul_residual(mix1, w_gla_out, h2, "gla_out_proj")
    w1g1, w2g1 = [unhalve(g) for g in _wait_copies("gather_ffn1_share_wait", ffn1_share, _plan_share, h3)[1]]
    h4, hp1, uf1 = _ffn_fwd(h3, row(ffn_norm_g[1]), w1g1, w2g1, "ffn1_fwd")

    def start_exchange(name, grads):
        srcs = [_split_rows(g) for g in grads]
        lands = [lax.empty((g.shape[0], g.shape[1] // 2, g.shape[2]), g.dtype) for g in grads]
        return _start_copies(name + "_exchange_start", srcs, lands, _plan_exchange(len(grads)), len(grads))

    def start_scatter(name, exchange, after):
        srcs, recv = _wait_copies(name + "_exchange_wait", exchange, _plan_exchange(len(exchange[2])), after)
        parts = [_sum_halves(g.reshape(g.shape[0], -1, g.shape[3]), r, c_idx, "%s_chip_sum_%d" % (name, k))
                 for k, (g, r) in enumerate(zip(srcs, recv))]
        lands = [lax.empty(p.shape, p.dtype) for p in parts]
        return _start_copies(name + "_scatter_start", parts, lands, _plan_scatter(len(parts)), 3 * len(parts))

    def finish_reduce(name, started, after):
        n = len(started[2])
        parts, lands = _wait_copies(name + "_scatter_wait", started, _plan_scatter(n), after)
        return [_sum_own_and_slots(p, s, chip_idx, "%s_slot_sum_%d" % (name, k)) for k, (p, s) in enumerate(zip(parts, lands))]

    dh4, d_final_g, loss_part = _loss_bwd(h4, row(final_norm_g), loss_target[0])

    dh3, dhp1, d_ffn_g1 = _ffn_bwd_data(dh4, h3, row(ffn_norm_g[1]), hp1, w1g1, w2g1, "ffn1_bwd")
    dw1_1 = _wgrad(uf1, dhp1, N_CHIPS, d, d, False, True, False, "ffn1_dw1")
    dw2_1 = _wgrad(hp1, dh4, N_CHIPS, d, d, True, False, True, "ffn1_dw2")
    ffn1_exchange = start_exchange("ffn1", [dw1_1, dw2_1])

    dmix1 = _dgrad(dh3, w_gla_out, "gla_out_dgrad", dep=ffn1_exchange[-1])
    dw_gla_out = _wgrad(mix1, dh3, 1, GLA_DV, d, False, False, False, "gla_out_dw", dep=ffn1_exchange[-1])
    ffn1_reduce = start_scatter("ffn1", ffn1_exchange, [dmix1, dw_gla_out])
    dz1, d_gate_w, d_gate_b, d_head_g = _gla_seq_bwd(dmix1, o1, z1, states, gate_w_pad, gate_b_f, head_g_f,
                                                     dep=ffn1_reduce[-1])
    dh2, d_mix_g1 = _dgrad_norm_bwd(dz1, w_gla_in, h2, row(mix_norm_g[1]), dh3, 640, "gla_in_dgrad")
    dw_gla_in = _wgrad(u2, dz1, GLA_IN_PAD // 640, d, 640, False, True, False, "gla_in_dw")
    gla_in_shards = jnp.stack([_take_cols(dw_gla_in, j * (GLA_IN // N_CHIPS), GLA_IN // N_CHIPS) for j in range(N_CHIPS)])
    gla_exchange = start_exchange("gla", [gla_in_shards, dw_gla_out.reshape(N_CHIPS, -1, d)])

    dh1, dhp0, d_ffn_g0 = _ffn_bwd_data(dh2, h1, row(ffn_norm_g[0]), hp0, w1g0, w2g0, "ffn0_bwd", dep=gla_exchange[-1])
    gla_reduce = start_scatter("gla", gla_exchange, dh1)
    dw1_0 = _wgrad(uf0, dhp0, N_CHIPS, d, d, False, True, False, "ffn0_dw1", dep=gla_reduce[-1])
    dw2_0 = _wgrad(hp0, dh2, N_CHIPS, d, d, True, False, True, "ffn0_dw2")
    ffn0_exchange = start_exchange("ffn0", [dw1_0, dw2_0])

    dmix0 = _dgrad(dh1, w_cp_out, "cp_out_dgrad", dep=ffn0_exchange[-1])
    dw_cp_out = _wgrad(mix0, dh1, 1, CONV_DIM + POOL_DIM, d, False, False, False, "cp_out_dw", dep=ffn0_exchange[-1])
    ffn0_reduce = start_scatter("ffn0", ffn0_exchange, [dmix0, dw_cp_out])
    dz0, d_conv_w, d_cp_vec, d_pool_w = _cp_seq_bwd(dmix0, z0, c0, pm0, conv_w_f, cp_ln_g, cp_ln_b, cp_pool_w[0],
                                                    cp_pool_scale, dep=ffn0_reduce[-1])
    grad_x, dh0_head, d_mix_g0 = _dgrad_norm_bwd_input(dz0, w_cp_in, h0, row(mix_norm_g[0]), dh1, 512, "cp_in_dgrad")
    grad_x = grad_x[None]
    dw_cp_in = _wgrad(u0, dz0, N_CHIPS, d, CP_IN // N_CHIPS, False, True, False, "cp_in_dw")

    cp_grads = [dw_cp_in, dw_cp_out.reshape(N_CHIPS, -1, d)]
    small_full = [dh0_head[PAD_ROWS:CHUNK],jnp.concatenate([d_mix_g0, d_mix_g1], axis=0),
                  jnp.concatenate([d_ffn_g0, d_ffn_g1], axis=0), d_conv_w[:CONV_WIDTH][None],
                  d_cp_vec[0:1], d_cp_vec[1:2], d_cp_vec[2:3], d_pool_w[None], d_cp_vec[3:4],
                  d_gate_w[:GATE_RANK][None], d_gate_b, d_head_g, d_final_g[0], loss_part[0, 0:1]]
    small_mine = _pack(small_full)
    cp_exchange_plan = _plan_exchange(len(cp_grads))
    cp_exchange = _start_copies(
        "cp_exchange_start", [_split_rows(g) for g in cp_grads] + [small_mine],
        [lax.empty((g.shape[0], g.shape[1] // 2, g.shape[2]), F32) for g in cp_grads] + [lax.empty(small_mine.shape, F32)],
        cp_exchange_plan, len(cp_grads) + 1)
    red_ffn1 = finish_reduce("ffn1", ffn1_reduce, cp_exchange[-1])
    red_gla = finish_reduce("gla", gla_reduce, cp_exchange[-1])
    red_ffn0 = finish_reduce("ffn0", ffn0_reduce, cp_exchange[-1])
    cp_srcs, cp_recv = _wait_copies("cp_exchange_wait", cp_exchange, cp_exchange_plan, [red_ffn1[1], red_gla[1], red_ffn0[1]])
    chip_sums = [_sum_halves(g.reshape(g.shape[0], -1, g.shape[3]), r, c_idx, "cp_chip_sum_%d" % k)
                 for k, (g, r) in enumerate(zip(cp_srcs[:-1], cp_recv[:-1]))]
    small_chip = _add2(cp_srcs[-1], cp_recv[-1], "chip_sum_small")
    small_slots = lax.dynamic_update_slice(jnp.zeros((N_CHIPS,) + small_chip.shape, F32), small_chip[None], (chip, 0, 0))
    cp_lands = [lax.empty(p.shape, p.dtype) for p in chip_sums] + [small_slots]
    cp_reduce = _start_copies("cp_scatter_start", chip_sums + [small_chip], cp_lands, _plan_scatter(len(chip_sums)),
                              3 * (len(chip_sums) + 1))

    big = {"w1": (ffn_w1, m_ffn_w1, v_ffn_w1), "w2": (ffn_w2, m_ffn_w2, v_ffn_w2),
           "cp_in": (cp_w_in, m_cp_w_in, v_cp_w_in), "cp_out": (cp_w_out, m_cp_w_out, v_cp_w_out),
           "gla_in": (gla_w_in, m_gla_w_in, v_gla_w_in), "gla_out": (gla_w_out, m_gla_w_out, v_gla_w_out)}
    other_idx = (1 - core).reshape(1).astype(jnp.int32)

    def adamw_by_halves(tag, reduced):
        flat = [r for n in reduced for r in reduced[n]]
        join_plan = _plan_exchange(0)
        join = _start_copies(tag + "_join_start", flat, [lax.empty(r.shape, F32) for r in flat], join_plan, len(flat))
        views = {n: [_split_rows(a) for a in big[n]] for n in reduced}
        own, k = {}, 0
        for n in reduced:
            mine = join[2][k:k + len(reduced[n])]
            k += len(reduced[n])
            own[n] = _adamw_half(views[n][0], mine, views[n][1], views[n][2], c_idx, None, "adamw_%s_own" % n)
        _, arrived_halves = _wait_copies(tag + "_join_wait", join, join_plan, [own[n][1] for n in reduced])
        outs, k = {}, 0
        for n in reduced:
            theirs = arrived_halves[k:k + len(reduced[n])]
            k += len(reduced[n])
            res = _adamw_half(views[n][0], theirs, views[n][1], views[n][2], other_idx, own[n], "adamw_%s_sibling" % n)
            outs[n] = [o.reshape(big[n][0].shape) for o in res]
        return outs

    big_out = adamw_by_halves("ffn_gla", {"w1": [red_ffn0[0], red_ffn1[0]], "w2": [red_ffn0[1], red_ffn1[1]],
                                          "gla_in": [red_gla[0]], "gla_out": [red_gla[1]]})
    cp_parts, cp_slots = _wait_copies("cp_scatter_wait", cp_reduce, _plan_scatter(len(chip_sums)), big_out["gla_out"][1])
    red_cp = [_sum_own_and_slots(a, s, chip_idx, "cp_slot_sum_%d" % k)
              for k, (a, s) in enumerate(zip(cp_parts[:-1], cp_slots[:-1]))]
    small_red = _sum_slots(cp_slots[-1], "slot_sum_small")
    big_out.update(adamw_by_halves("cp", {"cp_in": [red_cp[0]], "cp_out": [red_cp[1]]}))

    (g_meta, g_mix, g_ffn, g_conv_w, g_conv_b, g_ln_g, g_ln_b, g_pool_w, g_pool_scale, g_gate_w, g_gate_b, g_head,
     g_final, loss_sum) = _unpack(small_red, [a.shape for a in small_full])
    g_meta = _col_shard(g_meta, chip, meta_tokens.shape[-1])
    g_conv_w = _col_shard(g_conv_w, chip, cp_conv_w.shape[-1])
    g_gate_w = _col_shard(g_gate_w, chip, gla_gate_w2.shape[-1])
    g_gate_b = _col_shard(g_gate_b, chip, gla_gate_b.shape[-1])
    g_head = _col_shard(g_head, chip, gla_head_g.shape[-1])
    small_w = [meta_tokens, mix_norm_g, ffn_norm_g, cp_conv_w, cp_conv_b, cp_ln_g, cp_ln_b, cp_pool_w, cp_pool_scale,
               gla_gate_w2, gla_gate_b, gla_head_g, final_norm_g]
    small_m = [m_meta_tokens, m_mix_norm_g, m_ffn_norm_g, m_cp_conv_w, m_cp_conv_b, m_cp_ln_g, m_cp_ln_b, m_cp_pool_w,
               m_cp_pool_scale, m_gla_gate_w2, m_gla_gate_b, m_gla_head_g, m_final_norm_g]
    small_v = [v_meta_tokens, v_mix_norm_g, v_ffn_norm_g, v_cp_conv_w, v_cp_conv_b, v_cp_ln_g, v_cp_ln_b, v_cp_pool_w,
               v_cp_pool_scale, v_gla_gate_w2, v_gla_gate_b, v_gla_head_g, v_final_norm_g]
    small_g = [g_meta, g_mix, g_ffn, g_conv_w, g_conv_b, g_ln_g, g_ln_b, g_pool_w, g_pool_scale, g_gate_w, g_gate_b,
               g_head, g_final]
    shapes = [w.shape for w in small_w]
    small_g = [g.reshape(s) for g, s in zip(small_g, shapes)]
    at_least_2d = lambda arrs: [a.reshape(1, -1) if a.ndim == 1 else a for a in arrs]
    s_delta, s_m, s_v = _adamw_many(at_least_2d(small_w), at_least_2d(small_g), at_least_2d(small_m), at_least_2d(small_v))
    s_delta, s_m, s_v = [[a.reshape(s) for a, s in zip(group, shapes)] for group in (s_delta, s_m, s_v)]

    order = ["meta", "mix", "ffn", "w1", "w2", "cp_in", "conv_w", "conv_b", "ln_g", "ln_b", "pool_w", "pool_scale",
             "cp_out", "gla_in", "gate_w", "gate_b", "head", "gla_out", "final"]
    small_names = ["meta", "mix", "ffn", "conv_w", "conv_b", "ln_g", "ln_b", "pool_w", "pool_scale", "gate_w", "gate_b",
                   "head", "final"]
    big_names = ["w1", "w2", "cp_in", "cp_out", "gla_in", "gla_out"]
    table = {n: (small_g[i], s_delta[i], s_m[i], s_v[i]) for i, n in enumerate(small_names)}
    table.update({n: tuple(big_out[n]) for n in big_names})
    loss = loss_sum.reshape(())
    return (loss, grad_x, *[table[n][0] for n in order], *[table[n][1] for n in order],
            *[table[n][2] for n in order], *[table[n][3] for n in order])
```

```python
import functools

import jax
import jax.numpy as jnp
from jax import lax
from jax.experimental import pallas as pl
from jax.experimental.pallas import tpu as pltpu

F32 = jnp.float32
BF16 = jnp.bfloat16

D_MODEL = 1024
N_META = 16
CHUNK = 64
PAD_ROWS = CHUNK - N_META
EPS = 1e-5
CONV_DIM = 512
CONV_WIDTH = 31
CONV_HALO = 32
POOL_DIM = 512
POOL_WINDOWS = (2, 4, 8, 16)
POOL_GROUP = 128
POOL_HALO = 16
CP_IN = 2 * CONV_DIM + POOL_DIM
GLA_HEADS = 4
GLA_DK = 512
GLA_DV = 1024
GLA_HK = GLA_DK // GLA_HEADS
GLA_HV = GLA_DV // GLA_HEADS
GATE_RANK = 16
GATE_PAD = 128
GATE_NORM = 16.0
GLA_IN = 2 * GLA_DK + 2 * GLA_DV + GATE_RANK
GLA_IN_PAD = 2 * GLA_DK + 2 * GLA_DV + GATE_PAD
N_CHIPS = 4
ADAM_LR = 0.001
ADAM_B1 = 0.9
ADAM_B2 = 0.999
ADAM_EPS = 1e-08
ADAM_WD = 0.01
ADAM_STEP = 10

VMEM_LIMIT_BYTES = 56 * 1024 * 1024
ROW_TILE_TARGET = 832
TOKEN_TILE_TARGET = 1040
PACK_WIDTH = 1024
MESH = pl.DeviceIdType.MESH
HBM_SPEC = pl.BlockSpec(memory_space=pltpu.HBM)
ANY_SPEC = pl.BlockSpec(memory_space=pl.ANY)
SEM_SPEC = pl.BlockSpec(memory_space=pltpu.SEMAPHORE)
SIDE_EFFECT = pltpu.SideEffectType.DATAFLOW_SIDE_EFFECTING


def _cparams(*sem):
    return pltpu.CompilerParams(dimension_semantics=sem, vmem_limit_bytes=VMEM_LIMIT_BYTES)


def _row_tile(t, target, mult):
    best = mult
    for cand in range(mult, min(t, target) + 1, mult):
        if t % cand == 0:
            best = cand
    assert t % best == 0, (t, best)
    return best


def _rms(h, g):
    return h * lax.rsqrt(jnp.mean(h * h, axis=-1, keepdims=True) + EPS) * g


def _rms_bwd(h, g, du):
    r = lax.rsqrt(jnp.mean(h * h, axis=-1, keepdims=True) + EPS)
    xhat = h * r
    dxh = du * g
    dh = r * (dxh - xhat * jnp.mean(dxh * xhat, axis=-1, keepdims=True))
    return dh, du * xhat


def _valid_rows(i, tm):
    row = i * tm + lax.broadcasted_iota(jnp.int32, (tm, 1), 0)
    return row >= PAD_ROWS


def _dot(a, b):
    return jnp.dot(a, b, preferred_element_type=F32)


def _dot_nt(a, b):
    return lax.dot_general(a, b, (((1,), (1,)), ((), ())), preferred_element_type=F32)


def _dot_tn(a, b):
    return lax.dot_general(a, b, (((0,), (0,)), ((), ())), preferred_element_type=F32)


def _accumulate(ref, val, first):
    @pl.when(first)
    def _():
        ref[...] = val

    @pl.when(jnp.logical_not(first))
    def _():
        ref[...] += val


def _call_after(dep, body, n_in, in_specs, args, **kw):
    if dep is None:
        return pl.pallas_call(body, in_specs=in_specs, **kw)(*args)

    def with_dep(*refs):
        body(*refs[:n_in], *refs[n_in + 1:])

    return pl.pallas_call(with_dep, in_specs=list(in_specs) + [ANY_SPEC], **kw)(*args, dep)


def _norm_matmul(h, g, w, nc, name, dep=None):
    t, d = h.shape
    n = w.shape[1]
    tm = _row_tile(t, TOKEN_TILE_TARGET, 16)

    def body(h_ref, g_ref, w_ref, z_ref, u_ref):
        u = _rms(h_ref[...], g_ref[...]).astype(BF16)
        u_ref[...] = u
        for n0 in range(0, n, nc):
            z_ref[:, n0:n0 + nc] = _dot(u, w_ref[:, n0:n0 + nc]).astype(BF16)

    return _call_after(
        dep, body, 3,
        [pl.BlockSpec((tm, d), lambda i: (i, 0)), pl.BlockSpec((1, d), lambda i: (0, 0)),
         pl.BlockSpec((d, n), lambda i: (0, 0))], (h, g, w), grid=(t // tm,),
        out_specs=[pl.BlockSpec((tm, n), lambda i: (i, 0)), pl.BlockSpec((tm, d), lambda i: (i, 0))],
        out_shape=[jax.ShapeDtypeStruct((t, n), BF16), jax.ShapeDtypeStruct((t, d), BF16)],
        compiler_params=_cparams("parallel"), name=name)


def _matmul_residual(a, w, h, name, dep=None):
    t, k = a.shape
    d = w.shape[1]
    tm = _row_tile(t, TOKEN_TILE_TARGET, 16)

    def body(a_ref, w_ref, h_ref, o_ref):
        o_ref[...] = h_ref[...] + _dot(a_ref[...], w_ref[...])

    return _call_after(
        dep, body, 3,
        [pl.BlockSpec((tm, k), lambda i: (i, 0)), pl.BlockSpec((k, d), lambda i: (0, 0)),
         pl.BlockSpec((tm, d), lambda i: (i, 0))], (a, w, h), grid=(t // tm,),
        out_specs=pl.BlockSpec((tm, d), lambda i: (i, 0)),
        out_shape=jax.ShapeDtypeStruct((t, d), F32),
        compiler_params=_cparams("parallel"), name=name)


def _ffn_fwd(h, g, w1g, w2g, name):
    t, d = h.shape
    ns, ffs = w1g.shape[0], w1g.shape[2]
    tm = _row_tile(t, TOKEN_TILE_TARGET, 16)

    def body(h_ref, g_ref, w1_ref, w2_ref, ho_ref, hp_ref, u_ref, acc_ref):
        s = pl.program_id(1)

        @pl.when(s == 0)
        def _():
            u_ref[...] = _rms(h_ref[...], g_ref[...]).astype(BF16)

        hp = _dot(u_ref[...], w1_ref[...])
        hp_ref[...] = hp.astype(BF16)
        a = jnp.maximum(hp, 0.0)
        _accumulate(acc_ref, _dot((a * a).astype(BF16), w2_ref[...]), s == 0)

        @pl.when(s == ns - 1)
        def _():
            ho_ref[...] = h_ref[...] + acc_ref[...]

    return pl.pallas_call(
        body, grid=(t // tm, ns),
        in_specs=[pl.BlockSpec((tm, d), lambda i, s: (i, 0)), pl.BlockSpec((1, d), lambda i, s: (0, 0)),
                  pl.BlockSpec((None, d, ffs), lambda i, s: (s, 0, 0)),
                  pl.BlockSpec((None, ffs, d), lambda i, s: (s, 0, 0))],
        out_specs=[pl.BlockSpec((tm, d), lambda i, s: (i, 0)), pl.BlockSpec((tm, ffs), lambda i, s: (i, s)),
                   pl.BlockSpec((tm, d), lambda i, s: (i, 0))],
        out_shape=[jax.ShapeDtypeStruct((t, d), F32), jax.ShapeDtypeStruct((t, ns * ffs), BF16),
                   jax.ShapeDtypeStruct((t, d), BF16)],
        scratch_shapes=[pltpu.VMEM((tm, d), F32)],
        compiler_params=_cparams("parallel", "arbitrary"), name=name)(h, g, w1g, w2g)


def _ffn_bwd_data(dh, h, g, hp, w1g, w2g, name, dep=None):
    t, d = h.shape
    ns, ffs = w1g.shape[0], w1g.shape[2]
    tm = _row_tile(t, ROW_TILE_TARGET, CHUNK)

    def body(dh_ref, h_ref, g_ref, hp_ref, w1_ref, w2_ref, dhi_ref, dhp_ref, dg_ref, acc_ref):
        i, s = pl.program_id(0), pl.program_id(1)
        da = _dot_nt(dh_ref[...].astype(BF16), w2_ref[...])
        dhp = (da * (2.0 * jnp.maximum(hp_ref[...].astype(F32), 0.0))).astype(BF16)
        dhp_ref[...] = dhp
        _accumulate(acc_ref, _dot_nt(dhp, w1_ref[...]), s == 0)

        @pl.when(s == ns - 1)
        def _():
            dhn, dgr = _rms_bwd(h_ref[...], g_ref[...], acc_ref[...])
            dhi_ref[...] = jnp.where(_valid_rows(i, tm), dh_ref[...] + dhn, 0.0)
            _accumulate(dg_ref, jnp.sum(dgr, axis=0, keepdims=True), i == 0)

    return _call_after(
        dep, body, 6,
        [pl.BlockSpec((tm, d), lambda i, s: (i, 0)), pl.BlockSpec((tm, d), lambda i, s: (i, 0)),
         pl.BlockSpec((1, d), lambda i, s: (0, 0)), pl.BlockSpec((tm, ffs), lambda i, s: (i, s)),
         pl.BlockSpec((None, d, ffs), lambda i, s: (s, 0, 0)),
         pl.BlockSpec((None, ffs, d), lambda i, s: (s, 0, 0))], (dh, h, g, hp, w1g, w2g), grid=(t // tm, ns),
        out_specs=[pl.BlockSpec((tm, d), lambda i, s: (i, 0)), pl.BlockSpec((tm, ffs), lambda i, s: (i, s)),
                   pl.BlockSpec((1, d), lambda i, s: (0, 0))],
        out_shape=[jax.ShapeDtypeStruct((t, d), F32), jax.ShapeDtypeStruct((t, ns * ffs), BF16),
                   jax.ShapeDtypeStruct((1, d), F32)],
        scratch_shapes=[pltpu.VMEM((tm, d), F32)],
        compiler_params=_cparams("arbitrary", "arbitrary"), name=name)


WGRAD_ROWS = 1024


def _wgrad(x, dy, nb, xc, yc, x_by_block, dy_by_block, relu2, name, dep=None):
    t = x.shape[0]
    tk = _row_tile(t - CHUNK, WGRAD_ROWS, CHUNK)

    def prep(xv):
        if relu2:
            xv = jnp.maximum(xv.astype(F32), 0.0)
            xv = xv * xv
        return xv.astype(BF16)

    nk = (t - CHUNK) // tk

    def body(xh_ref, dyh_ref, x_ref, dy_ref, o_ref, acc_ref):
        k = pl.program_id(1)
        p = _dot_tn(prep(x_ref[...]), dy_ref[...].astype(BF16))

        @pl.when(k == 0)
        def _():
            acc_ref[...] = p + _dot_tn(prep(xh_ref[...]), dyh_ref[...].astype(BF16))

        @pl.when(k > 0)
        def _():
            acc_ref[...] += p

        @pl.when(k == nk - 1)
        def _():
            o_ref[...] = acc_ref[...].astype(BF16)

    def head(width, by_block):
        return pl.BlockSpec((CHUNK, width), (lambda b, k: (0, b)) if by_block else (lambda b, k: (0, 0)))

    def rest(width, by_block):
        def index(b, k):
            return pl.multiple_of(CHUNK + k * tk, CHUNK), (pl.multiple_of(b * width, 128) if by_block else 0)
        return pl.BlockSpec((pl.Element(tk), pl.Element(width)), index)

    return _call_after(
        dep, body, 4,
        [head(xc, x_by_block), head(yc, dy_by_block), rest(xc, x_by_block), rest(yc, dy_by_block)], (x, dy, x, dy),
        grid=(nb, nk),
        out_specs=pl.BlockSpec((None, xc, yc), lambda b, k: (b, 0, 0)),
        out_shape=jax.ShapeDtypeStruct((nb, xc, yc), BF16),
        scratch_shapes=[pltpu.VMEM((xc, yc), F32)],
        compiler_params=_cparams("parallel", "arbitrary"), name=name)


def _dgrad(dh, w, name, dep=None):
    t, d = dh.shape
    k = w.shape[0]
    tm = _row_tile(t, TOKEN_TILE_TARGET, 16)

    def body(dh_ref, w_ref, o_ref):
        o_ref[...] = _dot_nt(dh_ref[...].astype(BF16), w_ref[...]).astype(BF16)

    return _call_after(
        dep, body, 2,
        [pl.BlockSpec((tm, d), lambda i: (i, 0)), pl.BlockSpec((k, d), lambda i: (0, 0))], (dh, w), grid=(t // tm,),
        out_specs=pl.BlockSpec((tm, k), lambda i: (i, 0)),
        out_shape=jax.ShapeDtypeStruct((t, k), BF16),
        compiler_params=_cparams("parallel"), name=name)


def _dgrad_norm_bwd(dz, w, h, g, dh, nc, name):
    t, d = h.shape
    n = w.shape[1]
    tm = _row_tile(t, ROW_TILE_TARGET // 2, 16)

    def body(dz_ref, w_ref, h_ref, g_ref, dh_ref, dhi_ref, dg_ref):
        i = pl.program_id(0)
        du = jnp.zeros((tm, d), F32)
        for n0 in range(0, n, nc):
            du = du + _dot_nt(dz_ref[:, n0:n0 + nc], w_ref[:, n0:n0 + nc])
        dhn, dgr = _rms_bwd(h_ref[...], g_ref[...], du)
        dhi_ref[...] = jnp.where(_valid_rows(i, tm), dh_ref[...] + dhn, 0.0)
        _accumulate(dg_ref, jnp.sum(dgr, axis=0, keepdims=True), i == 0)

    return pl.pallas_call(
        body, grid=(t // tm,),
        in_specs=[pl.BlockSpec((tm, n), lambda i: (i, 0)), pl.BlockSpec((d, n), lambda i: (0, 0)),
                  pl.BlockSpec((tm, d), lambda i: (i, 0)), pl.BlockSpec((1, d), lambda i: (0, 0)),
                  pl.BlockSpec((tm, d), lambda i: (i, 0))],
        out_specs=[pl.BlockSpec((tm, d), lambda i: (i, 0)), pl.BlockSpec((1, d), lambda i: (0, 0))],
        out_shape=[jax.ShapeDtypeStruct((t, d), F32), jax.ShapeDtypeStruct((1, d), F32)],
        compiler_params=_cparams("arbitrary"), name=name)(dz, w, h, g, dh)


def _dgrad_norm_bwd_input(dz, w, h, g, dh, nc, name):
    t, d = h.shape
    n = w.shape[1]
    tl = _row_tile(t - CHUNK, 512, CHUNK)

    def grads(dz_ref, w_ref, h_ref, g_ref, dh_ref, rows):
        du = jnp.zeros((rows, d), F32)
        for n0 in range(0, n, nc):
            du = du + _dot_nt(dz_ref[:, n0:n0 + nc], w_ref[:, n0:n0 + nc])
        dhn, dgr = _rms_bwd(h_ref[...], g_ref[...], du)
        return dh_ref[...] + dhn, jnp.sum(dgr, axis=0, keepdims=True)

    def rest_body(dz_ref, w_ref, h_ref, g_ref, dh_ref, dg_head_ref, dx_ref, dg_ref):
        dx, dg = grads(dz_ref, w_ref, h_ref, g_ref, dh_ref, tl)
        dx_ref[...] = dx

        @pl.when(pl.program_id(0) == 0)
        def _():
            dg_ref[...] = dg_head_ref[...] + dg

        @pl.when(pl.program_id(0) > 0)
        def _():
            dg_ref[...] += dg

    def head_body(dz_ref, w_ref, h_ref, g_ref, dh_ref, dx_ref, dg_ref):
        dx, dg = grads(dz_ref, w_ref, h_ref, g_ref, dh_ref, CHUNK)
        dx_ref[...] = jnp.where(_valid_rows(0, CHUNK), dx, 0.0)
        dg_ref[...] = dg

    def shifted(width):
        return pl.BlockSpec((pl.Element(tl), pl.Element(width)), lambda i: (pl.multiple_of(CHUNK + i * tl, CHUNK), 0))

    whole = [pl.BlockSpec((d, n), lambda i: (0, 0)), pl.BlockSpec((1, d), lambda i: (0, 0))]
    head = lambda width: pl.BlockSpec((CHUNK, width), lambda i: (0, 0))
    dh_head, dg_head = pl.pallas_call(
        head_body, grid=(1,), in_specs=[head(n), whole[0], head(d), whole[1], head(d)],
        out_specs=[head(d), whole[1]],
        out_shape=[jax.ShapeDtypeStruct((CHUNK, d), F32), jax.ShapeDtypeStruct((1, d), F32)],
        compiler_params=_cparams("arbitrary"), name=name + "_head")(dz, w, h, g, dh)
    dx, dg = pl.pallas_call(
        rest_body, grid=((t - CHUNK) // tl,),
        in_specs=[shifted(n), whole[0], shifted(d), whole[1], shifted(d), whole[1]],
        out_specs=[pl.BlockSpec((tl, d), lambda i: (i, 0)), whole[1]],
        out_shape=[jax.ShapeDtypeStruct((t - CHUNK, d), F32), jax.ShapeDtypeStruct((1, d), F32)],
        compiler_params=_cparams("arbitrary"), name=name)(dz, w, h, g, dh, dg_head)
    return dx, dh_head, dg


def _loss_bwd(h, g, target):
    t, d = h.shape
    tl = _row_tile(t - CHUNK, 1024, CHUNK)

    def body(h_ref, g_ref, t_ref, dh_ref, dg_ref, loss_ref):
        i = pl.program_id(0)
        hv, gv = h_ref[...], g_ref[...]
        err = _rms(hv, gv) - t_ref[...]
        part = 0.5 * jnp.sum(jnp.mean(err * err, axis=-1, keepdims=True), axis=0, keepdims=True)
        dhn, dgr = _rms_bwd(hv, gv, err * (1.0 / d))
        dh_ref[...] = dhn
        _accumulate(dg_ref, jnp.sum(dgr, axis=0, keepdims=True), i == 0)
        _accumulate(loss_ref, jnp.broadcast_to(part, (8, 128)), i == 0)

    shifted = pl.BlockSpec((pl.Element(tl), pl.Element(d)), lambda i: (pl.multiple_of(CHUNK + i * tl, CHUNK), 0))
    dh, dg, loss = pl.pallas_call(
        body, grid=((t - CHUNK) // tl,),
        in_specs=[shifted, pl.BlockSpec((1, d), lambda i: (0, 0)), pl.BlockSpec((tl, d), lambda i: (i, 0))],
        out_specs=[shifted, pl.BlockSpec((1, d), lambda i: (0, 0)), pl.BlockSpec((8, 128), lambda i: (0, 0))],
        out_shape=[jax.ShapeDtypeStruct((t, d), F32), jax.ShapeDtypeStruct((1, d), F32),
                   jax.ShapeDtypeStruct((8, 128), F32)],
        compiler_params=_cparams("arbitrary"), name="loss_bwd")(h, g, target)

    def zero_head(dh_ref, o_ref):
        o_ref[...] = jnp.zeros_like(o_ref)

    dh = pl.pallas_call(
        zero_head, grid=(1,), in_specs=[ANY_SPEC], out_specs=pl.BlockSpec((CHUNK, d), lambda i: (0, 0)),
        out_shape=jax.ShapeDtypeStruct((t, d), F32), input_output_aliases={0: 0}, name="loss_bwd_head")(dh)
    return dh, dg, loss


CONV_BLOCK = 32


def _silu(x):
    return x * jax.nn.sigmoid(x)


def _row_shifts(win):
    n = win.shape[0]
    return [win] + [pltpu.roll(win, n - j, 0) for j in range(1, 8)]


def _cp_seq_fwd(z, conv_w, conv_b, ln_g, ln_b, pool_w, pool_scale):
    t = z.shape[0]
    tm = _row_tile(t, ROW_TILE_TARGET, CHUNK)

    def body(z_ref, cw_ref, cb_ref, lg_ref, lb_ref, pw_ref, ps_ref, c_ref, pm_ref, mix_ref, gbuf, pbuf):
        i = pl.program_id(0)

        @pl.when(i == 0)
        def _():
            gbuf[0:CONV_HALO, :] = jnp.zeros((CONV_HALO, CONV_DIM), F32)
            pbuf[0:POOL_HALO, :] = jnp.zeros((POOL_HALO, POOL_DIM), F32)

        @pl.when(i > 0)
        def _():
            gbuf[0:CONV_HALO, :] = gbuf[tm:tm + CONV_HALO, :]
            pbuf[0:POOL_HALO, :] = pbuf[tm:tm + POOL_HALO, :]

        av = z_ref[:, 0:CONV_DIM].astype(F32)
        ag = z_ref[:, CONV_DIM:2 * CONV_DIM].astype(F32)
        gbuf[CONV_HALO:CONV_HALO + tm, :] = av * jax.nn.sigmoid(ag)
        pbuf[POOL_HALO:POOL_HALO + tm, :] = z_ref[:, 2 * CONV_DIM:CP_IN].astype(F32)

        def conv_block(rb, carry):
            base = pl.multiple_of(rb * CONV_BLOCK, CONV_BLOCK)
            shifted = _row_shifts(gbuf[pl.ds(base, CONV_BLOCK + CONV_HALO), :])
            acc = jnp.zeros((CONV_BLOCK, CONV_DIM), F32)
            for k in range(CONV_WIDTH):
                whole, part = divmod(CONV_HALO - (CONV_WIDTH - 1) + k, 8)
                acc = acc + cw_ref[k:k + 1, :] * shifted[part][8 * whole:8 * whole + CONV_BLOCK, :]
            c_ref[pl.ds(base, CONV_BLOCK), :] = acc + cb_ref[...]
            return carry

        lax.fori_loop(0, tm // CONV_BLOCK, conv_block, 0)

        c = c_ref[...]
        mu = jnp.mean(c, axis=-1, keepdims=True)
        xc = c - mu
        ln = xc * lax.rsqrt(jnp.mean(xc * xc, axis=-1, keepdims=True) + EPS) * lg_ref[...] + lb_ref[...]
        row = i * tm + lax.broadcasted_iota(jnp.int32, (tm, 1), 0)
        mix_ref[:, 0:CONV_DIM] = jnp.where(row >= PAD_ROWS, _silu(ln), 0.0).astype(BF16)

        tpos = (row - PAD_ROWS + 1).astype(F32)
        for gi, wdw in enumerate(POOL_WINDOWS):
            lo = POOL_GROUP * gi
            cur = pbuf[POOL_HALO:POOL_HALO + tm, lo:lo + POOL_GROUP]
            sacc = cur
            for j in range(1, wdw):
                sacc = sacc + pbuf[POOL_HALO - j:POOL_HALO - j + tm, lo:lo + POOL_GROUP]
            pm = (sacc / jnp.clip(tpos, 1.0, float(wdw)) - cur).astype(BF16)
            pm_ref[:, lo:lo + POOL_GROUP] = pm
            pg = _dot(pm, pw_ref[gi].astype(BF16))
            mix_ref[:, CONV_DIM + lo:CONV_DIM + lo + POOL_GROUP] = (pg * ps_ref[:, lo:lo + POOL_GROUP]).astype(BF16)

    vec = pl.BlockSpec((1, CONV_DIM), lambda i: (0, 0))
    return pl.pallas_call(
        body, grid=(t // tm,),
        in_specs=[pl.BlockSpec((tm, CP_IN), lambda i: (i, 0)),
                  pl.BlockSpec((CONV_WIDTH, CONV_DIM), lambda i: (0, 0)), vec, vec, vec,
                  pl.BlockSpec((len(POOL_WINDOWS), POOL_GROUP, POOL_GROUP), lambda i: (0, 0, 0)), vec],
        out_specs=[pl.BlockSpec((tm, CONV_DIM), lambda i: (i, 0)), pl.BlockSpec((tm, POOL_DIM), lambda i: (i, 0)),
                   pl.BlockSpec((tm, CONV_DIM + POOL_DIM), lambda i: (i, 0))],
        out_shape=[jax.ShapeDtypeStruct((t, CONV_DIM), F32), jax.ShapeDtypeStruct((t, POOL_DIM), BF16),
                   jax.ShapeDtypeStruct((t, CONV_DIM + POOL_DIM), BF16)],
        scratch_shapes=[pltpu.VMEM((tm + CONV_HALO, CONV_DIM), F32), pltpu.VMEM((tm + POOL_HALO, POOL_DIM), F32)],
        compiler_params=_cparams("arbitrary"), name="cp_seq_fwd")(z, conv_w, conv_b, ln_g, ln_b, pool_w, pool_scale)


def _cp_seq_bwd(dmix, z, c, pm, conv_w, ln_g, ln_b, pool_w, pool_scale, dep=None):
    t = z.shape[0]
    tm = _row_tile(t, ROW_TILE_TARGET, CHUNK)
    nt = t // tm

    def body(dmix_ref, z_ref, c_ref, pm_ref, cw_ref, lg_ref, lb_ref, pw_ref, ps_ref,
             dz_ref, dcw_ref, dvec_ref, dpw_ref, dcbuf, qbuf, glu_buf, dwacc):
        i = pl.program_id(0)
        tile = nt - 1 - i

        @pl.when(i == 0)
        def _():
            dcbuf[tm:tm + CONV_HALO, :] = jnp.zeros((CONV_HALO, CONV_DIM), F32)
            qbuf[tm:tm + POOL_HALO, :] = jnp.zeros((POOL_HALO, POOL_DIM), F32)
            dcw_ref[...] = jnp.zeros_like(dcw_ref)
            dwacc[...] = jnp.zeros_like(dwacc)
            dvec_ref[...] = jnp.zeros_like(dvec_ref)
            dpw_ref[...] = jnp.zeros_like(dpw_ref)

        @pl.when(i > 0)
        def _():
            dcbuf[tm:tm + CONV_HALO, :] = dcbuf[0:CONV_HALO, :]
            qbuf[tm:tm + POOL_HALO, :] = qbuf[0:POOL_HALO, :]

        row = tile * tm + lax.broadcasted_iota(jnp.int32, (tm, 1), 0)
        cv = c_ref[...]
        mu = jnp.mean(cv, axis=-1, keepdims=True)
        xc = cv - mu
        rstd = lax.rsqrt(jnp.mean(xc * xc, axis=-1, keepdims=True) + EPS)
        xhat = xc * rstd
        ln = xhat * lg_ref[...] + lb_ref[...]
        sg = jax.nn.sigmoid(ln)
        da = jnp.where(row >= PAD_ROWS, dmix_ref[:, 0:CONV_DIM].astype(F32), 0.0)
        dln = da * (sg * (1.0 + ln * (1.0 - sg)))
        dxh = dln * lg_ref[...]
        dc = rstd * (dxh - jnp.mean(dxh, axis=-1, keepdims=True) - xhat * jnp.mean(dxh * xhat, axis=-1, keepdims=True))
        dcbuf[0:tm, :] = dc
        dvec_ref[0:1, :] += jnp.sum(dc, axis=0, keepdims=True)
        dvec_ref[1:2, :] += jnp.sum(dln * xhat, axis=0, keepdims=True)
        dvec_ref[2:3, :] += jnp.sum(dln, axis=0, keepdims=True)

        av = z_ref[:, 0:CONV_DIM].astype(F32)
        sig_g = jax.nn.sigmoid(z_ref[:, CONV_DIM:2 * CONV_DIM].astype(F32))
        glu_buf[...] = av * sig_g

        def conv_block(rb, carry):
            base = pl.multiple_of(rb * CONV_BLOCK, CONV_BLOCK)
            shifted = _row_shifts(dcbuf[pl.ds(base, CONV_BLOCK + CONV_HALO), :])
            glu = glu_buf[pl.ds(base, CONV_BLOCK), :]
            acc = jnp.zeros((CONV_BLOCK, CONV_DIM), F32)
            for k in range(CONV_WIDTH):
                whole, part = divmod(CONV_WIDTH - 1 - k, 8)
                slab = shifted[part][8 * whole:8 * whole + CONV_BLOCK, :]
                acc = acc + cw_ref[k:k + 1, :] * slab
                prod = slab * glu
                part = prod[0:8]
                for q in range(1, CONV_BLOCK // 8):
                    part = part + prod[8 * q:8 * q + 8]
                dwacc[k] += part
            glu_buf[pl.ds(base, CONV_BLOCK), :] = acc
            return carry

        lax.fori_loop(0, tm // CONV_BLOCK, conv_block, 0)

        @pl.when(i == nt - 1)
        def _():
            for k in range(CONV_WIDTH):
                dcw_ref[k:k + 1, :] = jnp.sum(dwacc[k], axis=0, keepdims=True)
        dglu = glu_buf[...]
        dz_ref[:, 0:CONV_DIM] = (dglu * sig_g).astype(BF16)
        dz_ref[:, CONV_DIM:2 * CONV_DIM] = (dglu * av * sig_g * (1.0 - sig_g)).astype(BF16)

        tpos = (row - PAD_ROWS + 1).astype(F32)
        for gi, wdw in enumerate(POOL_WINDOWS):
            lo = POOL_GROUP * gi
            dp = dmix_ref[:, CONV_DIM + lo:CONV_DIM + lo + POOL_GROUP].astype(F32)
            pmv = pm_ref[:, lo:lo + POOL_GROUP]
            pwb = pw_ref[gi].astype(BF16)
            dvec_ref[3:4, lo:lo + POOL_GROUP] += jnp.sum(dp * _dot(pmv, pwb), axis=0, keepdims=True)
            dq = (dp * ps_ref[:, lo:lo + POOL_GROUP]).astype(BF16)
            dpw_ref[gi] += _dot_tn(pmv, dq)
            dpm = _dot_nt(dq, pwb)
            qbuf[0:tm, lo:lo + POOL_GROUP] = dpm / jnp.clip(tpos, 1.0, float(wdw))
            sacc = -dpm
            for j in range(wdw):
                sacc = sacc + qbuf[j:j + tm, lo:lo + POOL_GROUP]
            dz_ref[:, 2 * CONV_DIM + lo:2 * CONV_DIM + lo + POOL_GROUP] = sacc.astype(BF16)

    vec = pl.BlockSpec((1, CONV_DIM), lambda i: (0, 0))
    rev = lambda i: (nt - 1 - i, 0)
    return _call_after(
        dep, body, 9,
        [pl.BlockSpec((tm, CONV_DIM + POOL_DIM), rev), pl.BlockSpec((tm, CP_IN), rev),
         pl.BlockSpec((tm, CONV_DIM), rev), pl.BlockSpec((tm, POOL_DIM), rev),
         pl.BlockSpec((CONV_WIDTH, CONV_DIM), lambda i: (0, 0)), vec, vec,
         pl.BlockSpec((len(POOL_WINDOWS), POOL_GROUP, POOL_GROUP), lambda i: (0, 0, 0)), vec],
        (dmix, z, c, pm, conv_w, ln_g, ln_b, pool_w, pool_scale), grid=(nt,),
        out_specs=[pl.BlockSpec((tm, CP_IN), rev), pl.BlockSpec((CONV_WIDTH + 1, CONV_DIM), lambda i: (0, 0)),
                   pl.BlockSpec((8, CONV_DIM), lambda i: (0, 0)),
                   pl.BlockSpec((len(POOL_WINDOWS), POOL_GROUP, POOL_GROUP), lambda i: (0, 0, 0))],
        out_shape=[jax.ShapeDtypeStruct((t, CP_IN), BF16), jax.ShapeDtypeStruct((CONV_WIDTH + 1, CONV_DIM), F32),
                   jax.ShapeDtypeStruct((8, CONV_DIM), F32),
                   jax.ShapeDtypeStruct((len(POOL_WINDOWS), POOL_GROUP, POOL_GROUP), F32)],
        scratch_shapes=[pltpu.VMEM((tm + CONV_HALO, CONV_DIM), F32), pltpu.VMEM((tm + POOL_HALO, POOL_DIM), F32),
                        pltpu.VMEM((tm, CONV_DIM), F32), pltpu.VMEM((CONV_WIDTH + 1, 8, CONV_DIM), F32)],
        compiler_params=_cparams("arbitrary"), name="cp_seq_bwd")


GLA_UNROLL = 2
Q0, K0, V0, G0, R0 =0, GLA_DK, 2 * GLA_DK, 2 * GLA_DK + GLA_DV, 2 * GLA_DK + 2 * GLA_DV


def _split3(x):
    hi = x.astype(BF16)
    r1 = x - hi.astype(F32)
    mid = r1.astype(BF16)
    lo = (r1 - mid.astype(F32)).astype(BF16)
    return hi, mid, lo


def _tri(strict):
    r = lax.broadcasted_iota(jnp.int32, (CHUNK, CHUNK), 0)
    c = lax.broadcasted_iota(jnp.int32, (CHUNK, CHUNK), 1)
    return ((r > c) if strict else (r >= c)).astype(BF16)


def _chunk_sums(x, cpt, strict, pieces):
    tri3 = jnp.broadcast_to(_tri(strict)[None], (cpt, CHUNK, CHUNK))
    acc = None
    for piece in _split3(x.reshape(cpt, CHUNK, x.shape[-1]))[:pieces]:
        part = jnp.einsum("bij,bjk->bik", tri3, piece, preferred_element_type=F32)
        acc = part if acc is None else acc + part
    return acc


def _chunk_decay(r, gw_ref, gb_ref, cpt):
    pre = _dot(r, gw_ref[...]) + gb_ref[...]
    lac = (jnp.minimum(pre, 0.0) - jnp.log(1.0 + jnp.exp(-jnp.abs(pre)))) * (1.0 / GATE_NORM)
    cum3 = _chunk_sums(lac, cpt, False, 3)
    return cum3, cum3[:, CHUNK - 1:CHUNK, :]


def _gla_seq_fwd(z, gate_w, gate_b, head_g, dep=None):
    t = z.shape[0]
    tm = _row_tile(t, ROW_TILE_TARGET, CHUNK)
    cpt = tm // CHUNK
    scale = GLA_HK ** -0.5

    def body(z_ref, gw_ref, gb_ref, hg_ref, o_ref, mix_ref, st_ref, state, kdec_s, e_s):
        @pl.when(pl.program_id(0) == 0)
        def _():
            state[...] = jnp.zeros_like(state)

        cum3, tot3 = _chunk_decay(z_ref[:, R0:R0 + GATE_PAD], gw_ref, gb_ref, cpt)
        dec = jnp.exp(jnp.broadcast_to(tot3, cum3.shape) - cum3).reshape(tm, GLA_DK)
        kdec_s[...] = (z_ref[:, K0:K0 + GLA_DK].astype(F32) * dec).astype(BF16)
        e_s[...] = jnp.exp(jnp.broadcast_to(tot3, (cpt, 8, GLA_DK))).reshape(cpt * 8, GLA_DK)

        def chunk(ci, carry):
            rows = pl.ds(pl.multiple_of(ci * CHUNK, CHUNK), CHUNK)
            e_all = e_s[pl.ds(pl.multiple_of(ci * 8, 8), 8), :][0:1, :]
            st_ref[ci] = state[...].astype(BF16)
            for hd in range(GLA_HEADS):
                ks = slice(hd * GLA_HK, (hd + 1) * GLA_HK)
                vs = slice(hd * GLA_HV, (hd + 1) * GLA_HV)
                v = z_ref[rows, V0 + hd * GLA_HV:V0 + (hd + 1) * GLA_HV]
                st = state[vs, :] * e_all[:, ks] + _dot_tn(v, kdec_s[rows, ks])
                state[vs, :] = st
                q = z_ref[rows, Q0 + hd * GLA_HK:Q0 + (hd + 1) * GLA_HK]
                o_ref[rows, vs] = (_dot_nt(q, st.astype(BF16)) * scale).astype(BF16)
            return carry

        lax.fori_loop(0, cpt, chunk, 0, unroll=GLA_UNROLL)

        for hd in range(GLA_HEADS):
            vs = slice(hd * GLA_HV, (hd + 1) * GLA_HV)
            on = _rms(o_ref[:, vs].astype(F32), hg_ref[...])
            gv = z_ref[:, G0 + hd * GLA_HV:G0 + (hd + 1) * GLA_HV].astype(F32)
            mix_ref[:, vs] = (on * _silu(gv)).astype(BF16)

    return _call_after(
        dep, body, 4,
        [pl.BlockSpec((tm, GLA_IN_PAD), lambda i: (i, 0)),
         pl.BlockSpec((GATE_PAD, GLA_DK), lambda i: (0, 0)), pl.BlockSpec((1, GLA_DK), lambda i: (0, 0)),
         pl.BlockSpec((1, GLA_HV), lambda i: (0, 0))], (z, gate_w, gate_b, head_g), grid=(t // tm,),
        out_specs=[pl.BlockSpec((tm, GLA_DV), lambda i: (i, 0)), pl.BlockSpec((tm, GLA_DV), lambda i: (i, 0)),
                   pl.BlockSpec((cpt, GLA_DV, GLA_HK), lambda i: (i, 0, 0))],
        out_shape=[jax.ShapeDtypeStruct((t, GLA_DV), BF16), jax.ShapeDtypeStruct((t, GLA_DV), BF16),
                   jax.ShapeDtypeStruct((t // CHUNK, GLA_DV, GLA_HK), BF16)],
        scratch_shapes=[pltpu.VMEM((GLA_DV, GLA_HK), F32), pltpu.VMEM((tm, GLA_DK), BF16),
                        pltpu.VMEM((cpt * 8, GLA_DK), F32)],
        compiler_params=_cparams("arbitrary"), name="gla_seq_fwd")


def _gla_seq_bwd(dmix, o, z, states, gate_w, gate_b, head_g, dep=None):
    t = z.shape[0]
    tm = _row_tile(t, ROW_TILE_TARGET, CHUNK)
    cpt = tm // CHUNK
    nt = t // tm
    scale = GLA_HK ** -0.5

    def body(dmix_ref, o_ref, z_ref, st_ref, gw_ref, gb_ref, hg_ref, dz_ref, dgw_ref, dgb_ref, dhg_ref,
             dstate, dec_s, kdec_s, dkdec_s, do_s, e_s, dtot_s):
        @pl.when(pl.program_id(0) == 0)
        def _():
            dstate[...] = jnp.zeros_like(dstate)
            dgw_ref[...] = jnp.zeros_like(dgw_ref)
            dgb_ref[...] = jnp.zeros_like(dgb_ref)
            dhg_ref[...] = jnp.zeros_like(dhg_ref)

        cum3, tot3 = _chunk_decay(z_ref[:, R0:R0 + GATE_PAD], gw_ref, gb_ref, cpt)
        dec = jnp.exp(jnp.broadcast_to(tot3, cum3.shape) - cum3).reshape(tm, GLA_DK)
        dec_s[...] = dec
        kdec = z_ref[:, K0:K0 + GLA_DK].astype(F32) * dec
        kdec_s[...] = kdec
        e3 = jnp.exp(tot3)
        e_s[...] = jnp.broadcast_to(e3, (cpt, 8, GLA_DK)).reshape(cpt * 8, GLA_DK)
        dhg = jnp.zeros((1, GLA_HV), F32)
        for hd in range(GLA_HEADS):
            ks = slice(hd * GLA_HK, (hd + 1) * GLA_HK)
            vs = slice(hd * GLA_HV, (hd + 1) * GLA_HV)
            gcols = slice(G0 + hd * GLA_HV, G0 + (hd + 1) * GLA_HV)
            ov = o_ref[:, vs].astype(F32)
            gv = z_ref[:, gcols].astype(F32)
            dm = dmix_ref[:, vs].astype(F32)
            sg = jax.nn.sigmoid(gv)
            rr = lax.rsqrt(jnp.mean(ov * ov, axis=-1, keepdims=True) + EPS)
            xhat = ov * rr
            don = dm * (gv * sg)
            dz_ref[:, gcols] = (dm * (xhat * hg_ref[...]) * (sg * (1.0 + gv * (1.0 - sg)))).astype(BF16)
            dhg = dhg + jnp.sum(don * xhat, axis=0, keepdims=True)
            dxh = don * hg_ref[...]
            do = (rr * (dxh - xhat * jnp.mean(dxh * xhat, axis=-1, keepdims=True)) * scale).astype(BF16)
            do_s[:, vs] = do
            v3 = z_ref[:, V0 + hd * GLA_HV:V0 + (hd + 1) * GLA_HV].reshape(cpt, CHUNK, GLA_HV)
            kdb3 = kdec[:, ks].astype(BF16).reshape(cpt, CHUNK, GLA_HK)
            st3 = st_ref[:, vs, :].astype(F32) * e3[:, :, ks] + jnp.einsum("bcv,bck->bvk", v3, kdb3,
                                                                            preferred_element_type=F32)
            dq3 = jnp.einsum("bcv,bvk->bck", do.reshape(cpt, CHUNK, GLA_HV), st3.astype(BF16), preferred_element_type=F32)
            dz_ref[:, Q0 + hd * GLA_HK:Q0 + (hd + 1) * GLA_HK] = dq3.reshape(tm, GLA_HK).astype(BF16)
        dhg_ref[...] += dhg

        def chunk(cj, carry):
            ci = cpt - 1 - cj
            rows = pl.ds(pl.multiple_of(ci * CHUNK, CHUNK), CHUNK)
            erows = pl.ds(pl.multiple_of(ci * 8, 8), 8)
            e_all = e_s[erows, :][0:1, :]
            for hd in range(GLA_HEADS):
                ks = slice(hd * GLA_HK, (hd + 1) * GLA_HK)
                vs = slice(hd * GLA_HV, (hd + 1) * GLA_HV)
                e = e_all[:, ks]
                kdb = kdec_s[rows, ks].astype(BF16)
                v = z_ref[rows, V0 + hd * GLA_HV:V0 + (hd + 1) * GLA_HV]
                q = z_ref[rows, Q0 + hd * GLA_HK:Q0 + (hd + 1) * GLA_HK]
                do = do_s[rows, vs]
                st_prev = st_ref[ci, vs, :].astype(F32)
                dst = dstate[vs, :] + _dot_tn(do, q)
                dstb = dst.astype(BF16)
                dkdec_s[rows, ks] = _dot(v, dstb)
                dz_ref[rows, V0 + hd * GLA_HV:V0 + (hd + 1) * GLA_HV] = _dot_nt(kdb, dstb).astype(BF16)
                dtot = jnp.sum(dst * st_prev, axis=0, keepdims=True) * e
                dtot_s[erows, ks] = jnp.broadcast_to(dtot, (8, GLA_HK))
                dstate[vs, :] = dst * e
            return carry

        lax.fori_loop(0, cpt, chunk, 0, unroll=GLA_UNROLL)

        dkdec = dkdec_s[...]
        dz_ref[:, K0:K0 + GLA_DK] = (dkdec * dec_s[...]).astype(BF16)
        before = _chunk_sums(dkdec * kdec_s[...], cpt, True, 2)
        dtot3 = dtot_s[...].reshape(cpt, 8, GLA_DK)[:, 0:1, :]
        dlac = (jnp.broadcast_to(dtot3, before.shape) + before).reshape(tm, GLA_DK)
        pre = _dot(z_ref[:, R0:R0 + GATE_PAD], gw_ref[...]) + gb_ref[...]
        dpre = dlac * (1.0 / GATE_NORM) * (1.0 - jax.nn.sigmoid(pre))
        dpb = dpre.astype(BF16)
        dz_ref[:, R0:R0 + GATE_PAD] = _dot_nt(dpb, gw_ref[...]).astype(BF16)
        dgw_ref[...] += _dot_tn(z_ref[:, R0:R0 + GATE_PAD], dpb)
        dgb_ref[...] += jnp.sum(dpre, axis=0, keepdims=True)

    rev = lambda i: (nt - 1 - i, 0)
    return _call_after(
        dep, body, 7,
        [pl.BlockSpec((tm, GLA_DV), rev), pl.BlockSpec((tm, GLA_DV), rev), pl.BlockSpec((tm, GLA_IN_PAD), rev),
         pl.BlockSpec((cpt, GLA_DV, GLA_HK), lambda i: (nt - 1 - i, 0, 0)),
         pl.BlockSpec((GATE_PAD, GLA_DK), lambda i: (0, 0)), pl.BlockSpec((1, GLA_DK), lambda i: (0, 0)),
         pl.BlockSpec((1, GLA_HV), lambda i: (0, 0))],
        (dmix, o, z, states, gate_w, gate_b, head_g), grid=(nt,),
        out_specs=[pl.BlockSpec((tm, GLA_IN_PAD), rev), pl.BlockSpec((GATE_PAD, GLA_DK), lambda i: (0, 0)),
                   pl.BlockSpec((1, GLA_DK), lambda i: (0, 0)), pl.BlockSpec((1, GLA_HV), lambda i: (0, 0))],
        out_shape=[jax.ShapeDtypeStruct((t, GLA_IN_PAD), BF16), jax.ShapeDtypeStruct((GATE_PAD, GLA_DK), F32),
                   jax.ShapeDtypeStruct((1, GLA_DK), F32), jax.ShapeDtypeStruct((1, GLA_HV), F32)],
        scratch_shapes=[pltpu.VMEM((GLA_DV, GLA_HK), F32), pltpu.VMEM((tm, GLA_DK), F32), pltpu.VMEM((tm, GLA_DK), F32),
                        pltpu.VMEM((tm, GLA_DK), F32), pltpu.VMEM((tm, GLA_DV), BF16),
                        pltpu.VMEM((cpt * 8, GLA_DK), F32), pltpu.VMEM((cpt * 8, GLA_DK), F32)],
        compiler_params=_cparams("arbitrary"), name="gla_seq_bwd")


def _sum_slots(x, name):
    n, r, cdim = x.shape
    tr = _row_tile(r, 256, 8)

    def body(x_ref, o_ref):
        acc = x_ref[0].astype(F32)
        for j in range(1, n):
            acc = acc + x_ref[j].astype(F32)
        o_ref[...] = acc

    return pl.pallas_call(
        body, grid=(r // tr,),
        in_specs=[pl.BlockSpec((n, tr, cdim), lambda i: (0, i, 0))],
        out_specs=pl.BlockSpec((tr, cdim), lambda i: (i, 0)),
        out_shape=jax.ShapeDtypeStruct((r, cdim), F32),
        compiler_params=_cparams("parallel"), name=name)(x)


def _sum_own_and_slots(own, slots, dev_idx, name):
    _, _, r, cdim = own.shape
    n = slots.shape[0]
    tr = _row_tile(r, 256, 8)

    def body(s_ref, own_ref, *rest):
        acc = own_ref[...].astype(F32)
        for other in rest[:n - 1]:
            acc = acc + other[...].astype(F32)
        rest[n - 1][...] = acc

    def slot(dd):
        return pl.BlockSpec((None, tr, cdim), lambda i, s: ((s[0] + dd) % n, i, 0))

    mine = pl.BlockSpec((None, None, tr, cdim), lambda i, s: (s[0] // 2, s[0] % 2, i, 0))
    return pl.pallas_call(
        body,
        grid_spec=pltpu.PrefetchScalarGridSpec(
            num_scalar_prefetch=1, grid=(r // tr,), in_specs=[mine] + [slot(dd) for dd in range(1, n)],
            out_specs=pl.BlockSpec((tr, cdim), lambda i, s: (i, 0))),
        out_shape=jax.ShapeDtypeStruct((r, cdim), F32),
        compiler_params=_cparams("parallel"), name=name)(dev_idx, own, *([slots] * (n - 1)))


def _add2(a, b, name):
    r, cdim = a.shape
    tr = _row_tile(r, 256, 8)

    def body(a_ref, b_ref, o_ref):
        o_ref[...] = a_ref[...] + b_ref[...]

    spec = pl.BlockSpec((tr, cdim), lambda i: (i, 0))
    return pl.pallas_call(body, grid=(r // tr,), in_specs=[spec, spec], out_specs=spec,
                          out_shape=jax.ShapeDtypeStruct((r, cdim), F32),
                          compiler_params=_cparams("parallel"), name=name)(a, b)


def _adamw_half(w, gs, m, v, half_idx, prev, name, dep=None):
    nl, _, h, cdim = w.shape
    tr = _row_tile(h, 256, 8)
    nprev = 0 if prev is None else 4
    extra = [] if dep is None else [dep]

    def body(s_ref, w_ref, m_ref, v_ref, *rest):
        g_refs = rest[:nl]
        go_ref, d_ref, mo_ref, vo_ref = rest[nl + nprev + len(extra):]
        layer = pl.program_id(0)
        gv = g_refs[0][...]
        for j in range(1, nl):
            gv = jnp.where(layer == j, g_refs[j][...], gv)
        go_ref[...] = gv
        mn = ADAM_B1 * m_ref[...] + (1.0 - ADAM_B1) * gv
        vn = ADAM_B2 * v_ref[...] + (1.0 - ADAM_B2) * (gv * gv)
        m_hat = mn / (1.0 - ADAM_B1 ** ADAM_STEP)
        v_hat = vn / (1.0 - ADAM_B2 ** ADAM_STEP)
        d_ref[...] = -ADAM_LR * (m_hat / (jnp.sqrt(v_hat) + ADAM_EPS) + ADAM_WD * w_ref[...])
        mo_ref[...] = mn
        vo_ref[...] = vn

    half = pl.BlockSpec((None, None, tr, cdim), lambda l, i, s: (l, s[0], i, 0))

    def of_layer(j):
        return pl.BlockSpec((tr, cdim), lambda l, i, s: (jnp.where(l == j, i, 0), 0))

    shp = jax.ShapeDtypeStruct(w.shape, F32)
    return pl.pallas_call(
        body,
        grid_spec=pltpu.PrefetchScalarGridSpec(
            num_scalar_prefetch=1, grid=(nl, h // tr),
            in_specs=[half] * 3 + [of_layer(j) for j in range(nl)] + [ANY_SPEC] * (nprev + len(extra)),
            out_specs=[half] * 4),
        out_shape=[shp] * 4, input_output_aliases={4 + nl + k: k for k in range(nprev)},
        compiler_params=_cparams("arbitrary", "arbitrary"), name=name,
    )(half_idx, w, m, v, *gs, *([] if prev is None else prev), *extra)


def _adamw_many(ws, gs, ms, vs):
    n = len(ws)

    def body(*refs):
        for i in range(n):
            w_ref, g_ref, m_ref, v_ref = refs[i], refs[n + i], refs[2 * n + i], refs[3 * n + i]
            d_ref, mo_ref, vo_ref = refs[4 * n + i], refs[5 * n + i], refs[6 * n + i]
            gv = g_ref[...]
            mn = ADAM_B1 * m_ref[...] + (1.0 - ADAM_B1) * gv
            vn = ADAM_B2 * v_ref[...] + (1.0 - ADAM_B2) * (gv * gv)
            m_hat = mn / (1.0 - ADAM_B1 ** ADAM_STEP)
            v_hat = vn / (1.0 - ADAM_B2 ** ADAM_STEP)
            d_ref[...] = -ADAM_LR * (m_hat / (jnp.sqrt(v_hat) + ADAM_EPS) + ADAM_WD * w_ref[...])
            mo_ref[...] = mn
            vo_ref[...] = vn

    shapes = [jax.ShapeDtypeStruct(w.shape, F32) for w in ws]
    outs = pl.pallas_call(body, out_shape=shapes * 3, name="adamw_small")(*ws, *gs, *ms, *vs)
    return outs[:n], outs[n:2 * n], outs[2 * n:]


def _split_rows(a):
    return a.reshape(a.shape[0], 2, a.shape[1] // 2, a.shape[2])


def _place():
    x, y, c = lax.axis_index("x"), lax.axis_index("y"), lax.axis_index("c")
    chips = [(1 - x, y), (x, 1 - y), (1 - x, 1 - y)]
    return x, y, c, chips


def _remote(src, dst, send_sem, recv_sem, to):
    return pltpu.make_async_remote_copy(src_ref=src, dst_ref=dst, send_sem=send_sem, recv_sem=recv_sem,
                                        device_id=to, device_id_type=MESH)


def _plan_gather(n_halved):
    def plan(src_refs, land_refs):
        x, y, c, chips = _place()
        me = 2 * x + y
        copies = []
        for k, (src, land) in enumerate(zip(src_refs, land_refs)):
            for (px, py) in chips:
                frm = 2 * px + py
                if k < n_halved:
                    copies.append((src.at[c], land.at[me, c], (px, py, c), land.at[frm, c]))
                else:
                    copies.append((src, land.at[me], (px, py, c), land.at[frm]))
        return copies
    return plan


def _plan_share(src_refs, land_refs):
    x, y, c, chips = _place()
    me = 2 * x + y
    sib = (x, y, 1 - c)
    copies = []
    for src, land in zip(src_refs, land_refs):
        copies.append((src, land.at[me], sib, land.at[me]))
        for (px, py) in chips:
            frm = 2 * px + py
            copies.append((land.at[frm, c], land.at[frm, c], sib, land.at[frm, 1 - c]))
    return copies


def _plan_scatter(n_parts):
    def plan(src_refs, land_refs):
        x, y, c, chips = _place()
        me = 2 * x + y
        copies = []
        for k, (src, land) in enumerate(zip(src_refs, land_refs)):
            for (px, py) in chips:
                to = 2 * px + py
                copies.append((src.at[to] if k < n_parts else src, land.at[me], (px, py, c), land.at[to]))
        return copies
    return plan


N_DEVICES = 8
OTHER_DEVICES = [(dx, dy, dc) for dx in (0, 1) for dy in (0, 1) for dc in (0, 1) if dx or dy or dc]


def _plan_scatter_all(src_refs, land_refs):
    x, y, c, _ = _place()
    me = 4 * x + 2 * y + c
    copies = []
    for src, land in zip(src_refs, land_refs):
        for dx, dy, dc in OTHER_DEVICES:
            px, py, pc = (1 - x if dx else x), (1 - y if dy else y), (1 - c if dc else c)
            copies.append((src.at[2 * px + py, pc], land.at[me], (px, py, pc), land.at[4 * px + 2 * py + pc]))
    return copies


def _plan_exchange(n_split):
    def plan(src_refs, land_refs):
        x, y, c, _ = _place()
        sib = (x, y, 1 - c)
        return [(src.at[:, 1 - c] if k < n_split else src, land, sib, land)
                for k, (src, land) in enumerate(zip(src_refs, land_refs))]
    return plan


def _hbm(a):
    return pltpu.HBM(a.shape, a.dtype)


def _start_copies(name, srcs, lands, plan, ncopy, dep=None):
    ns, nl = len(srcs), len(lands)
    nin = ns + nl + (0 if dep is None else 1)

    def body(*refs):
        send_sems, recv_sems, token = refs[nin], refs[nin + 1], refs[-1]
        for k, (src, dst, dev, _) in enumerate(plan(refs[:ns], refs[ns:ns + nl])):
            _remote(src, dst, send_sems.at[k], recv_sems.at[k], dev).start()
        token[...] = jnp.zeros_like(token)

    args = [pltpu.with_memory_space_constraint(a, pltpu.HBM) for a in list(srcs) + list(lands)]
    outs = pl.pallas_call(
        body, name=name,
        out_shape=(pltpu.SemaphoreType.DMA((ncopy,)), pltpu.SemaphoreType.DMA((ncopy,)),
                   *[_hbm(a) for a in list(srcs) + list(lands)], jax.ShapeDtypeStruct((8, 128), F32)),
        in_specs=[HBM_SPEC] * (ns + nl) + ([] if dep is None else [ANY_SPEC]),
        out_specs=(SEM_SPEC, SEM_SPEC, *([HBM_SPEC] * (ns + nl)), pl.BlockSpec(memory_space=pltpu.VMEM)),
        input_output_aliases={i: 2 + i for i in range(ns + nl)},
        compiler_params=pltpu.CompilerParams(has_side_effects=SIDE_EFFECT),
    )(*args, *([] if dep is None else [dep]))
    return outs[0], outs[1], list(outs[2:2 + ns]), list(outs[2 + ns:2 + ns + nl]), outs[-1]


def _wait_copies(name, started, plan, after, sem_offset=0):
    send_sems, recv_sems, srcs, lands, _ = started
    ns, nl = len(srcs), len(lands)
    after = list(after) if isinstance(after, (list, tuple)) else [after]

    def body(*refs):
        send_ref, recv_ref = refs[ns + nl], refs[ns + nl + 1]
        for k, (src, _, dev, mine) in enumerate(plan(refs[:ns], refs[ns:ns + nl])):
            copy = _remote(src, mine, send_ref.at[sem_offset + k], recv_ref.at[sem_offset + k], dev)
            copy.wait_send()
            copy.wait_recv()

    outs = pl.pallas_call(
        body, name=name, out_shape=tuple(_hbm(a) for a in srcs + lands),
        in_specs=[HBM_SPEC] * (ns + nl) + [SEM_SPEC, SEM_SPEC] + [ANY_SPEC] * len(after),
        out_specs=tuple([HBM_SPEC] * (ns + nl)),
        input_output_aliases={i: i for i in range(ns + nl)},
        compiler_params=pltpu.CompilerParams(has_side_effects=SIDE_EFFECT),
    )(*srcs, *lands, send_sems, recv_sems, *after)
    return list(outs[:ns]), list(outs[ns:])


def _share_with_sibling(name, srcs, lands):
    n = len(srcs)

    def body(*refs):
        src_refs, land_refs, out_refs = refs[:n], refs[n:2 * n], refs[2 * n:3 * n]
        send_sem, recv_sem = refs[3 * n:]
        x, y, c, chips = _place()
        me = 2 * x + y
        sib = (x, y, 1 - c)
        sends, recvs = [], []
        for k in range(n):
            sems = (send_sem.at[4 * k], recv_sem.at[4 * k])
            sends.append(_remote(src_refs[k], out_refs[k].at[me], *sems, sib))
            recvs.append(_remote(src_refs[k], out_refs[k].at[me], *sems, sib))
            for j, (px, py) in enumerate(chips):
                frm = 2 * px + py
                sems = (send_sem.at[4 * k + 1 + j], recv_sem.at[4 * k + 1 + j])
                sends.append(_remote(land_refs[k].at[frm, c], out_refs[k].at[frm, c], *sems, sib))
                recvs.append(_remote(land_refs[k].at[frm, c], out_refs[k].at[frm, 1 - c], *sems, sib))
        for cp in sends:
            cp.start()
        for cp in recvs:
            cp.wait_recv()
        for cp in sends:
            cp.wait_send()

    return pl.pallas_call(
        body, name=name, in_specs=[HBM_SPEC] * (2 * n), out_specs=[HBM_SPEC] * n,
        out_shape=[jax.ShapeDtypeStruct(a.shape, a.dtype) for a in lands],
        input_output_aliases={n + k: k for k in range(n)},
        scratch_shapes=[pltpu.SemaphoreType.DMA((4 * n,)), pltpu.SemaphoreType.DMA((4 * n,))],
    )(*srcs, *lands)


def _pack(arrs):
    flat = jnp.concatenate([a.reshape(-1).astype(F32) for a in arrs])
    n = flat.shape[0]
    rows = -(-n // PACK_WIDTH)
    rows = -(-rows // 8) * 8
    return jnp.pad(flat, (0, rows * PACK_WIDTH - n)).reshape(rows, PACK_WIDTH)


def _unpack(buf, shapes):
    flat = buf.reshape(-1)
    out, off = [], 0
    for shp in shapes:
        n = 1
        for s in shp:
            n *= s
        out.append(flat[off:off + n].reshape(shp))
        off += n
    return out


def _unshard_cols(stacked):
    moved = jnp.moveaxis(stacked, 0, -2)
    return moved.reshape(moved.shape[:-2] + (moved.shape[-2] * moved.shape[-1],))


def _take_cols(blocks, start, width):
    bw = blocks.shape[2]
    pieces, lo = [], start
    while lo < start + width:
        b = lo // bw
        hi = min(start + width, (b + 1) * bw)
        pieces.append(blocks[b][:, lo - b * bw:hi - b * bw])
        lo = hi
    return jnp.concatenate(pieces, axis=1)


def _col_shard(full, s, width):
    return lax.dynamic_slice_in_dim(full, s * width, width, axis=full.ndim - 1)


def kernel(x, meta_tokens, mix_norm_g, ffn_norm_g, ffn_w1, ffn_w2, cp_w_in, cp_conv_w, cp_conv_b, cp_ln_g, cp_ln_b, cp_pool_w, cp_pool_scale, cp_w_out, gla_w_in, gla_gate_w2, gla_gate_b, gla_head_g, gla_w_out, final_norm_g, loss_target, m_meta_tokens, m_mix_norm_g, m_ffn_norm_g, m_ffn_w1, m_ffn_w2, m_cp_w_in, m_cp_conv_w, m_cp_conv_b, m_cp_ln_g, m_cp_ln_b, m_cp_pool_w, m_cp_pool_scale, m_cp_w_out, m_gla_w_in, m_gla_gate_w2, m_gla_gate_b, m_gla_head_g, m_gla_w_out, m_final_norm_g, v_meta_tokens, v_mix_norm_g, v_ffn_norm_g, v_ffn_w1, v_ffn_w2, v_cp_w_in, v_cp_conv_w, v_cp_conv_b, v_cp_ln_g, v_cp_ln_b, v_cp_pool_w, v_cp_pool_scale, v_cp_w_out, v_gla_w_in, v_gla_gate_w2, v_gla_gate_b, v_gla_head_g, v_gla_w_out, v_final_norm_g):
    d = D_MODEL
    chip = 2 * lax.axis_index("x") + lax.axis_index("y")
    core = lax.axis_index("c")
    seq = x.shape[1]
    t = seq + CHUNK

    sharded_small = [meta_tokens, cp_conv_w, gla_gate_w2, gla_gate_b, gla_head_g]

    def halves(w):
        return w.astype(BF16).reshape(2, w.shape[0] // 2, w.shape[1])

    def unhalve(g):
        return g.reshape(N_CHIPS, 2 * g.shape[2], g.shape[3])

    def gather_group(srcs, whole=()):
        lands = [lax.empty((N_CHIPS,) + s.shape, s.dtype) for s in srcs]
        for a in whole:
            lands.append(lax.dynamic_update_slice(jnp.zeros((N_CHIPS,) + a.shape, a.dtype), a[None], (chip,) + (0,) * a.ndim))
        return list(srcs) + list(whole), lands, _plan_gather(len(srcs)), len(srcs)

    groups = [gather_group([halves(cp_w_in[0]), halves(cp_w_out[0])], [_pack(sharded_small)]),
              gather_group([halves(ffn_w1[0]), halves(ffn_w2[0])]), gather_group([halves(gla_w_in[0]), halves(gla_w_out[0])]),
              gather_group([halves(ffn_w1[1]), halves(ffn_w2[1])])]
    bounds, all_srcs, all_lands = [], [], []
    for srcs, lands, _, _ in groups:
        bounds.append((len(all_srcs), len(all_srcs) + len(srcs)))
        all_srcs += srcs
        all_lands += lands

    def plan_all(src_refs, land_refs):
        return [cp for (lo, hi), group in zip(bounds, groups) for cp in group[2](src_refs[lo:hi], land_refs[lo:hi])]

    gathers = _start_copies("gather_start", all_srcs, all_lands, plan_all, 3 * len(all_srcs))

    def arrived(name, gi, after):
        (lo, hi), plan, n = bounds[gi], groups[gi][2], groups[gi][3]
        mine = (gathers[0], gathers[1], gathers[2][lo:hi], gathers[3][lo:hi], gathers[4])
        srcs, lands = _wait_copies(name + "_wait", mine, plan, after, sem_offset=3 * lo)
        return srcs[:n], lands[:n], lands[n:]

    cp_gather, ffn0_gather, gla_gather, ffn1_gather = 0, 1, 2, 3
    h0_rows = jnp.concatenate([jnp.zeros((CHUNK, d), F32) + gathers[4][0, 0], x[0]], axis=0)
    cp_srcs, cp_lands, (small_g,) = arrived("gather_cp", cp_gather, h0_rows)
    cpin_g, cpout_g = [unhalve(g) for g in _share_with_sibling("gather_cp_share", cp_srcs, cp_lands)]
    per_chip = [_unpack(small_g[j], [a.shape for a in sharded_small]) for j in range(N_CHIPS)]
    meta_f, conv_w_f, gate_w_f, gate_b_f, head_g_f = [
        jnp.concatenate([per_chip[j][i] for j in range(N_CHIPS)], axis=-1) for i in range(len(sharded_small))]
    conv_w_f, gate_w_f = conv_w_f[0], gate_w_f[0]
    w_cp_in = _unshard_cols(cpin_g)
    w_cp_out = cpout_g.reshape(CONV_DIM + POOL_DIM, d)
    gate_w_pad = jnp.pad(gate_w_f, ((0, GATE_PAD - GATE_RANK), (0, 0))).astype(BF16)
    row = lambda a: a.reshape(1, -1)
    c_idx = core.reshape(1).astype(jnp.int32)

    h0 = lax.dynamic_update_slice(h0_rows, meta_f, (PAD_ROWS, 0))
    z0, u0 = _norm_matmul(h0, row(mix_norm_g[0]), w_cp_in, 512, "cp_in_proj")
    c0, pm0, mix0 = _cp_seq_fwd(z0, conv_w_f, cp_conv_b, cp_ln_g, cp_ln_b, cp_pool_w[0], cp_pool_scale)
    ffn0_srcs, ffn0_lands, _ = arrived("gather_ffn0", ffn0_gather, mix0)
    ffn0_share = _start_copies("gather_ffn0_share_start", ffn0_srcs, ffn0_lands, _plan_share, 4 * len(ffn0_srcs))
    h1 = _matmul_residual(mix0, w_cp_out, h0, "cp_out_proj", dep=ffn0_share[-1])
    w1g0, w2g0 = [unhalve(g) for g in _wait_copies("gather_ffn0_share_wait", ffn0_share, _plan_share, h1)[1]]
    h2, hp0, uf0 = _ffn_fwd(h1, row(ffn_norm_g[0]), w1g0, w2g0, "ffn0_fwd")
    gla_srcs, gla_lands, _ = arrived("gather_gla", gla_gather, h2)
    glain_g, glaout_g = [unhalve(g) for g in _share_with_sibling("gather_gla_share", gla_srcs, gla_lands)]
    w_gla_in = jnp.concatenate([glain_g[j] for j in range(N_CHIPS)] + [jnp.zeros((d, GLA_IN_PAD - GLA_IN), BF16)], axis=1)
    w_gla_out = glaout_g.reshape(GLA_DV, d)
    z1, u2 = _norm_matmul(h2, row(mix_norm_g[1]), w_gla_in, 640, "gla_in_proj")
    ffn1_srcs, ffn1_lands, _ = arrived("gather_ffn1", ffn1_gather, z1)
    ffn1_share = _start_copies("gather_ffn1_share_start", ffn1_srcs, ffn1_lands, _plan_share, 4 * len(ffn1_srcs))
    o1, mix1, states = _gla_seq_fwd(z1, gate_w_pad, gate_b_f, head_g_f, dep=ffn1_share[-1])
    h3 = _matmul_residual(mix1, w_gla_out, h2, "gla_out_proj")
    w1g1, w2g1 = [unhalve(g) for g in _wait_copies("gather_ffn1_share_wait", ffn1_share, _plan_share, h3)[1]]
    h4, hp1, uf1 = _ffn_fwd(h3, row(ffn_norm_g[1]), w1g1, w2g1, "ffn1_fwd")

    dev_idx = (2 * chip + core).reshape(1).astype(jnp.int32)

    def start_reduce(name, grads):
        srcs = [_split_rows(g) for g in grads]
        lands = [lax.empty((N_DEVICES,) + s.shape[2:], s.dtype) for s in srcs]
        return _start_copies(name + "_scatter_start", srcs, lands, _plan_scatter_all, len(OTHER_DEVICES) * len(srcs))

    def finish_reduce(name, started, after):
        srcs, lands = _wait_copies(name + "_scatter_wait", started, _plan_scatter_all, after)
        return [_sum_own_and_slots(s, l, dev_idx, "%s_slot_sum_%d" % (name, k)) for k, (s, l) in enumerate(zip(srcs, lands))]

    dh4, d_final_g, loss_part = _loss_bwd(h4, row(final_norm_g), loss_target[0])

    dh3, dhp1, d_ffn_g1 = _ffn_bwd_data(dh4, h3, row(ffn_norm_g[1]), hp1, w1g1, w2g1, "ffn1_bwd")
    dw1_1 = _wgrad(uf1, dhp1, N_CHIPS, d, d, False, True, False, "ffn1_dw1")
    dw2_1 = _wgrad(hp1, dh4, N_CHIPS, d, d, True, False, True, "ffn1_dw2")
    ffn1_reduce = start_reduce("ffn1", [dw1_1, dw2_1])

    dmix1 = _dgrad(dh3, w_gla_out, "gla_out_dgrad", dep=ffn1_reduce[-1])
    dw_gla_out = _wgrad(mix1, dh3, 1, GLA_DV, d, False, False, False, "gla_out_dw")
    dz1, d_gate_w, d_gate_b, d_head_g = _gla_seq_bwd(dmix1, o1, z1, states, gate_w_pad, gate_b_f, head_g_f)
    dh2, d_mix_g1 = _dgrad_norm_bwd(dz1, w_gla_in, h2, row(mix_norm_g[1]), dh3, 640, "gla_in_dgrad")
    dw_gla_in = _wgrad(u2, dz1, GLA_IN_PAD // 640, d, 640, False, True, False, "gla_in_dw")
    gla_in_shards = jnp.stack([_take_cols(dw_gla_in, j * (GLA_IN // N_CHIPS), GLA_IN // N_CHIPS) for j in range(N_CHIPS)])
    gla_reduce = start_reduce("gla", [gla_in_shards, dw_gla_out.reshape(N_CHIPS, -1, d)])

    dh1, dhp0, d_ffn_g0 = _ffn_bwd_data(dh2, h1, row(ffn_norm_g[0]), hp0, w1g0, w2g0, "ffn0_bwd", dep=gla_reduce[-1])
    dw1_0 = _wgrad(uf0, dhp0, N_CHIPS, d, d, False, True, False, "ffn0_dw1")
    dw2_0 = _wgrad(hp0, dh2, N_CHIPS, d, d, True, False, True, "ffn0_dw2")
    ffn0_reduce = start_reduce("ffn0", [dw1_0, dw2_0])

    dmix0 = _dgrad(dh1, w_cp_out, "cp_out_dgrad", dep=ffn0_reduce[-1])
    dw_cp_out = _wgrad(mix0, dh1, 1, CONV_DIM + POOL_DIM, d, False, False, False, "cp_out_dw")
    dz0, d_conv_w, d_cp_vec, d_pool_w = _cp_seq_bwd(dmix0, z0, c0, pm0, conv_w_f, cp_ln_g, cp_ln_b, cp_pool_w[0],
                                                    cp_pool_scale)
    grad_x, dh0_head, d_mix_g0 = _dgrad_norm_bwd_input(dz0, w_cp_in, h0, row(mix_norm_g[0]), dh1, 512, "cp_in_dgrad")
    grad_x = grad_x[None]
    dw_cp_in = _wgrad(u0, dz0, N_CHIPS, d, CP_IN // N_CHIPS, False, True, False, "cp_in_dw")

    cp_reduce = start_reduce("cp", [dw_cp_in, dw_cp_out.reshape(N_CHIPS, -1, d)])
    small_full = [dh0_head[PAD_ROWS:CHUNK],jnp.concatenate([d_mix_g0, d_mix_g1], axis=0),
                  jnp.concatenate([d_ffn_g0, d_ffn_g1], axis=0), d_conv_w[:CONV_WIDTH][None],
                  d_cp_vec[0:1], d_cp_vec[1:2], d_cp_vec[2:3], d_pool_w[None], d_cp_vec[3:4],
                  d_gate_w[:GATE_RANK][None], d_gate_b, d_head_g, d_final_g[0], loss_part[0, 0:1]]
    small_mine = _pack(small_full)
    whole = _plan_exchange(0)
    small_exchange = _start_copies("small_exchange_start", [small_mine], [lax.empty(small_mine.shape, F32)], whole, 1,
                                   dep=cp_reduce[-1])
    red_ffn1 = finish_reduce("ffn1", ffn1_reduce, small_exchange[-1])
    red_gla = finish_reduce("gla", gla_reduce, small_exchange[-1])
    (small_sent,), (small_recv,) = _wait_copies("small_exchange_wait", small_exchange, whole, [red_ffn1[1], red_gla[1]])
    small_chip = _add2(small_sent, small_recv, "chip_sum_small")
    small_slots = lax.dynamic_update_slice(jnp.zeros((N_CHIPS,) + small_chip.shape, F32), small_chip[None], (chip, 0, 0))
    small_reduce = _start_copies("small_scatter_start", [small_chip], [small_slots], _plan_scatter(0), 3)

    big = {"w1": (ffn_w1, m_ffn_w1, v_ffn_w1), "w2": (ffn_w2, m_ffn_w2, v_ffn_w2),
           "cp_in": (cp_w_in, m_cp_w_in, v_cp_w_in), "cp_out": (cp_w_out, m_cp_w_out, v_cp_w_out),
           "gla_in": (gla_w_in, m_gla_w_in, v_gla_w_in), "gla_out": (gla_w_out, m_gla_w_out, v_gla_w_out)}
    other_idx = (1 - core).reshape(1).astype(jnp.int32)

    def adamw_by_halves(tag, reduced):
        flat = [r for n in reduced for r in reduced[n]]
        join_plan = _plan_exchange(0)
        join = _start_copies(tag + "_join_start", flat, [lax.empty(r.shape, F32) for r in flat], join_plan, len(flat))
        views = {n: [_split_rows(a) for a in big[n]] for n in reduced}
        own, k = {}, 0
        for n in reduced:
            mine = join[2][k:k + len(reduced[n])]
            k += len(reduced[n])
            own[n] = _adamw_half(views[n][0], mine, views[n][1], views[n][2], c_idx, None, "adamw_%s_own" % n)
        _, arrived_halves = _wait_copies(tag + "_join_wait", join, join_plan, [own[n][1] for n in reduced])
        outs, k = {}, 0
        for n in reduced:
            theirs = arrived_halves[k:k + len(reduced[n])]
            k += len(reduced[n])
            res = _adamw_half(views[n][0], theirs, views[n][1], views[n][2], other_idx, own[n], "adamw_%s_sibling" % n)
            outs[n] = [o.reshape(big[n][0].shape) for o in res]
        return outs

    big_out = adamw_by_halves("gla", {"gla_in": [red_gla[0]], "gla_out": [red_gla[1]]})
    red_ffn0 = finish_reduce("ffn0", ffn0_reduce, big_out["gla_out"][1])
    big_out.update(adamw_by_halves("ffn", {"w1": [red_ffn0[0], red_ffn1[0]], "w2": [red_ffn0[1], red_ffn1[1]]}))
    red_cp = finish_reduce("cp", cp_reduce, big_out["w2"][1])
    _, (small_landed,) = _wait_copies("small_scatter_wait", small_reduce, _plan_scatter(0), big_out["w2"][1])
    small_red = _sum_slots(small_landed, "slot_sum_small")
    big_out.update(adamw_by_halves("cp", {"cp_in": [red_cp[0]], "cp_out": [red_cp[1]]}))

    (g_meta, g_mix, g_ffn, g_conv_w, g_conv_b, g_ln_g, g_ln_b, g_pool_w, g_pool_scale, g_gate_w, g_gate_b, g_head,
     g_final, loss_sum) = _unpack(small_red, [a.shape for a in small_full])
    g_meta = _col_shard(g_meta, chip, meta_tokens.shape[-1])
    g_conv_w = _col_shard(g_conv_w, chip, cp_conv_w.shape[-1])
    g_gate_w = _col_shard(g_gate_w, chip, gla_gate_w2.shape[-1])
    g_gate_b = _col_shard(g_gate_b, chip, gla_gate_b.shape[-1])
    g_head = _col_shard(g_head, chip, gla_head_g.shape[-1])
    small_w = [meta_tokens, mix_norm_g, ffn_norm_g, cp_conv_w, cp_conv_b, cp_ln_g, cp_ln_b, cp_pool_w, cp_pool_scale,
               gla_gate_w2, gla_gate_b, gla_head_g, final_norm_g]
    small_m = [m_meta_tokens, m_mix_norm_g, m_ffn_norm_g, m_cp_conv_w, m_cp_conv_b, m_cp_ln_g, m_cp_ln_b, m_cp_pool_w,
               m_cp_pool_scale, m_gla_gate_w2, m_gla_gate_b, m_gla_head_g, m_final_norm_g]
    small_v = [v_meta_tokens, v_mix_norm_g, v_ffn_norm_g, v_cp_conv_w, v_cp_conv_b, v_cp_ln_g, v_cp_ln_b, v_cp_pool_w,
               v_cp_pool_scale, v_gla_gate_w2, v_gla_gate_b, v_gla_head_g, v_final_norm_g]
    small_g = [g_meta, g_mix, g_ffn, g_conv_w, g_conv_b, g_ln_g, g_ln_b, g_pool_w, g_pool_scale, g_gate_w, g_gate_b,
               g_head, g_final]
    shapes = [w.shape for w in small_w]
    small_g = [g.reshape(s) for g, s in zip(small_g, shapes)]
    at_least_2d = lambda arrs: [a.reshape(1, -1) if a.ndim == 1 else a for a in arrs]
    s_delta, s_m, s_v = _adamw_many(at_least_2d(small_w), at_least_2d(small_g), at_least_2d(small_m), at_least_2d(small_v))
    s_delta, s_m, s_v = [[a.reshape(s) for a, s in zip(group, shapes)] for group in (s_delta, s_m, s_v)]

    order = ["meta", "mix", "ffn", "w1", "w2", "cp_in", "conv_w", "conv_b", "ln_g", "ln_b", "pool_w", "pool_scale",
             "cp_out", "gla_in", "gate_w", "gate_b", "head", "gla_out", "final"]
    small_names = ["meta", "mix", "ffn", "conv_w", "conv_b", "ln_g", "ln_b", "pool_w", "pool_scale", "gate_w", "gate_b",
                   "head", "final"]
    big_names = ["w1", "w2", "cp_in", "cp_out", "gla_in", "gla_out"]
    table = {n: (small_g[i], s_delta[i], s_m[i], s_v[i]) for i, n in enumerate(small_names)}
    table.update({n: tuple(big_out[n]) for n in big_names})
    loss = loss_sum.reshape(())
    return (loss, grad_x, *[table[n][0] for n in order], *[table[n][1] for n in order],
            *[table[n][2] for n in order], *[table[n][3] for n in order])
```

```python
import functools

import jax
import jax.numpy as jnp
from jax import lax
from jax.experimental import pallas as pl
from jax.experimental.pallas import tpu as pltpu

F32 = jnp.float32
BF16 = jnp.bfloat16

D_MODEL = 1024
N_META = 16
CHUNK = 64
PAD_ROWS = CHUNK - N_META
EPS = 1e-5
CONV_DIM = 512
CONV_WIDTH = 31
CONV_HALO = 32
POOL_DIM = 512
POOL_WINDOWS = (2, 4, 8, 16)
POOL_GROUP = 128
POOL_HALO = 16
CP_IN = 2 * CONV_DIM + POOL_DIM
GLA_HEADS = 4
GLA_DK = 512
GLA_DV = 1024
GLA_HK = GLA_DK // GLA_HEADS
GLA_HV = GLA_DV // GLA_HEADS
GATE_RANK = 16
GATE_PAD = 128
GATE_NORM = 16.0
GLA_IN = 2 * GLA_DK + 2 * GLA_DV + GATE_RANK
GLA_IN_PAD = 2 * GLA_DK + 2 * GLA_DV + GATE_PAD
N_CHIPS = 4
ADAM_LR = 0.001
ADAM_B1 = 0.9
ADAM_B2 = 0.999
ADAM_EPS = 1e-08
ADAM_WD = 0.01
ADAM_STEP = 10

VMEM_LIMIT_BYTES = 56 * 1024 * 1024
ROW_TILE_TARGET = 832
TOKEN_TILE_TARGET = 1040
PACK_WIDTH = 1024
MESH = pl.DeviceIdType.MESH
HBM_SPEC = pl.BlockSpec(memory_space=pltpu.HBM)
ANY_SPEC = pl.BlockSpec(memory_space=pl.ANY)
SEM_SPEC = pl.BlockSpec(memory_space=pltpu.SEMAPHORE)
SIDE_EFFECT = pltpu.SideEffectType.DATAFLOW_SIDE_EFFECTING


def _cparams(*sem):
    return pltpu.CompilerParams(dimension_semantics=sem, vmem_limit_bytes=VMEM_LIMIT_BYTES)


def _row_tile(t, target, mult):
    best = mult
    for cand in range(mult, min(t, target) + 1, mult):
        if t % cand == 0:
            best = cand
    assert t % best == 0, (t, best)
    return best


def _rms(h, g):
    return h * lax.rsqrt(jnp.mean(h * h, axis=-1, keepdims=True) + EPS) * g


def _rms_bwd(h, g, du):
    r = lax.rsqrt(jnp.mean(h * h, axis=-1, keepdims=True) + EPS)
    xhat = h * r
    dxh = du * g
    dh = r * (dxh - xhat * jnp.mean(dxh * xhat, axis=-1, keepdims=True))
    return dh, du * xhat


def _valid_rows(i, tm):
    row = i * tm + lax.broadcasted_iota(jnp.int32, (tm, 1), 0)
    return row >= PAD_ROWS


def _dot(a, b):
    return jnp.dot(a, b, preferred_element_type=F32)


def _dot_nt(a, b):
    return lax.dot_general(a, b, (((1,), (1,)), ((), ())), preferred_element_type=F32)


def _dot_tn(a, b):
    return lax.dot_general(a, b, (((0,), (0,)), ((), ())), preferred_element_type=F32)


def _accumulate(ref, val, first):
    @pl.when(first)
    def _():
        ref[...] = val

    @pl.when(jnp.logical_not(first))
    def _():
        ref[...] += val


def _call_after(dep, body, n_in, in_specs, args, **kw):
    if dep is None:
        return pl.pallas_call(body, in_specs=in_specs, **kw)(*args)

    def with_dep(*refs):
        body(*refs[:n_in], *refs[n_in + 1:])

    return pl.pallas_call(with_dep, in_specs=list(in_specs) + [ANY_SPEC], **kw)(*args, dep)


def _norm_matmul(h, g, w, nc, name, dep=None):
    t, d = h.shape
    n = w.shape[1]
    tm = _row_tile(t, TOKEN_TILE_TARGET, 16)

    def body(h_ref, g_ref, w_ref, z_ref, u_ref):
        u = _rms(h_ref[...], g_ref[...]).astype(BF16)
        u_ref[...] = u
        for n0 in range(0, n, nc):
            z_ref[:, n0:n0 + nc] = _dot(u, w_ref[:, n0:n0 + nc]).astype(BF16)

    return _call_after(
        dep, body, 3,
        [pl.BlockSpec((tm, d), lambda i: (i, 0)), pl.BlockSpec((1, d), lambda i: (0, 0)),
         pl.BlockSpec((d, n), lambda i: (0, 0))], (h, g, w), grid=(t // tm,),
        out_specs=[pl.BlockSpec((tm, n), lambda i: (i, 0)), pl.BlockSpec((tm, d), lambda i: (i, 0))],
        out_shape=[jax.ShapeDtypeStruct((t, n), BF16), jax.ShapeDtypeStruct((t, d), BF16)],
        compiler_params=_cparams("parallel"), name=name)


def _matmul_residual(a, w, h, name, dep=None):
    t, k = a.shape
    d = w.shape[1]
    tm = _row_tile(t, TOKEN_TILE_TARGET, 16)

    def body(a_ref, w_ref, h_ref, o_ref):
        o_ref[...] = h_ref[...] + _dot(a_ref[...], w_ref[...])

    return _call_after(
        dep, body, 3,
        [pl.BlockSpec((tm, k), lambda i: (i, 0)), pl.BlockSpec((k, d), lambda i: (0, 0)),
         pl.BlockSpec((tm, d), lambda i: (i, 0))], (a, w, h), grid=(t // tm,),
        out_specs=pl.BlockSpec((tm, d), lambda i: (i, 0)),
        out_shape=jax.ShapeDtypeStruct((t, d), F32),
        compiler_params=_cparams("parallel"), name=name)


def _ffn_fwd(h, g, w1g, w2g, name):
    t, d = h.shape
    ns, ffs = w1g.shape[0], w1g.shape[2]
    tm = _row_tile(t, TOKEN_TILE_TARGET, 16)

    def body(h_ref, g_ref, w1_ref, w2_ref, ho_ref, hp_ref, u_ref, acc_ref):
        s = pl.program_id(1)

        @pl.when(s == 0)
        def _():
            u_ref[...] = _rms(h_ref[...], g_ref[...]).astype(BF16)

        hp = _dot(u_ref[...], w1_ref[...])
        hp_ref[...] = hp.astype(BF16)
        a = jnp.maximum(hp, 0.0)
        _accumulate(acc_ref, _dot((a * a).astype(BF16), w2_ref[...]), s == 0)

        @pl.when(s == ns - 1)
        def _():
            ho_ref[...] = h_ref[...] + acc_ref[...]

    return pl.pallas_call(
        body, grid=(t // tm, ns),
        in_specs=[pl.BlockSpec((tm, d), lambda i, s: (i, 0)), pl.BlockSpec((1, d), lambda i, s: (0, 0)),
                  pl.BlockSpec((None, d, ffs), lambda i, s: (s, 0, 0)),
                  pl.BlockSpec((None, ffs, d), lambda i, s: (s, 0, 0))],
        out_specs=[pl.BlockSpec((tm, d), lambda i, s: (i, 0)), pl.BlockSpec((tm, ffs), lambda i, s: (i, s)),
                   pl.BlockSpec((tm, d), lambda i, s: (i, 0))],
        out_shape=[jax.ShapeDtypeStruct((t, d), F32), jax.ShapeDtypeStruct((t, ns * ffs), BF16),
                   jax.ShapeDtypeStruct((t, d), BF16)],
        scratch_shapes=[pltpu.VMEM((tm, d), F32)],
        compiler_params=_cparams("parallel", "arbitrary"), name=name)(h, g, w1g, w2g)


def _ffn_bwd_data(dh, h, g, hp, w1g, w2g, name, dep=None):
    t, d = h.shape
    ns, ffs = w1g.shape[0], w1g.shape[2]
    tm = _row_tile(t, ROW_TILE_TARGET, CHUNK)

    def body(dh_ref, h_ref, g_ref, hp_ref, w1_ref, w2_ref, dhi_ref, dhp_ref, dg_ref, acc_ref):
        i, s = pl.program_id(0), pl.program_id(1)
        da = _dot_nt(dh_ref[...].astype(BF16), w2_ref[...])
        dhp = (da * (2.0 * jnp.maximum(hp_ref[...].astype(F32), 0.0))).astype(BF16)
        dhp_ref[...] = dhp
        _accumulate(acc_ref, _dot_nt(dhp, w1_ref[...]), s == 0)

        @pl.when(s == ns - 1)
        def _():
            dhn, dgr = _rms_bwd(h_ref[...], g_ref[...], acc_ref[...])
            dhi_ref[...] = jnp.where(_valid_rows(i, tm), dh_ref[...] + dhn, 0.0)
            _accumulate(dg_ref, jnp.sum(dgr, axis=0, keepdims=True), i == 0)

    return _call_after(
        dep, body, 6,
        [pl.BlockSpec((tm, d), lambda i, s: (i, 0)), pl.BlockSpec((tm, d), lambda i, s: (i, 0)),
         pl.BlockSpec((1, d), lambda i, s: (0, 0)), pl.BlockSpec((tm, ffs), lambda i, s: (i, s)),
         pl.BlockSpec((None, d, ffs), lambda i, s: (s, 0, 0)),
         pl.BlockSpec((None, ffs, d), lambda i, s: (s, 0, 0))], (dh, h, g, hp, w1g, w2g), grid=(t // tm, ns),
        out_specs=[pl.BlockSpec((tm, d), lambda i, s: (i, 0)), pl.BlockSpec((tm, ffs), lambda i, s: (i, s)),
                   pl.BlockSpec((1, d), lambda i, s: (0, 0))],
        out_shape=[jax.ShapeDtypeStruct((t, d), F32), jax.ShapeDtypeStruct((t, ns * ffs), BF16),
                   jax.ShapeDtypeStruct((1, d), F32)],
        scratch_shapes=[pltpu.VMEM((tm, d), F32)],
        compiler_params=_cparams("arbitrary", "arbitrary"), name=name)


WGRAD_ROWS = 1024


def _wgrad(x, dy, nb, xc, yc, x_by_block, dy_by_block, relu2, name, dep=None):
    t = x.shape[0]
    tk = _row_tile(t - CHUNK, WGRAD_ROWS, CHUNK)

    def prep(xv):
        if relu2:
            xv = jnp.maximum(xv.astype(F32), 0.0)
            xv = xv * xv
        return xv.astype(BF16)

    nk = (t - CHUNK) // tk

    def body(xh_ref, dyh_ref, x_ref, dy_ref, o_ref, acc_ref):
        k = pl.program_id(1)
        p = _dot_tn(prep(x_ref[...]), dy_ref[...].astype(BF16))

        @pl.when(k == 0)
        def _():
            acc_ref[...] = p + _dot_tn(prep(xh_ref[...]), dyh_ref[...].astype(BF16))

        @pl.when(k > 0)
        def _():
            acc_ref[...] += p

        @pl.when(k == nk - 1)
        def _():
            o_ref[...] = acc_ref[...].astype(BF16)

    def head(width, by_block):
        return pl.BlockSpec((CHUNK, width), (lambda b, k: (0, b)) if by_block else (lambda b, k: (0, 0)))

    def rest(width, by_block):
        def index(b, k):
            return pl.multiple_of(CHUNK + k * tk, CHUNK), (pl.multiple_of(b * width, 128) if by_block else 0)
        return pl.BlockSpec((pl.Element(tk), pl.Element(width)), index)

    return _call_after(
        dep, body, 4,
        [head(xc, x_by_block), head(yc, dy_by_block), rest(xc, x_by_block), rest(yc, dy_by_block)], (x, dy, x, dy),
        grid=(nb, nk),
        out_specs=pl.BlockSpec((None, xc, yc), lambda b, k: (b, 0, 0)),
        out_shape=jax.ShapeDtypeStruct((nb, xc, yc), BF16),
        scratch_shapes=[pltpu.VMEM((xc, yc), F32)],
        compiler_params=_cparams("parallel", "arbitrary"), name=name)


def _dgrad(dh, w, name, dep=None):
    t, d = dh.shape
    k = w.shape[0]
    tm = _row_tile(t, TOKEN_TILE_TARGET, 16)

    def body(dh_ref, w_ref, o_ref):
        o_ref[...] = _dot_nt(dh_ref[...].astype(BF16), w_ref[...]).astype(BF16)

    return _call_after(
        dep, body, 2,
        [pl.BlockSpec((tm, d), lambda i: (i, 0)), pl.BlockSpec((k, d), lambda i: (0, 0))], (dh, w), grid=(t // tm,),
        out_specs=pl.BlockSpec((tm, k), lambda i: (i, 0)),
        out_shape=jax.ShapeDtypeStruct((t, k), BF16),
        compiler_params=_cparams("parallel"), name=name)


def _dgrad_norm_bwd(dz, w, h, g, dh, nc, name):
    t, d = h.shape
    n = w.shape[1]
    tm = _row_tile(t, ROW_TILE_TARGET // 2, 16)

    def body(dz_ref, w_ref, h_ref, g_ref, dh_ref, dhi_ref, dg_ref):
        i = pl.program_id(0)
        du = jnp.zeros((tm, d), F32)
        for n0 in range(0, n, nc):
            du = du + _dot_nt(dz_ref[:, n0:n0 + nc], w_ref[:, n0:n0 + nc])
        dhn, dgr = _rms_bwd(h_ref[...], g_ref[...], du)
        dhi_ref[...] = jnp.where(_valid_rows(i, tm), dh_ref[...] + dhn, 0.0)
        _accumulate(dg_ref, jnp.sum(dgr, axis=0, keepdims=True), i == 0)

    return pl.pallas_call(
        body, grid=(t // tm,),
        in_specs=[pl.BlockSpec((tm, n), lambda i: (i, 0)), pl.BlockSpec((d, n), lambda i: (0, 0)),
                  pl.BlockSpec((tm, d), lambda i: (i, 0)), pl.BlockSpec((1, d), lambda i: (0, 0)),
                  pl.BlockSpec((tm, d), lambda i: (i, 0))],
        out_specs=[pl.BlockSpec((tm, d), lambda i: (i, 0)), pl.BlockSpec((1, d), lambda i: (0, 0))],
        out_shape=[jax.ShapeDtypeStruct((t, d), F32), jax.ShapeDtypeStruct((1, d), F32)],
        compiler_params=_cparams("arbitrary"), name=name)(dz, w, h, g, dh)


def _dgrad_norm_bwd_input(dz, w, h, g, dh, nc, name):
    t, d = h.shape
    n = w.shape[1]
    tl = _row_tile(t - CHUNK, 512, CHUNK)

    def grads(dz_ref, w_ref, h_ref, g_ref, dh_ref, rows):
        du = jnp.zeros((rows, d), F32)
        for n0 in range(0, n, nc):
            du = du + _dot_nt(dz_ref[:, n0:n0 + nc], w_ref[:, n0:n0 + nc])
        dhn, dgr = _rms_bwd(h_ref[...], g_ref[...], du)
        return dh_ref[...] + dhn, jnp.sum(dgr, axis=0, keepdims=True)

    def rest_body(dz_ref, w_ref, h_ref, g_ref, dh_ref, dg_head_ref, dx_ref, dg_ref):
        dx, dg = grads(dz_ref, w_ref, h_ref, g_ref, dh_ref, tl)
        dx_ref[...] = dx

        @pl.when(pl.program_id(0) == 0)
        def _():
            dg_ref[...] = dg_head_ref[...] + dg

        @pl.when(pl.program_id(0) > 0)
        def _():
            dg_ref[...] += dg

    def head_body(dz_ref, w_ref, h_ref, g_ref, dh_ref, dx_ref, dg_ref):
        dx, dg = grads(dz_ref, w_ref, h_ref, g_ref, dh_ref, CHUNK)
        dx_ref[...] = jnp.where(_valid_rows(0, CHUNK), dx, 0.0)
        dg_ref[...] = dg

    def shifted(width):
        return pl.BlockSpec((pl.Element(tl), pl.Element(width)), lambda i: (pl.multiple_of(CHUNK + i * tl, CHUNK), 0))

    whole = [pl.BlockSpec((d, n), lambda i: (0, 0)), pl.BlockSpec((1, d), lambda i: (0, 0))]
    head = lambda width: pl.BlockSpec((CHUNK, width), lambda i: (0, 0))
    dh_head, dg_head = pl.pallas_call(
        head_body, grid=(1,), in_specs=[head(n), whole[0], head(d), whole[1], head(d)],
        out_specs=[head(d), whole[1]],
        out_shape=[jax.ShapeDtypeStruct((CHUNK, d), F32), jax.ShapeDtypeStruct((1, d), F32)],
        compiler_params=_cparams("arbitrary"), name=name + "_head")(dz, w, h, g, dh)
    dx, dg = pl.pallas_call(
        rest_body, grid=((t - CHUNK) // tl,),
        in_specs=[shifted(n), whole[0], shifted(d), whole[1], shifted(d), whole[1]],
        out_specs=[pl.BlockSpec((tl, d), lambda i: (i, 0)), whole[1]],
        out_shape=[jax.ShapeDtypeStruct((t - CHUNK, d), F32), jax.ShapeDtypeStruct((1, d), F32)],
        compiler_params=_cparams("arbitrary"), name=name)(dz, w, h, g, dh, dg_head)
    return dx, dh_head, dg


def _loss_bwd(h, g, target):
    t, d = h.shape
    tl = _row_tile(t - CHUNK, 1024, CHUNK)

    def body(h_ref, g_ref, t_ref, dh_ref, dg_ref, loss_ref):
        i = pl.program_id(0)
        hv, gv = h_ref[...], g_ref[...]
        err = _rms(hv, gv) - t_ref[...]
        part = 0.5 * jnp.sum(jnp.mean(err * err, axis=-1, keepdims=True), axis=0, keepdims=True)
        dhn, dgr = _rms_bwd(hv, gv, err * (1.0 / d))
        dh_ref[...] = dhn
        _accumulate(dg_ref, jnp.sum(dgr, axis=0, keepdims=True), i == 0)
        _accumulate(loss_ref, jnp.broadcast_to(part, (8, 128)), i == 0)

    shifted = pl.BlockSpec((pl.Element(tl), pl.Element(d)), lambda i: (pl.multiple_of(CHUNK + i * tl, CHUNK), 0))
    dh, dg, loss = pl.pallas_call(
        body, grid=((t - CHUNK) // tl,),
        in_specs=[shifted, pl.BlockSpec((1, d), lambda i: (0, 0)), pl.BlockSpec((tl, d), lambda i: (i, 0))],
        out_specs=[shifted, pl.BlockSpec((1, d), lambda i: (0, 0)), pl.BlockSpec((8, 128), lambda i: (0, 0))],
        out_shape=[jax.ShapeDtypeStruct((t, d), F32), jax.ShapeDtypeStruct((1, d), F32),
                   jax.ShapeDtypeStruct((8, 128), F32)],
        compiler_params=_cparams("arbitrary"), name="loss_bwd")(h, g, target)

    def zero_head(dh_ref, o_ref):
        o_ref[...] = jnp.zeros_like(o_ref)

    dh = pl.pallas_call(
        zero_head, grid=(1,), in_specs=[ANY_SPEC], out_specs=pl.BlockSpec((CHUNK, d), lambda i: (0, 0)),
        out_shape=jax.ShapeDtypeStruct((t, d), F32), input_output_aliases={0: 0}, name="loss_bwd_head")(dh)
    return dh, dg, loss


CONV_BLOCK = 32


def _silu(x):
    return x * jax.nn.sigmoid(x)


def _row_shifts(win):
    n = win.shape[0]
    return [win] + [pltpu.roll(win, n - j, 0) for j in range(1, 8)]


def _cp_seq_fwd(z, conv_w, conv_b, ln_g, ln_b, pool_w, pool_scale):
    t = z.shape[0]
    tm = _row_tile(t, ROW_TILE_TARGET, CHUNK)

    def body(z_ref, cw_ref, cb_ref, lg_ref, lb_ref, pw_ref, ps_ref, c_ref, pm_ref, mix_ref, gbuf, pbuf):
        i = pl.program_id(0)

        @pl.when(i == 0)
        def _():
            gbuf[0:CONV_HALO, :] = jnp.zeros((CONV_HALO, CONV_DIM), F32)
            pbuf[0:POOL_HALO, :] = jnp.zeros((POOL_HALO, POOL_DIM), F32)

        @pl.when(i > 0)
        def _():
            gbuf[0:CONV_HALO, :] = gbuf[tm:tm + CONV_HALO, :]
            pbuf[0:POOL_HALO, :] = pbuf[tm:tm + POOL_HALO, :]

        av = z_ref[:, 0:CONV_DIM].astype(F32)
        ag = z_ref[:, CONV_DIM:2 * CONV_DIM].astype(F32)
        gbuf[CONV_HALO:CONV_HALO + tm, :] = av * jax.nn.sigmoid(ag)
        pbuf[POOL_HALO:POOL_HALO + tm, :] = z_ref[:, 2 * CONV_DIM:CP_IN].astype(F32)

        def conv_block(rb, carry):
            base = pl.multiple_of(rb * CONV_BLOCK, CONV_BLOCK)
            shifted = _row_shifts(gbuf[pl.ds(base, CONV_BLOCK + CONV_HALO), :])
            acc = jnp.zeros((CONV_BLOCK, CONV_DIM), F32)
            for k in range(CONV_WIDTH):
                whole, part = divmod(CONV_HALO - (CONV_WIDTH - 1) + k, 8)
                acc = acc + cw_ref[k:k + 1, :] * shifted[part][8 * whole:8 * whole + CONV_BLOCK, :]
            c_ref[pl.ds(base, CONV_BLOCK), :] = acc + cb_ref[...]
            return carry

        lax.fori_loop(0, tm // CONV_BLOCK, conv_block, 0)

        c = c_ref[...]
        mu = jnp.mean(c, axis=-1, keepdims=True)
        xc = c - mu
        ln = xc * lax.rsqrt(jnp.mean(xc * xc, axis=-1, keepdims=True) + EPS) * lg_ref[...] + lb_ref[...]
        row = i * tm + lax.broadcasted_iota(jnp.int32, (tm, 1), 0)
        mix_ref[:, 0:CONV_DIM] = jnp.where(row >= PAD_ROWS, _silu(ln), 0.0).astype(BF16)

        tpos = (row - PAD_ROWS + 1).astype(F32)
        for gi, wdw in enumerate(POOL_WINDOWS):
            lo = POOL_GROUP * gi
            cur = pbuf[POOL_HALO:POOL_HALO + tm, lo:lo + POOL_GROUP]
            sacc = cur
            for j in range(1, wdw):
                sacc = sacc + pbuf[POOL_HALO - j:POOL_HALO - j + tm, lo:lo + POOL_GROUP]
            pm = (sacc / jnp.clip(tpos, 1.0, float(wdw)) - cur).astype(BF16)
            pm_ref[:, lo:lo + POOL_GROUP] = pm
            pg = _dot(pm, pw_ref[gi].astype(BF16))
            mix_ref[:, CONV_DIM + lo:CONV_DIM + lo + POOL_GROUP] = (pg * ps_ref[:, lo:lo + POOL_GROUP]).astype(BF16)

    vec = pl.BlockSpec((1, CONV_DIM), lambda i: (0, 0))
    return pl.pallas_call(
        body, grid=(t // tm,),
        in_specs=[pl.BlockSpec((tm, CP_IN), lambda i: (i, 0)),
                  pl.BlockSpec((CONV_WIDTH, CONV_DIM), lambda i: (0, 0)), vec, vec, vec,
                  pl.BlockSpec((len(POOL_WINDOWS), POOL_GROUP, POOL_GROUP), lambda i: (0, 0, 0)), vec],
        out_specs=[pl.BlockSpec((tm, CONV_DIM), lambda i: (i, 0)), pl.BlockSpec((tm, POOL_DIM), lambda i: (i, 0)),
                   pl.BlockSpec((tm, CONV_DIM + POOL_DIM), lambda i: (i, 0))],
        out_shape=[jax.ShapeDtypeStruct((t, CONV_DIM), F32), jax.ShapeDtypeStruct((t, POOL_DIM), BF16),
                   jax.ShapeDtypeStruct((t, CONV_DIM + POOL_DIM), BF16)],
        scratch_shapes=[pltpu.VMEM((tm + CONV_HALO, CONV_DIM), F32), pltpu.VMEM((tm + POOL_HALO, POOL_DIM), F32)],
        compiler_params=_cparams("arbitrary"), name="cp_seq_fwd")(z, conv_w, conv_b, ln_g, ln_b, pool_w, pool_scale)


def _cp_seq_bwd(dmix, z, c, pm, conv_w, ln_g, ln_b, pool_w, pool_scale, dep=None):
    t = z.shape[0]
    tm = _row_tile(t, ROW_TILE_TARGET, CHUNK)
    nt = t // tm

    def body(dmix_ref, z_ref, c_ref, pm_ref, cw_ref, lg_ref, lb_ref, pw_ref, ps_ref,
             dz_ref, dcw_ref, dvec_ref, dpw_ref, dcbuf, qbuf, glu_buf, dwacc):
        i = pl.program_id(0)
        tile = nt - 1 - i

        @pl.when(i == 0)
        def _():
            dcbuf[tm:tm + CONV_HALO, :] = jnp.zeros((CONV_HALO, CONV_DIM), F32)
            qbuf[tm:tm + POOL_HALO, :] = jnp.zeros((POOL_HALO, POOL_DIM), F32)
            dcw_ref[...] = jnp.zeros_like(dcw_ref)
            dwacc[...] = jnp.zeros_like(dwacc)
            dvec_ref[...] = jnp.zeros_like(dvec_ref)
            dpw_ref[...] = jnp.zeros_like(dpw_ref)

        @pl.when(i > 0)
        def _():
            dcbuf[tm:tm + CONV_HALO, :] = dcbuf[0:CONV_HALO, :]
            qbuf[tm:tm + POOL_HALO, :] = qbuf[0:POOL_HALO, :]

        row = tile * tm + lax.broadcasted_iota(jnp.int32, (tm, 1), 0)
        cv = c_ref[...]
        mu = jnp.mean(cv, axis=-1, keepdims=True)
        xc = cv - mu
        rstd = lax.rsqrt(jnp.mean(xc * xc, axis=-1, keepdims=True) + EPS)
        xhat = xc * rstd
        ln = xhat * lg_ref[...] + lb_ref[...]
        sg = jax.nn.sigmoid(ln)
        da = jnp.where(row >= PAD_ROWS, dmix_ref[:, 0:CONV_DIM].astype(F32), 0.0)
        dln = da * (sg * (1.0 + ln * (1.0 - sg)))
        dxh = dln * lg_ref[...]
        dc = rstd * (dxh - jnp.mean(dxh, axis=-1, keepdims=True) - xhat * jnp.mean(dxh * xhat, axis=-1, keepdims=True))
        dcbuf[0:tm, :] = dc
        dvec_ref[0:1, :] += jnp.sum(dc, axis=0, keepdims=True)
        dvec_ref[1:2, :] += jnp.sum(dln * xhat, axis=0, keepdims=True)
        dvec_ref[2:3, :] += jnp.sum(dln, axis=0, keepdims=True)

        av = z_ref[:, 0:CONV_DIM].astype(F32)
        sig_g = jax.nn.sigmoid(z_ref[:, CONV_DIM:2 * CONV_DIM].astype(F32))
        glu_buf[...] = av * sig_g

        def conv_block(rb, carry):
            base = pl.multiple_of(rb * CONV_BLOCK, CONV_BLOCK)
            shifted = _row_shifts(dcbuf[pl.ds(base, CONV_BLOCK + CONV_HALO), :])
            glu = glu_buf[pl.ds(base, CONV_BLOCK), :]
            acc = jnp.zeros((CONV_BLOCK, CONV_DIM), F32)
            for k in range(CONV_WIDTH):
                whole, part = divmod(CONV_WIDTH - 1 - k, 8)
                slab = shifted[part][8 * whole:8 * whole + CONV_BLOCK, :]
                acc = acc + cw_ref[k:k + 1, :] * slab
                prod = slab * glu
                part = prod[0:8]
                for q in range(1, CONV_BLOCK // 8):
                    part = part + prod[8 * q:8 * q + 8]
                dwacc[k] += part
            glu_buf[pl.ds(base, CONV_BLOCK), :] = acc
            return carry

        lax.fori_loop(0, tm // CONV_BLOCK, conv_block, 0)

        @pl.when(i == nt - 1)
        def _():
            for k in range(CONV_WIDTH):
                dcw_ref[k:k + 1, :] = jnp.sum(dwacc[k], axis=0, keepdims=True)
        dglu = glu_buf[...]
        dz_ref[:, 0:CONV_DIM] = (dglu * sig_g).astype(BF16)
        dz_ref[:, CONV_DIM:2 * CONV_DIM] = (dglu * av * sig_g * (1.0 - sig_g)).astype(BF16)

        tpos = (row - PAD_ROWS + 1).astype(F32)
        for gi, wdw in enumerate(POOL_WINDOWS):
            lo = POOL_GROUP * gi
            dp = dmix_ref[:, CONV_DIM + lo:CONV_DIM + lo + POOL_GROUP].astype(F32)
            pmv = pm_ref[:, lo:lo + POOL_GROUP]
            pwb = pw_ref[gi].astype(BF16)
            dvec_ref[3:4, lo:lo + POOL_GROUP] += jnp.sum(dp * _dot(pmv, pwb), axis=0, keepdims=True)
            dq = (dp * ps_ref[:, lo:lo + POOL_GROUP]).astype(BF16)
            dpw_ref[gi] += _dot_tn(pmv, dq)
            dpm = _dot_nt(dq, pwb)
            qbuf[0:tm, lo:lo + POOL_GROUP] = dpm / jnp.clip(tpos, 1.0, float(wdw))
            sacc = -dpm
            for j in range(wdw):
                sacc = sacc + qbuf[j:j + tm, lo:lo + POOL_GROUP]
            dz_ref[:, 2 * CONV_DIM + lo:2 * CONV_DIM + lo + POOL_GROUP] = sacc.astype(BF16)

    vec = pl.BlockSpec((1, CONV_DIM), lambda i: (0, 0))
    rev = lambda i: (nt - 1 - i, 0)
    return _call_after(
        dep, body, 9,
        [pl.BlockSpec((tm, CONV_DIM + POOL_DIM), rev), pl.BlockSpec((tm, CP_IN), rev),
         pl.BlockSpec((tm, CONV_DIM), rev), pl.BlockSpec((tm, POOL_DIM), rev),
         pl.BlockSpec((CONV_WIDTH, CONV_DIM), lambda i: (0, 0)), vec, vec,
         pl.BlockSpec((len(POOL_WINDOWS), POOL_GROUP, POOL_GROUP), lambda i: (0, 0, 0)), vec],
        (dmix, z, c, pm, conv_w, ln_g, ln_b, pool_w, pool_scale), grid=(nt,),
        out_specs=[pl.BlockSpec((tm, CP_IN), rev), pl.BlockSpec((CONV_WIDTH + 1, CONV_DIM), lambda i: (0, 0)),
                   pl.BlockSpec((8, CONV_DIM), lambda i: (0, 0)),
                   pl.BlockSpec((len(POOL_WINDOWS), POOL_GROUP, POOL_GROUP), lambda i: (0, 0, 0))],
        out_shape=[jax.ShapeDtypeStruct((t, CP_IN), BF16), jax.ShapeDtypeStruct((CONV_WIDTH + 1, CONV_DIM), F32),
                   jax.ShapeDtypeStruct((8, CONV_DIM), F32),
                   jax.ShapeDtypeStruct((len(POOL_WINDOWS), POOL_GROUP, POOL_GROUP), F32)],
        scratch_shapes=[pltpu.VMEM((tm + CONV_HALO, CONV_DIM), F32), pltpu.VMEM((tm + POOL_HALO, POOL_DIM), F32),
                        pltpu.VMEM((tm, CONV_DIM), F32), pltpu.VMEM((CONV_WIDTH + 1, 8, CONV_DIM), F32)],
        compiler_params=_cparams("arbitrary"), name="cp_seq_bwd")


GLA_UNROLL = 2
Q0, K0, V0, G0, R0 =0, GLA_DK, 2 * GLA_DK, 2 * GLA_DK + GLA_DV, 2 * GLA_DK + 2 * GLA_DV


def _split3(x):
    hi = x.astype(BF16)
    r1 = x - hi.astype(F32)
    mid = r1.astype(BF16)
    lo = (r1 - mid.astype(F32)).astype(BF16)
    return hi, mid, lo


def _tri(strict):
    r = lax.broadcasted_iota(jnp.int32, (CHUNK, CHUNK), 0)
    c = lax.broadcasted_iota(jnp.int32, (CHUNK, CHUNK), 1)
    return ((r > c) if strict else (r >= c)).astype(BF16)


def _chunk_sums(x, cpt, strict, pieces):
    tri3 = jnp.broadcast_to(_tri(strict)[None], (cpt, CHUNK, CHUNK))
    acc = None
    for piece in _split3(x.reshape(cpt, CHUNK, x.shape[-1]))[:pieces]:
        part = jnp.einsum("bij,bjk->bik", tri3, piece, preferred_element_type=F32)
        acc = part if acc is None else acc + part
    return acc


def _chunk_decay(r, gw_ref, gb_ref, cpt):
    pre = _dot(r, gw_ref[...]) + gb_ref[...]
    lac = (jnp.minimum(pre, 0.0) - jnp.log(1.0 + jnp.exp(-jnp.abs(pre)))) * (1.0 / GATE_NORM)
    cum3 = _chunk_sums(lac, cpt, False, 3)
    return cum3, cum3[:, CHUNK - 1:CHUNK, :]


def _gla_seq_fwd(z, gate_w, gate_b, head_g, dep=None):
    t = z.shape[0]
    tm = _row_tile(t, ROW_TILE_TARGET, CHUNK)
    cpt = tm // CHUNK
    scale = GLA_HK ** -0.5

    def body(z_ref, gw_ref, gb_ref, hg_ref, o_ref, mix_ref, st_ref, state, kdec_s, e_s):
        @pl.when(pl.program_id(0) == 0)
        def _():
            state[...] = jnp.zeros_like(state)

        cum3, tot3 = _chunk_decay(z_ref[:, R0:R0 + GATE_PAD], gw_ref, gb_ref, cpt)
        dec = jnp.exp(jnp.broadcast_to(tot3, cum3.shape) - cum3).reshape(tm, GLA_DK)
        kdec_s[...] = (z_ref[:, K0:K0 + GLA_DK].astype(F32) * dec).astype(BF16)
        e_s[...] = jnp.exp(jnp.broadcast_to(tot3, (cpt, 8, GLA_DK))).reshape(cpt * 8, GLA_DK)

        def chunk(ci, carry):
            rows = pl.ds(pl.multiple_of(ci * CHUNK, CHUNK), CHUNK)
            e_all = e_s[pl.ds(pl.multiple_of(ci * 8, 8), 8), :][0:1, :]
            st_ref[ci] = state[...].astype(BF16)
            for hd in range(GLA_HEADS):
                ks = slice(hd * GLA_HK, (hd + 1) * GLA_HK)
                vs = slice(hd * GLA_HV, (hd + 1) * GLA_HV)
                v = z_ref[rows, V0 + hd * GLA_HV:V0 + (hd + 1) * GLA_HV]
                st = state[vs, :] * e_all[:, ks] + _dot_tn(v, kdec_s[rows, ks])
                state[vs, :] = st
                q = z_ref[rows, Q0 + hd * GLA_HK:Q0 + (hd + 1) * GLA_HK]
                o_ref[rows, vs] = (_dot_nt(q, st.astype(BF16)) * scale).astype(BF16)
            return carry

        lax.fori_loop(0, cpt, chunk, 0, unroll=GLA_UNROLL)

        for hd in range(GLA_HEADS):
            vs = slice(hd * GLA_HV, (hd + 1) * GLA_HV)
            on = _rms(o_ref[:, vs].astype(F32), hg_ref[...])
            gv = z_ref[:, G0 + hd * GLA_HV:G0 + (hd + 1) * GLA_HV].astype(F32)
            mix_ref[:, vs] = (on * _silu(gv)).astype(BF16)

    return _call_after(
        dep, body, 4,
        [pl.BlockSpec((tm, GLA_IN_PAD), lambda i: (i, 0)),
         pl.BlockSpec((GATE_PAD, GLA_DK), lambda i: (0, 0)), pl.BlockSpec((1, GLA_DK), lambda i: (0, 0)),
         pl.BlockSpec((1, GLA_HV), lambda i: (0, 0))], (z, gate_w, gate_b, head_g), grid=(t // tm,),
        out_specs=[pl.BlockSpec((tm, GLA_DV), lambda i: (i, 0)), pl.BlockSpec((tm, GLA_DV), lambda i: (i, 0)),
                   pl.BlockSpec((cpt, GLA_DV, GLA_HK), lambda i: (i, 0, 0))],
        out_shape=[jax.ShapeDtypeStruct((t, GLA_DV), BF16), jax.ShapeDtypeStruct((t, GLA_DV), BF16),
                   jax.ShapeDtypeStruct((t // CHUNK, GLA_DV, GLA_HK), BF16)],
        scratch_shapes=[pltpu.VMEM((GLA_DV, GLA_HK), F32), pltpu.VMEM((tm, GLA_DK), BF16),
                        pltpu.VMEM((cpt * 8, GLA_DK), F32)],
        compiler_params=_cparams("arbitrary"), name="gla_seq_fwd")


def _gla_seq_bwd(dmix, o, z, states, gate_w, gate_b, head_g, dep=None):
    t = z.shape[0]
    tm = _row_tile(t, ROW_TILE_TARGET, CHUNK)
    cpt = tm // CHUNK
    nt = t // tm
    scale = GLA_HK ** -0.5

    def body(dmix_ref, o_ref, z_ref, st_ref, gw_ref, gb_ref, hg_ref, dz_ref, dgw_ref, dgb_ref, dhg_ref,
             dstate, dec_s, kdec_s, dkdec_s, do_s, e_s, dtot_s):
        @pl.when(pl.program_id(0) == 0)
        def _():
            dstate[...] = jnp.zeros_like(dstate)
            dgw_ref[...] = jnp.zeros_like(dgw_ref)
            dgb_ref[...] = jnp.zeros_like(dgb_ref)
            dhg_ref[...] = jnp.zeros_like(dhg_ref)

        cum3, tot3 = _chunk_decay(z_ref[:, R0:R0 + GATE_PAD], gw_ref, gb_ref, cpt)
        dec = jnp.exp(jnp.broadcast_to(tot3, cum3.shape) - cum3).reshape(tm, GLA_DK)
        dec_s[...] = dec
        kdec = z_ref[:, K0:K0 + GLA_DK].astype(F32) * dec
        kdec_s[...] = kdec
        e3 = jnp.exp(tot3)
        e_s[...] = jnp.broadcast_to(e3, (cpt, 8, GLA_DK)).reshape(cpt * 8, GLA_DK)
        dhg = jnp.zeros((1, GLA_HV), F32)
        for hd in range(GLA_HEADS):
            ks = slice(hd * GLA_HK, (hd + 1) * GLA_HK)
            vs = slice(hd * GLA_HV, (hd + 1) * GLA_HV)
            gcols = slice(G0 + hd * GLA_HV, G0 + (hd + 1) * GLA_HV)
            ov = o_ref[:, vs].astype(F32)
            gv = z_ref[:, gcols].astype(F32)
            dm = dmix_ref[:, vs].astype(F32)
            sg = jax.nn.sigmoid(gv)
            rr = lax.rsqrt(jnp.mean(ov * ov, axis=-1, keepdims=True) + EPS)
            xhat = ov * rr
            don = dm * (gv * sg)
            dz_ref[:, gcols] = (dm * (xhat * hg_ref[...]) * (sg * (1.0 + gv * (1.0 - sg)))).astype(BF16)
            dhg = dhg + jnp.sum(don * xhat, axis=0, keepdims=True)
            dxh = don * hg_ref[...]
            do = (rr * (dxh - xhat * jnp.mean(dxh * xhat, axis=-1, keepdims=True)) * scale).astype(BF16)
            do_s[:, vs] = do
            v3 = z_ref[:, V0 + hd * GLA_HV:V0 + (hd + 1) * GLA_HV].reshape(cpt, CHUNK, GLA_HV)
            kdb3 = kdec[:, ks].astype(BF16).reshape(cpt, CHUNK, GLA_HK)
            st3 = st_ref[:, vs, :].astype(F32) * e3[:, :, ks] + jnp.einsum("bcv,bck->bvk", v3, kdb3,
                                                                            preferred_element_type=F32)
            dq3 = jnp.einsum("bcv,bvk->bck", do.reshape(cpt, CHUNK, GLA_HV), st3.astype(BF16), preferred_element_type=F32)
            dz_ref[:, Q0 + hd * GLA_HK:Q0 + (hd + 1) * GLA_HK] = dq3.reshape(tm, GLA_HK).astype(BF16)
        dhg_ref[...] += dhg

        def chunk(cj, carry):
            ci = cpt - 1 - cj
            rows = pl.ds(pl.multiple_of(ci * CHUNK, CHUNK), CHUNK)
            erows = pl.ds(pl.multiple_of(ci * 8, 8), 8)
            e_all = e_s[erows, :][0:1, :]
            for hd in range(GLA_HEADS):
                ks = slice(hd * GLA_HK, (hd + 1) * GLA_HK)
                vs = slice(hd * GLA_HV, (hd + 1) * GLA_HV)
                e = e_all[:, ks]
                kdb = kdec_s[rows, ks].astype(BF16)
                v = z_ref[rows, V0 + hd * GLA_HV:V0 + (hd + 1) * GLA_HV]
                q = z_ref[rows, Q0 + hd * GLA_HK:Q0 + (hd + 1) * GLA_HK]
                do = do_s[rows, vs]
                st_prev = st_ref[ci, vs, :].astype(F32)
                dst = dstate[vs, :] + _dot_tn(do, q)
                dstb = dst.astype(BF16)
                dkdec_s[rows, ks] = _dot(v, dstb)
                dz_ref[rows, V0 + hd * GLA_HV:V0 + (hd + 1) * GLA_HV] = _dot_nt(kdb, dstb).astype(BF16)
                dtot = jnp.sum(dst * st_prev, axis=0, keepdims=True) * e
                dtot_s[erows, ks] = jnp.broadcast_to(dtot, (8, GLA_HK))
                dstate[vs, :] = dst * e
            return carry

        lax.fori_loop(0, cpt, chunk, 0, unroll=GLA_UNROLL)

        dkdec = dkdec_s[...]
        dz_ref[:, K0:K0 + GLA_DK] = (dkdec * dec_s[...]).astype(BF16)
        before = _chunk_sums(dkdec * kdec_s[...], cpt, True, 2)
        dtot3 = dtot_s[...].reshape(cpt, 8, GLA_DK)[:, 0:1, :]
        dlac = (jnp.broadcast_to(dtot3, before.shape) + before).reshape(tm, GLA_DK)
        pre = _dot(z_ref[:, R0:R0 + GATE_PAD], gw_ref[...]) + gb_ref[...]
        dpre = dlac * (1.0 / GATE_NORM) * (1.0 - jax.nn.sigmoid(pre))
        dpb = dpre.astype(BF16)
        dz_ref[:, R0:R0 + GATE_PAD] = _dot_nt(dpb, gw_ref[...]).astype(BF16)
        dgw_ref[...] += _dot_tn(z_ref[:, R0:R0 + GATE_PAD], dpb)
        dgb_ref[...] += jnp.sum(dpre, axis=0, keepdims=True)

    rev = lambda i: (nt - 1 - i, 0)
    return _call_after(
        dep, body, 7,
        [pl.BlockSpec((tm, GLA_DV), rev), pl.BlockSpec((tm, GLA_DV), rev), pl.BlockSpec((tm, GLA_IN_PAD), rev),
         pl.BlockSpec((cpt, GLA_DV, GLA_HK), lambda i: (nt - 1 - i, 0, 0)),
         pl.BlockSpec((GATE_PAD, GLA_DK), lambda i: (0, 0)), pl.BlockSpec((1, GLA_DK), lambda i: (0, 0)),
         pl.BlockSpec((1, GLA_HV), lambda i: (0, 0))],
        (dmix, o, z, states, gate_w, gate_b, head_g), grid=(nt,),
        out_specs=[pl.BlockSpec((tm, GLA_IN_PAD), rev), pl.BlockSpec((GATE_PAD, GLA_DK), lambda i: (0, 0)),
                   pl.BlockSpec((1, GLA_DK), lambda i: (0, 0)), pl.BlockSpec((1, GLA_HV), lambda i: (0, 0))],
        out_shape=[jax.ShapeDtypeStruct((t, GLA_IN_PAD), BF16), jax.ShapeDtypeStruct((GATE_PAD, GLA_DK), F32),
                   jax.ShapeDtypeStruct((1, GLA_DK), F32), jax.ShapeDtypeStruct((1, GLA_HV), F32)],
        scratch_shapes=[pltpu.VMEM((GLA_DV, GLA_HK), F32), pltpu.VMEM((tm, GLA_DK), F32), pltpu.VMEM((tm, GLA_DK), F32),
                        pltpu.VMEM((tm, GLA_DK), F32), pltpu.VMEM((tm, GLA_DV), BF16),
                        pltpu.VMEM((cpt * 8, GLA_DK), F32), pltpu.VMEM((cpt * 8, GLA_DK), F32)],
        compiler_params=_cparams("arbitrary"), name="gla_seq_bwd")


def _sum_slots(x, name):
    n, r, cdim = x.shape
    tr = _row_tile(r, 256, 8)

    def body(x_ref, o_ref):
        acc = x_ref[0].astype(F32)
        for j in range(1, n):
            acc = acc + x_ref[j].astype(F32)
        o_ref[...] = acc

    return pl.pallas_call(
        body, grid=(r // tr,),
        in_specs=[pl.BlockSpec((n, tr, cdim), lambda i: (0, i, 0))],
        out_specs=pl.BlockSpec((tr, cdim), lambda i: (i, 0)),
        out_shape=jax.ShapeDtypeStruct((r, cdim), F32),
        compiler_params=_cparams("parallel"), name=name)(x)


def _sum_own_and_slots(own, slots, dev_idx, name):
    _, _, r, cdim = own.shape
    n = slots.shape[0]
    tr = _row_tile(r, 256, 8)

    def body(s_ref, own_ref, *rest):
        acc = own_ref[...].astype(F32)
        for other in rest[:n - 1]:
            acc = acc + other[...].astype(F32)
        rest[n - 1][...] = acc

    def slot(dd):
        return pl.BlockSpec((None, tr, cdim), lambda i, s: ((s[0] + dd) % n, i, 0))

    mine = pl.BlockSpec((None, None, tr, cdim), lambda i, s: (s[0] // 2, s[0] % 2, i, 0))
    return pl.pallas_call(
        body,
        grid_spec=pltpu.PrefetchScalarGridSpec(
            num_scalar_prefetch=1, grid=(r // tr,), in_specs=[mine] + [slot(dd) for dd in range(1, n)],
            out_specs=pl.BlockSpec((tr, cdim), lambda i, s: (i, 0))),
        out_shape=jax.ShapeDtypeStruct((r, cdim), F32),
        compiler_params=_cparams("parallel"), name=name)(dev_idx, own, *([slots] * (n - 1)))


def _add2(a, b, name):
    r, cdim = a.shape
    tr = _row_tile(r, 256, 8)

    def body(a_ref, b_ref, o_ref):
        o_ref[...] = a_ref[...] + b_ref[...]

    spec = pl.BlockSpec((tr, cdim), lambda i: (i, 0))
    return pl.pallas_call(body, grid=(r // tr,), in_specs=[spec, spec], out_specs=spec,
                          out_shape=jax.ShapeDtypeStruct((r, cdim), F32),
                          compiler_params=_cparams("parallel"), name=name)(a, b)


def _adamw_half(w, gs, m, v, half_idx, prev, name, dep=None):
    nl, _, h, cdim = w.shape
    tr = _row_tile(h, 256, 8)
    nprev = 0 if prev is None else 4
    extra = [] if dep is None else [dep]

    def body(s_ref, w_ref, m_ref, v_ref, *rest):
        g_refs = rest[:nl]
        go_ref, d_ref, mo_ref, vo_ref = rest[nl + nprev + len(extra):]
        layer = pl.program_id(0)
        gv = g_refs[0][...]
        for j in range(1, nl):
            gv = jnp.where(layer == j, g_refs[j][...], gv)
        go_ref[...] = gv
        mn = ADAM_B1 * m_ref[...] + (1.0 - ADAM_B1) * gv
        vn = ADAM_B2 * v_ref[...] + (1.0 - ADAM_B2) * (gv * gv)
        m_hat = mn / (1.0 - ADAM_B1 ** ADAM_STEP)
        v_hat = vn / (1.0 - ADAM_B2 ** ADAM_STEP)
        d_ref[...] = -ADAM_LR * (m_hat / (jnp.sqrt(v_hat) + ADAM_EPS) + ADAM_WD * w_ref[...])
        mo_ref[...] = mn
        vo_ref[...] = vn

    half = pl.BlockSpec((None, None, tr, cdim), lambda l, i, s: (l, s[0], i, 0))

    def of_layer(j):
        return pl.BlockSpec((tr, cdim), lambda l, i, s: (jnp.where(l == j, i, 0), 0))

    shp = jax.ShapeDtypeStruct(w.shape, F32)
    return pl.pallas_call(
        body,
        grid_spec=pltpu.PrefetchScalarGridSpec(
            num_scalar_prefetch=1, grid=(nl, h // tr),
            in_specs=[half] * 3 + [of_layer(j) for j in range(nl)] + [ANY_SPEC] * (nprev + len(extra)),
            out_specs=[half] * 4),
        out_shape=[shp] * 4, input_output_aliases={4 + nl + k: k for k in range(nprev)},
        compiler_params=_cparams("arbitrary", "arbitrary"), name=name,
    )(half_idx, w, m, v, *gs, *([] if prev is None else prev), *extra)


def _adamw_many(ws, gs, ms, vs):
    n = len(ws)

    def body(*refs):
        for i in range(n):
            w_ref, g_ref, m_ref, v_ref = refs[i], refs[n + i], refs[2 * n + i], refs[3 * n + i]
            d_ref, mo_ref, vo_ref = refs[4 * n + i], refs[5 * n + i], refs[6 * n + i]
            gv = g_ref[...]
            mn = ADAM_B1 * m_ref[...] + (1.0 - ADAM_B1) * gv
            vn = ADAM_B2 * v_ref[...] + (1.0 - ADAM_B2) * (gv * gv)
            m_hat = mn / (1.0 - ADAM_B1 ** ADAM_STEP)
            v_hat = vn / (1.0 - ADAM_B2 ** ADAM_STEP)
            d_ref[...] = -ADAM_LR * (m_hat / (jnp.sqrt(v_hat) + ADAM_EPS) + ADAM_WD * w_ref[...])
            mo_ref[...] = mn
            vo_ref[...] = vn

    shapes = [jax.ShapeDtypeStruct(w.shape, F32) for w in ws]
    outs = pl.pallas_call(body, out_shape=shapes * 3, name="adamw_small")(*ws, *gs, *ms, *vs)
    return outs[:n], outs[n:2 * n], outs[2 * n:]


def _split_rows(a):
    return a.reshape(a.shape[0], 2, a.shape[1] // 2, a.shape[2])


def _place():
    x, y, c = lax.axis_index("x"), lax.axis_index("y"), lax.axis_index("c")
    chips = [(1 - x, y), (x, 1 - y), (1 - x, 1 - y)]
    return x, y, c, chips


def _remote(src, dst, send_sem, recv_sem, to):
    return pltpu.make_async_remote_copy(src_ref=src, dst_ref=dst, send_sem=send_sem, recv_sem=recv_sem,
                                        device_id=to, device_id_type=MESH)


def _plan_gather(n_halved):
    def plan(src_refs, land_refs):
        x, y, c, chips = _place()
        me = 2 * x + y
        copies = []
        for k, (src, land) in enumerate(zip(src_refs, land_refs)):
            for (px, py) in chips:
                frm = 2 * px + py
                if k < n_halved:
                    copies.append((src.at[c], land.at[me, c], (px, py, c), land.at[frm, c]))
                else:
                    copies.append((src, land.at[me], (px, py, c), land.at[frm]))
        return copies
    return plan


def _plan_share(src_refs, land_refs):
    x, y, c, chips = _place()
    me = 2 * x + y
    sib = (x, y, 1 - c)
    copies = []
    for src, land in zip(src_refs, land_refs):
        copies.append((src, land.at[me], sib, land.at[me]))
        for (px, py) in chips:
            frm = 2 * px + py
            copies.append((land.at[frm, c], land.at[frm, c], sib, land.at[frm, 1 - c]))
    return copies


def _plan_scatter(n_parts):
    def plan(src_refs, land_refs):
        x, y, c, chips = _place()
        me = 2 * x + y
        copies = []
        for k, (src, land) in enumerate(zip(src_refs, land_refs)):
            for (px, py) in chips:
                to = 2 * px + py
                copies.append((src.at[to] if k < n_parts else src, land.at[me], (px, py, c), land.at[to]))
        return copies
    return plan


N_DEVICES = 8
OTHER_DEVICES = [(dx, dy, dc) for dx in (0, 1) for dy in (0, 1) for dc in (0, 1) if dx or dy or dc]


def _plan_scatter_all(src_refs, land_refs):
    x, y, c, _ = _place()
    me = 4 * x + 2 * y + c
    copies = []
    for src, land in zip(src_refs, land_refs):
        for dx, dy, dc in OTHER_DEVICES:
            px, py, pc = (1 - x if dx else x), (1 - y if dy else y), (1 - c if dc else c)
            copies.append((src.at[2 * px + py, pc], land.at[me], (px, py, pc), land.at[4 * px + 2 * py + pc]))
    return copies


def _plan_exchange(n_split):
    def plan(src_refs, land_refs):
        x, y, c, _ = _place()
        sib = (x, y, 1 - c)
        return [(src.at[:, 1 - c] if k < n_split else src, land, sib, land)
                for k, (src, land) in enumerate(zip(src_refs, land_refs))]
    return plan


def _hbm(a):
    return pltpu.HBM(a.shape, a.dtype)


def _start_copies(name, srcs, lands, plan, ncopy, dep=None):
    ns, nl = len(srcs), len(lands)
    nin = ns + nl + (0 if dep is None else 1)

    def body(*refs):
        send_sems, recv_sems, token = refs[nin], refs[nin + 1], refs[-1]
        for k, (src, dst, dev, _) in enumerate(plan(refs[:ns], refs[ns:ns + nl])):
            _remote(src, dst, send_sems.at[k], recv_sems.at[k], dev).start()
        token[...] = jnp.zeros_like(token)

    args = [pltpu.with_memory_space_constraint(a, pltpu.HBM) for a in list(srcs) + list(lands)]
    outs = pl.pallas_call(
        body, name=name,
        out_shape=(pltpu.SemaphoreType.DMA((ncopy,)), pltpu.SemaphoreType.DMA((ncopy,)),
                   *[_hbm(a) for a in list(srcs) + list(lands)], jax.ShapeDtypeStruct((8, 128), F32)),
        in_specs=[HBM_SPEC] * (ns + nl) + ([] if dep is None else [ANY_SPEC]),
        out_specs=(SEM_SPEC, SEM_SPEC, *([HBM_SPEC] * (ns + nl)), pl.BlockSpec(memory_space=pltpu.VMEM)),
        input_output_aliases={i: 2 + i for i in range(ns + nl)},
        compiler_params=pltpu.CompilerParams(has_side_effects=SIDE_EFFECT),
    )(*args, *([] if dep is None else [dep]))
    return outs[0], outs[1], list(outs[2:2 + ns]), list(outs[2 + ns:2 + ns + nl]), outs[-1]


def _wait_copies(name, started, plan, after, sem_offset=0):
    send_sems, recv_sems, srcs, lands, _ = started
    ns, nl = len(srcs), len(lands)
    after = list(after) if isinstance(after, (list, tuple)) else [after]

    def body(*refs):
        send_ref, recv_ref = refs[ns + nl], refs[ns + nl + 1]
        for k, (src, _, dev, mine) in enumerate(plan(refs[:ns], refs[ns:ns + nl])):
            copy = _remote(src, mine, send_ref.at[sem_offset + k], recv_ref.at[sem_offset + k], dev)
            copy.wait_send()
            copy.wait_recv()

    outs = pl.pallas_call(
        body, name=name, out_shape=tuple(_hbm(a) for a in srcs + lands),
        in_specs=[HBM_SPEC] * (ns + nl) + [SEM_SPEC, SEM_SPEC] + [ANY_SPEC] * len(after),
        out_specs=tuple([HBM_SPEC] * (ns + nl)),
        input_output_aliases={i: i for i in range(ns + nl)},
        compiler_params=pltpu.CompilerParams(has_side_effects=SIDE_EFFECT),
    )(*srcs, *lands, send_sems, recv_sems, *after)
    return list(outs[:ns]), list(outs[ns:])


def _share_with_sibling(name, srcs, lands):
    n = len(srcs)

    def body(*refs):
        src_refs, land_refs, out_refs = refs[:n], refs[n:2 * n], refs[2 * n:3 * n]
        send_sem, recv_sem = refs[3 * n:]
        x, y, c, chips = _place()
        me = 2 * x + y
        sib = (x, y, 1 - c)
        sends, recvs = [], []
        for k in range(n):
            sems = (send_sem.at[4 * k], recv_sem.at[4 * k])
            sends.append(_remote(src_refs[k], out_refs[k].at[me], *sems, sib))
            recvs.append(_remote(src_refs[k], out_refs[k].at[me], *sems, sib))
            for j, (px, py) in enumerate(chips):
                frm = 2 * px + py
                sems = (send_sem.at[4 * k + 1 + j], recv_sem.at[4 * k + 1 + j])
                sends.append(_remote(land_refs[k].at[frm, c], out_refs[k].at[frm, c], *sems, sib))
                recvs.append(_remote(land_refs[k].at[frm, c], out_refs[k].at[frm, 1 - c], *sems, sib))
        for cp in sends:
            cp.start()
        for cp in recvs:
            cp.wait_recv()
        for cp in sends:
            cp.wait_send()

    return pl.pallas_call(
        body, name=name, in_specs=[HBM_SPEC] * (2 * n), out_specs=[HBM_SPEC] * n,
        out_shape=[jax.ShapeDtypeStruct(a.shape, a.dtype) for a in lands],
        input_output_aliases={n + k: k for k in range(n)},
        scratch_shapes=[pltpu.SemaphoreType.DMA((4 * n,)), pltpu.SemaphoreType.DMA((4 * n,))],
    )(*srcs, *lands)


def _pack(arrs):
    flat = jnp.concatenate([a.reshape(-1).astype(F32) for a in arrs])
    n = flat.shape[0]
    rows = -(-n // PACK_WIDTH)
    rows = -(-rows // 8) * 8
    return jnp.pad(flat, (0, rows * PACK_WIDTH - n)).reshape(rows, PACK_WIDTH)


def _unpack(buf, shapes):
    flat = buf.reshape(-1)
    out, off = [], 0
    for shp in shapes:
        n = 1
        for s in shp:
            n *= s
        out.append(flat[off:off + n].reshape(shp))
        off += n
    return out


def _unshard_cols(stacked):
    moved = jnp.moveaxis(stacked, 0, -2)
    return moved.reshape(moved.shape[:-2] + (moved.shape[-2] * moved.shape[-1],))


def _take_cols(blocks, start, width):
    bw = blocks.shape[2]
    pieces, lo = [], start
    while lo < start + width:
        b = lo // bw
        hi = min(start + width, (b + 1) * bw)
        pieces.append(blocks[b][:, lo - b * bw:hi - b * bw])
        lo = hi
    return jnp.concatenate(pieces, axis=1)


def _col_shard(full, s, width):
    return lax.dynamic_slice_in_dim(full, s * width, width, axis=full.ndim - 1)


def kernel(x, meta_tokens, mix_norm_g, ffn_norm_g, ffn_w1, ffn_w2, cp_w_in, cp_conv_w, cp_conv_b, cp_ln_g, cp_ln_b, cp_pool_w, cp_pool_scale, cp_w_out, gla_w_in, gla_gate_w2, gla_gate_b, gla_head_g, gla_w_out, final_norm_g, loss_target, m_meta_tokens, m_mix_norm_g, m_ffn_norm_g, m_ffn_w1, m_ffn_w2, m_cp_w_in, m_cp_conv_w, m_cp_conv_b, m_cp_ln_g, m_cp_ln_b, m_cp_pool_w, m_cp_pool_scale, m_cp_w_out, m_gla_w_in, m_gla_gate_w2, m_gla_gate_b, m_gla_head_g, m_gla_w_out, m_final_norm_g, v_meta_tokens, v_mix_norm_g, v_ffn_norm_g, v_ffn_w1, v_ffn_w2, v_cp_w_in, v_cp_conv_w, v_cp_conv_b, v_cp_ln_g, v_cp_ln_b, v_cp_pool_w, v_cp_pool_scale, v_cp_w_out, v_gla_w_in, v_gla_gate_w2, v_gla_gate_b, v_gla_head_g, v_gla_w_out, v_final_norm_g):
    d = D_MODEL
    chip = 2 * lax.axis_index("x") + lax.axis_index("y")
    core = lax.axis_index("c")
    seq = x.shape[1]
    t = seq + CHUNK

    sharded_small = [meta_tokens, cp_conv_w, gla_gate_w2, gla_gate_b, gla_head_g]

    def halves(w):
        return w.astype(BF16).reshape(2, w.shape[0] // 2, w.shape[1])

    def unhalve(g):
        return g.reshape(N_CHIPS, 2 * g.shape[2], g.shape[3])

    def gather_group(srcs, whole=()):
        lands = [lax.empty((N_CHIPS,) + s.shape, s.dtype) for s in srcs]
        for a in whole:
            lands.append(lax.dynamic_update_slice(jnp.zeros((N_CHIPS,) + a.shape, a.dtype), a[None], (chip,) + (0,) * a.ndim))
        return list(srcs) + list(whole), lands, _plan_gather(len(srcs)), len(srcs)

    groups = [gather_group([halves(cp_w_in[0]), halves(cp_w_out[0])], [_pack(sharded_small)]),
              gather_group([halves(ffn_w1[0]), halves(ffn_w2[0])]), gather_group([halves(gla_w_in[0]), halves(gla_w_out[0])]),
              gather_group([halves(ffn_w1[1]), halves(ffn_w2[1])])]
    bounds, all_srcs, all_lands = [], [], []
    for srcs, lands, _, _ in groups:
        bounds.append((len(all_srcs), len(all_srcs) + len(srcs)))
        all_srcs += srcs
        all_lands += lands

    def plan_all(src_refs, land_refs):
        return [cp for (lo, hi), group in zip(bounds, groups) for cp in group[2](src_refs[lo:hi], land_refs[lo:hi])]

    gathers = _start_copies("gather_start", all_srcs, all_lands, plan_all, 3 * len(all_srcs))

    def arrived(name, gi, after):
        (lo, hi), plan, n = bounds[gi], groups[gi][2], groups[gi][3]
        mine = (gathers[0], gathers[1], gathers[2][lo:hi], gathers[3][lo:hi], gathers[4])
        srcs, lands = _wait_copies(name + "_wait", mine, plan, after, sem_offset=3 * lo)
        return srcs[:n], lands[:n], lands[n:]

    cp_gather, ffn0_gather, gla_gather, ffn1_gather = 0, 1, 2, 3
    h0_rows = jnp.concatenate([jnp.zeros((CHUNK, d), F32) + gathers[4][0, 0], x[0]], axis=0)
    cp_srcs, cp_lands, (small_g,) = arrived("gather_cp", cp_gather, h0_rows)
    cpin_g, cpout_g = [unhalve(g) for g in _share_with_sibling("gather_cp_share", cp_srcs, cp_lands)]
    per_chip = [_unpack(small_g[j], [a.shape for a in sharded_small]) for j in range(N_CHIPS)]
    meta_f, conv_w_f, gate_w_f, gate_b_f, head_g_f = [
        jnp.concatenate([per_chip[j][i] for j in range(N_CHIPS)], axis=-1) for i in range(len(sharded_small))]
    conv_w_f, gate_w_f = conv_w_f[0], gate_w_f[0]
    w_cp_in = _unshard_cols(cpin_g)
    w_cp_out = cpout_g.reshape(CONV_DIM + POOL_DIM, d)
    gate_w_pad = jnp.pad(gate_w_f, ((0, GATE_PAD - GATE_RANK), (0, 0))).astype(BF16)
    row = lambda a: a.reshape(1, -1)
    c_idx = core.reshape(1).astype(jnp.int32)

    h0 = lax.dynamic_update_slice(h0_rows, meta_f, (PAD_ROWS, 0))
    z0, u0 = _norm_matmul(h0, row(mix_norm_g[0]), w_cp_in, 512, "cp_in_proj")
    c0, pm0, mix0 = _cp_seq_fwd(z0, conv_w_f, cp_conv_b, cp_ln_g, cp_ln_b, cp_pool_w[0], cp_pool_scale)
    ffn0_srcs, ffn0_lands, _ = arrived("gather_ffn0", ffn0_gather, mix0)
    ffn0_share = _start_copies("gather_ffn0_share_start", ffn0_srcs, ffn0_lands, _plan_share, 4 * len(ffn0_srcs))
    h1 = _matmul_residual(mix0, w_cp_out, h0, "cp_out_proj", dep=ffn0_share[-1])
    w1g0, w2g0 = [unhalve(g) for g in _wait_copies("gather_ffn0_share_wait", ffn0_share, _plan_share, h1)[1]]
    h2, hp0, uf0 = _ffn_fwd(h1, row(ffn_norm_g[0]), w1g0, w2g0, "ffn0_fwd")
    gla_srcs, gla_lands, _ = arrived("gather_gla", gla_gather, h2)
    glain_g, glaout_g = [unhalve(g) for g in _share_with_sibling("gather_gla_share", gla_srcs, gla_lands)]
    w_gla_in = jnp.concatenate([glain_g[j] for j in range(N_CHIPS)] + [jnp.zeros((d, GLA_IN_PAD - GLA_IN), BF16)], axis=1)
    w_gla_out = glaout_g.reshape(GLA_DV, d)
    z1, u2 = _norm_matmul(h2, row(mix_norm_g[1]), w_gla_in, 640, "gla_in_proj")
    ffn1_srcs, ffn1_lands, _ = arrived("gather_ffn1", ffn1_gather, z1)
    ffn1_share = _start_copies("gather_ffn1_share_start", ffn1_srcs, ffn1_lands, _plan_share, 4 * len(ffn1_srcs))
    o1, mix1, states = _gla_seq_fwd(z1, gate_w_pad, gate_b_f, head_g_f, dep=ffn1_share[-1])
    h3 = _matmul_residual(mix1, w_gla_out, h2, "gla_out_proj")
    w1g1, w2g1 = [unhalve(g) for g in _wait_copies("gather_ffn1_share_wait", ffn1_share, _plan_share, h3)[1]]
    h4, hp1, uf1 = _ffn_fwd(h3, row(ffn_norm_g[1]), w1g1, w2g1, "ffn1_fwd")

    dev_idx = (2 * chip + core).reshape(1).astype(jnp.int32)

    def start_reduce(name, grads):
        srcs = [_split_rows(g) for g in grads]
        lands = [lax.empty((N_DEVICES,) + s.shape[2:], s.dtype) for s in srcs]
        return _start_copies(name + "_scatter_start", srcs, lands, _plan_scatter_all, len(OTHER_DEVICES) * len(srcs))

    def finish_reduce(name, started, after):
        srcs, lands = _wait_copies(name + "_scatter_wait", started, _plan_scatter_all, after)
        return [_sum_own_and_slots(s, l, dev_idx, "%s_slot_sum_%d" % (name, k)) for k, (s, l) in enumerate(zip(srcs, lands))]

    dh4, d_final_g, loss_part = _loss_bwd(h4, row(final_norm_g), loss_target[0])

    dh3, dhp1, d_ffn_g1 = _ffn_bwd_data(dh4, h3, row(ffn_norm_g[1]), hp1, w1g1, w2g1, "ffn1_bwd")
    dw1_1 = _wgrad(uf1, dhp1, N_CHIPS, d, d, False, True, False, "ffn1_dw1")
    dw2_1 = _wgrad(hp1, dh4, N_CHIPS, d, d, True, False, True, "ffn1_dw2")
    ffn1_reduce = start_reduce("ffn1", [dw1_1, dw2_1])

    dmix1 = _dgrad(dh3, w_gla_out, "gla_out_dgrad", dep=ffn1_reduce[-1])
    dw_gla_out = _wgrad(mix1, dh3, 1, GLA_DV, d, False, False, False, "gla_out_dw")
    dz1, d_gate_w, d_gate_b, d_head_g = _gla_seq_bwd(dmix1, o1, z1, states, gate_w_pad, gate_b_f, head_g_f)
    dh2, d_mix_g1 = _dgrad_norm_bwd(dz1, w_gla_in, h2, row(mix_norm_g[1]), dh3, 640, "gla_in_dgrad")
    dw_gla_in = _wgrad(u2, dz1, GLA_IN_PAD // 640, d, 640, False, True, False, "gla_in_dw")
    gla_in_shards = jnp.stack([_take_cols(dw_gla_in, j * (GLA_IN // N_CHIPS), GLA_IN // N_CHIPS) for j in range(N_CHIPS)])
    gla_reduce = start_reduce("gla", [gla_in_shards, dw_gla_out.reshape(N_CHIPS, -1, d)])

    dh1, dhp0, d_ffn_g0 = _ffn_bwd_data(dh2, h1, row(ffn_norm_g[0]), hp0, w1g0, w2g0, "ffn0_bwd", dep=gla_reduce[-1])
    dw1_0 = _wgrad(uf0, dhp0, N_CHIPS, d, d, False, True, False, "ffn0_dw1")
    dw2_0 = _wgrad(hp0, dh2, N_CHIPS, d, d, True, False, True, "ffn0_dw2")
    ffn0_reduce = start_reduce("ffn0", [dw1_0, dw2_0])

    dmix0 = _dgrad(dh1, w_cp_out, "cp_out_dgrad", dep=ffn0_reduce[-1])
    dw_cp_out = _wgrad(mix0, dh1, 1, CONV_DIM + POOL_DIM, d, False, False, False, "cp_out_dw")
    dz0, d_conv_w, d_cp_vec, d_pool_w = _cp_seq_bwd(dmix0, z0, c0, pm0, conv_w_f, cp_ln_g, cp_ln_b, cp_pool_w[0],
                                                    cp_pool_scale)
    grad_x, dh0_head, d_mix_g0 = _dgrad_norm_bwd_input(dz0, w_cp_in, h0, row(mix_norm_g[0]), dh1, 512, "cp_in_dgrad")
    grad_x = grad_x[None]
    dw_cp_in = _wgrad(u0, dz0, N_CHIPS, d, CP_IN // N_CHIPS, False, True, False, "cp_in_dw")

    cp_reduce = start_reduce("cp", [dw_cp_in, dw_cp_out.reshape(N_CHIPS, -1, d)])
    small_full = [dh0_head[PAD_ROWS:CHUNK],jnp.concatenate([d_mix_g0, d_mix_g1], axis=0),
                  jnp.concatenate([d_ffn_g0, d_ffn_g1], axis=0), d_conv_w[:CONV_WIDTH][None],
                  d_cp_vec[0:1], d_cp_vec[1:2], d_cp_vec[2:3], d_pool_w[None], d_cp_vec[3:4],
                  d_gate_w[:GATE_RANK][None], d_gate_b, d_head_g, d_final_g[0], loss_part[0, 0:1]]
    small_mine = _pack(small_full)
    whole = _plan_exchange(0)
    small_exchange = _start_copies("small_exchange_start", [small_mine], [lax.empty(small_mine.shape, F32)], whole, 1,
                                   dep=cp_reduce[-1])
    red_ffn1 = finish_reduce("ffn1", ffn1_reduce, small_exchange[-1])
    red_gla = finish_reduce("gla", gla_reduce, small_exchange[-1])
    (small_sent,), (small_recv,) = _wait_copies("small_exchange_wait", small_exchange, whole, [red_ffn1[1], red_gla[1]])
    small_chip = _add2(small_sent, small_recv, "chip_sum_small")
    small_slots = lax.dynamic_update_slice(jnp.zeros((N_CHIPS,) + small_chip.shape, F32), small_chip[None], (chip, 0, 0))
    small_reduce = _start_copies("small_scatter_start", [small_chip], [small_slots], _plan_scatter(0), 3)

    big = {"w1": (ffn_w1, m_ffn_w1, v_ffn_w1), "w2": (ffn_w2, m_ffn_w2, v_ffn_w2),
           "cp_in": (cp_w_in, m_cp_w_in, v_cp_w_in), "cp_out": (cp_w_out, m_cp_w_out, v_cp_w_out),
           "gla_in": (gla_w_in, m_gla_w_in, v_gla_w_in), "gla_out": (gla_w_out, m_gla_w_out, v_gla_w_out)}
    other_idx = (1 - core).reshape(1).astype(jnp.int32)

    def adamw_by_halves(tag, reduced, dep=None):
        flat = [r for n in reduced for r in reduced[n]]
        join_plan = _plan_exchange(0)
        join = _start_copies(tag + "_join_start", flat, [lax.empty(r.shape, F32) for r in flat], join_plan, len(flat),
                             dep=dep)
        views = {n: [_split_rows(a) for a in big[n]] for n in reduced}
        own, k = {}, 0
        for n in reduced:
            mine = join[2][k:k + len(reduced[n])]
            k += len(reduced[n])
            own[n] = _adamw_half(views[n][0], mine, views[n][1], views[n][2], c_idx, None, "adamw_%s_own" % n)
        _, arrived_halves = _wait_copies(tag + "_join_wait", join, join_plan, [own[n][1] for n in reduced])
        outs, k = {}, 0
        for n in reduced:
            theirs = arrived_halves[k:k + len(reduced[n])]
            k += len(reduced[n])
            res = _adamw_half(views[n][0], theirs, views[n][1], views[n][2], other_idx, own[n], "adamw_%s_sibling" % n)
            outs[n] = [o.reshape(big[n][0].shape) for o in res]
        return outs

    big_out = adamw_by_halves("gla", {"gla_in": [red_gla[0]], "gla_out": [red_gla[1]]}, dep=small_reduce[-1])
    red_ffn0 = finish_reduce("ffn0", ffn0_reduce, big_out["gla_out"][1])
    big_out.update(adamw_by_halves("ffn", {"w1": [red_ffn0[0], red_ffn1[0]], "w2": [red_ffn0[1], red_ffn1[1]]}))
    red_cp = finish_reduce("cp", cp_reduce, big_out["w2"][1])
    _, (small_landed,) = _wait_copies("small_scatter_wait", small_reduce, _plan_scatter(0), big_out["w2"][1])
    small_red = _sum_slots(small_landed, "slot_sum_small")
    big_out.update(adamw_by_halves("cp", {"cp_in": [red_cp[0]], "cp_out": [red_cp[1]]}))

    (g_meta, g_mix, g_ffn, g_conv_w, g_conv_b, g_ln_g, g_ln_b, g_pool_w, g_pool_scale, g_gate_w, g_gate_b, g_head,
     g_final, loss_sum) = _unpack(small_red, [a.shape for a in small_full])
    g_meta = _col_shard(g_meta, chip, meta_tokens.shape[-1])
    g_conv_w = _col_shard(g_conv_w, chip, cp_conv_w.shape[-1])
    g_gate_w = _col_shard(g_gate_w, chip, gla_gate_w2.shape[-1])
    g_gate_b = _col_shard(g_gate_b, chip, gla_gate_b.shape[-1])
    g_head = _col_shard(g_head, chip, gla_head_g.shape[-1])
    small_w = [meta_tokens, mix_norm_g, ffn_norm_g, cp_conv_w, cp_conv_b, cp_ln_g, cp_ln_b, cp_pool_w, cp_pool_scale,
               gla_gate_w2, gla_gate_b, gla_head_g, final_norm_g]
    small_m = [m_meta_tokens, m_mix_norm_g, m_ffn_norm_g, m_cp_conv_w, m_cp_conv_b, m_cp_ln_g, m_cp_ln_b, m_cp_pool_w,
               m_cp_pool_scale, m_gla_gate_w2, m_gla_gate_b, m_gla_head_g, m_final_norm_g]
    small_v = [v_meta_tokens, v_mix_norm_g, v_ffn_norm_g, v_cp_conv_w, v_cp_conv_b, v_cp_ln_g, v_cp_ln_b, v_cp_pool_w,
               v_cp_pool_scale, v_gla_gate_w2, v_gla_gate_b, v_gla_head_g, v_final_norm_g]
    small_g = [g_meta, g_mix, g_ffn, g_conv_w, g_conv_b, g_ln_g, g_ln_b, g_pool_w, g_pool_scale, g_gate_w, g_gate_b,
               g_head, g_final]
    shapes = [w.shape for w in small_w]
    small_g = [g.reshape(s) for g, s in zip(small_g, shapes)]
    at_least_2d = lambda arrs: [a.reshape(1, -1) if a.ndim == 1 else a for a in arrs]
    s_delta, s_m, s_v = _adamw_many(at_least_2d(small_w), at_least_2d(small_g), at_least_2d(small_m), at_least_2d(small_v))
    s_delta, s_m, s_v = [[a.reshape(s) for a, s in zip(group, shapes)] for group in (s_delta, s_m, s_v)]

    order = ["meta", "mix", "ffn", "w1", "w2", "cp_in", "conv_w", "conv_b", "ln_g", "ln_b", "pool_w", "pool_scale",
             "cp_out", "gla_in", "gate_w", "gate_b", "head", "gla_out", "final"]
    small_names = ["meta", "mix", "ffn", "conv_w", "conv_b", "ln_g", "ln_b", "pool_w", "pool_scale", "gate_w", "gate_b",
                   "head", "final"]
    big_names = ["w1", "w2", "cp_in", "cp_out", "gla_in", "gla_out"]
    table = {n: (small_g[i], s_delta[i], s_m[i], s_v[i]) for i, n in enumerate(small_names)}
    table.update({n: tuple(big_out[n]) for n in big_names})
    loss = loss_sum.reshape(())
    return (loss, grad_x, *[table[n][0] for n in order], *[table[n][1] for n in order],
            *[table[n][2] for n in order], *[table[n][3] for n in order])
```

```python
import functools

import jax
import jax.numpy as jnp
from jax import lax
from jax.experimental import pallas as pl
from jax.experimental.pallas import tpu as pltpu

F32 = jnp.float32
BF16 = jnp.bfloat16

D_MODEL = 1024
N_META = 16
CHUNK = 64
PAD_ROWS = CHUNK - N_META
EPS = 1e-5
CONV_DIM = 512
CONV_WIDTH = 31
CONV_HALO = 32
POOL_DIM = 512
POOL_WINDOWS = (2, 4, 8, 16)
POOL_GROUP = 128
POOL_HALO = 16
CP_IN = 2 * CONV_DIM + POOL_DIM
GLA_HEADS = 4
GLA_DK = 512
GLA_DV = 1024
GLA_HK = GLA_DK // GLA_HEADS
GLA_HV = GLA_DV // GLA_HEADS
GATE_RANK = 16
GATE_PAD = 128
GATE_NORM = 16.0
GLA_IN = 2 * GLA_DK + 2 * GLA_DV + GATE_RANK
GLA_IN_PAD = 2 * GLA_DK + 2 * GLA_DV + GATE_PAD
GLA_COLS = 1280
N_CHIPS = 4
ADAM_LR = 0.001
ADAM_B1 = 0.9
ADAM_B2 = 0.999
ADAM_EPS = 1e-08
ADAM_WD = 0.01
ADAM_STEP = 10

VMEM_LIMIT_BYTES = 56 * 1024 * 1024
ROW_TILE_TARGET = 832
TOKEN_TILE_TARGET = 1040
PACK_WIDTH = 1024
MESH = pl.DeviceIdType.MESH
HBM_SPEC = pl.BlockSpec(memory_space=pltpu.HBM)
ANY_SPEC = pl.BlockSpec(memory_space=pl.ANY)
SEM_SPEC = pl.BlockSpec(memory_space=pltpu.SEMAPHORE)
SIDE_EFFECT = pltpu.SideEffectType.DATAFLOW_SIDE_EFFECTING


def _cparams(*sem):
    return pltpu.CompilerParams(dimension_semantics=sem, vmem_limit_bytes=VMEM_LIMIT_BYTES)


def _row_tile(t, target, mult):
    best = mult
    for cand in range(mult, min(t, target) + 1, mult):
        if t % cand == 0:
            best = cand
    assert t % best == 0, (t, best)
    return best


def _rms(h, g):
    return h * lax.rsqrt(jnp.mean(h * h, axis=-1, keepdims=True) + EPS) * g


def _rms_bwd(h, g, du):
    r = lax.rsqrt(jnp.mean(h * h, axis=-1, keepdims=True) + EPS)
    xhat = h * r
    dxh = du * g
    dh = r * (dxh - xhat * jnp.mean(dxh * xhat, axis=-1, keepdims=True))
    return dh, du * xhat


def _valid_rows(i, tm):
    row = i * tm + lax.broadcasted_iota(jnp.int32, (tm, 1), 0)
    return row >= PAD_ROWS


def _dot(a, b):
    return jnp.dot(a, b, preferred_element_type=F32)


def _dot_nt(a, b):
    return lax.dot_general(a, b, (((1,), (1,)), ((), ())), preferred_element_type=F32)


def _dot_tn(a, b):
    return lax.dot_general(a, b, (((0,), (0,)), ((), ())), preferred_element_type=F32)


def _accumulate(ref, val, first):
    @pl.when(first)
    def _():
        ref[...] = val

    @pl.when(jnp.logical_not(first))
    def _():
        ref[...] += val


def _call_after(dep, body, n_in, in_specs, args, **kw):
    if dep is None:
        return pl.pallas_call(body, in_specs=in_specs, **kw)(*args)

    def with_dep(*refs):
        body(*refs[:n_in], *refs[n_in + 1:])

    return pl.pallas_call(with_dep, in_specs=list(in_specs) + [ANY_SPEC], **kw)(*args, dep)


def _norm_matmul(h, g, w, nc, name, dep=None):
    t, d = h.shape
    n = w.shape[1]
    tm = _row_tile(t, TOKEN_TILE_TARGET, 16)

    def body(h_ref, g_ref, w_ref, z_ref, u_ref):
        u = _rms(h_ref[...], g_ref[...]).astype(BF16)
        u_ref[...] = u
        for n0 in range(0, n, nc):
            n1 = min(n0 + nc, n)
            z_ref[:, n0:n1] = _dot(u, w_ref[:, n0:n1]).astype(BF16)

    return _call_after(
        dep, body, 3,
        [pl.BlockSpec((tm, d), lambda i: (i, 0)), pl.BlockSpec((1, d), lambda i: (0, 0)),
         pl.BlockSpec((d, n), lambda i: (0, 0))], (h, g, w), grid=(t // tm,),
        out_specs=[pl.BlockSpec((tm, n), lambda i: (i, 0)), pl.BlockSpec((tm, d), lambda i: (i, 0))],
        out_shape=[jax.ShapeDtypeStruct((t, n), BF16), jax.ShapeDtypeStruct((t, d), BF16)],
        compiler_params=_cparams("parallel"), name=name)


def _matmul_residual(a, w, h, name, dep=None):
    t, k = a.shape
    d = w.shape[1]
    tm = _row_tile(t, TOKEN_TILE_TARGET, 16)

    def body(a_ref, w_ref, h_ref, o_ref):
        o_ref[...] = h_ref[...] + _dot(a_ref[...], w_ref[...])

    return _call_after(
        dep, body, 3,
        [pl.BlockSpec((tm, k), lambda i: (i, 0)), pl.BlockSpec((k, d), lambda i: (0, 0)),
         pl.BlockSpec((tm, d), lambda i: (i, 0))], (a, w, h), grid=(t // tm,),
        out_specs=pl.BlockSpec((tm, d), lambda i: (i, 0)),
        out_shape=jax.ShapeDtypeStruct((t, d), F32),
        compiler_params=_cparams("parallel"), name=name)


def _ffn_fwd(h, g, w1g, w2g, name):
    t, d = h.shape
    ns, ffs = w1g.shape[0], w1g.shape[2]
    tm = _row_tile(t, TOKEN_TILE_TARGET, 16)

    def body(h_ref, g_ref, w1_ref, w2_ref, ho_ref, hp_ref, u_ref, acc_ref):
        s = pl.program_id(1)

        @pl.when(s == 0)
        def _():
            u_ref[...] = _rms(h_ref[...], g_ref[...]).astype(BF16)

        hp = _dot(u_ref[...], w1_ref[...])
        hp_ref[...] = hp.astype(BF16)
        a = jnp.maximum(hp, 0.0)
        _accumulate(acc_ref, _dot((a * a).astype(BF16), w2_ref[...]), s == 0)

        @pl.when(s == ns - 1)
        def _():
            ho_ref[...] = h_ref[...] + acc_ref[...]

    return pl.pallas_call(
        body, grid=(t // tm, ns),
        in_specs=[pl.BlockSpec((tm, d), lambda i, s: (i, 0)), pl.BlockSpec((1, d), lambda i, s: (0, 0)),
                  pl.BlockSpec((None, d, ffs), lambda i, s: (s, 0, 0)),
                  pl.BlockSpec((None, ffs, d), lambda i, s: (s, 0, 0))],
        out_specs=[pl.BlockSpec((tm, d), lambda i, s: (i, 0)), pl.BlockSpec((tm, ffs), lambda i, s: (i, s)),
                   pl.BlockSpec((tm, d), lambda i, s: (i, 0))],
        out_shape=[jax.ShapeDtypeStruct((t, d), F32), jax.ShapeDtypeStruct((t, ns * ffs), BF16),
                   jax.ShapeDtypeStruct((t, d), BF16)],
        scratch_shapes=[pltpu.VMEM((tm, d), F32)],
        compiler_params=_cparams("parallel", "arbitrary"), name=name)(h, g, w1g, w2g)


def _ffn_bwd_data(dh, h, g, hp, w1g, w2g, name, dep=None):
    t, d = h.shape
    ns, ffs = w1g.shape[0], w1g.shape[2]
    tm = _row_tile(t, ROW_TILE_TARGET, CHUNK)

    def body(dh_ref, h_ref, g_ref, hp_ref, w1_ref, w2_ref, dhi_ref, dhp_ref, dg_ref, acc_ref):
        i, s = pl.program_id(0), pl.program_id(1)
        da = _dot_nt(dh_ref[...].astype(BF16), w2_ref[...])
        dhp = (da * (2.0 * jnp.maximum(hp_ref[...].astype(F32), 0.0))).astype(BF16)
        dhp_ref[...] = dhp
        _accumulate(acc_ref, _dot_nt(dhp, w1_ref[...]), s == 0)

        @pl.when(s == ns - 1)
        def _():
            dhn, dgr = _rms_bwd(h_ref[...], g_ref[...], acc_ref[...])
            dhi_ref[...] = jnp.where(_valid_rows(i, tm), dh_ref[...] + dhn, 0.0)
            _accumulate(dg_ref, jnp.sum(dgr, axis=0, keepdims=True), i == 0)

    return _call_after(
        dep, body, 6,
        [pl.BlockSpec((tm, d), lambda i, s: (i, 0)), pl.BlockSpec((tm, d), lambda i, s: (i, 0)),
         pl.BlockSpec((1, d), lambda i, s: (0, 0)), pl.BlockSpec((tm, ffs), lambda i, s: (i, s)),
         pl.BlockSpec((None, d, ffs), lambda i, s: (s, 0, 0)),
         pl.BlockSpec((None, ffs, d), lambda i, s: (s, 0, 0))], (dh, h, g, hp, w1g, w2g), grid=(t // tm, ns),
        out_specs=[pl.BlockSpec((tm, d), lambda i, s: (i, 0)), pl.BlockSpec((tm, ffs), lambda i, s: (i, s)),
                   pl.BlockSpec((1, d), lambda i, s: (0, 0))],
        out_shape=[jax.ShapeDtypeStruct((t, d), F32), jax.ShapeDtypeStruct((t, ns * ffs), BF16),
                   jax.ShapeDtypeStruct((1, d), F32)],
        scratch_shapes=[pltpu.VMEM((tm, d), F32)],
        compiler_params=_cparams("arbitrary", "arbitrary"), name=name)


WGRAD_ROWS = 1024


def _wgrad(x, dy, nb, xc, yc, x_by_block, dy_by_block, relu2, name, dep=None):
    t = x.shape[0]
    tk = _row_tile(t - CHUNK, WGRAD_ROWS, CHUNK)

    def prep(xv):
        if relu2:
            xv = jnp.maximum(xv.astype(F32), 0.0)
            xv = xv * xv
        return xv.astype(BF16)

    nk = (t - CHUNK) // tk

    def body(xh_ref, dyh_ref, x_ref, dy_ref, o_ref, acc_ref):
        k = pl.program_id(1)
        p = _dot_tn(prep(x_ref[...]), dy_ref[...].astype(BF16))

        @pl.when(k == 0)
        def _():
            acc_ref[...] = p + _dot_tn(prep(xh_ref[...]), dyh_ref[...].astype(BF16))

        @pl.when(k > 0)
        def _():
            acc_ref[...] += p

        @pl.when(k == nk - 1)
        def _():
            o_ref[...] = acc_ref[...].astype(BF16)

    def head(width, by_block):
        return pl.BlockSpec((CHUNK, width), (lambda b, k: (0, b)) if by_block else (lambda b, k: (0, 0)))

    def rest(width, by_block):
        def index(b, k):
            return pl.multiple_of(CHUNK + k * tk, CHUNK), (pl.multiple_of(b * width, 128) if by_block else 0)
        return pl.BlockSpec((pl.Element(tk), pl.Element(width)), index)

    return _call_after(
        dep, body, 4,
        [head(xc, x_by_block), head(yc, dy_by_block), rest(xc, x_by_block), rest(yc, dy_by_block)], (x, dy, x, dy),
        grid=(nb, nk),
        out_specs=pl.BlockSpec((None, xc, yc), lambda b, k: (b, 0, 0)),
        out_shape=jax.ShapeDtypeStruct((nb, xc, yc), BF16),
        scratch_shapes=[pltpu.VMEM((xc, yc), F32)],
        compiler_params=_cparams("parallel", "arbitrary"), name=name)


def _dgrad(dh, w, name, dep=None):
    t, d = dh.shape
    k = w.shape[0]
    tm = _row_tile(t, TOKEN_TILE_TARGET, 16)

    def body(dh_ref, w_ref, o_ref):
        o_ref[...] = _dot_nt(dh_ref[...].astype(BF16), w_ref[...]).astype(BF16)

    return _call_after(
        dep, body, 2,
        [pl.BlockSpec((tm, d), lambda i: (i, 0)), pl.BlockSpec((k, d), lambda i: (0, 0))], (dh, w), grid=(t // tm,),
        out_specs=pl.BlockSpec((tm, k), lambda i: (i, 0)),
        out_shape=jax.ShapeDtypeStruct((t, k), BF16),
        compiler_params=_cparams("parallel"), name=name)


def _dgrad_norm_bwd(dz, w, h, g, dh, nc, name):
    t, d = h.shape
    n = w.shape[1]
    tm = _row_tile(t, ROW_TILE_TARGET // 2, 16)

    def body(dz_ref, w_ref, h_ref, g_ref, dh_ref, dhi_ref, dg_ref):
        i = pl.program_id(0)
        du = jnp.zeros((tm, d), F32)
        for n0 in range(0, n, nc):
            n1 = min(n0 + nc, n)
            du = du + _dot_nt(dz_ref[:, n0:n1], w_ref[:, n0:n1])
        dhn, dgr = _rms_bwd(h_ref[...], g_ref[...], du)
        dhi_ref[...] = jnp.where(_valid_rows(i, tm), dh_ref[...] + dhn, 0.0)
        _accumulate(dg_ref, jnp.sum(dgr, axis=0, keepdims=True), i == 0)

    return pl.pallas_call(
        body, grid=(t // tm,),
        in_specs=[pl.BlockSpec((tm, n), lambda i: (i, 0)), pl.BlockSpec((d, n), lambda i: (0, 0)),
                  pl.BlockSpec((tm, d), lambda i: (i, 0)), pl.BlockSpec((1, d), lambda i: (0, 0)),
                  pl.BlockSpec((tm, d), lambda i: (i, 0))],
        out_specs=[pl.BlockSpec((tm, d), lambda i: (i, 0)), pl.BlockSpec((1, d), lambda i: (0, 0))],
        out_shape=[jax.ShapeDtypeStruct((t, d), F32), jax.ShapeDtypeStruct((1, d), F32)],
        compiler_params=_cparams("arbitrary"), name=name)(dz, w, h, g, dh)


def _dgrad_norm_bwd_input(dz, w, h, g, dh, nc, name):
    t, d = h.shape
    n = w.shape[1]
    tl = _row_tile(t - CHUNK, 512, CHUNK)

    def grads(dz_ref, w_ref, h_ref, g_ref, dh_ref, rows):
        du = jnp.zeros((rows, d), F32)
        for n0 in range(0, n, nc):
            n1 = min(n0 + nc, n)
            du = du + _dot_nt(dz_ref[:, n0:n1], w_ref[:, n0:n1])
        dhn, dgr = _rms_bwd(h_ref[...], g_ref[...], du)
        return dh_ref[...] + dhn, jnp.sum(dgr, axis=0, keepdims=True)

    def rest_body(dz_ref, w_ref, h_ref, g_ref, dh_ref, dg_head_ref, dx_ref, dg_ref):
        dx, dg = grads(dz_ref, w_ref, h_ref, g_ref, dh_ref, tl)
        dx_ref[...] = dx

        @pl.when(pl.program_id(0) == 0)
        def _():
            dg_ref[...] = dg_head_ref[...] + dg

        @pl.when(pl.program_id(0) > 0)
        def _():
            dg_ref[...] += dg

    def head_body(dz_ref, w_ref, h_ref, g_ref, dh_ref, dx_ref, dg_ref):
        dx, dg = grads(dz_ref, w_ref, h_ref, g_ref, dh_ref, CHUNK)
        dx_ref[...] = jnp.where(_valid_rows(0, CHUNK), dx, 0.0)
        dg_ref[...] = dg

    def shifted(width):
        return pl.BlockSpec((pl.Element(tl), pl.Element(width)), lambda i: (pl.multiple_of(CHUNK + i * tl, CHUNK), 0))

    whole = [pl.BlockSpec((d, n), lambda i: (0, 0)), pl.BlockSpec((1, d), lambda i: (0, 0))]
    head = lambda width: pl.BlockSpec((CHUNK, width), lambda i: (0, 0))
    dh_head, dg_head = pl.pallas_call(
        head_body, grid=(1,), in_specs=[head(n), whole[0], head(d), whole[1], head(d)],
        out_specs=[head(d), whole[1]],
        out_shape=[jax.ShapeDtypeStruct((CHUNK, d), F32), jax.ShapeDtypeStruct((1, d), F32)],
        compiler_params=_cparams("arbitrary"), name=name + "_head")(dz, w, h, g, dh)
    dx, dg = pl.pallas_call(
        rest_body, grid=((t - CHUNK) // tl,),
        in_specs=[shifted(n), whole[0], shifted(d), whole[1], shifted(d), whole[1]],
        out_specs=[pl.BlockSpec((tl, d), lambda i: (i, 0)), whole[1]],
        out_shape=[jax.ShapeDtypeStruct((t - CHUNK, d), F32), jax.ShapeDtypeStruct((1, d), F32)],
        compiler_params=_cparams("arbitrary"), name=name)(dz, w, h, g, dh, dg_head)
    return dx, dh_head, dg


def _loss_bwd(h, g, target):
    t, d = h.shape
    tl = _row_tile(t - CHUNK, 1024, CHUNK)

    def body(h_ref, g_ref, t_ref, dh_ref, dg_ref, loss_ref):
        i = pl.program_id(0)
        hv, gv = h_ref[...], g_ref[...]
        err = _rms(hv, gv) - t_ref[...]
        part = 0.5 * jnp.sum(jnp.mean(err * err, axis=-1, keepdims=True), axis=0, keepdims=True)
        dhn, dgr = _rms_bwd(hv, gv, err * (1.0 / d))
        dh_ref[...] = dhn
        _accumulate(dg_ref, jnp.sum(dgr, axis=0, keepdims=True), i == 0)
        _accumulate(loss_ref, jnp.broadcast_to(part, (8, 128)), i == 0)

    shifted = pl.BlockSpec((pl.Element(tl), pl.Element(d)), lambda i: (pl.multiple_of(CHUNK + i * tl, CHUNK), 0))
    dh, dg, loss = pl.pallas_call(
        body, grid=((t - CHUNK) // tl,),
        in_specs=[shifted, pl.BlockSpec((1, d), lambda i: (0, 0)), pl.BlockSpec((tl, d), lambda i: (i, 0))],
        out_specs=[shifted, pl.BlockSpec((1, d), lambda i: (0, 0)), pl.BlockSpec((8, 128), lambda i: (0, 0))],
        out_shape=[jax.ShapeDtypeStruct((t, d), F32), jax.ShapeDtypeStruct((1, d), F32),
                   jax.ShapeDtypeStruct((8, 128), F32)],
        compiler_params=_cparams("arbitrary"), name="loss_bwd")(h, g, target)

    def zero_head(dh_ref, o_ref):
        o_ref[...] = jnp.zeros_like(o_ref)

    dh = pl.pallas_call(
        zero_head, grid=(1,), in_specs=[ANY_SPEC], out_specs=pl.BlockSpec((CHUNK, d), lambda i: (0, 0)),
        out_shape=jax.ShapeDtypeStruct((t, d), F32), input_output_aliases={0: 0}, name="loss_bwd_head")(dh)
    return dh, dg, loss


CONV_BLOCK = 32


def _silu(x):
    return x * jax.nn.sigmoid(x)


def _row_shifts(win):
    n = win.shape[0]
    return [win] + [pltpu.roll(win, n - j, 0) for j in range(1, 8)]


def _cp_seq_fwd(z, conv_w, conv_b, ln_g, ln_b, pool_w, pool_scale):
    t = z.shape[0]
    tm = _row_tile(t, ROW_TILE_TARGET, CHUNK)

    def body(z_ref, cw_ref, cb_ref, lg_ref, lb_ref, pw_ref, ps_ref, c_ref, pm_ref, mix_ref, gbuf, pbuf):
        i = pl.program_id(0)

        @pl.when(i == 0)
        def _():
            gbuf[0:CONV_HALO, :] = jnp.zeros((CONV_HALO, CONV_DIM), F32)
            pbuf[0:POOL_HALO, :] = jnp.zeros((POOL_HALO, POOL_DIM), F32)

        @pl.when(i > 0)
        def _():
            gbuf[0:CONV_HALO, :] = gbuf[tm:tm + CONV_HALO, :]
            pbuf[0:POOL_HALO, :] = pbuf[tm:tm + POOL_HALO, :]

        av = z_ref[:, 0:CONV_DIM].astype(F32)
        ag = z_ref[:, CONV_DIM:2 * CONV_DIM].astype(F32)
        gbuf[CONV_HALO:CONV_HALO + tm, :] = av * jax.nn.sigmoid(ag)
        pbuf[POOL_HALO:POOL_HALO + tm, :] = z_ref[:, 2 * CONV_DIM:CP_IN].astype(F32)

        def conv_block(rb, carry):
            base = pl.multiple_of(rb * CONV_BLOCK, CONV_BLOCK)
            shifted = _row_shifts(gbuf[pl.ds(base, CONV_BLOCK + CONV_HALO), :])
            acc = jnp.zeros((CONV_BLOCK, CONV_DIM), F32)
            for k in range(CONV_WIDTH):
                whole, part = divmod(CONV_HALO - (CONV_WIDTH - 1) + k, 8)
                acc = acc + cw_ref[k:k + 1, :] * shifted[part][8 * whole:8 * whole + CONV_BLOCK, :]
            c_ref[pl.ds(base, CONV_BLOCK), :] = acc + cb_ref[...]
            return carry

        lax.fori_loop(0, tm // CONV_BLOCK, conv_block, 0)

        c = c_ref[...]
        mu = jnp.mean(c, axis=-1, keepdims=True)
        xc = c - mu
        ln = xc * lax.rsqrt(jnp.mean(xc * xc, axis=-1, keepdims=True) + EPS) * lg_ref[...] + lb_ref[...]
        row = i * tm + lax.broadcasted_iota(jnp.int32, (tm, 1), 0)
        mix_ref[:, 0:CONV_DIM] = jnp.where(row >= PAD_ROWS, _silu(ln), 0.0).astype(BF16)

        tpos = (row - PAD_ROWS + 1).astype(F32)
        for gi, wdw in enumerate(POOL_WINDOWS):
            lo = POOL_GROUP * gi
            cur = pbuf[POOL_HALO:POOL_HALO + tm, lo:lo + POOL_GROUP]
            sacc = cur
            for j in range(1, wdw):
                sacc = sacc + pbuf[POOL_HALO - j:POOL_HALO - j + tm, lo:lo + POOL_GROUP]
            pm = (sacc / jnp.clip(tpos, 1.0, float(wdw)) - cur).astype(BF16)
            pm_ref[:, lo:lo + POOL_GROUP] = pm
            pg = _dot(pm, pw_ref[gi].astype(BF16))
            mix_ref[:, CONV_DIM + lo:CONV_DIM + lo + POOL_GROUP] = (pg * ps_ref[:, lo:lo + POOL_GROUP]).astype(BF16)

    vec = pl.BlockSpec((1, CONV_DIM), lambda i: (0, 0))
    return pl.pallas_call(
        body, grid=(t // tm,),
        in_specs=[pl.BlockSpec((tm, CP_IN), lambda i: (i, 0)),
                  pl.BlockSpec((CONV_WIDTH, CONV_DIM), lambda i: (0, 0)), vec, vec, vec,
                  pl.BlockSpec((len(POOL_WINDOWS), POOL_GROUP, POOL_GROUP), lambda i: (0, 0, 0)), vec],
        out_specs=[pl.BlockSpec((tm, CONV_DIM), lambda i: (i, 0)), pl.BlockSpec((tm, POOL_DIM), lambda i: (i, 0)),
                   pl.BlockSpec((tm, CONV_DIM + POOL_DIM), lambda i: (i, 0))],
        out_shape=[jax.ShapeDtypeStruct((t, CONV_DIM), F32), jax.ShapeDtypeStruct((t, POOL_DIM), BF16),
                   jax.ShapeDtypeStruct((t, CONV_DIM + POOL_DIM), BF16)],
        scratch_shapes=[pltpu.VMEM((tm + CONV_HALO, CONV_DIM), F32), pltpu.VMEM((tm + POOL_HALO, POOL_DIM), F32)],
        compiler_params=_cparams("arbitrary"), name="cp_seq_fwd")(z, conv_w, conv_b, ln_g, ln_b, pool_w, pool_scale)


def _cp_seq_bwd(dmix, z, c, pm, conv_w, ln_g, ln_b, pool_w, pool_scale, dep=None):
    t = z.shape[0]
    tm = _row_tile(t, ROW_TILE_TARGET, CHUNK)
    nt = t // tm

    def body(dmix_ref, z_ref, c_ref, pm_ref, cw_ref, lg_ref, lb_ref, pw_ref, ps_ref,
             dz_ref, dcw_ref, dvec_ref, dpw_ref, dcbuf, qbuf, glu_buf, dwacc):
        i = pl.program_id(0)
        tile = nt - 1 - i

        @pl.when(i == 0)
        def _():
            dcbuf[tm:tm + CONV_HALO, :] = jnp.zeros((CONV_HALO, CONV_DIM), F32)
            qbuf[tm:tm + POOL_HALO, :] = jnp.zeros((POOL_HALO, POOL_DIM), F32)
            dcw_ref[...] = jnp.zeros_like(dcw_ref)
            dwacc[...] = jnp.zeros_like(dwacc)
            dvec_ref[...] = jnp.zeros_like(dvec_ref)
            dpw_ref[...] = jnp.zeros_like(dpw_ref)

        @pl.when(i > 0)
        def _():
            dcbuf[tm:tm + CONV_HALO, :] = dcbuf[0:CONV_HALO, :]
            qbuf[tm:tm + POOL_HALO, :] = qbuf[0:POOL_HALO, :]

        row = tile * tm + lax.broadcasted_iota(jnp.int32, (tm, 1), 0)
        cv = c_ref[...]
        mu = jnp.mean(cv, axis=-1, keepdims=True)
        xc = cv - mu
        rstd = lax.rsqrt(jnp.mean(xc * xc, axis=-1, keepdims=True) + EPS)
        xhat = xc * rstd
        ln = xhat * lg_ref[...] + lb_ref[...]
        sg = jax.nn.sigmoid(ln)
        da = jnp.where(row >= PAD_ROWS, dmix_ref[:, 0:CONV_DIM].astype(F32), 0.0)
        dln = da * (sg * (1.0 + ln * (1.0 - sg)))
        dxh = dln * lg_ref[...]
        dc = rstd * (dxh - jnp.mean(dxh, axis=-1, keepdims=True) - xhat * jnp.mean(dxh * xhat, axis=-1, keepdims=True))
        dcbuf[0:tm, :] = dc
        dvec_ref[0:1, :] += jnp.sum(dc, axis=0, keepdims=True)
        dvec_ref[1:2, :] += jnp.sum(dln * xhat, axis=0, keepdims=True)
        dvec_ref[2:3, :] += jnp.sum(dln, axis=0, keepdims=True)

        av = z_ref[:, 0:CONV_DIM].astype(F32)
        sig_g = jax.nn.sigmoid(z_ref[:, CONV_DIM:2 * CONV_DIM].astype(F32))
        glu_buf[...] = av * sig_g

        def conv_block(rb, carry):
            base = pl.multiple_of(rb * CONV_BLOCK, CONV_BLOCK)
            shifted = _row_shifts(dcbuf[pl.ds(base, CONV_BLOCK + CONV_HALO), :])
            glu = glu_buf[pl.ds(base, CONV_BLOCK), :]
            acc = jnp.zeros((CONV_BLOCK, CONV_DIM), F32)
            for k in range(CONV_WIDTH):
                whole, part = divmod(CONV_WIDTH - 1 - k, 8)
                slab = shifted[part][8 * whole:8 * whole + CONV_BLOCK, :]
                acc = acc + cw_ref[k:k + 1, :] * slab
                prod = slab * glu
                part = prod[0:8]
                for q in range(1, CONV_BLOCK // 8):
                    part = part + prod[8 * q:8 * q + 8]
                dwacc[k] += part
            glu_buf[pl.ds(base, CONV_BLOCK), :] = acc
            return carry

        lax.fori_loop(0, tm // CONV_BLOCK, conv_block, 0)

        @pl.when(i == nt - 1)
        def _():
            for k in range(CONV_WIDTH):
                dcw_ref[k:k + 1, :] = jnp.sum(dwacc[k], axis=0, keepdims=True)
        dglu = glu_buf[...]
        dz_ref[:, 0:CONV_DIM] = (dglu * sig_g).astype(BF16)
        dz_ref[:, CONV_DIM:2 * CONV_DIM] = (dglu * av * sig_g * (1.0 - sig_g)).astype(BF16)

        tpos = (row - PAD_ROWS + 1).astype(F32)
        for gi, wdw in enumerate(POOL_WINDOWS):
            lo = POOL_GROUP * gi
            dp = dmix_ref[:, CONV_DIM + lo:CONV_DIM + lo + POOL_GROUP].astype(F32)
            pmv = pm_ref[:, lo:lo + POOL_GROUP]
            pwb = pw_ref[gi].astype(BF16)
            dvec_ref[3:4, lo:lo + POOL_GROUP] += jnp.sum(dp * _dot(pmv, pwb), axis=0, keepdims=True)
            dq = (dp * ps_ref[:, lo:lo + POOL_GROUP]).astype(BF16)
            dpw_ref[gi] += _dot_tn(pmv, dq)
            dpm = _dot_nt(dq, pwb)
            qbuf[0:tm, lo:lo + POOL_GROUP] = dpm / jnp.clip(tpos, 1.0, float(wdw))
            sacc = -dpm
            for j in range(wdw):
                sacc = sacc + qbuf[j:j + tm, lo:lo + POOL_GROUP]
            dz_ref[:, 2 * CONV_DIM + lo:2 * CONV_DIM + lo + POOL_GROUP] = sacc.astype(BF16)

    vec = pl.BlockSpec((1, CONV_DIM), lambda i: (0, 0))
    rev = lambda i: (nt - 1 - i, 0)
    return _call_after(
        dep, body, 9,
        [pl.BlockSpec((tm, CONV_DIM + POOL_DIM), rev), pl.BlockSpec((tm, CP_IN), rev),
         pl.BlockSpec((tm, CONV_DIM), rev), pl.BlockSpec((tm, POOL_DIM), rev),
         pl.BlockSpec((CONV_WIDTH, CONV_DIM), lambda i: (0, 0)), vec, vec,
         pl.BlockSpec((len(POOL_WINDOWS), POOL_GROUP, POOL_GROUP), lambda i: (0, 0, 0)), vec],
        (dmix, z, c, pm, conv_w, ln_g, ln_b, pool_w, pool_scale), grid=(nt,),
        out_specs=[pl.BlockSpec((tm, CP_IN), rev), pl.BlockSpec((CONV_WIDTH + 1, CONV_DIM), lambda i: (0, 0)),
                   pl.BlockSpec((8, CONV_DIM), lambda i: (0, 0)),
                   pl.BlockSpec((len(POOL_WINDOWS), POOL_GROUP, POOL_GROUP), lambda i: (0, 0, 0))],
        out_shape=[jax.ShapeDtypeStruct((t, CP_IN), BF16), jax.ShapeDtypeStruct((CONV_WIDTH + 1, CONV_DIM), F32),
                   jax.ShapeDtypeStruct((8, CONV_DIM), F32),
                   jax.ShapeDtypeStruct((len(POOL_WINDOWS), POOL_GROUP, POOL_GROUP), F32)],
        scratch_shapes=[pltpu.VMEM((tm + CONV_HALO, CONV_DIM), F32), pltpu.VMEM((tm + POOL_HALO, POOL_DIM), F32),
                        pltpu.VMEM((tm, CONV_DIM), F32), pltpu.VMEM((CONV_WIDTH + 1, 8, CONV_DIM), F32)],
        compiler_params=_cparams("arbitrary"), name="cp_seq_bwd")


GLA_UNROLL = 2
Q0, K0, V0, G0, R0 =0, GLA_DK, 2 * GLA_DK, 2 * GLA_DK + GLA_DV, 2 * GLA_DK + 2 * GLA_DV


def _split3(x):
    hi = x.astype(BF16)
    r1 = x - hi.astype(F32)
    mid = r1.astype(BF16)
    lo = (r1 - mid.astype(F32)).astype(BF16)
    return hi, mid, lo


def _tri(strict):
    r = lax.broadcasted_iota(jnp.int32, (CHUNK, CHUNK), 0)
    c = lax.broadcasted_iota(jnp.int32, (CHUNK, CHUNK), 1)
    return ((r > c) if strict else (r >= c)).astype(BF16)


def _chunk_sums(x, cpt, strict, pieces):
    tri3 = jnp.broadcast_to(_tri(strict)[None], (cpt, CHUNK, CHUNK))
    acc = None
    for piece in _split3(x.reshape(cpt, CHUNK, x.shape[-1]))[:pieces]:
        part = jnp.einsum("bij,bjk->bik", tri3, piece, preferred_element_type=F32)
        acc = part if acc is None else acc + part
    return acc


def _chunk_decay(r, gw_ref, gb_ref, cpt):
    pre = _dot(r, gw_ref[...]) + gb_ref[...]
    lac = (jnp.minimum(pre, 0.0) - jnp.log(1.0 + jnp.exp(-jnp.abs(pre)))) * (1.0 / GATE_NORM)
    cum3 = _chunk_sums(lac, cpt, False, 3)
    return cum3, cum3[:, CHUNK - 1:CHUNK, :]


def _gla_seq_fwd(z, gate_w, gate_b, head_g, dep=None):
    t = z.shape[0]
    tm = _row_tile(t, ROW_TILE_TARGET, CHUNK)
    cpt = tm // CHUNK
    scale = GLA_HK ** -0.5

    def body(z_ref, gw_ref, gb_ref, hg_ref, o_ref, mix_ref, st_ref, state, kdec_s, e_s):
        @pl.when(pl.program_id(0) == 0)
        def _():
            state[...] = jnp.zeros_like(state)

        cum3, tot3 = _chunk_decay(z_ref[:, R0:R0 + GATE_PAD], gw_ref, gb_ref, cpt)
        dec = jnp.exp(jnp.broadcast_to(tot3, cum3.shape) - cum3).reshape(tm, GLA_DK)
        kdec_s[...] = (z_ref[:, K0:K0 + GLA_DK].astype(F32) * dec).astype(BF16)
        e_s[...] = jnp.exp(jnp.broadcast_to(tot3, (cpt, 8, GLA_DK))).reshape(cpt * 8, GLA_DK)

        def chunk(ci, carry):
            rows = pl.ds(pl.multiple_of(ci * CHUNK, CHUNK), CHUNK)
            e_all = e_s[pl.ds(pl.multiple_of(ci * 8, 8), 8), :][0:1, :]
            st_ref[ci] = state[...].astype(BF16)
            for hd in range(GLA_HEADS):
                ks = slice(hd * GLA_HK, (hd + 1) * GLA_HK)
                vs = slice(hd * GLA_HV, (hd + 1) * GLA_HV)
                v = z_ref[rows, V0 + hd * GLA_HV:V0 + (hd + 1) * GLA_HV]
                st = state[vs, :] * e_all[:, ks] + _dot_tn(v, kdec_s[rows, ks])
                state[vs, :] = st
                q = z_ref[rows, Q0 + hd * GLA_HK:Q0 + (hd + 1) * GLA_HK]
                o_ref[rows, vs] = (_dot_nt(q, st.astype(BF16)) * scale).astype(BF16)
            return carry

        lax.fori_loop(0, cpt, chunk, 0, unroll=GLA_UNROLL)

        for hd in range(GLA_HEADS):
            vs = slice(hd * GLA_HV, (hd + 1) * GLA_HV)
            on = _rms(o_ref[:, vs].astype(F32), hg_ref[...])
            gv = z_ref[:, G0 + hd * GLA_HV:G0 + (hd + 1) * GLA_HV].astype(F32)
            mix_ref[:, vs] = (on * _silu(gv)).astype(BF16)

    return _call_after(
        dep, body, 4,
        [pl.BlockSpec((tm, GLA_IN_PAD), lambda i: (i, 0)),
         pl.BlockSpec((GATE_PAD, GLA_DK), lambda i: (0, 0)), pl.BlockSpec((1, GLA_DK), lambda i: (0, 0)),
         pl.BlockSpec((1, GLA_HV), lambda i: (0, 0))], (z, gate_w, gate_b, head_g), grid=(t // tm,),
        out_specs=[pl.BlockSpec((tm, GLA_DV), lambda i: (i, 0)), pl.BlockSpec((tm, GLA_DV), lambda i: (i, 0)),
                   pl.BlockSpec((cpt, GLA_DV, GLA_HK), lambda i: (i, 0, 0))],
        out_shape=[jax.ShapeDtypeStruct((t, GLA_DV), BF16), jax.ShapeDtypeStruct((t, GLA_DV), BF16),
                   jax.ShapeDtypeStruct((t // CHUNK, GLA_DV, GLA_HK), BF16)],
        scratch_shapes=[pltpu.VMEM((GLA_DV, GLA_HK), F32), pltpu.VMEM((tm, GLA_DK), BF16),
                        pltpu.VMEM((cpt * 8, GLA_DK), F32)],
        compiler_params=_cparams("arbitrary"), name="gla_seq_fwd")


def _gla_seq_bwd(dmix, o, z, states, gate_w, gate_b, head_g, dep=None):
    t = z.shape[0]
    tm = _row_tile(t, ROW_TILE_TARGET, CHUNK)
    cpt = tm // CHUNK
    nt = t // tm
    scale = GLA_HK ** -0.5

    def body(dmix_ref, o_ref, z_ref, st_ref, gw_ref, gb_ref, hg_ref, dz_ref, dgw_ref, dgb_ref, dhg_ref,
             dstate, dec_s, kdec_s, dkdec_s, do_s, e_s, dtot_s):
        @pl.when(pl.program_id(0) == 0)
        def _():
            dstate[...] = jnp.zeros_like(dstate)
            dgw_ref[...] = jnp.zeros_like(dgw_ref)
            dgb_ref[...] = jnp.zeros_like(dgb_ref)
            dhg_ref[...] = jnp.zeros_like(dhg_ref)

        cum3, tot3 = _chunk_decay(z_ref[:, R0:R0 + GATE_PAD], gw_ref, gb_ref, cpt)
        dec = jnp.exp(jnp.broadcast_to(tot3, cum3.shape) - cum3).reshape(tm, GLA_DK)
        dec_s[...] = dec
        kdec = z_ref[:, K0:K0 + GLA_DK].astype(F32) * dec
        kdec_s[...] = kdec
        e3 = jnp.exp(tot3)
        e_s[...] = jnp.broadcast_to(e3, (cpt, 8, GLA_DK)).reshape(cpt * 8, GLA_DK)
        dhg = jnp.zeros((1, GLA_HV), F32)
        for hd in range(GLA_HEADS):
            ks = slice(hd * GLA_HK, (hd + 1) * GLA_HK)
            vs = slice(hd * GLA_HV, (hd + 1) * GLA_HV)
            gcols = slice(G0 + hd * GLA_HV, G0 + (hd + 1) * GLA_HV)
            ov = o_ref[:, vs].astype(F32)
            gv = z_ref[:, gcols].astype(F32)
            dm = dmix_ref[:, vs].astype(F32)
            sg = jax.nn.sigmoid(gv)
            rr = lax.rsqrt(jnp.mean(ov * ov, axis=-1, keepdims=True) + EPS)
            xhat = ov * rr
            don = dm * (gv * sg)
            dz_ref[:, gcols] = (dm * (xhat * hg_ref[...]) * (sg * (1.0 + gv * (1.0 - sg)))).astype(BF16)
            dhg = dhg + jnp.sum(don * xhat, axis=0, keepdims=True)
            dxh = don * hg_ref[...]
            do = (rr * (dxh - xhat * jnp.mean(dxh * xhat, axis=-1, keepdims=True)) * scale).astype(BF16)
            do_s[:, vs] = do
            v3 = z_ref[:, V0 + hd * GLA_HV:V0 + (hd + 1) * GLA_HV].reshape(cpt, CHUNK, GLA_HV)
            kdb3 = kdec[:, ks].astype(BF16).reshape(cpt, CHUNK, GLA_HK)
            st3 = st_ref[:, vs, :].astype(F32) * e3[:, :, ks] + jnp.einsum("bcv,bck->bvk", v3, kdb3,
                                                                            preferred_element_type=F32)
            dq3 = jnp.einsum("bcv,bvk->bck", do.reshape(cpt, CHUNK, GLA_HV), st3.astype(BF16), preferred_element_type=F32)
            dz_ref[:, Q0 + hd * GLA_HK:Q0 + (hd + 1) * GLA_HK] = dq3.reshape(tm, GLA_HK).astype(BF16)
        dhg_ref[...] += dhg

        def chunk(cj, carry):
            ci = cpt - 1 - cj
            rows = pl.ds(pl.multiple_of(ci * CHUNK, CHUNK), CHUNK)
            erows = pl.ds(pl.multiple_of(ci * 8, 8), 8)
            e_all = e_s[erows, :][0:1, :]
            for hd in range(GLA_HEADS):
                ks = slice(hd * GLA_HK, (hd + 1) * GLA_HK)
                vs = slice(hd * GLA_HV, (hd + 1) * GLA_HV)
                e = e_all[:, ks]
                kdb = kdec_s[rows, ks].astype(BF16)
                v = z_ref[rows, V0 + hd * GLA_HV:V0 + (hd + 1) * GLA_HV]
                q = z_ref[rows, Q0 + hd * GLA_HK:Q0 + (hd + 1) * GLA_HK]
                do = do_s[rows, vs]
                st_prev = st_ref[ci, vs, :].astype(F32)
                dst = dstate[vs, :] + _dot_tn(do, q)
                dstb = dst.astype(BF16)
                dkdec_s[rows, ks] = _dot(v, dstb)
                dz_ref[rows, V0 + hd * GLA_HV:V0 + (hd + 1) * GLA_HV] = _dot_nt(kdb, dstb).astype(BF16)
                dtot = jnp.sum(dst * st_prev, axis=0, keepdims=True) * e
                dtot_s[erows, ks] = jnp.broadcast_to(dtot, (8, GLA_HK))
                dstate[vs, :] = dst * e
            return carry

        lax.fori_loop(0, cpt, chunk, 0, unroll=GLA_UNROLL)

        dkdec = dkdec_s[...]
        dz_ref[:, K0:K0 + GLA_DK] = (dkdec * dec_s[...]).astype(BF16)
        before = _chunk_sums(dkdec * kdec_s[...], cpt, True, 2)
        dtot3 = dtot_s[...].reshape(cpt, 8, GLA_DK)[:, 0:1, :]
        dlac = (jnp.broadcast_to(dtot3, before.shape) + before).reshape(tm, GLA_DK)
        pre = _dot(z_ref[:, R0:R0 + GATE_PAD], gw_ref[...]) + gb_ref[...]
        dpre = dlac * (1.0 / GATE_NORM) * (1.0 - jax.nn.sigmoid(pre))
        dpb = dpre.astype(BF16)
        dz_ref[:, R0:R0 + GATE_PAD] = _dot_nt(dpb, gw_ref[...]).astype(BF16)
        dgw_ref[...] += _dot_tn(z_ref[:, R0:R0 + GATE_PAD], dpb)
        dgb_ref[...] += jnp.sum(dpre, axis=0, keepdims=True)

    rev = lambda i: (nt - 1 - i, 0)
    return _call_after(
        dep, body, 7,
        [pl.BlockSpec((tm, GLA_DV), rev), pl.BlockSpec((tm, GLA_DV), rev), pl.BlockSpec((tm, GLA_IN_PAD), rev),
         pl.BlockSpec((cpt, GLA_DV, GLA_HK), lambda i: (nt - 1 - i, 0, 0)),
         pl.BlockSpec((GATE_PAD, GLA_DK), lambda i: (0, 0)), pl.BlockSpec((1, GLA_DK), lambda i: (0, 0)),
         pl.BlockSpec((1, GLA_HV), lambda i: (0, 0))],
        (dmix, o, z, states, gate_w, gate_b, head_g), grid=(nt,),
        out_specs=[pl.BlockSpec((tm, GLA_IN_PAD), rev), pl.BlockSpec((GATE_PAD, GLA_DK), lambda i: (0, 0)),
                   pl.BlockSpec((1, GLA_DK), lambda i: (0, 0)), pl.BlockSpec((1, GLA_HV), lambda i: (0, 0))],
        out_shape=[jax.ShapeDtypeStruct((t, GLA_IN_PAD), BF16), jax.ShapeDtypeStruct((GATE_PAD, GLA_DK), F32),
                   jax.ShapeDtypeStruct((1, GLA_DK), F32), jax.ShapeDtypeStruct((1, GLA_HV), F32)],
        scratch_shapes=[pltpu.VMEM((GLA_DV, GLA_HK), F32), pltpu.VMEM((tm, GLA_DK), F32), pltpu.VMEM((tm, GLA_DK), F32),
                        pltpu.VMEM((tm, GLA_DK), F32), pltpu.VMEM((tm, GLA_DV), BF16),
                        pltpu.VMEM((cpt * 8, GLA_DK), F32), pltpu.VMEM((cpt * 8, GLA_DK), F32)],
        compiler_params=_cparams("arbitrary"), name="gla_seq_bwd")


def _sum_slots(x, name):
    n, r, cdim = x.shape
    tr = _row_tile(r, 256, 8)

    def body(x_ref, o_ref):
        acc = x_ref[0].astype(F32)
        for j in range(1, n):
            acc = acc + x_ref[j].astype(F32)
        o_ref[...] = acc

    return pl.pallas_call(
        body, grid=(r // tr,),
        in_specs=[pl.BlockSpec((n, tr, cdim), lambda i: (0, i, 0))],
        out_specs=pl.BlockSpec((tr, cdim), lambda i: (i, 0)),
        out_shape=jax.ShapeDtypeStruct((r, cdim), F32),
        compiler_params=_cparams("parallel"), name=name)(x)


def _sum_own_and_slots(own, slots, dev_idx, name):
    _, _, r, cdim = own.shape
    n = slots.shape[0]
    tr = _row_tile(r, 256, 8)

    def body(s_ref, own_ref, *rest):
        acc = own_ref[...].astype(F32)
        for other in rest[:n - 1]:
            acc = acc + other[...].astype(F32)
        rest[n - 1][...] = acc

    def slot(dd):
        return pl.BlockSpec((None, tr, cdim), lambda i, s: ((s[0] + dd) % n, i, 0))

    mine = pl.BlockSpec((None, None, tr, cdim), lambda i, s: (s[0] // 2, s[0] % 2, i, 0))
    return pl.pallas_call(
        body,
        grid_spec=pltpu.PrefetchScalarGridSpec(
            num_scalar_prefetch=1, grid=(r // tr,), in_specs=[mine] + [slot(dd) for dd in range(1, n)],
            out_specs=pl.BlockSpec((tr, cdim), lambda i, s: (i, 0))),
        out_shape=jax.ShapeDtypeStruct((r, cdim), F32),
        compiler_params=_cparams("parallel"), name=name)(dev_idx, own, *([slots] * (n - 1)))


def _add2(a, b, name):
    r, cdim = a.shape
    tr = _row_tile(r, 256, 8)

    def body(a_ref, b_ref, o_ref):
        o_ref[...] = a_ref[...] + b_ref[...]

    spec = pl.BlockSpec((tr, cdim), lambda i: (i, 0))
    return pl.pallas_call(body, grid=(r // tr,), in_specs=[spec, spec], out_specs=spec,
                          out_shape=jax.ShapeDtypeStruct((r, cdim), F32),
                          compiler_params=_cparams("parallel"), name=name)(a, b)


def _adamw_half(w, gs, m, v, half_idx, prev, name, dep=None):
    nl, _, h, cdim = w.shape
    tr = _row_tile(h, 256, 8)
    nprev = 0 if prev is None else 4
    extra = [] if dep is None else [dep]

    def body(s_ref, w_ref, m_ref, v_ref, *rest):
        g_refs = rest[:nl]
        go_ref, d_ref, mo_ref, vo_ref = rest[nl + nprev + len(extra):]
        layer = pl.program_id(0)
        gv = g_refs[0][...]
        for j in range(1, nl):
            gv = jnp.where(layer == j, g_refs[j][...], gv)
        go_ref[...] = gv
        mn = ADAM_B1 * m_ref[...] + (1.0 - ADAM_B1) * gv
        vn = ADAM_B2 * v_ref[...] + (1.0 - ADAM_B2) * (gv * gv)
        m_hat = mn / (1.0 - ADAM_B1 ** ADAM_STEP)
        v_hat = vn / (1.0 - ADAM_B2 ** ADAM_STEP)
        d_ref[...] = -ADAM_LR * (m_hat / (jnp.sqrt(v_hat) + ADAM_EPS) + ADAM_WD * w_ref[...])
        mo_ref[...] = mn
        vo_ref[...] = vn

    half = pl.BlockSpec((None, None, tr, cdim), lambda l, i, s: (l, s[0], i, 0))

    def of_layer(j):
        return pl.BlockSpec((tr, cdim), lambda l, i, s: (jnp.where(l == j, i, 0), 0))

    shp = jax.ShapeDtypeStruct(w.shape, F32)
    return pl.pallas_call(
        body,
        grid_spec=pltpu.PrefetchScalarGridSpec(
            num_scalar_prefetch=1, grid=(nl, h // tr),
            in_specs=[half] * 3 + [of_layer(j) for j in range(nl)] + [ANY_SPEC] * (nprev + len(extra)),
            out_specs=[half] * 4),
        out_shape=[shp] * 4, input_output_aliases={4 + nl + k: k for k in range(nprev)},
        compiler_params=_cparams("arbitrary", "arbitrary"), name=name,
    )(half_idx, w, m, v, *gs, *([] if prev is None else prev), *extra)


def _adamw_many(ws, gs, ms, vs):
    n = len(ws)

    def body(*refs):
        for i in range(n):
            w_ref, g_ref, m_ref, v_ref = refs[i], refs[n + i], refs[2 * n + i], refs[3 * n + i]
            d_ref, mo_ref, vo_ref = refs[4 * n + i], refs[5 * n + i], refs[6 * n + i]
            gv = g_ref[...]
            mn = ADAM_B1 * m_ref[...] + (1.0 - ADAM_B1) * gv
            vn = ADAM_B2 * v_ref[...] + (1.0 - ADAM_B2) * (gv * gv)
            m_hat = mn / (1.0 - ADAM_B1 ** ADAM_STEP)
            v_hat = vn / (1.0 - ADAM_B2 ** ADAM_STEP)
            d_ref[...] = -ADAM_LR * (m_hat / (jnp.sqrt(v_hat) + ADAM_EPS) + ADAM_WD * w_ref[...])
            mo_ref[...] = mn
            vo_ref[...] = vn

    shapes = [jax.ShapeDtypeStruct(w.shape, F32) for w in ws]
    outs = pl.pallas_call(body, out_shape=shapes * 3, name="adamw_small")(*ws, *gs, *ms, *vs)
    return outs[:n], outs[n:2 * n], outs[2 * n:]


def _split_rows(a):
    return a.reshape(a.shape[0], 2, a.shape[1] // 2, a.shape[2])


def _place():
    x, y, c = lax.axis_index("x"), lax.axis_index("y"), lax.axis_index("c")
    chips = [(1 - x, y), (x, 1 - y), (1 - x, 1 - y)]
    return x, y, c, chips


def _remote(src, dst, send_sem, recv_sem, to):
    return pltpu.make_async_remote_copy(src_ref=src, dst_ref=dst, send_sem=send_sem, recv_sem=recv_sem,
                                        device_id=to, device_id_type=MESH)


def _plan_gather(n_halved):
    def plan(src_refs, land_refs):
        x, y, c, chips = _place()
        me = 2 * x + y
        copies = []
        for k, (src, land) in enumerate(zip(src_refs, land_refs)):
            for (px, py) in chips:
                frm = 2 * px + py
                if k < n_halved:
                    copies.append((src.at[c], land.at[me, c], (px, py, c), land.at[frm, c]))
                else:
                    copies.append((src, land.at[me], (px, py, c), land.at[frm]))
        return copies
    return plan


def _plan_share(src_refs, land_refs):
    x, y, c, chips = _place()
    me = 2 * x + y
    sib = (x, y, 1 - c)
    copies = []
    for src, land in zip(src_refs, land_refs):
        copies.append((src, land.at[me], sib, land.at[me]))
        for (px, py) in chips:
            frm = 2 * px + py
            copies.append((land.at[frm, c], land.at[frm, c], sib, land.at[frm, 1 - c]))
    return copies


def _plan_scatter(n_parts):
    def plan(src_refs, land_refs):
        x, y, c, chips = _place()
        me = 2 * x + y
        copies = []
        for k, (src, land) in enumerate(zip(src_refs, land_refs)):
            for (px, py) in chips:
                to = 2 * px + py
                copies.append((src.at[to] if k < n_parts else src, land.at[me], (px, py, c), land.at[to]))
        return copies
    return plan


N_DEVICES = 8
OTHER_DEVICES = [(dx, dy, dc) for dx in (0, 1) for dy in (0, 1) for dc in (0, 1) if dx or dy or dc]


def _plan_scatter_all(src_refs, land_refs):
    x, y, c, _ = _place()
    me = 4 * x + 2 * y + c
    copies = []
    for src, land in zip(src_refs, land_refs):
        for dx, dy, dc in OTHER_DEVICES:
            px, py, pc = (1 - x if dx else x), (1 - y if dy else y), (1 - c if dc else c)
            copies.append((src.at[2 * px + py, pc], land.at[me], (px, py, pc), land.at[4 * px + 2 * py + pc]))
    return copies


def _plan_exchange(n_split):
    def plan(src_refs, land_refs):
        x, y, c, _ = _place()
        sib = (x, y, 1 - c)
        return [(src.at[:, 1 - c] if k < n_split else src, land, sib, land)
                for k, (src, land) in enumerate(zip(src_refs, land_refs))]
    return plan


def _hbm(a):
    return pltpu.HBM(a.shape, a.dtype)


def _start_copies(name, srcs, lands, plan, ncopy, dep=None):
    ns, nl = len(srcs), len(lands)
    nin = ns + nl + (0 if dep is None else 1)

    def body(*refs):
        send_sems, recv_sems, token = refs[nin], refs[nin + 1], refs[-1]
        for k, (src, dst, dev, _) in enumerate(plan(refs[:ns], refs[ns:ns + nl])):
            _remote(src, dst, send_sems.at[k], recv_sems.at[k], dev).start()
        token[...] = jnp.zeros_like(token)

    args = [pltpu.with_memory_space_constraint(a, pltpu.HBM) for a in list(srcs) + list(lands)]
    outs = pl.pallas_call(
        body, name=name,
        out_shape=(pltpu.SemaphoreType.DMA((ncopy,)), pltpu.SemaphoreType.DMA((ncopy,)),
                   *[_hbm(a) for a in list(srcs) + list(lands)], jax.ShapeDtypeStruct((8, 128), F32)),
        in_specs=[HBM_SPEC] * (ns + nl) + ([] if dep is None else [ANY_SPEC]),
        out_specs=(SEM_SPEC, SEM_SPEC, *([HBM_SPEC] * (ns + nl)), pl.BlockSpec(memory_space=pltpu.VMEM)),
        input_output_aliases={i: 2 + i for i in range(ns + nl)},
        compiler_params=pltpu.CompilerParams(has_side_effects=SIDE_EFFECT),
    )(*args, *([] if dep is None else [dep]))
    return outs[0], outs[1], list(outs[2:2 + ns]), list(outs[2 + ns:2 + ns + nl]), outs[-1]


def _wait_copies(name, started, plan, after, sem_offset=0):
    send_sems, recv_sems, srcs, lands, _ = started
    ns, nl = len(srcs), len(lands)
    after = list(after) if isinstance(after, (list, tuple)) else [after]

    def body(*refs):
        send_ref, recv_ref = refs[ns + nl], refs[ns + nl + 1]
        for k, (src, _, dev, mine) in enumerate(plan(refs[:ns], refs[ns:ns + nl])):
            copy = _remote(src, mine, send_ref.at[sem_offset + k], recv_ref.at[sem_offset + k], dev)
            copy.wait_send()
            copy.wait_recv()

    outs = pl.pallas_call(
        body, name=name, out_shape=tuple(_hbm(a) for a in srcs + lands),
        in_specs=[HBM_SPEC] * (ns + nl) + [SEM_SPEC, SEM_SPEC] + [ANY_SPEC] * len(after),
        out_specs=tuple([HBM_SPEC] * (ns + nl)),
        input_output_aliases={i: i for i in range(ns + nl)},
        compiler_params=pltpu.CompilerParams(has_side_effects=SIDE_EFFECT),
    )(*srcs, *lands, send_sems, recv_sems, *after)
    return list(outs[:ns]), list(outs[ns:])


def _share_with_sibling(name, srcs, lands):
    n = len(srcs)

    def body(*refs):
        src_refs, land_refs, out_refs = refs[:n], refs[n:2 * n], refs[2 * n:3 * n]
        send_sem, recv_sem = refs[3 * n:]
        x, y, c, chips = _place()
        me = 2 * x + y
        sib = (x, y, 1 - c)
        sends, recvs = [], []
        for k in range(n):
            sems = (send_sem.at[4 * k], recv_sem.at[4 * k])
            sends.append(_remote(src_refs[k], out_refs[k].at[me], *sems, sib))
            recvs.append(_remote(src_refs[k], out_refs[k].at[me], *sems, sib))
            for j, (px, py) in enumerate(chips):
                frm = 2 * px + py
                sems = (send_sem.at[4 * k + 1 + j], recv_sem.at[4 * k + 1 + j])
                sends.append(_remote(land_refs[k].at[frm, c], out_refs[k].at[frm, c], *sems, sib))
                recvs.append(_remote(land_refs[k].at[frm, c], out_refs[k].at[frm, 1 - c], *sems, sib))
        for cp in sends:
            cp.start()
        for cp in recvs:
            cp.wait_recv()
        for cp in sends:
            cp.wait_send()

    return pl.pallas_call(
        body, name=name, in_specs=[HBM_SPEC] * (2 * n), out_specs=[HBM_SPEC] * n,
        out_shape=[jax.ShapeDtypeStruct(a.shape, a.dtype) for a in lands],
        input_output_aliases={n + k: k for k in range(n)},
        scratch_shapes=[pltpu.SemaphoreType.DMA((4 * n,)), pltpu.SemaphoreType.DMA((4 * n,))],
    )(*srcs, *lands)


def _pack(arrs):
    flat = jnp.concatenate([a.reshape(-1).astype(F32) for a in arrs])
    n = flat.shape[0]
    rows = -(-n // PACK_WIDTH)
    rows = -(-rows // 8) * 8
    return jnp.pad(flat, (0, rows * PACK_WIDTH - n)).reshape(rows, PACK_WIDTH)


def _unpack(buf, shapes):
    flat = buf.reshape(-1)
    out, off = [], 0
    for shp in shapes:
        n = 1
        for s in shp:
            n *= s
        out.append(flat[off:off + n].reshape(shp))
        off += n
    return out


def _unshard_cols(stacked):
    moved = jnp.moveaxis(stacked, 0, -2)
    return moved.reshape(moved.shape[:-2] + (moved.shape[-2] * moved.shape[-1],))


def _take_cols(blocks, start, width):
    bw = blocks.shape[2]
    pieces, lo = [], start
    while lo < start + width:
        b = lo // bw
        hi = min(start + width, (b + 1) * bw)
        pieces.append(blocks[b][:, lo - b * bw:hi - b * bw])
        lo = hi
    return jnp.concatenate(pieces, axis=1)


def _col_shard(full, s, width):
    return lax.dynamic_slice_in_dim(full, s * width, width, axis=full.ndim - 1)


def kernel(x, meta_tokens, mix_norm_g, ffn_norm_g, ffn_w1, ffn_w2, cp_w_in, cp_conv_w, cp_conv_b, cp_ln_g, cp_ln_b, cp_pool_w, cp_pool_scale, cp_w_out, gla_w_in, gla_gate_w2, gla_gate_b, gla_head_g, gla_w_out, final_norm_g, loss_target, m_meta_tokens, m_mix_norm_g, m_ffn_norm_g, m_ffn_w1, m_ffn_w2, m_cp_w_in, m_cp_conv_w, m_cp_conv_b, m_cp_ln_g, m_cp_ln_b, m_cp_pool_w, m_cp_pool_scale, m_cp_w_out, m_gla_w_in, m_gla_gate_w2, m_gla_gate_b, m_gla_head_g, m_gla_w_out, m_final_norm_g, v_meta_tokens, v_mix_norm_g, v_ffn_norm_g, v_ffn_w1, v_ffn_w2, v_cp_w_in, v_cp_conv_w, v_cp_conv_b, v_cp_ln_g, v_cp_ln_b, v_cp_pool_w, v_cp_pool_scale, v_cp_w_out, v_gla_w_in, v_gla_gate_w2, v_gla_gate_b, v_gla_head_g, v_gla_w_out, v_final_norm_g):
    d = D_MODEL
    chip = 2 * lax.axis_index("x") + lax.axis_index("y")
    core = lax.axis_index("c")
    seq = x.shape[1]
    t = seq + CHUNK

    sharded_small = [meta_tokens, cp_conv_w, gla_gate_w2, gla_gate_b, gla_head_g]

    def halves(w):
        return w.astype(BF16).reshape(2, w.shape[0] // 2, w.shape[1])

    def unhalve(g):
        return g.reshape(N_CHIPS, 2 * g.shape[2], g.shape[3])

    def gather_group(srcs, whole=()):
        lands = [lax.empty((N_CHIPS,) + s.shape, s.dtype) for s in srcs]
        for a in whole:
            lands.append(lax.dynamic_update_slice(jnp.zeros((N_CHIPS,) + a.shape, a.dtype), a[None], (chip,) + (0,) * a.ndim))
        return list(srcs) + list(whole), lands, _plan_gather(len(srcs)), len(srcs)

    groups = [gather_group([halves(cp_w_in[0]), halves(cp_w_out[0])], [_pack(sharded_small)]),
              gather_group([halves(ffn_w1[0]), halves(ffn_w2[0])]), gather_group([halves(gla_w_in[0]), halves(gla_w_out[0])]),
              gather_group([halves(ffn_w1[1]), halves(ffn_w2[1])])]
    bounds, all_srcs, all_lands = [], [], []
    for srcs, lands, _, _ in groups:
        bounds.append((len(all_srcs), len(all_srcs) + len(srcs)))
        all_srcs += srcs
        all_lands += lands

    def plan_all(src_refs, land_refs):
        return [cp for (lo, hi), group in zip(bounds, groups) for cp in group[2](src_refs[lo:hi], land_refs[lo:hi])]

    gathers = _start_copies("gather_start", all_srcs, all_lands, plan_all, 3 * len(all_srcs))

    def arrived(name, gi, after):
        (lo, hi), plan, n = bounds[gi], groups[gi][2], groups[gi][3]
        mine = (gathers[0], gathers[1], gathers[2][lo:hi], gathers[3][lo:hi], gathers[4])
        srcs, lands = _wait_copies(name + "_wait", mine, plan, after, sem_offset=3 * lo)
        return srcs[:n], lands[:n], lands[n:]

    cp_gather, ffn0_gather, gla_gather, ffn1_gather = 0, 1, 2, 3
    h0_rows = jnp.concatenate([jnp.zeros((CHUNK, d), F32) + gathers[4][0, 0], x[0]], axis=0)
    cp_srcs, cp_lands, (small_g,) = arrived("gather_cp", cp_gather, h0_rows)
    cpin_g, cpout_g = [unhalve(g) for g in _share_with_sibling("gather_cp_share", cp_srcs, cp_lands)]
    per_chip = [_unpack(small_g[j], [a.shape for a in sharded_small]) for j in range(N_CHIPS)]
    meta_f, conv_w_f, gate_w_f, gate_b_f, head_g_f = [
        jnp.concatenate([per_chip[j][i] for j in range(N_CHIPS)], axis=-1) for i in range(len(sharded_small))]
    conv_w_f, gate_w_f = conv_w_f[0], gate_w_f[0]
    w_cp_in = _unshard_cols(cpin_g)
    w_cp_out = cpout_g.reshape(CONV_DIM + POOL_DIM, d)
    gate_w_pad = jnp.pad(gate_w_f, ((0, GATE_PAD - GATE_RANK), (0, 0))).astype(BF16)
    row = lambda a: a.reshape(1, -1)
    c_idx = core.reshape(1).astype(jnp.int32)

    h0 = lax.dynamic_update_slice(h0_rows, meta_f, (PAD_ROWS, 0))
    z0, u0 = _norm_matmul(h0, row(mix_norm_g[0]), w_cp_in, 512, "cp_in_proj")
    c0, pm0, mix0 = _cp_seq_fwd(z0, conv_w_f, cp_conv_b, cp_ln_g, cp_ln_b, cp_pool_w[0], cp_pool_scale)
    ffn0_srcs, ffn0_lands, _ = arrived("gather_ffn0", ffn0_gather, mix0)
    ffn0_share = _start_copies("gather_ffn0_share_start", ffn0_srcs, ffn0_lands, _plan_share, 4 * len(ffn0_srcs))
    h1 = _matmul_residual(mix0, w_cp_out, h0, "cp_out_proj", dep=ffn0_share[-1])
    w1g0, w2g0 = [unhalve(g) for g in _wait_copies("gather_ffn0_share_wait", ffn0_share, _plan_share, h1)[1]]
    h2, hp0, uf0 = _ffn_fwd(h1, row(ffn_norm_g[0]), w1g0, w2g0, "ffn0_fwd")
    gla_srcs, gla_lands, _ = arrived("gather_gla", gla_gather, h2)
    glain_g, glaout_g = [unhalve(g) for g in _share_with_sibling("gather_gla_share", gla_srcs, gla_lands)]
    w_gla_in = jnp.concatenate([glain_g[j] for j in range(N_CHIPS)] + [jnp.zeros((d, GLA_IN_PAD - GLA_IN), BF16)], axis=1)
    w_gla_out = glaout_g.reshape(GLA_DV, d)
    z1, u2 = _norm_matmul(h2, row(mix_norm_g[1]), w_gla_in, GLA_COLS, "gla_in_proj")
    ffn1_srcs, ffn1_lands, _ = arrived("gather_ffn1", ffn1_gather, z1)
    ffn1_share = _start_copies("gather_ffn1_share_start", ffn1_srcs, ffn1_lands, _plan_share, 4 * len(ffn1_srcs))
    o1, mix1, states = _gla_seq_fwd(z1, gate_w_pad, gate_b_f, head_g_f, dep=ffn1_share[-1])
    h3 = _matmul_residual(mix1, w_gla_out, h2, "gla_out_proj")
    w1g1, w2g1 = [unhalve(g) for g in _wait_copies("gather_ffn1_share_wait", ffn1_share, _plan_share, h3)[1]]
    h4, hp1, uf1 = _ffn_fwd(h3, row(ffn_norm_g[1]), w1g1, w2g1, "ffn1_fwd")

    dev_idx = (2 * chip + core).reshape(1).astype(jnp.int32)

    def start_reduce(name, grads):
        srcs = [_split_rows(g) for g in grads]
        lands = [lax.empty((N_DEVICES,) + s.shape[2:], s.dtype) for s in srcs]
        return _start_copies(name + "_scatter_start", srcs, lands, _plan_scatter_all, len(OTHER_DEVICES) * len(srcs))

    def finish_reduce(name, started, after):
        srcs, lands = _wait_copies(name + "_scatter_wait", started, _plan_scatter_all, after)
        return [_sum_own_and_slots(s, l, dev_idx, "%s_slot_sum_%d" % (name, k)) for k, (s, l) in enumerate(zip(srcs, lands))]

    dh4, d_final_g, loss_part = _loss_bwd(h4, row(final_norm_g), loss_target[0])

    dh3, dhp1, d_ffn_g1 = _ffn_bwd_data(dh4, h3, row(ffn_norm_g[1]), hp1, w1g1, w2g1, "ffn1_bwd")
    dw1_1 = _wgrad(uf1, dhp1, N_CHIPS, d, d, False, True, False, "ffn1_dw1")
    dw2_1 = _wgrad(hp1, dh4, N_CHIPS, d, d, True, False, True, "ffn1_dw2")
    ffn1_reduce = start_reduce("ffn1", [dw1_1, dw2_1])

    dmix1 = _dgrad(dh3, w_gla_out, "gla_out_dgrad", dep=ffn1_reduce[-1])
    dw_gla_out = _wgrad(mix1, dh3, 1, GLA_DV, d, False, False, False, "gla_out_dw")
    dz1, d_gate_w, d_gate_b, d_head_g = _gla_seq_bwd(dmix1, o1, z1, states, gate_w_pad, gate_b_f, head_g_f)
    dh2, d_mix_g1 = _dgrad_norm_bwd(dz1, w_gla_in, h2, row(mix_norm_g[1]), dh3, GLA_COLS, "gla_in_dgrad")
    dw_gla_in = _wgrad(u2, dz1, GLA_IN_PAD // 640, d, 640, False, True, False, "gla_in_dw")
    gla_in_shards = jnp.stack([_take_cols(dw_gla_in, j * (GLA_IN // N_CHIPS), GLA_IN // N_CHIPS) for j in range(N_CHIPS)])
    gla_reduce = start_reduce("gla", [gla_in_shards, dw_gla_out.reshape(N_CHIPS, -1, d)])

    dh1, dhp0, d_ffn_g0 = _ffn_bwd_data(dh2, h1, row(ffn_norm_g[0]), hp0, w1g0, w2g0, "ffn0_bwd", dep=gla_reduce[-1])
    dw1_0 = _wgrad(uf0, dhp0, N_CHIPS, d, d, False, True, False, "ffn0_dw1")
    dw2_0 = _wgrad(hp0, dh2, N_CHIPS, d, d, True, False, True, "ffn0_dw2")
    ffn0_reduce = start_reduce("ffn0", [dw1_0, dw2_0])

    dmix0 = _dgrad(dh1, w_cp_out, "cp_out_dgrad", dep=ffn0_reduce[-1])
    dw_cp_out = _wgrad(mix0, dh1, 1, CONV_DIM + POOL_DIM, d, False, False, False, "cp_out_dw")
    dz0, d_conv_w, d_cp_vec, d_pool_w = _cp_seq_bwd(dmix0, z0, c0, pm0, conv_w_f, cp_ln_g, cp_ln_b, cp_pool_w[0],
                                                    cp_pool_scale)
    grad_x, dh0_head, d_mix_g0 = _dgrad_norm_bwd_input(dz0, w_cp_in, h0, row(mix_norm_g[0]), dh1, 512, "cp_in_dgrad")
    grad_x = grad_x[None]
    dw_cp_in = _wgrad(u0, dz0, 1, d, CP_IN, False, False, False, "cp_in_dw")
    dw_cp_in = jnp.stack([_take_cols(dw_cp_in, j * (CP_IN // N_CHIPS), CP_IN // N_CHIPS) for j in range(N_CHIPS)])

    cp_reduce = start_reduce("cp", [dw_cp_in, dw_cp_out.reshape(N_CHIPS, -1, d)])
    small_full = [dh0_head[PAD_ROWS:CHUNK],jnp.concatenate([d_mix_g0, d_mix_g1], axis=0),
                  jnp.concatenate([d_ffn_g0, d_ffn_g1], axis=0), d_conv_w[:CONV_WIDTH][None],
                  d_cp_vec[0:1], d_cp_vec[1:2], d_cp_vec[2:3], d_pool_w[None], d_cp_vec[3:4],
                  d_gate_w[:GATE_RANK][None], d_gate_b, d_head_g, d_final_g[0], loss_part[0, 0:1]]
    small_mine = _pack(small_full)
    whole = _plan_exchange(0)
    small_exchange = _start_copies("small_exchange_start", [small_mine], [lax.empty(small_mine.shape, F32)], whole, 1,
                                   dep=cp_reduce[-1])
    red_ffn1 = finish_reduce("ffn1", ffn1_reduce, small_exchange[-1])
    red_gla = finish_reduce("gla", gla_reduce, small_exchange[-1])
    (small_sent,), (small_recv,) = _wait_copies("small_exchange_wait", small_exchange, whole, [red_ffn1[1], red_gla[1]])
    small_chip = _add2(small_sent, small_recv, "chip_sum_small")
    small_slots = lax.dynamic_update_slice(jnp.zeros((N_CHIPS,) + small_chip.shape, F32), small_chip[None], (chip, 0, 0))
    small_reduce = _start_copies("small_scatter_start", [small_chip], [small_slots], _plan_scatter(0), 3)

    big = {"w1": (ffn_w1, m_ffn_w1, v_ffn_w1), "w2": (ffn_w2, m_ffn_w2, v_ffn_w2),
           "cp_in": (cp_w_in, m_cp_w_in, v_cp_w_in), "cp_out": (cp_w_out, m_cp_w_out, v_cp_w_out),
           "gla_in": (gla_w_in, m_gla_w_in, v_gla_w_in), "gla_out": (gla_w_out, m_gla_w_out, v_gla_w_out)}
    other_idx = (1 - core).reshape(1).astype(jnp.int32)

    def adamw_by_halves(tag, reduced, dep=None):
        flat = [r for n in reduced for r in reduced[n]]
        join_plan = _plan_exchange(0)
        join = _start_copies(tag + "_join_start", flat, [lax.empty(r.shape, F32) for r in flat], join_plan, len(flat),
                             dep=dep)
        views = {n: [_split_rows(a) for a in big[n]] for n in reduced}
        own, k = {}, 0
        for n in reduced:
            mine = join[2][k:k + len(reduced[n])]
            k += len(reduced[n])
            own[n] = _adamw_half(views[n][0], mine, views[n][1], views[n][2], c_idx, None, "adamw_%s_own" % n)
        _, arrived_halves = _wait_copies(tag + "_join_wait", join, join_plan, [own[n][1] for n in reduced])
        outs, k = {}, 0
        for n in reduced:
            theirs = arrived_halves[k:k + len(reduced[n])]
            k += len(reduced[n])
            res = _adamw_half(views[n][0], theirs, views[n][1], views[n][2], other_idx, own[n], "adamw_%s_sibling" % n)
            outs[n] = [o.reshape(big[n][0].shape) for o in res]
        return outs

    big_out = adamw_by_halves("gla", {"gla_in": [red_gla[0]], "gla_out": [red_gla[1]]}, dep=small_reduce[-1])
    red_ffn0 = finish_reduce("ffn0", ffn0_reduce, big_out["gla_out"][1])
    big_out.update(adamw_by_halves("ffn", {"w1": [red_ffn0[0], red_ffn1[0]], "w2": [red_ffn0[1], red_ffn1[1]]}))
    red_cp = finish_reduce("cp", cp_reduce, big_out["w2"][1])
    _, (small_landed,) = _wait_copies("small_scatter_wait", small_reduce, _plan_scatter(0), big_out["w2"][1])
    small_red = _sum_slots(small_landed, "slot_sum_small")
    big_out.update(adamw_by_halves("cp", {"cp_in": [red_cp[0]], "cp_out": [red_cp[1]]}))

    (g_meta, g_mix, g_ffn, g_conv_w, g_conv_b, g_ln_g, g_ln_b, g_pool_w, g_pool_scale, g_gate_w, g_gate_b, g_head,
     g_final, loss_sum) = _unpack(small_red, [a.shape for a in small_full])
    g_meta = _col_shard(g_meta, chip, meta_tokens.shape[-1])
    g_conv_w = _col_shard(g_conv_w, chip, cp_conv_w.shape[-1])
    g_gate_w = _col_shard(g_gate_w, chip, gla_gate_w2.shape[-1])
    g_gate_b = _col_shard(g_gate_b, chip, gla_gate_b.shape[-1])
    g_head = _col_shard(g_head, chip, gla_head_g.shape[-1])
    small_w = [meta_tokens, mix_norm_g, ffn_norm_g, cp_conv_w, cp_conv_b, cp_ln_g, cp_ln_b, cp_pool_w, cp_pool_scale,
               gla_gate_w2, gla_gate_b, gla_head_g, final_norm_g]
    small_m = [m_meta_tokens, m_mix_norm_g, m_ffn_norm_g, m_cp_conv_w, m_cp_conv_b, m_cp_ln_g, m_cp_ln_b, m_cp_pool_w,
               m_cp_pool_scale, m_gla_gate_w2, m_gla_gate_b, m_gla_head_g, m_final_norm_g]
    small_v = [v_meta_tokens, v_mix_norm_g, v_ffn_norm_g, v_cp_conv_w, v_cp_conv_b, v_cp_ln_g, v_cp_ln_b, v_cp_pool_w,
               v_cp_pool_scale, v_gla_gate_w2, v_gla_gate_b, v_gla_head_g, v_final_norm_g]
    small_g = [g_meta, g_mix, g_ffn, g_conv_w, g_conv_b, g_ln_g, g_ln_b, g_pool_w, g_pool_scale, g_gate_w, g_gate_b,
               g_head, g_final]
    shapes = [w.shape for w in small_w]
    small_g = [g.reshape(s) for g, s in zip(small_g, shapes)]
    at_least_2d = lambda arrs: [a.reshape(1, -1) if a.ndim == 1 else a for a in arrs]
    s_delta, s_m, s_v = _adamw_many(at_least_2d(small_w), at_least_2d(small_g), at_least_2d(small_m), at_least_2d(small_v))
    s_delta, s_m, s_v = [[a.reshape(s) for a, s in zip(group, shapes)] for group in (s_delta, s_m, s_v)]

    order = ["meta", "mix", "ffn", "w1", "w2", "cp_in", "conv_w", "conv_b", "ln_g", "ln_b", "pool_w", "pool_scale",
             "cp_out", "gla_in", "gate_w", "gate_b", "head", "gla_out", "final"]
    small_names = ["meta", "mix", "ffn", "conv_w", "conv_b", "ln_g", "ln_b", "pool_w", "pool_scale", "gate_w", "gate_b",
                   "head", "final"]
    big_names = ["w1", "w2", "cp_in", "cp_out", "gla_in", "gla_out"]
    table = {n: (small_g[i], s_delta[i], s_m[i], s_v[i]) for i, n in enumerate(small_names)}
    table.update({n: tuple(big_out[n]) for n in big_names})
    loss = loss_sum.reshape(())
    return (loss, grad_x, *[table[n][0] for n in order], *[table[n][1] for n in order],
            *[table[n][2] for n in order], *[table[n][3] for n in order])
```

```python
import functools

import jax
import jax.numpy as jnp
from jax import lax
from jax.experimental import pallas as pl
from jax.experimental.pallas import tpu as pltpu

F32 = jnp.float32
BF16 = jnp.bfloat16

D_MODEL = 1024
N_META = 16
CHUNK = 64
PAD_ROWS = CHUNK - N_META
EPS = 1e-5
CONV_DIM = 512
CONV_WIDTH = 31
CONV_HALO = 32
POOL_DIM = 512
POOL_WINDOWS = (2, 4, 8, 16)
POOL_GROUP = 128
POOL_HALO = 16
CP_IN = 2 * CONV_DIM + POOL_DIM
GLA_HEADS = 4
GLA_DK = 512
GLA_DV = 1024
GLA_HK = GLA_DK // GLA_HEADS
GLA_HV = GLA_DV // GLA_HEADS
GATE_RANK = 16
GATE_PAD = 128
GATE_NORM = 16.0
GLA_IN = 2 * GLA_DK + 2 * GLA_DV + GATE_RANK
GLA_IN_PAD = 2 * GLA_DK + 2 * GLA_DV + GATE_PAD
GLA_COLS = 1280
N_CHIPS = 4
ADAM_LR = 0.001
ADAM_B1 = 0.9
ADAM_B2 = 0.999
ADAM_EPS = 1e-08
ADAM_WD = 0.01
ADAM_STEP = 10

VMEM_LIMIT_BYTES = 56 * 1024 * 1024
ROW_TILE_TARGET = 832
TOKEN_TILE_TARGET = 1040
PACK_WIDTH = 1024
MESH = pl.DeviceIdType.MESH
HBM_SPEC = pl.BlockSpec(memory_space=pltpu.HBM)
ANY_SPEC = pl.BlockSpec(memory_space=pl.ANY)
SEM_SPEC = pl.BlockSpec(memory_space=pltpu.SEMAPHORE)
SIDE_EFFECT = pltpu.SideEffectType.DATAFLOW_SIDE_EFFECTING


def _cparams(*sem):
    return pltpu.CompilerParams(dimension_semantics=sem, vmem_limit_bytes=VMEM_LIMIT_BYTES)


def _row_tile(t, target, mult):
    best = mult
    for cand in range(mult, min(t, target) + 1, mult):
        if t % cand == 0:
            best = cand
    assert t % best == 0, (t, best)
    return best


def _rms(h, g):
    return h * lax.rsqrt(jnp.mean(h * h, axis=-1, keepdims=True) + EPS) * g


def _rms_bwd(h, g, du):
    r = lax.rsqrt(jnp.mean(h * h, axis=-1, keepdims=True) + EPS)
    xhat = h * r
    dxh = du * g
    dh = r * (dxh - xhat * jnp.mean(dxh * xhat, axis=-1, keepdims=True))
    return dh, du * xhat


def _valid_rows(i, tm):
    row = i * tm + lax.broadcasted_iota(jnp.int32, (tm, 1), 0)
    return row >= PAD_ROWS


def _dot(a, b):
    return jnp.dot(a, b, preferred_element_type=F32)


def _dot_nt(a, b):
    return lax.dot_general(a, b, (((1,), (1,)), ((), ())), preferred_element_type=F32)


def _dot_tn(a, b):
    return lax.dot_general(a, b, (((0,), (0,)), ((), ())), preferred_element_type=F32)


def _accumulate(ref, val, first):
    @pl.when(first)
    def _():
        ref[...] = val

    @pl.when(jnp.logical_not(first))
    def _():
        ref[...] += val


def _call_after(dep, body, n_in, in_specs, args, **kw):
    if dep is None:
        return pl.pallas_call(body, in_specs=in_specs, **kw)(*args)

    def with_dep(*refs):
        body(*refs[:n_in], *refs[n_in + 1:])

    return pl.pallas_call(with_dep, in_specs=list(in_specs) + [ANY_SPEC], **kw)(*args, dep)


def _norm_matmul(h, g, w, nc, name, dep=None):
    t, d = h.shape
    n = w.shape[1]
    tm = _row_tile(t, TOKEN_TILE_TARGET, 16)

    def body(h_ref, g_ref, w_ref, z_ref, u_ref):
        u = _rms(h_ref[...], g_ref[...]).astype(BF16)
        u_ref[...] = u
        for n0 in range(0, n, nc):
            n1 = min(n0 + nc, n)
            z_ref[:, n0:n1] = _dot(u, w_ref[:, n0:n1]).astype(BF16)

    return _call_after(
        dep, body, 3,
        [pl.BlockSpec((tm, d), lambda i: (i, 0)), pl.BlockSpec((1, d), lambda i: (0, 0)),
         pl.BlockSpec((d, n), lambda i: (0, 0))], (h, g, w), grid=(t // tm,),
        out_specs=[pl.BlockSpec((tm, n), lambda i: (i, 0)), pl.BlockSpec((tm, d), lambda i: (i, 0))],
        out_shape=[jax.ShapeDtypeStruct((t, n), BF16), jax.ShapeDtypeStruct((t, d), BF16)],
        compiler_params=_cparams("parallel"), name=name)


def _matmul_residual(a, w, h, name, dep=None):
    t, k = a.shape
    d = w.shape[1]
    tm = _row_tile(t, TOKEN_TILE_TARGET, 16)

    def body(a_ref, w_ref, h_ref, o_ref):
        o_ref[...] = h_ref[...] + _dot(a_ref[...], w_ref[...])

    return _call_after(
        dep, body, 3,
        [pl.BlockSpec((tm, k), lambda i: (i, 0)), pl.BlockSpec((k, d), lambda i: (0, 0)),
         pl.BlockSpec((tm, d), lambda i: (i, 0))], (a, w, h), grid=(t // tm,),
        out_specs=pl.BlockSpec((tm, d), lambda i: (i, 0)),
        out_shape=jax.ShapeDtypeStruct((t, d), F32),
        compiler_params=_cparams("parallel"), name=name)


def _ffn_fwd(h, g, w1g, w2g, name):
    t, d = h.shape
    ns, ffs = w1g.shape[0], w1g.shape[2]
    tm = _row_tile(t, TOKEN_TILE_TARGET, 16)

    def body(h_ref, g_ref, w1_ref, w2_ref, ho_ref, hp_ref, u_ref, acc_ref):
        s = pl.program_id(1)

        @pl.when(s == 0)
        def _():
            u_ref[...] = _rms(h_ref[...], g_ref[...]).astype(BF16)

        hp = _dot(u_ref[...], w1_ref[...])
        hp_ref[...] = hp.astype(BF16)
        a = jnp.maximum(hp, 0.0)
        _accumulate(acc_ref, _dot((a * a).astype(BF16), w2_ref[...]), s == 0)

        @pl.when(s == ns - 1)
        def _():
            ho_ref[...] = h_ref[...] + acc_ref[...]

    return pl.pallas_call(
        body, grid=(t // tm, ns),
        in_specs=[pl.BlockSpec((tm, d), lambda i, s: (i, 0)), pl.BlockSpec((1, d), lambda i, s: (0, 0)),
                  pl.BlockSpec((None, d, ffs), lambda i, s: (s, 0, 0)),
                  pl.BlockSpec((None, ffs, d), lambda i, s: (s, 0, 0))],
        out_specs=[pl.BlockSpec((tm, d), lambda i, s: (i, 0)), pl.BlockSpec((tm, ffs), lambda i, s: (i, s)),
                   pl.BlockSpec((tm, d), lambda i, s: (i, 0))],
        out_shape=[jax.ShapeDtypeStruct((t, d), F32), jax.ShapeDtypeStruct((t, ns * ffs), BF16),
                   jax.ShapeDtypeStruct((t, d), BF16)],
        scratch_shapes=[pltpu.VMEM((tm, d), F32)],
        compiler_params=_cparams("parallel", "arbitrary"), name=name)(h, g, w1g, w2g)


def _ffn_bwd_data(dh, h, g, hp, w1g, w2g, name, dep=None):
    t, d = h.shape
    ns, ffs = w1g.shape[0], w1g.shape[2]
    tm = _row_tile(t, ROW_TILE_TARGET, CHUNK)

    def body(dh_ref, h_ref, g_ref, hp_ref, w1_ref, w2_ref, dhi_ref, dhp_ref, dg_ref, acc_ref):
        i, s = pl.program_id(0), pl.program_id(1)
        da = _dot_nt(dh_ref[...].astype(BF16), w2_ref[...])
        dhp = (da * (2.0 * jnp.maximum(hp_ref[...].astype(F32), 0.0))).astype(BF16)
        dhp_ref[...] = dhp
        _accumulate(acc_ref, _dot_nt(dhp, w1_ref[...]), s == 0)

        @pl.when(s == ns - 1)
        def _():
            dhn, dgr = _rms_bwd(h_ref[...], g_ref[...], acc_ref[...])
            dhi_ref[...] = jnp.where(_valid_rows(i, tm), dh_ref[...] + dhn, 0.0)
            _accumulate(dg_ref, jnp.sum(dgr, axis=0, keepdims=True), i == 0)

    return _call_after(
        dep, body, 6,
        [pl.BlockSpec((tm, d), lambda i, s: (i, 0)), pl.BlockSpec((tm, d), lambda i, s: (i, 0)),
         pl.BlockSpec((1, d), lambda i, s: (0, 0)), pl.BlockSpec((tm, ffs), lambda i, s: (i, s)),
         pl.BlockSpec((None, d, ffs), lambda i, s: (s, 0, 0)),
         pl.BlockSpec((None, ffs, d), lambda i, s: (s, 0, 0))], (dh, h, g, hp, w1g, w2g), grid=(t // tm, ns),
        out_specs=[pl.BlockSpec((tm, d), lambda i, s: (i, 0)), pl.BlockSpec((tm, ffs), lambda i, s: (i, s)),
                   pl.BlockSpec((1, d), lambda i, s: (0, 0))],
        out_shape=[jax.ShapeDtypeStruct((t, d), F32), jax.ShapeDtypeStruct((t, ns * ffs), BF16),
                   jax.ShapeDtypeStruct((1, d), F32)],
        scratch_shapes=[pltpu.VMEM((tm, d), F32)],
        compiler_params=_cparams("arbitrary", "arbitrary"), name=name)


WGRAD_ROWS = 2048


def _wgrad(x, dy, nb, xc, yc, x_by_block, dy_by_block, relu2, name, dep=None):
    t = x.shape[0]
    tk = _row_tile(t - CHUNK, WGRAD_ROWS, CHUNK)

    def prep(xv):
        if relu2:
            xv = jnp.maximum(xv.astype(F32), 0.0)
            xv = xv * xv
        return xv.astype(BF16)

    nk = (t - CHUNK) // tk

    def body(xh_ref, dyh_ref, x_ref, dy_ref, o_ref, acc_ref):
        k = pl.program_id(1)
        p = _dot_tn(prep(x_ref[...]), dy_ref[...].astype(BF16))

        @pl.when(k == 0)
        def _():
            acc_ref[...] = p + _dot_tn(prep(xh_ref[...]), dyh_ref[...].astype(BF16))

        @pl.when(k > 0)
        def _():
            acc_ref[...] += p

        @pl.when(k == nk - 1)
        def _():
            o_ref[...] = acc_ref[...].astype(BF16)

    def head(width, by_block):
        return pl.BlockSpec((CHUNK, width), (lambda b, k: (0, b)) if by_block else (lambda b, k: (0, 0)))

    def rest(width, by_block):
        def index(b, k):
            return pl.multiple_of(CHUNK + k * tk, CHUNK), (pl.multiple_of(b * width, 128) if by_block else 0)
        return pl.BlockSpec((pl.Element(tk), pl.Element(width)), index)

    return _call_after(
        dep, body, 4,
        [head(xc, x_by_block), head(yc, dy_by_block), rest(xc, x_by_block), rest(yc, dy_by_block)], (x, dy, x, dy),
        grid=(nb, nk),
        out_specs=pl.BlockSpec((None, xc, yc), lambda b, k: (b, 0, 0)),
        out_shape=jax.ShapeDtypeStruct((nb, xc, yc), BF16),
        scratch_shapes=[pltpu.VMEM((xc, yc), F32)],
        compiler_params=_cparams("parallel", "arbitrary"), name=name)


def _dgrad(dh, w, name, dep=None):
    t, d = dh.shape
    k = w.shape[0]
    tm = _row_tile(t, TOKEN_TILE_TARGET, 16)

    def body(dh_ref, w_ref, o_ref):
        o_ref[...] = _dot_nt(dh_ref[...].astype(BF16), w_ref[...]).astype(BF16)

    return _call_after(
        dep, body, 2,
        [pl.BlockSpec((tm, d), lambda i: (i, 0)), pl.BlockSpec((k, d), lambda i: (0, 0))], (dh, w), grid=(t // tm,),
        out_specs=pl.BlockSpec((tm, k), lambda i: (i, 0)),
        out_shape=jax.ShapeDtypeStruct((t, k), BF16),
        compiler_params=_cparams("parallel"), name=name)


def _dgrad_norm_bwd(dz, w, h, g, dh, nc, name):
    t, d = h.shape
    n = w.shape[1]
    tm = _row_tile(t, ROW_TILE_TARGET // 2, 16)

    def body(dz_ref, w_ref, h_ref, g_ref, dh_ref, dhi_ref, dg_ref):
        i = pl.program_id(0)
        du = jnp.zeros((tm, d), F32)
        for n0 in range(0, n, nc):
            n1 = min(n0 + nc, n)
            du = du + _dot_nt(dz_ref[:, n0:n1], w_ref[:, n0:n1])
        dhn, dgr = _rms_bwd(h_ref[...], g_ref[...], du)
        dhi_ref[...] = jnp.where(_valid_rows(i, tm), dh_ref[...] + dhn, 0.0)
        _accumulate(dg_ref, jnp.sum(dgr, axis=0, keepdims=True), i == 0)

    return pl.pallas_call(
        body, grid=(t // tm,),
        in_specs=[pl.BlockSpec((tm, n), lambda i: (i, 0)), pl.BlockSpec((d, n), lambda i: (0, 0)),
                  pl.BlockSpec((tm, d), lambda i: (i, 0)), pl.BlockSpec((1, d), lambda i: (0, 0)),
                  pl.BlockSpec((tm, d), lambda i: (i, 0))],
        out_specs=[pl.BlockSpec((tm, d), lambda i: (i, 0)), pl.BlockSpec((1, d), lambda i: (0, 0))],
        out_shape=[jax.ShapeDtypeStruct((t, d), F32), jax.ShapeDtypeStruct((1, d), F32)],
        compiler_params=_cparams("arbitrary"), name=name)(dz, w, h, g, dh)


def _dgrad_norm_bwd_input(dz, w, h, g, dh, nc, name):
    t, d = h.shape
    n = w.shape[1]
    tl = _row_tile(t - CHUNK, 512, CHUNK)

    def grads(dz_ref, w_ref, h_ref, g_ref, dh_ref, rows):
        du = jnp.zeros((rows, d), F32)
        for n0 in range(0, n, nc):
            n1 = min(n0 + nc, n)
            du = du + _dot_nt(dz_ref[:, n0:n1], w_ref[:, n0:n1])
        dhn, dgr = _rms_bwd(h_ref[...], g_ref[...], du)
        return dh_ref[...] + dhn, jnp.sum(dgr, axis=0, keepdims=True)

    def rest_body(dz_ref, w_ref, h_ref, g_ref, dh_ref, dg_head_ref, dx_ref, dg_ref):
        dx, dg = grads(dz_ref, w_ref, h_ref, g_ref, dh_ref, tl)
        dx_ref[...] = dx

        @pl.when(pl.program_id(0) == 0)
        def _():
            dg_ref[...] = dg_head_ref[...] + dg

        @pl.when(pl.program_id(0) > 0)
        def _():
            dg_ref[...] += dg

    def head_body(dz_ref, w_ref, h_ref, g_ref, dh_ref, dx_ref, dg_ref):
        dx, dg = grads(dz_ref, w_ref, h_ref, g_ref, dh_ref, CHUNK)
        dx_ref[...] = jnp.where(_valid_rows(0, CHUNK), dx, 0.0)
        dg_ref[...] = dg

    def shifted(width):
        return pl.BlockSpec((pl.Element(tl), pl.Element(width)), lambda i: (pl.multiple_of(CHUNK + i * tl, CHUNK), 0))

    whole = [pl.BlockSpec((d, n), lambda i: (0, 0)), pl.BlockSpec((1, d), lambda i: (0, 0))]
    head = lambda width: pl.BlockSpec((CHUNK, width), lambda i: (0, 0))
    dh_head, dg_head = pl.pallas_call(
        head_body, grid=(1,), in_specs=[head(n), whole[0], head(d), whole[1], head(d)],
        out_specs=[head(d), whole[1]],
        out_shape=[jax.ShapeDtypeStruct((CHUNK, d), F32), jax.ShapeDtypeStruct((1, d), F32)],
        compiler_params=_cparams("arbitrary"), name=name + "_head")(dz, w, h, g, dh)
    dx, dg = pl.pallas_call(
        rest_body, grid=((t - CHUNK) // tl,),
        in_specs=[shifted(n), whole[0], shifted(d), whole[1], shifted(d), whole[1]],
        out_specs=[pl.BlockSpec((tl, d), lambda i: (i, 0)), whole[1]],
        out_shape=[jax.ShapeDtypeStruct((t - CHUNK, d), F32), jax.ShapeDtypeStruct((1, d), F32)],
        compiler_params=_cparams("arbitrary"), name=name)(dz, w, h, g, dh, dg_head)
    return dx, dh_head, dg


def _loss_bwd(h, g, target):
    t, d = h.shape
    tl = _row_tile(t - CHUNK, 1024, CHUNK)

    def body(h_ref, g_ref, t_ref, dh_ref, dg_ref, loss_ref):
        i = pl.program_id(0)
        hv, gv = h_ref[...], g_ref[...]
        err = _rms(hv, gv) - t_ref[...]
        part = 0.5 * jnp.sum(jnp.mean(err * err, axis=-1, keepdims=True), axis=0, keepdims=True)
        dhn, dgr = _rms_bwd(hv, gv, err * (1.0 / d))
        dh_ref[...] = dhn
        _accumulate(dg_ref, jnp.sum(dgr, axis=0, keepdims=True), i == 0)
        _accumulate(loss_ref, jnp.broadcast_to(part, (8, 128)), i == 0)

    shifted = pl.BlockSpec((pl.Element(tl), pl.Element(d)), lambda i: (pl.multiple_of(CHUNK + i * tl, CHUNK), 0))
    dh, dg, loss = pl.pallas_call(
        body, grid=((t - CHUNK) // tl,),
        in_specs=[shifted, pl.BlockSpec((1, d), lambda i: (0, 0)), pl.BlockSpec((tl, d), lambda i: (i, 0))],
        out_specs=[shifted, pl.BlockSpec((1, d), lambda i: (0, 0)), pl.BlockSpec((8, 128), lambda i: (0, 0))],
        out_shape=[jax.ShapeDtypeStruct((t, d), F32), jax.ShapeDtypeStruct((1, d), F32),
                   jax.ShapeDtypeStruct((8, 128), F32)],
        compiler_params=_cparams("arbitrary"), name="loss_bwd")(h, g, target)

    def zero_head(dh_ref, o_ref):
        o_ref[...] = jnp.zeros_like(o_ref)

    dh = pl.pallas_call(
        zero_head, grid=(1,), in_specs=[ANY_SPEC], out_specs=pl.BlockSpec((CHUNK, d), lambda i: (0, 0)),
        out_shape=jax.ShapeDtypeStruct((t, d), F32), input_output_aliases={0: 0}, name="loss_bwd_head")(dh)
    return dh, dg, loss


CONV_BLOCK = 32


def _silu(x):
    return x * jax.nn.sigmoid(x)


def _row_shifts(win):
    n = win.shape[0]
    return [win] + [pltpu.roll(win, n - j, 0) for j in range(1, 8)]


def _cp_seq_fwd(z, conv_w, conv_b, ln_g, ln_b, pool_w, pool_scale):
    t = z.shape[0]
    tm = _row_tile(t, ROW_TILE_TARGET, CHUNK)

    def body(z_ref, cw_ref, cb_ref, lg_ref, lb_ref, pw_ref, ps_ref, c_ref, pm_ref, mix_ref, gbuf, pbuf):
        i = pl.program_id(0)

        @pl.when(i == 0)
        def _():
            gbuf[0:CONV_HALO, :] = jnp.zeros((CONV_HALO, CONV_DIM), F32)
            pbuf[0:POOL_HALO, :] = jnp.zeros((POOL_HALO, POOL_DIM), F32)

        @pl.when(i > 0)
        def _():
            gbuf[0:CONV_HALO, :] = gbuf[tm:tm + CONV_HALO, :]
            pbuf[0:POOL_HALO, :] = pbuf[tm:tm + POOL_HALO, :]

        av = z_ref[:, 0:CONV_DIM].astype(F32)
        ag = z_ref[:, CONV_DIM:2 * CONV_DIM].astype(F32)
        gbuf[CONV_HALO:CONV_HALO + tm, :] = av * jax.nn.sigmoid(ag)
        pbuf[POOL_HALO:POOL_HALO + tm, :] = z_ref[:, 2 * CONV_DIM:CP_IN].astype(F32)

        def conv_block(rb, carry):
            base = pl.multiple_of(rb * CONV_BLOCK, CONV_BLOCK)
            shifted = _row_shifts(gbuf[pl.ds(base, CONV_BLOCK + CONV_HALO), :])
            acc = jnp.zeros((CONV_BLOCK, CONV_DIM), F32)
            for k in range(CONV_WIDTH):
                whole, part = divmod(CONV_HALO - (CONV_WIDTH - 1) + k, 8)
                acc = acc + cw_ref[k:k + 1, :] * shifted[part][8 * whole:8 * whole + CONV_BLOCK, :]
            c_ref[pl.ds(base, CONV_BLOCK), :] = acc + cb_ref[...]
            return carry

        lax.fori_loop(0, tm // CONV_BLOCK, conv_block, 0)

        c = c_ref[...]
        mu = jnp.mean(c, axis=-1, keepdims=True)
        xc = c - mu
        ln = xc * lax.rsqrt(jnp.mean(xc * xc, axis=-1, keepdims=True) + EPS) * lg_ref[...] + lb_ref[...]
        row = i * tm + lax.broadcasted_iota(jnp.int32, (tm, 1), 0)
        mix_ref[:, 0:CONV_DIM] = jnp.where(row >= PAD_ROWS, _silu(ln), 0.0).astype(BF16)

        tpos = (row - PAD_ROWS + 1).astype(F32)
        for gi, wdw in enumerate(POOL_WINDOWS):
            lo = POOL_GROUP * gi
            cur = pbuf[POOL_HALO:POOL_HALO + tm, lo:lo + POOL_GROUP]
            sacc = cur
            for j in range(1, wdw):
                sacc = sacc + pbuf[POOL_HALO - j:POOL_HALO - j + tm, lo:lo + POOL_GROUP]
            pm = (sacc / jnp.clip(tpos, 1.0, float(wdw)) - cur).astype(BF16)
            pm_ref[:, lo:lo + POOL_GROUP] = pm
            pg = _dot(pm, pw_ref[gi].astype(BF16))
            mix_ref[:, CONV_DIM + lo:CONV_DIM + lo + POOL_GROUP] = (pg * ps_ref[:, lo:lo + POOL_GROUP]).astype(BF16)

    vec = pl.BlockSpec((1, CONV_DIM), lambda i: (0, 0))
    return pl.pallas_call(
        body, grid=(t // tm,),
        in_specs=[pl.BlockSpec((tm, CP_IN), lambda i: (i, 0)),
                  pl.BlockSpec((CONV_WIDTH, CONV_DIM), lambda i: (0, 0)), vec, vec, vec,
                  pl.BlockSpec((len(POOL_WINDOWS), POOL_GROUP, POOL_GROUP), lambda i: (0, 0, 0)), vec],
        out_specs=[pl.BlockSpec((tm, CONV_DIM), lambda i: (i, 0)), pl.BlockSpec((tm, POOL_DIM), lambda i: (i, 0)),
                   pl.BlockSpec((tm, CONV_DIM + POOL_DIM), lambda i: (i, 0))],
        out_shape=[jax.ShapeDtypeStruct((t, CONV_DIM), F32), jax.ShapeDtypeStruct((t, POOL_DIM), BF16),
                   jax.ShapeDtypeStruct((t, CONV_DIM + POOL_DIM), BF16)],
        scratch_shapes=[pltpu.VMEM((tm + CONV_HALO, CONV_DIM), F32), pltpu.VMEM((tm + POOL_HALO, POOL_DIM), F32)],
        compiler_params=_cparams("arbitrary"), name="cp_seq_fwd")(z, conv_w, conv_b, ln_g, ln_b, pool_w, pool_scale)


def _cp_seq_bwd(dmix, z, c, pm, conv_w, ln_g, ln_b, pool_w, pool_scale, dep=None):
    t = z.shape[0]
    tm = _row_tile(t, ROW_TILE_TARGET, CHUNK)
    nt = t // tm

    def body(dmix_ref, z_ref, c_ref, pm_ref, cw_ref, lg_ref, lb_ref, pw_ref, ps_ref,
             dz_ref, dcw_ref, dvec_ref, dpw_ref, dcbuf, qbuf, glu_buf, dwacc):
        i = pl.program_id(0)
        tile = nt - 1 - i

        @pl.when(i == 0)
        def _():
            dcbuf[tm:tm + CONV_HALO, :] = jnp.zeros((CONV_HALO, CONV_DIM), F32)
            qbuf[tm:tm + POOL_HALO, :] = jnp.zeros((POOL_HALO, POOL_DIM), F32)
            dcw_ref[...] = jnp.zeros_like(dcw_ref)
            dwacc[...] = jnp.zeros_like(dwacc)
            dvec_ref[...] = jnp.zeros_like(dvec_ref)
            dpw_ref[...] = jnp.zeros_like(dpw_ref)

        @pl.when(i > 0)
        def _():
            dcbuf[tm:tm + CONV_HALO, :] = dcbuf[0:CONV_HALO, :]
            qbuf[tm:tm + POOL_HALO, :] = qbuf[0:POOL_HALO, :]

        row = tile * tm + lax.broadcasted_iota(jnp.int32, (tm, 1), 0)
        cv = c_ref[...]
        mu = jnp.mean(cv, axis=-1, keepdims=True)
        xc = cv - mu
        rstd = lax.rsqrt(jnp.mean(xc * xc, axis=-1, keepdims=True) + EPS)
        xhat = xc * rstd
        ln = xhat * lg_ref[...] + lb_ref[...]
        sg = jax.nn.sigmoid(ln)
        da = jnp.where(row >= PAD_ROWS, dmix_ref[:, 0:CONV_DIM].astype(F32), 0.0)
        dln = da * (sg * (1.0 + ln * (1.0 - sg)))
        dxh = dln * lg_ref[...]
        dc = rstd * (dxh - jnp.mean(dxh, axis=-1, keepdims=True) - xhat * jnp.mean(dxh * xhat, axis=-1, keepdims=True))
        dcbuf[0:tm, :] = dc
        dvec_ref[0:1, :] += jnp.sum(dc, axis=0, keepdims=True)
        dvec_ref[1:2, :] += jnp.sum(dln * xhat, axis=0, keepdims=True)
        dvec_ref[2:3, :] += jnp.sum(dln, axis=0, keepdims=True)

        av = z_ref[:, 0:CONV_DIM].astype(F32)
        sig_g = jax.nn.sigmoid(z_ref[:, CONV_DIM:2 * CONV_DIM].astype(F32))
        glu_buf[...] = av * sig_g

        def conv_block(rb, carry):
            base = pl.multiple_of(rb * CONV_BLOCK, CONV_BLOCK)
            shifted = _row_shifts(dcbuf[pl.ds(base, CONV_BLOCK + CONV_HALO), :])
            glu = glu_buf[pl.ds(base, CONV_BLOCK), :]
            acc = jnp.zeros((CONV_BLOCK, CONV_DIM), F32)
            for k in range(CONV_WIDTH):
                whole, part = divmod(CONV_WIDTH - 1 - k, 8)
                slab = shifted[part][8 * whole:8 * whole + CONV_BLOCK, :]
                acc = acc + cw_ref[k:k + 1, :] * slab
                prod = slab * glu
                part = prod[0:8]
                for q in range(1, CONV_BLOCK // 8):
                    part = part + prod[8 * q:8 * q + 8]
                dwacc[k] += part
            glu_buf[pl.ds(base, CONV_BLOCK), :] = acc
            return carry

        lax.fori_loop(0, tm // CONV_BLOCK, conv_block, 0)

        @pl.when(i == nt - 1)
        def _():
            for k in range(CONV_WIDTH):
                dcw_ref[k:k + 1, :] = jnp.sum(dwacc[k], axis=0, keepdims=True)
        dglu = glu_buf[...]
        dz_ref[:, 0:CONV_DIM] = (dglu * sig_g).astype(BF16)
        dz_ref[:, CONV_DIM:2 * CONV_DIM] = (dglu * av * sig_g * (1.0 - sig_g)).astype(BF16)

        tpos = (row - PAD_ROWS + 1).astype(F32)
        for gi, wdw in enumerate(POOL_WINDOWS):
            lo = POOL_GROUP * gi
            dp = dmix_ref[:, CONV_DIM + lo:CONV_DIM + lo + POOL_GROUP].astype(F32)
            pmv = pm_ref[:, lo:lo + POOL_GROUP]
            pwb = pw_ref[gi].astype(BF16)
            dvec_ref[3:4, lo:lo + POOL_GROUP] += jnp.sum(dp * _dot(pmv, pwb), axis=0, keepdims=True)
            dq = (dp * ps_ref[:, lo:lo + POOL_GROUP]).astype(BF16)
            dpw_ref[gi] += _dot_tn(pmv, dq)
            dpm = _dot_nt(dq, pwb)
            qbuf[0:tm, lo:lo + POOL_GROUP] = dpm / jnp.clip(tpos, 1.0, float(wdw))
            sacc = -dpm
            for j in range(wdw):
                sacc = sacc + qbuf[j:j + tm, lo:lo + POOL_GROUP]
            dz_ref[:, 2 * CONV_DIM + lo:2 * CONV_DIM + lo + POOL_GROUP] = sacc.astype(BF16)

    vec = pl.BlockSpec((1, CONV_DIM), lambda i: (0, 0))
    rev = lambda i: (nt - 1 - i, 0)
    return _call_after(
        dep, body, 9,
        [pl.BlockSpec((tm, CONV_DIM + POOL_DIM), rev), pl.BlockSpec((tm, CP_IN), rev),
         pl.BlockSpec((tm, CONV_DIM), rev), pl.BlockSpec((tm, POOL_DIM), rev),
         pl.BlockSpec((CONV_WIDTH, CONV_DIM), lambda i: (0, 0)), vec, vec,
         pl.BlockSpec((len(POOL_WINDOWS), POOL_GROUP, POOL_GROUP), lambda i: (0, 0, 0)), vec],
        (dmix, z, c, pm, conv_w, ln_g, ln_b, pool_w, pool_scale), grid=(nt,),
        out_specs=[pl.BlockSpec((tm, CP_IN), rev), pl.BlockSpec((CONV_WIDTH + 1, CONV_DIM), lambda i: (0, 0)),
                   pl.BlockSpec((8, CONV_DIM), lambda i: (0, 0)),
                   pl.BlockSpec((len(POOL_WINDOWS), POOL_GROUP, POOL_GROUP), lambda i: (0, 0, 0))],
        out_shape=[jax.ShapeDtypeStruct((t, CP_IN), BF16), jax.ShapeDtypeStruct((CONV_WIDTH + 1, CONV_DIM), F32),
                   jax.ShapeDtypeStruct((8, CONV_DIM), F32),
                   jax.ShapeDtypeStruct((len(POOL_WINDOWS), POOL_GROUP, POOL_GROUP), F32)],
        scratch_shapes=[pltpu.VMEM((tm + CONV_HALO, CONV_DIM), F32), pltpu.VMEM((tm + POOL_HALO, POOL_DIM), F32),
                        pltpu.VMEM((tm, CONV_DIM), F32), pltpu.VMEM((CONV_WIDTH + 1, 8, CONV_DIM), F32)],
        compiler_params=_cparams("arbitrary"), name="cp_seq_bwd")


GLA_UNROLL = 2
Q0, K0, V0, G0, R0 =0, GLA_DK, 2 * GLA_DK, 2 * GLA_DK + GLA_DV, 2 * GLA_DK + 2 * GLA_DV


def _split3(x):
    hi = x.astype(BF16)
    r1 = x - hi.astype(F32)
    mid = r1.astype(BF16)
    lo = (r1 - mid.astype(F32)).astype(BF16)
    return hi, mid, lo


def _tri(strict):
    r = lax.broadcasted_iota(jnp.int32, (CHUNK, CHUNK), 0)
    c = lax.broadcasted_iota(jnp.int32, (CHUNK, CHUNK), 1)
    return ((r > c) if strict else (r >= c)).astype(BF16)


def _chunk_sums(x, cpt, strict, pieces):
    tri3 = jnp.broadcast_to(_tri(strict)[None], (cpt, CHUNK, CHUNK))
    acc = None
    for piece in _split3(x.reshape(cpt, CHUNK, x.shape[-1]))[:pieces]:
        part = jnp.einsum("bij,bjk->bik", tri3, piece, preferred_element_type=F32)
        acc = part if acc is None else acc + part
    return acc


def _chunk_decay(r, gw_ref, gb_ref, cpt):
    pre = _dot(r, gw_ref[...]) + gb_ref[...]
    lac = (jnp.minimum(pre, 0.0) - jnp.log(1.0 + jnp.exp(-jnp.abs(pre)))) * (1.0 / GATE_NORM)
    cum3 = _chunk_sums(lac, cpt, False, 3)
    return cum3, cum3[:, CHUNK - 1:CHUNK, :]


def _gla_seq_fwd(z, gate_w, gate_b, head_g, dep=None):
    t = z.shape[0]
    tm = _row_tile(t, ROW_TILE_TARGET, CHUNK)
    cpt = tm // CHUNK
    scale = GLA_HK ** -0.5

    def body(z_ref, gw_ref, gb_ref, hg_ref, o_ref, mix_ref, st_ref, state, kdec_s, e_s):
        @pl.when(pl.program_id(0) == 0)
        def _():
            state[...] = jnp.zeros_like(state)

        cum3, tot3 = _chunk_decay(z_ref[:, R0:R0 + GATE_PAD], gw_ref, gb_ref, cpt)
        dec = jnp.exp(jnp.broadcast_to(tot3, cum3.shape) - cum3).reshape(tm, GLA_DK)
        kdec_s[...] = (z_ref[:, K0:K0 + GLA_DK].astype(F32) * dec).astype(BF16)
        e_s[...] = jnp.exp(jnp.broadcast_to(tot3, (cpt, 8, GLA_DK))).reshape(cpt * 8, GLA_DK)

        def chunk(ci, carry):
            rows = pl.ds(pl.multiple_of(ci * CHUNK, CHUNK), CHUNK)
            e_all = e_s[pl.ds(pl.multiple_of(ci * 8, 8), 8), :][0:1, :]
            st_ref[ci] = state[...].astype(BF16)
            for hd in range(GLA_HEADS):
                ks = slice(hd * GLA_HK, (hd + 1) * GLA_HK)
                vs = slice(hd * GLA_HV, (hd + 1) * GLA_HV)
                v = z_ref[rows, V0 + hd * GLA_HV:V0 + (hd + 1) * GLA_HV]
                st = state[vs, :] * e_all[:, ks] + _dot_tn(v, kdec_s[rows, ks])
                state[vs, :] = st
                q = z_ref[rows, Q0 + hd * GLA_HK:Q0 + (hd + 1) * GLA_HK]
                o_ref[rows, vs] = (_dot_nt(q, st.astype(BF16)) * scale).astype(BF16)
            return carry

        lax.fori_loop(0, cpt, chunk, 0, unroll=GLA_UNROLL)

        for hd in range(GLA_HEADS):
            vs = slice(hd * GLA_HV, (hd + 1) * GLA_HV)
            on = _rms(o_ref[:, vs].astype(F32), hg_ref[...])
            gv = z_ref[:, G0 + hd * GLA_HV:G0 + (hd + 1) * GLA_HV].astype(F32)
            mix_ref[:, vs] = (on * _silu(gv)).astype(BF16)

    return _call_after(
        dep, body, 4,
        [pl.BlockSpec((tm, GLA_IN_PAD), lambda i: (i, 0)),
         pl.BlockSpec((GATE_PAD, GLA_DK), lambda i: (0, 0)), pl.BlockSpec((1, GLA_DK), lambda i: (0, 0)),
         pl.BlockSpec((1, GLA_HV), lambda i: (0, 0))], (z, gate_w, gate_b, head_g), grid=(t // tm,),
        out_specs=[pl.BlockSpec((tm, GLA_DV), lambda i: (i, 0)), pl.BlockSpec((tm, GLA_DV), lambda i: (i, 0)),
                   pl.BlockSpec((cpt, GLA_DV, GLA_HK), lambda i: (i, 0, 0))],
        out_shape=[jax.ShapeDtypeStruct((t, GLA_DV), BF16), jax.ShapeDtypeStruct((t, GLA_DV), BF16),
                   jax.ShapeDtypeStruct((t // CHUNK, GLA_DV, GLA_HK), BF16)],
        scratch_shapes=[pltpu.VMEM((GLA_DV, GLA_HK), F32), pltpu.VMEM((tm, GLA_DK), BF16),
                        pltpu.VMEM((cpt * 8, GLA_DK), F32)],
        compiler_params=_cparams("arbitrary"), name="gla_seq_fwd")


def _gla_seq_bwd(dmix, o, z, states, gate_w, gate_b, head_g, dep=None):
    t = z.shape[0]
    tm = _row_tile(t, ROW_TILE_TARGET, CHUNK)
    cpt = tm // CHUNK
    nt = t // tm
    scale = GLA_HK ** -0.5

    def body(dmix_ref, o_ref, z_ref, st_ref, gw_ref, gb_ref, hg_ref, dz_ref, dgw_ref, dgb_ref, dhg_ref,
             dstate, dec_s, kdec_s, dkdec_s, do_s, e_s, dtot_s):
        @pl.when(pl.program_id(0) == 0)
        def _():
            dstate[...] = jnp.zeros_like(dstate)
            dgw_ref[...] = jnp.zeros_like(dgw_ref)
            dgb_ref[...] = jnp.zeros_like(dgb_ref)
            dhg_ref[...] = jnp.zeros_like(dhg_ref)

        cum3, tot3 = _chunk_decay(z_ref[:, R0:R0 + GATE_PAD], gw_ref, gb_ref, cpt)
        dec = jnp.exp(jnp.broadcast_to(tot3, cum3.shape) - cum3).reshape(tm, GLA_DK)
        dec_s[...] = dec
        kdec = z_ref[:, K0:K0 + GLA_DK].astype(F32) * dec
        kdec_s[...] = kdec
        e3 = jnp.exp(tot3)
        e_s[...] = jnp.broadcast_to(e3, (cpt, 8, GLA_DK)).reshape(cpt * 8, GLA_DK)
        dhg = jnp.zeros((1, GLA_HV), F32)
        for hd in range(GLA_HEADS):
            ks = slice(hd * GLA_HK, (hd + 1) * GLA_HK)
            vs = slice(hd * GLA_HV, (hd + 1) * GLA_HV)
            gcols = slice(G0 + hd * GLA_HV, G0 + (hd + 1) * GLA_HV)
            ov = o_ref[:, vs].astype(F32)
            gv = z_ref[:, gcols].astype(F32)
            dm = dmix_ref[:, vs].astype(F32)
            sg = jax.nn.sigmoid(gv)
            rr = lax.rsqrt(jnp.mean(ov * ov, axis=-1, keepdims=True) + EPS)
            xhat = ov * rr
            don = dm * (gv * sg)
            dz_ref[:, gcols] = (dm * (xhat * hg_ref[...]) * (sg * (1.0 + gv * (1.0 - sg)))).astype(BF16)
            dhg = dhg + jnp.sum(don * xhat, axis=0, keepdims=True)
            dxh = don * hg_ref[...]
            do = (rr * (dxh - xhat * jnp.mean(dxh * xhat, axis=-1, keepdims=True)) * scale).astype(BF16)
            do_s[:, vs] = do
            v3 = z_ref[:, V0 + hd * GLA_HV:V0 + (hd + 1) * GLA_HV].reshape(cpt, CHUNK, GLA_HV)
            kdb3 = kdec[:, ks].astype(BF16).reshape(cpt, CHUNK, GLA_HK)
            st3 = st_ref[:, vs, :].astype(F32) * e3[:, :, ks] + jnp.einsum("bcv,bck->bvk", v3, kdb3,
                                                                            preferred_element_type=F32)
            dq3 = jnp.einsum("bcv,bvk->bck", do.reshape(cpt, CHUNK, GLA_HV), st3.astype(BF16), preferred_element_type=F32)
            dz_ref[:, Q0 + hd * GLA_HK:Q0 + (hd + 1) * GLA_HK] = dq3.reshape(tm, GLA_HK).astype(BF16)
        dhg_ref[...] += dhg

        def chunk(cj, carry):
            ci = cpt - 1 - cj
            rows = pl.ds(pl.multiple_of(ci * CHUNK, CHUNK), CHUNK)
            erows = pl.ds(pl.multiple_of(ci * 8, 8), 8)
            e_all = e_s[erows, :][0:1, :]
            for hd in range(GLA_HEADS):
                ks = slice(hd * GLA_HK, (hd + 1) * GLA_HK)
                vs = slice(hd * GLA_HV, (hd + 1) * GLA_HV)
                e = e_all[:, ks]
                kdb = kdec_s[rows, ks].astype(BF16)
                v = z_ref[rows, V0 + hd * GLA_HV:V0 + (hd + 1) * GLA_HV]
                q = z_ref[rows, Q0 + hd * GLA_HK:Q0 + (hd + 1) * GLA_HK]
                do = do_s[rows, vs]
                st_prev = st_ref[ci, vs, :].astype(F32)
                dst = dstate[vs, :] + _dot_tn(do, q)
                dstb = dst.astype(BF16)
                dkdec_s[rows, ks] = _dot(v, dstb)
                dz_ref[rows, V0 + hd * GLA_HV:V0 + (hd + 1) * GLA_HV] = _dot_nt(kdb, dstb).astype(BF16)
                dtot = jnp.sum(dst * st_prev, axis=0, keepdims=True) * e
                dtot_s[erows, ks] = jnp.broadcast_to(dtot, (8, GLA_HK))
                dstate[vs, :] = dst * e
            return carry

        lax.fori_loop(0, cpt, chunk, 0, unroll=GLA_UNROLL)

        dkdec = dkdec_s[...]
        dz_ref[:, K0:K0 + GLA_DK] = (dkdec * dec_s[...]).astype(BF16)
        before = _chunk_sums(dkdec * kdec_s[...], cpt, True, 2)
        dtot3 = dtot_s[...].reshape(cpt, 8, GLA_DK)[:, 0:1, :]
        dlac = (jnp.broadcast_to(dtot3, before.shape) + before).reshape(tm, GLA_DK)
        pre = _dot(z_ref[:, R0:R0 + GATE_PAD], gw_ref[...]) + gb_ref[...]
        dpre = dlac * (1.0 / GATE_NORM) * (1.0 - jax.nn.sigmoid(pre))
        dpb = dpre.astype(BF16)
        dz_ref[:, R0:R0 + GATE_PAD] = _dot_nt(dpb, gw_ref[...]).astype(BF16)
        dgw_ref[...] += _dot_tn(z_ref[:, R0:R0 + GATE_PAD], dpb)
        dgb_ref[...] += jnp.sum(dpre, axis=0, keepdims=True)

    rev = lambda i: (nt - 1 - i, 0)
    return _call_after(
        dep, body, 7,
        [pl.BlockSpec((tm, GLA_DV), rev), pl.BlockSpec((tm, GLA_DV), rev), pl.BlockSpec((tm, GLA_IN_PAD), rev),
         pl.BlockSpec((cpt, GLA_DV, GLA_HK), lambda i: (nt - 1 - i, 0, 0)),
         pl.BlockSpec((GATE_PAD, GLA_DK), lambda i: (0, 0)), pl.BlockSpec((1, GLA_DK), lambda i: (0, 0)),
         pl.BlockSpec((1, GLA_HV), lambda i: (0, 0))],
        (dmix, o, z, states, gate_w, gate_b, head_g), grid=(nt,),
        out_specs=[pl.BlockSpec((tm, GLA_IN_PAD), rev), pl.BlockSpec((GATE_PAD, GLA_DK), lambda i: (0, 0)),
                   pl.BlockSpec((1, GLA_DK), lambda i: (0, 0)), pl.BlockSpec((1, GLA_HV), lambda i: (0, 0))],
        out_shape=[jax.ShapeDtypeStruct((t, GLA_IN_PAD), BF16), jax.ShapeDtypeStruct((GATE_PAD, GLA_DK), F32),
                   jax.ShapeDtypeStruct((1, GLA_DK), F32), jax.ShapeDtypeStruct((1, GLA_HV), F32)],
        scratch_shapes=[pltpu.VMEM((GLA_DV, GLA_HK), F32), pltpu.VMEM((tm, GLA_DK), F32), pltpu.VMEM((tm, GLA_DK), F32),
                        pltpu.VMEM((tm, GLA_DK), F32), pltpu.VMEM((tm, GLA_DV), BF16),
                        pltpu.VMEM((cpt * 8, GLA_DK), F32), pltpu.VMEM((cpt * 8, GLA_DK), F32)],
        compiler_params=_cparams("arbitrary"), name="gla_seq_bwd")


def _sum_slots(x, name):
    n, r, cdim = x.shape
    tr = _row_tile(r, 256, 8)

    def body(x_ref, o_ref):
        acc = x_ref[0].astype(F32)
        for j in range(1, n):
            acc = acc + x_ref[j].astype(F32)
        o_ref[...] = acc

    return pl.pallas_call(
        body, grid=(r // tr,),
        in_specs=[pl.BlockSpec((n, tr, cdim), lambda i: (0, i, 0))],
        out_specs=pl.BlockSpec((tr, cdim), lambda i: (i, 0)),
        out_shape=jax.ShapeDtypeStruct((r, cdim), F32),
        compiler_params=_cparams("parallel"), name=name)(x)


def _sum_own_and_slots(own, slots, dev_idx, name):
    _, _, r, cdim = own.shape
    n = slots.shape[0]
    tr = _row_tile(r, 256, 8)

    def body(s_ref, own_ref, *rest):
        acc = own_ref[...].astype(F32)
        for other in rest[:n - 1]:
            acc = acc + other[...].astype(F32)
        rest[n - 1][...] = acc

    def slot(dd):
        return pl.BlockSpec((None, tr, cdim), lambda i, s: ((s[0] + dd) % n, i, 0))

    mine = pl.BlockSpec((None, None, tr, cdim), lambda i, s: (s[0] // 2, s[0] % 2, i, 0))
    return pl.pallas_call(
        body,
        grid_spec=pltpu.PrefetchScalarGridSpec(
            num_scalar_prefetch=1, grid=(r // tr,), in_specs=[mine] + [slot(dd) for dd in range(1, n)],
            out_specs=pl.BlockSpec((tr, cdim), lambda i, s: (i, 0))),
        out_shape=jax.ShapeDtypeStruct((r, cdim), F32),
        compiler_params=_cparams("parallel"), name=name)(dev_idx, own, *([slots] * (n - 1)))


def _add2(a, b, name):
    r, cdim = a.shape
    tr = _row_tile(r, 256, 8)

    def body(a_ref, b_ref, o_ref):
        o_ref[...] = a_ref[...] + b_ref[...]

    spec = pl.BlockSpec((tr, cdim), lambda i: (i, 0))
    return pl.pallas_call(body, grid=(r // tr,), in_specs=[spec, spec], out_specs=spec,
                          out_shape=jax.ShapeDtypeStruct((r, cdim), F32),
                          compiler_params=_cparams("parallel"), name=name)(a, b)


def _adamw_half(w, gs, m, v, half_idx, prev, name, dep=None):
    nl, _, h, cdim = w.shape
    tr = _row_tile(h, 256, 8)
    nprev = 0 if prev is None else 4
    extra = [] if dep is None else [dep]

    def body(s_ref, w_ref, m_ref, v_ref, *rest):
        g_refs = rest[:nl]
        go_ref, d_ref, mo_ref, vo_ref = rest[nl + nprev + len(extra):]
        layer = pl.program_id(0)
        gv = g_refs[0][...]
        for j in range(1, nl):
            gv = jnp.where(layer == j, g_refs[j][...], gv)
        go_ref[...] = gv
        mn = ADAM_B1 * m_ref[...] + (1.0 - ADAM_B1) * gv
        vn = ADAM_B2 * v_ref[...] + (1.0 - ADAM_B2) * (gv * gv)
        m_hat = mn / (1.0 - ADAM_B1 ** ADAM_STEP)
        v_hat = vn / (1.0 - ADAM_B2 ** ADAM_STEP)
        d_ref[...] = -ADAM_LR * (m_hat / (jnp.sqrt(v_hat) + ADAM_EPS) + ADAM_WD * w_ref[...])
        mo_ref[...] = mn
        vo_ref[...] = vn

    half = pl.BlockSpec((None, None, tr, cdim), lambda l, i, s: (l, s[0], i, 0))

    def of_layer(j):
        return pl.BlockSpec((tr, cdim), lambda l, i, s: (jnp.where(l == j, i, 0), 0))

    shp = jax.ShapeDtypeStruct(w.shape, F32)
    return pl.pallas_call(
        body,
        grid_spec=pltpu.PrefetchScalarGridSpec(
            num_scalar_prefetch=1, grid=(nl, h // tr),
            in_specs=[half] * 3 + [of_layer(j) for j in range(nl)] + [ANY_SPEC] * (nprev + len(extra)),
            out_specs=[half] * 4),
        out_shape=[shp] * 4, input_output_aliases={4 + nl + k: k for k in range(nprev)},
        compiler_params=_cparams("arbitrary", "arbitrary"), name=name,
    )(half_idx, w, m, v, *gs, *([] if prev is None else prev), *extra)


def _adamw_many(ws, gs, ms, vs):
    n = len(ws)

    def body(*refs):
        for i in range(n):
            w_ref, g_ref, m_ref, v_ref = refs[i], refs[n + i], refs[2 * n + i], refs[3 * n + i]
            d_ref, mo_ref, vo_ref = refs[4 * n + i], refs[5 * n + i], refs[6 * n + i]
            gv = g_ref[...]
            mn = ADAM_B1 * m_ref[...] + (1.0 - ADAM_B1) * gv
            vn = ADAM_B2 * v_ref[...] + (1.0 - ADAM_B2) * (gv * gv)
            m_hat = mn / (1.0 - ADAM_B1 ** ADAM_STEP)
            v_hat = vn / (1.0 - ADAM_B2 ** ADAM_STEP)
            d_ref[...] = -ADAM_LR * (m_hat / (jnp.sqrt(v_hat) + ADAM_EPS) + ADAM_WD * w_ref[...])
            mo_ref[...] = mn
            vo_ref[...] = vn

    shapes = [jax.ShapeDtypeStruct(w.shape, F32) for w in ws]
    outs = pl.pallas_call(body, out_shape=shapes * 3, name="adamw_small")(*ws, *gs, *ms, *vs)
    return outs[:n], outs[n:2 * n], outs[2 * n:]


def _split_rows(a):
    return a.reshape(a.shape[0], 2, a.shape[1] // 2, a.shape[2])


def _place():
    x, y, c = lax.axis_index("x"), lax.axis_index("y"), lax.axis_index("c")
    chips = [(1 - x, y), (x, 1 - y), (1 - x, 1 - y)]
    return x, y, c, chips


def _remote(src, dst, send_sem, recv_sem, to):
    return pltpu.make_async_remote_copy(src_ref=src, dst_ref=dst, send_sem=send_sem, recv_sem=recv_sem,
                                        device_id=to, device_id_type=MESH)


def _plan_gather(n_halved):
    def plan(src_refs, land_refs):
        x, y, c, chips = _place()
        me = 2 * x + y
        copies = []
        for k, (src, land) in enumerate(zip(src_refs, land_refs)):
            for (px, py) in chips:
                frm = 2 * px + py
                if k < n_halved:
                    copies.append((src.at[c], land.at[me, c], (px, py, c), land.at[frm, c]))
                else:
                    copies.append((src, land.at[me], (px, py, c), land.at[frm]))
        return copies
    return plan


def _plan_share(src_refs, land_refs):
    x, y, c, chips = _place()
    me = 2 * x + y
    sib = (x, y, 1 - c)
    copies = []
    for src, land in zip(src_refs, land_refs):
        copies.append((src, land.at[me], sib, land.at[me]))
        for (px, py) in chips:
            frm = 2 * px + py
            copies.append((land.at[frm, c], land.at[frm, c], sib, land.at[frm, 1 - c]))
    return copies


def _plan_scatter(n_parts):
    def plan(src_refs, land_refs):
        x, y, c, chips = _place()
        me = 2 * x + y
        copies = []
        for k, (src, land) in enumerate(zip(src_refs, land_refs)):
            for (px, py) in chips:
                to = 2 * px + py
                copies.append((src.at[to] if k < n_parts else src, land.at[me], (px, py, c), land.at[to]))
        return copies
    return plan


N_DEVICES = 8
OTHER_DEVICES = [(dx, dy, dc) for dx in (0, 1) for dy in (0, 1) for dc in (0, 1) if dx or dy or dc]


def _plan_scatter_all(src_refs, land_refs):
    x, y, c, _ = _place()
    me = 4 * x + 2 * y + c
    copies = []
    for src, land in zip(src_refs, land_refs):
        for dx, dy, dc in OTHER_DEVICES:
            px, py, pc = (1 - x if dx else x), (1 - y if dy else y), (1 - c if dc else c)
            copies.append((src.at[2 * px + py, pc], land.at[me], (px, py, pc), land.at[4 * px + 2 * py + pc]))
    return copies


def _plan_exchange(n_split):
    def plan(src_refs, land_refs):
        x, y, c, _ = _place()
        sib = (x, y, 1 - c)
        return [(src.at[:, 1 - c] if k < n_split else src, land, sib, land)
                for k, (src, land) in enumerate(zip(src_refs, land_refs))]
    return plan


def _hbm(a):
    return pltpu.HBM(a.shape, a.dtype)


def _start_copies(name, srcs, lands, plan, ncopy, dep=None):
    ns, nl = len(srcs), len(lands)
    nin = ns + nl + (0 if dep is None else 1)

    def body(*refs):
        send_sems, recv_sems, token = refs[nin], refs[nin + 1], refs[-1]
        for k, (src, dst, dev, _) in enumerate(plan(refs[:ns], refs[ns:ns + nl])):
            _remote(src, dst, send_sems.at[k], recv_sems.at[k], dev).start()
        token[...] = jnp.zeros_like(token)

    args = [pltpu.with_memory_space_constraint(a, pltpu.HBM) for a in list(srcs) + list(lands)]
    outs = pl.pallas_call(
        body, name=name,
        out_shape=(pltpu.SemaphoreType.DMA((ncopy,)), pltpu.SemaphoreType.DMA((ncopy,)),
                   *[_hbm(a) for a in list(srcs) + list(lands)], jax.ShapeDtypeStruct((8, 128), F32)),
        in_specs=[HBM_SPEC] * (ns + nl) + ([] if dep is None else [ANY_SPEC]),
        out_specs=(SEM_SPEC, SEM_SPEC, *([HBM_SPEC] * (ns + nl)), pl.BlockSpec(memory_space=pltpu.VMEM)),
        input_output_aliases={i: 2 + i for i in range(ns + nl)},
        compiler_params=pltpu.CompilerParams(has_side_effects=SIDE_EFFECT),
    )(*args, *([] if dep is None else [dep]))
    return outs[0], outs[1], list(outs[2:2 + ns]), list(outs[2 + ns:2 + ns + nl]), outs[-1]


def _wait_copies(name, started, plan, after, sem_offset=0):
    send_sems, recv_sems, srcs, lands, _ = started
    ns, nl = len(srcs), len(lands)
    after = list(after) if isinstance(after, (list, tuple)) else [after]

    def body(*refs):
        send_ref, recv_ref = refs[ns + nl], refs[ns + nl + 1]
        for k, (src, _, dev, mine) in enumerate(plan(refs[:ns], refs[ns:ns + nl])):
            copy = _remote(src, mine, send_ref.at[sem_offset + k], recv_ref.at[sem_offset + k], dev)
            copy.wait_send()
            copy.wait_recv()

    outs = pl.pallas_call(
        body, name=name, out_shape=tuple(_hbm(a) for a in srcs + lands),
        in_specs=[HBM_SPEC] * (ns + nl) + [SEM_SPEC, SEM_SPEC] + [ANY_SPEC] * len(after),
        out_specs=tuple([HBM_SPEC] * (ns + nl)),
        input_output_aliases={i: i for i in range(ns + nl)},
        compiler_params=pltpu.CompilerParams(has_side_effects=SIDE_EFFECT),
    )(*srcs, *lands, send_sems, recv_sems, *after)
    return list(outs[:ns]), list(outs[ns:])


def _share_with_sibling(name, srcs, lands):
    n = len(srcs)

    def body(*refs):
        src_refs, land_refs, out_refs = refs[:n], refs[n:2 * n], refs[2 * n:3 * n]
        send_sem, recv_sem = refs[3 * n:]
        x, y, c, chips = _place()
        me = 2 * x + y
        sib = (x, y, 1 - c)
        sends, recvs = [], []
        for k in range(n):
            sems = (send_sem.at[4 * k], recv_sem.at[4 * k])
            sends.append(_remote(src_refs[k], out_refs[k].at[me], *sems, sib))
            recvs.append(_remote(src_refs[k], out_refs[k].at[me], *sems, sib))
            for j, (px, py) in enumerate(chips):
                frm = 2 * px + py
                sems = (send_sem.at[4 * k + 1 + j], recv_sem.at[4 * k + 1 + j])
                sends.append(_remote(land_refs[k].at[frm, c], out_refs[k].at[frm, c], *sems, sib))
                recvs.append(_remote(land_refs[k].at[frm, c], out_refs[k].at[frm, 1 - c], *sems, sib))
        for cp in sends:
            cp.start()
        for cp in recvs:
            cp.wait_recv()
        for cp in sends:
            cp.wait_send()

    return pl.pallas_call(
        body, name=name, in_specs=[HBM_SPEC] * (2 * n), out_specs=[HBM_SPEC] * n,
        out_shape=[jax.ShapeDtypeStruct(a.shape, a.dtype) for a in lands],
        input_output_aliases={n + k: k for k in range(n)},
        scratch_shapes=[pltpu.SemaphoreType.DMA((4 * n,)), pltpu.SemaphoreType.DMA((4 * n,))],
    )(*srcs, *lands)


def _pack(arrs):
    flat = jnp.concatenate([a.reshape(-1).astype(F32) for a in arrs])
    n = flat.shape[0]
    rows = -(-n // PACK_WIDTH)
    rows = -(-rows // 8) * 8
    return jnp.pad(flat, (0, rows * PACK_WIDTH - n)).reshape(rows, PACK_WIDTH)


def _unpack(buf, shapes):
    flat = buf.reshape(-1)
    out, off = [], 0
    for shp in shapes:
        n = 1
        for s in shp:
            n *= s
        out.append(flat[off:off + n].reshape(shp))
        off += n
    return out


def _unshard_cols(stacked):
    moved = jnp.moveaxis(stacked, 0, -2)
    return moved.reshape(moved.shape[:-2] + (moved.shape[-2] * moved.shape[-1],))


def _take_cols(blocks, start, width):
    bw = blocks.shape[2]
    pieces, lo = [], start
    while lo < start + width:
        b = lo // bw
        hi = min(start + width, (b + 1) * bw)
        pieces.append(blocks[b][:, lo - b * bw:hi - b * bw])
        lo = hi
    return jnp.concatenate(pieces, axis=1)


def _col_shard(full, s, width):
    return lax.dynamic_slice_in_dim(full, s * width, width, axis=full.ndim - 1)


def kernel(x, meta_tokens, mix_norm_g, ffn_norm_g, ffn_w1, ffn_w2, cp_w_in, cp_conv_w, cp_conv_b, cp_ln_g, cp_ln_b, cp_pool_w, cp_pool_scale, cp_w_out, gla_w_in, gla_gate_w2, gla_gate_b, gla_head_g, gla_w_out, final_norm_g, loss_target, m_meta_tokens, m_mix_norm_g, m_ffn_norm_g, m_ffn_w1, m_ffn_w2, m_cp_w_in, m_cp_conv_w, m_cp_conv_b, m_cp_ln_g, m_cp_ln_b, m_cp_pool_w, m_cp_pool_scale, m_cp_w_out, m_gla_w_in, m_gla_gate_w2, m_gla_gate_b, m_gla_head_g, m_gla_w_out, m_final_norm_g, v_meta_tokens, v_mix_norm_g, v_ffn_norm_g, v_ffn_w1, v_ffn_w2, v_cp_w_in, v_cp_conv_w, v_cp_conv_b, v_cp_ln_g, v_cp_ln_b, v_cp_pool_w, v_cp_pool_scale, v_cp_w_out, v_gla_w_in, v_gla_gate_w2, v_gla_gate_b, v_gla_head_g, v_gla_w_out, v_final_norm_g):
    d = D_MODEL
    chip = 2 * lax.axis_index("x") + lax.axis_index("y")
    core = lax.axis_index("c")
    seq = x.shape[1]
    t = seq + CHUNK

    sharded_small = [meta_tokens, cp_conv_w, gla_gate_w2, gla_gate_b, gla_head_g]

    def halves(w):
        return w.astype(BF16).reshape(2, w.shape[0] // 2, w.shape[1])

    def unhalve(g):
        return g.reshape(N_CHIPS, 2 * g.shape[2], g.shape[3])

    def gather_group(srcs, whole=()):
        lands = [lax.empty((N_CHIPS,) + s.shape, s.dtype) for s in srcs]
        for a in whole:
            lands.append(lax.dynamic_update_slice(jnp.zeros((N_CHIPS,) + a.shape, a.dtype), a[None], (chip,) + (0,) * a.ndim))
        return list(srcs) + list(whole), lands, _plan_gather(len(srcs)), len(srcs)

    groups = [gather_group([halves(cp_w_in[0]), halves(cp_w_out[0])], [_pack(sharded_small)]),
              gather_group([halves(ffn_w1[0]), halves(ffn_w2[0])]), gather_group([halves(gla_w_in[0]), halves(gla_w_out[0])]),
              gather_group([halves(ffn_w1[1]), halves(ffn_w2[1])])]
    bounds, all_srcs, all_lands = [], [], []
    for srcs, lands, _, _ in groups:
        bounds.append((len(all_srcs), len(all_srcs) + len(srcs)))
        all_srcs += srcs
        all_lands += lands

    def plan_all(src_refs, land_refs):
        return [cp for (lo, hi), group in zip(bounds, groups) for cp in group[2](src_refs[lo:hi], land_refs[lo:hi])]

    gathers = _start_copies("gather_start", all_srcs, all_lands, plan_all, 3 * len(all_srcs))

    def arrived(name, gi, after):
        (lo, hi), plan, n = bounds[gi], groups[gi][2], groups[gi][3]
        mine = (gathers[0], gathers[1], gathers[2][lo:hi], gathers[3][lo:hi], gathers[4])
        srcs, lands = _wait_copies(name + "_wait", mine, plan, after, sem_offset=3 * lo)
        return srcs[:n], lands[:n], lands[n:]

    cp_gather, ffn0_gather, gla_gather, ffn1_gather = 0, 1, 2, 3
    h0_rows = jnp.concatenate([jnp.zeros((CHUNK, d), F32) + gathers[4][0, 0], x[0]], axis=0)
    cp_srcs, cp_lands, (small_g,) = arrived("gather_cp", cp_gather, h0_rows)
    cpin_g, cpout_g = [unhalve(g) for g in _share_with_sibling("gather_cp_share", cp_srcs, cp_lands)]
    per_chip = [_unpack(small_g[j], [a.shape for a in sharded_small]) for j in range(N_CHIPS)]
    meta_f, conv_w_f, gate_w_f, gate_b_f, head_g_f = [
        jnp.concatenate([per_chip[j][i] for j in range(N_CHIPS)], axis=-1) for i in range(len(sharded_small))]
    conv_w_f, gate_w_f = conv_w_f[0], gate_w_f[0]
    w_cp_in = _unshard_cols(cpin_g)
    w_cp_out = cpout_g.reshape(CONV_DIM + POOL_DIM, d)
    gate_w_pad = jnp.pad(gate_w_f, ((0, GATE_PAD - GATE_RANK), (0, 0))).astype(BF16)
    row = lambda a: a.reshape(1, -1)
    c_idx = core.reshape(1).astype(jnp.int32)

    h0 = lax.dynamic_update_slice(h0_rows, meta_f, (PAD_ROWS, 0))
    z0, u0 = _norm_matmul(h0, row(mix_norm_g[0]), w_cp_in, 512, "cp_in_proj")
    c0, pm0, mix0 = _cp_seq_fwd(z0, conv_w_f, cp_conv_b, cp_ln_g, cp_ln_b, cp_pool_w[0], cp_pool_scale)
    ffn0_srcs, ffn0_lands, _ = arrived("gather_ffn0", ffn0_gather, mix0)
    ffn0_share = _start_copies("gather_ffn0_share_start", ffn0_srcs, ffn0_lands, _plan_share, 4 * len(ffn0_srcs))
    h1 = _matmul_residual(mix0, w_cp_out, h0, "cp_out_proj", dep=ffn0_share[-1])
    w1g0, w2g0 = [unhalve(g) for g in _wait_copies("gather_ffn0_share_wait", ffn0_share, _plan_share, h1)[1]]
    h2, hp0, uf0 = _ffn_fwd(h1, row(ffn_norm_g[0]), w1g0, w2g0, "ffn0_fwd")
    gla_srcs, gla_lands, _ = arrived("gather_gla", gla_gather, h2)
    glain_g, glaout_g = [unhalve(g) for g in _share_with_sibling("gather_gla_share", gla_srcs, gla_lands)]
    w_gla_in = jnp.concatenate([glain_g[j] for j in range(N_CHIPS)] + [jnp.zeros((d, GLA_IN_PAD - GLA_IN), BF16)], axis=1)
    w_gla_out = glaout_g.reshape(GLA_DV, d)
    z1, u2 = _norm_matmul(h2, row(mix_norm_g[1]), w_gla_in, GLA_COLS, "gla_in_proj")
    ffn1_srcs, ffn1_lands, _ = arrived("gather_ffn1", ffn1_gather, z1)
    ffn1_share = _start_copies("gather_ffn1_share_start", ffn1_srcs, ffn1_lands, _plan_share, 4 * len(ffn1_srcs))
    o1, mix1, states = _gla_seq_fwd(z1, gate_w_pad, gate_b_f, head_g_f, dep=ffn1_share[-1])
    h3 = _matmul_residual(mix1, w_gla_out, h2, "gla_out_proj")
    w1g1, w2g1 = [unhalve(g) for g in _wait_copies("gather_ffn1_share_wait", ffn1_share, _plan_share, h3)[1]]
    h4, hp1, uf1 = _ffn_fwd(h3, row(ffn_norm_g[1]), w1g1, w2g1, "ffn1_fwd")

    dev_idx = (2 * chip + core).reshape(1).astype(jnp.int32)

    def start_reduce(name, grads):
        srcs = [_split_rows(g) for g in grads]
        lands = [lax.empty((N_DEVICES,) + s.shape[2:], s.dtype) for s in srcs]
        return _start_copies(name + "_scatter_start", srcs, lands, _plan_scatter_all, len(OTHER_DEVICES) * len(srcs))

    def finish_reduce(name, started, after):
        srcs, lands = _wait_copies(name + "_scatter_wait", started, _plan_scatter_all, after)
        return [_sum_own_and_slots(s, l, dev_idx, "%s_slot_sum_%d" % (name, k)) for k, (s, l) in enumerate(zip(srcs, lands))]

    dh4, d_final_g, loss_part = _loss_bwd(h4, row(final_norm_g), loss_target[0])

    dh3, dhp1, d_ffn_g1 = _ffn_bwd_data(dh4, h3, row(ffn_norm_g[1]), hp1, w1g1, w2g1, "ffn1_bwd")
    dw1_1 = _wgrad(uf1, dhp1, N_CHIPS, d, d, False, True, False, "ffn1_dw1")
    dw2_1 = _wgrad(hp1, dh4, N_CHIPS, d, d, True, False, True, "ffn1_dw2")
    ffn1_reduce = start_reduce("ffn1", [dw1_1, dw2_1])

    dmix1 = _dgrad(dh3, w_gla_out, "gla_out_dgrad", dep=ffn1_reduce[-1])
    dw_gla_out = _wgrad(mix1, dh3, 1, GLA_DV, d, False, False, False, "gla_out_dw")
    dz1, d_gate_w, d_gate_b, d_head_g = _gla_seq_bwd(dmix1, o1, z1, states, gate_w_pad, gate_b_f, head_g_f)
    dh2, d_mix_g1 = _dgrad_norm_bwd(dz1, w_gla_in, h2, row(mix_norm_g[1]), dh3, GLA_COLS, "gla_in_dgrad")
    dw_gla_in = _wgrad(u2, dz1, GLA_IN_PAD // 640, d, 640, False, True, False, "gla_in_dw")
    gla_in_shards = jnp.stack([_take_cols(dw_gla_in, j * (GLA_IN // N_CHIPS), GLA_IN // N_CHIPS) for j in range(N_CHIPS)])
    gla_reduce = start_reduce("gla", [gla_in_shards, dw_gla_out.reshape(N_CHIPS, -1, d)])

    dh1, dhp0, d_ffn_g0 = _ffn_bwd_data(dh2, h1, row(ffn_norm_g[0]), hp0, w1g0, w2g0, "ffn0_bwd", dep=gla_reduce[-1])
    dw1_0 = _wgrad(uf0, dhp0, N_CHIPS, d, d, False, True, False, "ffn0_dw1")
    dw2_0 = _wgrad(hp0, dh2, N_CHIPS, d, d, True, False, True, "ffn0_dw2")
    ffn0_reduce = start_reduce("ffn0", [dw1_0, dw2_0])

    dmix0 = _dgrad(dh1, w_cp_out, "cp_out_dgrad", dep=ffn0_reduce[-1])
    dw_cp_out = _wgrad(mix0, dh1, 1, CONV_DIM + POOL_DIM, d, False, False, False, "cp_out_dw")
    dz0, d_conv_w, d_cp_vec, d_pool_w = _cp_seq_bwd(dmix0, z0, c0, pm0, conv_w_f, cp_ln_g, cp_ln_b, cp_pool_w[0],
                                                    cp_pool_scale)
    grad_x, dh0_head, d_mix_g0 = _dgrad_norm_bwd_input(dz0, w_cp_in, h0, row(mix_norm_g[0]), dh1, 512, "cp_in_dgrad")
    grad_x = grad_x[None]
    dw_cp_in = _wgrad(u0, dz0, 1, d, CP_IN, False, False, False, "cp_in_dw")
    dw_cp_in = jnp.stack([_take_cols(dw_cp_in, j * (CP_IN // N_CHIPS), CP_IN // N_CHIPS) for j in range(N_CHIPS)])

    cp_reduce = start_reduce("cp", [dw_cp_in, dw_cp_out.reshape(N_CHIPS, -1, d)])
    small_full = [dh0_head[PAD_ROWS:CHUNK],jnp.concatenate([d_mix_g0, d_mix_g1], axis=0),
                  jnp.concatenate([d_ffn_g0, d_ffn_g1], axis=0), d_conv_w[:CONV_WIDTH][None],
                  d_cp_vec[0:1], d_cp_vec[1:2], d_cp_vec[2:3], d_pool_w[None], d_cp_vec[3:4],
                  d_gate_w[:GATE_RANK][None], d_gate_b, d_head_g, d_final_g[0], loss_part[0, 0:1]]
    small_mine = _pack(small_full)
    whole = _plan_exchange(0)
    small_exchange = _start_copies("small_exchange_start", [small_mine], [lax.empty(small_mine.shape, F32)], whole, 1,
                                   dep=cp_reduce[-1])
    red_ffn1 = finish_reduce("ffn1", ffn1_reduce, small_exchange[-1])
    red_gla = finish_reduce("gla", gla_reduce, small_exchange[-1])
    (small_sent,), (small_recv,) = _wait_copies("small_exchange_wait", small_exchange, whole, [red_ffn1[1], red_gla[1]])
    small_chip = _add2(small_sent, small_recv, "chip_sum_small")
    small_slots = lax.dynamic_update_slice(jnp.zeros((N_CHIPS,) + small_chip.shape, F32), small_chip[None], (chip, 0, 0))
    small_reduce = _start_copies("small_scatter_start", [small_chip], [small_slots], _plan_scatter(0), 3)

    big = {"w1": (ffn_w1, m_ffn_w1, v_ffn_w1), "w2": (ffn_w2, m_ffn_w2, v_ffn_w2),
           "cp_in": (cp_w_in, m_cp_w_in, v_cp_w_in), "cp_out": (cp_w_out, m_cp_w_out, v_cp_w_out),
           "gla_in": (gla_w_in, m_gla_w_in, v_gla_w_in), "gla_out": (gla_w_out, m_gla_w_out, v_gla_w_out)}
    other_idx = (1 - core).reshape(1).astype(jnp.int32)

    def adamw_by_halves(tag, reduced, dep=None):
        flat = [r for n in reduced for r in reduced[n]]
        join_plan = _plan_exchange(0)
        join = _start_copies(tag + "_join_start", flat, [lax.empty(r.shape, F32) for r in flat], join_plan, len(flat),
                             dep=dep)
        views = {n: [_split_rows(a) for a in big[n]] for n in reduced}
        own, k = {}, 0
        for n in reduced:
            mine = join[2][k:k + len(reduced[n])]
            k += len(reduced[n])
            own[n] = _adamw_half(views[n][0], mine, views[n][1], views[n][2], c_idx, None, "adamw_%s_own" % n)
        _, arrived_halves = _wait_copies(tag + "_join_wait", join, join_plan, [own[n][1] for n in reduced])
        outs, k = {}, 0
        for n in reduced:
            theirs = arrived_halves[k:k + len(reduced[n])]
            k += len(reduced[n])
            res = _adamw_half(views[n][0], theirs, views[n][1], views[n][2], other_idx, own[n], "adamw_%s_sibling" % n)
            outs[n] = [o.reshape(big[n][0].shape) for o in res]
        return outs

    big_out = adamw_by_halves("gla", {"gla_in": [red_gla[0]], "gla_out": [red_gla[1]]}, dep=small_reduce[-1])
    red_ffn0 = finish_reduce("ffn0", ffn0_reduce, big_out["gla_out"][1])
    big_out.update(adamw_by_halves("ffn", {"w1": [red_ffn0[0], red_ffn1[0]], "w2": [red_ffn0[1], red_ffn1[1]]}))
    red_cp = finish_reduce("cp", cp_reduce, big_out["w2"][1])
    _, (small_landed,) = _wait_copies("small_scatter_wait", small_reduce, _plan_scatter(0), big_out["w2"][1])
    small_red = _sum_slots(small_landed, "slot_sum_small")
    big_out.update(adamw_by_halves("cp", {"cp_in": [red_cp[0]], "cp_out": [red_cp[1]]}))

    (g_meta, g_mix, g_ffn, g_conv_w, g_conv_b, g_ln_g, g_ln_b, g_pool_w, g_pool_scale, g_gate_w, g_gate_b, g_head,
     g_final, loss_sum) = _unpack(small_red, [a.shape for a in small_full])
    g_meta = _col_shard(g_meta, chip, meta_tokens.shape[-1])
    g_conv_w = _col_shard(g_conv_w, chip, cp_conv_w.shape[-1])
    g_gate_w = _col_shard(g_gate_w, chip, gla_gate_w2.shape[-1])
    g_gate_b = _col_shard(g_gate_b, chip, gla_gate_b.shape[-1])
    g_head = _col_shard(g_head, chip, gla_head_g.shape[-1])
    small_w = [meta_tokens, mix_norm_g, ffn_norm_g, cp_conv_w, cp_conv_b, cp_ln_g, cp_ln_b, cp_pool_w, cp_pool_scale,
               gla_gate_w2, gla_gate_b, gla_head_g, final_norm_g]
    small_m = [m_meta_tokens, m_mix_norm_g, m_ffn_norm_g, m_cp_conv_w, m_cp_conv_b, m_cp_ln_g, m_cp_ln_b, m_cp_pool_w,
               m_cp_pool_scale, m_gla_gate_w2, m_gla_gate_b, m_gla_head_g, m_final_norm_g]
    small_v = [v_meta_tokens, v_mix_norm_g, v_ffn_norm_g, v_cp_conv_w, v_cp_conv_b, v_cp_ln_g, v_cp_ln_b, v_cp_pool_w,
               v_cp_pool_scale, v_gla_gate_w2, v_gla_gate_b, v_gla_head_g, v_final_norm_g]
    small_g = [g_meta, g_mix, g_ffn, g_conv_w, g_conv_b, g_ln_g, g_ln_b, g_pool_w, g_pool_scale, g_gate_w, g_gate_b,
               g_head, g_final]
    shapes = [w.shape for w in small_w]
    small_g = [g.reshape(s) for g, s in zip(small_g, shapes)]
    at_least_2d = lambda arrs: [a.reshape(1, -1) if a.ndim == 1 else a for a in arrs]
    s_delta, s_m, s_v = _adamw_many(at_least_2d(small_w), at_least_2d(small_g), at_least_2d(small_m), at_least_2d(small_v))
    s_delta, s_m, s_v = [[a.reshape(s) for a, s in zip(group, shapes)] for group in (s_delta, s_m, s_v)]

    order = ["meta", "mix", "ffn", "w1", "w2", "cp_in", "conv_w", "conv_b", "ln_g", "ln_b", "pool_w", "pool_scale",
             "cp_out", "gla_in", "gate_w", "gate_b", "head", "gla_out", "final"]
    small_names = ["meta", "mix", "ffn", "conv_w", "conv_b", "ln_g", "ln_b", "pool_w", "pool_scale", "gate_w", "gate_b",
                   "head", "final"]
    big_names = ["w1", "w2", "cp_in", "cp_out", "gla_in", "gla_out"]
    table = {n: (small_g[i], s_delta[i], s_m[i], s_v[i]) for i, n in enumerate(small_names)}
    table.update({n: tuple(big_out[n]) for n in big_names})
    loss = loss_sum.reshape(())
    return (loss, grad_x, *[table[n][0] for n in order], *[table[n][1] for n in order],
            *[table[n][2] for n in order], *[table[n][3] for n in order])
```

```python
import functools

import jax
import jax.numpy as jnp
from jax import lax
from jax.experimental import pallas as pl
from jax.experimental.pallas import tpu as pltpu

F32 = jnp.float32
BF16 = jnp.bfloat16

D_MODEL = 1024
N_META = 16
CHUNK = 64
PAD_ROWS = CHUNK - N_META
EPS = 1e-5
CONV_DIM = 512
CONV_WIDTH = 31
CONV_HALO = 32
POOL_DIM = 512
POOL_WINDOWS = (2, 4, 8, 16)
POOL_GROUP = 128
POOL_HALO = 16
CP_IN = 2 * CONV_DIM + POOL_DIM
GLA_HEADS = 4
GLA_DK = 512
GLA_DV = 1024
GLA_HK = GLA_DK // GLA_HEADS
GLA_HV = GLA_DV // GLA_HEADS
GATE_RANK = 16
GATE_PAD = 128
GATE_NORM = 16.0
GLA_IN = 2 * GLA_DK + 2 * GLA_DV + GATE_RANK
GLA_IN_PAD = 2 * GLA_DK + 2 * GLA_DV + GATE_PAD
GLA_COLS = 1280
N_CHIPS = 4
ADAM_LR = 0.001
ADAM_B1 = 0.9
ADAM_B2 = 0.999
ADAM_EPS = 1e-08
ADAM_WD = 0.01
ADAM_STEP = 10

VMEM_LIMIT_BYTES = 56 * 1024 * 1024
ROW_TILE_TARGET = 832
TOKEN_TILE_TARGET = 1040
PACK_WIDTH = 1024
MESH = pl.DeviceIdType.MESH
HBM_SPEC = pl.BlockSpec(memory_space=pltpu.HBM)
ANY_SPEC = pl.BlockSpec(memory_space=pl.ANY)
SEM_SPEC = pl.BlockSpec(memory_space=pltpu.SEMAPHORE)
SIDE_EFFECT = pltpu.SideEffectType.DATAFLOW_SIDE_EFFECTING


def _cparams(*sem):
    return pltpu.CompilerParams(dimension_semantics=sem, vmem_limit_bytes=VMEM_LIMIT_BYTES)


def _row_tile(t, target, mult):
    best = mult
    for cand in range(mult, min(t, target) + 1, mult):
        if t % cand == 0:
            best = cand
    assert t % best == 0, (t, best)
    return best


def _rms(h, g):
    return h * lax.rsqrt(jnp.mean(h * h, axis=-1, keepdims=True) + EPS) * g


def _rms_bwd(h, g, du):
    r = lax.rsqrt(jnp.mean(h * h, axis=-1, keepdims=True) + EPS)
    xhat = h * r
    dxh = du * g
    dh = r * (dxh - xhat * jnp.mean(dxh * xhat, axis=-1, keepdims=True))
    return dh, du * xhat


def _valid_rows(i, tm):
    row = i * tm + lax.broadcasted_iota(jnp.int32, (tm, 1), 0)
    return row >= PAD_ROWS


def _dot(a, b):
    return jnp.dot(a, b, preferred_element_type=F32)


def _dot_nt(a, b):
    return lax.dot_general(a, b, (((1,), (1,)), ((), ())), preferred_element_type=F32)


def _dot_tn(a, b):
    return lax.dot_general(a, b, (((0,), (0,)), ((), ())), preferred_element_type=F32)


def _accumulate(ref, val, first):
    @pl.when(first)
    def _():
        ref[...] = val

    @pl.when(jnp.logical_not(first))
    def _():
        ref[...] += val


def _call_after(dep, body, n_in, in_specs, args, **kw):
    if dep is None:
        return pl.pallas_call(body, in_specs=in_specs, **kw)(*args)

    def with_dep(*refs):
        body(*refs[:n_in], *refs[n_in + 1:])

    return pl.pallas_call(with_dep, in_specs=list(in_specs) + [ANY_SPEC], **kw)(*args, dep)


def _norm_matmul(h, g, w, nc, name, dep=None):
    t, d = h.shape
    n = w.shape[1]
    tm = _row_tile(t, TOKEN_TILE_TARGET, 16)

    def body(h_ref, g_ref, w_ref, z_ref, u_ref):
        u = _rms(h_ref[...], g_ref[...]).astype(BF16)
        u_ref[...] = u
        for n0 in range(0, n, nc):
            n1 = min(n0 + nc, n)
            z_ref[:, n0:n1] = _dot(u, w_ref[:, n0:n1]).astype(BF16)

    return _call_after(
        dep, body, 3,
        [pl.BlockSpec((tm, d), lambda i: (i, 0)), pl.BlockSpec((1, d), lambda i: (0, 0)),
         pl.BlockSpec((d, n), lambda i: (0, 0))], (h, g, w), grid=(t // tm,),
        out_specs=[pl.BlockSpec((tm, n), lambda i: (i, 0)), pl.BlockSpec((tm, d), lambda i: (i, 0))],
        out_shape=[jax.ShapeDtypeStruct((t, n), BF16), jax.ShapeDtypeStruct((t, d), BF16)],
        compiler_params=_cparams("parallel"), name=name)


def _matmul_residual(a, w, h, name, dep=None):
    t, k = a.shape
    d = w.shape[1]
    tm = _row_tile(t, TOKEN_TILE_TARGET, 16)

    def body(a_ref, w_ref, h_ref, o_ref):
        o_ref[...] = h_ref[...] + _dot(a_ref[...], w_ref[...])

    return _call_after(
        dep, body, 3,
        [pl.BlockSpec((tm, k), lambda i: (i, 0)), pl.BlockSpec((k, d), lambda i: (0, 0)),
         pl.BlockSpec((tm, d), lambda i: (i, 0))], (a, w, h), grid=(t // tm,),
        out_specs=pl.BlockSpec((tm, d), lambda i: (i, 0)),
        out_shape=jax.ShapeDtypeStruct((t, d), F32),
        compiler_params=_cparams("parallel"), name=name)


def _ffn_fwd(h, g, w1g, w2g, name):
    t, d = h.shape
    ns, ffs = w1g.shape[0], w1g.shape[2]
    tm = _row_tile(t, TOKEN_TILE_TARGET, 16)

    def body(h_ref, g_ref, w1_ref, w2_ref, ho_ref, hp_ref, u_ref, acc_ref):
        s = pl.program_id(1)

        @pl.when(s == 0)
        def _():
            u_ref[...] = _rms(h_ref[...], g_ref[...]).astype(BF16)

        hp = _dot(u_ref[...], w1_ref[...])
        hp_ref[...] = hp.astype(BF16)
        a = jnp.maximum(hp, 0.0)
        _accumulate(acc_ref, _dot((a * a).astype(BF16), w2_ref[...]), s == 0)

        @pl.when(s == ns - 1)
        def _():
            ho_ref[...] = h_ref[...] + acc_ref[...]

    return pl.pallas_call(
        body, grid=(t // tm, ns),
        in_specs=[pl.BlockSpec((tm, d), lambda i, s: (i, 0)), pl.BlockSpec((1, d), lambda i, s: (0, 0)),
                  pl.BlockSpec((None, d, ffs), lambda i, s: (s, 0, 0)),
                  pl.BlockSpec((None, ffs, d), lambda i, s: (s, 0, 0))],
        out_specs=[pl.BlockSpec((tm, d), lambda i, s: (i, 0)), pl.BlockSpec((tm, ffs), lambda i, s: (i, s)),
                   pl.BlockSpec((tm, d), lambda i, s: (i, 0))],
        out_shape=[jax.ShapeDtypeStruct((t, d), F32), jax.ShapeDtypeStruct((t, ns * ffs), BF16),
                   jax.ShapeDtypeStruct((t, d), BF16)],
        scratch_shapes=[pltpu.VMEM((tm, d), F32)],
        compiler_params=_cparams("parallel", "arbitrary"), name=name)(h, g, w1g, w2g)


def _ffn_bwd_data(dh, h, g, hp, w1g, w2g, name, dep=None):
    t, d = h.shape
    ns, ffs = w1g.shape[0], w1g.shape[2]
    tm = _row_tile(t, ROW_TILE_TARGET, CHUNK)

    def body(dh_ref, h_ref, g_ref, hp_ref, w1_ref, w2_ref, dhi_ref, dhp_ref, dg_ref, acc_ref):
        i, s = pl.program_id(0), pl.program_id(1)
        da = _dot_nt(dh_ref[...].astype(BF16), w2_ref[...])
        dhp = (da * (2.0 * jnp.maximum(hp_ref[...].astype(F32), 0.0))).astype(BF16)
        dhp_ref[...] = dhp
        _accumulate(acc_ref, _dot_nt(dhp, w1_ref[...]), s == 0)

        @pl.when(s == ns - 1)
        def _():
            dhn, dgr = _rms_bwd(h_ref[...], g_ref[...], acc_ref[...])
            dhi_ref[...] = jnp.where(_valid_rows(i, tm), dh_ref[...] + dhn, 0.0)
            _accumulate(dg_ref, jnp.sum(dgr, axis=0, keepdims=True), i == 0)

    return _call_after(
        dep, body, 6,
        [pl.BlockSpec((tm, d), lambda i, s: (i, 0)), pl.BlockSpec((tm, d), lambda i, s: (i, 0)),
         pl.BlockSpec((1, d), lambda i, s: (0, 0)), pl.BlockSpec((tm, ffs), lambda i, s: (i, s)),
         pl.BlockSpec((None, d, ffs), lambda i, s: (s, 0, 0)),
         pl.BlockSpec((None, ffs, d), lambda i, s: (s, 0, 0))], (dh, h, g, hp, w1g, w2g), grid=(t // tm, ns),
        out_specs=[pl.BlockSpec((tm, d), lambda i, s: (i, 0)), pl.BlockSpec((tm, ffs), lambda i, s: (i, s)),
                   pl.BlockSpec((1, d), lambda i, s: (0, 0))],
        out_shape=[jax.ShapeDtypeStruct((t, d), F32), jax.ShapeDtypeStruct((t, ns * ffs), BF16),
                   jax.ShapeDtypeStruct((1, d), F32)],
        scratch_shapes=[pltpu.VMEM((tm, d), F32)],
        compiler_params=_cparams("arbitrary", "arbitrary"), name=name)


WGRAD_ROWS = 2048
WGRAD_ROWS_BF16 = 4096


def _wgrad(x, dy, nb, xc, yc, x_by_block, dy_by_block, relu2, name, dep=None, rows=WGRAD_ROWS):
    t = x.shape[0]
    tk = _row_tile(t - CHUNK, rows, CHUNK)

    def prep(xv):
        if relu2:
            xv = jnp.maximum(xv.astype(F32), 0.0)
            xv = xv * xv
        return xv.astype(BF16)

    nk = (t - CHUNK) // tk

    def body(xh_ref, dyh_ref, x_ref, dy_ref, o_ref, acc_ref):
        k = pl.program_id(1)
        p = _dot_tn(prep(x_ref[...]), dy_ref[...].astype(BF16))

        @pl.when(k == 0)
        def _():
            acc_ref[...] = p + _dot_tn(prep(xh_ref[...]), dyh_ref[...].astype(BF16))

        @pl.when(k > 0)
        def _():
            acc_ref[...] += p

        @pl.when(k == nk - 1)
        def _():
            o_ref[...] = acc_ref[...].astype(BF16)

    def head(width, by_block):
        return pl.BlockSpec((CHUNK, width), (lambda b, k: (0, b)) if by_block else (lambda b, k: (0, 0)))

    def rest(width, by_block):
        def index(b, k):
            return pl.multiple_of(CHUNK + k * tk, CHUNK), (pl.multiple_of(b * width, 128) if by_block else 0)
        return pl.BlockSpec((pl.Element(tk), pl.Element(width)), index)

    return _call_after(
        dep, body, 4,
        [head(xc, x_by_block), head(yc, dy_by_block), rest(xc, x_by_block), rest(yc, dy_by_block)], (x, dy, x, dy),
        grid=(nb, nk),
        out_specs=pl.BlockSpec((None, xc, yc), lambda b, k: (b, 0, 0)),
        out_shape=jax.ShapeDtypeStruct((nb, xc, yc), BF16),
        scratch_shapes=[pltpu.VMEM((xc, yc), F32)],
        compiler_params=_cparams("parallel", "arbitrary"), name=name)


def _dgrad(dh, w, name, dep=None):
    t, d = dh.shape
    k = w.shape[0]
    tm = _row_tile(t, TOKEN_TILE_TARGET, 16)

    def body(dh_ref, w_ref, o_ref):
        o_ref[...] = _dot_nt(dh_ref[...].astype(BF16), w_ref[...]).astype(BF16)

    return _call_after(
        dep, body, 2,
        [pl.BlockSpec((tm, d), lambda i: (i, 0)), pl.BlockSpec((k, d), lambda i: (0, 0))], (dh, w), grid=(t // tm,),
        out_specs=pl.BlockSpec((tm, k), lambda i: (i, 0)),
        out_shape=jax.ShapeDtypeStruct((t, k), BF16),
        compiler_params=_cparams("parallel"), name=name)


def _dgrad_norm_bwd(dz, w, h, g, dh, nc, name):
    t, d = h.shape
    n = w.shape[1]
    tm = _row_tile(t, ROW_TILE_TARGET // 2, 16)

    def body(dz_ref, w_ref, h_ref, g_ref, dh_ref, dhi_ref, dg_ref):
        i = pl.program_id(0)
        du = jnp.zeros((tm, d), F32)
        for n0 in range(0, n, nc):
            n1 = min(n0 + nc, n)
            du = du + _dot_nt(dz_ref[:, n0:n1], w_ref[:, n0:n1])
        dhn, dgr = _rms_bwd(h_ref[...], g_ref[...], du)
        dhi_ref[...] = jnp.where(_valid_rows(i, tm), dh_ref[...] + dhn, 0.0)
        _accumulate(dg_ref, jnp.sum(dgr, axis=0, keepdims=True), i == 0)

    return pl.pallas_call(
        body, grid=(t // tm,),
        in_specs=[pl.BlockSpec((tm, n), lambda i: (i, 0)), pl.BlockSpec((d, n), lambda i: (0, 0)),
                  pl.BlockSpec((tm, d), lambda i: (i, 0)), pl.BlockSpec((1, d), lambda i: (0, 0)),
                  pl.BlockSpec((tm, d), lambda i: (i, 0))],
        out_specs=[pl.BlockSpec((tm, d), lambda i: (i, 0)), pl.BlockSpec((1, d), lambda i: (0, 0))],
        out_shape=[jax.ShapeDtypeStruct((t, d), F32), jax.ShapeDtypeStruct((1, d), F32)],
        compiler_params=_cparams("arbitrary"), name=name)(dz, w, h, g, dh)


def _dgrad_norm_bwd_input(dz, w, h, g, dh, nc, name):
    t, d = h.shape
    n = w.shape[1]
    tl = _row_tile(t - CHUNK, 512, CHUNK)

    def grads(dz_ref, w_ref, h_ref, g_ref, dh_ref, rows):
        du = jnp.zeros((rows, d), F32)
        for n0 in range(0, n, nc):
            n1 = min(n0 + nc, n)
            du = du + _dot_nt(dz_ref[:, n0:n1], w_ref[:, n0:n1])
        dhn, dgr = _rms_bwd(h_ref[...], g_ref[...], du)
        return dh_ref[...] + dhn, jnp.sum(dgr, axis=0, keepdims=True)

    def rest_body(dz_ref, w_ref, h_ref, g_ref, dh_ref, dg_head_ref, dx_ref, dg_ref):
        dx, dg = grads(dz_ref, w_ref, h_ref, g_ref, dh_ref, tl)
        dx_ref[...] = dx

        @pl.when(pl.program_id(0) == 0)
        def _():
            dg_ref[...] = dg_head_ref[...] + dg

        @pl.when(pl.program_id(0) > 0)
        def _():
            dg_ref[...] += dg

    def head_body(dz_ref, w_ref, h_ref, g_ref, dh_ref, dx_ref, dg_ref):
        dx, dg = grads(dz_ref, w_ref, h_ref, g_ref, dh_ref, CHUNK)
        dx_ref[...] = jnp.where(_valid_rows(0, CHUNK), dx, 0.0)
        dg_ref[...] = dg

    def shifted(width):
        return pl.BlockSpec((pl.Element(tl), pl.Element(width)), lambda i: (pl.multiple_of(CHUNK + i * tl, CHUNK), 0))

    whole = [pl.BlockSpec((d, n), lambda i: (0, 0)), pl.BlockSpec((1, d), lambda i: (0, 0))]
    head = lambda width: pl.BlockSpec((CHUNK, width), lambda i: (0, 0))
    dh_head, dg_head = pl.pallas_call(
        head_body, grid=(1,), in_specs=[head(n), whole[0], head(d), whole[1], head(d)],
        out_specs=[head(d), whole[1]],
        out_shape=[jax.ShapeDtypeStruct((CHUNK, d), F32), jax.ShapeDtypeStruct((1, d), F32)],
        compiler_params=_cparams("arbitrary"), name=name + "_head")(dz, w, h, g, dh)
    dx, dg = pl.pallas_call(
        rest_body, grid=((t - CHUNK) // tl,),
        in_specs=[shifted(n), whole[0], shifted(d), whole[1], shifted(d), whole[1]],
        out_specs=[pl.BlockSpec((tl, d), lambda i: (i, 0)), whole[1]],
        out_shape=[jax.ShapeDtypeStruct((t - CHUNK, d), F32), jax.ShapeDtypeStruct((1, d), F32)],
        compiler_params=_cparams("arbitrary"), name=name)(dz, w, h, g, dh, dg_head)
    return dx, dh_head, dg


def _loss_bwd(h, g, target):
    t, d = h.shape
    tl = _row_tile(t - CHUNK, 1024, CHUNK)

    def body(h_ref, g_ref, t_ref, dh_ref, dg_ref, loss_ref):
        i = pl.program_id(0)
        hv, gv = h_ref[...], g_ref[...]
        err = _rms(hv, gv) - t_ref[...]
        part = 0.5 * jnp.sum(jnp.mean(err * err, axis=-1, keepdims=True), axis=0, keepdims=True)
        dhn, dgr = _rms_bwd(hv, gv, err * (1.0 / d))
        dh_ref[...] = dhn
        _accumulate(dg_ref, jnp.sum(dgr, axis=0, keepdims=True), i == 0)
        _accumulate(loss_ref, jnp.broadcast_to(part, (8, 128)), i == 0)

    shifted = pl.BlockSpec((pl.Element(tl), pl.Element(d)), lambda i: (pl.multiple_of(CHUNK + i * tl, CHUNK), 0))
    dh, dg, loss = pl.pallas_call(
        body, grid=((t - CHUNK) // tl,),
        in_specs=[shifted, pl.BlockSpec((1, d), lambda i: (0, 0)), pl.BlockSpec((tl, d), lambda i: (i, 0))],
        out_specs=[shifted, pl.BlockSpec((1, d), lambda i: (0, 0)), pl.BlockSpec((8, 128), lambda i: (0, 0))],
        out_shape=[jax.ShapeDtypeStruct((t, d), F32), jax.ShapeDtypeStruct((1, d), F32),
                   jax.ShapeDtypeStruct((8, 128), F32)],
        compiler_params=_cparams("arbitrary"), name="loss_bwd")(h, g, target)

    def zero_head(dh_ref, o_ref):
        o_ref[...] = jnp.zeros_like(o_ref)

    dh = pl.pallas_call(
        zero_head, grid=(1,), in_specs=[ANY_SPEC], out_specs=pl.BlockSpec((CHUNK, d), lambda i: (0, 0)),
        out_shape=jax.ShapeDtypeStruct((t, d), F32), input_output_aliases={0: 0}, name="loss_bwd_head")(dh)
    return dh, dg, loss


CONV_BLOCK = 32


def _silu(x):
    return x * jax.nn.sigmoid(x)


def _row_shifts(win):
    n = win.shape[0]
    return [win] + [pltpu.roll(win, n - j, 0) for j in range(1, 8)]


def _cp_seq_fwd(z, conv_w, conv_b, ln_g, ln_b, pool_w, pool_scale):
    t = z.shape[0]
    tm = _row_tile(t, ROW_TILE_TARGET, CHUNK)

    def body(z_ref, cw_ref, cb_ref, lg_ref, lb_ref, pw_ref, ps_ref, c_ref, pm_ref, mix_ref, gbuf, pbuf):
        i = pl.program_id(0)

        @pl.when(i == 0)
        def _():
            gbuf[0:CONV_HALO, :] = jnp.zeros((CONV_HALO, CONV_DIM), F32)
            pbuf[0:POOL_HALO, :] = jnp.zeros((POOL_HALO, POOL_DIM), F32)

        @pl.when(i > 0)
        def _():
            gbuf[0:CONV_HALO, :] = gbuf[tm:tm + CONV_HALO, :]
            pbuf[0:POOL_HALO, :] = pbuf[tm:tm + POOL_HALO, :]

        av = z_ref[:, 0:CONV_DIM].astype(F32)
        ag = z_ref[:, CONV_DIM:2 * CONV_DIM].astype(F32)
        gbuf[CONV_HALO:CONV_HALO + tm, :] = av * jax.nn.sigmoid(ag)
        pbuf[POOL_HALO:POOL_HALO + tm, :] = z_ref[:, 2 * CONV_DIM:CP_IN].astype(F32)

        def conv_block(rb, carry):
            base = pl.multiple_of(rb * CONV_BLOCK, CONV_BLOCK)
            shifted = _row_shifts(gbuf[pl.ds(base, CONV_BLOCK + CONV_HALO), :])
            acc = jnp.zeros((CONV_BLOCK, CONV_DIM), F32)
            for k in range(CONV_WIDTH):
                whole, part = divmod(CONV_HALO - (CONV_WIDTH - 1) + k, 8)
                acc = acc + cw_ref[k:k + 1, :] * shifted[part][8 * whole:8 * whole + CONV_BLOCK, :]
            c_ref[pl.ds(base, CONV_BLOCK), :] = acc + cb_ref[...]
            return carry

        lax.fori_loop(0, tm // CONV_BLOCK, conv_block, 0)

        c = c_ref[...]
        mu = jnp.mean(c, axis=-1, keepdims=True)
        xc = c - mu
        ln = xc * lax.rsqrt(jnp.mean(xc * xc, axis=-1, keepdims=True) + EPS) * lg_ref[...] + lb_ref[...]
        row = i * tm + lax.broadcasted_iota(jnp.int32, (tm, 1), 0)
        mix_ref[:, 0:CONV_DIM] = jnp.where(row >= PAD_ROWS, _silu(ln), 0.0).astype(BF16)

        tpos = (row - PAD_ROWS + 1).astype(F32)
        for gi, wdw in enumerate(POOL_WINDOWS):
            lo = POOL_GROUP * gi
            run, step = pbuf[:, lo:lo + POOL_GROUP], 1
            cur = run[POOL_HALO:POOL_HALO + tm, :]
            while step < wdw:
                run = run + pltpu.roll(run, step, 0)
                step *= 2
            pm = (run[POOL_HALO:POOL_HALO + tm, :] / jnp.clip(tpos, 1.0, float(wdw)) - cur).astype(BF16)
            pm_ref[:, lo:lo + POOL_GROUP] = pm
            pg = _dot(pm, pw_ref[gi].astype(BF16))
            mix_ref[:, CONV_DIM + lo:CONV_DIM + lo + POOL_GROUP] = (pg * ps_ref[:, lo:lo + POOL_GROUP]).astype(BF16)

    vec = pl.BlockSpec((1, CONV_DIM), lambda i: (0, 0))
    return pl.pallas_call(
        body, grid=(t // tm,),
        in_specs=[pl.BlockSpec((tm, CP_IN), lambda i: (i, 0)),
                  pl.BlockSpec((CONV_WIDTH, CONV_DIM), lambda i: (0, 0)), vec, vec, vec,
                  pl.BlockSpec((len(POOL_WINDOWS), POOL_GROUP, POOL_GROUP), lambda i: (0, 0, 0)), vec],
        out_specs=[pl.BlockSpec((tm, CONV_DIM), lambda i: (i, 0)), pl.BlockSpec((tm, POOL_DIM), lambda i: (i, 0)),
                   pl.BlockSpec((tm, CONV_DIM + POOL_DIM), lambda i: (i, 0))],
        out_shape=[jax.ShapeDtypeStruct((t, CONV_DIM), F32), jax.ShapeDtypeStruct((t, POOL_DIM), BF16),
                   jax.ShapeDtypeStruct((t, CONV_DIM + POOL_DIM), BF16)],
        scratch_shapes=[pltpu.VMEM((tm + CONV_HALO, CONV_DIM), F32), pltpu.VMEM((tm + POOL_HALO, POOL_DIM), F32)],
        compiler_params=_cparams("arbitrary"), name="cp_seq_fwd")(z, conv_w, conv_b, ln_g, ln_b, pool_w, pool_scale)


def _cp_seq_bwd(dmix, z, c, pm, conv_w, ln_g, ln_b, pool_w, pool_scale, dep=None):
    t = z.shape[0]
    tm = _row_tile(t, ROW_TILE_TARGET, CHUNK)
    nt = t // tm

    def body(dmix_ref, z_ref, c_ref, pm_ref, cw_ref, lg_ref, lb_ref, pw_ref, ps_ref,
             dz_ref, dcw_ref, dvec_ref, dpw_ref, dcbuf, qbuf, glu_buf, dwacc):
        i = pl.program_id(0)
        tile = nt - 1 - i

        @pl.when(i == 0)
        def _():
            dcbuf[tm:tm + CONV_HALO, :] = jnp.zeros((CONV_HALO, CONV_DIM), F32)
            qbuf[tm:tm + POOL_HALO, :] = jnp.zeros((POOL_HALO, POOL_DIM), F32)
            dcw_ref[...] = jnp.zeros_like(dcw_ref)
            dwacc[...] = jnp.zeros_like(dwacc)
            dvec_ref[...] = jnp.zeros_like(dvec_ref)
            dpw_ref[...] = jnp.zeros_like(dpw_ref)

        @pl.when(i > 0)
        def _():
            dcbuf[tm:tm + CONV_HALO, :] = dcbuf[0:CONV_HALO, :]
            qbuf[tm:tm + POOL_HALO, :] = qbuf[0:POOL_HALO, :]

        row = tile * tm + lax.broadcasted_iota(jnp.int32, (tm, 1), 0)
        cv = c_ref[...]
        mu = jnp.mean(cv, axis=-1, keepdims=True)
        xc = cv - mu
        rstd = lax.rsqrt(jnp.mean(xc * xc, axis=-1, keepdims=True) + EPS)
        xhat = xc * rstd
        ln = xhat * lg_ref[...] + lb_ref[...]
        sg = jax.nn.sigmoid(ln)
        da = jnp.where(row >= PAD_ROWS, dmix_ref[:, 0:CONV_DIM].astype(F32), 0.0)
        dln = da * (sg * (1.0 + ln * (1.0 - sg)))
        dxh = dln * lg_ref[...]
        dc = rstd * (dxh - jnp.mean(dxh, axis=-1, keepdims=True) - xhat * jnp.mean(dxh * xhat, axis=-1, keepdims=True))
        dcbuf[0:tm, :] = dc
        dvec_ref[0:1, :] += jnp.sum(dc, axis=0, keepdims=True)
        dvec_ref[1:2, :] += jnp.sum(dln * xhat, axis=0, keepdims=True)
        dvec_ref[2:3, :] += jnp.sum(dln, axis=0, keepdims=True)

        av = z_ref[:, 0:CONV_DIM].astype(F32)
        sig_g = jax.nn.sigmoid(z_ref[:, CONV_DIM:2 * CONV_DIM].astype(F32))
        glu_buf[...] = av * sig_g

        def conv_block(rb, carry):
            base = pl.multiple_of(rb * CONV_BLOCK, CONV_BLOCK)
            shifted = _row_shifts(dcbuf[pl.ds(base, CONV_BLOCK + CONV_HALO), :])
            glu = glu_buf[pl.ds(base, CONV_BLOCK), :]
            acc = jnp.zeros((CONV_BLOCK, CONV_DIM), F32)
            for k in range(CONV_WIDTH):
                whole, part = divmod(CONV_WIDTH - 1 - k, 8)
                slab = shifted[part][8 * whole:8 * whole + CONV_BLOCK, :]
                acc = acc + cw_ref[k:k + 1, :] * slab
                prod = slab * glu
                part = prod[0:8]
                for q in range(1, CONV_BLOCK // 8):
                    part = part + prod[8 * q:8 * q + 8]
                dwacc[k] += part
            glu_buf[pl.ds(base, CONV_BLOCK), :] = acc
            return carry

        lax.fori_loop(0, tm // CONV_BLOCK, conv_block, 0)

        @pl.when(i == nt - 1)
        def _():
            for k in range(CONV_WIDTH):
                dcw_ref[k:k + 1, :] = jnp.sum(dwacc[k], axis=0, keepdims=True)
        dglu = glu_buf[...]
        dz_ref[:, 0:CONV_DIM] = (dglu * sig_g).astype(BF16)
        dz_ref[:, CONV_DIM:2 * CONV_DIM] = (dglu * av * sig_g * (1.0 - sig_g)).astype(BF16)

        tpos = (row - PAD_ROWS + 1).astype(F32)
        for gi, wdw in enumerate(POOL_WINDOWS):
            lo = POOL_GROUP * gi
            dp = dmix_ref[:, CONV_DIM + lo:CONV_DIM + lo + POOL_GROUP].astype(F32)
            pmv = pm_ref[:, lo:lo + POOL_GROUP]
            pwb = pw_ref[gi].astype(BF16)
            dvec_ref[3:4, lo:lo + POOL_GROUP] += jnp.sum(dp * _dot(pmv, pwb), axis=0, keepdims=True)
            dq = (dp * ps_ref[:, lo:lo + POOL_GROUP]).astype(BF16)
            dpw_ref[gi] += _dot_tn(pmv, dq)
            dpm = _dot_nt(dq, pwb)
            qbuf[0:tm, lo:lo + POOL_GROUP] = dpm / jnp.clip(tpos, 1.0, float(wdw))
            run, step = qbuf[:, lo:lo + POOL_GROUP], 1
            while step < wdw:
                run = run + pltpu.roll(run, tm + POOL_HALO - step, 0)
                step *= 2
            dz_ref[:, 2 * CONV_DIM + lo:2 * CONV_DIM + lo + POOL_GROUP] = (run[0:tm, :] - dpm).astype(BF16)

    vec = pl.BlockSpec((1, CONV_DIM), lambda i: (0, 0))
    rev = lambda i: (nt - 1 - i, 0)
    return _call_after(
        dep, body, 9,
        [pl.BlockSpec((tm, CONV_DIM + POOL_DIM), rev), pl.BlockSpec((tm, CP_IN), rev),
         pl.BlockSpec((tm, CONV_DIM), rev), pl.BlockSpec((tm, POOL_DIM), rev),
         pl.BlockSpec((CONV_WIDTH, CONV_DIM), lambda i: (0, 0)), vec, vec,
         pl.BlockSpec((len(POOL_WINDOWS), POOL_GROUP, POOL_GROUP), lambda i: (0, 0, 0)), vec],
        (dmix, z, c, pm, conv_w, ln_g, ln_b, pool_w, pool_scale), grid=(nt,),
        out_specs=[pl.BlockSpec((tm, CP_IN), rev), pl.BlockSpec((CONV_WIDTH + 1, CONV_DIM), lambda i: (0, 0)),
                   pl.BlockSpec((8, CONV_DIM), lambda i: (0, 0)),
                   pl.BlockSpec((len(POOL_WINDOWS), POOL_GROUP, POOL_GROUP), lambda i: (0, 0, 0))],
        out_shape=[jax.ShapeDtypeStruct((t, CP_IN), BF16), jax.ShapeDtypeStruct((CONV_WIDTH + 1, CONV_DIM), F32),
                   jax.ShapeDtypeStruct((8, CONV_DIM), F32),
                   jax.ShapeDtypeStruct((len(POOL_WINDOWS), POOL_GROUP, POOL_GROUP), F32)],
        scratch_shapes=[pltpu.VMEM((tm + CONV_HALO, CONV_DIM), F32), pltpu.VMEM((tm + POOL_HALO, POOL_DIM), F32),
                        pltpu.VMEM((tm, CONV_DIM), F32), pltpu.VMEM((CONV_WIDTH + 1, 8, CONV_DIM), F32)],
        compiler_params=_cparams("arbitrary"), name="cp_seq_bwd")


GLA_UNROLL = 2
Q0, K0, V0, G0, R0 =0, GLA_DK, 2 * GLA_DK, 2 * GLA_DK + GLA_DV, 2 * GLA_DK + 2 * GLA_DV


def _split3(x):
    hi = x.astype(BF16)
    r1 = x - hi.astype(F32)
    mid = r1.astype(BF16)
    lo = (r1 - mid.astype(F32)).astype(BF16)
    return hi, mid, lo


def _tri(strict):
    r = lax.broadcasted_iota(jnp.int32, (CHUNK, CHUNK), 0)
    c = lax.broadcasted_iota(jnp.int32, (CHUNK, CHUNK), 1)
    return ((r > c) if strict else (r >= c)).astype(BF16)


def _chunk_sums(x, cpt, strict, pieces):
    tri3 = jnp.broadcast_to(_tri(strict)[None], (cpt, CHUNK, CHUNK))
    acc = None
    for piece in _split3(x.reshape(cpt, CHUNK, x.shape[-1]))[:pieces]:
        part = jnp.einsum("bij,bjk->bik", tri3, piece, preferred_element_type=F32)
        acc = part if acc is None else acc + part
    return acc


def _chunk_decay(r, gw_ref, gb_ref, cpt):
    pre = _dot(r, gw_ref[...]) + gb_ref[...]
    lac = (jnp.minimum(pre, 0.0) - jnp.log(1.0 + jnp.exp(-jnp.abs(pre)))) * (1.0 / GATE_NORM)
    cum3 = _chunk_sums(lac, cpt, False, 3)
    return cum3, cum3[:, CHUNK - 1:CHUNK, :]


def _gla_seq_fwd(z, gate_w, gate_b, head_g, dep=None):
    t = z.shape[0]
    tm = _row_tile(t, ROW_TILE_TARGET, CHUNK)
    cpt = tm // CHUNK
    scale = GLA_HK ** -0.5

    def body(z_ref, gw_ref, gb_ref, hg_ref, o_ref, mix_ref, st_ref, state, kdec_s, e_s):
        @pl.when(pl.program_id(0) == 0)
        def _():
            state[...] = jnp.zeros_like(state)

        cum3, tot3 = _chunk_decay(z_ref[:, R0:R0 + GATE_PAD], gw_ref, gb_ref, cpt)
        dec = jnp.exp(jnp.broadcast_to(tot3, cum3.shape) - cum3).reshape(tm, GLA_DK)
        kdec_s[...] = (z_ref[:, K0:K0 + GLA_DK].astype(F32) * dec).astype(BF16)
        e_s[...] = jnp.exp(jnp.broadcast_to(tot3, (cpt, 8, GLA_DK))).reshape(cpt * 8, GLA_DK)

        def chunk(ci, carry):
            rows = pl.ds(pl.multiple_of(ci * CHUNK, CHUNK), CHUNK)
            e_all = e_s[pl.ds(pl.multiple_of(ci * 8, 8), 8), :][0:1, :]
            st_ref[ci] = state[...].astype(BF16)
            for hd in range(GLA_HEADS):
                ks = slice(hd * GLA_HK, (hd + 1) * GLA_HK)
                vs = slice(hd * GLA_HV, (hd + 1) * GLA_HV)
                v = z_ref[rows, V0 + hd * GLA_HV:V0 + (hd + 1) * GLA_HV]
                st = state[vs, :] * e_all[:, ks] + _dot_tn(v, kdec_s[rows, ks])
                state[vs, :] = st
                q = z_ref[rows, Q0 + hd * GLA_HK:Q0 + (hd + 1) * GLA_HK]
                o_ref[rows, vs] = (_dot_nt(q, st.astype(BF16)) * scale).astype(BF16)
            return carry

        lax.fori_loop(0, cpt, chunk, 0, unroll=GLA_UNROLL)

        for hd in range(GLA_HEADS):
            vs = slice(hd * GLA_HV, (hd + 1) * GLA_HV)
            on = _rms(o_ref[:, vs].astype(F32), hg_ref[...])
            gv = z_ref[:, G0 + hd * GLA_HV:G0 + (hd + 1) * GLA_HV].astype(F32)
            mix_ref[:, vs] = (on * _silu(gv)).astype(BF16)

    return _call_after(
        dep, body, 4,
        [pl.BlockSpec((tm, GLA_IN_PAD), lambda i: (i, 0)),
         pl.BlockSpec((GATE_PAD, GLA_DK), lambda i: (0, 0)), pl.BlockSpec((1, GLA_DK), lambda i: (0, 0)),
         pl.BlockSpec((1, GLA_HV), lambda i: (0, 0))], (z, gate_w, gate_b, head_g), grid=(t // tm,),
        out_specs=[pl.BlockSpec((tm, GLA_DV), lambda i: (i, 0)), pl.BlockSpec((tm, GLA_DV), lambda i: (i, 0)),
                   pl.BlockSpec((cpt, GLA_DV, GLA_HK), lambda i: (i, 0, 0))],
        out_shape=[jax.ShapeDtypeStruct((t, GLA_DV), BF16), jax.ShapeDtypeStruct((t, GLA_DV), BF16),
                   jax.ShapeDtypeStruct((t // CHUNK, GLA_DV, GLA_HK), BF16)],
        scratch_shapes=[pltpu.VMEM((GLA_DV, GLA_HK), F32), pltpu.VMEM((tm, GLA_DK), BF16),
                        pltpu.VMEM((cpt * 8, GLA_DK), F32)],
        compiler_params=_cparams("arbitrary"), name="gla_seq_fwd")


def _gla_seq_bwd(dmix, o, z, states, gate_w, gate_b, head_g, dep=None):
    t = z.shape[0]
    tm = _row_tile(t, ROW_TILE_TARGET, CHUNK)
    cpt = tm // CHUNK
    nt = t // tm
    scale = GLA_HK ** -0.5

    def body(dmix_ref, o_ref, z_ref, st_ref, gw_ref, gb_ref, hg_ref, dz_ref, dgw_ref, dgb_ref, dhg_ref,
             dstate, dec_s, kdec_s, dkdec_s, do_s, e_s, dtot_s):
        @pl.when(pl.program_id(0) == 0)
        def _():
            dstate[...] = jnp.zeros_like(dstate)
            dgw_ref[...] = jnp.zeros_like(dgw_ref)
            dgb_ref[...] = jnp.zeros_like(dgb_ref)
            dhg_ref[...] = jnp.zeros_like(dhg_ref)

        cum3, tot3 = _chunk_decay(z_ref[:, R0:R0 + GATE_PAD], gw_ref, gb_ref, cpt)
        dec = jnp.exp(jnp.broadcast_to(tot3, cum3.shape) - cum3).reshape(tm, GLA_DK)
        dec_s[...] = dec
        kdec = z_ref[:, K0:K0 + GLA_DK].astype(F32) * dec
        kdec_s[...] = kdec
        e3 = jnp.exp(tot3)
        e_s[...] = jnp.broadcast_to(e3, (cpt, 8, GLA_DK)).reshape(cpt * 8, GLA_DK)
        dhg = jnp.zeros((1, GLA_HV), F32)
        for hd in range(GLA_HEADS):
            ks = slice(hd * GLA_HK, (hd + 1) * GLA_HK)
            vs = slice(hd * GLA_HV, (hd + 1) * GLA_HV)
            gcols = slice(G0 + hd * GLA_HV, G0 + (hd + 1) * GLA_HV)
            ov = o_ref[:, vs].astype(F32)
            gv = z_ref[:, gcols].astype(F32)
            dm = dmix_ref[:, vs].astype(F32)
            sg = jax.nn.sigmoid(gv)
            rr = lax.rsqrt(jnp.mean(ov * ov, axis=-1, keepdims=True) + EPS)
            xhat = ov * rr
            don = dm * (gv * sg)
            dz_ref[:, gcols] = (dm * (xhat * hg_ref[...]) * (sg * (1.0 + gv * (1.0 - sg)))).astype(BF16)
            dhg = dhg + jnp.sum(don * xhat, axis=0, keepdims=True)
            dxh = don * hg_ref[...]
            do = (rr * (dxh - xhat * jnp.mean(dxh * xhat, axis=-1, keepdims=True)) * scale).astype(BF16)
            do_s[:, vs] = do
            v3 = z_ref[:, V0 + hd * GLA_HV:V0 + (hd + 1) * GLA_HV].reshape(cpt, CHUNK, GLA_HV)
            kdb3 = kdec[:, ks].astype(BF16).reshape(cpt, CHUNK, GLA_HK)
            st3 = st_ref[:, vs, :].astype(F32) * e3[:, :, ks] + jnp.einsum("bcv,bck->bvk", v3, kdb3,
                                                                            preferred_element_type=F32)
            dq3 = jnp.einsum("bcv,bvk->bck", do.reshape(cpt, CHUNK, GLA_HV), st3.astype(BF16), preferred_element_type=F32)
            dz_ref[:, Q0 + hd * GLA_HK:Q0 + (hd + 1) * GLA_HK] = dq3.reshape(tm, GLA_HK).astype(BF16)
        dhg_ref[...] += dhg

        def chunk(cj, carry):
            ci = cpt - 1 - cj
            rows = pl.ds(pl.multiple_of(ci * CHUNK, CHUNK), CHUNK)
            erows = pl.ds(pl.multiple_of(ci * 8, 8), 8)
            e_all = e_s[erows, :][0:1, :]
            for hd in range(GLA_HEADS):
                ks = slice(hd * GLA_HK, (hd + 1) * GLA_HK)
                vs = slice(hd * GLA_HV, (hd + 1) * GLA_HV)
                e = e_all[:, ks]
                kdb = kdec_s[rows, ks].astype(BF16)
                v = z_ref[rows, V0 + hd * GLA_HV:V0 + (hd + 1) * GLA_HV]
                q = z_ref[rows, Q0 + hd * GLA_HK:Q0 + (hd + 1) * GLA_HK]
                do = do_s[rows, vs]
                st_prev = st_ref[ci, vs, :].astype(F32)
                dst = dstate[vs, :] + _dot_tn(do, q)
                dstb = dst.astype(BF16)
                dkdec_s[rows, ks] = _dot(v, dstb)
                dz_ref[rows, V0 + hd * GLA_HV:V0 + (hd + 1) * GLA_HV] = _dot_nt(kdb, dstb).astype(BF16)
                dtot = jnp.sum(dst * st_prev, axis=0, keepdims=True) * e
                dtot_s[erows, ks] = jnp.broadcast_to(dtot, (8, GLA_HK))
                dstate[vs, :] = dst * e
            return carry

        lax.fori_loop(0, cpt, chunk, 0, unroll=GLA_UNROLL)

        dkdec = dkdec_s[...]
        dz_ref[:, K0:K0 + GLA_DK] = (dkdec * dec_s[...]).astype(BF16)
        before = _chunk_sums(dkdec * kdec_s[...], cpt, True, 2)
        dtot3 = dtot_s[...].reshape(cpt, 8, GLA_DK)[:, 0:1, :]
        dlac = (jnp.broadcast_to(dtot3, before.shape) + before).reshape(tm, GLA_DK)
        pre = _dot(z_ref[:, R0:R0 + GATE_PAD], gw_ref[...]) + gb_ref[...]
        dpre = dlac * (1.0 / GATE_NORM) * (1.0 - jax.nn.sigmoid(pre))
        dpb = dpre.astype(BF16)
        dz_ref[:, R0:R0 + GATE_PAD] = _dot_nt(dpb, gw_ref[...]).astype(BF16)
        dgw_ref[...] += _dot_tn(z_ref[:, R0:R0 + GATE_PAD], dpb)
        dgb_ref[...] += jnp.sum(dpre, axis=0, keepdims=True)

    rev = lambda i: (nt - 1 - i, 0)
    return _call_after(
        dep, body, 7,
        [pl.BlockSpec((tm, GLA_DV), rev), pl.BlockSpec((tm, GLA_DV), rev), pl.BlockSpec((tm, GLA_IN_PAD), rev),
         pl.BlockSpec((cpt, GLA_DV, GLA_HK), lambda i: (nt - 1 - i, 0, 0)),
         pl.BlockSpec((GATE_PAD, GLA_DK), lambda i: (0, 0)), pl.BlockSpec((1, GLA_DK), lambda i: (0, 0)),
         pl.BlockSpec((1, GLA_HV), lambda i: (0, 0))],
        (dmix, o, z, states, gate_w, gate_b, head_g), grid=(nt,),
        out_specs=[pl.BlockSpec((tm, GLA_IN_PAD), rev), pl.BlockSpec((GATE_PAD, GLA_DK), lambda i: (0, 0)),
                   pl.BlockSpec((1, GLA_DK), lambda i: (0, 0)), pl.BlockSpec((1, GLA_HV), lambda i: (0, 0))],
        out_shape=[jax.ShapeDtypeStruct((t, GLA_IN_PAD), BF16), jax.ShapeDtypeStruct((GATE_PAD, GLA_DK), F32),
                   jax.ShapeDtypeStruct((1, GLA_DK), F32), jax.ShapeDtypeStruct((1, GLA_HV), F32)],
        scratch_shapes=[pltpu.VMEM((GLA_DV, GLA_HK), F32), pltpu.VMEM((tm, GLA_DK), F32), pltpu.VMEM((tm, GLA_DK), F32),
                        pltpu.VMEM((tm, GLA_DK), F32), pltpu.VMEM((tm, GLA_DV), BF16),
                        pltpu.VMEM((cpt * 8, GLA_DK), F32), pltpu.VMEM((cpt * 8, GLA_DK), F32)],
        compiler_params=_cparams("arbitrary"), name="gla_seq_bwd")


def _sum_slots(x, name):
    n, r, cdim = x.shape
    tr = _row_tile(r, 256, 8)

    def body(x_ref, o_ref):
        acc = x_ref[0].astype(F32)
        for j in range(1, n):
            acc = acc + x_ref[j].astype(F32)
        o_ref[...] = acc

    return pl.pallas_call(
        body, grid=(r // tr,),
        in_specs=[pl.BlockSpec((n, tr, cdim), lambda i: (0, i, 0))],
        out_specs=pl.BlockSpec((tr, cdim), lambda i: (i, 0)),
        out_shape=jax.ShapeDtypeStruct((r, cdim), F32),
        compiler_params=_cparams("parallel"), name=name)(x)


def _sum_own_and_slots(own, slots, dev_idx, name):
    _, _, r, cdim = own.shape
    n = slots.shape[0]
    tr = _row_tile(r, 256, 8)

    def body(s_ref, own_ref, *rest):
        acc = own_ref[...].astype(F32)
        for other in rest[:n - 1]:
            acc = acc + other[...].astype(F32)
        rest[n - 1][...] = acc

    def slot(dd):
        return pl.BlockSpec((None, tr, cdim), lambda i, s: ((s[0] + dd) % n, i, 0))

    mine = pl.BlockSpec((None, None, tr, cdim), lambda i, s: (s[0] // 2, s[0] % 2, i, 0))
    return pl.pallas_call(
        body,
        grid_spec=pltpu.PrefetchScalarGridSpec(
            num_scalar_prefetch=1, grid=(r // tr,), in_specs=[mine] + [slot(dd) for dd in range(1, n)],
            out_specs=pl.BlockSpec((tr, cdim), lambda i, s: (i, 0))),
        out_shape=jax.ShapeDtypeStruct((r, cdim), F32),
        compiler_params=_cparams("parallel"), name=name)(dev_idx, own, *([slots] * (n - 1)))


def _add2(a, b, name):
    r, cdim = a.shape
    tr = _row_tile(r, 256, 8)

    def body(a_ref, b_ref, o_ref):
        o_ref[...] = a_ref[...] + b_ref[...]

    spec = pl.BlockSpec((tr, cdim), lambda i: (i, 0))
    return pl.pallas_call(body, grid=(r // tr,), in_specs=[spec, spec], out_specs=spec,
                          out_shape=jax.ShapeDtypeStruct((r, cdim), F32),
                          compiler_params=_cparams("parallel"), name=name)(a, b)


def _adamw_half(w, gs, m, v, half_idx, prev, name, dep=None):
    nl, _, h, cdim = w.shape
    tr = _row_tile(h, 256, 8)
    nprev = 0 if prev is None else 4
    extra = [] if dep is None else [dep]

    def body(s_ref, w_ref, m_ref, v_ref, *rest):
        g_refs = rest[:nl]
        go_ref, d_ref, mo_ref, vo_ref = rest[nl + nprev + len(extra):]
        layer = pl.program_id(0)
        gv = g_refs[0][...]
        for j in range(1, nl):
            gv = jnp.where(layer == j, g_refs[j][...], gv)
        go_ref[...] = gv
        mn = ADAM_B1 * m_ref[...] + (1.0 - ADAM_B1) * gv
        vn = ADAM_B2 * v_ref[...] + (1.0 - ADAM_B2) * (gv * gv)
        m_hat = mn / (1.0 - ADAM_B1 ** ADAM_STEP)
        v_hat = vn / (1.0 - ADAM_B2 ** ADAM_STEP)
        d_ref[...] = -ADAM_LR * (m_hat / (jnp.sqrt(v_hat) + ADAM_EPS) + ADAM_WD * w_ref[...])
        mo_ref[...] = mn
        vo_ref[...] = vn

    half = pl.BlockSpec((None, None, tr, cdim), lambda l, i, s: (l, s[0], i, 0))

    def of_layer(j):
        return pl.BlockSpec((tr, cdim), lambda l, i, s: (jnp.where(l == j, i, 0), 0))

    shp = jax.ShapeDtypeStruct(w.shape, F32)
    return pl.pallas_call(
        body,
        grid_spec=pltpu.PrefetchScalarGridSpec(
            num_scalar_prefetch=1, grid=(nl, h // tr),
            in_specs=[half] * 3 + [of_layer(j) for j in range(nl)] + [ANY_SPEC] * (nprev + len(extra)),
            out_specs=[half] * 4),
        out_shape=[shp] * 4, input_output_aliases={4 + nl + k: k for k in range(nprev)},
        compiler_params=_cparams("arbitrary", "arbitrary"), name=name,
    )(half_idx, w, m, v, *gs, *([] if prev is None else prev), *extra)


def _adamw_many(ws, gs, ms, vs):
    n = len(ws)

    def body(*refs):
        for i in range(n):
            w_ref, g_ref, m_ref, v_ref = refs[i], refs[n + i], refs[2 * n + i], refs[3 * n + i]
            d_ref, mo_ref, vo_ref = refs[4 * n + i], refs[5 * n + i], refs[6 * n + i]
            gv = g_ref[...]
            mn = ADAM_B1 * m_ref[...] + (1.0 - ADAM_B1) * gv
            vn = ADAM_B2 * v_ref[...] + (1.0 - ADAM_B2) * (gv * gv)
            m_hat = mn / (1.0 - ADAM_B1 ** ADAM_STEP)
            v_hat = vn / (1.0 - ADAM_B2 ** ADAM_STEP)
            d_ref[...] = -ADAM_LR * (m_hat / (jnp.sqrt(v_hat) + ADAM_EPS) + ADAM_WD * w_ref[...])
            mo_ref[...] = mn
            vo_ref[...] = vn

    shapes = [jax.ShapeDtypeStruct(w.shape, F32) for w in ws]
    outs = pl.pallas_call(body, out_shape=shapes * 3, name="adamw_small")(*ws, *gs, *ms, *vs)
    return outs[:n], outs[n:2 * n], outs[2 * n:]


def _split_rows(a):
    return a.reshape(a.shape[0], 2, a.shape[1] // 2, a.shape[2])


def _place():
    x, y, c = lax.axis_index("x"), lax.axis_index("y"), lax.axis_index("c")
    chips = [(1 - x, y), (x, 1 - y), (1 - x, 1 - y)]
    return x, y, c, chips


def _remote(src, dst, send_sem, recv_sem, to):
    return pltpu.make_async_remote_copy(src_ref=src, dst_ref=dst, send_sem=send_sem, recv_sem=recv_sem,
                                        device_id=to, device_id_type=MESH)


def _plan_gather(n_halved):
    def plan(src_refs, land_refs):
        x, y, c, chips = _place()
        me = 2 * x + y
        copies = []
        for k, (src, land) in enumerate(zip(src_refs, land_refs)):
            for (px, py) in chips:
                frm = 2 * px + py
                if k < n_halved:
                    copies.append((src.at[c], land.at[me, c], (px, py, c), land.at[frm, c]))
                else:
                    copies.append((src, land.at[me], (px, py, c), land.at[frm]))
        return copies
    return plan


def _plan_share(src_refs, land_refs):
    x, y, c, chips = _place()
    me = 2 * x + y
    sib = (x, y, 1 - c)
    copies = []
    for src, land in zip(src_refs, land_refs):
        copies.append((src, land.at[me], sib, land.at[me]))
        for (px, py) in chips:
            frm = 2 * px + py
            copies.append((land.at[frm, c], land.at[frm, c], sib, land.at[frm, 1 - c]))
    return copies


def _plan_scatter(n_parts):
    def plan(src_refs, land_refs):
        x, y, c, chips = _place()
        me = 2 * x + y
        copies = []
        for k, (src, land) in enumerate(zip(src_refs, land_refs)):
            for (px, py) in chips:
                to = 2 * px + py
                copies.append((src.at[to] if k < n_parts else src, land.at[me], (px, py, c), land.at[to]))
        return copies
    return plan


N_DEVICES = 8
OTHER_DEVICES = [(dx, dy, dc) for dx in (0, 1) for dy in (0, 1) for dc in (0, 1) if dx or dy or dc]


def _plan_scatter_all(src_refs, land_refs):
    x, y, c, _ = _place()
    me = 4 * x + 2 * y + c
    copies = []
    for src, land in zip(src_refs, land_refs):
        for dx, dy, dc in OTHER_DEVICES:
            px, py, pc = (1 - x if dx else x), (1 - y if dy else y), (1 - c if dc else c)
            copies.append((src.at[2 * px + py, pc], land.at[me], (px, py, pc), land.at[4 * px + 2 * py + pc]))
    return copies


def _plan_exchange(n_split):
    def plan(src_refs, land_refs):
        x, y, c, _ = _place()
        sib = (x, y, 1 - c)
        return [(src.at[:, 1 - c] if k < n_split else src, land, sib, land)
                for k, (src, land) in enumerate(zip(src_refs, land_refs))]
    return plan


def _hbm(a):
    return pltpu.HBM(a.shape, a.dtype)


def _start_copies(name, srcs, lands, plan, ncopy, dep=None):
    ns, nl = len(srcs), len(lands)
    nin = ns + nl + (0 if dep is None else 1)

    def body(*refs):
        send_sems, recv_sems, token = refs[nin], refs[nin + 1], refs[-1]
        for k, (src, dst, dev, _) in enumerate(plan(refs[:ns], refs[ns:ns + nl])):
            _remote(src, dst, send_sems.at[k], recv_sems.at[k], dev).start()
        token[...] = jnp.zeros_like(token)

    args = [pltpu.with_memory_space_constraint(a, pltpu.HBM) for a in list(srcs) + list(lands)]
    outs = pl.pallas_call(
        body, name=name,
        out_shape=(pltpu.SemaphoreType.DMA((ncopy,)), pltpu.SemaphoreType.DMA((ncopy,)),
                   *[_hbm(a) for a in list(srcs) + list(lands)], jax.ShapeDtypeStruct((8, 128), F32)),
        in_specs=[HBM_SPEC] * (ns + nl) + ([] if dep is None else [ANY_SPEC]),
        out_specs=(SEM_SPEC, SEM_SPEC, *([HBM_SPEC] * (ns + nl)), pl.BlockSpec(memory_space=pltpu.VMEM)),
        input_output_aliases={i: 2 + i for i in range(ns + nl)},
        compiler_params=pltpu.CompilerParams(has_side_effects=SIDE_EFFECT),
    )(*args, *([] if dep is None else [dep]))
    return outs[0], outs[1], list(outs[2:2 + ns]), list(outs[2 + ns:2 + ns + nl]), outs[-1]


def _wait_copies(name, started, plan, after, sem_offset=0):
    send_sems, recv_sems, srcs, lands, _ = started
    ns, nl = len(srcs), len(lands)
    after = list(after) if isinstance(after, (list, tuple)) else [after]

    def body(*refs):
        send_ref, recv_ref = refs[ns + nl], refs[ns + nl + 1]
        for k, (src, _, dev, mine) in enumerate(plan(refs[:ns], refs[ns:ns + nl])):
            copy = _remote(src, mine, send_ref.at[sem_offset + k], recv_ref.at[sem_offset + k], dev)
            copy.wait_send()
            copy.wait_recv()

    outs = pl.pallas_call(
        body, name=name, out_shape=tuple(_hbm(a) for a in srcs + lands),
        in_specs=[HBM_SPEC] * (ns + nl) + [SEM_SPEC, SEM_SPEC] + [ANY_SPEC] * len(after),
        out_specs=tuple([HBM_SPEC] * (ns + nl)),
        input_output_aliases={i: i for i in range(ns + nl)},
        compiler_params=pltpu.CompilerParams(has_side_effects=SIDE_EFFECT),
    )(*srcs, *lands, send_sems, recv_sems, *after)
    return list(outs[:ns]), list(outs[ns:])


def _share_with_sibling(name, srcs, lands):
    n = len(srcs)

    def body(*refs):
        src_refs, land_refs, out_refs = refs[:n], refs[n:2 * n], refs[2 * n:3 * n]
        send_sem, recv_sem = refs[3 * n:]
        x, y, c, chips = _place()
        me = 2 * x + y
        sib = (x, y, 1 - c)
        sends, recvs = [], []
        for k in range(n):
            sems = (send_sem.at[4 * k], recv_sem.at[4 * k])
            sends.append(_remote(src_refs[k], out_refs[k].at[me], *sems, sib))
            recvs.append(_remote(src_refs[k], out_refs[k].at[me], *sems, sib))
            for j, (px, py) in enumerate(chips):
                frm = 2 * px + py
                sems = (send_sem.at[4 * k + 1 + j], recv_sem.at[4 * k + 1 + j])
                sends.append(_remote(land_refs[k].at[frm, c], out_refs[k].at[frm, c], *sems, sib))
                recvs.append(_remote(land_refs[k].at[frm, c], out_refs[k].at[frm, 1 - c], *sems, sib))
        for cp in sends:
            cp.start()
        for cp in recvs:
            cp.wait_recv()
        for cp in sends:
            cp.wait_send()

    return pl.pallas_call(
        body, name=name, in_specs=[HBM_SPEC] * (2 * n), out_specs=[HBM_SPEC] * n,
        out_shape=[jax.ShapeDtypeStruct(a.shape, a.dtype) for a in lands],
        input_output_aliases={n + k: k for k in range(n)},
        scratch_shapes=[pltpu.SemaphoreType.DMA((4 * n,)), pltpu.SemaphoreType.DMA((4 * n,))],
    )(*srcs, *lands)


def _pack(arrs):
    flat = jnp.concatenate([a.reshape(-1).astype(F32) for a in arrs])
    n = flat.shape[0]
    rows = -(-n // PACK_WIDTH)
    rows = -(-rows // 8) * 8
    return jnp.pad(flat, (0, rows * PACK_WIDTH - n)).reshape(rows, PACK_WIDTH)


def _unpack(buf, shapes):
    flat = buf.reshape(-1)
    out, off = [], 0
    for shp in shapes:
        n = 1
        for s in shp:
            n *= s
        out.append(flat[off:off + n].reshape(shp))
        off += n
    return out


def _unshard_cols(stacked):
    moved = jnp.moveaxis(stacked, 0, -2)
    return moved.reshape(moved.shape[:-2] + (moved.shape[-2] * moved.shape[-1],))


def _take_cols(blocks, start, width):
    bw = blocks.shape[2]
    pieces, lo = [], start
    while lo < start + width:
        b = lo // bw
        hi = min(start + width, (b + 1) * bw)
        pieces.append(blocks[b][:, lo - b * bw:hi - b * bw])
        lo = hi
    return jnp.concatenate(pieces, axis=1)


def _col_shard(full, s, width):
    return lax.dynamic_slice_in_dim(full, s * width, width, axis=full.ndim - 1)


def kernel(x, meta_tokens, mix_norm_g, ffn_norm_g, ffn_w1, ffn_w2, cp_w_in, cp_conv_w, cp_conv_b, cp_ln_g, cp_ln_b, cp_pool_w, cp_pool_scale, cp_w_out, gla_w_in, gla_gate_w2, gla_gate_b, gla_head_g, gla_w_out, final_norm_g, loss_target, m_meta_tokens, m_mix_norm_g, m_ffn_norm_g, m_ffn_w1, m_ffn_w2, m_cp_w_in, m_cp_conv_w, m_cp_conv_b, m_cp_ln_g, m_cp_ln_b, m_cp_pool_w, m_cp_pool_scale, m_cp_w_out, m_gla_w_in, m_gla_gate_w2, m_gla_gate_b, m_gla_head_g, m_gla_w_out, m_final_norm_g, v_meta_tokens, v_mix_norm_g, v_ffn_norm_g, v_ffn_w1, v_ffn_w2, v_cp_w_in, v_cp_conv_w, v_cp_conv_b, v_cp_ln_g, v_cp_ln_b, v_cp_pool_w, v_cp_pool_scale, v_cp_w_out, v_gla_w_in, v_gla_gate_w2, v_gla_gate_b, v_gla_head_g, v_gla_w_out, v_final_norm_g):
    d = D_MODEL
    chip = 2 * lax.axis_index("x") + lax.axis_index("y")
    core = lax.axis_index("c")
    seq = x.shape[1]
    t = seq + CHUNK

    sharded_small = [meta_tokens, cp_conv_w, gla_gate_w2, gla_gate_b, gla_head_g]

    def halves(w):
        return w.astype(BF16).reshape(2, w.shape[0] // 2, w.shape[1])

    def unhalve(g):
        return g.reshape(N_CHIPS, 2 * g.shape[2], g.shape[3])

    def gather_group(srcs, whole=()):
        lands = [lax.empty((N_CHIPS,) + s.shape, s.dtype) for s in srcs]
        for a in whole:
            lands.append(lax.dynamic_update_slice(jnp.zeros((N_CHIPS,) + a.shape, a.dtype), a[None], (chip,) + (0,) * a.ndim))
        return list(srcs) + list(whole), lands, _plan_gather(len(srcs)), len(srcs)

    groups = [gather_group([halves(cp_w_in[0]), halves(cp_w_out[0])], [_pack(sharded_small)]),
              gather_group([halves(ffn_w1[0]), halves(ffn_w2[0])]), gather_group([halves(gla_w_in[0]), halves(gla_w_out[0])]),
              gather_group([halves(ffn_w1[1]), halves(ffn_w2[1])])]
    bounds, all_srcs, all_lands = [], [], []
    for srcs, lands, _, _ in groups:
        bounds.append((len(all_srcs), len(all_srcs) + len(srcs)))
        all_srcs += srcs
        all_lands += lands

    def plan_all(src_refs, land_refs):
        return [cp for (lo, hi), group in zip(bounds, groups) for cp in group[2](src_refs[lo:hi], land_refs[lo:hi])]

    gathers = _start_copies("gather_start", all_srcs, all_lands, plan_all, 3 * len(all_srcs))

    def arrived(name, gi, after):
        (lo, hi), plan, n = bounds[gi], groups[gi][2], groups[gi][3]
        mine = (gathers[0], gathers[1], gathers[2][lo:hi], gathers[3][lo:hi], gathers[4])
        srcs, lands = _wait_copies(name + "_wait", mine, plan, after, sem_offset=3 * lo)
        return srcs[:n], lands[:n], lands[n:]

    cp_gather, ffn0_gather, gla_gather, ffn1_gather = 0, 1, 2, 3
    h0_rows = jnp.concatenate([jnp.zeros((CHUNK, d), F32) + gathers[4][0, 0], x[0]], axis=0)
    cp_srcs, cp_lands, (small_g,) = arrived("gather_cp", cp_gather, h0_rows)
    cpin_g, cpout_g = [unhalve(g) for g in _share_with_sibling("gather_cp_share", cp_srcs, cp_lands)]
    per_chip = [_unpack(small_g[j], [a.shape for a in sharded_small]) for j in range(N_CHIPS)]
    meta_f, conv_w_f, gate_w_f, gate_b_f, head_g_f = [
        jnp.concatenate([per_chip[j][i] for j in range(N_CHIPS)], axis=-1) for i in range(len(sharded_small))]
    conv_w_f, gate_w_f = conv_w_f[0], gate_w_f[0]
    w_cp_in = _unshard_cols(cpin_g)
    w_cp_out = cpout_g.reshape(CONV_DIM + POOL_DIM, d)
    gate_w_pad = jnp.pad(gate_w_f, ((0, GATE_PAD - GATE_RANK), (0, 0))).astype(BF16)
    row = lambda a: a.reshape(1, -1)
    c_idx = core.reshape(1).astype(jnp.int32)

    h0 = lax.dynamic_update_slice(h0_rows, meta_f, (PAD_ROWS, 0))
    z0, u0 = _norm_matmul(h0, row(mix_norm_g[0]), w_cp_in, 512, "cp_in_proj")
    c0, pm0, mix0 = _cp_seq_fwd(z0, conv_w_f, cp_conv_b, cp_ln_g, cp_ln_b, cp_pool_w[0], cp_pool_scale)
    ffn0_srcs, ffn0_lands, _ = arrived("gather_ffn0", ffn0_gather, mix0)
    ffn0_share = _start_copies("gather_ffn0_share_start", ffn0_srcs, ffn0_lands, _plan_share, 4 * len(ffn0_srcs))
    h1 = _matmul_residual(mix0, w_cp_out, h0, "cp_out_proj", dep=ffn0_share[-1])
    w1g0, w2g0 = [unhalve(g) for g in _wait_copies("gather_ffn0_share_wait", ffn0_share, _plan_share, h1)[1]]
    h2, hp0, uf0 = _ffn_fwd(h1, row(ffn_norm_g[0]), w1g0, w2g0, "ffn0_fwd")
    gla_srcs, gla_lands, _ = arrived("gather_gla", gla_gather, h2)
    glain_g, glaout_g = [unhalve(g) for g in _share_with_sibling("gather_gla_share", gla_srcs, gla_lands)]
    w_gla_in = jnp.concatenate([glain_g[j] for j in range(N_CHIPS)] + [jnp.zeros((d, GLA_IN_PAD - GLA_IN), BF16)], axis=1)
    w_gla_out = glaout_g.reshape(GLA_DV, d)
    z1, u2 = _norm_matmul(h2, row(mix_norm_g[1]), w_gla_in, GLA_COLS, "gla_in_proj")
    ffn1_srcs, ffn1_lands, _ = arrived("gather_ffn1", ffn1_gather, z1)
    ffn1_share = _start_copies("gather_ffn1_share_start", ffn1_srcs, ffn1_lands, _plan_share, 4 * len(ffn1_srcs))
    o1, mix1, states = _gla_seq_fwd(z1, gate_w_pad, gate_b_f, head_g_f, dep=ffn1_share[-1])
    h3 = _matmul_residual(mix1, w_gla_out, h2, "gla_out_proj")
    w1g1, w2g1 = [unhalve(g) for g in _wait_copies("gather_ffn1_share_wait", ffn1_share, _plan_share, h3)[1]]
    h4, hp1, uf1 = _ffn_fwd(h3, row(ffn_norm_g[1]), w1g1, w2g1, "ffn1_fwd")

    dev_idx = (2 * chip + core).reshape(1).astype(jnp.int32)

    def start_reduce(name, grads):
        srcs = [_split_rows(g) for g in grads]
        lands = [lax.empty((N_DEVICES,) + s.shape[2:], s.dtype) for s in srcs]
        return _start_copies(name + "_scatter_start", srcs, lands, _plan_scatter_all, len(OTHER_DEVICES) * len(srcs))

    def finish_reduce(name, started, after):
        srcs, lands = _wait_copies(name + "_scatter_wait", started, _plan_scatter_all, after)
        return [_sum_own_and_slots(s, l, dev_idx, "%s_slot_sum_%d" % (name, k)) for k, (s, l) in enumerate(zip(srcs, lands))]

    dh4, d_final_g, loss_part = _loss_bwd(h4, row(final_norm_g), loss_target[0])

    dh3, dhp1, d_ffn_g1 = _ffn_bwd_data(dh4, h3, row(ffn_norm_g[1]), hp1, w1g1, w2g1, "ffn1_bwd")
    dw1_1 = _wgrad(uf1, dhp1, N_CHIPS, d, d, False, True, False, "ffn1_dw1", rows=WGRAD_ROWS_BF16)
    dw2_1 = _wgrad(hp1, dh4, N_CHIPS, d, d, True, False, True, "ffn1_dw2")
    ffn1_reduce = start_reduce("ffn1", [dw1_1, dw2_1])

    dmix1 = _dgrad(dh3, w_gla_out, "gla_out_dgrad", dep=ffn1_reduce[-1])
    dw_gla_out = _wgrad(mix1, dh3, 1, GLA_DV, d, False, False, False, "gla_out_dw")
    dz1, d_gate_w, d_gate_b, d_head_g = _gla_seq_bwd(dmix1, o1, z1, states, gate_w_pad, gate_b_f, head_g_f)
    dh2, d_mix_g1 = _dgrad_norm_bwd(dz1, w_gla_in, h2, row(mix_norm_g[1]), dh3, GLA_COLS, "gla_in_dgrad")
    dw_gla_in = _wgrad(u2, dz1, GLA_IN_PAD // 640, d, 640, False, True, False, "gla_in_dw", rows=WGRAD_ROWS_BF16)
    gla_in_shards = jnp.stack([_take_cols(dw_gla_in, j * (GLA_IN // N_CHIPS), GLA_IN // N_CHIPS) for j in range(N_CHIPS)])
    gla_reduce = start_reduce("gla", [gla_in_shards, dw_gla_out.reshape(N_CHIPS, -1, d)])

    dh1, dhp0, d_ffn_g0 = _ffn_bwd_data(dh2, h1, row(ffn_norm_g[0]), hp0, w1g0, w2g0, "ffn0_bwd", dep=gla_reduce[-1])
    dw1_0 = _wgrad(uf0, dhp0, N_CHIPS, d, d, False, True, False, "ffn0_dw1", rows=WGRAD_ROWS_BF16)
    dw2_0 = _wgrad(hp0, dh2, N_CHIPS, d, d, True, False, True, "ffn0_dw2")
    ffn0_reduce = start_reduce("ffn0", [dw1_0, dw2_0])

    dmix0 = _dgrad(dh1, w_cp_out, "cp_out_dgrad", dep=ffn0_reduce[-1])
    dw_cp_out = _wgrad(mix0, dh1, 1, CONV_DIM + POOL_DIM, d, False, False, False, "cp_out_dw")
    dz0, d_conv_w, d_cp_vec, d_pool_w = _cp_seq_bwd(dmix0, z0, c0, pm0, conv_w_f, cp_ln_g, cp_ln_b, cp_pool_w[0],
                                                    cp_pool_scale)
    grad_x, dh0_head, d_mix_g0 = _dgrad_norm_bwd_input(dz0, w_cp_in, h0, row(mix_norm_g[0]), dh1, 512, "cp_in_dgrad")
    grad_x = grad_x[None]
    dw_cp_in = _wgrad(u0, dz0, 1, d, CP_IN, False, False, False, "cp_in_dw")
    dw_cp_in = jnp.stack([_take_cols(dw_cp_in, j * (CP_IN // N_CHIPS), CP_IN // N_CHIPS) for j in range(N_CHIPS)])

    cp_reduce = start_reduce("cp", [dw_cp_in, dw_cp_out.reshape(N_CHIPS, -1, d)])
    small_full = [dh0_head[PAD_ROWS:CHUNK],jnp.concatenate([d_mix_g0, d_mix_g1], axis=0),
                  jnp.concatenate([d_ffn_g0, d_ffn_g1], axis=0), d_conv_w[:CONV_WIDTH][None],
                  d_cp_vec[0:1], d_cp_vec[1:2], d_cp_vec[2:3], d_pool_w[None], d_cp_vec[3:4],
                  d_gate_w[:GATE_RANK][None], d_gate_b, d_head_g, d_final_g[0], loss_part[0, 0:1]]
    small_mine = _pack(small_full)
    whole = _plan_exchange(0)
    small_exchange = _start_copies("small_exchange_start", [small_mine], [lax.empty(small_mine.shape, F32)], whole, 1,
                                   dep=cp_reduce[-1])
    red_ffn1 = finish_reduce("ffn1", ffn1_reduce, small_exchange[-1])
    red_gla = finish_reduce("gla", gla_reduce, small_exchange[-1])
    (small_sent,), (small_recv,) = _wait_copies("small_exchange_wait", small_exchange, whole, [red_ffn1[1], red_gla[1]])
    small_chip = _add2(small_sent, small_recv, "chip_sum_small")
    small_slots = lax.dynamic_update_slice(jnp.zeros((N_CHIPS,) + small_chip.shape, F32), small_chip[None], (chip, 0, 0))
    small_reduce = _start_copies("small_scatter_start", [small_chip], [small_slots], _plan_scatter(0), 3)

    big = {"w1": (ffn_w1, m_ffn_w1, v_ffn_w1), "w2": (ffn_w2, m_ffn_w2, v_ffn_w2),
           "cp_in": (cp_w_in, m_cp_w_in, v_cp_w_in), "cp_out": (cp_w_out, m_cp_w_out, v_cp_w_out),
           "gla_in": (gla_w_in, m_gla_w_in, v_gla_w_in), "gla_out": (gla_w_out, m_gla_w_out, v_gla_w_out)}
    other_idx = (1 - core).reshape(1).astype(jnp.int32)

    def adamw_by_halves(tag, reduced, dep=None):
        flat = [r for n in reduced for r in reduced[n]]
        join_plan = _plan_exchange(0)
        join = _start_copies(tag + "_join_start", flat, [lax.empty(r.shape, F32) for r in flat], join_plan, len(flat),
                             dep=dep)
        views = {n: [_split_rows(a) for a in big[n]] for n in reduced}
        own, k = {}, 0
        for n in reduced:
            mine = join[2][k:k + len(reduced[n])]
            k += len(reduced[n])
            own[n] = _adamw_half(views[n][0], mine, views[n][1], views[n][2], c_idx, None, "adamw_%s_own" % n)
        _, arrived_halves = _wait_copies(tag + "_join_wait", join, join_plan, [own[n][1] for n in reduced])
        outs, k = {}, 0
        for n in reduced:
            theirs = arrived_halves[k:k + len(reduced[n])]
            k += len(reduced[n])
            res = _adamw_half(views[n][0], theirs, views[n][1], views[n][2], other_idx, own[n], "adamw_%s_sibling" % n)
            outs[n] = [o.reshape(big[n][0].shape) for o in res]
        return outs

    big_out = adamw_by_halves("gla", {"gla_in": [red_gla[0]], "gla_out": [red_gla[1]]}, dep=small_reduce[-1])
    red_ffn0 = finish_reduce("ffn0", ffn0_reduce, big_out["gla_out"][1])
    big_out.update(adamw_by_halves("ffn", {"w1": [red_ffn0[0], red_ffn1[0]], "w2": [red_ffn0[1], red_ffn1[1]]}))
    red_cp = finish_reduce("cp", cp_reduce, big_out["w2"][1])
    _, (small_landed,) = _wait_copies("small_scatter_wait", small_reduce, _plan_scatter(0), big_out["w2"][1])
    small_red = _sum_slots(small_landed, "slot_sum_small")
    big_out.update(adamw_by_halves("cp", {"cp_in": [red_cp[0]], "cp_out": [red_cp[1]]}))

    (g_meta, g_mix, g_ffn, g_conv_w, g_conv_b, g_ln_g, g_ln_b, g_pool_w, g_pool_scale, g_gate_w, g_gate_b, g_head,
     g_final, loss_sum) = _unpack(small_red, [a.shape for a in small_full])
    g_meta = _col_shard(g_meta, chip, meta_tokens.shape[-1])
    g_conv_w = _col_shard(g_conv_w, chip, cp_conv_w.shape[-1])
    g_gate_w = _col_shard(g_gate_w, chip, gla_gate_w2.shape[-1])
    g_gate_b = _col_shard(g_gate_b, chip, gla_gate_b.shape[-1])
    g_head = _col_shard(g_head, chip, gla_head_g.shape[-1])
    small_w = [meta_tokens, mix_norm_g, ffn_norm_g, cp_conv_w, cp_conv_b, cp_ln_g, cp_ln_b, cp_pool_w, cp_pool_scale,
               gla_gate_w2, gla_gate_b, gla_head_g, final_norm_g]
    small_m = [m_meta_tokens, m_mix_norm_g, m_ffn_norm_g, m_cp_conv_w, m_cp_conv_b, m_cp_ln_g, m_cp_ln_b, m_cp_pool_w,
               m_cp_pool_scale, m_gla_gate_w2, m_gla_gate_b, m_gla_head_g, m_final_norm_g]
    small_v = [v_meta_tokens, v_mix_norm_g, v_ffn_norm_g, v_cp_conv_w, v_cp_conv_b, v_cp_ln_g, v_cp_ln_b, v_cp_pool_w,
               v_cp_pool_scale, v_gla_gate_w2, v_gla_gate_b, v_gla_head_g, v_final_norm_g]
    small_g = [g_meta, g_mix, g_ffn, g_conv_w, g_conv_b, g_ln_g, g_ln_b, g_pool_w, g_pool_scale, g_gate_w, g_gate_b,
               g_head, g_final]
    shapes = [w.shape for w in small_w]
    small_g = [g.reshape(s) for g, s in zip(small_g, shapes)]
    at_least_2d = lambda arrs: [a.reshape(1, -1) if a.ndim == 1 else a for a in arrs]
    s_delta, s_m, s_v = _adamw_many(at_least_2d(small_w), at_least_2d(small_g), at_least_2d(small_m), at_least_2d(small_v))
    s_delta, s_m, s_v = [[a.reshape(s) for a, s in zip(group, shapes)] for group in (s_delta, s_m, s_v)]

    order = ["meta", "mix", "ffn", "w1", "w2", "cp_in", "conv_w", "conv_b", "ln_g", "ln_b", "pool_w", "pool_scale",
             "cp_out", "gla_in", "gate_w", "gate_b", "head", "gla_out", "final"]
    small_names = ["meta", "mix", "ffn", "conv_w", "conv_b", "ln_g", "ln_b", "pool_w", "pool_scale", "gate_w", "gate_b",
                   "head", "final"]
    big_names = ["w1", "w2", "cp_in", "cp_out", "gla_in", "gla_out"]
    table = {n: (small_g[i], s_delta[i], s_m[i], s_v[i]) for i, n in enumerate(small_names)}
    table.update({n: tuple(big_out[n]) for n in big_names})
    loss = loss_sum.reshape(())
    return (loss, grad_x, *[table[n][0] for n in order], *[table[n][1] for n in order],
            *[table[n][2] for n in order], *[table[n][3] for n in order])
```

```python
import functools

import jax
import jax.numpy as jnp
from jax import lax
from jax.experimental import pallas as pl
from jax.experimental.pallas import tpu as pltpu

F32 = jnp.float32
BF16 = jnp.bfloat16

D_MODEL = 1024
N_META = 16
CHUNK = 64
PAD_ROWS = CHUNK - N_META
EPS = 1e-5
CONV_DIM = 512
CONV_WIDTH = 31
CONV_HALO = 32
POOL_DIM = 512
POOL_WINDOWS = (2, 4, 8, 16)
POOL_GROUP = 128
POOL_HALO = 16
CP_IN = 2 * CONV_DIM + POOL_DIM
GLA_HEADS = 4
GLA_DK = 512
GLA_DV = 1024
GLA_HK = GLA_DK // GLA_HEADS
GLA_HV = GLA_DV // GLA_HEADS
GATE_RANK = 16
GATE_PAD = 128
GATE_NORM = 16.0
GLA_IN = 2 * GLA_DK + 2 * GLA_DV + GATE_RANK
GLA_IN_PAD = 2 * GLA_DK + 2 * GLA_DV + GATE_PAD
GLA_COLS = 1280
N_CHIPS = 4
ADAM_LR = 0.001
ADAM_B1 = 0.9
ADAM_B2 = 0.999
ADAM_EPS = 1e-08
ADAM_WD = 0.01
ADAM_STEP = 10

VMEM_LIMIT_BYTES = 56 * 1024 * 1024
ROW_TILE_TARGET = 832
TOKEN_TILE_TARGET = 1040
PACK_WIDTH = 1024
MESH = pl.DeviceIdType.MESH
HBM_SPEC = pl.BlockSpec(memory_space=pltpu.HBM)
ANY_SPEC = pl.BlockSpec(memory_space=pl.ANY)
SEM_SPEC = pl.BlockSpec(memory_space=pltpu.SEMAPHORE)
SIDE_EFFECT = pltpu.SideEffectType.DATAFLOW_SIDE_EFFECTING


def _cparams(*sem):
    return pltpu.CompilerParams(dimension_semantics=sem, vmem_limit_bytes=VMEM_LIMIT_BYTES)


def _row_tile(t, target, mult):
    best = mult
    for cand in range(mult, min(t, target) + 1, mult):
        if t % cand == 0:
            best = cand
    assert t % best == 0, (t, best)
    return best


def _rms(h, g):
    return h * lax.rsqrt(jnp.mean(h * h, axis=-1, keepdims=True) + EPS) * g


def _rms_bwd(h, g, du):
    r = lax.rsqrt(jnp.mean(h * h, axis=-1, keepdims=True) + EPS)
    xhat = h * r
    dxh = du * g
    dh = r * (dxh - xhat * jnp.mean(dxh * xhat, axis=-1, keepdims=True))
    return dh, du * xhat


def _valid_rows(i, tm):
    row = i * tm + lax.broadcasted_iota(jnp.int32, (tm, 1), 0)
    return row >= PAD_ROWS


def _dot(a, b):
    return jnp.dot(a, b, preferred_element_type=F32)


def _dot_nt(a, b):
    return lax.dot_general(a, b, (((1,), (1,)), ((), ())), preferred_element_type=F32)


def _dot_tn(a, b):
    return lax.dot_general(a, b, (((0,), (0,)), ((), ())), preferred_element_type=F32)


def _accumulate(ref, val, first):
    @pl.when(first)
    def _():
        ref[...] = val

    @pl.when(jnp.logical_not(first))
    def _():
        ref[...] += val


def _call_after(dep, body, n_in, in_specs, args, **kw):
    if dep is None:
        return pl.pallas_call(body, in_specs=in_specs, **kw)(*args)

    def with_dep(*refs):
        body(*refs[:n_in], *refs[n_in + 1:])

    return pl.pallas_call(with_dep, in_specs=list(in_specs) + [ANY_SPEC], **kw)(*args, dep)


def _norm_matmul(h, g, w, nc, name, dep=None):
    t, d = h.shape
    n = w.shape[1]
    tm = _row_tile(t, TOKEN_TILE_TARGET, 16)

    def body(h_ref, g_ref, w_ref, z_ref, u_ref):
        u = _rms(h_ref[...], g_ref[...]).astype(BF16)
        u_ref[...] = u
        for n0 in range(0, n, nc):
            n1 = min(n0 + nc, n)
            z_ref[:, n0:n1] = _dot(u, w_ref[:, n0:n1]).astype(BF16)

    return _call_after(
        dep, body, 3,
        [pl.BlockSpec((tm, d), lambda i: (i, 0)), pl.BlockSpec((1, d), lambda i: (0, 0)),
         pl.BlockSpec((d, n), lambda i: (0, 0))], (h, g, w), grid=(t // tm,),
        out_specs=[pl.BlockSpec((tm, n), lambda i: (i, 0)), pl.BlockSpec((tm, d), lambda i: (i, 0))],
        out_shape=[jax.ShapeDtypeStruct((t, n), BF16), jax.ShapeDtypeStruct((t, d), BF16)],
        compiler_params=_cparams("parallel"), name=name)


def _matmul_residual(a, w, h, name, dep=None):
    t, k = a.shape
    d = w.shape[1]
    tm = _row_tile(t, TOKEN_TILE_TARGET, 16)

    def body(a_ref, w_ref, h_ref, o_ref):
        o_ref[...] = h_ref[...] + _dot(a_ref[...], w_ref[...])

    return _call_after(
        dep, body, 3,
        [pl.BlockSpec((tm, k), lambda i: (i, 0)), pl.BlockSpec((k, d), lambda i: (0, 0)),
         pl.BlockSpec((tm, d), lambda i: (i, 0))], (a, w, h), grid=(t // tm,),
        out_specs=pl.BlockSpec((tm, d), lambda i: (i, 0)),
        out_shape=jax.ShapeDtypeStruct((t, d), F32),
        compiler_params=_cparams("parallel"), name=name)


def _ffn_fwd(h, g, w1g, w2g, name):
    t, d = h.shape
    ns, ffs = w1g.shape[0], w1g.shape[2]
    tm = _row_tile(t, TOKEN_TILE_TARGET, 16)

    def body(h_ref, g_ref, w1_ref, w2_ref, ho_ref, hp_ref, u_ref, acc_ref):
        s = pl.program_id(1)

        @pl.when(s == 0)
        def _():
            u_ref[...] = _rms(h_ref[...], g_ref[...]).astype(BF16)

        hp = _dot(u_ref[...], w1_ref[...])
        hp_ref[...] = hp.astype(BF16)
        a = jnp.maximum(hp, 0.0)
        _accumulate(acc_ref, _dot((a * a).astype(BF16), w2_ref[...]), s == 0)

        @pl.when(s == ns - 1)
        def _():
            ho_ref[...] = h_ref[...] + acc_ref[...]

    return pl.pallas_call(
        body, grid=(t // tm, ns),
        in_specs=[pl.BlockSpec((tm, d), lambda i, s: (i, 0)), pl.BlockSpec((1, d), lambda i, s: (0, 0)),
                  pl.BlockSpec((None, d, ffs), lambda i, s: (s, 0, 0)),
                  pl.BlockSpec((None, ffs, d), lambda i, s: (s, 0, 0))],
        out_specs=[pl.BlockSpec((tm, d), lambda i, s: (i, 0)), pl.BlockSpec((tm, ffs), lambda i, s: (i, s)),
                   pl.BlockSpec((tm, d), lambda i, s: (i, 0))],
        out_shape=[jax.ShapeDtypeStruct((t, d), F32), jax.ShapeDtypeStruct((t, ns * ffs), BF16),
                   jax.ShapeDtypeStruct((t, d), BF16)],
        scratch_shapes=[pltpu.VMEM((tm, d), F32)],
        compiler_params=_cparams("parallel", "arbitrary"), name=name)(h, g, w1g, w2g)


def _ffn_bwd_data(dh, h, g, hp, w1g, w2g, name, dep=None):
    t, d = h.shape
    ns, ffs = w1g.shape[0], w1g.shape[2]
    tm = _row_tile(t, ROW_TILE_TARGET, CHUNK)

    def body(dh_ref, h_ref, g_ref, hp_ref, w1_ref, w2_ref, dhi_ref, dhp_ref, dg_ref, acc_ref):
        i, s = pl.program_id(0), pl.program_id(1)
        da = _dot_nt(dh_ref[...].astype(BF16), w2_ref[...])
        dhp = (da * (2.0 * jnp.maximum(hp_ref[...].astype(F32), 0.0))).astype(BF16)
        dhp_ref[...] = dhp
        _accumulate(acc_ref, _dot_nt(dhp, w1_ref[...]), s == 0)

        @pl.when(s == ns - 1)
        def _():
            dhn, dgr = _rms_bwd(h_ref[...], g_ref[...], acc_ref[...])
            dhi_ref[...] = jnp.where(_valid_rows(i, tm), dh_ref[...] + dhn, 0.0)
            _accumulate(dg_ref, jnp.sum(dgr, axis=0, keepdims=True), i == 0)

    return _call_after(
        dep, body, 6,
        [pl.BlockSpec((tm, d), lambda i, s: (i, 0)), pl.BlockSpec((tm, d), lambda i, s: (i, 0)),
         pl.BlockSpec((1, d), lambda i, s: (0, 0)), pl.BlockSpec((tm, ffs), lambda i, s: (i, s)),
         pl.BlockSpec((None, d, ffs), lambda i, s: (s, 0, 0)),
         pl.BlockSpec((None, ffs, d), lambda i, s: (s, 0, 0))], (dh, h, g, hp, w1g, w2g), grid=(t // tm, ns),
        out_specs=[pl.BlockSpec((tm, d), lambda i, s: (i, 0)), pl.BlockSpec((tm, ffs), lambda i, s: (i, s)),
                   pl.BlockSpec((1, d), lambda i, s: (0, 0))],
        out_shape=[jax.ShapeDtypeStruct((t, d), F32), jax.ShapeDtypeStruct((t, ns * ffs), BF16),
                   jax.ShapeDtypeStruct((1, d), F32)],
        scratch_shapes=[pltpu.VMEM((tm, d), F32)],
        compiler_params=_cparams("arbitrary", "arbitrary"), name=name)


WGRAD_ROWS = 2048
WGRAD_ROWS_BF16 = 4096


def _wgrad(x, dy, nb, xc, yc, x_by_block, dy_by_block, relu2, name, dep=None, rows=WGRAD_ROWS):
    t = x.shape[0]
    tk = _row_tile(t - CHUNK, rows, CHUNK)

    def prep(xv):
        if relu2:
            xv = jnp.maximum(xv.astype(F32), 0.0)
            xv = xv * xv
        return xv.astype(BF16)

    nk = (t - CHUNK) // tk

    def body(xh_ref, dyh_ref, x_ref, dy_ref, o_ref, acc_ref):
        k = pl.program_id(1)
        p = _dot_tn(prep(x_ref[...]), dy_ref[...].astype(BF16))

        @pl.when(k == 0)
        def _():
            acc_ref[...] = p + _dot_tn(prep(xh_ref[...]), dyh_ref[...].astype(BF16))

        @pl.when(k > 0)
        def _():
            acc_ref[...] += p

        @pl.when(k == nk - 1)
        def _():
            o_ref[...] = acc_ref[...].astype(BF16)

    def head(width, by_block):
        return pl.BlockSpec((CHUNK, width), (lambda b, k: (0, b)) if by_block else (lambda b, k: (0, 0)))

    def rest(width, by_block):
        def index(b, k):
            return pl.multiple_of(CHUNK + k * tk, CHUNK), (pl.multiple_of(b * width, 128) if by_block else 0)
        return pl.BlockSpec((pl.Element(tk), pl.Element(width)), index)

    return _call_after(
        dep, body, 4,
        [head(xc, x_by_block), head(yc, dy_by_block), rest(xc, x_by_block), rest(yc, dy_by_block)], (x, dy, x, dy),
        grid=(nb, nk),
        out_specs=pl.BlockSpec((None, xc, yc), lambda b, k: (b, 0, 0)),
        out_shape=jax.ShapeDtypeStruct((nb, xc, yc), BF16),
        scratch_shapes=[pltpu.VMEM((xc, yc), F32)],
        compiler_params=_cparams("parallel", "arbitrary"), name=name)


def _dgrad(dh, w, name, dep=None):
    t, d = dh.shape
    k = w.shape[0]
    tm = _row_tile(t, TOKEN_TILE_TARGET, 16)

    def body(dh_ref, w_ref, o_ref):
        o_ref[...] = _dot_nt(dh_ref[...].astype(BF16), w_ref[...]).astype(BF16)

    return _call_after(
        dep, body, 2,
        [pl.BlockSpec((tm, d), lambda i: (i, 0)), pl.BlockSpec((k, d), lambda i: (0, 0))], (dh, w), grid=(t // tm,),
        out_specs=pl.BlockSpec((tm, k), lambda i: (i, 0)),
        out_shape=jax.ShapeDtypeStruct((t, k), BF16),
        compiler_params=_cparams("parallel"), name=name)


def _dgrad_norm_bwd(dz, w, h, g, dh, nc, name):
    t, d = h.shape
    n = w.shape[1]
    tm = _row_tile(t, ROW_TILE_TARGET // 2, 16)

    def body(dz_ref, w_ref, h_ref, g_ref, dh_ref, dhi_ref, dg_ref):
        i = pl.program_id(0)
        du = jnp.zeros((tm, d), F32)
        for n0 in range(0, n, nc):
            n1 = min(n0 + nc, n)
            du = du + _dot_nt(dz_ref[:, n0:n1], w_ref[:, n0:n1])
        dhn, dgr = _rms_bwd(h_ref[...], g_ref[...], du)
        dhi_ref[...] = jnp.where(_valid_rows(i, tm), dh_ref[...] + dhn, 0.0)
        _accumulate(dg_ref, jnp.sum(dgr, axis=0, keepdims=True), i == 0)

    return pl.pallas_call(
        body, grid=(t // tm,),
        in_specs=[pl.BlockSpec((tm, n), lambda i: (i, 0)), pl.BlockSpec((d, n), lambda i: (0, 0)),
                  pl.BlockSpec((tm, d), lambda i: (i, 0)), pl.BlockSpec((1, d), lambda i: (0, 0)),
                  pl.BlockSpec((tm, d), lambda i: (i, 0))],
        out_specs=[pl.BlockSpec((tm, d), lambda i: (i, 0)), pl.BlockSpec((1, d), lambda i: (0, 0))],
        out_shape=[jax.ShapeDtypeStruct((t, d), F32), jax.ShapeDtypeStruct((1, d), F32)],
        compiler_params=_cparams("arbitrary"), name=name)(dz, w, h, g, dh)


def _dgrad_norm_bwd_input(dz, w, h, g, dh, nc, name):
    t, d = h.shape
    n = w.shape[1]
    tl = _row_tile(t - CHUNK, 512, CHUNK)

    def grads(dz_ref, w_ref, h_ref, g_ref, dh_ref, rows):
        du = jnp.zeros((rows, d), F32)
        for n0 in range(0, n, nc):
            n1 = min(n0 + nc, n)
            du = du + _dot_nt(dz_ref[:, n0:n1], w_ref[:, n0:n1])
        dhn, dgr = _rms_bwd(h_ref[...], g_ref[...], du)
        return dh_ref[...] + dhn, jnp.sum(dgr, axis=0, keepdims=True)

    def rest_body(dz_ref, w_ref, h_ref, g_ref, dh_ref, dg_head_ref, dx_ref, dg_ref):
        dx, dg = grads(dz_ref, w_ref, h_ref, g_ref, dh_ref, tl)
        dx_ref[...] = dx

        @pl.when(pl.program_id(0) == 0)
        def _():
            dg_ref[...] = dg_head_ref[...] + dg

        @pl.when(pl.program_id(0) > 0)
        def _():
            dg_ref[...] += dg

    def head_body(dz_ref, w_ref, h_ref, g_ref, dh_ref, dx_ref, dg_ref):
        dx, dg = grads(dz_ref, w_ref, h_ref, g_ref, dh_ref, CHUNK)
        dx_ref[...] = jnp.where(_valid_rows(0, CHUNK), dx, 0.0)
        dg_ref[...] = dg

    def shifted(width):
        return pl.BlockSpec((pl.Element(tl), pl.Element(width)), lambda i: (pl.multiple_of(CHUNK + i * tl, CHUNK), 0))

    whole = [pl.BlockSpec((d, n), lambda i: (0, 0)), pl.BlockSpec((1, d), lambda i: (0, 0))]
    head = lambda width: pl.BlockSpec((CHUNK, width), lambda i: (0, 0))
    dh_head, dg_head = pl.pallas_call(
        head_body, grid=(1,), in_specs=[head(n), whole[0], head(d), whole[1], head(d)],
        out_specs=[head(d), whole[1]],
        out_shape=[jax.ShapeDtypeStruct((CHUNK, d), F32), jax.ShapeDtypeStruct((1, d), F32)],
        compiler_params=_cparams("arbitrary"), name=name + "_head")(dz, w, h, g, dh)
    dx, dg = pl.pallas_call(
        rest_body, grid=((t - CHUNK) // tl,),
        in_specs=[shifted(n), whole[0], shifted(d), whole[1], shifted(d), whole[1]],
        out_specs=[pl.BlockSpec((tl, d), lambda i: (i, 0)), whole[1]],
        out_shape=[jax.ShapeDtypeStruct((t - CHUNK, d), F32), jax.ShapeDtypeStruct((1, d), F32)],
        compiler_params=_cparams("arbitrary"), name=name)(dz, w, h, g, dh, dg_head)
    return dx, dh_head, dg


def _loss_bwd(h, g, target):
    t, d = h.shape
    tl = _row_tile(t - CHUNK, 1024, CHUNK)

    def body(h_ref, g_ref, t_ref, dh_ref, dg_ref, loss_ref):
        i = pl.program_id(0)
        hv, gv = h_ref[...], g_ref[...]
        err = _rms(hv, gv) - t_ref[...]
        part = 0.5 * jnp.sum(jnp.mean(err * err, axis=-1, keepdims=True), axis=0, keepdims=True)
        dhn, dgr = _rms_bwd(hv, gv, err * (1.0 / d))
        dh_ref[...] = dhn
        _accumulate(dg_ref, jnp.sum(dgr, axis=0, keepdims=True), i == 0)
        _accumulate(loss_ref, jnp.broadcast_to(part, (8, 128)), i == 0)

    shifted = pl.BlockSpec((pl.Element(tl), pl.Element(d)), lambda i: (pl.multiple_of(CHUNK + i * tl, CHUNK), 0))
    dh, dg, loss = pl.pallas_call(
        body, grid=((t - CHUNK) // tl,),
        in_specs=[shifted, pl.BlockSpec((1, d), lambda i: (0, 0)), pl.BlockSpec((tl, d), lambda i: (i, 0))],
        out_specs=[shifted, pl.BlockSpec((1, d), lambda i: (0, 0)), pl.BlockSpec((8, 128), lambda i: (0, 0))],
        out_shape=[jax.ShapeDtypeStruct((t, d), F32), jax.ShapeDtypeStruct((1, d), F32),
                   jax.ShapeDtypeStruct((8, 128), F32)],
        compiler_params=_cparams("arbitrary"), name="loss_bwd")(h, g, target)

    def zero_head(dh_ref, o_ref):
        o_ref[...] = jnp.zeros_like(o_ref)

    dh = pl.pallas_call(
        zero_head, grid=(1,), in_specs=[ANY_SPEC], out_specs=pl.BlockSpec((CHUNK, d), lambda i: (0, 0)),
        out_shape=jax.ShapeDtypeStruct((t, d), F32), input_output_aliases={0: 0}, name="loss_bwd_head")(dh)
    return dh, dg, loss


CONV_BLOCK = 32


def _silu(x):
    return x * jax.nn.sigmoid(x)


def _row_shifts(win):
    n = win.shape[0]
    return [win] + [pltpu.roll(win, n - j, 0) for j in range(1, 8)]


def _cp_seq_fwd(z, conv_w, conv_b, ln_g, ln_b, pool_w, pool_scale):
    t = z.shape[0]
    tm = _row_tile(t, ROW_TILE_TARGET, CHUNK)

    def body(z_ref, cw_ref, cb_ref, lg_ref, lb_ref, pw_ref, ps_ref, c_ref, pm_ref, mix_ref, gbuf, pbuf):
        i = pl.program_id(0)

        @pl.when(i == 0)
        def _():
            gbuf[0:CONV_HALO, :] = jnp.zeros((CONV_HALO, CONV_DIM), F32)
            pbuf[0:POOL_HALO, :] = jnp.zeros((POOL_HALO, POOL_DIM), F32)

        @pl.when(i > 0)
        def _():
            gbuf[0:CONV_HALO, :] = gbuf[tm:tm + CONV_HALO, :]
            pbuf[0:POOL_HALO, :] = pbuf[tm:tm + POOL_HALO, :]

        av = z_ref[:, 0:CONV_DIM].astype(F32)
        ag = z_ref[:, CONV_DIM:2 * CONV_DIM].astype(F32)
        gbuf[CONV_HALO:CONV_HALO + tm, :] = av * jax.nn.sigmoid(ag)
        pbuf[POOL_HALO:POOL_HALO + tm, :] = z_ref[:, 2 * CONV_DIM:CP_IN].astype(F32)

        def conv_block(rb, carry):
            base = pl.multiple_of(rb * CONV_BLOCK, CONV_BLOCK)
            shifted = _row_shifts(gbuf[pl.ds(base, CONV_BLOCK + CONV_HALO), :])
            acc = jnp.zeros((CONV_BLOCK, CONV_DIM), F32)
            for k in range(CONV_WIDTH):
                whole, part = divmod(CONV_HALO - (CONV_WIDTH - 1) + k, 8)
                acc = acc + cw_ref[k:k + 1, :] * shifted[part][8 * whole:8 * whole + CONV_BLOCK, :]
            c_ref[pl.ds(base, CONV_BLOCK), :] = acc + cb_ref[...]
            return carry

        lax.fori_loop(0, tm // CONV_BLOCK, conv_block, 0)

        c = c_ref[...]
        mu = jnp.mean(c, axis=-1, keepdims=True)
        xc = c - mu
        ln = xc * lax.rsqrt(jnp.mean(xc * xc, axis=-1, keepdims=True) + EPS) * lg_ref[...] + lb_ref[...]
        row = i * tm + lax.broadcasted_iota(jnp.int32, (tm, 1), 0)
        mix_ref[:, 0:CONV_DIM] = jnp.where(row >= PAD_ROWS, _silu(ln), 0.0).astype(BF16)

        tpos = (row - PAD_ROWS + 1).astype(F32)
        for gi, wdw in enumerate(POOL_WINDOWS):
            lo = POOL_GROUP * gi
            run, step = pbuf[:, lo:lo + POOL_GROUP], 1
            cur = run[POOL_HALO:POOL_HALO + tm, :]
            while step < wdw:
                run = run + pltpu.roll(run, step, 0)
                step *= 2
            pm = (run[POOL_HALO:POOL_HALO + tm, :] / jnp.clip(tpos, 1.0, float(wdw)) - cur).astype(BF16)
            pm_ref[:, lo:lo + POOL_GROUP] = pm
            pg = _dot(pm, pw_ref[gi].astype(BF16))
            mix_ref[:, CONV_DIM + lo:CONV_DIM + lo + POOL_GROUP] = (pg * ps_ref[:, lo:lo + POOL_GROUP]).astype(BF16)

    vec = pl.BlockSpec((1, CONV_DIM), lambda i: (0, 0))
    return pl.pallas_call(
        body, grid=(t // tm,),
        in_specs=[pl.BlockSpec((tm, CP_IN), lambda i: (i, 0)),
                  pl.BlockSpec((CONV_WIDTH, CONV_DIM), lambda i: (0, 0)), vec, vec, vec,
                  pl.BlockSpec((len(POOL_WINDOWS), POOL_GROUP, POOL_GROUP), lambda i: (0, 0, 0)), vec],
        out_specs=[pl.BlockSpec((tm, CONV_DIM), lambda i: (i, 0)), pl.BlockSpec((tm, POOL_DIM), lambda i: (i, 0)),
                   pl.BlockSpec((tm, CONV_DIM + POOL_DIM), lambda i: (i, 0))],
        out_shape=[jax.ShapeDtypeStruct((t, CONV_DIM), F32), jax.ShapeDtypeStruct((t, POOL_DIM), BF16),
                   jax.ShapeDtypeStruct((t, CONV_DIM + POOL_DIM), BF16)],
        scratch_shapes=[pltpu.VMEM((tm + CONV_HALO, CONV_DIM), F32), pltpu.VMEM((tm + POOL_HALO, POOL_DIM), F32)],
        compiler_params=_cparams("arbitrary"), name="cp_seq_fwd")(z, conv_w, conv_b, ln_g, ln_b, pool_w, pool_scale)


def _cp_seq_bwd(dmix, z, c, pm, conv_w, ln_g, ln_b, pool_w, pool_scale, dep=None):
    t = z.shape[0]
    tm = _row_tile(t, ROW_TILE_TARGET, CHUNK)
    nt = t // tm

    def body(dmix_ref, z_ref, c_ref, pm_ref, cw_ref, lg_ref, lb_ref, pw_ref, ps_ref,
             dz_ref, dcw_ref, dvec_ref, dpw_ref, dcbuf, qbuf, glu_buf, dwacc):
        i = pl.program_id(0)
        tile = nt - 1 - i

        @pl.when(i == 0)
        def _():
            dcbuf[tm:tm + CONV_HALO, :] = jnp.zeros((CONV_HALO, CONV_DIM), F32)
            qbuf[tm:tm + POOL_HALO, :] = jnp.zeros((POOL_HALO, POOL_DIM), F32)
            dcw_ref[...] = jnp.zeros_like(dcw_ref)
            dwacc[...] = jnp.zeros_like(dwacc)
            dvec_ref[...] = jnp.zeros_like(dvec_ref)
            dpw_ref[...] = jnp.zeros_like(dpw_ref)

        @pl.when(i > 0)
        def _():
            dcbuf[tm:tm + CONV_HALO, :] = dcbuf[0:CONV_HALO, :]
            qbuf[tm:tm + POOL_HALO, :] = qbuf[0:POOL_HALO, :]

        row = tile * tm + lax.broadcasted_iota(jnp.int32, (tm, 1), 0)
        cv = c_ref[...]
        mu = jnp.mean(cv, axis=-1, keepdims=True)
        xc = cv - mu
        rstd = lax.rsqrt(jnp.mean(xc * xc, axis=-1, keepdims=True) + EPS)
        xhat = xc * rstd
        ln = xhat * lg_ref[...] + lb_ref[...]
        sg = jax.nn.sigmoid(ln)
        da = jnp.where(row >= PAD_ROWS, dmix_ref[:, 0:CONV_DIM].astype(F32), 0.0)
        dln = da * (sg * (1.0 + ln * (1.0 - sg)))
        dxh = dln * lg_ref[...]
        dc = rstd * (dxh - jnp.mean(dxh, axis=-1, keepdims=True) - xhat * jnp.mean(dxh * xhat, axis=-1, keepdims=True))
        dcbuf[0:tm, :] = dc
        dvec_ref[0:1, :] += jnp.sum(dc, axis=0, keepdims=True)
        dvec_ref[1:2, :] += jnp.sum(dln * xhat, axis=0, keepdims=True)
        dvec_ref[2:3, :] += jnp.sum(dln, axis=0, keepdims=True)

        av = z_ref[:, 0:CONV_DIM].astype(F32)
        sig_g = jax.nn.sigmoid(z_ref[:, CONV_DIM:2 * CONV_DIM].astype(F32))
        glu_buf[...] = av * sig_g

        def conv_block(rb, carry):
            base = pl.multiple_of(rb * CONV_BLOCK, CONV_BLOCK)
            shifted = _row_shifts(dcbuf[pl.ds(base, CONV_BLOCK + CONV_HALO), :])
            glu = glu_buf[pl.ds(base, CONV_BLOCK), :]
            acc = jnp.zeros((CONV_BLOCK, CONV_DIM), F32)
            for k in range(CONV_WIDTH):
                whole, part = divmod(CONV_WIDTH - 1 - k, 8)
                slab = shifted[part][8 * whole:8 * whole + CONV_BLOCK, :]
                acc = acc + cw_ref[k:k + 1, :] * slab
                prod = slab * glu
                part = prod[0:8]
                for q in range(1, CONV_BLOCK // 8):
                    part = part + prod[8 * q:8 * q + 8]
                dwacc[k] += part
            glu_buf[pl.ds(base, CONV_BLOCK), :] = acc
            return carry

        lax.fori_loop(0, tm // CONV_BLOCK, conv_block, 0)

        @pl.when(i == nt - 1)
        def _():
            for k in range(CONV_WIDTH):
                dcw_ref[k:k + 1, :] = jnp.sum(dwacc[k], axis=0, keepdims=True)
        dglu = glu_buf[...]
        dz_ref[:, 0:CONV_DIM] = (dglu * sig_g).astype(BF16)
        dz_ref[:, CONV_DIM:2 * CONV_DIM] = (dglu * av * sig_g * (1.0 - sig_g)).astype(BF16)

        tpos = (row - PAD_ROWS + 1).astype(F32)
        for gi, wdw in enumerate(POOL_WINDOWS):
            lo = POOL_GROUP * gi
            dp = dmix_ref[:, CONV_DIM + lo:CONV_DIM + lo + POOL_GROUP].astype(F32)
            pmv = pm_ref[:, lo:lo + POOL_GROUP]
            pwb = pw_ref[gi].astype(BF16)
            dvec_ref[3:4, lo:lo + POOL_GROUP] += jnp.sum(dp * _dot(pmv, pwb), axis=0, keepdims=True)
            dq = (dp * ps_ref[:, lo:lo + POOL_GROUP]).astype(BF16)
            dpw_ref[gi] += _dot_tn(pmv, dq)
            dpm = _dot_nt(dq, pwb)
            qbuf[0:tm, lo:lo + POOL_GROUP] = dpm / jnp.clip(tpos, 1.0, float(wdw))
            run, step = qbuf[:, lo:lo + POOL_GROUP], 1
            while step < wdw:
                run = run + pltpu.roll(run, tm + POOL_HALO - step, 0)
                step *= 2
            dz_ref[:, 2 * CONV_DIM + lo:2 * CONV_DIM + lo + POOL_GROUP] = (run[0:tm, :] - dpm).astype(BF16)

    vec = pl.BlockSpec((1, CONV_DIM), lambda i: (0, 0))
    rev = lambda i: (nt - 1 - i, 0)
    return _call_after(
        dep, body, 9,
        [pl.BlockSpec((tm, CONV_DIM + POOL_DIM), rev), pl.BlockSpec((tm, CP_IN), rev),
         pl.BlockSpec((tm, CONV_DIM), rev), pl.BlockSpec((tm, POOL_DIM), rev),
         pl.BlockSpec((CONV_WIDTH, CONV_DIM), lambda i: (0, 0)), vec, vec,
         pl.BlockSpec((len(POOL_WINDOWS), POOL_GROUP, POOL_GROUP), lambda i: (0, 0, 0)), vec],
        (dmix, z, c, pm, conv_w, ln_g, ln_b, pool_w, pool_scale), grid=(nt,),
        out_specs=[pl.BlockSpec((tm, CP_IN), rev), pl.BlockSpec((CONV_WIDTH + 1, CONV_DIM), lambda i: (0, 0)),
                   pl.BlockSpec((8, CONV_DIM), lambda i: (0, 0)),
                   pl.BlockSpec((len(POOL_WINDOWS), POOL_GROUP, POOL_GROUP), lambda i: (0, 0, 0))],
        out_shape=[jax.ShapeDtypeStruct((t, CP_IN), BF16), jax.ShapeDtypeStruct((CONV_WIDTH + 1, CONV_DIM), F32),
                   jax.ShapeDtypeStruct((8, CONV_DIM), F32),
                   jax.ShapeDtypeStruct((len(POOL_WINDOWS), POOL_GROUP, POOL_GROUP), F32)],
        scratch_shapes=[pltpu.VMEM((tm + CONV_HALO, CONV_DIM), F32), pltpu.VMEM((tm + POOL_HALO, POOL_DIM), F32),
                        pltpu.VMEM((tm, CONV_DIM), F32), pltpu.VMEM((CONV_WIDTH + 1, 8, CONV_DIM), F32)],
        compiler_params=_cparams("arbitrary"), name="cp_seq_bwd")


Q0, K0, V0, G0, R0 =0, GLA_DK, 2 * GLA_DK, 2 * GLA_DK + GLA_DV, 2 * GLA_DK + 2 * GLA_DV


def _split3(x):
    hi = x.astype(BF16)
    r1 = x - hi.astype(F32)
    mid = r1.astype(BF16)
    lo = (r1 - mid.astype(F32)).astype(BF16)
    return hi, mid, lo


def _tri(strict):
    r = lax.broadcasted_iota(jnp.int32, (CHUNK, CHUNK), 0)
    c = lax.broadcasted_iota(jnp.int32, (CHUNK, CHUNK), 1)
    return ((r > c) if strict else (r >= c)).astype(BF16)


def _chunk_sums(x, cpt, strict, pieces):
    tri3 = jnp.broadcast_to(_tri(strict)[None], (cpt, CHUNK, CHUNK))
    acc = None
    for piece in _split3(x.reshape(cpt, CHUNK, x.shape[-1]))[:pieces]:
        part = jnp.einsum("bij,bjk->bik", tri3, piece, preferred_element_type=F32)
        acc = part if acc is None else acc + part
    return acc


def _chunk_decay(r, gw_ref, gb_ref, cpt):
    pre = _dot(r, gw_ref[...]) + gb_ref[...]
    lac = (jnp.minimum(pre, 0.0) - jnp.log(1.0 + jnp.exp(-jnp.abs(pre)))) * (1.0 / GATE_NORM)
    cum3 = _chunk_sums(lac, cpt, False, 3)
    return cum3, cum3[:, CHUNK - 1:CHUNK, :]


def _gla_seq_fwd(z, gate_w, gate_b, head_g, dep=None):
    t = z.shape[0]
    tm = _row_tile(t, ROW_TILE_TARGET, CHUNK)
    cpt = tm // CHUNK
    scale = GLA_HK ** -0.5

    def body(z_ref, gw_ref, gb_ref, hg_ref, o_ref, mix_ref, st_ref, state, kdec_s, e_s):
        @pl.when(pl.program_id(0) == 0)
        def _():
            state[...] = jnp.zeros_like(state)

        cum3, tot3 = _chunk_decay(z_ref[:, R0:R0 + GATE_PAD], gw_ref, gb_ref, cpt)
        dec = jnp.exp(jnp.broadcast_to(tot3, cum3.shape) - cum3).reshape(tm, GLA_DK)
        kdec_s[...] = (z_ref[:, K0:K0 + GLA_DK].astype(F32) * dec).astype(BF16)
        e_s[...] = jnp.exp(jnp.broadcast_to(tot3, (cpt, 8, GLA_DK))).reshape(cpt * 8, GLA_DK)

        def chunk(ci, carry):
            rows = pl.ds(pl.multiple_of(ci * CHUNK, CHUNK), CHUNK)
            e_all = e_s[pl.ds(pl.multiple_of(ci * 8, 8), 8), :][0:1, :]
            st_ref[ci] = state[...].astype(BF16)
            for hd in range(GLA_HEADS):
                ks = slice(hd * GLA_HK, (hd + 1) * GLA_HK)
                vs = slice(hd * GLA_HV, (hd + 1) * GLA_HV)
                v = z_ref[rows, V0 + hd * GLA_HV:V0 + (hd + 1) * GLA_HV]
                st = state[vs, :] * e_all[:, ks] + _dot_tn(v, kdec_s[rows, ks])
                state[vs, :] = st
                q = z_ref[rows, Q0 + hd * GLA_HK:Q0 + (hd + 1) * GLA_HK]
                o_ref[rows, vs] = (_dot_nt(q, st.astype(BF16)) * scale).astype(BF16)
            return carry

        lax.fori_loop(0, cpt, chunk, 0, unroll=cpt)

        for hd in range(GLA_HEADS):
            vs = slice(hd * GLA_HV, (hd + 1) * GLA_HV)
            on = _rms(o_ref[:, vs].astype(F32), hg_ref[...])
            gv = z_ref[:, G0 + hd * GLA_HV:G0 + (hd + 1) * GLA_HV].astype(F32)
            mix_ref[:, vs] = (on * _silu(gv)).astype(BF16)

    return _call_after(
        dep, body, 4,
        [pl.BlockSpec((tm, GLA_IN_PAD), lambda i: (i, 0)),
         pl.BlockSpec((GATE_PAD, GLA_DK), lambda i: (0, 0)), pl.BlockSpec((1, GLA_DK), lambda i: (0, 0)),
         pl.BlockSpec((1, GLA_HV), lambda i: (0, 0))], (z, gate_w, gate_b, head_g), grid=(t // tm,),
        out_specs=[pl.BlockSpec((tm, GLA_DV), lambda i: (i, 0)), pl.BlockSpec((tm, GLA_DV), lambda i: (i, 0)),
                   pl.BlockSpec((cpt, GLA_DV, GLA_HK), lambda i: (i, 0, 0))],
        out_shape=[jax.ShapeDtypeStruct((t, GLA_DV), BF16), jax.ShapeDtypeStruct((t, GLA_DV), BF16),
                   jax.ShapeDtypeStruct((t // CHUNK, GLA_DV, GLA_HK), BF16)],
        scratch_shapes=[pltpu.VMEM((GLA_DV, GLA_HK), F32), pltpu.VMEM((tm, GLA_DK), BF16),
                        pltpu.VMEM((cpt * 8, GLA_DK), F32)],
        compiler_params=_cparams("arbitrary"), name="gla_seq_fwd")


def _gla_seq_bwd(dmix, o, z, states, gate_w, gate_b, head_g, dep=None):
    t = z.shape[0]
    tm = _row_tile(t, ROW_TILE_TARGET, CHUNK)
    cpt = tm // CHUNK
    nt = t // tm
    scale = GLA_HK ** -0.5

    def body(dmix_ref, o_ref, z_ref, st_ref, gw_ref, gb_ref, hg_ref, dz_ref, dgw_ref, dgb_ref, dhg_ref,
             dstate, dec_s, kdec_s, dkdec_s, do_s, e_s, dtot_s):
        @pl.when(pl.program_id(0) == 0)
        def _():
            dstate[...] = jnp.zeros_like(dstate)
            dgw_ref[...] = jnp.zeros_like(dgw_ref)
            dgb_ref[...] = jnp.zeros_like(dgb_ref)
            dhg_ref[...] = jnp.zeros_like(dhg_ref)

        cum3, tot3 = _chunk_decay(z_ref[:, R0:R0 + GATE_PAD], gw_ref, gb_ref, cpt)
        dec = jnp.exp(jnp.broadcast_to(tot3, cum3.shape) - cum3).reshape(tm, GLA_DK)
        dec_s[...] = dec
        kdec = z_ref[:, K0:K0 + GLA_DK].astype(F32) * dec
        kdec_s[...] = kdec
        e3 = jnp.exp(tot3)
        e_s[...] = jnp.broadcast_to(e3, (cpt, 8, GLA_DK)).reshape(cpt * 8, GLA_DK)
        dhg = jnp.zeros((1, GLA_HV), F32)
        for hd in range(GLA_HEADS):
            ks = slice(hd * GLA_HK, (hd + 1) * GLA_HK)
            vs = slice(hd * GLA_HV, (hd + 1) * GLA_HV)
            gcols = slice(G0 + hd * GLA_HV, G0 + (hd + 1) * GLA_HV)
            ov = o_ref[:, vs].astype(F32)
            gv = z_ref[:, gcols].astype(F32)
            dm = dmix_ref[:, vs].astype(F32)
            sg = jax.nn.sigmoid(gv)
            rr = lax.rsqrt(jnp.mean(ov * ov, axis=-1, keepdims=True) + EPS)
            xhat = ov * rr
            don = dm * (gv * sg)
            dz_ref[:, gcols] = (dm * (xhat * hg_ref[...]) * (sg * (1.0 + gv * (1.0 - sg)))).astype(BF16)
            dhg = dhg + jnp.sum(don * xhat, axis=0, keepdims=True)
            dxh = don * hg_ref[...]
            do = (rr * (dxh - xhat * jnp.mean(dxh * xhat, axis=-1, keepdims=True)) * scale).astype(BF16)
            do_s[:, vs] = do
            v3 = z_ref[:, V0 + hd * GLA_HV:V0 + (hd + 1) * GLA_HV].reshape(cpt, CHUNK, GLA_HV)
            kdb3 = kdec[:, ks].astype(BF16).reshape(cpt, CHUNK, GLA_HK)
            st3 = st_ref[:, vs, :].astype(F32) * e3[:, :, ks] + jnp.einsum("bcv,bck->bvk", v3, kdb3,
                                                                            preferred_element_type=F32)
            dq3 = jnp.einsum("bcv,bvk->bck", do.reshape(cpt, CHUNK, GLA_HV), st3.astype(BF16), preferred_element_type=F32)
            dz_ref[:, Q0 + hd * GLA_HK:Q0 + (hd + 1) * GLA_HK] = dq3.reshape(tm, GLA_HK).astype(BF16)
        dhg_ref[...] += dhg

        def chunk(cj, carry):
            ci = cpt - 1 - cj
            rows = pl.ds(pl.multiple_of(ci * CHUNK, CHUNK), CHUNK)
            erows = pl.ds(pl.multiple_of(ci * 8, 8), 8)
            e_all = e_s[erows, :][0:1, :]
            for hd in range(GLA_HEADS):
                ks = slice(hd * GLA_HK, (hd + 1) * GLA_HK)
                vs = slice(hd * GLA_HV, (hd + 1) * GLA_HV)
                e = e_all[:, ks]
                kdb = kdec_s[rows, ks].astype(BF16)
                v = z_ref[rows, V0 + hd * GLA_HV:V0 + (hd + 1) * GLA_HV]
                q = z_ref[rows, Q0 + hd * GLA_HK:Q0 + (hd + 1) * GLA_HK]
                do = do_s[rows, vs]
                st_prev = st_ref[ci, vs, :].astype(F32)
                dst = dstate[vs, :] + _dot_tn(do, q)
                dstb = dst.astype(BF16)
                dkdec_s[rows, ks] = _dot(v, dstb)
                dz_ref[rows, V0 + hd * GLA_HV:V0 + (hd + 1) * GLA_HV] = _dot_nt(kdb, dstb).astype(BF16)
                dtot = jnp.sum(dst * st_prev, axis=0, keepdims=True) * e
                dtot_s[erows, ks] = jnp.broadcast_to(dtot, (8, GLA_HK))
                dstate[vs, :] = dst * e
            return carry

        lax.fori_loop(0, cpt, chunk, 0, unroll=cpt)

        dkdec = dkdec_s[...]
        dz_ref[:, K0:K0 + GLA_DK] = (dkdec * dec_s[...]).astype(BF16)
        before = _chunk_sums(dkdec * kdec_s[...], cpt, True, 2)
        dtot3 = dtot_s[...].reshape(cpt, 8, GLA_DK)[:, 0:1, :]
        dlac = (jnp.broadcast_to(dtot3, before.shape) + before).reshape(tm, GLA_DK)
        pre = _dot(z_ref[:, R0:R0 + GATE_PAD], gw_ref[...]) + gb_ref[...]
        dpre = dlac * (1.0 / GATE_NORM) * (1.0 - jax.nn.sigmoid(pre))
        dpb = dpre.astype(BF16)
        dz_ref[:, R0:R0 + GATE_PAD] = _dot_nt(dpb, gw_ref[...]).astype(BF16)
        dgw_ref[...] += _dot_tn(z_ref[:, R0:R0 + GATE_PAD], dpb)
        dgb_ref[...] += jnp.sum(dpre, axis=0, keepdims=True)

    rev = lambda i: (nt - 1 - i, 0)
    return _call_after(
        dep, body, 7,
        [pl.BlockSpec((tm, GLA_DV), rev), pl.BlockSpec((tm, GLA_DV), rev), pl.BlockSpec((tm, GLA_IN_PAD), rev),
         pl.BlockSpec((cpt, GLA_DV, GLA_HK), lambda i: (nt - 1 - i, 0, 0)),
         pl.BlockSpec((GATE_PAD, GLA_DK), lambda i: (0, 0)), pl.BlockSpec((1, GLA_DK), lambda i: (0, 0)),
         pl.BlockSpec((1, GLA_HV), lambda i: (0, 0))],
        (dmix, o, z, states, gate_w, gate_b, head_g), grid=(nt,),
        out_specs=[pl.BlockSpec((tm, GLA_IN_PAD), rev), pl.BlockSpec((GATE_PAD, GLA_DK), lambda i: (0, 0)),
                   pl.BlockSpec((1, GLA_DK), lambda i: (0, 0)), pl.BlockSpec((1, GLA_HV), lambda i: (0, 0))],
        out_shape=[jax.ShapeDtypeStruct((t, GLA_IN_PAD), BF16), jax.ShapeDtypeStruct((GATE_PAD, GLA_DK), F32),
                   jax.ShapeDtypeStruct((1, GLA_DK), F32), jax.ShapeDtypeStruct((1, GLA_HV), F32)],
        scratch_shapes=[pltpu.VMEM((GLA_DV, GLA_HK), F32), pltpu.VMEM((tm, GLA_DK), F32), pltpu.VMEM((tm, GLA_DK), F32),
                        pltpu.VMEM((tm, GLA_DK), F32), pltpu.VMEM((tm, GLA_DV), BF16),
                        pltpu.VMEM((cpt * 8, GLA_DK), F32), pltpu.VMEM((cpt * 8, GLA_DK), F32)],
        compiler_params=_cparams("arbitrary"), name="gla_seq_bwd")


def _sum_slots(x, name):
    n, r, cdim = x.shape
    tr = _row_tile(r, 256, 8)

    def body(x_ref, o_ref):
        acc = x_ref[0].astype(F32)
        for j in range(1, n):
            acc = acc + x_ref[j].astype(F32)
        o_ref[...] = acc

    return pl.pallas_call(
        body, grid=(r // tr,),
        in_specs=[pl.BlockSpec((n, tr, cdim), lambda i: (0, i, 0))],
        out_specs=pl.BlockSpec((tr, cdim), lambda i: (i, 0)),
        out_shape=jax.ShapeDtypeStruct((r, cdim), F32),
        compiler_params=_cparams("parallel"), name=name)(x)


def _sum_own_and_slots(own, slots, dev_idx, name):
    _, _, r, cdim = own.shape
    n = slots.shape[0]
    tr = _row_tile(r, 256, 8)

    def body(s_ref, own_ref, *rest):
        acc = own_ref[...].astype(F32)
        for other in rest[:n - 1]:
            acc = acc + other[...].astype(F32)
        rest[n - 1][...] = acc

    def slot(dd):
        return pl.BlockSpec((None, tr, cdim), lambda i, s: ((s[0] + dd) % n, i, 0))

    mine = pl.BlockSpec((None, None, tr, cdim), lambda i, s: (s[0] // 2, s[0] % 2, i, 0))
    return pl.pallas_call(
        body,
        grid_spec=pltpu.PrefetchScalarGridSpec(
            num_scalar_prefetch=1, grid=(r // tr,), in_specs=[mine] + [slot(dd) for dd in range(1, n)],
            out_specs=pl.BlockSpec((tr, cdim), lambda i, s: (i, 0))),
        out_shape=jax.ShapeDtypeStruct((r, cdim), F32),
        compiler_params=_cparams("parallel"), name=name)(dev_idx, own, *([slots] * (n - 1)))


def _add2(a, b, name):
    r, cdim = a.shape
    tr = _row_tile(r, 256, 8)

    def body(a_ref, b_ref, o_ref):
        o_ref[...] = a_ref[...] + b_ref[...]

    spec = pl.BlockSpec((tr, cdim), lambda i: (i, 0))
    return pl.pallas_call(body, grid=(r // tr,), in_specs=[spec, spec], out_specs=spec,
                          out_shape=jax.ShapeDtypeStruct((r, cdim), F32),
                          compiler_params=_cparams("parallel"), name=name)(a, b)


def _adamw_half(w, gs, m, v, half_idx, prev, name, dep=None):
    nl, _, h, cdim = w.shape
    tr = _row_tile(h, 256, 8)
    nprev = 0 if prev is None else 4
    extra = [] if dep is None else [dep]

    def body(s_ref, w_ref, m_ref, v_ref, *rest):
        g_refs = rest[:nl]
        go_ref, d_ref, mo_ref, vo_ref = rest[nl + nprev + len(extra):]
        layer = pl.program_id(0)
        gv = g_refs[0][...]
        for j in range(1, nl):
            gv = jnp.where(layer == j, g_refs[j][...], gv)
        go_ref[...] = gv
        mn = ADAM_B1 * m_ref[...] + (1.0 - ADAM_B1) * gv
        vn = ADAM_B2 * v_ref[...] + (1.0 - ADAM_B2) * (gv * gv)
        m_hat = mn / (1.0 - ADAM_B1 ** ADAM_STEP)
        v_hat = vn / (1.0 - ADAM_B2 ** ADAM_STEP)
        d_ref[...] = -ADAM_LR * (m_hat / (jnp.sqrt(v_hat) + ADAM_EPS) + ADAM_WD * w_ref[...])
        mo_ref[...] = mn
        vo_ref[...] = vn

    half = pl.BlockSpec((None, None, tr, cdim), lambda l, i, s: (l, s[0], i, 0))

    def of_layer(j):
        return pl.BlockSpec((tr, cdim), lambda l, i, s: (jnp.where(l == j, i, 0), 0))

    shp = jax.ShapeDtypeStruct(w.shape, F32)
    return pl.pallas_call(
        body,
        grid_spec=pltpu.PrefetchScalarGridSpec(
            num_scalar_prefetch=1, grid=(nl, h // tr),
            in_specs=[half] * 3 + [of_layer(j) for j in range(nl)] + [ANY_SPEC] * (nprev + len(extra)),
            out_specs=[half] * 4),
        out_shape=[shp] * 4, input_output_aliases={4 + nl + k: k for k in range(nprev)},
        compiler_params=_cparams("arbitrary", "arbitrary"), name=name,
    )(half_idx, w, m, v, *gs, *([] if prev is None else prev), *extra)


def _adamw_many(ws, gs, ms, vs):
    n = len(ws)

    def body(*refs):
        for i in range(n):
            w_ref, g_ref, m_ref, v_ref = refs[i], refs[n + i], refs[2 * n + i], refs[3 * n + i]
            d_ref, mo_ref, vo_ref = refs[4 * n + i], refs[5 * n + i], refs[6 * n + i]
            gv = g_ref[...]
            mn = ADAM_B1 * m_ref[...] + (1.0 - ADAM_B1) * gv
            vn = ADAM_B2 * v_ref[...] + (1.0 - ADAM_B2) * (gv * gv)
            m_hat = mn / (1.0 - ADAM_B1 ** ADAM_STEP)
            v_hat = vn / (1.0 - ADAM_B2 ** ADAM_STEP)
            d_ref[...] = -ADAM_LR * (m_hat / (jnp.sqrt(v_hat) + ADAM_EPS) + ADAM_WD * w_ref[...])
            mo_ref[...] = mn
            vo_ref[...] = vn

    shapes = [jax.ShapeDtypeStruct(w.shape, F32) for w in ws]
    outs = pl.pallas_call(body, out_shape=shapes * 3, name="adamw_small")(*ws, *gs, *ms, *vs)
    return outs[:n], outs[n:2 * n], outs[2 * n:]


def _split_rows(a):
    return a.reshape(a.shape[0], 2, a.shape[1] // 2, a.shape[2])


def _place():
    x, y, c = lax.axis_index("x"), lax.axis_index("y"), lax.axis_index("c")
    chips = [(1 - x, y), (x, 1 - y), (1 - x, 1 - y)]
    return x, y, c, chips


def _remote(src, dst, send_sem, recv_sem, to):
    return pltpu.make_async_remote_copy(src_ref=src, dst_ref=dst, send_sem=send_sem, recv_sem=recv_sem,
                                        device_id=to, device_id_type=MESH)


def _plan_gather(n_halved):
    def plan(src_refs, land_refs):
        x, y, c, chips = _place()
        me = 2 * x + y
        copies = []
        for k, (src, land) in enumerate(zip(src_refs, land_refs)):
            for (px, py) in chips:
                frm = 2 * px + py
                if k < n_halved:
                    copies.append((src.at[c], land.at[me, c], (px, py, c), land.at[frm, c]))
                else:
                    copies.append((src, land.at[me], (px, py, c), land.at[frm]))
        return copies
    return plan


def _plan_share(src_refs, land_refs):
    x, y, c, chips = _place()
    me = 2 * x + y
    sib = (x, y, 1 - c)
    copies = []
    for src, land in zip(src_refs, land_refs):
        copies.append((src, land.at[me], sib, land.at[me]))
        for (px, py) in chips:
            frm = 2 * px + py
            copies.append((land.at[frm, c], land.at[frm, c], sib, land.at[frm, 1 - c]))
    return copies


def _plan_scatter(n_parts):
    def plan(src_refs, land_refs):
        x, y, c, chips = _place()
        me = 2 * x + y
        copies = []
        for k, (src, land) in enumerate(zip(src_refs, land_refs)):
            for (px, py) in chips:
                to = 2 * px + py
                copies.append((src.at[to] if k < n_parts else src, land.at[me], (px, py, c), land.at[to]))
        return copies
    return plan


N_DEVICES = 8
OTHER_DEVICES = [(dx, dy, dc) for dx in (0, 1) for dy in (0, 1) for dc in (0, 1) if dx or dy or dc]


def _plan_scatter_all(src_refs, land_refs):
    x, y, c, _ = _place()
    me = 4 * x + 2 * y + c
    copies = []
    for src, land in zip(src_refs, land_refs):
        for dx, dy, dc in OTHER_DEVICES:
            px, py, pc = (1 - x if dx else x), (1 - y if dy else y), (1 - c if dc else c)
            copies.append((src.at[2 * px + py, pc], land.at[me], (px, py, pc), land.at[4 * px + 2 * py + pc]))
    return copies


def _plan_exchange(n_split):
    def plan(src_refs, land_refs):
        x, y, c, _ = _place()
        sib = (x, y, 1 - c)
        return [(src.at[:, 1 - c] if k < n_split else src, land, sib, land)
                for k, (src, land) in enumerate(zip(src_refs, land_refs))]
    return plan


def _hbm(a):
    return pltpu.HBM(a.shape, a.dtype)


def _start_copies(name, srcs, lands, plan, ncopy, dep=None):
    ns, nl = len(srcs), len(lands)
    nin = ns + nl + (0 if dep is None else 1)

    def body(*refs):
        send_sems, recv_sems, token = refs[nin], refs[nin + 1], refs[-1]
        for k, (src, dst, dev, _) in enumerate(plan(refs[:ns], refs[ns:ns + nl])):
            _remote(src, dst, send_sems.at[k], recv_sems.at[k], dev).start()
        token[...] = jnp.zeros_like(token)

    args = [pltpu.with_memory_space_constraint(a, pltpu.HBM) for a in list(srcs) + list(lands)]
    outs = pl.pallas_call(
        body, name=name,
        out_shape=(pltpu.SemaphoreType.DMA((ncopy,)), pltpu.SemaphoreType.DMA((ncopy,)),
                   *[_hbm(a) for a in list(srcs) + list(lands)], jax.ShapeDtypeStruct((8, 128), F32)),
        in_specs=[HBM_SPEC] * (ns + nl) + ([] if dep is None else [ANY_SPEC]),
        out_specs=(SEM_SPEC, SEM_SPEC, *([HBM_SPEC] * (ns + nl)), pl.BlockSpec(memory_space=pltpu.VMEM)),
        input_output_aliases={i: 2 + i for i in range(ns + nl)},
        compiler_params=pltpu.CompilerParams(has_side_effects=SIDE_EFFECT),
    )(*args, *([] if dep is None else [dep]))
    return outs[0], outs[1], list(outs[2:2 + ns]), list(outs[2 + ns:2 + ns + nl]), outs[-1]


def _wait_copies(name, started, plan, after, sem_offset=0):
    send_sems, recv_sems, srcs, lands, _ = started
    ns, nl = len(srcs), len(lands)
    after = list(after) if isinstance(after, (list, tuple)) else [after]

    def body(*refs):
        send_ref, recv_ref = refs[ns + nl], refs[ns + nl + 1]
        for k, (src, _, dev, mine) in enumerate(plan(refs[:ns], refs[ns:ns + nl])):
            copy = _remote(src, mine, send_ref.at[sem_offset + k], recv_ref.at[sem_offset + k], dev)
            copy.wait_send()
            copy.wait_recv()

    outs = pl.pallas_call(
        body, name=name, out_shape=tuple(_hbm(a) for a in srcs + lands),
        in_specs=[HBM_SPEC] * (ns + nl) + [SEM_SPEC, SEM_SPEC] + [ANY_SPEC] * len(after),
        out_specs=tuple([HBM_SPEC] * (ns + nl)),
        input_output_aliases={i: i for i in range(ns + nl)},
        compiler_params=pltpu.CompilerParams(has_side_effects=SIDE_EFFECT),
    )(*srcs, *lands, send_sems, recv_sems, *after)
    return list(outs[:ns]), list(outs[ns:])


def _share_with_sibling(name, srcs, lands):
    n = len(srcs)

    def body(*refs):
        src_refs, land_refs, out_refs = refs[:n], refs[n:2 * n], refs[2 * n:3 * n]
        send_sem, recv_sem = refs[3 * n:]
        x, y, c, chips = _place()
        me = 2 * x + y
        sib = (x, y, 1 - c)
        sends, recvs = [], []
        for k in range(n):
            sems = (send_sem.at[4 * k], recv_sem.at[4 * k])
            sends.append(_remote(src_refs[k], out_refs[k].at[me], *sems, sib))
            recvs.append(_remote(src_refs[k], out_refs[k].at[me], *sems, sib))
            for j, (px, py) in enumerate(chips):
                frm = 2 * px + py
                sems = (send_sem.at[4 * k + 1 + j], recv_sem.at[4 * k + 1 + j])
                sends.append(_remote(land_refs[k].at[frm, c], out_refs[k].at[frm, c], *sems, sib))
                recvs.append(_remote(land_refs[k].at[frm, c], out_refs[k].at[frm, 1 - c], *sems, sib))
        for cp in sends:
            cp.start()
        for cp in recvs:
            cp.wait_recv()
        for cp in sends:
            cp.wait_send()

    return pl.pallas_call(
        body, name=name, in_specs=[HBM_SPEC] * (2 * n), out_specs=[HBM_SPEC] * n,
        out_shape=[jax.ShapeDtypeStruct(a.shape, a.dtype) for a in lands],
        input_output_aliases={n + k: k for k in range(n)},
        scratch_shapes=[pltpu.SemaphoreType.DMA((4 * n,)), pltpu.SemaphoreType.DMA((4 * n,))],
    )(*srcs, *lands)


def _pack(arrs):
    flat = jnp.concatenate([a.reshape(-1).astype(F32) for a in arrs])
    n = flat.shape[0]
    rows = -(-n // PACK_WIDTH)
    rows = -(-rows // 8) * 8
    return jnp.pad(flat, (0, rows * PACK_WIDTH - n)).reshape(rows, PACK_WIDTH)


def _unpack(buf, shapes):
    flat = buf.reshape(-1)
    out, off = [], 0
    for shp in shapes:
        n = 1
        for s in shp:
            n *= s
        out.append(flat[off:off + n].reshape(shp))
        off += n
    return out


def _unshard_cols(stacked):
    moved = jnp.moveaxis(stacked, 0, -2)
    return moved.reshape(moved.shape[:-2] + (moved.shape[-2] * moved.shape[-1],))


def _take_cols(blocks, start, width):
    bw = blocks.shape[2]
    pieces, lo = [], start
    while lo < start + width:
        b = lo // bw
        hi = min(start + width, (b + 1) * bw)
        pieces.append(blocks[b][:, lo - b * bw:hi - b * bw])
        lo = hi
    return jnp.concatenate(pieces, axis=1)


def _col_shard(full, s, width):
    return lax.dynamic_slice_in_dim(full, s * width, width, axis=full.ndim - 1)


def kernel(x, meta_tokens, mix_norm_g, ffn_norm_g, ffn_w1, ffn_w2, cp_w_in, cp_conv_w, cp_conv_b, cp_ln_g, cp_ln_b, cp_pool_w, cp_pool_scale, cp_w_out, gla_w_in, gla_gate_w2, gla_gate_b, gla_head_g, gla_w_out, final_norm_g, loss_target, m_meta_tokens, m_mix_norm_g, m_ffn_norm_g, m_ffn_w1, m_ffn_w2, m_cp_w_in, m_cp_conv_w, m_cp_conv_b, m_cp_ln_g, m_cp_ln_b, m_cp_pool_w, m_cp_pool_scale, m_cp_w_out, m_gla_w_in, m_gla_gate_w2, m_gla_gate_b, m_gla_head_g, m_gla_w_out, m_final_norm_g, v_meta_tokens, v_mix_norm_g, v_ffn_norm_g, v_ffn_w1, v_ffn_w2, v_cp_w_in, v_cp_conv_w, v_cp_conv_b, v_cp_ln_g, v_cp_ln_b, v_cp_pool_w, v_cp_pool_scale, v_cp_w_out, v_gla_w_in, v_gla_gate_w2, v_gla_gate_b, v_gla_head_g, v_gla_w_out, v_final_norm_g):
    d = D_MODEL
    chip = 2 * lax.axis_index("x") + lax.axis_index("y")
    core = lax.axis_index("c")
    seq = x.shape[1]
    t = seq + CHUNK

    sharded_small = [meta_tokens, cp_conv_w, gla_gate_w2, gla_gate_b, gla_head_g]

    def halves(w):
        return w.astype(BF16).reshape(2, w.shape[0] // 2, w.shape[1])

    def unhalve(g):
        return g.reshape(N_CHIPS, 2 * g.shape[2], g.shape[3])

    def gather_group(srcs, whole=()):
        lands = [lax.empty((N_CHIPS,) + s.shape, s.dtype) for s in srcs]
        for a in whole:
            lands.append(lax.dynamic_update_slice(jnp.zeros((N_CHIPS,) + a.shape, a.dtype), a[None], (chip,) + (0,) * a.ndim))
        return list(srcs) + list(whole), lands, _plan_gather(len(srcs)), len(srcs)

    groups = [gather_group([halves(cp_w_in[0]), halves(cp_w_out[0])], [_pack(sharded_small)]),
              gather_group([halves(ffn_w1[0]), halves(ffn_w2[0])]), gather_group([halves(gla_w_in[0]), halves(gla_w_out[0])]),
              gather_group([halves(ffn_w1[1]), halves(ffn_w2[1])])]
    bounds, all_srcs, all_lands = [], [], []
    for srcs, lands, _, _ in groups:
        bounds.append((len(all_srcs), len(all_srcs) + len(srcs)))
        all_srcs += srcs
        all_lands += lands

    def plan_all(src_refs, land_refs):
        return [cp for (lo, hi), group in zip(bounds, groups) for cp in group[2](src_refs[lo:hi], land_refs[lo:hi])]

    gathers = _start_copies("gather_start", all_srcs, all_lands, plan_all, 3 * len(all_srcs))

    def arrived(name, gi, after):
        (lo, hi), plan, n = bounds[gi], groups[gi][2], groups[gi][3]
        mine = (gathers[0], gathers[1], gathers[2][lo:hi], gathers[3][lo:hi], gathers[4])
        srcs, lands = _wait_copies(name + "_wait", mine, plan, after, sem_offset=3 * lo)
        return srcs[:n], lands[:n], lands[n:]

    cp_gather, ffn0_gather, gla_gather, ffn1_gather = 0, 1, 2, 3
    h0_rows = jnp.concatenate([jnp.zeros((CHUNK, d), F32) + gathers[4][0, 0], x[0]], axis=0)
    cp_srcs, cp_lands, (small_g,) = arrived("gather_cp", cp_gather, h0_rows)
    cpin_g, cpout_g = [unhalve(g) for g in _share_with_sibling("gather_cp_share", cp_srcs, cp_lands)]
    per_chip = [_unpack(small_g[j], [a.shape for a in sharded_small]) for j in range(N_CHIPS)]
    meta_f, conv_w_f, gate_w_f, gate_b_f, head_g_f = [
        jnp.concatenate([per_chip[j][i] for j in range(N_CHIPS)], axis=-1) for i in range(len(sharded_small))]
    conv_w_f, gate_w_f = conv_w_f[0], gate_w_f[0]
    w_cp_in = _unshard_cols(cpin_g)
    w_cp_out = cpout_g.reshape(CONV_DIM + POOL_DIM, d)
    gate_w_pad = jnp.pad(gate_w_f, ((0, GATE_PAD - GATE_RANK), (0, 0))).astype(BF16)
    row = lambda a: a.reshape(1, -1)
    c_idx = core.reshape(1).astype(jnp.int32)

    h0 = lax.dynamic_update_slice(h0_rows, meta_f, (PAD_ROWS, 0))
    z0, u0 = _norm_matmul(h0, row(mix_norm_g[0]), w_cp_in, 512, "cp_in_proj")
    c0, pm0, mix0 = _cp_seq_fwd(z0, conv_w_f, cp_conv_b, cp_ln_g, cp_ln_b, cp_pool_w[0], cp_pool_scale)
    ffn0_srcs, ffn0_lands, _ = arrived("gather_ffn0", ffn0_gather, mix0)
    ffn0_share = _start_copies("gather_ffn0_share_start", ffn0_srcs, ffn0_lands, _plan_share, 4 * len(ffn0_srcs))
    h1 = _matmul_residual(mix0, w_cp_out, h0, "cp_out_proj", dep=ffn0_share[-1])
    w1g0, w2g0 = [unhalve(g) for g in _wait_copies("gather_ffn0_share_wait", ffn0_share, _plan_share, h1)[1]]
    h2, hp0, uf0 = _ffn_fwd(h1, row(ffn_norm_g[0]), w1g0, w2g0, "ffn0_fwd")
    gla_srcs, gla_lands, _ = arrived("gather_gla", gla_gather, h2)
    glain_g, glaout_g = [unhalve(g) for g in _share_with_sibling("gather_gla_share", gla_srcs, gla_lands)]
    w_gla_in = jnp.concatenate([glain_g[j] for j in range(N_CHIPS)] + [jnp.zeros((d, GLA_IN_PAD - GLA_IN), BF16)], axis=1)
    w_gla_out = glaout_g.reshape(GLA_DV, d)
    z1, u2 = _norm_matmul(h2, row(mix_norm_g[1]), w_gla_in, GLA_COLS, "gla_in_proj")
    ffn1_srcs, ffn1_lands, _ = arrived("gather_ffn1", ffn1_gather, z1)
    ffn1_share = _start_copies("gather_ffn1_share_start", ffn1_srcs, ffn1_lands, _plan_share, 4 * len(ffn1_srcs))
    o1, mix1, states = _gla_seq_fwd(z1, gate_w_pad, gate_b_f, head_g_f, dep=ffn1_share[-1])
    h3 = _matmul_residual(mix1, w_gla_out, h2, "gla_out_proj")
    w1g1, w2g1 = [unhalve(g) for g in _wait_copies("gather_ffn1_share_wait", ffn1_share, _plan_share, h3)[1]]
    h4, hp1, uf1 = _ffn_fwd(h3, row(ffn_norm_g[1]), w1g1, w2g1, "ffn1_fwd")

    dev_idx = (2 * chip + core).reshape(1).astype(jnp.int32)

    def start_reduce(name, grads):
        srcs = [_split_rows(g) for g in grads]
        lands = [lax.empty((N_DEVICES,) + s.shape[2:], s.dtype) for s in srcs]
        return _start_copies(name + "_scatter_start", srcs, lands, _plan_scatter_all, len(OTHER_DEVICES) * len(srcs))

    def finish_reduce(name, started, after):
        srcs, lands = _wait_copies(name + "_scatter_wait", started, _plan_scatter_all, after)
        return [_sum_own_and_slots(s, l, dev_idx, "%s_slot_sum_%d" % (name, k)) for k, (s, l) in enumerate(zip(srcs, lands))]

    dh4, d_final_g, loss_part = _loss_bwd(h4, row(final_norm_g), loss_target[0])

    dh3, dhp1, d_ffn_g1 = _ffn_bwd_data(dh4, h3, row(ffn_norm_g[1]), hp1, w1g1, w2g1, "ffn1_bwd")
    dw1_1 = _wgrad(uf1, dhp1, N_CHIPS, d, d, False, True, False, "ffn1_dw1", rows=WGRAD_ROWS_BF16)
    dw2_1 = _wgrad(hp1, dh4, N_CHIPS, d, d, True, False, True, "ffn1_dw2")
    ffn1_reduce = start_reduce("ffn1", [dw1_1, dw2_1])

    dmix1 = _dgrad(dh3, w_gla_out, "gla_out_dgrad", dep=ffn1_reduce[-1])
    dw_gla_out = _wgrad(mix1, dh3, 1, GLA_DV, d, False, False, False, "gla_out_dw")
    dz1, d_gate_w, d_gate_b, d_head_g = _gla_seq_bwd(dmix1, o1, z1, states, gate_w_pad, gate_b_f, head_g_f)
    dh2, d_mix_g1 = _dgrad_norm_bwd(dz1, w_gla_in, h2, row(mix_norm_g[1]), dh3, GLA_COLS, "gla_in_dgrad")
    dw_gla_in = _wgrad(u2, dz1, GLA_IN_PAD // 640, d, 640, False, True, False, "gla_in_dw", rows=WGRAD_ROWS_BF16)
    gla_in_shards = jnp.stack([_take_cols(dw_gla_in, j * (GLA_IN // N_CHIPS), GLA_IN // N_CHIPS) for j in range(N_CHIPS)])
    gla_reduce = start_reduce("gla", [gla_in_shards, dw_gla_out.reshape(N_CHIPS, -1, d)])

    dh1, dhp0, d_ffn_g0 = _ffn_bwd_data(dh2, h1, row(ffn_norm_g[0]), hp0, w1g0, w2g0, "ffn0_bwd", dep=gla_reduce[-1])
    dw1_0 = _wgrad(uf0, dhp0, N_CHIPS, d, d, False, True, False, "ffn0_dw1", rows=WGRAD_ROWS_BF16)
    dw2_0 = _wgrad(hp0, dh2, N_CHIPS, d, d, True, False, True, "ffn0_dw2")
    ffn0_reduce = start_reduce("ffn0", [dw1_0, dw2_0])

    dmix0 = _dgrad(dh1, w_cp_out, "cp_out_dgrad", dep=ffn0_reduce[-1])
    dw_cp_out = _wgrad(mix0, dh1, 1, CONV_DIM + POOL_DIM, d, False, False, False, "cp_out_dw")
    dz0, d_conv_w, d_cp_vec, d_pool_w = _cp_seq_bwd(dmix0, z0, c0, pm0, conv_w_f, cp_ln_g, cp_ln_b, cp_pool_w[0],
                                                    cp_pool_scale)
    grad_x, dh0_head, d_mix_g0 = _dgrad_norm_bwd_input(dz0, w_cp_in, h0, row(mix_norm_g[0]), dh1, 512, "cp_in_dgrad")
    grad_x = grad_x[None]
    dw_cp_in = _wgrad(u0, dz0, 1, d, CP_IN, False, False, False, "cp_in_dw")
    dw_cp_in = jnp.stack([_take_cols(dw_cp_in, j * (CP_IN // N_CHIPS), CP_IN // N_CHIPS) for j in range(N_CHIPS)])

    cp_reduce = start_reduce("cp", [dw_cp_in, dw_cp_out.reshape(N_CHIPS, -1, d)])
    small_full = [dh0_head[PAD_ROWS:CHUNK],jnp.concatenate([d_mix_g0, d_mix_g1], axis=0),
                  jnp.concatenate([d_ffn_g0, d_ffn_g1], axis=0), d_conv_w[:CONV_WIDTH][None],
                  d_cp_vec[0:1], d_cp_vec[1:2], d_cp_vec[2:3], d_pool_w[None], d_cp_vec[3:4],
                  d_gate_w[:GATE_RANK][None], d_gate_b, d_head_g, d_final_g[0], loss_part[0, 0:1]]
    small_mine = _pack(small_full)
    whole = _plan_exchange(0)
    small_exchange = _start_copies("small_exchange_start", [small_mine], [lax.empty(small_mine.shape, F32)], whole, 1,
                                   dep=cp_reduce[-1])
    red_ffn1 = finish_reduce("ffn1", ffn1_reduce, small_exchange[-1])
    red_gla = finish_reduce("gla", gla_reduce, small_exchange[-1])
    (small_sent,), (small_recv,) = _wait_copies("small_exchange_wait", small_exchange, whole, [red_ffn1[1], red_gla[1]])
    small_chip = _add2(small_sent, small_recv, "chip_sum_small")
    small_slots = lax.dynamic_update_slice(jnp.zeros((N_CHIPS,) + small_chip.shape, F32), small_chip[None], (chip, 0, 0))
    small_reduce = _start_copies("small_scatter_start", [small_chip], [small_slots], _plan_scatter(0), 3)

    big = {"w1": (ffn_w1, m_ffn_w1, v_ffn_w1), "w2": (ffn_w2, m_ffn_w2, v_ffn_w2),
           "cp_in": (cp_w_in, m_cp_w_in, v_cp_w_in), "cp_out": (cp_w_out, m_cp_w_out, v_cp_w_out),
           "gla_in": (gla_w_in, m_gla_w_in, v_gla_w_in), "gla_out": (gla_w_out, m_gla_w_out, v_gla_w_out)}
    other_idx = (1 - core).reshape(1).astype(jnp.int32)

    def adamw_by_halves(tag, reduced, dep=None):
        flat = [r for n in reduced for r in reduced[n]]
        join_plan = _plan_exchange(0)
        join = _start_copies(tag + "_join_start", flat, [lax.empty(r.shape, F32) for r in flat], join_plan, len(flat),
                             dep=dep)
        views = {n: [_split_rows(a) for a in big[n]] for n in reduced}
        own, k = {}, 0
        for n in reduced:
            mine = join[2][k:k + len(reduced[n])]
            k += len(reduced[n])
            own[n] = _adamw_half(views[n][0], mine, views[n][1], views[n][2], c_idx, None, "adamw_%s_own" % n)
        _, arrived_halves = _wait_copies(tag + "_join_wait", join, join_plan, [own[n][1] for n in reduced])
        outs, k = {}, 0
        for n in reduced:
            theirs = arrived_halves[k:k + len(reduced[n])]
            k += len(reduced[n])
            res = _adamw_half(views[n][0], theirs, views[n][1], views[n][2], other_idx, own[n], "adamw_%s_sibling" % n)
            outs[n] = [o.reshape(big[n][0].shape) for o in res]
        return outs

    big_out = adamw_by_halves("gla", {"gla_in": [red_gla[0]], "gla_out": [red_gla[1]]}, dep=small_reduce[-1])
    red_ffn0 = finish_reduce("ffn0", ffn0_reduce, big_out["gla_out"][1])
    big_out.update(adamw_by_halves("ffn", {"w1": [red_ffn0[0], red_ffn1[0]], "w2": [red_ffn0[1], red_ffn1[1]]}))
    red_cp = finish_reduce("cp", cp_reduce, big_out["w2"][1])
    _, (small_landed,) = _wait_copies("small_scatter_wait", small_reduce, _plan_scatter(0), big_out["w2"][1])
    small_red = _sum_slots(small_landed, "slot_sum_small")
    big_out.update(adamw_by_halves("cp", {"cp_in": [red_cp[0]], "cp_out": [red_cp[1]]}))

    (g_meta, g_mix, g_ffn, g_conv_w, g_conv_b, g_ln_g, g_ln_b, g_pool_w, g_pool_scale, g_gate_w, g_gate_b, g_head,
     g_final, loss_sum) = _unpack(small_red, [a.shape for a in small_full])
    g_meta = _col_shard(g_meta, chip, meta_tokens.shape[-1])
    g_conv_w = _col_shard(g_conv_w, chip, cp_conv_w.shape[-1])
    g_gate_w = _col_shard(g_gate_w, chip, gla_gate_w2.shape[-1])
    g_gate_b = _col_shard(g_gate_b, chip, gla_gate_b.shape[-1])
    g_head = _col_shard(g_head, chip, gla_head_g.shape[-1])
    small_w = [meta_tokens, mix_norm_g, ffn_norm_g, cp_conv_w, cp_conv_b, cp_ln_g, cp_ln_b, cp_pool_w, cp_pool_scale,
               gla_gate_w2, gla_gate_b, gla_head_g, final_norm_g]
    small_m = [m_meta_tokens, m_mix_norm_g, m_ffn_norm_g, m_cp_conv_w, m_cp_conv_b, m_cp_ln_g, m_cp_ln_b, m_cp_pool_w,
               m_cp_pool_scale, m_gla_gate_w2, m_gla_gate_b, m_gla_head_g, m_final_norm_g]
    small_v = [v_meta_tokens, v_mix_norm_g, v_ffn_norm_g, v_cp_conv_w, v_cp_conv_b, v_cp_ln_g, v_cp_ln_b, v_cp_pool_w,
               v_cp_pool_scale, v_gla_gate_w2, v_gla_gate_b, v_gla_head_g, v_final_norm_g]
    small_g = [g_meta, g_mix, g_ffn, g_conv_w, g_conv_b, g_ln_g, g_ln_b, g_pool_w, g_pool_scale, g_gate_w, g_gate_b,
               g_head, g_final]
    shapes = [w.shape for w in small_w]
    small_g = [g.reshape(s) for g, s in zip(small_g, shapes)]
    at_least_2d = lambda arrs: [a.reshape(1, -1) if a.ndim == 1 else a for a in arrs]
    s_delta, s_m, s_v = _adamw_many(at_least_2d(small_w), at_least_2d(small_g), at_least_2d(small_m), at_least_2d(small_v))
    s_delta, s_m, s_v = [[a.reshape(s) for a, s in zip(group, shapes)] for group in (s_delta, s_m, s_v)]

    order = ["meta", "mix", "ffn", "w1", "w2", "cp_in", "conv_w", "conv_b", "ln_g", "ln_b", "pool_w", "pool_scale",
             "cp_out", "gla_in", "gate_w", "gate_b", "head", "gla_out", "final"]
    small_names = ["meta", "mix", "ffn", "conv_w", "conv_b", "ln_g", "ln_b", "pool_w", "pool_scale", "gate_w", "gate_b",
                   "head", "final"]
    big_names = ["w1", "w2", "cp_in", "cp_out", "gla_in", "gla_out"]
    table = {n: (small_g[i], s_delta[i], s_m[i], s_v[i]) for i, n in enumerate(small_names)}
    table.update({n: tuple(big_out[n]) for n in big_names})
    loss = loss_sum.reshape(())
    return (loss, grad_x, *[table[n][0] for n in order], *[table[n][1] for n in order],
            *[table[n][2] for n in order], *[table[n][3] for n in order])
```

```python
import functools

import jax
import jax.numpy as jnp
from jax import lax
from jax.experimental import pallas as pl
from jax.experimental.pallas import tpu as pltpu

F32 = jnp.float32
BF16 = jnp.bfloat16

D_MODEL = 1024
N_META = 16
CHUNK = 64
PAD_ROWS = CHUNK - N_META
EPS = 1e-5
CONV_DIM = 512
CONV_WIDTH = 31
CONV_HALO = 32
POOL_DIM = 512
POOL_WINDOWS = (2, 4, 8, 16)
POOL_GROUP = 128
POOL_HALO = 16
CP_IN = 2 * CONV_DIM + POOL_DIM
GLA_HEADS = 4
GLA_DK = 512
GLA_DV = 1024
GLA_HK = GLA_DK // GLA_HEADS
GLA_HV = GLA_DV // GLA_HEADS
GATE_RANK = 16
GATE_PAD = 128
GATE_NORM = 16.0
GLA_IN = 2 * GLA_DK + 2 * GLA_DV + GATE_RANK
GLA_IN_PAD = 2 * GLA_DK + 2 * GLA_DV + GATE_PAD
GLA_COLS = 1280
N_CHIPS = 4
ADAM_LR = 0.001
ADAM_B1 = 0.9
ADAM_B2 = 0.999
ADAM_EPS = 1e-08
ADAM_WD = 0.01
ADAM_STEP = 10

VMEM_LIMIT_BYTES = 56 * 1024 * 1024
ROW_TILE_TARGET = 832
TOKEN_TILE_TARGET = 1040
PACK_WIDTH = 1024
MESH = pl.DeviceIdType.MESH
HBM_SPEC = pl.BlockSpec(memory_space=pltpu.HBM)
ANY_SPEC = pl.BlockSpec(memory_space=pl.ANY)
SEM_SPEC = pl.BlockSpec(memory_space=pltpu.SEMAPHORE)
SIDE_EFFECT = pltpu.SideEffectType.DATAFLOW_SIDE_EFFECTING


def _cparams(*sem):
    return pltpu.CompilerParams(dimension_semantics=sem, vmem_limit_bytes=VMEM_LIMIT_BYTES)


def _row_tile(t, target, mult):
    best = mult
    for cand in range(mult, min(t, target) + 1, mult):
        if t % cand == 0:
            best = cand
    assert t % best == 0, (t, best)
    return best


def _rms(h, g):
    return h * lax.rsqrt(jnp.mean(h * h, axis=-1, keepdims=True) + EPS) * g


def _rms_bwd(h, g, du):
    r = lax.rsqrt(jnp.mean(h * h, axis=-1, keepdims=True) + EPS)
    xhat = h * r
    dxh = du * g
    dh = r * (dxh - xhat * jnp.mean(dxh * xhat, axis=-1, keepdims=True))
    return dh, du * xhat


def _valid_rows(i, tm):
    row = i * tm + lax.broadcasted_iota(jnp.int32, (tm, 1), 0)
    return row >= PAD_ROWS


def _dot(a, b):
    return jnp.dot(a, b, preferred_element_type=F32)


def _dot_nt(a, b):
    return lax.dot_general(a, b, (((1,), (1,)), ((), ())), preferred_element_type=F32)


def _dot_tn(a, b):
    return lax.dot_general(a, b, (((0,), (0,)), ((), ())), preferred_element_type=F32)


def _accumulate(ref, val, first):
    @pl.when(first)
    def _():
        ref[...] = val

    @pl.when(jnp.logical_not(first))
    def _():
        ref[...] += val


def _call_after(dep, body, n_in, in_specs, args, **kw):
    if dep is None:
        return pl.pallas_call(body, in_specs=in_specs, **kw)(*args)

    def with_dep(*refs):
        body(*refs[:n_in], *refs[n_in + 1:])

    return pl.pallas_call(with_dep, in_specs=list(in_specs) + [ANY_SPEC], **kw)(*args, dep)


def _norm_matmul(h, g, w, nc, name, dep=None):
    t, d = h.shape
    n = w.shape[1]
    tm = _row_tile(t, TOKEN_TILE_TARGET, 16)

    def body(h_ref, g_ref, w_ref, z_ref, u_ref):
        u = _rms(h_ref[...], g_ref[...]).astype(BF16)
        u_ref[...] = u
        for n0 in range(0, n, nc):
            n1 = min(n0 + nc, n)
            z_ref[:, n0:n1] = _dot(u, w_ref[:, n0:n1]).astype(BF16)

    return _call_after(
        dep, body, 3,
        [pl.BlockSpec((tm, d), lambda i: (i, 0)), pl.BlockSpec((1, d), lambda i: (0, 0)),
         pl.BlockSpec((d, n), lambda i: (0, 0))], (h, g, w), grid=(t // tm,),
        out_specs=[pl.BlockSpec((tm, n), lambda i: (i, 0)), pl.BlockSpec((tm, d), lambda i: (i, 0))],
        out_shape=[jax.ShapeDtypeStruct((t, n), BF16), jax.ShapeDtypeStruct((t, d), BF16)],
        compiler_params=_cparams("parallel"), name=name)


def _matmul_residual(a, w, h, name, dep=None):
    t, k = a.shape
    d = w.shape[1]
    tm = _row_tile(t, TOKEN_TILE_TARGET, 16)

    def body(a_ref, w_ref, h_ref, o_ref):
        o_ref[...] = h_ref[...] + _dot(a_ref[...], w_ref[...])

    return _call_after(
        dep, body, 3,
        [pl.BlockSpec((tm, k), lambda i: (i, 0)), pl.BlockSpec((k, d), lambda i: (0, 0)),
         pl.BlockSpec((tm, d), lambda i: (i, 0))], (a, w, h), grid=(t // tm,),
        out_specs=pl.BlockSpec((tm, d), lambda i: (i, 0)),
        out_shape=jax.ShapeDtypeStruct((t, d), F32),
        compiler_params=_cparams("parallel"), name=name)


def _ffn_fwd(h, g, w1g, w2g, name):
    t, d = h.shape
    ns, ffs = w1g.shape[0], w1g.shape[2]
    tm = _row_tile(t, TOKEN_TILE_TARGET, 16)

    def body(h_ref, g_ref, w1_ref, w2_ref, ho_ref, hp_ref, u_ref, acc_ref):
        s = pl.program_id(1)

        @pl.when(s == 0)
        def _():
            u_ref[...] = _rms(h_ref[...], g_ref[...]).astype(BF16)

        hp = _dot(u_ref[...], w1_ref[...])
        hp_ref[...] = hp.astype(BF16)
        a = jnp.maximum(hp, 0.0)
        _accumulate(acc_ref, _dot((a * a).astype(BF16), w2_ref[...]), s == 0)

        @pl.when(s == ns - 1)
        def _():
            ho_ref[...] = h_ref[...] + acc_ref[...]

    return pl.pallas_call(
        body, grid=(t // tm, ns),
        in_specs=[pl.BlockSpec((tm, d), lambda i, s: (i, 0)), pl.BlockSpec((1, d), lambda i, s: (0, 0)),
                  pl.BlockSpec((None, d, ffs), lambda i, s: (s, 0, 0)),
                  pl.BlockSpec((None, ffs, d), lambda i, s: (s, 0, 0))],
        out_specs=[pl.BlockSpec((tm, d), lambda i, s: (i, 0)), pl.BlockSpec((tm, ffs), lambda i, s: (i, s)),
                   pl.BlockSpec((tm, d), lambda i, s: (i, 0))],
        out_shape=[jax.ShapeDtypeStruct((t, d), F32), jax.ShapeDtypeStruct((t, ns * ffs), BF16),
                   jax.ShapeDtypeStruct((t, d), BF16)],
        scratch_shapes=[pltpu.VMEM((tm, d), F32)],
        compiler_params=_cparams("parallel", "arbitrary"), name=name)(h, g, w1g, w2g)


def _ffn_bwd_data(dh, h, g, hp, w1g, w2g, name, dep=None):
    t, d = h.shape
    ns, ffs = w1g.shape[0], w1g.shape[2]
    tm = _row_tile(t, ROW_TILE_TARGET, CHUNK)

    def body(dh_ref, h_ref, g_ref, hp_ref, w1_ref, w2_ref, dhi_ref, dhp_ref, dg_ref, acc_ref):
        i, s = pl.program_id(0), pl.program_id(1)
        da = _dot_nt(dh_ref[...].astype(BF16), w2_ref[...])
        dhp = (da * (2.0 * jnp.maximum(hp_ref[...].astype(F32), 0.0))).astype(BF16)
        dhp_ref[...] = dhp
        _accumulate(acc_ref, _dot_nt(dhp, w1_ref[...]), s == 0)

        @pl.when(s == ns - 1)
        def _():
            dhn, dgr = _rms_bwd(h_ref[...], g_ref[...], acc_ref[...])
            dhi_ref[...] = jnp.where(_valid_rows(i, tm), dh_ref[...] + dhn, 0.0)
            _accumulate(dg_ref, jnp.sum(dgr, axis=0, keepdims=True), i == 0)

    return _call_after(
        dep, body, 6,
        [pl.BlockSpec((tm, d), lambda i, s: (i, 0)), pl.BlockSpec((tm, d), lambda i, s: (i, 0)),
         pl.BlockSpec((1, d), lambda i, s: (0, 0)), pl.BlockSpec((tm, ffs), lambda i, s: (i, s)),
         pl.BlockSpec((None, d, ffs), lambda i, s: (s, 0, 0)),
         pl.BlockSpec((None, ffs, d), lambda i, s: (s, 0, 0))], (dh, h, g, hp, w1g, w2g), grid=(t // tm, ns),
        out_specs=[pl.BlockSpec((tm, d), lambda i, s: (i, 0)), pl.BlockSpec((tm, ffs), lambda i, s: (i, s)),
                   pl.BlockSpec((1, d), lambda i, s: (0, 0))],
        out_shape=[jax.ShapeDtypeStruct((t, d), F32), jax.ShapeDtypeStruct((t, ns * ffs), BF16),
                   jax.ShapeDtypeStruct((1, d), F32)],
        scratch_shapes=[pltpu.VMEM((tm, d), F32)],
        compiler_params=_cparams("arbitrary", "arbitrary"), name=name)


WGRAD_ROWS = 2048
WGRAD_ROWS_BF16 = 4096


def _wgrad(x, dy, nb, xc, yc, x_by_block, dy_by_block, relu2, name, dep=None, rows=WGRAD_ROWS):
    t = x.shape[0]
    tk = _row_tile(t - CHUNK, rows, CHUNK)

    def prep(xv):
        if relu2:
            xv = jnp.maximum(xv.astype(F32), 0.0)
            xv = xv * xv
        return xv.astype(BF16)

    nk = (t - CHUNK) // tk

    def body(xh_ref, dyh_ref, x_ref, dy_ref, o_ref, acc_ref):
        k = pl.program_id(1)
        p = _dot_tn(prep(x_ref[...]), dy_ref[...].astype(BF16))

        @pl.when(k == 0)
        def _():
            acc_ref[...] = p + _dot_tn(prep(xh_ref[...]), dyh_ref[...].astype(BF16))

        @pl.when(k > 0)
        def _():
            acc_ref[...] += p

        @pl.when(k == nk - 1)
        def _():
            o_ref[...] = acc_ref[...].astype(BF16)

    def head(width, by_block):
        return pl.BlockSpec((CHUNK, width), (lambda b, k: (0, b)) if by_block else (lambda b, k: (0, 0)))

    def rest(width, by_block):
        def index(b, k):
            return pl.multiple_of(CHUNK + k * tk, CHUNK), (pl.multiple_of(b * width, 128) if by_block else 0)
        return pl.BlockSpec((pl.Element(tk), pl.Element(width)), index)

    return _call_after(
        dep, body, 4,
        [head(xc, x_by_block), head(yc, dy_by_block), rest(xc, x_by_block), rest(yc, dy_by_block)], (x, dy, x, dy),
        grid=(nb, nk),
        out_specs=pl.BlockSpec((None, xc, yc), lambda b, k: (b, 0, 0)),
        out_shape=jax.ShapeDtypeStruct((nb, xc, yc), BF16),
        scratch_shapes=[pltpu.VMEM((xc, yc), F32)],
        compiler_params=_cparams("parallel", "arbitrary"), name=name)


def _dgrad(dh, w, name, dep=None):
    t, d = dh.shape
    k = w.shape[0]
    tm = _row_tile(t, TOKEN_TILE_TARGET, 16)

    def body(dh_ref, w_ref, o_ref):
        o_ref[...] = _dot_nt(dh_ref[...].astype(BF16), w_ref[...]).astype(BF16)

    return _call_after(
        dep, body, 2,
        [pl.BlockSpec((tm, d), lambda i: (i, 0)), pl.BlockSpec((k, d), lambda i: (0, 0))], (dh, w), grid=(t // tm,),
        out_specs=pl.BlockSpec((tm, k), lambda i: (i, 0)),
        out_shape=jax.ShapeDtypeStruct((t, k), BF16),
        compiler_params=_cparams("parallel"), name=name)


def _dgrad_norm_bwd(dz, w, h, g, dh, nc, name):
    t, d = h.shape
    n = w.shape[1]
    tm = _row_tile(t, ROW_TILE_TARGET // 2, 16)

    def body(dz_ref, w_ref, h_ref, g_ref, dh_ref, dhi_ref, dg_ref):
        i = pl.program_id(0)
        du = jnp.zeros((tm, d), F32)
        for n0 in range(0, n, nc):
            n1 = min(n0 + nc, n)
            du = du + _dot_nt(dz_ref[:, n0:n1], w_ref[:, n0:n1])
        dhn, dgr = _rms_bwd(h_ref[...], g_ref[...], du)
        dhi_ref[...] = jnp.where(_valid_rows(i, tm), dh_ref[...] + dhn, 0.0)
        _accumulate(dg_ref, jnp.sum(dgr, axis=0, keepdims=True), i == 0)

    return pl.pallas_call(
        body, grid=(t // tm,),
        in_specs=[pl.BlockSpec((tm, n), lambda i: (i, 0)), pl.BlockSpec((d, n), lambda i: (0, 0)),
                  pl.BlockSpec((tm, d), lambda i: (i, 0)), pl.BlockSpec((1, d), lambda i: (0, 0)),
                  pl.BlockSpec((tm, d), lambda i: (i, 0))],
        out_specs=[pl.BlockSpec((tm, d), lambda i: (i, 0)), pl.BlockSpec((1, d), lambda i: (0, 0))],
        out_shape=[jax.ShapeDtypeStruct((t, d), F32), jax.ShapeDtypeStruct((1, d), F32)],
        compiler_params=_cparams("arbitrary"), name=name)(dz, w, h, g, dh)


def _dgrad_norm_bwd_input(dz, w, h, g, dh, nc, name):
    t, d = h.shape
    n = w.shape[1]
    tl = _row_tile(t - CHUNK, 512, CHUNK)

    def grads(dz_ref, w_ref, h_ref, g_ref, dh_ref, rows):
        du = jnp.zeros((rows, d), F32)
        for n0 in range(0, n, nc):
            n1 = min(n0 + nc, n)
            du = du + _dot_nt(dz_ref[:, n0:n1], w_ref[:, n0:n1])
        dhn, dgr = _rms_bwd(h_ref[...], g_ref[...], du)
        return dh_ref[...] + dhn, jnp.sum(dgr, axis=0, keepdims=True)

    def rest_body(dz_ref, w_ref, h_ref, g_ref, dh_ref, dg_head_ref, dx_ref, dg_ref):
        dx, dg = grads(dz_ref, w_ref, h_ref, g_ref, dh_ref, tl)
        dx_ref[...] = dx

        @pl.when(pl.program_id(0) == 0)
        def _():
            dg_ref[...] = dg_head_ref[...] + dg

        @pl.when(pl.program_id(0) > 0)
        def _():
            dg_ref[...] += dg

    def head_body(dz_ref, w_ref, h_ref, g_ref, dh_ref, dx_ref, dg_ref):
        dx, dg = grads(dz_ref, w_ref, h_ref, g_ref, dh_ref, CHUNK)
        dx_ref[...] = jnp.where(_valid_rows(0, CHUNK), dx, 0.0)
        dg_ref[...] = dg

    def shifted(width):
        return pl.BlockSpec((pl.Element(tl), pl.Element(width)), lambda i: (pl.multiple_of(CHUNK + i * tl, CHUNK), 0))

    whole = [pl.BlockSpec((d, n), lambda i: (0, 0)), pl.BlockSpec((1, d), lambda i: (0, 0))]
    head = lambda width: pl.BlockSpec((CHUNK, width), lambda i: (0, 0))
    dh_head, dg_head = pl.pallas_call(
        head_body, grid=(1,), in_specs=[head(n), whole[0], head(d), whole[1], head(d)],
        out_specs=[head(d), whole[1]],
        out_shape=[jax.ShapeDtypeStruct((CHUNK, d), F32), jax.ShapeDtypeStruct((1, d), F32)],
        compiler_params=_cparams("arbitrary"), name=name + "_head")(dz, w, h, g, dh)
    dx, dg = pl.pallas_call(
        rest_body, grid=((t - CHUNK) // tl,),
        in_specs=[shifted(n), whole[0], shifted(d), whole[1], shifted(d), whole[1]],
        out_specs=[pl.BlockSpec((tl, d), lambda i: (i, 0)), whole[1]],
        out_shape=[jax.ShapeDtypeStruct((t - CHUNK, d), F32), jax.ShapeDtypeStruct((1, d), F32)],
        compiler_params=_cparams("arbitrary"), name=name)(dz, w, h, g, dh, dg_head)
    return dx, dh_head, dg


def _loss_bwd(h, g, target):
    t, d = h.shape
    tl = _row_tile(t - CHUNK, 1024, CHUNK)

    def body(h_ref, g_ref, t_ref, dh_ref, dg_ref, loss_ref):
        i = pl.program_id(0)
        hv, gv = h_ref[...], g_ref[...]
        err = _rms(hv, gv) - t_ref[...]
        part = 0.5 * jnp.sum(jnp.mean(err * err, axis=-1, keepdims=True), axis=0, keepdims=True)
        dhn, dgr = _rms_bwd(hv, gv, err * (1.0 / d))
        dh_ref[...] = dhn
        _accumulate(dg_ref, jnp.sum(dgr, axis=0, keepdims=True), i == 0)
        _accumulate(loss_ref, jnp.broadcast_to(part, (8, 128)), i == 0)

    shifted = pl.BlockSpec((pl.Element(tl), pl.Element(d)), lambda i: (pl.multiple_of(CHUNK + i * tl, CHUNK), 0))
    dh, dg, loss = pl.pallas_call(
        body, grid=((t - CHUNK) // tl,),
        in_specs=[shifted, pl.BlockSpec((1, d), lambda i: (0, 0)), pl.BlockSpec((tl, d), lambda i: (i, 0))],
        out_specs=[shifted, pl.BlockSpec((1, d), lambda i: (0, 0)), pl.BlockSpec((8, 128), lambda i: (0, 0))],
        out_shape=[jax.ShapeDtypeStruct((t, d), F32), jax.ShapeDtypeStruct((1, d), F32),
                   jax.ShapeDtypeStruct((8, 128), F32)],
        compiler_params=_cparams("arbitrary"), name="loss_bwd")(h, g, target)

    def zero_head(dh_ref, o_ref):
        o_ref[...] = jnp.zeros_like(o_ref)

    dh = pl.pallas_call(
        zero_head, grid=(1,), in_specs=[ANY_SPEC], out_specs=pl.BlockSpec((CHUNK, d), lambda i: (0, 0)),
        out_shape=jax.ShapeDtypeStruct((t, d), F32), input_output_aliases={0: 0}, name="loss_bwd_head")(dh)
    return dh, dg, loss


CONV_BLOCK = 32


def _silu(x):
    return x * jax.nn.sigmoid(x)


def _row_shifts(win):
    n = win.shape[0]
    return [win] + [pltpu.roll(win, n - j, 0) for j in range(1, 8)]


def _cp_seq_fwd(z, conv_w, conv_b, ln_g, ln_b, pool_w, pool_scale):
    t = z.shape[0]
    tm = _row_tile(t, ROW_TILE_TARGET, CHUNK)

    def body(z_ref, cw_ref, cb_ref, lg_ref, lb_ref, pw_ref, ps_ref, c_ref, pm_ref, mix_ref, gbuf, pbuf):
        i = pl.program_id(0)

        @pl.when(i == 0)
        def _():
            gbuf[0:CONV_HALO, :] = jnp.zeros((CONV_HALO, CONV_DIM), F32)
            pbuf[0:POOL_HALO, :] = jnp.zeros((POOL_HALO, POOL_DIM), F32)

        @pl.when(i > 0)
        def _():
            gbuf[0:CONV_HALO, :] = gbuf[tm:tm + CONV_HALO, :]
            pbuf[0:POOL_HALO, :] = pbuf[tm:tm + POOL_HALO, :]

        av = z_ref[:, 0:CONV_DIM].astype(F32)
        ag = z_ref[:, CONV_DIM:2 * CONV_DIM].astype(F32)
        gbuf[CONV_HALO:CONV_HALO + tm, :] = av * jax.nn.sigmoid(ag)
        pbuf[POOL_HALO:POOL_HALO + tm, :] = z_ref[:, 2 * CONV_DIM:CP_IN].astype(F32)

        def conv_block(rb, carry):
            base = pl.multiple_of(rb * CONV_BLOCK, CONV_BLOCK)
            shifted = _row_shifts(gbuf[pl.ds(base, CONV_BLOCK + CONV_HALO), :])
            acc = jnp.zeros((CONV_BLOCK, CONV_DIM), F32)
            for k in range(CONV_WIDTH):
                whole, part = divmod(CONV_HALO - (CONV_WIDTH - 1) + k, 8)
                acc = acc + cw_ref[k:k + 1, :] * shifted[part][8 * whole:8 * whole + CONV_BLOCK, :]
            c_ref[pl.ds(base, CONV_BLOCK), :] = acc + cb_ref[...]
            return carry

        lax.fori_loop(0, tm // CONV_BLOCK, conv_block, 0)

        c = c_ref[...]
        mu = jnp.mean(c, axis=-1, keepdims=True)
        xc = c - mu
        ln = xc * lax.rsqrt(jnp.mean(xc * xc, axis=-1, keepdims=True) + EPS) * lg_ref[...] + lb_ref[...]
        row = i * tm + lax.broadcasted_iota(jnp.int32, (tm, 1), 0)
        mix_ref[:, 0:CONV_DIM] = jnp.where(row >= PAD_ROWS, _silu(ln), 0.0).astype(BF16)

        tpos = (row - PAD_ROWS + 1).astype(F32)
        for gi, wdw in enumerate(POOL_WINDOWS):
            lo = POOL_GROUP * gi
            run, step = pbuf[:, lo:lo + POOL_GROUP], 1
            cur = run[POOL_HALO:POOL_HALO + tm, :]
            while step < wdw:
                run = run + pltpu.roll(run, step, 0)
                step *= 2
            pm = (run[POOL_HALO:POOL_HALO + tm, :] / jnp.clip(tpos, 1.0, float(wdw)) - cur).astype(BF16)
            pm_ref[:, lo:lo + POOL_GROUP] = pm
            pg = _dot(pm, pw_ref[gi].astype(BF16))
            mix_ref[:, CONV_DIM + lo:CONV_DIM + lo + POOL_GROUP] = (pg * ps_ref[:, lo:lo + POOL_GROUP]).astype(BF16)

    vec = pl.BlockSpec((1, CONV_DIM), lambda i: (0, 0))
    return pl.pallas_call(
        body, grid=(t // tm,),
        in_specs=[pl.BlockSpec((tm, CP_IN), lambda i: (i, 0)),
                  pl.BlockSpec((CONV_WIDTH, CONV_DIM), lambda i: (0, 0)), vec, vec, vec,
                  pl.BlockSpec((len(POOL_WINDOWS), POOL_GROUP, POOL_GROUP), lambda i: (0, 0, 0)), vec],
        out_specs=[pl.BlockSpec((tm, CONV_DIM), lambda i: (i, 0)), pl.BlockSpec((tm, POOL_DIM), lambda i: (i, 0)),
                   pl.BlockSpec((tm, CONV_DIM + POOL_DIM), lambda i: (i, 0))],
        out_shape=[jax.ShapeDtypeStruct((t, CONV_DIM), F32), jax.ShapeDtypeStruct((t, POOL_DIM), BF16),
                   jax.ShapeDtypeStruct((t, CONV_DIM + POOL_DIM), BF16)],
        scratch_shapes=[pltpu.VMEM((tm + CONV_HALO, CONV_DIM), F32), pltpu.VMEM((tm + POOL_HALO, POOL_DIM), F32)],
        compiler_params=_cparams("arbitrary"), name="cp_seq_fwd")(z, conv_w, conv_b, ln_g, ln_b, pool_w, pool_scale)


def _cp_seq_bwd(dmix, z, c, pm, conv_w, ln_g, ln_b, pool_w, pool_scale, dep=None):
    t = z.shape[0]
    tm = _row_tile(t, ROW_TILE_TARGET, CHUNK)
    nt = t // tm

    def body(dmix_ref, z_ref, c_ref, pm_ref, cw_ref, lg_ref, lb_ref, pw_ref, ps_ref,
             dz_ref, dcw_ref, dvec_ref, dpw_ref, dcbuf, qbuf, glu_buf, dwacc):
        i = pl.program_id(0)
        tile = nt - 1 - i

        @pl.when(i == 0)
        def _():
            dcbuf[tm:tm + CONV_HALO, :] = jnp.zeros((CONV_HALO, CONV_DIM), F32)
            qbuf[tm:tm + POOL_HALO, :] = jnp.zeros((POOL_HALO, POOL_DIM), F32)
            dcw_ref[...] = jnp.zeros_like(dcw_ref)
            dwacc[...] = jnp.zeros_like(dwacc)
            dvec_ref[...] = jnp.zeros_like(dvec_ref)
            dpw_ref[...] = jnp.zeros_like(dpw_ref)

        @pl.when(i > 0)
        def _():
            dcbuf[tm:tm + CONV_HALO, :] = dcbuf[0:CONV_HALO, :]
            qbuf[tm:tm + POOL_HALO, :] = qbuf[0:POOL_HALO, :]

        row = tile * tm + lax.broadcasted_iota(jnp.int32, (tm, 1), 0)
        cv = c_ref[...]
        mu = jnp.mean(cv, axis=-1, keepdims=True)
        xc = cv - mu
        rstd = lax.rsqrt(jnp.mean(xc * xc, axis=-1, keepdims=True) + EPS)
        xhat = xc * rstd
        ln = xhat * lg_ref[...] + lb_ref[...]
        sg = jax.nn.sigmoid(ln)
        da = jnp.where(row >= PAD_ROWS, dmix_ref[:, 0:CONV_DIM].astype(F32), 0.0)
        dln = da * (sg * (1.0 + ln * (1.0 - sg)))
        dxh = dln * lg_ref[...]
        dc = rstd * (dxh - jnp.mean(dxh, axis=-1, keepdims=True) - xhat * jnp.mean(dxh * xhat, axis=-1, keepdims=True))
        dcbuf[0:tm, :] = dc
        dvec_ref[0:1, :] += jnp.sum(dc, axis=0, keepdims=True)
        dvec_ref[1:2, :] += jnp.sum(dln * xhat, axis=0, keepdims=True)
        dvec_ref[2:3, :] += jnp.sum(dln, axis=0, keepdims=True)

        av = z_ref[:, 0:CONV_DIM].astype(F32)
        sig_g = jax.nn.sigmoid(z_ref[:, CONV_DIM:2 * CONV_DIM].astype(F32))
        glu_buf[...] = av * sig_g

        def conv_block(rb, carry):
            base = pl.multiple_of(rb * CONV_BLOCK, CONV_BLOCK)
            shifted = _row_shifts(dcbuf[pl.ds(base, CONV_BLOCK + CONV_HALO), :])
            glu = glu_buf[pl.ds(base, CONV_BLOCK), :]
            acc = jnp.zeros((CONV_BLOCK, CONV_DIM), F32)
            for k in range(CONV_WIDTH):
                whole, part = divmod(CONV_WIDTH - 1 - k, 8)
                slab = shifted[part][8 * whole:8 * whole + CONV_BLOCK, :]
                acc = acc + cw_ref[k:k + 1, :] * slab
                prod = slab * glu
                part = prod[0:8]
                for q in range(1, CONV_BLOCK // 8):
                    part = part + prod[8 * q:8 * q + 8]
                dwacc[k] += part
            glu_buf[pl.ds(base, CONV_BLOCK), :] = acc
            return carry

        lax.fori_loop(0, tm // CONV_BLOCK, conv_block, 0)

        @pl.when(i == nt - 1)
        def _():
            for k in range(CONV_WIDTH):
                dcw_ref[k:k + 1, :] = jnp.sum(dwacc[k], axis=0, keepdims=True)
        dglu = glu_buf[...]
        dz_ref[:, 0:CONV_DIM] = (dglu * sig_g).astype(BF16)
        dz_ref[:, CONV_DIM:2 * CONV_DIM] = (dglu * av * sig_g * (1.0 - sig_g)).astype(BF16)

        tpos = (row - PAD_ROWS + 1).astype(F32)
        for gi, wdw in enumerate(POOL_WINDOWS):
            lo = POOL_GROUP * gi
            dp = dmix_ref[:, CONV_DIM + lo:CONV_DIM + lo + POOL_GROUP].astype(F32)
            pmv = pm_ref[:, lo:lo + POOL_GROUP]
            pwb = pw_ref[gi].astype(BF16)
            dvec_ref[3:4, lo:lo + POOL_GROUP] += jnp.sum(dp * _dot(pmv, pwb), axis=0, keepdims=True)
            dq = (dp * ps_ref[:, lo:lo + POOL_GROUP]).astype(BF16)
            dpw_ref[gi] += _dot_tn(pmv, dq)
            dpm = _dot_nt(dq, pwb)
            qbuf[0:tm, lo:lo + POOL_GROUP] = dpm / jnp.clip(tpos, 1.0, float(wdw))
            run, step = qbuf[:, lo:lo + POOL_GROUP], 1
            while step < wdw:
                run = run + pltpu.roll(run, tm + POOL_HALO - step, 0)
                step *= 2
            dz_ref[:, 2 * CONV_DIM + lo:2 * CONV_DIM + lo + POOL_GROUP] = (run[0:tm, :] - dpm).astype(BF16)

    vec = pl.BlockSpec((1, CONV_DIM), lambda i: (0, 0))
    rev = lambda i: (nt - 1 - i, 0)
    return _call_after(
        dep, body, 9,
        [pl.BlockSpec((tm, CONV_DIM + POOL_DIM), rev), pl.BlockSpec((tm, CP_IN), rev),
         pl.BlockSpec((tm, CONV_DIM), rev), pl.BlockSpec((tm, POOL_DIM), rev),
         pl.BlockSpec((CONV_WIDTH, CONV_DIM), lambda i: (0, 0)), vec, vec,
         pl.BlockSpec((len(POOL_WINDOWS), POOL_GROUP, POOL_GROUP), lambda i: (0, 0, 0)), vec],
        (dmix, z, c, pm, conv_w, ln_g, ln_b, pool_w, pool_scale), grid=(nt,),
        out_specs=[pl.BlockSpec((tm, CP_IN), rev), pl.BlockSpec((CONV_WIDTH + 1, CONV_DIM), lambda i: (0, 0)),
                   pl.BlockSpec((8, CONV_DIM), lambda i: (0, 0)),
                   pl.BlockSpec((len(POOL_WINDOWS), POOL_GROUP, POOL_GROUP), lambda i: (0, 0, 0))],
        out_shape=[jax.ShapeDtypeStruct((t, CP_IN), BF16), jax.ShapeDtypeStruct((CONV_WIDTH + 1, CONV_DIM), F32),
                   jax.ShapeDtypeStruct((8, CONV_DIM), F32),
                   jax.ShapeDtypeStruct((len(POOL_WINDOWS), POOL_GROUP, POOL_GROUP), F32)],
        scratch_shapes=[pltpu.VMEM((tm + CONV_HALO, CONV_DIM), F32), pltpu.VMEM((tm + POOL_HALO, POOL_DIM), F32),
                        pltpu.VMEM((tm, CONV_DIM), F32), pltpu.VMEM((CONV_WIDTH + 1, 8, CONV_DIM), F32)],
        compiler_params=_cparams("arbitrary"), name="cp_seq_bwd")


Q0, K0, V0, G0, R0 =0, GLA_DK, 2 * GLA_DK, 2 * GLA_DK + GLA_DV, 2 * GLA_DK + 2 * GLA_DV


def _split3(x):
    hi = x.astype(BF16)
    r1 = x - hi.astype(F32)
    mid = r1.astype(BF16)
    lo = (r1 - mid.astype(F32)).astype(BF16)
    return hi, mid, lo


def _tri(strict):
    r = lax.broadcasted_iota(jnp.int32, (CHUNK, CHUNK), 0)
    c = lax.broadcasted_iota(jnp.int32, (CHUNK, CHUNK), 1)
    return ((r > c) if strict else (r >= c)).astype(BF16)


def _chunk_sums(x, cpt, strict, pieces):
    tri3 = jnp.broadcast_to(_tri(strict)[None], (cpt, CHUNK, CHUNK))
    acc = None
    for piece in _split3(x.reshape(cpt, CHUNK, x.shape[-1]))[:pieces]:
        part = jnp.einsum("bij,bjk->bik", tri3, piece, preferred_element_type=F32)
        acc = part if acc is None else acc + part
    return acc


def _chunk_decay(r, gw_ref, gb_ref, cpt):
    pre = _dot(r, gw_ref[...]) + gb_ref[...]
    lac = (jnp.minimum(pre, 0.0) - jnp.log(1.0 + jnp.exp(-jnp.abs(pre)))) * (1.0 / GATE_NORM)
    cum3 = _chunk_sums(lac, cpt, False, 3)
    return cum3, cum3[:, CHUNK - 1:CHUNK, :]


def _gla_seq_fwd(z, gate_w, gate_b, head_g, dep=None):
    t = z.shape[0]
    tm = _row_tile(t, ROW_TILE_TARGET, CHUNK)
    cpt = tm // CHUNK
    scale = GLA_HK ** -0.5

    def body(z_ref, gw_ref, gb_ref, hg_ref, o_ref, mix_ref, st_ref, state, kdec_s, e_s):
        @pl.when(pl.program_id(0) == 0)
        def _():
            state[...] = jnp.zeros_like(state)

        cum3, tot3 = _chunk_decay(z_ref[:, R0:R0 + GATE_PAD], gw_ref, gb_ref, cpt)
        dec = jnp.exp(jnp.broadcast_to(tot3, cum3.shape) - cum3).reshape(tm, GLA_DK)
        kdec_s[...] = (z_ref[:, K0:K0 + GLA_DK].astype(F32) * dec).astype(BF16)
        e_s[...] = jnp.exp(jnp.broadcast_to(tot3, (cpt, 8, GLA_DK))).reshape(cpt * 8, GLA_DK)

        def chunk(ci, carry):
            rows = pl.ds(pl.multiple_of(ci * CHUNK, CHUNK), CHUNK)
            e_all = e_s[pl.ds(pl.multiple_of(ci * 8, 8), 8), :][0:1, :]
            st_ref[ci] = state[...].astype(BF16)
            for hd in range(GLA_HEADS):
                ks = slice(hd * GLA_HK, (hd + 1) * GLA_HK)
                vs = slice(hd * GLA_HV, (hd + 1) * GLA_HV)
                v = z_ref[rows, V0 + hd * GLA_HV:V0 + (hd + 1) * GLA_HV]
                st = state[vs, :] * e_all[:, ks] + _dot_tn(v, kdec_s[rows, ks])
                state[vs, :] = st
                q = z_ref[rows, Q0 + hd * GLA_HK:Q0 + (hd + 1) * GLA_HK]
                o_ref[rows, vs] = (_dot_nt(q, st.astype(BF16)) * scale).astype(BF16)
            return carry

        lax.fori_loop(0, cpt, chunk, 0, unroll=cpt)

        for hd in range(GLA_HEADS):
            vs = slice(hd * GLA_HV, (hd + 1) * GLA_HV)
            on = _rms(o_ref[:, vs].astype(F32), hg_ref[...])
            gv = z_ref[:, G0 + hd * GLA_HV:G0 + (hd + 1) * GLA_HV].astype(F32)
            mix_ref[:, vs] = (on * _silu(gv)).astype(BF16)

    return _call_after(
        dep, body, 4,
        [pl.BlockSpec((tm, GLA_IN_PAD), lambda i: (i, 0)),
         pl.BlockSpec((GATE_PAD, GLA_DK), lambda i: (0, 0)), pl.BlockSpec((1, GLA_DK), lambda i: (0, 0)),
         pl.BlockSpec((1, GLA_HV), lambda i: (0, 0))], (z, gate_w, gate_b, head_g), grid=(t // tm,),
        out_specs=[pl.BlockSpec((tm, GLA_DV), lambda i: (i, 0)), pl.BlockSpec((tm, GLA_DV), lambda i: (i, 0)),
                   pl.BlockSpec((cpt, GLA_DV, GLA_HK), lambda i: (i, 0, 0))],
        out_shape=[jax.ShapeDtypeStruct((t, GLA_DV), BF16), jax.ShapeDtypeStruct((t, GLA_DV), BF16),
                   jax.ShapeDtypeStruct((t // CHUNK, GLA_DV, GLA_HK), BF16)],
        scratch_shapes=[pltpu.VMEM((GLA_DV, GLA_HK), F32), pltpu.VMEM((tm, GLA_DK), BF16),
                        pltpu.VMEM((cpt * 8, GLA_DK), F32)],
        compiler_params=_cparams("arbitrary"), name="gla_seq_fwd")


def _gla_seq_bwd(dmix, o, z, states, gate_w, gate_b, head_g, dep=None):
    t = z.shape[0]
    tm = _row_tile(t, ROW_TILE_TARGET, CHUNK)
    cpt = tm // CHUNK
    nt = t // tm
    scale = GLA_HK ** -0.5

    def body(dmix_ref, o_ref, z_ref, st_ref, gw_ref, gb_ref, hg_ref, dz_ref, dgw_ref, dgb_ref, dhg_ref,
             dstate, dec_s, kdec_s, dkdec_s, do_s, e_s, dtot_s):
        @pl.when(pl.program_id(0) == 0)
        def _():
            dstate[...] = jnp.zeros_like(dstate)
            dgw_ref[...] = jnp.zeros_like(dgw_ref)
            dgb_ref[...] = jnp.zeros_like(dgb_ref)
            dhg_ref[...] = jnp.zeros_like(dhg_ref)

        cum3, tot3 = _chunk_decay(z_ref[:, R0:R0 + GATE_PAD], gw_ref, gb_ref, cpt)
        dec = jnp.exp(jnp.broadcast_to(tot3, cum3.shape) - cum3).reshape(tm, GLA_DK)
        dec_s[...] = dec
        kdec = z_ref[:, K0:K0 + GLA_DK].astype(F32) * dec
        kdec_s[...] = kdec
        e3 = jnp.exp(tot3)
        e_s[...] = jnp.broadcast_to(e3, (cpt, 8, GLA_DK)).reshape(cpt * 8, GLA_DK)
        dhg = jnp.zeros((1, GLA_HV), F32)
        for hd in range(GLA_HEADS):
            ks = slice(hd * GLA_HK, (hd + 1) * GLA_HK)
            vs = slice(hd * GLA_HV, (hd + 1) * GLA_HV)
            gcols = slice(G0 + hd * GLA_HV, G0 + (hd + 1) * GLA_HV)
            ov = o_ref[:, vs].astype(F32)
            gv = z_ref[:, gcols].astype(F32)
            dm = dmix_ref[:, vs].astype(F32)
            sg = jax.nn.sigmoid(gv)
            rr = lax.rsqrt(jnp.mean(ov * ov, axis=-1, keepdims=True) + EPS)
            xhat = ov * rr
            don = dm * (gv * sg)
            dz_ref[:, gcols] = (dm * (xhat * hg_ref[...]) * (sg * (1.0 + gv * (1.0 - sg)))).astype(BF16)
            dhg = dhg + jnp.sum(don * xhat, axis=0, keepdims=True)
            dxh = don * hg_ref[...]
            do = (rr * (dxh - xhat * jnp.mean(dxh * xhat, axis=-1, keepdims=True)) * scale).astype(BF16)
            do_s[:, vs] = do
            v3 = z_ref[:, V0 + hd * GLA_HV:V0 + (hd + 1) * GLA_HV].reshape(cpt, CHUNK, GLA_HV)
            kdb3 = kdec[:, ks].astype(BF16).reshape(cpt, CHUNK, GLA_HK)
            st3 = st_ref[:, vs, :].astype(F32) * e3[:, :, ks] + jnp.einsum("bcv,bck->bvk", v3, kdb3,
                                                                            preferred_element_type=F32)
            dq3 = jnp.einsum("bcv,bvk->bck", do.reshape(cpt, CHUNK, GLA_HV), st3.astype(BF16), preferred_element_type=F32)
            dz_ref[:, Q0 + hd * GLA_HK:Q0 + (hd + 1) * GLA_HK] = dq3.reshape(tm, GLA_HK).astype(BF16)
        dhg_ref[...] += dhg

        def chunk(cj, carry):
            ci = cpt - 1 - cj
            rows = pl.ds(pl.multiple_of(ci * CHUNK, CHUNK), CHUNK)
            erows = pl.ds(pl.multiple_of(ci * 8, 8), 8)
            e_all = e_s[erows, :][0:1, :]
            for hd in range(GLA_HEADS):
                ks = slice(hd * GLA_HK, (hd + 1) * GLA_HK)
                vs = slice(hd * GLA_HV, (hd + 1) * GLA_HV)
                e = e_all[:, ks]
                kdb = kdec_s[rows, ks].astype(BF16)
                v = z_ref[rows, V0 + hd * GLA_HV:V0 + (hd + 1) * GLA_HV]
                q = z_ref[rows, Q0 + hd * GLA_HK:Q0 + (hd + 1) * GLA_HK]
                do = do_s[rows, vs]
                st_prev = st_ref[ci, vs, :].astype(F32)
                dst = dstate[vs, :] + _dot_tn(do, q)
                dstb = dst.astype(BF16)
                dkdec_s[rows, ks] = _dot(v, dstb)
                dz_ref[rows, V0 + hd * GLA_HV:V0 + (hd + 1) * GLA_HV] = _dot_nt(kdb, dstb).astype(BF16)
                dtot = jnp.sum(dst * st_prev, axis=0, keepdims=True) * e
                dtot_s[erows, ks] = jnp.broadcast_to(dtot, (8, GLA_HK))
                dstate[vs, :] = dst * e
            return carry

        lax.fori_loop(0, cpt, chunk, 0, unroll=cpt)

        dkdec = dkdec_s[...]
        dz_ref[:, K0:K0 + GLA_DK] = (dkdec * dec_s[...]).astype(BF16)
        before = _chunk_sums(dkdec * kdec_s[...], cpt, True, 2)
        dtot3 = dtot_s[...].reshape(cpt, 8, GLA_DK)[:, 0:1, :]
        dlac = (jnp.broadcast_to(dtot3, before.shape) + before).reshape(tm, GLA_DK)
        pre = _dot(z_ref[:, R0:R0 + GATE_PAD], gw_ref[...]) + gb_ref[...]
        dpre = dlac * (1.0 / GATE_NORM) * (1.0 - jax.nn.sigmoid(pre))
        dpb = dpre.astype(BF16)
        dz_ref[:, R0:R0 + GATE_PAD] = _dot_nt(dpb, gw_ref[...]).astype(BF16)
        dgw_ref[...] += _dot_tn(z_ref[:, R0:R0 + GATE_PAD], dpb)
        dgb_ref[...] += jnp.sum(dpre, axis=0, keepdims=True)

    rev = lambda i: (nt - 1 - i, 0)
    return _call_after(
        dep, body, 7,
        [pl.BlockSpec((tm, GLA_DV), rev), pl.BlockSpec((tm, GLA_DV), rev), pl.BlockSpec((tm, GLA_IN_PAD), rev),
         pl.BlockSpec((cpt, GLA_DV, GLA_HK), lambda i: (nt - 1 - i, 0, 0)),
         pl.BlockSpec((GATE_PAD, GLA_DK), lambda i: (0, 0)), pl.BlockSpec((1, GLA_DK), lambda i: (0, 0)),
         pl.BlockSpec((1, GLA_HV), lambda i: (0, 0))],
        (dmix, o, z, states, gate_w, gate_b, head_g), grid=(nt,),
        out_specs=[pl.BlockSpec((tm, GLA_IN_PAD), rev), pl.BlockSpec((GATE_PAD, GLA_DK), lambda i: (0, 0)),
                   pl.BlockSpec((1, GLA_DK), lambda i: (0, 0)), pl.BlockSpec((1, GLA_HV), lambda i: (0, 0))],
        out_shape=[jax.ShapeDtypeStruct((t, GLA_IN_PAD), BF16), jax.ShapeDtypeStruct((GATE_PAD, GLA_DK), F32),
                   jax.ShapeDtypeStruct((1, GLA_DK), F32), jax.ShapeDtypeStruct((1, GLA_HV), F32)],
        scratch_shapes=[pltpu.VMEM((GLA_DV, GLA_HK), F32), pltpu.VMEM((tm, GLA_DK), F32), pltpu.VMEM((tm, GLA_DK), F32),
                        pltpu.VMEM((tm, GLA_DK), F32), pltpu.VMEM((tm, GLA_DV), BF16),
                        pltpu.VMEM((cpt * 8, GLA_DK), F32), pltpu.VMEM((cpt * 8, GLA_DK), F32)],
        compiler_params=_cparams("arbitrary"), name="gla_seq_bwd")


def _sum_slots(x, name):
    n, r, cdim = x.shape
    tr = _row_tile(r, 256, 8)

    def body(x_ref, o_ref):
        acc = x_ref[0].astype(F32)
        for j in range(1, n):
            acc = acc + x_ref[j].astype(F32)
        o_ref[...] = acc

    return pl.pallas_call(
        body, grid=(r // tr,),
        in_specs=[pl.BlockSpec((n, tr, cdim), lambda i: (0, i, 0))],
        out_specs=pl.BlockSpec((tr, cdim), lambda i: (i, 0)),
        out_shape=jax.ShapeDtypeStruct((r, cdim), F32),
        compiler_params=_cparams("parallel"), name=name)(x)


def _sum_own_and_slots(own, slots, dev_idx, name):
    _, _, r, cdim = own.shape
    n = slots.shape[0]
    tr = _row_tile(r, 256, 8)

    def body(s_ref, own_ref, *rest):
        acc = own_ref[...].astype(F32)
        for other in rest[:n - 1]:
            acc = acc + other[...].astype(F32)
        rest[n - 1][...] = acc

    def slot(dd):
        return pl.BlockSpec((None, tr, cdim), lambda i, s: ((s[0] + dd) % n, i, 0))

    mine = pl.BlockSpec((None, None, tr, cdim), lambda i, s: (s[0] // 2, s[0] % 2, i, 0))
    return pl.pallas_call(
        body,
        grid_spec=pltpu.PrefetchScalarGridSpec(
            num_scalar_prefetch=1, grid=(r // tr,), in_specs=[mine] + [slot(dd) for dd in range(1, n)],
            out_specs=pl.BlockSpec((tr, cdim), lambda i, s: (i, 0))),
        out_shape=jax.ShapeDtypeStruct((r, cdim), F32),
        compiler_params=_cparams("parallel"), name=name)(dev_idx, own, *([slots] * (n - 1)))


def _add2(a, b, name):
    r, cdim = a.shape
    tr = _row_tile(r, 256, 8)

    def body(a_ref, b_ref, o_ref):
        o_ref[...] = a_ref[...] + b_ref[...]

    spec = pl.BlockSpec((tr, cdim), lambda i: (i, 0))
    return pl.pallas_call(body, grid=(r // tr,), in_specs=[spec, spec], out_specs=spec,
                          out_shape=jax.ShapeDtypeStruct((r, cdim), F32),
                          compiler_params=_cparams("parallel"), name=name)(a, b)


def _adamw_half(w, gs, m, v, half_idx, prev, name, dep=None):
    nl, _, h, cdim = w.shape
    tr = _row_tile(h, 256, 8)
    nprev = 0 if prev is None else 4
    extra = [] if dep is None else [dep]

    def body(s_ref, w_ref, m_ref, v_ref, *rest):
        g_refs = rest[:nl]
        go_ref, d_ref, mo_ref, vo_ref = rest[nl + nprev + len(extra):]
        layer = pl.program_id(0)
        gv = g_refs[0][...]
        for j in range(1, nl):
            gv = jnp.where(layer == j, g_refs[j][...], gv)
        go_ref[...] = gv
        mn = ADAM_B1 * m_ref[...] + (1.0 - ADAM_B1) * gv
        vn = ADAM_B2 * v_ref[...] + (1.0 - ADAM_B2) * (gv * gv)
        m_hat = mn / (1.0 - ADAM_B1 ** ADAM_STEP)
        v_hat = vn / (1.0 - ADAM_B2 ** ADAM_STEP)
        d_ref[...] = -ADAM_LR * (m_hat / (jnp.sqrt(v_hat) + ADAM_EPS) + ADAM_WD * w_ref[...])
        mo_ref[...] = mn
        vo_ref[...] = vn

    half = pl.BlockSpec((None, None, tr, cdim), lambda l, i, s: (l, s[0], i, 0))

    def of_layer(j):
        return pl.BlockSpec((tr, cdim), lambda l, i, s: (jnp.where(l == j, i, 0), 0))

    shp = jax.ShapeDtypeStruct(w.shape, F32)
    return pl.pallas_call(
        body,
        grid_spec=pltpu.PrefetchScalarGridSpec(
            num_scalar_prefetch=1, grid=(nl, h // tr),
            in_specs=[half] * 3 + [of_layer(j) for j in range(nl)] + [ANY_SPEC] * (nprev + len(extra)),
            out_specs=[half] * 4),
        out_shape=[shp] * 4, input_output_aliases={4 + nl + k: k for k in range(nprev)},
        compiler_params=_cparams("arbitrary", "arbitrary"), name=name,
    )(half_idx, w, m, v, *gs, *([] if prev is None else prev), *extra)


def _adamw_many(ws, gs, ms, vs):
    n = len(ws)

    def body(*refs):
        for i in range(n):
            w_ref, g_ref, m_ref, v_ref = refs[i], refs[n + i], refs[2 * n + i], refs[3 * n + i]
            d_ref, mo_ref, vo_ref = refs[4 * n + i], refs[5 * n + i], refs[6 * n + i]
            gv = g_ref[...]
            mn = ADAM_B1 * m_ref[...] + (1.0 - ADAM_B1) * gv
            vn = ADAM_B2 * v_ref[...] + (1.0 - ADAM_B2) * (gv * gv)
            m_hat = mn / (1.0 - ADAM_B1 ** ADAM_STEP)
            v_hat = vn / (1.0 - ADAM_B2 ** ADAM_STEP)
            d_ref[...] = -ADAM_LR * (m_hat / (jnp.sqrt(v_hat) + ADAM_EPS) + ADAM_WD * w_ref[...])
            mo_ref[...] = mn
            vo_ref[...] = vn

    shapes = [jax.ShapeDtypeStruct(w.shape, F32) for w in ws]
    outs = pl.pallas_call(body, out_shape=shapes * 3, name="adamw_small")(*ws, *gs, *ms, *vs)
    return outs[:n], outs[n:2 * n], outs[2 * n:]


def _split_rows(a):
    return a.reshape(a.shape[0], 2, a.shape[1] // 2, a.shape[2])


def _place():
    x, y, c = lax.axis_index("x"), lax.axis_index("y"), lax.axis_index("c")
    chips = [(1 - x, y), (x, 1 - y), (1 - x, 1 - y)]
    return x, y, c, chips


def _remote(src, dst, send_sem, recv_sem, to):
    return pltpu.make_async_remote_copy(src_ref=src, dst_ref=dst, send_sem=send_sem, recv_sem=recv_sem,
                                        device_id=to, device_id_type=MESH)


def _plan_gather(n_halved):
    def plan(src_refs, land_refs):
        x, y, c, chips = _place()
        me = 2 * x + y
        copies = []
        for k, (src, land) in enumerate(zip(src_refs, land_refs)):
            for (px, py) in chips:
                frm = 2 * px + py
                if k < n_halved:
                    copies.append((src.at[c], land.at[me, c], (px, py, c), land.at[frm, c]))
                else:
                    copies.append((src, land.at[me], (px, py, c), land.at[frm]))
        return copies
    return plan


def _plan_share(src_refs, land_refs):
    x, y, c, chips = _place()
    me = 2 * x + y
    sib = (x, y, 1 - c)
    copies = []
    for src, land in zip(src_refs, land_refs):
        copies.append((src, land.at[me], sib, land.at[me]))
        for (px, py) in chips:
            frm = 2 * px + py
            copies.append((land.at[frm, c], land.at[frm, c], sib, land.at[frm, 1 - c]))
    return copies


def _plan_scatter(n_parts):
    def plan(src_refs, land_refs):
        x, y, c, chips = _place()
        me = 2 * x + y
        copies = []
        for k, (src, land) in enumerate(zip(src_refs, land_refs)):
            for (px, py) in chips:
                to = 2 * px + py
                copies.append((src.at[to] if k < n_parts else src, land.at[me], (px, py, c), land.at[to]))
        return copies
    return plan


N_DEVICES = 8
OTHER_DEVICES = [(dx, dy, dc) for dx in (0, 1) for dy in (0, 1) for dc in (0, 1) if dx or dy or dc]


def _plan_scatter_all(src_refs, land_refs):
    x, y, c, _ = _place()
    me = 4 * x + 2 * y + c
    copies = []
    for src, land in zip(src_refs, land_refs):
        for dx, dy, dc in OTHER_DEVICES:
            px, py, pc = (1 - x if dx else x), (1 - y if dy else y), (1 - c if dc else c)
            copies.append((src.at[2 * px + py, pc], land.at[me], (px, py, pc), land.at[4 * px + 2 * py + pc]))
    return copies


def _plan_exchange(n_split):
    def plan(src_refs, land_refs):
        x, y, c, _ = _place()
        sib = (x, y, 1 - c)
        return [(src.at[:, 1 - c] if k < n_split else src, land, sib, land)
                for k, (src, land) in enumerate(zip(src_refs, land_refs))]
    return plan


def _hbm(a):
    return pltpu.HBM(a.shape, a.dtype)


def _start_copies(name, srcs, lands, plan, ncopy, dep=None):
    ns, nl = len(srcs), len(lands)
    nin = ns + nl + (0 if dep is None else 1)

    def body(*refs):
        send_sems, recv_sems, token = refs[nin], refs[nin + 1], refs[-1]
        for k, (src, dst, dev, _) in enumerate(plan(refs[:ns], refs[ns:ns + nl])):
            _remote(src, dst, send_sems.at[k], recv_sems.at[k], dev).start()
        token[...] = jnp.zeros_like(token)

    args = [pltpu.with_memory_space_constraint(a, pltpu.HBM) for a in list(srcs) + list(lands)]
    outs = pl.pallas_call(
        body, name=name,
        out_shape=(pltpu.SemaphoreType.DMA((ncopy,)), pltpu.SemaphoreType.DMA((ncopy,)),
                   *[_hbm(a) for a in list(srcs) + list(lands)], jax.ShapeDtypeStruct((8, 128), F32)),
        in_specs=[HBM_SPEC] * (ns + nl) + ([] if dep is None else [ANY_SPEC]),
        out_specs=(SEM_SPEC, SEM_SPEC, *([HBM_SPEC] * (ns + nl)), pl.BlockSpec(memory_space=pltpu.VMEM)),
        input_output_aliases={i: 2 + i for i in range(ns + nl)},
        compiler_params=pltpu.CompilerParams(has_side_effects=SIDE_EFFECT),
    )(*args, *([] if dep is None else [dep]))
    return outs[0], outs[1], list(outs[2:2 + ns]), list(outs[2 + ns:2 + ns + nl]), outs[-1]


def _wait_copies(name, started, plan, after, sem_offset=0):
    send_sems, recv_sems, srcs, lands, _ = started
    ns, nl = len(srcs), len(lands)
    after = list(after) if isinstance(after, (list, tuple)) else [after]

    def body(*refs):
        send_ref, recv_ref = refs[ns + nl], refs[ns + nl + 1]
        for k, (src, _, dev, mine) in enumerate(plan(refs[:ns], refs[ns:ns + nl])):
            copy = _remote(src, mine, send_ref.at[sem_offset + k], recv_ref.at[sem_offset + k], dev)
            copy.wait_send()
            copy.wait_recv()

    outs = pl.pallas_call(
        body, name=name, out_shape=tuple(_hbm(a) for a in srcs + lands),
        in_specs=[HBM_SPEC] * (ns + nl) + [SEM_SPEC, SEM_SPEC] + [ANY_SPEC] * len(after),
        out_specs=tuple([HBM_SPEC] * (ns + nl)),
        input_output_aliases={i: i for i in range(ns + nl)},
        compiler_params=pltpu.CompilerParams(has_side_effects=SIDE_EFFECT),
    )(*srcs, *lands, send_sems, recv_sems, *after)
    return list(outs[:ns]), list(outs[ns:])


def _share_with_sibling(name, srcs, lands):
    n = len(srcs)

    def body(*refs):
        src_refs, land_refs, out_refs = refs[:n], refs[n:2 * n], refs[2 * n:3 * n]
        send_sem, recv_sem = refs[3 * n:]
        x, y, c, chips = _place()
        me = 2 * x + y
        sib = (x, y, 1 - c)
        sends, recvs = [], []
        for k in range(n):
            sems = (send_sem.at[4 * k], recv_sem.at[4 * k])
            sends.append(_remote(src_refs[k], out_refs[k].at[me], *sems, sib))
            recvs.append(_remote(src_refs[k], out_refs[k].at[me], *sems, sib))
            for j, (px, py) in enumerate(chips):
                frm = 2 * px + py
                sems = (send_sem.at[4 * k + 1 + j], recv_sem.at[4 * k + 1 + j])
                sends.append(_remote(land_refs[k].at[frm, c], out_refs[k].at[frm, c], *sems, sib))
                recvs.append(_remote(land_refs[k].at[frm, c], out_refs[k].at[frm, 1 - c], *sems, sib))
        for cp in sends:
            cp.start()
        for cp in recvs:
            cp.wait_recv()
        for cp in sends:
            cp.wait_send()

    return pl.pallas_call(
        body, name=name, in_specs=[HBM_SPEC] * (2 * n), out_specs=[HBM_SPEC] * n,
        out_shape=[jax.ShapeDtypeStruct(a.shape, a.dtype) for a in lands],
        input_output_aliases={n + k: k for k in range(n)},
        scratch_shapes=[pltpu.SemaphoreType.DMA((4 * n,)), pltpu.SemaphoreType.DMA((4 * n,))],
    )(*srcs, *lands)


def _pack(arrs):
    flat = jnp.concatenate([a.reshape(-1).astype(F32) for a in arrs])
    n = flat.shape[0]
    rows = -(-n // PACK_WIDTH)
    rows = -(-rows // 8) * 8
    return jnp.pad(flat, (0, rows * PACK_WIDTH - n)).reshape(rows, PACK_WIDTH)


def _unpack(buf, shapes):
    flat = buf.reshape(-1)
    out, off = [], 0
    for shp in shapes:
        n = 1
        for s in shp:
            n *= s
        out.append(flat[off:off + n].reshape(shp))
        off += n
    return out


def _unshard_cols(stacked):
    moved = jnp.moveaxis(stacked, 0, -2)
    return moved.reshape(moved.shape[:-2] + (moved.shape[-2] * moved.shape[-1],))


def _take_cols(blocks, start, width):
    bw = blocks.shape[2]
    pieces, lo = [], start
    while lo < start + width:
        b = lo // bw
        hi = min(start + width, (b + 1) * bw)
        pieces.append(blocks[b][:, lo - b * bw:hi - b * bw])
        lo = hi
    return jnp.concatenate(pieces, axis=1)


def _col_shard(full, s, width):
    return lax.dynamic_slice_in_dim(full, s * width, width, axis=full.ndim - 1)


def kernel(x, meta_tokens, mix_norm_g, ffn_norm_g, ffn_w1, ffn_w2, cp_w_in, cp_conv_w, cp_conv_b, cp_ln_g, cp_ln_b, cp_pool_w, cp_pool_scale, cp_w_out, gla_w_in, gla_gate_w2, gla_gate_b, gla_head_g, gla_w_out, final_norm_g, loss_target, m_meta_tokens, m_mix_norm_g, m_ffn_norm_g, m_ffn_w1, m_ffn_w2, m_cp_w_in, m_cp_conv_w, m_cp_conv_b, m_cp_ln_g, m_cp_ln_b, m_cp_pool_w, m_cp_pool_scale, m_cp_w_out, m_gla_w_in, m_gla_gate_w2, m_gla_gate_b, m_gla_head_g, m_gla_w_out, m_final_norm_g, v_meta_tokens, v_mix_norm_g, v_ffn_norm_g, v_ffn_w1, v_ffn_w2, v_cp_w_in, v_cp_conv_w, v_cp_conv_b, v_cp_ln_g, v_cp_ln_b, v_cp_pool_w, v_cp_pool_scale, v_cp_w_out, v_gla_w_in, v_gla_gate_w2, v_gla_gate_b, v_gla_head_g, v_gla_w_out, v_final_norm_g):
    d = D_MODEL
    chip = 2 * lax.axis_index("x") + lax.axis_index("y")
    core = lax.axis_index("c")
    seq = x.shape[1]
    t = seq + CHUNK

    sharded_small = [meta_tokens, cp_conv_w, gla_gate_w2, gla_gate_b, gla_head_g]

    def halves(w, token=None):
        if token is not None:
            w = w + token[0, 0]
        return w.astype(BF16).reshape(2, w.shape[0] // 2, w.shape[1])

    def unhalve(g):
        return g.reshape(N_CHIPS, 2 * g.shape[2], g.shape[3])

    def gather_group(srcs, whole=()):
        lands = [lax.empty((N_CHIPS,) + s.shape, s.dtype) for s in srcs]
        for a in whole:
            lands.append(lax.dynamic_update_slice(jnp.zeros((N_CHIPS,) + a.shape, a.dtype), a[None], (chip,) + (0,) * a.ndim))
        return list(srcs) + list(whole), lands, _plan_gather(len(srcs)), len(srcs)

    def start_groups(name, groups, dep):
        bounds, all_srcs, all_lands = [], [], []
        for srcs, lands, _, _ in groups:
            bounds.append((len(all_srcs), len(all_srcs) + len(srcs)))
            all_srcs += srcs
            all_lands += lands

        def plan_all(src_refs, land_refs):
            return [cp for (lo, hi), group in zip(bounds, groups) for cp in group[2](src_refs[lo:hi], land_refs[lo:hi])]

        started = _start_copies(name, all_srcs, all_lands, plan_all, 3 * len(all_srcs), dep)
        return [(started, bound, group[2], group[3]) for bound, group in zip(bounds, groups)]

    def arrived(name, gather, after):
        started, (lo, hi), plan, n = gather
        mine = (started[0], started[1], started[2][lo:hi], started[3][lo:hi], started[4])
        srcs, lands = _wait_copies(name + "_wait", mine, plan, after, sem_offset=3 * lo)
        return srcs[:n], lands[:n], lands[n:]

    (cp_gather,) = start_groups("gather_cp_start", [gather_group([halves(cp_w_in[0]), halves(cp_w_out[0])],
                                                                 [_pack(sharded_small)])], None)
    tok = cp_gather[0][-1]
    ffn0_gather, gla_gather, ffn1_gather = start_groups(
        "gather_start", [gather_group([halves(ffn_w1[0], tok), halves(ffn_w2[0], tok)]),
                         gather_group([halves(gla_w_in[0], tok), halves(gla_w_out[0], tok)]),
                         gather_group([halves(ffn_w1[1], tok), halves(ffn_w2[1], tok)])], tok)
    h0_rows = jnp.concatenate([jnp.zeros((CHUNK, d), F32) + ffn0_gather[0][-1][0, 0], x[0]], axis=0)
    cp_srcs, cp_lands, (small_g,) = arrived("gather_cp", cp_gather, h0_rows)
    cpin_g, cpout_g = [unhalve(g) for g in _share_with_sibling("gather_cp_share", cp_srcs, cp_lands)]
    per_chip = [_unpack(small_g[j], [a.shape for a in sharded_small]) for j in range(N_CHIPS)]
    meta_f, conv_w_f, gate_w_f, gate_b_f, head_g_f = [
        jnp.concatenate([per_chip[j][i] for j in range(N_CHIPS)], axis=-1) for i in range(len(sharded_small))]
    conv_w_f, gate_w_f = conv_w_f[0], gate_w_f[0]
    w_cp_in = _unshard_cols(cpin_g)
    w_cp_out = cpout_g.reshape(CONV_DIM + POOL_DIM, d)
    gate_w_pad = jnp.pad(gate_w_f, ((0, GATE_PAD - GATE_RANK), (0, 0))).astype(BF16)
    row = lambda a: a.reshape(1, -1)
    c_idx = core.reshape(1).astype(jnp.int32)

    h0 = lax.dynamic_update_slice(h0_rows, meta_f, (PAD_ROWS, 0))
    z0, u0 = _norm_matmul(h0, row(mix_norm_g[0]), w_cp_in, 512, "cp_in_proj")
    c0, pm0, mix0 = _cp_seq_fwd(z0, conv_w_f, cp_conv_b, cp_ln_g, cp_ln_b, cp_pool_w[0], cp_pool_scale)
    ffn0_srcs, ffn0_lands, _ = arrived("gather_ffn0", ffn0_gather, mix0)
    ffn0_share = _start_copies("gather_ffn0_share_start", ffn0_srcs, ffn0_lands, _plan_share, 4 * len(ffn0_srcs))
    h1 = _matmul_residual(mix0, w_cp_out, h0, "cp_out_proj", dep=ffn0_share[-1])
    w1g0, w2g0 = [unhalve(g) for g in _wait_copies("gather_ffn0_share_wait", ffn0_share, _plan_share, h1)[1]]
    h2, hp0, uf0 = _ffn_fwd(h1, row(ffn_norm_g[0]), w1g0, w2g0, "ffn0_fwd")
    gla_srcs, gla_lands, _ = arrived("gather_gla", gla_gather, h2)
    glain_g, glaout_g = [unhalve(g) for g in _share_with_sibling("gather_gla_share", gla_srcs, gla_lands)]
    w_gla_in = jnp.concatenate([glain_g[j] for j in range(N_CHIPS)] + [jnp.zeros((d, GLA_IN_PAD - GLA_IN), BF16)], axis=1)
    w_gla_out = glaout_g.reshape(GLA_DV, d)
    z1, u2 = _norm_matmul(h2, row(mix_norm_g[1]), w_gla_in, GLA_COLS, "gla_in_proj")
    ffn1_srcs, ffn1_lands, _ = arrived("gather_ffn1", ffn1_gather, z1)
    ffn1_share = _start_copies("gather_ffn1_share_start", ffn1_srcs, ffn1_lands, _plan_share, 4 * len(ffn1_srcs))
    o1, mix1, states = _gla_seq_fwd(z1, gate_w_pad, gate_b_f, head_g_f, dep=ffn1_share[-1])
    h3 = _matmul_residual(mix1, w_gla_out, h2, "gla_out_proj")
    w1g1, w2g1 = [unhalve(g) for g in _wait_copies("gather_ffn1_share_wait", ffn1_share, _plan_share, h3)[1]]
    h4, hp1, uf1 = _ffn_fwd(h3, row(ffn_norm_g[1]), w1g1, w2g1, "ffn1_fwd")

    dev_idx = (2 * chip + core).reshape(1).astype(jnp.int32)

    def start_reduce(name, grads):
        srcs = [_split_rows(g) for g in grads]
        lands = [lax.empty((N_DEVICES,) + s.shape[2:], s.dtype) for s in srcs]
        return _start_copies(name + "_scatter_start", srcs, lands, _plan_scatter_all, len(OTHER_DEVICES) * len(srcs))

    def finish_reduce(name, started, after):
        srcs, lands = _wait_copies(name + "_scatter_wait", started, _plan_scatter_all, after)
        return [_sum_own_and_slots(s, l, dev_idx, "%s_slot_sum_%d" % (name, k)) for k, (s, l) in enumerate(zip(srcs, lands))]

    dh4, d_final_g, loss_part = _loss_bwd(h4, row(final_norm_g), loss_target[0])

    dh3, dhp1, d_ffn_g1 = _ffn_bwd_data(dh4, h3, row(ffn_norm_g[1]), hp1, w1g1, w2g1, "ffn1_bwd")
    dw1_1 = _wgrad(uf1, dhp1, N_CHIPS, d, d, False, True, False, "ffn1_dw1", rows=WGRAD_ROWS_BF16)
    dw2_1 = _wgrad(hp1, dh4, N_CHIPS, d, d, True, False, True, "ffn1_dw2")
    ffn1_reduce = start_reduce("ffn1", [dw1_1, dw2_1])

    dmix1 = _dgrad(dh3, w_gla_out, "gla_out_dgrad", dep=ffn1_reduce[-1])
    dw_gla_out = _wgrad(mix1, dh3, 1, GLA_DV, d, False, False, False, "gla_out_dw")
    dz1, d_gate_w, d_gate_b, d_head_g = _gla_seq_bwd(dmix1, o1, z1, states, gate_w_pad, gate_b_f, head_g_f)
    dh2, d_mix_g1 = _dgrad_norm_bwd(dz1, w_gla_in, h2, row(mix_norm_g[1]), dh3, GLA_COLS, "gla_in_dgrad")
    dw_gla_in = _wgrad(u2, dz1, GLA_IN_PAD // 640, d, 640, False, True, False, "gla_in_dw", rows=WGRAD_ROWS_BF16)
    gla_in_shards = jnp.stack([_take_cols(dw_gla_in, j * (GLA_IN // N_CHIPS), GLA_IN // N_CHIPS) for j in range(N_CHIPS)])
    gla_reduce = start_reduce("gla", [gla_in_shards, dw_gla_out.reshape(N_CHIPS, -1, d)])

    dh1, dhp0, d_ffn_g0 = _ffn_bwd_data(dh2, h1, row(ffn_norm_g[0]), hp0, w1g0, w2g0, "ffn0_bwd", dep=gla_reduce[-1])
    dw1_0 = _wgrad(uf0, dhp0, N_CHIPS, d, d, False, True, False, "ffn0_dw1", rows=WGRAD_ROWS_BF16)
    dw2_0 = _wgrad(hp0, dh2, N_CHIPS, d, d, True, False, True, "ffn0_dw2")
    ffn0_reduce = start_reduce("ffn0", [dw1_0, dw2_0])

    dmix0 = _dgrad(dh1, w_cp_out, "cp_out_dgrad", dep=ffn0_reduce[-1])
    dw_cp_out = _wgrad(mix0, dh1, 1, CONV_DIM + POOL_DIM, d, False, False, False, "cp_out_dw")
    dz0, d_conv_w, d_cp_vec, d_pool_w = _cp_seq_bwd(dmix0, z0, c0, pm0, conv_w_f, cp_ln_g, cp_ln_b, cp_pool_w[0],
                                                    cp_pool_scale)
    grad_x, dh0_head, d_mix_g0 = _dgrad_norm_bwd_input(dz0, w_cp_in, h0, row(mix_norm_g[0]), dh1, 512, "cp_in_dgrad")
    grad_x = grad_x[None]
    dw_cp_in = _wgrad(u0, dz0, 1, d, CP_IN, False, False, False, "cp_in_dw")
    dw_cp_in = jnp.stack([_take_cols(dw_cp_in, j * (CP_IN // N_CHIPS), CP_IN // N_CHIPS) for j in range(N_CHIPS)])

    cp_reduce = start_reduce("cp", [dw_cp_in, dw_cp_out.reshape(N_CHIPS, -1, d)])
    small_full = [dh0_head[PAD_ROWS:CHUNK],jnp.concatenate([d_mix_g0, d_mix_g1], axis=0),
                  jnp.concatenate([d_ffn_g0, d_ffn_g1], axis=0), d_conv_w[:CONV_WIDTH][None],
                  d_cp_vec[0:1], d_cp_vec[1:2], d_cp_vec[2:3], d_pool_w[None], d_cp_vec[3:4],
                  d_gate_w[:GATE_RANK][None], d_gate_b, d_head_g, d_final_g[0], loss_part[0, 0:1]]
    small_mine = _pack(small_full)
    whole = _plan_exchange(0)
    small_exchange = _start_copies("small_exchange_start", [small_mine], [lax.empty(small_mine.shape, F32)], whole, 1,
                                   dep=cp_reduce[-1])
    red_ffn1 = finish_reduce("ffn1", ffn1_reduce, small_exchange[-1])
    red_gla = finish_reduce("gla", gla_reduce, small_exchange[-1])
    (small_sent,), (small_recv,) = _wait_copies("small_exchange_wait", small_exchange, whole, [red_ffn1[1], red_gla[1]])
    small_chip = _add2(small_sent, small_recv, "chip_sum_small")
    small_slots = lax.dynamic_update_slice(jnp.zeros((N_CHIPS,) + small_chip.shape, F32), small_chip[None], (chip, 0, 0))
    small_reduce = _start_copies("small_scatter_start", [small_chip], [small_slots], _plan_scatter(0), 3)

    big = {"w1": (ffn_w1, m_ffn_w1, v_ffn_w1), "w2": (ffn_w2, m_ffn_w2, v_ffn_w2),
           "cp_in": (cp_w_in, m_cp_w_in, v_cp_w_in), "cp_out": (cp_w_out, m_cp_w_out, v_cp_w_out),
           "gla_in": (gla_w_in, m_gla_w_in, v_gla_w_in), "gla_out": (gla_w_out, m_gla_w_out, v_gla_w_out)}
    other_idx = (1 - core).reshape(1).astype(jnp.int32)

    def adamw_by_halves(tag, reduced, dep=None):
        flat = [r for n in reduced for r in reduced[n]]
        join_plan = _plan_exchange(0)
        join = _start_copies(tag + "_join_start", flat, [lax.empty(r.shape, F32) for r in flat], join_plan, len(flat),
                             dep=dep)
        views = {n: [_split_rows(a) for a in big[n]] for n in reduced}
        own, k = {}, 0
        for n in reduced:
            mine = join[2][k:k + len(reduced[n])]
            k += len(reduced[n])
            own[n] = _adamw_half(views[n][0], mine, views[n][1], views[n][2], c_idx, None, "adamw_%s_own" % n)
        _, arrived_halves = _wait_copies(tag + "_join_wait", join, join_plan, [own[n][1] for n in reduced])
        outs, k = {}, 0
        for n in reduced:
            theirs = arrived_halves[k:k + len(reduced[n])]
            k += len(reduced[n])
            res = _adamw_half(views[n][0], theirs, views[n][1], views[n][2], other_idx, own[n], "adamw_%s_sibling" % n)
            outs[n] = [o.reshape(big[n][0].shape) for o in res]
        return outs

    big_out = adamw_by_halves("gla", {"gla_in": [red_gla[0]], "gla_out": [red_gla[1]]}, dep=small_reduce[-1])
    red_ffn0 = finish_reduce("ffn0", ffn0_reduce, big_out["gla_out"][1])
    big_out.update(adamw_by_halves("ffn", {"w1": [red_ffn0[0], red_ffn1[0]], "w2": [red_ffn0[1], red_ffn1[1]]}))
    red_cp = finish_reduce("cp", cp_reduce, big_out["w2"][1])
    _, (small_landed,) = _wait_copies("small_scatter_wait", small_reduce, _plan_scatter(0), big_out["w2"][1])
    small_red = _sum_slots(small_landed, "slot_sum_small")
    big_out.update(adamw_by_halves("cp", {"cp_in": [red_cp[0]], "cp_out": [red_cp[1]]}))

    (g_meta, g_mix, g_ffn, g_conv_w, g_conv_b, g_ln_g, g_ln_b, g_pool_w, g_pool_scale, g_gate_w, g_gate_b, g_head,
     g_final, loss_sum) = _unpack(small_red, [a.shape for a in small_full])
    g_meta = _col_shard(g_meta, chip, meta_tokens.shape[-1])
    g_conv_w = _col_shard(g_conv_w, chip, cp_conv_w.shape[-1])
    g_gate_w = _col_shard(g_gate_w, chip, gla_gate_w2.shape[-1])
    g_gate_b = _col_shard(g_gate_b, chip, gla_gate_b.shape[-1])
    g_head = _col_shard(g_head, chip, gla_head_g.shape[-1])
    small_w = [meta_tokens, mix_norm_g, ffn_norm_g, cp_conv_w, cp_conv_b, cp_ln_g, cp_ln_b, cp_pool_w, cp_pool_scale,
               gla_gate_w2, gla_gate_b, gla_head_g, final_norm_g]
    small_m = [m_meta_tokens, m_mix_norm_g, m_ffn_norm_g, m_cp_conv_w, m_cp_conv_b, m_cp_ln_g, m_cp_ln_b, m_cp_pool_w,
               m_cp_pool_scale, m_gla_gate_w2, m_gla_gate_b, m_gla_head_g, m_final_norm_g]
    small_v = [v_meta_tokens, v_mix_norm_g, v_ffn_norm_g, v_cp_conv_w, v_cp_conv_b, v_cp_ln_g, v_cp_ln_b, v_cp_pool_w,
               v_cp_pool_scale, v_gla_gate_w2, v_gla_gate_b, v_gla_head_g, v_final_norm_g]
    small_g = [g_meta, g_mix, g_ffn, g_conv_w, g_conv_b, g_ln_g, g_ln_b, g_pool_w, g_pool_scale, g_gate_w, g_gate_b,
               g_head, g_final]
    shapes = [w.shape for w in small_w]
    small_g = [g.reshape(s) for g, s in zip(small_g, shapes)]
    at_least_2d = lambda arrs: [a.reshape(1, -1) if a.ndim == 1 else a for a in arrs]
    s_delta, s_m, s_v = _adamw_many(at_least_2d(small_w), at_least_2d(small_g), at_least_2d(small_m), at_least_2d(small_v))
    s_delta, s_m, s_v = [[a.reshape(s) for a, s in zip(group, shapes)] for group in (s_delta, s_m, s_v)]

    order = ["meta", "mix", "ffn", "w1", "w2", "cp_in", "conv_w", "conv_b", "ln_g", "ln_b", "pool_w", "pool_scale",
             "cp_out", "gla_in", "gate_w", "gate_b", "head", "gla_out", "final"]
    small_names = ["meta", "mix", "ffn", "conv_w", "conv_b", "ln_g", "ln_b", "pool_w", "pool_scale", "gate_w", "gate_b",
                   "head", "final"]
    big_names = ["w1", "w2", "cp_in", "cp_out", "gla_in", "gla_out"]
    table = {n: (small_g[i], s_delta[i], s_m[i], s_v[i]) for i, n in enumerate(small_names)}
    table.update({n: tuple(big_out[n]) for n in big_names})
    loss = loss_sum.reshape(())
    return (loss, grad_x, *[table[n][0] for n in order], *[table[n][1] for n in order],
            *[table[n][2] for n in order], *[table[n][3] for n in order])
```

```python
import functools

import jax
import jax.numpy as jnp
from jax import lax
from jax.experimental import pallas as pl
from jax.experimental.pallas import tpu as pltpu

F32 = jnp.float32
BF16 = jnp.bfloat16

D_MODEL = 1024
N_META = 16
CHUNK = 64
PAD_ROWS = CHUNK - N_META
EPS = 1e-5
CONV_DIM = 512
CONV_WIDTH = 31
CONV_HALO = 32
POOL_DIM = 512
POOL_WINDOWS = (2, 4, 8, 16)
POOL_GROUP = 128
POOL_HALO = 16
CP_IN = 2 * CONV_DIM + POOL_DIM
GLA_HEADS = 4
GLA_DK = 512
GLA_DV = 1024
GLA_HK = GLA_DK // GLA_HEADS
GLA_HV = GLA_DV // GLA_HEADS
GATE_RANK = 16
GATE_PAD = 128
GATE_NORM = 16.0
GLA_IN = 2 * GLA_DK + 2 * GLA_DV + GATE_RANK
GLA_IN_PAD = 2 * GLA_DK + 2 * GLA_DV + GATE_PAD
GLA_COLS = 1280
N_CHIPS = 4
ADAM_LR = 0.001
ADAM_B1 = 0.9
ADAM_B2 = 0.999
ADAM_EPS = 1e-08
ADAM_WD = 0.01
ADAM_STEP = 10

VMEM_LIMIT_BYTES = 56 * 1024 * 1024
ROW_TILE_TARGET = 832
TOKEN_TILE_TARGET = 1040
PACK_WIDTH = 1024
MESH = pl.DeviceIdType.MESH
HBM_SPEC = pl.BlockSpec(memory_space=pltpu.HBM)
ANY_SPEC = pl.BlockSpec(memory_space=pl.ANY)
SEM_SPEC = pl.BlockSpec(memory_space=pltpu.SEMAPHORE)
SIDE_EFFECT = pltpu.SideEffectType.DATAFLOW_SIDE_EFFECTING


def _cparams(*sem):
    return pltpu.CompilerParams(dimension_semantics=sem, vmem_limit_bytes=VMEM_LIMIT_BYTES)


def _row_tile(t, target, mult):
    best = mult
    for cand in range(mult, min(t, target) + 1, mult):
        if t % cand == 0:
            best = cand
    assert t % best == 0, (t, best)
    return best


def _rms(h, g):
    return h * lax.rsqrt(jnp.mean(h * h, axis=-1, keepdims=True) + EPS) * g


def _rms_bwd(h, g, du):
    r = lax.rsqrt(jnp.mean(h * h, axis=-1, keepdims=True) + EPS)
    xhat = h * r
    dxh = du * g
    dh = r * (dxh - xhat * jnp.mean(dxh * xhat, axis=-1, keepdims=True))
    return dh, du * xhat


def _valid_rows(i, tm):
    row = i * tm + lax.broadcasted_iota(jnp.int32, (tm, 1), 0)
    return row >= PAD_ROWS


def _dot(a, b):
    return jnp.dot(a, b, preferred_element_type=F32)


def _dot_nt(a, b):
    return lax.dot_general(a, b, (((1,), (1,)), ((), ())), preferred_element_type=F32)


def _dot_tn(a, b):
    return lax.dot_general(a, b, (((0,), (0,)), ((), ())), preferred_element_type=F32)


def _accumulate(ref, val, first):
    @pl.when(first)
    def _():
        ref[...] = val

    @pl.when(jnp.logical_not(first))
    def _():
        ref[...] += val


def _call_after(dep, body, n_in, in_specs, args, **kw):
    if dep is None:
        return pl.pallas_call(body, in_specs=in_specs, **kw)(*args)

    def with_dep(*refs):
        body(*refs[:n_in], *refs[n_in + 1:])

    return pl.pallas_call(with_dep, in_specs=list(in_specs) + [ANY_SPEC], **kw)(*args, dep)


def _norm_matmul(h, g, w, nc, name, dep=None):
    t, d = h.shape
    n = w.shape[1]
    tm = _row_tile(t, TOKEN_TILE_TARGET, 16)

    def body(h_ref, g_ref, w_ref, z_ref, u_ref):
        u = _rms(h_ref[...], g_ref[...]).astype(BF16)
        u_ref[...] = u
        for n0 in range(0, n, nc):
            n1 = min(n0 + nc, n)
            z_ref[:, n0:n1] = _dot(u, w_ref[:, n0:n1]).astype(BF16)

    return _call_after(
        dep, body, 3,
        [pl.BlockSpec((tm, d), lambda i: (i, 0)), pl.BlockSpec((1, d), lambda i: (0, 0)),
         pl.BlockSpec((d, n), lambda i: (0, 0))], (h, g, w), grid=(t // tm,),
        out_specs=[pl.BlockSpec((tm, n), lambda i: (i, 0)), pl.BlockSpec((tm, d), lambda i: (i, 0))],
        out_shape=[jax.ShapeDtypeStruct((t, n), BF16), jax.ShapeDtypeStruct((t, d), BF16)],
        compiler_params=_cparams("parallel"), name=name)


def _matmul_residual(a, w, h, name, dep=None):
    t, k = a.shape
    d = w.shape[1]
    tm = _row_tile(t, TOKEN_TILE_TARGET, 16)

    def body(a_ref, w_ref, h_ref, o_ref):
        o_ref[...] = h_ref[...] + _dot(a_ref[...], w_ref[...])

    return _call_after(
        dep, body, 3,
        [pl.BlockSpec((tm, k), lambda i: (i, 0)), pl.BlockSpec((k, d), lambda i: (0, 0)),
         pl.BlockSpec((tm, d), lambda i: (i, 0))], (a, w, h), grid=(t // tm,),
        out_specs=pl.BlockSpec((tm, d), lambda i: (i, 0)),
        out_shape=jax.ShapeDtypeStruct((t, d), F32),
        compiler_params=_cparams("parallel"), name=name)


def _ffn_fwd(h, g, w1g, w2g, name):
    t, d = h.shape
    ns, ffs = w1g.shape[0], w1g.shape[2]
    tm = _row_tile(t, TOKEN_TILE_TARGET, 16)

    def body(h_ref, g_ref, w1_ref, w2_ref, ho_ref, hp_ref, u_ref, acc_ref):
        s = pl.program_id(1)

        @pl.when(s == 0)
        def _():
            u_ref[...] = _rms(h_ref[...], g_ref[...]).astype(BF16)

        hp = _dot(u_ref[...], w1_ref[...])
        hp_ref[...] = hp.astype(BF16)
        a = jnp.maximum(hp, 0.0)
        _accumulate(acc_ref, _dot((a * a).astype(BF16), w2_ref[...]), s == 0)

        @pl.when(s == ns - 1)
        def _():
            ho_ref[...] = h_ref[...] + acc_ref[...]

    return pl.pallas_call(
        body, grid=(t // tm, ns),
        in_specs=[pl.BlockSpec((tm, d), lambda i, s: (i, 0)), pl.BlockSpec((1, d), lambda i, s: (0, 0)),
                  pl.BlockSpec((None, d, ffs), lambda i, s: (s, 0, 0)),
                  pl.BlockSpec((None, ffs, d), lambda i, s: (s, 0, 0))],
        out_specs=[pl.BlockSpec((tm, d), lambda i, s: (i, 0)), pl.BlockSpec((tm, ffs), lambda i, s: (i, s)),
                   pl.BlockSpec((tm, d), lambda i, s: (i, 0))],
        out_shape=[jax.ShapeDtypeStruct((t, d), F32), jax.ShapeDtypeStruct((t, ns * ffs), BF16),
                   jax.ShapeDtypeStruct((t, d), BF16)],
        scratch_shapes=[pltpu.VMEM((tm, d), F32)],
        compiler_params=_cparams("parallel", "arbitrary"), name=name)(h, g, w1g, w2g)


def _ffn_bwd_data(dh, h, g, hp, w1g, w2g, name, dep=None):
    t, d = h.shape
    ns, ffs = w1g.shape[0], w1g.shape[2]
    tm = _row_tile(t, ROW_TILE_TARGET, CHUNK)

    def body(dh_ref, h_ref, g_ref, hp_ref, w1_ref, w2_ref, dhi_ref, dhp_ref, dg_ref, acc_ref):
        i, s = pl.program_id(0), pl.program_id(1)
        da = _dot_nt(dh_ref[...].astype(BF16), w2_ref[...])
        dhp = (da * (2.0 * jnp.maximum(hp_ref[...].astype(F32), 0.0))).astype(BF16)
        dhp_ref[...] = dhp
        _accumulate(acc_ref, _dot_nt(dhp, w1_ref[...]), s == 0)

        @pl.when(s == ns - 1)
        def _():
            dhn, dgr = _rms_bwd(h_ref[...], g_ref[...], acc_ref[...])
            dhi_ref[...] = jnp.where(_valid_rows(i, tm), dh_ref[...] + dhn, 0.0)
            _accumulate(dg_ref, jnp.sum(dgr, axis=0, keepdims=True), i == 0)

    return _call_after(
        dep, body, 6,
        [pl.BlockSpec((tm, d), lambda i, s: (i, 0)), pl.BlockSpec((tm, d), lambda i, s: (i, 0)),
         pl.BlockSpec((1, d), lambda i, s: (0, 0)), pl.BlockSpec((tm, ffs), lambda i, s: (i, s)),
         pl.BlockSpec((None, d, ffs), lambda i, s: (s, 0, 0)),
         pl.BlockSpec((None, ffs, d), lambda i, s: (s, 0, 0))], (dh, h, g, hp, w1g, w2g), grid=(t // tm, ns),
        out_specs=[pl.BlockSpec((tm, d), lambda i, s: (i, 0)), pl.BlockSpec((tm, ffs), lambda i, s: (i, s)),
                   pl.BlockSpec((1, d), lambda i, s: (0, 0))],
        out_shape=[jax.ShapeDtypeStruct((t, d), F32), jax.ShapeDtypeStruct((t, ns * ffs), BF16),
                   jax.ShapeDtypeStruct((1, d), F32)],
        scratch_shapes=[pltpu.VMEM((tm, d), F32)],
        compiler_params=_cparams("arbitrary", "arbitrary"), name=name)


WGRAD_ROWS = 2048
WGRAD_ROWS_BF16 = 4096


def _wgrad(x, dy, nb, xc, yc, x_by_block, dy_by_block, relu2, name, dep=None, rows=WGRAD_ROWS):
    t = x.shape[0]
    tk = _row_tile(t - CHUNK, rows, CHUNK)

    def prep(xv):
        if relu2:
            xv = jnp.maximum(xv.astype(F32), 0.0)
            xv = xv * xv
        return xv.astype(BF16)

    nk = (t - CHUNK) // tk

    def body(xh_ref, dyh_ref, x_ref, dy_ref, o_ref, acc_ref):
        k = pl.program_id(1)
        p = _dot_tn(prep(x_ref[...]), dy_ref[...].astype(BF16))

        @pl.when(k == 0)
        def _():
            acc_ref[...] = p + _dot_tn(prep(xh_ref[...]), dyh_ref[...].astype(BF16))

        @pl.when(k > 0)
        def _():
            acc_ref[...] += p

        @pl.when(k == nk - 1)
        def _():
            o_ref[...] = acc_ref[...].astype(BF16)

    def head(width, by_block):
        return pl.BlockSpec((CHUNK, width), (lambda b, k: (0, b)) if by_block else (lambda b, k: (0, 0)))

    def rest(width, by_block):
        def index(b, k):
            return pl.multiple_of(CHUNK + k * tk, CHUNK), (pl.multiple_of(b * width, 128) if by_block else 0)
        return pl.BlockSpec((pl.Element(tk), pl.Element(width)), index)

    return _call_after(
        dep, body, 4,
        [head(xc, x_by_block), head(yc, dy_by_block), rest(xc, x_by_block), rest(yc, dy_by_block)], (x, dy, x, dy),
        grid=(nb, nk),
        out_specs=pl.BlockSpec((None, xc, yc), lambda b, k: (b, 0, 0)),
        out_shape=jax.ShapeDtypeStruct((nb, xc, yc), BF16),
        scratch_shapes=[pltpu.VMEM((xc, yc), F32)],
        compiler_params=_cparams("parallel", "arbitrary"), name=name)


def _dgrad(dh, w, name, dep=None):
    t, d = dh.shape
    k = w.shape[0]
    tm = _row_tile(t, TOKEN_TILE_TARGET, 16)

    def body(dh_ref, w_ref, o_ref):
        o_ref[...] = _dot_nt(dh_ref[...].astype(BF16), w_ref[...]).astype(BF16)

    return _call_after(
        dep, body, 2,
        [pl.BlockSpec((tm, d), lambda i: (i, 0)), pl.BlockSpec((k, d), lambda i: (0, 0))], (dh, w), grid=(t // tm,),
        out_specs=pl.BlockSpec((tm, k), lambda i: (i, 0)),
        out_shape=jax.ShapeDtypeStruct((t, k), BF16),
        compiler_params=_cparams("parallel"), name=name)


def _dgrad_norm_bwd(dz, w, h, g, dh, nc, name):
    t, d = h.shape
    n = w.shape[1]
    tm = _row_tile(t, ROW_TILE_TARGET, 16)

    def body(dz_ref, w_ref, h_ref, g_ref, dh_ref, dhi_ref, dg_ref):
        i = pl.program_id(0)
        du = jnp.zeros((tm, d), F32)
        for n0 in range(0, n, nc):
            n1 = min(n0 + nc, n)
            du = du + _dot_nt(dz_ref[:, n0:n1], w_ref[:, n0:n1])
        dhn, dgr = _rms_bwd(h_ref[...], g_ref[...], du)
        dhi_ref[...] = jnp.where(_valid_rows(i, tm), dh_ref[...] + dhn, 0.0)
        _accumulate(dg_ref, jnp.sum(dgr, axis=0, keepdims=True), i == 0)

    return pl.pallas_call(
        body, grid=(t // tm,),
        in_specs=[pl.BlockSpec((tm, n), lambda i: (i, 0)), pl.BlockSpec((d, n), lambda i: (0, 0)),
                  pl.BlockSpec((tm, d), lambda i: (i, 0)), pl.BlockSpec((1, d), lambda i: (0, 0)),
                  pl.BlockSpec((tm, d), lambda i: (i, 0))],
        out_specs=[pl.BlockSpec((tm, d), lambda i: (i, 0)), pl.BlockSpec((1, d), lambda i: (0, 0))],
        out_shape=[jax.ShapeDtypeStruct((t, d), F32), jax.ShapeDtypeStruct((1, d), F32)],
        compiler_params=_cparams("arbitrary"), name=name)(dz, w, h, g, dh)


def _dgrad_norm_bwd_input(dz, w, h, g, dh, nc, name):
    t, d = h.shape
    n = w.shape[1]
    tl = _row_tile(t - CHUNK, 512, CHUNK)

    def grads(dz_ref, w_ref, h_ref, g_ref, dh_ref, rows):
        du = jnp.zeros((rows, d), F32)
        for n0 in range(0, n, nc):
            n1 = min(n0 + nc, n)
            du = du + _dot_nt(dz_ref[:, n0:n1], w_ref[:, n0:n1])
        dhn, dgr = _rms_bwd(h_ref[...], g_ref[...], du)
        return dh_ref[...] + dhn, jnp.sum(dgr, axis=0, keepdims=True)

    def rest_body(dz_ref, w_ref, h_ref, g_ref, dh_ref, dg_head_ref, dx_ref, dg_ref):
        dx, dg = grads(dz_ref, w_ref, h_ref, g_ref, dh_ref, tl)
        dx_ref[...] = dx

        @pl.when(pl.program_id(0) == 0)
        def _():
            dg_ref[...] = dg_head_ref[...] + dg

        @pl.when(pl.program_id(0) > 0)
        def _():
            dg_ref[...] += dg

    def head_body(dz_ref, w_ref, h_ref, g_ref, dh_ref, dx_ref, dg_ref):
        dx, dg = grads(dz_ref, w_ref, h_ref, g_ref, dh_ref, CHUNK)
        dx_ref[...] = jnp.where(_valid_rows(0, CHUNK), dx, 0.0)
        dg_ref[...] = dg

    def shifted(width):
        return pl.BlockSpec((pl.Element(tl), pl.Element(width)), lambda i: (pl.multiple_of(CHUNK + i * tl, CHUNK), 0))

    whole = [pl.BlockSpec((d, n), lambda i: (0, 0)), pl.BlockSpec((1, d), lambda i: (0, 0))]
    head = lambda width: pl.BlockSpec((CHUNK, width), lambda i: (0, 0))
    dh_head, dg_head = pl.pallas_call(
        head_body, grid=(1,), in_specs=[head(n), whole[0], head(d), whole[1], head(d)],
        out_specs=[head(d), whole[1]],
        out_shape=[jax.ShapeDtypeStruct((CHUNK, d), F32), jax.ShapeDtypeStruct((1, d), F32)],
        compiler_params=_cparams("arbitrary"), name=name + "_head")(dz, w, h, g, dh)
    dx, dg = pl.pallas_call(
        rest_body, grid=((t - CHUNK) // tl,),
        in_specs=[shifted(n), whole[0], shifted(d), whole[1], shifted(d), whole[1]],
        out_specs=[pl.BlockSpec((tl, d), lambda i: (i, 0)), whole[1]],
        out_shape=[jax.ShapeDtypeStruct((t - CHUNK, d), F32), jax.ShapeDtypeStruct((1, d), F32)],
        compiler_params=_cparams("arbitrary"), name=name)(dz, w, h, g, dh, dg_head)
    return dx, dh_head, dg


def _loss_bwd(h, g, target):
    t, d = h.shape
    tl = _row_tile(t - CHUNK, 1024, CHUNK)

    def body(h_ref, g_ref, t_ref, dh_ref, dg_ref, loss_ref):
        i = pl.program_id(0)
        hv, gv = h_ref[...], g_ref[...]
        err = _rms(hv, gv) - t_ref[...]
        part = 0.5 * jnp.sum(jnp.mean(err * err, axis=-1, keepdims=True), axis=0, keepdims=True)
        dhn, dgr = _rms_bwd(hv, gv, err * (1.0 / d))
        dh_ref[...] = dhn
        _accumulate(dg_ref, jnp.sum(dgr, axis=0, keepdims=True), i == 0)
        _accumulate(loss_ref, jnp.broadcast_to(part, (8, 128)), i == 0)

    shifted = pl.BlockSpec((pl.Element(tl), pl.Element(d)), lambda i: (pl.multiple_of(CHUNK + i * tl, CHUNK), 0))
    dh, dg, loss = pl.pallas_call(
        body, grid=((t - CHUNK) // tl,),
        in_specs=[shifted, pl.BlockSpec((1, d), lambda i: (0, 0)), pl.BlockSpec((tl, d), lambda i: (i, 0))],
        out_specs=[shifted, pl.BlockSpec((1, d), lambda i: (0, 0)), pl.BlockSpec((8, 128), lambda i: (0, 0))],
        out_shape=[jax.ShapeDtypeStruct((t, d), F32), jax.ShapeDtypeStruct((1, d), F32),
                   jax.ShapeDtypeStruct((8, 128), F32)],
        compiler_params=_cparams("arbitrary"), name="loss_bwd")(h, g, target)

    def zero_head(dh_ref, o_ref):
        o_ref[...] = jnp.zeros_like(o_ref)

    dh = pl.pallas_call(
        zero_head, grid=(1,), in_specs=[ANY_SPEC], out_specs=pl.BlockSpec((CHUNK, d), lambda i: (0, 0)),
        out_shape=jax.ShapeDtypeStruct((t, d), F32), input_output_aliases={0: 0}, name="loss_bwd_head")(dh)
    return dh, dg, loss


CONV_BLOCK = 32


def _silu(x):
    return x * jax.nn.sigmoid(x)


def _row_shifts(win):
    n = win.shape[0]
    return [win] + [pltpu.roll(win, n - j, 0) for j in range(1, 8)]


def _cp_seq_fwd(z, conv_w, conv_b, ln_g, ln_b, pool_w, pool_scale):
    t = z.shape[0]
    tm = _row_tile(t, ROW_TILE_TARGET, CHUNK)

    def body(z_ref, cw_ref, cb_ref, lg_ref, lb_ref, pw_ref, ps_ref, c_ref, pm_ref, mix_ref, gbuf, pbuf):
        i = pl.program_id(0)

        @pl.when(i == 0)
        def _():
            gbuf[0:CONV_HALO, :] = jnp.zeros((CONV_HALO, CONV_DIM), F32)
            pbuf[0:POOL_HALO, :] = jnp.zeros((POOL_HALO, POOL_DIM), F32)

        @pl.when(i > 0)
        def _():
            gbuf[0:CONV_HALO, :] = gbuf[tm:tm + CONV_HALO, :]
            pbuf[0:POOL_HALO, :] = pbuf[tm:tm + POOL_HALO, :]

        av = z_ref[:, 0:CONV_DIM].astype(F32)
        ag = z_ref[:, CONV_DIM:2 * CONV_DIM].astype(F32)
        gbuf[CONV_HALO:CONV_HALO + tm, :] = av * jax.nn.sigmoid(ag)
        pbuf[POOL_HALO:POOL_HALO + tm, :] = z_ref[:, 2 * CONV_DIM:CP_IN].astype(F32)

        def conv_block(rb, carry):
            base = pl.multiple_of(rb * CONV_BLOCK, CONV_BLOCK)
            shifted = _row_shifts(gbuf[pl.ds(base, CONV_BLOCK + CONV_HALO), :])
            acc = jnp.zeros((CONV_BLOCK, CONV_DIM), F32)
            for k in range(CONV_WIDTH):
                whole, part = divmod(CONV_HALO - (CONV_WIDTH - 1) + k, 8)
                acc = acc + cw_ref[k:k + 1, :] * shifted[part][8 * whole:8 * whole + CONV_BLOCK, :]
            c_ref[pl.ds(base, CONV_BLOCK), :] = acc + cb_ref[...]
            return carry

        lax.fori_loop(0, tm // CONV_BLOCK, conv_block, 0)

        c = c_ref[...]
        mu = jnp.mean(c, axis=-1, keepdims=True)
        xc = c - mu
        ln = xc * lax.rsqrt(jnp.mean(xc * xc, axis=-1, keepdims=True) + EPS) * lg_ref[...] + lb_ref[...]
        row = i * tm + lax.broadcasted_iota(jnp.int32, (tm, 1), 0)
        mix_ref[:, 0:CONV_DIM] = jnp.where(row >= PAD_ROWS, _silu(ln), 0.0).astype(BF16)

        tpos = (row - PAD_ROWS + 1).astype(F32)
        for gi, wdw in enumerate(POOL_WINDOWS):
            lo = POOL_GROUP * gi
            run, step = pbuf[:, lo:lo + POOL_GROUP], 1
            cur = run[POOL_HALO:POOL_HALO + tm, :]
            while step < wdw:
                run = run + pltpu.roll(run, step, 0)
                step *= 2
            pm = (run[POOL_HALO:POOL_HALO + tm, :] / jnp.clip(tpos, 1.0, float(wdw)) - cur).astype(BF16)
            pm_ref[:, lo:lo + POOL_GROUP] = pm
            pg = _dot(pm, pw_ref[gi].astype(BF16))
            mix_ref[:, CONV_DIM + lo:CONV_DIM + lo + POOL_GROUP] = (pg * ps_ref[:, lo:lo + POOL_GROUP]).astype(BF16)

    vec = pl.BlockSpec((1, CONV_DIM), lambda i: (0, 0))
    return pl.pallas_call(
        body, grid=(t // tm,),
        in_specs=[pl.BlockSpec((tm, CP_IN), lambda i: (i, 0)),
                  pl.BlockSpec((CONV_WIDTH, CONV_DIM), lambda i: (0, 0)), vec, vec, vec,
                  pl.BlockSpec((len(POOL_WINDOWS), POOL_GROUP, POOL_GROUP), lambda i: (0, 0, 0)), vec],
        out_specs=[pl.BlockSpec((tm, CONV_DIM), lambda i: (i, 0)), pl.BlockSpec((tm, POOL_DIM), lambda i: (i, 0)),
                   pl.BlockSpec((tm, CONV_DIM + POOL_DIM), lambda i: (i, 0))],
        out_shape=[jax.ShapeDtypeStruct((t, CONV_DIM), F32), jax.ShapeDtypeStruct((t, POOL_DIM), BF16),
                   jax.ShapeDtypeStruct((t, CONV_DIM + POOL_DIM), BF16)],
        scratch_shapes=[pltpu.VMEM((tm + CONV_HALO, CONV_DIM), F32), pltpu.VMEM((tm + POOL_HALO, POOL_DIM), F32)],
        compiler_params=_cparams("arbitrary"), name="cp_seq_fwd")(z, conv_w, conv_b, ln_g, ln_b, pool_w, pool_scale)


def _cp_seq_bwd(dmix, z, c, pm, conv_w, ln_g, ln_b, pool_w, pool_scale, dep=None):
    t = z.shape[0]
    tm = _row_tile(t, ROW_TILE_TARGET, CHUNK)
    nt = t // tm

    def body(dmix_ref, z_ref, c_ref, pm_ref, cw_ref, lg_ref, lb_ref, pw_ref, ps_ref,
             dz_ref, dcw_ref, dvec_ref, dpw_ref, dcbuf, qbuf, glu_buf, dwacc):
        i = pl.program_id(0)
        tile = nt - 1 - i

        @pl.when(i == 0)
        def _():
            dcbuf[tm:tm + CONV_HALO, :] = jnp.zeros((CONV_HALO, CONV_DIM), F32)
            qbuf[tm:tm + POOL_HALO, :] = jnp.zeros((POOL_HALO, POOL_DIM), F32)
            dcw_ref[...] = jnp.zeros_like(dcw_ref)
            dwacc[...] = jnp.zeros_like(dwacc)
            dvec_ref[...] = jnp.zeros_like(dvec_ref)
            dpw_ref[...] = jnp.zeros_like(dpw_ref)

        @pl.when(i > 0)
        def _():
            dcbuf[tm:tm + CONV_HALO, :] = dcbuf[0:CONV_HALO, :]
            qbuf[tm:tm + POOL_HALO, :] = qbuf[0:POOL_HALO, :]

        row = tile * tm + lax.broadcasted_iota(jnp.int32, (tm, 1), 0)
        cv = c_ref[...]
        mu = jnp.mean(cv, axis=-1, keepdims=True)
        xc = cv - mu
        rstd = lax.rsqrt(jnp.mean(xc * xc, axis=-1, keepdims=True) + EPS)
        xhat = xc * rstd
        ln = xhat * lg_ref[...] + lb_ref[...]
        sg = jax.nn.sigmoid(ln)
        da = jnp.where(row >= PAD_ROWS, dmix_ref[:, 0:CONV_DIM].astype(F32), 0.0)
        dln = da * (sg * (1.0 + ln * (1.0 - sg)))
        dxh = dln * lg_ref[...]
        dc = rstd * (dxh - jnp.mean(dxh, axis=-1, keepdims=True) - xhat * jnp.mean(dxh * xhat, axis=-1, keepdims=True))
        dcbuf[0:tm, :] = dc
        dvec_ref[0:1, :] += jnp.sum(dc, axis=0, keepdims=True)
        dvec_ref[1:2, :] += jnp.sum(dln * xhat, axis=0, keepdims=True)
        dvec_ref[2:3, :] += jnp.sum(dln, axis=0, keepdims=True)

        av = z_ref[:, 0:CONV_DIM].astype(F32)
        sig_g = jax.nn.sigmoid(z_ref[:, CONV_DIM:2 * CONV_DIM].astype(F32))
        glu_buf[...] = av * sig_g

        def conv_block(rb, carry):
            base = pl.multiple_of(rb * CONV_BLOCK, CONV_BLOCK)
            shifted = _row_shifts(dcbuf[pl.ds(base, CONV_BLOCK + CONV_HALO), :])
            glu = glu_buf[pl.ds(base, CONV_BLOCK), :]
            acc = jnp.zeros((CONV_BLOCK, CONV_DIM), F32)
            for k in range(CONV_WIDTH):
                whole, part = divmod(CONV_WIDTH - 1 - k, 8)
                slab = shifted[part][8 * whole:8 * whole + CONV_BLOCK, :]
                acc = acc + cw_ref[k:k + 1, :] * slab
                prod = slab * glu
                part = prod[0:8]
                for q in range(1, CONV_BLOCK // 8):
                    part = part + prod[8 * q:8 * q + 8]
                dwacc[k] += part
            glu_buf[pl.ds(base, CONV_BLOCK), :] = acc
            return carry

        lax.fori_loop(0, tm // CONV_BLOCK, conv_block, 0)

        @pl.when(i == nt - 1)
        def _():
            for k in range(CONV_WIDTH):
                dcw_ref[k:k + 1, :] = jnp.sum(dwacc[k], axis=0, keepdims=True)
        dglu = glu_buf[...]
        dz_ref[:, 0:CONV_DIM] = (dglu * sig_g).astype(BF16)
        dz_ref[:, CONV_DIM:2 * CONV_DIM] = (dglu * av * sig_g * (1.0 - sig_g)).astype(BF16)

        tpos = (row - PAD_ROWS + 1).astype(F32)
        for gi, wdw in enumerate(POOL_WINDOWS):
            lo = POOL_GROUP * gi
            dp = dmix_ref[:, CONV_DIM + lo:CONV_DIM + lo + POOL_GROUP].astype(F32)
            pmv = pm_ref[:, lo:lo + POOL_GROUP]
            pwb = pw_ref[gi].astype(BF16)
            dvec_ref[3:4, lo:lo + POOL_GROUP] += jnp.sum(dp * _dot(pmv, pwb), axis=0, keepdims=True)
            dq = (dp * ps_ref[:, lo:lo + POOL_GROUP]).astype(BF16)
            dpw_ref[gi] += _dot_tn(pmv, dq)
            dpm = _dot_nt(dq, pwb)
            qbuf[0:tm, lo:lo + POOL_GROUP] = dpm / jnp.clip(tpos, 1.0, float(wdw))
            run, step = qbuf[:, lo:lo + POOL_GROUP], 1
            while step < wdw:
                run = run + pltpu.roll(run, tm + POOL_HALO - step, 0)
                step *= 2
            dz_ref[:, 2 * CONV_DIM + lo:2 * CONV_DIM + lo + POOL_GROUP] = (run[0:tm, :] - dpm).astype(BF16)

    vec = pl.BlockSpec((1, CONV_DIM), lambda i: (0, 0))
    rev = lambda i: (nt - 1 - i, 0)
    return _call_after(
        dep, body, 9,
        [pl.BlockSpec((tm, CONV_DIM + POOL_DIM), rev), pl.BlockSpec((tm, CP_IN), rev),
         pl.BlockSpec((tm, CONV_DIM), rev), pl.BlockSpec((tm, POOL_DIM), rev),
         pl.BlockSpec((CONV_WIDTH, CONV_DIM), lambda i: (0, 0)), vec, vec,
         pl.BlockSpec((len(POOL_WINDOWS), POOL_GROUP, POOL_GROUP), lambda i: (0, 0, 0)), vec],
        (dmix, z, c, pm, conv_w, ln_g, ln_b, pool_w, pool_scale), grid=(nt,),
        out_specs=[pl.BlockSpec((tm, CP_IN), rev), pl.BlockSpec((CONV_WIDTH + 1, CONV_DIM), lambda i: (0, 0)),
                   pl.BlockSpec((8, CONV_DIM), lambda i: (0, 0)),
                   pl.BlockSpec((len(POOL_WINDOWS), POOL_GROUP, POOL_GROUP), lambda i: (0, 0, 0))],
        out_shape=[jax.ShapeDtypeStruct((t, CP_IN), BF16), jax.ShapeDtypeStruct((CONV_WIDTH + 1, CONV_DIM), F32),
                   jax.ShapeDtypeStruct((8, CONV_DIM), F32),
                   jax.ShapeDtypeStruct((len(POOL_WINDOWS), POOL_GROUP, POOL_GROUP), F32)],
        scratch_shapes=[pltpu.VMEM((tm + CONV_HALO, CONV_DIM), F32), pltpu.VMEM((tm + POOL_HALO, POOL_DIM), F32),
                        pltpu.VMEM((tm, CONV_DIM), F32), pltpu.VMEM((CONV_WIDTH + 1, 8, CONV_DIM), F32)],
        compiler_params=_cparams("arbitrary"), name="cp_seq_bwd")


Q0, K0, V0, G0, R0 =0, GLA_DK, 2 * GLA_DK, 2 * GLA_DK + GLA_DV, 2 * GLA_DK + 2 * GLA_DV


def _split3(x):
    hi = x.astype(BF16)
    r1 = x - hi.astype(F32)
    mid = r1.astype(BF16)
    lo = (r1 - mid.astype(F32)).astype(BF16)
    return hi, mid, lo


def _tri(strict):
    r = lax.broadcasted_iota(jnp.int32, (CHUNK, CHUNK), 0)
    c = lax.broadcasted_iota(jnp.int32, (CHUNK, CHUNK), 1)
    return ((r > c) if strict else (r >= c)).astype(BF16)


def _chunk_sums(x, cpt, strict, pieces):
    tri3 = jnp.broadcast_to(_tri(strict)[None], (cpt, CHUNK, CHUNK))
    acc = None
    for piece in _split3(x.reshape(cpt, CHUNK, x.shape[-1]))[:pieces]:
        part = jnp.einsum("bij,bjk->bik", tri3, piece, preferred_element_type=F32)
        acc = part if acc is None else acc + part
    return acc


def _chunk_decay(r, gw_ref, gb_ref, cpt):
    pre = _dot(r, gw_ref[...]) + gb_ref[...]
    lac = (jnp.minimum(pre, 0.0) - jnp.log(1.0 + jnp.exp(-jnp.abs(pre)))) * (1.0 / GATE_NORM)
    cum3 = _chunk_sums(lac, cpt, False, 3)
    return cum3, cum3[:, CHUNK - 1:CHUNK, :]


def _gla_seq_fwd(z, gate_w, gate_b, head_g, dep=None):
    t = z.shape[0]
    tm = _row_tile(t, ROW_TILE_TARGET, CHUNK)
    cpt = tm // CHUNK
    scale = GLA_HK ** -0.5

    def body(z_ref, gw_ref, gb_ref, hg_ref, o_ref, mix_ref, st_ref, state, kdec_s, e_s):
        @pl.when(pl.program_id(0) == 0)
        def _():
            state[...] = jnp.zeros_like(state)

        cum3, tot3 = _chunk_decay(z_ref[:, R0:R0 + GATE_PAD], gw_ref, gb_ref, cpt)
        dec = jnp.exp(jnp.broadcast_to(tot3, cum3.shape) - cum3).reshape(tm, GLA_DK)
        kdec_s[...] = (z_ref[:, K0:K0 + GLA_DK].astype(F32) * dec).astype(BF16)
        e_s[...] = jnp.exp(jnp.broadcast_to(tot3, (cpt, 8, GLA_DK))).reshape(cpt * 8, GLA_DK)

        def chunk(ci, carry):
            rows = pl.ds(pl.multiple_of(ci * CHUNK, CHUNK), CHUNK)
            e_all = e_s[pl.ds(pl.multiple_of(ci * 8, 8), 8), :][0:1, :]
            st_ref[ci] = state[...].astype(BF16)
            for hd in range(GLA_HEADS):
                ks = slice(hd * GLA_HK, (hd + 1) * GLA_HK)
                vs = slice(hd * GLA_HV, (hd + 1) * GLA_HV)
                v = z_ref[rows, V0 + hd * GLA_HV:V0 + (hd + 1) * GLA_HV]
                st = state[vs, :] * e_all[:, ks] + _dot_tn(v, kdec_s[rows, ks])
                state[vs, :] = st
                q = z_ref[rows, Q0 + hd * GLA_HK:Q0 + (hd + 1) * GLA_HK]
                o_ref[rows, vs] = (_dot_nt(q, st.astype(BF16)) * scale).astype(BF16)
            return carry

        lax.fori_loop(0, cpt, chunk, 0, unroll=cpt)

        for hd in range(GLA_HEADS):
            vs = slice(hd * GLA_HV, (hd + 1) * GLA_HV)
            on = _rms(o_ref[:, vs].astype(F32), hg_ref[...])
            gv = z_ref[:, G0 + hd * GLA_HV:G0 + (hd + 1) * GLA_HV].astype(F32)
            mix_ref[:, vs] = (on * _silu(gv)).astype(BF16)

    return _call_after(
        dep, body, 4,
        [pl.BlockSpec((tm, GLA_IN_PAD), lambda i: (i, 0)),
         pl.BlockSpec((GATE_PAD, GLA_DK), lambda i: (0, 0)), pl.BlockSpec((1, GLA_DK), lambda i: (0, 0)),
         pl.BlockSpec((1, GLA_HV), lambda i: (0, 0))], (z, gate_w, gate_b, head_g), grid=(t // tm,),
        out_specs=[pl.BlockSpec((tm, GLA_DV), lambda i: (i, 0)), pl.BlockSpec((tm, GLA_DV), lambda i: (i, 0)),
                   pl.BlockSpec((cpt, GLA_DV, GLA_HK), lambda i: (i, 0, 0))],
        out_shape=[jax.ShapeDtypeStruct((t, GLA_DV), BF16), jax.ShapeDtypeStruct((t, GLA_DV), BF16),
                   jax.ShapeDtypeStruct((t // CHUNK, GLA_DV, GLA_HK), BF16)],
        scratch_shapes=[pltpu.VMEM((GLA_DV, GLA_HK), F32), pltpu.VMEM((tm, GLA_DK), BF16),
                        pltpu.VMEM((cpt * 8, GLA_DK), F32)],
        compiler_params=_cparams("arbitrary"), name="gla_seq_fwd")


def _gla_seq_bwd(dmix, o, z, states, gate_w, gate_b, head_g, dep=None):
    t = z.shape[0]
    tm = _row_tile(t, ROW_TILE_TARGET, CHUNK)
    cpt = tm // CHUNK
    nt = t // tm
    scale = GLA_HK ** -0.5

    def body(dmix_ref, o_ref, z_ref, st_ref, gw_ref, gb_ref, hg_ref, dz_ref, dgw_ref, dgb_ref, dhg_ref,
             dstate, dec_s, kdec_s, dkdec_s, do_s, e_s, dtot_s):
        @pl.when(pl.program_id(0) == 0)
        def _():
            dstate[...] = jnp.zeros_like(dstate)
            dgw_ref[...] = jnp.zeros_like(dgw_ref)
            dgb_ref[...] = jnp.zeros_like(dgb_ref)
            dhg_ref[...] = jnp.zeros_like(dhg_ref)

        cum3, tot3 = _chunk_decay(z_ref[:, R0:R0 + GATE_PAD], gw_ref, gb_ref, cpt)
        dec = jnp.exp(jnp.broadcast_to(tot3, cum3.shape) - cum3).reshape(tm, GLA_DK)
        dec_s[...] = dec
        kdec = z_ref[:, K0:K0 + GLA_DK].astype(F32) * dec
        kdec_s[...] = kdec
        e3 = jnp.exp(tot3)
        e_s[...] = jnp.broadcast_to(e3, (cpt, 8, GLA_DK)).reshape(cpt * 8, GLA_DK)
        dhg = jnp.zeros((1, GLA_HV), F32)
        for hd in range(GLA_HEADS):
            ks = slice(hd * GLA_HK, (hd + 1) * GLA_HK)
            vs = slice(hd * GLA_HV, (hd + 1) * GLA_HV)
            gcols = slice(G0 + hd * GLA_HV, G0 + (hd + 1) * GLA_HV)
            ov = o_ref[:, vs].astype(F32)
            gv = z_ref[:, gcols].astype(F32)
            dm = dmix_ref[:, vs].astype(F32)
            sg = jax.nn.sigmoid(gv)
            rr = lax.rsqrt(jnp.mean(ov * ov, axis=-1, keepdims=True) + EPS)
            xhat = ov * rr
            don = dm * (gv * sg)
            dz_ref[:, gcols] = (dm * (xhat * hg_ref[...]) * (sg * (1.0 + gv * (1.0 - sg)))).astype(BF16)
            dhg = dhg + jnp.sum(don * xhat, axis=0, keepdims=True)
            dxh = don * hg_ref[...]
            do = (rr * (dxh - xhat * jnp.mean(dxh * xhat, axis=-1, keepdims=True)) * scale).astype(BF16)
            do_s[:, vs] = do
            v3 = z_ref[:, V0 + hd * GLA_HV:V0 + (hd + 1) * GLA_HV].reshape(cpt, CHUNK, GLA_HV)
            kdb3 = kdec[:, ks].astype(BF16).reshape(cpt, CHUNK, GLA_HK)
            st3 = st_ref[:, vs, :].astype(F32) * e3[:, :, ks] + jnp.einsum("bcv,bck->bvk", v3, kdb3,
                                                                            preferred_element_type=F32)
            dq3 = jnp.einsum("bcv,bvk->bck", do.reshape(cpt, CHUNK, GLA_HV), st3.astype(BF16), preferred_element_type=F32)
            dz_ref[:, Q0 + hd * GLA_HK:Q0 + (hd + 1) * GLA_HK] = dq3.reshape(tm, GLA_HK).astype(BF16)
        dhg_ref[...] += dhg

        def chunk(cj, carry):
            ci = cpt - 1 - cj
            rows = pl.ds(pl.multiple_of(ci * CHUNK, CHUNK), CHUNK)
            erows = pl.ds(pl.multiple_of(ci * 8, 8), 8)
            e_all = e_s[erows, :][0:1, :]
            for hd in range(GLA_HEADS):
                ks = slice(hd * GLA_HK, (hd + 1) * GLA_HK)
                vs = slice(hd * GLA_HV, (hd + 1) * GLA_HV)
                e = e_all[:, ks]
                kdb = kdec_s[rows, ks].astype(BF16)
                v = z_ref[rows, V0 + hd * GLA_HV:V0 + (hd + 1) * GLA_HV]
                q = z_ref[rows, Q0 + hd * GLA_HK:Q0 + (hd + 1) * GLA_HK]
                do = do_s[rows, vs]
                st_prev = st_ref[ci, vs, :].astype(F32)
                dst = dstate[vs, :] + _dot_tn(do, q)
                dstb = dst.astype(BF16)
                dkdec_s[rows, ks] = _dot(v, dstb)
                dz_ref[rows, V0 + hd * GLA_HV:V0 + (hd + 1) * GLA_HV] = _dot_nt(kdb, dstb).astype(BF16)
                dtot = jnp.sum(dst * st_prev, axis=0, keepdims=True) * e
                dtot_s[erows, ks] = jnp.broadcast_to(dtot, (8, GLA_HK))
                dstate[vs, :] = dst * e
            return carry

        lax.fori_loop(0, cpt, chunk, 0, unroll=cpt)

        dkdec = dkdec_s[...]
        dz_ref[:, K0:K0 + GLA_DK] = (dkdec * dec_s[...]).astype(BF16)
        before = _chunk_sums(dkdec * kdec_s[...], cpt, True, 2)
        dtot3 = dtot_s[...].reshape(cpt, 8, GLA_DK)[:, 0:1, :]
        dlac = (jnp.broadcast_to(dtot3, before.shape) + before).reshape(tm, GLA_DK)
        pre = _dot(z_ref[:, R0:R0 + GATE_PAD], gw_ref[...]) + gb_ref[...]
        dpre = dlac * (1.0 / GATE_NORM) * (1.0 - jax.nn.sigmoid(pre))
        dpb = dpre.astype(BF16)
        dz_ref[:, R0:R0 + GATE_PAD] = _dot_nt(dpb, gw_ref[...]).astype(BF16)
        dgw_ref[...] += _dot_tn(z_ref[:, R0:R0 + GATE_PAD], dpb)
        dgb_ref[...] += jnp.sum(dpre, axis=0, keepdims=True)

    rev = lambda i: (nt - 1 - i, 0)
    return _call_after(
        dep, body, 7,
        [pl.BlockSpec((tm, GLA_DV), rev), pl.BlockSpec((tm, GLA_DV), rev), pl.BlockSpec((tm, GLA_IN_PAD), rev),
         pl.BlockSpec((cpt, GLA_DV, GLA_HK), lambda i: (nt - 1 - i, 0, 0)),
         pl.BlockSpec((GATE_PAD, GLA_DK), lambda i: (0, 0)), pl.BlockSpec((1, GLA_DK), lambda i: (0, 0)),
         pl.BlockSpec((1, GLA_HV), lambda i: (0, 0))],
        (dmix, o, z, states, gate_w, gate_b, head_g), grid=(nt,),
        out_specs=[pl.BlockSpec((tm, GLA_IN_PAD), rev), pl.BlockSpec((GATE_PAD, GLA_DK), lambda i: (0, 0)),
                   pl.BlockSpec((1, GLA_DK), lambda i: (0, 0)), pl.BlockSpec((1, GLA_HV), lambda i: (0, 0))],
        out_shape=[jax.ShapeDtypeStruct((t, GLA_IN_PAD), BF16), jax.ShapeDtypeStruct((GATE_PAD, GLA_DK), F32),
                   jax.ShapeDtypeStruct((1, GLA_DK), F32), jax.ShapeDtypeStruct((1, GLA_HV), F32)],
        scratch_shapes=[pltpu.VMEM((GLA_DV, GLA_HK), F32), pltpu.VMEM((tm, GLA_DK), F32), pltpu.VMEM((tm, GLA_DK), F32),
                        pltpu.VMEM((tm, GLA_DK), F32), pltpu.VMEM((tm, GLA_DV), BF16),
                        pltpu.VMEM((cpt * 8, GLA_DK), F32), pltpu.VMEM((cpt * 8, GLA_DK), F32)],
        compiler_params=_cparams("arbitrary"), name="gla_seq_bwd")


def _sum_slots(x, name):
    n, r, cdim = x.shape
    tr = _row_tile(r, 256, 8)

    def body(x_ref, o_ref):
        acc = x_ref[0].astype(F32)
        for j in range(1, n):
            acc = acc + x_ref[j].astype(F32)
        o_ref[...] = acc

    return pl.pallas_call(
        body, grid=(r // tr,),
        in_specs=[pl.BlockSpec((n, tr, cdim), lambda i: (0, i, 0))],
        out_specs=pl.BlockSpec((tr, cdim), lambda i: (i, 0)),
        out_shape=jax.ShapeDtypeStruct((r, cdim), F32),
        compiler_params=_cparams("parallel"), name=name)(x)


def _sum_own_and_slots(own, slots, dev_idx, name):
    _, _, r, cdim = own.shape
    n = slots.shape[0]
    tr = _row_tile(r, 128, 8)

    def body(s_ref, own_ref, *rest):
        acc = own_ref[...].astype(F32)
        for other in rest[:n - 1]:
            acc = acc + other[...].astype(F32)
        rest[n - 1][...] = acc

    def slot(dd):
        return pl.BlockSpec((None, tr, cdim), lambda i, s: ((s[0] + dd) % n, i, 0))

    mine = pl.BlockSpec((None, None, tr, cdim), lambda i, s: (s[0] // 2, s[0] % 2, i, 0))
    return pl.pallas_call(
        body,
        grid_spec=pltpu.PrefetchScalarGridSpec(
            num_scalar_prefetch=1, grid=(r // tr,), in_specs=[mine] + [slot(dd) for dd in range(1, n)],
            out_specs=pl.BlockSpec((tr, cdim), lambda i, s: (i, 0))),
        out_shape=jax.ShapeDtypeStruct((r, cdim), F32),
        compiler_params=_cparams("parallel"), name=name)(dev_idx, own, *([slots] * (n - 1)))


def _add2(a, b, name):
    r, cdim = a.shape
    tr = _row_tile(r, 256, 8)

    def body(a_ref, b_ref, o_ref):
        o_ref[...] = a_ref[...] + b_ref[...]

    spec = pl.BlockSpec((tr, cdim), lambda i: (i, 0))
    return pl.pallas_call(body, grid=(r // tr,), in_specs=[spec, spec], out_specs=spec,
                          out_shape=jax.ShapeDtypeStruct((r, cdim), F32),
                          compiler_params=_cparams("parallel"), name=name)(a, b)


def _adamw_half(w, gs, m, v, half_idx, prev, name, dep=None):
    nl, _, h, cdim = w.shape
    tr = _row_tile(h, 128, 8)
    nprev = 0 if prev is None else 4
    extra = [] if dep is None else [dep]

    def body(s_ref, w_ref, m_ref, v_ref, *rest):
        g_refs = rest[:nl]
        go_ref, d_ref, mo_ref, vo_ref = rest[nl + nprev + len(extra):]
        layer = pl.program_id(0)
        gv = g_refs[0][...]
        for j in range(1, nl):
            gv = jnp.where(layer == j, g_refs[j][...], gv)
        go_ref[...] = gv
        mn = ADAM_B1 * m_ref[...] + (1.0 - ADAM_B1) * gv
        vn = ADAM_B2 * v_ref[...] + (1.0 - ADAM_B2) * (gv * gv)
        m_hat = mn / (1.0 - ADAM_B1 ** ADAM_STEP)
        v_hat = vn / (1.0 - ADAM_B2 ** ADAM_STEP)
        d_ref[...] = -ADAM_LR * (m_hat / (jnp.sqrt(v_hat) + ADAM_EPS) + ADAM_WD * w_ref[...])
        mo_ref[...] = mn
        vo_ref[...] = vn

    half = pl.BlockSpec((None, None, tr, cdim), lambda l, i, s: (l, s[0], i, 0))

    def of_layer(j):
        return pl.BlockSpec((tr, cdim), lambda l, i, s: (jnp.where(l == j, i, 0), 0))

    shp = jax.ShapeDtypeStruct(w.shape, F32)
    return pl.pallas_call(
        body,
        grid_spec=pltpu.PrefetchScalarGridSpec(
            num_scalar_prefetch=1, grid=(nl, h // tr),
            in_specs=[half] * 3 + [of_layer(j) for j in range(nl)] + [ANY_SPEC] * (nprev + len(extra)),
            out_specs=[half] * 4),
        out_shape=[shp] * 4, input_output_aliases={4 + nl + k: k for k in range(nprev)},
        compiler_params=_cparams("arbitrary", "arbitrary"), name=name,
    )(half_idx, w, m, v, *gs, *([] if prev is None else prev), *extra)


def _adamw_many(ws, gs, ms, vs):
    n = len(ws)

    def body(*refs):
        for i in range(n):
            w_ref, g_ref, m_ref, v_ref = refs[i], refs[n + i], refs[2 * n + i], refs[3 * n + i]
            d_ref, mo_ref, vo_ref = refs[4 * n + i], refs[5 * n + i], refs[6 * n + i]
            gv = g_ref[...]
            mn = ADAM_B1 * m_ref[...] + (1.0 - ADAM_B1) * gv
            vn = ADAM_B2 * v_ref[...] + (1.0 - ADAM_B2) * (gv * gv)
            m_hat = mn / (1.0 - ADAM_B1 ** ADAM_STEP)
            v_hat = vn / (1.0 - ADAM_B2 ** ADAM_STEP)
            d_ref[...] = -ADAM_LR * (m_hat / (jnp.sqrt(v_hat) + ADAM_EPS) + ADAM_WD * w_ref[...])
            mo_ref[...] = mn
            vo_ref[...] = vn

    shapes = [jax.ShapeDtypeStruct(w.shape, F32) for w in ws]
    outs = pl.pallas_call(body, out_shape=shapes * 3, name="adamw_small")(*ws, *gs, *ms, *vs)
    return outs[:n], outs[n:2 * n], outs[2 * n:]


def _split_rows(a):
    return a.reshape(a.shape[0], 2, a.shape[1] // 2, a.shape[2])


def _place():
    x, y, c = lax.axis_index("x"), lax.axis_index("y"), lax.axis_index("c")
    chips = [(1 - x, y), (x, 1 - y), (1 - x, 1 - y)]
    return x, y, c, chips


def _remote(src, dst, send_sem, recv_sem, to):
    return pltpu.make_async_remote_copy(src_ref=src, dst_ref=dst, send_sem=send_sem, recv_sem=recv_sem,
                                        device_id=to, device_id_type=MESH)


def _plan_gather(n_halved):
    def plan(src_refs, land_refs):
        x, y, c, chips = _place()
        me = 2 * x + y
        copies = []
        for k, (src, land) in enumerate(zip(src_refs, land_refs)):
            for (px, py) in chips:
                frm = 2 * px + py
                if k < n_halved:
                    copies.append((src.at[c], land.at[me, c], (px, py, c), land.at[frm, c]))
                else:
                    copies.append((src, land.at[me], (px, py, c), land.at[frm]))
        return copies
    return plan


def _plan_share(src_refs, land_refs):
    x, y, c, chips = _place()
    me = 2 * x + y
    sib = (x, y, 1 - c)
    copies = []
    for src, land in zip(src_refs, land_refs):
        copies.append((src, land.at[me], sib, land.at[me]))
        for (px, py) in chips:
            frm = 2 * px + py
            copies.append((land.at[frm, c], land.at[frm, c], sib, land.at[frm, 1 - c]))
    return copies


def _plan_scatter(n_parts):
    def plan(src_refs, land_refs):
        x, y, c, chips = _place()
        me = 2 * x + y
        copies = []
        for k, (src, land) in enumerate(zip(src_refs, land_refs)):
            for (px, py) in chips:
                to = 2 * px + py
                copies.append((src.at[to] if k < n_parts else src, land.at[me], (px, py, c), land.at[to]))
        return copies
    return plan


N_DEVICES = 8
OTHER_DEVICES = [(dx, dy, dc) for dx in (0, 1) for dy in (0, 1) for dc in (0, 1) if dx or dy or dc]


def _plan_scatter_all(src_refs, land_refs):
    x, y, c, _ = _place()
    me = 4 * x + 2 * y + c
    copies = []
    for src, land in zip(src_refs, land_refs):
        for dx, dy, dc in OTHER_DEVICES:
            px, py, pc = (1 - x if dx else x), (1 - y if dy else y), (1 - c if dc else c)
            copies.append((src.at[2 * px + py, pc], land.at[me], (px, py, pc), land.at[4 * px + 2 * py + pc]))
    return copies


def _plan_exchange(n_split):
    def plan(src_refs, land_refs):
        x, y, c, _ = _place()
        sib = (x, y, 1 - c)
        return [(src.at[:, 1 - c] if k < n_split else src, land, sib, land)
                for k, (src, land) in enumerate(zip(src_refs, land_refs))]
    return plan


def _hbm(a):
    return pltpu.HBM(a.shape, a.dtype)


def _start_copies(name, srcs, lands, plan, ncopy, dep=None):
    ns, nl = len(srcs), len(lands)
    nin = ns + nl + (0 if dep is None else 1)

    def body(*refs):
        send_sems, recv_sems, token = refs[nin], refs[nin + 1], refs[-1]
        for k, (src, dst, dev, _) in enumerate(plan(refs[:ns], refs[ns:ns + nl])):
            _remote(src, dst, send_sems.at[k], recv_sems.at[k], dev).start()
        token[...] = jnp.zeros_like(token)

    args = [pltpu.with_memory_space_constraint(a, pltpu.HBM) for a in list(srcs) + list(lands)]
    outs = pl.pallas_call(
        body, name=name,
        out_shape=(pltpu.SemaphoreType.DMA((ncopy,)), pltpu.SemaphoreType.DMA((ncopy,)),
                   *[_hbm(a) for a in list(srcs) + list(lands)], jax.ShapeDtypeStruct((8, 128), F32)),
        in_specs=[HBM_SPEC] * (ns + nl) + ([] if dep is None else [ANY_SPEC]),
        out_specs=(SEM_SPEC, SEM_SPEC, *([HBM_SPEC] * (ns + nl)), pl.BlockSpec(memory_space=pltpu.VMEM)),
        input_output_aliases={i: 2 + i for i in range(ns + nl)},
        compiler_params=pltpu.CompilerParams(has_side_effects=SIDE_EFFECT),
    )(*args, *([] if dep is None else [dep]))
    return outs[0], outs[1], list(outs[2:2 + ns]), list(outs[2 + ns:2 + ns + nl]), outs[-1]


def _wait_copies(name, started, plan, after, sem_offset=0):
    send_sems, recv_sems, srcs, lands, _ = started
    ns, nl = len(srcs), len(lands)
    after = list(after) if isinstance(after, (list, tuple)) else [after]

    def body(*refs):
        send_ref, recv_ref = refs[ns + nl], refs[ns + nl + 1]
        for k, (src, _, dev, mine) in enumerate(plan(refs[:ns], refs[ns:ns + nl])):
            copy = _remote(src, mine, send_ref.at[sem_offset + k], recv_ref.at[sem_offset + k], dev)
            copy.wait_send()
            copy.wait_recv()

    outs = pl.pallas_call(
        body, name=name, out_shape=tuple(_hbm(a) for a in srcs + lands),
        in_specs=[HBM_SPEC] * (ns + nl) + [SEM_SPEC, SEM_SPEC] + [ANY_SPEC] * len(after),
        out_specs=tuple([HBM_SPEC] * (ns + nl)),
        input_output_aliases={i: i for i in range(ns + nl)},
        compiler_params=pltpu.CompilerParams(has_side_effects=SIDE_EFFECT),
    )(*srcs, *lands, send_sems, recv_sems, *after)
    return list(outs[:ns]), list(outs[ns:])


def _share_with_sibling(name, srcs, lands):
    n = len(srcs)

    def body(*refs):
        src_refs, land_refs, out_refs = refs[:n], refs[n:2 * n], refs[2 * n:3 * n]
        send_sem, recv_sem = refs[3 * n:]
        x, y, c, chips = _place()
        me = 2 * x + y
        sib = (x, y, 1 - c)
        sends, recvs = [], []
        for k in range(n):
            sems = (send_sem.at[4 * k], recv_sem.at[4 * k])
            sends.append(_remote(src_refs[k], out_refs[k].at[me], *sems, sib))
            recvs.append(_remote(src_refs[k], out_refs[k].at[me], *sems, sib))
            for j, (px, py) in enumerate(chips):
                frm = 2 * px + py
                sems = (send_sem.at[4 * k + 1 + j], recv_sem.at[4 * k + 1 + j])
                sends.append(_remote(land_refs[k].at[frm, c], out_refs[k].at[frm, c], *sems, sib))
                recvs.append(_remote(land_refs[k].at[frm, c], out_refs[k].at[frm, 1 - c], *sems, sib))
        for cp in sends:
            cp.start()
        for cp in recvs:
            cp.wait_recv()
        for cp in sends:
            cp.wait_send()

    return pl.pallas_call(
        body, name=name, in_specs=[HBM_SPEC] * (2 * n), out_specs=[HBM_SPEC] * n,
        out_shape=[jax.ShapeDtypeStruct(a.shape, a.dtype) for a in lands],
        input_output_aliases={n + k: k for k in range(n)},
        scratch_shapes=[pltpu.SemaphoreType.DMA((4 * n,)), pltpu.SemaphoreType.DMA((4 * n,))],
    )(*srcs, *lands)


def _pack(arrs):
    flat = jnp.concatenate([a.reshape(-1).astype(F32) for a in arrs])
    n = flat.shape[0]
    rows = -(-n // PACK_WIDTH)
    rows = -(-rows // 8) * 8
    return jnp.pad(flat, (0, rows * PACK_WIDTH - n)).reshape(rows, PACK_WIDTH)


def _unpack(buf, shapes):
    flat = buf.reshape(-1)
    out, off = [], 0
    for shp in shapes:
        n = 1
        for s in shp:
            n *= s
        out.append(flat[off:off + n].reshape(shp))
        off += n
    return out


def _unshard_cols(stacked):
    moved = jnp.moveaxis(stacked, 0, -2)
    return moved.reshape(moved.shape[:-2] + (moved.shape[-2] * moved.shape[-1],))


def _take_cols(blocks, start, width):
    bw = blocks.shape[2]
    pieces, lo = [], start
    while lo < start + width:
        b = lo // bw
        hi = min(start + width, (b + 1) * bw)
        pieces.append(blocks[b][:, lo - b * bw:hi - b * bw])
        lo = hi
    return jnp.concatenate(pieces, axis=1)


def _col_shard(full, s, width):
    return lax.dynamic_slice_in_dim(full, s * width, width, axis=full.ndim - 1)


def kernel(x, meta_tokens, mix_norm_g, ffn_norm_g, ffn_w1, ffn_w2, cp_w_in, cp_conv_w, cp_conv_b, cp_ln_g, cp_ln_b, cp_pool_w, cp_pool_scale, cp_w_out, gla_w_in, gla_gate_w2, gla_gate_b, gla_head_g, gla_w_out, final_norm_g, loss_target, m_meta_tokens, m_mix_norm_g, m_ffn_norm_g, m_ffn_w1, m_ffn_w2, m_cp_w_in, m_cp_conv_w, m_cp_conv_b, m_cp_ln_g, m_cp_ln_b, m_cp_pool_w, m_cp_pool_scale, m_cp_w_out, m_gla_w_in, m_gla_gate_w2, m_gla_gate_b, m_gla_head_g, m_gla_w_out, m_final_norm_g, v_meta_tokens, v_mix_norm_g, v_ffn_norm_g, v_ffn_w1, v_ffn_w2, v_cp_w_in, v_cp_conv_w, v_cp_conv_b, v_cp_ln_g, v_cp_ln_b, v_cp_pool_w, v_cp_pool_scale, v_cp_w_out, v_gla_w_in, v_gla_gate_w2, v_gla_gate_b, v_gla_head_g, v_gla_w_out, v_final_norm_g):
    d = D_MODEL
    chip = 2 * lax.axis_index("x") + lax.axis_index("y")
    core = lax.axis_index("c")
    seq = x.shape[1]
    t = seq + CHUNK

    sharded_small = [meta_tokens, cp_conv_w, gla_gate_w2, gla_gate_b, gla_head_g]

    def halves(w, token=None):
        if token is not None:
            w = w + token[0, 0]
        return w.astype(BF16).reshape(2, w.shape[0] // 2, w.shape[1])

    def unhalve(g):
        return g.reshape(N_CHIPS, 2 * g.shape[2], g.shape[3])

    def gather_group(srcs, whole=()):
        lands = [lax.empty((N_CHIPS,) + s.shape, s.dtype) for s in srcs]
        for a in whole:
            lands.append(lax.dynamic_update_slice(jnp.zeros((N_CHIPS,) + a.shape, a.dtype), a[None], (chip,) + (0,) * a.ndim))
        return list(srcs) + list(whole), lands, _plan_gather(len(srcs)), len(srcs)

    def start_groups(name, groups, dep):
        bounds, all_srcs, all_lands = [], [], []
        for srcs, lands, _, _ in groups:
            bounds.append((len(all_srcs), len(all_srcs) + len(srcs)))
            all_srcs += srcs
            all_lands += lands

        def plan_all(src_refs, land_refs):
            return [cp for (lo, hi), group in zip(bounds, groups) for cp in group[2](src_refs[lo:hi], land_refs[lo:hi])]

        started = _start_copies(name, all_srcs, all_lands, plan_all, 3 * len(all_srcs), dep)
        return [(started, bound, group[2], group[3]) for bound, group in zip(bounds, groups)]

    def arrived(name, gather, after):
        started, (lo, hi), plan, n = gather
        mine = (started[0], started[1], started[2][lo:hi], started[3][lo:hi], started[4])
        srcs, lands = _wait_copies(name + "_wait", mine, plan, after, sem_offset=3 * lo)
        return srcs[:n], lands[:n], lands[n:]

    (cp_gather,) = start_groups("gather_cp_start", [gather_group([halves(cp_w_in[0]), halves(cp_w_out[0])],
                                                                 [_pack(sharded_small)])], None)
    tok = cp_gather[0][-1]
    ffn0_gather, gla_gather, ffn1_gather = start_groups(
        "gather_start", [gather_group([halves(ffn_w1[0], tok), halves(ffn_w2[0], tok)]),
                         gather_group([halves(gla_w_in[0], tok), halves(gla_w_out[0], tok)]),
                         gather_group([halves(ffn_w1[1], tok), halves(ffn_w2[1], tok)])], tok)
    h0_rows = jnp.concatenate([jnp.zeros((CHUNK, d), F32) + ffn0_gather[0][-1][0, 0], x[0]], axis=0)
    cp_srcs, cp_lands, (small_g,) = arrived("gather_cp", cp_gather, h0_rows)
    cpin_g, cpout_g = [unhalve(g) for g in _share_with_sibling("gather_cp_share", cp_srcs, cp_lands)]
    per_chip = [_unpack(small_g[j], [a.shape for a in sharded_small]) for j in range(N_CHIPS)]
    meta_f, conv_w_f, gate_w_f, gate_b_f, head_g_f = [
        jnp.concatenate([per_chip[j][i] for j in range(N_CHIPS)], axis=-1) for i in range(len(sharded_small))]
    conv_w_f, gate_w_f = conv_w_f[0], gate_w_f[0]
    w_cp_in = _unshard_cols(cpin_g)
    w_cp_out = cpout_g.reshape(CONV_DIM + POOL_DIM, d)
    gate_w_pad = jnp.pad(gate_w_f, ((0, GATE_PAD - GATE_RANK), (0, 0))).astype(BF16)
    row = lambda a: a.reshape(1, -1)
    c_idx = core.reshape(1).astype(jnp.int32)

    h0 = lax.dynamic_update_slice(h0_rows, meta_f, (PAD_ROWS, 0))
    z0, u0 = _norm_matmul(h0, row(mix_norm_g[0]), w_cp_in, 512, "cp_in_proj")
    c0, pm0, mix0 = _cp_seq_fwd(z0, conv_w_f, cp_conv_b, cp_ln_g, cp_ln_b, cp_pool_w[0], cp_pool_scale)
    ffn0_srcs, ffn0_lands, _ = arrived("gather_ffn0", ffn0_gather, mix0)
    ffn0_share = _start_copies("gather_ffn0_share_start", ffn0_srcs, ffn0_lands, _plan_share, 4 * len(ffn0_srcs))
    h1 = _matmul_residual(mix0, w_cp_out, h0, "cp_out_proj", dep=ffn0_share[-1])
    w1g0, w2g0 = [unhalve(g) for g in _wait_copies("gather_ffn0_share_wait", ffn0_share, _plan_share, h1)[1]]
    h2, hp0, uf0 = _ffn_fwd(h1, row(ffn_norm_g[0]), w1g0, w2g0, "ffn0_fwd")
    gla_srcs, gla_lands, _ = arrived("gather_gla", gla_gather, h2)
    glain_g, glaout_g = [unhalve(g) for g in _share_with_sibling("gather_gla_share", gla_srcs, gla_lands)]
    w_gla_in = jnp.concatenate([glain_g[j] for j in range(N_CHIPS)] + [jnp.zeros((d, GLA_IN_PAD - GLA_IN), BF16)], axis=1)
    w_gla_out = glaout_g.reshape(GLA_DV, d)
    z1, u2 = _norm_matmul(h2, row(mix_norm_g[1]), w_gla_in, GLA_COLS, "gla_in_proj")
    ffn1_srcs, ffn1_lands, _ = arrived("gather_ffn1", ffn1_gather, z1)
    ffn1_share = _start_copies("gather_ffn1_share_start", ffn1_srcs, ffn1_lands, _plan_share, 4 * len(ffn1_srcs))
    o1, mix1, states = _gla_seq_fwd(z1, gate_w_pad, gate_b_f, head_g_f, dep=ffn1_share[-1])
    h3 = _matmul_residual(mix1, w_gla_out, h2, "gla_out_proj")
    w1g1, w2g1 = [unhalve(g) for g in _wait_copies("gather_ffn1_share_wait", ffn1_share, _plan_share, h3)[1]]
    h4, hp1, uf1 = _ffn_fwd(h3, row(ffn_norm_g[1]), w1g1, w2g1, "ffn1_fwd")

    dev_idx = (2 * chip + core).reshape(1).astype(jnp.int32)

    def start_reduce(name, grads):
        srcs = [_split_rows(g) for g in grads]
        lands = [lax.empty((N_DEVICES,) + s.shape[2:], s.dtype) for s in srcs]
        return _start_copies(name + "_scatter_start", srcs, lands, _plan_scatter_all, len(OTHER_DEVICES) * len(srcs))

    def finish_reduce(name, started, after):
        srcs, lands = _wait_copies(name + "_scatter_wait", started, _plan_scatter_all, after)
        return [_sum_own_and_slots(s, l, dev_idx, "%s_slot_sum_%d" % (name, k)) for k, (s, l) in enumerate(zip(srcs, lands))]

    dh4, d_final_g, loss_part = _loss_bwd(h4, row(final_norm_g), loss_target[0])

    dh3, dhp1, d_ffn_g1 = _ffn_bwd_data(dh4, h3, row(ffn_norm_g[1]), hp1, w1g1, w2g1, "ffn1_bwd")
    dw1_1 = _wgrad(uf1, dhp1, N_CHIPS, d, d, False, True, False, "ffn1_dw1", rows=WGRAD_ROWS_BF16)
    dw2_1 = _wgrad(hp1, dh4, N_CHIPS, d, d, True, False, True, "ffn1_dw2")
    ffn1_reduce = start_reduce("ffn1", [dw1_1, dw2_1])

    dmix1 = _dgrad(dh3, w_gla_out, "gla_out_dgrad", dep=ffn1_reduce[-1])
    dw_gla_out = _wgrad(mix1, dh3, 1, GLA_DV, d, False, False, False, "gla_out_dw")
    dz1, d_gate_w, d_gate_b, d_head_g = _gla_seq_bwd(dmix1, o1, z1, states, gate_w_pad, gate_b_f, head_g_f)
    dh2, d_mix_g1 = _dgrad_norm_bwd(dz1, w_gla_in, h2, row(mix_norm_g[1]), dh3, GLA_COLS, "gla_in_dgrad")
    dw_gla_in = _wgrad(u2, dz1, GLA_IN_PAD // 640, d, 640, False, True, False, "gla_in_dw", rows=WGRAD_ROWS_BF16)
    gla_in_shards = jnp.stack([_take_cols(dw_gla_in, j * (GLA_IN // N_CHIPS), GLA_IN // N_CHIPS) for j in range(N_CHIPS)])
    gla_reduce = start_reduce("gla", [gla_in_shards, dw_gla_out.reshape(N_CHIPS, -1, d)])

    dh1, dhp0, d_ffn_g0 = _ffn_bwd_data(dh2, h1, row(ffn_norm_g[0]), hp0, w1g0, w2g0, "ffn0_bwd", dep=gla_reduce[-1])
    dw1_0 = _wgrad(uf0, dhp0, N_CHIPS, d, d, False, True, False, "ffn0_dw1", rows=WGRAD_ROWS_BF16)
    dw2_0 = _wgrad(hp0, dh2, N_CHIPS, d, d, True, False, True, "ffn0_dw2")
    ffn0_reduce = start_reduce("ffn0", [dw1_0, dw2_0])

    dmix0 = _dgrad(dh1, w_cp_out, "cp_out_dgrad", dep=ffn0_reduce[-1])
    dw_cp_out = _wgrad(mix0, dh1, 1, CONV_DIM + POOL_DIM, d, False, False, False, "cp_out_dw")
    dz0, d_conv_w, d_cp_vec, d_pool_w = _cp_seq_bwd(dmix0, z0, c0, pm0, conv_w_f, cp_ln_g, cp_ln_b, cp_pool_w[0],
                                                    cp_pool_scale)
    grad_x, dh0_head, d_mix_g0 = _dgrad_norm_bwd_input(dz0, w_cp_in, h0, row(mix_norm_g[0]), dh1, 512, "cp_in_dgrad")
    grad_x = grad_x[None]
    dw_cp_in = _wgrad(u0, dz0, 1, d, CP_IN, False, False, False, "cp_in_dw")
    dw_cp_in = jnp.stack([_take_cols(dw_cp_in, j * (CP_IN // N_CHIPS), CP_IN // N_CHIPS) for j in range(N_CHIPS)])

    cp_reduce = start_reduce("cp", [dw_cp_in, dw_cp_out.reshape(N_CHIPS, -1, d)])
    small_full = [dh0_head[PAD_ROWS:CHUNK],jnp.concatenate([d_mix_g0, d_mix_g1], axis=0),
                  jnp.concatenate([d_ffn_g0, d_ffn_g1], axis=0), d_conv_w[:CONV_WIDTH][None],
                  d_cp_vec[0:1], d_cp_vec[1:2], d_cp_vec[2:3], d_pool_w[None], d_cp_vec[3:4],
                  d_gate_w[:GATE_RANK][None], d_gate_b, d_head_g, d_final_g[0], loss_part[0, 0:1]]
    small_mine = _pack(small_full)
    whole = _plan_exchange(0)
    small_exchange = _start_copies("small_exchange_start", [small_mine], [lax.empty(small_mine.shape, F32)], whole, 1,
                                   dep=cp_reduce[-1])
    red_ffn1 = finish_reduce("ffn1", ffn1_reduce, small_exchange[-1])
    red_gla = finish_reduce("gla", gla_reduce, small_exchange[-1])
    (small_sent,), (small_recv,) = _wait_copies("small_exchange_wait", small_exchange, whole, [red_ffn1[1], red_gla[1]])
    small_chip = _add2(small_sent, small_recv, "chip_sum_small")
    small_slots = lax.dynamic_update_slice(jnp.zeros((N_CHIPS,) + small_chip.shape, F32), small_chip[None], (chip, 0, 0))
    small_reduce = _start_copies("small_scatter_start", [small_chip], [small_slots], _plan_scatter(0), 3)

    big = {"w1": (ffn_w1, m_ffn_w1, v_ffn_w1), "w2": (ffn_w2, m_ffn_w2, v_ffn_w2),
           "cp_in": (cp_w_in, m_cp_w_in, v_cp_w_in), "cp_out": (cp_w_out, m_cp_w_out, v_cp_w_out),
           "gla_in": (gla_w_in, m_gla_w_in, v_gla_w_in), "gla_out": (gla_w_out, m_gla_w_out, v_gla_w_out)}
    other_idx = (1 - core).reshape(1).astype(jnp.int32)

    def adamw_by_halves(tag, reduced, dep=None):
        flat = [r for n in reduced for r in reduced[n]]
        join_plan = _plan_exchange(0)
        join = _start_copies(tag + "_join_start", flat, [lax.empty(r.shape, F32) for r in flat], join_plan, len(flat),
                             dep=dep)
        views = {n: [_split_rows(a) for a in big[n]] for n in reduced}
        own, k = {}, 0
        for n in reduced:
            mine = join[2][k:k + len(reduced[n])]
            k += len(reduced[n])
            own[n] = _adamw_half(views[n][0], mine, views[n][1], views[n][2], c_idx, None, "adamw_%s_own" % n)
        _, arrived_halves = _wait_copies(tag + "_join_wait", join, join_plan, [own[n][1] for n in reduced])
        outs, k = {}, 0
        for n in reduced:
            theirs = arrived_halves[k:k + len(reduced[n])]
            k += len(reduced[n])
            res = _adamw_half(views[n][0], theirs, views[n][1], views[n][2], other_idx, own[n], "adamw_%s_sibling" % n)
            outs[n] = [o.reshape(big[n][0].shape) for o in res]
        return outs

    big_out = adamw_by_halves("gla", {"gla_in": [red_gla[0]], "gla_out": [red_gla[1]]}, dep=small_reduce[-1])
    red_ffn0 = finish_reduce("ffn0", ffn0_reduce, big_out["gla_out"][1])
    big_out.update(adamw_by_halves("ffn", {"w1": [red_ffn0[0], red_ffn1[0]], "w2": [red_ffn0[1], red_ffn1[1]]}))
    red_cp = finish_reduce("cp", cp_reduce, big_out["w2"][1])
    _, (small_landed,) = _wait_copies("small_scatter_wait", small_reduce, _plan_scatter(0), big_out["w2"][1])
    small_red = _sum_slots(small_landed, "slot_sum_small")
    big_out.update(adamw_by_halves("cp", {"cp_in": [red_cp[0]], "cp_out": [red_cp[1]]}))

    (g_meta, g_mix, g_ffn, g_conv_w, g_conv_b, g_ln_g, g_ln_b, g_pool_w, g_pool_scale, g_gate_w, g_gate_b, g_head,
     g_final, loss_sum) = _unpack(small_red, [a.shape for a in small_full])
    g_meta = _col_shard(g_meta, chip, meta_tokens.shape[-1])
    g_conv_w = _col_shard(g_conv_w, chip, cp_conv_w.shape[-1])
    g_gate_w = _col_shard(g_gate_w, chip, gla_gate_w2.shape[-1])
    g_gate_b = _col_shard(g_gate_b, chip, gla_gate_b.shape[-1])
    g_head = _col_shard(g_head, chip, gla_head_g.shape[-1])
    small_w = [meta_tokens, mix_norm_g, ffn_norm_g, cp_conv_w, cp_conv_b, cp_ln_g, cp_ln_b, cp_pool_w, cp_pool_scale,
               gla_gate_w2, gla_gate_b, gla_head_g, final_norm_g]
    small_m = [m_meta_tokens, m_mix_norm_g, m_ffn_norm_g, m_cp_conv_w, m_cp_conv_b, m_cp_ln_g, m_cp_ln_b, m_cp_pool_w,
               m_cp_pool_scale, m_gla_gate_w2, m_gla_gate_b, m_gla_head_g, m_final_norm_g]
    small_v = [v_meta_tokens, v_mix_norm_g, v_ffn_norm_g, v_cp_conv_w, v_cp_conv_b, v_cp_ln_g, v_cp_ln_b, v_cp_pool_w,
               v_cp_pool_scale, v_gla_gate_w2, v_gla_gate_b, v_gla_head_g, v_final_norm_g]
    small_g = [g_meta, g_mix, g_ffn, g_conv_w, g_conv_b, g_ln_g, g_ln_b, g_pool_w, g_pool_scale, g_gate_w, g_gate_b,
               g_head, g_final]
    shapes = [w.shape for w in small_w]
    small_g = [g.reshape(s) for g, s in zip(small_g, shapes)]
    at_least_2d = lambda arrs: [a.reshape(1, -1) if a.ndim == 1 else a for a in arrs]
    s_delta, s_m, s_v = _adamw_many(at_least_2d(small_w), at_least_2d(small_g), at_least_2d(small_m), at_least_2d(small_v))
    s_delta, s_m, s_v = [[a.reshape(s) for a, s in zip(group, shapes)] for group in (s_delta, s_m, s_v)]

    order = ["meta", "mix", "ffn", "w1", "w2", "cp_in", "conv_w", "conv_b", "ln_g", "ln_b", "pool_w", "pool_scale",
             "cp_out", "gla_in", "gate_w", "gate_b", "head", "gla_out", "final"]
    small_names = ["meta", "mix", "ffn", "conv_w", "conv_b", "ln_g", "ln_b", "pool_w", "pool_scale", "gate_w", "gate_b",
                   "head", "final"]
    big_names = ["w1", "w2", "cp_in", "cp_out", "gla_in", "gla_out"]
    table = {n: (small_g[i], s_delta[i], s_m[i], s_v[i]) for i, n in enumerate(small_names)}
    table.update({n: tuple(big_out[n]) for n in big_names})
    loss = loss_sum.reshape(())
    return (loss, grad_x, *[table[n][0] for n in order], *[table[n][1] for n in order],
            *[table[n][2] for n in order], *[table[n][3] for n in order])
```

```python
import functools

import jax
import jax.numpy as jnp
from jax import lax
from jax.experimental import pallas as pl
from jax.experimental.pallas import tpu as pltpu

F32 = jnp.float32
BF16 = jnp.bfloat16

D_MODEL = 1024
N_META = 16
CHUNK = 64
PAD_ROWS = CHUNK - N_META
EPS = 1e-5
CONV_DIM = 512
CONV_WIDTH = 31
CONV_HALO = 32
POOL_DIM = 512
POOL_WINDOWS = (2, 4, 8, 16)
POOL_GROUP = 128
POOL_HALO = 16
CP_IN = 2 * CONV_DIM + POOL_DIM
GLA_HEADS = 4
GLA_DK = 512
GLA_DV = 1024
GLA_HK = GLA_DK // GLA_HEADS
GLA_HV = GLA_DV // GLA_HEADS
GATE_RANK = 16
GATE_PAD = 128
GATE_NORM = 16.0
GLA_IN = 2 * GLA_DK + 2 * GLA_DV + GATE_RANK
GLA_IN_PAD = 2 * GLA_DK + 2 * GLA_DV + GATE_PAD
GLA_COLS = 1280
N_CHIPS = 4
ADAM_LR = 0.001
ADAM_B1 = 0.9
ADAM_B2 = 0.999
ADAM_EPS = 1e-08
ADAM_WD = 0.01
ADAM_STEP = 10

VMEM_LIMIT_BYTES = 56 * 1024 * 1024
ROW_TILE_TARGET = 832
TOKEN_TILE_TARGET = 1040
PACK_WIDTH = 1024
MESH = pl.DeviceIdType.MESH
HBM_SPEC = pl.BlockSpec(memory_space=pltpu.HBM)
ANY_SPEC = pl.BlockSpec(memory_space=pl.ANY)
SEM_SPEC = pl.BlockSpec(memory_space=pltpu.SEMAPHORE)
SIDE_EFFECT = pltpu.SideEffectType.DATAFLOW_SIDE_EFFECTING


def _cparams(*sem):
    return pltpu.CompilerParams(dimension_semantics=sem, vmem_limit_bytes=VMEM_LIMIT_BYTES)


def _row_tile(t, target, mult):
    best = mult
    for cand in range(mult, min(t, target) + 1, mult):
        if t % cand == 0:
            best = cand
    assert t % best == 0, (t, best)
    return best


def _rms(h, g):
    return h * lax.rsqrt(jnp.mean(h * h, axis=-1, keepdims=True) + EPS) * g


def _rms_bwd(h, g, du):
    r = lax.rsqrt(jnp.mean(h * h, axis=-1, keepdims=True) + EPS)
    xhat = h * r
    dxh = du * g
    dh = r * (dxh - xhat * jnp.mean(dxh * xhat, axis=-1, keepdims=True))
    return dh, du * xhat


def _valid_rows(i, tm):
    row = i * tm + lax.broadcasted_iota(jnp.int32, (tm, 1), 0)
    return row >= PAD_ROWS


def _dot(a, b):
    return jnp.dot(a, b, preferred_element_type=F32)


def _dot_nt(a, b):
    return lax.dot_general(a, b, (((1,), (1,)), ((), ())), preferred_element_type=F32)


def _dot_tn(a, b):
    return lax.dot_general(a, b, (((0,), (0,)), ((), ())), preferred_element_type=F32)


def _accumulate(ref, val, first):
    @pl.when(first)
    def _():
        ref[...] = val

    @pl.when(jnp.logical_not(first))
    def _():
        ref[...] += val


def _call_after(dep, body, n_in, in_specs, args, **kw):
    if dep is None:
        return pl.pallas_call(body, in_specs=in_specs, **kw)(*args)

    def with_dep(*refs):
        body(*refs[:n_in], *refs[n_in + 1:])

    return pl.pallas_call(with_dep, in_specs=list(in_specs) + [ANY_SPEC], **kw)(*args, dep)


def _norm_matmul(h, g, w, nc, name, dep=None):
    t, d = h.shape
    n = w.shape[1]
    tm = _row_tile(t, TOKEN_TILE_TARGET, 16)

    def body(h_ref, g_ref, w_ref, z_ref, u_ref):
        u = _rms(h_ref[...], g_ref[...]).astype(BF16)
        u_ref[...] = u
        for n0 in range(0, n, nc):
            n1 = min(n0 + nc, n)
            z_ref[:, n0:n1] = _dot(u, w_ref[:, n0:n1]).astype(BF16)

    return _call_after(
        dep, body, 3,
        [pl.BlockSpec((tm, d), lambda i: (i, 0)), pl.BlockSpec((1, d), lambda i: (0, 0)),
         pl.BlockSpec((d, n), lambda i: (0, 0))], (h, g, w), grid=(t // tm,),
        out_specs=[pl.BlockSpec((tm, n), lambda i: (i, 0)), pl.BlockSpec((tm, d), lambda i: (i, 0))],
        out_shape=[jax.ShapeDtypeStruct((t, n), BF16), jax.ShapeDtypeStruct((t, d), BF16)],
        compiler_params=_cparams("parallel"), name=name)


def _matmul_residual(a, w, h, name, dep=None):
    t, k = a.shape
    d = w.shape[1]
    tm = _row_tile(t, TOKEN_TILE_TARGET, 16)

    def body(a_ref, w_ref, h_ref, o_ref):
        o_ref[...] = h_ref[...] + _dot(a_ref[...], w_ref[...])

    return _call_after(
        dep, body, 3,
        [pl.BlockSpec((tm, k), lambda i: (i, 0)), pl.BlockSpec((k, d), lambda i: (0, 0)),
         pl.BlockSpec((tm, d), lambda i: (i, 0))], (a, w, h), grid=(t // tm,),
        out_specs=pl.BlockSpec((tm, d), lambda i: (i, 0)),
        out_shape=jax.ShapeDtypeStruct((t, d), F32),
        compiler_params=_cparams("parallel"), name=name)


def _ffn_fwd(h, g, w1g, w2g, name):
    t, d = h.shape
    ns, ffs = w1g.shape[0], w1g.shape[2]
    tm = _row_tile(t, TOKEN_TILE_TARGET, 16)

    def body(h_ref, g_ref, w1_ref, w2_ref, ho_ref, hp_ref, u_ref, acc_ref):
        s = pl.program_id(1)

        @pl.when(s == 0)
        def _():
            u_ref[...] = _rms(h_ref[...], g_ref[...]).astype(BF16)

        hp = _dot(u_ref[...], w1_ref[...])
        hp_ref[...] = hp.astype(BF16)
        a = jnp.maximum(hp, 0.0)
        _accumulate(acc_ref, _dot((a * a).astype(BF16), w2_ref[...]), s == 0)

        @pl.when(s == ns - 1)
        def _():
            ho_ref[...] = h_ref[...] + acc_ref[...]

    return pl.pallas_call(
        body, grid=(t // tm, ns),
        in_specs=[pl.BlockSpec((tm, d), lambda i, s: (i, 0)), pl.BlockSpec((1, d), lambda i, s: (0, 0)),
                  pl.BlockSpec((None, d, ffs), lambda i, s: (s, 0, 0)),
                  pl.BlockSpec((None, ffs, d), lambda i, s: (s, 0, 0))],
        out_specs=[pl.BlockSpec((tm, d), lambda i, s: (i, 0)), pl.BlockSpec((tm, ffs), lambda i, s: (i, s)),
                   pl.BlockSpec((tm, d), lambda i, s: (i, 0))],
        out_shape=[jax.ShapeDtypeStruct((t, d), F32), jax.ShapeDtypeStruct((t, ns * ffs), BF16),
                   jax.ShapeDtypeStruct((t, d), BF16)],
        scratch_shapes=[pltpu.VMEM((tm, d), F32)],
        compiler_params=_cparams("parallel", "arbitrary"), name=name)(h, g, w1g, w2g)


def _ffn_bwd_data(dh, h, g, hp, w1g, w2g, name, dep=None):
    t, d = h.shape
    ns, ffs = w1g.shape[0], w1g.shape[2]
    tm = _row_tile(t, ROW_TILE_TARGET, CHUNK)

    def body(dh_ref, h_ref, g_ref, hp_ref, w1_ref, w2_ref, dhi_ref, dhp_ref, dg_ref, acc_ref):
        i, s = pl.program_id(0), pl.program_id(1)
        da = _dot_nt(dh_ref[...].astype(BF16), w2_ref[...])
        dhp = (da * (2.0 * jnp.maximum(hp_ref[...].astype(F32), 0.0))).astype(BF16)
        dhp_ref[...] = dhp
        _accumulate(acc_ref, _dot_nt(dhp, w1_ref[...]), s == 0)

        @pl.when(s == ns - 1)
        def _():
            dhn, dgr = _rms_bwd(h_ref[...], g_ref[...], acc_ref[...])
            dhi_ref[...] = jnp.where(_valid_rows(i, tm), dh_ref[...] + dhn, 0.0)
            _accumulate(dg_ref, jnp.sum(dgr, axis=0, keepdims=True), i == 0)

    return _call_after(
        dep, body, 6,
        [pl.BlockSpec((tm, d), lambda i, s: (i, 0)), pl.BlockSpec((tm, d), lambda i, s: (i, 0)),
         pl.BlockSpec((1, d), lambda i, s: (0, 0)), pl.BlockSpec((tm, ffs), lambda i, s: (i, s)),
         pl.BlockSpec((None, d, ffs), lambda i, s: (s, 0, 0)),
         pl.BlockSpec((None, ffs, d), lambda i, s: (s, 0, 0))], (dh, h, g, hp, w1g, w2g), grid=(t // tm, ns),
        out_specs=[pl.BlockSpec((tm, d), lambda i, s: (i, 0)), pl.BlockSpec((tm, ffs), lambda i, s: (i, s)),
                   pl.BlockSpec((1, d), lambda i, s: (0, 0))],
        out_shape=[jax.ShapeDtypeStruct((t, d), F32), jax.ShapeDtypeStruct((t, ns * ffs), BF16),
                   jax.ShapeDtypeStruct((1, d), F32)],
        scratch_shapes=[pltpu.VMEM((tm, d), F32)],
        compiler_params=_cparams("arbitrary", "arbitrary"), name=name)


WGRAD_ROWS = 2048
WGRAD_ROWS_BF16 = 4096


def _wgrad(x, dy, nb, xc, yc, x_by_block, dy_by_block, relu2, name, dep=None, rows=WGRAD_ROWS):
    t = x.shape[0]
    tk = _row_tile(t - CHUNK, rows, CHUNK)

    def prep(xv):
        if relu2:
            xv = jnp.maximum(xv.astype(F32), 0.0)
            xv = xv * xv
        return xv.astype(BF16)

    nk = (t - CHUNK) // tk

    def body(xh_ref, dyh_ref, x_ref, dy_ref, o_ref, acc_ref):
        k = pl.program_id(1)
        p = _dot_tn(prep(x_ref[...]), dy_ref[...].astype(BF16))

        @pl.when(k == 0)
        def _():
            acc_ref[...] = p + _dot_tn(prep(xh_ref[...]), dyh_ref[...].astype(BF16))

        @pl.when(k > 0)
        def _():
            acc_ref[...] += p

        @pl.when(k == nk - 1)
        def _():
            o_ref[...] = acc_ref[...].astype(BF16)

    def head(width, by_block):
        return pl.BlockSpec((CHUNK, width), (lambda b, k: (0, b)) if by_block else (lambda b, k: (0, 0)))

    def rest(width, by_block):
        def index(b, k):
            return pl.multiple_of(CHUNK + k * tk, CHUNK), (pl.multiple_of(b * width, 128) if by_block else 0)
        return pl.BlockSpec((pl.Element(tk), pl.Element(width)), index)

    return _call_after(
        dep, body, 4,
        [head(xc, x_by_block), head(yc, dy_by_block), rest(xc, x_by_block), rest(yc, dy_by_block)], (x, dy, x, dy),
        grid=(nb, nk),
        out_specs=pl.BlockSpec((None, xc, yc), lambda b, k: (b, 0, 0)),
        out_shape=jax.ShapeDtypeStruct((nb, xc, yc), BF16),
        scratch_shapes=[pltpu.VMEM((xc, yc), F32)],
        compiler_params=_cparams("parallel", "arbitrary"), name=name)


def _dgrad(dh, w, name, dep=None):
    t, d = dh.shape
    k = w.shape[0]
    tm = _row_tile(t, TOKEN_TILE_TARGET, 16)

    def body(dh_ref, w_ref, o_ref):
        o_ref[...] = _dot_nt(dh_ref[...].astype(BF16), w_ref[...]).astype(BF16)

    return _call_after(
        dep, body, 2,
        [pl.BlockSpec((tm, d), lambda i: (i, 0)), pl.BlockSpec((k, d), lambda i: (0, 0))], (dh, w), grid=(t // tm,),
        out_specs=pl.BlockSpec((tm, k), lambda i: (i, 0)),
        out_shape=jax.ShapeDtypeStruct((t, k), BF16),
        compiler_params=_cparams("parallel"), name=name)


def _dgrad_norm_bwd(dz, w, h, g, dh, nc, name):
    t, d = h.shape
    n = w.shape[1]
    tm = _row_tile(t, ROW_TILE_TARGET // 2, 16)

    def body(dz_ref, w_ref, h_ref, g_ref, dh_ref, dhi_ref, dg_ref):
        i = pl.program_id(0)
        du = jnp.zeros((tm, d), F32)
        for n0 in range(0, n, nc):
            n1 = min(n0 + nc, n)
            du = du + _dot_nt(dz_ref[:, n0:n1], w_ref[:, n0:n1])
        dhn, dgr = _rms_bwd(h_ref[...], g_ref[...], du)
        dhi_ref[...] = jnp.where(_valid_rows(i, tm), dh_ref[...] + dhn, 0.0)
        _accumulate(dg_ref, jnp.sum(dgr, axis=0, keepdims=True), i == 0)

    return pl.pallas_call(
        body, grid=(t // tm,),
        in_specs=[pl.BlockSpec((tm, n), lambda i: (i, 0)), pl.BlockSpec((d, n), lambda i: (0, 0)),
                  pl.BlockSpec((tm, d), lambda i: (i, 0)), pl.BlockSpec((1, d), lambda i: (0, 0)),
                  pl.BlockSpec((tm, d), lambda i: (i, 0))],
        out_specs=[pl.BlockSpec((tm, d), lambda i: (i, 0)), pl.BlockSpec((1, d), lambda i: (0, 0))],
        out_shape=[jax.ShapeDtypeStruct((t, d), F32), jax.ShapeDtypeStruct((1, d), F32)],
        compiler_params=_cparams("arbitrary"), name=name)(dz, w, h, g, dh)


def _dgrad_norm_bwd_input(dz, w, h, g, dh, nc, name):
    t, d = h.shape
    n = w.shape[1]
    tl = _row_tile(t - CHUNK, 512, CHUNK)

    def grads(dz_ref, w_ref, h_ref, g_ref, dh_ref, rows):
        du = jnp.zeros((rows, d), F32)
        for n0 in range(0, n, nc):
            n1 = min(n0 + nc, n)
            du = du + _dot_nt(dz_ref[:, n0:n1], w_ref[:, n0:n1])
        dhn, dgr = _rms_bwd(h_ref[...], g_ref[...], du)
        return dh_ref[...] + dhn, jnp.sum(dgr, axis=0, keepdims=True)

    def rest_body(dz_ref, w_ref, h_ref, g_ref, dh_ref, dg_head_ref, dx_ref, dg_ref):
        dx, dg = grads(dz_ref, w_ref, h_ref, g_ref, dh_ref, tl)
        dx_ref[...] = dx

        @pl.when(pl.program_id(0) == 0)
        def _():
            dg_ref[...] = dg_head_ref[...] + dg

        @pl.when(pl.program_id(0) > 0)
        def _():
            dg_ref[...] += dg

    def head_body(dz_ref, w_ref, h_ref, g_ref, dh_ref, dx_ref, dg_ref):
        dx, dg = grads(dz_ref, w_ref, h_ref, g_ref, dh_ref, CHUNK)
        dx_ref[...] = jnp.where(_valid_rows(0, CHUNK), dx, 0.0)
        dg_ref[...] = dg

    def shifted(width):
        return pl.BlockSpec((pl.Element(tl), pl.Element(width)), lambda i: (pl.multiple_of(CHUNK + i * tl, CHUNK), 0))

    whole = [pl.BlockSpec((d, n), lambda i: (0, 0)), pl.BlockSpec((1, d), lambda i: (0, 0))]
    head = lambda width: pl.BlockSpec((CHUNK, width), lambda i: (0, 0))
    dh_head, dg_head = pl.pallas_call(
        head_body, grid=(1,), in_specs=[head(n), whole[0], head(d), whole[1], head(d)],
        out_specs=[head(d), whole[1]],
        out_shape=[jax.ShapeDtypeStruct((CHUNK, d), F32), jax.ShapeDtypeStruct((1, d), F32)],
        compiler_params=_cparams("arbitrary"), name=name + "_head")(dz, w, h, g, dh)
    dx, dg = pl.pallas_call(
        rest_body, grid=((t - CHUNK) // tl,),
        in_specs=[shifted(n), whole[0], shifted(d), whole[1], shifted(d), whole[1]],
        out_specs=[pl.BlockSpec((tl, d), lambda i: (i, 0)), whole[1]],
        out_shape=[jax.ShapeDtypeStruct((t - CHUNK, d), F32), jax.ShapeDtypeStruct((1, d), F32)],
        compiler_params=_cparams("arbitrary"), name=name)(dz, w, h, g, dh, dg_head)
    return dx, dh_head, dg


def _loss_bwd(h, g, target):
    t, d = h.shape
    tl = _row_tile(t - CHUNK, 1024, CHUNK)

    def body(h_ref, g_ref, t_ref, dh_ref, dg_ref, loss_ref):
        i = pl.program_id(0)
        hv, gv = h_ref[...], g_ref[...]
        err = _rms(hv, gv) - t_ref[...]
        part = 0.5 * jnp.sum(jnp.mean(err * err, axis=-1, keepdims=True), axis=0, keepdims=True)
        dhn, dgr = _rms_bwd(hv, gv, err * (1.0 / d))
        dh_ref[...] = dhn
        _accumulate(dg_ref, jnp.sum(dgr, axis=0, keepdims=True), i == 0)
        _accumulate(loss_ref, jnp.broadcast_to(part, (8, 128)), i == 0)

    shifted = pl.BlockSpec((pl.Element(tl), pl.Element(d)), lambda i: (pl.multiple_of(CHUNK + i * tl, CHUNK), 0))
    dh, dg, loss = pl.pallas_call(
        body, grid=((t - CHUNK) // tl,),
        in_specs=[shifted, pl.BlockSpec((1, d), lambda i: (0, 0)), pl.BlockSpec((tl, d), lambda i: (i, 0))],
        out_specs=[shifted, pl.BlockSpec((1, d), lambda i: (0, 0)), pl.BlockSpec((8, 128), lambda i: (0, 0))],
        out_shape=[jax.ShapeDtypeStruct((t, d), F32), jax.ShapeDtypeStruct((1, d), F32),
                   jax.ShapeDtypeStruct((8, 128), F32)],
        compiler_params=_cparams("arbitrary"), name="loss_bwd")(h, g, target)

    def zero_head(dh_ref, o_ref):
        o_ref[...] = jnp.zeros_like(o_ref)

    dh = pl.pallas_call(
        zero_head, grid=(1,), in_specs=[ANY_SPEC], out_specs=pl.BlockSpec((CHUNK, d), lambda i: (0, 0)),
        out_shape=jax.ShapeDtypeStruct((t, d), F32), input_output_aliases={0: 0}, name="loss_bwd_head")(dh)
    return dh, dg, loss


CONV_BLOCK = 32


def _silu(x):
    return x * jax.nn.sigmoid(x)


def _row_shifts(win):
    n = win.shape[0]
    return [win] + [pltpu.roll(win, n - j, 0) for j in range(1, 8)]


def _cp_seq_fwd(z, conv_w, conv_b, ln_g, ln_b, pool_w, pool_scale):
    t = z.shape[0]
    tm = _row_tile(t, ROW_TILE_TARGET, CHUNK)

    def body(z_ref, cw_ref, cb_ref, lg_ref, lb_ref, pw_ref, ps_ref, c_ref, pm_ref, mix_ref, gbuf, pbuf):
        i = pl.program_id(0)

        @pl.when(i == 0)
        def _():
            gbuf[0:CONV_HALO, :] = jnp.zeros((CONV_HALO, CONV_DIM), F32)
            pbuf[0:POOL_HALO, :] = jnp.zeros((POOL_HALO, POOL_DIM), F32)

        @pl.when(i > 0)
        def _():
            gbuf[0:CONV_HALO, :] = gbuf[tm:tm + CONV_HALO, :]
            pbuf[0:POOL_HALO, :] = pbuf[tm:tm + POOL_HALO, :]

        av = z_ref[:, 0:CONV_DIM].astype(F32)
        ag = z_ref[:, CONV_DIM:2 * CONV_DIM].astype(F32)
        gbuf[CONV_HALO:CONV_HALO + tm, :] = av * jax.nn.sigmoid(ag)
        pbuf[POOL_HALO:POOL_HALO + tm, :] = z_ref[:, 2 * CONV_DIM:CP_IN].astype(F32)

        def conv_block(rb, carry):
            base = pl.multiple_of(rb * CONV_BLOCK, CONV_BLOCK)
            shifted = _row_shifts(gbuf[pl.ds(base, CONV_BLOCK + CONV_HALO), :])
            acc = jnp.zeros((CONV_BLOCK, CONV_DIM), F32)
            for k in range(CONV_WIDTH):
                whole, part = divmod(CONV_HALO - (CONV_WIDTH - 1) + k, 8)
                acc = acc + cw_ref[k:k + 1, :] * shifted[part][8 * whole:8 * whole + CONV_BLOCK, :]
            c_ref[pl.ds(base, CONV_BLOCK), :] = acc + cb_ref[...]
            return carry

        lax.fori_loop(0, tm // CONV_BLOCK, conv_block, 0)

        c = c_ref[...]
        mu = jnp.mean(c, axis=-1, keepdims=True)
        xc = c - mu
        ln = xc * lax.rsqrt(jnp.mean(xc * xc, axis=-1, keepdims=True) + EPS) * lg_ref[...] + lb_ref[...]
        row = i * tm + lax.broadcasted_iota(jnp.int32, (tm, 1), 0)
        mix_ref[:, 0:CONV_DIM] = jnp.where(row >= PAD_ROWS, _silu(ln), 0.0).astype(BF16)

        tpos = (row - PAD_ROWS + 1).astype(F32)
        for gi, wdw in enumerate(POOL_WINDOWS):
            lo = POOL_GROUP * gi
            run, step = pbuf[:, lo:lo + POOL_GROUP], 1
            cur = run[POOL_HALO:POOL_HALO + tm, :]
            while step < wdw:
                run = run + pltpu.roll(run, step, 0)
                step *= 2
            pm = (run[POOL_HALO:POOL_HALO + tm, :] / jnp.clip(tpos, 1.0, float(wdw)) - cur).astype(BF16)
            pm_ref[:, lo:lo + POOL_GROUP] = pm
            pg = _dot(pm, pw_ref[gi].astype(BF16))
            mix_ref[:, CONV_DIM + lo:CONV_DIM + lo + POOL_GROUP] = (pg * ps_ref[:, lo:lo + POOL_GROUP]).astype(BF16)

    vec = pl.BlockSpec((1, CONV_DIM), lambda i: (0, 0))
    return pl.pallas_call(
        body, grid=(t // tm,),
        in_specs=[pl.BlockSpec((tm, CP_IN), lambda i: (i, 0)),
                  pl.BlockSpec((CONV_WIDTH, CONV_DIM), lambda i: (0, 0)), vec, vec, vec,
                  pl.BlockSpec((len(POOL_WINDOWS), POOL_GROUP, POOL_GROUP), lambda i: (0, 0, 0)), vec],
        out_specs=[pl.BlockSpec((tm, CONV_DIM), lambda i: (i, 0)), pl.BlockSpec((tm, POOL_DIM), lambda i: (i, 0)),
                   pl.BlockSpec((tm, CONV_DIM + POOL_DIM), lambda i: (i, 0))],
        out_shape=[jax.ShapeDtypeStruct((t, CONV_DIM), F32), jax.ShapeDtypeStruct((t, POOL_DIM), BF16),
                   jax.ShapeDtypeStruct((t, CONV_DIM + POOL_DIM), BF16)],
        scratch_shapes=[pltpu.VMEM((tm + CONV_HALO, CONV_DIM), F32), pltpu.VMEM((tm + POOL_HALO, POOL_DIM), F32)],
        compiler_params=_cparams("arbitrary"), name="cp_seq_fwd")(z, conv_w, conv_b, ln_g, ln_b, pool_w, pool_scale)


def _cp_seq_bwd(dmix, z, c, pm, conv_w, ln_g, ln_b, pool_w, pool_scale, dep=None):
    t = z.shape[0]
    tm = _row_tile(t, ROW_TILE_TARGET, CHUNK)
    nt = t // tm

    def body(dmix_ref, z_ref, c_ref, pm_ref, cw_ref, lg_ref, lb_ref, pw_ref, ps_ref,
             dz_ref, dcw_ref, dvec_ref, dpw_ref, dcbuf, qbuf, glu_buf, dwacc):
        i = pl.program_id(0)
        tile = nt - 1 - i

        @pl.when(i == 0)
        def _():
            dcbuf[tm:tm + CONV_HALO, :] = jnp.zeros((CONV_HALO, CONV_DIM), F32)
            qbuf[tm:tm + POOL_HALO, :] = jnp.zeros((POOL_HALO, POOL_DIM), F32)
            dcw_ref[...] = jnp.zeros_like(dcw_ref)
            dwacc[...] = jnp.zeros_like(dwacc)
            dvec_ref[...] = jnp.zeros_like(dvec_ref)
            dpw_ref[...] = jnp.zeros_like(dpw_ref)

        @pl.when(i > 0)
        def _():
            dcbuf[tm:tm + CONV_HALO, :] = dcbuf[0:CONV_HALO, :]
            qbuf[tm:tm + POOL_HALO, :] = qbuf[0:POOL_HALO, :]

        row = tile * tm + lax.broadcasted_iota(jnp.int32, (tm, 1), 0)
        cv = c_ref[...]
        mu = jnp.mean(cv, axis=-1, keepdims=True)
        xc = cv - mu
        rstd = lax.rsqrt(jnp.mean(xc * xc, axis=-1, keepdims=True) + EPS)
        xhat = xc * rstd
        ln = xhat * lg_ref[...] + lb_ref[...]
        sg = jax.nn.sigmoid(ln)
        da = jnp.where(row >= PAD_ROWS, dmix_ref[:, 0:CONV_DIM].astype(F32), 0.0)
        dln = da * (sg * (1.0 + ln * (1.0 - sg)))
        dxh = dln * lg_ref[...]
        dc = rstd * (dxh - jnp.mean(dxh, axis=-1, keepdims=True) - xhat * jnp.mean(dxh * xhat, axis=-1, keepdims=True))
        dcbuf[0:tm, :] = dc
        dvec_ref[0:1, :] += jnp.sum(dc, axis=0, keepdims=True)
        dvec_ref[1:2, :] += jnp.sum(dln * xhat, axis=0, keepdims=True)
        dvec_ref[2:3, :] += jnp.sum(dln, axis=0, keepdims=True)

        av = z_ref[:, 0:CONV_DIM].astype(F32)
        sig_g = jax.nn.sigmoid(z_ref[:, CONV_DIM:2 * CONV_DIM].astype(F32))
        glu_buf[...] = av * sig_g

        def conv_block(rb, carry):
            base = pl.multiple_of(rb * CONV_BLOCK, CONV_BLOCK)
            shifted = _row_shifts(dcbuf[pl.ds(base, CONV_BLOCK + CONV_HALO), :])
            glu = glu_buf[pl.ds(base, CONV_BLOCK), :]
            acc = jnp.zeros((CONV_BLOCK, CONV_DIM), F32)
            for k in range(CONV_WIDTH):
                whole, part = divmod(CONV_WIDTH - 1 - k, 8)
                slab = shifted[part][8 * whole:8 * whole + CONV_BLOCK, :]
                acc = acc + cw_ref[k:k + 1, :] * slab
                prod = slab * glu
                part = prod[0:8]
                for q in range(1, CONV_BLOCK // 8):
                    part = part + prod[8 * q:8 * q + 8]
                dwacc[k] += part
            glu_buf[pl.ds(base, CONV_BLOCK), :] = acc
            return carry

        lax.fori_loop(0, tm // CONV_BLOCK, conv_block, 0)

        @pl.when(i == nt - 1)
        def _():
            for k in range(CONV_WIDTH):
                dcw_ref[k:k + 1, :] = jnp.sum(dwacc[k], axis=0, keepdims=True)
        dglu = glu_buf[...]
        dz_ref[:, 0:CONV_DIM] = (dglu * sig_g).astype(BF16)
        dz_ref[:, CONV_DIM:2 * CONV_DIM] = (dglu * av * sig_g * (1.0 - sig_g)).astype(BF16)

        tpos = (row - PAD_ROWS + 1).astype(F32)
        for gi, wdw in enumerate(POOL_WINDOWS):
            lo = POOL_GROUP * gi
            dp = dmix_ref[:, CONV_DIM + lo:CONV_DIM + lo + POOL_GROUP].astype(F32)
            pmv = pm_ref[:, lo:lo + POOL_GROUP]
            pwb = pw_ref[gi].astype(BF16)
            dvec_ref[3:4, lo:lo + POOL_GROUP] += jnp.sum(dp * _dot(pmv, pwb), axis=0, keepdims=True)
            dq = (dp * ps_ref[:, lo:lo + POOL_GROUP]).astype(BF16)
            dpw_ref[gi] += _dot_tn(pmv, dq)
            dpm = _dot_nt(dq, pwb)
            qbuf[0:tm, lo:lo + POOL_GROUP] = dpm / jnp.clip(tpos, 1.0, float(wdw))
            run, step = qbuf[:, lo:lo + POOL_GROUP], 1
            while step < wdw:
                run = run + pltpu.roll(run, tm + POOL_HALO - step, 0)
                step *= 2
            dz_ref[:, 2 * CONV_DIM + lo:2 * CONV_DIM + lo + POOL_GROUP] = (run[0:tm, :] - dpm).astype(BF16)

    vec = pl.BlockSpec((1, CONV_DIM), lambda i: (0, 0))
    rev = lambda i: (nt - 1 - i, 0)
    return _call_after(
        dep, body, 9,
        [pl.BlockSpec((tm, CONV_DIM + POOL_DIM), rev), pl.BlockSpec((tm, CP_IN), rev),
         pl.BlockSpec((tm, CONV_DIM), rev), pl.BlockSpec((tm, POOL_DIM), rev),
         pl.BlockSpec((CONV_WIDTH, CONV_DIM), lambda i: (0, 0)), vec, vec,
         pl.BlockSpec((len(POOL_WINDOWS), POOL_GROUP, POOL_GROUP), lambda i: (0, 0, 0)), vec],
        (dmix, z, c, pm, conv_w, ln_g, ln_b, pool_w, pool_scale), grid=(nt,),
        out_specs=[pl.BlockSpec((tm, CP_IN), rev), pl.BlockSpec((CONV_WIDTH + 1, CONV_DIM), lambda i: (0, 0)),
                   pl.BlockSpec((8, CONV_DIM), lambda i: (0, 0)),
                   pl.BlockSpec((len(POOL_WINDOWS), POOL_GROUP, POOL_GROUP), lambda i: (0, 0, 0))],
        out_shape=[jax.ShapeDtypeStruct((t, CP_IN), BF16), jax.ShapeDtypeStruct((CONV_WIDTH + 1, CONV_DIM), F32),
                   jax.ShapeDtypeStruct((8, CONV_DIM), F32),
                   jax.ShapeDtypeStruct((len(POOL_WINDOWS), POOL_GROUP, POOL_GROUP), F32)],
        scratch_shapes=[pltpu.VMEM((tm + CONV_HALO, CONV_DIM), F32), pltpu.VMEM((tm + POOL_HALO, POOL_DIM), F32),
                        pltpu.VMEM((tm, CONV_DIM), F32), pltpu.VMEM((CONV_WIDTH + 1, 8, CONV_DIM), F32)],
        compiler_params=_cparams("arbitrary"), name="cp_seq_bwd")


Q0, K0, V0, G0, R0 =0, GLA_DK, 2 * GLA_DK, 2 * GLA_DK + GLA_DV, 2 * GLA_DK + 2 * GLA_DV


def _split3(x):
    hi = x.astype(BF16)
    r1 = x - hi.astype(F32)
    mid = r1.astype(BF16)
    lo = (r1 - mid.astype(F32)).astype(BF16)
    return hi, mid, lo


def _tri(strict):
    r = lax.broadcasted_iota(jnp.int32, (CHUNK, CHUNK), 0)
    c = lax.broadcasted_iota(jnp.int32, (CHUNK, CHUNK), 1)
    return ((r > c) if strict else (r >= c)).astype(BF16)


def _chunk_sums(x, cpt, strict, pieces):
    tri3 = jnp.broadcast_to(_tri(strict)[None], (cpt, CHUNK, CHUNK))
    acc = None
    for piece in _split3(x.reshape(cpt, CHUNK, x.shape[-1]))[:pieces]:
        part = jnp.einsum("bij,bjk->bik", tri3, piece, preferred_element_type=F32)
        acc = part if acc is None else acc + part
    return acc


def _chunk_decay(r, gw_ref, gb_ref, cpt):
    pre = _dot(r, gw_ref[...]) + gb_ref[...]
    lac = (jnp.minimum(pre, 0.0) - jnp.log(1.0 + jnp.exp(-jnp.abs(pre)))) * (1.0 / GATE_NORM)
    cum3 = _chunk_sums(lac, cpt, False, 3)
    return cum3, cum3[:, CHUNK - 1:CHUNK, :]


def _gla_seq_fwd(z, gate_w, gate_b, head_g, dep=None):
    t = z.shape[0]
    tm = _row_tile(t, ROW_TILE_TARGET, CHUNK)
    cpt = tm // CHUNK
    scale = GLA_HK ** -0.5

    def body(z_ref, gw_ref, gb_ref, hg_ref, o_ref, mix_ref, st_ref, state, kdec_s, e_s):
        @pl.when(pl.program_id(0) == 0)
        def _():
            state[...] = jnp.zeros_like(state)

        cum3, tot3 = _chunk_decay(z_ref[:, R0:R0 + GATE_PAD], gw_ref, gb_ref, cpt)
        dec = jnp.exp(jnp.broadcast_to(tot3, cum3.shape) - cum3).reshape(tm, GLA_DK)
        kdec_s[...] = (z_ref[:, K0:K0 + GLA_DK].astype(F32) * dec).astype(BF16)
        e_s[...] = jnp.exp(jnp.broadcast_to(tot3, (cpt, 8, GLA_DK))).reshape(cpt * 8, GLA_DK)

        def chunk(ci, carry):
            rows = pl.ds(pl.multiple_of(ci * CHUNK, CHUNK), CHUNK)
            e_all = e_s[pl.ds(pl.multiple_of(ci * 8, 8), 8), :][0:1, :]
            st_ref[ci] = state[...].astype(BF16)
            for hd in range(GLA_HEADS):
                ks = slice(hd * GLA_HK, (hd + 1) * GLA_HK)
                vs = slice(hd * GLA_HV, (hd + 1) * GLA_HV)
                v = z_ref[rows, V0 + hd * GLA_HV:V0 + (hd + 1) * GLA_HV]
                st = state[vs, :] * e_all[:, ks] + _dot_tn(v, kdec_s[rows, ks])
                state[vs, :] = st
                q = z_ref[rows, Q0 + hd * GLA_HK:Q0 + (hd + 1) * GLA_HK]
                o_ref[rows, vs] = (_dot_nt(q, st.astype(BF16)) * scale).astype(BF16)
            return carry

        lax.fori_loop(0, cpt, chunk, 0, unroll=cpt)

        for hd in range(GLA_HEADS):
            vs = slice(hd * GLA_HV, (hd + 1) * GLA_HV)
            on = _rms(o_ref[:, vs].astype(F32), hg_ref[...])
            gv = z_ref[:, G0 + hd * GLA_HV:G0 + (hd + 1) * GLA_HV].astype(F32)
            mix_ref[:, vs] = (on * _silu(gv)).astype(BF16)

    return _call_after(
        dep, body, 4,
        [pl.BlockSpec((tm, GLA_IN_PAD), lambda i: (i, 0)),
         pl.BlockSpec((GATE_PAD, GLA_DK), lambda i: (0, 0)), pl.BlockSpec((1, GLA_DK), lambda i: (0, 0)),
         pl.BlockSpec((1, GLA_HV), lambda i: (0, 0))], (z, gate_w, gate_b, head_g), grid=(t // tm,),
        out_specs=[pl.BlockSpec((tm, GLA_DV), lambda i: (i, 0)), pl.BlockSpec((tm, GLA_DV), lambda i: (i, 0)),
                   pl.BlockSpec((cpt, GLA_DV, GLA_HK), lambda i: (i, 0, 0))],
        out_shape=[jax.ShapeDtypeStruct((t, GLA_DV), BF16), jax.ShapeDtypeStruct((t, GLA_DV), BF16),
                   jax.ShapeDtypeStruct((t // CHUNK, GLA_DV, GLA_HK), BF16)],
        scratch_shapes=[pltpu.VMEM((GLA_DV, GLA_HK), F32), pltpu.VMEM((tm, GLA_DK), BF16),
                        pltpu.VMEM((cpt * 8, GLA_DK), F32)],
        compiler_params=_cparams("arbitrary"), name="gla_seq_fwd")


def _gla_seq_bwd(dmix, o, z, states, gate_w, gate_b, head_g, dep=None):
    t = z.shape[0]
    tm = _row_tile(t, ROW_TILE_TARGET, CHUNK)
    cpt = tm // CHUNK
    nt = t // tm
    scale = GLA_HK ** -0.5

    def body(dmix_ref, o_ref, z_ref, st_ref, gw_ref, gb_ref, hg_ref, dz_ref, dgw_ref, dgb_ref, dhg_ref,
             dstate, dec_s, kdec_s, dkdec_s, do_s, e_s, dtot_s):
        @pl.when(pl.program_id(0) == 0)
        def _():
            dstate[...] = jnp.zeros_like(dstate)
            dgw_ref[...] = jnp.zeros_like(dgw_ref)
            dgb_ref[...] = jnp.zeros_like(dgb_ref)
            dhg_ref[...] = jnp.zeros_like(dhg_ref)

        cum3, tot3 = _chunk_decay(z_ref[:, R0:R0 + GATE_PAD], gw_ref, gb_ref, cpt)
        dec = jnp.exp(jnp.broadcast_to(tot3, cum3.shape) - cum3).reshape(tm, GLA_DK)
        dec_s[...] = dec
        kdec = z_ref[:, K0:K0 + GLA_DK].astype(F32) * dec
        kdec_s[...] = kdec
        e3 = jnp.exp(tot3)
        e_s[...] = jnp.broadcast_to(e3, (cpt, 8, GLA_DK)).reshape(cpt * 8, GLA_DK)
        dhg = jnp.zeros((1, GLA_HV), F32)
        for hd in range(GLA_HEADS):
            ks = slice(hd * GLA_HK, (hd + 1) * GLA_HK)
            vs = slice(hd * GLA_HV, (hd + 1) * GLA_HV)
            gcols = slice(G0 + hd * GLA_HV, G0 + (hd + 1) * GLA_HV)
            ov = o_ref[:, vs].astype(F32)
            gv = z_ref[:, gcols].astype(F32)
            dm = dmix_ref[:, vs].astype(F32)
            sg = jax.nn.sigmoid(gv)
            rr = lax.rsqrt(jnp.mean(ov * ov, axis=-1, keepdims=True) + EPS)
            xhat = ov * rr
            don = dm * (gv * sg)
            dz_ref[:, gcols] = (dm * (xhat * hg_ref[...]) * (sg * (1.0 + gv * (1.0 - sg)))).astype(BF16)
            dhg = dhg + jnp.sum(don * xhat, axis=0, keepdims=True)
            dxh = don * hg_ref[...]
            do = (rr * (dxh - xhat * jnp.mean(dxh * xhat, axis=-1, keepdims=True)) * scale).astype(BF16)
            do_s[:, vs] = do
            v3 = z_ref[:, V0 + hd * GLA_HV:V0 + (hd + 1) * GLA_HV].reshape(cpt, CHUNK, GLA_HV)
            kdb3 = kdec[:, ks].astype(BF16).reshape(cpt, CHUNK, GLA_HK)
            st3 = st_ref[:, vs, :].astype(F32) * e3[:, :, ks] + jnp.einsum("bcv,bck->bvk", v3, kdb3,
                                                                            preferred_element_type=F32)
            dq3 = jnp.einsum("bcv,bvk->bck", do.reshape(cpt, CHUNK, GLA_HV), st3.astype(BF16), preferred_element_type=F32)
            dz_ref[:, Q0 + hd * GLA_HK:Q0 + (hd + 1) * GLA_HK] = dq3.reshape(tm, GLA_HK).astype(BF16)
        dhg_ref[...] += dhg

        def chunk(cj, carry):
            ci = cpt - 1 - cj
            rows = pl.ds(pl.multiple_of(ci * CHUNK, CHUNK), CHUNK)
            erows = pl.ds(pl.multiple_of(ci * 8, 8), 8)
            e_all = e_s[erows, :][0:1, :]
            for hd in range(GLA_HEADS):
                ks = slice(hd * GLA_HK, (hd + 1) * GLA_HK)
                vs = slice(hd * GLA_HV, (hd + 1) * GLA_HV)
                e = e_all[:, ks]
                kdb = kdec_s[rows, ks].astype(BF16)
                v = z_ref[rows, V0 + hd * GLA_HV:V0 + (hd + 1) * GLA_HV]
                q = z_ref[rows, Q0 + hd * GLA_HK:Q0 + (hd + 1) * GLA_HK]
                do = do_s[rows, vs]
                st_prev = st_ref[ci, vs, :].astype(F32)
                dst = dstate[vs, :] + _dot_tn(do, q)
                dstb = dst.astype(BF16)
                dkdec_s[rows, ks] = _dot(v, dstb)
                dz_ref[rows, V0 + hd * GLA_HV:V0 + (hd + 1) * GLA_HV] = _dot_nt(kdb, dstb).astype(BF16)
                dtot = jnp.sum(dst * st_prev, axis=0, keepdims=True) * e
                dtot_s[erows, ks] = jnp.broadcast_to(dtot, (8, GLA_HK))
                dstate[vs, :] = dst * e
            return carry

        lax.fori_loop(0, cpt, chunk, 0, unroll=cpt)

        dkdec = dkdec_s[...]
        dz_ref[:, K0:K0 + GLA_DK] = (dkdec * dec_s[...]).astype(BF16)
        before = _chunk_sums(dkdec * kdec_s[...], cpt, True, 2)
        dtot3 = dtot_s[...].reshape(cpt, 8, GLA_DK)[:, 0:1, :]
        dlac = (jnp.broadcast_to(dtot3, before.shape) + before).reshape(tm, GLA_DK)
        pre = _dot(z_ref[:, R0:R0 + GATE_PAD], gw_ref[...]) + gb_ref[...]
        dpre = dlac * (1.0 / GATE_NORM) * (1.0 - jax.nn.sigmoid(pre))
        dpb = dpre.astype(BF16)
        dz_ref[:, R0:R0 + GATE_PAD] = _dot_nt(dpb, gw_ref[...]).astype(BF16)
        dgw_ref[...] += _dot_tn(z_ref[:, R0:R0 + GATE_PAD], dpb)
        dgb_ref[...] += jnp.sum(dpre, axis=0, keepdims=True)

    rev = lambda i: (nt - 1 - i, 0)
    return _call_after(
        dep, body, 7,
        [pl.BlockSpec((tm, GLA_DV), rev), pl.BlockSpec((tm, GLA_DV), rev), pl.BlockSpec((tm, GLA_IN_PAD), rev),
         pl.BlockSpec((cpt, GLA_DV, GLA_HK), lambda i: (nt - 1 - i, 0, 0)),
         pl.BlockSpec((GATE_PAD, GLA_DK), lambda i: (0, 0)), pl.BlockSpec((1, GLA_DK), lambda i: (0, 0)),
         pl.BlockSpec((1, GLA_HV), lambda i: (0, 0))],
        (dmix, o, z, states, gate_w, gate_b, head_g), grid=(nt,),
        out_specs=[pl.BlockSpec((tm, GLA_IN_PAD), rev), pl.BlockSpec((GATE_PAD, GLA_DK), lambda i: (0, 0)),
                   pl.BlockSpec((1, GLA_DK), lambda i: (0, 0)), pl.BlockSpec((1, GLA_HV), lambda i: (0, 0))],
        out_shape=[jax.ShapeDtypeStruct((t, GLA_IN_PAD), BF16), jax.ShapeDtypeStruct((GATE_PAD, GLA_DK), F32),
                   jax.ShapeDtypeStruct((1, GLA_DK), F32), jax.ShapeDtypeStruct((1, GLA_HV), F32)],
        scratch_shapes=[pltpu.VMEM((GLA_DV, GLA_HK), F32), pltpu.VMEM((tm, GLA_DK), F32), pltpu.VMEM((tm, GLA_DK), F32),
                        pltpu.VMEM((tm, GLA_DK), F32), pltpu.VMEM((tm, GLA_DV), BF16),
                        pltpu.VMEM((cpt * 8, GLA_DK), F32), pltpu.VMEM((cpt * 8, GLA_DK), F32)],
        compiler_params=_cparams("arbitrary"), name="gla_seq_bwd")


def _sum_slots(x, name):
    n, r, cdim = x.shape
    tr = _row_tile(r, 256, 8)

    def body(x_ref, o_ref):
        acc = x_ref[0].astype(F32)
        for j in range(1, n):
            acc = acc + x_ref[j].astype(F32)
        o_ref[...] = acc

    return pl.pallas_call(
        body, grid=(r // tr,),
        in_specs=[pl.BlockSpec((n, tr, cdim), lambda i: (0, i, 0))],
        out_specs=pl.BlockSpec((tr, cdim), lambda i: (i, 0)),
        out_shape=jax.ShapeDtypeStruct((r, cdim), F32),
        compiler_params=_cparams("parallel"), name=name)(x)


def _sum_own_and_slots(own, slots, dev_idx, name):
    _, _, r, cdim = own.shape
    n = slots.shape[0]
    tr = _row_tile(r, 512, 8)

    def body(s_ref, own_ref, *rest):
        acc = own_ref[...].astype(F32)
        for other in rest[:n - 1]:
            acc = acc + other[...].astype(F32)
        rest[n - 1][...] = acc

    def slot(dd):
        return pl.BlockSpec((None, tr, cdim), lambda i, s: ((s[0] + dd) % n, i, 0))

    mine = pl.BlockSpec((None, None, tr, cdim), lambda i, s: (s[0] // 2, s[0] % 2, i, 0))
    return pl.pallas_call(
        body,
        grid_spec=pltpu.PrefetchScalarGridSpec(
            num_scalar_prefetch=1, grid=(r // tr,), in_specs=[mine] + [slot(dd) for dd in range(1, n)],
            out_specs=pl.BlockSpec((tr, cdim), lambda i, s: (i, 0))),
        out_shape=jax.ShapeDtypeStruct((r, cdim), F32),
        compiler_params=_cparams("parallel"), name=name)(dev_idx, own, *([slots] * (n - 1)))


def _add2(a, b, name):
    r, cdim = a.shape
    tr = _row_tile(r, 256, 8)

    def body(a_ref, b_ref, o_ref):
        o_ref[...] = a_ref[...] + b_ref[...]

    spec = pl.BlockSpec((tr, cdim), lambda i: (i, 0))
    return pl.pallas_call(body, grid=(r // tr,), in_specs=[spec, spec], out_specs=spec,
                          out_shape=jax.ShapeDtypeStruct((r, cdim), F32),
                          compiler_params=_cparams("parallel"), name=name)(a, b)


def _adamw_half(w, gs, m, v, half_idx, prev, name, dep=None):
    nl, _, h, cdim = w.shape
    tr = _row_tile(h, 512, 8)
    nprev = 0 if prev is None else 4
    extra = [] if dep is None else [dep]

    def body(s_ref, w_ref, m_ref, v_ref, *rest):
        g_refs = rest[:nl]
        go_ref, d_ref, mo_ref, vo_ref = rest[nl + nprev + len(extra):]
        layer = pl.program_id(0)
        gv = g_refs[0][...]
        for j in range(1, nl):
            gv = jnp.where(layer == j, g_refs[j][...], gv)
        go_ref[...] = gv
        mn = ADAM_B1 * m_ref[...] + (1.0 - ADAM_B1) * gv
        vn = ADAM_B2 * v_ref[...] + (1.0 - ADAM_B2) * (gv * gv)
        m_hat = mn / (1.0 - ADAM_B1 ** ADAM_STEP)
        v_hat = vn / (1.0 - ADAM_B2 ** ADAM_STEP)
        d_ref[...] = -ADAM_LR * (m_hat / (jnp.sqrt(v_hat) + ADAM_EPS) + ADAM_WD * w_ref[...])
        mo_ref[...] = mn
        vo_ref[...] = vn

    half = pl.BlockSpec((None, None, tr, cdim), lambda l, i, s: (l, s[0], i, 0))

    def of_layer(j):
        return pl.BlockSpec((tr, cdim), lambda l, i, s: (jnp.where(l == j, i, 0), 0))

    shp = jax.ShapeDtypeStruct(w.shape, F32)
    return pl.pallas_call(
        body,
        grid_spec=pltpu.PrefetchScalarGridSpec(
            num_scalar_prefetch=1, grid=(nl, h // tr),
            in_specs=[half] * 3 + [of_layer(j) for j in range(nl)] + [ANY_SPEC] * (nprev + len(extra)),
            out_specs=[half] * 4),
        out_shape=[shp] * 4, input_output_aliases={4 + nl + k: k for k in range(nprev)},
        compiler_params=_cparams("arbitrary", "arbitrary"), name=name,
    )(half_idx, w, m, v, *gs, *([] if prev is None else prev), *extra)


def _adamw_many(ws, gs, ms, vs):
    n = len(ws)

    def body(*refs):
        for i in range(n):
            w_ref, g_ref, m_ref, v_ref = refs[i], refs[n + i], refs[2 * n + i], refs[3 * n + i]
            d_ref, mo_ref, vo_ref = refs[4 * n + i], refs[5 * n + i], refs[6 * n + i]
            gv = g_ref[...]
            mn = ADAM_B1 * m_ref[...] + (1.0 - ADAM_B1) * gv
            vn = ADAM_B2 * v_ref[...] + (1.0 - ADAM_B2) * (gv * gv)
            m_hat = mn / (1.0 - ADAM_B1 ** ADAM_STEP)
            v_hat = vn / (1.0 - ADAM_B2 ** ADAM_STEP)
            d_ref[...] = -ADAM_LR * (m_hat / (jnp.sqrt(v_hat) + ADAM_EPS) + ADAM_WD * w_ref[...])
            mo_ref[...] = mn
            vo_ref[...] = vn

    shapes = [jax.ShapeDtypeStruct(w.shape, F32) for w in ws]
    outs = pl.pallas_call(body, out_shape=shapes * 3, name="adamw_small")(*ws, *gs, *ms, *vs)
    return outs[:n], outs[n:2 * n], outs[2 * n:]


def _split_rows(a):
    return a.reshape(a.shape[0], 2, a.shape[1] // 2, a.shape[2])


def _place():
    x, y, c = lax.axis_index("x"), lax.axis_index("y"), lax.axis_index("c")
    chips = [(1 - x, y), (x, 1 - y), (1 - x, 1 - y)]
    return x, y, c, chips


def _remote(src, dst, send_sem, recv_sem, to):
    return pltpu.make_async_remote_copy(src_ref=src, dst_ref=dst, send_sem=send_sem, recv_sem=recv_sem,
                                        device_id=to, device_id_type=MESH)


def _plan_gather(n_halved):
    def plan(src_refs, land_refs):
        x, y, c, chips = _place()
        me = 2 * x + y
        copies = []
        for k, (src, land) in enumerate(zip(src_refs, land_refs)):
            for (px, py) in chips:
                frm = 2 * px + py
                if k < n_halved:
                    copies.append((src.at[c], land.at[me, c], (px, py, c), land.at[frm, c]))
                else:
                    copies.append((src, land.at[me], (px, py, c), land.at[frm]))
        return copies
    return plan


def _plan_share(src_refs, land_refs):
    x, y, c, chips = _place()
    me = 2 * x + y
    sib = (x, y, 1 - c)
    copies = []
    for src, land in zip(src_refs, land_refs):
        copies.append((src, land.at[me], sib, land.at[me]))
        for (px, py) in chips:
            frm = 2 * px + py
            copies.append((land.at[frm, c], land.at[frm, c], sib, land.at[frm, 1 - c]))
    return copies


def _plan_scatter(n_parts):
    def plan(src_refs, land_refs):
        x, y, c, chips = _place()
        me = 2 * x + y
        copies = []
        for k, (src, land) in enumerate(zip(src_refs, land_refs)):
            for (px, py) in chips:
                to = 2 * px + py
                copies.append((src.at[to] if k < n_parts else src, land.at[me], (px, py, c), land.at[to]))
        return copies
    return plan


N_DEVICES = 8
OTHER_DEVICES = [(dx, dy, dc) for dx in (0, 1) for dy in (0, 1) for dc in (0, 1) if dx or dy or dc]


def _plan_scatter_all(src_refs, land_refs):
    x, y, c, _ = _place()
    me = 4 * x + 2 * y + c
    copies = []
    for src, land in zip(src_refs, land_refs):
        for dx, dy, dc in OTHER_DEVICES:
            px, py, pc = (1 - x if dx else x), (1 - y if dy else y), (1 - c if dc else c)
            copies.append((src.at[2 * px + py, pc], land.at[me], (px, py, pc), land.at[4 * px + 2 * py + pc]))
    return copies


def _plan_exchange(n_split):
    def plan(src_refs, land_refs):
        x, y, c, _ = _place()
        sib = (x, y, 1 - c)
        return [(src.at[:, 1 - c] if k < n_split else src, land, sib, land)
                for k, (src, land) in enumerate(zip(src_refs, land_refs))]
    return plan


def _hbm(a):
    return pltpu.HBM(a.shape, a.dtype)


def _start_copies(name, srcs, lands, plan, ncopy, dep=None):
    ns, nl = len(srcs), len(lands)
    nin = ns + nl + (0 if dep is None else 1)

    def body(*refs):
        send_sems, recv_sems, token = refs[nin], refs[nin + 1], refs[-1]
        for k, (src, dst, dev, _) in enumerate(plan(refs[:ns], refs[ns:ns + nl])):
            _remote(src, dst, send_sems.at[k], recv_sems.at[k], dev).start()
        token[...] = jnp.zeros_like(token)

    args = [pltpu.with_memory_space_constraint(a, pltpu.HBM) for a in list(srcs) + list(lands)]
    outs = pl.pallas_call(
        body, name=name,
        out_shape=(pltpu.SemaphoreType.DMA((ncopy,)), pltpu.SemaphoreType.DMA((ncopy,)),
                   *[_hbm(a) for a in list(srcs) + list(lands)], jax.ShapeDtypeStruct((8, 128), F32)),
        in_specs=[HBM_SPEC] * (ns + nl) + ([] if dep is None else [ANY_SPEC]),
        out_specs=(SEM_SPEC, SEM_SPEC, *([HBM_SPEC] * (ns + nl)), pl.BlockSpec(memory_space=pltpu.VMEM)),
        input_output_aliases={i: 2 + i for i in range(ns + nl)},
        compiler_params=pltpu.CompilerParams(has_side_effects=SIDE_EFFECT),
    )(*args, *([] if dep is None else [dep]))
    return outs[0], outs[1], list(outs[2:2 + ns]), list(outs[2 + ns:2 + ns + nl]), outs[-1]


def _wait_copies(name, started, plan, after, sem_offset=0):
    send_sems, recv_sems, srcs, lands, _ = started
    ns, nl = len(srcs), len(lands)
    after = list(after) if isinstance(after, (list, tuple)) else [after]

    def body(*refs):
        send_ref, recv_ref = refs[ns + nl], refs[ns + nl + 1]
        for k, (src, _, dev, mine) in enumerate(plan(refs[:ns], refs[ns:ns + nl])):
            copy = _remote(src, mine, send_ref.at[sem_offset + k], recv_ref.at[sem_offset + k], dev)
            copy.wait_send()
            copy.wait_recv()

    outs = pl.pallas_call(
        body, name=name, out_shape=tuple(_hbm(a) for a in srcs + lands),
        in_specs=[HBM_SPEC] * (ns + nl) + [SEM_SPEC, SEM_SPEC] + [ANY_SPEC] * len(after),
        out_specs=tuple([HBM_SPEC] * (ns + nl)),
        input_output_aliases={i: i for i in range(ns + nl)},
        compiler_params=pltpu.CompilerParams(has_side_effects=SIDE_EFFECT),
    )(*srcs, *lands, send_sems, recv_sems, *after)
    return list(outs[:ns]), list(outs[ns:])


def _share_with_sibling(name, srcs, lands):
    n = len(srcs)

    def body(*refs):
        src_refs, land_refs, out_refs = refs[:n], refs[n:2 * n], refs[2 * n:3 * n]
        send_sem, recv_sem = refs[3 * n:]
        x, y, c, chips = _place()
        me = 2 * x + y
        sib = (x, y, 1 - c)
        sends, recvs = [], []
        for k in range(n):
            sems = (send_sem.at[4 * k], recv_sem.at[4 * k])
            sends.append(_remote(src_refs[k], out_refs[k].at[me], *sems, sib))
            recvs.append(_remote(src_refs[k], out_refs[k].at[me], *sems, sib))
            for j, (px, py) in enumerate(chips):
                frm = 2 * px + py
                sems = (send_sem.at[4 * k + 1 + j], recv_sem.at[4 * k + 1 + j])
                sends.append(_remote(land_refs[k].at[frm, c], out_refs[k].at[frm, c], *sems, sib))
                recvs.append(_remote(land_refs[k].at[frm, c], out_refs[k].at[frm, 1 - c], *sems, sib))
        for cp in sends:
            cp.start()
        for cp in recvs:
            cp.wait_recv()
        for cp in sends:
            cp.wait_send()

    return pl.pallas_call(
        body, name=name, in_specs=[HBM_SPEC] * (2 * n), out_specs=[HBM_SPEC] * n,
        out_shape=[jax.ShapeDtypeStruct(a.shape, a.dtype) for a in lands],
        input_output_aliases={n + k: k for k in range(n)},
        scratch_shapes=[pltpu.SemaphoreType.DMA((4 * n,)), pltpu.SemaphoreType.DMA((4 * n,))],
    )(*srcs, *lands)


def _pack(arrs):
    flat = jnp.concatenate([a.reshape(-1).astype(F32) for a in arrs])
    n = flat.shape[0]
    rows = -(-n // PACK_WIDTH)
    rows = -(-rows // 8) * 8
    return jnp.pad(flat, (0, rows * PACK_WIDTH - n)).reshape(rows, PACK_WIDTH)


def _unpack(buf, shapes):
    flat = buf.reshape(-1)
    out, off = [], 0
    for shp in shapes:
        n = 1
        for s in shp:
            n *= s
        out.append(flat[off:off + n].reshape(shp))
        off += n
    return out


def _unshard_cols(stacked):
    moved = jnp.moveaxis(stacked, 0, -2)
    return moved.reshape(moved.shape[:-2] + (moved.shape[-2] * moved.shape[-1],))


def _take_cols(blocks, start, width):
    bw = blocks.shape[2]
    pieces, lo = [], start
    while lo < start + width:
        b = lo // bw
        hi = min(start + width, (b + 1) * bw)
        pieces.append(blocks[b][:, lo - b * bw:hi - b * bw])
        lo = hi
    return jnp.concatenate(pieces, axis=1)


def _col_shard(full, s, width):
    return lax.dynamic_slice_in_dim(full, s * width, width, axis=full.ndim - 1)


def kernel(x, meta_tokens, mix_norm_g, ffn_norm_g, ffn_w1, ffn_w2, cp_w_in, cp_conv_w, cp_conv_b, cp_ln_g, cp_ln_b, cp_pool_w, cp_pool_scale, cp_w_out, gla_w_in, gla_gate_w2, gla_gate_b, gla_head_g, gla_w_out, final_norm_g, loss_target, m_meta_tokens, m_mix_norm_g, m_ffn_norm_g, m_ffn_w1, m_ffn_w2, m_cp_w_in, m_cp_conv_w, m_cp_conv_b, m_cp_ln_g, m_cp_ln_b, m_cp_pool_w, m_cp_pool_scale, m_cp_w_out, m_gla_w_in, m_gla_gate_w2, m_gla_gate_b, m_gla_head_g, m_gla_w_out, m_final_norm_g, v_meta_tokens, v_mix_norm_g, v_ffn_norm_g, v_ffn_w1, v_ffn_w2, v_cp_w_in, v_cp_conv_w, v_cp_conv_b, v_cp_ln_g, v_cp_ln_b, v_cp_pool_w, v_cp_pool_scale, v_cp_w_out, v_gla_w_in, v_gla_gate_w2, v_gla_gate_b, v_gla_head_g, v_gla_w_out, v_final_norm_g):
    d = D_MODEL
    chip = 2 * lax.axis_index("x") + lax.axis_index("y")
    core = lax.axis_index("c")
    seq = x.shape[1]
    t = seq + CHUNK

    sharded_small = [meta_tokens, cp_conv_w, gla_gate_w2, gla_gate_b, gla_head_g]

    def halves(w, token=None):
        if token is not None:
            w = w + token[0, 0]
        return w.astype(BF16).reshape(2, w.shape[0] // 2, w.shape[1])

    def unhalve(g):
        return g.reshape(N_CHIPS, 2 * g.shape[2], g.shape[3])

    def gather_group(srcs, whole=()):
        lands = [lax.empty((N_CHIPS,) + s.shape, s.dtype) for s in srcs]
        for a in whole:
            lands.append(lax.dynamic_update_slice(jnp.zeros((N_CHIPS,) + a.shape, a.dtype), a[None], (chip,) + (0,) * a.ndim))
        return list(srcs) + list(whole), lands, _plan_gather(len(srcs)), len(srcs)

    def start_groups(name, groups, dep):
        bounds, all_srcs, all_lands = [], [], []
        for srcs, lands, _, _ in groups:
            bounds.append((len(all_srcs), len(all_srcs) + len(srcs)))
            all_srcs += srcs
            all_lands += lands

        def plan_all(src_refs, land_refs):
            return [cp for (lo, hi), group in zip(bounds, groups) for cp in group[2](src_refs[lo:hi], land_refs[lo:hi])]

        started = _start_copies(name, all_srcs, all_lands, plan_all, 3 * len(all_srcs), dep)
        return [(started, bound, group[2], group[3]) for bound, group in zip(bounds, groups)]

    def arrived(name, gather, after):
        started, (lo, hi), plan, n = gather
        mine = (started[0], started[1], started[2][lo:hi], started[3][lo:hi], started[4])
        srcs, lands = _wait_copies(name + "_wait", mine, plan, after, sem_offset=3 * lo)
        return srcs[:n], lands[:n], lands[n:]

    (cp_gather,) = start_groups("gather_cp_start", [gather_group([halves(cp_w_in[0]), halves(cp_w_out[0])],
                                                                 [_pack(sharded_small)])], None)
    tok = cp_gather[0][-1]
    ffn0_gather, gla_gather, ffn1_gather = start_groups(
        "gather_start", [gather_group([halves(ffn_w1[0], tok), halves(ffn_w2[0], tok)]),
                         gather_group([halves(gla_w_in[0], tok), halves(gla_w_out[0], tok)]),
                         gather_group([halves(ffn_w1[1], tok), halves(ffn_w2[1], tok)])], tok)
    h0_rows = jnp.concatenate([jnp.zeros((CHUNK, d), F32) + ffn0_gather[0][-1][0, 0], x[0]], axis=0)
    cp_srcs, cp_lands, (small_g,) = arrived("gather_cp", cp_gather, h0_rows)
    cpin_g, cpout_g = [unhalve(g) for g in _share_with_sibling("gather_cp_share", cp_srcs, cp_lands)]
    per_chip = [_unpack(small_g[j], [a.shape for a in sharded_small]) for j in range(N_CHIPS)]
    meta_f, conv_w_f, gate_w_f, gate_b_f, head_g_f = [
        jnp.concatenate([per_chip[j][i] for j in range(N_CHIPS)], axis=-1) for i in range(len(sharded_small))]
    conv_w_f, gate_w_f = conv_w_f[0], gate_w_f[0]
    w_cp_in = _unshard_cols(cpin_g)
    w_cp_out = cpout_g.reshape(CONV_DIM + POOL_DIM, d)
    gate_w_pad = jnp.pad(gate_w_f, ((0, GATE_PAD - GATE_RANK), (0, 0))).astype(BF16)
    row = lambda a: a.reshape(1, -1)
    c_idx = core.reshape(1).astype(jnp.int32)

    h0 = lax.dynamic_update_slice(h0_rows, meta_f, (PAD_ROWS, 0))
    z0, u0 = _norm_matmul(h0, row(mix_norm_g[0]), w_cp_in, 512, "cp_in_proj")
    c0, pm0, mix0 = _cp_seq_fwd(z0, conv_w_f, cp_conv_b, cp_ln_g, cp_ln_b, cp_pool_w[0], cp_pool_scale)
    ffn0_srcs, ffn0_lands, _ = arrived("gather_ffn0", ffn0_gather, mix0)
    ffn0_share = _start_copies("gather_ffn0_share_start", ffn0_srcs, ffn0_lands, _plan_share, 4 * len(ffn0_srcs))
    h1 = _matmul_residual(mix0, w_cp_out, h0, "cp_out_proj", dep=ffn0_share[-1])
    w1g0, w2g0 = [unhalve(g) for g in _wait_copies("gather_ffn0_share_wait", ffn0_share, _plan_share, h1)[1]]
    h2, hp0, uf0 = _ffn_fwd(h1, row(ffn_norm_g[0]), w1g0, w2g0, "ffn0_fwd")
    gla_srcs, gla_lands, _ = arrived("gather_gla", gla_gather, h2)
    glain_g, glaout_g = [unhalve(g) for g in _share_with_sibling("gather_gla_share", gla_srcs, gla_lands)]
    w_gla_in = jnp.concatenate([glain_g[j] for j in range(N_CHIPS)] + [jnp.zeros((d, GLA_IN_PAD - GLA_IN), BF16)], axis=1)
    w_gla_out = glaout_g.reshape(GLA_DV, d)
    z1, u2 = _norm_matmul(h2, row(mix_norm_g[1]), w_gla_in, GLA_COLS, "gla_in_proj")
    ffn1_srcs, ffn1_lands, _ = arrived("gather_ffn1", ffn1_gather, z1)
    ffn1_share = _start_copies("gather_ffn1_share_start", ffn1_srcs, ffn1_lands, _plan_share, 4 * len(ffn1_srcs))
    o1, mix1, states = _gla_seq_fwd(z1, gate_w_pad, gate_b_f, head_g_f, dep=ffn1_share[-1])
    h3 = _matmul_residual(mix1, w_gla_out, h2, "gla_out_proj")
    w1g1, w2g1 = [unhalve(g) for g in _wait_copies("gather_ffn1_share_wait", ffn1_share, _plan_share, h3)[1]]
    h4, hp1, uf1 = _ffn_fwd(h3, row(ffn_norm_g[1]), w1g1, w2g1, "ffn1_fwd")

    dev_idx = (2 * chip + core).reshape(1).astype(jnp.int32)

    def start_reduce(name, grads):
        srcs = [_split_rows(g) for g in grads]
        lands = [lax.empty((N_DEVICES,) + s.shape[2:], s.dtype) for s in srcs]
        return _start_copies(name + "_scatter_start", srcs, lands, _plan_scatter_all, len(OTHER_DEVICES) * len(srcs))

    def finish_reduce(name, started, after):
        srcs, lands = _wait_copies(name + "_scatter_wait", started, _plan_scatter_all, after)
        return [_sum_own_and_slots(s, l, dev_idx, "%s_slot_sum_%d" % (name, k)) for k, (s, l) in enumerate(zip(srcs, lands))]

    dh4, d_final_g, loss_part = _loss_bwd(h4, row(final_norm_g), loss_target[0])

    dh3, dhp1, d_ffn_g1 = _ffn_bwd_data(dh4, h3, row(ffn_norm_g[1]), hp1, w1g1, w2g1, "ffn1_bwd")
    dw1_1 = _wgrad(uf1, dhp1, N_CHIPS, d, d, False, True, False, "ffn1_dw1", rows=WGRAD_ROWS_BF16)
    dw2_1 = _wgrad(hp1, dh4, N_CHIPS, d, d, True, False, True, "ffn1_dw2")
    ffn1_reduce = start_reduce("ffn1", [dw1_1, dw2_1])

    dmix1 = _dgrad(dh3, w_gla_out, "gla_out_dgrad", dep=ffn1_reduce[-1])
    dw_gla_out = _wgrad(mix1, dh3, 1, GLA_DV, d, False, False, False, "gla_out_dw")
    dz1, d_gate_w, d_gate_b, d_head_g = _gla_seq_bwd(dmix1, o1, z1, states, gate_w_pad, gate_b_f, head_g_f)
    dh2, d_mix_g1 = _dgrad_norm_bwd(dz1, w_gla_in, h2, row(mix_norm_g[1]), dh3, GLA_COLS, "gla_in_dgrad")
    dw_gla_in = _wgrad(u2, dz1, GLA_IN_PAD // 640, d, 640, False, True, False, "gla_in_dw", rows=WGRAD_ROWS_BF16)
    gla_in_shards = jnp.stack([_take_cols(dw_gla_in, j * (GLA_IN // N_CHIPS), GLA_IN // N_CHIPS) for j in range(N_CHIPS)])
    gla_reduce = start_reduce("gla", [gla_in_shards, dw_gla_out.reshape(N_CHIPS, -1, d)])

    dh1, dhp0, d_ffn_g0 = _ffn_bwd_data(dh2, h1, row(ffn_norm_g[0]), hp0, w1g0, w2g0, "ffn0_bwd", dep=gla_reduce[-1])
    dw1_0 = _wgrad(uf0, dhp0, N_CHIPS, d, d, False, True, False, "ffn0_dw1", rows=WGRAD_ROWS_BF16)
    dw2_0 = _wgrad(hp0, dh2, N_CHIPS, d, d, True, False, True, "ffn0_dw2")
    ffn0_reduce = start_reduce("ffn0", [dw1_0, dw2_0])

    dmix0 = _dgrad(dh1, w_cp_out, "cp_out_dgrad", dep=ffn0_reduce[-1])
    dw_cp_out = _wgrad(mix0, dh1, 1, CONV_DIM + POOL_DIM, d, False, False, False, "cp_out_dw")
    dz0, d_conv_w, d_cp_vec, d_pool_w = _cp_seq_bwd(dmix0, z0, c0, pm0, conv_w_f, cp_ln_g, cp_ln_b, cp_pool_w[0],
                                                    cp_pool_scale)
    grad_x, dh0_head, d_mix_g0 = _dgrad_norm_bwd_input(dz0, w_cp_in, h0, row(mix_norm_g[0]), dh1, 512, "cp_in_dgrad")
    grad_x = grad_x[None]
    dw_cp_in = _wgrad(u0, dz0, 1, d, CP_IN, False, False, False, "cp_in_dw")
    dw_cp_in = jnp.stack([_take_cols(dw_cp_in, j * (CP_IN // N_CHIPS), CP_IN // N_CHIPS) for j in range(N_CHIPS)])

    cp_reduce = start_reduce("cp", [dw_cp_in, dw_cp_out.reshape(N_CHIPS, -1, d)])
    small_full = [dh0_head[PAD_ROWS:CHUNK],jnp.concatenate([d_mix_g0, d_mix_g1], axis=0),
                  jnp.concatenate([d_ffn_g0, d_ffn_g1], axis=0), d_conv_w[:CONV_WIDTH][None],
                  d_cp_vec[0:1], d_cp_vec[1:2], d_cp_vec[2:3], d_pool_w[None], d_cp_vec[3:4],
                  d_gate_w[:GATE_RANK][None], d_gate_b, d_head_g, d_final_g[0], loss_part[0, 0:1]]
    small_mine = _pack(small_full)
    whole = _plan_exchange(0)
    small_exchange = _start_copies("small_exchange_start", [small_mine], [lax.empty(small_mine.shape, F32)], whole, 1,
                                   dep=cp_reduce[-1])
    red_ffn1 = finish_reduce("ffn1", ffn1_reduce, small_exchange[-1])
    red_gla = finish_reduce("gla", gla_reduce, small_exchange[-1])
    (small_sent,), (small_recv,) = _wait_copies("small_exchange_wait", small_exchange, whole, [red_ffn1[1], red_gla[1]])
    small_chip = _add2(small_sent, small_recv, "chip_sum_small")
    small_slots = lax.dynamic_update_slice(jnp.zeros((N_CHIPS,) + small_chip.shape, F32), small_chip[None], (chip, 0, 0))
    small_reduce = _start_copies("small_scatter_start", [small_chip], [small_slots], _plan_scatter(0), 3)

    big = {"w1": (ffn_w1, m_ffn_w1, v_ffn_w1), "w2": (ffn_w2, m_ffn_w2, v_ffn_w2),
           "cp_in": (cp_w_in, m_cp_w_in, v_cp_w_in), "cp_out": (cp_w_out, m_cp_w_out, v_cp_w_out),
           "gla_in": (gla_w_in, m_gla_w_in, v_gla_w_in), "gla_out": (gla_w_out, m_gla_w_out, v_gla_w_out)}
    other_idx = (1 - core).reshape(1).astype(jnp.int32)

    def adamw_by_halves(tag, reduced, dep=None):
        flat = [r for n in reduced for r in reduced[n]]
        join_plan = _plan_exchange(0)
        join = _start_copies(tag + "_join_start", flat, [lax.empty(r.shape, F32) for r in flat], join_plan, len(flat),
                             dep=dep)
        views = {n: [_split_rows(a) for a in big[n]] for n in reduced}
        own, k = {}, 0
        for n in reduced:
            mine = join[2][k:k + len(reduced[n])]
            k += len(reduced[n])
            own[n] = _adamw_half(views[n][0], mine, views[n][1], views[n][2], c_idx, None, "adamw_%s_own" % n)
        _, arrived_halves = _wait_copies(tag + "_join_wait", join, join_plan, [own[n][1] for n in reduced])
        outs, k = {}, 0
        for n in reduced:
            theirs = arrived_halves[k:k + len(reduced[n])]
            k += len(reduced[n])
            res = _adamw_half(views[n][0], theirs, views[n][1], views[n][2], other_idx, own[n], "adamw_%s_sibling" % n)
            outs[n] = [o.reshape(big[n][0].shape) for o in res]
        return outs

    big_out = adamw_by_halves("gla", {"gla_in": [red_gla[0]], "gla_out": [red_gla[1]]}, dep=small_reduce[-1])
    red_ffn0 = finish_reduce("ffn0", ffn0_reduce, big_out["gla_out"][1])
    big_out.update(adamw_by_halves("ffn", {"w1": [red_ffn0[0], red_ffn1[0]], "w2": [red_ffn0[1], red_ffn1[1]]}))
    red_cp = finish_reduce("cp", cp_reduce, big_out["w2"][1])
    _, (small_landed,) = _wait_copies("small_scatter_wait", small_reduce, _plan_scatter(0), big_out["w2"][1])
    small_red = _sum_slots(small_landed, "slot_sum_small")
    big_out.update(adamw_by_halves("cp", {"cp_in": [red_cp[0]], "cp_out": [red_cp[1]]}))

    (g_meta, g_mix, g_ffn, g_conv_w, g_conv_b, g_ln_g, g_ln_b, g_pool_w, g_pool_scale, g_gate_w, g_gate_b, g_head,
     g_final, loss_sum) = _unpack(small_red, [a.shape for a in small_full])
    g_meta = _col_shard(g_meta, chip, meta_tokens.shape[-1])
    g_conv_w = _col_shard(g_conv_w, chip, cp_conv_w.shape[-1])
    g_gate_w = _col_shard(g_gate_w, chip, gla_gate_w2.shape[-1])
    g_gate_b = _col_shard(g_gate_b, chip, gla_gate_b.shape[-1])
    g_head = _col_shard(g_head, chip, gla_head_g.shape[-1])
    small_w = [meta_tokens, mix_norm_g, ffn_norm_g, cp_conv_w, cp_conv_b, cp_ln_g, cp_ln_b, cp_pool_w, cp_pool_scale,
               gla_gate_w2, gla_gate_b, gla_head_g, final_norm_g]
    small_m = [m_meta_tokens, m_mix_norm_g, m_ffn_norm_g, m_cp_conv_w, m_cp_conv_b, m_cp_ln_g, m_cp_ln_b, m_cp_pool_w,
               m_cp_pool_scale, m_gla_gate_w2, m_gla_gate_b, m_gla_head_g, m_final_norm_g]
    small_v = [v_meta_tokens, v_mix_norm_g, v_ffn_norm_g, v_cp_conv_w, v_cp_conv_b, v_cp_ln_g, v_cp_ln_b, v_cp_pool_w,
               v_cp_pool_scale, v_gla_gate_w2, v_gla_gate_b, v_gla_head_g, v_final_norm_g]
    small_g = [g_meta, g_mix, g_ffn, g_conv_w, g_conv_b, g_ln_g, g_ln_b, g_pool_w, g_pool_scale, g_gate_w, g_gate_b,
               g_head, g_final]
    shapes = [w.shape for w in small_w]
    small_g = [g.reshape(s) for g, s in zip(small_g, shapes)]
    at_least_2d = lambda arrs: [a.reshape(1, -1) if a.ndim == 1 else a for a in arrs]
    s_delta, s_m, s_v = _adamw_many(at_least_2d(small_w), at_least_2d(small_g), at_least_2d(small_m), at_least_2d(small_v))
    s_delta, s_m, s_v = [[a.reshape(s) for a, s in zip(group, shapes)] for group in (s_delta, s_m, s_v)]

    order = ["meta", "mix", "ffn", "w1", "w2", "cp_in", "conv_w", "conv_b", "ln_g", "ln_b", "pool_w", "pool_scale",
             "cp_out", "gla_in", "gate_w", "gate_b", "head", "gla_out", "final"]
    small_names = ["meta", "mix", "ffn", "conv_w", "conv_b", "ln_g", "ln_b", "pool_w", "pool_scale", "gate_w", "gate_b",
                   "head", "final"]
    big_names = ["w1", "w2", "cp_in", "cp_out", "gla_in", "gla_out"]
    table = {n: (small_g[i], s_delta[i], s_m[i], s_v[i]) for i, n in enumerate(small_names)}
    table.update({n: tuple(big_out[n]) for n in big_names})
    loss = loss_sum.reshape(())
    return (loss, grad_x, *[table[n][0] for n in order], *[table[n][1] for n in order],
            *[table[n][2] for n in order], *[table[n][3] for n in order])
```

```python
import functools

import jax
import jax.numpy as jnp
from jax import lax
from jax.experimental import pallas as pl
from jax.experimental.pallas import tpu as pltpu

F32 = jnp.float32
BF16 = jnp.bfloat16

D_MODEL = 1024
N_META = 16
CHUNK = 64
PAD_ROWS = CHUNK - N_META
EPS = 1e-5
CONV_DIM = 512
CONV_WIDTH = 31
CONV_HALO = 32
POOL_DIM = 512
POOL_WINDOWS = (2, 4, 8, 16)
POOL_GROUP = 128
POOL_HALO = 16
CP_IN = 2 * CONV_DIM + POOL_DIM
GLA_HEADS = 4
GLA_DK = 512
GLA_DV = 1024
GLA_HK = GLA_DK // GLA_HEADS
GLA_HV = GLA_DV // GLA_HEADS
GATE_RANK = 16
GATE_PAD = 128
GATE_NORM = 16.0
GLA_IN = 2 * GLA_DK + 2 * GLA_DV + GATE_RANK
GLA_IN_PAD = 2 * GLA_DK + 2 * GLA_DV + GATE_PAD
GLA_COLS = 1280
N_CHIPS = 4
ADAM_LR = 0.001
ADAM_B1 = 0.9
ADAM_B2 = 0.999
ADAM_EPS = 1e-08
ADAM_WD = 0.01
ADAM_STEP = 10

VMEM_LIMIT_BYTES = 56 * 1024 * 1024
ROW_TILE_TARGET = 832
TOKEN_TILE_TARGET = 1040
PACK_WIDTH = 1024
MESH = pl.DeviceIdType.MESH
HBM_SPEC = pl.BlockSpec(memory_space=pltpu.HBM)
ANY_SPEC = pl.BlockSpec(memory_space=pl.ANY)
SEM_SPEC = pl.BlockSpec(memory_space=pltpu.SEMAPHORE)
SIDE_EFFECT = pltpu.SideEffectType.DATAFLOW_SIDE_EFFECTING


def _cparams(*sem):
    return pltpu.CompilerParams(dimension_semantics=sem, vmem_limit_bytes=VMEM_LIMIT_BYTES)


def _row_tile(t, target, mult):
    best = mult
    for cand in range(mult, min(t, target) + 1, mult):
        if t % cand == 0:
            best = cand
    assert t % best == 0, (t, best)
    return best


def _rms(h, g):
    return h * lax.rsqrt(jnp.mean(h * h, axis=-1, keepdims=True) + EPS) * g


def _rms_bwd(h, g, du):
    r = lax.rsqrt(jnp.mean(h * h, axis=-1, keepdims=True) + EPS)
    xhat = h * r
    dxh = du * g
    dh = r * (dxh - xhat * jnp.mean(dxh * xhat, axis=-1, keepdims=True))
    return dh, du * xhat


def _valid_rows(i, tm):
    row = i * tm + lax.broadcasted_iota(jnp.int32, (tm, 1), 0)
    return row >= PAD_ROWS


def _dot(a, b):
    return jnp.dot(a, b, preferred_element_type=F32)


def _dot_nt(a, b):
    return lax.dot_general(a, b, (((1,), (1,)), ((), ())), preferred_element_type=F32)


def _dot_tn(a, b):
    return lax.dot_general(a, b, (((0,), (0,)), ((), ())), preferred_element_type=F32)


def _accumulate(ref, val, first):
    @pl.when(first)
    def _():
        ref[...] = val

    @pl.when(jnp.logical_not(first))
    def _():
        ref[...] += val


def _call_after(dep, body, n_in, in_specs, args, **kw):
    if dep is None:
        return pl.pallas_call(body, in_specs=in_specs, **kw)(*args)

    def with_dep(*refs):
        body(*refs[:n_in], *refs[n_in + 1:])

    return pl.pallas_call(with_dep, in_specs=list(in_specs) + [ANY_SPEC], **kw)(*args, dep)


def _norm_matmul(h, g, w, nc, name, dep=None):
    t, d = h.shape
    n = w.shape[1]
    tm = _row_tile(t, TOKEN_TILE_TARGET, 16)

    def body(h_ref, g_ref, w_ref, z_ref, u_ref):
        u = _rms(h_ref[...], g_ref[...]).astype(BF16)
        u_ref[...] = u
        for n0 in range(0, n, nc):
            n1 = min(n0 + nc, n)
            z_ref[:, n0:n1] = _dot(u, w_ref[:, n0:n1]).astype(BF16)

    return _call_after(
        dep, body, 3,
        [pl.BlockSpec((tm, d), lambda i: (i, 0)), pl.BlockSpec((1, d), lambda i: (0, 0)),
         pl.BlockSpec((d, n), lambda i: (0, 0))], (h, g, w), grid=(t // tm,),
        out_specs=[pl.BlockSpec((tm, n), lambda i: (i, 0)), pl.BlockSpec((tm, d), lambda i: (i, 0))],
        out_shape=[jax.ShapeDtypeStruct((t, n), BF16), jax.ShapeDtypeStruct((t, d), BF16)],
        compiler_params=_cparams("parallel"), name=name)


def _matmul_residual(a, w, h, name, dep=None):
    t, k = a.shape
    d = w.shape[1]
    tm = _row_tile(t, TOKEN_TILE_TARGET, 16)

    def body(a_ref, w_ref, h_ref, o_ref):
        o_ref[...] = h_ref[...] + _dot(a_ref[...], w_ref[...])

    return _call_after(
        dep, body, 3,
        [pl.BlockSpec((tm, k), lambda i: (i, 0)), pl.BlockSpec((k, d), lambda i: (0, 0)),
         pl.BlockSpec((tm, d), lambda i: (i, 0))], (a, w, h), grid=(t // tm,),
        out_specs=pl.BlockSpec((tm, d), lambda i: (i, 0)),
        out_shape=jax.ShapeDtypeStruct((t, d), F32),
        compiler_params=_cparams("parallel"), name=name)


def _ffn_fwd(h, g, w1g, w2g, name):
    t, d = h.shape
    ns, ffs = w1g.shape[0], w1g.shape[2]
    tm = _row_tile(t, TOKEN_TILE_TARGET, 16)

    def body(h_ref, g_ref, w1_ref, w2_ref, ho_ref, hp_ref, u_ref, acc_ref):
        s = pl.program_id(1)

        @pl.when(s == 0)
        def _():
            u_ref[...] = _rms(h_ref[...], g_ref[...]).astype(BF16)

        hp = _dot(u_ref[...], w1_ref[...])
        hp_ref[...] = hp.astype(BF16)
        a = jnp.maximum(hp, 0.0)
        _accumulate(acc_ref, _dot((a * a).astype(BF16), w2_ref[...]), s == 0)

        @pl.when(s == ns - 1)
        def _():
            ho_ref[...] = h_ref[...] + acc_ref[...]

    return pl.pallas_call(
        body, grid=(t // tm, ns),
        in_specs=[pl.BlockSpec((tm, d), lambda i, s: (i, 0)), pl.BlockSpec((1, d), lambda i, s: (0, 0)),
                  pl.BlockSpec((None, d, ffs), lambda i, s: (s, 0, 0)),
                  pl.BlockSpec((None, ffs, d), lambda i, s: (s, 0, 0))],
        out_specs=[pl.BlockSpec((tm, d), lambda i, s: (i, 0)), pl.BlockSpec((tm, ffs), lambda i, s: (i, s)),
                   pl.BlockSpec((tm, d), lambda i, s: (i, 0))],
        out_shape=[jax.ShapeDtypeStruct((t, d), F32), jax.ShapeDtypeStruct((t, ns * ffs), BF16),
                   jax.ShapeDtypeStruct((t, d), BF16)],
        scratch_shapes=[pltpu.VMEM((tm, d), F32)],
        compiler_params=_cparams("parallel", "arbitrary"), name=name)(h, g, w1g, w2g)


def _ffn_bwd_data(dh, h, g, hp, w1g, w2g, name, dep=None):
    t, d = h.shape
    ns, ffs = w1g.shape[0], w1g.shape[2]
    tm = _row_tile(t, ROW_TILE_TARGET, CHUNK)

    def body(dh_ref, h_ref, g_ref, hp_ref, w1_ref, w2_ref, dhi_ref, dhp_ref, dg_ref, acc_ref):
        i, s = pl.program_id(0), pl.program_id(1)
        da = _dot_nt(dh_ref[...].astype(BF16), w2_ref[...])
        dhp = (da * (2.0 * jnp.maximum(hp_ref[...].astype(F32), 0.0))).astype(BF16)
        dhp_ref[...] = dhp
        _accumulate(acc_ref, _dot_nt(dhp, w1_ref[...]), s == 0)

        @pl.when(s == ns - 1)
        def _():
            dhn, dgr = _rms_bwd(h_ref[...], g_ref[...], acc_ref[...])
            dhi_ref[...] = jnp.where(_valid_rows(i, tm), dh_ref[...] + dhn, 0.0)
            _accumulate(dg_ref, jnp.sum(dgr, axis=0, keepdims=True), i == 0)

    return _call_after(
        dep, body, 6,
        [pl.BlockSpec((tm, d), lambda i, s: (i, 0)), pl.BlockSpec((tm, d), lambda i, s: (i, 0)),
         pl.BlockSpec((1, d), lambda i, s: (0, 0)), pl.BlockSpec((tm, ffs), lambda i, s: (i, s)),
         pl.BlockSpec((None, d, ffs), lambda i, s: (s, 0, 0)),
         pl.BlockSpec((None, ffs, d), lambda i, s: (s, 0, 0))], (dh, h, g, hp, w1g, w2g), grid=(t // tm, ns),
        out_specs=[pl.BlockSpec((tm, d), lambda i, s: (i, 0)), pl.BlockSpec((tm, ffs), lambda i, s: (i, s)),
                   pl.BlockSpec((1, d), lambda i, s: (0, 0))],
        out_shape=[jax.ShapeDtypeStruct((t, d), F32), jax.ShapeDtypeStruct((t, ns * ffs), BF16),
                   jax.ShapeDtypeStruct((1, d), F32)],
        scratch_shapes=[pltpu.VMEM((tm, d), F32)],
        compiler_params=_cparams("arbitrary", "arbitrary"), name=name)


WGRAD_ROWS = 2048
WGRAD_ROWS_BF16 = 4096


def _wgrad(x, dy, nb, xc, yc, x_by_block, dy_by_block, relu2, name, dep=None, rows=WGRAD_ROWS):
    t = x.shape[0]
    tk = _row_tile(t - CHUNK, rows, CHUNK)

    def prep(xv):
        if relu2:
            xv = jnp.maximum(xv.astype(F32), 0.0)
            xv = xv * xv
        return xv.astype(BF16)

    nk = (t - CHUNK) // tk

    def body(xh_ref, dyh_ref, x_ref, dy_ref, o_ref, acc_ref):
        k = pl.program_id(1)
        p = _dot_tn(prep(x_ref[...]), dy_ref[...].astype(BF16))

        @pl.when(k == 0)
        def _():
            acc_ref[...] = p + _dot_tn(prep(xh_ref[...]), dyh_ref[...].astype(BF16))

        @pl.when(k > 0)
        def _():
            acc_ref[...] += p

        @pl.when(k == nk - 1)
        def _():
            o_ref[...] = acc_ref[...].astype(BF16)

    def head(width, by_block):
        return pl.BlockSpec((CHUNK, width), (lambda b, k: (0, b)) if by_block else (lambda b, k: (0, 0)))

    def rest(width, by_block):
        def index(b, k):
            return pl.multiple_of(CHUNK + k * tk, CHUNK), (pl.multiple_of(b * width, 128) if by_block else 0)
        return pl.BlockSpec((pl.Element(tk), pl.Element(width)), index)

    return _call_after(
        dep, body, 4,
        [head(xc, x_by_block), head(yc, dy_by_block), rest(xc, x_by_block), rest(yc, dy_by_block)], (x, dy, x, dy),
        grid=(nb, nk),
        out_specs=pl.BlockSpec((None, xc, yc), lambda b, k: (b, 0, 0)),
        out_shape=jax.ShapeDtypeStruct((nb, xc, yc), BF16),
        scratch_shapes=[pltpu.VMEM((xc, yc), F32)],
        compiler_params=_cparams("parallel", "arbitrary"), name=name)


def _dgrad(dh, w, name, dep=None):
    t, d = dh.shape
    k = w.shape[0]
    tm = _row_tile(t, TOKEN_TILE_TARGET, 16)

    def body(dh_ref, w_ref, o_ref):
        o_ref[...] = _dot_nt(dh_ref[...].astype(BF16), w_ref[...]).astype(BF16)

    return _call_after(
        dep, body, 2,
        [pl.BlockSpec((tm, d), lambda i: (i, 0)), pl.BlockSpec((k, d), lambda i: (0, 0))], (dh, w), grid=(t // tm,),
        out_specs=pl.BlockSpec((tm, k), lambda i: (i, 0)),
        out_shape=jax.ShapeDtypeStruct((t, k), BF16),
        compiler_params=_cparams("parallel"), name=name)


def _dgrad_norm_bwd(dz, w, h, g, dh, nc, name):
    t, d = h.shape
    n = w.shape[1]
    tm = _row_tile(t, ROW_TILE_TARGET // 2, 16)

    def body(dz_ref, w_ref, h_ref, g_ref, dh_ref, dhi_ref, dg_ref):
        i = pl.program_id(0)
        du = jnp.zeros((tm, d), F32)
        for n0 in range(0, n, nc):
            n1 = min(n0 + nc, n)
            du = du + _dot_nt(dz_ref[:, n0:n1], w_ref[:, n0:n1])
        dhn, dgr = _rms_bwd(h_ref[...], g_ref[...], du)
        dhi_ref[...] = jnp.where(_valid_rows(i, tm), dh_ref[...] + dhn, 0.0)
        _accumulate(dg_ref, jnp.sum(dgr, axis=0, keepdims=True), i == 0)

    return pl.pallas_call(
        body, grid=(t // tm,),
        in_specs=[pl.BlockSpec((tm, n), lambda i: (i, 0)), pl.BlockSpec((d, n), lambda i: (0, 0)),
                  pl.BlockSpec((tm, d), lambda i: (i, 0)), pl.BlockSpec((1, d), lambda i: (0, 0)),
                  pl.BlockSpec((tm, d), lambda i: (i, 0))],
        out_specs=[pl.BlockSpec((tm, d), lambda i: (i, 0)), pl.BlockSpec((1, d), lambda i: (0, 0))],
        out_shape=[jax.ShapeDtypeStruct((t, d), F32), jax.ShapeDtypeStruct((1, d), F32)],
        compiler_params=_cparams("arbitrary"), name=name)(dz, w, h, g, dh)


def _dgrad_norm_bwd_input(dz, w, h, g, dh, nc, name):
    t, d = h.shape
    n = w.shape[1]
    tl = _row_tile(t - CHUNK, 512, CHUNK)

    def grads(dz_ref, w_ref, h_ref, g_ref, dh_ref, rows):
        du = jnp.zeros((rows, d), F32)
        for n0 in range(0, n, nc):
            n1 = min(n0 + nc, n)
            du = du + _dot_nt(dz_ref[:, n0:n1], w_ref[:, n0:n1])
        dhn, dgr = _rms_bwd(h_ref[...], g_ref[...], du)
        return dh_ref[...] + dhn, jnp.sum(dgr, axis=0, keepdims=True)

    def rest_body(dz_ref, w_ref, h_ref, g_ref, dh_ref, dg_head_ref, dx_ref, dg_ref):
        dx, dg = grads(dz_ref, w_ref, h_ref, g_ref, dh_ref, tl)
        dx_ref[...] = dx

        @pl.when(pl.program_id(0) == 0)
        def _():
            dg_ref[...] = dg_head_ref[...] + dg

        @pl.when(pl.program_id(0) > 0)
        def _():
            dg_ref[...] += dg

    def head_body(dz_ref, w_ref, h_ref, g_ref, dh_ref, dx_ref, dg_ref):
        dx, dg = grads(dz_ref, w_ref, h_ref, g_ref, dh_ref, CHUNK)
        dx_ref[...] = jnp.where(_valid_rows(0, CHUNK), dx, 0.0)
        dg_ref[...] = dg

    def shifted(width):
        return pl.BlockSpec((pl.Element(tl), pl.Element(width)), lambda i: (pl.multiple_of(CHUNK + i * tl, CHUNK), 0))

    whole = [pl.BlockSpec((d, n), lambda i: (0, 0)), pl.BlockSpec((1, d), lambda i: (0, 0))]
    head = lambda width: pl.BlockSpec((CHUNK, width), lambda i: (0, 0))
    dh_head, dg_head = pl.pallas_call(
        head_body, grid=(1,), in_specs=[head(n), whole[0], head(d), whole[1], head(d)],
        out_specs=[head(d), whole[1]],
        out_shape=[jax.ShapeDtypeStruct((CHUNK, d), F32), jax.ShapeDtypeStruct((1, d), F32)],
        compiler_params=_cparams("arbitrary"), name=name + "_head")(dz, w, h, g, dh)
    dx, dg = pl.pallas_call(
        rest_body, grid=((t - CHUNK) // tl,),
        in_specs=[shifted(n), whole[0], shifted(d), whole[1], shifted(d), whole[1]],
        out_specs=[pl.BlockSpec((tl, d), lambda i: (i, 0)), whole[1]],
        out_shape=[jax.ShapeDtypeStruct((t - CHUNK, d), F32), jax.ShapeDtypeStruct((1, d), F32)],
        compiler_params=_cparams("arbitrary"), name=name)(dz, w, h, g, dh, dg_head)
    return dx, dh_head, dg


def _loss_bwd(h, g, target):
    t, d = h.shape
    tl = _row_tile(t - CHUNK, 1024, CHUNK)

    def body(h_ref, g_ref, t_ref, dh_ref, dg_ref, loss_ref):
        i = pl.program_id(0)
        hv, gv = h_ref[...], g_ref[...]
        err = _rms(hv, gv) - t_ref[...]
        part = 0.5 * jnp.sum(jnp.mean(err * err, axis=-1, keepdims=True), axis=0, keepdims=True)
        dhn, dgr = _rms_bwd(hv, gv, err * (1.0 / d))
        dh_ref[...] = dhn
        _accumulate(dg_ref, jnp.sum(dgr, axis=0, keepdims=True), i == 0)
        _accumulate(loss_ref, jnp.broadcast_to(part, (8, 128)), i == 0)

    shifted = pl.BlockSpec((pl.Element(tl), pl.Element(d)), lambda i: (pl.multiple_of(CHUNK + i * tl, CHUNK), 0))
    dh, dg, loss = pl.pallas_call(
        body, grid=((t - CHUNK) // tl,),
        in_specs=[shifted, pl.BlockSpec((1, d), lambda i: (0, 0)), pl.BlockSpec((tl, d), lambda i: (i, 0))],
        out_specs=[shifted, pl.BlockSpec((1, d), lambda i: (0, 0)), pl.BlockSpec((8, 128), lambda i: (0, 0))],
        out_shape=[jax.ShapeDtypeStruct((t, d), F32), jax.ShapeDtypeStruct((1, d), F32),
                   jax.ShapeDtypeStruct((8, 128), F32)],
        compiler_params=_cparams("arbitrary"), name="loss_bwd")(h, g, target)

    def zero_head(dh_ref, o_ref):
        o_ref[...] = jnp.zeros_like(o_ref)

    dh = pl.pallas_call(
        zero_head, grid=(1,), in_specs=[ANY_SPEC], out_specs=pl.BlockSpec((CHUNK, d), lambda i: (0, 0)),
        out_shape=jax.ShapeDtypeStruct((t, d), F32), input_output_aliases={0: 0}, name="loss_bwd_head")(dh)
    return dh, dg, loss


CONV_BLOCK = 32


def _silu(x):
    return x * jax.nn.sigmoid(x)


def _row_shifts(win):
    n = win.shape[0]
    return [win] + [pltpu.roll(win, n - j, 0) for j in range(1, 8)]


def _cp_seq_fwd(z, conv_w, conv_b, ln_g, ln_b, pool_w, pool_scale):
    t = z.shape[0]
    tm = _row_tile(t, ROW_TILE_TARGET, CHUNK)

    def body(z_ref, cw_ref, cb_ref, lg_ref, lb_ref, pw_ref, ps_ref, c_ref, pm_ref, mix_ref, gbuf, pbuf):
        i = pl.program_id(0)

        @pl.when(i == 0)
        def _():
            gbuf[0:CONV_HALO, :] = jnp.zeros((CONV_HALO, CONV_DIM), F32)
            pbuf[0:POOL_HALO, :] = jnp.zeros((POOL_HALO, POOL_DIM), F32)

        @pl.when(i > 0)
        def _():
            gbuf[0:CONV_HALO, :] = gbuf[tm:tm + CONV_HALO, :]
            pbuf[0:POOL_HALO, :] = pbuf[tm:tm + POOL_HALO, :]

        av = z_ref[:, 0:CONV_DIM].astype(F32)
        ag = z_ref[:, CONV_DIM:2 * CONV_DIM].astype(F32)
        gbuf[CONV_HALO:CONV_HALO + tm, :] = av * jax.nn.sigmoid(ag)
        pbuf[POOL_HALO:POOL_HALO + tm, :] = z_ref[:, 2 * CONV_DIM:CP_IN].astype(F32)

        def conv_block(rb, carry):
            base = pl.multiple_of(rb * CONV_BLOCK, CONV_BLOCK)
            shifted = _row_shifts(gbuf[pl.ds(base, CONV_BLOCK + CONV_HALO), :])
            acc = jnp.zeros((CONV_BLOCK, CONV_DIM), F32)
            for k in range(CONV_WIDTH):
                whole, part = divmod(CONV_HALO - (CONV_WIDTH - 1) + k, 8)
                acc = acc + cw_ref[k:k + 1, :] * shifted[part][8 * whole:8 * whole + CONV_BLOCK, :]
            c_ref[pl.ds(base, CONV_BLOCK), :] = acc + cb_ref[...]
            return carry

        lax.fori_loop(0, tm // CONV_BLOCK, conv_block, 0)

        c = c_ref[...]
        mu = jnp.mean(c, axis=-1, keepdims=True)
        xc = c - mu
        ln = xc * lax.rsqrt(jnp.mean(xc * xc, axis=-1, keepdims=True) + EPS) * lg_ref[...] + lb_ref[...]
        row = i * tm + lax.broadcasted_iota(jnp.int32, (tm, 1), 0)
        mix_ref[:, 0:CONV_DIM] = jnp.where(row >= PAD_ROWS, _silu(ln), 0.0).astype(BF16)

        tpos = (row - PAD_ROWS + 1).astype(F32)
        for gi, wdw in enumerate(POOL_WINDOWS):
            lo = POOL_GROUP * gi
            run, step = pbuf[:, lo:lo + POOL_GROUP], 1
            cur = run[POOL_HALO:POOL_HALO + tm, :]
            while step < wdw:
                run = run + pltpu.roll(run, step, 0)
                step *= 2
            pm = (run[POOL_HALO:POOL_HALO + tm, :] / jnp.clip(tpos, 1.0, float(wdw)) - cur).astype(BF16)
            pm_ref[:, lo:lo + POOL_GROUP] = pm
            pg = _dot(pm, pw_ref[gi].astype(BF16))
            mix_ref[:, CONV_DIM + lo:CONV_DIM + lo + POOL_GROUP] = (pg * ps_ref[:, lo:lo + POOL_GROUP]).astype(BF16)

    vec = pl.BlockSpec((1, CONV_DIM), lambda i: (0, 0))
    return pl.pallas_call(
        body, grid=(t // tm,),
        in_specs=[pl.BlockSpec((tm, CP_IN), lambda i: (i, 0)),
                  pl.BlockSpec((CONV_WIDTH, CONV_DIM), lambda i: (0, 0)), vec, vec, vec,
                  pl.BlockSpec((len(POOL_WINDOWS), POOL_GROUP, POOL_GROUP), lambda i: (0, 0, 0)), vec],
        out_specs=[pl.BlockSpec((tm, CONV_DIM), lambda i: (i, 0)), pl.BlockSpec((tm, POOL_DIM), lambda i: (i, 0)),
                   pl.BlockSpec((tm, CONV_DIM + POOL_DIM), lambda i: (i, 0))],
        out_shape=[jax.ShapeDtypeStruct((t, CONV_DIM), F32), jax.ShapeDtypeStruct((t, POOL_DIM), BF16),
                   jax.ShapeDtypeStruct((t, CONV_DIM + POOL_DIM), BF16)],
        scratch_shapes=[pltpu.VMEM((tm + CONV_HALO, CONV_DIM), F32), pltpu.VMEM((tm + POOL_HALO, POOL_DIM), F32)],
        compiler_params=_cparams("arbitrary"), name="cp_seq_fwd")(z, conv_w, conv_b, ln_g, ln_b, pool_w, pool_scale)


def _cp_seq_bwd(dh, w_out, z, c, pm, conv_w, ln_g, ln_b, pool_w, pool_scale, dep=None):
    t, d = dh.shape
    tm = _row_tile(t, ROW_TILE_TARGET, CHUNK)
    nt = t // tm

    def body(dh_ref, wo_ref, z_ref, c_ref, pm_ref, cw_ref, lg_ref, lb_ref, pw_ref, ps_ref,
             dz_ref, dcw_ref, dvec_ref, dpw_ref, dcbuf, qbuf, glu_buf, dwacc, dmix_ref):
        i = pl.program_id(0)
        tile = nt - 1 - i
        dmix_ref[...] = _dot_nt(dh_ref[...].astype(BF16), wo_ref[...])

        @pl.when(i == 0)
        def _():
            dcbuf[tm:tm + CONV_HALO, :] = jnp.zeros((CONV_HALO, CONV_DIM), F32)
            qbuf[tm:tm + POOL_HALO, :] = jnp.zeros((POOL_HALO, POOL_DIM), F32)
            dcw_ref[...] = jnp.zeros_like(dcw_ref)
            dwacc[...] = jnp.zeros_like(dwacc)
            dvec_ref[...] = jnp.zeros_like(dvec_ref)
            dpw_ref[...] = jnp.zeros_like(dpw_ref)

        @pl.when(i > 0)
        def _():
            dcbuf[tm:tm + CONV_HALO, :] = dcbuf[0:CONV_HALO, :]
            qbuf[tm:tm + POOL_HALO, :] = qbuf[0:POOL_HALO, :]

        row = tile * tm + lax.broadcasted_iota(jnp.int32, (tm, 1), 0)
        cv = c_ref[...]
        mu = jnp.mean(cv, axis=-1, keepdims=True)
        xc = cv - mu
        rstd = lax.rsqrt(jnp.mean(xc * xc, axis=-1, keepdims=True) + EPS)
        xhat = xc * rstd
        ln = xhat * lg_ref[...] + lb_ref[...]
        sg = jax.nn.sigmoid(ln)
        da = jnp.where(row >= PAD_ROWS, dmix_ref[:, 0:CONV_DIM], 0.0)
        dln = da * (sg * (1.0 + ln * (1.0 - sg)))
        dxh = dln * lg_ref[...]
        dc = rstd * (dxh - jnp.mean(dxh, axis=-1, keepdims=True) - xhat * jnp.mean(dxh * xhat, axis=-1, keepdims=True))
        dcbuf[0:tm, :] = dc
        dvec_ref[0:1, :] += jnp.sum(dc, axis=0, keepdims=True)
        dvec_ref[1:2, :] += jnp.sum(dln * xhat, axis=0, keepdims=True)
        dvec_ref[2:3, :] += jnp.sum(dln, axis=0, keepdims=True)

        av = z_ref[:, 0:CONV_DIM].astype(F32)
        sig_g = jax.nn.sigmoid(z_ref[:, CONV_DIM:2 * CONV_DIM].astype(F32))
        glu_buf[...] = av * sig_g

        def conv_block(rb, carry):
            base = pl.multiple_of(rb * CONV_BLOCK, CONV_BLOCK)
            shifted = _row_shifts(dcbuf[pl.ds(base, CONV_BLOCK + CONV_HALO), :])
            glu = glu_buf[pl.ds(base, CONV_BLOCK), :]
            acc = jnp.zeros((CONV_BLOCK, CONV_DIM), F32)
            for k in range(CONV_WIDTH):
                whole, part = divmod(CONV_WIDTH - 1 - k, 8)
                slab = shifted[part][8 * whole:8 * whole + CONV_BLOCK, :]
                acc = acc + cw_ref[k:k + 1, :] * slab
                prod = slab * glu
                part = prod[0:8]
                for q in range(1, CONV_BLOCK // 8):
                    part = part + prod[8 * q:8 * q + 8]
                dwacc[k] += part
            glu_buf[pl.ds(base, CONV_BLOCK), :] = acc
            return carry

        lax.fori_loop(0, tm // CONV_BLOCK, conv_block, 0)

        @pl.when(i == nt - 1)
        def _():
            for k in range(CONV_WIDTH):
                dcw_ref[k:k + 1, :] = jnp.sum(dwacc[k], axis=0, keepdims=True)
        dglu = glu_buf[...]
        dz_ref[:, 0:CONV_DIM] = (dglu * sig_g).astype(BF16)
        dz_ref[:, CONV_DIM:2 * CONV_DIM] = (dglu * av * sig_g * (1.0 - sig_g)).astype(BF16)

        tpos = (row - PAD_ROWS + 1).astype(F32)
        for gi, wdw in enumerate(POOL_WINDOWS):
            lo = POOL_GROUP * gi
            dp = dmix_ref[:, CONV_DIM + lo:CONV_DIM + lo + POOL_GROUP]
            pmv = pm_ref[:, lo:lo + POOL_GROUP]
            pwb = pw_ref[gi].astype(BF16)
            dvec_ref[3:4, lo:lo + POOL_GROUP] += jnp.sum(dp * _dot(pmv, pwb), axis=0, keepdims=True)
            dq = (dp * ps_ref[:, lo:lo + POOL_GROUP]).astype(BF16)
            dpw_ref[gi] += _dot_tn(pmv, dq)
            dpm = _dot_nt(dq, pwb)
            qbuf[0:tm, lo:lo + POOL_GROUP] = dpm / jnp.clip(tpos, 1.0, float(wdw))
            run, step = qbuf[:, lo:lo + POOL_GROUP], 1
            while step < wdw:
                run = run + pltpu.roll(run, tm + POOL_HALO - step, 0)
                step *= 2
            dz_ref[:, 2 * CONV_DIM + lo:2 * CONV_DIM + lo + POOL_GROUP] = (run[0:tm, :] - dpm).astype(BF16)

    vec = pl.BlockSpec((1, CONV_DIM), lambda i: (0, 0))
    rev = lambda i: (nt - 1 - i, 0)
    return _call_after(
        dep, body, 10,
        [pl.BlockSpec((tm, d), rev), pl.BlockSpec((CONV_DIM + POOL_DIM, d), lambda i: (0, 0)), pl.BlockSpec((tm, CP_IN), rev),
         pl.BlockSpec((tm, CONV_DIM), rev), pl.BlockSpec((tm, POOL_DIM), rev),
         pl.BlockSpec((CONV_WIDTH, CONV_DIM), lambda i: (0, 0)), vec, vec,
         pl.BlockSpec((len(POOL_WINDOWS), POOL_GROUP, POOL_GROUP), lambda i: (0, 0, 0)), vec],
        (dh, w_out, z, c, pm, conv_w, ln_g, ln_b, pool_w, pool_scale), grid=(nt,),
        out_specs=[pl.BlockSpec((tm, CP_IN), rev), pl.BlockSpec((CONV_WIDTH + 1, CONV_DIM), lambda i: (0, 0)),
                   pl.BlockSpec((8, CONV_DIM), lambda i: (0, 0)),
                   pl.BlockSpec((len(POOL_WINDOWS), POOL_GROUP, POOL_GROUP), lambda i: (0, 0, 0))],
        out_shape=[jax.ShapeDtypeStruct((t, CP_IN), BF16), jax.ShapeDtypeStruct((CONV_WIDTH + 1, CONV_DIM), F32),
                   jax.ShapeDtypeStruct((8, CONV_DIM), F32),
                   jax.ShapeDtypeStruct((len(POOL_WINDOWS), POOL_GROUP, POOL_GROUP), F32)],
        scratch_shapes=[pltpu.VMEM((tm + CONV_HALO, CONV_DIM), F32), pltpu.VMEM((tm + POOL_HALO, POOL_DIM), F32),
                        pltpu.VMEM((tm, CONV_DIM), F32), pltpu.VMEM((CONV_WIDTH + 1, 8, CONV_DIM), F32),
                        pltpu.VMEM((tm, CONV_DIM + POOL_DIM), F32)],
        compiler_params=_cparams("arbitrary"), name="cp_seq_bwd")


Q0, K0, V0, G0, R0 =0, GLA_DK, 2 * GLA_DK, 2 * GLA_DK + GLA_DV, 2 * GLA_DK + 2 * GLA_DV


def _split3(x):
    hi = x.astype(BF16)
    r1 = x - hi.astype(F32)
    mid = r1.astype(BF16)
    lo = (r1 - mid.astype(F32)).astype(BF16)
    return hi, mid, lo


def _tri(strict):
    r = lax.broadcasted_iota(jnp.int32, (CHUNK, CHUNK), 0)
    c = lax.broadcasted_iota(jnp.int32, (CHUNK, CHUNK), 1)
    return ((r > c) if strict else (r >= c)).astype(BF16)


def _chunk_sums(x, cpt, strict, pieces):
    tri3 = jnp.broadcast_to(_tri(strict)[None], (cpt, CHUNK, CHUNK))
    acc = None
    for piece in _split3(x.reshape(cpt, CHUNK, x.shape[-1]))[:pieces]:
        part = jnp.einsum("bij,bjk->bik", tri3, piece, preferred_element_type=F32)
        acc = part if acc is None else acc + part
    return acc


def _chunk_decay(r, gw_ref, gb_ref, cpt):
    pre = _dot(r, gw_ref[...]) + gb_ref[...]
    lac = (jnp.minimum(pre, 0.0) - jnp.log(1.0 + jnp.exp(-jnp.abs(pre)))) * (1.0 / GATE_NORM)
    cum3 = _chunk_sums(lac, cpt, False, 3)
    return cum3, cum3[:, CHUNK - 1:CHUNK, :]


def _gla_seq_fwd(z, gate_w, gate_b, head_g, dep=None):
    t = z.shape[0]
    tm = _row_tile(t, ROW_TILE_TARGET, CHUNK)
    cpt = tm // CHUNK
    scale = GLA_HK ** -0.5

    def body(z_ref, gw_ref, gb_ref, hg_ref, o_ref, mix_ref, st_ref, state, kdec_s, e_s):
        @pl.when(pl.program_id(0) == 0)
        def _():
            state[...] = jnp.zeros_like(state)

        cum3, tot3 = _chunk_decay(z_ref[:, R0:R0 + GATE_PAD], gw_ref, gb_ref, cpt)
        dec = jnp.exp(jnp.broadcast_to(tot3, cum3.shape) - cum3).reshape(tm, GLA_DK)
        kdec_s[...] = (z_ref[:, K0:K0 + GLA_DK].astype(F32) * dec).astype(BF16)
        e_s[...] = jnp.exp(jnp.broadcast_to(tot3, (cpt, 8, GLA_DK))).reshape(cpt * 8, GLA_DK)

        def chunk(ci, carry):
            rows = pl.ds(pl.multiple_of(ci * CHUNK, CHUNK), CHUNK)
            e_all = e_s[pl.ds(pl.multiple_of(ci * 8, 8), 8), :][0:1, :]
            st_ref[ci] = state[...].astype(BF16)
            for hd in range(GLA_HEADS):
                ks = slice(hd * GLA_HK, (hd + 1) * GLA_HK)
                vs = slice(hd * GLA_HV, (hd + 1) * GLA_HV)
                v = z_ref[rows, V0 + hd * GLA_HV:V0 + (hd + 1) * GLA_HV]
                st = state[vs, :] * e_all[:, ks] + _dot_tn(v, kdec_s[rows, ks])
                state[vs, :] = st
                q = z_ref[rows, Q0 + hd * GLA_HK:Q0 + (hd + 1) * GLA_HK]
                o_ref[rows, vs] = (_dot_nt(q, st.astype(BF16)) * scale).astype(BF16)
            return carry

        lax.fori_loop(0, cpt, chunk, 0, unroll=cpt)

        for hd in range(GLA_HEADS):
            vs = slice(hd * GLA_HV, (hd + 1) * GLA_HV)
            on = _rms(o_ref[:, vs].astype(F32), hg_ref[...])
            gv = z_ref[:, G0 + hd * GLA_HV:G0 + (hd + 1) * GLA_HV].astype(F32)
            mix_ref[:, vs] = (on * _silu(gv)).astype(BF16)

    return _call_after(
        dep, body, 4,
        [pl.BlockSpec((tm, GLA_IN_PAD), lambda i: (i, 0)),
         pl.BlockSpec((GATE_PAD, GLA_DK), lambda i: (0, 0)), pl.BlockSpec((1, GLA_DK), lambda i: (0, 0)),
         pl.BlockSpec((1, GLA_HV), lambda i: (0, 0))], (z, gate_w, gate_b, head_g), grid=(t // tm,),
        out_specs=[pl.BlockSpec((tm, GLA_DV), lambda i: (i, 0)), pl.BlockSpec((tm, GLA_DV), lambda i: (i, 0)),
                   pl.BlockSpec((cpt, GLA_DV, GLA_HK), lambda i: (i, 0, 0))],
        out_shape=[jax.ShapeDtypeStruct((t, GLA_DV), BF16), jax.ShapeDtypeStruct((t, GLA_DV), BF16),
                   jax.ShapeDtypeStruct((t // CHUNK, GLA_DV, GLA_HK), BF16)],
        scratch_shapes=[pltpu.VMEM((GLA_DV, GLA_HK), F32), pltpu.VMEM((tm, GLA_DK), BF16),
                        pltpu.VMEM((cpt * 8, GLA_DK), F32)],
        compiler_params=_cparams("arbitrary"), name="gla_seq_fwd")


def _gla_seq_bwd(dmix, o, z, states, gate_w, gate_b, head_g, dep=None):
    t = z.shape[0]
    tm = _row_tile(t, ROW_TILE_TARGET, CHUNK)
    cpt = tm // CHUNK
    nt = t // tm
    scale = GLA_HK ** -0.5

    def body(dmix_ref, o_ref, z_ref, st_ref, gw_ref, gb_ref, hg_ref, dz_ref, dgw_ref, dgb_ref, dhg_ref,
             dstate, dec_s, kdec_s, dkdec_s, do_s, e_s, dtot_s):
        @pl.when(pl.program_id(0) == 0)
        def _():
            dstate[...] = jnp.zeros_like(dstate)
            dgw_ref[...] = jnp.zeros_like(dgw_ref)
            dgb_ref[...] = jnp.zeros_like(dgb_ref)
            dhg_ref[...] = jnp.zeros_like(dhg_ref)

        cum3, tot3 = _chunk_decay(z_ref[:, R0:R0 + GATE_PAD], gw_ref, gb_ref, cpt)
        dec = jnp.exp(jnp.broadcast_to(tot3, cum3.shape) - cum3).reshape(tm, GLA_DK)
        dec_s[...] = dec
        kdec = z_ref[:, K0:K0 + GLA_DK].astype(F32) * dec
        kdec_s[...] = kdec
        e3 = jnp.exp(tot3)
        e_s[...] = jnp.broadcast_to(e3, (cpt, 8, GLA_DK)).reshape(cpt * 8, GLA_DK)
        dhg = jnp.zeros((1, GLA_HV), F32)
        for hd in range(GLA_HEADS):
            ks = slice(hd * GLA_HK, (hd + 1) * GLA_HK)
            vs = slice(hd * GLA_HV, (hd + 1) * GLA_HV)
            gcols = slice(G0 + hd * GLA_HV, G0 + (hd + 1) * GLA_HV)
            ov = o_ref[:, vs].astype(F32)
            gv = z_ref[:, gcols].astype(F32)
            dm = dmix_ref[:, vs].astype(F32)
            sg = jax.nn.sigmoid(gv)
            rr = lax.rsqrt(jnp.mean(ov * ov, axis=-1, keepdims=True) + EPS)
            xhat = ov * rr
            don = dm * (gv * sg)
            dz_ref[:, gcols] = (dm * (xhat * hg_ref[...]) * (sg * (1.0 + gv * (1.0 - sg)))).astype(BF16)
            dhg = dhg + jnp.sum(don * xhat, axis=0, keepdims=True)
            dxh = don * hg_ref[...]
            do = (rr * (dxh - xhat * jnp.mean(dxh * xhat, axis=-1, keepdims=True)) * scale).astype(BF16)
            do_s[:, vs] = do
            v3 = z_ref[:, V0 + hd * GLA_HV:V0 + (hd + 1) * GLA_HV].reshape(cpt, CHUNK, GLA_HV)
            kdb3 = kdec[:, ks].astype(BF16).reshape(cpt, CHUNK, GLA_HK)
            st3 = st_ref[:, vs, :].astype(F32) * e3[:, :, ks] + jnp.einsum("bcv,bck->bvk", v3, kdb3,
                                                                            preferred_element_type=F32)
            dq3 = jnp.einsum("bcv,bvk->bck", do.reshape(cpt, CHUNK, GLA_HV), st3.astype(BF16), preferred_element_type=F32)
            dz_ref[:, Q0 + hd * GLA_HK:Q0 + (hd + 1) * GLA_HK] = dq3.reshape(tm, GLA_HK).astype(BF16)
        dhg_ref[...] += dhg

        def chunk(cj, carry):
            ci = cpt - 1 - cj
            rows = pl.ds(pl.multiple_of(ci * CHUNK, CHUNK), CHUNK)
            erows = pl.ds(pl.multiple_of(ci * 8, 8), 8)
            e_all = e_s[erows, :][0:1, :]
            for hd in range(GLA_HEADS):
                ks = slice(hd * GLA_HK, (hd + 1) * GLA_HK)
                vs = slice(hd * GLA_HV, (hd + 1) * GLA_HV)
                e = e_all[:, ks]
                kdb = kdec_s[rows, ks].astype(BF16)
                v = z_ref[rows, V0 + hd * GLA_HV:V0 + (hd + 1) * GLA_HV]
                q = z_ref[rows, Q0 + hd * GLA_HK:Q0 + (hd + 1) * GLA_HK]
                do = do_s[rows, vs]
                st_prev = st_ref[ci, vs, :].astype(F32)
                dst = dstate[vs, :] + _dot_tn(do, q)
                dstb = dst.astype(BF16)
                dkdec_s[rows, ks] = _dot(v, dstb)
                dz_ref[rows, V0 + hd * GLA_HV:V0 + (hd + 1) * GLA_HV] = _dot_nt(kdb, dstb).astype(BF16)
                dtot = jnp.sum(dst * st_prev, axis=0, keepdims=True) * e
                dtot_s[erows, ks] = jnp.broadcast_to(dtot, (8, GLA_HK))
                dstate[vs, :] = dst * e
            return carry

        lax.fori_loop(0, cpt, chunk, 0, unroll=cpt)

        dkdec = dkdec_s[...]
        dz_ref[:, K0:K0 + GLA_DK] = (dkdec * dec_s[...]).astype(BF16)
        before = _chunk_sums(dkdec * kdec_s[...], cpt, True, 2)
        dtot3 = dtot_s[...].reshape(cpt, 8, GLA_DK)[:, 0:1, :]
        dlac = (jnp.broadcast_to(dtot3, before.shape) + before).reshape(tm, GLA_DK)
        pre = _dot(z_ref[:, R0:R0 + GATE_PAD], gw_ref[...]) + gb_ref[...]
        dpre = dlac * (1.0 / GATE_NORM) * (1.0 - jax.nn.sigmoid(pre))
        dpb = dpre.astype(BF16)
        dz_ref[:, R0:R0 + GATE_PAD] = _dot_nt(dpb, gw_ref[...]).astype(BF16)
        dgw_ref[...] += _dot_tn(z_ref[:, R0:R0 + GATE_PAD], dpb)
        dgb_ref[...] += jnp.sum(dpre, axis=0, keepdims=True)

    rev = lambda i: (nt - 1 - i, 0)
    return _call_after(
        dep, body, 7,
        [pl.BlockSpec((tm, GLA_DV), rev), pl.BlockSpec((tm, GLA_DV), rev), pl.BlockSpec((tm, GLA_IN_PAD), rev),
         pl.BlockSpec((cpt, GLA_DV, GLA_HK), lambda i: (nt - 1 - i, 0, 0)),
         pl.BlockSpec((GATE_PAD, GLA_DK), lambda i: (0, 0)), pl.BlockSpec((1, GLA_DK), lambda i: (0, 0)),
         pl.BlockSpec((1, GLA_HV), lambda i: (0, 0))],
        (dmix, o, z, states, gate_w, gate_b, head_g), grid=(nt,),
        out_specs=[pl.BlockSpec((tm, GLA_IN_PAD), rev), pl.BlockSpec((GATE_PAD, GLA_DK), lambda i: (0, 0)),
                   pl.BlockSpec((1, GLA_DK), lambda i: (0, 0)), pl.BlockSpec((1, GLA_HV), lambda i: (0, 0))],
        out_shape=[jax.ShapeDtypeStruct((t, GLA_IN_PAD), BF16), jax.ShapeDtypeStruct((GATE_PAD, GLA_DK), F32),
                   jax.ShapeDtypeStruct((1, GLA_DK), F32), jax.ShapeDtypeStruct((1, GLA_HV), F32)],
        scratch_shapes=[pltpu.VMEM((GLA_DV, GLA_HK), F32), pltpu.VMEM((tm, GLA_DK), F32), pltpu.VMEM((tm, GLA_DK), F32),
                        pltpu.VMEM((tm, GLA_DK), F32), pltpu.VMEM((tm, GLA_DV), BF16),
                        pltpu.VMEM((cpt * 8, GLA_DK), F32), pltpu.VMEM((cpt * 8, GLA_DK), F32)],
        compiler_params=_cparams("arbitrary"), name="gla_seq_bwd")


def _sum_slots(x, name):
    n, r, cdim = x.shape
    tr = _row_tile(r, 256, 8)

    def body(x_ref, o_ref):
        acc = x_ref[0].astype(F32)
        for j in range(1, n):
            acc = acc + x_ref[j].astype(F32)
        o_ref[...] = acc

    return pl.pallas_call(
        body, grid=(r // tr,),
        in_specs=[pl.BlockSpec((n, tr, cdim), lambda i: (0, i, 0))],
        out_specs=pl.BlockSpec((tr, cdim), lambda i: (i, 0)),
        out_shape=jax.ShapeDtypeStruct((r, cdim), F32),
        compiler_params=_cparams("parallel"), name=name)(x)


def _sum_own_and_slots(own, slots, dev_idx, name):
    _, _, r, cdim = own.shape
    n = slots.shape[0]
    tr = _row_tile(r, 256, 8)

    def body(s_ref, own_ref, *rest):
        acc = own_ref[...].astype(F32)
        for other in rest[:n - 1]:
            acc = acc + other[...].astype(F32)
        rest[n - 1][...] = acc

    def slot(dd):
        return pl.BlockSpec((None, tr, cdim), lambda i, s: ((s[0] + dd) % n, i, 0))

    mine = pl.BlockSpec((None, None, tr, cdim), lambda i, s: (s[0] // 2, s[0] % 2, i, 0))
    return pl.pallas_call(
        body,
        grid_spec=pltpu.PrefetchScalarGridSpec(
            num_scalar_prefetch=1, grid=(r // tr,), in_specs=[mine] + [slot(dd) for dd in range(1, n)],
            out_specs=pl.BlockSpec((tr, cdim), lambda i, s: (i, 0))),
        out_shape=jax.ShapeDtypeStruct((r, cdim), F32),
        compiler_params=_cparams("parallel"), name=name)(dev_idx, own, *([slots] * (n - 1)))


def _add2(a, b, name):
    r, cdim = a.shape
    tr = _row_tile(r, 256, 8)

    def body(a_ref, b_ref, o_ref):
        o_ref[...] = a_ref[...] + b_ref[...]

    spec = pl.BlockSpec((tr, cdim), lambda i: (i, 0))
    return pl.pallas_call(body, grid=(r // tr,), in_specs=[spec, spec], out_specs=spec,
                          out_shape=jax.ShapeDtypeStruct((r, cdim), F32),
                          compiler_params=_cparams("parallel"), name=name)(a, b)


def _adamw_half(w, gs, m, v, half_idx, prev, name, dep=None):
    nl, _, h, cdim = w.shape
    tr = _row_tile(h, 256, 8)
    nprev = 0 if prev is None else 4
    extra = [] if dep is None else [dep]

    def body(s_ref, w_ref, m_ref, v_ref, *rest):
        g_refs = rest[:nl]
        go_ref, d_ref, mo_ref, vo_ref = rest[nl + nprev + len(extra):]
        layer = pl.program_id(0)
        gv = g_refs[0][...]
        for j in range(1, nl):
            gv = jnp.where(layer == j, g_refs[j][...], gv)
        go_ref[...] = gv
        mn = ADAM_B1 * m_ref[...] + (1.0 - ADAM_B1) * gv
        vn = ADAM_B2 * v_ref[...] + (1.0 - ADAM_B2) * (gv * gv)
        m_hat = mn / (1.0 - ADAM_B1 ** ADAM_STEP)
        v_hat = vn / (1.0 - ADAM_B2 ** ADAM_STEP)
        d_ref[...] = -ADAM_LR * (m_hat / (jnp.sqrt(v_hat) + ADAM_EPS) + ADAM_WD * w_ref[...])
        mo_ref[...] = mn
        vo_ref[...] = vn

    half = pl.BlockSpec((None, None, tr, cdim), lambda l, i, s: (l, s[0], i, 0))

    def of_layer(j):
        return pl.BlockSpec((tr, cdim), lambda l, i, s: (jnp.where(l == j, i, 0), 0))

    shp = jax.ShapeDtypeStruct(w.shape, F32)
    return pl.pallas_call(
        body,
        grid_spec=pltpu.PrefetchScalarGridSpec(
            num_scalar_prefetch=1, grid=(nl, h // tr),
            in_specs=[half] * 3 + [of_layer(j) for j in range(nl)] + [ANY_SPEC] * (nprev + len(extra)),
            out_specs=[half] * 4),
        out_shape=[shp] * 4, input_output_aliases={4 + nl + k: k for k in range(nprev)},
        compiler_params=_cparams("arbitrary", "arbitrary"), name=name,
    )(half_idx, w, m, v, *gs, *([] if prev is None else prev), *extra)


def _adamw_many(ws, gs, ms, vs):
    n = len(ws)

    def body(*refs):
        for i in range(n):
            w_ref, g_ref, m_ref, v_ref = refs[i], refs[n + i], refs[2 * n + i], refs[3 * n + i]
            d_ref, mo_ref, vo_ref = refs[4 * n + i], refs[5 * n + i], refs[6 * n + i]
            gv = g_ref[...]
            mn = ADAM_B1 * m_ref[...] + (1.0 - ADAM_B1) * gv
            vn = ADAM_B2 * v_ref[...] + (1.0 - ADAM_B2) * (gv * gv)
            m_hat = mn / (1.0 - ADAM_B1 ** ADAM_STEP)
            v_hat = vn / (1.0 - ADAM_B2 ** ADAM_STEP)
            d_ref[...] = -ADAM_LR * (m_hat / (jnp.sqrt(v_hat) + ADAM_EPS) + ADAM_WD * w_ref[...])
            mo_ref[...] = mn
            vo_ref[...] = vn

    shapes = [jax.ShapeDtypeStruct(w.shape, F32) for w in ws]
    outs = pl.pallas_call(body, out_shape=shapes * 3, name="adamw_small")(*ws, *gs, *ms, *vs)
    return outs[:n], outs[n:2 * n], outs[2 * n:]


def _split_rows(a):
    return a.reshape(a.shape[0], 2, a.shape[1] // 2, a.shape[2])


def _place():
    x, y, c = lax.axis_index("x"), lax.axis_index("y"), lax.axis_index("c")
    chips = [(1 - x, y), (x, 1 - y), (1 - x, 1 - y)]
    return x, y, c, chips


def _remote(src, dst, send_sem, recv_sem, to):
    return pltpu.make_async_remote_copy(src_ref=src, dst_ref=dst, send_sem=send_sem, recv_sem=recv_sem,
                                        device_id=to, device_id_type=MESH)


def _plan_gather(n_halved):
    def plan(src_refs, land_refs):
        x, y, c, chips = _place()
        me = 2 * x + y
        copies = []
        for k, (src, land) in enumerate(zip(src_refs, land_refs)):
            for (px, py) in chips:
                frm = 2 * px + py
                if k < n_halved:
                    copies.append((src.at[c], land.at[me, c], (px, py, c), land.at[frm, c]))
                else:
                    copies.append((src, land.at[me], (px, py, c), land.at[frm]))
        return copies
    return plan


def _plan_share(src_refs, land_refs):
    x, y, c, chips = _place()
    me = 2 * x + y
    sib = (x, y, 1 - c)
    copies = []
    for src, land in zip(src_refs, land_refs):
        copies.append((src, land.at[me], sib, land.at[me]))
        for (px, py) in chips:
            frm = 2 * px + py
            copies.append((land.at[frm, c], land.at[frm, c], sib, land.at[frm, 1 - c]))
    return copies


def _plan_scatter(n_parts):
    def plan(src_refs, land_refs):
        x, y, c, chips = _place()
        me = 2 * x + y
        copies = []
        for k, (src, land) in enumerate(zip(src_refs, land_refs)):
            for (px, py) in chips:
                to = 2 * px + py
                copies.append((src.at[to] if k < n_parts else src, land.at[me], (px, py, c), land.at[to]))
        return copies
    return plan


N_DEVICES = 8
OTHER_DEVICES = [(dx, dy, dc) for dx in (0, 1) for dy in (0, 1) for dc in (0, 1) if dx or dy or dc]


def _plan_scatter_all(src_refs, land_refs):
    x, y, c, _ = _place()
    me = 4 * x + 2 * y + c
    copies = []
    for src, land in zip(src_refs, land_refs):
        for dx, dy, dc in OTHER_DEVICES:
            px, py, pc = (1 - x if dx else x), (1 - y if dy else y), (1 - c if dc else c)
            copies.append((src.at[2 * px + py, pc], land.at[me], (px, py, pc), land.at[4 * px + 2 * py + pc]))
    return copies


def _plan_exchange(n_split):
    def plan(src_refs, land_refs):
        x, y, c, _ = _place()
        sib = (x, y, 1 - c)
        return [(src.at[:, 1 - c] if k < n_split else src, land, sib, land)
                for k, (src, land) in enumerate(zip(src_refs, land_refs))]
    return plan


def _hbm(a):
    return pltpu.HBM(a.shape, a.dtype)


def _start_copies(name, srcs, lands, plan, ncopy, dep=None):
    ns, nl = len(srcs), len(lands)
    nin = ns + nl + (0 if dep is None else 1)

    def body(*refs):
        send_sems, recv_sems, token = refs[nin], refs[nin + 1], refs[-1]
        for k, (src, dst, dev, _) in enumerate(plan(refs[:ns], refs[ns:ns + nl])):
            _remote(src, dst, send_sems.at[k], recv_sems.at[k], dev).start()
        token[...] = jnp.zeros_like(token)

    args = [pltpu.with_memory_space_constraint(a, pltpu.HBM) for a in list(srcs) + list(lands)]
    outs = pl.pallas_call(
        body, name=name,
        out_shape=(pltpu.SemaphoreType.DMA((ncopy,)), pltpu.SemaphoreType.DMA((ncopy,)),
                   *[_hbm(a) for a in list(srcs) + list(lands)], jax.ShapeDtypeStruct((8, 128), F32)),
        in_specs=[HBM_SPEC] * (ns + nl) + ([] if dep is None else [ANY_SPEC]),
        out_specs=(SEM_SPEC, SEM_SPEC, *([HBM_SPEC] * (ns + nl)), pl.BlockSpec(memory_space=pltpu.VMEM)),
        input_output_aliases={i: 2 + i for i in range(ns + nl)},
        compiler_params=pltpu.CompilerParams(has_side_effects=SIDE_EFFECT),
    )(*args, *([] if dep is None else [dep]))
    return outs[0], outs[1], list(outs[2:2 + ns]), list(outs[2 + ns:2 + ns + nl]), outs[-1]


def _wait_copies(name, started, plan, after, sem_offset=0):
    send_sems, recv_sems, srcs, lands, _ = started
    ns, nl = len(srcs), len(lands)
    after = list(after) if isinstance(after, (list, tuple)) else [after]

    def body(*refs):
        send_ref, recv_ref = refs[ns + nl], refs[ns + nl + 1]
        for k, (src, _, dev, mine) in enumerate(plan(refs[:ns], refs[ns:ns + nl])):
            copy = _remote(src, mine, send_ref.at[sem_offset + k], recv_ref.at[sem_offset + k], dev)
            copy.wait_send()
            copy.wait_recv()

    outs = pl.pallas_call(
        body, name=name, out_shape=tuple(_hbm(a) for a in srcs + lands),
        in_specs=[HBM_SPEC] * (ns + nl) + [SEM_SPEC, SEM_SPEC] + [ANY_SPEC] * len(after),
        out_specs=tuple([HBM_SPEC] * (ns + nl)),
        input_output_aliases={i: i for i in range(ns + nl)},
        compiler_params=pltpu.CompilerParams(has_side_effects=SIDE_EFFECT),
    )(*srcs, *lands, send_sems, recv_sems, *after)
    return list(outs[:ns]), list(outs[ns:])


def _share_with_sibling(name, srcs, lands):
    n = len(srcs)

    def body(*refs):
        src_refs, land_refs, out_refs = refs[:n], refs[n:2 * n], refs[2 * n:3 * n]
        send_sem, recv_sem = refs[3 * n:]
        x, y, c, chips = _place()
        me = 2 * x + y
        sib = (x, y, 1 - c)
        sends, recvs = [], []
        for k in range(n):
            sems = (send_sem.at[4 * k], recv_sem.at[4 * k])
            sends.append(_remote(src_refs[k], out_refs[k].at[me], *sems, sib))
            recvs.append(_remote(src_refs[k], out_refs[k].at[me], *sems, sib))
            for j, (px, py) in enumerate(chips):
                frm = 2 * px + py
                sems = (send_sem.at[4 * k + 1 + j], recv_sem.at[4 * k + 1 + j])
                sends.append(_remote(land_refs[k].at[frm, c], out_refs[k].at[frm, c], *sems, sib))
                recvs.append(_remote(land_refs[k].at[frm, c], out_refs[k].at[frm, 1 - c], *sems, sib))
        for cp in sends:
            cp.start()
        for cp in recvs:
            cp.wait_recv()
        for cp in sends:
            cp.wait_send()

    return pl.pallas_call(
        body, name=name, in_specs=[HBM_SPEC] * (2 * n), out_specs=[HBM_SPEC] * n,
        out_shape=[jax.ShapeDtypeStruct(a.shape, a.dtype) for a in lands],
        input_output_aliases={n + k: k for k in range(n)},
        scratch_shapes=[pltpu.SemaphoreType.DMA((4 * n,)), pltpu.SemaphoreType.DMA((4 * n,))],
    )(*srcs, *lands)


def _pack(arrs):
    flat = jnp.concatenate([a.reshape(-1).astype(F32) for a in arrs])
    n = flat.shape[0]
    rows = -(-n // PACK_WIDTH)
    rows = -(-rows // 8) * 8
    return jnp.pad(flat, (0, rows * PACK_WIDTH - n)).reshape(rows, PACK_WIDTH)


def _unpack(buf, shapes):
    flat = buf.reshape(-1)
    out, off = [], 0
    for shp in shapes:
        n = 1
        for s in shp:
            n *= s
        out.append(flat[off:off + n].reshape(shp))
        off += n
    return out


def _unshard_cols(stacked):
    moved = jnp.moveaxis(stacked, 0, -2)
    return moved.reshape(moved.shape[:-2] + (moved.shape[-2] * moved.shape[-1],))


def _take_cols(blocks, start, width):
    bw = blocks.shape[2]
    pieces, lo = [], start
    while lo < start + width:
        b = lo // bw
        hi = min(start + width, (b + 1) * bw)
        pieces.append(blocks[b][:, lo - b * bw:hi - b * bw])
        lo = hi
    return jnp.concatenate(pieces, axis=1)


def _col_shard(full, s, width):
    return lax.dynamic_slice_in_dim(full, s * width, width, axis=full.ndim - 1)


def kernel(x, meta_tokens, mix_norm_g, ffn_norm_g, ffn_w1, ffn_w2, cp_w_in, cp_conv_w, cp_conv_b, cp_ln_g, cp_ln_b, cp_pool_w, cp_pool_scale, cp_w_out, gla_w_in, gla_gate_w2, gla_gate_b, gla_head_g, gla_w_out, final_norm_g, loss_target, m_meta_tokens, m_mix_norm_g, m_ffn_norm_g, m_ffn_w1, m_ffn_w2, m_cp_w_in, m_cp_conv_w, m_cp_conv_b, m_cp_ln_g, m_cp_ln_b, m_cp_pool_w, m_cp_pool_scale, m_cp_w_out, m_gla_w_in, m_gla_gate_w2, m_gla_gate_b, m_gla_head_g, m_gla_w_out, m_final_norm_g, v_meta_tokens, v_mix_norm_g, v_ffn_norm_g, v_ffn_w1, v_ffn_w2, v_cp_w_in, v_cp_conv_w, v_cp_conv_b, v_cp_ln_g, v_cp_ln_b, v_cp_pool_w, v_cp_pool_scale, v_cp_w_out, v_gla_w_in, v_gla_gate_w2, v_gla_gate_b, v_gla_head_g, v_gla_w_out, v_final_norm_g):
    d = D_MODEL
    chip = 2 * lax.axis_index("x") + lax.axis_index("y")
    core = lax.axis_index("c")
    seq = x.shape[1]
    t = seq + CHUNK

    sharded_small = [meta_tokens, cp_conv_w, gla_gate_w2, gla_gate_b, gla_head_g]

    def halves(w, token=None):
        if token is not None:
            w = w + token[0, 0]
        return w.astype(BF16).reshape(2, w.shape[0] // 2, w.shape[1])

    def unhalve(g):
        return g.reshape(N_CHIPS, 2 * g.shape[2], g.shape[3])

    def gather_group(srcs, whole=()):
        lands = [lax.empty((N_CHIPS,) + s.shape, s.dtype) for s in srcs]
        for a in whole:
            lands.append(lax.dynamic_update_slice(jnp.zeros((N_CHIPS,) + a.shape, a.dtype), a[None], (chip,) + (0,) * a.ndim))
        return list(srcs) + list(whole), lands, _plan_gather(len(srcs)), len(srcs)

    def start_groups(name, groups, dep):
        bounds, all_srcs, all_lands = [], [], []
        for srcs, lands, _, _ in groups:
            bounds.append((len(all_srcs), len(all_srcs) + len(srcs)))
            all_srcs += srcs
            all_lands += lands

        def plan_all(src_refs, land_refs):
            return [cp for (lo, hi), group in zip(bounds, groups) for cp in group[2](src_refs[lo:hi], land_refs[lo:hi])]

        started = _start_copies(name, all_srcs, all_lands, plan_all, 3 * len(all_srcs), dep)
        return [(started, bound, group[2], group[3]) for bound, group in zip(bounds, groups)]

    def arrived(name, gather, after):
        started, (lo, hi), plan, n = gather
        mine = (started[0], started[1], started[2][lo:hi], started[3][lo:hi], started[4])
        srcs, lands = _wait_copies(name + "_wait", mine, plan, after, sem_offset=3 * lo)
        return srcs[:n], lands[:n], lands[n:]

    (cp_gather,) = start_groups("gather_cp_start", [gather_group([halves(cp_w_in[0]), halves(cp_w_out[0])],
                                                                 [_pack(sharded_small)])], None)
    tok = cp_gather[0][-1]
    ffn0_gather, gla_gather, ffn1_gather = start_groups(
        "gather_start", [gather_group([halves(ffn_w1[0], tok), halves(ffn_w2[0], tok)]),
                         gather_group([halves(gla_w_in[0], tok), halves(gla_w_out[0], tok)]),
                         gather_group([halves(ffn_w1[1], tok), halves(ffn_w2[1], tok)])], tok)
    h0_rows = jnp.concatenate([jnp.zeros((CHUNK, d), F32) + ffn0_gather[0][-1][0, 0], x[0]], axis=0)
    cp_srcs, cp_lands, (small_g,) = arrived("gather_cp", cp_gather, h0_rows)
    cpin_g, cpout_g = [unhalve(g) for g in _share_with_sibling("gather_cp_share", cp_srcs, cp_lands)]
    per_chip = [_unpack(small_g[j], [a.shape for a in sharded_small]) for j in range(N_CHIPS)]
    meta_f, conv_w_f, gate_w_f, gate_b_f, head_g_f = [
        jnp.concatenate([per_chip[j][i] for j in range(N_CHIPS)], axis=-1) for i in range(len(sharded_small))]
    conv_w_f, gate_w_f = conv_w_f[0], gate_w_f[0]
    w_cp_in = _unshard_cols(cpin_g)
    w_cp_out = cpout_g.reshape(CONV_DIM + POOL_DIM, d)
    gate_w_pad = jnp.pad(gate_w_f, ((0, GATE_PAD - GATE_RANK), (0, 0))).astype(BF16)
    row = lambda a: a.reshape(1, -1)
    c_idx = core.reshape(1).astype(jnp.int32)

    h0 = lax.dynamic_update_slice(h0_rows, meta_f, (PAD_ROWS, 0))
    z0, u0 = _norm_matmul(h0, row(mix_norm_g[0]), w_cp_in, 512, "cp_in_proj")
    c0, pm0, mix0 = _cp_seq_fwd(z0, conv_w_f, cp_conv_b, cp_ln_g, cp_ln_b, cp_pool_w[0], cp_pool_scale)
    ffn0_srcs, ffn0_lands, _ = arrived("gather_ffn0", ffn0_gather, mix0)
    ffn0_share = _start_copies("gather_ffn0_share_start", ffn0_srcs, ffn0_lands, _plan_share, 4 * len(ffn0_srcs))
    h1 = _matmul_residual(mix0, w_cp_out, h0, "cp_out_proj", dep=ffn0_share[-1])
    w1g0, w2g0 = [unhalve(g) for g in _wait_copies("gather_ffn0_share_wait", ffn0_share, _plan_share, h1)[1]]
    h2, hp0, uf0 = _ffn_fwd(h1, row(ffn_norm_g[0]), w1g0, w2g0, "ffn0_fwd")
    gla_srcs, gla_lands, _ = arrived("gather_gla", gla_gather, h2)
    glain_g, glaout_g = [unhalve(g) for g in _share_with_sibling("gather_gla_share", gla_srcs, gla_lands)]
    w_gla_in = jnp.concatenate([glain_g[j] for j in range(N_CHIPS)] + [jnp.zeros((d, GLA_IN_PAD - GLA_IN), BF16)], axis=1)
    w_gla_out = glaout_g.reshape(GLA_DV, d)
    z1, u2 = _norm_matmul(h2, row(mix_norm_g[1]), w_gla_in, GLA_COLS, "gla_in_proj")
    ffn1_srcs, ffn1_lands, _ = arrived("gather_ffn1", ffn1_gather, z1)
    ffn1_share = _start_copies("gather_ffn1_share_start", ffn1_srcs, ffn1_lands, _plan_share, 4 * len(ffn1_srcs))
    o1, mix1, states = _gla_seq_fwd(z1, gate_w_pad, gate_b_f, head_g_f, dep=ffn1_share[-1])
    h3 = _matmul_residual(mix1, w_gla_out, h2, "gla_out_proj")
    w1g1, w2g1 = [unhalve(g) for g in _wait_copies("gather_ffn1_share_wait", ffn1_share, _plan_share, h3)[1]]
    h4, hp1, uf1 = _ffn_fwd(h3, row(ffn_norm_g[1]), w1g1, w2g1, "ffn1_fwd")

    dev_idx = (2 * chip + core).reshape(1).astype(jnp.int32)

    def start_reduce(name, grads):
        srcs = [_split_rows(g) for g in grads]
        lands = [lax.empty((N_DEVICES,) + s.shape[2:], s.dtype) for s in srcs]
        return _start_copies(name + "_scatter_start", srcs, lands, _plan_scatter_all, len(OTHER_DEVICES) * len(srcs))

    def finish_reduce(name, started, after):
        srcs, lands = _wait_copies(name + "_scatter_wait", started, _plan_scatter_all, after)
        return [_sum_own_and_slots(s, l, dev_idx, "%s_slot_sum_%d" % (name, k)) for k, (s, l) in enumerate(zip(srcs, lands))]

    dh4, d_final_g, loss_part = _loss_bwd(h4, row(final_norm_g), loss_target[0])

    dh3, dhp1, d_ffn_g1 = _ffn_bwd_data(dh4, h3, row(ffn_norm_g[1]), hp1, w1g1, w2g1, "ffn1_bwd")
    dw1_1 = _wgrad(uf1, dhp1, N_CHIPS, d, d, False, True, False, "ffn1_dw1", rows=WGRAD_ROWS_BF16)
    dw2_1 = _wgrad(hp1, dh4, N_CHIPS, d, d, True, False, True, "ffn1_dw2")
    ffn1_reduce = start_reduce("ffn1", [dw1_1, dw2_1])

    dmix1 = _dgrad(dh3, w_gla_out, "gla_out_dgrad", dep=ffn1_reduce[-1])
    dw_gla_out = _wgrad(mix1, dh3, 1, GLA_DV, d, False, False, False, "gla_out_dw")
    dz1, d_gate_w, d_gate_b, d_head_g = _gla_seq_bwd(dmix1, o1, z1, states, gate_w_pad, gate_b_f, head_g_f)
    dh2, d_mix_g1 = _dgrad_norm_bwd(dz1, w_gla_in, h2, row(mix_norm_g[1]), dh3, GLA_COLS, "gla_in_dgrad")
    dw_gla_in = _wgrad(u2, dz1, GLA_IN_PAD // 640, d, 640, False, True, False, "gla_in_dw", rows=WGRAD_ROWS_BF16)
    gla_in_shards = jnp.stack([_take_cols(dw_gla_in, j * (GLA_IN // N_CHIPS), GLA_IN // N_CHIPS) for j in range(N_CHIPS)])
    gla_reduce = start_reduce("gla", [gla_in_shards, dw_gla_out.reshape(N_CHIPS, -1, d)])

    dh1, dhp0, d_ffn_g0 = _ffn_bwd_data(dh2, h1, row(ffn_norm_g[0]), hp0, w1g0, w2g0, "ffn0_bwd", dep=gla_reduce[-1])
    dw1_0 = _wgrad(uf0, dhp0, N_CHIPS, d, d, False, True, False, "ffn0_dw1", rows=WGRAD_ROWS_BF16)
    dw2_0 = _wgrad(hp0, dh2, N_CHIPS, d, d, True, False, True, "ffn0_dw2")
    ffn0_reduce = start_reduce("ffn0", [dw1_0, dw2_0])

    dw_cp_out = _wgrad(mix0, dh1, 1, CONV_DIM + POOL_DIM, d, False, False, False, "cp_out_dw", dep=ffn0_reduce[-1])
    dz0, d_conv_w, d_cp_vec, d_pool_w = _cp_seq_bwd(dh1, w_cp_out, z0, c0, pm0, conv_w_f, cp_ln_g, cp_ln_b, cp_pool_w[0],
                                                    cp_pool_scale)
    grad_x, dh0_head, d_mix_g0 = _dgrad_norm_bwd_input(dz0, w_cp_in, h0, row(mix_norm_g[0]), dh1, 512, "cp_in_dgrad")
    grad_x = grad_x[None]
    dw_cp_in = _wgrad(u0, dz0, 1, d, CP_IN, False, False, False, "cp_in_dw")
    dw_cp_in = jnp.stack([_take_cols(dw_cp_in, j * (CP_IN // N_CHIPS), CP_IN // N_CHIPS) for j in range(N_CHIPS)])

    cp_reduce = start_reduce("cp", [dw_cp_in, dw_cp_out.reshape(N_CHIPS, -1, d)])
    small_full = [dh0_head[PAD_ROWS:CHUNK],jnp.concatenate([d_mix_g0, d_mix_g1], axis=0),
                  jnp.concatenate([d_ffn_g0, d_ffn_g1], axis=0), d_conv_w[:CONV_WIDTH][None],
                  d_cp_vec[0:1], d_cp_vec[1:2], d_cp_vec[2:3], d_pool_w[None], d_cp_vec[3:4],
                  d_gate_w[:GATE_RANK][None], d_gate_b, d_head_g, d_final_g[0], loss_part[0, 0:1]]
    small_mine = _pack(small_full)
    whole = _plan_exchange(0)
    small_exchange = _start_copies("small_exchange_start", [small_mine], [lax.empty(small_mine.shape, F32)], whole, 1,
                                   dep=cp_reduce[-1])
    red_ffn1 = finish_reduce("ffn1", ffn1_reduce, small_exchange[-1])
    red_gla = finish_reduce("gla", gla_reduce, small_exchange[-1])
    (small_sent,), (small_recv,) = _wait_copies("small_exchange_wait", small_exchange, whole, [red_ffn1[1], red_gla[1]])
    small_chip = _add2(small_sent, small_recv, "chip_sum_small")
    small_slots = lax.dynamic_update_slice(jnp.zeros((N_CHIPS,) + small_chip.shape, F32), small_chip[None], (chip, 0, 0))
    small_reduce = _start_copies("small_scatter_start", [small_chip], [small_slots], _plan_scatter(0), 3)

    big = {"w1": (ffn_w1, m_ffn_w1, v_ffn_w1), "w2": (ffn_w2, m_ffn_w2, v_ffn_w2),
           "cp_in": (cp_w_in, m_cp_w_in, v_cp_w_in), "cp_out": (cp_w_out, m_cp_w_out, v_cp_w_out),
           "gla_in": (gla_w_in, m_gla_w_in, v_gla_w_in), "gla_out": (gla_w_out, m_gla_w_out, v_gla_w_out)}
    other_idx = (1 - core).reshape(1).astype(jnp.int32)

    def adamw_by_halves(tag, reduced, dep=None):
        flat = [r for n in reduced for r in reduced[n]]
        join_plan = _plan_exchange(0)
        join = _start_copies(tag + "_join_start", flat, [lax.empty(r.shape, F32) for r in flat], join_plan, len(flat),
                             dep=dep)
        views = {n: [_split_rows(a) for a in big[n]] for n in reduced}
        own, k = {}, 0
        for n in reduced:
            mine = join[2][k:k + len(reduced[n])]
            k += len(reduced[n])
            own[n] = _adamw_half(views[n][0], mine, views[n][1], views[n][2], c_idx, None, "adamw_%s_own" % n)
        _, arrived_halves = _wait_copies(tag + "_join_wait", join, join_plan, [own[n][1] for n in reduced])
        outs, k = {}, 0
        for n in reduced:
            theirs = arrived_halves[k:k + len(reduced[n])]
            k += len(reduced[n])
            res = _adamw_half(views[n][0], theirs, views[n][1], views[n][2], other_idx, own[n], "adamw_%s_sibling" % n)
            outs[n] = [o.reshape(big[n][0].shape) for o in res]
        return outs

    big_out = adamw_by_halves("gla", {"gla_in": [red_gla[0]], "gla_out": [red_gla[1]]}, dep=small_reduce[-1])
    red_ffn0 = finish_reduce("ffn0", ffn0_reduce, big_out["gla_out"][1])
    big_out.update(adamw_by_halves("ffn", {"w1": [red_ffn0[0], red_ffn1[0]], "w2": [red_ffn0[1], red_ffn1[1]]}))
    red_cp = finish_reduce("cp", cp_reduce, big_out["w2"][1])
    _, (small_landed,) = _wait_copies("small_scatter_wait", small_reduce, _plan_scatter(0), big_out["w2"][1])
    small_red = _sum_slots(small_landed, "slot_sum_small")
    big_out.update(adamw_by_halves("cp", {"cp_in": [red_cp[0]], "cp_out": [red_cp[1]]}))

    (g_meta, g_mix, g_ffn, g_conv_w, g_conv_b, g_ln_g, g_ln_b, g_pool_w, g_pool_scale, g_gate_w, g_gate_b, g_head,
     g_final, loss_sum) = _unpack(small_red, [a.shape for a in small_full])
    g_meta = _col_shard(g_meta, chip, meta_tokens.shape[-1])
    g_conv_w = _col_shard(g_conv_w, chip, cp_conv_w.shape[-1])
    g_gate_w = _col_shard(g_gate_w, chip, gla_gate_w2.shape[-1])
    g_gate_b = _col_shard(g_gate_b, chip, gla_gate_b.shape[-1])
    g_head = _col_shard(g_head, chip, gla_head_g.shape[-1])
    small_w = [meta_tokens, mix_norm_g, ffn_norm_g, cp_conv_w, cp_conv_b, cp_ln_g, cp_ln_b, cp_pool_w, cp_pool_scale,
               gla_gate_w2, gla_gate_b, gla_head_g, final_norm_g]
    small_m = [m_meta_tokens, m_mix_norm_g, m_ffn_norm_g, m_cp_conv_w, m_cp_conv_b, m_cp_ln_g, m_cp_ln_b, m_cp_pool_w,
               m_cp_pool_scale, m_gla_gate_w2, m_gla_gate_b, m_gla_head_g, m_final_norm_g]
    small_v = [v_meta_tokens, v_mix_norm_g, v_ffn_norm_g, v_cp_conv_w, v_cp_conv_b, v_cp_ln_g, v_cp_ln_b, v_cp_pool_w,
               v_cp_pool_scale, v_gla_gate_w2, v_gla_gate_b, v_gla_head_g, v_final_norm_g]
    small_g = [g_meta, g_mix, g_ffn, g_conv_w, g_conv_b, g_ln_g, g_ln_b, g_pool_w, g_pool_scale, g_gate_w, g_gate_b,
               g_head, g_final]
    shapes = [w.shape for w in small_w]
    small_g = [g.reshape(s) for g, s in zip(small_g, shapes)]
    at_least_2d = lambda arrs: [a.reshape(1, -1) if a.ndim == 1 else a for a in arrs]
    s_delta, s_m, s_v = _adamw_many(at_least_2d(small_w), at_least_2d(small_g), at_least_2d(small_m), at_least_2d(small_v))
    s_delta, s_m, s_v = [[a.reshape(s) for a, s in zip(group, shapes)] for group in (s_delta, s_m, s_v)]

    order = ["meta", "mix", "ffn", "w1", "w2", "cp_in", "conv_w", "conv_b", "ln_g", "ln_b", "pool_w", "pool_scale",
             "cp_out", "gla_in", "gate_w", "gate_b", "head", "gla_out", "final"]
    small_names = ["meta", "mix", "ffn", "conv_w", "conv_b", "ln_g", "ln_b", "pool_w", "pool_scale", "gate_w", "gate_b",
                   "head", "final"]
    big_names = ["w1", "w2", "cp_in", "cp_out", "gla_in", "gla_out"]
    table = {n: (small_g[i], s_delta[i], s_m[i], s_v[i]) for i, n in enumerate(small_names)}
    table.update({n: tuple(big_out[n]) for n in big_names})
    loss = loss_sum.reshape(())
    return (loss, grad_x, *[table[n][0] for n in order], *[table[n][1] for n in order],
            *[table[n][2] for n in order], *[table[n][3] for n in order])
```

```python
import functools

import jax
import jax.numpy as jnp
from jax import lax
from jax.experimental import pallas as pl
from jax.experimental.pallas import tpu as pltpu

F32 = jnp.float32
BF16 = jnp.bfloat16

D_MODEL = 1024
N_META = 16
CHUNK = 64
PAD_ROWS = CHUNK - N_META
EPS = 1e-5
CONV_DIM = 512
CONV_WIDTH = 31
CONV_HALO = 32
POOL_DIM = 512
POOL_WINDOWS = (2, 4, 8, 16)
POOL_GROUP = 128
POOL_HALO = 16
CP_IN = 2 * CONV_DIM + POOL_DIM
GLA_HEADS = 4
GLA_DK = 512
GLA_DV = 1024
GLA_HK = GLA_DK // GLA_HEADS
GLA_HV = GLA_DV // GLA_HEADS
GATE_RANK = 16
GATE_PAD = 128
GATE_NORM = 16.0
GLA_IN = 2 * GLA_DK + 2 * GLA_DV + GATE_RANK
GLA_IN_PAD = 2 * GLA_DK + 2 * GLA_DV + GATE_PAD
GLA_COLS = 1280
N_CHIPS = 4
ADAM_LR = 0.001
ADAM_B1 = 0.9
ADAM_B2 = 0.999
ADAM_EPS = 1e-08
ADAM_WD = 0.01
ADAM_STEP = 10

VMEM_LIMIT_BYTES = 56 * 1024 * 1024
ROW_TILE_TARGET = 832
TOKEN_TILE_TARGET = 1040
PACK_WIDTH = 1024
MESH = pl.DeviceIdType.MESH
HBM_SPEC = pl.BlockSpec(memory_space=pltpu.HBM)
ANY_SPEC = pl.BlockSpec(memory_space=pl.ANY)
SEM_SPEC = pl.BlockSpec(memory_space=pltpu.SEMAPHORE)
SIDE_EFFECT = pltpu.SideEffectType.DATAFLOW_SIDE_EFFECTING


def _cparams(*sem):
    return pltpu.CompilerParams(dimension_semantics=sem, vmem_limit_bytes=VMEM_LIMIT_BYTES)


def _row_tile(t, target, mult):
    best = mult
    for cand in range(mult, min(t, target) + 1, mult):
        if t % cand == 0:
            best = cand
    assert t % best == 0, (t, best)
    return best


def _rms(h, g):
    return h * lax.rsqrt(jnp.mean(h * h, axis=-1, keepdims=True) + EPS) * g


def _rms_bwd(h, g, du):
    r = lax.rsqrt(jnp.mean(h * h, axis=-1, keepdims=True) + EPS)
    xhat = h * r
    dxh = du * g
    dh = r * (dxh - xhat * jnp.mean(dxh * xhat, axis=-1, keepdims=True))
    return dh, du * xhat


def _valid_rows(i, tm):
    row = i * tm + lax.broadcasted_iota(jnp.int32, (tm, 1), 0)
    return row >= PAD_ROWS


def _dot(a, b):
    return jnp.dot(a, b, preferred_element_type=F32)


def _dot_nt(a, b):
    return lax.dot_general(a, b, (((1,), (1,)), ((), ())), preferred_element_type=F32)


def _dot_tn(a, b):
    return lax.dot_general(a, b, (((0,), (0,)), ((), ())), preferred_element_type=F32)


def _accumulate(ref, val, first):
    @pl.when(first)
    def _():
        ref[...] = val

    @pl.when(jnp.logical_not(first))
    def _():
        ref[...] += val


def _call_after(dep, body, n_in, in_specs, args, **kw):
    if dep is None:
        return pl.pallas_call(body, in_specs=in_specs, **kw)(*args)

    def with_dep(*refs):
        body(*refs[:n_in], *refs[n_in + 1:])

    return pl.pallas_call(with_dep, in_specs=list(in_specs) + [ANY_SPEC], **kw)(*args, dep)


def _norm_matmul(h, g, w, nc, name, dep=None):
    t, d = h.shape
    n = w.shape[1]
    tm = _row_tile(t, TOKEN_TILE_TARGET, 16)

    def body(h_ref, g_ref, w_ref, z_ref, u_ref):
        u = _rms(h_ref[...], g_ref[...]).astype(BF16)
        u_ref[...] = u
        for n0 in range(0, n, nc):
            n1 = min(n0 + nc, n)
            z_ref[:, n0:n1] = _dot(u, w_ref[:, n0:n1]).astype(BF16)

    return _call_after(
        dep, body, 3,
        [pl.BlockSpec((tm, d), lambda i: (i, 0)), pl.BlockSpec((1, d), lambda i: (0, 0)),
         pl.BlockSpec((d, n), lambda i: (0, 0))], (h, g, w), grid=(t // tm,),
        out_specs=[pl.BlockSpec((tm, n), lambda i: (i, 0)), pl.BlockSpec((tm, d), lambda i: (i, 0))],
        out_shape=[jax.ShapeDtypeStruct((t, n), BF16), jax.ShapeDtypeStruct((t, d), BF16)],
        compiler_params=_cparams("parallel"), name=name)


def _matmul_residual(a, w, h, name, dep=None):
    t, k = a.shape
    d = w.shape[1]
    tm = _row_tile(t, TOKEN_TILE_TARGET, 16)

    def body(a_ref, w_ref, h_ref, o_ref):
        o_ref[...] = h_ref[...] + _dot(a_ref[...], w_ref[...])

    return _call_after(
        dep, body, 3,
        [pl.BlockSpec((tm, k), lambda i: (i, 0)), pl.BlockSpec((k, d), lambda i: (0, 0)),
         pl.BlockSpec((tm, d), lambda i: (i, 0))], (a, w, h), grid=(t // tm,),
        out_specs=pl.BlockSpec((tm, d), lambda i: (i, 0)),
        out_shape=jax.ShapeDtypeStruct((t, d), F32),
        compiler_params=_cparams("parallel"), name=name)


def _ffn_fwd(h, g, w1g, w2g, name):
    t, d = h.shape
    ns, ffs = w1g.shape[0], w1g.shape[2]
    tm = _row_tile(t, TOKEN_TILE_TARGET, 16)

    def body(h_ref, g_ref, w1_ref, w2_ref, ho_ref, hp_ref, u_ref, acc_ref):
        s = pl.program_id(1)

        @pl.when(s == 0)
        def _():
            u_ref[...] = _rms(h_ref[...], g_ref[...]).astype(BF16)

        hp = _dot(u_ref[...], w1_ref[...])
        hp_ref[...] = hp.astype(BF16)
        a = jnp.maximum(hp, 0.0)
        _accumulate(acc_ref, _dot((a * a).astype(BF16), w2_ref[...]), s == 0)

        @pl.when(s == ns - 1)
        def _():
            ho_ref[...] = h_ref[...] + acc_ref[...]

    return pl.pallas_call(
        body, grid=(t // tm, ns),
        in_specs=[pl.BlockSpec((tm, d), lambda i, s: (i, 0)), pl.BlockSpec((1, d), lambda i, s: (0, 0)),
                  pl.BlockSpec((None, d, ffs), lambda i, s: (s, 0, 0)),
                  pl.BlockSpec((None, ffs, d), lambda i, s: (s, 0, 0))],
        out_specs=[pl.BlockSpec((tm, d), lambda i, s: (i, 0)), pl.BlockSpec((tm, ffs), lambda i, s: (i, s)),
                   pl.BlockSpec((tm, d), lambda i, s: (i, 0))],
        out_shape=[jax.ShapeDtypeStruct((t, d), F32), jax.ShapeDtypeStruct((t, ns * ffs), BF16),
                   jax.ShapeDtypeStruct((t, d), BF16)],
        scratch_shapes=[pltpu.VMEM((tm, d), F32)],
        compiler_params=_cparams("parallel", "arbitrary"), name=name)(h, g, w1g, w2g)


def _ffn_bwd_data(dh, h, g, hp, w1g, w2g, name, dep=None):
    t, d = h.shape
    ns, ffs = w1g.shape[0], w1g.shape[2]
    tm = _row_tile(t, ROW_TILE_TARGET, CHUNK)

    def body(dh_ref, h_ref, g_ref, hp_ref, w1_ref, w2_ref, dhi_ref, dhp_ref, dg_ref, acc_ref):
        i, s = pl.program_id(0), pl.program_id(1)
        da = _dot_nt(dh_ref[...].astype(BF16), w2_ref[...])
        dhp = (da * (2.0 * jnp.maximum(hp_ref[...].astype(F32), 0.0))).astype(BF16)
        dhp_ref[...] = dhp
        _accumulate(acc_ref, _dot_nt(dhp, w1_ref[...]), s == 0)

        @pl.when(s == ns - 1)
        def _():
            dhn, dgr = _rms_bwd(h_ref[...], g_ref[...], acc_ref[...])
            dhi_ref[...] = jnp.where(_valid_rows(i, tm), dh_ref[...] + dhn, 0.0)
            _accumulate(dg_ref, jnp.sum(dgr, axis=0, keepdims=True), i == 0)

    return _call_after(
        dep, body, 6,
        [pl.BlockSpec((tm, d), lambda i, s: (i, 0)), pl.BlockSpec((tm, d), lambda i, s: (i, 0)),
         pl.BlockSpec((1, d), lambda i, s: (0, 0)), pl.BlockSpec((tm, ffs), lambda i, s: (i, s)),
         pl.BlockSpec((None, d, ffs), lambda i, s: (s, 0, 0)),
         pl.BlockSpec((None, ffs, d), lambda i, s: (s, 0, 0))], (dh, h, g, hp, w1g, w2g), grid=(t // tm, ns),
        out_specs=[pl.BlockSpec((tm, d), lambda i, s: (i, 0)), pl.BlockSpec((tm, ffs), lambda i, s: (i, s)),
                   pl.BlockSpec((1, d), lambda i, s: (0, 0))],
        out_shape=[jax.ShapeDtypeStruct((t, d), F32), jax.ShapeDtypeStruct((t, ns * ffs), BF16),
                   jax.ShapeDtypeStruct((1, d), F32)],
        scratch_shapes=[pltpu.VMEM((tm, d), F32)],
        compiler_params=_cparams("arbitrary", "arbitrary"), name=name)


WGRAD_ROWS = 2048
WGRAD_ROWS_BF16 = 4096


def _wgrad(x, dy, nb, xc, yc, x_by_block, dy_by_block, relu2, name, dep=None, rows=WGRAD_ROWS):
    t = x.shape[0]
    tk = _row_tile(t - CHUNK, rows, CHUNK)

    def prep(xv):
        if relu2:
            xv = jnp.maximum(xv.astype(F32), 0.0)
            xv = xv * xv
        return xv.astype(BF16)

    nk = (t - CHUNK) // tk

    def body(xh_ref, dyh_ref, x_ref, dy_ref, o_ref, acc_ref):
        k = pl.program_id(1)
        p = _dot_tn(prep(x_ref[...]), dy_ref[...].astype(BF16))

        @pl.when(k == 0)
        def _():
            acc_ref[...] = p + _dot_tn(prep(xh_ref[...]), dyh_ref[...].astype(BF16))

        @pl.when(k > 0)
        def _():
            acc_ref[...] += p

        @pl.when(k == nk - 1)
        def _():
            o_ref[...] = acc_ref[...].astype(BF16)

    def head(width, by_block):
        return pl.BlockSpec((CHUNK, width), (lambda b, k: (0, b)) if by_block else (lambda b, k: (0, 0)))

    def rest(width, by_block):
        def index(b, k):
            return pl.multiple_of(CHUNK + k * tk, CHUNK), (pl.multiple_of(b * width, 128) if by_block else 0)
        return pl.BlockSpec((pl.Element(tk), pl.Element(width)), index)

    return _call_after(
        dep, body, 4,
        [head(xc, x_by_block), head(yc, dy_by_block), rest(xc, x_by_block), rest(yc, dy_by_block)], (x, dy, x, dy),
        grid=(nb, nk),
        out_specs=pl.BlockSpec((None, xc, yc), lambda b, k: (b, 0, 0)),
        out_shape=jax.ShapeDtypeStruct((nb, xc, yc), BF16),
        scratch_shapes=[pltpu.VMEM((xc, yc), F32)],
        compiler_params=_cparams("parallel", "arbitrary"), name=name)


def _dgrad(dh, w, name, dep=None):
    t, d = dh.shape
    k = w.shape[0]
    tm = _row_tile(t, TOKEN_TILE_TARGET, 16)

    def body(dh_ref, w_ref, o_ref):
        o_ref[...] = _dot_nt(dh_ref[...].astype(BF16), w_ref[...]).astype(BF16)

    return _call_after(
        dep, body, 2,
        [pl.BlockSpec((tm, d), lambda i: (i, 0)), pl.BlockSpec((k, d), lambda i: (0, 0))], (dh, w), grid=(t // tm,),
        out_specs=pl.BlockSpec((tm, k), lambda i: (i, 0)),
        out_shape=jax.ShapeDtypeStruct((t, k), BF16),
        compiler_params=_cparams("parallel"), name=name)


def _dgrad_norm_bwd(dz, w, h, g, dh, nc, name):
    t, d = h.shape
    n = w.shape[1]
    tm = _row_tile(t, ROW_TILE_TARGET // 2, 16)

    def body(dz_ref, w_ref, h_ref, g_ref, dh_ref, dhi_ref, dg_ref):
        i = pl.program_id(0)
        du = jnp.zeros((tm, d), F32)
        for n0 in range(0, n, nc):
            n1 = min(n0 + nc, n)
            du = du + _dot_nt(dz_ref[:, n0:n1], w_ref[:, n0:n1])
        dhn, dgr = _rms_bwd(h_ref[...], g_ref[...], du)
        dhi_ref[...] = jnp.where(_valid_rows(i, tm), dh_ref[...] + dhn, 0.0)
        _accumulate(dg_ref, jnp.sum(dgr, axis=0, keepdims=True), i == 0)

    return pl.pallas_call(
        body, grid=(t // tm,),
        in_specs=[pl.BlockSpec((tm, n), lambda i: (i, 0)), pl.BlockSpec((d, n), lambda i: (0, 0)),
                  pl.BlockSpec((tm, d), lambda i: (i, 0)), pl.BlockSpec((1, d), lambda i: (0, 0)),
                  pl.BlockSpec((tm, d), lambda i: (i, 0))],
        out_specs=[pl.BlockSpec((tm, d), lambda i: (i, 0)), pl.BlockSpec((1, d), lambda i: (0, 0))],
        out_shape=[jax.ShapeDtypeStruct((t, d), F32), jax.ShapeDtypeStruct((1, d), F32)],
        compiler_params=_cparams("arbitrary"), name=name)(dz, w, h, g, dh)


def _dgrad_norm_bwd_input(dz, w, h, g, dh, nc, name):
    t, d = h.shape
    n = w.shape[1]
    tl = _row_tile(t - CHUNK, 512, CHUNK)

    def grads(dz_ref, w_ref, h_ref, g_ref, dh_ref, rows):
        du = jnp.zeros((rows, d), F32)
        for n0 in range(0, n, nc):
            n1 = min(n0 + nc, n)
            du = du + _dot_nt(dz_ref[:, n0:n1], w_ref[:, n0:n1])
        dhn, dgr = _rms_bwd(h_ref[...], g_ref[...], du)
        return dh_ref[...] + dhn, jnp.sum(dgr, axis=0, keepdims=True)

    def rest_body(dz_ref, w_ref, h_ref, g_ref, dh_ref, dg_head_ref, dx_ref, dg_ref):
        dx, dg = grads(dz_ref, w_ref, h_ref, g_ref, dh_ref, tl)
        dx_ref[...] = dx

        @pl.when(pl.program_id(0) == 0)
        def _():
            dg_ref[...] = dg_head_ref[...] + dg

        @pl.when(pl.program_id(0) > 0)
        def _():
            dg_ref[...] += dg

    def head_body(dz_ref, w_ref, h_ref, g_ref, dh_ref, dx_ref, dg_ref):
        dx, dg = grads(dz_ref, w_ref, h_ref, g_ref, dh_ref, CHUNK)
        dx_ref[...] = jnp.where(_valid_rows(0, CHUNK), dx, 0.0)
        dg_ref[...] = dg

    def shifted(width):
        return pl.BlockSpec((pl.Element(tl), pl.Element(width)), lambda i: (pl.multiple_of(CHUNK + i * tl, CHUNK), 0))

    whole = [pl.BlockSpec((d, n), lambda i: (0, 0)), pl.BlockSpec((1, d), lambda i: (0, 0))]
    head = lambda width: pl.BlockSpec((CHUNK, width), lambda i: (0, 0))
    dh_head, dg_head = pl.pallas_call(
        head_body, grid=(1,), in_specs=[head(n), whole[0], head(d), whole[1], head(d)],
        out_specs=[head(d), whole[1]],
        out_shape=[jax.ShapeDtypeStruct((CHUNK, d), F32), jax.ShapeDtypeStruct((1, d), F32)],
        compiler_params=_cparams("arbitrary"), name=name + "_head")(dz, w, h, g, dh)
    dx, dg = pl.pallas_call(
        rest_body, grid=((t - CHUNK) // tl,),
        in_specs=[shifted(n), whole[0], shifted(d), whole[1], shifted(d), whole[1]],
        out_specs=[pl.BlockSpec((tl, d), lambda i: (i, 0)), whole[1]],
        out_shape=[jax.ShapeDtypeStruct((t - CHUNK, d), F32), jax.ShapeDtypeStruct((1, d), F32)],
        compiler_params=_cparams("arbitrary"), name=name)(dz, w, h, g, dh, dg_head)
    return dx, dh_head, dg


def _loss_bwd(h, g, target):
    t, d = h.shape
    tl = _row_tile(t - CHUNK, 1024, CHUNK)

    def body(h_ref, g_ref, t_ref, dh_ref, dg_ref, loss_ref):
        i = pl.program_id(0)
        hv, gv = h_ref[...], g_ref[...]
        err = _rms(hv, gv) - t_ref[...]
        part = 0.5 * jnp.sum(jnp.mean(err * err, axis=-1, keepdims=True), axis=0, keepdims=True)
        dhn, dgr = _rms_bwd(hv, gv, err * (1.0 / d))
        dh_ref[...] = dhn
        _accumulate(dg_ref, jnp.sum(dgr, axis=0, keepdims=True), i == 0)
        _accumulate(loss_ref, jnp.broadcast_to(part, (8, 128)), i == 0)

    shifted = pl.BlockSpec((pl.Element(tl), pl.Element(d)), lambda i: (pl.multiple_of(CHUNK + i * tl, CHUNK), 0))
    dh, dg, loss = pl.pallas_call(
        body, grid=((t - CHUNK) // tl,),
        in_specs=[shifted, pl.BlockSpec((1, d), lambda i: (0, 0)), pl.BlockSpec((tl, d), lambda i: (i, 0))],
        out_specs=[shifted, pl.BlockSpec((1, d), lambda i: (0, 0)), pl.BlockSpec((8, 128), lambda i: (0, 0))],
        out_shape=[jax.ShapeDtypeStruct((t, d), F32), jax.ShapeDtypeStruct((1, d), F32),
                   jax.ShapeDtypeStruct((8, 128), F32)],
        compiler_params=_cparams("arbitrary"), name="loss_bwd")(h, g, target)

    def zero_head(dh_ref, o_ref):
        o_ref[...] = jnp.zeros_like(o_ref)

    dh = pl.pallas_call(
        zero_head, grid=(1,), in_specs=[ANY_SPEC], out_specs=pl.BlockSpec((CHUNK, d), lambda i: (0, 0)),
        out_shape=jax.ShapeDtypeStruct((t, d), F32), input_output_aliases={0: 0}, name="loss_bwd_head")(dh)
    return dh, dg, loss


CONV_BLOCK = 32


def _silu(x):
    return x * jax.nn.sigmoid(x)


def _row_shifts(win):
    n = win.shape[0]
    return [win] + [pltpu.roll(win, n - j, 0) for j in range(1, 8)]


def _cp_seq_fwd(z, conv_w, conv_b, ln_g, ln_b, pool_w, pool_scale):
    t = z.shape[0]
    tm = _row_tile(t, ROW_TILE_TARGET, CHUNK)

    def body(z_ref, cw_ref, cb_ref, lg_ref, lb_ref, pw_ref, ps_ref, c_ref, pm_ref, mix_ref, gbuf, pbuf):
        i = pl.program_id(0)

        @pl.when(i == 0)
        def _():
            gbuf[0:CONV_HALO, :] = jnp.zeros((CONV_HALO, CONV_DIM), F32)
            pbuf[0:POOL_HALO, :] = jnp.zeros((POOL_HALO, POOL_DIM), F32)

        @pl.when(i > 0)
        def _():
            gbuf[0:CONV_HALO, :] = gbuf[tm:tm + CONV_HALO, :]
            pbuf[0:POOL_HALO, :] = pbuf[tm:tm + POOL_HALO, :]

        av = z_ref[:, 0:CONV_DIM].astype(F32)
        ag = z_ref[:, CONV_DIM:2 * CONV_DIM].astype(F32)
        gbuf[CONV_HALO:CONV_HALO + tm, :] = av * jax.nn.sigmoid(ag)
        pbuf[POOL_HALO:POOL_HALO + tm, :] = z_ref[:, 2 * CONV_DIM:CP_IN].astype(F32)

        def conv_block(rb, carry):
            base = pl.multiple_of(rb * CONV_BLOCK, CONV_BLOCK)
            shifted = _row_shifts(gbuf[pl.ds(base, CONV_BLOCK + CONV_HALO), :])
            acc = jnp.zeros((CONV_BLOCK, CONV_DIM), F32)
            for k in range(CONV_WIDTH):
                whole, part = divmod(CONV_HALO - (CONV_WIDTH - 1) + k, 8)
                acc = acc + cw_ref[k:k + 1, :] * shifted[part][8 * whole:8 * whole + CONV_BLOCK, :]
            c_ref[pl.ds(base, CONV_BLOCK), :] = acc + cb_ref[...]
            return carry

        lax.fori_loop(0, tm // CONV_BLOCK, conv_block, 0)

        c = c_ref[...]
        mu = jnp.mean(c, axis=-1, keepdims=True)
        xc = c - mu
        ln = xc * lax.rsqrt(jnp.mean(xc * xc, axis=-1, keepdims=True) + EPS) * lg_ref[...] + lb_ref[...]
        row = i * tm + lax.broadcasted_iota(jnp.int32, (tm, 1), 0)
        mix_ref[:, 0:CONV_DIM] = jnp.where(row >= PAD_ROWS, _silu(ln), 0.0).astype(BF16)

        tpos = (row - PAD_ROWS + 1).astype(F32)
        for gi, wdw in enumerate(POOL_WINDOWS):
            lo = POOL_GROUP * gi
            run, step = pbuf[:, lo:lo + POOL_GROUP], 1
            cur = run[POOL_HALO:POOL_HALO + tm, :]
            while step < wdw:
                run = run + pltpu.roll(run, step, 0)
                step *= 2
            pm = (run[POOL_HALO:POOL_HALO + tm, :] / jnp.clip(tpos, 1.0, float(wdw)) - cur).astype(BF16)
            pm_ref[:, lo:lo + POOL_GROUP] = pm
            pg = _dot(pm, pw_ref[gi].astype(BF16))
            mix_ref[:, CONV_DIM + lo:CONV_DIM + lo + POOL_GROUP] = (pg * ps_ref[:, lo:lo + POOL_GROUP]).astype(BF16)

    vec = pl.BlockSpec((1, CONV_DIM), lambda i: (0, 0))
    return pl.pallas_call(
        body, grid=(t // tm,),
        in_specs=[pl.BlockSpec((tm, CP_IN), lambda i: (i, 0)),
                  pl.BlockSpec((CONV_WIDTH, CONV_DIM), lambda i: (0, 0)), vec, vec, vec,
                  pl.BlockSpec((len(POOL_WINDOWS), POOL_GROUP, POOL_GROUP), lambda i: (0, 0, 0)), vec],
        out_specs=[pl.BlockSpec((tm, CONV_DIM), lambda i: (i, 0)), pl.BlockSpec((tm, POOL_DIM), lambda i: (i, 0)),
                   pl.BlockSpec((tm, CONV_DIM + POOL_DIM), lambda i: (i, 0))],
        out_shape=[jax.ShapeDtypeStruct((t, CONV_DIM), F32), jax.ShapeDtypeStruct((t, POOL_DIM), BF16),
                   jax.ShapeDtypeStruct((t, CONV_DIM + POOL_DIM), BF16)],
        scratch_shapes=[pltpu.VMEM((tm + CONV_HALO, CONV_DIM), F32), pltpu.VMEM((tm + POOL_HALO, POOL_DIM), F32)],
        compiler_params=_cparams("arbitrary"), name="cp_seq_fwd")(z, conv_w, conv_b, ln_g, ln_b, pool_w, pool_scale)


def _cp_seq_bwd(dh, w_out, z, c, pm, conv_w, ln_g, ln_b, pool_w, pool_scale, dep=None):
    t, d = dh.shape
    tm = _row_tile(t, ROW_TILE_TARGET, CHUNK)
    nt = t // tm

    def body(dh_ref, wo_ref, z_ref, c_ref, pm_ref, cw_ref, lg_ref, lb_ref, pw_ref, ps_ref,
             dz_ref, dcw_ref, dvec_ref, dpw_ref, dcbuf, qbuf, glu_buf, dwacc, dmix_ref):
        i = pl.program_id(0)
        tile = nt - 1 - i
        dmix_ref[...] = _dot_nt(dh_ref[...].astype(BF16), wo_ref[...])

        @pl.when(i == 0)
        def _():
            dcbuf[tm:tm + CONV_HALO, :] = jnp.zeros((CONV_HALO, CONV_DIM), F32)
            qbuf[tm:tm + POOL_HALO, :] = jnp.zeros((POOL_HALO, POOL_DIM), F32)
            dcw_ref[...] = jnp.zeros_like(dcw_ref)
            dwacc[...] = jnp.zeros_like(dwacc)
            dvec_ref[...] = jnp.zeros_like(dvec_ref)
            dpw_ref[...] = jnp.zeros_like(dpw_ref)

        @pl.when(i > 0)
        def _():
            dcbuf[tm:tm + CONV_HALO, :] = dcbuf[0:CONV_HALO, :]
            qbuf[tm:tm + POOL_HALO, :] = qbuf[0:POOL_HALO, :]

        row = tile * tm + lax.broadcasted_iota(jnp.int32, (tm, 1), 0)
        cv = c_ref[...]
        mu = jnp.mean(cv, axis=-1, keepdims=True)
        xc = cv - mu
        rstd = lax.rsqrt(jnp.mean(xc * xc, axis=-1, keepdims=True) + EPS)
        xhat = xc * rstd
        ln = xhat * lg_ref[...] + lb_ref[...]
        sg = jax.nn.sigmoid(ln)
        da = jnp.where(row >= PAD_ROWS, dmix_ref[:, 0:CONV_DIM], 0.0)
        dln = da * (sg * (1.0 + ln * (1.0 - sg)))
        dxh = dln * lg_ref[...]
        dc = rstd * (dxh - jnp.mean(dxh, axis=-1, keepdims=True) - xhat * jnp.mean(dxh * xhat, axis=-1, keepdims=True))
        dcbuf[0:tm, :] = dc
        dvec_ref[0:1, :] += jnp.sum(dc, axis=0, keepdims=True)
        dvec_ref[1:2, :] += jnp.sum(dln * xhat, axis=0, keepdims=True)
        dvec_ref[2:3, :] += jnp.sum(dln, axis=0, keepdims=True)

        av = z_ref[:, 0:CONV_DIM].astype(F32)
        sig_g = jax.nn.sigmoid(z_ref[:, CONV_DIM:2 * CONV_DIM].astype(F32))
        glu_buf[...] = av * sig_g

        def conv_block(rb, carry):
            base = pl.multiple_of(rb * CONV_BLOCK, CONV_BLOCK)
            shifted = _row_shifts(dcbuf[pl.ds(base, CONV_BLOCK + CONV_HALO), :])
            glu = glu_buf[pl.ds(base, CONV_BLOCK), :]
            acc = jnp.zeros((CONV_BLOCK, CONV_DIM), F32)
            for k in range(CONV_WIDTH):
                whole, part = divmod(CONV_WIDTH - 1 - k, 8)
                slab = shifted[part][8 * whole:8 * whole + CONV_BLOCK, :]
                acc = acc + cw_ref[k:k + 1, :] * slab
                prod = slab * glu
                part = prod[0:8]
                for q in range(1, CONV_BLOCK // 8):
                    part = part + prod[8 * q:8 * q + 8]
                dwacc[k] += part
            glu_buf[pl.ds(base, CONV_BLOCK), :] = acc
            return carry

        lax.fori_loop(0, tm // CONV_BLOCK, conv_block, 0)

        @pl.when(i == nt - 1)
        def _():
            for k in range(CONV_WIDTH):
                dcw_ref[k:k + 1, :] = jnp.sum(dwacc[k], axis=0, keepdims=True)
        dglu = glu_buf[...]
        dz_ref[:, 0:CONV_DIM] = (dglu * sig_g).astype(BF16)
        dz_ref[:, CONV_DIM:2 * CONV_DIM] = (dglu * av * sig_g * (1.0 - sig_g)).astype(BF16)

        tpos = (row - PAD_ROWS + 1).astype(F32)
        for gi, wdw in enumerate(POOL_WINDOWS):
            lo = POOL_GROUP * gi
            dp = dmix_ref[:, CONV_DIM + lo:CONV_DIM + lo + POOL_GROUP]
            pmv = pm_ref[:, lo:lo + POOL_GROUP]
            pwb = pw_ref[gi].astype(BF16)
            dvec_ref[3:4, lo:lo + POOL_GROUP] += jnp.sum(dp * _dot(pmv, pwb), axis=0, keepdims=True)
            dq = (dp * ps_ref[:, lo:lo + POOL_GROUP]).astype(BF16)
            dpw_ref[gi] += _dot_tn(pmv, dq)
            dpm = _dot_nt(dq, pwb)
            qbuf[0:tm, lo:lo + POOL_GROUP] = dpm / jnp.clip(tpos, 1.0, float(wdw))
            run, step = qbuf[:, lo:lo + POOL_GROUP], 1
            while step < wdw:
                run = run + pltpu.roll(run, tm + POOL_HALO - step, 0)
                step *= 2
            dz_ref[:, 2 * CONV_DIM + lo:2 * CONV_DIM + lo + POOL_GROUP] = (run[0:tm, :] - dpm).astype(BF16)

    vec = pl.BlockSpec((1, CONV_DIM), lambda i: (0, 0))
    rev = lambda i: (nt - 1 - i, 0)
    return _call_after(
        dep, body, 10,
        [pl.BlockSpec((tm, d), rev), pl.BlockSpec((CONV_DIM + POOL_DIM, d), lambda i: (0, 0)), pl.BlockSpec((tm, CP_IN), rev),
         pl.BlockSpec((tm, CONV_DIM), rev), pl.BlockSpec((tm, POOL_DIM), rev),
         pl.BlockSpec((CONV_WIDTH, CONV_DIM), lambda i: (0, 0)), vec, vec,
         pl.BlockSpec((len(POOL_WINDOWS), POOL_GROUP, POOL_GROUP), lambda i: (0, 0, 0)), vec],
        (dh, w_out, z, c, pm, conv_w, ln_g, ln_b, pool_w, pool_scale), grid=(nt,),
        out_specs=[pl.BlockSpec((tm, CP_IN), rev), pl.BlockSpec((CONV_WIDTH + 1, CONV_DIM), lambda i: (0, 0)),
                   pl.BlockSpec((8, CONV_DIM), lambda i: (0, 0)),
                   pl.BlockSpec((len(POOL_WINDOWS), POOL_GROUP, POOL_GROUP), lambda i: (0, 0, 0))],
        out_shape=[jax.ShapeDtypeStruct((t, CP_IN), BF16), jax.ShapeDtypeStruct((CONV_WIDTH + 1, CONV_DIM), F32),
                   jax.ShapeDtypeStruct((8, CONV_DIM), F32),
                   jax.ShapeDtypeStruct((len(POOL_WINDOWS), POOL_GROUP, POOL_GROUP), F32)],
        scratch_shapes=[pltpu.VMEM((tm + CONV_HALO, CONV_DIM), F32), pltpu.VMEM((tm + POOL_HALO, POOL_DIM), F32),
                        pltpu.VMEM((tm, CONV_DIM), F32), pltpu.VMEM((CONV_WIDTH + 1, 8, CONV_DIM), F32),
                        pltpu.VMEM((tm, CONV_DIM + POOL_DIM), F32)],
        compiler_params=_cparams("arbitrary"), name="cp_seq_bwd")


Q0, K0, V0, G0, R0 =0, GLA_DK, 2 * GLA_DK, 2 * GLA_DK + GLA_DV, 2 * GLA_DK + 2 * GLA_DV


def _split3(x):
    hi = x.astype(BF16)
    r1 = x - hi.astype(F32)
    mid = r1.astype(BF16)
    lo = (r1 - mid.astype(F32)).astype(BF16)
    return hi, mid, lo


def _tri(strict):
    r = lax.broadcasted_iota(jnp.int32, (CHUNK, CHUNK), 0)
    c = lax.broadcasted_iota(jnp.int32, (CHUNK, CHUNK), 1)
    return ((r > c) if strict else (r >= c)).astype(BF16)


def _chunk_sums(x, cpt, strict, pieces):
    tri3 = jnp.broadcast_to(_tri(strict)[None], (cpt, CHUNK, CHUNK))
    acc = None
    for piece in _split3(x.reshape(cpt, CHUNK, x.shape[-1]))[:pieces]:
        part = jnp.einsum("bij,bjk->bik", tri3, piece, preferred_element_type=F32)
        acc = part if acc is None else acc + part
    return acc


def _chunk_decay(r, gw_ref, gb_ref, cpt):
    pre = _dot(r, gw_ref[...]) + gb_ref[...]
    lac = (jnp.minimum(pre, 0.0) - jnp.log(1.0 + jnp.exp(-jnp.abs(pre)))) * (1.0 / GATE_NORM)
    cum3 = _chunk_sums(lac, cpt, False, 3)
    return cum3, cum3[:, CHUNK - 1:CHUNK, :]


def _gla_seq_fwd(z, gate_w, gate_b, head_g, dep=None):
    t = z.shape[0]
    tm = _row_tile(t, ROW_TILE_TARGET, CHUNK)
    cpt = tm // CHUNK
    scale = GLA_HK ** -0.5

    def body(z_ref, gw_ref, gb_ref, hg_ref, o_ref, mix_ref, st_ref, state, kdec_s, e_s):
        @pl.when(pl.program_id(0) == 0)
        def _():
            state[...] = jnp.zeros_like(state)

        cum3, tot3 = _chunk_decay(z_ref[:, R0:R0 + GATE_PAD], gw_ref, gb_ref, cpt)
        dec = jnp.exp(jnp.broadcast_to(tot3, cum3.shape) - cum3).reshape(tm, GLA_DK)
        kdec_s[...] = (z_ref[:, K0:K0 + GLA_DK].astype(F32) * dec).astype(BF16)
        e_s[...] = jnp.exp(jnp.broadcast_to(tot3, (cpt, 8, GLA_DK))).reshape(cpt * 8, GLA_DK)

        def chunk(ci, carry):
            rows = pl.ds(pl.multiple_of(ci * CHUNK, CHUNK), CHUNK)
            e_all = e_s[pl.ds(pl.multiple_of(ci * 8, 8), 8), :][0:1, :]
            st_ref[ci] = state[...].astype(BF16)
            for hd in range(GLA_HEADS):
                ks = slice(hd * GLA_HK, (hd + 1) * GLA_HK)
                vs = slice(hd * GLA_HV, (hd + 1) * GLA_HV)
                v = z_ref[rows, V0 + hd * GLA_HV:V0 + (hd + 1) * GLA_HV]
                st = state[vs, :] * e_all[:, ks] + _dot_tn(v, kdec_s[rows, ks])
                state[vs, :] = st
                q = z_ref[rows, Q0 + hd * GLA_HK:Q0 + (hd + 1) * GLA_HK]
                o_ref[rows, vs] = (_dot_nt(q, st.astype(BF16)) * scale).astype(BF16)
            return carry

        lax.fori_loop(0, cpt, chunk, 0, unroll=cpt)

        for hd in range(GLA_HEADS):
            vs = slice(hd * GLA_HV, (hd + 1) * GLA_HV)
            on = _rms(o_ref[:, vs].astype(F32), hg_ref[...])
            gv = z_ref[:, G0 + hd * GLA_HV:G0 + (hd + 1) * GLA_HV].astype(F32)
            mix_ref[:, vs] = (on * _silu(gv)).astype(BF16)

    return _call_after(
        dep, body, 4,
        [pl.BlockSpec((tm, GLA_IN_PAD), lambda i: (i, 0)),
         pl.BlockSpec((GATE_PAD, GLA_DK), lambda i: (0, 0)), pl.BlockSpec((1, GLA_DK), lambda i: (0, 0)),
         pl.BlockSpec((1, GLA_HV), lambda i: (0, 0))], (z, gate_w, gate_b, head_g), grid=(t // tm,),
        out_specs=[pl.BlockSpec((tm, GLA_DV), lambda i: (i, 0)), pl.BlockSpec((tm, GLA_DV), lambda i: (i, 0)),
                   pl.BlockSpec((cpt, GLA_DV, GLA_HK), lambda i: (i, 0, 0))],
        out_shape=[jax.ShapeDtypeStruct((t, GLA_DV), BF16), jax.ShapeDtypeStruct((t, GLA_DV), BF16),
                   jax.ShapeDtypeStruct((t // CHUNK, GLA_DV, GLA_HK), BF16)],
        scratch_shapes=[pltpu.VMEM((GLA_DV, GLA_HK), F32), pltpu.VMEM((tm, GLA_DK), BF16),
                        pltpu.VMEM((cpt * 8, GLA_DK), F32)],
        compiler_params=_cparams("arbitrary"), name="gla_seq_fwd")


def _gla_seq_bwd(dmix, o, z, states, gate_w, gate_b, head_g, dep=None):
    t = z.shape[0]
    tm = _row_tile(t, ROW_TILE_TARGET, CHUNK)
    cpt = tm // CHUNK
    nt = t // tm
    scale = GLA_HK ** -0.5

    def body(dmix_ref, o_ref, z_ref, st_ref, gw_ref, gb_ref, hg_ref, dz_ref, dgw_ref, dgb_ref, dhg_ref,
             dstate, dec_s, kdec_s, dkdec_s, do_s, e_s, dtot_s):
        @pl.when(pl.program_id(0) == 0)
        def _():
            dstate[...] = jnp.zeros_like(dstate)
            dgw_ref[...] = jnp.zeros_like(dgw_ref)
            dgb_ref[...] = jnp.zeros_like(dgb_ref)
            dhg_ref[...] = jnp.zeros_like(dhg_ref)

        cum3, tot3 = _chunk_decay(z_ref[:, R0:R0 + GATE_PAD], gw_ref, gb_ref, cpt)
        dec = jnp.exp(jnp.broadcast_to(tot3, cum3.shape) - cum3).reshape(tm, GLA_DK)
        dec_s[...] = dec
        kdec = z_ref[:, K0:K0 + GLA_DK].astype(F32) * dec
        kdec_s[...] = kdec
        e3 = jnp.exp(tot3)
        e_s[...] = jnp.broadcast_to(e3, (cpt, 8, GLA_DK)).reshape(cpt * 8, GLA_DK)
        dhg = jnp.zeros((1, GLA_HV), F32)
        for hd in range(GLA_HEADS):
            ks = slice(hd * GLA_HK, (hd + 1) * GLA_HK)
            vs = slice(hd * GLA_HV, (hd + 1) * GLA_HV)
            gcols = slice(G0 + hd * GLA_HV, G0 + (hd + 1) * GLA_HV)
            ov = o_ref[:, vs].astype(F32)
            gv = z_ref[:, gcols].astype(F32)
            dm = dmix_ref[:, vs].astype(F32)
            sg = jax.nn.sigmoid(gv)
            rr = lax.rsqrt(jnp.mean(ov * ov, axis=-1, keepdims=True) + EPS)
            xhat = ov * rr
            don = dm * (gv * sg)
            dz_ref[:, gcols] = (dm * (xhat * hg_ref[...]) * (sg * (1.0 + gv * (1.0 - sg)))).astype(BF16)
            dhg = dhg + jnp.sum(don * xhat, axis=0, keepdims=True)
            dxh = don * hg_ref[...]
            do = (rr * (dxh - xhat * jnp.mean(dxh * xhat, axis=-1, keepdims=True)) * scale).astype(BF16)
            do_s[:, vs] = do
            v3 = z_ref[:, V0 + hd * GLA_HV:V0 + (hd + 1) * GLA_HV].reshape(cpt, CHUNK, GLA_HV)
            kdb3 = kdec[:, ks].astype(BF16).reshape(cpt, CHUNK, GLA_HK)
            st3 = st_ref[:, vs, :].astype(F32) * e3[:, :, ks] + jnp.einsum("bcv,bck->bvk", v3, kdb3,
                                                                            preferred_element_type=F32)
            dq3 = jnp.einsum("bcv,bvk->bck", do.reshape(cpt, CHUNK, GLA_HV), st3.astype(BF16), preferred_element_type=F32)
            dz_ref[:, Q0 + hd * GLA_HK:Q0 + (hd + 1) * GLA_HK] = dq3.reshape(tm, GLA_HK).astype(BF16)
        dhg_ref[...] += dhg

        def chunk(cj, carry):
            ci = cpt - 1 - cj
            rows = pl.ds(pl.multiple_of(ci * CHUNK, CHUNK), CHUNK)
            erows = pl.ds(pl.multiple_of(ci * 8, 8), 8)
            e_all = e_s[erows, :][0:1, :]
            for hd in range(GLA_HEADS):
                ks = slice(hd * GLA_HK, (hd + 1) * GLA_HK)
                vs = slice(hd * GLA_HV, (hd + 1) * GLA_HV)
                e = e_all[:, ks]
                kdb = kdec_s[rows, ks].astype(BF16)
                v = z_ref[rows, V0 + hd * GLA_HV:V0 + (hd + 1) * GLA_HV]
                q = z_ref[rows, Q0 + hd * GLA_HK:Q0 + (hd + 1) * GLA_HK]
                do = do_s[rows, vs]
                st_prev = st_ref[ci, vs, :].astype(F32)
                dst = dstate[vs, :] + _dot_tn(do, q)
                dstb = dst.astype(BF16)
                dkdec_s[rows, ks] = _dot(v, dstb)
                dz_ref[rows, V0 + hd * GLA_HV:V0 + (hd + 1) * GLA_HV] = _dot_nt(kdb, dstb).astype(BF16)
                dtot = jnp.sum(dst * st_prev, axis=0, keepdims=True) * e
                dtot_s[erows, ks] = jnp.broadcast_to(dtot, (8, GLA_HK))
                dstate[vs, :] = dst * e
            return carry

        lax.fori_loop(0, cpt, chunk, 0, unroll=cpt)

        dkdec = dkdec_s[...]
        dz_ref[:, K0:K0 + GLA_DK] = (dkdec * dec_s[...]).astype(BF16)
        before = _chunk_sums(dkdec * kdec_s[...], cpt, True, 2)
        dtot3 = dtot_s[...].reshape(cpt, 8, GLA_DK)[:, 0:1, :]
        dlac = (jnp.broadcast_to(dtot3, before.shape) + before).reshape(tm, GLA_DK)
        pre = _dot(z_ref[:, R0:R0 + GATE_PAD], gw_ref[...]) + gb_ref[...]
        dpre = dlac * (1.0 / GATE_NORM) * (1.0 - jax.nn.sigmoid(pre))
        dpb = dpre.astype(BF16)
        dz_ref[:, R0:R0 + GATE_PAD] = _dot_nt(dpb, gw_ref[...]).astype(BF16)
        dgw_ref[...] += _dot_tn(z_ref[:, R0:R0 + GATE_PAD], dpb)
        dgb_ref[...] += jnp.sum(dpre, axis=0, keepdims=True)

    rev = lambda i: (nt - 1 - i, 0)
    return _call_after(
        dep, body, 7,
        [pl.BlockSpec((tm, GLA_DV), rev), pl.BlockSpec((tm, GLA_DV), rev), pl.BlockSpec((tm, GLA_IN_PAD), rev),
         pl.BlockSpec((cpt, GLA_DV, GLA_HK), lambda i: (nt - 1 - i, 0, 0)),
         pl.BlockSpec((GATE_PAD, GLA_DK), lambda i: (0, 0)), pl.BlockSpec((1, GLA_DK), lambda i: (0, 0)),
         pl.BlockSpec((1, GLA_HV), lambda i: (0, 0))],
        (dmix, o, z, states, gate_w, gate_b, head_g), grid=(nt,),
        out_specs=[pl.BlockSpec((tm, GLA_IN_PAD), rev), pl.BlockSpec((GATE_PAD, GLA_DK), lambda i: (0, 0)),
                   pl.BlockSpec((1, GLA_DK), lambda i: (0, 0)), pl.BlockSpec((1, GLA_HV), lambda i: (0, 0))],
        out_shape=[jax.ShapeDtypeStruct((t, GLA_IN_PAD), BF16), jax.ShapeDtypeStruct((GATE_PAD, GLA_DK), F32),
                   jax.ShapeDtypeStruct((1, GLA_DK), F32), jax.ShapeDtypeStruct((1, GLA_HV), F32)],
        scratch_shapes=[pltpu.VMEM((GLA_DV, GLA_HK), F32), pltpu.VMEM((tm, GLA_DK), F32), pltpu.VMEM((tm, GLA_DK), F32),
                        pltpu.VMEM((tm, GLA_DK), F32), pltpu.VMEM((tm, GLA_DV), BF16),
                        pltpu.VMEM((cpt * 8, GLA_DK), F32), pltpu.VMEM((cpt * 8, GLA_DK), F32)],
        compiler_params=_cparams("arbitrary"), name="gla_seq_bwd")


def _sum_slots(x, name):
    n, r, cdim = x.shape
    tr = _row_tile(r, 256, 8)

    def body(x_ref, o_ref):
        acc = x_ref[0].astype(F32)
        for j in range(1, n):
            acc = acc + x_ref[j].astype(F32)
        o_ref[...] = acc

    return pl.pallas_call(
        body, grid=(r // tr,),
        in_specs=[pl.BlockSpec((n, tr, cdim), lambda i: (0, i, 0))],
        out_specs=pl.BlockSpec((tr, cdim), lambda i: (i, 0)),
        out_shape=jax.ShapeDtypeStruct((r, cdim), F32),
        compiler_params=_cparams("parallel"), name=name)(x)


def _sum_own_and_slots(own, slots, dev_idx, name):
    _, _, r, cdim = own.shape
    n = slots.shape[0]
    tr = _row_tile(r, 256, 8)

    def body(s_ref, own_ref, *rest):
        acc = own_ref[...].astype(F32)
        for other in rest[:n - 1]:
            acc = acc + other[...].astype(F32)
        rest[n - 1][...] = acc

    def slot(dd):
        return pl.BlockSpec((None, tr, cdim), lambda i, s: ((s[0] + dd) % n, i, 0))

    mine = pl.BlockSpec((None, None, tr, cdim), lambda i, s: (s[0] // 2, s[0] % 2, i, 0))
    return pl.pallas_call(
        body,
        grid_spec=pltpu.PrefetchScalarGridSpec(
            num_scalar_prefetch=1, grid=(r // tr,), in_specs=[mine] + [slot(dd) for dd in range(1, n)],
            out_specs=pl.BlockSpec((tr, cdim), lambda i, s: (i, 0))),
        out_shape=jax.ShapeDtypeStruct((r, cdim), F32),
        compiler_params=_cparams("parallel"), name=name)(dev_idx, own, *([slots] * (n - 1)))


def _add2(a, b, name):
    r, cdim = a.shape
    tr = _row_tile(r, 256, 8)

    def body(a_ref, b_ref, o_ref):
        o_ref[...] = a_ref[...] + b_ref[...]

    spec = pl.BlockSpec((tr, cdim), lambda i: (i, 0))
    return pl.pallas_call(body, grid=(r // tr,), in_specs=[spec, spec], out_specs=spec,
                          out_shape=jax.ShapeDtypeStruct((r, cdim), F32),
                          compiler_params=_cparams("parallel"), name=name)(a, b)


def _adamw_half(w, gs, m, v, half_idx, prev, name, dep=None):
    nl, _, h, cdim = w.shape
    tr = _row_tile(h, 256, 8)
    nprev = 0 if prev is None else 4
    extra = [] if dep is None else [dep]

    def body(s_ref, w_ref, m_ref, v_ref, *rest):
        g_refs = rest[:nl]
        go_ref, d_ref, mo_ref, vo_ref = rest[nl + nprev + len(extra):]
        layer = pl.program_id(0)
        gv = g_refs[0][...]
        for j in range(1, nl):
            gv = jnp.where(layer == j, g_refs[j][...], gv)
        go_ref[...] = gv
        mn = ADAM_B1 * m_ref[...] + (1.0 - ADAM_B1) * gv
        vn = ADAM_B2 * v_ref[...] + (1.0 - ADAM_B2) * (gv * gv)
        m_hat = mn / (1.0 - ADAM_B1 ** ADAM_STEP)
        v_hat = vn / (1.0 - ADAM_B2 ** ADAM_STEP)
        d_ref[...] = -ADAM_LR * (m_hat / (jnp.sqrt(v_hat) + ADAM_EPS) + ADAM_WD * w_ref[...])
        mo_ref[...] = mn
        vo_ref[...] = vn

    half = pl.BlockSpec((None, None, tr, cdim), lambda l, i, s: (l, s[0], i, 0))

    def of_layer(j):
        return pl.BlockSpec((tr, cdim), lambda l, i, s: (jnp.where(l == j, i, 0), 0))

    shp = jax.ShapeDtypeStruct(w.shape, F32)
    return pl.pallas_call(
        body,
        grid_spec=pltpu.PrefetchScalarGridSpec(
            num_scalar_prefetch=1, grid=(nl, h // tr),
            in_specs=[half] * 3 + [of_layer(j) for j in range(nl)] + [ANY_SPEC] * (nprev + len(extra)),
            out_specs=[half] * 4),
        out_shape=[shp] * 4, input_output_aliases={4 + nl + k: k for k in range(nprev)},
        compiler_params=_cparams("arbitrary", "arbitrary"), name=name,
    )(half_idx, w, m, v, *gs, *([] if prev is None else prev), *extra)


def _adamw_many(ws, gs, ms, vs):
    n = len(ws)

    def body(*refs):
        for i in range(n):
            w_ref, g_ref, m_ref, v_ref = refs[i], refs[n + i], refs[2 * n + i], refs[3 * n + i]
            d_ref, mo_ref, vo_ref = refs[4 * n + i], refs[5 * n + i], refs[6 * n + i]
            gv = g_ref[...]
            mn = ADAM_B1 * m_ref[...] + (1.0 - ADAM_B1) * gv
            vn = ADAM_B2 * v_ref[...] + (1.0 - ADAM_B2) * (gv * gv)
            m_hat = mn / (1.0 - ADAM_B1 ** ADAM_STEP)
            v_hat = vn / (1.0 - ADAM_B2 ** ADAM_STEP)
            d_ref[...] = -ADAM_LR * (m_hat / (jnp.sqrt(v_hat) + ADAM_EPS) + ADAM_WD * w_ref[...])
            mo_ref[...] = mn
            vo_ref[...] = vn

    shapes = [jax.ShapeDtypeStruct(w.shape, F32) for w in ws]
    outs = pl.pallas_call(body, out_shape=shapes * 3, name="adamw_small")(*ws, *gs, *ms, *vs)
    return outs[:n], outs[n:2 * n], outs[2 * n:]


def _split_rows(a):
    return a.reshape(a.shape[0], 2, a.shape[1] // 2, a.shape[2])


def _place():
    x, y, c = lax.axis_index("x"), lax.axis_index("y"), lax.axis_index("c")
    chips = [(1 - x, y), (x, 1 - y), (1 - x, 1 - y)]
    return x, y, c, chips


def _remote(src, dst, send_sem, recv_sem, to):
    return pltpu.make_async_remote_copy(src_ref=src, dst_ref=dst, send_sem=send_sem, recv_sem=recv_sem,
                                        device_id=to, device_id_type=MESH)


def _plan_gather(n_halved):
    def plan(src_refs, land_refs):
        x, y, c, chips = _place()
        me = 2 * x + y
        copies = []
        for k, (src, land) in enumerate(zip(src_refs, land_refs)):
            for (px, py) in chips:
                frm = 2 * px + py
                if k < n_halved:
                    copies.append((src.at[c], land.at[me, c], (px, py, c), land.at[frm, c]))
                else:
                    copies.append((src, land.at[me], (px, py, c), land.at[frm]))
        return copies
    return plan


def _plan_share(src_refs, land_refs):
    x, y, c, chips = _place()
    me = 2 * x + y
    sib = (x, y, 1 - c)
    copies = []
    for src, land in zip(src_refs, land_refs):
        copies.append((src, land.at[me], sib, land.at[me]))
        for (px, py) in chips:
            frm = 2 * px + py
            copies.append((land.at[frm, c], land.at[frm, c], sib, land.at[frm, 1 - c]))
    return copies


def _plan_scatter(n_parts):
    def plan(src_refs, land_refs):
        x, y, c, chips = _place()
        me = 2 * x + y
        copies = []
        for k, (src, land) in enumerate(zip(src_refs, land_refs)):
            for (px, py) in chips:
                to = 2 * px + py
                copies.append((src.at[to] if k < n_parts else src, land.at[me], (px, py, c), land.at[to]))
        return copies
    return plan


N_DEVICES = 8
OTHER_DEVICES = [(dx, dy, dc) for dx in (0, 1) for dy in (0, 1) for dc in (0, 1) if dx or dy or dc]


def _plan_scatter_all(src_refs, land_refs):
    x, y, c, _ = _place()
    me = 4 * x + 2 * y + c
    copies = []
    for src, land in zip(src_refs, land_refs):
        for dx, dy, dc in OTHER_DEVICES:
            px, py, pc = (1 - x if dx else x), (1 - y if dy else y), (1 - c if dc else c)
            copies.append((src.at[2 * px + py, pc], land.at[me], (px, py, pc), land.at[4 * px + 2 * py + pc]))
    return copies


def _plan_exchange(n_split):
    def plan(src_refs, land_refs):
        x, y, c, _ = _place()
        sib = (x, y, 1 - c)
        return [(src.at[:, 1 - c] if k < n_split else src, land, sib, land)
                for k, (src, land) in enumerate(zip(src_refs, land_refs))]
    return plan


def _hbm(a):
    return pltpu.HBM(a.shape, a.dtype)


def _start_copies(name, srcs, lands, plan, ncopy, dep=None):
    ns, nl = len(srcs), len(lands)
    nin = ns + nl + (0 if dep is None else 1)

    def body(*refs):
        send_sems, recv_sems, token = refs[nin], refs[nin + 1], refs[-1]
        for k, (src, dst, dev, _) in enumerate(plan(refs[:ns], refs[ns:ns + nl])):
            _remote(src, dst, send_sems.at[k], recv_sems.at[k], dev).start()
        token[...] = jnp.zeros_like(token)

    args = [pltpu.with_memory_space_constraint(a, pltpu.HBM) for a in list(srcs) + list(lands)]
    outs = pl.pallas_call(
        body, name=name,
        out_shape=(pltpu.SemaphoreType.DMA((ncopy,)), pltpu.SemaphoreType.DMA((ncopy,)),
                   *[_hbm(a) for a in list(srcs) + list(lands)], jax.ShapeDtypeStruct((8, 128), F32)),
        in_specs=[HBM_SPEC] * (ns + nl) + ([] if dep is None else [ANY_SPEC]),
        out_specs=(SEM_SPEC, SEM_SPEC, *([HBM_SPEC] * (ns + nl)), pl.BlockSpec(memory_space=pltpu.VMEM)),
        input_output_aliases={i: 2 + i for i in range(ns + nl)},
        compiler_params=pltpu.CompilerParams(has_side_effects=SIDE_EFFECT),
    )(*args, *([] if dep is None else [dep]))
    return outs[0], outs[1], list(outs[2:2 + ns]), list(outs[2 + ns:2 + ns + nl]), outs[-1]


def _wait_copies(name, started, plan, after, sem_offset=0):
    send_sems, recv_sems, srcs, lands, _ = started
    ns, nl = len(srcs), len(lands)
    after = list(after) if isinstance(after, (list, tuple)) else [after]

    def body(*refs):
        send_ref, recv_ref = refs[ns + nl], refs[ns + nl + 1]
        for k, (src, _, dev, mine) in enumerate(plan(refs[:ns], refs[ns:ns + nl])):
            copy = _remote(src, mine, send_ref.at[sem_offset + k], recv_ref.at[sem_offset + k], dev)
            copy.wait_send()
            copy.wait_recv()

    outs = pl.pallas_call(
        body, name=name, out_shape=tuple(_hbm(a) for a in srcs + lands),
        in_specs=[HBM_SPEC] * (ns + nl) + [SEM_SPEC, SEM_SPEC] + [ANY_SPEC] * len(after),
        out_specs=tuple([HBM_SPEC] * (ns + nl)),
        input_output_aliases={i: i for i in range(ns + nl)},
        compiler_params=pltpu.CompilerParams(has_side_effects=SIDE_EFFECT),
    )(*srcs, *lands, send_sems, recv_sems, *after)
    return list(outs[:ns]), list(outs[ns:])


def _share_with_sibling(name, srcs, lands):
    n = len(srcs)

    def body(*refs):
        src_refs, land_refs, out_refs = refs[:n], refs[n:2 * n], refs[2 * n:3 * n]
        send_sem, recv_sem = refs[3 * n:]
        x, y, c, chips = _place()
        me = 2 * x + y
        sib = (x, y, 1 - c)
        sends, recvs = [], []
        for k in range(n):
            sems = (send_sem.at[4 * k], recv_sem.at[4 * k])
            sends.append(_remote(src_refs[k], out_refs[k].at[me], *sems, sib))
            recvs.append(_remote(src_refs[k], out_refs[k].at[me], *sems, sib))
            for j, (px, py) in enumerate(chips):
                frm = 2 * px + py
                sems = (send_sem.at[4 * k + 1 + j], recv_sem.at[4 * k + 1 + j])
                sends.append(_remote(land_refs[k].at[frm, c], out_refs[k].at[frm, c], *sems, sib))
                recvs.append(_remote(land_refs[k].at[frm, c], out_refs[k].at[frm, 1 - c], *sems, sib))
        for cp in sends:
            cp.start()
        for cp in recvs:
            cp.wait_recv()
        for cp in sends:
            cp.wait_send()

    return pl.pallas_call(
        body, name=name, in_specs=[HBM_SPEC] * (2 * n), out_specs=[HBM_SPEC] * n,
        out_shape=[jax.ShapeDtypeStruct(a.shape, a.dtype) for a in lands],
        input_output_aliases={n + k: k for k in range(n)},
        scratch_shapes=[pltpu.SemaphoreType.DMA((4 * n,)), pltpu.SemaphoreType.DMA((4 * n,))],
    )(*srcs, *lands)


def _pack(arrs):
    flat = jnp.concatenate([a.reshape(-1).astype(F32) for a in arrs])
    n = flat.shape[0]
    rows = -(-n // PACK_WIDTH)
    rows = -(-rows // 8) * 8
    return jnp.pad(flat, (0, rows * PACK_WIDTH - n)).reshape(rows, PACK_WIDTH)


def _unpack(buf, shapes):
    flat = buf.reshape(-1)
    out, off = [], 0
    for shp in shapes:
        n = 1
        for s in shp:
            n *= s
        out.append(flat[off:off + n].reshape(shp))
        off += n
    return out


def _unshard_cols(stacked):
    moved = jnp.moveaxis(stacked, 0, -2)
    return moved.reshape(moved.shape[:-2] + (moved.shape[-2] * moved.shape[-1],))


def _take_cols(blocks, start, width):
    bw = blocks.shape[2]
    pieces, lo = [], start
    while lo < start + width:
        b = lo // bw
        hi = min(start + width, (b + 1) * bw)
        pieces.append(blocks[b][:, lo - b * bw:hi - b * bw])
        lo = hi
    return jnp.concatenate(pieces, axis=1)


def _col_shard(full, s, width):
    return lax.dynamic_slice_in_dim(full, s * width, width, axis=full.ndim - 1)


def kernel(x, meta_tokens, mix_norm_g, ffn_norm_g, ffn_w1, ffn_w2, cp_w_in, cp_conv_w, cp_conv_b, cp_ln_g, cp_ln_b, cp_pool_w, cp_pool_scale, cp_w_out, gla_w_in, gla_gate_w2, gla_gate_b, gla_head_g, gla_w_out, final_norm_g, loss_target, m_meta_tokens, m_mix_norm_g, m_ffn_norm_g, m_ffn_w1, m_ffn_w2, m_cp_w_in, m_cp_conv_w, m_cp_conv_b, m_cp_ln_g, m_cp_ln_b, m_cp_pool_w, m_cp_pool_scale, m_cp_w_out, m_gla_w_in, m_gla_gate_w2, m_gla_gate_b, m_gla_head_g, m_gla_w_out, m_final_norm_g, v_meta_tokens, v_mix_norm_g, v_ffn_norm_g, v_ffn_w1, v_ffn_w2, v_cp_w_in, v_cp_conv_w, v_cp_conv_b, v_cp_ln_g, v_cp_ln_b, v_cp_pool_w, v_cp_pool_scale, v_cp_w_out, v_gla_w_in, v_gla_gate_w2, v_gla_gate_b, v_gla_head_g, v_gla_w_out, v_final_norm_g):
    d = D_MODEL
    chip = 2 * lax.axis_index("x") + lax.axis_index("y")
    core = lax.axis_index("c")
    seq = x.shape[1]
    t = seq + CHUNK

    sharded_small = [meta_tokens, cp_conv_w, gla_gate_w2, gla_gate_b, gla_head_g]

    def halves(w, token=None):
        if token is not None:
            w = w + token[0, 0]
        return w.astype(BF16).reshape(2, w.shape[0] // 2, w.shape[1])

    def unhalve(g):
        return g.reshape(N_CHIPS, 2 * g.shape[2], g.shape[3])

    def gather_group(srcs, whole=()):
        lands = [lax.empty((N_CHIPS,) + s.shape, s.dtype) for s in srcs]
        for a in whole:
            lands.append(lax.dynamic_update_slice(jnp.zeros((N_CHIPS,) + a.shape, a.dtype), a[None], (chip,) + (0,) * a.ndim))
        return list(srcs) + list(whole), lands, _plan_gather(len(srcs)), len(srcs)

    def start_groups(name, groups, dep):
        bounds, all_srcs, all_lands = [], [], []
        for srcs, lands, _, _ in groups:
            bounds.append((len(all_srcs), len(all_srcs) + len(srcs)))
            all_srcs += srcs
            all_lands += lands

        def plan_all(src_refs, land_refs):
            return [cp for (lo, hi), group in zip(bounds, groups) for cp in group[2](src_refs[lo:hi], land_refs[lo:hi])]

        started = _start_copies(name, all_srcs, all_lands, plan_all, 3 * len(all_srcs), dep)
        return [(started, bound, group[2], group[3]) for bound, group in zip(bounds, groups)]

    def arrived(name, gather, after):
        started, (lo, hi), plan, n = gather
        mine = (started[0], started[1], started[2][lo:hi], started[3][lo:hi], started[4])
        srcs, lands = _wait_copies(name + "_wait", mine, plan, after, sem_offset=3 * lo)
        return srcs[:n], lands[:n], lands[n:]

    (cp_gather,) = start_groups("gather_cp_start", [gather_group([halves(cp_w_in[0]), halves(cp_w_out[0])],
                                                                 [_pack(sharded_small)])], None)
    tok = cp_gather[0][-1]
    ffn0_gather, gla_gather, ffn1_gather = start_groups(
        "gather_start", [gather_group([halves(ffn_w1[0], tok), halves(ffn_w2[0], tok)]),
                         gather_group([halves(gla_w_in[0], tok), halves(gla_w_out[0], tok)]),
                         gather_group([halves(ffn_w1[1], tok), halves(ffn_w2[1], tok)])], tok)
    h0_rows = jnp.concatenate([jnp.zeros((CHUNK, d), F32) + ffn0_gather[0][-1][0, 0], x[0]], axis=0)
    cp_srcs, cp_lands, (small_g,) = arrived("gather_cp", cp_gather, h0_rows)
    cpin_g, cpout_g = [unhalve(g) for g in _share_with_sibling("gather_cp_share", cp_srcs, cp_lands)]
    per_chip = [_unpack(small_g[j], [a.shape for a in sharded_small]) for j in range(N_CHIPS)]
    meta_f, conv_w_f, gate_w_f, gate_b_f, head_g_f = [
        jnp.concatenate([per_chip[j][i] for j in range(N_CHIPS)], axis=-1) for i in range(len(sharded_small))]
    conv_w_f, gate_w_f = conv_w_f[0], gate_w_f[0]
    w_cp_in = _unshard_cols(cpin_g)
    w_cp_out = cpout_g.reshape(CONV_DIM + POOL_DIM, d)
    gate_w_pad = jnp.pad(gate_w_f, ((0, GATE_PAD - GATE_RANK), (0, 0))).astype(BF16)
    row = lambda a: a.reshape(1, -1)
    c_idx = core.reshape(1).astype(jnp.int32)

    h0 = lax.dynamic_update_slice(h0_rows, meta_f, (PAD_ROWS, 0))
    z0, u0 = _norm_matmul(h0, row(mix_norm_g[0]), w_cp_in, 512, "cp_in_proj")
    c0, pm0, mix0 = _cp_seq_fwd(z0, conv_w_f, cp_conv_b, cp_ln_g, cp_ln_b, cp_pool_w[0], cp_pool_scale)
    ffn0_srcs, ffn0_lands, _ = arrived("gather_ffn0", ffn0_gather, mix0)
    ffn0_share = _start_copies("gather_ffn0_share_start", ffn0_srcs, ffn0_lands, _plan_share, 4 * len(ffn0_srcs))
    h1 = _matmul_residual(mix0, w_cp_out, h0, "cp_out_proj", dep=ffn0_share[-1])
    w1g0, w2g0 = [unhalve(g) for g in _wait_copies("gather_ffn0_share_wait", ffn0_share, _plan_share, h1)[1]]
    h2, hp0, uf0 = _ffn_fwd(h1, row(ffn_norm_g[0]), w1g0, w2g0, "ffn0_fwd")
    gla_srcs, gla_lands, _ = arrived("gather_gla", gla_gather, h2)
    glain_g, glaout_g = [unhalve(g) for g in _share_with_sibling("gather_gla_share", gla_srcs, gla_lands)]
    w_gla_in = jnp.concatenate([glain_g[j] for j in range(N_CHIPS)] + [jnp.zeros((d, GLA_IN_PAD - GLA_IN), BF16)], axis=1)
    w_gla_out = glaout_g.reshape(GLA_DV, d)
    z1, u2 = _norm_matmul(h2, row(mix_norm_g[1]), w_gla_in, GLA_COLS, "gla_in_proj")
    ffn1_srcs, ffn1_lands, _ = arrived("gather_ffn1", ffn1_gather, z1)
    ffn1_share = _start_copies("gather_ffn1_share_start", ffn1_srcs, ffn1_lands, _plan_share, 4 * len(ffn1_srcs))
    o1, mix1, states = _gla_seq_fwd(z1, gate_w_pad, gate_b_f, head_g_f, dep=ffn1_share[-1])
    h3 = _matmul_residual(mix1, w_gla_out, h2, "gla_out_proj")
    w1g1, w2g1 = [unhalve(g) for g in _wait_copies("gather_ffn1_share_wait", ffn1_share, _plan_share, h3)[1]]
    h4, hp1, uf1 = _ffn_fwd(h3, row(ffn_norm_g[1]), w1g1, w2g1, "ffn1_fwd")

    dev_idx = (2 * chip + core).reshape(1).astype(jnp.int32)

    def start_reduce(name, grads):
        srcs = [_split_rows(g) for g in grads]
        lands = [lax.empty((N_DEVICES,) + s.shape[2:], s.dtype) for s in srcs]
        return _start_copies(name + "_scatter_start", srcs, lands, _plan_scatter_all, len(OTHER_DEVICES) * len(srcs))

    def finish_reduce(name, started, after):
        srcs, lands = _wait_copies(name + "_scatter_wait", started, _plan_scatter_all, after)
        return [_sum_own_and_slots(s, l, dev_idx, "%s_slot_sum_%d" % (name, k)) for k, (s, l) in enumerate(zip(srcs, lands))]

    dh4, d_final_g, loss_part = _loss_bwd(h4, row(final_norm_g), loss_target[0])

    dh3, dhp1, d_ffn_g1 = _ffn_bwd_data(dh4, h3, row(ffn_norm_g[1]), hp1, w1g1, w2g1, "ffn1_bwd")
    dw1_1 = _wgrad(uf1, dhp1, N_CHIPS, d, d, False, True, False, "ffn1_dw1", rows=WGRAD_ROWS_BF16)
    dw2_1 = _wgrad(hp1, dh4, N_CHIPS, d, d, True, False, True, "ffn1_dw2")
    ffn1_reduce = start_reduce("ffn1", [dw1_1, dw2_1])

    dmix1 = _dgrad(dh3, w_gla_out, "gla_out_dgrad", dep=ffn1_reduce[-1])
    dw_gla_out = _wgrad(mix1, dh3, 1, GLA_DV, d, False, False, False, "gla_out_dw")
    dz1, d_gate_w, d_gate_b, d_head_g = _gla_seq_bwd(dmix1, o1, z1, states, gate_w_pad, gate_b_f, head_g_f)
    dh2, d_mix_g1 = _dgrad_norm_bwd(dz1, w_gla_in, h2, row(mix_norm_g[1]), dh3, GLA_COLS, "gla_in_dgrad")
    dw_gla_in = _wgrad(u2, dz1, GLA_IN_PAD // 640, d, 640, False, True, False, "gla_in_dw", rows=WGRAD_ROWS_BF16)
    gla_in_shards = jnp.stack([_take_cols(dw_gla_in, j * (GLA_IN // N_CHIPS), GLA_IN // N_CHIPS) for j in range(N_CHIPS)])
    gla_reduce = start_reduce("gla", [gla_in_shards, dw_gla_out.reshape(N_CHIPS, -1, d)])

    dh1, dhp0, d_ffn_g0 = _ffn_bwd_data(dh2, h1, row(ffn_norm_g[0]), hp0, w1g0, w2g0, "ffn0_bwd", dep=gla_reduce[-1])
    dw1_0 = _wgrad(uf0, dhp0, N_CHIPS, d, d, False, True, False, "ffn0_dw1", rows=WGRAD_ROWS_BF16)
    dw2_0 = _wgrad(hp0, dh2, N_CHIPS, d, d, True, False, True, "ffn0_dw2")
    ffn0_reduce = start_reduce("ffn0", [dw1_0, dw2_0])

    dw_cp_out = _wgrad(mix0, dh1, 1, CONV_DIM + POOL_DIM, d, False, False, False, "cp_out_dw", dep=ffn0_reduce[-1])
    dz0, d_conv_w, d_cp_vec, d_pool_w = _cp_seq_bwd(dh1, w_cp_out, z0, c0, pm0, conv_w_f, cp_ln_g, cp_ln_b, cp_pool_w[0],
                                                    cp_pool_scale, dep=ffn0_reduce[-1])
    grad_x, dh0_head, d_mix_g0 = _dgrad_norm_bwd_input(dz0, w_cp_in, h0, row(mix_norm_g[0]), dh1, 512, "cp_in_dgrad")
    grad_x = grad_x[None]
    dw_cp_in = _wgrad(u0, dz0, 1, d, CP_IN, False, False, False, "cp_in_dw")
    dw_cp_in = jnp.stack([_take_cols(dw_cp_in, j * (CP_IN // N_CHIPS), CP_IN // N_CHIPS) for j in range(N_CHIPS)])

    cp_reduce = start_reduce("cp", [dw_cp_in, dw_cp_out.reshape(N_CHIPS, -1, d)])
    small_full = [dh0_head[PAD_ROWS:CHUNK],jnp.concatenate([d_mix_g0, d_mix_g1], axis=0),
                  jnp.concatenate([d_ffn_g0, d_ffn_g1], axis=0), d_conv_w[:CONV_WIDTH][None],
                  d_cp_vec[0:1], d_cp_vec[1:2], d_cp_vec[2:3], d_pool_w[None], d_cp_vec[3:4],
                  d_gate_w[:GATE_RANK][None], d_gate_b, d_head_g, d_final_g[0], loss_part[0, 0:1]]
    small_mine = _pack(small_full)
    whole = _plan_exchange(0)
    small_exchange = _start_copies("small_exchange_start", [small_mine], [lax.empty(small_mine.shape, F32)], whole, 1,
                                   dep=cp_reduce[-1])
    red_ffn1 = finish_reduce("ffn1", ffn1_reduce, small_exchange[-1])
    red_gla = finish_reduce("gla", gla_reduce, small_exchange[-1])
    (small_sent,), (small_recv,) = _wait_copies("small_exchange_wait", small_exchange, whole, [red_ffn1[1], red_gla[1]])
    small_chip = _add2(small_sent, small_recv, "chip_sum_small")
    small_slots = lax.dynamic_update_slice(jnp.zeros((N_CHIPS,) + small_chip.shape, F32), small_chip[None], (chip, 0, 0))
    small_reduce = _start_copies("small_scatter_start", [small_chip], [small_slots], _plan_scatter(0), 3)

    big = {"w1": (ffn_w1, m_ffn_w1, v_ffn_w1), "w2": (ffn_w2, m_ffn_w2, v_ffn_w2),
           "cp_in": (cp_w_in, m_cp_w_in, v_cp_w_in), "cp_out": (cp_w_out, m_cp_w_out, v_cp_w_out),
           "gla_in": (gla_w_in, m_gla_w_in, v_gla_w_in), "gla_out": (gla_w_out, m_gla_w_out, v_gla_w_out)}
    other_idx = (1 - core).reshape(1).astype(jnp.int32)

    def adamw_by_halves(tag, reduced, dep=None):
        flat = [r for n in reduced for r in reduced[n]]
        join_plan = _plan_exchange(0)
        join = _start_copies(tag + "_join_start", flat, [lax.empty(r.shape, F32) for r in flat], join_plan, len(flat),
                             dep=dep)
        views = {n: [_split_rows(a) for a in big[n]] for n in reduced}
        own, k = {}, 0
        for n in reduced:
            mine = join[2][k:k + len(reduced[n])]
            k += len(reduced[n])
            own[n] = _adamw_half(views[n][0], mine, views[n][1], views[n][2], c_idx, None, "adamw_%s_own" % n)
        _, arrived_halves = _wait_copies(tag + "_join_wait", join, join_plan, [own[n][1] for n in reduced])
        outs, k = {}, 0
        for n in reduced:
            theirs = arrived_halves[k:k + len(reduced[n])]
            k += len(reduced[n])
            res = _adamw_half(views[n][0], theirs, views[n][1], views[n][2], other_idx, own[n], "adamw_%s_sibling" % n)
            outs[n] = [o.reshape(big[n][0].shape) for o in res]
        return outs

    big_out = adamw_by_halves("gla", {"gla_in": [red_gla[0]], "gla_out": [red_gla[1]]}, dep=small_reduce[-1])
    red_ffn0 = finish_reduce("ffn0", ffn0_reduce, big_out["gla_out"][1])
    big_out.update(adamw_by_halves("ffn", {"w1": [red_ffn0[0], red_ffn1[0]], "w2": [red_ffn0[1], red_ffn1[1]]}))
    red_cp = finish_reduce("cp", cp_reduce, big_out["w2"][1])
    _, (small_landed,) = _wait_copies("small_scatter_wait", small_reduce, _plan_scatter(0), big_out["w2"][1])
    small_red = _sum_slots(small_landed, "slot_sum_small")
    big_out.update(adamw_by_halves("cp", {"cp_in": [red_cp[0]], "cp_out": [red_cp[1]]}))

    (g_meta, g_mix, g_ffn, g_conv_w, g_conv_b, g_ln_g, g_ln_b, g_pool_w, g_pool_scale, g_gate_w, g_gate_b, g_head,
     g_final, loss_sum) = _unpack(small_red, [a.shape for a in small_full])
    g_meta = _col_shard(g_meta, chip, meta_tokens.shape[-1])
    g_conv_w = _col_shard(g_conv_w, chip, cp_conv_w.shape[-1])
    g_gate_w = _col_shard(g_gate_w, chip, gla_gate_w2.shape[-1])
    g_gate_b = _col_shard(g_gate_b, chip, gla_gate_b.shape[-1])
    g_head = _col_shard(g_head, chip, gla_head_g.shape[-1])
    small_w = [meta_tokens, mix_norm_g, ffn_norm_g, cp_conv_w, cp_conv_b, cp_ln_g, cp_ln_b, cp_pool_w, cp_pool_scale,
               gla_gate_w2, gla_gate_b, gla_head_g, final_norm_g]
    small_m = [m_meta_tokens, m_mix_norm_g, m_ffn_norm_g, m_cp_conv_w, m_cp_conv_b, m_cp_ln_g, m_cp_ln_b, m_cp_pool_w,
               m_cp_pool_scale, m_gla_gate_w2, m_gla_gate_b, m_gla_head_g, m_final_norm_g]
    small_v = [v_meta_tokens, v_mix_norm_g, v_ffn_norm_g, v_cp_conv_w, v_cp_conv_b, v_cp_ln_g, v_cp_ln_b, v_cp_pool_w,
               v_cp_pool_scale, v_gla_gate_w2, v_gla_gate_b, v_gla_head_g, v_final_norm_g]
    small_g = [g_meta, g_mix, g_ffn, g_conv_w, g_conv_b, g_ln_g, g_ln_b, g_pool_w, g_pool_scale, g_gate_w, g_gate_b,
               g_head, g_final]
    shapes = [w.shape for w in small_w]
    small_g = [g.reshape(s) for g, s in zip(small_g, shapes)]
    at_least_2d = lambda arrs: [a.reshape(1, -1) if a.ndim == 1 else a for a in arrs]
    s_delta, s_m, s_v = _adamw_many(at_least_2d(small_w), at_least_2d(small_g), at_least_2d(small_m), at_least_2d(small_v))
    s_delta, s_m, s_v = [[a.reshape(s) for a, s in zip(group, shapes)] for group in (s_delta, s_m, s_v)]

    order = ["meta", "mix", "ffn", "w1", "w2", "cp_in", "conv_w", "conv_b", "ln_g", "ln_b", "pool_w", "pool_scale",
             "cp_out", "gla_in", "gate_w", "gate_b", "head", "gla_out", "final"]
    small_names = ["meta", "mix", "ffn", "conv_w", "conv_b", "ln_g", "ln_b", "pool_w", "pool_scale", "gate_w", "gate_b",
                   "head", "final"]
    big_names = ["w1", "w2", "cp_in", "cp_out", "gla_in", "gla_out"]
    table = {n: (small_g[i], s_delta[i], s_m[i], s_v[i]) for i, n in enumerate(small_names)}
    table.update({n: tuple(big_out[n]) for n in big_names})
    loss = loss_sum.reshape(())
    return (loss, grad_x, *[table[n][0] for n in order], *[table[n][1] for n in order],
            *[table[n][2] for n in order], *[table[n][3] for n in order])
```

```python
import functools

import jax
import jax.numpy as jnp
from jax import lax
from jax.experimental import pallas as pl
from jax.experimental.pallas import tpu as pltpu

F32 = jnp.float32
BF16 = jnp.bfloat16

D_MODEL = 1024
N_META = 16
CHUNK = 64
PAD_ROWS = CHUNK - N_META
EPS = 1e-5
CONV_DIM = 512
CONV_WIDTH = 31
CONV_HALO = 32
POOL_DIM = 512
POOL_WINDOWS = (2, 4, 8, 16)
POOL_GROUP = 128
POOL_HALO = 16
CP_IN = 2 * CONV_DIM + POOL_DIM
GLA_HEADS = 4
GLA_DK = 512
GLA_DV = 1024
GLA_HK = GLA_DK // GLA_HEADS
GLA_HV = GLA_DV // GLA_HEADS
GATE_RANK = 16
GATE_PAD = 128
GATE_NORM = 16.0
GLA_IN = 2 * GLA_DK + 2 * GLA_DV + GATE_RANK
GLA_IN_PAD = 2 * GLA_DK + 2 * GLA_DV + GATE_PAD
GLA_COLS = 1280
N_CHIPS = 4
ADAM_LR = 0.001
ADAM_B1 = 0.9
ADAM_B2 = 0.999
ADAM_EPS = 1e-08
ADAM_WD = 0.01
ADAM_STEP = 10

VMEM_LIMIT_BYTES = 56 * 1024 * 1024
ROW_TILE_TARGET = 832
TOKEN_TILE_TARGET = 1040
PACK_WIDTH = 1024
MESH = pl.DeviceIdType.MESH
HBM_SPEC = pl.BlockSpec(memory_space=pltpu.HBM)
ANY_SPEC = pl.BlockSpec(memory_space=pl.ANY)
SEM_SPEC = pl.BlockSpec(memory_space=pltpu.SEMAPHORE)
SIDE_EFFECT = pltpu.SideEffectType.DATAFLOW_SIDE_EFFECTING


def _cparams(*sem):
    return pltpu.CompilerParams(dimension_semantics=sem, vmem_limit_bytes=VMEM_LIMIT_BYTES)


def _row_tile(t, target, mult):
    best = mult
    for cand in range(mult, min(t, target) + 1, mult):
        if t % cand == 0:
            best = cand
    assert t % best == 0, (t, best)
    return best


def _rms(h, g):
    return h * lax.rsqrt(jnp.mean(h * h, axis=-1, keepdims=True) + EPS) * g


def _rms_bwd(h, g, du):
    r = lax.rsqrt(jnp.mean(h * h, axis=-1, keepdims=True) + EPS)
    xhat = h * r
    dxh = du * g
    dh = r * (dxh - xhat * jnp.mean(dxh * xhat, axis=-1, keepdims=True))
    return dh, du * xhat


def _valid_rows(i, tm):
    row = i * tm + lax.broadcasted_iota(jnp.int32, (tm, 1), 0)
    return row >= PAD_ROWS


def _dot(a, b):
    return jnp.dot(a, b, preferred_element_type=F32)


def _dot_nt(a, b):
    return lax.dot_general(a, b, (((1,), (1,)), ((), ())), preferred_element_type=F32)


def _dot_tn(a, b):
    return lax.dot_general(a, b, (((0,), (0,)), ((), ())), preferred_element_type=F32)


def _accumulate(ref, val, first):
    @pl.when(first)
    def _():
        ref[...] = val

    @pl.when(jnp.logical_not(first))
    def _():
        ref[...] += val


def _call_after(dep, body, n_in, in_specs, args, **kw):
    if dep is None:
        return pl.pallas_call(body, in_specs=in_specs, **kw)(*args)

    def with_dep(*refs):
        body(*refs[:n_in], *refs[n_in + 1:])

    return pl.pallas_call(with_dep, in_specs=list(in_specs) + [ANY_SPEC], **kw)(*args, dep)


def _norm_matmul(h, g, w, nc, name, dep=None):
    t, d = h.shape
    n = w.shape[1]
    tm = _row_tile(t, TOKEN_TILE_TARGET, 16)

    def body(h_ref, g_ref, w_ref, z_ref, u_ref):
        u = _rms(h_ref[...], g_ref[...]).astype(BF16)
        u_ref[...] = u
        for n0 in range(0, n, nc):
            n1 = min(n0 + nc, n)
            z_ref[:, n0:n1] = _dot(u, w_ref[:, n0:n1]).astype(BF16)

    return _call_after(
        dep, body, 3,
        [pl.BlockSpec((tm, d), lambda i: (i, 0)), pl.BlockSpec((1, d), lambda i: (0, 0)),
         pl.BlockSpec((d, n), lambda i: (0, 0))], (h, g, w), grid=(t // tm,),
        out_specs=[pl.BlockSpec((tm, n), lambda i: (i, 0)), pl.BlockSpec((tm, d), lambda i: (i, 0))],
        out_shape=[jax.ShapeDtypeStruct((t, n), BF16), jax.ShapeDtypeStruct((t, d), BF16)],
        compiler_params=_cparams("parallel"), name=name)


def _matmul_residual(a, w, h, name, dep=None):
    t, k = a.shape
    d = w.shape[1]
    tm = _row_tile(t, TOKEN_TILE_TARGET, 16)

    def body(a_ref, w_ref, h_ref, o_ref):
        o_ref[...] = h_ref[...] + _dot(a_ref[...], w_ref[...])

    return _call_after(
        dep, body, 3,
        [pl.BlockSpec((tm, k), lambda i: (i, 0)), pl.BlockSpec((k, d), lambda i: (0, 0)),
         pl.BlockSpec((tm, d), lambda i: (i, 0))], (a, w, h), grid=(t // tm,),
        out_specs=pl.BlockSpec((tm, d), lambda i: (i, 0)),
        out_shape=jax.ShapeDtypeStruct((t, d), F32),
        compiler_params=_cparams("parallel"), name=name)


def _ffn_fwd(h, g, w1g, w2g, name):
    t, d = h.shape
    ns, ffs = w1g.shape[0], w1g.shape[2]
    tm = _row_tile(t, TOKEN_TILE_TARGET, 16)

    def body(h_ref, g_ref, w1_ref, w2_ref, ho_ref, hp_ref, u_ref, acc_ref):
        s = pl.program_id(1)

        @pl.when(s == 0)
        def _():
            u_ref[...] = _rms(h_ref[...], g_ref[...]).astype(BF16)

        hp = _dot(u_ref[...], w1_ref[...])
        hp_ref[...] = hp.astype(BF16)
        a = jnp.maximum(hp, 0.0)
        _accumulate(acc_ref, _dot((a * a).astype(BF16), w2_ref[...]), s == 0)

        @pl.when(s == ns - 1)
        def _():
            ho_ref[...] = h_ref[...] + acc_ref[...]

    return pl.pallas_call(
        body, grid=(t // tm, ns),
        in_specs=[pl.BlockSpec((tm, d), lambda i, s: (i, 0)), pl.BlockSpec((1, d), lambda i, s: (0, 0)),
                  pl.BlockSpec((None, d, ffs), lambda i, s: (s, 0, 0)),
                  pl.BlockSpec((None, ffs, d), lambda i, s: (s, 0, 0))],
        out_specs=[pl.BlockSpec((tm, d), lambda i, s: (i, 0)), pl.BlockSpec((tm, ffs), lambda i, s: (i, s)),
                   pl.BlockSpec((tm, d), lambda i, s: (i, 0))],
        out_shape=[jax.ShapeDtypeStruct((t, d), F32), jax.ShapeDtypeStruct((t, ns * ffs), BF16),
                   jax.ShapeDtypeStruct((t, d), BF16)],
        scratch_shapes=[pltpu.VMEM((tm, d), F32)],
        compiler_params=_cparams("parallel", "arbitrary"), name=name)(h, g, w1g, w2g)


def _ffn_bwd_data(dh, h, g, hp, w1g, w2g, name, dep=None):
    t, d = h.shape
    ns, ffs = w1g.shape[0], w1g.shape[2]
    tm = _row_tile(t, ROW_TILE_TARGET, CHUNK)

    def body(dh_ref, h_ref, g_ref, hp_ref, w1_ref, w2_ref, dhi_ref, dhp_ref, dg_ref, acc_ref):
        i, s = pl.program_id(0), pl.program_id(1)
        da = _dot_nt(dh_ref[...].astype(BF16), w2_ref[...])
        dhp = (da * (2.0 * jnp.maximum(hp_ref[...].astype(F32), 0.0))).astype(BF16)
        dhp_ref[...] = dhp
        _accumulate(acc_ref, _dot_nt(dhp, w1_ref[...]), s == 0)

        @pl.when(s == ns - 1)
        def _():
            dhn, dgr = _rms_bwd(h_ref[...], g_ref[...], acc_ref[...])
            dhi_ref[...] = jnp.where(_valid_rows(i, tm), dh_ref[...] + dhn, 0.0)
            _accumulate(dg_ref, jnp.sum(dgr, axis=0, keepdims=True), i == 0)

    return _call_after(
        dep, body, 6,
        [pl.BlockSpec((tm, d), lambda i, s: (i, 0)), pl.BlockSpec((tm, d), lambda i, s: (i, 0)),
         pl.BlockSpec((1, d), lambda i, s: (0, 0)), pl.BlockSpec((tm, ffs), lambda i, s: (i, s)),
         pl.BlockSpec((None, d, ffs), lambda i, s: (s, 0, 0)),
         pl.BlockSpec((None, ffs, d), lambda i, s: (s, 0, 0))], (dh, h, g, hp, w1g, w2g), grid=(t // tm, ns),
        out_specs=[pl.BlockSpec((tm, d), lambda i, s: (i, 0)), pl.BlockSpec((tm, ffs), lambda i, s: (i, s)),
                   pl.BlockSpec((1, d), lambda i, s: (0, 0))],
        out_shape=[jax.ShapeDtypeStruct((t, d), F32), jax.ShapeDtypeStruct((t, ns * ffs), BF16),
                   jax.ShapeDtypeStruct((1, d), F32)],
        scratch_shapes=[pltpu.VMEM((tm, d), F32)],
        compiler_params=_cparams("arbitrary", "arbitrary"), name=name)


WGRAD_ROWS = 2048
WGRAD_ROWS_BF16 = 4096


def _wgrad(x, dy, nb, xc, yc, x_by_block, dy_by_block, relu2, name, dep=None, rows=WGRAD_ROWS):
    t = x.shape[0]
    tk = _row_tile(t - CHUNK, rows, CHUNK)

    def prep(xv):
        if relu2:
            xv = jnp.maximum(xv.astype(F32), 0.0)
            xv = xv * xv
        return xv.astype(BF16)

    nk = (t - CHUNK) // tk

    def body(xh_ref, dyh_ref, x_ref, dy_ref, o_ref, acc_ref):
        k = pl.program_id(1)
        p = _dot_tn(prep(x_ref[...]), dy_ref[...].astype(BF16))

        @pl.when(k == 0)
        def _():
            acc_ref[...] = p + _dot_tn(prep(xh_ref[...]), dyh_ref[...].astype(BF16))

        @pl.when(k > 0)
        def _():
            acc_ref[...] += p

        @pl.when(k == nk - 1)
        def _():
            o_ref[...] = acc_ref[...].astype(BF16)

    def head(width, by_block):
        return pl.BlockSpec((CHUNK, width), (lambda b, k: (0, b)) if by_block else (lambda b, k: (0, 0)))

    def rest(width, by_block):
        def index(b, k):
            return pl.multiple_of(CHUNK + k * tk, CHUNK), (pl.multiple_of(b * width, 128) if by_block else 0)
        return pl.BlockSpec((pl.Element(tk), pl.Element(width)), index)

    return _call_after(
        dep, body, 4,
        [head(xc, x_by_block), head(yc, dy_by_block), rest(xc, x_by_block), rest(yc, dy_by_block)], (x, dy, x, dy),
        grid=(nb, nk),
        out_specs=pl.BlockSpec((None, xc, yc), lambda b, k: (b, 0, 0)),
        out_shape=jax.ShapeDtypeStruct((nb, xc, yc), BF16),
        scratch_shapes=[pltpu.VMEM((xc, yc), F32)],
        compiler_params=_cparams("parallel", "arbitrary"), name=name)


def _dgrad(dh, w, name, dep=None):
    t, d = dh.shape
    k = w.shape[0]
    tm = _row_tile(t, TOKEN_TILE_TARGET, 16)

    def body(dh_ref, w_ref, o_ref):
        o_ref[...] = _dot_nt(dh_ref[...].astype(BF16), w_ref[...]).astype(BF16)

    return _call_after(
        dep, body, 2,
        [pl.BlockSpec((tm, d), lambda i: (i, 0)), pl.BlockSpec((k, d), lambda i: (0, 0))], (dh, w), grid=(t // tm,),
        out_specs=pl.BlockSpec((tm, k), lambda i: (i, 0)),
        out_shape=jax.ShapeDtypeStruct((t, k), BF16),
        compiler_params=_cparams("parallel"), name=name)


def _dgrad_norm_bwd(dz, w, h, g, dh, nc, name):
    t, d = h.shape
    n = w.shape[1]
    tm = _row_tile(t, ROW_TILE_TARGET // 2, 16)

    def body(dz_ref, w_ref, h_ref, g_ref, dh_ref, dhi_ref, dg_ref):
        i = pl.program_id(0)
        du = jnp.zeros((tm, d), F32)
        for n0 in range(0, n, nc):
            n1 = min(n0 + nc, n)
            du = du + _dot_nt(dz_ref[:, n0:n1], w_ref[:, n0:n1])
        dhn, dgr = _rms_bwd(h_ref[...], g_ref[...], du)
        dhi_ref[...] = jnp.where(_valid_rows(i, tm), dh_ref[...] + dhn, 0.0)
        _accumulate(dg_ref, jnp.sum(dgr, axis=0, keepdims=True), i == 0)

    return pl.pallas_call(
        body, grid=(t // tm,),
        in_specs=[pl.BlockSpec((tm, n), lambda i: (i, 0)), pl.BlockSpec((d, n), lambda i: (0, 0)),
                  pl.BlockSpec((tm, d), lambda i: (i, 0)), pl.BlockSpec((1, d), lambda i: (0, 0)),
                  pl.BlockSpec((tm, d), lambda i: (i, 0))],
        out_specs=[pl.BlockSpec((tm, d), lambda i: (i, 0)), pl.BlockSpec((1, d), lambda i: (0, 0))],
        out_shape=[jax.ShapeDtypeStruct((t, d), F32), jax.ShapeDtypeStruct((1, d), F32)],
        compiler_params=_cparams("arbitrary"), name=name)(dz, w, h, g, dh)


def _dgrad_norm_bwd_input(dz, w, h, g, dh, nc, name):
    t, d = h.shape
    n = w.shape[1]
    tl = _row_tile(t - CHUNK, 512, CHUNK)

    def grads(dz_ref, w_ref, h_ref, g_ref, dh_ref, rows):
        du = jnp.zeros((rows, d), F32)
        for n0 in range(0, n, nc):
            n1 = min(n0 + nc, n)
            du = du + _dot_nt(dz_ref[:, n0:n1], w_ref[:, n0:n1])
        dhn, dgr = _rms_bwd(h_ref[...], g_ref[...], du)
        return dh_ref[...] + dhn, jnp.sum(dgr, axis=0, keepdims=True)

    def rest_body(dz_ref, w_ref, h_ref, g_ref, dh_ref, dg_head_ref, dx_ref, dg_ref):
        dx, dg = grads(dz_ref, w_ref, h_ref, g_ref, dh_ref, tl)
        dx_ref[...] = dx

        @pl.when(pl.program_id(0) == 0)
        def _():
            dg_ref[...] = dg_head_ref[...] + dg

        @pl.when(pl.program_id(0) > 0)
        def _():
            dg_ref[...] += dg

    def head_body(dz_ref, w_ref, h_ref, g_ref, dh_ref, dx_ref, dg_ref):
        dx, dg = grads(dz_ref, w_ref, h_ref, g_ref, dh_ref, CHUNK)
        dx_ref[...] = jnp.where(_valid_rows(0, CHUNK), dx, 0.0)
        dg_ref[...] = dg

    def shifted(width):
        return pl.BlockSpec((pl.Element(tl), pl.Element(width)), lambda i: (pl.multiple_of(CHUNK + i * tl, CHUNK), 0))

    whole = [pl.BlockSpec((d, n), lambda i: (0, 0)), pl.BlockSpec((1, d), lambda i: (0, 0))]
    head = lambda width: pl.BlockSpec((CHUNK, width), lambda i: (0, 0))
    dh_head, dg_head = pl.pallas_call(
        head_body, grid=(1,), in_specs=[head(n), whole[0], head(d), whole[1], head(d)],
        out_specs=[head(d), whole[1]],
        out_shape=[jax.ShapeDtypeStruct((CHUNK, d), F32), jax.ShapeDtypeStruct((1, d), F32)],
        compiler_params=_cparams("arbitrary"), name=name + "_head")(dz, w, h, g, dh)
    dx, dg = pl.pallas_call(
        rest_body, grid=((t - CHUNK) // tl,),
        in_specs=[shifted(n), whole[0], shifted(d), whole[1], shifted(d), whole[1]],
        out_specs=[pl.BlockSpec((tl, d), lambda i: (i, 0)), whole[1]],
        out_shape=[jax.ShapeDtypeStruct((t - CHUNK, d), F32), jax.ShapeDtypeStruct((1, d), F32)],
        compiler_params=_cparams("arbitrary"), name=name)(dz, w, h, g, dh, dg_head)
    return dx, dh_head, dg


def _loss_bwd(h, g, target):
    t, d = h.shape
    tl = _row_tile(t - CHUNK, 1024, CHUNK)

    def body(h_ref, g_ref, t_ref, dh_ref, dg_ref, loss_ref):
        i = pl.program_id(0)
        hv, gv = h_ref[...], g_ref[...]
        err = _rms(hv, gv) - t_ref[...]
        part = 0.5 * jnp.sum(jnp.mean(err * err, axis=-1, keepdims=True), axis=0, keepdims=True)
        dhn, dgr = _rms_bwd(hv, gv, err * (1.0 / d))
        dh_ref[...] = dhn
        _accumulate(dg_ref, jnp.sum(dgr, axis=0, keepdims=True), i == 0)
        _accumulate(loss_ref, jnp.broadcast_to(part, (8, 128)), i == 0)

    shifted = pl.BlockSpec((pl.Element(tl), pl.Element(d)), lambda i: (pl.multiple_of(CHUNK + i * tl, CHUNK), 0))
    dh, dg, loss = pl.pallas_call(
        body, grid=((t - CHUNK) // tl,),
        in_specs=[shifted, pl.BlockSpec((1, d), lambda i: (0, 0)), pl.BlockSpec((tl, d), lambda i: (i, 0))],
        out_specs=[shifted, pl.BlockSpec((1, d), lambda i: (0, 0)), pl.BlockSpec((8, 128), lambda i: (0, 0))],
        out_shape=[jax.ShapeDtypeStruct((t, d), F32), jax.ShapeDtypeStruct((1, d), F32),
                   jax.ShapeDtypeStruct((8, 128), F32)],
        compiler_params=_cparams("arbitrary"), name="loss_bwd")(h, g, target)

    def zero_head(dh_ref, o_ref):
        o_ref[...] = jnp.zeros_like(o_ref)

    dh = pl.pallas_call(
        zero_head, grid=(1,), in_specs=[ANY_SPEC], out_specs=pl.BlockSpec((CHUNK, d), lambda i: (0, 0)),
        out_shape=jax.ShapeDtypeStruct((t, d), F32), input_output_aliases={0: 0}, name="loss_bwd_head")(dh)
    return dh, dg, loss


CONV_BLOCK = 32


def _silu(x):
    return x * jax.nn.sigmoid(x)


def _row_shifts(win):
    n = win.shape[0]
    return [win] + [pltpu.roll(win, n - j, 0) for j in range(1, 8)]


def _cp_seq_fwd(z, conv_w, conv_b, ln_g, ln_b, pool_w, pool_scale):
    t = z.shape[0]
    tm = _row_tile(t, ROW_TILE_TARGET, CHUNK)

    def body(z_ref, cw_ref, cb_ref, lg_ref, lb_ref, pw_ref, ps_ref, c_ref, pm_ref, mix_ref, gbuf, pbuf):
        i = pl.program_id(0)

        @pl.when(i == 0)
        def _():
            gbuf[0:CONV_HALO, :] = jnp.zeros((CONV_HALO, CONV_DIM), F32)
            pbuf[0:POOL_HALO, :] = jnp.zeros((POOL_HALO, POOL_DIM), F32)

        @pl.when(i > 0)
        def _():
            gbuf[0:CONV_HALO, :] = gbuf[tm:tm + CONV_HALO, :]
            pbuf[0:POOL_HALO, :] = pbuf[tm:tm + POOL_HALO, :]

        av = z_ref[:, 0:CONV_DIM].astype(F32)
        ag = z_ref[:, CONV_DIM:2 * CONV_DIM].astype(F32)
        gbuf[CONV_HALO:CONV_HALO + tm, :] = av * jax.nn.sigmoid(ag)
        pbuf[POOL_HALO:POOL_HALO + tm, :] = z_ref[:, 2 * CONV_DIM:CP_IN].astype(F32)

        def conv_block(rb, carry):
            base = pl.multiple_of(rb * CONV_BLOCK, CONV_BLOCK)
            shifted = _row_shifts(gbuf[pl.ds(base, CONV_BLOCK + CONV_HALO), :])
            acc = jnp.zeros((CONV_BLOCK, CONV_DIM), F32)
            for k in range(CONV_WIDTH):
                whole, part = divmod(CONV_HALO - (CONV_WIDTH - 1) + k, 8)
                acc = acc + cw_ref[k:k + 1, :] * shifted[part][8 * whole:8 * whole + CONV_BLOCK, :]
            c_ref[pl.ds(base, CONV_BLOCK), :] = acc + cb_ref[...]
            return carry

        lax.fori_loop(0, tm // CONV_BLOCK, conv_block, 0)

        c = c_ref[...]
        mu = jnp.mean(c, axis=-1, keepdims=True)
        xc = c - mu
        ln = xc * lax.rsqrt(jnp.mean(xc * xc, axis=-1, keepdims=True) + EPS) * lg_ref[...] + lb_ref[...]
        row = i * tm + lax.broadcasted_iota(jnp.int32, (tm, 1), 0)
        mix_ref[:, 0:CONV_DIM] = jnp.where(row >= PAD_ROWS, _silu(ln), 0.0).astype(BF16)

        tpos = (row - PAD_ROWS + 1).astype(F32)
        for gi, wdw in enumerate(POOL_WINDOWS):
            lo = POOL_GROUP * gi
            run, step = pbuf[:, lo:lo + POOL_GROUP], 1
            cur = run[POOL_HALO:POOL_HALO + tm, :]
            while step < wdw:
                run = run + pltpu.roll(run, step, 0)
                step *= 2
            pm = (run[POOL_HALO:POOL_HALO + tm, :] / jnp.clip(tpos, 1.0, float(wdw)) - cur).astype(BF16)
            pm_ref[:, lo:lo + POOL_GROUP] = pm
            pg = _dot(pm, pw_ref[gi].astype(BF16))
            mix_ref[:, CONV_DIM + lo:CONV_DIM + lo + POOL_GROUP] = (pg * ps_ref[:, lo:lo + POOL_GROUP]).astype(BF16)

    vec = pl.BlockSpec((1, CONV_DIM), lambda i: (0, 0))
    return pl.pallas_call(
        body, grid=(t // tm,),
        in_specs=[pl.BlockSpec((tm, CP_IN), lambda i: (i, 0)),
                  pl.BlockSpec((CONV_WIDTH, CONV_DIM), lambda i: (0, 0)), vec, vec, vec,
                  pl.BlockSpec((len(POOL_WINDOWS), POOL_GROUP, POOL_GROUP), lambda i: (0, 0, 0)), vec],
        out_specs=[pl.BlockSpec((tm, CONV_DIM), lambda i: (i, 0)), pl.BlockSpec((tm, POOL_DIM), lambda i: (i, 0)),
                   pl.BlockSpec((tm, CONV_DIM + POOL_DIM), lambda i: (i, 0))],
        out_shape=[jax.ShapeDtypeStruct((t, CONV_DIM), F32), jax.ShapeDtypeStruct((t, POOL_DIM), BF16),
                   jax.ShapeDtypeStruct((t, CONV_DIM + POOL_DIM), BF16)],
        scratch_shapes=[pltpu.VMEM((tm + CONV_HALO, CONV_DIM), F32), pltpu.VMEM((tm + POOL_HALO, POOL_DIM), F32)],
        compiler_params=_cparams("arbitrary"), name="cp_seq_fwd")(z, conv_w, conv_b, ln_g, ln_b, pool_w, pool_scale)


def _cp_seq_bwd(dh, w_out, z, c, pm, conv_w, ln_g, ln_b, pool_w, pool_scale, dep=None):
    t, d = dh.shape
    tm = _row_tile(t, ROW_TILE_TARGET, CHUNK)
    nt = t // tm

    def body(dh_ref, wo_ref, z_ref, c_ref, pm_ref, cw_ref, lg_ref, lb_ref, pw_ref, ps_ref,
             dz_ref, dcw_ref, dvec_ref, dpw_ref, dcbuf, qbuf, glu_buf, dwacc, dmix_ref):
        i = pl.program_id(0)
        tile = nt - 1 - i
        dmix_ref[...] = _dot_nt(dh_ref[...].astype(BF16), wo_ref[...])

        @pl.when(i == 0)
        def _():
            dcbuf[tm:tm + CONV_HALO, :] = jnp.zeros((CONV_HALO, CONV_DIM), F32)
            qbuf[tm:tm + POOL_HALO, :] = jnp.zeros((POOL_HALO, POOL_DIM), F32)
            dcw_ref[...] = jnp.zeros_like(dcw_ref)
            dwacc[...] = jnp.zeros_like(dwacc)
            dvec_ref[...] = jnp.zeros_like(dvec_ref)
            dpw_ref[...] = jnp.zeros_like(dpw_ref)

        @pl.when(i > 0)
        def _():
            dcbuf[tm:tm + CONV_HALO, :] = dcbuf[0:CONV_HALO, :]
            qbuf[tm:tm + POOL_HALO, :] = qbuf[0:POOL_HALO, :]

        row = tile * tm + lax.broadcasted_iota(jnp.int32, (tm, 1), 0)
        cv = c_ref[...]
        mu = jnp.mean(cv, axis=-1, keepdims=True)
        xc = cv - mu
        rstd = lax.rsqrt(jnp.mean(xc * xc, axis=-1, keepdims=True) + EPS)
        xhat = xc * rstd
        ln = xhat * lg_ref[...] + lb_ref[...]
        sg = jax.nn.sigmoid(ln)
        da = jnp.where(row >= PAD_ROWS, dmix_ref[:, 0:CONV_DIM], 0.0)
        dln = da * (sg * (1.0 + ln * (1.0 - sg)))
        dxh = dln * lg_ref[...]
        dc = rstd * (dxh - jnp.mean(dxh, axis=-1, keepdims=True) - xhat * jnp.mean(dxh * xhat, axis=-1, keepdims=True))
        dcbuf[0:tm, :] = dc
        dvec_ref[0:1, :] += jnp.sum(dc, axis=0, keepdims=True)
        dvec_ref[1:2, :] += jnp.sum(dln * xhat, axis=0, keepdims=True)
        dvec_ref[2:3, :] += jnp.sum(dln, axis=0, keepdims=True)

        av = z_ref[:, 0:CONV_DIM].astype(F32)
        sig_g = jax.nn.sigmoid(z_ref[:, CONV_DIM:2 * CONV_DIM].astype(F32))
        glu_buf[...] = av * sig_g

        def conv_block(rb, carry):
            base = pl.multiple_of(rb * CONV_BLOCK, CONV_BLOCK)
            shifted = _row_shifts(dcbuf[pl.ds(base, CONV_BLOCK + CONV_HALO), :])
            glu = glu_buf[pl.ds(base, CONV_BLOCK), :]
            acc = jnp.zeros((CONV_BLOCK, CONV_DIM), F32)
            for k in range(CONV_WIDTH):
                whole, part = divmod(CONV_WIDTH - 1 - k, 8)
                slab = shifted[part][8 * whole:8 * whole + CONV_BLOCK, :]
                acc = acc + cw_ref[k:k + 1, :] * slab
                prod = slab * glu
                part = prod[0:8]
                for q in range(1, CONV_BLOCK // 8):
                    part = part + prod[8 * q:8 * q + 8]
                dwacc[k] += part
            glu_buf[pl.ds(base, CONV_BLOCK), :] = acc
            return carry

        lax.fori_loop(0, tm // CONV_BLOCK, conv_block, 0)

        @pl.when(i == nt - 1)
        def _():
            for k in range(CONV_WIDTH):
                dcw_ref[k:k + 1, :] = jnp.sum(dwacc[k], axis=0, keepdims=True)
        dglu = glu_buf[...]
        dz_ref[:, 0:CONV_DIM] = (dglu * sig_g).astype(BF16)
        dz_ref[:, CONV_DIM:2 * CONV_DIM] = (dglu * av * sig_g * (1.0 - sig_g)).astype(BF16)

        tpos = (row - PAD_ROWS + 1).astype(F32)
        for gi, wdw in enumerate(POOL_WINDOWS):
            lo = POOL_GROUP * gi
            dp = dmix_ref[:, CONV_DIM + lo:CONV_DIM + lo + POOL_GROUP]
            pmv = pm_ref[:, lo:lo + POOL_GROUP]
            pwb = pw_ref[gi].astype(BF16)
            dvec_ref[3:4, lo:lo + POOL_GROUP] += jnp.sum(dp * _dot(pmv, pwb), axis=0, keepdims=True)
            dq = (dp * ps_ref[:, lo:lo + POOL_GROUP]).astype(BF16)
            dpw_ref[gi] += _dot_tn(pmv, dq)
            dpm = _dot_nt(dq, pwb)
            qbuf[0:tm, lo:lo + POOL_GROUP] = dpm / jnp.clip(tpos, 1.0, float(wdw))
            run, step = qbuf[:, lo:lo + POOL_GROUP], 1
            while step < wdw:
                run = run + pltpu.roll(run, tm + POOL_HALO - step, 0)
                step *= 2
            dz_ref[:, 2 * CONV_DIM + lo:2 * CONV_DIM + lo + POOL_GROUP] = (run[0:tm, :] - dpm).astype(BF16)

    vec = pl.BlockSpec((1, CONV_DIM), lambda i: (0, 0))
    rev = lambda i: (nt - 1 - i, 0)
    return _call_after(
        dep, body, 10,
        [pl.BlockSpec((tm, d), rev), pl.BlockSpec((CONV_DIM + POOL_DIM, d), lambda i: (0, 0)), pl.BlockSpec((tm, CP_IN), rev),
         pl.BlockSpec((tm, CONV_DIM), rev), pl.BlockSpec((tm, POOL_DIM), rev),
         pl.BlockSpec((CONV_WIDTH, CONV_DIM), lambda i: (0, 0)), vec, vec,
         pl.BlockSpec((len(POOL_WINDOWS), POOL_GROUP, POOL_GROUP), lambda i: (0, 0, 0)), vec],
        (dh, w_out, z, c, pm, conv_w, ln_g, ln_b, pool_w, pool_scale), grid=(nt,),
        out_specs=[pl.BlockSpec((tm, CP_IN), rev), pl.BlockSpec((CONV_WIDTH + 1, CONV_DIM), lambda i: (0, 0)),
                   pl.BlockSpec((8, CONV_DIM), lambda i: (0, 0)),
                   pl.BlockSpec((len(POOL_WINDOWS), POOL_GROUP, POOL_GROUP), lambda i: (0, 0, 0))],
        out_shape=[jax.ShapeDtypeStruct((t, CP_IN), BF16), jax.ShapeDtypeStruct((CONV_WIDTH + 1, CONV_DIM), F32),
                   jax.ShapeDtypeStruct((8, CONV_DIM), F32),
                   jax.ShapeDtypeStruct((len(POOL_WINDOWS), POOL_GROUP, POOL_GROUP), F32)],
        scratch_shapes=[pltpu.VMEM((tm + CONV_HALO, CONV_DIM), F32), pltpu.VMEM((tm + POOL_HALO, POOL_DIM), F32),
                        pltpu.VMEM((tm, CONV_DIM), F32), pltpu.VMEM((CONV_WIDTH + 1, 8, CONV_DIM), F32),
                        pltpu.VMEM((tm, CONV_DIM + POOL_DIM), F32)],
        compiler_params=_cparams("arbitrary"), name="cp_seq_bwd")


Q0, K0, V0, G0, R0 =0, GLA_DK, 2 * GLA_DK, 2 * GLA_DK + GLA_DV, 2 * GLA_DK + 2 * GLA_DV


def _split3(x):
    hi = x.astype(BF16)
    r1 = x - hi.astype(F32)
    mid = r1.astype(BF16)
    lo = (r1 - mid.astype(F32)).astype(BF16)
    return hi, mid, lo


def _tri(strict):
    r = lax.broadcasted_iota(jnp.int32, (CHUNK, CHUNK), 0)
    c = lax.broadcasted_iota(jnp.int32, (CHUNK, CHUNK), 1)
    return ((r > c) if strict else (r >= c)).astype(BF16)


def _chunk_sums(x, cpt, strict, pieces):
    tri3 = jnp.broadcast_to(_tri(strict)[None], (cpt, CHUNK, CHUNK))
    acc = None
    for piece in _split3(x.reshape(cpt, CHUNK, x.shape[-1]))[:pieces]:
        part = jnp.einsum("bij,bjk->bik", tri3, piece, preferred_element_type=F32)
        acc = part if acc is None else acc + part
    return acc


def _chunk_decay(r, gw_ref, gb_ref, cpt):
    pre = _dot(r, gw_ref[...]) + gb_ref[...]
    lac = (jnp.minimum(pre, 0.0) - jnp.log(1.0 + jnp.exp(-jnp.abs(pre)))) * (1.0 / GATE_NORM)
    cum3 = _chunk_sums(lac, cpt, False, 3)
    return cum3, cum3[:, CHUNK - 1:CHUNK, :]


def _gla_seq_fwd(z, gate_w, gate_b, head_g, h, w_out, dep=None):
    t = z.shape[0]
    tm = _row_tile(t, ROW_TILE_TARGET, CHUNK)
    cpt = tm // CHUNK
    scale = GLA_HK ** -0.5

    def body(z_ref, gw_ref, gb_ref, hg_ref, h_ref, wo_ref, o_ref, mix_ref, st_ref, ho_ref, state, kdec_s, e_s):
        @pl.when(pl.program_id(0) == 0)
        def _():
            state[...] = jnp.zeros_like(state)

        cum3, tot3 = _chunk_decay(z_ref[:, R0:R0 + GATE_PAD], gw_ref, gb_ref, cpt)
        dec = jnp.exp(jnp.broadcast_to(tot3, cum3.shape) - cum3).reshape(tm, GLA_DK)
        kdec_s[...] = (z_ref[:, K0:K0 + GLA_DK].astype(F32) * dec).astype(BF16)
        e_s[...] = jnp.exp(jnp.broadcast_to(tot3, (cpt, 8, GLA_DK))).reshape(cpt * 8, GLA_DK)

        def chunk(ci, carry):
            rows = pl.ds(pl.multiple_of(ci * CHUNK, CHUNK), CHUNK)
            e_all = e_s[pl.ds(pl.multiple_of(ci * 8, 8), 8), :][0:1, :]
            st_ref[ci] = state[...].astype(BF16)
            for hd in range(GLA_HEADS):
                ks = slice(hd * GLA_HK, (hd + 1) * GLA_HK)
                vs = slice(hd * GLA_HV, (hd + 1) * GLA_HV)
                v = z_ref[rows, V0 + hd * GLA_HV:V0 + (hd + 1) * GLA_HV]
                st = state[vs, :] * e_all[:, ks] + _dot_tn(v, kdec_s[rows, ks])
                state[vs, :] = st
                q = z_ref[rows, Q0 + hd * GLA_HK:Q0 + (hd + 1) * GLA_HK]
                o_ref[rows, vs] = (_dot_nt(q, st.astype(BF16)) * scale).astype(BF16)
            return carry

        lax.fori_loop(0, cpt, chunk, 0, unroll=cpt)

        for hd in range(GLA_HEADS):
            vs = slice(hd * GLA_HV, (hd + 1) * GLA_HV)
            on = _rms(o_ref[:, vs].astype(F32), hg_ref[...])
            gv = z_ref[:, G0 + hd * GLA_HV:G0 + (hd + 1) * GLA_HV].astype(F32)
            mix_ref[:, vs] = (on * _silu(gv)).astype(BF16)
        ho_ref[...] = h_ref[...] + _dot(mix_ref[...], wo_ref[...])

    d = h.shape[1]
    return _call_after(
        dep, body, 6,
        [pl.BlockSpec((tm, GLA_IN_PAD), lambda i: (i, 0)),
         pl.BlockSpec((GATE_PAD, GLA_DK), lambda i: (0, 0)), pl.BlockSpec((1, GLA_DK), lambda i: (0, 0)),
         pl.BlockSpec((1, GLA_HV), lambda i: (0, 0)), pl.BlockSpec((tm, d), lambda i: (i, 0)),
         pl.BlockSpec((GLA_DV, d), lambda i: (0, 0))], (z, gate_w, gate_b, head_g, h, w_out), grid=(t // tm,),
        out_specs=[pl.BlockSpec((tm, GLA_DV), lambda i: (i, 0)), pl.BlockSpec((tm, GLA_DV), lambda i: (i, 0)),
                   pl.BlockSpec((cpt, GLA_DV, GLA_HK), lambda i: (i, 0, 0)), pl.BlockSpec((tm, d), lambda i: (i, 0))],
        out_shape=[jax.ShapeDtypeStruct((t, GLA_DV), BF16), jax.ShapeDtypeStruct((t, GLA_DV), BF16),
                   jax.ShapeDtypeStruct((t // CHUNK, GLA_DV, GLA_HK), BF16), jax.ShapeDtypeStruct((t, d), F32)],
        scratch_shapes=[pltpu.VMEM((GLA_DV, GLA_HK), F32), pltpu.VMEM((tm, GLA_DK), BF16),
                        pltpu.VMEM((cpt * 8, GLA_DK), F32)],
        compiler_params=_cparams("arbitrary"), name="gla_seq_fwd")


def _gla_seq_bwd(dmix, o, z, states, gate_w, gate_b, head_g, dep=None):
    t = z.shape[0]
    tm = _row_tile(t, ROW_TILE_TARGET, CHUNK)
    cpt = tm // CHUNK
    nt = t // tm
    scale = GLA_HK ** -0.5

    def body(dmix_ref, o_ref, z_ref, st_ref, gw_ref, gb_ref, hg_ref, dz_ref, dgw_ref, dgb_ref, dhg_ref,
             dstate, dec_s, kdec_s, dkdec_s, do_s, e_s, dtot_s):
        @pl.when(pl.program_id(0) == 0)
        def _():
            dstate[...] = jnp.zeros_like(dstate)
            dgw_ref[...] = jnp.zeros_like(dgw_ref)
            dgb_ref[...] = jnp.zeros_like(dgb_ref)
            dhg_ref[...] = jnp.zeros_like(dhg_ref)

        cum3, tot3 = _chunk_decay(z_ref[:, R0:R0 + GATE_PAD], gw_ref, gb_ref, cpt)
        dec = jnp.exp(jnp.broadcast_to(tot3, cum3.shape) - cum3).reshape(tm, GLA_DK)
        dec_s[...] = dec
        kdec = z_ref[:, K0:K0 + GLA_DK].astype(F32) * dec
        kdec_s[...] = kdec
        e3 = jnp.exp(tot3)
        e_s[...] = jnp.broadcast_to(e3, (cpt, 8, GLA_DK)).reshape(cpt * 8, GLA_DK)
        dhg = jnp.zeros((1, GLA_HV), F32)
        for hd in range(GLA_HEADS):
            ks = slice(hd * GLA_HK, (hd + 1) * GLA_HK)
            vs = slice(hd * GLA_HV, (hd + 1) * GLA_HV)
            gcols = slice(G0 + hd * GLA_HV, G0 + (hd + 1) * GLA_HV)
            ov = o_ref[:, vs].astype(F32)
            gv = z_ref[:, gcols].astype(F32)
            dm = dmix_ref[:, vs].astype(F32)
            sg = jax.nn.sigmoid(gv)
            rr = lax.rsqrt(jnp.mean(ov * ov, axis=-1, keepdims=True) + EPS)
            xhat = ov * rr
            don = dm * (gv * sg)
            dz_ref[:, gcols] = (dm * (xhat * hg_ref[...]) * (sg * (1.0 + gv * (1.0 - sg)))).astype(BF16)
            dhg = dhg + jnp.sum(don * xhat, axis=0, keepdims=True)
            dxh = don * hg_ref[...]
            do = (rr * (dxh - xhat * jnp.mean(dxh * xhat, axis=-1, keepdims=True)) * scale).astype(BF16)
            do_s[:, vs] = do
            v3 = z_ref[:, V0 + hd * GLA_HV:V0 + (hd + 1) * GLA_HV].reshape(cpt, CHUNK, GLA_HV)
            kdb3 = kdec[:, ks].astype(BF16).reshape(cpt, CHUNK, GLA_HK)
            st3 = st_ref[:, vs, :].astype(F32) * e3[:, :, ks] + jnp.einsum("bcv,bck->bvk", v3, kdb3,
                                                                            preferred_element_type=F32)
            dq3 = jnp.einsum("bcv,bvk->bck", do.reshape(cpt, CHUNK, GLA_HV), st3.astype(BF16), preferred_element_type=F32)
            dz_ref[:, Q0 + hd * GLA_HK:Q0 + (hd + 1) * GLA_HK] = dq3.reshape(tm, GLA_HK).astype(BF16)
        dhg_ref[...] += dhg

        def chunk(cj, carry):
            ci = cpt - 1 - cj
            rows = pl.ds(pl.multiple_of(ci * CHUNK, CHUNK), CHUNK)
            erows = pl.ds(pl.multiple_of(ci * 8, 8), 8)
            e_all = e_s[erows, :][0:1, :]
            for hd in range(GLA_HEADS):
                ks = slice(hd * GLA_HK, (hd + 1) * GLA_HK)
                vs = slice(hd * GLA_HV, (hd + 1) * GLA_HV)
                e = e_all[:, ks]
                kdb = kdec_s[rows, ks].astype(BF16)
                v = z_ref[rows, V0 + hd * GLA_HV:V0 + (hd + 1) * GLA_HV]
                q = z_ref[rows, Q0 + hd * GLA_HK:Q0 + (hd + 1) * GLA_HK]
                do = do_s[rows, vs]
                st_prev = st_ref[ci, vs, :].astype(F32)
                dst = dstate[vs, :] + _dot_tn(do, q)
                dstb = dst.astype(BF16)
                dkdec_s[rows, ks] = _dot(v, dstb)
                dz_ref[rows, V0 + hd * GLA_HV:V0 + (hd + 1) * GLA_HV] = _dot_nt(kdb, dstb).astype(BF16)
                dtot = jnp.sum(dst * st_prev, axis=0, keepdims=True) * e
                dtot_s[erows, ks] = jnp.broadcast_to(dtot, (8, GLA_HK))
                dstate[vs, :] = dst * e
            return carry

        lax.fori_loop(0, cpt, chunk, 0, unroll=cpt)

        dkdec = dkdec_s[...]
        dz_ref[:, K0:K0 + GLA_DK] = (dkdec * dec_s[...]).astype(BF16)
        before = _chunk_sums(dkdec * kdec_s[...], cpt, True, 2)
        dtot3 = dtot_s[...].reshape(cpt, 8, GLA_DK)[:, 0:1, :]
        dlac = (jnp.broadcast_to(dtot3, before.shape) + before).reshape(tm, GLA_DK)
        pre = _dot(z_ref[:, R0:R0 + GATE_PAD], gw_ref[...]) + gb_ref[...]
        dpre = dlac * (1.0 / GATE_NORM) * (1.0 - jax.nn.sigmoid(pre))
        dpb = dpre.astype(BF16)
        dz_ref[:, R0:R0 + GATE_PAD] = _dot_nt(dpb, gw_ref[...]).astype(BF16)
        dgw_ref[...] += _dot_tn(z_ref[:, R0:R0 + GATE_PAD], dpb)
        dgb_ref[...] += jnp.sum(dpre, axis=0, keepdims=True)

    rev = lambda i: (nt - 1 - i, 0)
    return _call_after(
        dep, body, 7,
        [pl.BlockSpec((tm, GLA_DV), rev), pl.BlockSpec((tm, GLA_DV), rev), pl.BlockSpec((tm, GLA_IN_PAD), rev),
         pl.BlockSpec((cpt, GLA_DV, GLA_HK), lambda i: (nt - 1 - i, 0, 0)),
         pl.BlockSpec((GATE_PAD, GLA_DK), lambda i: (0, 0)), pl.BlockSpec((1, GLA_DK), lambda i: (0, 0)),
         pl.BlockSpec((1, GLA_HV), lambda i: (0, 0))],
        (dmix, o, z, states, gate_w, gate_b, head_g), grid=(nt,),
        out_specs=[pl.BlockSpec((tm, GLA_IN_PAD), rev), pl.BlockSpec((GATE_PAD, GLA_DK), lambda i: (0, 0)),
                   pl.BlockSpec((1, GLA_DK), lambda i: (0, 0)), pl.BlockSpec((1, GLA_HV), lambda i: (0, 0))],
        out_shape=[jax.ShapeDtypeStruct((t, GLA_IN_PAD), BF16), jax.ShapeDtypeStruct((GATE_PAD, GLA_DK), F32),
                   jax.ShapeDtypeStruct((1, GLA_DK), F32), jax.ShapeDtypeStruct((1, GLA_HV), F32)],
        scratch_shapes=[pltpu.VMEM((GLA_DV, GLA_HK), F32), pltpu.VMEM((tm, GLA_DK), F32), pltpu.VMEM((tm, GLA_DK), F32),
                        pltpu.VMEM((tm, GLA_DK), F32), pltpu.VMEM((tm, GLA_DV), BF16),
                        pltpu.VMEM((cpt * 8, GLA_DK), F32), pltpu.VMEM((cpt * 8, GLA_DK), F32)],
        compiler_params=_cparams("arbitrary"), name="gla_seq_bwd")


def _sum_slots(x, name):
    n, r, cdim = x.shape
    tr = _row_tile(r, 256, 8)

    def body(x_ref, o_ref):
        acc = x_ref[0].astype(F32)
        for j in range(1, n):
            acc = acc + x_ref[j].astype(F32)
        o_ref[...] = acc

    return pl.pallas_call(
        body, grid=(r // tr,),
        in_specs=[pl.BlockSpec((n, tr, cdim), lambda i: (0, i, 0))],
        out_specs=pl.BlockSpec((tr, cdim), lambda i: (i, 0)),
        out_shape=jax.ShapeDtypeStruct((r, cdim), F32),
        compiler_params=_cparams("parallel"), name=name)(x)


def _sum_own_and_slots(own, slots, dev_idx, name):
    _, _, r, cdim = own.shape
    n = slots.shape[0]
    tr = _row_tile(r, 256, 8)

    def body(s_ref, own_ref, *rest):
        acc = own_ref[...].astype(F32)
        for other in rest[:n - 1]:
            acc = acc + other[...].astype(F32)
        rest[n - 1][...] = acc

    def slot(dd):
        return pl.BlockSpec((None, tr, cdim), lambda i, s: ((s[0] + dd) % n, i, 0))

    mine = pl.BlockSpec((None, None, tr, cdim), lambda i, s: (s[0] // 2, s[0] % 2, i, 0))
    return pl.pallas_call(
        body,
        grid_spec=pltpu.PrefetchScalarGridSpec(
            num_scalar_prefetch=1, grid=(r // tr,), in_specs=[mine] + [slot(dd) for dd in range(1, n)],
            out_specs=pl.BlockSpec((tr, cdim), lambda i, s: (i, 0))),
        out_shape=jax.ShapeDtypeStruct((r, cdim), F32),
        compiler_params=_cparams("parallel"), name=name)(dev_idx, own, *([slots] * (n - 1)))


def _add2(a, b, name):
    r, cdim = a.shape
    tr = _row_tile(r, 256, 8)

    def body(a_ref, b_ref, o_ref):
        o_ref[...] = a_ref[...] + b_ref[...]

    spec = pl.BlockSpec((tr, cdim), lambda i: (i, 0))
    return pl.pallas_call(body, grid=(r // tr,), in_specs=[spec, spec], out_specs=spec,
                          out_shape=jax.ShapeDtypeStruct((r, cdim), F32),
                          compiler_params=_cparams("parallel"), name=name)(a, b)


def _adamw_half(w, gs, m, v, half_idx, prev, name, dep=None):
    nl, _, h, cdim = w.shape
    tr = _row_tile(h, 256, 8)
    nprev = 0 if prev is None else 4
    extra = [] if dep is None else [dep]

    def body(s_ref, w_ref, m_ref, v_ref, *rest):
        g_refs = rest[:nl]
        go_ref, d_ref, mo_ref, vo_ref = rest[nl + nprev + len(extra):]
        layer = pl.program_id(0)
        gv = g_refs[0][...]
        for j in range(1, nl):
            gv = jnp.where(layer == j, g_refs[j][...], gv)
        go_ref[...] = gv
        mn = ADAM_B1 * m_ref[...] + (1.0 - ADAM_B1) * gv
        vn = ADAM_B2 * v_ref[...] + (1.0 - ADAM_B2) * (gv * gv)
        m_hat = mn / (1.0 - ADAM_B1 ** ADAM_STEP)
        v_hat = vn / (1.0 - ADAM_B2 ** ADAM_STEP)
        d_ref[...] = -ADAM_LR * (m_hat / (jnp.sqrt(v_hat) + ADAM_EPS) + ADAM_WD * w_ref[...])
        mo_ref[...] = mn
        vo_ref[...] = vn

    half = pl.BlockSpec((None, None, tr, cdim), lambda l, i, s: (l, s[0], i, 0))

    def of_layer(j):
        return pl.BlockSpec((tr, cdim), lambda l, i, s: (jnp.where(l == j, i, 0), 0))

    shp = jax.ShapeDtypeStruct(w.shape, F32)
    return pl.pallas_call(
        body,
        grid_spec=pltpu.PrefetchScalarGridSpec(
            num_scalar_prefetch=1, grid=(nl, h // tr),
            in_specs=[half] * 3 + [of_layer(j) for j in range(nl)] + [ANY_SPEC] * (nprev + len(extra)),
            out_specs=[half] * 4),
        out_shape=[shp] * 4, input_output_aliases={4 + nl + k: k for k in range(nprev)},
        compiler_params=_cparams("arbitrary", "arbitrary"), name=name,
    )(half_idx, w, m, v, *gs, *([] if prev is None else prev), *extra)


def _adamw_many(ws, gs, ms, vs):
    n = len(ws)

    def body(*refs):
        for i in range(n):
            w_ref, g_ref, m_ref, v_ref = refs[i], refs[n + i], refs[2 * n + i], refs[3 * n + i]
            d_ref, mo_ref, vo_ref = refs[4 * n + i], refs[5 * n + i], refs[6 * n + i]
            gv = g_ref[...]
            mn = ADAM_B1 * m_ref[...] + (1.0 - ADAM_B1) * gv
            vn = ADAM_B2 * v_ref[...] + (1.0 - ADAM_B2) * (gv * gv)
            m_hat = mn / (1.0 - ADAM_B1 ** ADAM_STEP)
            v_hat = vn / (1.0 - ADAM_B2 ** ADAM_STEP)
            d_ref[...] = -ADAM_LR * (m_hat / (jnp.sqrt(v_hat) + ADAM_EPS) + ADAM_WD * w_ref[...])
            mo_ref[...] = mn
            vo_ref[...] = vn

    shapes = [jax.ShapeDtypeStruct(w.shape, F32) for w in ws]
    outs = pl.pallas_call(body, out_shape=shapes * 3, name="adamw_small")(*ws, *gs, *ms, *vs)
    return outs[:n], outs[n:2 * n], outs[2 * n:]


def _split_rows(a):
    return a.reshape(a.shape[0], 2, a.shape[1] // 2, a.shape[2])


def _place():
    x, y, c = lax.axis_index("x"), lax.axis_index("y"), lax.axis_index("c")
    chips = [(1 - x, y), (x, 1 - y), (1 - x, 1 - y)]
    return x, y, c, chips


def _remote(src, dst, send_sem, recv_sem, to):
    return pltpu.make_async_remote_copy(src_ref=src, dst_ref=dst, send_sem=send_sem, recv_sem=recv_sem,
                                        device_id=to, device_id_type=MESH)


def _plan_gather(n_halved):
    def plan(src_refs, land_refs):
        x, y, c, chips = _place()
        me = 2 * x + y
        copies = []
        for k, (src, land) in enumerate(zip(src_refs, land_refs)):
            for (px, py) in chips:
                frm = 2 * px + py
                if k < n_halved:
                    copies.append((src.at[c], land.at[me, c], (px, py, c), land.at[frm, c]))
                else:
                    copies.append((src, land.at[me], (px, py, c), land.at[frm]))
        return copies
    return plan


def _plan_share(src_refs, land_refs):
    x, y, c, chips = _place()
    me = 2 * x + y
    sib = (x, y, 1 - c)
    copies = []
    for src, land in zip(src_refs, land_refs):
        copies.append((src, land.at[me], sib, land.at[me]))
        for (px, py) in chips:
            frm = 2 * px + py
            copies.append((land.at[frm, c], land.at[frm, c], sib, land.at[frm, 1 - c]))
    return copies


def _plan_scatter(n_parts):
    def plan(src_refs, land_refs):
        x, y, c, chips = _place()
        me = 2 * x + y
        copies = []
        for k, (src, land) in enumerate(zip(src_refs, land_refs)):
            for (px, py) in chips:
                to = 2 * px + py
                copies.append((src.at[to] if k < n_parts else src, land.at[me], (px, py, c), land.at[to]))
        return copies
    return plan


N_DEVICES = 8
OTHER_DEVICES = [(dx, dy, dc) for dx in (0, 1) for dy in (0, 1) for dc in (0, 1) if dx or dy or dc]


def _plan_scatter_all(src_refs, land_refs):
    x, y, c, _ = _place()
    me = 4 * x + 2 * y + c
    copies = []
    for src, land in zip(src_refs, land_refs):
        for dx, dy, dc in OTHER_DEVICES:
            px, py, pc = (1 - x if dx else x), (1 - y if dy else y), (1 - c if dc else c)
            copies.append((src.at[2 * px + py, pc], land.at[me], (px, py, pc), land.at[4 * px + 2 * py + pc]))
    return copies


def _plan_exchange(n_split):
    def plan(src_refs, land_refs):
        x, y, c, _ = _place()
        sib = (x, y, 1 - c)
        return [(src.at[:, 1 - c] if k < n_split else src, land, sib, land)
                for k, (src, land) in enumerate(zip(src_refs, land_refs))]
    return plan


def _hbm(a):
    return pltpu.HBM(a.shape, a.dtype)


def _start_copies(name, srcs, lands, plan, ncopy, dep=None):
    ns, nl = len(srcs), len(lands)
    nin = ns + nl + (0 if dep is None else 1)

    def body(*refs):
        send_sems, recv_sems, token = refs[nin], refs[nin + 1], refs[-1]
        for k, (src, dst, dev, _) in enumerate(plan(refs[:ns], refs[ns:ns + nl])):
            _remote(src, dst, send_sems.at[k], recv_sems.at[k], dev).start()
        token[...] = jnp.zeros_like(token)

    args = [pltpu.with_memory_space_constraint(a, pltpu.HBM) for a in list(srcs) + list(lands)]
    outs = pl.pallas_call(
        body, name=name,
        out_shape=(pltpu.SemaphoreType.DMA((ncopy,)), pltpu.SemaphoreType.DMA((ncopy,)),
                   *[_hbm(a) for a in list(srcs) + list(lands)], jax.ShapeDtypeStruct((8, 128), F32)),
        in_specs=[HBM_SPEC] * (ns + nl) + ([] if dep is None else [ANY_SPEC]),
        out_specs=(SEM_SPEC, SEM_SPEC, *([HBM_SPEC] * (ns + nl)), pl.BlockSpec(memory_space=pltpu.VMEM)),
        input_output_aliases={i: 2 + i for i in range(ns + nl)},
        compiler_params=pltpu.CompilerParams(has_side_effects=SIDE_EFFECT),
    )(*args, *([] if dep is None else [dep]))
    return outs[0], outs[1], list(outs[2:2 + ns]), list(outs[2 + ns:2 + ns + nl]), outs[-1]


def _wait_copies(name, started, plan, after, sem_offset=0):
    send_sems, recv_sems, srcs, lands, _ = started
    ns, nl = len(srcs), len(lands)
    after = list(after) if isinstance(after, (list, tuple)) else [after]

    def body(*refs):
        send_ref, recv_ref = refs[ns + nl], refs[ns + nl + 1]
        for k, (src, _, dev, mine) in enumerate(plan(refs[:ns], refs[ns:ns + nl])):
            copy = _remote(src, mine, send_ref.at[sem_offset + k], recv_ref.at[sem_offset + k], dev)
            copy.wait_send()
            copy.wait_recv()

    outs = pl.pallas_call(
        body, name=name, out_shape=tuple(_hbm(a) for a in srcs + lands),
        in_specs=[HBM_SPEC] * (ns + nl) + [SEM_SPEC, SEM_SPEC] + [ANY_SPEC] * len(after),
        out_specs=tuple([HBM_SPEC] * (ns + nl)),
        input_output_aliases={i: i for i in range(ns + nl)},
        compiler_params=pltpu.CompilerParams(has_side_effects=SIDE_EFFECT),
    )(*srcs, *lands, send_sems, recv_sems, *after)
    return list(outs[:ns]), list(outs[ns:])


def _share_with_sibling(name, srcs, lands):
    n = len(srcs)

    def body(*refs):
        src_refs, land_refs, out_refs = refs[:n], refs[n:2 * n], refs[2 * n:3 * n]
        send_sem, recv_sem = refs[3 * n:]
        x, y, c, chips = _place()
        me = 2 * x + y
        sib = (x, y, 1 - c)
        sends, recvs = [], []
        for k in range(n):
            sems = (send_sem.at[4 * k], recv_sem.at[4 * k])
            sends.append(_remote(src_refs[k], out_refs[k].at[me], *sems, sib))
            recvs.append(_remote(src_refs[k], out_refs[k].at[me], *sems, sib))
            for j, (px, py) in enumerate(chips):
                frm = 2 * px + py
                sems = (send_sem.at[4 * k + 1 + j], recv_sem.at[4 * k + 1 + j])
                sends.append(_remote(land_refs[k].at[frm, c], out_refs[k].at[frm, c], *sems, sib))
                recvs.append(_remote(land_refs[k].at[frm, c], out_refs[k].at[frm, 1 - c], *sems, sib))
        for cp in sends:
            cp.start()
        for cp in recvs:
            cp.wait_recv()
        for cp in sends:
            cp.wait_send()

    return pl.pallas_call(
        body, name=name, in_specs=[HBM_SPEC] * (2 * n), out_specs=[HBM_SPEC] * n,
        out_shape=[jax.ShapeDtypeStruct(a.shape, a.dtype) for a in lands],
        input_output_aliases={n + k: k for k in range(n)},
        scratch_shapes=[pltpu.SemaphoreType.DMA((4 * n,)), pltpu.SemaphoreType.DMA((4 * n,))],
    )(*srcs, *lands)


def _pack(arrs):
    flat = jnp.concatenate([a.reshape(-1).astype(F32) for a in arrs])
    n = flat.shape[0]
    rows = -(-n // PACK_WIDTH)
    rows = -(-rows // 8) * 8
    return jnp.pad(flat, (0, rows * PACK_WIDTH - n)).reshape(rows, PACK_WIDTH)


def _unpack(buf, shapes):
    flat = buf.reshape(-1)
    out, off = [], 0
    for shp in shapes:
        n = 1
        for s in shp:
            n *= s
        out.append(flat[off:off + n].reshape(shp))
        off += n
    return out


def _unshard_cols(stacked):
    moved = jnp.moveaxis(stacked, 0, -2)
    return moved.reshape(moved.shape[:-2] + (moved.shape[-2] * moved.shape[-1],))


def _take_cols(blocks, start, width):
    bw = blocks.shape[2]
    pieces, lo = [], start
    while lo < start + width:
        b = lo // bw
        hi = min(start + width, (b + 1) * bw)
        pieces.append(blocks[b][:, lo - b * bw:hi - b * bw])
        lo = hi
    return jnp.concatenate(pieces, axis=1)


def _col_shard(full, s, width):
    return lax.dynamic_slice_in_dim(full, s * width, width, axis=full.ndim - 1)


def kernel(x, meta_tokens, mix_norm_g, ffn_norm_g, ffn_w1, ffn_w2, cp_w_in, cp_conv_w, cp_conv_b, cp_ln_g, cp_ln_b, cp_pool_w, cp_pool_scale, cp_w_out, gla_w_in, gla_gate_w2, gla_gate_b, gla_head_g, gla_w_out, final_norm_g, loss_target, m_meta_tokens, m_mix_norm_g, m_ffn_norm_g, m_ffn_w1, m_ffn_w2, m_cp_w_in, m_cp_conv_w, m_cp_conv_b, m_cp_ln_g, m_cp_ln_b, m_cp_pool_w, m_cp_pool_scale, m_cp_w_out, m_gla_w_in, m_gla_gate_w2, m_gla_gate_b, m_gla_head_g, m_gla_w_out, m_final_norm_g, v_meta_tokens, v_mix_norm_g, v_ffn_norm_g, v_ffn_w1, v_ffn_w2, v_cp_w_in, v_cp_conv_w, v_cp_conv_b, v_cp_ln_g, v_cp_ln_b, v_cp_pool_w, v_cp_pool_scale, v_cp_w_out, v_gla_w_in, v_gla_gate_w2, v_gla_gate_b, v_gla_head_g, v_gla_w_out, v_final_norm_g):
    d = D_MODEL
    chip = 2 * lax.axis_index("x") + lax.axis_index("y")
    core = lax.axis_index("c")
    seq = x.shape[1]
    t = seq + CHUNK

    sharded_small = [meta_tokens, cp_conv_w, gla_gate_w2, gla_gate_b, gla_head_g]

    def halves(w, token=None):
        if token is not None:
            w = w + token[0, 0]
        return w.astype(BF16).reshape(2, w.shape[0] // 2, w.shape[1])

    def unhalve(g):
        return g.reshape(N_CHIPS, 2 * g.shape[2], g.shape[3])

    def gather_group(srcs, whole=()):
        lands = [lax.empty((N_CHIPS,) + s.shape, s.dtype) for s in srcs]
        for a in whole:
            lands.append(lax.dynamic_update_slice(jnp.zeros((N_CHIPS,) + a.shape, a.dtype), a[None], (chip,) + (0,) * a.ndim))
        return list(srcs) + list(whole), lands, _plan_gather(len(srcs)), len(srcs)

    def start_groups(name, groups, dep):
        bounds, all_srcs, all_lands = [], [], []
        for srcs, lands, _, _ in groups:
            bounds.append((len(all_srcs), len(all_srcs) + len(srcs)))
            all_srcs += srcs
            all_lands += lands

        def plan_all(src_refs, land_refs):
            return [cp for (lo, hi), group in zip(bounds, groups) for cp in group[2](src_refs[lo:hi], land_refs[lo:hi])]

        started = _start_copies(name, all_srcs, all_lands, plan_all, 3 * len(all_srcs), dep)
        return [(started, bound, group[2], group[3]) for bound, group in zip(bounds, groups)]

    def arrived(name, gather, after):
        started, (lo, hi), plan, n = gather
        mine = (started[0], started[1], started[2][lo:hi], started[3][lo:hi], started[4])
        srcs, lands = _wait_copies(name + "_wait", mine, plan, after, sem_offset=3 * lo)
        return srcs[:n], lands[:n], lands[n:]

    (cp_gather,) = start_groups("gather_cp_start", [gather_group([halves(cp_w_in[0]), halves(cp_w_out[0])],
                                                                 [_pack(sharded_small)])], None)
    tok = cp_gather[0][-1]
    ffn0_gather, gla_gather, ffn1_gather = start_groups(
        "gather_start", [gather_group([halves(ffn_w1[0], tok), halves(ffn_w2[0], tok)]),
                         gather_group([halves(gla_w_in[0], tok), halves(gla_w_out[0], tok)]),
                         gather_group([halves(ffn_w1[1], tok), halves(ffn_w2[1], tok)])], tok)
    h0_rows = jnp.concatenate([jnp.zeros((CHUNK, d), F32) + ffn0_gather[0][-1][0, 0], x[0]], axis=0)
    cp_srcs, cp_lands, (small_g,) = arrived("gather_cp", cp_gather, h0_rows)
    cpin_g, cpout_g = [unhalve(g) for g in _share_with_sibling("gather_cp_share", cp_srcs, cp_lands)]
    per_chip = [_unpack(small_g[j], [a.shape for a in sharded_small]) for j in range(N_CHIPS)]
    meta_f, conv_w_f, gate_w_f, gate_b_f, head_g_f = [
        jnp.concatenate([per_chip[j][i] for j in range(N_CHIPS)], axis=-1) for i in range(len(sharded_small))]
    conv_w_f, gate_w_f = conv_w_f[0], gate_w_f[0]
    w_cp_in = _unshard_cols(cpin_g)
    w_cp_out = cpout_g.reshape(CONV_DIM + POOL_DIM, d)
    gate_w_pad = jnp.pad(gate_w_f, ((0, GATE_PAD - GATE_RANK), (0, 0))).astype(BF16)
    row = lambda a: a.reshape(1, -1)
    c_idx = core.reshape(1).astype(jnp.int32)

    h0 = lax.dynamic_update_slice(h0_rows, meta_f, (PAD_ROWS, 0))
    z0, u0 = _norm_matmul(h0, row(mix_norm_g[0]), w_cp_in, 512, "cp_in_proj")
    c0, pm0, mix0 = _cp_seq_fwd(z0, conv_w_f, cp_conv_b, cp_ln_g, cp_ln_b, cp_pool_w[0], cp_pool_scale)
    ffn0_srcs, ffn0_lands, _ = arrived("gather_ffn0", ffn0_gather, mix0)
    ffn0_share = _start_copies("gather_ffn0_share_start", ffn0_srcs, ffn0_lands, _plan_share, 4 * len(ffn0_srcs))
    h1 = _matmul_residual(mix0, w_cp_out, h0, "cp_out_proj", dep=ffn0_share[-1])
    w1g0, w2g0 = [unhalve(g) for g in _wait_copies("gather_ffn0_share_wait", ffn0_share, _plan_share, h1)[1]]
    h2, hp0, uf0 = _ffn_fwd(h1, row(ffn_norm_g[0]), w1g0, w2g0, "ffn0_fwd")
    gla_srcs, gla_lands, _ = arrived("gather_gla", gla_gather, h2)
    glain_g, glaout_g = [unhalve(g) for g in _share_with_sibling("gather_gla_share", gla_srcs, gla_lands)]
    w_gla_in = jnp.concatenate([glain_g[j] for j in range(N_CHIPS)] + [jnp.zeros((d, GLA_IN_PAD - GLA_IN), BF16)], axis=1)
    w_gla_out = glaout_g.reshape(GLA_DV, d)
    z1, u2 = _norm_matmul(h2, row(mix_norm_g[1]), w_gla_in, GLA_COLS, "gla_in_proj")
    ffn1_srcs, ffn1_lands, _ = arrived("gather_ffn1", ffn1_gather, z1)
    ffn1_share = _start_copies("gather_ffn1_share_start", ffn1_srcs, ffn1_lands, _plan_share, 4 * len(ffn1_srcs))
    o1, mix1, states, h3 = _gla_seq_fwd(z1, gate_w_pad, gate_b_f, head_g_f, h2, w_gla_out, dep=ffn1_share[-1])
    w1g1, w2g1 = [unhalve(g) for g in _wait_copies("gather_ffn1_share_wait", ffn1_share, _plan_share, h3)[1]]
    h4, hp1, uf1 = _ffn_fwd(h3, row(ffn_norm_g[1]), w1g1, w2g1, "ffn1_fwd")

    dev_idx = (2 * chip + core).reshape(1).astype(jnp.int32)

    def start_reduce(name, grads):
        srcs = [_split_rows(g) for g in grads]
        lands = [lax.empty((N_DEVICES,) + s.shape[2:], s.dtype) for s in srcs]
        return _start_copies(name + "_scatter_start", srcs, lands, _plan_scatter_all, len(OTHER_DEVICES) * len(srcs))

    def finish_reduce(name, started, after):
        srcs, lands = _wait_copies(name + "_scatter_wait", started, _plan_scatter_all, after)
        return [_sum_own_and_slots(s, l, dev_idx, "%s_slot_sum_%d" % (name, k)) for k, (s, l) in enumerate(zip(srcs, lands))]

    dh4, d_final_g, loss_part = _loss_bwd(h4, row(final_norm_g), loss_target[0])

    dh3, dhp1, d_ffn_g1 = _ffn_bwd_data(dh4, h3, row(ffn_norm_g[1]), hp1, w1g1, w2g1, "ffn1_bwd")
    dw1_1 = _wgrad(uf1, dhp1, N_CHIPS, d, d, False, True, False, "ffn1_dw1", rows=WGRAD_ROWS_BF16)
    dw2_1 = _wgrad(hp1, dh4, N_CHIPS, d, d, True, False, True, "ffn1_dw2")
    ffn1_reduce = start_reduce("ffn1", [dw1_1, dw2_1])

    dmix1 = _dgrad(dh3, w_gla_out, "gla_out_dgrad", dep=ffn1_reduce[-1])
    dw_gla_out = _wgrad(mix1, dh3, 1, GLA_DV, d, False, False, False, "gla_out_dw")
    dz1, d_gate_w, d_gate_b, d_head_g = _gla_seq_bwd(dmix1, o1, z1, states, gate_w_pad, gate_b_f, head_g_f)
    dh2, d_mix_g1 = _dgrad_norm_bwd(dz1, w_gla_in, h2, row(mix_norm_g[1]), dh3, GLA_COLS, "gla_in_dgrad")
    dw_gla_in = _wgrad(u2, dz1, GLA_IN_PAD // 640, d, 640, False, True, False, "gla_in_dw", rows=WGRAD_ROWS_BF16)
    gla_in_shards = jnp.stack([_take_cols(dw_gla_in, j * (GLA_IN // N_CHIPS), GLA_IN // N_CHIPS) for j in range(N_CHIPS)])
    gla_reduce = start_reduce("gla", [gla_in_shards, dw_gla_out.reshape(N_CHIPS, -1, d)])

    dh1, dhp0, d_ffn_g0 = _ffn_bwd_data(dh2, h1, row(ffn_norm_g[0]), hp0, w1g0, w2g0, "ffn0_bwd", dep=gla_reduce[-1])
    dw1_0 = _wgrad(uf0, dhp0, N_CHIPS, d, d, False, True, False, "ffn0_dw1", rows=WGRAD_ROWS_BF16)
    dw2_0 = _wgrad(hp0, dh2, N_CHIPS, d, d, True, False, True, "ffn0_dw2")
    ffn0_reduce = start_reduce("ffn0", [dw1_0, dw2_0])

    dw_cp_out = _wgrad(mix0, dh1, 1, CONV_DIM + POOL_DIM, d, False, False, False, "cp_out_dw", dep=ffn0_reduce[-1])
    dz0, d_conv_w, d_cp_vec, d_pool_w = _cp_seq_bwd(dh1, w_cp_out, z0, c0, pm0, conv_w_f, cp_ln_g, cp_ln_b, cp_pool_w[0],
                                                    cp_pool_scale, dep=ffn0_reduce[-1])
    grad_x, dh0_head, d_mix_g0 = _dgrad_norm_bwd_input(dz0, w_cp_in, h0, row(mix_norm_g[0]), dh1, 512, "cp_in_dgrad")
    grad_x = grad_x[None]
    dw_cp_in = _wgrad(u0, dz0, 1, d, CP_IN, False, False, False, "cp_in_dw")
    dw_cp_in = jnp.stack([_take_cols(dw_cp_in, j * (CP_IN // N_CHIPS), CP_IN // N_CHIPS) for j in range(N_CHIPS)])

    cp_reduce = start_reduce("cp", [dw_cp_in, dw_cp_out.reshape(N_CHIPS, -1, d)])
    small_full = [dh0_head[PAD_ROWS:CHUNK],jnp.concatenate([d_mix_g0, d_mix_g1], axis=0),
                  jnp.concatenate([d_ffn_g0, d_ffn_g1], axis=0), d_conv_w[:CONV_WIDTH][None],
                  d_cp_vec[0:1], d_cp_vec[1:2], d_cp_vec[2:3], d_pool_w[None], d_cp_vec[3:4],
                  d_gate_w[:GATE_RANK][None], d_gate_b, d_head_g, d_final_g[0], loss_part[0, 0:1]]
    small_mine = _pack(small_full)
    whole = _plan_exchange(0)
    small_exchange = _start_copies("small_exchange_start", [small_mine], [lax.empty(small_mine.shape, F32)], whole, 1,
                                   dep=cp_reduce[-1])
    red_ffn1 = finish_reduce("ffn1", ffn1_reduce, small_exchange[-1])
    red_gla = finish_reduce("gla", gla_reduce, small_exchange[-1])
    (small_sent,), (small_recv,) = _wait_copies("small_exchange_wait", small_exchange, whole, [red_ffn1[1], red_gla[1]])
    small_chip = _add2(small_sent, small_recv, "chip_sum_small")
    small_slots = lax.dynamic_update_slice(jnp.zeros((N_CHIPS,) + small_chip.shape, F32), small_chip[None], (chip, 0, 0))
    small_reduce = _start_copies("small_scatter_start", [small_chip], [small_slots], _plan_scatter(0), 3)

    big = {"w1": (ffn_w1, m_ffn_w1, v_ffn_w1), "w2": (ffn_w2, m_ffn_w2, v_ffn_w2),
           "cp_in": (cp_w_in, m_cp_w_in, v_cp_w_in), "cp_out": (cp_w_out, m_cp_w_out, v_cp_w_out),
           "gla_in": (gla_w_in, m_gla_w_in, v_gla_w_in), "gla_out": (gla_w_out, m_gla_w_out, v_gla_w_out)}
    other_idx = (1 - core).reshape(1).astype(jnp.int32)

    def adamw_by_halves(tag, reduced, dep=None):
        flat = [r for n in reduced for r in reduced[n]]
        join_plan = _plan_exchange(0)
        join = _start_copies(tag + "_join_start", flat, [lax.empty(r.shape, F32) for r in flat], join_plan, len(flat),
                             dep=dep)
        views = {n: [_split_rows(a) for a in big[n]] for n in reduced}
        own, k = {}, 0
        for n in reduced:
            mine = join[2][k:k + len(reduced[n])]
            k += len(reduced[n])
            own[n] = _adamw_half(views[n][0], mine, views[n][1], views[n][2], c_idx, None, "adamw_%s_own" % n)
        _, arrived_halves = _wait_copies(tag + "_join_wait", join, join_plan, [own[n][1] for n in reduced])
        outs, k = {}, 0
        for n in reduced:
            theirs = arrived_halves[k:k + len(reduced[n])]
            k += len(reduced[n])
            res = _adamw_half(views[n][0], theirs, views[n][1], views[n][2], other_idx, own[n], "adamw_%s_sibling" % n)
            outs[n] = [o.reshape(big[n][0].shape) for o in res]
        return outs

    big_out = adamw_by_halves("gla", {"gla_in": [red_gla[0]], "gla_out": [red_gla[1]]}, dep=small_reduce[-1])
    red_ffn0 = finish_reduce("ffn0", ffn0_reduce, big_out["gla_out"][1])
    big_out.update(adamw_by_halves("ffn", {"w1": [red_ffn0[0], red_ffn1[0]], "w2": [red_ffn0[1], red_ffn1[1]]}))
    red_cp = finish_reduce("cp", cp_reduce, big_out["w2"][1])
    _, (small_landed,) = _wait_copies("small_scatter_wait", small_reduce, _plan_scatter(0), big_out["w2"][1])
    small_red = _sum_slots(small_landed, "slot_sum_small")
    big_out.update(adamw_by_halves("cp", {"cp_in": [red_cp[0]], "cp_out": [red_cp[1]]}))

    (g_meta, g_mix, g_ffn, g_conv_w, g_conv_b, g_ln_g, g_ln_b, g_pool_w, g_pool_scale, g_gate_w, g_gate_b, g_head,
     g_final, loss_sum) = _unpack(small_red, [a.shape for a in small_full])
    g_meta = _col_shard(g_meta, chip, meta_tokens.shape[-1])
    g_conv_w = _col_shard(g_conv_w, chip, cp_conv_w.shape[-1])
    g_gate_w = _col_shard(g_gate_w, chip, gla_gate_w2.shape[-1])
    g_gate_b = _col_shard(g_gate_b, chip, gla_gate_b.shape[-1])
    g_head = _col_shard(g_head, chip, gla_head_g.shape[-1])
    small_w = [meta_tokens, mix_norm_g, ffn_norm_g, cp_conv_w, cp_conv_b, cp_ln_g, cp_ln_b, cp_pool_w, cp_pool_scale,
               gla_gate_w2, gla_gate_b, gla_head_g, final_norm_g]
    small_m = [m_meta_tokens, m_mix_norm_g, m_ffn_norm_g, m_cp_conv_w, m_cp_conv_b, m_cp_ln_g, m_cp_ln_b, m_cp_pool_w,
               m_cp_pool_scale, m_gla_gate_w2, m_gla_gate_b, m_gla_head_g, m_final_norm_g]
    small_v = [v_meta_tokens, v_mix_norm_g, v_ffn_norm_g, v_cp_conv_w, v_cp_conv_b, v_cp_ln_g, v_cp_ln_b, v_cp_pool_w,
               v_cp_pool_scale, v_gla_gate_w2, v_gla_gate_b, v_gla_head_g, v_final_norm_g]
    small_g = [g_meta, g_mix, g_ffn, g_conv_w, g_conv_b, g_ln_g, g_ln_b, g_pool_w, g_pool_scale, g_gate_w, g_gate_b,
               g_head, g_final]
    shapes = [w.shape for w in small_w]
    small_g = [g.reshape(s) for g, s in zip(small_g, shapes)]
    at_least_2d = lambda arrs: [a.reshape(1, -1) if a.ndim == 1 else a for a in arrs]
    s_delta, s_m, s_v = _adamw_many(at_least_2d(small_w), at_least_2d(small_g), at_least_2d(small_m), at_least_2d(small_v))
    s_delta, s_m, s_v = [[a.reshape(s) for a, s in zip(group, shapes)] for group in (s_delta, s_m, s_v)]

    order = ["meta", "mix", "ffn", "w1", "w2", "cp_in", "conv_w", "conv_b", "ln_g", "ln_b", "pool_w", "pool_scale",
             "cp_out", "gla_in", "gate_w", "gate_b", "head", "gla_out", "final"]
    small_names = ["meta", "mix", "ffn", "conv_w", "conv_b", "ln_g", "ln_b", "pool_w", "pool_scale", "gate_w", "gate_b",
                   "head", "final"]
    big_names = ["w1", "w2", "cp_in", "cp_out", "gla_in", "gla_out"]
    table = {n: (small_g[i], s_delta[i], s_m[i], s_v[i]) for i, n in enumerate(small_names)}
    table.update({n: tuple(big_out[n]) for n in big_names})
    loss = loss_sum.reshape(())
    return (loss, grad_x, *[table[n][0] for n in order], *[table[n][1] for n in order],
            *[table[n][2] for n in order], *[table[n][3] for n in order])
```

```python
import functools

import jax
import jax.numpy as jnp
from jax import lax
from jax.experimental import pallas as pl
from jax.experimental.pallas import tpu as pltpu

F32 = jnp.float32
BF16 = jnp.bfloat16

D_MODEL = 1024
N_META = 16
CHUNK = 64
PAD_ROWS = CHUNK - N_META
EPS = 1e-5
CONV_DIM = 512
CONV_WIDTH = 31
CONV_HALO = 32
POOL_DIM = 512
POOL_WINDOWS = (2, 4, 8, 16)
POOL_GROUP = 128
POOL_HALO = 16
CP_IN = 2 * CONV_DIM + POOL_DIM
GLA_HEADS = 4
GLA_DK = 512
GLA_DV = 1024
GLA_HK = GLA_DK // GLA_HEADS
GLA_HV = GLA_DV // GLA_HEADS
GATE_RANK = 16
GATE_PAD = 128
GATE_NORM = 16.0
GLA_IN = 2 * GLA_DK + 2 * GLA_DV + GATE_RANK
GLA_IN_PAD = 2 * GLA_DK + 2 * GLA_DV + GATE_PAD
GLA_COLS = 1280
N_CHIPS = 4
ADAM_LR = 0.001
ADAM_B1 = 0.9
ADAM_B2 = 0.999
ADAM_EPS = 1e-08
ADAM_WD = 0.01
ADAM_STEP = 10

VMEM_LIMIT_BYTES = 56 * 1024 * 1024
ROW_TILE_TARGET = 832
TOKEN_TILE_TARGET = 1040
PACK_WIDTH = 1024
MESH = pl.DeviceIdType.MESH
HBM_SPEC = pl.BlockSpec(memory_space=pltpu.HBM)
ANY_SPEC = pl.BlockSpec(memory_space=pl.ANY)
SEM_SPEC = pl.BlockSpec(memory_space=pltpu.SEMAPHORE)
SIDE_EFFECT = pltpu.SideEffectType.DATAFLOW_SIDE_EFFECTING


def _cparams(*sem):
    return pltpu.CompilerParams(dimension_semantics=sem, vmem_limit_bytes=VMEM_LIMIT_BYTES)


def _row_tile(t, target, mult):
    best = mult
    for cand in range(mult, min(t, target) + 1, mult):
        if t % cand == 0:
            best = cand
    assert t % best == 0, (t, best)
    return best


def _rms(h, g):
    return h * lax.rsqrt(jnp.mean(h * h, axis=-1, keepdims=True) + EPS) * g


def _rms_bwd(h, g, du):
    r = lax.rsqrt(jnp.mean(h * h, axis=-1, keepdims=True) + EPS)
    xhat = h * r
    dxh = du * g
    dh = r * (dxh - xhat * jnp.mean(dxh * xhat, axis=-1, keepdims=True))
    return dh, du * xhat


def _valid_rows(i, tm):
    row = i * tm + lax.broadcasted_iota(jnp.int32, (tm, 1), 0)
    return row >= PAD_ROWS


def _dot(a, b):
    return jnp.dot(a, b, preferred_element_type=F32)


def _dot_nt(a, b):
    return lax.dot_general(a, b, (((1,), (1,)), ((), ())), preferred_element_type=F32)


def _dot_tn(a, b):
    return lax.dot_general(a, b, (((0,), (0,)), ((), ())), preferred_element_type=F32)


def _accumulate(ref, val, first):
    @pl.when(first)
    def _():
        ref[...] = val

    @pl.when(jnp.logical_not(first))
    def _():
        ref[...] += val


def _call_after(dep, body, n_in, in_specs, args, **kw):
    if dep is None:
        return pl.pallas_call(body, in_specs=in_specs, **kw)(*args)

    def with_dep(*refs):
        body(*refs[:n_in], *refs[n_in + 1:])

    return pl.pallas_call(with_dep, in_specs=list(in_specs) + [ANY_SPEC], **kw)(*args, dep)


def _norm_matmul(h, g, w, nc, name, dep=None):
    t, d = h.shape
    n = w.shape[1]
    tm = _row_tile(t, TOKEN_TILE_TARGET, 16)

    def body(h_ref, g_ref, w_ref, z_ref, u_ref):
        u = _rms(h_ref[...], g_ref[...]).astype(BF16)
        u_ref[...] = u
        for n0 in range(0, n, nc):
            n1 = min(n0 + nc, n)
            z_ref[:, n0:n1] = _dot(u, w_ref[:, n0:n1]).astype(BF16)

    return _call_after(
        dep, body, 3,
        [pl.BlockSpec((tm, d), lambda i: (i, 0)), pl.BlockSpec((1, d), lambda i: (0, 0)),
         pl.BlockSpec((d, n), lambda i: (0, 0))], (h, g, w), grid=(t // tm,),
        out_specs=[pl.BlockSpec((tm, n), lambda i: (i, 0)), pl.BlockSpec((tm, d), lambda i: (i, 0))],
        out_shape=[jax.ShapeDtypeStruct((t, n), BF16), jax.ShapeDtypeStruct((t, d), BF16)],
        compiler_params=_cparams("parallel"), name=name)


def _matmul_residual(a, w, h, name, dep=None):
    t, k = a.shape
    d = w.shape[1]
    tm = _row_tile(t, TOKEN_TILE_TARGET, 16)

    def body(a_ref, w_ref, h_ref, o_ref):
        o_ref[...] = h_ref[...] + _dot(a_ref[...], w_ref[...])

    return _call_after(
        dep, body, 3,
        [pl.BlockSpec((tm, k), lambda i: (i, 0)), pl.BlockSpec((k, d), lambda i: (0, 0)),
         pl.BlockSpec((tm, d), lambda i: (i, 0))], (a, w, h), grid=(t // tm,),
        out_specs=pl.BlockSpec((tm, d), lambda i: (i, 0)),
        out_shape=jax.ShapeDtypeStruct((t, d), F32),
        compiler_params=_cparams("parallel"), name=name)


def _ffn_fwd(h, g, w1g, w2g, name):
    t, d = h.shape
    ns, ffs = w1g.shape[0], w1g.shape[2]
    tm = _row_tile(t, TOKEN_TILE_TARGET, 16)

    def body(h_ref, g_ref, w1_ref, w2_ref, ho_ref, hp_ref, u_ref, acc_ref):
        s = pl.program_id(1)

        @pl.when(s == 0)
        def _():
            u_ref[...] = _rms(h_ref[...], g_ref[...]).astype(BF16)

        hp = _dot(u_ref[...], w1_ref[...])
        hp_ref[...] = hp.astype(BF16)
        a = jnp.maximum(hp, 0.0)
        _accumulate(acc_ref, _dot((a * a).astype(BF16), w2_ref[...]), s == 0)

        @pl.when(s == ns - 1)
        def _():
            ho_ref[...] = h_ref[...] + acc_ref[...]

    return pl.pallas_call(
        body, grid=(t // tm, ns),
        in_specs=[pl.BlockSpec((tm, d), lambda i, s: (i, 0)), pl.BlockSpec((1, d), lambda i, s: (0, 0)),
                  pl.BlockSpec((None, d, ffs), lambda i, s: (s, 0, 0)),
                  pl.BlockSpec((None, ffs, d), lambda i, s: (s, 0, 0))],
        out_specs=[pl.BlockSpec((tm, d), lambda i, s: (i, 0)), pl.BlockSpec((tm, ffs), lambda i, s: (i, s)),
                   pl.BlockSpec((tm, d), lambda i, s: (i, 0))],
        out_shape=[jax.ShapeDtypeStruct((t, d), F32), jax.ShapeDtypeStruct((t, ns * ffs), BF16),
                   jax.ShapeDtypeStruct((t, d), BF16)],
        scratch_shapes=[pltpu.VMEM((tm, d), F32)],
        compiler_params=_cparams("parallel", "arbitrary"), name=name)(h, g, w1g, w2g)


def _ffn_bwd_data(dh, h, g, hp, w1g, w2g, name, dep=None):
    t, d = h.shape
    ns, ffs = w1g.shape[0], w1g.shape[2]
    tm = _row_tile(t, ROW_TILE_TARGET, CHUNK)

    def body(dh_ref, h_ref, g_ref, hp_ref, w1_ref, w2_ref, dhi_ref, dhp_ref, dg_ref, acc_ref):
        i, s = pl.program_id(0), pl.program_id(1)
        da = _dot_nt(dh_ref[...].astype(BF16), w2_ref[...])
        dhp = (da * (2.0 * jnp.maximum(hp_ref[...].astype(F32), 0.0))).astype(BF16)
        dhp_ref[...] = dhp
        _accumulate(acc_ref, _dot_nt(dhp, w1_ref[...]), s == 0)

        @pl.when(s == ns - 1)
        def _():
            dhn, dgr = _rms_bwd(h_ref[...], g_ref[...], acc_ref[...])
            dhi_ref[...] = jnp.where(_valid_rows(i, tm), dh_ref[...] + dhn, 0.0)
            _accumulate(dg_ref, jnp.sum(dgr, axis=0, keepdims=True), i == 0)

    return _call_after(
        dep, body, 6,
        [pl.BlockSpec((tm, d), lambda i, s: (i, 0)), pl.BlockSpec((tm, d), lambda i, s: (i, 0)),
         pl.BlockSpec((1, d), lambda i, s: (0, 0)), pl.BlockSpec((tm, ffs), lambda i, s: (i, s)),
         pl.BlockSpec((None, d, ffs), lambda i, s: (s, 0, 0)),
         pl.BlockSpec((None, ffs, d), lambda i, s: (s, 0, 0))], (dh, h, g, hp, w1g, w2g), grid=(t // tm, ns),
        out_specs=[pl.BlockSpec((tm, d), lambda i, s: (i, 0)), pl.BlockSpec((tm, ffs), lambda i, s: (i, s)),
                   pl.BlockSpec((1, d), lambda i, s: (0, 0))],
        out_shape=[jax.ShapeDtypeStruct((t, d), F32), jax.ShapeDtypeStruct((t, ns * ffs), BF16),
                   jax.ShapeDtypeStruct((1, d), F32)],
        scratch_shapes=[pltpu.VMEM((tm, d), F32)],
        compiler_params=_cparams("arbitrary", "arbitrary"), name=name)


WGRAD_ROWS = 2048
WGRAD_ROWS_BF16 = 4096


def _wgrad(x, dy, nb, xc, yc, x_by_block, dy_by_block, relu2, name, dep=None, rows=WGRAD_ROWS):
    t = x.shape[0]
    tk = _row_tile(t - CHUNK, rows, CHUNK)

    def prep(xv):
        if relu2:
            xv = jnp.maximum(xv.astype(F32), 0.0)
            xv = xv * xv
        return xv.astype(BF16)

    nk = (t - CHUNK) // tk

    def body(xh_ref, dyh_ref, x_ref, dy_ref, o_ref, acc_ref):
        k = pl.program_id(1)
        p = _dot_tn(prep(x_ref[...]), dy_ref[...].astype(BF16))

        @pl.when(k == 0)
        def _():
            acc_ref[...] = p + _dot_tn(prep(xh_ref[...]), dyh_ref[...].astype(BF16))

        @pl.when(k > 0)
        def _():
            acc_ref[...] += p

        @pl.when(k == nk - 1)
        def _():
            o_ref[...] = acc_ref[...].astype(BF16)

    def head(width, by_block):
        return pl.BlockSpec((CHUNK, width), (lambda b, k: (0, b)) if by_block else (lambda b, k: (0, 0)))

    def rest(width, by_block):
        def index(b, k):
            return pl.multiple_of(CHUNK + k * tk, CHUNK), (pl.multiple_of(b * width, 128) if by_block else 0)
        return pl.BlockSpec((pl.Element(tk), pl.Element(width)), index)

    return _call_after(
        dep, body, 4,
        [head(xc, x_by_block), head(yc, dy_by_block), rest(xc, x_by_block), rest(yc, dy_by_block)], (x, dy, x, dy),
        grid=(nb, nk),
        out_specs=pl.BlockSpec((None, xc, yc), lambda b, k: (b, 0, 0)),
        out_shape=jax.ShapeDtypeStruct((nb, xc, yc), BF16),
        scratch_shapes=[pltpu.VMEM((xc, yc), F32)],
        compiler_params=_cparams("parallel", "arbitrary"), name=name)


def _dgrad(dh, w, name, dep=None):
    t, d = dh.shape
    k = w.shape[0]
    tm = _row_tile(t, TOKEN_TILE_TARGET, 16)

    def body(dh_ref, w_ref, o_ref):
        o_ref[...] = _dot_nt(dh_ref[...].astype(BF16), w_ref[...]).astype(BF16)

    return _call_after(
        dep, body, 2,
        [pl.BlockSpec((tm, d), lambda i: (i, 0)), pl.BlockSpec((k, d), lambda i: (0, 0))], (dh, w), grid=(t // tm,),
        out_specs=pl.BlockSpec((tm, k), lambda i: (i, 0)),
        out_shape=jax.ShapeDtypeStruct((t, k), BF16),
        compiler_params=_cparams("parallel"), name=name)


def _dgrad_norm_bwd(dz, w, h, g, dh, nc, name):
    t, d = h.shape
    n = w.shape[1]
    tm = _row_tile(t, ROW_TILE_TARGET // 2, 16)

    def body(dz_ref, w_ref, h_ref, g_ref, dh_ref, dhi_ref, dg_ref):
        i = pl.program_id(0)
        du = jnp.zeros((tm, d), F32)
        for n0 in range(0, n, nc):
            n1 = min(n0 + nc, n)
            du = du + _dot_nt(dz_ref[:, n0:n1], w_ref[:, n0:n1])
        dhn, dgr = _rms_bwd(h_ref[...], g_ref[...], du)
        dhi_ref[...] = jnp.where(_valid_rows(i, tm), dh_ref[...] + dhn, 0.0)
        _accumulate(dg_ref, jnp.sum(dgr, axis=0, keepdims=True), i == 0)

    return pl.pallas_call(
        body, grid=(t // tm,),
        in_specs=[pl.BlockSpec((tm, n), lambda i: (i, 0)), pl.BlockSpec((d, n), lambda i: (0, 0)),
                  pl.BlockSpec((tm, d), lambda i: (i, 0)), pl.BlockSpec((1, d), lambda i: (0, 0)),
                  pl.BlockSpec((tm, d), lambda i: (i, 0))],
        out_specs=[pl.BlockSpec((tm, d), lambda i: (i, 0)), pl.BlockSpec((1, d), lambda i: (0, 0))],
        out_shape=[jax.ShapeDtypeStruct((t, d), F32), jax.ShapeDtypeStruct((1, d), F32)],
        compiler_params=_cparams("arbitrary"), name=name)(dz, w, h, g, dh)


def _dgrad_norm_bwd_input(dz, w, h, g, dh, nc, name):
    t, d = h.shape
    n = w.shape[1]
    tl = _row_tile(t - CHUNK, 512, CHUNK)

    def grads(dz_ref, w_ref, h_ref, g_ref, dh_ref, rows):
        du = jnp.zeros((rows, d), F32)
        for n0 in range(0, n, nc):
            n1 = min(n0 + nc, n)
            du = du + _dot_nt(dz_ref[:, n0:n1], w_ref[:, n0:n1])
        dhn, dgr = _rms_bwd(h_ref[...], g_ref[...], du)
        return dh_ref[...] + dhn, jnp.sum(dgr, axis=0, keepdims=True)

    def rest_body(dz_ref, w_ref, h_ref, g_ref, dh_ref, dg_head_ref, dx_ref, dg_ref):
        dx, dg = grads(dz_ref, w_ref, h_ref, g_ref, dh_ref, tl)
        dx_ref[...] = dx

        @pl.when(pl.program_id(0) == 0)
        def _():
            dg_ref[...] = dg_head_ref[...] + dg

        @pl.when(pl.program_id(0) > 0)
        def _():
            dg_ref[...] += dg

    def head_body(dz_ref, w_ref, h_ref, g_ref, dh_ref, dx_ref, dg_ref):
        dx, dg = grads(dz_ref, w_ref, h_ref, g_ref, dh_ref, CHUNK)
        dx_ref[...] = jnp.where(_valid_rows(0, CHUNK), dx, 0.0)
        dg_ref[...] = dg

    def shifted(width):
        return pl.BlockSpec((pl.Element(tl), pl.Element(width)), lambda i: (pl.multiple_of(CHUNK + i * tl, CHUNK), 0))

    whole = [pl.BlockSpec((d, n), lambda i: (0, 0)), pl.BlockSpec((1, d), lambda i: (0, 0))]
    head = lambda width: pl.BlockSpec((CHUNK, width), lambda i: (0, 0))
    dh_head, dg_head = pl.pallas_call(
        head_body, grid=(1,), in_specs=[head(n), whole[0], head(d), whole[1], head(d)],
        out_specs=[head(d), whole[1]],
        out_shape=[jax.ShapeDtypeStruct((CHUNK, d), F32), jax.ShapeDtypeStruct((1, d), F32)],
        compiler_params=_cparams("arbitrary"), name=name + "_head")(dz, w, h, g, dh)
    dx, dg = pl.pallas_call(
        rest_body, grid=((t - CHUNK) // tl,),
        in_specs=[shifted(n), whole[0], shifted(d), whole[1], shifted(d), whole[1]],
        out_specs=[pl.BlockSpec((tl, d), lambda i: (i, 0)), whole[1]],
        out_shape=[jax.ShapeDtypeStruct((t - CHUNK, d), F32), jax.ShapeDtypeStruct((1, d), F32)],
        compiler_params=_cparams("arbitrary"), name=name)(dz, w, h, g, dh, dg_head)
    return dx, dh_head, dg


def _loss_bwd(h, g, target):
    t, d = h.shape
    tl = _row_tile(t - CHUNK, 1024, CHUNK)

    def body(h_ref, g_ref, t_ref, dh_ref, dg_ref, loss_ref):
        i = pl.program_id(0)
        hv, gv = h_ref[...], g_ref[...]
        err = _rms(hv, gv) - t_ref[...]
        part = 0.5 * jnp.sum(jnp.mean(err * err, axis=-1, keepdims=True), axis=0, keepdims=True)
        dhn, dgr = _rms_bwd(hv, gv, err * (1.0 / d))
        dh_ref[...] = dhn
        _accumulate(dg_ref, jnp.sum(dgr, axis=0, keepdims=True), i == 0)
        _accumulate(loss_ref, jnp.broadcast_to(part, (8, 128)), i == 0)

    shifted = pl.BlockSpec((pl.Element(tl), pl.Element(d)), lambda i: (pl.multiple_of(CHUNK + i * tl, CHUNK), 0))
    dh, dg, loss = pl.pallas_call(
        body, grid=((t - CHUNK) // tl,),
        in_specs=[shifted, pl.BlockSpec((1, d), lambda i: (0, 0)), pl.BlockSpec((tl, d), lambda i: (i, 0))],
        out_specs=[shifted, pl.BlockSpec((1, d), lambda i: (0, 0)), pl.BlockSpec((8, 128), lambda i: (0, 0))],
        out_shape=[jax.ShapeDtypeStruct((t, d), F32), jax.ShapeDtypeStruct((1, d), F32),
                   jax.ShapeDtypeStruct((8, 128), F32)],
        compiler_params=_cparams("arbitrary"), name="loss_bwd")(h, g, target)

    def zero_head(dh_ref, o_ref):
        o_ref[...] = jnp.zeros_like(o_ref)

    dh = pl.pallas_call(
        zero_head, grid=(1,), in_specs=[ANY_SPEC], out_specs=pl.BlockSpec((CHUNK, d), lambda i: (0, 0)),
        out_shape=jax.ShapeDtypeStruct((t, d), F32), input_output_aliases={0: 0}, name="loss_bwd_head")(dh)
    return dh, dg, loss


CONV_BLOCK = 32


def _silu(x):
    return x * jax.nn.sigmoid(x)


def _row_shifts(win):
    n = win.shape[0]
    return [win] + [pltpu.roll(win, n - j, 0) for j in range(1, 8)]


def _cp_seq_fwd(z, conv_w, conv_b, ln_g, ln_b, pool_w, pool_scale):
    t = z.shape[0]
    tm = _row_tile(t, ROW_TILE_TARGET, CHUNK)

    def body(z_ref, cw_ref, cb_ref, lg_ref, lb_ref, pw_ref, ps_ref, c_ref, pm_ref, mix_ref, gbuf, pbuf):
        i = pl.program_id(0)

        @pl.when(i == 0)
        def _():
            gbuf[0:CONV_HALO, :] = jnp.zeros((CONV_HALO, CONV_DIM), F32)
            pbuf[0:POOL_HALO, :] = jnp.zeros((POOL_HALO, POOL_DIM), F32)

        @pl.when(i > 0)
        def _():
            gbuf[0:CONV_HALO, :] = gbuf[tm:tm + CONV_HALO, :]
            pbuf[0:POOL_HALO, :] = pbuf[tm:tm + POOL_HALO, :]

        av = z_ref[:, 0:CONV_DIM].astype(F32)
        ag = z_ref[:, CONV_DIM:2 * CONV_DIM].astype(F32)
        gbuf[CONV_HALO:CONV_HALO + tm, :] = av * jax.nn.sigmoid(ag)
        pbuf[POOL_HALO:POOL_HALO + tm, :] = z_ref[:, 2 * CONV_DIM:CP_IN].astype(F32)

        def conv_block(rb, carry):
            base = pl.multiple_of(rb * CONV_BLOCK, CONV_BLOCK)
            shifted = _row_shifts(gbuf[pl.ds(base, CONV_BLOCK + CONV_HALO), :])
            acc = jnp.zeros((CONV_BLOCK, CONV_DIM), F32)
            for k in range(CONV_WIDTH):
                whole, part = divmod(CONV_HALO - (CONV_WIDTH - 1) + k, 8)
                acc = acc + cw_ref[k:k + 1, :] * shifted[part][8 * whole:8 * whole + CONV_BLOCK, :]
            c_ref[pl.ds(base, CONV_BLOCK), :] = acc + cb_ref[...]
            return carry

        lax.fori_loop(0, tm // CONV_BLOCK, conv_block, 0)

        c = c_ref[...]
        mu = jnp.mean(c, axis=-1, keepdims=True)
        xc = c - mu
        ln = xc * lax.rsqrt(jnp.mean(xc * xc, axis=-1, keepdims=True) + EPS) * lg_ref[...] + lb_ref[...]
        row = i * tm + lax.broadcasted_iota(jnp.int32, (tm, 1), 0)
        mix_ref[:, 0:CONV_DIM] = jnp.where(row >= PAD_ROWS, _silu(ln), 0.0).astype(BF16)

        tpos = (row - PAD_ROWS + 1).astype(F32)
        for gi, wdw in enumerate(POOL_WINDOWS):
            lo = POOL_GROUP * gi
            run, step = pbuf[:, lo:lo + POOL_GROUP], 1
            cur = run[POOL_HALO:POOL_HALO + tm, :]
            while step < wdw:
                run = run + pltpu.roll(run, step, 0)
                step *= 2
            pm = (run[POOL_HALO:POOL_HALO + tm, :] / jnp.clip(tpos, 1.0, float(wdw)) - cur).astype(BF16)
            pm_ref[:, lo:lo + POOL_GROUP] = pm
            pg = _dot(pm, pw_ref[gi].astype(BF16))
            mix_ref[:, CONV_DIM + lo:CONV_DIM + lo + POOL_GROUP] = (pg * ps_ref[:, lo:lo + POOL_GROUP]).astype(BF16)

    vec = pl.BlockSpec((1, CONV_DIM), lambda i: (0, 0))
    return pl.pallas_call(
        body, grid=(t // tm,),
        in_specs=[pl.BlockSpec((tm, CP_IN), lambda i: (i, 0)),
                  pl.BlockSpec((CONV_WIDTH, CONV_DIM), lambda i: (0, 0)), vec, vec, vec,
                  pl.BlockSpec((len(POOL_WINDOWS), POOL_GROUP, POOL_GROUP), lambda i: (0, 0, 0)), vec],
        out_specs=[pl.BlockSpec((tm, CONV_DIM), lambda i: (i, 0)), pl.BlockSpec((tm, POOL_DIM), lambda i: (i, 0)),
                   pl.BlockSpec((tm, CONV_DIM + POOL_DIM), lambda i: (i, 0))],
        out_shape=[jax.ShapeDtypeStruct((t, CONV_DIM), F32), jax.ShapeDtypeStruct((t, POOL_DIM), BF16),
                   jax.ShapeDtypeStruct((t, CONV_DIM + POOL_DIM), BF16)],
        scratch_shapes=[pltpu.VMEM((tm + CONV_HALO, CONV_DIM), F32), pltpu.VMEM((tm + POOL_HALO, POOL_DIM), F32)],
        compiler_params=_cparams("arbitrary"), name="cp_seq_fwd")(z, conv_w, conv_b, ln_g, ln_b, pool_w, pool_scale)


def _cp_seq_bwd(dh, w_out, z, c, pm, conv_w, ln_g, ln_b, pool_w, pool_scale, dep=None):
    t, d = dh.shape
    tm = _row_tile(t, ROW_TILE_TARGET, CHUNK)
    nt = t // tm

    def body(dh_ref, wo_ref, z_ref, c_ref, pm_ref, cw_ref, lg_ref, lb_ref, pw_ref, ps_ref,
             dz_ref, dcw_ref, dvec_ref, dpw_ref, dcbuf, qbuf, glu_buf, dwacc, dmix_ref):
        i = pl.program_id(0)
        tile = nt - 1 - i
        dmix_ref[...] = _dot_nt(dh_ref[...].astype(BF16), wo_ref[...])

        @pl.when(i == 0)
        def _():
            dcbuf[tm:tm + CONV_HALO, :] = jnp.zeros((CONV_HALO, CONV_DIM), F32)
            qbuf[tm:tm + POOL_HALO, :] = jnp.zeros((POOL_HALO, POOL_DIM), F32)
            dcw_ref[...] = jnp.zeros_like(dcw_ref)
            dwacc[...] = jnp.zeros_like(dwacc)
            dvec_ref[...] = jnp.zeros_like(dvec_ref)
            dpw_ref[...] = jnp.zeros_like(dpw_ref)

        @pl.when(i > 0)
        def _():
            dcbuf[tm:tm + CONV_HALO, :] = dcbuf[0:CONV_HALO, :]
            qbuf[tm:tm + POOL_HALO, :] = qbuf[0:POOL_HALO, :]

        row = tile * tm + lax.broadcasted_iota(jnp.int32, (tm, 1), 0)
        cv = c_ref[...]
        mu = jnp.mean(cv, axis=-1, keepdims=True)
        xc = cv - mu
        rstd = lax.rsqrt(jnp.mean(xc * xc, axis=-1, keepdims=True) + EPS)
        xhat = xc * rstd
        ln = xhat * lg_ref[...] + lb_ref[...]
        sg = jax.nn.sigmoid(ln)
        da = jnp.where(row >= PAD_ROWS, dmix_ref[:, 0:CONV_DIM], 0.0)
        dln = da * (sg * (1.0 + ln * (1.0 - sg)))
        dxh = dln * lg_ref[...]
        dc = rstd * (dxh - jnp.mean(dxh, axis=-1, keepdims=True) - xhat * jnp.mean(dxh * xhat, axis=-1, keepdims=True))
        dcbuf[0:tm, :] = dc
        dvec_ref[0:1, :] += jnp.sum(dc, axis=0, keepdims=True)
        dvec_ref[1:2, :] += jnp.sum(dln * xhat, axis=0, keepdims=True)
        dvec_ref[2:3, :] += jnp.sum(dln, axis=0, keepdims=True)

        av = z_ref[:, 0:CONV_DIM].astype(F32)
        sig_g = jax.nn.sigmoid(z_ref[:, CONV_DIM:2 * CONV_DIM].astype(F32))
        glu_buf[...] = av * sig_g

        def conv_block(rb, carry):
            base = pl.multiple_of(rb * CONV_BLOCK, CONV_BLOCK)
            shifted = _row_shifts(dcbuf[pl.ds(base, CONV_BLOCK + CONV_HALO), :])
            glu = glu_buf[pl.ds(base, CONV_BLOCK), :]
            acc = jnp.zeros((CONV_BLOCK, CONV_DIM), F32)
            for k in range(CONV_WIDTH):
                whole, part = divmod(CONV_WIDTH - 1 - k, 8)
                slab = shifted[part][8 * whole:8 * whole + CONV_BLOCK, :]
                acc = acc + cw_ref[k:k + 1, :] * slab
                prod = slab * glu
                part = prod[0:8]
                for q in range(1, CONV_BLOCK // 8):
                    part = part + prod[8 * q:8 * q + 8]
                dwacc[k] += part
            glu_buf[pl.ds(base, CONV_BLOCK), :] = acc
            return carry

        lax.fori_loop(0, tm // CONV_BLOCK, conv_block, 0)

        @pl.when(i == nt - 1)
        def _():
            for k in range(CONV_WIDTH):
                dcw_ref[k:k + 1, :] = jnp.sum(dwacc[k], axis=0, keepdims=True)
        dglu = glu_buf[...]
        dz_ref[:, 0:CONV_DIM] = (dglu * sig_g).astype(BF16)
        dz_ref[:, CONV_DIM:2 * CONV_DIM] = (dglu * av * sig_g * (1.0 - sig_g)).astype(BF16)

        tpos = (row - PAD_ROWS + 1).astype(F32)
        for gi, wdw in enumerate(POOL_WINDOWS):
            lo = POOL_GROUP * gi
            dp = dmix_ref[:, CONV_DIM + lo:CONV_DIM + lo + POOL_GROUP]
            pmv = pm_ref[:, lo:lo + POOL_GROUP]
            pwb = pw_ref[gi].astype(BF16)
            dvec_ref[3:4, lo:lo + POOL_GROUP] += jnp.sum(dp * _dot(pmv, pwb), axis=0, keepdims=True)
            dq = (dp * ps_ref[:, lo:lo + POOL_GROUP]).astype(BF16)
            dpw_ref[gi] += _dot_tn(pmv, dq)
            dpm = _dot_nt(dq, pwb)
            qbuf[0:tm, lo:lo + POOL_GROUP] = dpm / jnp.clip(tpos, 1.0, float(wdw))
            run, step = qbuf[:, lo:lo + POOL_GROUP], 1
            while step < wdw:
                run = run + pltpu.roll(run, tm + POOL_HALO - step, 0)
                step *= 2
            dz_ref[:, 2 * CONV_DIM + lo:2 * CONV_DIM + lo + POOL_GROUP] = (run[0:tm, :] - dpm).astype(BF16)

    vec = pl.BlockSpec((1, CONV_DIM), lambda i: (0, 0))
    rev = lambda i: (nt - 1 - i, 0)
    return _call_after(
        dep, body, 10,
        [pl.BlockSpec((tm, d), rev), pl.BlockSpec((CONV_DIM + POOL_DIM, d), lambda i: (0, 0)), pl.BlockSpec((tm, CP_IN), rev),
         pl.BlockSpec((tm, CONV_DIM), rev), pl.BlockSpec((tm, POOL_DIM), rev),
         pl.BlockSpec((CONV_WIDTH, CONV_DIM), lambda i: (0, 0)), vec, vec,
         pl.BlockSpec((len(POOL_WINDOWS), POOL_GROUP, POOL_GROUP), lambda i: (0, 0, 0)), vec],
        (dh, w_out, z, c, pm, conv_w, ln_g, ln_b, pool_w, pool_scale), grid=(nt,),
        out_specs=[pl.BlockSpec((tm, CP_IN), rev), pl.BlockSpec((CONV_WIDTH + 1, CONV_DIM), lambda i: (0, 0)),
                   pl.BlockSpec((8, CONV_DIM), lambda i: (0, 0)),
                   pl.BlockSpec((len(POOL_WINDOWS), POOL_GROUP, POOL_GROUP), lambda i: (0, 0, 0))],
        out_shape=[jax.ShapeDtypeStruct((t, CP_IN), BF16), jax.ShapeDtypeStruct((CONV_WIDTH + 1, CONV_DIM), F32),
                   jax.ShapeDtypeStruct((8, CONV_DIM), F32),
                   jax.ShapeDtypeStruct((len(POOL_WINDOWS), POOL_GROUP, POOL_GROUP), F32)],
        scratch_shapes=[pltpu.VMEM((tm + CONV_HALO, CONV_DIM), F32), pltpu.VMEM((tm + POOL_HALO, POOL_DIM), F32),
                        pltpu.VMEM((tm, CONV_DIM), F32), pltpu.VMEM((CONV_WIDTH + 1, 8, CONV_DIM), F32),
                        pltpu.VMEM((tm, CONV_DIM + POOL_DIM), F32)],
        compiler_params=_cparams("arbitrary"), name="cp_seq_bwd")


Q0, K0, V0, G0, R0 =0, GLA_DK, 2 * GLA_DK, 2 * GLA_DK + GLA_DV, 2 * GLA_DK + 2 * GLA_DV


def _split3(x):
    hi = x.astype(BF16)
    r1 = x - hi.astype(F32)
    mid = r1.astype(BF16)
    lo = (r1 - mid.astype(F32)).astype(BF16)
    return hi, mid, lo


def _tri(strict):
    r = lax.broadcasted_iota(jnp.int32, (CHUNK, CHUNK), 0)
    c = lax.broadcasted_iota(jnp.int32, (CHUNK, CHUNK), 1)
    return ((r > c) if strict else (r >= c)).astype(BF16)


def _chunk_sums(x, cpt, strict, pieces):
    tri3 = jnp.broadcast_to(_tri(strict)[None], (cpt, CHUNK, CHUNK))
    acc = None
    for piece in _split3(x.reshape(cpt, CHUNK, x.shape[-1]))[:pieces]:
        part = jnp.einsum("bij,bjk->bik", tri3, piece, preferred_element_type=F32)
        acc = part if acc is None else acc + part
    return acc


def _chunk_decay(r, gw_ref, gb_ref, cpt):
    pre = _dot(r, gw_ref[...]) + gb_ref[...]
    lac = (jnp.minimum(pre, 0.0) - jnp.log(1.0 + jnp.exp(-jnp.abs(pre)))) * (1.0 / GATE_NORM)
    cum3 = _chunk_sums(lac, cpt, False, 3)
    return cum3, cum3[:, CHUNK - 1:CHUNK, :]


def _gla_seq_fwd(z, gate_w, gate_b, head_g, h, w_out, dep=None):
    t = z.shape[0]
    tm = _row_tile(t, ROW_TILE_TARGET, CHUNK)
    cpt = tm // CHUNK
    scale = GLA_HK ** -0.5

    def body(z_ref, gw_ref, gb_ref, hg_ref, h_ref, wo_ref, o_ref, mix_ref, st_ref, ho_ref, state, kdec_s, e_s):
        @pl.when(pl.program_id(0) == 0)
        def _():
            state[...] = jnp.zeros_like(state)

        cum3, tot3 = _chunk_decay(z_ref[:, R0:R0 + GATE_PAD], gw_ref, gb_ref, cpt)
        dec = jnp.exp(jnp.broadcast_to(tot3, cum3.shape) - cum3).reshape(tm, GLA_DK)
        kdec_s[...] = (z_ref[:, K0:K0 + GLA_DK].astype(F32) * dec).astype(BF16)
        e_s[...] = jnp.exp(jnp.broadcast_to(tot3, (cpt, 8, GLA_DK))).reshape(cpt * 8, GLA_DK)

        def chunk(ci, carry):
            rows = pl.ds(pl.multiple_of(ci * CHUNK, CHUNK), CHUNK)
            e_all = e_s[pl.ds(pl.multiple_of(ci * 8, 8), 8), :][0:1, :]
            st_ref[ci] = state[...].astype(BF16)
            for hd in range(GLA_HEADS):
                ks = slice(hd * GLA_HK, (hd + 1) * GLA_HK)
                vs = slice(hd * GLA_HV, (hd + 1) * GLA_HV)
                v = z_ref[rows, V0 + hd * GLA_HV:V0 + (hd + 1) * GLA_HV]
                st = state[vs, :] * e_all[:, ks] + _dot_tn(v, kdec_s[rows, ks])
                state[vs, :] = st
                q = z_ref[rows, Q0 + hd * GLA_HK:Q0 + (hd + 1) * GLA_HK]
                o_ref[rows, vs] = (_dot_nt(q, st.astype(BF16)) * scale).astype(BF16)
            return carry

        lax.fori_loop(0, cpt, chunk, 0, unroll=cpt)

        for hd in range(GLA_HEADS):
            vs = slice(hd * GLA_HV, (hd + 1) * GLA_HV)
            on = _rms(o_ref[:, vs].astype(F32), hg_ref[...])
            gv = z_ref[:, G0 + hd * GLA_HV:G0 + (hd + 1) * GLA_HV].astype(F32)
            mix_ref[:, vs] = (on * _silu(gv)).astype(BF16)
        ho_ref[...] = h_ref[...] + _dot(mix_ref[...], wo_ref[...])

    d = h.shape[1]
    return _call_after(
        dep, body, 6,
        [pl.BlockSpec((tm, GLA_IN_PAD), lambda i: (i, 0)),
         pl.BlockSpec((GATE_PAD, GLA_DK), lambda i: (0, 0)), pl.BlockSpec((1, GLA_DK), lambda i: (0, 0)),
         pl.BlockSpec((1, GLA_HV), lambda i: (0, 0)), pl.BlockSpec((tm, d), lambda i: (i, 0)),
         pl.BlockSpec((GLA_DV, d), lambda i: (0, 0))], (z, gate_w, gate_b, head_g, h, w_out), grid=(t // tm,),
        out_specs=[pl.BlockSpec((tm, GLA_DV), lambda i: (i, 0)), pl.BlockSpec((tm, GLA_DV), lambda i: (i, 0)),
                   pl.BlockSpec((cpt, GLA_DV, GLA_HK), lambda i: (i, 0, 0)), pl.BlockSpec((tm, d), lambda i: (i, 0))],
        out_shape=[jax.ShapeDtypeStruct((t, GLA_DV), BF16), jax.ShapeDtypeStruct((t, GLA_DV), BF16),
                   jax.ShapeDtypeStruct((t // CHUNK, GLA_DV, GLA_HK), BF16), jax.ShapeDtypeStruct((t, d), F32)],
        scratch_shapes=[pltpu.VMEM((GLA_DV, GLA_HK), F32), pltpu.VMEM((tm, GLA_DK), BF16),
                        pltpu.VMEM((cpt * 8, GLA_DK), F32)],
        compiler_params=_cparams("arbitrary"), name="gla_seq_fwd")


def _gla_seq_bwd(dh, w_out, o, z, states, gate_w, gate_b, head_g, dep=None):
    t, d = dh.shape
    tm = _row_tile(t, ROW_TILE_TARGET, CHUNK)
    cpt = tm // CHUNK
    nt = t // tm
    scale = GLA_HK ** -0.5

    def body(dh_ref, wo_ref, o_ref, z_ref, st_ref, gw_ref, gb_ref, hg_ref, dz_ref, dgw_ref, dgb_ref, dhg_ref,
             dstate, dec_s, kdec_s, dkdec_s, do_s, e_s, dtot_s):
        @pl.when(pl.program_id(0) == 0)
        def _():
            dstate[...] = jnp.zeros_like(dstate)
            dgw_ref[...] = jnp.zeros_like(dgw_ref)
            dgb_ref[...] = jnp.zeros_like(dgb_ref)
            dhg_ref[...] = jnp.zeros_like(dhg_ref)

        cum3, tot3 = _chunk_decay(z_ref[:, R0:R0 + GATE_PAD], gw_ref, gb_ref, cpt)
        dec = jnp.exp(jnp.broadcast_to(tot3, cum3.shape) - cum3).reshape(tm, GLA_DK)
        dec_s[...] = dec
        kdec = z_ref[:, K0:K0 + GLA_DK].astype(F32) * dec
        kdec_s[...] = kdec
        e3 = jnp.exp(tot3)
        e_s[...] = jnp.broadcast_to(e3, (cpt, 8, GLA_DK)).reshape(cpt * 8, GLA_DK)
        dhg = jnp.zeros((1, GLA_HV), F32)
        dhb = dh_ref[...].astype(BF16)
        for hd in range(GLA_HEADS):
            ks = slice(hd * GLA_HK, (hd + 1) * GLA_HK)
            vs = slice(hd * GLA_HV, (hd + 1) * GLA_HV)
            gcols = slice(G0 + hd * GLA_HV, G0 + (hd + 1) * GLA_HV)
            ov = o_ref[:, vs].astype(F32)
            gv = z_ref[:, gcols].astype(F32)
            dm = _dot_nt(dhb, wo_ref[vs, :])
            sg = jax.nn.sigmoid(gv)
            rr = lax.rsqrt(jnp.mean(ov * ov, axis=-1, keepdims=True) + EPS)
            xhat = ov * rr
            don = dm * (gv * sg)
            dz_ref[:, gcols] = (dm * (xhat * hg_ref[...]) * (sg * (1.0 + gv * (1.0 - sg)))).astype(BF16)
            dhg = dhg + jnp.sum(don * xhat, axis=0, keepdims=True)
            dxh = don * hg_ref[...]
            do = (rr * (dxh - xhat * jnp.mean(dxh * xhat, axis=-1, keepdims=True)) * scale).astype(BF16)
            do_s[:, vs] = do
            v3 = z_ref[:, V0 + hd * GLA_HV:V0 + (hd + 1) * GLA_HV].reshape(cpt, CHUNK, GLA_HV)
            kdb3 = kdec[:, ks].astype(BF16).reshape(cpt, CHUNK, GLA_HK)
            st3 = st_ref[:, vs, :].astype(F32) * e3[:, :, ks] + jnp.einsum("bcv,bck->bvk", v3, kdb3,
                                                                            preferred_element_type=F32)
            dq3 = jnp.einsum("bcv,bvk->bck", do.reshape(cpt, CHUNK, GLA_HV), st3.astype(BF16), preferred_element_type=F32)
            dz_ref[:, Q0 + hd * GLA_HK:Q0 + (hd + 1) * GLA_HK] = dq3.reshape(tm, GLA_HK).astype(BF16)
        dhg_ref[...] += dhg

        def chunk(cj, carry):
            ci = cpt - 1 - cj
            rows = pl.ds(pl.multiple_of(ci * CHUNK, CHUNK), CHUNK)
            erows = pl.ds(pl.multiple_of(ci * 8, 8), 8)
            e_all = e_s[erows, :][0:1, :]
            for hd in range(GLA_HEADS):
                ks = slice(hd * GLA_HK, (hd + 1) * GLA_HK)
                vs = slice(hd * GLA_HV, (hd + 1) * GLA_HV)
                e = e_all[:, ks]
                kdb = kdec_s[rows, ks].astype(BF16)
                v = z_ref[rows, V0 + hd * GLA_HV:V0 + (hd + 1) * GLA_HV]
                q = z_ref[rows, Q0 + hd * GLA_HK:Q0 + (hd + 1) * GLA_HK]
                do = do_s[rows, vs]
                st_prev = st_ref[ci, vs, :].astype(F32)
                dst = dstate[vs, :] + _dot_tn(do, q)
                dstb = dst.astype(BF16)
                dkdec_s[rows, ks] = _dot(v, dstb)
                dz_ref[rows, V0 + hd * GLA_HV:V0 + (hd + 1) * GLA_HV] = _dot_nt(kdb, dstb).astype(BF16)
                dtot = jnp.sum(dst * st_prev, axis=0, keepdims=True) * e
                dtot_s[erows, ks] = jnp.broadcast_to(dtot, (8, GLA_HK))
                dstate[vs, :] = dst * e
            return carry

        lax.fori_loop(0, cpt, chunk, 0, unroll=cpt)

        dkdec = dkdec_s[...]
        dz_ref[:, K0:K0 + GLA_DK] = (dkdec * dec_s[...]).astype(BF16)
        before = _chunk_sums(dkdec * kdec_s[...], cpt, True, 2)
        dtot3 = dtot_s[...].reshape(cpt, 8, GLA_DK)[:, 0:1, :]
        dlac = (jnp.broadcast_to(dtot3, before.shape) + before).reshape(tm, GLA_DK)
        pre = _dot(z_ref[:, R0:R0 + GATE_PAD], gw_ref[...]) + gb_ref[...]
        dpre = dlac * (1.0 / GATE_NORM) * (1.0 - jax.nn.sigmoid(pre))
        dpb = dpre.astype(BF16)
        dz_ref[:, R0:R0 + GATE_PAD] = _dot_nt(dpb, gw_ref[...]).astype(BF16)
        dgw_ref[...] += _dot_tn(z_ref[:, R0:R0 + GATE_PAD], dpb)
        dgb_ref[...] += jnp.sum(dpre, axis=0, keepdims=True)

    rev = lambda i: (nt - 1 - i, 0)
    return _call_after(
        dep, body, 8,
        [pl.BlockSpec((tm, d), rev), pl.BlockSpec((GLA_DV, d), lambda i: (0, 0)),
         pl.BlockSpec((tm, GLA_DV), rev), pl.BlockSpec((tm, GLA_IN_PAD), rev),
         pl.BlockSpec((cpt, GLA_DV, GLA_HK), lambda i: (nt - 1 - i, 0, 0)),
         pl.BlockSpec((GATE_PAD, GLA_DK), lambda i: (0, 0)), pl.BlockSpec((1, GLA_DK), lambda i: (0, 0)),
         pl.BlockSpec((1, GLA_HV), lambda i: (0, 0))],
        (dh, w_out, o, z, states, gate_w, gate_b, head_g), grid=(nt,),
        out_specs=[pl.BlockSpec((tm, GLA_IN_PAD), rev), pl.BlockSpec((GATE_PAD, GLA_DK), lambda i: (0, 0)),
                   pl.BlockSpec((1, GLA_DK), lambda i: (0, 0)), pl.BlockSpec((1, GLA_HV), lambda i: (0, 0))],
        out_shape=[jax.ShapeDtypeStruct((t, GLA_IN_PAD), BF16), jax.ShapeDtypeStruct((GATE_PAD, GLA_DK), F32),
                   jax.ShapeDtypeStruct((1, GLA_DK), F32), jax.ShapeDtypeStruct((1, GLA_HV), F32)],
        scratch_shapes=[pltpu.VMEM((GLA_DV, GLA_HK), F32), pltpu.VMEM((tm, GLA_DK), F32), pltpu.VMEM((tm, GLA_DK), F32),
                        pltpu.VMEM((tm, GLA_DK), F32), pltpu.VMEM((tm, GLA_DV), BF16),
                        pltpu.VMEM((cpt * 8, GLA_DK), F32), pltpu.VMEM((cpt * 8, GLA_DK), F32)],
        compiler_params=_cparams("arbitrary"), name="gla_seq_bwd")


def _sum_slots(x, name):
    n, r, cdim = x.shape
    tr = _row_tile(r, 256, 8)

    def body(x_ref, o_ref):
        acc = x_ref[0].astype(F32)
        for j in range(1, n):
            acc = acc + x_ref[j].astype(F32)
        o_ref[...] = acc

    return pl.pallas_call(
        body, grid=(r // tr,),
        in_specs=[pl.BlockSpec((n, tr, cdim), lambda i: (0, i, 0))],
        out_specs=pl.BlockSpec((tr, cdim), lambda i: (i, 0)),
        out_shape=jax.ShapeDtypeStruct((r, cdim), F32),
        compiler_params=_cparams("parallel"), name=name)(x)


def _sum_own_and_slots(own, slots, dev_idx, name):
    _, _, r, cdim = own.shape
    n = slots.shape[0]
    tr = _row_tile(r, 256, 8)

    def body(s_ref, own_ref, *rest):
        acc = own_ref[...].astype(F32)
        for other in rest[:n - 1]:
            acc = acc + other[...].astype(F32)
        rest[n - 1][...] = acc

    def slot(dd):
        return pl.BlockSpec((None, tr, cdim), lambda i, s: ((s[0] + dd) % n, i, 0))

    mine = pl.BlockSpec((None, None, tr, cdim), lambda i, s: (s[0] // 2, s[0] % 2, i, 0))
    return pl.pallas_call(
        body,
        grid_spec=pltpu.PrefetchScalarGridSpec(
            num_scalar_prefetch=1, grid=(r // tr,), in_specs=[mine] + [slot(dd) for dd in range(1, n)],
            out_specs=pl.BlockSpec((tr, cdim), lambda i, s: (i, 0))),
        out_shape=jax.ShapeDtypeStruct((r, cdim), F32),
        compiler_params=_cparams("parallel"), name=name)(dev_idx, own, *([slots] * (n - 1)))


def _add2(a, b, name):
    r, cdim = a.shape
    tr = _row_tile(r, 256, 8)

    def body(a_ref, b_ref, o_ref):
        o_ref[...] = a_ref[...] + b_ref[...]

    spec = pl.BlockSpec((tr, cdim), lambda i: (i, 0))
    return pl.pallas_call(body, grid=(r // tr,), in_specs=[spec, spec], out_specs=spec,
                          out_shape=jax.ShapeDtypeStruct((r, cdim), F32),
                          compiler_params=_cparams("parallel"), name=name)(a, b)


def _adamw_half(w, gs, m, v, half_idx, prev, name, dep=None):
    nl, _, h, cdim = w.shape
    tr = _row_tile(h, 256, 8)
    nprev = 0 if prev is None else 4
    extra = [] if dep is None else [dep]

    def body(s_ref, w_ref, m_ref, v_ref, *rest):
        g_refs = rest[:nl]
        go_ref, d_ref, mo_ref, vo_ref = rest[nl + nprev + len(extra):]
        layer = pl.program_id(0)
        gv = g_refs[0][...]
        for j in range(1, nl):
            gv = jnp.where(layer == j, g_refs[j][...], gv)
        go_ref[...] = gv
        mn = ADAM_B1 * m_ref[...] + (1.0 - ADAM_B1) * gv
        vn = ADAM_B2 * v_ref[...] + (1.0 - ADAM_B2) * (gv * gv)
        m_hat = mn / (1.0 - ADAM_B1 ** ADAM_STEP)
        v_hat = vn / (1.0 - ADAM_B2 ** ADAM_STEP)
        d_ref[...] = -ADAM_LR * (m_hat / (jnp.sqrt(v_hat) + ADAM_EPS) + ADAM_WD * w_ref[...])
        mo_ref[...] = mn
        vo_ref[...] = vn

    half = pl.BlockSpec((None, None, tr, cdim), lambda l, i, s: (l, s[0], i, 0))

    def of_layer(j):
        return pl.BlockSpec((tr, cdim), lambda l, i, s: (jnp.where(l == j, i, 0), 0))

    shp = jax.ShapeDtypeStruct(w.shape, F32)
    return pl.pallas_call(
        body,
        grid_spec=pltpu.PrefetchScalarGridSpec(
            num_scalar_prefetch=1, grid=(nl, h // tr),
            in_specs=[half] * 3 + [of_layer(j) for j in range(nl)] + [ANY_SPEC] * (nprev + len(extra)),
            out_specs=[half] * 4),
        out_shape=[shp] * 4, input_output_aliases={4 + nl + k: k for k in range(nprev)},
        compiler_params=_cparams("arbitrary", "arbitrary"), name=name,
    )(half_idx, w, m, v, *gs, *([] if prev is None else prev), *extra)


def _adamw_many(ws, gs, ms, vs):
    n = len(ws)

    def body(*refs):
        for i in range(n):
            w_ref, g_ref, m_ref, v_ref = refs[i], refs[n + i], refs[2 * n + i], refs[3 * n + i]
            d_ref, mo_ref, vo_ref = refs[4 * n + i], refs[5 * n + i], refs[6 * n + i]
            gv = g_ref[...]
            mn = ADAM_B1 * m_ref[...] + (1.0 - ADAM_B1) * gv
            vn = ADAM_B2 * v_ref[...] + (1.0 - ADAM_B2) * (gv * gv)
            m_hat = mn / (1.0 - ADAM_B1 ** ADAM_STEP)
            v_hat = vn / (1.0 - ADAM_B2 ** ADAM_STEP)
            d_ref[...] = -ADAM_LR * (m_hat / (jnp.sqrt(v_hat) + ADAM_EPS) + ADAM_WD * w_ref[...])
            mo_ref[...] = mn
            vo_ref[...] = vn

    shapes = [jax.ShapeDtypeStruct(w.shape, F32) for w in ws]
    outs = pl.pallas_call(body, out_shape=shapes * 3, name="adamw_small")(*ws, *gs, *ms, *vs)
    return outs[:n], outs[n:2 * n], outs[2 * n:]


def _split_rows(a):
    return a.reshape(a.shape[0], 2, a.shape[1] // 2, a.shape[2])


def _place():
    x, y, c = lax.axis_index("x"), lax.axis_index("y"), lax.axis_index("c")
    chips = [(1 - x, y), (x, 1 - y), (1 - x, 1 - y)]
    return x, y, c, chips


def _remote(src, dst, send_sem, recv_sem, to):
    return pltpu.make_async_remote_copy(src_ref=src, dst_ref=dst, send_sem=send_sem, recv_sem=recv_sem,
                                        device_id=to, device_id_type=MESH)


def _plan_gather(n_halved):
    def plan(src_refs, land_refs):
        x, y, c, chips = _place()
        me = 2 * x + y
        copies = []
        for k, (src, land) in enumerate(zip(src_refs, land_refs)):
            for (px, py) in chips:
                frm = 2 * px + py
                if k < n_halved:
                    copies.append((src.at[c], land.at[me, c], (px, py, c), land.at[frm, c]))
                else:
                    copies.append((src, land.at[me], (px, py, c), land.at[frm]))
        return copies
    return plan


def _plan_share(src_refs, land_refs):
    x, y, c, chips = _place()
    me = 2 * x + y
    sib = (x, y, 1 - c)
    copies = []
    for src, land in zip(src_refs, land_refs):
        copies.append((src, land.at[me], sib, land.at[me]))
        for (px, py) in chips:
            frm = 2 * px + py
            copies.append((land.at[frm, c], land.at[frm, c], sib, land.at[frm, 1 - c]))
    return copies


def _plan_scatter(n_parts):
    def plan(src_refs, land_refs):
        x, y, c, chips = _place()
        me = 2 * x + y
        copies = []
        for k, (src, land) in enumerate(zip(src_refs, land_refs)):
            for (px, py) in chips:
                to = 2 * px + py
                copies.append((src.at[to] if k < n_parts else src, land.at[me], (px, py, c), land.at[to]))
        return copies
    return plan


N_DEVICES = 8
OTHER_DEVICES = [(dx, dy, dc) for dx in (0, 1) for dy in (0, 1) for dc in (0, 1) if dx or dy or dc]


def _plan_scatter_all(src_refs, land_refs):
    x, y, c, _ = _place()
    me = 4 * x + 2 * y + c
    copies = []
    for src, land in zip(src_refs, land_refs):
        for dx, dy, dc in OTHER_DEVICES:
            px, py, pc = (1 - x if dx else x), (1 - y if dy else y), (1 - c if dc else c)
            copies.append((src.at[2 * px + py, pc], land.at[me], (px, py, pc), land.at[4 * px + 2 * py + pc]))
    return copies


def _plan_exchange(n_split):
    def plan(src_refs, land_refs):
        x, y, c, _ = _place()
        sib = (x, y, 1 - c)
        return [(src.at[:, 1 - c] if k < n_split else src, land, sib, land)
                for k, (src, land) in enumerate(zip(src_refs, land_refs))]
    return plan


def _hbm(a):
    return pltpu.HBM(a.shape, a.dtype)


def _start_copies(name, srcs, lands, plan, ncopy, dep=None):
    ns, nl = len(srcs), len(lands)
    nin = ns + nl + (0 if dep is None else 1)

    def body(*refs):
        send_sems, recv_sems, token = refs[nin], refs[nin + 1], refs[-1]
        for k, (src, dst, dev, _) in enumerate(plan(refs[:ns], refs[ns:ns + nl])):
            _remote(src, dst, send_sems.at[k], recv_sems.at[k], dev).start()
        token[...] = jnp.zeros_like(token)

    args = [pltpu.with_memory_space_constraint(a, pltpu.HBM) for a in list(srcs) + list(lands)]
    outs = pl.pallas_call(
        body, name=name,
        out_shape=(pltpu.SemaphoreType.DMA((ncopy,)), pltpu.SemaphoreType.DMA((ncopy,)),
                   *[_hbm(a) for a in list(srcs) + list(lands)], jax.ShapeDtypeStruct((8, 128), F32)),
        in_specs=[HBM_SPEC] * (ns + nl) + ([] if dep is None else [ANY_SPEC]),
        out_specs=(SEM_SPEC, SEM_SPEC, *([HBM_SPEC] * (ns + nl)), pl.BlockSpec(memory_space=pltpu.VMEM)),
        input_output_aliases={i: 2 + i for i in range(ns + nl)},
        compiler_params=pltpu.CompilerParams(has_side_effects=SIDE_EFFECT),
    )(*args, *([] if dep is None else [dep]))
    return outs[0], outs[1], list(outs[2:2 + ns]), list(outs[2 + ns:2 + ns + nl]), outs[-1]


def _wait_copies(name, started, plan, after, sem_offset=0):
    send_sems, recv_sems, srcs, lands, _ = started
    ns, nl = len(srcs), len(lands)
    after = list(after) if isinstance(after, (list, tuple)) else [after]

    def body(*refs):
        send_ref, recv_ref = refs[ns + nl], refs[ns + nl + 1]
        for k, (src, _, dev, mine) in enumerate(plan(refs[:ns], refs[ns:ns + nl])):
            copy = _remote(src, mine, send_ref.at[sem_offset + k], recv_ref.at[sem_offset + k], dev)
            copy.wait_send()
            copy.wait_recv()

    outs = pl.pallas_call(
        body, name=name, out_shape=tuple(_hbm(a) for a in srcs + lands),
        in_specs=[HBM_SPEC] * (ns + nl) + [SEM_SPEC, SEM_SPEC] + [ANY_SPEC] * len(after),
        out_specs=tuple([HBM_SPEC] * (ns + nl)),
        input_output_aliases={i: i for i in range(ns + nl)},
        compiler_params=pltpu.CompilerParams(has_side_effects=SIDE_EFFECT),
    )(*srcs, *lands, send_sems, recv_sems, *after)
    return list(outs[:ns]), list(outs[ns:])


def _share_with_sibling(name, srcs, lands):
    n = len(srcs)

    def body(*refs):
        src_refs, land_refs, out_refs = refs[:n], refs[n:2 * n], refs[2 * n:3 * n]
        send_sem, recv_sem = refs[3 * n:]
        x, y, c, chips = _place()
        me = 2 * x + y
        sib = (x, y, 1 - c)
        sends, recvs = [], []
        for k in range(n):
            sems = (send_sem.at[4 * k], recv_sem.at[4 * k])
            sends.append(_remote(src_refs[k], out_refs[k].at[me], *sems, sib))
            recvs.append(_remote(src_refs[k], out_refs[k].at[me], *sems, sib))
            for j, (px, py) in enumerate(chips):
                frm = 2 * px + py
                sems = (send_sem.at[4 * k + 1 + j], recv_sem.at[4 * k + 1 + j])
                sends.append(_remote(land_refs[k].at[frm, c], out_refs[k].at[frm, c], *sems, sib))
                recvs.append(_remote(land_refs[k].at[frm, c], out_refs[k].at[frm, 1 - c], *sems, sib))
        for cp in sends:
            cp.start()
        for cp in recvs:
            cp.wait_recv()
        for cp in sends:
            cp.wait_send()

    return pl.pallas_call(
        body, name=name, in_specs=[HBM_SPEC] * (2 * n), out_specs=[HBM_SPEC] * n,
        out_shape=[jax.ShapeDtypeStruct(a.shape, a.dtype) for a in lands],
        input_output_aliases={n + k: k for k in range(n)},
        scratch_shapes=[pltpu.SemaphoreType.DMA((4 * n,)), pltpu.SemaphoreType.DMA((4 * n,))],
    )(*srcs, *lands)


def _pack(arrs):
    flat = jnp.concatenate([a.reshape(-1).astype(F32) for a in arrs])
    n = flat.shape[0]
    rows = -(-n // PACK_WIDTH)
    rows = -(-rows // 8) * 8
    return jnp.pad(flat, (0, rows * PACK_WIDTH - n)).reshape(rows, PACK_WIDTH)


def _unpack(buf, shapes):
    flat = buf.reshape(-1)
    out, off = [], 0
    for shp in shapes:
        n = 1
        for s in shp:
            n *= s
        out.append(flat[off:off + n].reshape(shp))
        off += n
    return out


def _unshard_cols(stacked):
    moved = jnp.moveaxis(stacked, 0, -2)
    return moved.reshape(moved.shape[:-2] + (moved.shape[-2] * moved.shape[-1],))


def _take_cols(blocks, start, width):
    bw = blocks.shape[2]
    pieces, lo = [], start
    while lo < start + width:
        b = lo // bw
        hi = min(start + width, (b + 1) * bw)
        pieces.append(blocks[b][:, lo - b * bw:hi - b * bw])
        lo = hi
    return jnp.concatenate(pieces, axis=1)


def _col_shard(full, s, width):
    return lax.dynamic_slice_in_dim(full, s * width, width, axis=full.ndim - 1)


def kernel(x, meta_tokens, mix_norm_g, ffn_norm_g, ffn_w1, ffn_w2, cp_w_in, cp_conv_w, cp_conv_b, cp_ln_g, cp_ln_b, cp_pool_w, cp_pool_scale, cp_w_out, gla_w_in, gla_gate_w2, gla_gate_b, gla_head_g, gla_w_out, final_norm_g, loss_target, m_meta_tokens, m_mix_norm_g, m_ffn_norm_g, m_ffn_w1, m_ffn_w2, m_cp_w_in, m_cp_conv_w, m_cp_conv_b, m_cp_ln_g, m_cp_ln_b, m_cp_pool_w, m_cp_pool_scale, m_cp_w_out, m_gla_w_in, m_gla_gate_w2, m_gla_gate_b, m_gla_head_g, m_gla_w_out, m_final_norm_g, v_meta_tokens, v_mix_norm_g, v_ffn_norm_g, v_ffn_w1, v_ffn_w2, v_cp_w_in, v_cp_conv_w, v_cp_conv_b, v_cp_ln_g, v_cp_ln_b, v_cp_pool_w, v_cp_pool_scale, v_cp_w_out, v_gla_w_in, v_gla_gate_w2, v_gla_gate_b, v_gla_head_g, v_gla_w_out, v_final_norm_g):
    d = D_MODEL
    chip = 2 * lax.axis_index("x") + lax.axis_index("y")
    core = lax.axis_index("c")
    seq = x.shape[1]
    t = seq + CHUNK

    sharded_small = [meta_tokens, cp_conv_w, gla_gate_w2, gla_gate_b, gla_head_g]

    def halves(w, token=None):
        if token is not None:
            w = w + token[0, 0]
        return w.astype(BF16).reshape(2, w.shape[0] // 2, w.shape[1])

    def unhalve(g):
        return g.reshape(N_CHIPS, 2 * g.shape[2], g.shape[3])

    def gather_group(srcs, whole=()):
        lands = [lax.empty((N_CHIPS,) + s.shape, s.dtype) for s in srcs]
        for a in whole:
            lands.append(lax.dynamic_update_slice(jnp.zeros((N_CHIPS,) + a.shape, a.dtype), a[None], (chip,) + (0,) * a.ndim))
        return list(srcs) + list(whole), lands, _plan_gather(len(srcs)), len(srcs)

    def start_groups(name, groups, dep):
        bounds, all_srcs, all_lands = [], [], []
        for srcs, lands, _, _ in groups:
            bounds.append((len(all_srcs), len(all_srcs) + len(srcs)))
            all_srcs += srcs
            all_lands += lands

        def plan_all(src_refs, land_refs):
            return [cp for (lo, hi), group in zip(bounds, groups) for cp in group[2](src_refs[lo:hi], land_refs[lo:hi])]

        started = _start_copies(name, all_srcs, all_lands, plan_all, 3 * len(all_srcs), dep)
        return [(started, bound, group[2], group[3]) for bound, group in zip(bounds, groups)]

    def arrived(name, gather, after):
        started, (lo, hi), plan, n = gather
        mine = (started[0], started[1], started[2][lo:hi], started[3][lo:hi], started[4])
        srcs, lands = _wait_copies(name + "_wait", mine, plan, after, sem_offset=3 * lo)
        return srcs[:n], lands[:n], lands[n:]

    (cp_gather,) = start_groups("gather_cp_start", [gather_group([halves(cp_w_in[0]), halves(cp_w_out[0])],
                                                                 [_pack(sharded_small)])], None)
    tok = cp_gather[0][-1]
    ffn0_gather, gla_gather, ffn1_gather = start_groups(
        "gather_start", [gather_group([halves(ffn_w1[0], tok), halves(ffn_w2[0], tok)]),
                         gather_group([halves(gla_w_in[0], tok), halves(gla_w_out[0], tok)]),
                         gather_group([halves(ffn_w1[1], tok), halves(ffn_w2[1], tok)])], tok)
    h0_rows = jnp.concatenate([jnp.zeros((CHUNK, d), F32) + ffn0_gather[0][-1][0, 0], x[0]], axis=0)
    cp_srcs, cp_lands, (small_g,) = arrived("gather_cp", cp_gather, h0_rows)
    cpin_g, cpout_g = [unhalve(g) for g in _share_with_sibling("gather_cp_share", cp_srcs, cp_lands)]
    per_chip = [_unpack(small_g[j], [a.shape for a in sharded_small]) for j in range(N_CHIPS)]
    meta_f, conv_w_f, gate_w_f, gate_b_f, head_g_f = [
        jnp.concatenate([per_chip[j][i] for j in range(N_CHIPS)], axis=-1) for i in range(len(sharded_small))]
    conv_w_f, gate_w_f = conv_w_f[0], gate_w_f[0]
    w_cp_in = _unshard_cols(cpin_g)
    w_cp_out = cpout_g.reshape(CONV_DIM + POOL_DIM, d)
    gate_w_pad = jnp.pad(gate_w_f, ((0, GATE_PAD - GATE_RANK), (0, 0))).astype(BF16)
    row = lambda a: a.reshape(1, -1)
    c_idx = core.reshape(1).astype(jnp.int32)

    h0 = lax.dynamic_update_slice(h0_rows, meta_f, (PAD_ROWS, 0))
    z0, u0 = _norm_matmul(h0, row(mix_norm_g[0]), w_cp_in, 512, "cp_in_proj")
    c0, pm0, mix0 = _cp_seq_fwd(z0, conv_w_f, cp_conv_b, cp_ln_g, cp_ln_b, cp_pool_w[0], cp_pool_scale)
    ffn0_srcs, ffn0_lands, _ = arrived("gather_ffn0", ffn0_gather, mix0)
    ffn0_share = _start_copies("gather_ffn0_share_start", ffn0_srcs, ffn0_lands, _plan_share, 4 * len(ffn0_srcs))
    h1 = _matmul_residual(mix0, w_cp_out, h0, "cp_out_proj", dep=ffn0_share[-1])
    w1g0, w2g0 = [unhalve(g) for g in _wait_copies("gather_ffn0_share_wait", ffn0_share, _plan_share, h1)[1]]
    h2, hp0, uf0 = _ffn_fwd(h1, row(ffn_norm_g[0]), w1g0, w2g0, "ffn0_fwd")
    gla_srcs, gla_lands, _ = arrived("gather_gla", gla_gather, h2)
    glain_g, glaout_g = [unhalve(g) for g in _share_with_sibling("gather_gla_share", gla_srcs, gla_lands)]
    w_gla_in = jnp.concatenate([glain_g[j] for j in range(N_CHIPS)] + [jnp.zeros((d, GLA_IN_PAD - GLA_IN), BF16)], axis=1)
    w_gla_out = glaout_g.reshape(GLA_DV, d)
    z1, u2 = _norm_matmul(h2, row(mix_norm_g[1]), w_gla_in, GLA_COLS, "gla_in_proj")
    ffn1_srcs, ffn1_lands, _ = arrived("gather_ffn1", ffn1_gather, z1)
    ffn1_share = _start_copies("gather_ffn1_share_start", ffn1_srcs, ffn1_lands, _plan_share, 4 * len(ffn1_srcs))
    o1, mix1, states, h3 = _gla_seq_fwd(z1, gate_w_pad, gate_b_f, head_g_f, h2, w_gla_out, dep=ffn1_share[-1])
    w1g1, w2g1 = [unhalve(g) for g in _wait_copies("gather_ffn1_share_wait", ffn1_share, _plan_share, h3)[1]]
    h4, hp1, uf1 = _ffn_fwd(h3, row(ffn_norm_g[1]), w1g1, w2g1, "ffn1_fwd")

    dev_idx = (2 * chip + core).reshape(1).astype(jnp.int32)

    def start_reduce(name, grads):
        srcs = [_split_rows(g) for g in grads]
        lands = [lax.empty((N_DEVICES,) + s.shape[2:], s.dtype) for s in srcs]
        return _start_copies(name + "_scatter_start", srcs, lands, _plan_scatter_all, len(OTHER_DEVICES) * len(srcs))

    def finish_reduce(name, started, after):
        srcs, lands = _wait_copies(name + "_scatter_wait", started, _plan_scatter_all, after)
        return [_sum_own_and_slots(s, l, dev_idx, "%s_slot_sum_%d" % (name, k)) for k, (s, l) in enumerate(zip(srcs, lands))]

    dh4, d_final_g, loss_part = _loss_bwd(h4, row(final_norm_g), loss_target[0])

    dh3, dhp1, d_ffn_g1 = _ffn_bwd_data(dh4, h3, row(ffn_norm_g[1]), hp1, w1g1, w2g1, "ffn1_bwd")
    dw1_1 = _wgrad(uf1, dhp1, N_CHIPS, d, d, False, True, False, "ffn1_dw1", rows=WGRAD_ROWS_BF16)
    dw2_1 = _wgrad(hp1, dh4, N_CHIPS, d, d, True, False, True, "ffn1_dw2")
    ffn1_reduce = start_reduce("ffn1", [dw1_1, dw2_1])

    dw_gla_out = _wgrad(mix1, dh3, 1, GLA_DV, d, False, False, False, "gla_out_dw", dep=ffn1_reduce[-1])
    dz1, d_gate_w, d_gate_b, d_head_g = _gla_seq_bwd(dh3, w_gla_out, o1, z1, states, gate_w_pad, gate_b_f, head_g_f,
                                                     dep=ffn1_reduce[-1])
    dh2, d_mix_g1 = _dgrad_norm_bwd(dz1, w_gla_in, h2, row(mix_norm_g[1]), dh3, GLA_COLS, "gla_in_dgrad")
    dw_gla_in = _wgrad(u2, dz1, GLA_IN_PAD // 640, d, 640, False, True, False, "gla_in_dw", rows=WGRAD_ROWS_BF16)
    gla_in_shards = jnp.stack([_take_cols(dw_gla_in, j * (GLA_IN // N_CHIPS), GLA_IN // N_CHIPS) for j in range(N_CHIPS)])
    gla_reduce = start_reduce("gla", [gla_in_shards, dw_gla_out.reshape(N_CHIPS, -1, d)])

    dh1, dhp0, d_ffn_g0 = _ffn_bwd_data(dh2, h1, row(ffn_norm_g[0]), hp0, w1g0, w2g0, "ffn0_bwd", dep=gla_reduce[-1])
    dw1_0 = _wgrad(uf0, dhp0, N_CHIPS, d, d, False, True, False, "ffn0_dw1", rows=WGRAD_ROWS_BF16)
    dw2_0 = _wgrad(hp0, dh2, N_CHIPS, d, d, True, False, True, "ffn0_dw2")
    ffn0_reduce = start_reduce("ffn0", [dw1_0, dw2_0])

    dw_cp_out = _wgrad(mix0, dh1, 1, CONV_DIM + POOL_DIM, d, False, False, False, "cp_out_dw", dep=ffn0_reduce[-1])
    dz0, d_conv_w, d_cp_vec, d_pool_w = _cp_seq_bwd(dh1, w_cp_out, z0, c0, pm0, conv_w_f, cp_ln_g, cp_ln_b, cp_pool_w[0],
                                                    cp_pool_scale, dep=ffn0_reduce[-1])
    grad_x, dh0_head, d_mix_g0 = _dgrad_norm_bwd_input(dz0, w_cp_in, h0, row(mix_norm_g[0]), dh1, 512, "cp_in_dgrad")
    grad_x = grad_x[None]
    dw_cp_in = _wgrad(u0, dz0, 1, d, CP_IN, False, False, False, "cp_in_dw")
    dw_cp_in = jnp.stack([_take_cols(dw_cp_in, j * (CP_IN // N_CHIPS), CP_IN // N_CHIPS) for j in range(N_CHIPS)])

    cp_reduce = start_reduce("cp", [dw_cp_in, dw_cp_out.reshape(N_CHIPS, -1, d)])
    small_full = [dh0_head[PAD_ROWS:CHUNK],jnp.concatenate([d_mix_g0, d_mix_g1], axis=0),
                  jnp.concatenate([d_ffn_g0, d_ffn_g1], axis=0), d_conv_w[:CONV_WIDTH][None],
                  d_cp_vec[0:1], d_cp_vec[1:2], d_cp_vec[2:3], d_pool_w[None], d_cp_vec[3:4],
                  d_gate_w[:GATE_RANK][None], d_gate_b, d_head_g, d_final_g[0], loss_part[0, 0:1]]
    small_mine = _pack(small_full)
    whole = _plan_exchange(0)
    small_exchange = _start_copies("small_exchange_start", [small_mine], [lax.empty(small_mine.shape, F32)], whole, 1,
                                   dep=cp_reduce[-1])
    red_ffn1 = finish_reduce("ffn1", ffn1_reduce, small_exchange[-1])
    red_gla = finish_reduce("gla", gla_reduce, small_exchange[-1])
    (small_sent,), (small_recv,) = _wait_copies("small_exchange_wait", small_exchange, whole, [red_ffn1[1], red_gla[1]])
    small_chip = _add2(small_sent, small_recv, "chip_sum_small")
    small_slots = lax.dynamic_update_slice(jnp.zeros((N_CHIPS,) + small_chip.shape, F32), small_chip[None], (chip, 0, 0))
    small_reduce = _start_copies("small_scatter_start", [small_chip], [small_slots], _plan_scatter(0), 3)

    big = {"w1": (ffn_w1, m_ffn_w1, v_ffn_w1), "w2": (ffn_w2, m_ffn_w2, v_ffn_w2),
           "cp_in": (cp_w_in, m_cp_w_in, v_cp_w_in), "cp_out": (cp_w_out, m_cp_w_out, v_cp_w_out),
           "gla_in": (gla_w_in, m_gla_w_in, v_gla_w_in), "gla_out": (gla_w_out, m_gla_w_out, v_gla_w_out)}
    other_idx = (1 - core).reshape(1).astype(jnp.int32)

    def adamw_by_halves(tag, reduced, dep=None):
        flat = [r for n in reduced for r in reduced[n]]
        join_plan = _plan_exchange(0)
        join = _start_copies(tag + "_join_start", flat, [lax.empty(r.shape, F32) for r in flat], join_plan, len(flat),
                             dep=dep)
        views = {n: [_split_rows(a) for a in big[n]] for n in reduced}
        own, k = {}, 0
        for n in reduced:
            mine = join[2][k:k + len(reduced[n])]
            k += len(reduced[n])
            own[n] = _adamw_half(views[n][0], mine, views[n][1], views[n][2], c_idx, None, "adamw_%s_own" % n)
        _, arrived_halves = _wait_copies(tag + "_join_wait", join, join_plan, [own[n][1] for n in reduced])
        outs, k = {}, 0
        for n in reduced:
            theirs = arrived_halves[k:k + len(reduced[n])]
            k += len(reduced[n])
            res = _adamw_half(views[n][0], theirs, views[n][1], views[n][2], other_idx, own[n], "adamw_%s_sibling" % n)
            outs[n] = [o.reshape(big[n][0].shape) for o in res]
        return outs

    big_out = adamw_by_halves("gla", {"gla_in": [red_gla[0]], "gla_out": [red_gla[1]]}, dep=small_reduce[-1])
    red_ffn0 = finish_reduce("ffn0", ffn0_reduce, big_out["gla_out"][1])
    big_out.update(adamw_by_halves("ffn", {"w1": [red_ffn0[0], red_ffn1[0]], "w2": [red_ffn0[1], red_ffn1[1]]}))
    red_cp = finish_reduce("cp", cp_reduce, big_out["w2"][1])
    _, (small_landed,) = _wait_copies("small_scatter_wait", small_reduce, _plan_scatter(0), big_out["w2"][1])
    small_red = _sum_slots(small_landed, "slot_sum_small")
    big_out.update(adamw_by_halves("cp", {"cp_in": [red_cp[0]], "cp_out": [red_cp[1]]}))

    (g_meta, g_mix, g_ffn, g_conv_w, g_conv_b, g_ln_g, g_ln_b, g_pool_w, g_pool_scale, g_gate_w, g_gate_b, g_head,
     g_final, loss_sum) = _unpack(small_red, [a.shape for a in small_full])
    g_meta = _col_shard(g_meta, chip, meta_tokens.shape[-1])
    g_conv_w = _col_shard(g_conv_w, chip, cp_conv_w.shape[-1])
    g_gate_w = _col_shard(g_gate_w, chip, gla_gate_w2.shape[-1])
    g_gate_b = _col_shard(g_gate_b, chip, gla_gate_b.shape[-1])
    g_head = _col_shard(g_head, chip, gla_head_g.shape[-1])
    small_w = [meta_tokens, mix_norm_g, ffn_norm_g, cp_conv_w, cp_conv_b, cp_ln_g, cp_ln_b, cp_pool_w, cp_pool_scale,
               gla_gate_w2, gla_gate_b, gla_head_g, final_norm_g]
    small_m = [m_meta_tokens, m_mix_norm_g, m_ffn_norm_g, m_cp_conv_w, m_cp_conv_b, m_cp_ln_g, m_cp_ln_b, m_cp_pool_w,
               m_cp_pool_scale, m_gla_gate_w2, m_gla_gate_b, m_gla_head_g, m_final_norm_g]
    small_v = [v_meta_tokens, v_mix_norm_g, v_ffn_norm_g, v_cp_conv_w, v_cp_conv_b, v_cp_ln_g, v_cp_ln_b, v_cp_pool_w,
               v_cp_pool_scale, v_gla_gate_w2, v_gla_gate_b, v_gla_head_g, v_final_norm_g]
    small_g = [g_meta, g_mix, g_ffn, g_conv_w, g_conv_b, g_ln_g, g_ln_b, g_pool_w, g_pool_scale, g_gate_w, g_gate_b,
               g_head, g_final]
    shapes = [w.shape for w in small_w]
    small_g = [g.reshape(s) for g, s in zip(small_g, shapes)]
    at_least_2d = lambda arrs: [a.reshape(1, -1) if a.ndim == 1 else a for a in arrs]
    s_delta, s_m, s_v = _adamw_many(at_least_2d(small_w), at_least_2d(small_g), at_least_2d(small_m), at_least_2d(small_v))
    s_delta, s_m, s_v = [[a.reshape(s) for a, s in zip(group, shapes)] for group in (s_delta, s_m, s_v)]

    order = ["meta", "mix", "ffn", "w1", "w2", "cp_in", "conv_w", "conv_b", "ln_g", "ln_b", "pool_w", "pool_scale",
             "cp_out", "gla_in", "gate_w", "gate_b", "head", "gla_out", "final"]
    small_names = ["meta", "mix", "ffn", "conv_w", "conv_b", "ln_g", "ln_b", "pool_w", "pool_scale", "gate_w", "gate_b",
                   "head", "final"]
    big_names = ["w1", "w2", "cp_in", "cp_out", "gla_in", "gla_out"]
    table = {n: (small_g[i], s_delta[i], s_m[i], s_v[i]) for i, n in enumerate(small_names)}
    table.update({n: tuple(big_out[n]) for n in big_names})
    loss = loss_sum.reshape(())
    return (loss, grad_x, *[table[n][0] for n in order], *[table[n][1] for n in order],
            *[table[n][2] for n in order], *[table[n][3] for n in order])
```
